```python
import jax, jax.numpy as jnp
from jax import lax
import numpy as np

D_MODEL = 1024
BATCH = 8
SEQ = 4096
DEPTH = 4

CHUNK = 64
N_MIXERS = 4
MIX_WIDTH = D_MODEL
GROUP_WIDTH = MIX_WIDTH // N_MIXERS
N_HEADS_PER_MIXER = 4
HEAD_DIM = GROUP_WIDTH // N_HEADS_PER_MIXER
CONFORMER_KERNEL = 31
SHORT_CONV_KERNEL = 3
POOL_WINDOWS = (2, 4, 8, 16)
SGU_BLOCK = 128
N_IN_SLICES = 12
IN_WIDTH = N_IN_SLICES * GROUP_WIDTH
LN_EPS = 1e-5

kernel_name = "hybrid_conv_pool_sgu_deepnorm_trunk"


def layer_norm(x, g, b):
    xf = x.astype(jnp.float32)
    mu = jnp.mean(xf, axis=-1, keepdims=True)
    var = jnp.mean(jnp.square(xf - mu), axis=-1, keepdims=True)
    y = (xf - mu) * lax.rsqrt(var + LN_EPS)
    return (y * g + b).astype(x.dtype)


def causal_dwconv(x, w):
    k, c = w.shape
    return lax.conv_general_dilated(
        x, w[:, None, :].astype(x.dtype), window_strides=(1,), padding=[(k - 1, 0)],
        dimension_numbers=("NWC", "WIO", "NWC"), feature_group_count=c)


def multi_scale_pool(h):
    bsz, s, _ = h.shape
    hg = h.reshape(bsz, s, len(POOL_WINDOWS), HEAD_DIM).astype(jnp.float32)
    cs = jnp.cumsum(hg, axis=1)
    pos1 = jnp.arange(1, s + 1)
    means = []
    for g, w in enumerate(POOL_WINDOWS):
        c = cs[:, :, g]
        prev = jnp.pad(c[:, : s - w], ((0, 0), (w, 0), (0, 0)))
        cnt = jnp.minimum(pos1, w).astype(jnp.float32)[None, :, None]
        means.append((c - prev) / cnt)
    mean = jnp.stack(means, axis=2)
    return (mean - hg).astype(h.dtype)


def sgu_mask():
    idx = jnp.arange(SGU_BLOCK) // CHUNK
    return (idx[None, :] <= idx[:, None])


def hybrid_layer(x, ln_g, ln_b, w_in, b_in, conv_a_w, conv_a_b, norm_a_g, norm_a_b,
                 conv_b_w, pool_w, pool_scale, sgu_ln_g, sgu_ln_b, sgu_w, sgu_bias,
                 w_out, b_out):
    bsz, s, _ = x.shape
    alpha = float((2.0 * DEPTH) ** 0.25)
    h = jnp.einsum("bsd,de->bse", x, w_in) + b_in
    (a_val, a_glu, a_z, b_b, b_c, b_h, b_z, c_h, c_z, d_u, d_v, d_z) = jnp.split(h, N_IN_SLICES, axis=-1)

    a = a_val * jax.nn.sigmoid(a_glu)
    a = causal_dwconv(a, conv_a_w) + conv_a_b
    a = layer_norm(a.reshape(bsz, s, N_HEADS_PER_MIXER, HEAD_DIM),
                   norm_a_g.reshape(N_HEADS_PER_MIXER, HEAD_DIM),
                   norm_a_b.reshape(N_HEADS_PER_MIXER, HEAD_DIM)).reshape(bsz, s, GROUP_WIDTH)
    y_a = jax.nn.silu(a) * jax.nn.silu(a_z)

    y_b = b_b * causal_dwconv(b_c * b_h, conv_b_w) * jax.nn.silu(b_z)

    pooled = multi_scale_pool(c_h)
    y_c = jnp.einsum("bsgc,gcd->bsgd", pooled, pool_w).reshape(bsz, s, GROUP_WIDTH)
    y_c = y_c * pool_scale * jax.nn.silu(c_z)

    v = layer_norm(d_v, sgu_ln_g, sgu_ln_b)
    vb = v.reshape(bsz, s // SGU_BLOCK, SGU_BLOCK, N_HEADS_PER_MIXER, HEAD_DIM)
    w_s = jnp.where(sgu_mask()[None], sgu_w, jnp.zeros_like(sgu_w))
    sp = jnp.einsum("hij,bnjhc->bnihc", w_s, vb) + sgu_bias.T[None, None, :, :, None]
    y_d = d_u * sp.reshape(bsz, s, GROUP_WIDTH) * jax.nn.silu(d_z)

    mix = jnp.concatenate([y_a, y_b, y_c, y_d], axis=-1)
    out = jnp.einsum("bse,ed->bsd", mix, w_out) + b_out
    return layer_norm(alpha * x + out, ln_g, ln_b)


def _fwd_setup_inputs(seed: int = 0) -> dict:
    key = jax.random.key(seed)
    ks = jax.random.split(key, 20)
    f32 = jnp.float32
    L = DEPTH
    beta = (8.0 * DEPTH) ** -0.25
    nrm = lambda k, shape, sc: jax.random.normal(k, shape, f32) * sc
    return {
        "x": jax.random.normal(ks[0], (BATCH, SEQ, D_MODEL), f32),
        "ln_g": 1.0 + nrm(ks[1], (L, D_MODEL), 0.05),
        "ln_b": nrm(ks[2], (L, D_MODEL), 0.02),
        "w_in": nrm(ks[3], (L, D_MODEL, IN_WIDTH), D_MODEL ** -0.5),
        "b_in": nrm(ks[4], (L, IN_WIDTH), 0.02),
        "conv_a_w": nrm(ks[5], (L, CONFORMER_KERNEL, GROUP_WIDTH), CONFORMER_KERNEL ** -0.5),
        "conv_a_b": nrm(ks[6], (L, GROUP_WIDTH), 0.02),
        "norm_a_g": 1.0 + nrm(ks[7], (L, GROUP_WIDTH), 0.05),
        "norm_a_b": nrm(ks[8], (L, GROUP_WIDTH), 0.02),
        "conv_b_w": nrm(ks[9], (L, SHORT_CONV_KERNEL, GROUP_WIDTH), SHORT_CONV_KERNEL ** -0.5),
        "pool_w": nrm(ks[10], (L, len(POOL_WINDOWS), HEAD_DIM, HEAD_DIM), HEAD_DIM ** -0.5),
        "pool_scale": 1.0 + nrm(ks[11], (L, GROUP_WIDTH), 0.1),
        "sgu_ln_g": 1.0 + nrm(ks[12], (L, GROUP_WIDTH), 0.05),
        "sgu_ln_b": nrm(ks[13], (L, GROUP_WIDTH), 0.02),
        "sgu_w": nrm(ks[14], (L, N_HEADS_PER_MIXER, SGU_BLOCK, SGU_BLOCK), SGU_BLOCK ** -0.5),
        "sgu_bias": 1.0 + nrm(ks[15], (L, N_HEADS_PER_MIXER, SGU_BLOCK), 0.1),
        "w_out": nrm(ks[16], (L, MIX_WIDTH, D_MODEL), beta * MIX_WIDTH ** -0.5),
        "b_out": nrm(ks[17], (L, D_MODEL), 0.01),
    }


def _fwd_reference(x, ln_g, ln_b, w_in, b_in, conv_a_w, conv_a_b, norm_a_g, norm_a_b,
              conv_b_w, pool_w, pool_scale, sgu_ln_g, sgu_ln_b, sgu_w, sgu_bias,
              w_out, b_out):
    h = x
    for l in range(DEPTH):
        h = hybrid_layer(h, ln_g[l], ln_b[l], w_in[l], b_in[l], conv_a_w[l], conv_a_b[l],
                         norm_a_g[l], norm_a_b[l], conv_b_w[l], pool_w[l], pool_scale[l],
                         sgu_ln_g[l], sgu_ln_b[l], sgu_w[l], sgu_bias[l], w_out[l], b_out[l])
    return h


import jax as _jax
import jax.numpy as _jnp

TWIN_FORMAT = 'train_step'
FWD_PARAMS = ['x', 'ln_g', 'ln_b', 'w_in', 'b_in', 'conv_a_w', 'conv_a_b', 'norm_a_g', 'norm_a_b', 'conv_b_w', 'pool_w', 'pool_scale', 'sgu_ln_g', 'sgu_ln_b', 'sgu_w', 'sgu_bias', 'w_out', 'b_out']
TWIN_WEIGHTS = ['ln_g', 'ln_b', 'w_in', 'b_in', 'conv_a_w', 'conv_a_b', 'norm_a_g', 'norm_a_b', 'conv_b_w', 'pool_w', 'pool_scale', 'sgu_ln_g', 'sgu_ln_b', 'sgu_w', 'sgu_bias', 'w_out', 'b_out']
TWIN_DIFF_INPUT = 'x'
TWIN_INPUTS = ['x', 'ln_g', 'ln_b', 'w_in', 'b_in', 'conv_a_w', 'conv_a_b', 'norm_a_g', 'norm_a_b', 'conv_b_w', 'pool_w', 'pool_scale', 'sgu_ln_g', 'sgu_ln_b', 'sgu_w', 'sgu_bias', 'w_out', 'b_out', 'loss_target', 'm_ln_g', 'm_ln_b', 'm_w_in', 'm_b_in', 'm_conv_a_w', 'm_conv_a_b', 'm_norm_a_g', 'm_norm_a_b', 'm_conv_b_w', 'm_pool_w', 'm_pool_scale', 'm_sgu_ln_g', 'm_sgu_ln_b', 'm_sgu_w', 'm_sgu_bias', 'm_w_out', 'm_b_out', 'v_ln_g', 'v_ln_b', 'v_w_in', 'v_b_in', 'v_conv_a_w', 'v_conv_a_b', 'v_norm_a_g', 'v_norm_a_b', 'v_conv_b_w', 'v_pool_w', 'v_pool_scale', 'v_sgu_ln_g', 'v_sgu_ln_b', 'v_sgu_w', 'v_sgu_bias', 'v_w_out', 'v_b_out']
TWIN_OUTPUTS = ['loss', 'grad_x', 'grad_ln_g', 'grad_ln_b', 'grad_w_in', 'grad_b_in', 'grad_conv_a_w', 'grad_conv_a_b', 'grad_norm_a_g', 'grad_norm_a_b', 'grad_conv_b_w', 'grad_pool_w', 'grad_pool_scale', 'grad_sgu_ln_g', 'grad_sgu_ln_b', 'grad_sgu_w', 'grad_sgu_bias', 'grad_w_out', 'grad_b_out', 'delta_ln_g', 'delta_ln_b', 'delta_w_in', 'delta_b_in', 'delta_conv_a_w', 'delta_conv_a_b', 'delta_norm_a_g', 'delta_norm_a_b', 'delta_conv_b_w', 'delta_pool_w', 'delta_pool_scale', 'delta_sgu_ln_g', 'delta_sgu_ln_b', 'delta_sgu_w', 'delta_sgu_bias', 'delta_w_out', 'delta_b_out', 'new_m_ln_g', 'new_m_ln_b', 'new_m_w_in', 'new_m_b_in', 'new_m_conv_a_w', 'new_m_conv_a_b', 'new_m_norm_a_g', 'new_m_norm_a_b', 'new_m_conv_b_w', 'new_m_pool_w', 'new_m_pool_scale', 'new_m_sgu_ln_g', 'new_m_sgu_ln_b', 'new_m_sgu_w', 'new_m_sgu_bias', 'new_m_w_out', 'new_m_b_out', 'new_v_ln_g', 'new_v_ln_b', 'new_v_w_in', 'new_v_b_in', 'new_v_conv_a_w', 'new_v_conv_a_b', 'new_v_norm_a_g', 'new_v_norm_a_b', 'new_v_conv_b_w', 'new_v_pool_w', 'new_v_pool_scale', 'new_v_sgu_ln_g', 'new_v_sgu_ln_b', 'new_v_sgu_w', 'new_v_sgu_bias', 'new_v_w_out', 'new_v_b_out']
TWIN_LEAF_KINDS = {'loss': 'loss', 'grad_x': 'grad_x', 'grad_ln_g': 'grad_w', 'grad_ln_b': 'grad_w', 'grad_w_in': 'grad_w', 'grad_b_in': 'grad_w', 'grad_conv_a_w': 'grad_w', 'grad_conv_a_b': 'grad_w', 'grad_norm_a_g': 'grad_w', 'grad_norm_a_b': 'grad_w', 'grad_conv_b_w': 'grad_w', 'grad_pool_w': 'grad_w', 'grad_pool_scale': 'grad_w', 'grad_sgu_ln_g': 'grad_w', 'grad_sgu_ln_b': 'grad_w', 'grad_sgu_w': 'grad_w', 'grad_sgu_bias': 'grad_w', 'grad_w_out': 'grad_w', 'grad_b_out': 'grad_w', 'delta_ln_g': 'delta_w', 'delta_ln_b': 'delta_w', 'delta_w_in': 'delta_w', 'delta_b_in': 'delta_w', 'delta_conv_a_w': 'delta_w', 'delta_conv_a_b': 'delta_w', 'delta_norm_a_g': 'delta_w', 'delta_norm_a_b': 'delta_w', 'delta_conv_b_w': 'delta_w', 'delta_pool_w': 'delta_w', 'delta_pool_scale': 'delta_w', 'delta_sgu_ln_g': 'delta_w', 'delta_sgu_ln_b': 'delta_w', 'delta_sgu_w': 'delta_w', 'delta_sgu_bias': 'delta_w', 'delta_w_out': 'delta_w', 'delta_b_out': 'delta_w', 'new_m_ln_g': 'new_m', 'new_m_ln_b': 'new_m', 'new_m_w_in': 'new_m', 'new_m_b_in': 'new_m', 'new_m_conv_a_w': 'new_m', 'new_m_conv_a_b': 'new_m', 'new_m_norm_a_g': 'new_m', 'new_m_norm_a_b': 'new_m', 'new_m_conv_b_w': 'new_m', 'new_m_pool_w': 'new_m', 'new_m_pool_scale': 'new_m', 'new_m_sgu_ln_g': 'new_m', 'new_m_sgu_ln_b': 'new_m', 'new_m_sgu_w': 'new_m', 'new_m_sgu_bias': 'new_m', 'new_m_w_out': 'new_m', 'new_m_b_out': 'new_m', 'new_v_ln_g': 'new_v', 'new_v_ln_b': 'new_v', 'new_v_w_in': 'new_v', 'new_v_b_in': 'new_v', 'new_v_conv_a_w': 'new_v', 'new_v_conv_a_b': 'new_v', 'new_v_norm_a_g': 'new_v', 'new_v_norm_a_b': 'new_v', 'new_v_conv_b_w': 'new_v', 'new_v_pool_w': 'new_v', 'new_v_pool_scale': 'new_v', 'new_v_sgu_ln_g': 'new_v', 'new_v_sgu_ln_b': 'new_v', 'new_v_sgu_w': 'new_v', 'new_v_sgu_bias': 'new_v', 'new_v_w_out': 'new_v', 'new_v_b_out': 'new_v'}


def _forward(args):
    return _fwd_reference(*[args[k] for k in FWD_PARAMS])


def _output_shape():
    def fwd():
        inp = _fwd_setup_inputs(0)
        return _fwd_reference(*[inp[k] for k in FWD_PARAMS])
    out = _jax.eval_shape(fwd)
    return out.shape, out.dtype

N_MICROBATCH = 1
ADAM_LR = 0.001
ADAM_B1 = 0.9
ADAM_B2 = 0.999
ADAM_EPS = 1e-08
ADAM_WD = 0.01
ADAM_STEP = 10
PER_EXAMPLE_BATCH_AXIS = {'x': 0, 'loss_target': 0}
SHARED_INPUTS = []
_WEIGHT_DTYPES = {'ln_g': _jnp.float32, 'ln_b': _jnp.float32, 'w_in': _jnp.float32, 'b_in': _jnp.float32, 'conv_a_w': _jnp.float32, 'conv_a_b': _jnp.float32, 'norm_a_g': _jnp.float32, 'norm_a_b': _jnp.float32, 'conv_b_w': _jnp.float32, 'pool_w': _jnp.float32, 'pool_scale': _jnp.float32, 'sgu_ln_g': _jnp.float32, 'sgu_ln_b': _jnp.float32, 'sgu_w': _jnp.float32, 'sgu_bias': _jnp.float32, 'w_out': _jnp.float32, 'b_out': _jnp.float32}
MOMENT_SCALE = {'ln_g': 1.645304e+01, 'ln_b': 8.283782e-01, 'w_in': 2.593320e-02, 'b_in': 2.610221e-02, 'conv_a_w': 1.696839e-02, 'conv_a_b': 3.774617e-02, 'norm_a_g': 2.022909e-02, 'norm_a_b': 1.995782e-02, 'conv_b_w': 2.692379e-02, 'pool_w': 2.421553e-02, 'pool_scale': 2.346276e-02, 'sgu_ln_g': 2.405390e-02, 'sgu_ln_b': 2.378540e-02, 'sgu_w': 1.675970e-02, 'sgu_bias': 1.931750e-02, 'w_out': 6.416834e-02, 'b_out': 3.500768e-01}


def _to_microbatches(a, axis):
    t = _jnp.moveaxis(a, axis, 0)
    t = t.reshape((N_MICROBATCH, t.shape[0] // N_MICROBATCH) + t.shape[1:])
    return _jnp.moveaxis(t, 1, axis + 1)


def setup_inputs(seed: int = 0) -> dict:
    inp = _fwd_setup_inputs(seed)
    key = _jax.random.fold_in(_jax.random.key(seed), 7919)
    shape, _ = _output_shape()
    out = dict(inp)
    out["loss_target"] = _jax.random.normal(_jax.random.fold_in(key, 0), shape, _jnp.float32)
    for i, name in enumerate(TWIN_WEIGHTS):
        w = inp[name].astype(_jnp.float32)
        if MOMENT_SCALE is None:
            s = _jnp.sqrt(_jnp.mean(_jnp.square(w)) + 1e-30)
        else:
            s = MOMENT_SCALE[name]
        km, kv = _jax.random.split(_jax.random.fold_in(key, i + 1))
        out[name] = w
        out["m_" + name] = s * _jax.random.normal(km, w.shape, _jnp.float32)
        out["v_" + name] = (s * s) * _jax.random.uniform(kv, w.shape, _jnp.float32, 0.5, 1.5)
    if N_MICROBATCH > 1:
        for name, axis in PER_EXAMPLE_BATCH_AXIS.items():
            out[name] = _to_microbatches(out[name], axis)
    return {'x': out['x'], 'ln_g': out['ln_g'], 'ln_b': out['ln_b'], 'w_in': out['w_in'], 'b_in': out['b_in'], 'conv_a_w': out['conv_a_w'], 'conv_a_b': out['conv_a_b'], 'norm_a_g': out['norm_a_g'], 'norm_a_b': out['norm_a_b'], 'conv_b_w': out['conv_b_w'], 'pool_w': out['pool_w'], 'pool_scale': out['pool_scale'], 'sgu_ln_g': out['sgu_ln_g'], 'sgu_ln_b': out['sgu_ln_b'], 'sgu_w': out['sgu_w'], 'sgu_bias': out['sgu_bias'], 'w_out': out['w_out'], 'b_out': out['b_out'], 'loss_target': out['loss_target'], 'm_ln_g': out['m_ln_g'], 'm_ln_b': out['m_ln_b'], 'm_w_in': out['m_w_in'], 'm_b_in': out['m_b_in'], 'm_conv_a_w': out['m_conv_a_w'], 'm_conv_a_b': out['m_conv_a_b'], 'm_norm_a_g': out['m_norm_a_g'], 'm_norm_a_b': out['m_norm_a_b'], 'm_conv_b_w': out['m_conv_b_w'], 'm_pool_w': out['m_pool_w'], 'm_pool_scale': out['m_pool_scale'], 'm_sgu_ln_g': out['m_sgu_ln_g'], 'm_sgu_ln_b': out['m_sgu_ln_b'], 'm_sgu_w': out['m_sgu_w'], 'm_sgu_bias': out['m_sgu_bias'], 'm_w_out': out['m_w_out'], 'm_b_out': out['m_b_out'], 'v_ln_g': out['v_ln_g'], 'v_ln_b': out['v_ln_b'], 'v_w_in': out['v_w_in'], 'v_b_in': out['v_b_in'], 'v_conv_a_w': out['v_conv_a_w'], 'v_conv_a_b': out['v_conv_a_b'], 'v_norm_a_g': out['v_norm_a_g'], 'v_norm_a_b': out['v_norm_a_b'], 'v_conv_b_w': out['v_conv_b_w'], 'v_pool_w': out['v_pool_w'], 'v_pool_scale': out['v_pool_scale'], 'v_sgu_ln_g': out['v_sgu_ln_g'], 'v_sgu_ln_b': out['v_sgu_ln_b'], 'v_sgu_w': out['v_sgu_w'], 'v_sgu_bias': out['v_sgu_bias'], 'v_w_out': out['v_w_out'], 'v_b_out': out['v_b_out']}


def _loss(weights, diff, rest, loss_target):
    with _jax.named_scope("forward"):
        args = {**rest, TWIN_DIFF_INPUT: diff, **{k: w.astype(_WEIGHT_DTYPES[k]) for k, w in weights.items()}}
        y = _forward(args)
    with _jax.named_scope("loss_head"):
        err = _jnp.square(y.astype(_jnp.float32) - loss_target)
        return 0.5 * _jnp.sum(_jnp.mean(err, axis=-1)) if err.ndim else 0.5 * err


def _adamw(w, g, m, v):
    m = ADAM_B1 * m + (1.0 - ADAM_B1) * g
    v = ADAM_B2 * v + (1.0 - ADAM_B2) * _jnp.square(g)
    m_hat = m / (1.0 - ADAM_B1 ** ADAM_STEP)
    v_hat = v / (1.0 - ADAM_B2 ** ADAM_STEP)
    delta = -ADAM_LR * (m_hat / (_jnp.sqrt(v_hat) + ADAM_EPS) + ADAM_WD * w)
    return delta, m, v


def reference(x, ln_g, ln_b, w_in, b_in, conv_a_w, conv_a_b, norm_a_g, norm_a_b, conv_b_w, pool_w, pool_scale, sgu_ln_g, sgu_ln_b, sgu_w, sgu_bias, w_out, b_out, loss_target, m_ln_g, m_ln_b, m_w_in, m_b_in, m_conv_a_w, m_conv_a_b, m_norm_a_g, m_norm_a_b, m_conv_b_w, m_pool_w, m_pool_scale, m_sgu_ln_g, m_sgu_ln_b, m_sgu_w, m_sgu_bias, m_w_out, m_b_out, v_ln_g, v_ln_b, v_w_in, v_b_in, v_conv_a_w, v_conv_a_b, v_norm_a_g, v_norm_a_b, v_conv_b_w, v_pool_w, v_pool_scale, v_sgu_ln_g, v_sgu_ln_b, v_sgu_w, v_sgu_bias, v_w_out, v_b_out):
    given = dict(x=x, ln_g=ln_g, ln_b=ln_b, w_in=w_in, b_in=b_in, conv_a_w=conv_a_w, conv_a_b=conv_a_b, norm_a_g=norm_a_g, norm_a_b=norm_a_b, conv_b_w=conv_b_w, pool_w=pool_w, pool_scale=pool_scale, sgu_ln_g=sgu_ln_g, sgu_ln_b=sgu_ln_b, sgu_w=sgu_w, sgu_bias=sgu_bias, w_out=w_out, b_out=b_out, loss_target=loss_target, m_ln_g=m_ln_g, m_ln_b=m_ln_b, m_w_in=m_w_in, m_b_in=m_b_in, m_conv_a_w=m_conv_a_w, m_conv_a_b=m_conv_a_b, m_norm_a_g=m_norm_a_g, m_norm_a_b=m_norm_a_b, m_conv_b_w=m_conv_b_w, m_pool_w=m_pool_w, m_pool_scale=m_pool_scale, m_sgu_ln_g=m_sgu_ln_g, m_sgu_ln_b=m_sgu_ln_b, m_sgu_w=m_sgu_w, m_sgu_bias=m_sgu_bias, m_w_out=m_w_out, m_b_out=m_b_out, v_ln_g=v_ln_g, v_ln_b=v_ln_b, v_w_in=v_w_in, v_b_in=v_b_in, v_conv_a_w=v_conv_a_w, v_conv_a_b=v_conv_a_b, v_norm_a_g=v_norm_a_g, v_norm_a_b=v_norm_a_b, v_conv_b_w=v_conv_b_w, v_pool_w=v_pool_w, v_pool_scale=v_pool_scale, v_sgu_ln_g=v_sgu_ln_g, v_sgu_ln_b=v_sgu_ln_b, v_sgu_w=v_sgu_w, v_sgu_bias=v_sgu_bias, v_w_out=v_w_out, v_b_out=v_b_out)
    weights = {n: given[n] for n in TWIN_WEIGHTS}
    shared = {n: given[n] for n in SHARED_INPUTS}
    per_example = {n: given[n] for n in ['x']}
    grad_fn = _jax.value_and_grad(_loss, argnums=(0, 1))

    def one_microbatch(ex, loss_target):
        ex = dict(ex)
        diff = ex.pop(TWIN_DIFF_INPUT)
        return grad_fn(weights, diff, {**shared, **ex}, loss_target)

    if N_MICROBATCH == 1:
        loss, (grad_w, grad_x) = one_microbatch(per_example, given["loss_target"])
    else:
        def body(carry, xs):
            loss_sum, grad_sum = carry
            l_k, (gw_k, gx_k) = one_microbatch(xs[0], xs[1])
            with _jax.named_scope("update"):
                return (loss_sum + l_k, _jax.tree.map(_jnp.add, grad_sum, gw_k)), gx_k

        init = (_jnp.zeros((), _jnp.float32), _jax.tree.map(_jnp.zeros_like, weights))
        (loss, grad_w), grad_x = _jax.lax.scan(body, init, (per_example, given["loss_target"]))
    with _jax.named_scope("update"):
        delta_w, new_m, new_v = {}, {}, {}
        for n in TWIN_WEIGHTS:
            delta_w[n], new_m[n], new_v[n] = _adamw(weights[n], grad_w[n], given["m_" + n], given["v_" + n])
    return (loss, grad_x, *[grad_w[n] for n in TWIN_WEIGHTS], *[delta_w[n] for n in TWIN_WEIGHTS],
            *[new_m[n] for n in TWIN_WEIGHTS], *[new_v[n] for n in TWIN_WEIGHTS])
```

```python
import functools

import jax
import jax.numpy as jnp
from jax import lax
from jax.experimental import pallas as pl
from jax.experimental.pallas import tpu as pltpu

F32 = jnp.float32
BF16 = jnp.bfloat16
MESH = pl.DeviceIdType.MESH

D_MODEL = 1024
GROUP = 256
HEAD = 64
N_SLICES = 12
IN_WIDTH = N_SLICES * GROUP
N_CHIPS = 4
COLS = IN_WIDTH // N_CHIPS
KA = 31
KB = 3
HALO_A, HALO_B, HALO_C = 32, 8, 16
POOL_WINDOWS = (2, 4, 8, 16)
SGU_BLOCK = 128
CHUNK = 64
LN_EPS = 1e-5
ROWS = 64
V7X_VMEM_BYTES = 64 * 1024 * 1024
VMEM_LIMIT = 56 * 1024 * 1024

ADAM_LR, ADAM_B1, ADAM_B2, ADAM_EPS, ADAM_WD, ADAM_STEP = 0.001, 0.9, 0.999, 1e-08, 0.01, 10


def _vmem_params(**kw):
    return pltpu.CompilerParams(vmem_limit_bytes=VMEM_LIMIT, **kw)


def _sig(v):
    return jax.nn.sigmoid(v)


def _dot(a, b):
    return jnp.dot(a, b, preferred_element_type=F32)


def _dot_nt(a, b):
    return lax.dot_general(a, b, (((1,), (1,)), ((), ())), preferred_element_type=F32)


def _dot_tn(a, b):
    return lax.dot_general(a, b, (((0,), (0,)), ((), ())), preferred_element_type=F32)


def _split3(v):
    hi = v.astype(BF16)
    r = v - hi.astype(F32)
    mid = r.astype(BF16)
    lo = (r - mid.astype(F32)).astype(BF16)
    return hi, mid, lo


def _segdot(v, m):
    hi, mid, lo = _split3(v)
    return _dot(hi, m) + _dot(mid, m) + _dot(lo, m)


def _colsum(v):
    return jnp.sum(v, axis=0, keepdims=True)


def _rowmean(v):
    return jnp.mean(v, axis=-1, keepdims=True)


def _lane_group(n):
    return lax.broadcasted_iota(jnp.int32, (1, n), 1) // HEAD


def _pool_cnt(tile, t_rows):
    pos = tile * t_rows + lax.broadcasted_iota(jnp.int32, (t_rows, GROUP), 0) + 1
    grp = lax.broadcasted_iota(jnp.int32, (t_rows, GROUP), 1) // HEAD
    win = jnp.where(grp == 0, 2, jnp.where(grp == 1, 4, jnp.where(grp == 2, 8, 16)))
    return jnp.minimum(pos, win).astype(F32)


def _sgu_masks(wm_ref, wmt_ref, wm_s, wmt_s):
    r = lax.broadcasted_iota(jnp.int32, (SGU_BLOCK, 4 * SGU_BLOCK), 0) // CHUNK
    c = (lax.broadcasted_iota(jnp.int32, (SGU_BLOCK, 4 * SGU_BLOCK), 1) % SGU_BLOCK) // CHUNK
    wm_s[...] = jnp.where(c <= r, wm_ref[...], 0.0).astype(BF16)
    if wmt_ref is not None:
        rt = (lax.broadcasted_iota(jnp.int32, (4 * SGU_BLOCK, SGU_BLOCK), 0) % SGU_BLOCK) // CHUNK
        ct = lax.broadcasted_iota(jnp.int32, (4 * SGU_BLOCK, SGU_BLOCK), 1) // CHUNK
        wmt_s[...] = jnp.where(rt <= ct, wmt_ref[...], 0.0).astype(BF16)


def _vstack(v_blk):
    grp = _lane_group(GROUP)
    return jnp.concatenate([jnp.where(grp == h, v_blk, 0.0) for h in range(4)], axis=0).astype(BF16)


def _fwd_layer(x, wi, bin_, caw, cbw, s256, seg, pw, wm, sb, wo, v1024, *, tile):
    S = x.shape[0]
    T = tile
    nt = S // T
    alpha = float((2.0 * 4) ** 0.25)

    def body(x_ref, wi_ref, bin_ref, caw_ref, cbw_ref, s256_ref, seg_ref, pw_ref, wm_ref, sb_ref, wo_ref, v1024_ref,
             y_ref, xb_ref, h_ref, aux_ref, mix_ref, z_ref, abuf, bbuf, cbuf, wm_s):
        i = pl.program_id(0)

        @pl.when(i == 0)
        def _():
            abuf[0:HALO_A, :] = jnp.zeros((HALO_A, GROUP), F32)
            bbuf[0:HALO_B, :] = jnp.zeros((HALO_B, GROUP), F32)
            cbuf[0:HALO_C, :] = jnp.zeros((HALO_C, GROUP), F32)
            _sgu_masks(wm_ref, None, wm_s, None)

        x = x_ref[...]
        xb = x.astype(BF16)
        xb_ref[...] = xb
        for k in range(N_CHIPS):
            h_ref[:, COLS * k:COLS * (k + 1)] = _dot(xb, wi_ref[k]) + bin_ref[:, COLS * k:COLS * (k + 1)]

        def hs(j):
            return h_ref[:, GROUP * j:GROUP * (j + 1)]

        abuf[HALO_A:HALO_A + T, :] = hs(0) * _sig(hs(1))
        for r0 in range(0, T, ROWS):
            acc = None
            for k in range(KA):
                off = HALO_A - (KA - 1) + k + r0
                term = caw_ref[k:k + 1, :] * abuf[off:off + ROWS, :]
                acc = term if acc is None else acc + term
            aux_ref[r0:r0 + ROWS, 0:GROUP] = acc + s256_ref[0:1, :]
        abuf[0:HALO_A, :] = abuf[T:T + HALO_A, :]
        a1 = aux_ref[:, 0:GROUP]
        segm = seg_ref[...]
        cen = a1 - _segdot(a1, segm)
        var = _segdot(cen * cen, segm)
        a2 = cen * lax.rsqrt(var + LN_EPS) * s256_ref[1:2, :] + s256_ref[2:3, :]
        az = hs(2)
        mix_ref[:, 0:GROUP] = (a2 * _sig(a2) * (az * _sig(az))).astype(BF16)

        bbuf[HALO_B:HALO_B + T, :] = hs(4) * hs(5)
        for r0 in range(0, T, ROWS):
            acc = None
            for k in range(KB):
                off = HALO_B - (KB - 1) + k + r0
                term = cbw_ref[k:k + 1, :] * bbuf[off:off + ROWS, :]
                acc = term if acc is None else acc + term
            aux_ref[r0:r0 + ROWS, GROUP:2 * GROUP] = acc
        bbuf[0:HALO_B, :] = bbuf[T:T + HALO_B, :]
        bz = hs(6)
        mix_ref[:, GROUP:2 * GROUP] = (hs(3) * aux_ref[:, GROUP:2 * GROUP] * (bz * _sig(bz))).astype(BF16)

        ch = hs(7)
        cbuf[HALO_C:HALO_C + T, :] = ch
        hi_lane = (lax.broadcasted_iota(jnp.int32, (1, 128), 1) // HEAD) == 1
        for r0 in range(0, T, ROWS):
            def win(col, j0, j1):
                s = None
                for j in range(j0, j1):
                    off = HALO_C - j + r0
                    term = cbuf[off:off + ROWS, 128 * col:128 * (col + 1)]
                    s = term if s is None else s + term
                return s
            w0 = win(0, 0, 2) + jnp.where(hi_lane, win(0, 2, 4), 0.0)
            w1 = win(1, 0, 8) + jnp.where(hi_lane, win(1, 8, 16), 0.0)
            aux_ref[r0:r0 + ROWS, 2 * GROUP:2 * GROUP + 128] = w0
            aux_ref[r0:r0 + ROWS, 2 * GROUP + 128:3 * GROUP] = w1
        cbuf[0:HALO_C, :] = cbuf[T:T + HALO_C, :]
        pooled = aux_ref[:, 2 * GROUP:3 * GROUP] / _pool_cnt(i, T) - ch
        aux_ref[:, 2 * GROUP:3 * GROUP] = pooled
        q = _dot(pooled.astype(BF16), pw_ref[...])
        cz = hs(8)
        mix_ref[:, 2 * GROUP:3 * GROUP] = (q * s256_ref[3:4, :] * (cz * _sig(cz))).astype(BF16)

        dv = hs(10)
        cen = dv - _rowmean(dv)
        var = _rowmean(cen * cen)
        v = cen * lax.rsqrt(var + LN_EPS) * s256_ref[4:5, :] + s256_ref[5:6, :]
        sps = []
        for n in range(T // SGU_BLOCK):
            vb = v[n * SGU_BLOCK:(n + 1) * SGU_BLOCK, :]
            sps.append(_dot(wm_s[...], _vstack(vb)) + sb_ref[...])
        sp = jnp.concatenate(sps, axis=0)
        dz = hs(11)
        mix_ref[:, 3 * GROUP:4 * GROUP] = (hs(9) * sp * (dz * _sig(dz))).astype(BF16)

        out = v1024_ref[0:1, :]
        for k in range(N_CHIPS):
            out = out + _dot(mix_ref[:, GROUP * k:GROUP * (k + 1)], wo_ref[k])
        z = alpha * x + out
        z_ref[...] = z
        cen = z - _rowmean(z)
        var = _rowmean(cen * cen)
        y_ref[...] = cen * lax.rsqrt(var + LN_EPS) * v1024_ref[1:2, :] + v1024_ref[2:3, :]

    def full(a):
        nd = a.ndim
        return pl.BlockSpec(a.shape, lambda i, _n=nd: (0,) * _n)

    def rows(width):
        return pl.BlockSpec((T, width), lambda i: (i, 0))

    consts = (wi, bin_, caw, cbw, s256, seg, pw, wm, sb, wo, v1024)
    return pl.pallas_call(
        body, name="fwd_layer",
        grid=(nt,),
        in_specs=[rows(D_MODEL)] + [full(a) for a in consts],
        out_specs=[rows(D_MODEL), rows(D_MODEL), rows(IN_WIDTH), rows(3 * GROUP), rows(D_MODEL), rows(D_MODEL)],
        out_shape=[jax.ShapeDtypeStruct((S, D_MODEL), F32), jax.ShapeDtypeStruct((S, D_MODEL), BF16),
                   jax.ShapeDtypeStruct((S, IN_WIDTH), F32), jax.ShapeDtypeStruct((S, 3 * GROUP), F32),
                   jax.ShapeDtypeStruct((S, D_MODEL), BF16), jax.ShapeDtypeStruct((S, D_MODEL), F32)],
        scratch_shapes=[pltpu.VMEM((T + HALO_A, GROUP), F32), pltpu.VMEM((T + HALO_B, GROUP), F32),
                        pltpu.VMEM((T + HALO_C, GROUP), F32), pltpu.VMEM((SGU_BLOCK, 4 * SGU_BLOCK), BF16)],
        compiler_params=_vmem_params(dimension_semantics=("arbitrary",)),
    )(x, *consts)


def _loss_head(y, target, *, tile):
    S = y.shape[0]
    T = tile

    def body(y_ref, t_ref, dy_ref, acc_ref):
        @pl.when(pl.program_id(0) == 0)
        def _():
            acc_ref[...] = jnp.zeros_like(acc_ref)
        err = y_ref[...] - t_ref[...]
        dy_ref[...] = err * (1.0 / D_MODEL)
        acc_ref[...] += jnp.sum(_colsum(err * err), axis=1, keepdims=True) * (0.5 / D_MODEL)

    return pl.pallas_call(
        body, name="loss_head",
        grid=(S // T,),
        in_specs=[pl.BlockSpec((T, D_MODEL), lambda i: (i, 0))] * 2,
        out_specs=[pl.BlockSpec((T, D_MODEL), lambda i: (i, 0)), pl.BlockSpec((8, 128), lambda i: (0, 0))],
        out_shape=[jax.ShapeDtypeStruct((S, D_MODEL), F32), jax.ShapeDtypeStruct((8, 128), F32)],
        compiler_params=_vmem_params(dimension_semantics=("arbitrary",)),
    )(y, target)


S256_ROWS = 48
ROW_CBW = 8
ROW_CAW = 16


def _bwd_layer(dy, z, h, aux, wi, caw, cbw, s256, seg, pw, wm, wmt, sb, wo, v1024, e4, *, tile):
    S = dy.shape[0]
    T = tile
    nt = S // T
    alpha = float((2.0 * 4) ** 0.25)

    def body(dy_ref, z_ref, h_ref, aux_ref, wi_ref, caw_ref, cbw_ref, s256_ref, seg_ref, pw_ref, wm_ref, wmt_ref,
             sb_ref, wo_ref, v1024_ref, e4_ref,
             dx_ref, dhb_ref, dzb_ref, o1024_ref, obin_ref, o256_ref, opw_ref, owc_ref, osb_ref,
             dbuf, ebuf, fbuf, a0_s, u_s, wm_s, wmt_s, dsp_acc, pw_acc):
        i = pl.program_id(0)
        tile_idx = nt - 1 - i

        @pl.when(i == 0)
        def _():
            dbuf[T:T + HALO_A, :] = jnp.zeros((HALO_A, GROUP), F32)
            ebuf[T:T + HALO_B, :] = jnp.zeros((HALO_B, GROUP), F32)
            fbuf[T:T + HALO_C, :] = jnp.zeros((HALO_C, GROUP), F32)
            _sgu_masks(wm_ref, wmt_ref, wm_s, wmt_s)
            o1024_ref[...] = jnp.zeros_like(o1024_ref)
            obin_ref[...] = jnp.zeros_like(obin_ref)
            o256_ref[...] = jnp.zeros_like(o256_ref)
            owc_ref[...] = jnp.zeros_like(owc_ref)
            dsp_acc[...] = jnp.zeros_like(dsp_acc)
            pw_acc[...] = jnp.zeros_like(pw_acc)

        def hs(j):
            return h_ref[:, GROUP * j:GROUP * (j + 1)]

        def put_dh(j, val):
            obin_ref[0:1, GROUP * j:GROUP * (j + 1)] += _colsum(val)
            dhb_ref[:, GROUP * j:GROUP * (j + 1)] = val.astype(BF16)

        def dsilu(v, s):
            return s * (1.0 + v * (1.0 - s))

        dy = dy_ref[...]
        z = z_ref[...]
        cen = z - _rowmean(z)
        rstd = lax.rsqrt(_rowmean(cen * cen) + LN_EPS)
        xhat = cen * rstd
        o1024_ref[0:1, :] += _colsum(dy * xhat)
        o1024_ref[1:2, :] += _colsum(dy)
        gdy = dy * v1024_ref[1:2, :]
        dz = rstd * (gdy - _rowmean(gdy) - xhat * _rowmean(gdy * xhat))
        o1024_ref[2:3, :] += _colsum(dz)
        dzb = dz.astype(BF16)
        dzb_ref[...] = dzb

        def dmix(k):
            return _dot_nt(dzb, wo_ref[k])

        segm = seg_ref[...]

        a_val, a_glu, a_z = hs(0), hs(1), hs(2)
        sg = _sig(a_glu)
        a0_s[...] = a_val * sg
        a1 = aux_ref[:, 0:GROUP]
        cen = a1 - _segdot(a1, segm)
        rstd_a = lax.rsqrt(_segdot(cen * cen, segm) + LN_EPS)
        xh = cen * rstd_a
        a2 = xh * s256_ref[1:2, :] + s256_ref[2:3, :]
        s2 = _sig(a2)
        sz = _sig(a_z)
        dya = dmix(0)
        put_dh(2, dya * (a2 * s2) * dsilu(a_z, sz))
        d_a2 = dya * (a_z * sz) * dsilu(a2, s2)
        o256_ref[1:2, :] += _colsum(d_a2 * xh)
        o256_ref[2:3, :] += _colsum(d_a2)
        gd = d_a2 * s256_ref[1:2, :]
        d_a1 = rstd_a * (gd - _segdot(gd, segm) - xh * _segdot(gd * xh, segm))
        o256_ref[0:1, :] += _colsum(d_a1)
        dbuf[0:T, :] = d_a1
        for r0 in range(0, T, ROWS):
            a0c = a0_s[r0:r0 + ROWS, :]
            acc = None
            for k in range(KA):
                off = (KA - 1) - k + r0
                w = dbuf[off:off + ROWS, :]
                term = caw_ref[k:k + 1, :] * w
                acc = term if acc is None else acc + term
                o256_ref[ROW_CAW + k:ROW_CAW + k + 1, :] += _colsum(a0c * w)
            u_s[r0:r0 + ROWS, :] = acc
        dbuf[T:T + HALO_A, :] = dbuf[0:HALO_A, :]
        d_a0 = u_s[...]
        put_dh(0, d_a0 * sg)
        put_dh(1, d_a0 * a_val * sg * (1.0 - sg))

        b_b, b_c, b_h, b_z = hs(3), hs(4), hs(5), hs(6)
        cb = aux_ref[:, GROUP:2 * GROUP]
        sz = _sig(b_z)
        dyb = dmix(1)
        put_dh(3, dyb * cb * (b_z * sz))
        put_dh(6, dyb * b_b * cb * dsilu(b_z, sz))
        ebuf[0:T, :] = dyb * b_b * (b_z * sz)
        a0_s[...] = b_c * b_h
        for r0 in range(0, T, ROWS):
            uc = a0_s[r0:r0 + ROWS, :]
            acc = None
            for k in range(KB):
                off = (KB - 1) - k + r0
                w = ebuf[off:off + ROWS, :]
                term = cbw_ref[k:k + 1, :] * w
                acc = term if acc is None else acc + term
                o256_ref[ROW_CBW + k:ROW_CBW + k + 1, :] += _colsum(uc * w)
            u_s[r0:r0 + ROWS, :] = acc
        ebuf[T:T + HALO_B, :] = ebuf[0:HALO_B, :]
        d_u = u_s[...]
        put_dh(4, d_u * b_h)
        put_dh(5, d_u * b_c)

        c_z = hs(8)
        pooled = aux_ref[:, 2 * GROUP:3 * GROUP]
        pooled_b = pooled.astype(BF16)
        q = _dot(pooled_b, pw_ref[...])
        sz = _sig(c_z)
        dyc = dmix(2)
        ps = s256_ref[3:4, :]
        o256_ref[3:4, :] += _colsum(dyc * q * (c_z * sz))
        put_dh(8, dyc * q * ps * dsilu(c_z, sz))
        d_q = (dyc * ps * (c_z * sz)).astype(BF16)
        pw_acc[...] += _dot_tn(pooled_b, d_q)
        d_pooled = _dot_nt(d_q, pw_ref[...])
        fbuf[0:T, :] = d_pooled / _pool_cnt(tile_idx, T)
        hi_lane = (lax.broadcasted_iota(jnp.int32, (1, 128), 1) // HEAD) == 1
        for r0 in range(0, T, ROWS):
            def win(col, j0, j1):
                s = None
                for j in range(j0, j1):
                    term = fbuf[r0 + j:r0 + j + ROWS, 128 * col:128 * (col + 1)]
                    s = term if s is None else s + term
                return s
            u_s[r0:r0 + ROWS, 0:128] = win(0, 0, 2) + jnp.where(hi_lane, win(0, 2, 4), 0.0)
            u_s[r0:r0 + ROWS, 128:256] = win(1, 0, 8) + jnp.where(hi_lane, win(1, 8, 16), 0.0)
        fbuf[T:T + HALO_C, :] = fbuf[0:HALO_C, :]
        put_dh(7, u_s[...] - d_pooled)

        d_u_, d_v_, d_z_ = hs(9), hs(10), hs(11)
        cen = d_v_ - _rowmean(d_v_)
        rstd_v = lax.rsqrt(_rowmean(cen * cen) + LN_EPS)
        xv = cen * rstd_v
        v = xv * s256_ref[4:5, :] + s256_ref[5:6, :]
        sz = _sig(d_z_)
        dyd = dmix(3)
        d_sp = dyd * d_u_ * (d_z_ * sz)
        grp = _lane_group(GROUP)
        sps, dvs = [], []
        for n in range(T // SGU_BLOCK):
            blk = slice(n * SGU_BLOCK, (n + 1) * SGU_BLOCK)
            vst = _vstack(v[blk, :])
            sps.append(_dot(wm_s[...], vst) + sb_ref[...])
            dspb = d_sp[blk, :]
            dsp_acc[...] += dspb
            dspb16 = dspb.astype(BF16)
            dvst = _dot(wmt_s[...], dspb16)
            dvb = None
            for hh in range(4):
                part = jnp.where(grp == hh, dvst[hh * SGU_BLOCK:(hh + 1) * SGU_BLOCK, :], 0.0)
                dvb = part if dvb is None else dvb + part
            dvs.append(dvb)
            owc_ref[...] += _dot_nt(dspb16, vst)
        sp = jnp.concatenate(sps, axis=0)
        d_v = jnp.concatenate(dvs, axis=0)
        put_dh(9, dyd * sp * (d_z_ * sz))
        put_dh(11, dyd * d_u_ * sp * dsilu(d_z_, sz))
        o256_ref[4:5, :] += _colsum(d_v * xv)
        o256_ref[5:6, :] += _colsum(d_v)
        gd = d_v * s256_ref[4:5, :]
        put_dh(10, rstd_v * (gd - _rowmean(gd) - xv * _rowmean(gd * xv)))

        dx = alpha * dz
        for k in range(N_CHIPS):
            dx = dx + _dot_nt(dhb_ref[:, COLS * k:COLS * (k + 1)], wi_ref[k])
        dx_ref[...] = dx

        @pl.when(i == nt - 1)
        def _():
            r = lax.broadcasted_iota(jnp.int32, (SGU_BLOCK, 4 * SGU_BLOCK), 0) // CHUNK
            c = (lax.broadcasted_iota(jnp.int32, (SGU_BLOCK, 4 * SGU_BLOCK), 1) % SGU_BLOCK) // CHUNK
            owc_ref[...] = jnp.where(c <= r, owc_ref[...], 0.0)
            osb_ref[...] = _segdot(dsp_acc[...], e4_ref[...])
            for g in range(4):
                opw_ref[:, HEAD * g:HEAD * (g + 1)] = pw_acc[HEAD * g:HEAD * (g + 1), HEAD * g:HEAD * (g + 1)]

    def full(a):
        nd = a.ndim
        return pl.BlockSpec(a.shape, lambda i, _n=nd: (0,) * _n)

    def rows(width):
        return pl.BlockSpec((T, width), lambda i: (nt - 1 - i, 0))

    def acc(shape):
        return pl.BlockSpec(shape, lambda i: (0, 0))

    consts = (wi, caw, cbw, s256, seg, pw, wm, wmt, sb, wo, v1024, e4)
    acc_shapes = [(8, D_MODEL), (8, IN_WIDTH), (S256_ROWS, GROUP), (HEAD, GROUP), (SGU_BLOCK, 4 * SGU_BLOCK),
                  (SGU_BLOCK, 128)]
    return pl.pallas_call(
        body, name="bwd_layer",
        grid=(nt,),
        in_specs=[rows(D_MODEL), rows(D_MODEL), rows(IN_WIDTH), rows(3 * GROUP)] + [full(a) for a in consts],
        out_specs=[rows(D_MODEL), rows(IN_WIDTH), rows(D_MODEL)] + [acc(s) for s in acc_shapes],
        out_shape=[jax.ShapeDtypeStruct((S, D_MODEL), F32), jax.ShapeDtypeStruct((S, IN_WIDTH), BF16),
                   jax.ShapeDtypeStruct((S, D_MODEL), BF16)] + [jax.ShapeDtypeStruct(s, F32) for s in acc_shapes],
        scratch_shapes=[pltpu.VMEM((T + HALO_A, GROUP), F32), pltpu.VMEM((T + HALO_B, GROUP), F32),
                        pltpu.VMEM((T + HALO_C, GROUP), F32), pltpu.VMEM((T, GROUP), F32), pltpu.VMEM((T, GROUP), F32),
                        pltpu.VMEM((SGU_BLOCK, 4 * SGU_BLOCK), BF16), pltpu.VMEM((4 * SGU_BLOCK, SGU_BLOCK), BF16),
                        pltpu.VMEM((SGU_BLOCK, GROUP), F32), pltpu.VMEM((GROUP, GROUP), F32)],
        compiler_params=_vmem_params(dimension_semantics=("arbitrary",)),
    )(dy, z, h, aux, *consts)


def _dw_proj(layer, lhs, rhs, slab, *, lhs_cols, rhs_cols, by_lhs, tk, name):
    S = lhs.shape[0]

    def body(l_ref, a_ref, b_ref, slab_ref, o_ref):
        del l_ref, slab_ref

        @pl.when(pl.program_id(1) == 0)
        def _():
            o_ref[...] = jnp.zeros_like(o_ref)
        o_ref[...] += _dot_tn(a_ref[...], b_ref[...])

    if by_lhs:
        a_spec = pl.BlockSpec((tk, lhs_cols), lambda j, s, l: (s, j))
        b_spec = pl.BlockSpec((tk, rhs_cols), lambda j, s, l: (s, 0))
    else:
        a_spec = pl.BlockSpec((tk, lhs_cols), lambda j, s, l: (s, 0))
        b_spec = pl.BlockSpec((tk, rhs_cols), lambda j, s, l: (s, j))
    grid_spec = pltpu.PrefetchScalarGridSpec(
        num_scalar_prefetch=1, grid=(N_CHIPS, S // tk),
        in_specs=[a_spec, b_spec, pl.BlockSpec(memory_space=pl.ANY)],
        out_specs=pl.BlockSpec((None, None, lhs_cols, rhs_cols), lambda j, s, l: (l[0], j, 0, 0)))
    return pl.pallas_call(
        body, name=name, grid_spec=grid_spec,
        out_shape=jax.ShapeDtypeStruct(slab.shape, F32),
        input_output_aliases={3: 0},
        compiler_params=_vmem_params(dimension_semantics=("arbitrary", "arbitrary")),
    )(layer, lhs, rhs, slab)


def _adamw(w, g, m, v, *, rows_per_step, name):
    R, C = w.shape
    tr = rows_per_step
    c1 = 1.0 - ADAM_B1 ** ADAM_STEP
    c2 = 1.0 - ADAM_B2 ** ADAM_STEP

    def body(w_ref, g_ref, m_ref, v_ref, d_ref, nm_ref, nv_ref):
        g_ = g_ref[...]
        nm = ADAM_B1 * m_ref[...] + (1.0 - ADAM_B1) * g_
        nv = ADAM_B2 * v_ref[...] + (1.0 - ADAM_B2) * (g_ * g_)
        nm_ref[...] = nm
        nv_ref[...] = nv
        d_ref[...] = -ADAM_LR * ((nm / c1) / (jnp.sqrt(nv / c2) + ADAM_EPS) + ADAM_WD * w_ref[...])

    spec = pl.BlockSpec((tr, C), lambda i: (i, 0))
    return pl.pallas_call(
        body, name=name, grid=(R // tr,),
        in_specs=[spec] * 4, out_specs=[spec] * 3,
        out_shape=[jax.ShapeDtypeStruct((R, C), F32)] * 3,
        compiler_params=_vmem_params(dimension_semantics=("arbitrary",)),
    )(w, g, m, v)


def _place():
    return lax.axis_index("x"), lax.axis_index("y"), lax.axis_index("c")


def _other_chips(x, y):
    return [(1 - x, y, 2 * (1 - x) + y), (x, 1 - y, 2 * x + (1 - y)), (1 - x, 1 - y, 2 * (1 - x) + (1 - y))]


ANY = pl.BlockSpec(memory_space=pl.ANY)


def _gather_weights(wi16, wo16, cw):
    L = wi16.shape[0]
    hi_rows, ho_rows = D_MODEL // 2, GROUP // 2
    n_ici = 2 * L + 1
    n_fwd = 2 * L

    def body(wi_ref, wo_ref, cw_ref, *rest):
        wig = rest[0:L]
        wog = rest[L:2 * L]
        cwg = rest[2 * L]
        send_sems, recv_sems, loc_sems = rest[2 * L + 1:]
        x, y, c = _place()
        me_k = 2 * x + y
        sibling = (x, y, 1 - c)
        chips = _other_chips(x, y)

        def half_i(ref, blk):
            return ref.at[blk, pl.ds(c * hi_rows, hi_rows), :]

        def half_o(ref, blk):
            return ref.at[blk, pl.ds(c * ho_rows, ho_rows), :]

        def other_half_i(ref, blk):
            return ref.at[blk, pl.ds((1 - c) * hi_rows, hi_rows), :]

        def other_half_o(ref, blk):
            return ref.at[blk, pl.ds((1 - c) * ho_rows, ho_rows), :]

        local = []
        for l in range(L):
            local.append(pltpu.make_async_copy(wi_ref.at[l], wig[l].at[me_k], loc_sems.at[2 * l]))
            local.append(pltpu.make_async_copy(wo_ref.at[l], wog[l].at[me_k], loc_sems.at[2 * l + 1]))
        local.append(pltpu.make_async_copy(cw_ref, cwg.at[me_k], loc_sems.at[2 * L]))
        for cp in local:
            cp.start()

        def remote(src, dst, sem, to):
            return pltpu.make_async_remote_copy(src_ref=src, dst_ref=dst, send_sem=send_sems.at[sem],
                                                recv_sem=recv_sems.at[sem], device_id=to, device_id_type=MESH)

        sends = []
        for r, (px, py, _) in enumerate(chips):
            to = (px, py, c)
            for l in range(L):
                sends.append(remote(half_i(wi_ref, l), half_i(wig[l], me_k), r * n_ici + 2 * l, to))
                sends.append(remote(half_o(wo_ref, l), half_o(wog[l], me_k), r * n_ici + 2 * l + 1, to))
            sends.append(remote(cw_ref, cwg.at[me_k], r * n_ici + 2 * L, to))
        for cp in sends:
            cp.start()

        base = 3 * n_ici
        fwds = []
        for r, (px, py, pk) in enumerate(chips):
            for l in range(L):
                remote(half_i(wig[l], pk), half_i(wig[l], pk), r * n_ici + 2 * l, sibling).wait_recv()
                f = remote(half_i(wig[l], pk), half_i(wig[l], pk), base + r * n_fwd + 2 * l, sibling)
                f.start()
                fwds.append(f)
                remote(half_o(wog[l], pk), half_o(wog[l], pk), r * n_ici + 2 * l + 1, sibling).wait_recv()
                f = remote(half_o(wog[l], pk), half_o(wog[l], pk), base + r * n_fwd + 2 * l + 1, sibling)
                f.start()
                fwds.append(f)
            remote(cwg.at[pk], cwg.at[pk], r * n_ici + 2 * L, sibling).wait_recv()
        for r, (px, py, pk) in enumerate(chips):
            for l in range(L):
                remote(other_half_i(wig[l], pk), other_half_i(wig[l], pk), base + r * n_fwd + 2 * l, sibling).wait_recv()
                remote(other_half_o(wog[l], pk), other_half_o(wog[l], pk), base + r * n_fwd + 2 * l + 1, sibling).wait_recv()
        for cp in sends + fwds:
            cp.wait_send()
        for cp in local:
            cp.wait()

    n_sem = 3 * n_ici + 3 * n_fwd
    out_shape = ([jax.ShapeDtypeStruct((N_CHIPS, D_MODEL, COLS), BF16)] * L
                 + [jax.ShapeDtypeStruct((N_CHIPS, GROUP, D_MODEL), BF16)] * L
                 + [jax.ShapeDtypeStruct((N_CHIPS,) + cw.shape, F32)])
    outs = pl.pallas_call(
        body, name="gather_weights",
        in_specs=[ANY, ANY, ANY], out_specs=[ANY] * (2 * L + 1), out_shape=out_shape,
        scratch_shapes=[pltpu.SemaphoreType.DMA((n_sem,)), pltpu.SemaphoreType.DMA((n_sem,)),
                        pltpu.SemaphoreType.DMA((2 * L + 1,))],
        compiler_params=pltpu.CompilerParams(has_side_effects=True),
    )(wi16, wo16, cw)
    return outs[0:L], outs[L:2 * L], outs[2 * L]


def _swap_halves(gwi, gwo):
    L = gwi.shape[0]
    hi_rows, ho_rows = D_MODEL // 2, GROUP // 2

    def body(gwi_ref, gwo_ref, ri_ref, ro_ref, send_sems, recv_sems):
        x, y, c = _place()
        sibling = (x, y, 1 - c)
        cps = [
            pltpu.make_async_remote_copy(src_ref=gwi_ref.at[:, :, pl.ds((1 - c) * hi_rows, hi_rows), :], dst_ref=ri_ref,
                                         send_sem=send_sems.at[0], recv_sem=recv_sems.at[0], device_id=sibling,
                                         device_id_type=MESH),
            pltpu.make_async_remote_copy(src_ref=gwo_ref.at[:, :, pl.ds((1 - c) * ho_rows, ho_rows), :], dst_ref=ro_ref,
                                         send_sem=send_sems.at[1], recv_sem=recv_sems.at[1], device_id=sibling,
                                         device_id_type=MESH),
        ]
        for cp in cps:
            cp.start()
        for cp in cps:
            cp.wait()

    return pl.pallas_call(
        body, name="swap_halves",
        in_specs=[ANY, ANY], out_specs=[ANY, ANY],
        out_shape=[jax.ShapeDtypeStruct((L, N_CHIPS, hi_rows, COLS), F32),
                   jax.ShapeDtypeStruct((L, N_CHIPS, ho_rows, D_MODEL), F32)],
        scratch_shapes=[pltpu.SemaphoreType.DMA((2,)), pltpu.SemaphoreType.DMA((2,))],
        compiler_params=pltpu.CompilerParams(has_side_effects=True),
    )(gwi, gwo)


def _add_halves(c_arr, g, r, *, rows, cols, tr, name):
    L = g.shape[0]
    nb = rows // tr

    def body(c_ref, g_ref, r_ref, o_ref):
        del c_ref
        o_ref[...] = g_ref[...] + r_ref[...]

    grid_spec = pltpu.PrefetchScalarGridSpec(
        num_scalar_prefetch=1, grid=(L, N_CHIPS, nb),
        in_specs=[pl.BlockSpec((None, None, tr, cols), lambda l, k, i, c: (l, k, c[0] * nb + i, 0)),
                  pl.BlockSpec((None, None, tr, cols), lambda l, k, i, c: (l, k, i, 0))],
        out_specs=pl.BlockSpec((None, None, tr, cols), lambda l, k, i, c: (l, k, i, 0)))
    return pl.pallas_call(
        body, name=name, grid_spec=grid_spec,
        out_shape=jax.ShapeDtypeStruct((L, N_CHIPS, rows, cols), F32),
        compiler_params=_vmem_params(dimension_semantics=("arbitrary",) * 3),
    )(c_arr, g, r)


def _exchange_chunks(pi, po):
    L = pi.shape[0]
    hi_rows, ho_rows = pi.shape[2], po.shape[2]

    def body(pi_ref, po_ref, ri_ref, ro_ref, send_sems, recv_sems):
        x, y, c = _place()
        cps = []
        for r, (px, py, pk) in enumerate(_other_chips(x, y)):
            to = (px, py, c)
            cps.append(pltpu.make_async_remote_copy(src_ref=pi_ref.at[:, pk], dst_ref=ri_ref.at[r],
                                                    send_sem=send_sems.at[2 * r], recv_sem=recv_sems.at[2 * r],
                                                    device_id=to, device_id_type=MESH))
            cps.append(pltpu.make_async_remote_copy(src_ref=po_ref.at[:, pk], dst_ref=ro_ref.at[r],
                                                    send_sem=send_sems.at[2 * r + 1], recv_sem=recv_sems.at[2 * r + 1],
                                                    device_id=to, device_id_type=MESH))
        for cp in cps:
            cp.start()
        for cp in cps:
            cp.wait()

    return pl.pallas_call(
        body, name="exchange_chunks",
        in_specs=[ANY, ANY], out_specs=[ANY, ANY],
        out_shape=[jax.ShapeDtypeStruct((3, L, hi_rows, COLS), F32),
                   jax.ShapeDtypeStruct((3, L, ho_rows, D_MODEL), F32)],
        scratch_shapes=[pltpu.SemaphoreType.DMA((6,)), pltpu.SemaphoreType.DMA((6,))],
        compiler_params=pltpu.CompilerParams(has_side_effects=True),
    )(pi, po)


def _sum_chunks(kc_arr, p, r, *, rows, cols, tr, name):
    L = p.shape[0]
    nb = rows // tr

    def body(kc_ref, p_ref, r0_ref, r1_ref, r2_ref, o_ref):
        del kc_ref
        o_ref[...] = ((p_ref[...] + r0_ref[...]) + r1_ref[...]) + r2_ref[...]

    def rspec(j):
        return pl.BlockSpec((None, None, tr, cols), lambda l, i, kc, _j=j: (_j, l, i, 0))

    grid_spec = pltpu.PrefetchScalarGridSpec(
        num_scalar_prefetch=1, grid=(L, nb),
        in_specs=[pl.BlockSpec((None, None, tr, cols), lambda l, i, kc: (l, kc[0], i, 0)), rspec(0), rspec(1), rspec(2)],
        out_specs=pl.BlockSpec((None, tr, cols), lambda l, i, kc: (l, kc[1] * nb + i, 0)))
    return pl.pallas_call(
        body, name=name, grid_spec=grid_spec,
        out_shape=jax.ShapeDtypeStruct((L, 2 * rows, cols), F32),
        compiler_params=_vmem_params(dimension_semantics=("arbitrary",) * 2),
    )(kc_arr, p, r, r, r)


def _share_result(gi, go):
    hi_rows, ho_rows = gi.shape[1] // 2, go.shape[1] // 2

    def body(gi_ref, go_ref, oi_ref, oo_ref, send_sems, recv_sems):
        del gi_ref, go_ref
        x, y, c = _place()
        sibling = (x, y, 1 - c)
        cps = []
        for j, (ref, n) in enumerate(((oi_ref, hi_rows), (oo_ref, ho_rows))):
            mine = ref.at[:, pl.ds(c * n, n), :]
            cps.append(pltpu.make_async_remote_copy(src_ref=mine, dst_ref=mine, send_sem=send_sems.at[j],
                                                    recv_sem=recv_sems.at[j], device_id=sibling, device_id_type=MESH))
        for cp in cps:
            cp.start()
        for j, (ref, n) in enumerate(((oi_ref, hi_rows), (oo_ref, ho_rows))):
            theirs = ref.at[:, pl.ds((1 - c) * n, n), :]
            pltpu.make_async_remote_copy(src_ref=theirs, dst_ref=theirs, send_sem=send_sems.at[j],
                                         recv_sem=recv_sems.at[j], device_id=sibling, device_id_type=MESH).wait_recv()
        for cp in cps:
            cp.wait_send()

    return pl.pallas_call(
        body, name="share_result",
        in_specs=[ANY, ANY], out_specs=[ANY, ANY],
        out_shape=[jax.ShapeDtypeStruct(gi.shape, F32), jax.ShapeDtypeStruct(go.shape, F32)],
        input_output_aliases={0: 0, 1: 1},
        scratch_shapes=[pltpu.SemaphoreType.DMA((2,)), pltpu.SemaphoreType.DMA((2,))],
        compiler_params=pltpu.CompilerParams(has_side_effects=True),
    )(gi, go)


def _allreduce_small(g):
    def body(g_ref, o_ref, rbuf, send_sems, recv_sems):
        x, y, c = _place()
        o_ref[...] = g_ref[...]
        for step, peer in enumerate(((x, y, 1 - c), (1 - x, y, c), (x, 1 - y, c))):
            cp = pltpu.make_async_remote_copy(src_ref=o_ref, dst_ref=rbuf.at[step], send_sem=send_sems.at[step],
                                              recv_sem=recv_sems.at[step], device_id=peer, device_id_type=MESH)
            cp.start()
            cp.wait()
            o_ref[...] = o_ref[...] + rbuf[step]

    return pl.pallas_call(
        body, name="allreduce_small",
        in_specs=[pl.BlockSpec(memory_space=pltpu.VMEM)], out_specs=pl.BlockSpec(memory_space=pltpu.VMEM),
        out_shape=jax.ShapeDtypeStruct(g.shape, F32),
        scratch_shapes=[pltpu.VMEM((3,) + g.shape, F32), pltpu.SemaphoreType.DMA((3,)), pltpu.SemaphoreType.DMA((3,))],
        compiler_params=_vmem_params(has_side_effects=True),
    )(g)


SMALL = ("ln_g", "ln_b", "b_in", "conv_a_w", "conv_a_b", "norm_a_g", "norm_a_b", "conv_b_w", "pool_w", "pool_scale",
         "sgu_ln_g", "sgu_ln_b", "sgu_w", "sgu_bias", "b_out")
WEIGHTS = ("ln_g", "ln_b", "w_in", "b_in", "conv_a_w", "conv_a_b", "norm_a_g", "norm_a_b", "conv_b_w", "pool_w",
           "pool_scale", "sgu_ln_g", "sgu_ln_b", "sgu_w", "sgu_bias", "w_out", "b_out")


def _pad_rows(a, rows):
    return jnp.pad(a, ((0, rows - a.shape[0]), (0, 0)))


def _indicator_consts():
    seg = jnp.where((jnp.arange(GROUP)[:, None] // HEAD) == (jnp.arange(GROUP)[None, :] // HEAD),
                    1.0 / HEAD, 0.0).astype(BF16)
    e4 = ((jnp.arange(GROUP)[:, None] // HEAD) == jnp.arange(128)[None, :]).astype(BF16)
    return seg, e4


def _layer_consts(p, conv_full, l):
    same_head = jnp.eye(4, dtype=F32)[:, None, :, None] > 0
    caw = _pad_rows(conv_full[l, :KA], 32)
    cbw = _pad_rows(conv_full[l, KA:], 8)
    s256 = _pad_rows(jnp.stack([p["conv_a_b"][l], p["norm_a_g"][l], p["norm_a_b"][l], p["pool_scale"][l],
                                p["sgu_ln_g"][l], p["sgu_ln_b"][l]]), 8)
    pw = jnp.where(same_head, p["pool_w"][l][:, :, None, :], 0.0).reshape(GROUP, GROUP).astype(BF16)
    wm = jnp.transpose(p["sgu_w"][l], (1, 0, 2)).reshape(SGU_BLOCK, 4 * SGU_BLOCK)
    wmt = jnp.transpose(p["sgu_w"][l], (0, 2, 1)).reshape(4 * SGU_BLOCK, SGU_BLOCK)
    sb = jnp.repeat(p["sgu_bias"][l].T, HEAD, axis=1)
    v1024 = _pad_rows(jnp.stack([p["b_out"][l], p["ln_g"][l], p["ln_b"][l]]), 8)
    return dict(caw=caw, cbw=cbw, s256=s256, pw=pw, wm=wm, wmt=wmt, sb=sb, v1024=v1024, bin=p["b_in"][l][None, :])


def _step(p, m, v, x, target, *, tile_f, tile_b, tk):
    L = p["ln_g"].shape[0]
    S = x.shape[0]
    xi, yi, ci = _place()
    me_k = 2 * xi + yi

    cw = jnp.concatenate([p["conv_a_w"], p["conv_b_w"]], axis=1).reshape(-1, 128)
    cw_rows = cw.shape[0]
    cw = _pad_rows(cw, 72)
    wig, wog, cwg = _gather_weights(p["w_in"].astype(BF16), p["w_out"].astype(BF16), cw)
    cwg = cwg[:, :cw_rows].reshape(N_CHIPS, L, KA + KB, HEAD)
    conv_full = jnp.transpose(cwg, (1, 2, 0, 3)).reshape(L, KA + KB, GROUP)

    seg, e4 = _indicator_consts()
    consts = [_layer_consts(p, conv_full, l) for l in range(L)]

    hcur = x
    saved = []
    for l in range(L):
        k = consts[l]
        y, xb, h, aux, mixb, z = _fwd_layer(hcur, wig[l], k["bin"], k["caw"], k["cbw"], k["s256"], seg, k["pw"], k["wm"],
                                            k["sb"], wog[l], k["v1024"], tile=tile_f)
        saved.append((xb, h, aux, mixb, z))
        hcur = y

    dy, loss_acc = _loss_head(hcur, target, tile=tile_f)
    loss = lax.psum(loss_acc[0, 0], ("x", "y", "c"))

    gwi = lax.empty((L, N_CHIPS, D_MODEL, COLS), F32)
    gwo = lax.empty((L, N_CHIPS, GROUP, D_MODEL), F32)
    small = {}
    for l in reversed(range(L)):
        k = consts[l]
        xb, h, aux, mixb, z = saved[l]
        dy, dhb, dzb, o1024, obin, o256, opw, owc, osb = _bwd_layer(
            dy, z, h, aux, wig[l], k["caw"], k["cbw"], k["s256"], seg, k["pw"], k["wm"], k["wmt"], k["sb"], wog[l],
            k["v1024"], e4, tile=tile_b)
        larr = jnp.full((1,), l, jnp.int32)
        gwi = _dw_proj(larr, xb, dhb, gwi, lhs_cols=D_MODEL, rhs_cols=COLS, by_lhs=False, tk=tk, name="dw_in")
        gwo = _dw_proj(larr, mixb, dzb, gwo, lhs_cols=GROUP, rhs_cols=D_MODEL, by_lhs=True, tk=tk, name="dw_out")
        small[l] = dict(
            ln_g=o1024[0], ln_b=o1024[1], b_out=o1024[2], b_in=obin[0],
            conv_a_b=o256[0], norm_a_g=o256[1], norm_a_b=o256[2], pool_scale=o256[3], sgu_ln_g=o256[4], sgu_ln_b=o256[5],
            conv_b_w=o256[ROW_CBW:ROW_CBW + KB], conv_a_w=o256[ROW_CAW:ROW_CAW + KA],
            pool_w=jnp.transpose(opw.reshape(HEAD, 4, HEAD), (1, 0, 2)),
            sgu_w=jnp.transpose(owc.reshape(SGU_BLOCK, 4, SGU_BLOCK), (1, 0, 2)),
            sgu_bias=osb[:, 0:4].T)
    grad_x = dy

    parts, shapes = [], []
    for n in SMALL:
        a = jnp.stack([small[l][n] for l in range(L)])
        shapes.append((n, a.shape))
        parts.append(a.reshape(-1, 128))
    packed = _allreduce_small(jnp.concatenate(parts, axis=0))
    grads = {}
    r0 = 0
    for (n, shp), part in zip(shapes, parts):
        grads[n] = packed[r0:r0 + part.shape[0]].reshape(shp)
        r0 += part.shape[0]
    for n in ("conv_a_w", "conv_b_w"):
        grads[n] = lax.dynamic_slice_in_dim(grads[n], me_k * HEAD, HEAD, axis=2)

    c_arr = jnp.reshape(ci, (1,)).astype(jnp.int32)
    kc_arr = jnp.stack([me_k, ci]).astype(jnp.int32)
    ri, ro = _swap_halves(gwi, gwo)
    p_i = _add_halves(c_arr, gwi, ri, rows=D_MODEL // 2, cols=COLS, tr=256, name="add_halves_in")
    p_o = _add_halves(c_arr, gwo, ro, rows=GROUP // 2, cols=D_MODEL, tr=128, name="add_halves_out")
    qi, qo = _exchange_chunks(p_i, p_o)
    g_i = _sum_chunks(kc_arr, p_i, qi, rows=D_MODEL // 2, cols=COLS, tr=256, name="sum_chunks_in")
    g_o = _sum_chunks(kc_arr, p_o, qo, rows=GROUP // 2, cols=D_MODEL, tr=128, name="sum_chunks_out")
    g_i, g_o = _share_result(g_i, g_o)
    grads["w_in"] = g_i
    grads["w_out"] = g_o

    delta, new_m, new_v = {}, {}, {}
    for n in WEIGHTS:
        shp = p[n].shape
        if n in ("w_in", "w_out"):
            two_d = (shp[0] * shp[1], shp[2])
            tr = 512 if n == "w_in" else 256
        else:
            two_d = (-1, shp[-1])
            tr = None
        args = [a.reshape(two_d) for a in (p[n], grads[n], m[n], v[n])]
        d, nm, nv = _adamw(*args, rows_per_step=tr or args[0].shape[0], name="adamw_" + n)
        delta[n], new_m[n], new_v[n] = d.reshape(shp), nm.reshape(shp), nv.reshape(shp)

    return (loss, grad_x[None], *[grads[n] for n in WEIGHTS], *[delta[n] for n in WEIGHTS],
            *[new_m[n] for n in WEIGHTS], *[new_v[n] for n in WEIGHTS])


def kernel(x, ln_g, ln_b, w_in, b_in, conv_a_w, conv_a_b, norm_a_g, norm_a_b, conv_b_w, pool_w, pool_scale, sgu_ln_g, sgu_ln_b, sgu_w, sgu_bias, w_out, b_out, loss_target, m_ln_g, m_ln_b, m_w_in, m_b_in, m_conv_a_w, m_conv_a_b, m_norm_a_g, m_norm_a_b, m_conv_b_w, m_pool_w, m_pool_scale, m_sgu_ln_g, m_sgu_ln_b, m_sgu_w, m_sgu_bias, m_w_out, m_b_out, v_ln_g, v_ln_b, v_w_in, v_b_in, v_conv_a_w, v_conv_a_b, v_norm_a_g, v_norm_a_b, v_conv_b_w, v_pool_w, v_pool_scale, v_sgu_ln_g, v_sgu_ln_b, v_sgu_w, v_sgu_bias, v_w_out, v_b_out):
    p = dict(ln_g=ln_g, ln_b=ln_b, w_in=w_in, b_in=b_in, conv_a_w=conv_a_w, conv_a_b=conv_a_b, norm_a_g=norm_a_g,
             norm_a_b=norm_a_b, conv_b_w=conv_b_w, pool_w=pool_w, pool_scale=pool_scale, sgu_ln_g=sgu_ln_g,
             sgu_ln_b=sgu_ln_b, sgu_w=sgu_w, sgu_bias=sgu_bias, w_out=w_out, b_out=b_out)
    m = dict(ln_g=m_ln_g, ln_b=m_ln_b, w_in=m_w_in, b_in=m_b_in, conv_a_w=m_conv_a_w, conv_a_b=m_conv_a_b,
             norm_a_g=m_norm_a_g, norm_a_b=m_norm_a_b, conv_b_w=m_conv_b_w, pool_w=m_pool_w, pool_scale=m_pool_scale,
             sgu_ln_g=m_sgu_ln_g, sgu_ln_b=m_sgu_ln_b, sgu_w=m_sgu_w, sgu_bias=m_sgu_bias, w_out=m_w_out, b_out=m_b_out)
    v = dict(ln_g=v_ln_g, ln_b=v_ln_b, w_in=v_w_in, b_in=v_b_in, conv_a_w=v_conv_a_w, conv_a_b=v_conv_a_b,
             norm_a_g=v_norm_a_g, norm_a_b=v_norm_a_b, conv_b_w=v_conv_b_w, pool_w=v_pool_w, pool_scale=v_pool_scale,
             sgu_ln_g=v_sgu_ln_g, sgu_ln_b=v_sgu_ln_b, sgu_w=v_sgu_w, sgu_bias=v_sgu_bias, w_out=v_w_out, b_out=v_b_out)
    return _step(p, m, v, x[0], loss_target[0], tile_f=256, tile_b=256, tk=512)
```

```python
import functools

import jax
import jax.numpy as jnp
from jax import lax
from jax.experimental import pallas as pl
from jax.experimental.pallas import tpu as pltpu

F32 = jnp.float32
BF16 = jnp.bfloat16
MESH = pl.DeviceIdType.MESH

D_MODEL = 1024
GROUP = 256
HEAD = 64
N_SLICES = 12
IN_WIDTH = N_SLICES * GROUP
N_CHIPS = 4
COLS = IN_WIDTH // N_CHIPS
KA = 31
KB = 3
HALO_A, HALO_B, HALO_C = 32, 8, 16
POOL_WINDOWS = (2, 4, 8, 16)
SGU_BLOCK = 128
CHUNK = 64
LN_EPS = 1e-5
ROWS = 64
V7X_VMEM_BYTES = 64 * 1024 * 1024
VMEM_LIMIT = 56 * 1024 * 1024

ADAM_LR, ADAM_B1, ADAM_B2, ADAM_EPS, ADAM_WD, ADAM_STEP = 0.001, 0.9, 0.999, 1e-08, 0.01, 10


def _vmem_params(**kw):
    return pltpu.CompilerParams(vmem_limit_bytes=VMEM_LIMIT, **kw)


def _sig(v):
    return 0.5 * jnp.tanh(0.5 * v) + 0.5


def _dot(a, b):
    return jnp.dot(a, b, preferred_element_type=F32)


def _dot_nt(a, b):
    return lax.dot_general(a, b, (((1,), (1,)), ((), ())), preferred_element_type=F32)


def _dot_tn(a, b):
    return lax.dot_general(a, b, (((0,), (0,)), ((), ())), preferred_element_type=F32)


def _segdot(v, m):
    hi = v.astype(BF16)
    lo = (v - hi.astype(F32)).astype(BF16)
    return _dot(hi, m) + _dot(lo, m)


def _colsum(v):
    return jnp.sum(v, axis=0, keepdims=True)


def _rowmean(v):
    return jnp.mean(v, axis=-1, keepdims=True)


def _lane_group(n):
    return lax.broadcasted_iota(jnp.int32, (1, n), 1) // HEAD


def _pool_cnt(tile, t_rows):
    pos = tile * t_rows + lax.broadcasted_iota(jnp.int32, (t_rows, GROUP), 0) + 1
    grp = lax.broadcasted_iota(jnp.int32, (t_rows, GROUP), 1) // HEAD
    win = jnp.where(grp == 0, 2, jnp.where(grp == 1, 4, jnp.where(grp == 2, 8, 16)))
    return jnp.minimum(pos, win).astype(F32)


def _sgu_masks(wm_ref, wmt_ref, wm_s, wmt_s):
    r = lax.broadcasted_iota(jnp.int32, (SGU_BLOCK, 4 * SGU_BLOCK), 0) // CHUNK
    c = (lax.broadcasted_iota(jnp.int32, (SGU_BLOCK, 4 * SGU_BLOCK), 1) % SGU_BLOCK) // CHUNK
    wm_s[...] = jnp.where(c <= r, wm_ref[...], 0.0).astype(BF16)
    if wmt_ref is not None:
        rt = (lax.broadcasted_iota(jnp.int32, (4 * SGU_BLOCK, SGU_BLOCK), 0) % SGU_BLOCK) // CHUNK
        ct = lax.broadcasted_iota(jnp.int32, (4 * SGU_BLOCK, SGU_BLOCK), 1) // CHUNK
        wmt_s[...] = jnp.where(rt <= ct, wmt_ref[...], 0.0).astype(BF16)


def _vstack(v_blk):
    grp = _lane_group(GROUP)
    return jnp.concatenate([jnp.where(grp == h, v_blk, 0.0) for h in range(4)], axis=0).astype(BF16)


def _fwd_layer(x, wi, bin_, caw, cbw, s256, seg, pw, wm, sb, wo, v1024, *, tile):
    S = x.shape[0]
    T = tile
    nt = S // T
    alpha = float((2.0 * 4) ** 0.25)

    def body(x_ref, wi_ref, bin_ref, caw_ref, cbw_ref, s256_ref, seg_ref, pw_ref, wm_ref, sb_ref, wo_ref, v1024_ref,
             y_ref, xb_ref, h_ref, aux_ref, mix_ref, z_ref, abuf, bbuf, cbuf, wm_s):
        i = pl.program_id(0)

        @pl.when(i == 0)
        def _():
            abuf[0:HALO_A, :] = jnp.zeros((HALO_A, GROUP), F32)
            bbuf[0:HALO_B, :] = jnp.zeros((HALO_B, GROUP), F32)
            cbuf[0:HALO_C, :] = jnp.zeros((HALO_C, GROUP), F32)
            _sgu_masks(wm_ref, None, wm_s, None)

        x = x_ref[...]
        xb = x.astype(BF16)
        xb_ref[...] = xb
        for k in range(N_CHIPS):
            h_ref[:, COLS * k:COLS * (k + 1)] = _dot(xb, wi_ref[k]) + bin_ref[:, COLS * k:COLS * (k + 1)]

        def hs(j):
            return h_ref[:, GROUP * j:GROUP * (j + 1)]

        abuf[HALO_A:HALO_A + T, :] = hs(0) * _sig(hs(1))
        for r0 in range(0, T, ROWS):
            acc = None
            for k in range(KA):
                off = HALO_A - (KA - 1) + k + r0
                term = caw_ref[k:k + 1, :] * abuf[off:off + ROWS, :]
                acc = term if acc is None else acc + term
            aux_ref[r0:r0 + ROWS, 0:GROUP] = acc + s256_ref[0:1, :]
        abuf[0:HALO_A, :] = abuf[T:T + HALO_A, :]
        a1 = aux_ref[:, 0:GROUP]
        segm = seg_ref[...]
        cen = a1 - _segdot(a1, segm)
        var = _segdot(cen * cen, segm)
        a2 = cen * lax.rsqrt(var + LN_EPS) * s256_ref[1:2, :] + s256_ref[2:3, :]
        az = hs(2)
        mix_ref[:, 0:GROUP] = (a2 * _sig(a2) * (az * _sig(az))).astype(BF16)

        bbuf[HALO_B:HALO_B + T, :] = hs(4) * hs(5)
        for r0 in range(0, T, ROWS):
            acc = None
            for k in range(KB):
                off = HALO_B - (KB - 1) + k + r0
                term = cbw_ref[k:k + 1, :] * bbuf[off:off + ROWS, :]
                acc = term if acc is None else acc + term
            aux_ref[r0:r0 + ROWS, GROUP:2 * GROUP] = acc
        bbuf[0:HALO_B, :] = bbuf[T:T + HALO_B, :]
        bz = hs(6)
        mix_ref[:, GROUP:2 * GROUP] = (hs(3) * aux_ref[:, GROUP:2 * GROUP] * (bz * _sig(bz))).astype(BF16)

        ch = hs(7)
        cbuf[HALO_C:HALO_C + T, :] = ch
        hi_lane = (lax.broadcasted_iota(jnp.int32, (1, 128), 1) // HEAD) == 1
        for r0 in range(0, T, ROWS):
            def win(col, j0, j1):
                s = None
                for j in range(j0, j1):
                    off = HALO_C - j + r0
                    term = cbuf[off:off + ROWS, 128 * col:128 * (col + 1)]
                    s = term if s is None else s + term
                return s
            w0 = win(0, 0, 2) + jnp.where(hi_lane, win(0, 2, 4), 0.0)
            w1 = win(1, 0, 8) + jnp.where(hi_lane, win(1, 8, 16), 0.0)
            aux_ref[r0:r0 + ROWS, 2 * GROUP:2 * GROUP + 128] = w0
            aux_ref[r0:r0 + ROWS, 2 * GROUP + 128:3 * GROUP] = w1
        cbuf[0:HALO_C, :] = cbuf[T:T + HALO_C, :]
        pooled = aux_ref[:, 2 * GROUP:3 * GROUP] / _pool_cnt(i, T) - ch
        aux_ref[:, 2 * GROUP:3 * GROUP] = pooled
        q = _dot(pooled.astype(BF16), pw_ref[...])
        cz = hs(8)
        mix_ref[:, 2 * GROUP:3 * GROUP] = (q * s256_ref[3:4, :] * (cz * _sig(cz))).astype(BF16)

        dv = hs(10)
        cen = dv - _rowmean(dv)
        var = _rowmean(cen * cen)
        v = cen * lax.rsqrt(var + LN_EPS) * s256_ref[4:5, :] + s256_ref[5:6, :]
        sps = []
        for n in range(T // SGU_BLOCK):
            vb = v[n * SGU_BLOCK:(n + 1) * SGU_BLOCK, :]
            sps.append(_dot(wm_s[...], _vstack(vb)) + sb_ref[...])
        sp = jnp.concatenate(sps, axis=0)
        dz = hs(11)
        mix_ref[:, 3 * GROUP:4 * GROUP] = (hs(9) * sp * (dz * _sig(dz))).astype(BF16)

        out = v1024_ref[0:1, :]
        for k in range(N_CHIPS):
            out = out + _dot(mix_ref[:, GROUP * k:GROUP * (k + 1)], wo_ref[k])
        z = alpha * x + out
        z_ref[...] = z
        cen = z - _rowmean(z)
        var = _rowmean(cen * cen)
        y_ref[...] = cen * lax.rsqrt(var + LN_EPS) * v1024_ref[1:2, :] + v1024_ref[2:3, :]

    def full(a):
        nd = a.ndim
        return pl.BlockSpec(a.shape, lambda i, _n=nd: (0,) * _n)

    def rows(width):
        return pl.BlockSpec((T, width), lambda i: (i, 0))

    consts = (wi, bin_, caw, cbw, s256, seg, pw, wm, sb, wo, v1024)
    return pl.pallas_call(
        body, name="fwd_layer",
        grid=(nt,),
        in_specs=[rows(D_MODEL)] + [full(a) for a in consts],
        out_specs=[rows(D_MODEL), rows(D_MODEL), rows(IN_WIDTH), rows(3 * GROUP), rows(D_MODEL), rows(D_MODEL)],
        out_shape=[jax.ShapeDtypeStruct((S, D_MODEL), F32), jax.ShapeDtypeStruct((S, D_MODEL), BF16),
                   jax.ShapeDtypeStruct((S, IN_WIDTH), F32), jax.ShapeDtypeStruct((S, 3 * GROUP), F32),
                   jax.ShapeDtypeStruct((S, D_MODEL), BF16), jax.ShapeDtypeStruct((S, D_MODEL), F32)],
        scratch_shapes=[pltpu.VMEM((T + HALO_A, GROUP), F32), pltpu.VMEM((T + HALO_B, GROUP), F32),
                        pltpu.VMEM((T + HALO_C, GROUP), F32), pltpu.VMEM((SGU_BLOCK, 4 * SGU_BLOCK), BF16)],
        compiler_params=_vmem_params(dimension_semantics=("arbitrary",)),
    )(x, *consts)


def _loss_head(y, target, *, tile):
    S = y.shape[0]
    T = tile

    def body(y_ref, t_ref, dy_ref, acc_ref):
        @pl.when(pl.program_id(0) == 0)
        def _():
            acc_ref[...] = jnp.zeros_like(acc_ref)
        err = y_ref[...] - t_ref[...]
        dy_ref[...] = err * (1.0 / D_MODEL)
        acc_ref[...] += jnp.sum(_colsum(err * err), axis=1, keepdims=True) * (0.5 / D_MODEL)

    return pl.pallas_call(
        body, name="loss_head",
        grid=(S // T,),
        in_specs=[pl.BlockSpec((T, D_MODEL), lambda i: (i, 0))] * 2,
        out_specs=[pl.BlockSpec((T, D_MODEL), lambda i: (i, 0)), pl.BlockSpec((8, 128), lambda i: (0, 0))],
        out_shape=[jax.ShapeDtypeStruct((S, D_MODEL), F32), jax.ShapeDtypeStruct((8, 128), F32)],
        compiler_params=_vmem_params(dimension_semantics=("arbitrary",)),
    )(y, target)


S256_ROWS = 48
ROW_CBW = 8
ROW_CAW = 16


def _bwd_layer(dy, z, h, aux, wi, caw, cbw, s256, seg, pw, wm, wmt, sb, wo, v1024, e4, *, tile):
    S = dy.shape[0]
    T = tile
    nt = S // T
    alpha = float((2.0 * 4) ** 0.25)

    def body(dy_ref, z_ref, h_ref, aux_ref, wi_ref, caw_ref, cbw_ref, s256_ref, seg_ref, pw_ref, wm_ref, wmt_ref,
             sb_ref, wo_ref, v1024_ref, e4_ref,
             dx_ref, dhb_ref, dzb_ref, o1024_ref, obin_ref, o256_ref, opw_ref, owc_ref, osb_ref,
             dbuf, ebuf, fbuf, a0_s, u_s, wm_s, wmt_s, dsp_acc, pw_acc):
        i = pl.program_id(0)
        tile_idx = nt - 1 - i

        @pl.when(i == 0)
        def _():
            dbuf[T:T + HALO_A, :] = jnp.zeros((HALO_A, GROUP), F32)
            ebuf[T:T + HALO_B, :] = jnp.zeros((HALO_B, GROUP), F32)
            fbuf[T:T + HALO_C, :] = jnp.zeros((HALO_C, GROUP), F32)
            _sgu_masks(wm_ref, wmt_ref, wm_s, wmt_s)
            o1024_ref[...] = jnp.zeros_like(o1024_ref)
            obin_ref[...] = jnp.zeros_like(obin_ref)
            o256_ref[...] = jnp.zeros_like(o256_ref)
            owc_ref[...] = jnp.zeros_like(owc_ref)
            dsp_acc[...] = jnp.zeros_like(dsp_acc)
            pw_acc[...] = jnp.zeros_like(pw_acc)

        def hs(j):
            return h_ref[:, GROUP * j:GROUP * (j + 1)]

        def put_dh(j, val):
            obin_ref[0:1, GROUP * j:GROUP * (j + 1)] += _colsum(val)
            dhb_ref[:, GROUP * j:GROUP * (j + 1)] = val.astype(BF16)

        def dsilu(v, s):
            return s * (1.0 + v * (1.0 - s))

        dy = dy_ref[...]
        z = z_ref[...]
        cen = z - _rowmean(z)
        rstd = lax.rsqrt(_rowmean(cen * cen) + LN_EPS)
        xhat = cen * rstd
        o1024_ref[0:1, :] += _colsum(dy * xhat)
        o1024_ref[1:2, :] += _colsum(dy)
        gdy = dy * v1024_ref[1:2, :]
        dz = rstd * (gdy - _rowmean(gdy) - xhat * _rowmean(gdy * xhat))
        o1024_ref[2:3, :] += _colsum(dz)
        dzb = dz.astype(BF16)
        dzb_ref[...] = dzb

        def dmix(k):
            return _dot_nt(dzb, wo_ref[k])

        segm = seg_ref[...]

        a_val, a_glu, a_z = hs(0), hs(1), hs(2)
        sg = _sig(a_glu)
        a0_s[...] = a_val * sg
        a1 = aux_ref[:, 0:GROUP]
        cen = a1 - _segdot(a1, segm)
        rstd_a = lax.rsqrt(_segdot(cen * cen, segm) + LN_EPS)
        xh = cen * rstd_a
        a2 = xh * s256_ref[1:2, :] + s256_ref[2:3, :]
        s2 = _sig(a2)
        sz = _sig(a_z)
        dya = dmix(0)
        put_dh(2, dya * (a2 * s2) * dsilu(a_z, sz))
        d_a2 = dya * (a_z * sz) * dsilu(a2, s2)
        o256_ref[1:2, :] += _colsum(d_a2 * xh)
        o256_ref[2:3, :] += _colsum(d_a2)
        gd = d_a2 * s256_ref[1:2, :]
        d_a1 = rstd_a * (gd - _segdot(gd, segm) - xh * _segdot(gd * xh, segm))
        o256_ref[0:1, :] += _colsum(d_a1)
        dbuf[0:T, :] = d_a1
        for r0 in range(0, T, ROWS):
            a0c = a0_s[r0:r0 + ROWS, :]
            acc = None
            for k in range(KA):
                off = (KA - 1) - k + r0
                w = dbuf[off:off + ROWS, :]
                term = caw_ref[k:k + 1, :] * w
                acc = term if acc is None else acc + term
                o256_ref[ROW_CAW + k:ROW_CAW + k + 1, :] += _colsum(a0c * w)
            u_s[r0:r0 + ROWS, :] = acc
        dbuf[T:T + HALO_A, :] = dbuf[0:HALO_A, :]
        d_a0 = u_s[...]
        put_dh(0, d_a0 * sg)
        put_dh(1, d_a0 * a_val * sg * (1.0 - sg))

        b_b, b_c, b_h, b_z = hs(3), hs(4), hs(5), hs(6)
        cb = aux_ref[:, GROUP:2 * GROUP]
        sz = _sig(b_z)
        dyb = dmix(1)
        put_dh(3, dyb * cb * (b_z * sz))
        put_dh(6, dyb * b_b * cb * dsilu(b_z, sz))
        ebuf[0:T, :] = dyb * b_b * (b_z * sz)
        a0_s[...] = b_c * b_h
        for r0 in range(0, T, ROWS):
            uc = a0_s[r0:r0 + ROWS, :]
            acc = None
            for k in range(KB):
                off = (KB - 1) - k + r0
                w = ebuf[off:off + ROWS, :]
                term = cbw_ref[k:k + 1, :] * w
                acc = term if acc is None else acc + term
                o256_ref[ROW_CBW + k:ROW_CBW + k + 1, :] += _colsum(uc * w)
            u_s[r0:r0 + ROWS, :] = acc
        ebuf[T:T + HALO_B, :] = ebuf[0:HALO_B, :]
        d_u = u_s[...]
        put_dh(4, d_u * b_h)
        put_dh(5, d_u * b_c)

        c_z = hs(8)
        pooled = aux_ref[:, 2 * GROUP:3 * GROUP]
        pooled_b = pooled.astype(BF16)
        q = _dot(pooled_b, pw_ref[...])
        sz = _sig(c_z)
        dyc = dmix(2)
        ps = s256_ref[3:4, :]
        o256_ref[3:4, :] += _colsum(dyc * q * (c_z * sz))
        put_dh(8, dyc * q * ps * dsilu(c_z, sz))
        d_q = (dyc * ps * (c_z * sz)).astype(BF16)
        pw_acc[...] += _dot_tn(pooled_b, d_q)
        d_pooled = _dot_nt(d_q, pw_ref[...])
        fbuf[0:T, :] = d_pooled / _pool_cnt(tile_idx, T)
        hi_lane = (lax.broadcasted_iota(jnp.int32, (1, 128), 1) // HEAD) == 1
        for r0 in range(0, T, ROWS):
            def win(col, j0, j1):
                s = None
                for j in range(j0, j1):
                    term = fbuf[r0 + j:r0 + j + ROWS, 128 * col:128 * (col + 1)]
                    s = term if s is None else s + term
                return s
            u_s[r0:r0 + ROWS, 0:128] = win(0, 0, 2) + jnp.where(hi_lane, win(0, 2, 4), 0.0)
            u_s[r0:r0 + ROWS, 128:256] = win(1, 0, 8) + jnp.where(hi_lane, win(1, 8, 16), 0.0)
        fbuf[T:T + HALO_C, :] = fbuf[0:HALO_C, :]
        put_dh(7, u_s[...] - d_pooled)

        d_u_, d_v_, d_z_ = hs(9), hs(10), hs(11)
        cen = d_v_ - _rowmean(d_v_)
        rstd_v = lax.rsqrt(_rowmean(cen * cen) + LN_EPS)
        xv = cen * rstd_v
        v = xv * s256_ref[4:5, :] + s256_ref[5:6, :]
        sz = _sig(d_z_)
        dyd = dmix(3)
        d_sp = dyd * d_u_ * (d_z_ * sz)
        grp = _lane_group(GROUP)
        sps, dvs = [], []
        for n in range(T // SGU_BLOCK):
            blk = slice(n * SGU_BLOCK, (n + 1) * SGU_BLOCK)
            vst = _vstack(v[blk, :])
            sps.append(_dot(wm_s[...], vst) + sb_ref[...])
            dspb = d_sp[blk, :]
            dsp_acc[...] += dspb
            dspb16 = dspb.astype(BF16)
            dvst = _dot(wmt_s[...], dspb16)
            dvb = None
            for hh in range(4):
                part = jnp.where(grp == hh, dvst[hh * SGU_BLOCK:(hh + 1) * SGU_BLOCK, :], 0.0)
                dvb = part if dvb is None else dvb + part
            dvs.append(dvb)
            owc_ref[...] += _dot_nt(dspb16, vst)
        sp = jnp.concatenate(sps, axis=0)
        d_v = jnp.concatenate(dvs, axis=0)
        put_dh(9, dyd * sp * (d_z_ * sz))
        put_dh(11, dyd * d_u_ * sp * dsilu(d_z_, sz))
        o256_ref[4:5, :] += _colsum(d_v * xv)
        o256_ref[5:6, :] += _colsum(d_v)
        gd = d_v * s256_ref[4:5, :]
        put_dh(10, rstd_v * (gd - _rowmean(gd) - xv * _rowmean(gd * xv)))

        dx = alpha * dz
        for k in range(N_CHIPS):
            dx = dx + _dot_nt(dhb_ref[:, COLS * k:COLS * (k + 1)], wi_ref[k])
        dx_ref[...] = dx

        @pl.when(i == nt - 1)
        def _():
            r = lax.broadcasted_iota(jnp.int32, (SGU_BLOCK, 4 * SGU_BLOCK), 0) // CHUNK
            c = (lax.broadcasted_iota(jnp.int32, (SGU_BLOCK, 4 * SGU_BLOCK), 1) % SGU_BLOCK) // CHUNK
            owc_ref[...] = jnp.where(c <= r, owc_ref[...], 0.0)
            osb_ref[...] = _segdot(dsp_acc[...], e4_ref[...])
            for g in range(4):
                opw_ref[:, HEAD * g:HEAD * (g + 1)] = pw_acc[HEAD * g:HEAD * (g + 1), HEAD * g:HEAD * (g + 1)]

    def full(a):
        nd = a.ndim
        return pl.BlockSpec(a.shape, lambda i, _n=nd: (0,) * _n)

    def rows(width):
        return pl.BlockSpec((T, width), lambda i: (nt - 1 - i, 0))

    def acc(shape):
        return pl.BlockSpec(shape, lambda i: (0, 0))

    consts = (wi, caw, cbw, s256, seg, pw, wm, wmt, sb, wo, v1024, e4)
    acc_shapes = [(8, D_MODEL), (8, IN_WIDTH), (S256_ROWS, GROUP), (HEAD, GROUP), (SGU_BLOCK, 4 * SGU_BLOCK),
                  (SGU_BLOCK, 128)]
    return pl.pallas_call(
        body, name="bwd_layer",
        grid=(nt,),
        in_specs=[rows(D_MODEL), rows(D_MODEL), rows(IN_WIDTH), rows(3 * GROUP)] + [full(a) for a in consts],
        out_specs=[rows(D_MODEL), rows(IN_WIDTH), rows(D_MODEL)] + [acc(s) for s in acc_shapes],
        out_shape=[jax.ShapeDtypeStruct((S, D_MODEL), F32), jax.ShapeDtypeStruct((S, IN_WIDTH), BF16),
                   jax.ShapeDtypeStruct((S, D_MODEL), BF16)] + [jax.ShapeDtypeStruct(s, F32) for s in acc_shapes],
        scratch_shapes=[pltpu.VMEM((T + HALO_A, GROUP), F32), pltpu.VMEM((T + HALO_B, GROUP), F32),
                        pltpu.VMEM((T + HALO_C, GROUP), F32), pltpu.VMEM((T, GROUP), F32), pltpu.VMEM((T, GROUP), F32),
                        pltpu.VMEM((SGU_BLOCK, 4 * SGU_BLOCK), BF16), pltpu.VMEM((4 * SGU_BLOCK, SGU_BLOCK), BF16),
                        pltpu.VMEM((SGU_BLOCK, GROUP), F32), pltpu.VMEM((GROUP, GROUP), F32)],
        compiler_params=_vmem_params(dimension_semantics=("arbitrary",)),
    )(dy, z, h, aux, *consts)


def _dw_proj(layer, lhs, rhs, slab, *, lhs_cols, rhs_cols, by_lhs, tk, name):
    S = lhs.shape[0]

    def body(l_ref, a_ref, b_ref, slab_ref, o_ref):
        del l_ref, slab_ref

        @pl.when(pl.program_id(1) == 0)
        def _():
            o_ref[...] = jnp.zeros_like(o_ref)
        o_ref[...] += _dot_tn(a_ref[...], b_ref[...])

    if by_lhs:
        a_spec = pl.BlockSpec((tk, lhs_cols), lambda j, s, l: (s, j))
        b_spec = pl.BlockSpec((tk, rhs_cols), lambda j, s, l: (s, 0))
    else:
        a_spec = pl.BlockSpec((tk, lhs_cols), lambda j, s, l: (s, 0))
        b_spec = pl.BlockSpec((tk, rhs_cols), lambda j, s, l: (s, j))
    grid_spec = pltpu.PrefetchScalarGridSpec(
        num_scalar_prefetch=1, grid=(N_CHIPS, S // tk),
        in_specs=[a_spec, b_spec, pl.BlockSpec(memory_space=pl.ANY)],
        out_specs=pl.BlockSpec((None, None, lhs_cols, rhs_cols), lambda j, s, l: (l[0], j, 0, 0)))
    return pl.pallas_call(
        body, name=name, grid_spec=grid_spec,
        out_shape=jax.ShapeDtypeStruct(slab.shape, F32),
        input_output_aliases={3: 0},
        compiler_params=_vmem_params(dimension_semantics=("arbitrary", "arbitrary")),
    )(layer, lhs, rhs, slab)


def _adamw(w, g, m, v, *, rows_per_step, name):
    R, C = w.shape
    tr = rows_per_step
    c1 = 1.0 - ADAM_B1 ** ADAM_STEP
    c2 = 1.0 - ADAM_B2 ** ADAM_STEP

    def body(w_ref, g_ref, m_ref, v_ref, d_ref, nm_ref, nv_ref):
        g_ = g_ref[...]
        nm = ADAM_B1 * m_ref[...] + (1.0 - ADAM_B1) * g_
        nv = ADAM_B2 * v_ref[...] + (1.0 - ADAM_B2) * (g_ * g_)
        nm_ref[...] = nm
        nv_ref[...] = nv
        d_ref[...] = -ADAM_LR * ((nm / c1) / (jnp.sqrt(nv / c2) + ADAM_EPS) + ADAM_WD * w_ref[...])

    spec = pl.BlockSpec((tr, C), lambda i: (i, 0))
    return pl.pallas_call(
        body, name=name, grid=(R // tr,),
        in_specs=[spec] * 4, out_specs=[spec] * 3,
        out_shape=[jax.ShapeDtypeStruct((R, C), F32)] * 3,
        compiler_params=_vmem_params(dimension_semantics=("arbitrary",)),
    )(w, g, m, v)


def _place():
    return lax.axis_index("x"), lax.axis_index("y"), lax.axis_index("c")


def _other_chips(x, y):
    return [(1 - x, y, 2 * (1 - x) + y), (x, 1 - y, 2 * x + (1 - y)), (1 - x, 1 - y, 2 * (1 - x) + (1 - y))]


ANY = pl.BlockSpec(memory_space=pl.ANY)


def _gather_weights(wi16, wo16, cw):
    L = wi16.shape[0]
    hi_rows, ho_rows = D_MODEL // 2, GROUP // 2
    n_ici = 2 * L + 1
    n_fwd = 2 * L

    def body(wi_ref, wo_ref, cw_ref, *rest):
        wig = rest[0:L]
        wog = rest[L:2 * L]
        cwg = rest[2 * L]
        send_sems, recv_sems, loc_sems = rest[2 * L + 1:]
        x, y, c = _place()
        me_k = 2 * x + y
        sibling = (x, y, 1 - c)
        chips = _other_chips(x, y)

        def half_i(ref, blk):
            return ref.at[blk, pl.ds(c * hi_rows, hi_rows), :]

        def half_o(ref, blk):
            return ref.at[blk, pl.ds(c * ho_rows, ho_rows), :]

        def other_half_i(ref, blk):
            return ref.at[blk, pl.ds((1 - c) * hi_rows, hi_rows), :]

        def other_half_o(ref, blk):
            return ref.at[blk, pl.ds((1 - c) * ho_rows, ho_rows), :]

        local = []
        for l in range(L):
            local.append(pltpu.make_async_copy(wi_ref.at[l], wig[l].at[me_k], loc_sems.at[2 * l]))
            local.append(pltpu.make_async_copy(wo_ref.at[l], wog[l].at[me_k], loc_sems.at[2 * l + 1]))
        local.append(pltpu.make_async_copy(cw_ref, cwg.at[me_k], loc_sems.at[2 * L]))
        for cp in local:
            cp.start()

        def remote(src, dst, sem, to):
            return pltpu.make_async_remote_copy(src_ref=src, dst_ref=dst, send_sem=send_sems.at[sem],
                                                recv_sem=recv_sems.at[sem], device_id=to, device_id_type=MESH)

        sends = []
        for r, (px, py, _) in enumerate(chips):
            to = (px, py, c)
            for l in range(L):
                sends.append(remote(half_i(wi_ref, l), half_i(wig[l], me_k), r * n_ici + 2 * l, to))
                sends.append(remote(half_o(wo_ref, l), half_o(wog[l], me_k), r * n_ici + 2 * l + 1, to))
            sends.append(remote(cw_ref, cwg.at[me_k], r * n_ici + 2 * L, to))
        for cp in sends:
            cp.start()

        base = 3 * n_ici
        fwds = []
        for r, (px, py, pk) in enumerate(chips):
            for l in range(L):
                remote(half_i(wig[l], pk), half_i(wig[l], pk), r * n_ici + 2 * l, sibling).wait_recv()
                f = remote(half_i(wig[l], pk), half_i(wig[l], pk), base + r * n_fwd + 2 * l, sibling)
                f.start()
                fwds.append(f)
                remote(half_o(wog[l], pk), half_o(wog[l], pk), r * n_ici + 2 * l + 1, sibling).wait_recv()
                f = remote(half_o(wog[l], pk), half_o(wog[l], pk), base + r * n_fwd + 2 * l + 1, sibling)
                f.start()
                fwds.append(f)
            remote(cwg.at[pk], cwg.at[pk], r * n_ici + 2 * L, sibling).wait_recv()
        for r, (px, py, pk) in enumerate(chips):
            for l in range(L):
                remote(other_half_i(wig[l], pk), other_half_i(wig[l], pk), base + r * n_fwd + 2 * l, sibling).wait_recv()
                remote(other_half_o(wog[l], pk), other_half_o(wog[l], pk), base + r * n_fwd + 2 * l + 1, sibling).wait_recv()
        for cp in sends + fwds:
            cp.wait_send()
        for cp in local:
            cp.wait()

    n_sem = 3 * n_ici + 3 * n_fwd
    out_shape = ([jax.ShapeDtypeStruct((N_CHIPS, D_MODEL, COLS), BF16)] * L
                 + [jax.ShapeDtypeStruct((N_CHIPS, GROUP, D_MODEL), BF16)] * L
                 + [jax.ShapeDtypeStruct((N_CHIPS,) + cw.shape, F32)])
    outs = pl.pallas_call(
        body, name="gather_weights",
        in_specs=[ANY, ANY, ANY], out_specs=[ANY] * (2 * L + 1), out_shape=out_shape,
        scratch_shapes=[pltpu.SemaphoreType.DMA((n_sem,)), pltpu.SemaphoreType.DMA((n_sem,)),
                        pltpu.SemaphoreType.DMA((2 * L + 1,))],
        compiler_params=pltpu.CompilerParams(has_side_effects=True),
    )(wi16, wo16, cw)
    return outs[0:L], outs[L:2 * L], outs[2 * L]


def _swap_halves(gwi, gwo):
    L = gwi.shape[0]
    hi_rows, ho_rows = D_MODEL // 2, GROUP // 2

    def body(gwi_ref, gwo_ref, ri_ref, ro_ref, send_sems, recv_sems):
        x, y, c = _place()
        sibling = (x, y, 1 - c)
        cps = [
            pltpu.make_async_remote_copy(src_ref=gwi_ref.at[:, :, pl.ds((1 - c) * hi_rows, hi_rows), :], dst_ref=ri_ref,
                                         send_sem=send_sems.at[0], recv_sem=recv_sems.at[0], device_id=sibling,
                                         device_id_type=MESH),
            pltpu.make_async_remote_copy(src_ref=gwo_ref.at[:, :, pl.ds((1 - c) * ho_rows, ho_rows), :], dst_ref=ro_ref,
                                         send_sem=send_sems.at[1], recv_sem=recv_sems.at[1], device_id=sibling,
                                         device_id_type=MESH),
        ]
        for cp in cps:
            cp.start()
        for cp in cps:
            cp.wait()

    return pl.pallas_call(
        body, name="swap_halves",
        in_specs=[ANY, ANY], out_specs=[ANY, ANY],
        out_shape=[jax.ShapeDtypeStruct((L, N_CHIPS, hi_rows, COLS), F32),
                   jax.ShapeDtypeStruct((L, N_CHIPS, ho_rows, D_MODEL), F32)],
        scratch_shapes=[pltpu.SemaphoreType.DMA((2,)), pltpu.SemaphoreType.DMA((2,))],
        compiler_params=pltpu.CompilerParams(has_side_effects=True),
    )(gwi, gwo)


def _add_halves(c_arr, g, r, *, rows, cols, tr, name):
    L = g.shape[0]
    nb = rows // tr

    def body(c_ref, g_ref, r_ref, o_ref):
        del c_ref
        o_ref[...] = g_ref[...] + r_ref[...]

    grid_spec = pltpu.PrefetchScalarGridSpec(
        num_scalar_prefetch=1, grid=(L, N_CHIPS, nb),
        in_specs=[pl.BlockSpec((None, None, tr, cols), lambda l, k, i, c: (l, k, c[0] * nb + i, 0)),
                  pl.BlockSpec((None, None, tr, cols), lambda l, k, i, c: (l, k, i, 0))],
        out_specs=pl.BlockSpec((None, None, tr, cols), lambda l, k, i, c: (l, k, i, 0)))
    return pl.pallas_call(
        body, name=name, grid_spec=grid_spec,
        out_shape=jax.ShapeDtypeStruct((L, N_CHIPS, rows, cols), F32),
        compiler_params=_vmem_params(dimension_semantics=("arbitrary",) * 3),
    )(c_arr, g, r)


def _exchange_chunks(pi, po):
    L = pi.shape[0]
    hi_rows, ho_rows = pi.shape[2], po.shape[2]

    def body(pi_ref, po_ref, ri_ref, ro_ref, send_sems, recv_sems):
        x, y, c = _place()
        cps = []
        for r, (px, py, pk) in enumerate(_other_chips(x, y)):
            to = (px, py, c)
            cps.append(pltpu.make_async_remote_copy(src_ref=pi_ref.at[:, pk], dst_ref=ri_ref.at[r],
                                                    send_sem=send_sems.at[2 * r], recv_sem=recv_sems.at[2 * r],
                                                    device_id=to, device_id_type=MESH))
            cps.append(pltpu.make_async_remote_copy(src_ref=po_ref.at[:, pk], dst_ref=ro_ref.at[r],
                                                    send_sem=send_sems.at[2 * r + 1], recv_sem=recv_sems.at[2 * r + 1],
                                                    device_id=to, device_id_type=MESH))
        for cp in cps:
            cp.start()
        for cp in cps:
            cp.wait()

    return pl.pallas_call(
        body, name="exchange_chunks",
        in_specs=[ANY, ANY], out_specs=[ANY, ANY],
        out_shape=[jax.ShapeDtypeStruct((3, L, hi_rows, COLS), F32),
                   jax.ShapeDtypeStruct((3, L, ho_rows, D_MODEL), F32)],
        scratch_shapes=[pltpu.SemaphoreType.DMA((6,)), pltpu.SemaphoreType.DMA((6,))],
        compiler_params=pltpu.CompilerParams(has_side_effects=True),
    )(pi, po)


def _sum_chunks(kc_arr, p, r, *, rows, cols, tr, name):
    L = p.shape[0]
    nb = rows // tr

    def body(kc_ref, p_ref, r0_ref, r1_ref, r2_ref, o_ref):
        del kc_ref
        o_ref[...] = ((p_ref[...] + r0_ref[...]) + r1_ref[...]) + r2_ref[...]

    def rspec(j):
        return pl.BlockSpec((None, None, tr, cols), lambda l, i, kc, _j=j: (_j, l, i, 0))

    grid_spec = pltpu.PrefetchScalarGridSpec(
        num_scalar_prefetch=1, grid=(L, nb),
        in_specs=[pl.BlockSpec((None, None, tr, cols), lambda l, i, kc: (l, kc[0], i, 0)), rspec(0), rspec(1), rspec(2)],
        out_specs=pl.BlockSpec((None, tr, cols), lambda l, i, kc: (l, kc[1] * nb + i, 0)))
    return pl.pallas_call(
        body, name=name, grid_spec=grid_spec,
        out_shape=jax.ShapeDtypeStruct((L, 2 * rows, cols), F32),
        compiler_params=_vmem_params(dimension_semantics=("arbitrary",) * 2),
    )(kc_arr, p, r, r, r)


def _share_result(gi, go):
    hi_rows, ho_rows = gi.shape[1] // 2, go.shape[1] // 2

    def body(gi_ref, go_ref, oi_ref, oo_ref, send_sems, recv_sems):
        del gi_ref, go_ref
        x, y, c = _place()
        sibling = (x, y, 1 - c)
        cps = []
        for j, (ref, n) in enumerate(((oi_ref, hi_rows), (oo_ref, ho_rows))):
            mine = ref.at[:, pl.ds(c * n, n), :]
            cps.append(pltpu.make_async_remote_copy(src_ref=mine, dst_ref=mine, send_sem=send_sems.at[j],
                                                    recv_sem=recv_sems.at[j], device_id=sibling, device_id_type=MESH))
        for cp in cps:
            cp.start()
        for j, (ref, n) in enumerate(((oi_ref, hi_rows), (oo_ref, ho_rows))):
            theirs = ref.at[:, pl.ds((1 - c) * n, n), :]
            pltpu.make_async_remote_copy(src_ref=theirs, dst_ref=theirs, send_sem=send_sems.at[j],
                                         recv_sem=recv_sems.at[j], device_id=sibling, device_id_type=MESH).wait_recv()
        for cp in cps:
            cp.wait_send()

    return pl.pallas_call(
        body, name="share_result",
        in_specs=[ANY, ANY], out_specs=[ANY, ANY],
        out_shape=[jax.ShapeDtypeStruct(gi.shape, F32), jax.ShapeDtypeStruct(go.shape, F32)],
        input_output_aliases={0: 0, 1: 1},
        scratch_shapes=[pltpu.SemaphoreType.DMA((2,)), pltpu.SemaphoreType.DMA((2,))],
        compiler_params=pltpu.CompilerParams(has_side_effects=True),
    )(gi, go)


def _allreduce_small(g):
    def body(g_ref, o_ref, rbuf, send_sems, recv_sems):
        x, y, c = _place()
        o_ref[...] = g_ref[...]
        for step, peer in enumerate(((x, y, 1 - c), (1 - x, y, c), (x, 1 - y, c))):
            cp = pltpu.make_async_remote_copy(src_ref=o_ref, dst_ref=rbuf.at[step], send_sem=send_sems.at[step],
                                              recv_sem=recv_sems.at[step], device_id=peer, device_id_type=MESH)
            cp.start()
            cp.wait()
            o_ref[...] = o_ref[...] + rbuf[step]

    return pl.pallas_call(
        body, name="allreduce_small",
        in_specs=[pl.BlockSpec(memory_space=pltpu.VMEM)], out_specs=pl.BlockSpec(memory_space=pltpu.VMEM),
        out_shape=jax.ShapeDtypeStruct(g.shape, F32),
        scratch_shapes=[pltpu.VMEM((3,) + g.shape, F32), pltpu.SemaphoreType.DMA((3,)), pltpu.SemaphoreType.DMA((3,))],
        compiler_params=_vmem_params(has_side_effects=True),
    )(g)


SMALL = ("ln_g", "ln_b", "b_in", "conv_a_w", "conv_a_b", "norm_a_g", "norm_a_b", "conv_b_w", "pool_w", "pool_scale",
         "sgu_ln_g", "sgu_ln_b", "sgu_w", "sgu_bias", "b_out")
WEIGHTS = ("ln_g", "ln_b", "w_in", "b_in", "conv_a_w", "conv_a_b", "norm_a_g", "norm_a_b", "conv_b_w", "pool_w",
           "pool_scale", "sgu_ln_g", "sgu_ln_b", "sgu_w", "sgu_bias", "w_out", "b_out")


def _pad_rows(a, rows):
    return jnp.pad(a, ((0, rows - a.shape[0]), (0, 0)))


def _indicator_consts():
    seg = jnp.where((jnp.arange(GROUP)[:, None] // HEAD) == (jnp.arange(GROUP)[None, :] // HEAD),
                    1.0 / HEAD, 0.0).astype(BF16)
    e4 = ((jnp.arange(GROUP)[:, None] // HEAD) == jnp.arange(128)[None, :]).astype(BF16)
    return seg, e4


def _layer_consts(p, conv_full, l):
    same_head = jnp.eye(4, dtype=F32)[:, None, :, None] > 0
    caw = _pad_rows(conv_full[l, :KA], 32)
    cbw = _pad_rows(conv_full[l, KA:], 8)
    s256 = _pad_rows(jnp.stack([p["conv_a_b"][l], p["norm_a_g"][l], p["norm_a_b"][l], p["pool_scale"][l],
                                p["sgu_ln_g"][l], p["sgu_ln_b"][l]]), 8)
    pw = jnp.where(same_head, p["pool_w"][l][:, :, None, :], 0.0).reshape(GROUP, GROUP).astype(BF16)
    wm = jnp.transpose(p["sgu_w"][l], (1, 0, 2)).reshape(SGU_BLOCK, 4 * SGU_BLOCK)
    wmt = jnp.transpose(p["sgu_w"][l], (0, 2, 1)).reshape(4 * SGU_BLOCK, SGU_BLOCK)
    sb = jnp.repeat(p["sgu_bias"][l].T, HEAD, axis=1)
    v1024 = _pad_rows(jnp.stack([p["b_out"][l], p["ln_g"][l], p["ln_b"][l]]), 8)
    return dict(caw=caw, cbw=cbw, s256=s256, pw=pw, wm=wm, wmt=wmt, sb=sb, v1024=v1024, bin=p["b_in"][l][None, :])


def _step(p, m, v, x, target, *, tile_f, tile_b, tk):
    L = p["ln_g"].shape[0]
    S = x.shape[0]
    xi, yi, ci = _place()
    me_k = 2 * xi + yi

    cw = jnp.concatenate([p["conv_a_w"], p["conv_b_w"]], axis=1).reshape(-1, 128)
    cw_rows = cw.shape[0]
    cw = _pad_rows(cw, 72)
    wig, wog, cwg = _gather_weights(p["w_in"].astype(BF16), p["w_out"].astype(BF16), cw)
    cwg = cwg[:, :cw_rows].reshape(N_CHIPS, L, KA + KB, HEAD)
    conv_full = jnp.transpose(cwg, (1, 2, 0, 3)).reshape(L, KA + KB, GROUP)

    seg, e4 = _indicator_consts()
    consts = [_layer_consts(p, conv_full, l) for l in range(L)]

    hcur = x
    saved = []
    for l in range(L):
        k = consts[l]
        y, xb, h, aux, mixb, z = _fwd_layer(hcur, wig[l], k["bin"], k["caw"], k["cbw"], k["s256"], seg, k["pw"], k["wm"],
                                            k["sb"], wog[l], k["v1024"], tile=tile_f)
        saved.append((xb, h, aux, mixb, z))
        hcur = y

    dy, loss_acc = _loss_head(hcur, target, tile=tile_f)
    loss = lax.psum(loss_acc[0, 0], ("x", "y", "c"))

    gwi = lax.empty((L, N_CHIPS, D_MODEL, COLS), F32)
    gwo = lax.empty((L, N_CHIPS, GROUP, D_MODEL), F32)
    small = {}
    for l in reversed(range(L)):
        k = consts[l]
        xb, h, aux, mixb, z = saved[l]
        dy, dhb, dzb, o1024, obin, o256, opw, owc, osb = _bwd_layer(
            dy, z, h, aux, wig[l], k["caw"], k["cbw"], k["s256"], seg, k["pw"], k["wm"], k["wmt"], k["sb"], wog[l],
            k["v1024"], e4, tile=tile_b)
        larr = jnp.full((1,), l, jnp.int32)
        gwi = _dw_proj(larr, xb, dhb, gwi, lhs_cols=D_MODEL, rhs_cols=COLS, by_lhs=False, tk=tk, name="dw_in")
        gwo = _dw_proj(larr, mixb, dzb, gwo, lhs_cols=GROUP, rhs_cols=D_MODEL, by_lhs=True, tk=tk, name="dw_out")
        small[l] = dict(
            ln_g=o1024[0], ln_b=o1024[1], b_out=o1024[2], b_in=obin[0],
            conv_a_b=o256[0], norm_a_g=o256[1], norm_a_b=o256[2], pool_scale=o256[3], sgu_ln_g=o256[4], sgu_ln_b=o256[5],
            conv_b_w=o256[ROW_CBW:ROW_CBW + KB], conv_a_w=o256[ROW_CAW:ROW_CAW + KA],
            pool_w=jnp.transpose(opw.reshape(HEAD, 4, HEAD), (1, 0, 2)),
            sgu_w=jnp.transpose(owc.reshape(SGU_BLOCK, 4, SGU_BLOCK), (1, 0, 2)),
            sgu_bias=osb[:, 0:4].T)
    grad_x = dy

    parts, shapes = [], []
    for n in SMALL:
        a = jnp.stack([small[l][n] for l in range(L)])
        shapes.append((n, a.shape))
        parts.append(a.reshape(-1, 128))
    packed = _allreduce_small(jnp.concatenate(parts, axis=0))
    grads = {}
    r0 = 0
    for (n, shp), part in zip(shapes, parts):
        grads[n] = packed[r0:r0 + part.shape[0]].reshape(shp)
        r0 += part.shape[0]
    for n in ("conv_a_w", "conv_b_w"):
        grads[n] = lax.dynamic_slice_in_dim(grads[n], me_k * HEAD, HEAD, axis=2)

    c_arr = jnp.reshape(ci, (1,)).astype(jnp.int32)
    kc_arr = jnp.stack([me_k, ci]).astype(jnp.int32)
    ri, ro = _swap_halves(gwi, gwo)
    p_i = _add_halves(c_arr, gwi, ri, rows=D_MODEL // 2, cols=COLS, tr=256, name="add_halves_in")
    p_o = _add_halves(c_arr, gwo, ro, rows=GROUP // 2, cols=D_MODEL, tr=128, name="add_halves_out")
    qi, qo = _exchange_chunks(p_i, p_o)
    g_i = _sum_chunks(kc_arr, p_i, qi, rows=D_MODEL // 2, cols=COLS, tr=256, name="sum_chunks_in")
    g_o = _sum_chunks(kc_arr, p_o, qo, rows=GROUP // 2, cols=D_MODEL, tr=128, name="sum_chunks_out")
    g_i, g_o = _share_result(g_i, g_o)
    grads["w_in"] = g_i
    grads["w_out"] = g_o

    delta, new_m, new_v = {}, {}, {}
    for n in WEIGHTS:
        shp = p[n].shape
        if n in ("w_in", "w_out"):
            two_d = (shp[0] * shp[1], shp[2])
            tr = 512 if n == "w_in" else 256
        else:
            two_d = (-1, shp[-1])
            tr = None
        args = [a.reshape(two_d) for a in (p[n], grads[n], m[n], v[n])]
        d, nm, nv = _adamw(*args, rows_per_step=tr or args[0].shape[0], name="adamw_" + n)
        delta[n], new_m[n], new_v[n] = d.reshape(shp), nm.reshape(shp), nv.reshape(shp)

    return (loss, grad_x[None], *[grads[n] for n in WEIGHTS], *[delta[n] for n in WEIGHTS],
            *[new_m[n] for n in WEIGHTS], *[new_v[n] for n in WEIGHTS])


def kernel(x, ln_g, ln_b, w_in, b_in, conv_a_w, conv_a_b, norm_a_g, norm_a_b, conv_b_w, pool_w, pool_scale, sgu_ln_g, sgu_ln_b, sgu_w, sgu_bias, w_out, b_out, loss_target, m_ln_g, m_ln_b, m_w_in, m_b_in, m_conv_a_w, m_conv_a_b, m_norm_a_g, m_norm_a_b, m_conv_b_w, m_pool_w, m_pool_scale, m_sgu_ln_g, m_sgu_ln_b, m_sgu_w, m_sgu_bias, m_w_out, m_b_out, v_ln_g, v_ln_b, v_w_in, v_b_in, v_conv_a_w, v_conv_a_b, v_norm_a_g, v_norm_a_b, v_conv_b_w, v_pool_w, v_pool_scale, v_sgu_ln_g, v_sgu_ln_b, v_sgu_w, v_sgu_bias, v_w_out, v_b_out):
    p = dict(ln_g=ln_g, ln_b=ln_b, w_in=w_in, b_in=b_in, conv_a_w=conv_a_w, conv_a_b=conv_a_b, norm_a_g=norm_a_g,
             norm_a_b=norm_a_b, conv_b_w=conv_b_w, pool_w=pool_w, pool_scale=pool_scale, sgu_ln_g=sgu_ln_g,
             sgu_ln_b=sgu_ln_b, sgu_w=sgu_w, sgu_bias=sgu_bias, w_out=w_out, b_out=b_out)
    m = dict(ln_g=m_ln_g, ln_b=m_ln_b, w_in=m_w_in, b_in=m_b_in, conv_a_w=m_conv_a_w, conv_a_b=m_conv_a_b,
             norm_a_g=m_norm_a_g, norm_a_b=m_norm_a_b, conv_b_w=m_conv_b_w, pool_w=m_pool_w, pool_scale=m_pool_scale,
             sgu_ln_g=m_sgu_ln_g, sgu_ln_b=m_sgu_ln_b, sgu_w=m_sgu_w, sgu_bias=m_sgu_bias, w_out=m_w_out, b_out=m_b_out)
    v = dict(ln_g=v_ln_g, ln_b=v_ln_b, w_in=v_w_in, b_in=v_b_in, conv_a_w=v_conv_a_w, conv_a_b=v_conv_a_b,
             norm_a_g=v_norm_a_g, norm_a_b=v_norm_a_b, conv_b_w=v_conv_b_w, pool_w=v_pool_w, pool_scale=v_pool_scale,
             sgu_ln_g=v_sgu_ln_g, sgu_ln_b=v_sgu_ln_b, sgu_w=v_sgu_w, sgu_bias=v_sgu_bias, w_out=v_w_out, b_out=v_b_out)
    return _step(p, m, v, x[0], loss_target[0], tile_f=256, tile_b=256, tk=2048)
```

```python
import functools

import jax
import jax.numpy as jnp
from jax import lax
from jax.experimental import pallas as pl
from jax.experimental.pallas import tpu as pltpu

F32 = jnp.float32
BF16 = jnp.bfloat16
MESH = pl.DeviceIdType.MESH

D_MODEL = 1024
GROUP = 256
HEAD = 64
N_SLICES = 12
IN_WIDTH = N_SLICES * GROUP
N_CHIPS = 4
COLS = IN_WIDTH // N_CHIPS
KA = 31
KB = 3
HALO_A, HALO_B, HALO_C = 32, 8, 16
POOL_WINDOWS = (2, 4, 8, 16)
SGU_BLOCK = 128
CHUNK = 64
LN_EPS = 1e-5
ROWS = 64
V7X_VMEM_BYTES = 64 * 1024 * 1024
VMEM_LIMIT = 56 * 1024 * 1024

ADAM_LR, ADAM_B1, ADAM_B2, ADAM_EPS, ADAM_WD, ADAM_STEP = 0.001, 0.9, 0.999, 1e-08, 0.01, 10


ANY = pl.BlockSpec(memory_space=pl.ANY)


def _vmem_params(**kw):
    return pltpu.CompilerParams(vmem_limit_bytes=VMEM_LIMIT, **kw)


def _place():
    return lax.axis_index("x"), lax.axis_index("y"), lax.axis_index("c")


def _other_chips(x, y):
    return [(1 - x, y, 2 * (1 - x) + y), (x, 1 - y, 2 * x + (1 - y)), (1 - x, 1 - y, 2 * (1 - x) + (1 - y))]


def _sig(v):
    return 0.5 * jnp.tanh(0.5 * v) + 0.5


def _dot(a, b):
    return jnp.dot(a, b, preferred_element_type=F32)


def _dot_nt(a, b):
    return lax.dot_general(a, b, (((1,), (1,)), ((), ())), preferred_element_type=F32)


def _dot_tn(a, b):
    return lax.dot_general(a, b, (((0,), (0,)), ((), ())), preferred_element_type=F32)


def _segdot(v, m):
    hi = v.astype(BF16)
    lo = (v - hi.astype(F32)).astype(BF16)
    return _dot(hi, m) + _dot(lo, m)


def _colsum(v):
    return jnp.sum(v, axis=0, keepdims=True)


def _rowmean(v):
    return jnp.mean(v, axis=-1, keepdims=True)


def _lane_group(n):
    return lax.broadcasted_iota(jnp.int32, (1, n), 1) // HEAD


def _pool_cnt(tile, t_rows):
    pos = tile * t_rows + lax.broadcasted_iota(jnp.int32, (t_rows, GROUP), 0) + 1
    grp = lax.broadcasted_iota(jnp.int32, (t_rows, GROUP), 1) // HEAD
    win = jnp.where(grp == 0, 2, jnp.where(grp == 1, 4, jnp.where(grp == 2, 8, 16)))
    return jnp.minimum(pos, win).astype(F32)


def _sgu_masks(wm_ref, wmt_ref, wm_s, wmt_s):
    r = lax.broadcasted_iota(jnp.int32, (SGU_BLOCK, 4 * SGU_BLOCK), 0) // CHUNK
    c = (lax.broadcasted_iota(jnp.int32, (SGU_BLOCK, 4 * SGU_BLOCK), 1) % SGU_BLOCK) // CHUNK
    wm_s[...] = jnp.where(c <= r, wm_ref[...], 0.0).astype(BF16)
    if wmt_ref is not None:
        rt = (lax.broadcasted_iota(jnp.int32, (4 * SGU_BLOCK, SGU_BLOCK), 0) % SGU_BLOCK) // CHUNK
        ct = lax.broadcasted_iota(jnp.int32, (4 * SGU_BLOCK, SGU_BLOCK), 1) // CHUNK
        wmt_s[...] = jnp.where(rt <= ct, wmt_ref[...], 0.0).astype(BF16)


def _vstack(v_blk):
    grp = _lane_group(GROUP)
    return jnp.concatenate([jnp.where(grp == h, v_blk, 0.0) for h in range(4)], axis=0).astype(BF16)


def _gather_next(step, nt, nwi, nwo, gwi, gwo, send_sems, recv_sems, loc_sems):
    x, y, c = _place()
    me_k = 2 * x + y
    sibling = (x, y, 1 - c)
    chips = _other_chips(x, y)
    hi, ho = D_MODEL // 2, GROUP // 2

    def rc(src, dst, sem, to):
        return pltpu.make_async_remote_copy(src_ref=src, dst_ref=dst, send_sem=send_sems.at[sem],
                                            recv_sem=recv_sems.at[sem], device_id=to, device_id_type=MESH)

    def blk(ref, k, n, cc):
        return ref.at[k, pl.ds(cc * n, n), :]

    def ici(r):
        px, py, _ = chips[r]
        to = (px, py, c)
        return [rc(nwi.at[pl.ds(c * hi, hi), :], blk(gwi, me_k, hi, c), 2 * r, to),
                rc(nwo.at[pl.ds(c * ho, ho), :], blk(gwo, me_k, ho, c), 2 * r + 1, to)]

    def landed(r, cc, base):
        pk = chips[r][2]
        return [rc(blk(gwi, pk, hi, cc), blk(gwi, pk, hi, cc), base + 2 * r, sibling),
                rc(blk(gwo, pk, ho, cc), blk(gwo, pk, ho, cc), base + 2 * r + 1, sibling)]

    def local():
        return [pltpu.make_async_copy(nwi, gwi.at[me_k], loc_sems.at[0]),
                pltpu.make_async_copy(nwo, gwo.at[me_k], loc_sems.at[1])]

    @pl.when(step == 0)
    def _():
        for cp in local():
            cp.start()
        for r in range(3):
            for cp in ici(r):
                cp.start()

    @pl.when(step == nt // 2)
    def _():
        for r in range(3):
            for got, fwd in zip(landed(r, c, 0), landed(r, c, 6)):
                got.wait_recv()
                fwd.start()

    @pl.when(step == nt - 1)
    def _():
        for r in range(3):
            for got in landed(r, 1 - c, 6):
                got.wait_recv()
        for r in range(3):
            for cp in ici(r) + landed(r, c, 6):
                cp.wait_send()
        for cp in local():
            cp.wait()


def _fwd_layer(x, wi, bin_, caw, cbw, s256, seg, pw, wm, sb, wo, v1024, *, tile, nxt=None):
    S = x.shape[0]
    T = tile
    nt = S // T
    alpha = float((2.0 * 4) ** 0.25)
    n_in = 12 + (2 if nxt is not None else 0)
    n_out = 6 + (2 if nxt is not None else 0)

    def body(*refs):
        (x_ref, wi_ref, bin_ref, caw_ref, cbw_ref, s256_ref, seg_ref, pw_ref, wm_ref, sb_ref, wo_ref,
         v1024_ref) = refs[0:12]
        y_ref, xb_ref, h_ref, aux_ref, mix_ref, z_ref = refs[n_in:n_in + 6]
        abuf, bbuf, cbuf, wm_s = refs[n_in + n_out:n_in + n_out + 4]
        i = pl.program_id(0)
        if nxt is not None:
            _gather_next(i, nt, refs[12], refs[13], refs[n_in + 6], refs[n_in + 7], *refs[n_in + n_out + 4:])

        @pl.when(i == 0)
        def _():
            abuf[0:HALO_A, :] = jnp.zeros((HALO_A, GROUP), F32)
            bbuf[0:HALO_B, :] = jnp.zeros((HALO_B, GROUP), F32)
            cbuf[0:HALO_C, :] = jnp.zeros((HALO_C, GROUP), F32)
            _sgu_masks(wm_ref, None, wm_s, None)

        x = x_ref[...]
        xb = x.astype(BF16)
        xb_ref[...] = xb
        for k in range(N_CHIPS):
            h_ref[:, COLS * k:COLS * (k + 1)] = _dot(xb, wi_ref[k]) + bin_ref[:, COLS * k:COLS * (k + 1)]

        def hs(j):
            return h_ref[:, GROUP * j:GROUP * (j + 1)]

        abuf[HALO_A:HALO_A + T, :] = hs(0) * _sig(hs(1))
        for r0 in range(0, T, ROWS):
            acc = None
            for k in range(KA):
                off = HALO_A - (KA - 1) + k + r0
                term = caw_ref[k:k + 1, :] * abuf[off:off + ROWS, :]
                acc = term if acc is None else acc + term
            aux_ref[r0:r0 + ROWS, 0:GROUP] = acc + s256_ref[0:1, :]
        abuf[0:HALO_A, :] = abuf[T:T + HALO_A, :]
        a1 = aux_ref[:, 0:GROUP]
        segm = seg_ref[...]
        cen = a1 - _segdot(a1, segm)
        var = _segdot(cen * cen, segm)
        a2 = cen * lax.rsqrt(var + LN_EPS) * s256_ref[1:2, :] + s256_ref[2:3, :]
        az = hs(2)
        mix_ref[:, 0:GROUP] = (a2 * _sig(a2) * (az * _sig(az))).astype(BF16)

        bbuf[HALO_B:HALO_B + T, :] = hs(4) * hs(5)
        for r0 in range(0, T, ROWS):
            acc = None
            for k in range(KB):
                off = HALO_B - (KB - 1) + k + r0
                term = cbw_ref[k:k + 1, :] * bbuf[off:off + ROWS, :]
                acc = term if acc is None else acc + term
            aux_ref[r0:r0 + ROWS, GROUP:2 * GROUP] = acc
        bbuf[0:HALO_B, :] = bbuf[T:T + HALO_B, :]
        bz = hs(6)
        mix_ref[:, GROUP:2 * GROUP] = (hs(3) * aux_ref[:, GROUP:2 * GROUP] * (bz * _sig(bz))).astype(BF16)

        ch = hs(7)
        cbuf[HALO_C:HALO_C + T, :] = ch
        hi_lane = (lax.broadcasted_iota(jnp.int32, (1, 128), 1) // HEAD) == 1
        for r0 in range(0, T, ROWS):
            def win(col, j0, j1):
                s = None
                for j in range(j0, j1):
                    off = HALO_C - j + r0
                    term = cbuf[off:off + ROWS, 128 * col:128 * (col + 1)]
                    s = term if s is None else s + term
                return s
            w0 = win(0, 0, 2) + jnp.where(hi_lane, win(0, 2, 4), 0.0)
            w1 = win(1, 0, 8) + jnp.where(hi_lane, win(1, 8, 16), 0.0)
            aux_ref[r0:r0 + ROWS, 2 * GROUP:2 * GROUP + 128] = w0
            aux_ref[r0:r0 + ROWS, 2 * GROUP + 128:3 * GROUP] = w1
        cbuf[0:HALO_C, :] = cbuf[T:T + HALO_C, :]
        pooled = aux_ref[:, 2 * GROUP:3 * GROUP] / _pool_cnt(i, T) - ch
        aux_ref[:, 2 * GROUP:3 * GROUP] = pooled
        q = _dot(pooled.astype(BF16), pw_ref[...])
        cz = hs(8)
        mix_ref[:, 2 * GROUP:3 * GROUP] = (q * s256_ref[3:4, :] * (cz * _sig(cz))).astype(BF16)

        dv = hs(10)
        cen = dv - _rowmean(dv)
        var = _rowmean(cen * cen)
        v = cen * lax.rsqrt(var + LN_EPS) * s256_ref[4:5, :] + s256_ref[5:6, :]
        sps = []
        for n in range(T // SGU_BLOCK):
            vb = v[n * SGU_BLOCK:(n + 1) * SGU_BLOCK, :]
            sps.append(_dot(wm_s[...], _vstack(vb)) + sb_ref[...])
        sp = jnp.concatenate(sps, axis=0)
        dz = hs(11)
        mix_ref[:, 3 * GROUP:4 * GROUP] = (hs(9) * sp * (dz * _sig(dz))).astype(BF16)

        out = v1024_ref[0:1, :]
        for k in range(N_CHIPS):
            out = out + _dot(mix_ref[:, GROUP * k:GROUP * (k + 1)], wo_ref[k])
        z = alpha * x + out
        z_ref[...] = z
        cen = z - _rowmean(z)
        var = _rowmean(cen * cen)
        y_ref[...] = cen * lax.rsqrt(var + LN_EPS) * v1024_ref[1:2, :] + v1024_ref[2:3, :]

    def full(a):
        nd = a.ndim
        return pl.BlockSpec(a.shape, lambda i, _n=nd: (0,) * _n)

    def rows(width):
        return pl.BlockSpec((T, width), lambda i: (i, 0))

    consts = (wi, bin_, caw, cbw, s256, seg, pw, wm, sb, wo, v1024)
    in_specs = [rows(D_MODEL)] + [full(a) for a in consts]
    out_specs = [rows(D_MODEL), rows(D_MODEL), rows(IN_WIDTH), rows(3 * GROUP), rows(D_MODEL), rows(D_MODEL)]
    out_shape = [jax.ShapeDtypeStruct((S, D_MODEL), F32), jax.ShapeDtypeStruct((S, D_MODEL), BF16),
                 jax.ShapeDtypeStruct((S, IN_WIDTH), F32), jax.ShapeDtypeStruct((S, 3 * GROUP), F32),
                 jax.ShapeDtypeStruct((S, D_MODEL), BF16), jax.ShapeDtypeStruct((S, D_MODEL), F32)]
    scratch = [pltpu.VMEM((T + HALO_A, GROUP), F32), pltpu.VMEM((T + HALO_B, GROUP), F32),
               pltpu.VMEM((T + HALO_C, GROUP), F32), pltpu.VMEM((SGU_BLOCK, 4 * SGU_BLOCK), BF16)]
    extra = ()
    if nxt is not None:
        extra = tuple(nxt)
        in_specs += [ANY, ANY]
        out_specs += [ANY, ANY]
        out_shape += [jax.ShapeDtypeStruct((N_CHIPS, D_MODEL, COLS), BF16),
                      jax.ShapeDtypeStruct((N_CHIPS, GROUP, D_MODEL), BF16)]
        scratch += [pltpu.SemaphoreType.DMA((12,)), pltpu.SemaphoreType.DMA((12,)), pltpu.SemaphoreType.DMA((2,))]
    return pl.pallas_call(
        body, name="fwd_layer" if nxt is None else "fwd_layer_gather",
        grid=(nt,), in_specs=in_specs, out_specs=out_specs, out_shape=out_shape, scratch_shapes=scratch,
        compiler_params=_vmem_params(dimension_semantics=("arbitrary",), has_side_effects=nxt is not None),
    )(x, *consts, *extra)


def _loss_head(y, target, *, tile):
    S = y.shape[0]
    T = tile

    def body(y_ref, t_ref, dy_ref, acc_ref):
        @pl.when(pl.program_id(0) == 0)
        def _():
            acc_ref[...] = jnp.zeros_like(acc_ref)
        err = y_ref[...] - t_ref[...]
        dy_ref[...] = err * (1.0 / D_MODEL)
        acc_ref[...] += jnp.sum(_colsum(err * err), axis=1, keepdims=True) * (0.5 / D_MODEL)

    return pl.pallas_call(
        body, name="loss_head",
        grid=(S // T,),
        in_specs=[pl.BlockSpec((T, D_MODEL), lambda i: (i, 0))] * 2,
        out_specs=[pl.BlockSpec((T, D_MODEL), lambda i: (i, 0)), pl.BlockSpec((8, 128), lambda i: (0, 0))],
        out_shape=[jax.ShapeDtypeStruct((S, D_MODEL), F32), jax.ShapeDtypeStruct((8, 128), F32)],
        compiler_params=_vmem_params(dimension_semantics=("arbitrary",)),
    )(y, target)


ROW_CBW = 8
ROW_CAW = 16
ROW_PW = 48
ROW_LNG = 112
ROW_LNB = 116
ROW_BOUT = 120
ROW_BIN = 124
ROW_WC = 136
ROW_SB = 392
SM_ROWS = 520
N_DEV = 8


def _exchange_comm(start, finish, l, p_i, p_o, sm, r_i, r_o, r_sm, send_sems, recv_sems, loc_sem):
    x, y, c = _place()
    me = 4 * x + 2 * y + c
    chips = _other_chips(x, y)

    def rc(src, dst, sem, to):
        return pltpu.make_async_remote_copy(src_ref=src, dst_ref=dst, send_sem=send_sems.at[sem],
                                            recv_sem=recv_sems.at[sem], device_id=to, device_id_type=MESH)

    def big(r):
        px, py, pk = chips[r]
        to = (px, py, c)
        return [rc(p_i.at[l, pk], r_i.at[r, l], 2 * r, to), rc(p_o.at[l, pk], r_o.at[r, l], 2 * r + 1, to)]

    def peer(rel):
        px = 1 - x if rel & 4 else x
        py = 1 - y if rel & 2 else y
        pc = 1 - c if rel & 1 else c
        return (px, py, pc), 4 * px + 2 * py + pc

    def small_out(rel):
        to, _ = peer(rel)
        return rc(sm, r_sm.at[me], 5 + rel, to)

    def small_in(rel):
        to, idx = peer(rel)
        return rc(sm, r_sm.at[idx], 5 + rel, to)

    def local():
        return pltpu.make_async_copy(sm, r_sm.at[me], loc_sem.at[0])

    @pl.when(start)
    def _():
        local().start()
        for r in range(3):
            for cp in big(r):
                cp.start()
        for rel in range(1, N_DEV):
            small_out(rel).start()

    @pl.when(finish)
    def _():
        for r in range(3):
            for cp in big(r):
                cp.wait()
        for rel in range(1, N_DEV):
            small_in(rel).wait_recv()
            small_out(rel).wait_send()
        local().wait()


def _bwd_layer(dy, z, h, aux, wi, caw, cbw, s256, seg, pw, wm, wmt, sb, wo, v1024, e4, *, tile, exch=None):
    S = dy.shape[0]
    T = tile
    nt = S // T
    alpha = float((2.0 * 4) ** 0.25)
    n_in = 16 + (6 if exch is not None else 0)
    n_out = 4 + (3 if exch is not None else 0)

    def body(*refs):
        (dy_ref, z_ref, h_ref, aux_ref, wi_ref, caw_ref, cbw_ref, s256_ref, seg_ref, pw_ref, wm_ref, wmt_ref,
         sb_ref, wo_ref, v1024_ref, e4_ref) = refs[0:16]
        dx_ref, dhb_ref, dzb_ref, osm_ref = refs[n_in:n_in + 4]
        dbuf, ebuf, fbuf, a0_s, u_s, wm_s, wmt_s, dsp_acc, pw_acc = refs[n_in + n_out:n_in + n_out + 9]
        i = pl.program_id(0)
        tile_idx = nt - 1 - i
        if exch is not None:
            l_ref, p_i, p_o, sm = refs[16:20]
            r_i, r_o, r_sm = refs[n_in + 4:n_in + 7]
            _exchange_comm(i == 0, i == nt - 1, l_ref[0], p_i, p_o, sm, r_i, r_o, r_sm, *refs[n_in + n_out + 9:])

        @pl.when(i == 0)
        def _():
            dbuf[T:T + HALO_A, :] = jnp.zeros((HALO_A, GROUP), F32)
            ebuf[T:T + HALO_B, :] = jnp.zeros((HALO_B, GROUP), F32)
            fbuf[T:T + HALO_C, :] = jnp.zeros((HALO_C, GROUP), F32)
            _sgu_masks(wm_ref, wmt_ref, wm_s, wmt_s)
            osm_ref[...] = jnp.zeros_like(osm_ref)
            dsp_acc[...] = jnp.zeros_like(dsp_acc)
            pw_acc[...] = jnp.zeros_like(pw_acc)

        def hs(j):
            return h_ref[:, GROUP * j:GROUP * (j + 1)]

        def acc_row(row, val):
            osm_ref[row:row + 1, :] += _colsum(val)

        def acc_wide(row, val):
            cs = _colsum(val)
            for j in range(D_MODEL // GROUP):
                osm_ref[row + j:row + j + 1, :] += cs[:, GROUP * j:GROUP * (j + 1)]

        def put_dh(j, val):
            acc_row(ROW_BIN + j, val)
            dhb_ref[:, GROUP * j:GROUP * (j + 1)] = val.astype(BF16)

        def dsilu(v, s):
            return s * (1.0 + v * (1.0 - s))

        dy = dy_ref[...]
        z = z_ref[...]
        cen = z - _rowmean(z)
        rstd = lax.rsqrt(_rowmean(cen * cen) + LN_EPS)
        xhat = cen * rstd
        acc_wide(ROW_LNG, dy * xhat)
        acc_wide(ROW_LNB, dy)
        gdy = dy * v1024_ref[1:2, :]
        dz = rstd * (gdy - _rowmean(gdy) - xhat * _rowmean(gdy * xhat))
        acc_wide(ROW_BOUT, dz)
        dzb = dz.astype(BF16)
        dzb_ref[...] = dzb

        def dmix(k):
            return _dot_nt(dzb, wo_ref[k])

        segm = seg_ref[...]

        a_val, a_glu, a_z = hs(0), hs(1), hs(2)
        sg = _sig(a_glu)
        a0_s[...] = a_val * sg
        a1 = aux_ref[:, 0:GROUP]
        cen = a1 - _segdot(a1, segm)
        rstd_a = lax.rsqrt(_segdot(cen * cen, segm) + LN_EPS)
        xh = cen * rstd_a
        a2 = xh * s256_ref[1:2, :] + s256_ref[2:3, :]
        s2 = _sig(a2)
        sz = _sig(a_z)
        dya = dmix(0)
        put_dh(2, dya * (a2 * s2) * dsilu(a_z, sz))
        d_a2 = dya * (a_z * sz) * dsilu(a2, s2)
        acc_row(1, d_a2 * xh)
        acc_row(2, d_a2)
        gd = d_a2 * s256_ref[1:2, :]
        d_a1 = rstd_a * (gd - _segdot(gd, segm) - xh * _segdot(gd * xh, segm))
        acc_row(0, d_a1)
        dbuf[0:T, :] = d_a1
        for r0 in range(0, T, ROWS):
            a0c = a0_s[r0:r0 + ROWS, :]
            acc = None
            for k in range(KA):
                off = (KA - 1) - k + r0
                w = dbuf[off:off + ROWS, :]
                term = caw_ref[k:k + 1, :] * w
                acc = term if acc is None else acc + term
                acc_row(ROW_CAW + k, a0c * w)
            u_s[r0:r0 + ROWS, :] = acc
        dbuf[T:T + HALO_A, :] = dbuf[0:HALO_A, :]
        d_a0 = u_s[...]
        put_dh(0, d_a0 * sg)
        put_dh(1, d_a0 * a_val * sg * (1.0 - sg))

        b_b, b_c, b_h, b_z = hs(3), hs(4), hs(5), hs(6)
        cb = aux_ref[:, GROUP:2 * GROUP]
        sz = _sig(b_z)
        dyb = dmix(1)
        put_dh(3, dyb * cb * (b_z * sz))
        put_dh(6, dyb * b_b * cb * dsilu(b_z, sz))
        ebuf[0:T, :] = dyb * b_b * (b_z * sz)
        a0_s[...] = b_c * b_h
        for r0 in range(0, T, ROWS):
            uc = a0_s[r0:r0 + ROWS, :]
            acc = None
            for k in range(KB):
                off = (KB - 1) - k + r0
                w = ebuf[off:off + ROWS, :]
                term = cbw_ref[k:k + 1, :] * w
                acc = term if acc is None else acc + term
                acc_row(ROW_CBW + k, uc * w)
            u_s[r0:r0 + ROWS, :] = acc
        ebuf[T:T + HALO_B, :] = ebuf[0:HALO_B, :]
        d_u = u_s[...]
        put_dh(4, d_u * b_h)
        put_dh(5, d_u * b_c)

        c_z = hs(8)
        pooled = aux_ref[:, 2 * GROUP:3 * GROUP]
        pooled_b = pooled.astype(BF16)
        q = _dot(pooled_b, pw_ref[...])
        sz = _sig(c_z)
        dyc = dmix(2)
        ps = s256_ref[3:4, :]
        acc_row(3, dyc * q * (c_z * sz))
        put_dh(8, dyc * q * ps * dsilu(c_z, sz))
        d_q = (dyc * ps * (c_z * sz)).astype(BF16)
        pw_acc[...] += _dot_tn(pooled_b, d_q)
        d_pooled = _dot_nt(d_q, pw_ref[...])
        fbuf[0:T, :] = d_pooled / _pool_cnt(tile_idx, T)
        hi_lane = (lax.broadcasted_iota(jnp.int32, (1, 128), 1) // HEAD) == 1
        for r0 in range(0, T, ROWS):
            def win(col, j0, j1):
                s = None
                for j in range(j0, j1):
                    term = fbuf[r0 + j:r0 + j + ROWS, 128 * col:128 * (col + 1)]
                    s = term if s is None else s + term
                return s
            u_s[r0:r0 + ROWS, 0:128] = win(0, 0, 2) + jnp.where(hi_lane, win(0, 2, 4), 0.0)
            u_s[r0:r0 + ROWS, 128:256] = win(1, 0, 8) + jnp.where(hi_lane, win(1, 8, 16), 0.0)
        fbuf[T:T + HALO_C, :] = fbuf[0:HALO_C, :]
        put_dh(7, u_s[...] - d_pooled)

        d_u_, d_v_, d_z_ = hs(9), hs(10), hs(11)
        cen = d_v_ - _rowmean(d_v_)
        rstd_v = lax.rsqrt(_rowmean(cen * cen) + LN_EPS)
        xv = cen * rstd_v
        v = xv * s256_ref[4:5, :] + s256_ref[5:6, :]
        sz = _sig(d_z_)
        dyd = dmix(3)
        d_sp = dyd * d_u_ * (d_z_ * sz)
        grp = _lane_group(GROUP)
        sps, dvs = [], []
        for n in range(T // SGU_BLOCK):
            blk = slice(n * SGU_BLOCK, (n + 1) * SGU_BLOCK)
            vst = _vstack(v[blk, :])
            sps.append(_dot(wm_s[...], vst) + sb_ref[...])
            dspb = d_sp[blk, :]
            dsp_acc[...] += dspb
            dspb16 = dspb.astype(BF16)
            dvst = _dot(wmt_s[...], dspb16)
            dvb = None
            for hh in range(4):
                part = jnp.where(grp == hh, dvst[hh * SGU_BLOCK:(hh + 1) * SGU_BLOCK, :], 0.0)
                dvb = part if dvb is None else dvb + part
            dvs.append(dvb)
            dwc = _dot_nt(dspb16, vst)
            osm_ref[ROW_WC:ROW_WC + SGU_BLOCK, :] += dwc[:, 0:GROUP]
            osm_ref[ROW_WC + SGU_BLOCK:ROW_WC + 2 * SGU_BLOCK, :] += dwc[:, GROUP:2 * GROUP]
        sp = jnp.concatenate(sps, axis=0)
        d_v = jnp.concatenate(dvs, axis=0)
        put_dh(9, dyd * sp * (d_z_ * sz))
        put_dh(11, dyd * d_u_ * sp * dsilu(d_z_, sz))
        acc_row(4, d_v * xv)
        acc_row(5, d_v)
        gd = d_v * s256_ref[4:5, :]
        put_dh(10, rstd_v * (gd - _rowmean(gd) - xv * _rowmean(gd * xv)))

        dx = alpha * dz
        for k in range(N_CHIPS):
            dx = dx + _dot_nt(dhb_ref[:, COLS * k:COLS * (k + 1)], wi_ref[k])
        dx_ref[...] = dx

        @pl.when(i == nt - 1)
        def _():
            r = lax.broadcasted_iota(jnp.int32, (SGU_BLOCK, GROUP), 0) // CHUNK
            c = (lax.broadcasted_iota(jnp.int32, (SGU_BLOCK, GROUP), 1) % SGU_BLOCK) // CHUNK
            for half in range(2):
                rows_ = slice(ROW_WC + half * SGU_BLOCK, ROW_WC + (half + 1) * SGU_BLOCK)
                osm_ref[rows_, :] = jnp.where(c <= r, osm_ref[rows_, :], 0.0)
            osm_ref[ROW_SB:ROW_SB + SGU_BLOCK, 0:128] = _segdot(dsp_acc[...], e4_ref[...])
            for g in range(4):
                osm_ref[ROW_PW:ROW_PW + HEAD, HEAD * g:HEAD * (g + 1)] = (
                    pw_acc[HEAD * g:HEAD * (g + 1), HEAD * g:HEAD * (g + 1)])

    def full(a):
        nd = a.ndim
        return pl.BlockSpec(a.shape, lambda i, _n=nd: (0,) * _n)

    def rows(width):
        return pl.BlockSpec((T, width), lambda i: (nt - 1 - i, 0))

    def acc(shape):
        return pl.BlockSpec(shape, lambda i: (0, 0))

    consts = (wi, caw, cbw, s256, seg, pw, wm, wmt, sb, wo, v1024, e4)
    in_specs = [rows(D_MODEL), rows(D_MODEL), rows(IN_WIDTH), rows(3 * GROUP)] + [full(a) for a in consts]
    out_specs = [rows(D_MODEL), rows(IN_WIDTH), rows(D_MODEL), acc((SM_ROWS, GROUP))]
    out_shape = [jax.ShapeDtypeStruct((S, D_MODEL), F32), jax.ShapeDtypeStruct((S, IN_WIDTH), BF16),
                 jax.ShapeDtypeStruct((S, D_MODEL), BF16), jax.ShapeDtypeStruct((SM_ROWS, GROUP), F32)]
    scratch = [pltpu.VMEM((T + HALO_A, GROUP), F32), pltpu.VMEM((T + HALO_B, GROUP), F32),
               pltpu.VMEM((T + HALO_C, GROUP), F32), pltpu.VMEM((T, GROUP), F32), pltpu.VMEM((T, GROUP), F32),
               pltpu.VMEM((SGU_BLOCK, 4 * SGU_BLOCK), BF16), pltpu.VMEM((4 * SGU_BLOCK, SGU_BLOCK), BF16),
               pltpu.VMEM((SGU_BLOCK, GROUP), F32), pltpu.VMEM((GROUP, GROUP), F32)]
    extra, aliases = (), {}
    if exch is not None:
        extra = tuple(exch)
        r_i, r_o = exch[4], exch[5]
        in_specs += [pl.BlockSpec(memory_space=pltpu.SMEM)] + [ANY] * 5
        out_specs += [ANY] * 3
        out_shape += [jax.ShapeDtypeStruct(r_i.shape, F32), jax.ShapeDtypeStruct(r_o.shape, F32),
                      jax.ShapeDtypeStruct((N_DEV, SM_ROWS, GROUP), F32)]
        scratch += [pltpu.SemaphoreType.DMA((13,)), pltpu.SemaphoreType.DMA((13,)), pltpu.SemaphoreType.DMA((1,))]
        aliases = {20: 4, 21: 5}
    return pl.pallas_call(
        body, name="bwd_layer" if exch is None else "bwd_layer_exchange",
        grid=(nt,), in_specs=in_specs, out_specs=out_specs, out_shape=out_shape, scratch_shapes=scratch,
        input_output_aliases=aliases,
        compiler_params=_vmem_params(dimension_semantics=("arbitrary",), has_side_effects=exch is not None),
    )(dy, z, h, aux, *consts, *extra)


def _dw_proj(layer, lhs, rhs, slab, *, lhs_cols, rhs_cols, by_lhs, tk, name):
    S = lhs.shape[0]

    def body(l_ref, a_ref, b_ref, slab_ref, o_ref):
        del l_ref, slab_ref

        @pl.when(pl.program_id(1) == 0)
        def _():
            o_ref[...] = jnp.zeros_like(o_ref)
        o_ref[...] += _dot_tn(a_ref[...], b_ref[...])

    if by_lhs:
        a_spec = pl.BlockSpec((tk, lhs_cols), lambda j, s, l: (s, j))
        b_spec = pl.BlockSpec((tk, rhs_cols), lambda j, s, l: (s, 0))
    else:
        a_spec = pl.BlockSpec((tk, lhs_cols), lambda j, s, l: (s, 0))
        b_spec = pl.BlockSpec((tk, rhs_cols), lambda j, s, l: (s, j))
    grid_spec = pltpu.PrefetchScalarGridSpec(
        num_scalar_prefetch=1, grid=(N_CHIPS, S // tk),
        in_specs=[a_spec, b_spec, pl.BlockSpec(memory_space=pl.ANY)],
        out_specs=pl.BlockSpec((None, None, lhs_cols, rhs_cols), lambda j, s, l: (l[0], j, 0, 0)))
    return pl.pallas_call(
        body, name=name, grid_spec=grid_spec,
        out_shape=jax.ShapeDtypeStruct(slab.shape, F32),
        input_output_aliases={3: 0},
        compiler_params=_vmem_params(dimension_semantics=("arbitrary", "arbitrary")),
    )(layer, lhs, rhs, slab)


def _adamw(w, g, m, v, *, rows_per_step, name):
    R, C = w.shape
    tr = rows_per_step
    c1 = 1.0 - ADAM_B1 ** ADAM_STEP
    c2 = 1.0 - ADAM_B2 ** ADAM_STEP

    def body(w_ref, g_ref, m_ref, v_ref, d_ref, nm_ref, nv_ref):
        g_ = g_ref[...]
        nm = ADAM_B1 * m_ref[...] + (1.0 - ADAM_B1) * g_
        nv = ADAM_B2 * v_ref[...] + (1.0 - ADAM_B2) * (g_ * g_)
        nm_ref[...] = nm
        nv_ref[...] = nv
        d_ref[...] = -ADAM_LR * ((nm / c1) / (jnp.sqrt(nv / c2) + ADAM_EPS) + ADAM_WD * w_ref[...])

    spec = pl.BlockSpec((tr, C), lambda i: (i, 0))
    return pl.pallas_call(
        body, name=name, grid=(R // tr,),
        in_specs=[spec] * 4, out_specs=[spec] * 3,
        out_shape=[jax.ShapeDtypeStruct((R, C), F32)] * 3,
        compiler_params=_vmem_params(dimension_semantics=("arbitrary",)),
    )(w, g, m, v)


def _gather_weights(wi16, wo16, cw):
    L = wi16.shape[0]
    hi_rows, ho_rows = D_MODEL // 2, GROUP // 2
    n_ici = 2 * L + 1
    n_fwd = 2 * L

    def body(wi_ref, wo_ref, cw_ref, *rest):
        wig = rest[0:L]
        wog = rest[L:2 * L]
        cwg = rest[2 * L]
        send_sems, recv_sems, loc_sems = rest[2 * L + 1:]
        x, y, c = _place()
        me_k = 2 * x + y
        sibling = (x, y, 1 - c)
        chips = _other_chips(x, y)

        def half_i(ref, blk):
            return ref.at[blk, pl.ds(c * hi_rows, hi_rows), :]

        def half_o(ref, blk):
            return ref.at[blk, pl.ds(c * ho_rows, ho_rows), :]

        def other_half_i(ref, blk):
            return ref.at[blk, pl.ds((1 - c) * hi_rows, hi_rows), :]

        def other_half_o(ref, blk):
            return ref.at[blk, pl.ds((1 - c) * ho_rows, ho_rows), :]

        local = []
        for l in range(L):
            local.append(pltpu.make_async_copy(wi_ref.at[l], wig[l].at[me_k], loc_sems.at[2 * l]))
            local.append(pltpu.make_async_copy(wo_ref.at[l], wog[l].at[me_k], loc_sems.at[2 * l + 1]))
        local.append(pltpu.make_async_copy(cw_ref, cwg.at[me_k], loc_sems.at[2 * L]))
        for cp in local:
            cp.start()

        def remote(src, dst, sem, to):
            return pltpu.make_async_remote_copy(src_ref=src, dst_ref=dst, send_sem=send_sems.at[sem],
                                                recv_sem=recv_sems.at[sem], device_id=to, device_id_type=MESH)

        sends = []
        for r, (px, py, _) in enumerate(chips):
            to = (px, py, c)
            for l in range(L):
                sends.append(remote(half_i(wi_ref, l), half_i(wig[l], me_k), r * n_ici + 2 * l, to))
                sends.append(remote(half_o(wo_ref, l), half_o(wog[l], me_k), r * n_ici + 2 * l + 1, to))
            sends.append(remote(cw_ref, cwg.at[me_k], r * n_ici + 2 * L, to))
        for cp in sends:
            cp.start()

        base = 3 * n_ici
        fwds = []
        for r, (px, py, pk) in enumerate(chips):
            for l in range(L):
                remote(half_i(wig[l], pk), half_i(wig[l], pk), r * n_ici + 2 * l, sibling).wait_recv()
                f = remote(half_i(wig[l], pk), half_i(wig[l], pk), base + r * n_fwd + 2 * l, sibling)
                f.start()
                fwds.append(f)
                remote(half_o(wog[l], pk), half_o(wog[l], pk), r * n_ici + 2 * l + 1, sibling).wait_recv()
                f = remote(half_o(wog[l], pk), half_o(wog[l], pk), base + r * n_fwd + 2 * l + 1, sibling)
                f.start()
                fwds.append(f)
            remote(cwg.at[pk], cwg.at[pk], r * n_ici + 2 * L, sibling).wait_recv()
        for r, (px, py, pk) in enumerate(chips):
            for l in range(L):
                remote(other_half_i(wig[l], pk), other_half_i(wig[l], pk), base + r * n_fwd + 2 * l, sibling).wait_recv()
                remote(other_half_o(wog[l], pk), other_half_o(wog[l], pk), base + r * n_fwd + 2 * l + 1, sibling).wait_recv()
        for cp in sends + fwds:
            cp.wait_send()
        for cp in local:
            cp.wait()

    n_sem = 3 * n_ici + 3 * n_fwd
    out_shape = ([jax.ShapeDtypeStruct((N_CHIPS, D_MODEL, COLS), BF16)] * L
                 + [jax.ShapeDtypeStruct((N_CHIPS, GROUP, D_MODEL), BF16)] * L
                 + [jax.ShapeDtypeStruct((N_CHIPS,) + cw.shape, F32)])
    outs = pl.pallas_call(
        body, name="gather_weights",
        in_specs=[ANY, ANY, ANY], out_specs=[ANY] * (2 * L + 1), out_shape=out_shape,
        scratch_shapes=[pltpu.SemaphoreType.DMA((n_sem,)), pltpu.SemaphoreType.DMA((n_sem,)),
                        pltpu.SemaphoreType.DMA((2 * L + 1,))],
        compiler_params=pltpu.CompilerParams(has_side_effects=True),
    )(wi16, wo16, cw)
    return outs[0:L], outs[L:2 * L], outs[2 * L]


def _swap_halves(l_arr, gwi, gwo, ri, ro):
    hi_rows, ho_rows = D_MODEL // 2, GROUP // 2

    def body(l_ref, gwi_ref, gwo_ref, ri_in, ro_in, ri_ref, ro_ref, send_sems, recv_sems):
        del ri_in, ro_in
        x, y, c = _place()
        l = l_ref[0]
        sibling = (x, y, 1 - c)
        cps = [
            pltpu.make_async_remote_copy(src_ref=gwi_ref.at[l, :, pl.ds((1 - c) * hi_rows, hi_rows), :],
                                         dst_ref=ri_ref.at[l], send_sem=send_sems.at[0], recv_sem=recv_sems.at[0],
                                         device_id=sibling, device_id_type=MESH),
            pltpu.make_async_remote_copy(src_ref=gwo_ref.at[l, :, pl.ds((1 - c) * ho_rows, ho_rows), :],
                                         dst_ref=ro_ref.at[l], send_sem=send_sems.at[1], recv_sem=recv_sems.at[1],
                                         device_id=sibling, device_id_type=MESH),
        ]
        for cp in cps:
            cp.start()
        for cp in cps:
            cp.wait()

    return pl.pallas_call(
        body, name="swap_halves",
        in_specs=[pl.BlockSpec(memory_space=pltpu.SMEM), ANY, ANY, ANY, ANY], out_specs=[ANY, ANY],
        out_shape=[jax.ShapeDtypeStruct(ri.shape, F32), jax.ShapeDtypeStruct(ro.shape, F32)],
        input_output_aliases={3: 0, 4: 1},
        scratch_shapes=[pltpu.SemaphoreType.DMA((2,)), pltpu.SemaphoreType.DMA((2,))],
        compiler_params=pltpu.CompilerParams(has_side_effects=True),
    )(l_arr, gwi, gwo, ri, ro)


def _add_halves(cl_arr, g, r, p, *, rows, cols, tr, name):
    nb = rows // tr

    def body(cl_ref, g_ref, r_ref, p_in, o_ref):
        del cl_ref, p_in
        o_ref[...] = g_ref[...] + r_ref[...]

    grid_spec = pltpu.PrefetchScalarGridSpec(
        num_scalar_prefetch=1, grid=(N_CHIPS, nb),
        in_specs=[pl.BlockSpec((None, None, tr, cols), lambda k, i, cl: (cl[1], k, cl[0] * nb + i, 0)),
                  pl.BlockSpec((None, None, tr, cols), lambda k, i, cl: (cl[1], k, i, 0)), ANY],
        out_specs=pl.BlockSpec((None, None, tr, cols), lambda k, i, cl: (cl[1], k, i, 0)))
    return pl.pallas_call(
        body, name=name, grid_spec=grid_spec,
        out_shape=jax.ShapeDtypeStruct(p.shape, F32),
        input_output_aliases={3: 0},
        compiler_params=_vmem_params(dimension_semantics=("arbitrary",) * 2),
    )(cl_arr, g, r, p)


def _exchange_last(l_arr, p_i, p_o, sm, r_i, r_o):
    def body(l_ref, p_i_ref, p_o_ref, sm_ref, ri_in, ro_in, ri_ref, ro_ref, rsm_ref, send_sems, recv_sems, loc_sem):
        del ri_in, ro_in
        always = l_ref[0] >= 0
        _exchange_comm(always, always, l_ref[0], p_i_ref, p_o_ref, sm_ref, ri_ref, ro_ref, rsm_ref,
                       send_sems, recv_sems, loc_sem)

    return pl.pallas_call(
        body, name="exchange_last",
        in_specs=[pl.BlockSpec(memory_space=pltpu.SMEM)] + [ANY] * 5, out_specs=[ANY] * 3,
        out_shape=[jax.ShapeDtypeStruct(r_i.shape, F32), jax.ShapeDtypeStruct(r_o.shape, F32),
                   jax.ShapeDtypeStruct((N_DEV, SM_ROWS, GROUP), F32)],
        input_output_aliases={4: 0, 5: 1},
        scratch_shapes=[pltpu.SemaphoreType.DMA((13,)), pltpu.SemaphoreType.DMA((13,)), pltpu.SemaphoreType.DMA((1,))],
        compiler_params=pltpu.CompilerParams(has_side_effects=True),
    )(l_arr, p_i, p_o, sm, r_i, r_o)


def _sum_small(r_sm):
    def body(r_ref, o_ref):
        acc = r_ref[0]
        for d in range(1, N_DEV):
            acc = acc + r_ref[d]
        o_ref[...] = acc

    return pl.pallas_call(
        body, name="sum_small",
        out_shape=jax.ShapeDtypeStruct(r_sm.shape[1:], F32),
        compiler_params=_vmem_params(),
    )(r_sm)


def _sum_chunks(kc_arr, p, r, *, rows, cols, tr, name):
    L = p.shape[0]
    nb = rows // tr

    def body(kc_ref, p_ref, r0_ref, r1_ref, r2_ref, o_ref):
        del kc_ref
        o_ref[...] = ((p_ref[...] + r0_ref[...]) + r1_ref[...]) + r2_ref[...]

    def rspec(j):
        return pl.BlockSpec((None, None, tr, cols), lambda l, i, kc, _j=j: (_j, l, i, 0))

    grid_spec = pltpu.PrefetchScalarGridSpec(
        num_scalar_prefetch=1, grid=(L, nb),
        in_specs=[pl.BlockSpec((None, None, tr, cols), lambda l, i, kc: (l, kc[0], i, 0)), rspec(0), rspec(1), rspec(2)],
        out_specs=pl.BlockSpec((None, tr, cols), lambda l, i, kc: (l, kc[1] * nb + i, 0)))
    return pl.pallas_call(
        body, name=name, grid_spec=grid_spec,
        out_shape=jax.ShapeDtypeStruct((L, 2 * rows, cols), F32),
        compiler_params=_vmem_params(dimension_semantics=("arbitrary",) * 2),
    )(kc_arr, p, r, r, r)


def _share_result(gi, go):
    hi_rows, ho_rows = gi.shape[1] // 2, go.shape[1] // 2

    def body(gi_ref, go_ref, oi_ref, oo_ref, send_sems, recv_sems):
        del gi_ref, go_ref
        x, y, c = _place()
        sibling = (x, y, 1 - c)
        cps = []
        for j, (ref, n) in enumerate(((oi_ref, hi_rows), (oo_ref, ho_rows))):
            mine = ref.at[:, pl.ds(c * n, n), :]
            cps.append(pltpu.make_async_remote_copy(src_ref=mine, dst_ref=mine, send_sem=send_sems.at[j],
                                                    recv_sem=recv_sems.at[j], device_id=sibling, device_id_type=MESH))
        for cp in cps:
            cp.start()
        for j, (ref, n) in enumerate(((oi_ref, hi_rows), (oo_ref, ho_rows))):
            theirs = ref.at[:, pl.ds((1 - c) * n, n), :]
            pltpu.make_async_remote_copy(src_ref=theirs, dst_ref=theirs, send_sem=send_sems.at[j],
                                         recv_sem=recv_sems.at[j], device_id=sibling, device_id_type=MESH).wait_recv()
        for cp in cps:
            cp.wait_send()

    return pl.pallas_call(
        body, name="share_result",
        in_specs=[ANY, ANY], out_specs=[ANY, ANY],
        out_shape=[jax.ShapeDtypeStruct(gi.shape, F32), jax.ShapeDtypeStruct(go.shape, F32)],
        input_output_aliases={0: 0, 1: 1},
        scratch_shapes=[pltpu.SemaphoreType.DMA((2,)), pltpu.SemaphoreType.DMA((2,))],
        compiler_params=pltpu.CompilerParams(has_side_effects=True),
    )(gi, go)


SMALL = ("ln_g", "ln_b", "b_in", "conv_a_w", "conv_a_b", "norm_a_g", "norm_a_b", "conv_b_w", "pool_w", "pool_scale",
         "sgu_ln_g", "sgu_ln_b", "sgu_w", "sgu_bias", "b_out")
WEIGHTS = ("ln_g", "ln_b", "w_in", "b_in", "conv_a_w", "conv_a_b", "norm_a_g", "norm_a_b", "conv_b_w", "pool_w",
           "pool_scale", "sgu_ln_g", "sgu_ln_b", "sgu_w", "sgu_bias", "w_out", "b_out")


def _pad_rows(a, rows):
    return jnp.pad(a, ((0, rows - a.shape[0]), (0, 0)))


def _indicator_consts():
    seg = jnp.where((jnp.arange(GROUP)[:, None] // HEAD) == (jnp.arange(GROUP)[None, :] // HEAD),
                    1.0 / HEAD, 0.0).astype(BF16)
    e4 = ((jnp.arange(GROUP)[:, None] // HEAD) == jnp.arange(128)[None, :]).astype(BF16)
    return seg, e4


def _layer_consts(p, conv_full, l):
    same_head = jnp.eye(4, dtype=F32)[:, None, :, None] > 0
    caw = _pad_rows(conv_full[l, :KA], 32)
    cbw = _pad_rows(conv_full[l, KA:], 8)
    s256 = _pad_rows(jnp.stack([p["conv_a_b"][l], p["norm_a_g"][l], p["norm_a_b"][l], p["pool_scale"][l],
                                p["sgu_ln_g"][l], p["sgu_ln_b"][l]]), 8)
    pw = jnp.where(same_head, p["pool_w"][l][:, :, None, :], 0.0).reshape(GROUP, GROUP).astype(BF16)
    wm = jnp.transpose(p["sgu_w"][l], (1, 0, 2)).reshape(SGU_BLOCK, 4 * SGU_BLOCK)
    wmt = jnp.transpose(p["sgu_w"][l], (0, 2, 1)).reshape(4 * SGU_BLOCK, SGU_BLOCK)
    sb = jnp.repeat(p["sgu_bias"][l].T, HEAD, axis=1)
    v1024 = _pad_rows(jnp.stack([p["b_out"][l], p["ln_g"][l], p["ln_b"][l]]), 8)
    return dict(caw=caw, cbw=cbw, s256=s256, pw=pw, wm=wm, wmt=wmt, sb=sb, v1024=v1024, bin=p["b_in"][l][None, :])


def _unpack_small(sm):
    owc = jnp.concatenate([sm[ROW_WC:ROW_WC + SGU_BLOCK], sm[ROW_WC + SGU_BLOCK:ROW_WC + 2 * SGU_BLOCK]], axis=1)
    return dict(
        conv_a_b=sm[0], norm_a_g=sm[1], norm_a_b=sm[2], pool_scale=sm[3], sgu_ln_g=sm[4], sgu_ln_b=sm[5],
        conv_b_w=sm[ROW_CBW:ROW_CBW + KB], conv_a_w=sm[ROW_CAW:ROW_CAW + KA],
        pool_w=jnp.transpose(sm[ROW_PW:ROW_PW + HEAD].reshape(HEAD, 4, HEAD), (1, 0, 2)),
        ln_g=sm[ROW_LNG:ROW_LNG + 4].reshape(D_MODEL), ln_b=sm[ROW_LNB:ROW_LNB + 4].reshape(D_MODEL),
        b_out=sm[ROW_BOUT:ROW_BOUT + 4].reshape(D_MODEL), b_in=sm[ROW_BIN:ROW_BIN + N_SLICES].reshape(IN_WIDTH),
        sgu_w=jnp.transpose(owc.reshape(SGU_BLOCK, 4, SGU_BLOCK), (1, 0, 2)),
        sgu_bias=sm[ROW_SB:ROW_SB + SGU_BLOCK, 0:4].T)


def _step(p, m, v, x, target, *, tile_f, tile_b, tk):
    L = p["ln_g"].shape[0]
    xi, yi, ci = _place()
    me_k = 2 * xi + yi
    hi_rows, ho_rows = D_MODEL // 2, GROUP // 2

    cw = jnp.concatenate([p["conv_a_w"], p["conv_b_w"]], axis=1).reshape(-1, 128)
    cw_rows = cw.shape[0]
    cw = _pad_rows(cw, 72)
    wi16 = p["w_in"].astype(BF16)
    wo16 = p["w_out"].astype(BF16)
    wig0, wog0, cwg = _gather_weights(wi16[0:1], wo16[0:1], cw)
    cwg = cwg[:, :cw_rows].reshape(N_CHIPS, L, KA + KB, HEAD)
    conv_full = jnp.transpose(cwg, (1, 2, 0, 3)).reshape(L, KA + KB, GROUP)
    seg, e4 = _indicator_consts()
    consts = [_layer_consts(p, conv_full, l) for l in range(L)]

    hcur = x
    saved, wig, wog = [], [wig0[0]], [wog0[0]]
    for l in range(L):
        k = consts[l]
        nxt = (wi16[l + 1], wo16[l + 1]) if l + 1 < L else None
        outs = _fwd_layer(hcur, wig[l], k["bin"], k["caw"], k["cbw"], k["s256"], seg, k["pw"], k["wm"], k["sb"], wog[l],
                          k["v1024"], tile=tile_f, nxt=nxt)
        y, xb, h, aux, mixb, z = outs[0:6]
        if nxt is not None:
            wig.append(outs[6])
            wog.append(outs[7])
        saved.append((xb, h, aux, mixb, z))
        hcur = y

    dy, loss_acc = _loss_head(hcur, target, tile=tile_f)
    loss = lax.psum(loss_acc[0, 0], ("x", "y", "c"))

    gwi = lax.empty((L, N_CHIPS, D_MODEL, COLS), F32)
    gwo = lax.empty((L, N_CHIPS, GROUP, D_MODEL), F32)
    ri = lax.empty((L, N_CHIPS, hi_rows, COLS), F32)
    ro = lax.empty((L, N_CHIPS, ho_rows, D_MODEL), F32)
    p_i = lax.empty((L, N_CHIPS, hi_rows, COLS), F32)
    p_o = lax.empty((L, N_CHIPS, ho_rows, D_MODEL), F32)
    q_i = lax.empty((3, L, hi_rows, COLS), F32)
    q_o = lax.empty((3, L, ho_rows, D_MODEL), F32)
    r_sm = [None] * L
    pending = None
    for l in reversed(range(L)):
        k = consts[l]
        xb, h, aux, mixb, z = saved[l]
        exch = None if pending is None else (pending[0], p_i, p_o, pending[1], q_i, q_o)
        outs = _bwd_layer(dy, z, h, aux, wig[l], k["caw"], k["cbw"], k["s256"], seg, k["pw"], k["wm"], k["wmt"],
                          k["sb"], wog[l], k["v1024"], e4, tile=tile_b, exch=exch)
        dy, dhb, dzb, osm = outs[0:4]
        if exch is not None:
            q_i, q_o, r_sm[l + 1] = outs[4:7]
        larr = jnp.full((1,), l, jnp.int32)
        gwi = _dw_proj(larr, xb, dhb, gwi, lhs_cols=D_MODEL, rhs_cols=COLS, by_lhs=False, tk=tk, name="dw_in")
        gwo = _dw_proj(larr, mixb, dzb, gwo, lhs_cols=GROUP, rhs_cols=D_MODEL, by_lhs=True, tk=tk, name="dw_out")
        ri, ro = _swap_halves(larr, gwi, gwo, ri, ro)
        cl_arr = jnp.stack([ci, jnp.int32(l)]).astype(jnp.int32)
        p_i = _add_halves(cl_arr, gwi, ri, p_i, rows=hi_rows, cols=COLS, tr=256, name="add_halves_in")
        p_o = _add_halves(cl_arr, gwo, ro, p_o, rows=ho_rows, cols=D_MODEL, tr=128, name="add_halves_out")
        pending = (larr, osm)
    grad_x = dy
    q_i, q_o, r_sm[0] = _exchange_last(pending[0], p_i, p_o, pending[1], q_i, q_o)

    per_layer = [_unpack_small(_sum_small(r_sm[l])) for l in range(L)]
    grads = {n: jnp.stack([per_layer[l][n] for l in range(L)]) for n in SMALL}
    for n in ("conv_a_w", "conv_b_w"):
        grads[n] = lax.dynamic_slice_in_dim(grads[n], me_k * HEAD, HEAD, axis=2)

    kc_arr = jnp.stack([me_k, ci]).astype(jnp.int32)
    g_i = _sum_chunks(kc_arr, p_i, q_i, rows=hi_rows, cols=COLS, tr=256, name="sum_chunks_in")
    g_o = _sum_chunks(kc_arr, p_o, q_o, rows=ho_rows, cols=D_MODEL, tr=128, name="sum_chunks_out")
    g_i, g_o = _share_result(g_i, g_o)
    grads["w_in"] = g_i
    grads["w_out"] = g_o

    delta, new_m, new_v = {}, {}, {}
    for n in WEIGHTS:
        shp = p[n].shape
        if n in ("w_in", "w_out"):
            two_d = (shp[0] * shp[1], shp[2])
            tr = 512 if n == "w_in" else 256
        else:
            two_d = (-1, shp[-1])
            tr = None
        args = [a.reshape(two_d) for a in (p[n], grads[n], m[n], v[n])]
        d, nm, nv = _adamw(*args, rows_per_step=tr or args[0].shape[0], name="adamw_" + n)
        delta[n], new_m[n], new_v[n] = d.reshape(shp), nm.reshape(shp), nv.reshape(shp)

    return (loss, grad_x[None], *[grads[n] for n in WEIGHTS], *[delta[n] for n in WEIGHTS],
            *[new_m[n] for n in WEIGHTS], *[new_v[n] for n in WEIGHTS])


def kernel(x, ln_g, ln_b, w_in, b_in, conv_a_w, conv_a_b, norm_a_g, norm_a_b, conv_b_w, pool_w, pool_scale, sgu_ln_g, sgu_ln_b, sgu_w, sgu_bias, w_out, b_out, loss_target, m_ln_g, m_ln_b, m_w_in, m_b_in, m_conv_a_w, m_conv_a_b, m_norm_a_g, m_norm_a_b, m_conv_b_w, m_pool_w, m_pool_scale, m_sgu_ln_g, m_sgu_ln_b, m_sgu_w, m_sgu_bias, m_w_out, m_b_out, v_ln_g, v_ln_b, v_w_in, v_b_in, v_conv_a_w, v_conv_a_b, v_norm_a_g, v_norm_a_b, v_conv_b_w, v_pool_w, v_pool_scale, v_sgu_ln_g, v_sgu_ln_b, v_sgu_w, v_sgu_bias, v_w_out, v_b_out):
    p = dict(ln_g=ln_g, ln_b=ln_b, w_in=w_in, b_in=b_in, conv_a_w=conv_a_w, conv_a_b=conv_a_b, norm_a_g=norm_a_g,
             norm_a_b=norm_a_b, conv_b_w=conv_b_w, pool_w=pool_w, pool_scale=pool_scale, sgu_ln_g=sgu_ln_g,
             sgu_ln_b=sgu_ln_b, sgu_w=sgu_w, sgu_bias=sgu_bias, w_out=w_out, b_out=b_out)
    m = dict(ln_g=m_ln_g, ln_b=m_ln_b, w_in=m_w_in, b_in=m_b_in, conv_a_w=m_conv_a_w, conv_a_b=m_conv_a_b,
             norm_a_g=m_norm_a_g, norm_a_b=m_norm_a_b, conv_b_w=m_conv_b_w, pool_w=m_pool_w, pool_scale=m_pool_scale,
             sgu_ln_g=m_sgu_ln_g, sgu_ln_b=m_sgu_ln_b, sgu_w=m_sgu_w, sgu_bias=m_sgu_bias, w_out=m_w_out, b_out=m_b_out)
    v = dict(ln_g=v_ln_g, ln_b=v_ln_b, w_in=v_w_in, b_in=v_b_in, conv_a_w=v_conv_a_w, conv_a_b=v_conv_a_b,
             norm_a_g=v_norm_a_g, norm_a_b=v_norm_a_b, conv_b_w=v_conv_b_w, pool_w=v_pool_w, pool_scale=v_pool_scale,
             sgu_ln_g=v_sgu_ln_g, sgu_ln_b=v_sgu_ln_b, sgu_w=v_sgu_w, sgu_bias=v_sgu_bias, w_out=v_w_out, b_out=v_b_out)
    return _step(p, m, v, x[0], loss_target[0], tile_f=256, tile_b=256, tk=2048)
```

```python
import functools

import jax
import jax.numpy as jnp
from jax import lax
from jax.experimental import pallas as pl
from jax.experimental.pallas import tpu as pltpu

F32 = jnp.float32
BF16 = jnp.bfloat16
MESH = pl.DeviceIdType.MESH

D_MODEL = 1024
GROUP = 256
HEAD = 64
N_SLICES = 12
IN_WIDTH = N_SLICES * GROUP
N_CHIPS = 4
COLS = IN_WIDTH // N_CHIPS
KA = 31
KB = 3
HALO_A, HALO_B, HALO_C = 32, 8, 16
POOL_WINDOWS = (2, 4, 8, 16)
SGU_BLOCK = 128
CHUNK = 64
LN_EPS = 1e-5
ROWS = 64
V7X_VMEM_BYTES = 64 * 1024 * 1024
VMEM_LIMIT = 56 * 1024 * 1024

ADAM_LR, ADAM_B1, ADAM_B2, ADAM_EPS, ADAM_WD, ADAM_STEP = 0.001, 0.9, 0.999, 1e-08, 0.01, 10


ANY = pl.BlockSpec(memory_space=pl.ANY)


def _vmem_params(**kw):
    return pltpu.CompilerParams(vmem_limit_bytes=VMEM_LIMIT, **kw)


def _place():
    return lax.axis_index("x"), lax.axis_index("y"), lax.axis_index("c")


def _other_chips(x, y):
    return [(1 - x, y, 2 * (1 - x) + y), (x, 1 - y, 2 * x + (1 - y)), (1 - x, 1 - y, 2 * (1 - x) + (1 - y))]


def _sig(v):
    return 0.5 * jnp.tanh(0.5 * v) + 0.5


def _dot(a, b):
    return jnp.dot(a, b, preferred_element_type=F32)


def _dot_nt(a, b):
    return lax.dot_general(a, b, (((1,), (1,)), ((), ())), preferred_element_type=F32)


def _dot_tn(a, b):
    return lax.dot_general(a, b, (((0,), (0,)), ((), ())), preferred_element_type=F32)


def _segdot(v, m):
    hi = v.astype(BF16)
    lo = (v - hi.astype(F32)).astype(BF16)
    return _dot(hi, m) + _dot(lo, m)


def _colsum(v):
    return jnp.sum(v, axis=0, keepdims=True)


def _rowmean(v):
    return jnp.mean(v, axis=-1, keepdims=True)


def _lane_group(n):
    return lax.broadcasted_iota(jnp.int32, (1, n), 1) // HEAD


def _pool_cnt(tile, t_rows):
    pos = tile * t_rows + lax.broadcasted_iota(jnp.int32, (t_rows, GROUP), 0) + 1
    grp = lax.broadcasted_iota(jnp.int32, (t_rows, GROUP), 1) // HEAD
    win = jnp.where(grp == 0, 2, jnp.where(grp == 1, 4, jnp.where(grp == 2, 8, 16)))
    return jnp.minimum(pos, win).astype(F32)


def _sgu_masks(wm_ref, wmt_ref, wm_s, wmt_s):
    r = lax.broadcasted_iota(jnp.int32, (SGU_BLOCK, 4 * SGU_BLOCK), 0) // CHUNK
    c = (lax.broadcasted_iota(jnp.int32, (SGU_BLOCK, 4 * SGU_BLOCK), 1) % SGU_BLOCK) // CHUNK
    wm_s[...] = jnp.where(c <= r, wm_ref[...], 0.0).astype(BF16)
    if wmt_ref is not None:
        rt = (lax.broadcasted_iota(jnp.int32, (4 * SGU_BLOCK, SGU_BLOCK), 0) % SGU_BLOCK) // CHUNK
        ct = lax.broadcasted_iota(jnp.int32, (4 * SGU_BLOCK, SGU_BLOCK), 1) // CHUNK
        wmt_s[...] = jnp.where(rt <= ct, wmt_ref[...], 0.0).astype(BF16)


def _vstack(v_blk):
    grp = _lane_group(GROUP)
    return jnp.concatenate([jnp.where(grp == h, v_blk, 0.0) for h in range(4)], axis=0).astype(BF16)


def _gather_next(step, nt, nwi, nwo, gwi, gwo, send_sems, recv_sems, loc_sems):
    x, y, c = _place()
    me_k = 2 * x + y
    sibling = (x, y, 1 - c)
    chips = _other_chips(x, y)
    hi, ho = D_MODEL // 2, GROUP // 2

    def rc(src, dst, sem, to):
        return pltpu.make_async_remote_copy(src_ref=src, dst_ref=dst, send_sem=send_sems.at[sem],
                                            recv_sem=recv_sems.at[sem], device_id=to, device_id_type=MESH)

    def blk(ref, k, n, cc):
        return ref.at[k, pl.ds(cc * n, n), :]

    def ici(r):
        px, py, _ = chips[r]
        to = (px, py, c)
        return [rc(nwi.at[pl.ds(c * hi, hi), :], blk(gwi, me_k, hi, c), 2 * r, to),
                rc(nwo.at[pl.ds(c * ho, ho), :], blk(gwo, me_k, ho, c), 2 * r + 1, to)]

    def landed(r, cc, base):
        pk = chips[r][2]
        return [rc(blk(gwi, pk, hi, cc), blk(gwi, pk, hi, cc), base + 2 * r, sibling),
                rc(blk(gwo, pk, ho, cc), blk(gwo, pk, ho, cc), base + 2 * r + 1, sibling)]

    def local():
        return [pltpu.make_async_copy(nwi, gwi.at[me_k], loc_sems.at[0]),
                pltpu.make_async_copy(nwo, gwo.at[me_k], loc_sems.at[1])]

    @pl.when(step == 0)
    def _():
        for cp in local():
            cp.start()
        for r in range(3):
            for cp in ici(r):
                cp.start()

    @pl.when(step == nt // 2)
    def _():
        for r in range(3):
            for got, fwd in zip(landed(r, c, 0), landed(r, c, 6)):
                got.wait_recv()
                fwd.start()

    @pl.when(step == nt - 1)
    def _():
        for r in range(3):
            for got in landed(r, 1 - c, 6):
                got.wait_recv()
        for r in range(3):
            for cp in ici(r) + landed(r, c, 6):
                cp.wait_send()
        for cp in local():
            cp.wait()


def _fwd_layer(x, wi, bin_, caw, cbw, s256, seg, pw, wm, sb, wo, v1024, *, tile, nxt=None, target=None):
    assert nxt is None or target is None
    S = x.shape[0]
    T = tile
    nt = S // T
    alpha = float((2.0 * 4) ** 0.25)
    n_in = 12 + (2 if nxt is not None else 0) + (1 if target is not None else 0)
    n_out = 6 + (2 if nxt is not None else 0) + (1 if target is not None else 0)

    def body(*refs):
        (x_ref, wi_ref, bin_ref, caw_ref, cbw_ref, s256_ref, seg_ref, pw_ref, wm_ref, sb_ref, wo_ref,
         v1024_ref) = refs[0:12]
        y_ref, xb_ref, h_ref, aux_ref, mix_ref, z_ref = refs[n_in:n_in + 6]
        abuf, bbuf, cbuf, wm_s = refs[n_in + n_out:n_in + n_out + 4]
        i = pl.program_id(0)
        if nxt is not None:
            _gather_next(i, nt, refs[12], refs[13], refs[n_in + 6], refs[n_in + 7], *refs[n_in + n_out + 4:])

        @pl.when(i == 0)
        def _():
            abuf[0:HALO_A, :] = jnp.zeros((HALO_A, GROUP), F32)
            bbuf[0:HALO_B, :] = jnp.zeros((HALO_B, GROUP), F32)
            cbuf[0:HALO_C, :] = jnp.zeros((HALO_C, GROUP), F32)
            _sgu_masks(wm_ref, None, wm_s, None)

        x = x_ref[...]
        xb = x.astype(BF16)
        xb_ref[...] = xb
        for k in range(N_CHIPS):
            h_ref[:, COLS * k:COLS * (k + 1)] = _dot(xb, wi_ref[k]) + bin_ref[:, COLS * k:COLS * (k + 1)]

        def hs(j):
            return h_ref[:, GROUP * j:GROUP * (j + 1)]

        abuf[HALO_A:HALO_A + T, :] = hs(0) * _sig(hs(1))
        for r0 in range(0, T, ROWS):
            acc = None
            for k in range(KA):
                off = HALO_A - (KA - 1) + k + r0
                term = caw_ref[k:k + 1, :] * abuf[off:off + ROWS, :]
                acc = term if acc is None else acc + term
            aux_ref[r0:r0 + ROWS, 0:GROUP] = acc + s256_ref[0:1, :]
        abuf[0:HALO_A, :] = abuf[T:T + HALO_A, :]
        a1 = aux_ref[:, 0:GROUP]
        segm = seg_ref[...]
        cen = a1 - _segdot(a1, segm)
        var = _segdot(cen * cen, segm)
        a2 = cen * lax.rsqrt(var + LN_EPS) * s256_ref[1:2, :] + s256_ref[2:3, :]
        az = hs(2)
        mix_ref[:, 0:GROUP] = (a2 * _sig(a2) * (az * _sig(az))).astype(BF16)

        bbuf[HALO_B:HALO_B + T, :] = hs(4) * hs(5)
        for r0 in range(0, T, ROWS):
            acc = None
            for k in range(KB):
                off = HALO_B - (KB - 1) + k + r0
                term = cbw_ref[k:k + 1, :] * bbuf[off:off + ROWS, :]
                acc = term if acc is None else acc + term
            aux_ref[r0:r0 + ROWS, GROUP:2 * GROUP] = acc
        bbuf[0:HALO_B, :] = bbuf[T:T + HALO_B, :]
        bz = hs(6)
        mix_ref[:, GROUP:2 * GROUP] = (hs(3) * aux_ref[:, GROUP:2 * GROUP] * (bz * _sig(bz))).astype(BF16)

        ch = hs(7)
        cbuf[HALO_C:HALO_C + T, :] = ch
        hi_lane = (lax.broadcasted_iota(jnp.int32, (1, 128), 1) // HEAD) == 1
        for r0 in range(0, T, ROWS):
            def win(col, j0, j1):
                s = None
                for j in range(j0, j1):
                    off = HALO_C - j + r0
                    term = cbuf[off:off + ROWS, 128 * col:128 * (col + 1)]
                    s = term if s is None else s + term
                return s
            w0 = win(0, 0, 2) + jnp.where(hi_lane, win(0, 2, 4), 0.0)
            w1 = win(1, 0, 8) + jnp.where(hi_lane, win(1, 8, 16), 0.0)
            aux_ref[r0:r0 + ROWS, 2 * GROUP:2 * GROUP + 128] = w0
            aux_ref[r0:r0 + ROWS, 2 * GROUP + 128:3 * GROUP] = w1
        cbuf[0:HALO_C, :] = cbuf[T:T + HALO_C, :]
        pooled = aux_ref[:, 2 * GROUP:3 * GROUP] / _pool_cnt(i, T) - ch
        aux_ref[:, 2 * GROUP:3 * GROUP] = pooled
        q = _dot(pooled.astype(BF16), pw_ref[...])
        cz = hs(8)
        mix_ref[:, 2 * GROUP:3 * GROUP] = (q * s256_ref[3:4, :] * (cz * _sig(cz))).astype(BF16)

        dv = hs(10)
        cen = dv - _rowmean(dv)
        var = _rowmean(cen * cen)
        v = cen * lax.rsqrt(var + LN_EPS) * s256_ref[4:5, :] + s256_ref[5:6, :]
        sps = []
        for n in range(T // SGU_BLOCK):
            vb = v[n * SGU_BLOCK:(n + 1) * SGU_BLOCK, :]
            sps.append(_dot(wm_s[...], _vstack(vb)) + sb_ref[...])
        sp = jnp.concatenate(sps, axis=0)
        dz = hs(11)
        mix_ref[:, 3 * GROUP:4 * GROUP] = (hs(9) * sp * (dz * _sig(dz))).astype(BF16)

        out = v1024_ref[0:1, :]
        for k in range(N_CHIPS):
            out = out + _dot(mix_ref[:, GROUP * k:GROUP * (k + 1)], wo_ref[k])
        z = alpha * x + out
        z_ref[...] = z
        cen = z - _rowmean(z)
        var = _rowmean(cen * cen)
        y = cen * lax.rsqrt(var + LN_EPS) * v1024_ref[1:2, :] + v1024_ref[2:3, :]
        if target is None:
            y_ref[...] = y
        else:
            t_ref, loss_ref = refs[12], refs[n_in + 6]

            @pl.when(i == 0)
            def _():
                loss_ref[...] = jnp.zeros_like(loss_ref)
            err = y - t_ref[...]
            y_ref[...] = err * (1.0 / D_MODEL)
            loss_ref[...] += jnp.sum(_colsum(err * err), axis=1, keepdims=True) * (0.5 / D_MODEL)

    def full(a):
        nd = a.ndim
        return pl.BlockSpec(a.shape, lambda i, _n=nd: (0,) * _n)

    def rows(width):
        return pl.BlockSpec((T, width), lambda i: (i, 0))

    consts = (wi, bin_, caw, cbw, s256, seg, pw, wm, sb, wo, v1024)
    in_specs = [rows(D_MODEL)] + [full(a) for a in consts]
    out_specs = [rows(D_MODEL), rows(D_MODEL), rows(IN_WIDTH), rows(3 * GROUP), rows(D_MODEL), rows(D_MODEL)]
    out_shape = [jax.ShapeDtypeStruct((S, D_MODEL), F32), jax.ShapeDtypeStruct((S, D_MODEL), BF16),
                 jax.ShapeDtypeStruct((S, IN_WIDTH), F32), jax.ShapeDtypeStruct((S, 3 * GROUP), F32),
                 jax.ShapeDtypeStruct((S, D_MODEL), BF16), jax.ShapeDtypeStruct((S, D_MODEL), F32)]
    scratch = [pltpu.VMEM((T + HALO_A, GROUP), F32), pltpu.VMEM((T + HALO_B, GROUP), F32),
               pltpu.VMEM((T + HALO_C, GROUP), F32), pltpu.VMEM((SGU_BLOCK, 4 * SGU_BLOCK), BF16)]
    extra = ()
    if nxt is not None:
        extra = tuple(nxt)
        in_specs += [ANY, ANY]
        out_specs += [ANY, ANY]
        out_shape += [jax.ShapeDtypeStruct((N_CHIPS, D_MODEL, COLS), BF16),
                      jax.ShapeDtypeStruct((N_CHIPS, GROUP, D_MODEL), BF16)]
        scratch += [pltpu.SemaphoreType.DMA((12,)), pltpu.SemaphoreType.DMA((12,)), pltpu.SemaphoreType.DMA((2,))]
    if target is not None:
        extra = (target,)
        in_specs += [rows(D_MODEL)]
        out_specs += [pl.BlockSpec((8, 128), lambda i: (0, 0))]
        out_shape += [jax.ShapeDtypeStruct((8, 128), F32)]
    return pl.pallas_call(
        body, name=("fwd_layer_loss" if target is not None else "fwd_layer") if nxt is None else "fwd_layer_gather",
        grid=(nt,), in_specs=in_specs, out_specs=out_specs, out_shape=out_shape, scratch_shapes=scratch,
        compiler_params=_vmem_params(dimension_semantics=("arbitrary",), has_side_effects=nxt is not None),
    )(x, *consts, *extra)


ROW_CBW = 8
ROW_CAW = 16
ROW_PW = 48
ROW_LNG = 112
ROW_LNB = 116
ROW_BOUT = 120
ROW_BIN = 124
ROW_WC = 136
ROW_SB = 392
SM_ROWS = 520
N_DEV = 8


def _exchange_comm(start, finish, l, p_i, p_o, sm, r_i, r_o, r_sm, send_sems, recv_sems, loc_sem):
    x, y, c = _place()
    me = 4 * x + 2 * y + c
    chips = _other_chips(x, y)

    def rc(src, dst, sem, to):
        return pltpu.make_async_remote_copy(src_ref=src, dst_ref=dst, send_sem=send_sems.at[sem],
                                            recv_sem=recv_sems.at[sem], device_id=to, device_id_type=MESH)

    def big(r):
        px, py, pk = chips[r]
        to = (px, py, c)
        return [rc(p_i.at[l, pk], r_i.at[r, l], 2 * r, to), rc(p_o.at[l, pk], r_o.at[r, l], 2 * r + 1, to)]

    def peer(rel):
        px = 1 - x if rel & 4 else x
        py = 1 - y if rel & 2 else y
        pc = 1 - c if rel & 1 else c
        return (px, py, pc), 4 * px + 2 * py + pc

    def small_out(rel):
        to, _ = peer(rel)
        return rc(sm, r_sm.at[me], 5 + rel, to)

    def small_in(rel):
        to, idx = peer(rel)
        return rc(sm, r_sm.at[idx], 5 + rel, to)

    def local():
        return pltpu.make_async_copy(sm, r_sm.at[me], loc_sem.at[0])

    @pl.when(start)
    def _():
        local().start()
        for r in range(3):
            for cp in big(r):
                cp.start()
        for rel in range(1, N_DEV):
            small_out(rel).start()

    @pl.when(finish)
    def _():
        for r in range(3):
            for cp in big(r):
                cp.wait()
        for rel in range(1, N_DEV):
            small_in(rel).wait_recv()
            small_out(rel).wait_send()
        local().wait()


def _bwd_layer(dy, z, h, aux, wi, caw, cbw, s256, seg, pw, wm, wmt, sb, wo, v1024, e4, *, tile, exch=None):
    S = dy.shape[0]
    T = tile
    nt = S // T
    alpha = float((2.0 * 4) ** 0.25)
    n_in = 16 + (6 if exch is not None else 0)
    n_out = 4 + (3 if exch is not None else 0)

    def body(*refs):
        (dy_ref, z_ref, h_ref, aux_ref, wi_ref, caw_ref, cbw_ref, s256_ref, seg_ref, pw_ref, wm_ref, wmt_ref,
         sb_ref, wo_ref, v1024_ref, e4_ref) = refs[0:16]
        dx_ref, dhb_ref, dzb_ref, osm_ref = refs[n_in:n_in + 4]
        dbuf, ebuf, fbuf, a0_s, u_s, wm_s, wmt_s, dsp_acc, pw_acc = refs[n_in + n_out:n_in + n_out + 9]
        i = pl.program_id(0)
        tile_idx = nt - 1 - i
        if exch is not None:
            l_ref, p_i, p_o, sm = refs[16:20]
            r_i, r_o, r_sm = refs[n_in + 4:n_in + 7]
            _exchange_comm(i == 0, i == nt - 1, l_ref[0], p_i, p_o, sm, r_i, r_o, r_sm, *refs[n_in + n_out + 9:])

        @pl.when(i == 0)
        def _():
            dbuf[T:T + HALO_A, :] = jnp.zeros((HALO_A, GROUP), F32)
            ebuf[T:T + HALO_B, :] = jnp.zeros((HALO_B, GROUP), F32)
            fbuf[T:T + HALO_C, :] = jnp.zeros((HALO_C, GROUP), F32)
            _sgu_masks(wm_ref, wmt_ref, wm_s, wmt_s)
            osm_ref[...] = jnp.zeros_like(osm_ref)
            dsp_acc[...] = jnp.zeros_like(dsp_acc)
            pw_acc[...] = jnp.zeros_like(pw_acc)

        def hs(j):
            return h_ref[:, GROUP * j:GROUP * (j + 1)]

        def acc_row(row, val):
            osm_ref[row:row + 1, :] += _colsum(val)

        def acc_wide(row, val):
            cs = _colsum(val)
            for j in range(D_MODEL // GROUP):
                osm_ref[row + j:row + j + 1, :] += cs[:, GROUP * j:GROUP * (j + 1)]

        def put_dh(j, val):
            acc_row(ROW_BIN + j, val)
            dhb_ref[:, GROUP * j:GROUP * (j + 1)] = val.astype(BF16)

        def dsilu(v, s):
            return s * (1.0 + v * (1.0 - s))

        dy = dy_ref[...]
        z = z_ref[...]
        cen = z - _rowmean(z)
        rstd = lax.rsqrt(_rowmean(cen * cen) + LN_EPS)
        xhat = cen * rstd
        acc_wide(ROW_LNG, dy * xhat)
        acc_wide(ROW_LNB, dy)
        gdy = dy * v1024_ref[1:2, :]
        dz = rstd * (gdy - _rowmean(gdy) - xhat * _rowmean(gdy * xhat))
        acc_wide(ROW_BOUT, dz)
        dzb = dz.astype(BF16)
        dzb_ref[...] = dzb

        def dmix(k):
            return _dot_nt(dzb, wo_ref[k])

        segm = seg_ref[...]

        a_val, a_glu, a_z = hs(0), hs(1), hs(2)
        sg = _sig(a_glu)
        a0_s[...] = a_val * sg
        a1 = aux_ref[:, 0:GROUP]
        cen = a1 - _segdot(a1, segm)
        rstd_a = lax.rsqrt(_segdot(cen * cen, segm) + LN_EPS)
        xh = cen * rstd_a
        a2 = xh * s256_ref[1:2, :] + s256_ref[2:3, :]
        s2 = _sig(a2)
        sz = _sig(a_z)
        dya = dmix(0)
        put_dh(2, dya * (a2 * s2) * dsilu(a_z, sz))
        d_a2 = dya * (a_z * sz) * dsilu(a2, s2)
        acc_row(1, d_a2 * xh)
        acc_row(2, d_a2)
        gd = d_a2 * s256_ref[1:2, :]
        d_a1 = rstd_a * (gd - _segdot(gd, segm) - xh * _segdot(gd * xh, segm))
        acc_row(0, d_a1)
        dbuf[0:T, :] = d_a1
        for r0 in range(0, T, ROWS):
            a0c = a0_s[r0:r0 + ROWS, :]
            acc = None
            for k in range(KA):
                off = (KA - 1) - k + r0
                w = dbuf[off:off + ROWS, :]
                term = caw_ref[k:k + 1, :] * w
                acc = term if acc is None else acc + term
                acc_row(ROW_CAW + k, a0c * w)
            u_s[r0:r0 + ROWS, :] = acc
        dbuf[T:T + HALO_A, :] = dbuf[0:HALO_A, :]
        d_a0 = u_s[...]
        put_dh(0, d_a0 * sg)
        put_dh(1, d_a0 * a_val * sg * (1.0 - sg))

        b_b, b_c, b_h, b_z = hs(3), hs(4), hs(5), hs(6)
        cb = aux_ref[:, GROUP:2 * GROUP]
        sz = _sig(b_z)
        dyb = dmix(1)
        put_dh(3, dyb * cb * (b_z * sz))
        put_dh(6, dyb * b_b * cb * dsilu(b_z, sz))
        ebuf[0:T, :] = dyb * b_b * (b_z * sz)
        a0_s[...] = b_c * b_h
        for r0 in range(0, T, ROWS):
            uc = a0_s[r0:r0 + ROWS, :]
            acc = None
            for k in range(KB):
                off = (KB - 1) - k + r0
                w = ebuf[off:off + ROWS, :]
                term = cbw_ref[k:k + 1, :] * w
                acc = term if acc is None else acc + term
                acc_row(ROW_CBW + k, uc * w)
            u_s[r0:r0 + ROWS, :] = acc
        ebuf[T:T + HALO_B, :] = ebuf[0:HALO_B, :]
        d_u = u_s[...]
        put_dh(4, d_u * b_h)
        put_dh(5, d_u * b_c)

        c_z = hs(8)
        pooled = aux_ref[:, 2 * GROUP:3 * GROUP]
        pooled_b = pooled.astype(BF16)
        q = _dot(pooled_b, pw_ref[...])
        sz = _sig(c_z)
        dyc = dmix(2)
        ps = s256_ref[3:4, :]
        acc_row(3, dyc * q * (c_z * sz))
        put_dh(8, dyc * q * ps * dsilu(c_z, sz))
        d_q = (dyc * ps * (c_z * sz)).astype(BF16)
        pw_acc[...] += _dot_tn(pooled_b, d_q)
        d_pooled = _dot_nt(d_q, pw_ref[...])
        fbuf[0:T, :] = d_pooled / _pool_cnt(tile_idx, T)
        hi_lane = (lax.broadcasted_iota(jnp.int32, (1, 128), 1) // HEAD) == 1
        for r0 in range(0, T, ROWS):
            def win(col, j0, j1):
                s = None
                for j in range(j0, j1):
                    term = fbuf[r0 + j:r0 + j + ROWS, 128 * col:128 * (col + 1)]
                    s = term if s is None else s + term
                return s
            u_s[r0:r0 + ROWS, 0:128] = win(0, 0, 2) + jnp.where(hi_lane, win(0, 2, 4), 0.0)
            u_s[r0:r0 + ROWS, 128:256] = win(1, 0, 8) + jnp.where(hi_lane, win(1, 8, 16), 0.0)
        fbuf[T:T + HALO_C, :] = fbuf[0:HALO_C, :]
        put_dh(7, u_s[...] - d_pooled)

        d_u_, d_v_, d_z_ = hs(9), hs(10), hs(11)
        cen = d_v_ - _rowmean(d_v_)
        rstd_v = lax.rsqrt(_rowmean(cen * cen) + LN_EPS)
        xv = cen * rstd_v
        v = xv * s256_ref[4:5, :] + s256_ref[5:6, :]
        sz = _sig(d_z_)
        dyd = dmix(3)
        d_sp = dyd * d_u_ * (d_z_ * sz)
        grp = _lane_group(GROUP)
        sps, dvs = [], []
        for n in range(T // SGU_BLOCK):
            blk = slice(n * SGU_BLOCK, (n + 1) * SGU_BLOCK)
            vst = _vstack(v[blk, :])
            sps.append(_dot(wm_s[...], vst) + sb_ref[...])
            dspb = d_sp[blk, :]
            dsp_acc[...] += dspb
            dspb16 = dspb.astype(BF16)
            dvst = _dot(wmt_s[...], dspb16)
            dvb = None
            for hh in range(4):
                part = jnp.where(grp == hh, dvst[hh * SGU_BLOCK:(hh + 1) * SGU_BLOCK, :], 0.0)
                dvb = part if dvb is None else dvb + part
            dvs.append(dvb)
            dwc = _dot_nt(dspb16, vst)
            osm_ref[ROW_WC:ROW_WC + SGU_BLOCK, :] += dwc[:, 0:GROUP]
            osm_ref[ROW_WC + SGU_BLOCK:ROW_WC + 2 * SGU_BLOCK, :] += dwc[:, GROUP:2 * GROUP]
        sp = jnp.concatenate(sps, axis=0)
        d_v = jnp.concatenate(dvs, axis=0)
        put_dh(9, dyd * sp * (d_z_ * sz))
        put_dh(11, dyd * d_u_ * sp * dsilu(d_z_, sz))
        acc_row(4, d_v * xv)
        acc_row(5, d_v)
        gd = d_v * s256_ref[4:5, :]
        put_dh(10, rstd_v * (gd - _rowmean(gd) - xv * _rowmean(gd * xv)))

        dx = alpha * dz
        for k in range(N_CHIPS):
            dx = dx + _dot_nt(dhb_ref[:, COLS * k:COLS * (k + 1)], wi_ref[k])
        dx_ref[...] = dx

        @pl.when(i == nt - 1)
        def _():
            r = lax.broadcasted_iota(jnp.int32, (SGU_BLOCK, GROUP), 0) // CHUNK
            c = (lax.broadcasted_iota(jnp.int32, (SGU_BLOCK, GROUP), 1) % SGU_BLOCK) // CHUNK
            for half in range(2):
                rows_ = slice(ROW_WC + half * SGU_BLOCK, ROW_WC + (half + 1) * SGU_BLOCK)
                osm_ref[rows_, :] = jnp.where(c <= r, osm_ref[rows_, :], 0.0)
            osm_ref[ROW_SB:ROW_SB + SGU_BLOCK, 0:128] = _segdot(dsp_acc[...], e4_ref[...])
            for g in range(4):
                osm_ref[ROW_PW:ROW_PW + HEAD, HEAD * g:HEAD * (g + 1)] = (
                    pw_acc[HEAD * g:HEAD * (g + 1), HEAD * g:HEAD * (g + 1)])

    def full(a):
        nd = a.ndim
        return pl.BlockSpec(a.shape, lambda i, _n=nd: (0,) * _n)

    def rows(width):
        return pl.BlockSpec((T, width), lambda i: (nt - 1 - i, 0))

    def acc(shape):
        return pl.BlockSpec(shape, lambda i: (0, 0))

    consts = (wi, caw, cbw, s256, seg, pw, wm, wmt, sb, wo, v1024, e4)
    in_specs = [rows(D_MODEL), rows(D_MODEL), rows(IN_WIDTH), rows(3 * GROUP)] + [full(a) for a in consts]
    out_specs = [rows(D_MODEL), rows(IN_WIDTH), rows(D_MODEL), acc((SM_ROWS, GROUP))]
    out_shape = [jax.ShapeDtypeStruct((S, D_MODEL), F32), jax.ShapeDtypeStruct((S, IN_WIDTH), BF16),
                 jax.ShapeDtypeStruct((S, D_MODEL), BF16), jax.ShapeDtypeStruct((SM_ROWS, GROUP), F32)]
    scratch = [pltpu.VMEM((T + HALO_A, GROUP), F32), pltpu.VMEM((T + HALO_B, GROUP), F32),
               pltpu.VMEM((T + HALO_C, GROUP), F32), pltpu.VMEM((T, GROUP), F32), pltpu.VMEM((T, GROUP), F32),
               pltpu.VMEM((SGU_BLOCK, 4 * SGU_BLOCK), BF16), pltpu.VMEM((4 * SGU_BLOCK, SGU_BLOCK), BF16),
               pltpu.VMEM((SGU_BLOCK, GROUP), F32), pltpu.VMEM((GROUP, GROUP), F32)]
    extra, aliases = (), {}
    if exch is not None:
        extra = tuple(exch)
        r_i, r_o = exch[4], exch[5]
        in_specs += [pl.BlockSpec(memory_space=pltpu.SMEM)] + [ANY] * 5
        out_specs += [ANY] * 3
        out_shape += [jax.ShapeDtypeStruct(r_i.shape, r_i.dtype), jax.ShapeDtypeStruct(r_o.shape, r_o.dtype),
                      jax.ShapeDtypeStruct((N_DEV, SM_ROWS, GROUP), F32)]
        scratch += [pltpu.SemaphoreType.DMA((13,)), pltpu.SemaphoreType.DMA((13,)), pltpu.SemaphoreType.DMA((1,))]
        aliases = {20: 4, 21: 5}
    return pl.pallas_call(
        body, name="bwd_layer" if exch is None else "bwd_layer_exchange",
        grid=(nt,), in_specs=in_specs, out_specs=out_specs, out_shape=out_shape, scratch_shapes=scratch,
        input_output_aliases=aliases,
        compiler_params=_vmem_params(dimension_semantics=("arbitrary",), has_side_effects=exch is not None),
    )(dy, z, h, aux, *consts, *extra)


def _dw_proj(layer, lhs, rhs, slab, slab16, *, lhs_cols, rhs_cols, by_lhs, tk, name):
    S = lhs.shape[0]
    ns = S // tk

    def body(l_ref, a_ref, b_ref, slab_ref, slab16_ref, o_ref, o16_ref):
        del l_ref, slab_ref, slab16_ref

        @pl.when(pl.program_id(1) == 0)
        def _():
            o_ref[...] = jnp.zeros_like(o_ref)
        o_ref[...] += _dot_tn(a_ref[...], b_ref[...])

        @pl.when(pl.program_id(1) == ns - 1)
        def _():
            o16_ref[...] = o_ref[...].astype(BF16)

    if by_lhs:
        a_spec = pl.BlockSpec((tk, lhs_cols), lambda j, s, l: (s, j))
        b_spec = pl.BlockSpec((tk, rhs_cols), lambda j, s, l: (s, 0))
    else:
        a_spec = pl.BlockSpec((tk, lhs_cols), lambda j, s, l: (s, 0))
        b_spec = pl.BlockSpec((tk, rhs_cols), lambda j, s, l: (s, j))
    o_spec = pl.BlockSpec((None, None, lhs_cols, rhs_cols), lambda j, s, l: (l[0], j, 0, 0))
    grid_spec = pltpu.PrefetchScalarGridSpec(
        num_scalar_prefetch=1, grid=(N_CHIPS, ns),
        in_specs=[a_spec, b_spec, ANY, ANY], out_specs=[o_spec, o_spec])
    return pl.pallas_call(
        body, name=name, grid_spec=grid_spec,
        out_shape=[jax.ShapeDtypeStruct(slab.shape, F32), jax.ShapeDtypeStruct(slab.shape, BF16)],
        input_output_aliases={3: 0, 4: 1},
        compiler_params=_vmem_params(dimension_semantics=("arbitrary", "arbitrary")),
    )(layer, lhs, rhs, slab, slab16)


def _adamw(w, g, m, v, *, rows_per_step, name):
    R, C = w.shape
    tr = rows_per_step
    c1 = 1.0 - ADAM_B1 ** ADAM_STEP
    c2 = 1.0 - ADAM_B2 ** ADAM_STEP

    def body(w_ref, g_ref, m_ref, v_ref, d_ref, nm_ref, nv_ref):
        g_ = g_ref[...]
        nm = ADAM_B1 * m_ref[...] + (1.0 - ADAM_B1) * g_
        nv = ADAM_B2 * v_ref[...] + (1.0 - ADAM_B2) * (g_ * g_)
        nm_ref[...] = nm
        nv_ref[...] = nv
        d_ref[...] = -ADAM_LR * ((nm / c1) / (jnp.sqrt(nv / c2) + ADAM_EPS) + ADAM_WD * w_ref[...])

    spec = pl.BlockSpec((tr, C), lambda i: (i, 0))
    return pl.pallas_call(
        body, name=name, grid=(R // tr,),
        in_specs=[spec] * 4, out_specs=[spec] * 3,
        out_shape=[jax.ShapeDtypeStruct((R, C), F32)] * 3,
        compiler_params=_vmem_params(dimension_semantics=("arbitrary",)),
    )(w, g, m, v)


def _gather_weights(wi16, wo16, cw):
    L = wi16.shape[0]
    hi_rows, ho_rows = D_MODEL // 2, GROUP // 2
    n_ici = 2 * L + 1
    n_fwd = 2 * L

    def body(wi_ref, wo_ref, cw_ref, *rest):
        wig = rest[0:L]
        wog = rest[L:2 * L]
        cwg = rest[2 * L]
        send_sems, recv_sems, loc_sems = rest[2 * L + 1:]
        x, y, c = _place()
        me_k = 2 * x + y
        sibling = (x, y, 1 - c)
        chips = _other_chips(x, y)

        def half_i(ref, blk):
            return ref.at[blk, pl.ds(c * hi_rows, hi_rows), :]

        def half_o(ref, blk):
            return ref.at[blk, pl.ds(c * ho_rows, ho_rows), :]

        def other_half_i(ref, blk):
            return ref.at[blk, pl.ds((1 - c) * hi_rows, hi_rows), :]

        def other_half_o(ref, blk):
            return ref.at[blk, pl.ds((1 - c) * ho_rows, ho_rows), :]

        local = []
        for l in range(L):
            local.append(pltpu.make_async_copy(wi_ref.at[l], wig[l].at[me_k], loc_sems.at[2 * l]))
            local.append(pltpu.make_async_copy(wo_ref.at[l], wog[l].at[me_k], loc_sems.at[2 * l + 1]))
        local.append(pltpu.make_async_copy(cw_ref, cwg.at[me_k], loc_sems.at[2 * L]))
        for cp in local:
            cp.start()

        def remote(src, dst, sem, to):
            return pltpu.make_async_remote_copy(src_ref=src, dst_ref=dst, send_sem=send_sems.at[sem],
                                                recv_sem=recv_sems.at[sem], device_id=to, device_id_type=MESH)

        sends = []
        for r, (px, py, _) in enumerate(chips):
            to = (px, py, c)
            for l in range(L):
                sends.append(remote(half_i(wi_ref, l), half_i(wig[l], me_k), r * n_ici + 2 * l, to))
                sends.append(remote(half_o(wo_ref, l), half_o(wog[l], me_k), r * n_ici + 2 * l + 1, to))
            sends.append(remote(cw_ref, cwg.at[me_k], r * n_ici + 2 * L, to))
        for cp in sends:
            cp.start()

        base = 3 * n_ici
        fwds = []
        for r, (px, py, pk) in enumerate(chips):
            for l in range(L):
                remote(half_i(wig[l], pk), half_i(wig[l], pk), r * n_ici + 2 * l, sibling).wait_recv()
                f = remote(half_i(wig[l], pk), half_i(wig[l], pk), base + r * n_fwd + 2 * l, sibling)
                f.start()
                fwds.append(f)
                remote(half_o(wog[l], pk), half_o(wog[l], pk), r * n_ici + 2 * l + 1, sibling).wait_recv()
                f = remote(half_o(wog[l], pk), half_o(wog[l], pk), base + r * n_fwd + 2 * l + 1, sibling)
                f.start()
                fwds.append(f)
            remote(cwg.at[pk], cwg.at[pk], r * n_ici + 2 * L, sibling).wait_recv()
        for r, (px, py, pk) in enumerate(chips):
            for l in range(L):
                remote(other_half_i(wig[l], pk), other_half_i(wig[l], pk), base + r * n_fwd + 2 * l, sibling).wait_recv()
                remote(other_half_o(wog[l], pk), other_half_o(wog[l], pk), base + r * n_fwd + 2 * l + 1, sibling).wait_recv()
        for cp in sends + fwds:
            cp.wait_send()
        for cp in local:
            cp.wait()

    n_sem = 3 * n_ici + 3 * n_fwd
    out_shape = ([jax.ShapeDtypeStruct((N_CHIPS, D_MODEL, COLS), BF16)] * L
                 + [jax.ShapeDtypeStruct((N_CHIPS, GROUP, D_MODEL), BF16)] * L
                 + [jax.ShapeDtypeStruct((N_CHIPS,) + cw.shape, F32)])
    outs = pl.pallas_call(
        body, name="gather_weights",
        in_specs=[ANY, ANY, ANY], out_specs=[ANY] * (2 * L + 1), out_shape=out_shape,
        scratch_shapes=[pltpu.SemaphoreType.DMA((n_sem,)), pltpu.SemaphoreType.DMA((n_sem,)),
                        pltpu.SemaphoreType.DMA((2 * L + 1,))],
        compiler_params=pltpu.CompilerParams(has_side_effects=True),
    )(wi16, wo16, cw)
    return outs[0:L], outs[L:2 * L], outs[2 * L]


def _swap_halves(l_arr, gwi, gwo, ri, ro):
    hi_rows, ho_rows = D_MODEL // 2, GROUP // 2

    def body(l_ref, gwi_ref, gwo_ref, ri_in, ro_in, ri_ref, ro_ref, send_sems, recv_sems):
        del ri_in, ro_in
        x, y, c = _place()
        l = l_ref[0]
        sibling = (x, y, 1 - c)
        cps = [
            pltpu.make_async_remote_copy(src_ref=gwi_ref.at[l, :, pl.ds((1 - c) * hi_rows, hi_rows), :],
                                         dst_ref=ri_ref.at[l], send_sem=send_sems.at[0], recv_sem=recv_sems.at[0],
                                         device_id=sibling, device_id_type=MESH),
            pltpu.make_async_remote_copy(src_ref=gwo_ref.at[l, :, pl.ds((1 - c) * ho_rows, ho_rows), :],
                                         dst_ref=ro_ref.at[l], send_sem=send_sems.at[1], recv_sem=recv_sems.at[1],
                                         device_id=sibling, device_id_type=MESH),
        ]
        for cp in cps:
            cp.start()
        for cp in cps:
            cp.wait()

    return pl.pallas_call(
        body, name="swap_halves",
        in_specs=[pl.BlockSpec(memory_space=pltpu.SMEM), ANY, ANY, ANY, ANY], out_specs=[ANY, ANY],
        out_shape=[jax.ShapeDtypeStruct(ri.shape, ri.dtype), jax.ShapeDtypeStruct(ro.shape, ro.dtype)],
        input_output_aliases={3: 0, 4: 1},
        scratch_shapes=[pltpu.SemaphoreType.DMA((2,)), pltpu.SemaphoreType.DMA((2,))],
        compiler_params=pltpu.CompilerParams(has_side_effects=True),
    )(l_arr, gwi, gwo, ri, ro)


def _add_halves(cl_arr, g, r, p, *, rows, cols, tr, name):
    nb = rows // tr

    def body(cl_ref, g_ref, r_ref, p_in, o_ref):
        del cl_ref, p_in
        o_ref[...] = (g_ref[...] + r_ref[...].astype(F32)).astype(o_ref.dtype)

    grid_spec = pltpu.PrefetchScalarGridSpec(
        num_scalar_prefetch=1, grid=(N_CHIPS, nb),
        in_specs=[pl.BlockSpec((None, None, tr, cols), lambda k, i, cl: (cl[1], k, cl[0] * nb + i, 0)),
                  pl.BlockSpec((None, None, tr, cols), lambda k, i, cl: (cl[1], k, i, 0)), ANY],
        out_specs=pl.BlockSpec((None, None, tr, cols), lambda k, i, cl: (cl[1], k, i, 0)))
    return pl.pallas_call(
        body, name=name, grid_spec=grid_spec,
        out_shape=jax.ShapeDtypeStruct(p.shape, p.dtype),
        input_output_aliases={3: 0},
        compiler_params=_vmem_params(dimension_semantics=("arbitrary",) * 2),
    )(cl_arr, g, r, p)


def _exchange_last(l_arr, p_i, p_o, sm, r_i, r_o):
    def body(l_ref, p_i_ref, p_o_ref, sm_ref, ri_in, ro_in, ri_ref, ro_ref, rsm_ref, send_sems, recv_sems, loc_sem):
        del ri_in, ro_in
        always = l_ref[0] >= 0
        _exchange_comm(always, always, l_ref[0], p_i_ref, p_o_ref, sm_ref, ri_ref, ro_ref, rsm_ref,
                       send_sems, recv_sems, loc_sem)

    return pl.pallas_call(
        body, name="exchange_last",
        in_specs=[pl.BlockSpec(memory_space=pltpu.SMEM)] + [ANY] * 5, out_specs=[ANY] * 3,
        out_shape=[jax.ShapeDtypeStruct(r_i.shape, r_i.dtype), jax.ShapeDtypeStruct(r_o.shape, r_o.dtype),
                   jax.ShapeDtypeStruct((N_DEV, SM_ROWS, GROUP), F32)],
        input_output_aliases={4: 0, 5: 1},
        scratch_shapes=[pltpu.SemaphoreType.DMA((13,)), pltpu.SemaphoreType.DMA((13,)), pltpu.SemaphoreType.DMA((1,))],
        compiler_params=pltpu.CompilerParams(has_side_effects=True),
    )(l_arr, p_i, p_o, sm, r_i, r_o)


def _sum_small(r_sm):
    def body(r_ref, o_ref):
        acc = r_ref[0]
        for d in range(1, N_DEV):
            acc = acc + r_ref[d]
        o_ref[...] = acc

    return pl.pallas_call(
        body, name="sum_small",
        out_shape=jax.ShapeDtypeStruct(r_sm.shape[1:], F32),
        compiler_params=_vmem_params(),
    )(r_sm)


def _sum_chunks(kc_arr, p, r, *, rows, cols, tr, name):
    L = p.shape[0]
    nb = rows // tr

    def body(kc_ref, p_ref, r0_ref, r1_ref, r2_ref, o_ref):
        del kc_ref
        f = lambda ref: ref[...].astype(F32)
        o_ref[...] = ((f(p_ref) + f(r0_ref)) + f(r1_ref)) + f(r2_ref)

    def rspec(j):
        return pl.BlockSpec((None, None, tr, cols), lambda l, i, kc, _j=j: (_j, l, i, 0))

    grid_spec = pltpu.PrefetchScalarGridSpec(
        num_scalar_prefetch=1, grid=(L, nb),
        in_specs=[pl.BlockSpec((None, None, tr, cols), lambda l, i, kc: (l, kc[0], i, 0)), rspec(0), rspec(1), rspec(2)],
        out_specs=pl.BlockSpec((None, tr, cols), lambda l, i, kc: (l, kc[1] * nb + i, 0)))
    return pl.pallas_call(
        body, name=name, grid_spec=grid_spec,
        out_shape=jax.ShapeDtypeStruct((L, 2 * rows, cols), F32),
        compiler_params=_vmem_params(dimension_semantics=("arbitrary",) * 2),
    )(kc_arr, p, r, r, r)


def _share_result(gi, go):
    hi_rows, ho_rows = gi.shape[1] // 2, go.shape[1] // 2

    def body(gi_ref, go_ref, oi_ref, oo_ref, send_sems, recv_sems):
        del gi_ref, go_ref
        x, y, c = _place()
        sibling = (x, y, 1 - c)
        cps = []
        for j, (ref, n) in enumerate(((oi_ref, hi_rows), (oo_ref, ho_rows))):
            mine = ref.at[:, pl.ds(c * n, n), :]
            cps.append(pltpu.make_async_remote_copy(src_ref=mine, dst_ref=mine, send_sem=send_sems.at[j],
                                                    recv_sem=recv_sems.at[j], device_id=sibling, device_id_type=MESH))
        for cp in cps:
            cp.start()
        for j, (ref, n) in enumerate(((oi_ref, hi_rows), (oo_ref, ho_rows))):
            theirs = ref.at[:, pl.ds((1 - c) * n, n), :]
            pltpu.make_async_remote_copy(src_ref=theirs, dst_ref=theirs, send_sem=send_sems.at[j],
                                         recv_sem=recv_sems.at[j], device_id=sibling, device_id_type=MESH).wait_recv()
        for cp in cps:
            cp.wait_send()

    return pl.pallas_call(
        body, name="share_result",
        in_specs=[ANY, ANY], out_specs=[ANY, ANY],
        out_shape=[jax.ShapeDtypeStruct(gi.shape, F32), jax.ShapeDtypeStruct(go.shape, F32)],
        input_output_aliases={0: 0, 1: 1},
        scratch_shapes=[pltpu.SemaphoreType.DMA((2,)), pltpu.SemaphoreType.DMA((2,))],
        compiler_params=pltpu.CompilerParams(has_side_effects=True),
    )(gi, go)


SMALL = ("ln_g", "ln_b", "b_in", "conv_a_w", "conv_a_b", "norm_a_g", "norm_a_b", "conv_b_w", "pool_w", "pool_scale",
         "sgu_ln_g", "sgu_ln_b", "sgu_w", "sgu_bias", "b_out")
WEIGHTS = ("ln_g", "ln_b", "w_in", "b_in", "conv_a_w", "conv_a_b", "norm_a_g", "norm_a_b", "conv_b_w", "pool_w",
           "pool_scale", "sgu_ln_g", "sgu_ln_b", "sgu_w", "sgu_bias", "w_out", "b_out")


def _pad_rows(a, rows):
    return jnp.pad(a, ((0, rows - a.shape[0]), (0, 0)))


def _indicator_consts():
    seg = jnp.where((jnp.arange(GROUP)[:, None] // HEAD) == (jnp.arange(GROUP)[None, :] // HEAD),
                    1.0 / HEAD, 0.0).astype(BF16)
    e4 = ((jnp.arange(GROUP)[:, None] // HEAD) == jnp.arange(128)[None, :]).astype(BF16)
    return seg, e4


def _layer_consts(p, conv_full, l):
    same_head = jnp.eye(4, dtype=F32)[:, None, :, None] > 0
    caw = _pad_rows(conv_full[l, :KA], 32)
    cbw = _pad_rows(conv_full[l, KA:], 8)
    s256 = _pad_rows(jnp.stack([p["conv_a_b"][l], p["norm_a_g"][l], p["norm_a_b"][l], p["pool_scale"][l],
                                p["sgu_ln_g"][l], p["sgu_ln_b"][l]]), 8)
    pw = jnp.where(same_head, p["pool_w"][l][:, :, None, :], 0.0).reshape(GROUP, GROUP).astype(BF16)
    wm = jnp.transpose(p["sgu_w"][l], (1, 0, 2)).reshape(SGU_BLOCK, 4 * SGU_BLOCK)
    wmt = jnp.transpose(p["sgu_w"][l], (0, 2, 1)).reshape(4 * SGU_BLOCK, SGU_BLOCK)
    sb = jnp.repeat(p["sgu_bias"][l].T, HEAD, axis=1)
    v1024 = _pad_rows(jnp.stack([p["b_out"][l], p["ln_g"][l], p["ln_b"][l]]), 8)
    return dict(caw=caw, cbw=cbw, s256=s256, pw=pw, wm=wm, wmt=wmt, sb=sb, v1024=v1024, bin=p["b_in"][l][None, :])


def _unpack_small(sm):
    owc = jnp.concatenate([sm[ROW_WC:ROW_WC + SGU_BLOCK], sm[ROW_WC + SGU_BLOCK:ROW_WC + 2 * SGU_BLOCK]], axis=1)
    return dict(
        conv_a_b=sm[0], norm_a_g=sm[1], norm_a_b=sm[2], pool_scale=sm[3], sgu_ln_g=sm[4], sgu_ln_b=sm[5],
        conv_b_w=sm[ROW_CBW:ROW_CBW + KB], conv_a_w=sm[ROW_CAW:ROW_CAW + KA],
        pool_w=jnp.transpose(sm[ROW_PW:ROW_PW + HEAD].reshape(HEAD, 4, HEAD), (1, 0, 2)),
        ln_g=sm[ROW_LNG:ROW_LNG + 4].reshape(D_MODEL), ln_b=sm[ROW_LNB:ROW_LNB + 4].reshape(D_MODEL),
        b_out=sm[ROW_BOUT:ROW_BOUT + 4].reshape(D_MODEL), b_in=sm[ROW_BIN:ROW_BIN + N_SLICES].reshape(IN_WIDTH),
        sgu_w=jnp.transpose(owc.reshape(SGU_BLOCK, 4, SGU_BLOCK), (1, 0, 2)),
        sgu_bias=sm[ROW_SB:ROW_SB + SGU_BLOCK, 0:4].T)


def _step(p, m, v, x, target, *, tile_f, tile_b, tk):
    L = p["ln_g"].shape[0]
    xi, yi, ci = _place()
    me_k = 2 * xi + yi
    hi_rows, ho_rows = D_MODEL // 2, GROUP // 2

    cw = jnp.concatenate([p["conv_a_w"], p["conv_b_w"]], axis=1).reshape(-1, 128)
    cw_rows = cw.shape[0]
    cw = _pad_rows(cw, 72)
    wi16 = p["w_in"].astype(BF16)
    wo16 = p["w_out"].astype(BF16)
    wig0, wog0, cwg = _gather_weights(wi16[0:1], wo16[0:1], cw)
    cwg = cwg[:, :cw_rows].reshape(N_CHIPS, L, KA + KB, HEAD)
    conv_full = jnp.transpose(cwg, (1, 2, 0, 3)).reshape(L, KA + KB, GROUP)
    seg, e4 = _indicator_consts()
    consts = [_layer_consts(p, conv_full, l) for l in range(L)]

    hcur = x
    saved, wig, wog = [], [wig0[0]], [wog0[0]]
    for l in range(L):
        k = consts[l]
        nxt = (wi16[l + 1], wo16[l + 1]) if l + 1 < L else None
        outs = _fwd_layer(hcur, wig[l], k["bin"], k["caw"], k["cbw"], k["s256"], seg, k["pw"], k["wm"], k["sb"], wog[l],
                          k["v1024"], tile=tile_f, nxt=nxt, target=None if nxt is not None else target)
        y, xb, h, aux, mixb, z = outs[0:6]
        if nxt is not None:
            wig.append(outs[6])
            wog.append(outs[7])
        saved.append((xb, h, aux, mixb, z))
        hcur = y

    dy = hcur
    loss = lax.psum(outs[6][0, 0], ("x", "y", "c"))

    gwi = lax.empty((L, N_CHIPS, D_MODEL, COLS), F32)
    gwo = lax.empty((L, N_CHIPS, GROUP, D_MODEL), F32)
    gwi16 = lax.empty((L, N_CHIPS, D_MODEL, COLS), BF16)
    gwo16 = lax.empty((L, N_CHIPS, GROUP, D_MODEL), BF16)
    ri = lax.empty((L, N_CHIPS, hi_rows, COLS), BF16)
    ro = lax.empty((L, N_CHIPS, ho_rows, D_MODEL), BF16)
    p_i = lax.empty((L, N_CHIPS, hi_rows, COLS), BF16)
    p_o = lax.empty((L, N_CHIPS, ho_rows, D_MODEL), BF16)
    q_i = lax.empty((3, L, hi_rows, COLS), BF16)
    q_o = lax.empty((3, L, ho_rows, D_MODEL), BF16)
    r_sm = [None] * L
    pending = None
    for l in reversed(range(L)):
        k = consts[l]
        xb, h, aux, mixb, z = saved[l]
        exch = None if pending is None else (pending[0], p_i, p_o, pending[1], q_i, q_o)
        outs = _bwd_layer(dy, z, h, aux, wig[l], k["caw"], k["cbw"], k["s256"], seg, k["pw"], k["wm"], k["wmt"],
                          k["sb"], wog[l], k["v1024"], e4, tile=tile_b, exch=exch)
        dy, dhb, dzb, osm = outs[0:4]
        if exch is not None:
            q_i, q_o, r_sm[l + 1] = outs[4:7]
        larr = jnp.full((1,), l, jnp.int32)
        gwi, gwi16 = _dw_proj(larr, xb, dhb, gwi, gwi16, lhs_cols=D_MODEL, rhs_cols=COLS, by_lhs=False, tk=tk,
                              name="dw_in")
        gwo, gwo16 = _dw_proj(larr, mixb, dzb, gwo, gwo16, lhs_cols=GROUP, rhs_cols=D_MODEL, by_lhs=True, tk=tk,
                              name="dw_out")
        ri, ro = _swap_halves(larr, gwi16, gwo16, ri, ro)
        cl_arr = jnp.stack([ci, jnp.int32(l)]).astype(jnp.int32)
        p_i = _add_halves(cl_arr, gwi, ri, p_i, rows=hi_rows, cols=COLS, tr=256, name="add_halves_in")
        p_o = _add_halves(cl_arr, gwo, ro, p_o, rows=ho_rows, cols=D_MODEL, tr=128, name="add_halves_out")
        pending = (larr, osm)
    grad_x = dy
    q_i, q_o, r_sm[0] = _exchange_last(pending[0], p_i, p_o, pending[1], q_i, q_o)

    per_layer = [_unpack_small(_sum_small(r_sm[l])) for l in range(L)]
    grads = {n: jnp.stack([per_layer[l][n] for l in range(L)]) for n in SMALL}
    for n in ("conv_a_w", "conv_b_w"):
        grads[n] = lax.dynamic_slice_in_dim(grads[n], me_k * HEAD, HEAD, axis=2)

    kc_arr = jnp.stack([me_k, ci]).astype(jnp.int32)
    g_i = _sum_chunks(kc_arr, p_i, q_i, rows=hi_rows, cols=COLS, tr=256, name="sum_chunks_in")
    g_o = _sum_chunks(kc_arr, p_o, q_o, rows=ho_rows, cols=D_MODEL, tr=128, name="sum_chunks_out")
    g_i, g_o = _share_result(g_i, g_o)
    grads["w_in"] = g_i
    grads["w_out"] = g_o

    delta, new_m, new_v = {}, {}, {}
    for n in WEIGHTS:
        shp = p[n].shape
        if n in ("w_in", "w_out"):
            two_d = (shp[0] * shp[1], shp[2])
            tr = 512 if n == "w_in" else 256
        else:
            two_d = (-1, shp[-1])
            tr = None
        args = [a.reshape(two_d) for a in (p[n], grads[n], m[n], v[n])]
        d, nm, nv = _adamw(*args, rows_per_step=tr or args[0].shape[0], name="adamw_" + n)
        delta[n], new_m[n], new_v[n] = d.reshape(shp), nm.reshape(shp), nv.reshape(shp)

    return (loss, grad_x[None], *[grads[n] for n in WEIGHTS], *[delta[n] for n in WEIGHTS],
            *[new_m[n] for n in WEIGHTS], *[new_v[n] for n in WEIGHTS])


def kernel(x, ln_g, ln_b, w_in, b_in, conv_a_w, conv_a_b, norm_a_g, norm_a_b, conv_b_w, pool_w, pool_scale, sgu_ln_g, sgu_ln_b, sgu_w, sgu_bias, w_out, b_out, loss_target, m_ln_g, m_ln_b, m_w_in, m_b_in, m_conv_a_w, m_conv_a_b, m_norm_a_g, m_norm_a_b, m_conv_b_w, m_pool_w, m_pool_scale, m_sgu_ln_g, m_sgu_ln_b, m_sgu_w, m_sgu_bias, m_w_out, m_b_out, v_ln_g, v_ln_b, v_w_in, v_b_in, v_conv_a_w, v_conv_a_b, v_norm_a_g, v_norm_a_b, v_conv_b_w, v_pool_w, v_pool_scale, v_sgu_ln_g, v_sgu_ln_b, v_sgu_w, v_sgu_bias, v_w_out, v_b_out):
    p = dict(ln_g=ln_g, ln_b=ln_b, w_in=w_in, b_in=b_in, conv_a_w=conv_a_w, conv_a_b=conv_a_b, norm_a_g=norm_a_g,
             norm_a_b=norm_a_b, conv_b_w=conv_b_w, pool_w=pool_w, pool_scale=pool_scale, sgu_ln_g=sgu_ln_g,
             sgu_ln_b=sgu_ln_b, sgu_w=sgu_w, sgu_bias=sgu_bias, w_out=w_out, b_out=b_out)
    m = dict(ln_g=m_ln_g, ln_b=m_ln_b, w_in=m_w_in, b_in=m_b_in, conv_a_w=m_conv_a_w, conv_a_b=m_conv_a_b,
             norm_a_g=m_norm_a_g, norm_a_b=m_norm_a_b, conv_b_w=m_conv_b_w, pool_w=m_pool_w, pool_scale=m_pool_scale,
             sgu_ln_g=m_sgu_ln_g, sgu_ln_b=m_sgu_ln_b, sgu_w=m_sgu_w, sgu_bias=m_sgu_bias, w_out=m_w_out, b_out=m_b_out)
    v = dict(ln_g=v_ln_g, ln_b=v_ln_b, w_in=v_w_in, b_in=v_b_in, conv_a_w=v_conv_a_w, conv_a_b=v_conv_a_b,
             norm_a_g=v_norm_a_g, norm_a_b=v_norm_a_b, conv_b_w=v_conv_b_w, pool_w=v_pool_w, pool_scale=v_pool_scale,
             sgu_ln_g=v_sgu_ln_g, sgu_ln_b=v_sgu_ln_b, sgu_w=v_sgu_w, sgu_bias=v_sgu_bias, w_out=v_w_out, b_out=v_b_out)
    return _step(p, m, v, x[0], loss_target[0], tile_f=256, tile_b=256, tk=2048)
```

```python
import functools

import jax
import jax.numpy as jnp
from jax import lax
from jax.experimental import pallas as pl
from jax.experimental.pallas import tpu as pltpu

F32 = jnp.float32
BF16 = jnp.bfloat16
MESH = pl.DeviceIdType.MESH

D_MODEL = 1024
GROUP = 256
HEAD = 64
N_SLICES = 12
IN_WIDTH = N_SLICES * GROUP
N_CHIPS = 4
COLS = IN_WIDTH // N_CHIPS
KA = 31
KB = 3
HALO_A, HALO_B, HALO_C = 32, 8, 16
POOL_WINDOWS = (2, 4, 8, 16)
SGU_BLOCK = 128
CHUNK = 64
LN_EPS = 1e-5
ROWS = 64
V7X_VMEM_BYTES = 64 * 1024 * 1024
VMEM_LIMIT = 56 * 1024 * 1024

ADAM_LR, ADAM_B1, ADAM_B2, ADAM_EPS, ADAM_WD, ADAM_STEP = 0.001, 0.9, 0.999, 1e-08, 0.01, 10


ANY = pl.BlockSpec(memory_space=pl.ANY)


def _vmem_params(**kw):
    return pltpu.CompilerParams(vmem_limit_bytes=VMEM_LIMIT, **kw)


def _place():
    return lax.axis_index("x"), lax.axis_index("y"), lax.axis_index("c")


def _other_chips(x, y):
    return [(1 - x, y, 2 * (1 - x) + y), (x, 1 - y, 2 * x + (1 - y)), (1 - x, 1 - y, 2 * (1 - x) + (1 - y))]


def _sig(v):
    return 0.5 * jnp.tanh(0.5 * v) + 0.5


def _dot(a, b):
    return jnp.dot(a, b, preferred_element_type=F32)


def _dot_nt(a, b):
    return lax.dot_general(a, b, (((1,), (1,)), ((), ())), preferred_element_type=F32)


def _dot_tn(a, b):
    return lax.dot_general(a, b, (((0,), (0,)), ((), ())), preferred_element_type=F32)


def _segdot(v, m):
    hi = v.astype(BF16)
    lo = (v - hi.astype(F32)).astype(BF16)
    return _dot(hi, m) + _dot(lo, m)


def _colsum(v):
    return jnp.sum(v, axis=0, keepdims=True)


def _rowmean(v):
    return jnp.mean(v, axis=-1, keepdims=True)


def _lane_group(n):
    return lax.broadcasted_iota(jnp.int32, (1, n), 1) // HEAD


def _pool_cnt(tile, t_rows):
    pos = tile * t_rows + lax.broadcasted_iota(jnp.int32, (t_rows, GROUP), 0) + 1
    grp = lax.broadcasted_iota(jnp.int32, (t_rows, GROUP), 1) // HEAD
    win = jnp.where(grp == 0, 2, jnp.where(grp == 1, 4, jnp.where(grp == 2, 8, 16)))
    return jnp.minimum(pos, win).astype(F32)


def _sgu_masks(wm_ref, wmt_ref, wm_s, wmt_s):
    r = lax.broadcasted_iota(jnp.int32, (SGU_BLOCK, 4 * SGU_BLOCK), 0) // CHUNK
    c = (lax.broadcasted_iota(jnp.int32, (SGU_BLOCK, 4 * SGU_BLOCK), 1) % SGU_BLOCK) // CHUNK
    wm_s[...] = jnp.where(c <= r, wm_ref[...], 0.0).astype(BF16)
    if wmt_ref is not None:
        rt = (lax.broadcasted_iota(jnp.int32, (4 * SGU_BLOCK, SGU_BLOCK), 0) % SGU_BLOCK) // CHUNK
        ct = lax.broadcasted_iota(jnp.int32, (4 * SGU_BLOCK, SGU_BLOCK), 1) // CHUNK
        wmt_s[...] = jnp.where(rt <= ct, wmt_ref[...], 0.0).astype(BF16)


def _vstack(v_blk):
    grp = _lane_group(GROUP)
    return jnp.concatenate([jnp.where(grp == h, v_blk, 0.0) for h in range(4)], axis=0).astype(BF16)


def _gather_next(step, nt, nwi, nwo, gwi, gwo, send_sems, recv_sems, loc_sems):
    x, y, c = _place()
    me_k = 2 * x + y
    sibling = (x, y, 1 - c)
    chips = _other_chips(x, y)
    hi, ho = D_MODEL // 2, GROUP // 2

    def rc(src, dst, sem, to):
        return pltpu.make_async_remote_copy(src_ref=src, dst_ref=dst, send_sem=send_sems.at[sem],
                                            recv_sem=recv_sems.at[sem], device_id=to, device_id_type=MESH)

    def blk(ref, k, n, cc):
        return ref.at[k, pl.ds(cc * n, n), :]

    def ici(r):
        px, py, _ = chips[r]
        to = (px, py, c)
        return [rc(nwi.at[pl.ds(c * hi, hi), :], blk(gwi, me_k, hi, c), 2 * r, to),
                rc(nwo.at[pl.ds(c * ho, ho), :], blk(gwo, me_k, ho, c), 2 * r + 1, to)]

    def landed(r, cc, base):
        pk = chips[r][2]
        return [rc(blk(gwi, pk, hi, cc), blk(gwi, pk, hi, cc), base + 2 * r, sibling),
                rc(blk(gwo, pk, ho, cc), blk(gwo, pk, ho, cc), base + 2 * r + 1, sibling)]

    def local():
        return [pltpu.make_async_copy(nwi, gwi.at[me_k], loc_sems.at[0]),
                pltpu.make_async_copy(nwo, gwo.at[me_k], loc_sems.at[1])]

    @pl.when(step == 0)
    def _():
        for cp in local():
            cp.start()
        for r in range(3):
            for cp in ici(r):
                cp.start()

    @pl.when(step == (3 * nt) // 4)
    def _():
        for r in range(3):
            for got, fwd in zip(landed(r, c, 0), landed(r, c, 6)):
                got.wait_recv()
                fwd.start()

    @pl.when(step == nt - 1)
    def _():
        for r in range(3):
            for got in landed(r, 1 - c, 6):
                got.wait_recv()
        for r in range(3):
            for cp in ici(r) + landed(r, c, 6):
                cp.wait_send()
        for cp in local():
            cp.wait()


def _fwd_layer(x, wi, bin_, caw, cbw, s256, seg, pw, wm, sb, wo, v1024, *, tile, nxt=None, target=None):
    assert nxt is None or target is None
    S = x.shape[0]
    T = tile
    nt = S // T
    alpha = float((2.0 * 4) ** 0.25)
    n_in = 12 + (2 if nxt is not None else 0) + (1 if target is not None else 0)
    n_out = 6 + (2 if nxt is not None else 0) + (1 if target is not None else 0)

    def body(*refs):
        (x_ref, wi_ref, bin_ref, caw_ref, cbw_ref, s256_ref, seg_ref, pw_ref, wm_ref, sb_ref, wo_ref,
         v1024_ref) = refs[0:12]
        y_ref, xbt_ref, h_ref, aux_ref, mixt_ref, z_ref = refs[n_in:n_in + 6]
        abuf, bbuf, cbuf, wm_s, mix_ref = refs[n_in + n_out:n_in + n_out + 5]
        i = pl.program_id(0)
        if nxt is not None:
            _gather_next(i, nt, refs[12], refs[13], refs[n_in + 6], refs[n_in + 7], *refs[n_in + n_out + 5:])

        @pl.when(i == 0)
        def _():
            abuf[0:HALO_A, :] = jnp.zeros((HALO_A, GROUP), F32)
            bbuf[0:HALO_B, :] = jnp.zeros((HALO_B, GROUP), F32)
            cbuf[0:HALO_C, :] = jnp.zeros((HALO_C, GROUP), F32)
            _sgu_masks(wm_ref, None, wm_s, None)

        x = x_ref[...]
        xb = x.astype(BF16)
        xbt_ref[...] = xb.T
        for k in range(N_CHIPS):
            h_ref[:, COLS * k:COLS * (k + 1)] = _dot(xb, wi_ref[k]) + bin_ref[:, COLS * k:COLS * (k + 1)]

        def hs(j):
            return h_ref[:, GROUP * j:GROUP * (j + 1)]

        abuf[HALO_A:HALO_A + T, :] = hs(0) * _sig(hs(1))
        for r0 in range(0, T, ROWS):
            acc = None
            for k in range(KA):
                off = HALO_A - (KA - 1) + k + r0
                term = caw_ref[k:k + 1, :] * abuf[off:off + ROWS, :]
                acc = term if acc is None else acc + term
            aux_ref[r0:r0 + ROWS, 0:GROUP] = acc + s256_ref[0:1, :]
        abuf[0:HALO_A, :] = abuf[T:T + HALO_A, :]
        a1 = aux_ref[:, 0:GROUP]
        segm = seg_ref[...]
        cen = a1 - _segdot(a1, segm)
        var = _segdot(cen * cen, segm)
        a2 = cen * lax.rsqrt(var + LN_EPS) * s256_ref[1:2, :] + s256_ref[2:3, :]
        az = hs(2)
        mix_ref[:, 0:GROUP] = (a2 * _sig(a2) * (az * _sig(az))).astype(BF16)

        bbuf[HALO_B:HALO_B + T, :] = hs(4) * hs(5)
        for r0 in range(0, T, ROWS):
            acc = None
            for k in range(KB):
                off = HALO_B - (KB - 1) + k + r0
                term = cbw_ref[k:k + 1, :] * bbuf[off:off + ROWS, :]
                acc = term if acc is None else acc + term
            aux_ref[r0:r0 + ROWS, GROUP:2 * GROUP] = acc
        bbuf[0:HALO_B, :] = bbuf[T:T + HALO_B, :]
        bz = hs(6)
        mix_ref[:, GROUP:2 * GROUP] = (hs(3) * aux_ref[:, GROUP:2 * GROUP] * (bz * _sig(bz))).astype(BF16)

        ch = hs(7)
        cbuf[HALO_C:HALO_C + T, :] = ch
        hi_lane = (lax.broadcasted_iota(jnp.int32, (1, 128), 1) // HEAD) == 1
        for r0 in range(0, T, ROWS):
            def win(col, j0, j1):
                s = None
                for j in range(j0, j1):
                    off = HALO_C - j + r0
                    term = cbuf[off:off + ROWS, 128 * col:128 * (col + 1)]
                    s = term if s is None else s + term
                return s
            w0 = win(0, 0, 2) + jnp.where(hi_lane, win(0, 2, 4), 0.0)
            w1 = win(1, 0, 8) + jnp.where(hi_lane, win(1, 8, 16), 0.0)
            aux_ref[r0:r0 + ROWS, 2 * GROUP:2 * GROUP + 128] = w0
            aux_ref[r0:r0 + ROWS, 2 * GROUP + 128:3 * GROUP] = w1
        cbuf[0:HALO_C, :] = cbuf[T:T + HALO_C, :]
        pooled = aux_ref[:, 2 * GROUP:3 * GROUP] / _pool_cnt(i, T) - ch
        aux_ref[:, 2 * GROUP:3 * GROUP] = pooled
        q = _dot(pooled.astype(BF16), pw_ref[...])
        cz = hs(8)
        mix_ref[:, 2 * GROUP:3 * GROUP] = (q * s256_ref[3:4, :] * (cz * _sig(cz))).astype(BF16)

        dv = hs(10)
        cen = dv - _rowmean(dv)
        var = _rowmean(cen * cen)
        v = cen * lax.rsqrt(var + LN_EPS) * s256_ref[4:5, :] + s256_ref[5:6, :]
        sps = []
        for n in range(T // SGU_BLOCK):
            vb = v[n * SGU_BLOCK:(n + 1) * SGU_BLOCK, :]
            sps.append(_dot(wm_s[...], _vstack(vb)) + sb_ref[...])
        sp = jnp.concatenate(sps, axis=0)
        dz = hs(11)
        mix_ref[:, 3 * GROUP:4 * GROUP] = (hs(9) * sp * (dz * _sig(dz))).astype(BF16)

        mixt_ref[...] = mix_ref[...].T
        out = v1024_ref[0:1, :]
        for k in range(N_CHIPS):
            out = out + _dot(mix_ref[:, GROUP * k:GROUP * (k + 1)], wo_ref[k])
        z = alpha * x + out
        z_ref[...] = z
        cen = z - _rowmean(z)
        var = _rowmean(cen * cen)
        y = cen * lax.rsqrt(var + LN_EPS) * v1024_ref[1:2, :] + v1024_ref[2:3, :]
        if target is None:
            y_ref[...] = y
        else:
            t_ref, loss_ref = refs[12], refs[n_in + 6]

            @pl.when(i == 0)
            def _():
                loss_ref[...] = jnp.zeros_like(loss_ref)
            err = y - t_ref[...]
            y_ref[...] = err * (1.0 / D_MODEL)
            loss_ref[...] += jnp.sum(_colsum(err * err), axis=1, keepdims=True) * (0.5 / D_MODEL)

    def full(a):
        nd = a.ndim
        return pl.BlockSpec(a.shape, lambda i, _n=nd: (0,) * _n)

    def rows(width):
        return pl.BlockSpec((T, width), lambda i: (i, 0))

    consts = (wi, bin_, caw, cbw, s256, seg, pw, wm, sb, wo, v1024)
    in_specs = [rows(D_MODEL)] + [full(a) for a in consts]
    cols_t = pl.BlockSpec((D_MODEL, T), lambda i: (0, i))
    out_specs = [rows(D_MODEL), cols_t, rows(IN_WIDTH), rows(3 * GROUP), cols_t, rows(D_MODEL)]
    out_shape = [jax.ShapeDtypeStruct((S, D_MODEL), F32), jax.ShapeDtypeStruct((D_MODEL, S), BF16),
                 jax.ShapeDtypeStruct((S, IN_WIDTH), F32), jax.ShapeDtypeStruct((S, 3 * GROUP), F32),
                 jax.ShapeDtypeStruct((D_MODEL, S), BF16), jax.ShapeDtypeStruct((S, D_MODEL), F32)]
    scratch = [pltpu.VMEM((T + HALO_A, GROUP), F32), pltpu.VMEM((T + HALO_B, GROUP), F32),
               pltpu.VMEM((T + HALO_C, GROUP), F32), pltpu.VMEM((SGU_BLOCK, 4 * SGU_BLOCK), BF16),
               pltpu.VMEM((T, D_MODEL), BF16)]
    extra = ()
    if nxt is not None:
        extra = tuple(nxt)
        in_specs += [ANY, ANY]
        out_specs += [ANY, ANY]
        out_shape += [jax.ShapeDtypeStruct((N_CHIPS, D_MODEL, COLS), BF16),
                      jax.ShapeDtypeStruct((N_CHIPS, GROUP, D_MODEL), BF16)]
        scratch += [pltpu.SemaphoreType.DMA((12,)), pltpu.SemaphoreType.DMA((12,)), pltpu.SemaphoreType.DMA((2,))]
    if target is not None:
        extra = (target,)
        in_specs += [rows(D_MODEL)]
        out_specs += [pl.BlockSpec((8, 128), lambda i: (0, 0))]
        out_shape += [jax.ShapeDtypeStruct((8, 128), F32)]
    return pl.pallas_call(
        body, name=("fwd_layer_loss" if target is not None else "fwd_layer") if nxt is None else "fwd_layer_gather",
        grid=(nt,), in_specs=in_specs, out_specs=out_specs, out_shape=out_shape, scratch_shapes=scratch,
        compiler_params=_vmem_params(dimension_semantics=("arbitrary",), has_side_effects=nxt is not None),
    )(x, *consts, *extra)


ROW_CBW = 8
ROW_CAW = 16
ROW_LOSS = 7
ROW_PW = 48
ROW_LNG = 112
ROW_LNB = 116
ROW_BOUT = 120
ROW_BIN = 124
ROW_WC = 136
ROW_SB = 392
SM_ROWS = 520
N_DEV = 8


def _exchange_comm(start, finish, l, p_i, p_o, sm, r_i, r_o, r_sm, send_sems, recv_sems, loc_sem):
    x, y, c = _place()
    me = 4 * x + 2 * y + c
    chips = _other_chips(x, y)

    def rc(src, dst, sem, to):
        return pltpu.make_async_remote_copy(src_ref=src, dst_ref=dst, send_sem=send_sems.at[sem],
                                            recv_sem=recv_sems.at[sem], device_id=to, device_id_type=MESH)

    def big(r):
        px, py, pk = chips[r]
        to = (px, py, c)
        return [rc(p_i.at[l, pk], r_i.at[r, l], 2 * r, to), rc(p_o.at[l, pk], r_o.at[r, l], 2 * r + 1, to)]

    def peer(rel):
        px = 1 - x if rel & 4 else x
        py = 1 - y if rel & 2 else y
        pc = 1 - c if rel & 1 else c
        return (px, py, pc), 4 * px + 2 * py + pc

    def small_out(rel):
        to, _ = peer(rel)
        return rc(sm, r_sm.at[me], 5 + rel, to)

    def small_in(rel):
        to, idx = peer(rel)
        return rc(sm, r_sm.at[idx], 5 + rel, to)

    def local():
        return pltpu.make_async_copy(sm, r_sm.at[me], loc_sem.at[0])

    @pl.when(start)
    def _():
        local().start()
        for r in range(3):
            for cp in big(r):
                cp.start()
        for rel in range(1, N_DEV):
            small_out(rel).start()

    @pl.when(finish)
    def _():
        for r in range(3):
            for cp in big(r):
                cp.wait()
        for rel in range(1, N_DEV):
            small_in(rel).wait_recv()
            small_out(rel).wait_send()
        local().wait()


def _bwd_layer(dy, z, h, aux, wi, caw, cbw, s256, seg, pw, wm, wmt, sb, wo, v1024, e4, *, tile, exch=None):
    S = dy.shape[0]
    T = tile
    nt = S // T
    alpha = float((2.0 * 4) ** 0.25)
    n_in = 16 + (6 if exch is not None else 0)
    n_out = 4 + (3 if exch is not None else 0)

    def body(*refs):
        (dy_ref, z_ref, h_ref, aux_ref, wi_ref, caw_ref, cbw_ref, s256_ref, seg_ref, pw_ref, wm_ref, wmt_ref,
         sb_ref, wo_ref, v1024_ref, e4_ref) = refs[0:16]
        dx_ref, dhb_ref, dzb_ref, osm_ref = refs[n_in:n_in + 4]
        dbuf, ebuf, fbuf, a0_s, u_s, wm_s, wmt_s, dsp_acc, pw_acc = refs[n_in + n_out:n_in + n_out + 9]
        i = pl.program_id(0)
        tile_idx = nt - 1 - i
        if exch is not None:
            l_ref, p_i, p_o, sm = refs[16:20]
            r_i, r_o, r_sm = refs[n_in + 4:n_in + 7]
            _exchange_comm(i == 0, i == nt - 1, l_ref[0], p_i, p_o, sm, r_i, r_o, r_sm, *refs[n_in + n_out + 9:])

        @pl.when(i == 0)
        def _():
            dbuf[T:T + HALO_A, :] = jnp.zeros((HALO_A, GROUP), F32)
            ebuf[T:T + HALO_B, :] = jnp.zeros((HALO_B, GROUP), F32)
            fbuf[T:T + HALO_C, :] = jnp.zeros((HALO_C, GROUP), F32)
            _sgu_masks(wm_ref, wmt_ref, wm_s, wmt_s)
            osm_ref[...] = jnp.zeros_like(osm_ref)
            dsp_acc[...] = jnp.zeros_like(dsp_acc)
            pw_acc[...] = jnp.zeros_like(pw_acc)

        def hs(j):
            return h_ref[:, GROUP * j:GROUP * (j + 1)]

        def acc_row(row, val):
            osm_ref[row:row + 1, :] += _colsum(val)

        def acc_wide(row, val):
            cs = _colsum(val)
            for j in range(D_MODEL // GROUP):
                osm_ref[row + j:row + j + 1, :] += cs[:, GROUP * j:GROUP * (j + 1)]

        def put_dh(j, val):
            acc_row(ROW_BIN + j, val)
            dhb_ref[:, GROUP * j:GROUP * (j + 1)] = val.astype(BF16)

        def dsilu(v, s):
            return s * (1.0 + v * (1.0 - s))

        dy = dy_ref[...]
        z = z_ref[...]
        cen = z - _rowmean(z)
        rstd = lax.rsqrt(_rowmean(cen * cen) + LN_EPS)
        xhat = cen * rstd
        acc_wide(ROW_LNG, dy * xhat)
        acc_wide(ROW_LNB, dy)
        gdy = dy * v1024_ref[1:2, :]
        dz = rstd * (gdy - _rowmean(gdy) - xhat * _rowmean(gdy * xhat))
        acc_wide(ROW_BOUT, dz)
        dzb = dz.astype(BF16)
        dzb_ref[...] = dzb

        def dmix(k):
            return _dot_nt(dzb, wo_ref[k])

        segm = seg_ref[...]

        a_val, a_glu, a_z = hs(0), hs(1), hs(2)
        sg = _sig(a_glu)
        a0_s[...] = a_val * sg
        a1 = aux_ref[:, 0:GROUP]
        cen = a1 - _segdot(a1, segm)
        rstd_a = lax.rsqrt(_segdot(cen * cen, segm) + LN_EPS)
        xh = cen * rstd_a
        a2 = xh * s256_ref[1:2, :] + s256_ref[2:3, :]
        s2 = _sig(a2)
        sz = _sig(a_z)
        dya = dmix(0)
        put_dh(2, dya * (a2 * s2) * dsilu(a_z, sz))
        d_a2 = dya * (a_z * sz) * dsilu(a2, s2)
        acc_row(1, d_a2 * xh)
        acc_row(2, d_a2)
        gd = d_a2 * s256_ref[1:2, :]
        d_a1 = rstd_a * (gd - _segdot(gd, segm) - xh * _segdot(gd * xh, segm))
        acc_row(0, d_a1)
        dbuf[0:T, :] = d_a1
        for r0 in range(0, T, ROWS):
            a0c = a0_s[r0:r0 + ROWS, :]
            acc = None
            for k in range(KA):
                off = (KA - 1) - k + r0
                w = dbuf[off:off + ROWS, :]
                term = caw_ref[k:k + 1, :] * w
                acc = term if acc is None else acc + term
                acc_row(ROW_CAW + k, a0c * w)
            u_s[r0:r0 + ROWS, :] = acc
        dbuf[T:T + HALO_A, :] = dbuf[0:HALO_A, :]
        d_a0 = u_s[...]
        put_dh(0, d_a0 * sg)
        put_dh(1, d_a0 * a_val * sg * (1.0 - sg))

        b_b, b_c, b_h, b_z = hs(3), hs(4), hs(5), hs(6)
        cb = aux_ref[:, GROUP:2 * GROUP]
        sz = _sig(b_z)
        dyb = dmix(1)
        put_dh(3, dyb * cb * (b_z * sz))
        put_dh(6, dyb * b_b * cb * dsilu(b_z, sz))
        ebuf[0:T, :] = dyb * b_b * (b_z * sz)
        a0_s[...] = b_c * b_h
        for r0 in range(0, T, ROWS):
            uc = a0_s[r0:r0 + ROWS, :]
            acc = None
            for k in range(KB):
                off = (KB - 1) - k + r0
                w = ebuf[off:off + ROWS, :]
                term = cbw_ref[k:k + 1, :] * w
                acc = term if acc is None else acc + term
                acc_row(ROW_CBW + k, uc * w)
            u_s[r0:r0 + ROWS, :] = acc
        ebuf[T:T + HALO_B, :] = ebuf[0:HALO_B, :]
        d_u = u_s[...]
        put_dh(4, d_u * b_h)
        put_dh(5, d_u * b_c)

        c_z = hs(8)
        pooled = aux_ref[:, 2 * GROUP:3 * GROUP]
        pooled_b = pooled.astype(BF16)
        q = _dot(pooled_b, pw_ref[...])
        sz = _sig(c_z)
        dyc = dmix(2)
        ps = s256_ref[3:4, :]
        acc_row(3, dyc * q * (c_z * sz))
        put_dh(8, dyc * q * ps * dsilu(c_z, sz))
        d_q = (dyc * ps * (c_z * sz)).astype(BF16)
        pw_acc[...] += _dot_tn(pooled_b, d_q)
        d_pooled = _dot_nt(d_q, pw_ref[...])
        fbuf[0:T, :] = d_pooled / _pool_cnt(tile_idx, T)
        hi_lane = (lax.broadcasted_iota(jnp.int32, (1, 128), 1) // HEAD) == 1
        for r0 in range(0, T, ROWS):
            def win(col, j0, j1):
                s = None
                for j in range(j0, j1):
                    term = fbuf[r0 + j:r0 + j + ROWS, 128 * col:128 * (col + 1)]
                    s = term if s is None else s + term
                return s
            u_s[r0:r0 + ROWS, 0:128] = win(0, 0, 2) + jnp.where(hi_lane, win(0, 2, 4), 0.0)
            u_s[r0:r0 + ROWS, 128:256] = win(1, 0, 8) + jnp.where(hi_lane, win(1, 8, 16), 0.0)
        fbuf[T:T + HALO_C, :] = fbuf[0:HALO_C, :]
        put_dh(7, u_s[...] - d_pooled)

        d_u_, d_v_, d_z_ = hs(9), hs(10), hs(11)
        cen = d_v_ - _rowmean(d_v_)
        rstd_v = lax.rsqrt(_rowmean(cen * cen) + LN_EPS)
        xv = cen * rstd_v
        v = xv * s256_ref[4:5, :] + s256_ref[5:6, :]
        sz = _sig(d_z_)
        dyd = dmix(3)
        d_sp = dyd * d_u_ * (d_z_ * sz)
        grp = _lane_group(GROUP)
        sps, dvs = [], []
        for n in range(T // SGU_BLOCK):
            blk = slice(n * SGU_BLOCK, (n + 1) * SGU_BLOCK)
            vst = _vstack(v[blk, :])
            sps.append(_dot(wm_s[...], vst) + sb_ref[...])
            dspb = d_sp[blk, :]
            dsp_acc[...] += dspb
            dspb16 = dspb.astype(BF16)
            dvst = _dot(wmt_s[...], dspb16)
            dvb = None
            for hh in range(4):
                part = jnp.where(grp == hh, dvst[hh * SGU_BLOCK:(hh + 1) * SGU_BLOCK, :], 0.0)
                dvb = part if dvb is None else dvb + part
            dvs.append(dvb)
            dwc = _dot_nt(dspb16, vst)
            osm_ref[ROW_WC:ROW_WC + SGU_BLOCK, :] += dwc[:, 0:GROUP]
            osm_ref[ROW_WC + SGU_BLOCK:ROW_WC + 2 * SGU_BLOCK, :] += dwc[:, GROUP:2 * GROUP]
        sp = jnp.concatenate(sps, axis=0)
        d_v = jnp.concatenate(dvs, axis=0)
        put_dh(9, dyd * sp * (d_z_ * sz))
        put_dh(11, dyd * d_u_ * sp * dsilu(d_z_, sz))
        acc_row(4, d_v * xv)
        acc_row(5, d_v)
        gd = d_v * s256_ref[4:5, :]
        put_dh(10, rstd_v * (gd - _rowmean(gd) - xv * _rowmean(gd * xv)))

        dx = alpha * dz
        for k in range(N_CHIPS):
            dx = dx + _dot_nt(dhb_ref[:, COLS * k:COLS * (k + 1)], wi_ref[k])
        dx_ref[...] = dx

        @pl.when(i == nt - 1)
        def _():
            r = lax.broadcasted_iota(jnp.int32, (SGU_BLOCK, GROUP), 0) // CHUNK
            c = (lax.broadcasted_iota(jnp.int32, (SGU_BLOCK, GROUP), 1) % SGU_BLOCK) // CHUNK
            for half in range(2):
                rows_ = slice(ROW_WC + half * SGU_BLOCK, ROW_WC + (half + 1) * SGU_BLOCK)
                osm_ref[rows_, :] = jnp.where(c <= r, osm_ref[rows_, :], 0.0)
            osm_ref[ROW_SB:ROW_SB + SGU_BLOCK, 0:128] = _segdot(dsp_acc[...], e4_ref[...])
            for g in range(4):
                osm_ref[ROW_PW:ROW_PW + HEAD, HEAD * g:HEAD * (g + 1)] = (
                    pw_acc[HEAD * g:HEAD * (g + 1), HEAD * g:HEAD * (g + 1)])

    def full(a):
        nd = a.ndim
        return pl.BlockSpec(a.shape, lambda i, _n=nd: (0,) * _n)

    def rows(width):
        return pl.BlockSpec((T, width), lambda i: (nt - 1 - i, 0))

    def acc(shape):
        return pl.BlockSpec(shape, lambda i: (0, 0))

    consts = (wi, caw, cbw, s256, seg, pw, wm, wmt, sb, wo, v1024, e4)
    in_specs = [rows(D_MODEL), rows(D_MODEL), rows(IN_WIDTH), rows(3 * GROUP)] + [full(a) for a in consts]
    out_specs = [rows(D_MODEL), rows(IN_WIDTH), rows(D_MODEL), acc((SM_ROWS, GROUP))]
    out_shape = [jax.ShapeDtypeStruct((S, D_MODEL), F32), jax.ShapeDtypeStruct((S, IN_WIDTH), BF16),
                 jax.ShapeDtypeStruct((S, D_MODEL), BF16), jax.ShapeDtypeStruct((SM_ROWS, GROUP), F32)]
    scratch = [pltpu.VMEM((T + HALO_A, GROUP), F32), pltpu.VMEM((T + HALO_B, GROUP), F32),
               pltpu.VMEM((T + HALO_C, GROUP), F32), pltpu.VMEM((T, GROUP), F32), pltpu.VMEM((T, GROUP), F32),
               pltpu.VMEM((SGU_BLOCK, 4 * SGU_BLOCK), BF16), pltpu.VMEM((4 * SGU_BLOCK, SGU_BLOCK), BF16),
               pltpu.VMEM((SGU_BLOCK, GROUP), F32), pltpu.VMEM((GROUP, GROUP), F32)]
    extra, aliases = (), {}
    if exch is not None:
        extra = tuple(exch)
        r_i, r_o = exch[4], exch[5]
        in_specs += [pl.BlockSpec(memory_space=pltpu.SMEM)] + [ANY] * 5
        out_specs += [ANY] * 3
        out_shape += [jax.ShapeDtypeStruct(r_i.shape, r_i.dtype), jax.ShapeDtypeStruct(r_o.shape, r_o.dtype),
                      jax.ShapeDtypeStruct((N_DEV, SM_ROWS, GROUP), F32)]
        scratch += [pltpu.SemaphoreType.DMA((13,)), pltpu.SemaphoreType.DMA((13,)), pltpu.SemaphoreType.DMA((1,))]
        aliases = {20: 4, 21: 5}
    return pl.pallas_call(
        body, name="bwd_layer" if exch is None else "bwd_layer_exchange",
        grid=(nt,), in_specs=in_specs, out_specs=out_specs, out_shape=out_shape, scratch_shapes=scratch,
        input_output_aliases=aliases,
        compiler_params=_vmem_params(dimension_semantics=("arbitrary",), has_side_effects=exch is not None),
    )(dy, z, h, aux, *consts, *extra)


def _dw_proj(layer, lhs, rhs, slab, slab16, *, lhs_cols, rhs_cols, by_lhs, tk, name):
    S = rhs.shape[0]
    ns = S // tk

    def body(l_ref, a_ref, b_ref, slab_ref, slab16_ref, o_ref, o16_ref):
        del l_ref, slab_ref, slab16_ref

        @pl.when(pl.program_id(1) == 0)
        def _():
            o_ref[...] = jnp.zeros_like(o_ref)
        o_ref[...] += _dot(a_ref[...], b_ref[...])

        @pl.when(pl.program_id(1) == ns - 1)
        def _():
            o16_ref[...] = o_ref[...].astype(BF16)

    if by_lhs:
        a_spec = pl.BlockSpec((lhs_cols, tk), lambda j, s, l: (j, s))
        b_spec = pl.BlockSpec((tk, rhs_cols), lambda j, s, l: (s, 0))
    else:
        a_spec = pl.BlockSpec((lhs_cols, tk), lambda j, s, l: (0, s))
        b_spec = pl.BlockSpec((tk, rhs_cols), lambda j, s, l: (s, j))
    o_spec = pl.BlockSpec((None, None, lhs_cols, rhs_cols), lambda j, s, l: (l[0], j, 0, 0))
    grid_spec = pltpu.PrefetchScalarGridSpec(
        num_scalar_prefetch=1, grid=(N_CHIPS, ns),
        in_specs=[a_spec, b_spec, ANY, ANY], out_specs=[o_spec, o_spec])
    return pl.pallas_call(
        body, name=name, grid_spec=grid_spec,
        out_shape=[jax.ShapeDtypeStruct(slab.shape, F32), jax.ShapeDtypeStruct(slab.shape, BF16)],
        input_output_aliases={3: 0, 4: 1},
        compiler_params=_vmem_params(dimension_semantics=("arbitrary", "arbitrary")),
    )(layer, lhs, rhs, slab, slab16)


def _adamw(w, g, m, v, *, rows_per_step, name, copy_g=False):
    R, C = w.shape
    tr = rows_per_step
    c1 = 1.0 - ADAM_B1 ** ADAM_STEP
    c2 = 1.0 - ADAM_B2 ** ADAM_STEP

    def body(w_ref, g_ref, m_ref, v_ref, d_ref, nm_ref, nv_ref, *g_out):
        g_ = g_ref[...]
        nm = ADAM_B1 * m_ref[...] + (1.0 - ADAM_B1) * g_
        nv = ADAM_B2 * v_ref[...] + (1.0 - ADAM_B2) * (g_ * g_)
        nm_ref[...] = nm
        nv_ref[...] = nv
        d_ref[...] = -ADAM_LR * ((nm / c1) / (jnp.sqrt(nv / c2) + ADAM_EPS) + ADAM_WD * w_ref[...])
        if copy_g:
            g_out[0][...] = g_

    spec = pl.BlockSpec((tr, C), lambda i: (i, 0))
    n_out = 4 if copy_g else 3
    return pl.pallas_call(
        body, name=name, grid=(R // tr,),
        in_specs=[spec] * 4, out_specs=[spec] * n_out,
        out_shape=[jax.ShapeDtypeStruct((R, C), F32)] * n_out,
        compiler_params=_vmem_params(dimension_semantics=("arbitrary",)),
    )(w, g, m, v)


def _gather_weights(wi16, wo16, cw):
    L = wi16.shape[0]
    hi_rows, ho_rows = D_MODEL // 2, GROUP // 2
    n_ici = 2 * L + 1
    n_fwd = 2 * L

    def body(wi_ref, wo_ref, cw_ref, *rest):
        wig = rest[0:L]
        wog = rest[L:2 * L]
        cwg = rest[2 * L]
        send_sems, recv_sems, loc_sems = rest[2 * L + 1:]
        x, y, c = _place()
        me_k = 2 * x + y
        sibling = (x, y, 1 - c)
        chips = _other_chips(x, y)

        def half_i(ref, blk):
            return ref.at[blk, pl.ds(c * hi_rows, hi_rows), :]

        def half_o(ref, blk):
            return ref.at[blk, pl.ds(c * ho_rows, ho_rows), :]

        def other_half_i(ref, blk):
            return ref.at[blk, pl.ds((1 - c) * hi_rows, hi_rows), :]

        def other_half_o(ref, blk):
            return ref.at[blk, pl.ds((1 - c) * ho_rows, ho_rows), :]

        local = []
        for l in range(L):
            local.append(pltpu.make_async_copy(wi_ref.at[l], wig[l].at[me_k], loc_sems.at[2 * l]))
            local.append(pltpu.make_async_copy(wo_ref.at[l], wog[l].at[me_k], loc_sems.at[2 * l + 1]))
        local.append(pltpu.make_async_copy(cw_ref, cwg.at[me_k], loc_sems.at[2 * L]))
        for cp in local:
            cp.start()

        def remote(src, dst, sem, to):
            return pltpu.make_async_remote_copy(src_ref=src, dst_ref=dst, send_sem=send_sems.at[sem],
                                                recv_sem=recv_sems.at[sem], device_id=to, device_id_type=MESH)

        sends = []
        for r, (px, py, _) in enumerate(chips):
            to = (px, py, c)
            for l in range(L):
                sends.append(remote(half_i(wi_ref, l), half_i(wig[l], me_k), r * n_ici + 2 * l, to))
                sends.append(remote(half_o(wo_ref, l), half_o(wog[l], me_k), r * n_ici + 2 * l + 1, to))
            sends.append(remote(cw_ref, cwg.at[me_k], r * n_ici + 2 * L, to))
        for cp in sends:
            cp.start()

        base = 3 * n_ici
        fwds = []
        for r, (px, py, pk) in enumerate(chips):
            for l in range(L):
                remote(half_i(wig[l], pk), half_i(wig[l], pk), r * n_ici + 2 * l, sibling).wait_recv()
                f = remote(half_i(wig[l], pk), half_i(wig[l], pk), base + r * n_fwd + 2 * l, sibling)
                f.start()
                fwds.append(f)
                remote(half_o(wog[l], pk), half_o(wog[l], pk), r * n_ici + 2 * l + 1, sibling).wait_recv()
                f = remote(half_o(wog[l], pk), half_o(wog[l], pk), base + r * n_fwd + 2 * l + 1, sibling)
                f.start()
                fwds.append(f)
            remote(cwg.at[pk], cwg.at[pk], r * n_ici + 2 * L, sibling).wait_recv()
        for r, (px, py, pk) in enumerate(chips):
            for l in range(L):
                remote(other_half_i(wig[l], pk), other_half_i(wig[l], pk), base + r * n_fwd + 2 * l, sibling).wait_recv()
                remote(other_half_o(wog[l], pk), other_half_o(wog[l], pk), base + r * n_fwd + 2 * l + 1, sibling).wait_recv()
        for cp in sends + fwds:
            cp.wait_send()
        for cp in local:
            cp.wait()

    n_sem = 3 * n_ici + 3 * n_fwd
    out_shape = ([jax.ShapeDtypeStruct((N_CHIPS, D_MODEL, COLS), BF16)] * L
                 + [jax.ShapeDtypeStruct((N_CHIPS, GROUP, D_MODEL), BF16)] * L
                 + [jax.ShapeDtypeStruct((N_CHIPS,) + cw.shape, F32)])
    outs = pl.pallas_call(
        body, name="gather_weights",
        in_specs=[ANY, ANY, ANY], out_specs=[ANY] * (2 * L + 1), out_shape=out_shape,
        scratch_shapes=[pltpu.SemaphoreType.DMA((n_sem,)), pltpu.SemaphoreType.DMA((n_sem,)),
                        pltpu.SemaphoreType.DMA((2 * L + 1,))],
        compiler_params=pltpu.CompilerParams(has_side_effects=True),
    )(wi16, wo16, cw)
    return outs[0:L], outs[L:2 * L], outs[2 * L]


def _swap_halves(l_arr, gwi, gwo, ri, ro):
    hi_rows, ho_rows = D_MODEL // 2, GROUP // 2

    def body(l_ref, gwi_ref, gwo_ref, ri_in, ro_in, ri_ref, ro_ref, send_sems, recv_sems):
        del ri_in, ro_in
        x, y, c = _place()
        l = l_ref[0]
        sibling = (x, y, 1 - c)
        cps = [
            pltpu.make_async_remote_copy(src_ref=gwi_ref.at[l, :, pl.ds((1 - c) * hi_rows, hi_rows), :],
                                         dst_ref=ri_ref.at[l], send_sem=send_sems.at[0], recv_sem=recv_sems.at[0],
                                         device_id=sibling, device_id_type=MESH),
            pltpu.make_async_remote_copy(src_ref=gwo_ref.at[l, :, pl.ds((1 - c) * ho_rows, ho_rows), :],
                                         dst_ref=ro_ref.at[l], send_sem=send_sems.at[1], recv_sem=recv_sems.at[1],
                                         device_id=sibling, device_id_type=MESH),
        ]
        for cp in cps:
            cp.start()
        for cp in cps:
            cp.wait()

    return pl.pallas_call(
        body, name="swap_halves",
        in_specs=[pl.BlockSpec(memory_space=pltpu.SMEM), ANY, ANY, ANY, ANY], out_specs=[ANY, ANY],
        out_shape=[jax.ShapeDtypeStruct(ri.shape, ri.dtype), jax.ShapeDtypeStruct(ro.shape, ro.dtype)],
        input_output_aliases={3: 0, 4: 1},
        scratch_shapes=[pltpu.SemaphoreType.DMA((2,)), pltpu.SemaphoreType.DMA((2,))],
        compiler_params=pltpu.CompilerParams(has_side_effects=True),
    )(l_arr, gwi, gwo, ri, ro)


def _add_halves(cl_arr, g, r, p, *, rows, cols, tr, name):
    nb = rows // tr

    def body(cl_ref, g_ref, r_ref, p_in, o_ref):
        del cl_ref, p_in
        o_ref[...] = (g_ref[...] + r_ref[...].astype(F32)).astype(o_ref.dtype)

    grid_spec = pltpu.PrefetchScalarGridSpec(
        num_scalar_prefetch=1, grid=(N_CHIPS, nb),
        in_specs=[pl.BlockSpec((None, None, tr, cols), lambda k, i, cl: (cl[1], k, cl[0] * nb + i, 0)),
                  pl.BlockSpec((None, None, tr, cols), lambda k, i, cl: (cl[1], k, i, 0)), ANY],
        out_specs=pl.BlockSpec((None, None, tr, cols), lambda k, i, cl: (cl[1], k, i, 0)))
    return pl.pallas_call(
        body, name=name, grid_spec=grid_spec,
        out_shape=jax.ShapeDtypeStruct(p.shape, p.dtype),
        input_output_aliases={3: 0},
        compiler_params=_vmem_params(dimension_semantics=("arbitrary",) * 2),
    )(cl_arr, g, r, p)


def _exchange_last(l_arr, p_i, p_o, sm, r_i, r_o):
    def body(l_ref, p_i_ref, p_o_ref, sm_ref, ri_in, ro_in, ri_ref, ro_ref, rsm_ref, send_sems, recv_sems, loc_sem):
        del ri_in, ro_in
        always = l_ref[0] >= 0
        _exchange_comm(always, always, l_ref[0], p_i_ref, p_o_ref, sm_ref, ri_ref, ro_ref, rsm_ref,
                       send_sems, recv_sems, loc_sem)

    return pl.pallas_call(
        body, name="exchange_last",
        in_specs=[pl.BlockSpec(memory_space=pltpu.SMEM)] + [ANY] * 5, out_specs=[ANY] * 3,
        out_shape=[jax.ShapeDtypeStruct(r_i.shape, r_i.dtype), jax.ShapeDtypeStruct(r_o.shape, r_o.dtype),
                   jax.ShapeDtypeStruct((N_DEV, SM_ROWS, GROUP), F32)],
        input_output_aliases={4: 0, 5: 1},
        scratch_shapes=[pltpu.SemaphoreType.DMA((13,)), pltpu.SemaphoreType.DMA((13,)), pltpu.SemaphoreType.DMA((1,))],
        compiler_params=pltpu.CompilerParams(has_side_effects=True),
    )(l_arr, p_i, p_o, sm, r_i, r_o)


def _sum_small(r_sm):
    def body(r_ref, o_ref):
        acc = r_ref[0]
        for d in range(1, N_DEV):
            acc = acc + r_ref[d]
        o_ref[...] = acc

    return pl.pallas_call(
        body, name="sum_small",
        out_shape=jax.ShapeDtypeStruct(r_sm.shape[1:], F32),
        compiler_params=_vmem_params(),
    )(r_sm)


def _sum_chunks(kc_arr, p, r, *, rows, cols, tr, name):
    L = p.shape[0]
    nb = rows // tr

    def body(kc_ref, p_ref, r0_ref, r1_ref, r2_ref, o_ref):
        del kc_ref
        f = lambda ref: ref[...].astype(F32)
        o_ref[...] = ((f(p_ref) + f(r0_ref)) + f(r1_ref)) + f(r2_ref)

    def rspec(j):
        return pl.BlockSpec((None, None, tr, cols), lambda l, i, kc, _j=j: (_j, l, i, 0))

    grid_spec = pltpu.PrefetchScalarGridSpec(
        num_scalar_prefetch=1, grid=(L, nb),
        in_specs=[pl.BlockSpec((None, None, tr, cols), lambda l, i, kc: (l, kc[0], i, 0)), rspec(0), rspec(1), rspec(2)],
        out_specs=pl.BlockSpec((None, tr, cols), lambda l, i, kc: (l, kc[1] * nb + i, 0)))
    return pl.pallas_call(
        body, name=name, grid_spec=grid_spec,
        out_shape=jax.ShapeDtypeStruct((L, 2 * rows, cols), F32),
        compiler_params=_vmem_params(dimension_semantics=("arbitrary",) * 2),
    )(kc_arr, p, r, r, r)


def _share_result(gi, go):
    hi_rows, ho_rows = gi.shape[1] // 2, go.shape[1] // 2

    def body(gi_ref, go_ref, oi_ref, oo_ref, send_sems, recv_sems):
        del gi_ref, go_ref
        x, y, c = _place()
        sibling = (x, y, 1 - c)
        cps = []
        for j, (ref, n) in enumerate(((oi_ref, hi_rows), (oo_ref, ho_rows))):
            mine = ref.at[:, pl.ds(c * n, n), :]
            cps.append(pltpu.make_async_remote_copy(src_ref=mine, dst_ref=mine, send_sem=send_sems.at[j],
                                                    recv_sem=recv_sems.at[j], device_id=sibling, device_id_type=MESH))
        for cp in cps:
            cp.start()
        for j, (ref, n) in enumerate(((oi_ref, hi_rows), (oo_ref, ho_rows))):
            theirs = ref.at[:, pl.ds((1 - c) * n, n), :]
            pltpu.make_async_remote_copy(src_ref=theirs, dst_ref=theirs, send_sem=send_sems.at[j],
                                         recv_sem=recv_sems.at[j], device_id=sibling, device_id_type=MESH).wait_recv()
        for cp in cps:
            cp.wait_send()

    return pl.pallas_call(
        body, name="share_result",
        in_specs=[ANY, ANY], out_specs=[ANY, ANY],
        out_shape=[jax.ShapeDtypeStruct(gi.shape, F32), jax.ShapeDtypeStruct(go.shape, F32)],
        input_output_aliases={0: 0, 1: 1},
        scratch_shapes=[pltpu.SemaphoreType.DMA((2,)), pltpu.SemaphoreType.DMA((2,))],
        compiler_params=pltpu.CompilerParams(has_side_effects=True),
    )(gi, go)


SMALL = ("ln_g", "ln_b", "b_in", "conv_a_w", "conv_a_b", "norm_a_g", "norm_a_b", "conv_b_w", "pool_w", "pool_scale",
         "sgu_ln_g", "sgu_ln_b", "sgu_w", "sgu_bias", "b_out")
WEIGHTS = ("ln_g", "ln_b", "w_in", "b_in", "conv_a_w", "conv_a_b", "norm_a_g", "norm_a_b", "conv_b_w", "pool_w",
           "pool_scale", "sgu_ln_g", "sgu_ln_b", "sgu_w", "sgu_bias", "w_out", "b_out")


def _pad_rows(a, rows):
    return jnp.pad(a, ((0, rows - a.shape[0]), (0, 0)))


def _indicator_consts():
    seg = jnp.where((jnp.arange(GROUP)[:, None] // HEAD) == (jnp.arange(GROUP)[None, :] // HEAD),
                    1.0 / HEAD, 0.0).astype(BF16)
    e4 = ((jnp.arange(GROUP)[:, None] // HEAD) == jnp.arange(128)[None, :]).astype(BF16)
    return seg, e4


def _layer_consts(p, conv_full, l):
    same_head = jnp.eye(4, dtype=F32)[:, None, :, None] > 0
    caw = _pad_rows(conv_full[l, :KA], 32)
    cbw = _pad_rows(conv_full[l, KA:], 8)
    s256 = _pad_rows(jnp.stack([p["conv_a_b"][l], p["norm_a_g"][l], p["norm_a_b"][l], p["pool_scale"][l],
                                p["sgu_ln_g"][l], p["sgu_ln_b"][l]]), 8)
    pw = jnp.where(same_head, p["pool_w"][l][:, :, None, :], 0.0).reshape(GROUP, GROUP).astype(BF16)
    wm = jnp.transpose(p["sgu_w"][l], (1, 0, 2)).reshape(SGU_BLOCK, 4 * SGU_BLOCK)
    wmt = jnp.transpose(p["sgu_w"][l], (0, 2, 1)).reshape(4 * SGU_BLOCK, SGU_BLOCK)
    sb = jnp.repeat(p["sgu_bias"][l].T, HEAD, axis=1)
    v1024 = _pad_rows(jnp.stack([p["b_out"][l], p["ln_g"][l], p["ln_b"][l]]), 8)
    return dict(caw=caw, cbw=cbw, s256=s256, pw=pw, wm=wm, wmt=wmt, sb=sb, v1024=v1024, bin=p["b_in"][l][None, :])


def _unpack_small(sm):
    owc = jnp.concatenate([sm[ROW_WC:ROW_WC + SGU_BLOCK], sm[ROW_WC + SGU_BLOCK:ROW_WC + 2 * SGU_BLOCK]], axis=1)
    return dict(
        conv_a_b=sm[0], norm_a_g=sm[1], norm_a_b=sm[2], pool_scale=sm[3], sgu_ln_g=sm[4], sgu_ln_b=sm[5],
        conv_b_w=sm[ROW_CBW:ROW_CBW + KB], conv_a_w=sm[ROW_CAW:ROW_CAW + KA],
        pool_w=jnp.transpose(sm[ROW_PW:ROW_PW + HEAD].reshape(HEAD, 4, HEAD), (1, 0, 2)),
        ln_g=sm[ROW_LNG:ROW_LNG + 4].reshape(D_MODEL), ln_b=sm[ROW_LNB:ROW_LNB + 4].reshape(D_MODEL),
        b_out=sm[ROW_BOUT:ROW_BOUT + 4].reshape(D_MODEL), b_in=sm[ROW_BIN:ROW_BIN + N_SLICES].reshape(IN_WIDTH),
        sgu_w=jnp.transpose(owc.reshape(SGU_BLOCK, 4, SGU_BLOCK), (1, 0, 2)),
        sgu_bias=sm[ROW_SB:ROW_SB + SGU_BLOCK, 0:4].T)


def _step(p, m, v, x, target, *, tile_f, tile_b, tk):
    L = p["ln_g"].shape[0]
    xi, yi, ci = _place()
    me_k = 2 * xi + yi
    hi_rows, ho_rows = D_MODEL // 2, GROUP // 2

    cw = jnp.concatenate([p["conv_a_w"], p["conv_b_w"]], axis=1).reshape(-1, 128)
    cw_rows = cw.shape[0]
    cw = _pad_rows(cw, 72)
    wi16 = p["w_in"].astype(BF16)
    wo16 = p["w_out"].astype(BF16)
    wig0, wog0, cwg = _gather_weights(wi16[0:1], wo16[0:1], cw)
    cwg = cwg[:, :cw_rows].reshape(N_CHIPS, L, KA + KB, HEAD)
    conv_full = jnp.transpose(cwg, (1, 2, 0, 3)).reshape(L, KA + KB, GROUP)
    seg, e4 = _indicator_consts()
    consts = [_layer_consts(p, conv_full, l) for l in range(L)]

    hcur = x
    saved, wig, wog = [], [wig0[0]], [wog0[0]]
    for l in range(L):
        k = consts[l]
        nxt = (wi16[l + 1], wo16[l + 1]) if l + 1 < L else None
        outs = _fwd_layer(hcur, wig[l], k["bin"], k["caw"], k["cbw"], k["s256"], seg, k["pw"], k["wm"], k["sb"], wog[l],
                          k["v1024"], tile=tile_f, nxt=nxt, target=None if nxt is not None else target)
        y, xb, h, aux, mixb, z = outs[0:6]
        if nxt is not None:
            wig.append(outs[6])
            wog.append(outs[7])
        saved.append((xb, h, aux, mixb, z))
        hcur = y

    dy = hcur
    loss_local = outs[6][0, 0]

    gwi = lax.empty((L, N_CHIPS, D_MODEL, COLS), F32)
    gwo = lax.empty((L, N_CHIPS, GROUP, D_MODEL), F32)
    gwi16 = lax.empty((L, N_CHIPS, D_MODEL, COLS), BF16)
    gwo16 = lax.empty((L, N_CHIPS, GROUP, D_MODEL), BF16)
    ri = lax.empty((L, N_CHIPS, hi_rows, COLS), BF16)
    ro = lax.empty((L, N_CHIPS, ho_rows, D_MODEL), BF16)
    p_i = lax.empty((L, N_CHIPS, hi_rows, COLS), BF16)
    p_o = lax.empty((L, N_CHIPS, ho_rows, D_MODEL), BF16)
    q_i = lax.empty((3, L, hi_rows, COLS), BF16)
    q_o = lax.empty((3, L, ho_rows, D_MODEL), BF16)
    r_sm = [None] * L
    pending = None
    for l in reversed(range(L)):
        k = consts[l]
        xb, h, aux, mixb, z = saved[l]
        exch = None if pending is None else (pending[0], p_i, p_o, pending[1], q_i, q_o)
        outs = _bwd_layer(dy, z, h, aux, wig[l], k["caw"], k["cbw"], k["s256"], seg, k["pw"], k["wm"], k["wmt"],
                          k["sb"], wog[l], k["v1024"], e4, tile=tile_b, exch=exch)
        dy, dhb, dzb, osm = outs[0:4]
        if l == L - 1:
            osm = osm.at[ROW_LOSS, 0].set(loss_local)
        if exch is not None:
            q_i, q_o, r_sm[l + 1] = outs[4:7]
        larr = jnp.full((1,), l, jnp.int32)
        gwi, gwi16 = _dw_proj(larr, xb, dhb, gwi, gwi16, lhs_cols=D_MODEL, rhs_cols=COLS, by_lhs=False, tk=tk,
                              name="dw_in")
        gwo, gwo16 = _dw_proj(larr, mixb, dzb, gwo, gwo16, lhs_cols=GROUP, rhs_cols=D_MODEL, by_lhs=True, tk=tk,
                              name="dw_out")
        ri, ro = _swap_halves(larr, gwi16, gwo16, ri, ro)
        cl_arr = jnp.stack([ci, jnp.int32(l)]).astype(jnp.int32)
        p_i = _add_halves(cl_arr, gwi, ri, p_i, rows=hi_rows, cols=COLS, tr=256, name="add_halves_in")
        p_o = _add_halves(cl_arr, gwo, ro, p_o, rows=ho_rows, cols=D_MODEL, tr=128, name="add_halves_out")
        pending = (larr, osm)
    grad_x = dy
    q_i, q_o, r_sm[0] = _exchange_last(pending[0], p_i, p_o, pending[1], q_i, q_o)

    summed = [_sum_small(r_sm[l]) for l in range(L)]
    loss = summed[L - 1][ROW_LOSS, 0]
    per_layer = [_unpack_small(sm) for sm in summed]
    grads = {n: jnp.stack([per_layer[l][n] for l in range(L)]) for n in SMALL}
    for n in ("conv_a_w", "conv_b_w"):
        grads[n] = lax.dynamic_slice_in_dim(grads[n], me_k * HEAD, HEAD, axis=2)

    kc_arr = jnp.stack([me_k, ci]).astype(jnp.int32)
    g_i = _sum_chunks(kc_arr, p_i, q_i, rows=hi_rows, cols=COLS, tr=256, name="sum_chunks_in")
    g_o = _sum_chunks(kc_arr, p_o, q_o, rows=ho_rows, cols=D_MODEL, tr=128, name="sum_chunks_out")
    g_i, g_o = _share_result(g_i, g_o)
    grads["w_in"] = g_i
    grads["w_out"] = g_o

    delta, new_m, new_v = {}, {}, {}
    for n in WEIGHTS:
        shp = p[n].shape
        if n in ("w_in", "w_out"):
            two_d = (shp[0] * shp[1], shp[2])
            tr = 512 if n == "w_in" else 256
        else:
            two_d = (-1, shp[-1])
            tr = None
        args = [a.reshape(two_d) for a in (p[n], grads[n], m[n], v[n])]
        outs = _adamw(*args, rows_per_step=tr or args[0].shape[0], name="adamw_" + n, copy_g=tr is not None)
        delta[n], new_m[n], new_v[n] = (a.reshape(shp) for a in outs[0:3])
        if tr is not None:
            grads[n] = outs[3].reshape(shp)

    return (loss, grad_x[None], *[grads[n] for n in WEIGHTS], *[delta[n] for n in WEIGHTS],
            *[new_m[n] for n in WEIGHTS], *[new_v[n] for n in WEIGHTS])


def kernel(x, ln_g, ln_b, w_in, b_in, conv_a_w, conv_a_b, norm_a_g, norm_a_b, conv_b_w, pool_w, pool_scale, sgu_ln_g, sgu_ln_b, sgu_w, sgu_bias, w_out, b_out, loss_target, m_ln_g, m_ln_b, m_w_in, m_b_in, m_conv_a_w, m_conv_a_b, m_norm_a_g, m_norm_a_b, m_conv_b_w, m_pool_w, m_pool_scale, m_sgu_ln_g, m_sgu_ln_b, m_sgu_w, m_sgu_bias, m_w_out, m_b_out, v_ln_g, v_ln_b, v_w_in, v_b_in, v_conv_a_w, v_conv_a_b, v_norm_a_g, v_norm_a_b, v_conv_b_w, v_pool_w, v_pool_scale, v_sgu_ln_g, v_sgu_ln_b, v_sgu_w, v_sgu_bias, v_w_out, v_b_out):
    p = dict(ln_g=ln_g, ln_b=ln_b, w_in=w_in, b_in=b_in, conv_a_w=conv_a_w, conv_a_b=conv_a_b, norm_a_g=norm_a_g,
             norm_a_b=norm_a_b, conv_b_w=conv_b_w, pool_w=pool_w, pool_scale=pool_scale, sgu_ln_g=sgu_ln_g,
             sgu_ln_b=sgu_ln_b, sgu_w=sgu_w, sgu_bias=sgu_bias, w_out=w_out, b_out=b_out)
    m = dict(ln_g=m_ln_g, ln_b=m_ln_b, w_in=m_w_in, b_in=m_b_in, conv_a_w=m_conv_a_w, conv_a_b=m_conv_a_b,
             norm_a_g=m_norm_a_g, norm_a_b=m_norm_a_b, conv_b_w=m_conv_b_w, pool_w=m_pool_w, pool_scale=m_pool_scale,
             sgu_ln_g=m_sgu_ln_g, sgu_ln_b=m_sgu_ln_b, sgu_w=m_sgu_w, sgu_bias=m_sgu_bias, w_out=m_w_out, b_out=m_b_out)
    v = dict(ln_g=v_ln_g, ln_b=v_ln_b, w_in=v_w_in, b_in=v_b_in, conv_a_w=v_conv_a_w, conv_a_b=v_conv_a_b,
             norm_a_g=v_norm_a_g, norm_a_b=v_norm_a_b, conv_b_w=v_conv_b_w, pool_w=v_pool_w, pool_scale=v_pool_scale,
             sgu_ln_g=v_sgu_ln_g, sgu_ln_b=v_sgu_ln_b, sgu_w=v_sgu_w, sgu_bias=v_sgu_bias, w_out=v_w_out, b_out=v_b_out)
    return _step(p, m, v, x[0], loss_target[0], tile_f=256, tile_b=256, tk=2048)
```

```python
import functools

import jax
import jax.numpy as jnp
from jax import lax
from jax.experimental import pallas as pl
from jax.experimental.pallas import tpu as pltpu

F32 = jnp.float32
BF16 = jnp.bfloat16
MESH = pl.DeviceIdType.MESH

D_MODEL = 1024
GROUP = 256
HEAD = 64
N_SLICES = 12
IN_WIDTH = N_SLICES * GROUP
N_CHIPS = 4
COLS = IN_WIDTH // N_CHIPS
KA = 31
KB = 3
HALO_A, HALO_B, HALO_C = 32, 8, 16
POOL_WINDOWS = (2, 4, 8, 16)
SGU_BLOCK = 128
CHUNK = 64
LN_EPS = 1e-5
ROWS = 64
V7X_VMEM_BYTES = 64 * 1024 * 1024
VMEM_LIMIT = 56 * 1024 * 1024

ADAM_LR, ADAM_B1, ADAM_B2, ADAM_EPS, ADAM_WD, ADAM_STEP = 0.001, 0.9, 0.999, 1e-08, 0.01, 10


ANY = pl.BlockSpec(memory_space=pl.ANY)


def _vmem_params(**kw):
    return pltpu.CompilerParams(vmem_limit_bytes=VMEM_LIMIT, **kw)


def _place():
    return lax.axis_index("x"), lax.axis_index("y"), lax.axis_index("c")


def _other_chips(x, y):
    return [(1 - x, y, 2 * (1 - x) + y), (x, 1 - y, 2 * x + (1 - y)), (1 - x, 1 - y, 2 * (1 - x) + (1 - y))]


def _sig(v):
    return 0.5 * jnp.tanh(0.5 * v) + 0.5


def _dot(a, b):
    return jnp.dot(a, b, preferred_element_type=F32)


def _dot_nt(a, b):
    return lax.dot_general(a, b, (((1,), (1,)), ((), ())), preferred_element_type=F32)


def _dot_tn(a, b):
    return lax.dot_general(a, b, (((0,), (0,)), ((), ())), preferred_element_type=F32)


def _segdot(v, m):
    hi = v.astype(BF16)
    lo = (v - hi.astype(F32)).astype(BF16)
    return _dot(hi, m) + _dot(lo, m)


def _colsum(v):
    return jnp.sum(v, axis=0, keepdims=True)


def _rowmean(v):
    return jnp.mean(v, axis=-1, keepdims=True)


def _lane_group(n):
    return lax.broadcasted_iota(jnp.int32, (1, n), 1) // HEAD


def _pool_cnt(tile, t_rows):
    pos = tile * t_rows + lax.broadcasted_iota(jnp.int32, (t_rows, GROUP), 0) + 1
    grp = lax.broadcasted_iota(jnp.int32, (t_rows, GROUP), 1) // HEAD
    win = jnp.where(grp == 0, 2, jnp.where(grp == 1, 4, jnp.where(grp == 2, 8, 16)))
    return jnp.minimum(pos, win).astype(F32)


def _sgu_masks(wm_ref, wmt_ref, wm_s, wmt_s):
    r = lax.broadcasted_iota(jnp.int32, (SGU_BLOCK, 4 * SGU_BLOCK), 0) // CHUNK
    c = (lax.broadcasted_iota(jnp.int32, (SGU_BLOCK, 4 * SGU_BLOCK), 1) % SGU_BLOCK) // CHUNK
    wm_s[...] = jnp.where(c <= r, wm_ref[...], 0.0).astype(BF16)
    if wmt_ref is not None:
        rt = (lax.broadcasted_iota(jnp.int32, (4 * SGU_BLOCK, SGU_BLOCK), 0) % SGU_BLOCK) // CHUNK
        ct = lax.broadcasted_iota(jnp.int32, (4 * SGU_BLOCK, SGU_BLOCK), 1) // CHUNK
        wmt_s[...] = jnp.where(rt <= ct, wmt_ref[...], 0.0).astype(BF16)


def _vstack(v_blk):
    grp = _lane_group(GROUP)
    return jnp.concatenate([jnp.where(grp == h, v_blk, 0.0) for h in range(4)], axis=0).astype(BF16)


def _gather_next(step, nt, nwi, nwo, gwi, gwo, send_sems, recv_sems, loc_sems):
    x, y, c = _place()
    me_k = 2 * x + y
    sibling = (x, y, 1 - c)
    chips = _other_chips(x, y)
    hi, ho = D_MODEL // 2, GROUP // 2

    def rc(src, dst, sem, to):
        return pltpu.make_async_remote_copy(src_ref=src, dst_ref=dst, send_sem=send_sems.at[sem],
                                            recv_sem=recv_sems.at[sem], device_id=to, device_id_type=MESH)

    def blk(ref, k, n, cc):
        return ref.at[k, pl.ds(cc * n, n), :]

    def ici(r):
        px, py, _ = chips[r]
        to = (px, py, c)
        return [rc(nwi.at[pl.ds(c * hi, hi), :], blk(gwi, me_k, hi, c), 2 * r, to),
                rc(nwo.at[pl.ds(c * ho, ho), :], blk(gwo, me_k, ho, c), 2 * r + 1, to)]

    def landed(r, cc, base):
        pk = chips[r][2]
        return [rc(blk(gwi, pk, hi, cc), blk(gwi, pk, hi, cc), base + 2 * r, sibling),
                rc(blk(gwo, pk, ho, cc), blk(gwo, pk, ho, cc), base + 2 * r + 1, sibling)]

    def local():
        return [pltpu.make_async_copy(nwi, gwi.at[me_k], loc_sems.at[0]),
                pltpu.make_async_copy(nwo, gwo.at[me_k], loc_sems.at[1])]

    @pl.when(step == 0)
    def _():
        for cp in local():
            cp.start()
        for r in range(3):
            for cp in ici(r):
                cp.start()

    @pl.when(step == (3 * nt) // 4)
    def _():
        for r in range(3):
            for got, fwd in zip(landed(r, c, 0), landed(r, c, 6)):
                got.wait_recv()
                fwd.start()

    @pl.when(step == nt - 1)
    def _():
        for r in range(3):
            for got in landed(r, 1 - c, 6):
                got.wait_recv()
        for r in range(3):
            for cp in ici(r) + landed(r, c, 6):
                cp.wait_send()
        for cp in local():
            cp.wait()


def _fwd_layer(x, wi, bin_, caw, cbw, s256, seg, pw, wm, sb, wo, v1024, *, tile, nxt=None, target=None):
    assert nxt is None or target is None
    S = x.shape[0]
    T = tile
    nt = S // T
    alpha = float((2.0 * 4) ** 0.25)
    n_in = 12 + (2 if nxt is not None else 0) + (1 if target is not None else 0)
    n_out = 6 + (2 if nxt is not None else 0) + (1 if target is not None else 0)

    def body(*refs):
        (x_ref, wi_ref, bin_ref, caw_ref, cbw_ref, s256_ref, seg_ref, pw_ref, wm_ref, sb_ref, wo_ref,
         v1024_ref) = refs[0:12]
        y_ref, xb_ref, h_ref, aux_ref, mix_ref, z_ref = refs[n_in:n_in + 6]
        abuf, bbuf, cbuf, wm_s = refs[n_in + n_out:n_in + n_out + 4]
        i = pl.program_id(0)
        if nxt is not None:
            _gather_next(i, nt, refs[12], refs[13], refs[n_in + 6], refs[n_in + 7], *refs[n_in + n_out + 4:])

        @pl.when(i == 0)
        def _():
            abuf[0:HALO_A, :] = jnp.zeros((HALO_A, GROUP), F32)
            bbuf[0:HALO_B, :] = jnp.zeros((HALO_B, GROUP), F32)
            cbuf[0:HALO_C, :] = jnp.zeros((HALO_C, GROUP), F32)
            _sgu_masks(wm_ref, None, wm_s, None)

        x = x_ref[...]
        xb = x.astype(BF16)
        xb_ref[...] = xb
        for k in range(N_CHIPS):
            h_ref[:, COLS * k:COLS * (k + 1)] = _dot(xb, wi_ref[k]) + bin_ref[:, COLS * k:COLS * (k + 1)]

        def hs(j):
            return h_ref[:, GROUP * j:GROUP * (j + 1)]

        abuf[HALO_A:HALO_A + T, :] = hs(0) * _sig(hs(1))
        for r0 in range(0, T, ROWS):
            acc = None
            for k in range(KA):
                off = HALO_A - (KA - 1) + k + r0
                term = caw_ref[k:k + 1, :] * abuf[off:off + ROWS, :]
                acc = term if acc is None else acc + term
            aux_ref[r0:r0 + ROWS, 0:GROUP] = acc + s256_ref[0:1, :]
        abuf[0:HALO_A, :] = abuf[T:T + HALO_A, :]
        a1 = aux_ref[:, 0:GROUP]
        segm = seg_ref[...]
        cen = a1 - _segdot(a1, segm)
        var = _segdot(cen * cen, segm)
        a2 = cen * lax.rsqrt(var + LN_EPS) * s256_ref[1:2, :] + s256_ref[2:3, :]
        az = hs(2)
        mix_ref[:, 0:GROUP] = (a2 * _sig(a2) * (az * _sig(az))).astype(BF16)

        bbuf[HALO_B:HALO_B + T, :] = hs(4) * hs(5)
        for r0 in range(0, T, ROWS):
            acc = None
            for k in range(KB):
                off = HALO_B - (KB - 1) + k + r0
                term = cbw_ref[k:k + 1, :] * bbuf[off:off + ROWS, :]
                acc = term if acc is None else acc + term
            aux_ref[r0:r0 + ROWS, GROUP:2 * GROUP] = acc
        bbuf[0:HALO_B, :] = bbuf[T:T + HALO_B, :]
        bz = hs(6)
        mix_ref[:, GROUP:2 * GROUP] = (hs(3) * aux_ref[:, GROUP:2 * GROUP] * (bz * _sig(bz))).astype(BF16)

        ch = hs(7)
        cbuf[HALO_C:HALO_C + T, :] = ch
        hi_lane = (lax.broadcasted_iota(jnp.int32, (1, 128), 1) // HEAD) == 1
        for r0 in range(0, T, ROWS):
            def win(col, j0, j1):
                s = None
                for j in range(j0, j1):
                    off = HALO_C - j + r0
                    term = cbuf[off:off + ROWS, 128 * col:128 * (col + 1)]
                    s = term if s is None else s + term
                return s
            w0 = win(0, 0, 2) + jnp.where(hi_lane, win(0, 2, 4), 0.0)
            w1 = win(1, 0, 8) + jnp.where(hi_lane, win(1, 8, 16), 0.0)
            aux_ref[r0:r0 + ROWS, 2 * GROUP:2 * GROUP + 128] = w0
            aux_ref[r0:r0 + ROWS, 2 * GROUP + 128:3 * GROUP] = w1
        cbuf[0:HALO_C, :] = cbuf[T:T + HALO_C, :]
        pooled = aux_ref[:, 2 * GROUP:3 * GROUP] / _pool_cnt(i, T) - ch
        aux_ref[:, 2 * GROUP:3 * GROUP] = pooled
        q = _dot(pooled.astype(BF16), pw_ref[...])
        cz = hs(8)
        mix_ref[:, 2 * GROUP:3 * GROUP] = (q * s256_ref[3:4, :] * (cz * _sig(cz))).astype(BF16)

        dv = hs(10)
        cen = dv - _rowmean(dv)
        var = _rowmean(cen * cen)
        v = cen * lax.rsqrt(var + LN_EPS) * s256_ref[4:5, :] + s256_ref[5:6, :]
        sps = []
        for n in range(T // SGU_BLOCK):
            vb = v[n * SGU_BLOCK:(n + 1) * SGU_BLOCK, :]
            sps.append(_dot(wm_s[...], _vstack(vb)) + sb_ref[...])
        sp = jnp.concatenate(sps, axis=0)
        dz = hs(11)
        mix_ref[:, 3 * GROUP:4 * GROUP] = (hs(9) * sp * (dz * _sig(dz))).astype(BF16)

        out = v1024_ref[0:1, :]
        for k in range(N_CHIPS):
            out = out + _dot(mix_ref[:, GROUP * k:GROUP * (k + 1)], wo_ref[k])
        z = alpha * x + out
        z_ref[...] = z
        cen = z - _rowmean(z)
        var = _rowmean(cen * cen)
        y = cen * lax.rsqrt(var + LN_EPS) * v1024_ref[1:2, :] + v1024_ref[2:3, :]
        if target is None:
            y_ref[...] = y
        else:
            t_ref, loss_ref = refs[12], refs[n_in + 6]

            @pl.when(i == 0)
            def _():
                loss_ref[...] = jnp.zeros_like(loss_ref)
            err = y - t_ref[...]
            y_ref[...] = err * (1.0 / D_MODEL)
            loss_ref[...] += jnp.sum(_colsum(err * err), axis=1, keepdims=True) * (0.5 / D_MODEL)

    def full(a):
        nd = a.ndim
        return pl.BlockSpec(a.shape, lambda i, _n=nd: (0,) * _n)

    def rows(width):
        return pl.BlockSpec((T, width), lambda i: (i, 0))

    consts = (wi, bin_, caw, cbw, s256, seg, pw, wm, sb, wo, v1024)
    in_specs = [rows(D_MODEL)] + [full(a) for a in consts]
    out_specs = [rows(D_MODEL), rows(D_MODEL), rows(IN_WIDTH), rows(3 * GROUP), rows(D_MODEL), rows(D_MODEL)]
    out_shape = [jax.ShapeDtypeStruct((S, D_MODEL), F32), jax.ShapeDtypeStruct((S, D_MODEL), BF16),
                 jax.ShapeDtypeStruct((S, IN_WIDTH), F32), jax.ShapeDtypeStruct((S, 3 * GROUP), F32),
                 jax.ShapeDtypeStruct((S, D_MODEL), BF16), jax.ShapeDtypeStruct((S, D_MODEL), F32)]
    scratch = [pltpu.VMEM((T + HALO_A, GROUP), F32), pltpu.VMEM((T + HALO_B, GROUP), F32),
               pltpu.VMEM((T + HALO_C, GROUP), F32), pltpu.VMEM((SGU_BLOCK, 4 * SGU_BLOCK), BF16)]
    extra = ()
    if nxt is not None:
        extra = tuple(nxt)
        in_specs += [ANY, ANY]
        out_specs += [ANY, ANY]
        out_shape += [jax.ShapeDtypeStruct((N_CHIPS, D_MODEL, COLS), BF16),
                      jax.ShapeDtypeStruct((N_CHIPS, GROUP, D_MODEL), BF16)]
        scratch += [pltpu.SemaphoreType.DMA((12,)), pltpu.SemaphoreType.DMA((12,)), pltpu.SemaphoreType.DMA((2,))]
    if target is not None:
        extra = (target,)
        in_specs += [rows(D_MODEL)]
        out_specs += [pl.BlockSpec((8, 128), lambda i: (0, 0))]
        out_shape += [jax.ShapeDtypeStruct((8, 128), F32)]
    return pl.pallas_call(
        body, name=("fwd_layer_loss" if target is not None else "fwd_layer") if nxt is None else "fwd_layer_gather",
        grid=(nt,), in_specs=in_specs, out_specs=out_specs, out_shape=out_shape, scratch_shapes=scratch,
        compiler_params=_vmem_params(dimension_semantics=("arbitrary",), has_side_effects=nxt is not None),
    )(x, *consts, *extra)


ROW_CBW = 8
ROW_CAW = 16
ROW_LOSS = 7
ROW_PW = 48
ROW_LNG = 112
ROW_LNB = 116
ROW_BOUT = 120
ROW_BIN = 124
ROW_WC = 136
ROW_SB = 392
SM_ROWS = 520
N_DEV = 8


def _exchange_comm(start, finish, l, p_i, p_o, sm, r_i, r_o, r_sm, send_sems, recv_sems, loc_sem):
    x, y, c = _place()
    me = 4 * x + 2 * y + c
    chips = _other_chips(x, y)

    def rc(src, dst, sem, to):
        return pltpu.make_async_remote_copy(src_ref=src, dst_ref=dst, send_sem=send_sems.at[sem],
                                            recv_sem=recv_sems.at[sem], device_id=to, device_id_type=MESH)

    def big(r):
        px, py, pk = chips[r]
        to = (px, py, c)
        return [rc(p_i.at[l, pk], r_i.at[r, l], 2 * r, to), rc(p_o.at[l, pk], r_o.at[r, l], 2 * r + 1, to)]

    def peer(rel):
        px = 1 - x if rel & 4 else x
        py = 1 - y if rel & 2 else y
        pc = 1 - c if rel & 1 else c
        return (px, py, pc), 4 * px + 2 * py + pc

    def small_out(rel):
        to, _ = peer(rel)
        return rc(sm, r_sm.at[me], 5 + rel, to)

    def small_in(rel):
        to, idx = peer(rel)
        return rc(sm, r_sm.at[idx], 5 + rel, to)

    def local():
        return pltpu.make_async_copy(sm, r_sm.at[me], loc_sem.at[0])

    @pl.when(start)
    def _():
        local().start()
        for r in range(3):
            for cp in big(r):
                cp.start()
        for rel in range(1, N_DEV):
            small_out(rel).start()

    @pl.when(finish)
    def _():
        for r in range(3):
            for cp in big(r):
                cp.wait()
        for rel in range(1, N_DEV):
            small_in(rel).wait_recv()
            small_out(rel).wait_send()
        local().wait()


def _bwd_layer(dy, z, h, aux, wi, caw, cbw, s256, seg, pw, wm, wmt, sb, wo, v1024, e4, *, tile, exch=None):
    S = dy.shape[0]
    T = tile
    nt = S // T
    alpha = float((2.0 * 4) ** 0.25)
    n_in = 16 + (6 if exch is not None else 0)
    n_out = 4 + (3 if exch is not None else 0)

    def body(*refs):
        (dy_ref, z_ref, h_ref, aux_ref, wi_ref, caw_ref, cbw_ref, s256_ref, seg_ref, pw_ref, wm_ref, wmt_ref,
         sb_ref, wo_ref, v1024_ref, e4_ref) = refs[0:16]
        dx_ref, dhb_ref, dzb_ref, osm_ref = refs[n_in:n_in + 4]
        dbuf, ebuf, fbuf, a0_s, u_s, wm_s, wmt_s, dsp_acc, pw_acc = refs[n_in + n_out:n_in + n_out + 9]
        i = pl.program_id(0)
        tile_idx = nt - 1 - i
        if exch is not None:
            l_ref, p_i, p_o, sm = refs[16:20]
            r_i, r_o, r_sm = refs[n_in + 4:n_in + 7]
            _exchange_comm(i == 0, i == nt - 1, l_ref[0], p_i, p_o, sm, r_i, r_o, r_sm, *refs[n_in + n_out + 9:])

        @pl.when(i == 0)
        def _():
            dbuf[T:T + HALO_A, :] = jnp.zeros((HALO_A, GROUP), F32)
            ebuf[T:T + HALO_B, :] = jnp.zeros((HALO_B, GROUP), F32)
            fbuf[T:T + HALO_C, :] = jnp.zeros((HALO_C, GROUP), F32)
            _sgu_masks(wm_ref, wmt_ref, wm_s, wmt_s)
            osm_ref[...] = jnp.zeros_like(osm_ref)
            dsp_acc[...] = jnp.zeros_like(dsp_acc)
            pw_acc[...] = jnp.zeros_like(pw_acc)

        def hs(j):
            return h_ref[:, GROUP * j:GROUP * (j + 1)]

        def acc_row(row, val):
            osm_ref[row:row + 1, :] += _colsum(val)

        def acc_wide(row, val):
            cs = _colsum(val)
            for j in range(D_MODEL // GROUP):
                osm_ref[row + j:row + j + 1, :] += cs[:, GROUP * j:GROUP * (j + 1)]

        def put_dh(j, val):
            acc_row(ROW_BIN + j, val)
            dhb_ref[:, GROUP * j:GROUP * (j + 1)] = val.astype(BF16)

        def dsilu(v, s):
            return s * (1.0 + v * (1.0 - s))

        dy = dy_ref[...]
        z = z_ref[...]
        cen = z - _rowmean(z)
        rstd = lax.rsqrt(_rowmean(cen * cen) + LN_EPS)
        xhat = cen * rstd
        acc_wide(ROW_LNG, dy * xhat)
        acc_wide(ROW_LNB, dy)
        gdy = dy * v1024_ref[1:2, :]
        dz = rstd * (gdy - _rowmean(gdy) - xhat * _rowmean(gdy * xhat))
        acc_wide(ROW_BOUT, dz)
        dzb = dz.astype(BF16)
        dzb_ref[...] = dzb

        def dmix(k):
            return _dot_nt(dzb, wo_ref[k])

        segm = seg_ref[...]

        a_val, a_glu, a_z = hs(0), hs(1), hs(2)
        sg = _sig(a_glu)
        a0_s[...] = a_val * sg
        a1 = aux_ref[:, 0:GROUP]
        cen = a1 - _segdot(a1, segm)
        rstd_a = lax.rsqrt(_segdot(cen * cen, segm) + LN_EPS)
        xh = cen * rstd_a
        a2 = xh * s256_ref[1:2, :] + s256_ref[2:3, :]
        s2 = _sig(a2)
        sz = _sig(a_z)
        dya = dmix(0)
        put_dh(2, dya * (a2 * s2) * dsilu(a_z, sz))
        d_a2 = dya * (a_z * sz) * dsilu(a2, s2)
        acc_row(1, d_a2 * xh)
        acc_row(2, d_a2)
        gd = d_a2 * s256_ref[1:2, :]
        d_a1 = rstd_a * (gd - _segdot(gd, segm) - xh * _segdot(gd * xh, segm))
        acc_row(0, d_a1)
        dbuf[0:T, :] = d_a1
        for r0 in range(0, T, ROWS):
            a0c = a0_s[r0:r0 + ROWS, :]
            acc = None
            for k in range(KA):
                off = (KA - 1) - k + r0
                w = dbuf[off:off + ROWS, :]
                term = caw_ref[k:k + 1, :] * w
                acc = term if acc is None else acc + term
                acc_row(ROW_CAW + k, a0c * w)
            u_s[r0:r0 + ROWS, :] = acc
        dbuf[T:T + HALO_A, :] = dbuf[0:HALO_A, :]
        d_a0 = u_s[...]
        put_dh(0, d_a0 * sg)
        put_dh(1, d_a0 * a_val * sg * (1.0 - sg))

        b_b, b_c, b_h, b_z = hs(3), hs(4), hs(5), hs(6)
        cb = aux_ref[:, GROUP:2 * GROUP]
        sz = _sig(b_z)
        dyb = dmix(1)
        put_dh(3, dyb * cb * (b_z * sz))
        put_dh(6, dyb * b_b * cb * dsilu(b_z, sz))
        ebuf[0:T, :] = dyb * b_b * (b_z * sz)
        a0_s[...] = b_c * b_h
        for r0 in range(0, T, ROWS):
            uc = a0_s[r0:r0 + ROWS, :]
            acc = None
            for k in range(KB):
                off = (KB - 1) - k + r0
                w = ebuf[off:off + ROWS, :]
                term = cbw_ref[k:k + 1, :] * w
                acc = term if acc is None else acc + term
                acc_row(ROW_CBW + k, uc * w)
            u_s[r0:r0 + ROWS, :] = acc
        ebuf[T:T + HALO_B, :] = ebuf[0:HALO_B, :]
        d_u = u_s[...]
        put_dh(4, d_u * b_h)
        put_dh(5, d_u * b_c)

        c_z = hs(8)
        pooled = aux_ref[:, 2 * GROUP:3 * GROUP]
        pooled_b = pooled.astype(BF16)
        q = _dot(pooled_b, pw_ref[...])
        sz = _sig(c_z)
        dyc = dmix(2)
        ps = s256_ref[3:4, :]
        acc_row(3, dyc * q * (c_z * sz))
        put_dh(8, dyc * q * ps * dsilu(c_z, sz))
        d_q = (dyc * ps * (c_z * sz)).astype(BF16)
        pw_acc[...] += _dot_tn(pooled_b, d_q)
        d_pooled = _dot_nt(d_q, pw_ref[...])
        fbuf[0:T, :] = d_pooled / _pool_cnt(tile_idx, T)
        hi_lane = (lax.broadcasted_iota(jnp.int32, (1, 128), 1) // HEAD) == 1
        for r0 in range(0, T, ROWS):
            def win(col, j0, j1):
                s = None
                for j in range(j0, j1):
                    term = fbuf[r0 + j:r0 + j + ROWS, 128 * col:128 * (col + 1)]
                    s = term if s is None else s + term
                return s
            u_s[r0:r0 + ROWS, 0:128] = win(0, 0, 2) + jnp.where(hi_lane, win(0, 2, 4), 0.0)
            u_s[r0:r0 + ROWS, 128:256] = win(1, 0, 8) + jnp.where(hi_lane, win(1, 8, 16), 0.0)
        fbuf[T:T + HALO_C, :] = fbuf[0:HALO_C, :]
        put_dh(7, u_s[...] - d_pooled)

        d_u_, d_v_, d_z_ = hs(9), hs(10), hs(11)
        cen = d_v_ - _rowmean(d_v_)
        rstd_v = lax.rsqrt(_rowmean(cen * cen) + LN_EPS)
        xv = cen * rstd_v
        v = xv * s256_ref[4:5, :] + s256_ref[5:6, :]
        sz = _sig(d_z_)
        dyd = dmix(3)
        d_sp = dyd * d_u_ * (d_z_ * sz)
        grp = _lane_group(GROUP)
        sps, dvs = [], []
        for n in range(T // SGU_BLOCK):
            blk = slice(n * SGU_BLOCK, (n + 1) * SGU_BLOCK)
            vst = _vstack(v[blk, :])
            sps.append(_dot(wm_s[...], vst) + sb_ref[...])
            dspb = d_sp[blk, :]
            dsp_acc[...] += dspb
            dspb16 = dspb.astype(BF16)
            dvst = _dot(wmt_s[...], dspb16)
            dvb = None
            for hh in range(4):
                part = jnp.where(grp == hh, dvst[hh * SGU_BLOCK:(hh + 1) * SGU_BLOCK, :], 0.0)
                dvb = part if dvb is None else dvb + part
            dvs.append(dvb)
            dwc = _dot_nt(dspb16, vst)
            osm_ref[ROW_WC:ROW_WC + SGU_BLOCK, :] += dwc[:, 0:GROUP]
            osm_ref[ROW_WC + SGU_BLOCK:ROW_WC + 2 * SGU_BLOCK, :] += dwc[:, GROUP:2 * GROUP]
        sp = jnp.concatenate(sps, axis=0)
        d_v = jnp.concatenate(dvs, axis=0)
        put_dh(9, dyd * sp * (d_z_ * sz))
        put_dh(11, dyd * d_u_ * sp * dsilu(d_z_, sz))
        acc_row(4, d_v * xv)
        acc_row(5, d_v)
        gd = d_v * s256_ref[4:5, :]
        put_dh(10, rstd_v * (gd - _rowmean(gd) - xv * _rowmean(gd * xv)))

        dx = alpha * dz
        for k in range(N_CHIPS):
            dx = dx + _dot_nt(dhb_ref[:, COLS * k:COLS * (k + 1)], wi_ref[k])
        dx_ref[...] = dx

        @pl.when(i == nt - 1)
        def _():
            r = lax.broadcasted_iota(jnp.int32, (SGU_BLOCK, GROUP), 0) // CHUNK
            c = (lax.broadcasted_iota(jnp.int32, (SGU_BLOCK, GROUP), 1) % SGU_BLOCK) // CHUNK
            for half in range(2):
                rows_ = slice(ROW_WC + half * SGU_BLOCK, ROW_WC + (half + 1) * SGU_BLOCK)
                osm_ref[rows_, :] = jnp.where(c <= r, osm_ref[rows_, :], 0.0)
            osm_ref[ROW_SB:ROW_SB + SGU_BLOCK, 0:128] = _segdot(dsp_acc[...], e4_ref[...])
            for g in range(4):
                osm_ref[ROW_PW:ROW_PW + HEAD, HEAD * g:HEAD * (g + 1)] = (
                    pw_acc[HEAD * g:HEAD * (g + 1), HEAD * g:HEAD * (g + 1)])

    def full(a):
        nd = a.ndim
        return pl.BlockSpec(a.shape, lambda i, _n=nd: (0,) * _n)

    def rows(width):
        return pl.BlockSpec((T, width), lambda i: (nt - 1 - i, 0))

    def acc(shape):
        return pl.BlockSpec(shape, lambda i: (0, 0))

    consts = (wi, caw, cbw, s256, seg, pw, wm, wmt, sb, wo, v1024, e4)
    in_specs = [rows(D_MODEL), rows(D_MODEL), rows(IN_WIDTH), rows(3 * GROUP)] + [full(a) for a in consts]
    out_specs = [rows(D_MODEL), rows(IN_WIDTH), rows(D_MODEL), acc((SM_ROWS, GROUP))]
    out_shape = [jax.ShapeDtypeStruct((S, D_MODEL), F32), jax.ShapeDtypeStruct((S, IN_WIDTH), BF16),
                 jax.ShapeDtypeStruct((S, D_MODEL), BF16), jax.ShapeDtypeStruct((SM_ROWS, GROUP), F32)]
    scratch = [pltpu.VMEM((T + HALO_A, GROUP), F32), pltpu.VMEM((T + HALO_B, GROUP), F32),
               pltpu.VMEM((T + HALO_C, GROUP), F32), pltpu.VMEM((T, GROUP), F32), pltpu.VMEM((T, GROUP), F32),
               pltpu.VMEM((SGU_BLOCK, 4 * SGU_BLOCK), BF16), pltpu.VMEM((4 * SGU_BLOCK, SGU_BLOCK), BF16),
               pltpu.VMEM((SGU_BLOCK, GROUP), F32), pltpu.VMEM((GROUP, GROUP), F32)]
    extra, aliases = (), {}
    if exch is not None:
        extra = tuple(exch)
        r_i, r_o = exch[4], exch[5]
        in_specs += [pl.BlockSpec(memory_space=pltpu.SMEM)] + [ANY] * 5
        out_specs += [ANY] * 3
        out_shape += [jax.ShapeDtypeStruct(r_i.shape, r_i.dtype), jax.ShapeDtypeStruct(r_o.shape, r_o.dtype),
                      jax.ShapeDtypeStruct((N_DEV, SM_ROWS, GROUP), F32)]
        scratch += [pltpu.SemaphoreType.DMA((13,)), pltpu.SemaphoreType.DMA((13,)), pltpu.SemaphoreType.DMA((1,))]
        aliases = {20: 4, 21: 5}
    return pl.pallas_call(
        body, name="bwd_layer" if exch is None else "bwd_layer_exchange",
        grid=(nt,), in_specs=in_specs, out_specs=out_specs, out_shape=out_shape, scratch_shapes=scratch,
        input_output_aliases=aliases,
        compiler_params=_vmem_params(dimension_semantics=("arbitrary",), has_side_effects=exch is not None),
    )(dy, z, h, aux, *consts, *extra)


def _dw_in(layer, xb, dhb, slab, slab16, *, tk):
    S = xb.shape[0]
    ns = S // tk

    def body(l_ref, a_ref, b_ref, slab_ref, slab16_ref, o_ref, o16_ref):
        del l_ref, slab_ref, slab16_ref

        @pl.when(pl.program_id(1) == 0)
        def _():
            o_ref[...] = jnp.zeros_like(o_ref)
        o_ref[...] += _dot_tn(a_ref[...], b_ref[...])

        @pl.when(pl.program_id(1) == ns - 1)
        def _():
            o16_ref[...] = o_ref[...].astype(BF16)

    o_spec = pl.BlockSpec((None, None, D_MODEL, COLS), lambda j, s, l: (l[0], j, 0, 0))
    grid_spec = pltpu.PrefetchScalarGridSpec(
        num_scalar_prefetch=1, grid=(N_CHIPS, ns),
        in_specs=[pl.BlockSpec((tk, D_MODEL), lambda j, s, l: (s, 0)), pl.BlockSpec((tk, COLS), lambda j, s, l: (s, j)),
                  ANY, ANY],
        out_specs=[o_spec, o_spec])
    return pl.pallas_call(
        body, name="dw_in", grid_spec=grid_spec,
        out_shape=[jax.ShapeDtypeStruct(slab.shape, F32), jax.ShapeDtypeStruct(slab.shape, BF16)],
        input_output_aliases={3: 0, 4: 1},
        compiler_params=_vmem_params(dimension_semantics=("arbitrary", "arbitrary")),
    )(layer, xb, dhb, slab, slab16)


def _dw_out(layer, mixb, dzb, slab, slab16, *, tk):
    S = mixb.shape[0]
    ns = S // tk

    def body(l_ref, a_ref, b_ref, slab_ref, slab16_ref, o_ref, o16_ref):
        del l_ref, slab_ref, slab16_ref

        @pl.when(pl.program_id(0) == 0)
        def _():
            o_ref[...] = jnp.zeros_like(o_ref)
        o_ref[...] += _dot_tn(a_ref[...], b_ref[...]).reshape(N_CHIPS, GROUP, D_MODEL)

        @pl.when(pl.program_id(0) == ns - 1)
        def _():
            o16_ref[...] = o_ref[...].astype(BF16)

    o_spec = pl.BlockSpec((None, N_CHIPS, GROUP, D_MODEL), lambda s, l: (l[0], 0, 0, 0))
    grid_spec = pltpu.PrefetchScalarGridSpec(
        num_scalar_prefetch=1, grid=(ns,),
        in_specs=[pl.BlockSpec((tk, D_MODEL), lambda s, l: (s, 0)), pl.BlockSpec((tk, D_MODEL), lambda s, l: (s, 0)),
                  ANY, ANY],
        out_specs=[o_spec, o_spec])
    return pl.pallas_call(
        body, name="dw_out", grid_spec=grid_spec,
        out_shape=[jax.ShapeDtypeStruct(slab.shape, F32), jax.ShapeDtypeStruct(slab.shape, BF16)],
        input_output_aliases={3: 0, 4: 1},
        compiler_params=_vmem_params(dimension_semantics=("arbitrary",)),
    )(layer, mixb, dzb, slab, slab16)


def _adamw(w, g, m, v, *, rows_per_step, name, copy_g=False):
    R, C = w.shape
    tr = rows_per_step
    c1 = 1.0 - ADAM_B1 ** ADAM_STEP
    c2 = 1.0 - ADAM_B2 ** ADAM_STEP

    def body(w_ref, g_ref, m_ref, v_ref, d_ref, nm_ref, nv_ref, *g_out):
        g_ = g_ref[...]
        nm = ADAM_B1 * m_ref[...] + (1.0 - ADAM_B1) * g_
        nv = ADAM_B2 * v_ref[...] + (1.0 - ADAM_B2) * (g_ * g_)
        nm_ref[...] = nm
        nv_ref[...] = nv
        d_ref[...] = -ADAM_LR * ((nm / c1) / (jnp.sqrt(nv / c2) + ADAM_EPS) + ADAM_WD * w_ref[...])
        if copy_g:
            g_out[0][...] = g_

    spec = pl.BlockSpec((tr, C), lambda i: (i, 0))
    n_out = 4 if copy_g else 3
    return pl.pallas_call(
        body, name=name, grid=(R // tr,),
        in_specs=[spec] * 4, out_specs=[spec] * n_out,
        out_shape=[jax.ShapeDtypeStruct((R, C), F32)] * n_out,
        compiler_params=_vmem_params(dimension_semantics=("arbitrary",)),
    )(w, g, m, v)


def _gather_weights(wi16, wo16, cw):
    L = wi16.shape[0]
    hi_rows, ho_rows = D_MODEL // 2, GROUP // 2
    n_ici = 2 * L + 1
    n_fwd = 2 * L

    def body(wi_ref, wo_ref, cw_ref, *rest):
        wig = rest[0:L]
        wog = rest[L:2 * L]
        cwg = rest[2 * L]
        send_sems, recv_sems, loc_sems = rest[2 * L + 1:]
        x, y, c = _place()
        me_k = 2 * x + y
        sibling = (x, y, 1 - c)
        chips = _other_chips(x, y)

        def half_i(ref, blk):
            return ref.at[blk, pl.ds(c * hi_rows, hi_rows), :]

        def half_o(ref, blk):
            return ref.at[blk, pl.ds(c * ho_rows, ho_rows), :]

        def other_half_i(ref, blk):
            return ref.at[blk, pl.ds((1 - c) * hi_rows, hi_rows), :]

        def other_half_o(ref, blk):
            return ref.at[blk, pl.ds((1 - c) * ho_rows, ho_rows), :]

        local = []
        for l in range(L):
            local.append(pltpu.make_async_copy(wi_ref.at[l], wig[l].at[me_k], loc_sems.at[2 * l]))
            local.append(pltpu.make_async_copy(wo_ref.at[l], wog[l].at[me_k], loc_sems.at[2 * l + 1]))
        local.append(pltpu.make_async_copy(cw_ref, cwg.at[me_k], loc_sems.at[2 * L]))
        for cp in local:
            cp.start()

        def remote(src, dst, sem, to):
            return pltpu.make_async_remote_copy(src_ref=src, dst_ref=dst, send_sem=send_sems.at[sem],
                                                recv_sem=recv_sems.at[sem], device_id=to, device_id_type=MESH)

        sends = []
        for r, (px, py, _) in enumerate(chips):
            to = (px, py, c)
            for l in range(L):
                sends.append(remote(half_i(wi_ref, l), half_i(wig[l], me_k), r * n_ici + 2 * l, to))
                sends.append(remote(half_o(wo_ref, l), half_o(wog[l], me_k), r * n_ici + 2 * l + 1, to))
            sends.append(remote(cw_ref, cwg.at[me_k], r * n_ici + 2 * L, to))
        for cp in sends:
            cp.start()

        base = 3 * n_ici
        fwds = []
        for r, (px, py, pk) in enumerate(chips):
            for l in range(L):
                remote(half_i(wig[l], pk), half_i(wig[l], pk), r * n_ici + 2 * l, sibling).wait_recv()
                f = remote(half_i(wig[l], pk), half_i(wig[l], pk), base + r * n_fwd + 2 * l, sibling)
                f.start()
                fwds.append(f)
                remote(half_o(wog[l], pk), half_o(wog[l], pk), r * n_ici + 2 * l + 1, sibling).wait_recv()
                f = remote(half_o(wog[l], pk), half_o(wog[l], pk), base + r * n_fwd + 2 * l + 1, sibling)
                f.start()
                fwds.append(f)
            remote(cwg.at[pk], cwg.at[pk], r * n_ici + 2 * L, sibling).wait_recv()
        for r, (px, py, pk) in enumerate(chips):
            for l in range(L):
                remote(other_half_i(wig[l], pk), other_half_i(wig[l], pk), base + r * n_fwd + 2 * l, sibling).wait_recv()
                remote(other_half_o(wog[l], pk), other_half_o(wog[l], pk), base + r * n_fwd + 2 * l + 1, sibling).wait_recv()
        for cp in sends + fwds:
            cp.wait_send()
        for cp in local:
            cp.wait()

    n_sem = 3 * n_ici + 3 * n_fwd
    out_shape = ([jax.ShapeDtypeStruct((N_CHIPS, D_MODEL, COLS), BF16)] * L
                 + [jax.ShapeDtypeStruct((N_CHIPS, GROUP, D_MODEL), BF16)] * L
                 + [jax.ShapeDtypeStruct((N_CHIPS,) + cw.shape, F32)])
    outs = pl.pallas_call(
        body, name="gather_weights",
        in_specs=[ANY, ANY, ANY], out_specs=[ANY] * (2 * L + 1), out_shape=out_shape,
        scratch_shapes=[pltpu.SemaphoreType.DMA((n_sem,)), pltpu.SemaphoreType.DMA((n_sem,)),
                        pltpu.SemaphoreType.DMA((2 * L + 1,))],
        compiler_params=pltpu.CompilerParams(has_side_effects=True),
    )(wi16, wo16, cw)
    return outs[0:L], outs[L:2 * L], outs[2 * L]


def _swap_halves(l_arr, gwi, gwo, ri, ro):
    hi_rows, ho_rows = D_MODEL // 2, GROUP // 2

    def body(l_ref, gwi_ref, gwo_ref, ri_in, ro_in, ri_ref, ro_ref, send_sems, recv_sems):
        del ri_in, ro_in
        x, y, c = _place()
        l = l_ref[0]
        sibling = (x, y, 1 - c)
        cps = [
            pltpu.make_async_remote_copy(src_ref=gwi_ref.at[l, :, pl.ds((1 - c) * hi_rows, hi_rows), :],
                                         dst_ref=ri_ref.at[l], send_sem=send_sems.at[0], recv_sem=recv_sems.at[0],
                                         device_id=sibling, device_id_type=MESH),
            pltpu.make_async_remote_copy(src_ref=gwo_ref.at[l, :, pl.ds((1 - c) * ho_rows, ho_rows), :],
                                         dst_ref=ro_ref.at[l], send_sem=send_sems.at[1], recv_sem=recv_sems.at[1],
                                         device_id=sibling, device_id_type=MESH),
        ]
        for cp in cps:
            cp.start()
        for cp in cps:
            cp.wait()

    return pl.pallas_call(
        body, name="swap_halves",
        in_specs=[pl.BlockSpec(memory_space=pltpu.SMEM), ANY, ANY, ANY, ANY], out_specs=[ANY, ANY],
        out_shape=[jax.ShapeDtypeStruct(ri.shape, ri.dtype), jax.ShapeDtypeStruct(ro.shape, ro.dtype)],
        input_output_aliases={3: 0, 4: 1},
        scratch_shapes=[pltpu.SemaphoreType.DMA((2,)), pltpu.SemaphoreType.DMA((2,))],
        compiler_params=pltpu.CompilerParams(has_side_effects=True),
    )(l_arr, gwi, gwo, ri, ro)


def _add_halves(cl_arr, g, r, p, *, rows, cols, tr, name):
    nb = rows // tr

    def body(cl_ref, g_ref, r_ref, p_in, o_ref):
        del cl_ref, p_in
        o_ref[...] = (g_ref[...] + r_ref[...].astype(F32)).astype(o_ref.dtype)

    grid_spec = pltpu.PrefetchScalarGridSpec(
        num_scalar_prefetch=1, grid=(N_CHIPS, nb),
        in_specs=[pl.BlockSpec((None, None, tr, cols), lambda k, i, cl: (cl[1], k, cl[0] * nb + i, 0)),
                  pl.BlockSpec((None, None, tr, cols), lambda k, i, cl: (cl[1], k, i, 0)), ANY],
        out_specs=pl.BlockSpec((None, None, tr, cols), lambda k, i, cl: (cl[1], k, i, 0)))
    return pl.pallas_call(
        body, name=name, grid_spec=grid_spec,
        out_shape=jax.ShapeDtypeStruct(p.shape, p.dtype),
        input_output_aliases={3: 0},
        compiler_params=_vmem_params(dimension_semantics=("arbitrary",) * 2),
    )(cl_arr, g, r, p)


def _exchange_last(l_arr, p_i, p_o, sm, r_i, r_o):
    def body(l_ref, p_i_ref, p_o_ref, sm_ref, ri_in, ro_in, ri_ref, ro_ref, rsm_ref, send_sems, recv_sems, loc_sem):
        del ri_in, ro_in
        always = l_ref[0] >= 0
        _exchange_comm(always, always, l_ref[0], p_i_ref, p_o_ref, sm_ref, ri_ref, ro_ref, rsm_ref,
                       send_sems, recv_sems, loc_sem)

    return pl.pallas_call(
        body, name="exchange_last",
        in_specs=[pl.BlockSpec(memory_space=pltpu.SMEM)] + [ANY] * 5, out_specs=[ANY] * 3,
        out_shape=[jax.ShapeDtypeStruct(r_i.shape, r_i.dtype), jax.ShapeDtypeStruct(r_o.shape, r_o.dtype),
                   jax.ShapeDtypeStruct((N_DEV, SM_ROWS, GROUP), F32)],
        input_output_aliases={4: 0, 5: 1},
        scratch_shapes=[pltpu.SemaphoreType.DMA((13,)), pltpu.SemaphoreType.DMA((13,)), pltpu.SemaphoreType.DMA((1,))],
        compiler_params=pltpu.CompilerParams(has_side_effects=True),
    )(l_arr, p_i, p_o, sm, r_i, r_o)


def _sum_small(r_sms):
    L = len(r_sms)

    def body(*refs):
        o_ref = refs[L]
        for l in range(L):
            acc = refs[l][0]
            for d in range(1, N_DEV):
                acc = acc + refs[l][d]
            o_ref[l] = acc

    return pl.pallas_call(
        body, name="sum_small",
        out_shape=jax.ShapeDtypeStruct((L,) + r_sms[0].shape[1:], F32),
        compiler_params=_vmem_params(),
    )(*r_sms)


def _sum_chunks(kc_arr, p, r, *, rows, cols, tr, name):
    L = p.shape[0]
    nb = rows // tr

    def body(kc_ref, p_ref, r0_ref, r1_ref, r2_ref, o_ref):
        del kc_ref
        f = lambda ref: ref[...].astype(F32)
        o_ref[...] = ((f(p_ref) + f(r0_ref)) + f(r1_ref)) + f(r2_ref)

    def rspec(j):
        return pl.BlockSpec((None, None, tr, cols), lambda l, i, kc, _j=j: (_j, l, i, 0))

    grid_spec = pltpu.PrefetchScalarGridSpec(
        num_scalar_prefetch=1, grid=(L, nb),
        in_specs=[pl.BlockSpec((None, None, tr, cols), lambda l, i, kc: (l, kc[0], i, 0)), rspec(0), rspec(1), rspec(2)],
        out_specs=pl.BlockSpec((None, tr, cols), lambda l, i, kc: (l, kc[1] * nb + i, 0)))
    return pl.pallas_call(
        body, name=name, grid_spec=grid_spec,
        out_shape=jax.ShapeDtypeStruct((L, 2 * rows, cols), F32),
        compiler_params=_vmem_params(dimension_semantics=("arbitrary",) * 2),
    )(kc_arr, p, r, r, r)


def _share_result(gi, go):
    hi_rows, ho_rows = gi.shape[1] // 2, go.shape[1] // 2

    def body(gi_ref, go_ref, oi_ref, oo_ref, send_sems, recv_sems):
        del gi_ref, go_ref
        x, y, c = _place()
        sibling = (x, y, 1 - c)
        cps = []
        for j, (ref, n) in enumerate(((oi_ref, hi_rows), (oo_ref, ho_rows))):
            mine = ref.at[:, pl.ds(c * n, n), :]
            cps.append(pltpu.make_async_remote_copy(src_ref=mine, dst_ref=mine, send_sem=send_sems.at[j],
                                                    recv_sem=recv_sems.at[j], device_id=sibling, device_id_type=MESH))
        for cp in cps:
            cp.start()
        for j, (ref, n) in enumerate(((oi_ref, hi_rows), (oo_ref, ho_rows))):
            theirs = ref.at[:, pl.ds((1 - c) * n, n), :]
            pltpu.make_async_remote_copy(src_ref=theirs, dst_ref=theirs, send_sem=send_sems.at[j],
                                         recv_sem=recv_sems.at[j], device_id=sibling, device_id_type=MESH).wait_recv()
        for cp in cps:
            cp.wait_send()

    return pl.pallas_call(
        body, name="share_result",
        in_specs=[ANY, ANY], out_specs=[ANY, ANY],
        out_shape=[jax.ShapeDtypeStruct(gi.shape, F32), jax.ShapeDtypeStruct(go.shape, F32)],
        input_output_aliases={0: 0, 1: 1},
        scratch_shapes=[pltpu.SemaphoreType.DMA((2,)), pltpu.SemaphoreType.DMA((2,))],
        compiler_params=pltpu.CompilerParams(has_side_effects=True),
    )(gi, go)


WEIGHTS = ("ln_g", "ln_b", "w_in", "b_in", "conv_a_w", "conv_a_b", "norm_a_g", "norm_a_b", "conv_b_w", "pool_w",
           "pool_scale", "sgu_ln_g", "sgu_ln_b", "sgu_w", "sgu_bias", "w_out", "b_out")


def _pad_rows(a, rows):
    return jnp.pad(a, ((0, rows - a.shape[0]), (0, 0)))


def _indicator_consts():
    seg = jnp.where((jnp.arange(GROUP)[:, None] // HEAD) == (jnp.arange(GROUP)[None, :] // HEAD),
                    1.0 / HEAD, 0.0).astype(BF16)
    e4 = ((jnp.arange(GROUP)[:, None] // HEAD) == jnp.arange(128)[None, :]).astype(BF16)
    return seg, e4


def _layer_consts(p, conv_full, l):
    same_head = jnp.eye(4, dtype=F32)[:, None, :, None] > 0
    caw = _pad_rows(conv_full[l, :KA], 32)
    cbw = _pad_rows(conv_full[l, KA:], 8)
    s256 = _pad_rows(jnp.stack([p["conv_a_b"][l], p["norm_a_g"][l], p["norm_a_b"][l], p["pool_scale"][l],
                                p["sgu_ln_g"][l], p["sgu_ln_b"][l]]), 8)
    pw = jnp.where(same_head, p["pool_w"][l][:, :, None, :], 0.0).reshape(GROUP, GROUP).astype(BF16)
    wm = jnp.transpose(p["sgu_w"][l], (1, 0, 2)).reshape(SGU_BLOCK, 4 * SGU_BLOCK)
    wmt = jnp.transpose(p["sgu_w"][l], (0, 2, 1)).reshape(4 * SGU_BLOCK, SGU_BLOCK)
    sb = jnp.repeat(p["sgu_bias"][l].T, HEAD, axis=1)
    v1024 = _pad_rows(jnp.stack([p["b_out"][l], p["ln_g"][l], p["ln_b"][l]]), 8)
    return dict(caw=caw, cbw=cbw, s256=s256, pw=pw, wm=wm, wmt=wmt, sb=sb, v1024=v1024, bin=p["b_in"][l][None, :])


def _unpack_small(sm):
    L = sm.shape[0]
    owc = jnp.concatenate([sm[:, ROW_WC:ROW_WC + SGU_BLOCK], sm[:, ROW_WC + SGU_BLOCK:ROW_WC + 2 * SGU_BLOCK]], axis=2)
    return dict(
        conv_a_b=sm[:, 0], norm_a_g=sm[:, 1], norm_a_b=sm[:, 2], pool_scale=sm[:, 3], sgu_ln_g=sm[:, 4],
        sgu_ln_b=sm[:, 5], conv_b_w=sm[:, ROW_CBW:ROW_CBW + KB], conv_a_w=sm[:, ROW_CAW:ROW_CAW + KA],
        pool_w=jnp.transpose(sm[:, ROW_PW:ROW_PW + HEAD].reshape(L, HEAD, 4, HEAD), (0, 2, 1, 3)),
        ln_g=sm[:, ROW_LNG:ROW_LNG + 4].reshape(L, D_MODEL), ln_b=sm[:, ROW_LNB:ROW_LNB + 4].reshape(L, D_MODEL),
        b_out=sm[:, ROW_BOUT:ROW_BOUT + 4].reshape(L, D_MODEL),
        b_in=sm[:, ROW_BIN:ROW_BIN + N_SLICES].reshape(L, IN_WIDTH),
        sgu_w=jnp.transpose(owc.reshape(L, SGU_BLOCK, 4, SGU_BLOCK), (0, 2, 1, 3)),
        sgu_bias=jnp.transpose(sm[:, ROW_SB:ROW_SB + SGU_BLOCK, 0:4], (0, 2, 1)))


def _step(p, m, v, x, target, *, tile_f, tile_b, tk):
    L = p["ln_g"].shape[0]
    xi, yi, ci = _place()
    me_k = 2 * xi + yi
    hi_rows, ho_rows = D_MODEL // 2, GROUP // 2

    cw = jnp.concatenate([p["conv_a_w"], p["conv_b_w"]], axis=1).reshape(-1, 128)
    cw_rows = cw.shape[0]
    cw = _pad_rows(cw, 72)
    wi16 = p["w_in"].astype(BF16)
    wo16 = p["w_out"].astype(BF16)
    wig0, wog0, cwg = _gather_weights(wi16[0:1], wo16[0:1], cw)
    cwg = cwg[:, :cw_rows].reshape(N_CHIPS, L, KA + KB, HEAD)
    conv_full = jnp.transpose(cwg, (1, 2, 0, 3)).reshape(L, KA + KB, GROUP)
    seg, e4 = _indicator_consts()
    consts = [_layer_consts(p, conv_full, l) for l in range(L)]

    hcur = x
    saved, wig, wog = [], [wig0[0]], [wog0[0]]
    for l in range(L):
        k = consts[l]
        nxt = (wi16[l + 1], wo16[l + 1]) if l + 1 < L else None
        outs = _fwd_layer(hcur, wig[l], k["bin"], k["caw"], k["cbw"], k["s256"], seg, k["pw"], k["wm"], k["sb"], wog[l],
                          k["v1024"], tile=tile_f, nxt=nxt, target=None if nxt is not None else target)
        y, xb, h, aux, mixb, z = outs[0:6]
        if nxt is not None:
            wig.append(outs[6])
            wog.append(outs[7])
        saved.append((xb, h, aux, mixb, z))
        hcur = y

    dy = hcur
    loss_local = outs[6][0, 0]

    gwi = lax.empty((L, N_CHIPS, D_MODEL, COLS), F32)
    gwo = lax.empty((L, N_CHIPS, GROUP, D_MODEL), F32)
    gwi16 = lax.empty((L, N_CHIPS, D_MODEL, COLS), BF16)
    gwo16 = lax.empty((L, N_CHIPS, GROUP, D_MODEL), BF16)
    ri = lax.empty((L, N_CHIPS, hi_rows, COLS), BF16)
    ro = lax.empty((L, N_CHIPS, ho_rows, D_MODEL), BF16)
    p_i = lax.empty((L, N_CHIPS, hi_rows, COLS), BF16)
    p_o = lax.empty((L, N_CHIPS, ho_rows, D_MODEL), BF16)
    q_i = lax.empty((3, L, hi_rows, COLS), BF16)
    q_o = lax.empty((3, L, ho_rows, D_MODEL), BF16)
    r_sm = [None] * L
    pending = None
    for l in reversed(range(L)):
        k = consts[l]
        xb, h, aux, mixb, z = saved[l]
        exch = None if pending is None else (pending[0], p_i, p_o, pending[1], q_i, q_o)
        outs = _bwd_layer(dy, z, h, aux, wig[l], k["caw"], k["cbw"], k["s256"], seg, k["pw"], k["wm"], k["wmt"],
                          k["sb"], wog[l], k["v1024"], e4, tile=tile_b, exch=exch)
        dy, dhb, dzb, osm = outs[0:4]
        if l == L - 1:
            osm = osm.at[ROW_LOSS, 0].set(loss_local)
        if exch is not None:
            q_i, q_o, r_sm[l + 1] = outs[4:7]
        larr = jnp.full((1,), l, jnp.int32)
        gwi, gwi16 = _dw_in(larr, xb, dhb, gwi, gwi16, tk=tk)
        gwo, gwo16 = _dw_out(larr, mixb, dzb, gwo, gwo16, tk=tk)
        ri, ro = _swap_halves(larr, gwi16, gwo16, ri, ro)
        cl_arr = jnp.stack([ci, jnp.int32(l)]).astype(jnp.int32)
        p_i = _add_halves(cl_arr, gwi, ri, p_i, rows=hi_rows, cols=COLS, tr=256, name="add_halves_in")
        p_o = _add_halves(cl_arr, gwo, ro, p_o, rows=ho_rows, cols=D_MODEL, tr=128, name="add_halves_out")
        pending = (larr, osm)
    grad_x = dy
    q_i, q_o, r_sm[0] = _exchange_last(pending[0], p_i, p_o, pending[1], q_i, q_o)

    summed = _sum_small(r_sm)
    loss = summed[L - 1, ROW_LOSS, 0]
    grads = _unpack_small(summed)
    for n in ("conv_a_w", "conv_b_w"):
        grads[n] = lax.dynamic_slice_in_dim(grads[n], me_k * HEAD, HEAD, axis=2)

    kc_arr = jnp.stack([me_k, ci]).astype(jnp.int32)
    g_i = _sum_chunks(kc_arr, p_i, q_i, rows=hi_rows, cols=COLS, tr=256, name="sum_chunks_in")
    g_o = _sum_chunks(kc_arr, p_o, q_o, rows=ho_rows, cols=D_MODEL, tr=128, name="sum_chunks_out")
    g_i, g_o = _share_result(g_i, g_o)
    grads["w_in"] = g_i
    grads["w_out"] = g_o

    delta, new_m, new_v = {}, {}, {}
    for n in WEIGHTS:
        shp = p[n].shape
        if n in ("w_in", "w_out"):
            two_d = (shp[0] * shp[1], shp[2])
            tr = 512 if n == "w_in" else 256
        else:
            two_d = (-1, shp[-1])
            tr = None
        args = [a.reshape(two_d) for a in (p[n], grads[n], m[n], v[n])]
        outs = _adamw(*args, rows_per_step=tr or args[0].shape[0], name="adamw_" + n, copy_g=tr is not None)
        delta[n], new_m[n], new_v[n] = (a.reshape(shp) for a in outs[0:3])
        if tr is not None:
            grads[n] = outs[3].reshape(shp)

    return (loss, grad_x[None], *[grads[n] for n in WEIGHTS], *[delta[n] for n in WEIGHTS],
            *[new_m[n] for n in WEIGHTS], *[new_v[n] for n in WEIGHTS])


def kernel(x, ln_g, ln_b, w_in, b_in, conv_a_w, conv_a_b, norm_a_g, norm_a_b, conv_b_w, pool_w, pool_scale, sgu_ln_g, sgu_ln_b, sgu_w, sgu_bias, w_out, b_out, loss_target, m_ln_g, m_ln_b, m_w_in, m_b_in, m_conv_a_w, m_conv_a_b, m_norm_a_g, m_norm_a_b, m_conv_b_w, m_pool_w, m_pool_scale, m_sgu_ln_g, m_sgu_ln_b, m_sgu_w, m_sgu_bias, m_w_out, m_b_out, v_ln_g, v_ln_b, v_w_in, v_b_in, v_conv_a_w, v_conv_a_b, v_norm_a_g, v_norm_a_b, v_conv_b_w, v_pool_w, v_pool_scale, v_sgu_ln_g, v_sgu_ln_b, v_sgu_w, v_sgu_bias, v_w_out, v_b_out):
    p = dict(ln_g=ln_g, ln_b=ln_b, w_in=w_in, b_in=b_in, conv_a_w=conv_a_w, conv_a_b=conv_a_b, norm_a_g=norm_a_g,
             norm_a_b=norm_a_b, conv_b_w=conv_b_w, pool_w=pool_w, pool_scale=pool_scale, sgu_ln_g=sgu_ln_g,
             sgu_ln_b=sgu_ln_b, sgu_w=sgu_w, sgu_bias=sgu_bias, w_out=w_out, b_out=b_out)
    m = dict(ln_g=m_ln_g, ln_b=m_ln_b, w_in=m_w_in, b_in=m_b_in, conv_a_w=m_conv_a_w, conv_a_b=m_conv_a_b,
             norm_a_g=m_norm_a_g, norm_a_b=m_norm_a_b, conv_b_w=m_conv_b_w, pool_w=m_pool_w, pool_scale=m_pool_scale,
             sgu_ln_g=m_sgu_ln_g, sgu_ln_b=m_sgu_ln_b, sgu_w=m_sgu_w, sgu_bias=m_sgu_bias, w_out=m_w_out, b_out=m_b_out)
    v = dict(ln_g=v_ln_g, ln_b=v_ln_b, w_in=v_w_in, b_in=v_b_in, conv_a_w=v_conv_a_w, conv_a_b=v_conv_a_b,
             norm_a_g=v_norm_a_g, norm_a_b=v_norm_a_b, conv_b_w=v_conv_b_w, pool_w=v_pool_w, pool_scale=v_pool_scale,
             sgu_ln_g=v_sgu_ln_g, sgu_ln_b=v_sgu_ln_b, sgu_w=v_sgu_w, sgu_bias=v_sgu_bias, w_out=v_w_out, b_out=v_b_out)
    return _step(p, m, v, x[0], loss_target[0], tile_f=256, tile_b=256, tk=2048)
```

```python
import functools

import jax
import jax.numpy as jnp
from jax import lax
from jax.experimental import pallas as pl
from jax.experimental.pallas import tpu as pltpu

F32 = jnp.float32
BF16 = jnp.bfloat16
MESH = pl.DeviceIdType.MESH

D_MODEL = 1024
GROUP = 256
HEAD = 64
N_SLICES = 12
IN_WIDTH = N_SLICES * GROUP
N_CHIPS = 4
COLS = IN_WIDTH // N_CHIPS
KA = 31
KB = 3
HALO_A, HALO_B, HALO_C = 32, 8, 16
POOL_WINDOWS = (2, 4, 8, 16)
SGU_BLOCK = 128
CHUNK = 64
LN_EPS = 1e-5
ROWS = 64
V7X_VMEM_BYTES = 64 * 1024 * 1024
VMEM_LIMIT = 56 * 1024 * 1024

ADAM_LR, ADAM_B1, ADAM_B2, ADAM_EPS, ADAM_WD, ADAM_STEP = 0.001, 0.9, 0.999, 1e-08, 0.01, 10


ANY = pl.BlockSpec(memory_space=pl.ANY)


def _vmem_params(**kw):
    return pltpu.CompilerParams(vmem_limit_bytes=VMEM_LIMIT, **kw)


def _place():
    return lax.axis_index("x"), lax.axis_index("y"), lax.axis_index("c")


def _other_chips(x, y):
    return [(1 - x, y, 2 * (1 - x) + y), (x, 1 - y, 2 * x + (1 - y)), (1 - x, 1 - y, 2 * (1 - x) + (1 - y))]


def _sig(v):
    return 0.5 * jnp.tanh(0.5 * v) + 0.5


def _dot(a, b):
    return jnp.dot(a, b, preferred_element_type=F32)


def _dot_nt(a, b):
    return lax.dot_general(a, b, (((1,), (1,)), ((), ())), preferred_element_type=F32)


def _dot_tn(a, b):
    return lax.dot_general(a, b, (((0,), (0,)), ((), ())), preferred_element_type=F32)


def _segdot(v, m):
    hi = v.astype(BF16)
    lo = (v - hi.astype(F32)).astype(BF16)
    return _dot(hi, m) + _dot(lo, m)


def _colsum(v):
    return jnp.sum(v, axis=0, keepdims=True)


def _rowmean(v):
    return jnp.mean(v, axis=-1, keepdims=True)


def _lane_group(n):
    return lax.broadcasted_iota(jnp.int32, (1, n), 1) // HEAD


def _pool_cnt(tile, t_rows):
    pos = tile * t_rows + lax.broadcasted_iota(jnp.int32, (t_rows, GROUP), 0) + 1
    grp = lax.broadcasted_iota(jnp.int32, (t_rows, GROUP), 1) // HEAD
    win = jnp.where(grp == 0, 2, jnp.where(grp == 1, 4, jnp.where(grp == 2, 8, 16)))
    return jnp.minimum(pos, win).astype(F32)


def _sgu_masks(wm_ref, wmt_ref, wm_s, wmt_s):
    r = lax.broadcasted_iota(jnp.int32, (SGU_BLOCK, 4 * SGU_BLOCK), 0) // CHUNK
    c = (lax.broadcasted_iota(jnp.int32, (SGU_BLOCK, 4 * SGU_BLOCK), 1) % SGU_BLOCK) // CHUNK
    wm_s[...] = jnp.where(c <= r, wm_ref[...], 0.0).astype(BF16)
    if wmt_ref is not None:
        rt = (lax.broadcasted_iota(jnp.int32, (4 * SGU_BLOCK, SGU_BLOCK), 0) % SGU_BLOCK) // CHUNK
        ct = lax.broadcasted_iota(jnp.int32, (4 * SGU_BLOCK, SGU_BLOCK), 1) // CHUNK
        wmt_s[...] = jnp.where(rt <= ct, wmt_ref[...], 0.0).astype(BF16)


def _vstack(v_blk):
    grp = _lane_group(GROUP)
    return jnp.concatenate([jnp.where(grp == h, v_blk, 0.0) for h in range(4)], axis=0).astype(BF16)


def _gather_next(step, nt, nwi, nwo, gwi, gwo, send_sems, recv_sems, loc_sems):
    x, y, c = _place()
    me_k = 2 * x + y
    sibling = (x, y, 1 - c)
    chips = _other_chips(x, y)
    hi, ho = D_MODEL // 2, GROUP // 2

    def rc(src, dst, sem, to):
        return pltpu.make_async_remote_copy(src_ref=src, dst_ref=dst, send_sem=send_sems.at[sem],
                                            recv_sem=recv_sems.at[sem], device_id=to, device_id_type=MESH)

    def blk(ref, k, n, cc):
        return ref.at[k, pl.ds(cc * n, n), :]

    def ici(r):
        px, py, _ = chips[r]
        to = (px, py, c)
        return [rc(nwi.at[pl.ds(c * hi, hi), :], blk(gwi, me_k, hi, c), 2 * r, to),
                rc(nwo.at[pl.ds(c * ho, ho), :], blk(gwo, me_k, ho, c), 2 * r + 1, to)]

    def landed(r, cc, base):
        pk = chips[r][2]
        return [rc(blk(gwi, pk, hi, cc), blk(gwi, pk, hi, cc), base + 2 * r, sibling),
                rc(blk(gwo, pk, ho, cc), blk(gwo, pk, ho, cc), base + 2 * r + 1, sibling)]

    def local():
        return [pltpu.make_async_copy(nwi, gwi.at[me_k], loc_sems.at[0]),
                pltpu.make_async_copy(nwo, gwo.at[me_k], loc_sems.at[1])]

    @pl.when(step == 0)
    def _():
        for cp in local():
            cp.start()
        for r in range(3):
            for cp in ici(r):
                cp.start()

    @pl.when(step == (3 * nt) // 4)
    def _():
        for r in range(3):
            for got, fwd in zip(landed(r, c, 0), landed(r, c, 6)):
                got.wait_recv()
                fwd.start()

    @pl.when(step == nt - 1)
    def _():
        for r in range(3):
            for got in landed(r, 1 - c, 6):
                got.wait_recv()
        for r in range(3):
            for cp in ici(r) + landed(r, c, 6):
                cp.wait_send()
        for cp in local():
            cp.wait()


def _fwd_layer(x, wi, bin_, caw, cbw, s256, seg, pw, wm, sb, wo, v1024, *, tile, nxt=None, target=None):
    assert nxt is None or target is None
    S = x.shape[0]
    T = tile
    nt = S // T
    alpha = float((2.0 * 4) ** 0.25)
    n_in = 12 + (2 if nxt is not None else 0) + (1 if target is not None else 0)
    n_out = 6 + (2 if nxt is not None else 0) + (1 if target is not None else 0)

    def body(*refs):
        (x_ref, wi_ref, bin_ref, caw_ref, cbw_ref, s256_ref, seg_ref, pw_ref, wm_ref, sb_ref, wo_ref,
         v1024_ref) = refs[0:12]
        y_ref, xb_ref, h_ref, aux_ref, mix_ref, z_ref = refs[n_in:n_in + 6]
        abuf, bbuf, cbuf, wm_s = refs[n_in + n_out:n_in + n_out + 4]
        i = pl.program_id(0)
        if nxt is not None:
            _gather_next(i, nt, refs[12], refs[13], refs[n_in + 6], refs[n_in + 7], *refs[n_in + n_out + 4:])

        @pl.when(i == 0)
        def _():
            abuf[0:HALO_A, :] = jnp.zeros((HALO_A, GROUP), F32)
            bbuf[0:HALO_B, :] = jnp.zeros((HALO_B, GROUP), F32)
            cbuf[0:HALO_C, :] = jnp.zeros((HALO_C, GROUP), F32)
            _sgu_masks(wm_ref, None, wm_s, None)

        x = x_ref[...]
        xb = x.astype(BF16)
        xb_ref[...] = xb
        for k in range(N_CHIPS):
            h_ref[:, COLS * k:COLS * (k + 1)] = _dot(xb, wi_ref[k]) + bin_ref[:, COLS * k:COLS * (k + 1)]

        def hs(j):
            return h_ref[:, GROUP * j:GROUP * (j + 1)]

        abuf[HALO_A:HALO_A + T, :] = hs(0) * _sig(hs(1))
        for r0 in range(0, T, ROWS):
            acc = None
            for k in range(KA):
                off = HALO_A - (KA - 1) + k + r0
                term = caw_ref[k:k + 1, :] * abuf[off:off + ROWS, :]
                acc = term if acc is None else acc + term
            aux_ref[r0:r0 + ROWS, 0:GROUP] = acc + s256_ref[0:1, :]
        abuf[0:HALO_A, :] = abuf[T:T + HALO_A, :]
        a1 = aux_ref[:, 0:GROUP]
        segm = seg_ref[...]
        cen = a1 - _segdot(a1, segm)
        var = _segdot(cen * cen, segm)
        a2 = cen * lax.rsqrt(var + LN_EPS) * s256_ref[1:2, :] + s256_ref[2:3, :]
        az = hs(2)
        mix_ref[:, 0:GROUP] = (a2 * _sig(a2) * (az * _sig(az))).astype(BF16)

        bbuf[HALO_B:HALO_B + T, :] = hs(4) * hs(5)
        for r0 in range(0, T, ROWS):
            acc = None
            for k in range(KB):
                off = HALO_B - (KB - 1) + k + r0
                term = cbw_ref[k:k + 1, :] * bbuf[off:off + ROWS, :]
                acc = term if acc is None else acc + term
            aux_ref[r0:r0 + ROWS, GROUP:2 * GROUP] = acc
        bbuf[0:HALO_B, :] = bbuf[T:T + HALO_B, :]
        bz = hs(6)
        mix_ref[:, GROUP:2 * GROUP] = (hs(3) * aux_ref[:, GROUP:2 * GROUP] * (bz * _sig(bz))).astype(BF16)

        ch = hs(7)
        cbuf[HALO_C:HALO_C + T, :] = ch
        hi_lane = (lax.broadcasted_iota(jnp.int32, (1, 128), 1) // HEAD) == 1
        for r0 in range(0, T, ROWS):
            def win(col, j0, j1):
                s = None
                for j in range(j0, j1):
                    off = HALO_C - j + r0
                    term = cbuf[off:off + ROWS, 128 * col:128 * (col + 1)]
                    s = term if s is None else s + term
                return s
            w0 = win(0, 0, 2) + jnp.where(hi_lane, win(0, 2, 4), 0.0)
            w1 = win(1, 0, 8) + jnp.where(hi_lane, win(1, 8, 16), 0.0)
            aux_ref[r0:r0 + ROWS, 2 * GROUP:2 * GROUP + 128] = w0
            aux_ref[r0:r0 + ROWS, 2 * GROUP + 128:3 * GROUP] = w1
        cbuf[0:HALO_C, :] = cbuf[T:T + HALO_C, :]
        pooled = aux_ref[:, 2 * GROUP:3 * GROUP] / _pool_cnt(i, T) - ch
        aux_ref[:, 2 * GROUP:3 * GROUP] = pooled
        q = _dot(pooled.astype(BF16), pw_ref[...])
        cz = hs(8)
        mix_ref[:, 2 * GROUP:3 * GROUP] = (q * s256_ref[3:4, :] * (cz * _sig(cz))).astype(BF16)

        dv = hs(10)
        cen = dv - _rowmean(dv)
        var = _rowmean(cen * cen)
        v = cen * lax.rsqrt(var + LN_EPS) * s256_ref[4:5, :] + s256_ref[5:6, :]
        sps = []
        for n in range(T // SGU_BLOCK):
            vb = v[n * SGU_BLOCK:(n + 1) * SGU_BLOCK, :]
            sps.append(_dot(wm_s[...], _vstack(vb)) + sb_ref[...])
        sp = jnp.concatenate(sps, axis=0)
        dz = hs(11)
        mix_ref[:, 3 * GROUP:4 * GROUP] = (hs(9) * sp * (dz * _sig(dz))).astype(BF16)

        out = v1024_ref[0:1, :]
        for k in range(N_CHIPS):
            out = out + _dot(mix_ref[:, GROUP * k:GROUP * (k + 1)], wo_ref[k])
        z = alpha * x + out
        z_ref[...] = z
        cen = z - _rowmean(z)
        var = _rowmean(cen * cen)
        y = cen * lax.rsqrt(var + LN_EPS) * v1024_ref[1:2, :] + v1024_ref[2:3, :]
        if target is None:
            y_ref[...] = y
        else:
            t_ref, loss_ref = refs[12], refs[n_in + 6]

            @pl.when(i == 0)
            def _():
                loss_ref[...] = jnp.zeros_like(loss_ref)
            err = y - t_ref[...]
            y_ref[...] = err * (1.0 / D_MODEL)
            loss_ref[...] += jnp.sum(_colsum(err * err), axis=1, keepdims=True) * (0.5 / D_MODEL)

    def full(a):
        nd = a.ndim
        return pl.BlockSpec(a.shape, lambda i, _n=nd: (0,) * _n)

    def rows(width):
        return pl.BlockSpec((T, width), lambda i: (i, 0))

    consts = (wi, bin_, caw, cbw, s256, seg, pw, wm, sb, wo, v1024)
    in_specs = [rows(D_MODEL)] + [full(a) for a in consts]
    out_specs = [rows(D_MODEL), rows(D_MODEL), rows(IN_WIDTH), rows(3 * GROUP), rows(D_MODEL), rows(D_MODEL)]
    out_shape = [jax.ShapeDtypeStruct((S, D_MODEL), F32), jax.ShapeDtypeStruct((S, D_MODEL), BF16),
                 jax.ShapeDtypeStruct((S, IN_WIDTH), F32), jax.ShapeDtypeStruct((S, 3 * GROUP), F32),
                 jax.ShapeDtypeStruct((S, D_MODEL), BF16), jax.ShapeDtypeStruct((S, D_MODEL), F32)]
    scratch = [pltpu.VMEM((T + HALO_A, GROUP), F32), pltpu.VMEM((T + HALO_B, GROUP), F32),
               pltpu.VMEM((T + HALO_C, GROUP), F32), pltpu.VMEM((SGU_BLOCK, 4 * SGU_BLOCK), BF16)]
    extra = ()
    if nxt is not None:
        extra = tuple(nxt)
        in_specs += [ANY, ANY]
        out_specs += [ANY, ANY]
        out_shape += [jax.ShapeDtypeStruct((N_CHIPS, D_MODEL, COLS), BF16),
                      jax.ShapeDtypeStruct((N_CHIPS, GROUP, D_MODEL), BF16)]
        scratch += [pltpu.SemaphoreType.DMA((12,)), pltpu.SemaphoreType.DMA((12,)), pltpu.SemaphoreType.DMA((2,))]
    if target is not None:
        extra = (target,)
        in_specs += [rows(D_MODEL)]
        out_specs += [pl.BlockSpec((8, 128), lambda i: (0, 0))]
        out_shape += [jax.ShapeDtypeStruct((8, 128), F32)]
    return pl.pallas_call(
        body, name=("fwd_layer_loss" if target is not None else "fwd_layer") if nxt is None else "fwd_layer_gather",
        grid=(nt,), in_specs=in_specs, out_specs=out_specs, out_shape=out_shape, scratch_shapes=scratch,
        compiler_params=_vmem_params(dimension_semantics=("arbitrary",), has_side_effects=nxt is not None),
    )(x, *consts, *extra)


ROW_CBW = 8
ROW_CAW = 16
ROW_LOSS = 7
ROW_PW = 48
ROW_LNG = 112
ROW_LNB = 116
ROW_BOUT = 120
ROW_BIN = 124
ROW_WC = 136
ROW_SB = 392
SM_ROWS = 520
N_DEV = 8


def _exchange_comm(start, finish, l, p_i, p_o, sm, r_i, r_o, r_sm, send_sems, recv_sems, loc_sem):
    x, y, c = _place()
    me = 4 * x + 2 * y + c
    chips = _other_chips(x, y)

    def rc(src, dst, sem, to):
        return pltpu.make_async_remote_copy(src_ref=src, dst_ref=dst, send_sem=send_sems.at[sem],
                                            recv_sem=recv_sems.at[sem], device_id=to, device_id_type=MESH)

    def big(r):
        px, py, pk = chips[r]
        to = (px, py, c)
        return [rc(p_i.at[l, pk], r_i.at[r, l], 2 * r, to), rc(p_o.at[l, pk], r_o.at[r, l], 2 * r + 1, to)]

    def peer(rel):
        px = 1 - x if rel & 4 else x
        py = 1 - y if rel & 2 else y
        pc = 1 - c if rel & 1 else c
        return (px, py, pc), 4 * px + 2 * py + pc

    def small_out(rel):
        to, _ = peer(rel)
        return rc(sm, r_sm.at[me], 5 + rel, to)

    def small_in(rel):
        to, idx = peer(rel)
        return rc(sm, r_sm.at[idx], 5 + rel, to)

    def local():
        return pltpu.make_async_copy(sm, r_sm.at[me], loc_sem.at[0])

    @pl.when(start)
    def _():
        local().start()
        for r in range(3):
            for cp in big(r):
                cp.start()
        for rel in range(1, N_DEV):
            small_out(rel).start()

    @pl.when(finish)
    def _():
        for r in range(3):
            for cp in big(r):
                cp.wait()
        for rel in range(1, N_DEV):
            small_in(rel).wait_recv()
            small_out(rel).wait_send()
        local().wait()


RC = 32
RC_WIDE = 16
ACC_ROWS = 136


def _rsum8(v):
    r = v[0:8]
    for j in range(1, v.shape[0] // 8):
        r = r + v[8 * j:8 * j + 8]
    return r


def _bwd_layer(dy, z, h, aux, wi, caw, cbw, s256, seg, pw, wm, wmt, sb, wo, v1024, e4, *, tile, exch=None):
    S = dy.shape[0]
    T = tile
    nt = S // T
    nblk = T // SGU_BLOCK
    alpha = float((2.0 * 4) ** 0.25)
    n_in = 16 + (6 if exch is not None else 0)
    n_out = 4 + (3 if exch is not None else 0)
    slab = pltpu.VMEM((T, GROUP), F32)
    scratch = dict(
        dbuf=pltpu.VMEM((T + HALO_A, GROUP), F32), ebuf=pltpu.VMEM((T + HALO_B, GROUP), F32),
        fbuf=pltpu.VMEM((T + HALO_C, GROUP), F32), sh=pltpu.VMEM((7, T + HALO_A - 8, GROUP), F32),
        wm_s=pltpu.VMEM((SGU_BLOCK, 4 * SGU_BLOCK), BF16), wmt_s=pltpu.VMEM((4 * SGU_BLOCK, SGU_BLOCK), BF16),
        dsp_acc=pltpu.VMEM((SGU_BLOCK, GROUP), F32), pw_acc=pltpu.VMEM((GROUP, GROUP), F32),
        acc_s=pltpu.VMEM((8 * ACC_ROWS, GROUP), F32), acc_w=pltpu.VMEM((24, D_MODEL), F32),
        dmix_s=pltpu.VMEM((T, D_MODEL), F32), vst_s=pltpu.VMEM((nblk, 4 * SGU_BLOCK, GROUP), BF16),
        dq_s=pltpu.VMEM((T, GROUP), BF16),
        mean_s=slab, t1_s=slab, t2_s=slab, q_s=slab, xv_s=slab, rv_s=slab, v_s=slab, sp_s=slab, a0_s=slab, sg_s=slab,
        xh_s=slab, ra_s=slab, ub_s=slab, dsp_s=slab, m1_s=slab, m2_s=slab, dpool_s=slab, dvd_s=slab, u_s=slab,
        du_s=slab, cw_s=slab)
    names = list(scratch)

    def body(*refs):
        (dy_ref, z_ref, h_ref, aux_ref, wi_ref, caw_ref, cbw_ref, s256_ref, seg_ref, pw_ref, wm_ref, wmt_ref,
         sb_ref, wo_ref, v1024_ref, e4_ref) = refs[0:16]
        dx_ref, dhb_ref, dzb_ref, osm_ref = refs[n_in:n_in + 4]
        k0 = n_in + n_out
        sc = dict(zip(names, refs[k0:k0 + len(names)]))
        dbuf, ebuf, fbuf, sh = sc["dbuf"], sc["ebuf"], sc["fbuf"], sc["sh"]
        wm_s, wmt_s, dsp_acc, pw_acc, acc_s, acc_w = (sc[n] for n in ("wm_s", "wmt_s", "dsp_acc", "pw_acc", "acc_s",
                                                                        "acc_w"))
        dmix_s, vst_s, dq_s = sc["dmix_s"], sc["vst_s"], sc["dq_s"]
        i = pl.program_id(0)
        tile_idx = nt - 1 - i
        if exch is not None:
            l_ref, p_i, p_o, sm = refs[16:20]
            r_i, r_o, r_sm = refs[n_in + 4:n_in + 7]
            _exchange_comm(i == 0, i == nt - 1, l_ref[0], p_i, p_o, sm, r_i, r_o, r_sm, *refs[k0 + len(names):])

        @pl.when(i == 0)
        def _():
            dbuf[T:T + HALO_A, :] = jnp.zeros((HALO_A, GROUP), F32)
            ebuf[T:T + HALO_B, :] = jnp.zeros((HALO_B, GROUP), F32)
            fbuf[T:T + HALO_C, :] = jnp.zeros((HALO_C, GROUP), F32)
            _sgu_masks(wm_ref, wmt_ref, wm_s, wmt_s)
            osm_ref[...] = jnp.zeros_like(osm_ref)
            dsp_acc[...] = jnp.zeros_like(dsp_acc)
            pw_acc[...] = jnp.zeros_like(pw_acc)
            acc_s[...] = jnp.zeros_like(acc_s)
            acc_w[...] = jnp.zeros_like(acc_w)

        def chunks(rc, fn):
            for c in range(T // rc):
                fn(pl.ds(c * rc, rc))

        def hs(j, rows):
            return h_ref[rows, GROUP * j:GROUP * (j + 1)]

        def acc_add(row, val):
            acc_s[8 * row:8 * row + 8, :] += _rsum8(val)

        def put_dh(j, rows, val):
            acc_add(ROW_BIN + j, val)
            dhb_ref[rows, GROUP * j:GROUP * (j + 1)] = val.astype(BF16)

        def dsilu(v, s):
            return s * (1.0 + v * (1.0 - s))

        def vec(r):
            return s256_ref[r:r + 1, :]

        def ln_bwd(rows):
            dyc = dy_ref[rows, :]
            zc = z_ref[rows, :]
            cen = zc - _rowmean(zc)
            rstd = lax.rsqrt(_rowmean(cen * cen) + LN_EPS)
            xhat = cen * rstd
            acc_w[0:8, :] += _rsum8(dyc * xhat)
            acc_w[8:16, :] += _rsum8(dyc)
            gdy = dyc * v1024_ref[1:2, :]
            dz = rstd * (gdy - _rowmean(gdy) - xhat * _rowmean(gdy * xhat))
            acc_w[16:24, :] += _rsum8(dz)
            dzb_ref[rows, :] = dz.astype(BF16)
            dx_ref[rows, :] = alpha * dz
        chunks(RC_WIDE, ln_bwd)

        segm = seg_ref[...]
        dzb = dzb_ref[...]
        for k in range(N_CHIPS):
            dmix_s[:, GROUP * k:GROUP * (k + 1)] = _dot_nt(dzb, wo_ref[k])
        sc["mean_s"][...] = _segdot(aux_ref[:, 0:GROUP], segm)
        pooled_b = aux_ref[:, 2 * GROUP:3 * GROUP].astype(BF16)
        sc["q_s"][...] = _dot(pooled_b, pw_ref[...])

        def centre(rows):
            cen = aux_ref[rows, 0:GROUP] - sc["mean_s"][rows, :]
            sc["t1_s"][rows, :] = cen * cen
            dv_in = hs(10, rows)
            cen_v = dv_in - _rowmean(dv_in)
            rstd_v = lax.rsqrt(_rowmean(cen_v * cen_v) + LN_EPS)
            xv = cen_v * rstd_v
            sc["xv_s"][rows, :] = xv
            sc["rv_s"][rows, :] = jnp.broadcast_to(rstd_v, xv.shape)
            sc["v_s"][rows, :] = xv * vec(4) + vec(5)
        chunks(RC, centre)

        sc["t2_s"][...] = _segdot(sc["t1_s"][...], segm)
        for n in range(nblk):
            blk = slice(n * SGU_BLOCK, (n + 1) * SGU_BLOCK)
            vst_s[n] = _vstack(sc["v_s"][blk, :])
            sc["sp_s"][blk, :] = _dot(wm_s[...], vst_s[n]) + sb_ref[...]

        def mixers(rows):
            a_val, a_glu, a_z = hs(0, rows), hs(1, rows), hs(2, rows)
            sg = _sig(a_glu)
            sc["a0_s"][rows, :] = a_val * sg
            sc["sg_s"][rows, :] = sg
            rstd_a = lax.rsqrt(sc["t2_s"][rows, :] + LN_EPS)
            xh = (aux_ref[rows, 0:GROUP] - sc["mean_s"][rows, :]) * rstd_a
            a2 = xh * vec(1) + vec(2)
            s2 = _sig(a2)
            sz = _sig(a_z)
            dya = dmix_s[rows, 0:GROUP]
            put_dh(2, rows, dya * (a2 * s2) * dsilu(a_z, sz))
            d_a2 = dya * (a_z * sz) * dsilu(a2, s2)
            acc_add(1, d_a2 * xh)
            acc_add(2, d_a2)
            gd = d_a2 * vec(1)
            sc["t1_s"][rows, :] = gd
            sc["t2_s"][rows, :] = gd * xh
            sc["xh_s"][rows, :] = xh
            sc["ra_s"][rows, :] = rstd_a
            b_b, b_c, b_h, b_z = hs(3, rows), hs(4, rows), hs(5, rows), hs(6, rows)
            cb = aux_ref[rows, GROUP:2 * GROUP]
            sz = _sig(b_z)
            dyb = dmix_s[rows, GROUP:2 * GROUP]
            put_dh(3, rows, dyb * cb * (b_z * sz))
            put_dh(6, rows, dyb * b_b * cb * dsilu(b_z, sz))
            ebuf[rows, :] = dyb * b_b * (b_z * sz)
            sc["ub_s"][rows, :] = b_c * b_h
            c_z = hs(8, rows)
            q = sc["q_s"][rows, :]
            sz = _sig(c_z)
            dyc = dmix_s[rows, 2 * GROUP:3 * GROUP]
            acc_add(3, dyc * q * (c_z * sz))
            put_dh(8, rows, dyc * q * vec(3) * dsilu(c_z, sz))
            dq_s[rows, :] = (dyc * vec(3) * (c_z * sz)).astype(BF16)
            d_u, d_z = hs(9, rows), hs(11, rows)
            sp = sc["sp_s"][rows, :]
            sz = _sig(d_z)
            dyd = dmix_s[rows, 3 * GROUP:4 * GROUP]
            put_dh(9, rows, dyd * sp * (d_z * sz))
            put_dh(11, rows, dyd * d_u * sp * dsilu(d_z, sz))
            sc["dsp_s"][rows, :] = dyd * d_u * (d_z * sz)
        chunks(RC, mixers)

        sc["m1_s"][...] = _segdot(sc["t1_s"][...], segm)
        sc["m2_s"][...] = _segdot(sc["t2_s"][...], segm)
        d_q = dq_s[...]
        pw_acc[...] += _dot_tn(pooled_b, d_q)
        sc["dpool_s"][...] = _dot_nt(d_q, pw_ref[...])
        grp = _lane_group(GROUP)
        for n in range(nblk):
            blk = slice(n * SGU_BLOCK, (n + 1) * SGU_BLOCK)
            dspb = sc["dsp_s"][blk, :]
            dsp_acc[...] += dspb
            dspb16 = dspb.astype(BF16)
            dvst = _dot(wmt_s[...], dspb16)
            dvb = None
            for hh in range(4):
                part = jnp.where(grp == hh, dvst[hh * SGU_BLOCK:(hh + 1) * SGU_BLOCK, :], 0.0)
                dvb = part if dvb is None else dvb + part
            sc["dvd_s"][blk, :] = dvb
            dwc = _dot_nt(dspb16, vst_s[n])
            osm_ref[ROW_WC:ROW_WC + SGU_BLOCK, :] += dwc[:, 0:GROUP]
            osm_ref[ROW_WC + SGU_BLOCK:ROW_WC + 2 * SGU_BLOCK, :] += dwc[:, GROUP:2 * GROUP]

        def ln_sums(rows):
            xh = sc["xh_s"][rows, :]
            d_a1 = sc["ra_s"][rows, :] * (sc["t1_s"][rows, :] - sc["m1_s"][rows, :] - xh * sc["m2_s"][rows, :])
            acc_add(0, d_a1)
            dbuf[rows, :] = d_a1
            pos = tile_idx * T + rows.start + lax.broadcasted_iota(jnp.int32, (RC, GROUP), 0) + 1
            lane = lax.broadcasted_iota(jnp.int32, (RC, GROUP), 1) // HEAD
            win = jnp.where(lane == 0, 2, jnp.where(lane == 1, 4, jnp.where(lane == 2, 8, 16)))
            fbuf[rows, :] = sc["dpool_s"][rows, :] / jnp.minimum(pos, win).astype(F32)
            d_v = sc["dvd_s"][rows, :]
            xv = sc["xv_s"][rows, :]
            acc_add(4, d_v * xv)
            acc_add(5, d_v)
            gd = d_v * vec(4)
            put_dh(10, rows, sc["rv_s"][rows, :] * (gd - _rowmean(gd) - xv * _rowmean(gd * xv)))
        chunks(RC, ln_sums)

        span = T + HALO_A - 8
        for p in range(1, 8):
            sh[p - 1, :, :] = dbuf[p:p + span, :]

        def conv_a(rows):
            a0c = sc["a0_s"][rows, :]
            acc = None
            for k in range(KA):
                off = (KA - 1) - k
                p, q8 = off % 8, off - off % 8
                w = dbuf[pl.ds(rows.start + q8, RC), :] if p == 0 else sh[p - 1, pl.ds(rows.start + q8, RC), :]
                term = caw_ref[k:k + 1, :] * w
                acc = term if acc is None else acc + term
                acc_add(ROW_CAW + k, a0c * w)
            sc["u_s"][rows, :] = acc
        chunks(RC, conv_a)
        dbuf[T:T + HALO_A, :] = dbuf[0:HALO_A, :]

        for r0 in range(0, T, ROWS):
            uc = sc["ub_s"][r0:r0 + ROWS, :]
            acc = None
            for k in range(KB):
                off = (KB - 1) - k + r0
                w = ebuf[off:off + ROWS, :]
                term = cbw_ref[k:k + 1, :] * w
                acc = term if acc is None else acc + term
                acc_add(ROW_CBW + k, uc * w)
            sc["du_s"][r0:r0 + ROWS, :] = acc
        ebuf[T:T + HALO_B, :] = ebuf[0:HALO_B, :]

        hi_lane = (lax.broadcasted_iota(jnp.int32, (1, 128), 1) // HEAD) == 1
        for r0 in range(0, T, ROWS):
            def win(col, j0, j1):
                s = None
                for j in range(j0, j1):
                    term = fbuf[r0 + j:r0 + j + ROWS, 128 * col:128 * (col + 1)]
                    s = term if s is None else s + term
                return s
            sc["cw_s"][r0:r0 + ROWS, 0:128] = win(0, 0, 2) + jnp.where(hi_lane, win(0, 2, 4), 0.0)
            sc["cw_s"][r0:r0 + ROWS, 128:256] = win(1, 0, 8) + jnp.where(hi_lane, win(1, 8, 16), 0.0)
        fbuf[T:T + HALO_C, :] = fbuf[0:HALO_C, :]

        def rest(rows):
            d_a0 = sc["u_s"][rows, :]
            sg = sc["sg_s"][rows, :]
            put_dh(0, rows, d_a0 * sg)
            put_dh(1, rows, d_a0 * hs(0, rows) * sg * (1.0 - sg))
            d_u = sc["du_s"][rows, :]
            put_dh(4, rows, d_u * hs(5, rows))
            put_dh(5, rows, d_u * hs(4, rows))
            put_dh(7, rows, sc["cw_s"][rows, :] - sc["dpool_s"][rows, :])
        chunks(RC, rest)

        acc = None
        for k in range(N_CHIPS):
            term = _dot_nt(dhb_ref[:, COLS * k:COLS * (k + 1)], wi_ref[k])
            acc = term if acc is None else acc + term
        dx_ref[...] += acc

        @pl.when(i == nt - 1)
        def _():
            for row in list(range(6)) + list(range(ROW_CBW, ROW_CBW + KB)) + list(range(ROW_CAW, ROW_CAW + KA)) + list(
                    range(ROW_BIN, ROW_BIN + N_SLICES)):
                osm_ref[row:row + 1, :] = _colsum(acc_s[8 * row:8 * row + 8, :])
            for j, row in enumerate((ROW_LNG, ROW_LNB, ROW_BOUT)):
                cs = _colsum(acc_w[8 * j:8 * j + 8, :])
                for q in range(D_MODEL // GROUP):
                    osm_ref[row + q:row + q + 1, :] = cs[:, GROUP * q:GROUP * (q + 1)]
            r = lax.broadcasted_iota(jnp.int32, (SGU_BLOCK, GROUP), 0) // CHUNK
            c = (lax.broadcasted_iota(jnp.int32, (SGU_BLOCK, GROUP), 1) % SGU_BLOCK) // CHUNK
            for half in range(2):
                rows_ = slice(ROW_WC + half * SGU_BLOCK, ROW_WC + (half + 1) * SGU_BLOCK)
                osm_ref[rows_, :] = jnp.where(c <= r, osm_ref[rows_, :], 0.0)
            osm_ref[ROW_SB:ROW_SB + SGU_BLOCK, 0:128] = _segdot(dsp_acc[...], e4_ref[...])
            for g in range(4):
                osm_ref[ROW_PW:ROW_PW + HEAD, HEAD * g:HEAD * (g + 1)] = (
                    pw_acc[HEAD * g:HEAD * (g + 1), HEAD * g:HEAD * (g + 1)])

    def full(a):
        nd = a.ndim
        return pl.BlockSpec(a.shape, lambda i, _n=nd: (0,) * _n)

    def rows(width):
        return pl.BlockSpec((T, width), lambda i: (nt - 1 - i, 0))

    def acc(shape):
        return pl.BlockSpec(shape, lambda i: (0, 0))

    consts = (wi, caw, cbw, s256, seg, pw, wm, wmt, sb, wo, v1024, e4)
    in_specs = [rows(D_MODEL), rows(D_MODEL), rows(IN_WIDTH), rows(3 * GROUP)] + [full(a) for a in consts]
    out_specs = [rows(D_MODEL), rows(IN_WIDTH), rows(D_MODEL), acc((SM_ROWS, GROUP))]
    out_shape = [jax.ShapeDtypeStruct((S, D_MODEL), F32), jax.ShapeDtypeStruct((S, IN_WIDTH), BF16),
                 jax.ShapeDtypeStruct((S, D_MODEL), BF16), jax.ShapeDtypeStruct((SM_ROWS, GROUP), F32)]
    scratch_shapes = list(scratch.values())
    extra, aliases = (), {}
    if exch is not None:
        extra = tuple(exch)
        r_i, r_o = exch[4], exch[5]
        in_specs += [pl.BlockSpec(memory_space=pltpu.SMEM)] + [ANY] * 5
        out_specs += [ANY] * 3
        out_shape += [jax.ShapeDtypeStruct(r_i.shape, r_i.dtype), jax.ShapeDtypeStruct(r_o.shape, r_o.dtype),
                      jax.ShapeDtypeStruct((N_DEV, SM_ROWS, GROUP), F32)]
        scratch_shapes += [pltpu.SemaphoreType.DMA((13,)), pltpu.SemaphoreType.DMA((13,)),
                           pltpu.SemaphoreType.DMA((1,))]
        aliases = {20: 4, 21: 5}
    return pl.pallas_call(
        body, name="bwd_layer" if exch is None else "bwd_layer_exchange",
        grid=(nt,), in_specs=in_specs, out_specs=out_specs, out_shape=out_shape, scratch_shapes=scratch_shapes,
        input_output_aliases=aliases,
        compiler_params=_vmem_params(dimension_semantics=("arbitrary",), has_side_effects=exch is not None),
    )(dy, z, h, aux, *consts, *extra)


def _bwd_layer_slabwise(dy, z, h, aux, wi, caw, cbw, s256, seg, pw, wm, wmt, sb, wo, v1024, e4, *, tile, exch=None):
    S = dy.shape[0]
    T = tile
    nt = S // T
    alpha = float((2.0 * 4) ** 0.25)
    n_in = 16 + (6 if exch is not None else 0)
    n_out = 4 + (3 if exch is not None else 0)

    def body(*refs):
        (dy_ref, z_ref, h_ref, aux_ref, wi_ref, caw_ref, cbw_ref, s256_ref, seg_ref, pw_ref, wm_ref, wmt_ref,
         sb_ref, wo_ref, v1024_ref, e4_ref) = refs[0:16]
        dx_ref, dhb_ref, dzb_ref, osm_ref = refs[n_in:n_in + 4]
        dbuf, ebuf, fbuf, a0_s, u_s, wm_s, wmt_s, dsp_acc, pw_acc = refs[n_in + n_out:n_in + n_out + 9]
        i = pl.program_id(0)
        tile_idx = nt - 1 - i
        if exch is not None:
            l_ref, p_i, p_o, sm = refs[16:20]
            r_i, r_o, r_sm = refs[n_in + 4:n_in + 7]
            _exchange_comm(i == 0, i == nt - 1, l_ref[0], p_i, p_o, sm, r_i, r_o, r_sm, *refs[n_in + n_out + 9:])

        @pl.when(i == 0)
        def _():
            dbuf[T:T + HALO_A, :] = jnp.zeros((HALO_A, GROUP), F32)
            ebuf[T:T + HALO_B, :] = jnp.zeros((HALO_B, GROUP), F32)
            fbuf[T:T + HALO_C, :] = jnp.zeros((HALO_C, GROUP), F32)
            _sgu_masks(wm_ref, wmt_ref, wm_s, wmt_s)
            osm_ref[...] = jnp.zeros_like(osm_ref)
            dsp_acc[...] = jnp.zeros_like(dsp_acc)
            pw_acc[...] = jnp.zeros_like(pw_acc)

        def hs(j):
            return h_ref[:, GROUP * j:GROUP * (j + 1)]

        def acc_row(row, val):
            osm_ref[row:row + 1, :] += _colsum(val)

        def acc_wide(row, val):
            cs = _colsum(val)
            for j in range(D_MODEL // GROUP):
                osm_ref[row + j:row + j + 1, :] += cs[:, GROUP * j:GROUP * (j + 1)]

        def put_dh(j, val):
            acc_row(ROW_BIN + j, val)
            dhb_ref[:, GROUP * j:GROUP * (j + 1)] = val.astype(BF16)

        def dsilu(v, s):
            return s * (1.0 + v * (1.0 - s))

        dy = dy_ref[...]
        z = z_ref[...]
        cen = z - _rowmean(z)
        rstd = lax.rsqrt(_rowmean(cen * cen) + LN_EPS)
        xhat = cen * rstd
        acc_wide(ROW_LNG, dy * xhat)
        acc_wide(ROW_LNB, dy)
        gdy = dy * v1024_ref[1:2, :]
        dz = rstd * (gdy - _rowmean(gdy) - xhat * _rowmean(gdy * xhat))
        acc_wide(ROW_BOUT, dz)
        dzb = dz.astype(BF16)
        dzb_ref[...] = dzb

        def dmix(k):
            return _dot_nt(dzb, wo_ref[k])

        segm = seg_ref[...]

        a_val, a_glu, a_z = hs(0), hs(1), hs(2)
        sg = _sig(a_glu)
        a0_s[...] = a_val * sg
        a1 = aux_ref[:, 0:GROUP]
        cen = a1 - _segdot(a1, segm)
        rstd_a = lax.rsqrt(_segdot(cen * cen, segm) + LN_EPS)
        xh = cen * rstd_a
        a2 = xh * s256_ref[1:2, :] + s256_ref[2:3, :]
        s2 = _sig(a2)
        sz = _sig(a_z)
        dya = dmix(0)
        put_dh(2, dya * (a2 * s2) * dsilu(a_z, sz))
        d_a2 = dya * (a_z * sz) * dsilu(a2, s2)
        acc_row(1, d_a2 * xh)
        acc_row(2, d_a2)
        gd = d_a2 * s256_ref[1:2, :]
        d_a1 = rstd_a * (gd - _segdot(gd, segm) - xh * _segdot(gd * xh, segm))
        acc_row(0, d_a1)
        dbuf[0:T, :] = d_a1
        for r0 in range(0, T, ROWS):
            a0c = a0_s[r0:r0 + ROWS, :]
            acc = None
            for k in range(KA):
                off = (KA - 1) - k + r0
                w = dbuf[off:off + ROWS, :]
                term = caw_ref[k:k + 1, :] * w
                acc = term if acc is None else acc + term
                acc_row(ROW_CAW + k, a0c * w)
            u_s[r0:r0 + ROWS, :] = acc
        dbuf[T:T + HALO_A, :] = dbuf[0:HALO_A, :]
        d_a0 = u_s[...]
        put_dh(0, d_a0 * sg)
        put_dh(1, d_a0 * a_val * sg * (1.0 - sg))

        b_b, b_c, b_h, b_z = hs(3), hs(4), hs(5), hs(6)
        cb = aux_ref[:, GROUP:2 * GROUP]
        sz = _sig(b_z)
        dyb = dmix(1)
        put_dh(3, dyb * cb * (b_z * sz))
        put_dh(6, dyb * b_b * cb * dsilu(b_z, sz))
        ebuf[0:T, :] = dyb * b_b * (b_z * sz)
        a0_s[...] = b_c * b_h
        for r0 in range(0, T, ROWS):
            uc = a0_s[r0:r0 + ROWS, :]
            acc = None
            for k in range(KB):
                off = (KB - 1) - k + r0
                w = ebuf[off:off + ROWS, :]
                term = cbw_ref[k:k + 1, :] * w
                acc = term if acc is None else acc + term
                acc_row(ROW_CBW + k, uc * w)
            u_s[r0:r0 + ROWS, :] = acc
        ebuf[T:T + HALO_B, :] = ebuf[0:HALO_B, :]
        d_u = u_s[...]
        put_dh(4, d_u * b_h)
        put_dh(5, d_u * b_c)

        c_z = hs(8)
        pooled = aux_ref[:, 2 * GROUP:3 * GROUP]
        pooled_b = pooled.astype(BF16)
        q = _dot(pooled_b, pw_ref[...])
        sz = _sig(c_z)
        dyc = dmix(2)
        ps = s256_ref[3:4, :]
        acc_row(3, dyc * q * (c_z * sz))
        put_dh(8, dyc * q * ps * dsilu(c_z, sz))
        d_q = (dyc * ps * (c_z * sz)).astype(BF16)
        pw_acc[...] += _dot_tn(pooled_b, d_q)
        d_pooled = _dot_nt(d_q, pw_ref[...])
        fbuf[0:T, :] = d_pooled / _pool_cnt(tile_idx, T)
        hi_lane = (lax.broadcasted_iota(jnp.int32, (1, 128), 1) // HEAD) == 1
        for r0 in range(0, T, ROWS):
            def win(col, j0, j1):
                s = None
                for j in range(j0, j1):
                    term = fbuf[r0 + j:r0 + j + ROWS, 128 * col:128 * (col + 1)]
                    s = term if s is None else s + term
                return s
            u_s[r0:r0 + ROWS, 0:128] = win(0, 0, 2) + jnp.where(hi_lane, win(0, 2, 4), 0.0)
            u_s[r0:r0 + ROWS, 128:256] = win(1, 0, 8) + jnp.where(hi_lane, win(1, 8, 16), 0.0)
        fbuf[T:T + HALO_C, :] = fbuf[0:HALO_C, :]
        put_dh(7, u_s[...] - d_pooled)

        d_u_, d_v_, d_z_ = hs(9), hs(10), hs(11)
        cen = d_v_ - _rowmean(d_v_)
        rstd_v = lax.rsqrt(_rowmean(cen * cen) + LN_EPS)
        xv = cen * rstd_v
        v = xv * s256_ref[4:5, :] + s256_ref[5:6, :]
        sz = _sig(d_z_)
        dyd = dmix(3)
        d_sp = dyd * d_u_ * (d_z_ * sz)
        grp = _lane_group(GROUP)
        sps, dvs = [], []
        for n in range(T // SGU_BLOCK):
            blk = slice(n * SGU_BLOCK, (n + 1) * SGU_BLOCK)
            vst = _vstack(v[blk, :])
            sps.append(_dot(wm_s[...], vst) + sb_ref[...])
            dspb = d_sp[blk, :]
            dsp_acc[...] += dspb
            dspb16 = dspb.astype(BF16)
            dvst = _dot(wmt_s[...], dspb16)
            dvb = None
            for hh in range(4):
                part = jnp.where(grp == hh, dvst[hh * SGU_BLOCK:(hh + 1) * SGU_BLOCK, :], 0.0)
                dvb = part if dvb is None else dvb + part
            dvs.append(dvb)
            dwc = _dot_nt(dspb16, vst)
            osm_ref[ROW_WC:ROW_WC + SGU_BLOCK, :] += dwc[:, 0:GROUP]
            osm_ref[ROW_WC + SGU_BLOCK:ROW_WC + 2 * SGU_BLOCK, :] += dwc[:, GROUP:2 * GROUP]
        sp = jnp.concatenate(sps, axis=0)
        d_v = jnp.concatenate(dvs, axis=0)
        put_dh(9, dyd * sp * (d_z_ * sz))
        put_dh(11, dyd * d_u_ * sp * dsilu(d_z_, sz))
        acc_row(4, d_v * xv)
        acc_row(5, d_v)
        gd = d_v * s256_ref[4:5, :]
        put_dh(10, rstd_v * (gd - _rowmean(gd) - xv * _rowmean(gd * xv)))

        dx = alpha * dz
        for k in range(N_CHIPS):
            dx = dx + _dot_nt(dhb_ref[:, COLS * k:COLS * (k + 1)], wi_ref[k])
        dx_ref[...] = dx

        @pl.when(i == nt - 1)
        def _():
            r = lax.broadcasted_iota(jnp.int32, (SGU_BLOCK, GROUP), 0) // CHUNK
            c = (lax.broadcasted_iota(jnp.int32, (SGU_BLOCK, GROUP), 1) % SGU_BLOCK) // CHUNK
            for half in range(2):
                rows_ = slice(ROW_WC + half * SGU_BLOCK, ROW_WC + (half + 1) * SGU_BLOCK)
                osm_ref[rows_, :] = jnp.where(c <= r, osm_ref[rows_, :], 0.0)
            osm_ref[ROW_SB:ROW_SB + SGU_BLOCK, 0:128] = _segdot(dsp_acc[...], e4_ref[...])
            for g in range(4):
                osm_ref[ROW_PW:ROW_PW + HEAD, HEAD * g:HEAD * (g + 1)] = (
                    pw_acc[HEAD * g:HEAD * (g + 1), HEAD * g:HEAD * (g + 1)])

    def full(a):
        nd = a.ndim
        return pl.BlockSpec(a.shape, lambda i, _n=nd: (0,) * _n)

    def rows(width):
        return pl.BlockSpec((T, width), lambda i: (nt - 1 - i, 0))

    def acc(shape):
        return pl.BlockSpec(shape, lambda i: (0, 0))

    consts = (wi, caw, cbw, s256, seg, pw, wm, wmt, sb, wo, v1024, e4)
    in_specs = [rows(D_MODEL), rows(D_MODEL), rows(IN_WIDTH), rows(3 * GROUP)] + [full(a) for a in consts]
    out_specs = [rows(D_MODEL), rows(IN_WIDTH), rows(D_MODEL), acc((SM_ROWS, GROUP))]
    out_shape = [jax.ShapeDtypeStruct((S, D_MODEL), F32), jax.ShapeDtypeStruct((S, IN_WIDTH), BF16),
                 jax.ShapeDtypeStruct((S, D_MODEL), BF16), jax.ShapeDtypeStruct((SM_ROWS, GROUP), F32)]
    scratch = [pltpu.VMEM((T + HALO_A, GROUP), F32), pltpu.VMEM((T + HALO_B, GROUP), F32),
               pltpu.VMEM((T + HALO_C, GROUP), F32), pltpu.VMEM((T, GROUP), F32), pltpu.VMEM((T, GROUP), F32),
               pltpu.VMEM((SGU_BLOCK, 4 * SGU_BLOCK), BF16), pltpu.VMEM((4 * SGU_BLOCK, SGU_BLOCK), BF16),
               pltpu.VMEM((SGU_BLOCK, GROUP), F32), pltpu.VMEM((GROUP, GROUP), F32)]
    extra, aliases = (), {}
    if exch is not None:
        extra = tuple(exch)
        r_i, r_o = exch[4], exch[5]
        in_specs += [pl.BlockSpec(memory_space=pltpu.SMEM)] + [ANY] * 5
        out_specs += [ANY] * 3
        out_shape += [jax.ShapeDtypeStruct(r_i.shape, r_i.dtype), jax.ShapeDtypeStruct(r_o.shape, r_o.dtype),
                      jax.ShapeDtypeStruct((N_DEV, SM_ROWS, GROUP), F32)]
        scratch += [pltpu.SemaphoreType.DMA((13,)), pltpu.SemaphoreType.DMA((13,)), pltpu.SemaphoreType.DMA((1,))]
        aliases = {20: 4, 21: 5}
    return pl.pallas_call(
        body, name="bwd_layer" if exch is None else "bwd_layer_exchange",
        grid=(nt,), in_specs=in_specs, out_specs=out_specs, out_shape=out_shape, scratch_shapes=scratch,
        input_output_aliases=aliases,
        compiler_params=_vmem_params(dimension_semantics=("arbitrary",), has_side_effects=exch is not None),
    )(dy, z, h, aux, *consts, *extra)


def _dw_in(layer, xb, dhb, slab, slab16, *, tk):
    S = xb.shape[0]
    ns = S // tk

    def body(l_ref, a_ref, b_ref, slab_ref, slab16_ref, o_ref, o16_ref):
        del l_ref, slab_ref, slab16_ref

        @pl.when(pl.program_id(1) == 0)
        def _():
            o_ref[...] = jnp.zeros_like(o_ref)
        o_ref[...] += _dot_tn(a_ref[...], b_ref[...])

        @pl.when(pl.program_id(1) == ns - 1)
        def _():
            o16_ref[...] = o_ref[...].astype(BF16)

    o_spec = pl.BlockSpec((None, None, D_MODEL, COLS), lambda j, s, l: (l[0], j, 0, 0))
    grid_spec = pltpu.PrefetchScalarGridSpec(
        num_scalar_prefetch=1, grid=(N_CHIPS, ns),
        in_specs=[pl.BlockSpec((tk, D_MODEL), lambda j, s, l: (s, 0)), pl.BlockSpec((tk, COLS), lambda j, s, l: (s, j)),
                  ANY, ANY],
        out_specs=[o_spec, o_spec])
    return pl.pallas_call(
        body, name="dw_in", grid_spec=grid_spec,
        out_shape=[jax.ShapeDtypeStruct(slab.shape, F32), jax.ShapeDtypeStruct(slab.shape, BF16)],
        input_output_aliases={3: 0, 4: 1},
        compiler_params=_vmem_params(dimension_semantics=("arbitrary", "arbitrary")),
    )(layer, xb, dhb, slab, slab16)


def _dw_out(layer, mixb, dzb, slab, slab16, *, tk):
    S = mixb.shape[0]
    ns = S // tk

    def body(l_ref, a_ref, b_ref, slab_ref, slab16_ref, o_ref, o16_ref):
        del l_ref, slab_ref, slab16_ref

        @pl.when(pl.program_id(0) == 0)
        def _():
            o_ref[...] = jnp.zeros_like(o_ref)
        o_ref[...] += _dot_tn(a_ref[...], b_ref[...]).reshape(N_CHIPS, GROUP, D_MODEL)

        @pl.when(pl.program_id(0) == ns - 1)
        def _():
            o16_ref[...] = o_ref[...].astype(BF16)

    o_spec = pl.BlockSpec((None, N_CHIPS, GROUP, D_MODEL), lambda s, l: (l[0], 0, 0, 0))
    grid_spec = pltpu.PrefetchScalarGridSpec(
        num_scalar_prefetch=1, grid=(ns,),
        in_specs=[pl.BlockSpec((tk, D_MODEL), lambda s, l: (s, 0)), pl.BlockSpec((tk, D_MODEL), lambda s, l: (s, 0)),
                  ANY, ANY],
        out_specs=[o_spec, o_spec])
    return pl.pallas_call(
        body, name="dw_out", grid_spec=grid_spec,
        out_shape=[jax.ShapeDtypeStruct(slab.shape, F32), jax.ShapeDtypeStruct(slab.shape, BF16)],
        input_output_aliases={3: 0, 4: 1},
        compiler_params=_vmem_params(dimension_semantics=("arbitrary",)),
    )(layer, mixb, dzb, slab, slab16)


def _adamw(w, g, m, v, *, rows_per_step, name, copy_g=False):
    R, C = w.shape
    tr = rows_per_step
    c1 = 1.0 - ADAM_B1 ** ADAM_STEP
    c2 = 1.0 - ADAM_B2 ** ADAM_STEP

    def body(w_ref, g_ref, m_ref, v_ref, d_ref, nm_ref, nv_ref, *g_out):
        g_ = g_ref[...]
        nm = ADAM_B1 * m_ref[...] + (1.0 - ADAM_B1) * g_
        nv = ADAM_B2 * v_ref[...] + (1.0 - ADAM_B2) * (g_ * g_)
        nm_ref[...] = nm
        nv_ref[...] = nv
        d_ref[...] = -ADAM_LR * ((nm / c1) / (jnp.sqrt(nv / c2) + ADAM_EPS) + ADAM_WD * w_ref[...])
        if copy_g:
            g_out[0][...] = g_

    spec = pl.BlockSpec((tr, C), lambda i: (i, 0))
    n_out = 4 if copy_g else 3
    return pl.pallas_call(
        body, name=name, grid=(R // tr,),
        in_specs=[spec] * 4, out_specs=[spec] * n_out,
        out_shape=[jax.ShapeDtypeStruct((R, C), F32)] * n_out,
        compiler_params=_vmem_params(dimension_semantics=("arbitrary",)),
    )(w, g, m, v)


def _gather_weights(wi16, wo16, cw):
    L = wi16.shape[0]
    hi_rows, ho_rows = D_MODEL // 2, GROUP // 2
    n_ici = 2 * L + 1
    n_fwd = 2 * L

    def body(wi_ref, wo_ref, cw_ref, *rest):
        wig = rest[0:L]
        wog = rest[L:2 * L]
        cwg = rest[2 * L]
        send_sems, recv_sems, loc_sems = rest[2 * L + 1:]
        x, y, c = _place()
        me_k = 2 * x + y
        sibling = (x, y, 1 - c)
        chips = _other_chips(x, y)

        def half_i(ref, blk):
            return ref.at[blk, pl.ds(c * hi_rows, hi_rows), :]

        def half_o(ref, blk):
            return ref.at[blk, pl.ds(c * ho_rows, ho_rows), :]

        def other_half_i(ref, blk):
            return ref.at[blk, pl.ds((1 - c) * hi_rows, hi_rows), :]

        def other_half_o(ref, blk):
            return ref.at[blk, pl.ds((1 - c) * ho_rows, ho_rows), :]

        local = []
        for l in range(L):
            local.append(pltpu.make_async_copy(wi_ref.at[l], wig[l].at[me_k], loc_sems.at[2 * l]))
            local.append(pltpu.make_async_copy(wo_ref.at[l], wog[l].at[me_k], loc_sems.at[2 * l + 1]))
        local.append(pltpu.make_async_copy(cw_ref, cwg.at[me_k], loc_sems.at[2 * L]))
        for cp in local:
            cp.start()

        def remote(src, dst, sem, to):
            return pltpu.make_async_remote_copy(src_ref=src, dst_ref=dst, send_sem=send_sems.at[sem],
                                                recv_sem=recv_sems.at[sem], device_id=to, device_id_type=MESH)

        sends = []
        for r, (px, py, _) in enumerate(chips):
            to = (px, py, c)
            for l in range(L):
                sends.append(remote(half_i(wi_ref, l), half_i(wig[l], me_k), r * n_ici + 2 * l, to))
                sends.append(remote(half_o(wo_ref, l), half_o(wog[l], me_k), r * n_ici + 2 * l + 1, to))
            sends.append(remote(cw_ref, cwg.at[me_k], r * n_ici + 2 * L, to))
        for cp in sends:
            cp.start()

        base = 3 * n_ici
        fwds = []
        for r, (px, py, pk) in enumerate(chips):
            for l in range(L):
                remote(half_i(wig[l], pk), half_i(wig[l], pk), r * n_ici + 2 * l, sibling).wait_recv()
                f = remote(half_i(wig[l], pk), half_i(wig[l], pk), base + r * n_fwd + 2 * l, sibling)
                f.start()
                fwds.append(f)
                remote(half_o(wog[l], pk), half_o(wog[l], pk), r * n_ici + 2 * l + 1, sibling).wait_recv()
                f = remote(half_o(wog[l], pk), half_o(wog[l], pk), base + r * n_fwd + 2 * l + 1, sibling)
                f.start()
                fwds.append(f)
            remote(cwg.at[pk], cwg.at[pk], r * n_ici + 2 * L, sibling).wait_recv()
        for r, (px, py, pk) in enumerate(chips):
            for l in range(L):
                remote(other_half_i(wig[l], pk), other_half_i(wig[l], pk), base + r * n_fwd + 2 * l, sibling).wait_recv()
                remote(other_half_o(wog[l], pk), other_half_o(wog[l], pk), base + r * n_fwd + 2 * l + 1, sibling).wait_recv()
        for cp in sends + fwds:
            cp.wait_send()
        for cp in local:
            cp.wait()

    n_sem = 3 * n_ici + 3 * n_fwd
    out_shape = ([jax.ShapeDtypeStruct((N_CHIPS, D_MODEL, COLS), BF16)] * L
                 + [jax.ShapeDtypeStruct((N_CHIPS, GROUP, D_MODEL), BF16)] * L
                 + [jax.ShapeDtypeStruct((N_CHIPS,) + cw.shape, F32)])
    outs = pl.pallas_call(
        body, name="gather_weights",
        in_specs=[ANY, ANY, ANY], out_specs=[ANY] * (2 * L + 1), out_shape=out_shape,
        scratch_shapes=[pltpu.SemaphoreType.DMA((n_sem,)), pltpu.SemaphoreType.DMA((n_sem,)),
                        pltpu.SemaphoreType.DMA((2 * L + 1,))],
        compiler_params=pltpu.CompilerParams(has_side_effects=True),
    )(wi16, wo16, cw)
    return outs[0:L], outs[L:2 * L], outs[2 * L]


def _swap_halves(l_arr, gwi, gwo, ri, ro):
    hi_rows, ho_rows = D_MODEL // 2, GROUP // 2

    def body(l_ref, gwi_ref, gwo_ref, ri_in, ro_in, ri_ref, ro_ref, send_sems, recv_sems):
        del ri_in, ro_in
        x, y, c = _place()
        l = l_ref[0]
        sibling = (x, y, 1 - c)
        cps = [
            pltpu.make_async_remote_copy(src_ref=gwi_ref.at[l, :, pl.ds((1 - c) * hi_rows, hi_rows), :],
                                         dst_ref=ri_ref.at[l], send_sem=send_sems.at[0], recv_sem=recv_sems.at[0],
                                         device_id=sibling, device_id_type=MESH),
            pltpu.make_async_remote_copy(src_ref=gwo_ref.at[l, :, pl.ds((1 - c) * ho_rows, ho_rows), :],
                                         dst_ref=ro_ref.at[l], send_sem=send_sems.at[1], recv_sem=recv_sems.at[1],
                                         device_id=sibling, device_id_type=MESH),
        ]
        for cp in cps:
            cp.start()
        for cp in cps:
            cp.wait()

    return pl.pallas_call(
        body, name="swap_halves",
        in_specs=[pl.BlockSpec(memory_space=pltpu.SMEM), ANY, ANY, ANY, ANY], out_specs=[ANY, ANY],
        out_shape=[jax.ShapeDtypeStruct(ri.shape, ri.dtype), jax.ShapeDtypeStruct(ro.shape, ro.dtype)],
        input_output_aliases={3: 0, 4: 1},
        scratch_shapes=[pltpu.SemaphoreType.DMA((2,)), pltpu.SemaphoreType.DMA((2,))],
        compiler_params=pltpu.CompilerParams(has_side_effects=True),
    )(l_arr, gwi, gwo, ri, ro)


def _add_halves(cl_arr, g, r, p, *, rows, cols, tr, name):
    nb = rows // tr

    def body(cl_ref, g_ref, r_ref, p_in, o_ref):
        del cl_ref, p_in
        o_ref[...] = (g_ref[...] + r_ref[...].astype(F32)).astype(o_ref.dtype)

    grid_spec = pltpu.PrefetchScalarGridSpec(
        num_scalar_prefetch=1, grid=(N_CHIPS, nb),
        in_specs=[pl.BlockSpec((None, None, tr, cols), lambda k, i, cl: (cl[1], k, cl[0] * nb + i, 0)),
                  pl.BlockSpec((None, None, tr, cols), lambda k, i, cl: (cl[1], k, i, 0)), ANY],
        out_specs=pl.BlockSpec((None, None, tr, cols), lambda k, i, cl: (cl[1], k, i, 0)))
    return pl.pallas_call(
        body, name=name, grid_spec=grid_spec,
        out_shape=jax.ShapeDtypeStruct(p.shape, p.dtype),
        input_output_aliases={3: 0},
        compiler_params=_vmem_params(dimension_semantics=("arbitrary",) * 2),
    )(cl_arr, g, r, p)


def _exchange_last(l_arr, p_i, p_o, sm, r_i, r_o):
    def body(l_ref, p_i_ref, p_o_ref, sm_ref, ri_in, ro_in, ri_ref, ro_ref, rsm_ref, send_sems, recv_sems, loc_sem):
        del ri_in, ro_in
        always = l_ref[0] >= 0
        _exchange_comm(always, always, l_ref[0], p_i_ref, p_o_ref, sm_ref, ri_ref, ro_ref, rsm_ref,
                       send_sems, recv_sems, loc_sem)

    return pl.pallas_call(
        body, name="exchange_last",
        in_specs=[pl.BlockSpec(memory_space=pltpu.SMEM)] + [ANY] * 5, out_specs=[ANY] * 3,
        out_shape=[jax.ShapeDtypeStruct(r_i.shape, r_i.dtype), jax.ShapeDtypeStruct(r_o.shape, r_o.dtype),
                   jax.ShapeDtypeStruct((N_DEV, SM_ROWS, GROUP), F32)],
        input_output_aliases={4: 0, 5: 1},
        scratch_shapes=[pltpu.SemaphoreType.DMA((13,)), pltpu.SemaphoreType.DMA((13,)), pltpu.SemaphoreType.DMA((1,))],
        compiler_params=pltpu.CompilerParams(has_side_effects=True),
    )(l_arr, p_i, p_o, sm, r_i, r_o)


def _sum_small(r_sms):
    L = len(r_sms)

    def body(*refs):
        o_ref = refs[L]
        for l in range(L):
            acc = refs[l][0]
            for d in range(1, N_DEV):
                acc = acc + refs[l][d]
            o_ref[l] = acc

    return pl.pallas_call(
        body, name="sum_small",
        out_shape=jax.ShapeDtypeStruct((L,) + r_sms[0].shape[1:], F32),
        compiler_params=_vmem_params(),
    )(*r_sms)


def _sum_chunks(kc_arr, p, r, *, rows, cols, tr, name):
    L = p.shape[0]
    nb = rows // tr

    def body(kc_ref, p_ref, r0_ref, r1_ref, r2_ref, o_ref):
        del kc_ref
        f = lambda ref: ref[...].astype(F32)
        o_ref[...] = ((f(p_ref) + f(r0_ref)) + f(r1_ref)) + f(r2_ref)

    def rspec(j):
        return pl.BlockSpec((None, None, tr, cols), lambda l, i, kc, _j=j: (_j, l, i, 0))

    grid_spec = pltpu.PrefetchScalarGridSpec(
        num_scalar_prefetch=1, grid=(L, nb),
        in_specs=[pl.BlockSpec((None, None, tr, cols), lambda l, i, kc: (l, kc[0], i, 0)), rspec(0), rspec(1), rspec(2)],
        out_specs=pl.BlockSpec((None, tr, cols), lambda l, i, kc: (l, kc[1] * nb + i, 0)))
    return pl.pallas_call(
        body, name=name, grid_spec=grid_spec,
        out_shape=jax.ShapeDtypeStruct((L, 2 * rows, cols), F32),
        compiler_params=_vmem_params(dimension_semantics=("arbitrary",) * 2),
    )(kc_arr, p, r, r, r)


def _share_result(gi, go):
    hi_rows, ho_rows = gi.shape[1] // 2, go.shape[1] // 2

    def body(gi_ref, go_ref, oi_ref, oo_ref, send_sems, recv_sems):
        del gi_ref, go_ref
        x, y, c = _place()
        sibling = (x, y, 1 - c)
        cps = []
        for j, (ref, n) in enumerate(((oi_ref, hi_rows), (oo_ref, ho_rows))):
            mine = ref.at[:, pl.ds(c * n, n), :]
            cps.append(pltpu.make_async_remote_copy(src_ref=mine, dst_ref=mine, send_sem=send_sems.at[j],
                                                    recv_sem=recv_sems.at[j], device_id=sibling, device_id_type=MESH))
        for cp in cps:
            cp.start()
        for j, (ref, n) in enumerate(((oi_ref, hi_rows), (oo_ref, ho_rows))):
            theirs = ref.at[:, pl.ds((1 - c) * n, n), :]
            pltpu.make_async_remote_copy(src_ref=theirs, dst_ref=theirs, send_sem=send_sems.at[j],
                                         recv_sem=recv_sems.at[j], device_id=sibling, device_id_type=MESH).wait_recv()
        for cp in cps:
            cp.wait_send()

    return pl.pallas_call(
        body, name="share_result",
        in_specs=[ANY, ANY], out_specs=[ANY, ANY],
        out_shape=[jax.ShapeDtypeStruct(gi.shape, F32), jax.ShapeDtypeStruct(go.shape, F32)],
        input_output_aliases={0: 0, 1: 1},
        scratch_shapes=[pltpu.SemaphoreType.DMA((2,)), pltpu.SemaphoreType.DMA((2,))],
        compiler_params=pltpu.CompilerParams(has_side_effects=True),
    )(gi, go)


WEIGHTS = ("ln_g", "ln_b", "w_in", "b_in", "conv_a_w", "conv_a_b", "norm_a_g", "norm_a_b", "conv_b_w", "pool_w",
           "pool_scale", "sgu_ln_g", "sgu_ln_b", "sgu_w", "sgu_bias", "w_out", "b_out")


def _pad_rows(a, rows):
    return jnp.pad(a, ((0, rows - a.shape[0]), (0, 0)))


def _indicator_consts():
    seg = jnp.where((jnp.arange(GROUP)[:, None] // HEAD) == (jnp.arange(GROUP)[None, :] // HEAD),
                    1.0 / HEAD, 0.0).astype(BF16)
    e4 = ((jnp.arange(GROUP)[:, None] // HEAD) == jnp.arange(128)[None, :]).astype(BF16)
    return seg, e4


def _layer_consts(p, conv_full, l):
    same_head = jnp.eye(4, dtype=F32)[:, None, :, None] > 0
    caw = _pad_rows(conv_full[l, :KA], 32)
    cbw = _pad_rows(conv_full[l, KA:], 8)
    s256 = _pad_rows(jnp.stack([p["conv_a_b"][l], p["norm_a_g"][l], p["norm_a_b"][l], p["pool_scale"][l],
                                p["sgu_ln_g"][l], p["sgu_ln_b"][l]]), 8)
    pw = jnp.where(same_head, p["pool_w"][l][:, :, None, :], 0.0).reshape(GROUP, GROUP).astype(BF16)
    wm = jnp.transpose(p["sgu_w"][l], (1, 0, 2)).reshape(SGU_BLOCK, 4 * SGU_BLOCK)
    wmt = jnp.transpose(p["sgu_w"][l], (0, 2, 1)).reshape(4 * SGU_BLOCK, SGU_BLOCK)
    sb = jnp.repeat(p["sgu_bias"][l].T, HEAD, axis=1)
    v1024 = _pad_rows(jnp.stack([p["b_out"][l], p["ln_g"][l], p["ln_b"][l]]), 8)
    return dict(caw=caw, cbw=cbw, s256=s256, pw=pw, wm=wm, wmt=wmt, sb=sb, v1024=v1024, bin=p["b_in"][l][None, :])


def _unpack_small(sm):
    L = sm.shape[0]
    owc = jnp.concatenate([sm[:, ROW_WC:ROW_WC + SGU_BLOCK], sm[:, ROW_WC + SGU_BLOCK:ROW_WC + 2 * SGU_BLOCK]], axis=2)
    return dict(
        conv_a_b=sm[:, 0], norm_a_g=sm[:, 1], norm_a_b=sm[:, 2], pool_scale=sm[:, 3], sgu_ln_g=sm[:, 4],
        sgu_ln_b=sm[:, 5], conv_b_w=sm[:, ROW_CBW:ROW_CBW + KB], conv_a_w=sm[:, ROW_CAW:ROW_CAW + KA],
        pool_w=jnp.transpose(sm[:, ROW_PW:ROW_PW + HEAD].reshape(L, HEAD, 4, HEAD), (0, 2, 1, 3)),
        ln_g=sm[:, ROW_LNG:ROW_LNG + 4].reshape(L, D_MODEL), ln_b=sm[:, ROW_LNB:ROW_LNB + 4].reshape(L, D_MODEL),
        b_out=sm[:, ROW_BOUT:ROW_BOUT + 4].reshape(L, D_MODEL),
        b_in=sm[:, ROW_BIN:ROW_BIN + N_SLICES].reshape(L, IN_WIDTH),
        sgu_w=jnp.transpose(owc.reshape(L, SGU_BLOCK, 4, SGU_BLOCK), (0, 2, 1, 3)),
        sgu_bias=jnp.transpose(sm[:, ROW_SB:ROW_SB + SGU_BLOCK, 0:4], (0, 2, 1)))


def _step(p, m, v, x, target, *, tile_f, tile_b, tk):
    L = p["ln_g"].shape[0]
    xi, yi, ci = _place()
    me_k = 2 * xi + yi
    hi_rows, ho_rows = D_MODEL // 2, GROUP // 2

    cw = jnp.concatenate([p["conv_a_w"], p["conv_b_w"]], axis=1).reshape(-1, 128)
    cw_rows = cw.shape[0]
    cw = _pad_rows(cw, 72)
    wi16 = p["w_in"].astype(BF16)
    wo16 = p["w_out"].astype(BF16)
    wig0, wog0, cwg = _gather_weights(wi16[0:1], wo16[0:1], cw)
    cwg = cwg[:, :cw_rows].reshape(N_CHIPS, L, KA + KB, HEAD)
    conv_full = jnp.transpose(cwg, (1, 2, 0, 3)).reshape(L, KA + KB, GROUP)
    seg, e4 = _indicator_consts()
    consts = [_layer_consts(p, conv_full, l) for l in range(L)]

    hcur = x
    saved, wig, wog = [], [wig0[0]], [wog0[0]]
    for l in range(L):
        k = consts[l]
        nxt = (wi16[l + 1], wo16[l + 1]) if l + 1 < L else None
        outs = _fwd_layer(hcur, wig[l], k["bin"], k["caw"], k["cbw"], k["s256"], seg, k["pw"], k["wm"], k["sb"], wog[l],
                          k["v1024"], tile=tile_f, nxt=nxt, target=None if nxt is not None else target)
        y, xb, h, aux, mixb, z = outs[0:6]
        if nxt is not None:
            wig.append(outs[6])
            wog.append(outs[7])
        saved.append((xb, h, aux, mixb, z))
        hcur = y

    dy = hcur
    loss_local = outs[6][0, 0]

    gwi = lax.empty((L, N_CHIPS, D_MODEL, COLS), F32)
    gwo = lax.empty((L, N_CHIPS, GROUP, D_MODEL), F32)
    gwi16 = lax.empty((L, N_CHIPS, D_MODEL, COLS), BF16)
    gwo16 = lax.empty((L, N_CHIPS, GROUP, D_MODEL), BF16)
    ri = lax.empty((L, N_CHIPS, hi_rows, COLS), BF16)
    ro = lax.empty((L, N_CHIPS, ho_rows, D_MODEL), BF16)
    p_i = lax.empty((L, N_CHIPS, hi_rows, COLS), BF16)
    p_o = lax.empty((L, N_CHIPS, ho_rows, D_MODEL), BF16)
    q_i = lax.empty((3, L, hi_rows, COLS), BF16)
    q_o = lax.empty((3, L, ho_rows, D_MODEL), BF16)
    r_sm = [None] * L
    pending = None
    for l in reversed(range(L)):
        k = consts[l]
        xb, h, aux, mixb, z = saved[l]
        exch = None if pending is None else (pending[0], p_i, p_o, pending[1], q_i, q_o)
        outs = _bwd_layer(dy, z, h, aux, wig[l], k["caw"], k["cbw"], k["s256"], seg, k["pw"], k["wm"], k["wmt"],
                          k["sb"], wog[l], k["v1024"], e4, tile=tile_b, exch=exch)
        dy, dhb, dzb, osm = outs[0:4]
        if l == L - 1:
            osm = osm.at[ROW_LOSS, 0].set(loss_local)
        if exch is not None:
            q_i, q_o, r_sm[l + 1] = outs[4:7]
        larr = jnp.full((1,), l, jnp.int32)
        gwi, gwi16 = _dw_in(larr, xb, dhb, gwi, gwi16, tk=tk)
        gwo, gwo16 = _dw_out(larr, mixb, dzb, gwo, gwo16, tk=tk)
        ri, ro = _swap_halves(larr, gwi16, gwo16, ri, ro)
        cl_arr = jnp.stack([ci, jnp.int32(l)]).astype(jnp.int32)
        p_i = _add_halves(cl_arr, gwi, ri, p_i, rows=hi_rows, cols=COLS, tr=256, name="add_halves_in")
        p_o = _add_halves(cl_arr, gwo, ro, p_o, rows=ho_rows, cols=D_MODEL, tr=128, name="add_halves_out")
        pending = (larr, osm)
    grad_x = dy
    q_i, q_o, r_sm[0] = _exchange_last(pending[0], p_i, p_o, pending[1], q_i, q_o)

    summed = _sum_small(r_sm)
    loss = summed[L - 1, ROW_LOSS, 0]
    grads = _unpack_small(summed)
    for n in ("conv_a_w", "conv_b_w"):
        grads[n] = lax.dynamic_slice_in_dim(grads[n], me_k * HEAD, HEAD, axis=2)

    kc_arr = jnp.stack([me_k, ci]).astype(jnp.int32)
    g_i = _sum_chunks(kc_arr, p_i, q_i, rows=hi_rows, cols=COLS, tr=256, name="sum_chunks_in")
    g_o = _sum_chunks(kc_arr, p_o, q_o, rows=ho_rows, cols=D_MODEL, tr=128, name="sum_chunks_out")
    g_i, g_o = _share_result(g_i, g_o)
    grads["w_in"] = g_i
    grads["w_out"] = g_o

    delta, new_m, new_v = {}, {}, {}
    for n in WEIGHTS:
        shp = p[n].shape
        if n in ("w_in", "w_out"):
            two_d = (shp[0] * shp[1], shp[2])
            tr = 512 if n == "w_in" else 256
        else:
            two_d = (-1, shp[-1])
            tr = None
        args = [a.reshape(two_d) for a in (p[n], grads[n], m[n], v[n])]
        outs = _adamw(*args, rows_per_step=tr or args[0].shape[0], name="adamw_" + n, copy_g=tr is not None)
        delta[n], new_m[n], new_v[n] = (a.reshape(shp) for a in outs[0:3])
        if tr is not None:
            grads[n] = outs[3].reshape(shp)

    return (loss, grad_x[None], *[grads[n] for n in WEIGHTS], *[delta[n] for n in WEIGHTS],
            *[new_m[n] for n in WEIGHTS], *[new_v[n] for n in WEIGHTS])


def kernel(x, ln_g, ln_b, w_in, b_in, conv_a_w, conv_a_b, norm_a_g, norm_a_b, conv_b_w, pool_w, pool_scale, sgu_ln_g, sgu_ln_b, sgu_w, sgu_bias, w_out, b_out, loss_target, m_ln_g, m_ln_b, m_w_in, m_b_in, m_conv_a_w, m_conv_a_b, m_norm_a_g, m_norm_a_b, m_conv_b_w, m_pool_w, m_pool_scale, m_sgu_ln_g, m_sgu_ln_b, m_sgu_w, m_sgu_bias, m_w_out, m_b_out, v_ln_g, v_ln_b, v_w_in, v_b_in, v_conv_a_w, v_conv_a_b, v_norm_a_g, v_norm_a_b, v_conv_b_w, v_pool_w, v_pool_scale, v_sgu_ln_g, v_sgu_ln_b, v_sgu_w, v_sgu_bias, v_w_out, v_b_out):
    p = dict(ln_g=ln_g, ln_b=ln_b, w_in=w_in, b_in=b_in, conv_a_w=conv_a_w, conv_a_b=conv_a_b, norm_a_g=norm_a_g,
             norm_a_b=norm_a_b, conv_b_w=conv_b_w, pool_w=pool_w, pool_scale=pool_scale, sgu_ln_g=sgu_ln_g,
             sgu_ln_b=sgu_ln_b, sgu_w=sgu_w, sgu_bias=sgu_bias, w_out=w_out, b_out=b_out)
    m = dict(ln_g=m_ln_g, ln_b=m_ln_b, w_in=m_w_in, b_in=m_b_in, conv_a_w=m_conv_a_w, conv_a_b=m_conv_a_b,
             norm_a_g=m_norm_a_g, norm_a_b=m_norm_a_b, conv_b_w=m_conv_b_w, pool_w=m_pool_w, pool_scale=m_pool_scale,
             sgu_ln_g=m_sgu_ln_g, sgu_ln_b=m_sgu_ln_b, sgu_w=m_sgu_w, sgu_bias=m_sgu_bias, w_out=m_w_out, b_out=m_b_out)
    v = dict(ln_g=v_ln_g, ln_b=v_ln_b, w_in=v_w_in, b_in=v_b_in, conv_a_w=v_conv_a_w, conv_a_b=v_conv_a_b,
             norm_a_g=v_norm_a_g, norm_a_b=v_norm_a_b, conv_b_w=v_conv_b_w, pool_w=v_pool_w, pool_scale=v_pool_scale,
             sgu_ln_g=v_sgu_ln_g, sgu_ln_b=v_sgu_ln_b, sgu_w=v_sgu_w, sgu_bias=v_sgu_bias, w_out=v_w_out, b_out=v_b_out)
    return _step(p, m, v, x[0], loss_target[0], tile_f=256, tile_b=256, tk=2048)
```

```python
import functools

import jax
import jax.numpy as jnp
from jax import lax
from jax.experimental import pallas as pl
from jax.experimental.pallas import tpu as pltpu

F32 = jnp.float32
BF16 = jnp.bfloat16
MESH = pl.DeviceIdType.MESH

D_MODEL = 1024
GROUP = 256
HEAD = 64
N_SLICES = 12
IN_WIDTH = N_SLICES * GROUP
N_CHIPS = 4
COLS = IN_WIDTH // N_CHIPS
KA = 31
KB = 3
HALO_A, HALO_B, HALO_C = 32, 8, 16
POOL_WINDOWS = (2, 4, 8, 16)
SGU_BLOCK = 128
CHUNK = 64
LN_EPS = 1e-5
ROWS = 64
V7X_VMEM_BYTES = 64 * 1024 * 1024
VMEM_LIMIT = 56 * 1024 * 1024

ADAM_LR, ADAM_B1, ADAM_B2, ADAM_EPS, ADAM_WD, ADAM_STEP = 0.001, 0.9, 0.999, 1e-08, 0.01, 10


ANY = pl.BlockSpec(memory_space=pl.ANY)


def _vmem_params(**kw):
    return pltpu.CompilerParams(vmem_limit_bytes=VMEM_LIMIT, **kw)


def _place():
    return lax.axis_index("x"), lax.axis_index("y"), lax.axis_index("c")


def _other_chips(x, y):
    return [(1 - x, y, 2 * (1 - x) + y), (x, 1 - y, 2 * x + (1 - y)), (1 - x, 1 - y, 2 * (1 - x) + (1 - y))]


def _sig(v):
    return 0.5 * jnp.tanh(0.5 * v) + 0.5


def _dot(a, b):
    return jnp.dot(a, b, preferred_element_type=F32)


def _dot_nt(a, b):
    return lax.dot_general(a, b, (((1,), (1,)), ((), ())), preferred_element_type=F32)


def _dot_tn(a, b):
    return lax.dot_general(a, b, (((0,), (0,)), ((), ())), preferred_element_type=F32)


def _segdot(v, m):
    hi = v.astype(BF16)
    lo = (v - hi.astype(F32)).astype(BF16)
    return _dot(hi, m) + _dot(lo, m)


def _colsum(v):
    return jnp.sum(v, axis=0, keepdims=True)


def _rowmean(v):
    return jnp.mean(v, axis=-1, keepdims=True)


def _lane_group(n):
    return lax.broadcasted_iota(jnp.int32, (1, n), 1) // HEAD


def _pool_cnt(tile, t_rows):
    pos = tile * t_rows + lax.broadcasted_iota(jnp.int32, (t_rows, GROUP), 0) + 1
    grp = lax.broadcasted_iota(jnp.int32, (t_rows, GROUP), 1) // HEAD
    win = jnp.where(grp == 0, 2, jnp.where(grp == 1, 4, jnp.where(grp == 2, 8, 16)))
    return jnp.minimum(pos, win).astype(F32)


def _sgu_masks(wm_ref, wmt_ref, wm_s, wmt_s):
    r = lax.broadcasted_iota(jnp.int32, (SGU_BLOCK, 4 * SGU_BLOCK), 0) // CHUNK
    c = (lax.broadcasted_iota(jnp.int32, (SGU_BLOCK, 4 * SGU_BLOCK), 1) % SGU_BLOCK) // CHUNK
    wm_s[...] = jnp.where(c <= r, wm_ref[...], 0.0).astype(BF16)
    if wmt_ref is not None:
        rt = (lax.broadcasted_iota(jnp.int32, (4 * SGU_BLOCK, SGU_BLOCK), 0) % SGU_BLOCK) // CHUNK
        ct = lax.broadcasted_iota(jnp.int32, (4 * SGU_BLOCK, SGU_BLOCK), 1) // CHUNK
        wmt_s[...] = jnp.where(rt <= ct, wmt_ref[...], 0.0).astype(BF16)


def _vstack(v_blk):
    grp = _lane_group(GROUP)
    return jnp.concatenate([jnp.where(grp == h, v_blk, 0.0) for h in range(4)], axis=0).astype(BF16)


def _gather_next(step, nt, nwi, nwo, gwi, gwo, send_sems, recv_sems, loc_sems):
    x, y, c = _place()
    me_k = 2 * x + y
    sibling = (x, y, 1 - c)
    chips = _other_chips(x, y)
    hi, ho = D_MODEL // 2, GROUP // 2

    def rc(src, dst, sem, to):
        return pltpu.make_async_remote_copy(src_ref=src, dst_ref=dst, send_sem=send_sems.at[sem],
                                            recv_sem=recv_sems.at[sem], device_id=to, device_id_type=MESH)

    def blk(ref, k, n, cc):
        return ref.at[k, pl.ds(cc * n, n), :]

    def ici(r):
        px, py, _ = chips[r]
        to = (px, py, c)
        return [rc(nwi.at[pl.ds(c * hi, hi), :], blk(gwi, me_k, hi, c), 2 * r, to),
                rc(nwo.at[pl.ds(c * ho, ho), :], blk(gwo, me_k, ho, c), 2 * r + 1, to)]

    def landed(r, cc, base):
        pk = chips[r][2]
        return [rc(blk(gwi, pk, hi, cc), blk(gwi, pk, hi, cc), base + 2 * r, sibling),
                rc(blk(gwo, pk, ho, cc), blk(gwo, pk, ho, cc), base + 2 * r + 1, sibling)]

    def local():
        return [pltpu.make_async_copy(nwi, gwi.at[me_k], loc_sems.at[0]),
                pltpu.make_async_copy(nwo, gwo.at[me_k], loc_sems.at[1])]

    @pl.when(step == 0)
    def _():
        for cp in local():
            cp.start()
        for r in range(3):
            for cp in ici(r):
                cp.start()

    @pl.when(step == (3 * nt) // 4)
    def _():
        for r in range(3):
            for got, fwd in zip(landed(r, c, 0), landed(r, c, 6)):
                got.wait_recv()
                fwd.start()

    @pl.when(step == nt - 1)
    def _():
        for r in range(3):
            for got in landed(r, 1 - c, 6):
                got.wait_recv()
        for r in range(3):
            for cp in ici(r) + landed(r, c, 6):
                cp.wait_send()
        for cp in local():
            cp.wait()


def _fwd_layer(x, wi, bin_, caw, cbw, s256, seg, pw, wm, sb, wo, v1024, *, tile, nxt=None, target=None):
    assert nxt is None or target is None
    S = x.shape[0]
    T = tile
    nt = S // T
    alpha = float((2.0 * 4) ** 0.25)
    n_in = 12 + (2 if nxt is not None else 0) + (1 if target is not None else 0)
    n_out = 6 + (2 if nxt is not None else 0) + (1 if target is not None else 0)

    def body(*refs):
        (x_ref, wi_ref, bin_ref, caw_ref, cbw_ref, s256_ref, seg_ref, pw_ref, wm_ref, sb_ref, wo_ref,
         v1024_ref) = refs[0:12]
        y_ref, xb_ref, h_ref, aux_ref, mix_ref, z_ref = refs[n_in:n_in + 6]
        abuf, bbuf, cbuf, wm_s = refs[n_in + n_out:n_in + n_out + 4]
        i = pl.program_id(0)
        if nxt is not None:
            _gather_next(i, nt, refs[12], refs[13], refs[n_in + 6], refs[n_in + 7], *refs[n_in + n_out + 4:])

        @pl.when(i == 0)
        def _():
            abuf[0:HALO_A, :] = jnp.zeros((HALO_A, GROUP), F32)
            bbuf[0:HALO_B, :] = jnp.zeros((HALO_B, GROUP), F32)
            cbuf[0:HALO_C, :] = jnp.zeros((HALO_C, GROUP), F32)
            _sgu_masks(wm_ref, None, wm_s, None)

        x = x_ref[...]
        xb = x.astype(BF16)
        xb_ref[...] = xb
        for k in range(N_CHIPS):
            h_ref[:, COLS * k:COLS * (k + 1)] = _dot(xb, wi_ref[k]) + bin_ref[:, COLS * k:COLS * (k + 1)]

        def hs(j):
            return h_ref[:, GROUP * j:GROUP * (j + 1)]

        abuf[HALO_A:HALO_A + T, :] = hs(0) * _sig(hs(1))
        for r0 in range(0, T, ROWS):
            acc = None
            for k in range(KA):
                off = HALO_A - (KA - 1) + k + r0
                term = caw_ref[k:k + 1, :] * abuf[off:off + ROWS, :]
                acc = term if acc is None else acc + term
            aux_ref[r0:r0 + ROWS, 0:GROUP] = acc + s256_ref[0:1, :]
        abuf[0:HALO_A, :] = abuf[T:T + HALO_A, :]
        a1 = aux_ref[:, 0:GROUP]
        segm = seg_ref[...]
        cen = a1 - _segdot(a1, segm)
        var = _segdot(cen * cen, segm)
        a2 = cen * lax.rsqrt(var + LN_EPS) * s256_ref[1:2, :] + s256_ref[2:3, :]
        az = hs(2)
        mix_ref[:, 0:GROUP] = (a2 * _sig(a2) * (az * _sig(az))).astype(BF16)

        bbuf[HALO_B:HALO_B + T, :] = hs(4) * hs(5)
        for r0 in range(0, T, ROWS):
            acc = None
            for k in range(KB):
                off = HALO_B - (KB - 1) + k + r0
                term = cbw_ref[k:k + 1, :] * bbuf[off:off + ROWS, :]
                acc = term if acc is None else acc + term
            aux_ref[r0:r0 + ROWS, GROUP:2 * GROUP] = acc
        bbuf[0:HALO_B, :] = bbuf[T:T + HALO_B, :]
        bz = hs(6)
        mix_ref[:, GROUP:2 * GROUP] = (hs(3) * aux_ref[:, GROUP:2 * GROUP] * (bz * _sig(bz))).astype(BF16)

        ch = hs(7)
        cbuf[HALO_C:HALO_C + T, :] = ch
        hi_lane = (lax.broadcasted_iota(jnp.int32, (1, 128), 1) // HEAD) == 1
        for r0 in range(0, T, ROWS):
            def win(col, j0, j1):
                s = None
                for j in range(j0, j1):
                    off = HALO_C - j + r0
                    term = cbuf[off:off + ROWS, 128 * col:128 * (col + 1)]
                    s = term if s is None else s + term
                return s
            w0 = win(0, 0, 2) + jnp.where(hi_lane, win(0, 2, 4), 0.0)
            w1 = win(1, 0, 8) + jnp.where(hi_lane, win(1, 8, 16), 0.0)
            aux_ref[r0:r0 + ROWS, 2 * GROUP:2 * GROUP + 128] = w0
            aux_ref[r0:r0 + ROWS, 2 * GROUP + 128:3 * GROUP] = w1
        cbuf[0:HALO_C, :] = cbuf[T:T + HALO_C, :]
        pooled = aux_ref[:, 2 * GROUP:3 * GROUP] / _pool_cnt(i, T) - ch
        aux_ref[:, 2 * GROUP:3 * GROUP] = pooled
        q = _dot(pooled.astype(BF16), pw_ref[...])
        cz = hs(8)
        mix_ref[:, 2 * GROUP:3 * GROUP] = (q * s256_ref[3:4, :] * (cz * _sig(cz))).astype(BF16)

        dv = hs(10)
        cen = dv - _rowmean(dv)
        var = _rowmean(cen * cen)
        v = cen * lax.rsqrt(var + LN_EPS) * s256_ref[4:5, :] + s256_ref[5:6, :]
        sps = []
        for n in range(T // SGU_BLOCK):
            vb = v[n * SGU_BLOCK:(n + 1) * SGU_BLOCK, :]
            sps.append(_dot(wm_s[...], _vstack(vb)) + sb_ref[...])
        sp = jnp.concatenate(sps, axis=0)
        dz = hs(11)
        mix_ref[:, 3 * GROUP:4 * GROUP] = (hs(9) * sp * (dz * _sig(dz))).astype(BF16)

        out = v1024_ref[0:1, :]
        for k in range(N_CHIPS):
            out = out + _dot(mix_ref[:, GROUP * k:GROUP * (k + 1)], wo_ref[k])
        z = alpha * x + out
        z_ref[...] = z
        cen = z - _rowmean(z)
        var = _rowmean(cen * cen)
        y = cen * lax.rsqrt(var + LN_EPS) * v1024_ref[1:2, :] + v1024_ref[2:3, :]
        if target is None:
            y_ref[...] = y
        else:
            t_ref, loss_ref = refs[12], refs[n_in + 6]

            @pl.when(i == 0)
            def _():
                loss_ref[...] = jnp.zeros_like(loss_ref)
            err = y - t_ref[...]
            y_ref[...] = err * (1.0 / D_MODEL)
            loss_ref[...] += jnp.sum(_colsum(err * err), axis=1, keepdims=True) * (0.5 / D_MODEL)

    def full(a):
        nd = a.ndim
        return pl.BlockSpec(a.shape, lambda i, _n=nd: (0,) * _n)

    def rows(width):
        return pl.BlockSpec((T, width), lambda i: (i, 0))

    consts = (wi, bin_, caw, cbw, s256, seg, pw, wm, sb, wo, v1024)
    in_specs = [rows(D_MODEL)] + [full(a) for a in consts]
    out_specs = [rows(D_MODEL), rows(D_MODEL), rows(IN_WIDTH), rows(3 * GROUP), rows(D_MODEL), rows(D_MODEL)]
    out_shape = [jax.ShapeDtypeStruct((S, D_MODEL), F32), jax.ShapeDtypeStruct((S, D_MODEL), BF16),
                 jax.ShapeDtypeStruct((S, IN_WIDTH), F32), jax.ShapeDtypeStruct((S, 3 * GROUP), F32),
                 jax.ShapeDtypeStruct((S, D_MODEL), BF16), jax.ShapeDtypeStruct((S, D_MODEL), F32)]
    scratch = [pltpu.VMEM((T + HALO_A, GROUP), F32), pltpu.VMEM((T + HALO_B, GROUP), F32),
               pltpu.VMEM((T + HALO_C, GROUP), F32), pltpu.VMEM((SGU_BLOCK, 4 * SGU_BLOCK), BF16)]
    extra = ()
    if nxt is not None:
        extra = tuple(nxt)
        in_specs += [ANY, ANY]
        out_specs += [ANY, ANY]
        out_shape += [jax.ShapeDtypeStruct((N_CHIPS, D_MODEL, COLS), BF16),
                      jax.ShapeDtypeStruct((N_CHIPS, GROUP, D_MODEL), BF16)]
        scratch += [pltpu.SemaphoreType.DMA((12,)), pltpu.SemaphoreType.DMA((12,)), pltpu.SemaphoreType.DMA((2,))]
    if target is not None:
        extra = (target,)
        in_specs += [rows(D_MODEL)]
        out_specs += [pl.BlockSpec((8, 128), lambda i: (0, 0))]
        out_shape += [jax.ShapeDtypeStruct((8, 128), F32)]
    return pl.pallas_call(
        body, name=("fwd_layer_loss" if target is not None else "fwd_layer") if nxt is None else "fwd_layer_gather",
        grid=(nt,), in_specs=in_specs, out_specs=out_specs, out_shape=out_shape, scratch_shapes=scratch,
        compiler_params=_vmem_params(dimension_semantics=("arbitrary",), has_side_effects=nxt is not None),
    )(x, *consts, *extra)


ROW_CBW = 8
ROW_CAW = 16
ROW_LOSS = 7
ROW_PW = 48
ROW_LNG = 112
ROW_LNB = 116
ROW_BOUT = 120
ROW_BIN = 124
ROW_WC = 136
ROW_SB = 392
SM_ROWS = 520
N_DEV = 8


def _exchange_comm(start, finish, l, p_i, p_o, sm, r_i, r_o, r_sm, send_sems, recv_sems, loc_sem):
    x, y, c = _place()
    me = 4 * x + 2 * y + c
    chips = _other_chips(x, y)

    def rc(src, dst, sem, to):
        return pltpu.make_async_remote_copy(src_ref=src, dst_ref=dst, send_sem=send_sems.at[sem],
                                            recv_sem=recv_sems.at[sem], device_id=to, device_id_type=MESH)

    def big(r):
        px, py, pk = chips[r]
        to = (px, py, c)
        return [rc(p_i.at[l, pk], r_i.at[r, l], 2 * r, to), rc(p_o.at[l, pk], r_o.at[r, l], 2 * r + 1, to)]

    def peer(rel):
        px = 1 - x if rel & 4 else x
        py = 1 - y if rel & 2 else y
        pc = 1 - c if rel & 1 else c
        return (px, py, pc), 4 * px + 2 * py + pc

    def small_out(rel):
        to, _ = peer(rel)
        return rc(sm, r_sm.at[me], 5 + rel, to)

    def small_in(rel):
        to, idx = peer(rel)
        return rc(sm, r_sm.at[idx], 5 + rel, to)

    def local():
        return pltpu.make_async_copy(sm, r_sm.at[me], loc_sem.at[0])

    @pl.when(start)
    def _():
        local().start()
        for r in range(3):
            for cp in big(r):
                cp.start()
        for rel in range(1, N_DEV):
            small_out(rel).start()

    @pl.when(finish)
    def _():
        for r in range(3):
            for cp in big(r):
                cp.wait()
        for rel in range(1, N_DEV):
            small_in(rel).wait_recv()
            small_out(rel).wait_send()
        local().wait()


RC = 32
RC_WIDE = 16
ACC_ROWS = 136


def _rsum8(v):
    r = v[0:8]
    for j in range(1, v.shape[0] // 8):
        r = r + v[8 * j:8 * j + 8]
    return r


def _bwd_layer(dy, z, h, aux, wi, caw, cbw, s256, seg, pw, wm, wmt, sb, wo, v1024, e4, *, tile, exch=None):
    S = dy.shape[0]
    T = tile
    nt = S // T
    nblk = T // SGU_BLOCK
    alpha = float((2.0 * 4) ** 0.25)
    n_in = 16 + (6 if exch is not None else 0)
    n_out = 4 + (3 if exch is not None else 0)
    slab = pltpu.VMEM((T, GROUP), F32)
    scratch = dict(
        dbuf=pltpu.VMEM((T + HALO_A, GROUP), F32), ebuf=pltpu.VMEM((T + HALO_B, GROUP), F32),
        fbuf=pltpu.VMEM((T + HALO_C, GROUP), F32), sh=pltpu.VMEM((7, T + HALO_A - 8, GROUP), F32),
        wm_s=pltpu.VMEM((SGU_BLOCK, 4 * SGU_BLOCK), BF16), wmt_s=pltpu.VMEM((4 * SGU_BLOCK, SGU_BLOCK), BF16),
        dsp_acc=pltpu.VMEM((SGU_BLOCK, GROUP), F32), pw_acc=pltpu.VMEM((GROUP, GROUP), F32),
        acc_s=pltpu.VMEM((8 * ACC_ROWS, GROUP), F32), acc_w=pltpu.VMEM((24, D_MODEL), F32),
        dmix_s=pltpu.VMEM((T, D_MODEL), F32), vst_s=pltpu.VMEM((nblk, 4 * SGU_BLOCK, GROUP), BF16),
        dq_s=pltpu.VMEM((T, GROUP), BF16), dxt_s=pltpu.VMEM((D_MODEL, T), F32),
        mean_s=slab, t1_s=slab, t2_s=slab, q_s=slab, xv_s=slab, rv_s=slab, v_s=slab, sp_s=slab, a0_s=slab, sg_s=slab,
        xh_s=slab, ra_s=slab, ub_s=slab, dsp_s=slab, m1_s=slab, m2_s=slab, dpool_s=slab, dvd_s=slab, u_s=slab,
        du_s=slab, cw_s=slab)
    names = list(scratch)

    def body(*refs):
        (dy_ref, z_ref, h_ref, aux_ref, wi_ref, caw_ref, cbw_ref, s256_ref, seg_ref, pw_ref, wm_ref, wmt_ref,
         sb_ref, wo_ref, v1024_ref, e4_ref) = refs[0:16]
        dx_ref, dhb_ref, dzb_ref, osm_ref = refs[n_in:n_in + 4]
        k0 = n_in + n_out
        sc = dict(zip(names, refs[k0:k0 + len(names)]))
        dbuf, ebuf, fbuf, sh = sc["dbuf"], sc["ebuf"], sc["fbuf"], sc["sh"]
        wm_s, wmt_s, dsp_acc, pw_acc, acc_s, acc_w = (sc[n] for n in ("wm_s", "wmt_s", "dsp_acc", "pw_acc", "acc_s",
                                                                        "acc_w"))
        dmix_s, vst_s, dq_s = sc["dmix_s"], sc["vst_s"], sc["dq_s"]
        i = pl.program_id(0)
        tile_idx = nt - 1 - i
        if exch is not None:
            l_ref, p_i, p_o, sm = refs[16:20]
            r_i, r_o, r_sm = refs[n_in + 4:n_in + 7]
            _exchange_comm(i == 0, i == nt - 1, l_ref[0], p_i, p_o, sm, r_i, r_o, r_sm, *refs[k0 + len(names):])

        @pl.when(i == 0)
        def _():
            dbuf[T:T + HALO_A, :] = jnp.zeros((HALO_A, GROUP), F32)
            ebuf[T:T + HALO_B, :] = jnp.zeros((HALO_B, GROUP), F32)
            fbuf[T:T + HALO_C, :] = jnp.zeros((HALO_C, GROUP), F32)
            _sgu_masks(wm_ref, wmt_ref, wm_s, wmt_s)
            osm_ref[...] = jnp.zeros_like(osm_ref)
            dsp_acc[...] = jnp.zeros_like(dsp_acc)
            pw_acc[...] = jnp.zeros_like(pw_acc)
            acc_s[...] = jnp.zeros_like(acc_s)
            acc_w[...] = jnp.zeros_like(acc_w)

        def chunks(rc, fn):
            for c in range(T // rc):
                fn(pl.ds(c * rc, rc))

        def hs(j, rows):
            return h_ref[rows, GROUP * j:GROUP * (j + 1)]

        def acc_add(row, val):
            acc_s[8 * row:8 * row + 8, :] += _rsum8(val)

        def put_dh(j, rows, val):
            acc_add(ROW_BIN + j, val)
            dhb_ref[rows, GROUP * j:GROUP * (j + 1)] = val.astype(BF16)

        def dsilu(v, s):
            return s * (1.0 + v * (1.0 - s))

        def vec(r):
            return s256_ref[r:r + 1, :]

        def ln_bwd(rows):
            dyc = dy_ref[rows, :]
            zc = z_ref[rows, :]
            cen = zc - _rowmean(zc)
            rstd = lax.rsqrt(_rowmean(cen * cen) + LN_EPS)
            xhat = cen * rstd
            acc_w[0:8, :] += _rsum8(dyc * xhat)
            acc_w[8:16, :] += _rsum8(dyc)
            gdy = dyc * v1024_ref[1:2, :]
            dz = rstd * (gdy - _rowmean(gdy) - xhat * _rowmean(gdy * xhat))
            acc_w[16:24, :] += _rsum8(dz)
            dzb_ref[rows, :] = dz.astype(BF16)
            dx_ref[rows, :] = alpha * dz
        chunks(RC_WIDE, ln_bwd)

        segm = seg_ref[...]
        dzb = dzb_ref[...]
        for k in range(N_CHIPS):
            dmix_s[:, GROUP * k:GROUP * (k + 1)] = _dot_nt(dzb, wo_ref[k])
        sc["mean_s"][...] = _segdot(aux_ref[:, 0:GROUP], segm)
        pooled_b = aux_ref[:, 2 * GROUP:3 * GROUP].astype(BF16)
        sc["q_s"][...] = _dot(pooled_b, pw_ref[...])

        def centre(rows):
            cen = aux_ref[rows, 0:GROUP] - sc["mean_s"][rows, :]
            sc["t1_s"][rows, :] = cen * cen
            dv_in = hs(10, rows)
            cen_v = dv_in - _rowmean(dv_in)
            rstd_v = lax.rsqrt(_rowmean(cen_v * cen_v) + LN_EPS)
            xv = cen_v * rstd_v
            sc["xv_s"][rows, :] = xv
            sc["rv_s"][rows, :] = jnp.broadcast_to(rstd_v, xv.shape)
            sc["v_s"][rows, :] = xv * vec(4) + vec(5)
        chunks(RC, centre)

        sc["t2_s"][...] = _segdot(sc["t1_s"][...], segm)
        for n in range(nblk):
            blk = slice(n * SGU_BLOCK, (n + 1) * SGU_BLOCK)
            vst_s[n] = _vstack(sc["v_s"][blk, :])
            sc["sp_s"][blk, :] = _dot(wm_s[...], vst_s[n]) + sb_ref[...]

        def mixers(rows):
            a_val, a_glu, a_z = hs(0, rows), hs(1, rows), hs(2, rows)
            sg = _sig(a_glu)
            sc["a0_s"][rows, :] = a_val * sg
            sc["sg_s"][rows, :] = sg
            rstd_a = lax.rsqrt(sc["t2_s"][rows, :] + LN_EPS)
            xh = (aux_ref[rows, 0:GROUP] - sc["mean_s"][rows, :]) * rstd_a
            a2 = xh * vec(1) + vec(2)
            s2 = _sig(a2)
            sz = _sig(a_z)
            dya = dmix_s[rows, 0:GROUP]
            put_dh(2, rows, dya * (a2 * s2) * dsilu(a_z, sz))
            d_a2 = dya * (a_z * sz) * dsilu(a2, s2)
            acc_add(1, d_a2 * xh)
            acc_add(2, d_a2)
            gd = d_a2 * vec(1)
            sc["t1_s"][rows, :] = gd
            sc["t2_s"][rows, :] = gd * xh
            sc["xh_s"][rows, :] = xh
            sc["ra_s"][rows, :] = rstd_a
            b_b, b_c, b_h, b_z = hs(3, rows), hs(4, rows), hs(5, rows), hs(6, rows)
            cb = aux_ref[rows, GROUP:2 * GROUP]
            sz = _sig(b_z)
            dyb = dmix_s[rows, GROUP:2 * GROUP]
            put_dh(3, rows, dyb * cb * (b_z * sz))
            put_dh(6, rows, dyb * b_b * cb * dsilu(b_z, sz))
            ebuf[rows, :] = dyb * b_b * (b_z * sz)
            sc["ub_s"][rows, :] = b_c * b_h
            c_z = hs(8, rows)
            q = sc["q_s"][rows, :]
            sz = _sig(c_z)
            dyc = dmix_s[rows, 2 * GROUP:3 * GROUP]
            acc_add(3, dyc * q * (c_z * sz))
            put_dh(8, rows, dyc * q * vec(3) * dsilu(c_z, sz))
            dq_s[rows, :] = (dyc * vec(3) * (c_z * sz)).astype(BF16)
            d_u, d_z = hs(9, rows), hs(11, rows)
            sp = sc["sp_s"][rows, :]
            sz = _sig(d_z)
            dyd = dmix_s[rows, 3 * GROUP:4 * GROUP]
            put_dh(9, rows, dyd * sp * (d_z * sz))
            put_dh(11, rows, dyd * d_u * sp * dsilu(d_z, sz))
            sc["dsp_s"][rows, :] = dyd * d_u * (d_z * sz)
        chunks(RC, mixers)

        sc["m1_s"][...] = _segdot(sc["t1_s"][...], segm)
        sc["m2_s"][...] = _segdot(sc["t2_s"][...], segm)
        d_q = dq_s[...]
        pw_acc[...] += _dot_tn(pooled_b, d_q)
        sc["dpool_s"][...] = _dot_nt(d_q, pw_ref[...])
        grp = _lane_group(GROUP)
        for n in range(nblk):
            blk = slice(n * SGU_BLOCK, (n + 1) * SGU_BLOCK)
            dspb = sc["dsp_s"][blk, :]
            dsp_acc[...] += dspb
            dspb16 = dspb.astype(BF16)
            dvst = _dot(wmt_s[...], dspb16)
            dvb = None
            for hh in range(4):
                part = jnp.where(grp == hh, dvst[hh * SGU_BLOCK:(hh + 1) * SGU_BLOCK, :], 0.0)
                dvb = part if dvb is None else dvb + part
            sc["dvd_s"][blk, :] = dvb
            dwc = _dot_nt(dspb16, vst_s[n])
            osm_ref[ROW_WC:ROW_WC + SGU_BLOCK, :] += dwc[:, 0:GROUP]
            osm_ref[ROW_WC + SGU_BLOCK:ROW_WC + 2 * SGU_BLOCK, :] += dwc[:, GROUP:2 * GROUP]

        def ln_sums(rows):
            xh = sc["xh_s"][rows, :]
            d_a1 = sc["ra_s"][rows, :] * (sc["t1_s"][rows, :] - sc["m1_s"][rows, :] - xh * sc["m2_s"][rows, :])
            acc_add(0, d_a1)
            dbuf[rows, :] = d_a1
            pos = tile_idx * T + rows.start + lax.broadcasted_iota(jnp.int32, (RC, GROUP), 0) + 1
            lane = lax.broadcasted_iota(jnp.int32, (RC, GROUP), 1) // HEAD
            win = jnp.where(lane == 0, 2, jnp.where(lane == 1, 4, jnp.where(lane == 2, 8, 16)))
            fbuf[rows, :] = sc["dpool_s"][rows, :] / jnp.minimum(pos, win).astype(F32)
            d_v = sc["dvd_s"][rows, :]
            xv = sc["xv_s"][rows, :]
            acc_add(4, d_v * xv)
            acc_add(5, d_v)
            gd = d_v * vec(4)
            put_dh(10, rows, sc["rv_s"][rows, :] * (gd - _rowmean(gd) - xv * _rowmean(gd * xv)))
        chunks(RC, ln_sums)

        span = T + HALO_A - 8
        for p in range(1, 8):
            sh[p - 1, :, :] = dbuf[p:p + span, :]

        for r0 in range(0, T, ROWS):
            uc = sc["ub_s"][r0:r0 + ROWS, :]
            acc = None
            for k in range(KB):
                off = (KB - 1) - k + r0
                w = ebuf[off:off + ROWS, :]
                term = cbw_ref[k:k + 1, :] * w
                acc = term if acc is None else acc + term
                acc_add(ROW_CBW + k, uc * w)
            sc["du_s"][r0:r0 + ROWS, :] = acc
        ebuf[T:T + HALO_B, :] = ebuf[0:HALO_B, :]

        hi_lane = (lax.broadcasted_iota(jnp.int32, (1, 128), 1) // HEAD) == 1
        for r0 in range(0, T, ROWS):
            def win(col, j0, j1):
                s = None
                for j in range(j0, j1):
                    term = fbuf[r0 + j:r0 + j + ROWS, 128 * col:128 * (col + 1)]
                    s = term if s is None else s + term
                return s
            sc["cw_s"][r0:r0 + ROWS, 0:128] = win(0, 0, 2) + jnp.where(hi_lane, win(0, 2, 4), 0.0)
            sc["cw_s"][r0:r0 + ROWS, 128:256] = win(1, 0, 8) + jnp.where(hi_lane, win(1, 8, 16), 0.0)
        fbuf[T:T + HALO_C, :] = fbuf[0:HALO_C, :]

        def rest_bc(rows):
            d_u = sc["du_s"][rows, :]
            put_dh(4, rows, d_u * hs(5, rows))
            put_dh(5, rows, d_u * hs(4, rows))
            put_dh(7, rows, sc["cw_s"][rows, :] - sc["dpool_s"][rows, :])
        chunks(RC, rest_bc)

        dxt_s = sc["dxt_s"]

        def dx_term(k):
            term = _dot_nt(wi_ref[k], dhb_ref[:, COLS * k:COLS * (k + 1)])
            if k == 1:
                dxt_s[...] = term
            else:
                dxt_s[...] += term

        def conv_a(rows):
            a0c = sc["a0_s"][rows, :]
            acc = None
            for k in range(KA):
                off = (KA - 1) - k
                p, q8 = off % 8, off - off % 8
                w = dbuf[pl.ds(rows.start + q8, RC), :] if p == 0 else sh[p - 1, pl.ds(rows.start + q8, RC), :]
                term = caw_ref[k:k + 1, :] * w
                acc = term if acc is None else acc + term
                acc_add(ROW_CAW + k, a0c * w)
            sc["u_s"][rows, :] = acc
        n_chunks = T // RC
        after = {(n_chunks * j) // 3: j + 1 for j in range(3)}
        for c in range(n_chunks):
            conv_a(pl.ds(c * RC, RC))
            if c in after:
                dx_term(after[c])
        dbuf[T:T + HALO_A, :] = dbuf[0:HALO_A, :]

        def rest_a(rows):
            d_a0 = sc["u_s"][rows, :]
            sg = sc["sg_s"][rows, :]
            put_dh(0, rows, d_a0 * sg)
            put_dh(1, rows, d_a0 * hs(0, rows) * sg * (1.0 - sg))
        chunks(RC, rest_a)
        dx_term(0)
        dx_ref[...] += dxt_s[...].T

        @pl.when(i == nt - 1)
        def _():
            for row in list(range(6)) + list(range(ROW_CBW, ROW_CBW + KB)) + list(range(ROW_CAW, ROW_CAW + KA)) + list(
                    range(ROW_BIN, ROW_BIN + N_SLICES)):
                osm_ref[row:row + 1, :] = _colsum(acc_s[8 * row:8 * row + 8, :])
            for j, row in enumerate((ROW_LNG, ROW_LNB, ROW_BOUT)):
                cs = _colsum(acc_w[8 * j:8 * j + 8, :])
                for q in range(D_MODEL // GROUP):
                    osm_ref[row + q:row + q + 1, :] = cs[:, GROUP * q:GROUP * (q + 1)]
            r = lax.broadcasted_iota(jnp.int32, (SGU_BLOCK, GROUP), 0) // CHUNK
            c = (lax.broadcasted_iota(jnp.int32, (SGU_BLOCK, GROUP), 1) % SGU_BLOCK) // CHUNK
            for half in range(2):
                rows_ = slice(ROW_WC + half * SGU_BLOCK, ROW_WC + (half + 1) * SGU_BLOCK)
                osm_ref[rows_, :] = jnp.where(c <= r, osm_ref[rows_, :], 0.0)
            osm_ref[ROW_SB:ROW_SB + SGU_BLOCK, 0:128] = _segdot(dsp_acc[...], e4_ref[...])
            for g in range(4):
                osm_ref[ROW_PW:ROW_PW + HEAD, HEAD * g:HEAD * (g + 1)] = (
                    pw_acc[HEAD * g:HEAD * (g + 1), HEAD * g:HEAD * (g + 1)])

    def full(a):
        nd = a.ndim
        return pl.BlockSpec(a.shape, lambda i, _n=nd: (0,) * _n)

    def rows(width):
        return pl.BlockSpec((T, width), lambda i: (nt - 1 - i, 0))

    def acc(shape):
        return pl.BlockSpec(shape, lambda i: (0, 0))

    consts = (wi, caw, cbw, s256, seg, pw, wm, wmt, sb, wo, v1024, e4)
    in_specs = [rows(D_MODEL), rows(D_MODEL), rows(IN_WIDTH), rows(3 * GROUP)] + [full(a) for a in consts]
    out_specs = [rows(D_MODEL), rows(IN_WIDTH), rows(D_MODEL), acc((SM_ROWS, GROUP))]
    out_shape = [jax.ShapeDtypeStruct((S, D_MODEL), F32), jax.ShapeDtypeStruct((S, IN_WIDTH), BF16),
                 jax.ShapeDtypeStruct((S, D_MODEL), BF16), jax.ShapeDtypeStruct((SM_ROWS, GROUP), F32)]
    scratch_shapes = list(scratch.values())
    extra, aliases = (), {}
    if exch is not None:
        extra = tuple(exch)
        r_i, r_o = exch[4], exch[5]
        in_specs += [pl.BlockSpec(memory_space=pltpu.SMEM)] + [ANY] * 5
        out_specs += [ANY] * 3
        out_shape += [jax.ShapeDtypeStruct(r_i.shape, r_i.dtype), jax.ShapeDtypeStruct(r_o.shape, r_o.dtype),
                      jax.ShapeDtypeStruct((N_DEV, SM_ROWS, GROUP), F32)]
        scratch_shapes += [pltpu.SemaphoreType.DMA((13,)), pltpu.SemaphoreType.DMA((13,)),
                           pltpu.SemaphoreType.DMA((1,))]
        aliases = {20: 4, 21: 5}
    return pl.pallas_call(
        body, name="bwd_layer" if exch is None else "bwd_layer_exchange",
        grid=(nt,), in_specs=in_specs, out_specs=out_specs, out_shape=out_shape, scratch_shapes=scratch_shapes,
        input_output_aliases=aliases,
        compiler_params=_vmem_params(dimension_semantics=("arbitrary",), has_side_effects=exch is not None),
    )(dy, z, h, aux, *consts, *extra)


def _bwd_layer_slabwise(dy, z, h, aux, wi, caw, cbw, s256, seg, pw, wm, wmt, sb, wo, v1024, e4, *, tile, exch=None):
    S = dy.shape[0]
    T = tile
    nt = S // T
    alpha = float((2.0 * 4) ** 0.25)
    n_in = 16 + (6 if exch is not None else 0)
    n_out = 4 + (3 if exch is not None else 0)

    def body(*refs):
        (dy_ref, z_ref, h_ref, aux_ref, wi_ref, caw_ref, cbw_ref, s256_ref, seg_ref, pw_ref, wm_ref, wmt_ref,
         sb_ref, wo_ref, v1024_ref, e4_ref) = refs[0:16]
        dx_ref, dhb_ref, dzb_ref, osm_ref = refs[n_in:n_in + 4]
        dbuf, ebuf, fbuf, a0_s, u_s, wm_s, wmt_s, dsp_acc, pw_acc = refs[n_in + n_out:n_in + n_out + 9]
        i = pl.program_id(0)
        tile_idx = nt - 1 - i
        if exch is not None:
            l_ref, p_i, p_o, sm = refs[16:20]
            r_i, r_o, r_sm = refs[n_in + 4:n_in + 7]
            _exchange_comm(i == 0, i == nt - 1, l_ref[0], p_i, p_o, sm, r_i, r_o, r_sm, *refs[n_in + n_out + 9:])

        @pl.when(i == 0)
        def _():
            dbuf[T:T + HALO_A, :] = jnp.zeros((HALO_A, GROUP), F32)
            ebuf[T:T + HALO_B, :] = jnp.zeros((HALO_B, GROUP), F32)
            fbuf[T:T + HALO_C, :] = jnp.zeros((HALO_C, GROUP), F32)
            _sgu_masks(wm_ref, wmt_ref, wm_s, wmt_s)
            osm_ref[...] = jnp.zeros_like(osm_ref)
            dsp_acc[...] = jnp.zeros_like(dsp_acc)
            pw_acc[...] = jnp.zeros_like(pw_acc)

        def hs(j):
            return h_ref[:, GROUP * j:GROUP * (j + 1)]

        def acc_row(row, val):
            osm_ref[row:row + 1, :] += _colsum(val)

        def acc_wide(row, val):
            cs = _colsum(val)
            for j in range(D_MODEL // GROUP):
                osm_ref[row + j:row + j + 1, :] += cs[:, GROUP * j:GROUP * (j + 1)]

        def put_dh(j, val):
            acc_row(ROW_BIN + j, val)
            dhb_ref[:, GROUP * j:GROUP * (j + 1)] = val.astype(BF16)

        def dsilu(v, s):
            return s * (1.0 + v * (1.0 - s))

        dy = dy_ref[...]
        z = z_ref[...]
        cen = z - _rowmean(z)
        rstd = lax.rsqrt(_rowmean(cen * cen) + LN_EPS)
        xhat = cen * rstd
        acc_wide(ROW_LNG, dy * xhat)
        acc_wide(ROW_LNB, dy)
        gdy = dy * v1024_ref[1:2, :]
        dz = rstd * (gdy - _rowmean(gdy) - xhat * _rowmean(gdy * xhat))
        acc_wide(ROW_BOUT, dz)
        dzb = dz.astype(BF16)
        dzb_ref[...] = dzb

        def dmix(k):
            return _dot_nt(dzb, wo_ref[k])

        segm = seg_ref[...]

        a_val, a_glu, a_z = hs(0), hs(1), hs(2)
        sg = _sig(a_glu)
        a0_s[...] = a_val * sg
        a1 = aux_ref[:, 0:GROUP]
        cen = a1 - _segdot(a1, segm)
        rstd_a = lax.rsqrt(_segdot(cen * cen, segm) + LN_EPS)
        xh = cen * rstd_a
        a2 = xh * s256_ref[1:2, :] + s256_ref[2:3, :]
        s2 = _sig(a2)
        sz = _sig(a_z)
        dya = dmix(0)
        put_dh(2, dya * (a2 * s2) * dsilu(a_z, sz))
        d_a2 = dya * (a_z * sz) * dsilu(a2, s2)
        acc_row(1, d_a2 * xh)
        acc_row(2, d_a2)
        gd = d_a2 * s256_ref[1:2, :]
        d_a1 = rstd_a * (gd - _segdot(gd, segm) - xh * _segdot(gd * xh, segm))
        acc_row(0, d_a1)
        dbuf[0:T, :] = d_a1
        for r0 in range(0, T, ROWS):
            a0c = a0_s[r0:r0 + ROWS, :]
            acc = None
            for k in range(KA):
                off = (KA - 1) - k + r0
                w = dbuf[off:off + ROWS, :]
                term = caw_ref[k:k + 1, :] * w
                acc = term if acc is None else acc + term
                acc_row(ROW_CAW + k, a0c * w)
            u_s[r0:r0 + ROWS, :] = acc
        dbuf[T:T + HALO_A, :] = dbuf[0:HALO_A, :]
        d_a0 = u_s[...]
        put_dh(0, d_a0 * sg)
        put_dh(1, d_a0 * a_val * sg * (1.0 - sg))

        b_b, b_c, b_h, b_z = hs(3), hs(4), hs(5), hs(6)
        cb = aux_ref[:, GROUP:2 * GROUP]
        sz = _sig(b_z)
        dyb = dmix(1)
        put_dh(3, dyb * cb * (b_z * sz))
        put_dh(6, dyb * b_b * cb * dsilu(b_z, sz))
        ebuf[0:T, :] = dyb * b_b * (b_z * sz)
        a0_s[...] = b_c * b_h
        for r0 in range(0, T, ROWS):
            uc = a0_s[r0:r0 + ROWS, :]
            acc = None
            for k in range(KB):
                off = (KB - 1) - k + r0
                w = ebuf[off:off + ROWS, :]
                term = cbw_ref[k:k + 1, :] * w
                acc = term if acc is None else acc + term
                acc_row(ROW_CBW + k, uc * w)
            u_s[r0:r0 + ROWS, :] = acc
        ebuf[T:T + HALO_B, :] = ebuf[0:HALO_B, :]
        d_u = u_s[...]
        put_dh(4, d_u * b_h)
        put_dh(5, d_u * b_c)

        c_z = hs(8)
        pooled = aux_ref[:, 2 * GROUP:3 * GROUP]
        pooled_b = pooled.astype(BF16)
        q = _dot(pooled_b, pw_ref[...])
        sz = _sig(c_z)
        dyc = dmix(2)
        ps = s256_ref[3:4, :]
        acc_row(3, dyc * q * (c_z * sz))
        put_dh(8, dyc * q * ps * dsilu(c_z, sz))
        d_q = (dyc * ps * (c_z * sz)).astype(BF16)
        pw_acc[...] += _dot_tn(pooled_b, d_q)
        d_pooled = _dot_nt(d_q, pw_ref[...])
        fbuf[0:T, :] = d_pooled / _pool_cnt(tile_idx, T)
        hi_lane = (lax.broadcasted_iota(jnp.int32, (1, 128), 1) // HEAD) == 1
        for r0 in range(0, T, ROWS):
            def win(col, j0, j1):
                s = None
                for j in range(j0, j1):
                    term = fbuf[r0 + j:r0 + j + ROWS, 128 * col:128 * (col + 1)]
                    s = term if s is None else s + term
                return s
            u_s[r0:r0 + ROWS, 0:128] = win(0, 0, 2) + jnp.where(hi_lane, win(0, 2, 4), 0.0)
            u_s[r0:r0 + ROWS, 128:256] = win(1, 0, 8) + jnp.where(hi_lane, win(1, 8, 16), 0.0)
        fbuf[T:T + HALO_C, :] = fbuf[0:HALO_C, :]
        put_dh(7, u_s[...] - d_pooled)

        d_u_, d_v_, d_z_ = hs(9), hs(10), hs(11)
        cen = d_v_ - _rowmean(d_v_)
        rstd_v = lax.rsqrt(_rowmean(cen * cen) + LN_EPS)
        xv = cen * rstd_v
        v = xv * s256_ref[4:5, :] + s256_ref[5:6, :]
        sz = _sig(d_z_)
        dyd = dmix(3)
        d_sp = dyd * d_u_ * (d_z_ * sz)
        grp = _lane_group(GROUP)
        sps, dvs = [], []
        for n in range(T // SGU_BLOCK):
            blk = slice(n * SGU_BLOCK, (n + 1) * SGU_BLOCK)
            vst = _vstack(v[blk, :])
            sps.append(_dot(wm_s[...], vst) + sb_ref[...])
            dspb = d_sp[blk, :]
            dsp_acc[...] += dspb
            dspb16 = dspb.astype(BF16)
            dvst = _dot(wmt_s[...], dspb16)
            dvb = None
            for hh in range(4):
                part = jnp.where(grp == hh, dvst[hh * SGU_BLOCK:(hh + 1) * SGU_BLOCK, :], 0.0)
                dvb = part if dvb is None else dvb + part
            dvs.append(dvb)
            dwc = _dot_nt(dspb16, vst)
            osm_ref[ROW_WC:ROW_WC + SGU_BLOCK, :] += dwc[:, 0:GROUP]
            osm_ref[ROW_WC + SGU_BLOCK:ROW_WC + 2 * SGU_BLOCK, :] += dwc[:, GROUP:2 * GROUP]
        sp = jnp.concatenate(sps, axis=0)
        d_v = jnp.concatenate(dvs, axis=0)
        put_dh(9, dyd * sp * (d_z_ * sz))
        put_dh(11, dyd * d_u_ * sp * dsilu(d_z_, sz))
        acc_row(4, d_v * xv)
        acc_row(5, d_v)
        gd = d_v * s256_ref[4:5, :]
        put_dh(10, rstd_v * (gd - _rowmean(gd) - xv * _rowmean(gd * xv)))

        dx = alpha * dz
        for k in range(N_CHIPS):
            dx = dx + _dot_nt(dhb_ref[:, COLS * k:COLS * (k + 1)], wi_ref[k])
        dx_ref[...] = dx

        @pl.when(i == nt - 1)
        def _():
            r = lax.broadcasted_iota(jnp.int32, (SGU_BLOCK, GROUP), 0) // CHUNK
            c = (lax.broadcasted_iota(jnp.int32, (SGU_BLOCK, GROUP), 1) % SGU_BLOCK) // CHUNK
            for half in range(2):
                rows_ = slice(ROW_WC + half * SGU_BLOCK, ROW_WC + (half + 1) * SGU_BLOCK)
                osm_ref[rows_, :] = jnp.where(c <= r, osm_ref[rows_, :], 0.0)
            osm_ref[ROW_SB:ROW_SB + SGU_BLOCK, 0:128] = _segdot(dsp_acc[...], e4_ref[...])
            for g in range(4):
                osm_ref[ROW_PW:ROW_PW + HEAD, HEAD * g:HEAD * (g + 1)] = (
                    pw_acc[HEAD * g:HEAD * (g + 1), HEAD * g:HEAD * (g + 1)])

    def full(a):
        nd = a.ndim
        return pl.BlockSpec(a.shape, lambda i, _n=nd: (0,) * _n)

    def rows(width):
        return pl.BlockSpec((T, width), lambda i: (nt - 1 - i, 0))

    def acc(shape):
        return pl.BlockSpec(shape, lambda i: (0, 0))

    consts = (wi, caw, cbw, s256, seg, pw, wm, wmt, sb, wo, v1024, e4)
    in_specs = [rows(D_MODEL), rows(D_MODEL), rows(IN_WIDTH), rows(3 * GROUP)] + [full(a) for a in consts]
    out_specs = [rows(D_MODEL), rows(IN_WIDTH), rows(D_MODEL), acc((SM_ROWS, GROUP))]
    out_shape = [jax.ShapeDtypeStruct((S, D_MODEL), F32), jax.ShapeDtypeStruct((S, IN_WIDTH), BF16),
                 jax.ShapeDtypeStruct((S, D_MODEL), BF16), jax.ShapeDtypeStruct((SM_ROWS, GROUP), F32)]
    scratch = [pltpu.VMEM((T + HALO_A, GROUP), F32), pltpu.VMEM((T + HALO_B, GROUP), F32),
               pltpu.VMEM((T + HALO_C, GROUP), F32), pltpu.VMEM((T, GROUP), F32), pltpu.VMEM((T, GROUP), F32),
               pltpu.VMEM((SGU_BLOCK, 4 * SGU_BLOCK), BF16), pltpu.VMEM((4 * SGU_BLOCK, SGU_BLOCK), BF16),
               pltpu.VMEM((SGU_BLOCK, GROUP), F32), pltpu.VMEM((GROUP, GROUP), F32)]
    extra, aliases = (), {}
    if exch is not None:
        extra = tuple(exch)
        r_i, r_o = exch[4], exch[5]
        in_specs += [pl.BlockSpec(memory_space=pltpu.SMEM)] + [ANY] * 5
        out_specs += [ANY] * 3
        out_shape += [jax.ShapeDtypeStruct(r_i.shape, r_i.dtype), jax.ShapeDtypeStruct(r_o.shape, r_o.dtype),
                      jax.ShapeDtypeStruct((N_DEV, SM_ROWS, GROUP), F32)]
        scratch += [pltpu.SemaphoreType.DMA((13,)), pltpu.SemaphoreType.DMA((13,)), pltpu.SemaphoreType.DMA((1,))]
        aliases = {20: 4, 21: 5}
    return pl.pallas_call(
        body, name="bwd_layer" if exch is None else "bwd_layer_exchange",
        grid=(nt,), in_specs=in_specs, out_specs=out_specs, out_shape=out_shape, scratch_shapes=scratch,
        input_output_aliases=aliases,
        compiler_params=_vmem_params(dimension_semantics=("arbitrary",), has_side_effects=exch is not None),
    )(dy, z, h, aux, *consts, *extra)


def _dw_in(layer, xb, dhb, slab, slab16, *, tk):
    S = xb.shape[0]
    ns = S // tk

    def body(l_ref, a_ref, b_ref, slab_ref, slab16_ref, o_ref, o16_ref):
        del l_ref, slab_ref, slab16_ref

        @pl.when(pl.program_id(1) == 0)
        def _():
            o_ref[...] = jnp.zeros_like(o_ref)
        o_ref[...] += _dot_tn(a_ref[...], b_ref[...])

        @pl.when(pl.program_id(1) == ns - 1)
        def _():
            o16_ref[...] = o_ref[...].astype(BF16)

    o_spec = pl.BlockSpec((None, None, D_MODEL, COLS), lambda j, s, l: (l[0], j, 0, 0))
    grid_spec = pltpu.PrefetchScalarGridSpec(
        num_scalar_prefetch=1, grid=(N_CHIPS, ns),
        in_specs=[pl.BlockSpec((tk, D_MODEL), lambda j, s, l: (s, 0)), pl.BlockSpec((tk, COLS), lambda j, s, l: (s, j)),
                  ANY, ANY],
        out_specs=[o_spec, o_spec])
    return pl.pallas_call(
        body, name="dw_in", grid_spec=grid_spec,
        out_shape=[jax.ShapeDtypeStruct(slab.shape, F32), jax.ShapeDtypeStruct(slab.shape, BF16)],
        input_output_aliases={3: 0, 4: 1},
        compiler_params=_vmem_params(dimension_semantics=("arbitrary", "arbitrary")),
    )(layer, xb, dhb, slab, slab16)


def _dw_out(layer, mixb, dzb, slab, slab16, *, tk):
    S = mixb.shape[0]
    ns = S // tk

    def body(l_ref, a_ref, b_ref, slab_ref, slab16_ref, o_ref, o16_ref):
        del l_ref, slab_ref, slab16_ref

        @pl.when(pl.program_id(0) == 0)
        def _():
            o_ref[...] = jnp.zeros_like(o_ref)
        o_ref[...] += _dot_tn(a_ref[...], b_ref[...]).reshape(N_CHIPS, GROUP, D_MODEL)

        @pl.when(pl.program_id(0) == ns - 1)
        def _():
            o16_ref[...] = o_ref[...].astype(BF16)

    o_spec = pl.BlockSpec((None, N_CHIPS, GROUP, D_MODEL), lambda s, l: (l[0], 0, 0, 0))
    grid_spec = pltpu.PrefetchScalarGridSpec(
        num_scalar_prefetch=1, grid=(ns,),
        in_specs=[pl.BlockSpec((tk, D_MODEL), lambda s, l: (s, 0)), pl.BlockSpec((tk, D_MODEL), lambda s, l: (s, 0)),
                  ANY, ANY],
        out_specs=[o_spec, o_spec])
    return pl.pallas_call(
        body, name="dw_out", grid_spec=grid_spec,
        out_shape=[jax.ShapeDtypeStruct(slab.shape, F32), jax.ShapeDtypeStruct(slab.shape, BF16)],
        input_output_aliases={3: 0, 4: 1},
        compiler_params=_vmem_params(dimension_semantics=("arbitrary",)),
    )(layer, mixb, dzb, slab, slab16)


def _adamw(w, g, m, v, *, rows_per_step, name, copy_g=False):
    R, C = w.shape
    tr = rows_per_step
    c1 = 1.0 - ADAM_B1 ** ADAM_STEP
    c2 = 1.0 - ADAM_B2 ** ADAM_STEP

    def body(w_ref, g_ref, m_ref, v_ref, d_ref, nm_ref, nv_ref, *g_out):
        g_ = g_ref[...]
        nm = ADAM_B1 * m_ref[...] + (1.0 - ADAM_B1) * g_
        nv = ADAM_B2 * v_ref[...] + (1.0 - ADAM_B2) * (g_ * g_)
        nm_ref[...] = nm
        nv_ref[...] = nv
        d_ref[...] = -ADAM_LR * ((nm / c1) / (jnp.sqrt(nv / c2) + ADAM_EPS) + ADAM_WD * w_ref[...])
        if copy_g:
            g_out[0][...] = g_

    spec = pl.BlockSpec((tr, C), lambda i: (i, 0))
    n_out = 4 if copy_g else 3
    return pl.pallas_call(
        body, name=name, grid=(R // tr,),
        in_specs=[spec] * 4, out_specs=[spec] * n_out,
        out_shape=[jax.ShapeDtypeStruct((R, C), F32)] * n_out,
        compiler_params=_vmem_params(dimension_semantics=("arbitrary",)),
    )(w, g, m, v)


def _gather_weights(wi16, wo16, cw):
    L = wi16.shape[0]
    hi_rows, ho_rows = D_MODEL // 2, GROUP // 2
    n_ici = 2 * L + 1
    n_fwd = 2 * L

    def body(wi_ref, wo_ref, cw_ref, *rest):
        wig = rest[0:L]
        wog = rest[L:2 * L]
        cwg = rest[2 * L]
        send_sems, recv_sems, loc_sems = rest[2 * L + 1:]
        x, y, c = _place()
        me_k = 2 * x + y
        sibling = (x, y, 1 - c)
        chips = _other_chips(x, y)

        def half_i(ref, blk):
            return ref.at[blk, pl.ds(c * hi_rows, hi_rows), :]

        def half_o(ref, blk):
            return ref.at[blk, pl.ds(c * ho_rows, ho_rows), :]

        def other_half_i(ref, blk):
            return ref.at[blk, pl.ds((1 - c) * hi_rows, hi_rows), :]

        def other_half_o(ref, blk):
            return ref.at[blk, pl.ds((1 - c) * ho_rows, ho_rows), :]

        local = []
        for l in range(L):
            local.append(pltpu.make_async_copy(wi_ref.at[l], wig[l].at[me_k], loc_sems.at[2 * l]))
            local.append(pltpu.make_async_copy(wo_ref.at[l], wog[l].at[me_k], loc_sems.at[2 * l + 1]))
        local.append(pltpu.make_async_copy(cw_ref, cwg.at[me_k], loc_sems.at[2 * L]))
        for cp in local:
            cp.start()

        def remote(src, dst, sem, to):
            return pltpu.make_async_remote_copy(src_ref=src, dst_ref=dst, send_sem=send_sems.at[sem],
                                                recv_sem=recv_sems.at[sem], device_id=to, device_id_type=MESH)

        sends = []
        for r, (px, py, _) in enumerate(chips):
            to = (px, py, c)
            for l in range(L):
                sends.append(remote(half_i(wi_ref, l), half_i(wig[l], me_k), r * n_ici + 2 * l, to))
                sends.append(remote(half_o(wo_ref, l), half_o(wog[l], me_k), r * n_ici + 2 * l + 1, to))
            sends.append(remote(cw_ref, cwg.at[me_k], r * n_ici + 2 * L, to))
        for cp in sends:
            cp.start()

        base = 3 * n_ici
        fwds = []
        for r, (px, py, pk) in enumerate(chips):
            for l in range(L):
                remote(half_i(wig[l], pk), half_i(wig[l], pk), r * n_ici + 2 * l, sibling).wait_recv()
                f = remote(half_i(wig[l], pk), half_i(wig[l], pk), base + r * n_fwd + 2 * l, sibling)
                f.start()
                fwds.append(f)
                remote(half_o(wog[l], pk), half_o(wog[l], pk), r * n_ici + 2 * l + 1, sibling).wait_recv()
                f = remote(half_o(wog[l], pk), half_o(wog[l], pk), base + r * n_fwd + 2 * l + 1, sibling)
                f.start()
                fwds.append(f)
            remote(cwg.at[pk], cwg.at[pk], r * n_ici + 2 * L, sibling).wait_recv()
        for r, (px, py, pk) in enumerate(chips):
            for l in range(L):
                remote(other_half_i(wig[l], pk), other_half_i(wig[l], pk), base + r * n_fwd + 2 * l, sibling).wait_recv()
                remote(other_half_o(wog[l], pk), other_half_o(wog[l], pk), base + r * n_fwd + 2 * l + 1, sibling).wait_recv()
        for cp in sends + fwds:
            cp.wait_send()
        for cp in local:
            cp.wait()

    n_sem = 3 * n_ici + 3 * n_fwd
    out_shape = ([jax.ShapeDtypeStruct((N_CHIPS, D_MODEL, COLS), BF16)] * L
                 + [jax.ShapeDtypeStruct((N_CHIPS, GROUP, D_MODEL), BF16)] * L
                 + [jax.ShapeDtypeStruct((N_CHIPS,) + cw.shape, F32)])
    outs = pl.pallas_call(
        body, name="gather_weights",
        in_specs=[ANY, ANY, ANY], out_specs=[ANY] * (2 * L + 1), out_shape=out_shape,
        scratch_shapes=[pltpu.SemaphoreType.DMA((n_sem,)), pltpu.SemaphoreType.DMA((n_sem,)),
                        pltpu.SemaphoreType.DMA((2 * L + 1,))],
        compiler_params=pltpu.CompilerParams(has_side_effects=True),
    )(wi16, wo16, cw)
    return outs[0:L], outs[L:2 * L], outs[2 * L]


def _swap_halves(l_arr, gwi, gwo, ri, ro):
    hi_rows, ho_rows = D_MODEL // 2, GROUP // 2

    def body(l_ref, gwi_ref, gwo_ref, ri_in, ro_in, ri_ref, ro_ref, send_sems, recv_sems):
        del ri_in, ro_in
        x, y, c = _place()
        l = l_ref[0]
        sibling = (x, y, 1 - c)
        cps = [
            pltpu.make_async_remote_copy(src_ref=gwi_ref.at[l, :, pl.ds((1 - c) * hi_rows, hi_rows), :],
                                         dst_ref=ri_ref.at[l], send_sem=send_sems.at[0], recv_sem=recv_sems.at[0],
                                         device_id=sibling, device_id_type=MESH),
            pltpu.make_async_remote_copy(src_ref=gwo_ref.at[l, :, pl.ds((1 - c) * ho_rows, ho_rows), :],
                                         dst_ref=ro_ref.at[l], send_sem=send_sems.at[1], recv_sem=recv_sems.at[1],
                                         device_id=sibling, device_id_type=MESH),
        ]
        for cp in cps:
            cp.start()
        for cp in cps:
            cp.wait()

    return pl.pallas_call(
        body, name="swap_halves",
        in_specs=[pl.BlockSpec(memory_space=pltpu.SMEM), ANY, ANY, ANY, ANY], out_specs=[ANY, ANY],
        out_shape=[jax.ShapeDtypeStruct(ri.shape, ri.dtype), jax.ShapeDtypeStruct(ro.shape, ro.dtype)],
        input_output_aliases={3: 0, 4: 1},
        scratch_shapes=[pltpu.SemaphoreType.DMA((2,)), pltpu.SemaphoreType.DMA((2,))],
        compiler_params=pltpu.CompilerParams(has_side_effects=True),
    )(l_arr, gwi, gwo, ri, ro)


def _add_halves(cl_arr, g, r, p, *, rows, cols, tr, name):
    nb = rows // tr

    def body(cl_ref, g_ref, r_ref, p_in, o_ref):
        del cl_ref, p_in
        o_ref[...] = (g_ref[...] + r_ref[...].astype(F32)).astype(o_ref.dtype)

    grid_spec = pltpu.PrefetchScalarGridSpec(
        num_scalar_prefetch=1, grid=(N_CHIPS, nb),
        in_specs=[pl.BlockSpec((None, None, tr, cols), lambda k, i, cl: (cl[1], k, cl[0] * nb + i, 0)),
                  pl.BlockSpec((None, None, tr, cols), lambda k, i, cl: (cl[1], k, i, 0)), ANY],
        out_specs=pl.BlockSpec((None, None, tr, cols), lambda k, i, cl: (cl[1], k, i, 0)))
    return pl.pallas_call(
        body, name=name, grid_spec=grid_spec,
        out_shape=jax.ShapeDtypeStruct(p.shape, p.dtype),
        input_output_aliases={3: 0},
        compiler_params=_vmem_params(dimension_semantics=("arbitrary",) * 2),
    )(cl_arr, g, r, p)


def _exchange_last(l_arr, p_i, p_o, sm, r_i, r_o):
    def body(l_ref, p_i_ref, p_o_ref, sm_ref, ri_in, ro_in, ri_ref, ro_ref, rsm_ref, send_sems, recv_sems, loc_sem):
        del ri_in, ro_in
        always = l_ref[0] >= 0
        _exchange_comm(always, always, l_ref[0], p_i_ref, p_o_ref, sm_ref, ri_ref, ro_ref, rsm_ref,
                       send_sems, recv_sems, loc_sem)

    return pl.pallas_call(
        body, name="exchange_last",
        in_specs=[pl.BlockSpec(memory_space=pltpu.SMEM)] + [ANY] * 5, out_specs=[ANY] * 3,
        out_shape=[jax.ShapeDtypeStruct(r_i.shape, r_i.dtype), jax.ShapeDtypeStruct(r_o.shape, r_o.dtype),
                   jax.ShapeDtypeStruct((N_DEV, SM_ROWS, GROUP), F32)],
        input_output_aliases={4: 0, 5: 1},
        scratch_shapes=[pltpu.SemaphoreType.DMA((13,)), pltpu.SemaphoreType.DMA((13,)), pltpu.SemaphoreType.DMA((1,))],
        compiler_params=pltpu.CompilerParams(has_side_effects=True),
    )(l_arr, p_i, p_o, sm, r_i, r_o)


def _sum_small(r_sms):
    L = len(r_sms)

    def body(*refs):
        o_ref = refs[L]
        for l in range(L):
            acc = refs[l][0]
            for d in range(1, N_DEV):
                acc = acc + refs[l][d]
            o_ref[l] = acc

    return pl.pallas_call(
        body, name="sum_small",
        out_shape=jax.ShapeDtypeStruct((L,) + r_sms[0].shape[1:], F32),
        compiler_params=_vmem_params(),
    )(*r_sms)


def _sum_chunks(kc_arr, p, r, *, rows, cols, tr, name):
    L = p.shape[0]
    nb = rows // tr

    def body(kc_ref, p_ref, r0_ref, r1_ref, r2_ref, o_ref):
        del kc_ref
        f = lambda ref: ref[...].astype(F32)
        o_ref[...] = ((f(p_ref) + f(r0_ref)) + f(r1_ref)) + f(r2_ref)

    def rspec(j):
        return pl.BlockSpec((None, None, tr, cols), lambda l, i, kc, _j=j: (_j, l, i, 0))

    grid_spec = pltpu.PrefetchScalarGridSpec(
        num_scalar_prefetch=1, grid=(L, nb),
        in_specs=[pl.BlockSpec((None, None, tr, cols), lambda l, i, kc: (l, kc[0], i, 0)), rspec(0), rspec(1), rspec(2)],
        out_specs=pl.BlockSpec((None, tr, cols), lambda l, i, kc: (l, kc[1] * nb + i, 0)))
    return pl.pallas_call(
        body, name=name, grid_spec=grid_spec,
        out_shape=jax.ShapeDtypeStruct((L, 2 * rows, cols), F32),
        compiler_params=_vmem_params(dimension_semantics=("arbitrary",) * 2),
    )(kc_arr, p, r, r, r)


def _share_result(gi, go):
    hi_rows, ho_rows = gi.shape[1] // 2, go.shape[1] // 2

    def body(gi_ref, go_ref, oi_ref, oo_ref, send_sems, recv_sems):
        del gi_ref, go_ref
        x, y, c = _place()
        sibling = (x, y, 1 - c)
        cps = []
        for j, (ref, n) in enumerate(((oi_ref, hi_rows), (oo_ref, ho_rows))):
            mine = ref.at[:, pl.ds(c * n, n), :]
            cps.append(pltpu.make_async_remote_copy(src_ref=mine, dst_ref=mine, send_sem=send_sems.at[j],
                                                    recv_sem=recv_sems.at[j], device_id=sibling, device_id_type=MESH))
        for cp in cps:
            cp.start()
        for j, (ref, n) in enumerate(((oi_ref, hi_rows), (oo_ref, ho_rows))):
            theirs = ref.at[:, pl.ds((1 - c) * n, n), :]
            pltpu.make_async_remote_copy(src_ref=theirs, dst_ref=theirs, send_sem=send_sems.at[j],
                                         recv_sem=recv_sems.at[j], device_id=sibling, device_id_type=MESH).wait_recv()
        for cp in cps:
            cp.wait_send()

    return pl.pallas_call(
        body, name="share_result",
        in_specs=[ANY, ANY], out_specs=[ANY, ANY],
        out_shape=[jax.ShapeDtypeStruct(gi.shape, F32), jax.ShapeDtypeStruct(go.shape, F32)],
        input_output_aliases={0: 0, 1: 1},
        scratch_shapes=[pltpu.SemaphoreType.DMA((2,)), pltpu.SemaphoreType.DMA((2,))],
        compiler_params=pltpu.CompilerParams(has_side_effects=True),
    )(gi, go)


WEIGHTS = ("ln_g", "ln_b", "w_in", "b_in", "conv_a_w", "conv_a_b", "norm_a_g", "norm_a_b", "conv_b_w", "pool_w",
           "pool_scale", "sgu_ln_g", "sgu_ln_b", "sgu_w", "sgu_bias", "w_out", "b_out")


def _pad_rows(a, rows):
    return jnp.pad(a, ((0, rows - a.shape[0]), (0, 0)))


def _indicator_consts():
    seg = jnp.where((jnp.arange(GROUP)[:, None] // HEAD) == (jnp.arange(GROUP)[None, :] // HEAD),
                    1.0 / HEAD, 0.0).astype(BF16)
    e4 = ((jnp.arange(GROUP)[:, None] // HEAD) == jnp.arange(128)[None, :]).astype(BF16)
    return seg, e4


def _layer_consts(p, conv_full, l):
    same_head = jnp.eye(4, dtype=F32)[:, None, :, None] > 0
    caw = _pad_rows(conv_full[l, :KA], 32)
    cbw = _pad_rows(conv_full[l, KA:], 8)
    s256 = _pad_rows(jnp.stack([p["conv_a_b"][l], p["norm_a_g"][l], p["norm_a_b"][l], p["pool_scale"][l],
                                p["sgu_ln_g"][l], p["sgu_ln_b"][l]]), 8)
    pw = jnp.where(same_head, p["pool_w"][l][:, :, None, :], 0.0).reshape(GROUP, GROUP).astype(BF16)
    wm = jnp.transpose(p["sgu_w"][l], (1, 0, 2)).reshape(SGU_BLOCK, 4 * SGU_BLOCK)
    wmt = jnp.transpose(p["sgu_w"][l], (0, 2, 1)).reshape(4 * SGU_BLOCK, SGU_BLOCK)
    sb = jnp.repeat(p["sgu_bias"][l].T, HEAD, axis=1)
    v1024 = _pad_rows(jnp.stack([p["b_out"][l], p["ln_g"][l], p["ln_b"][l]]), 8)
    return dict(caw=caw, cbw=cbw, s256=s256, pw=pw, wm=wm, wmt=wmt, sb=sb, v1024=v1024, bin=p["b_in"][l][None, :])


def _unpack_small(sm):
    L = sm.shape[0]
    owc = jnp.concatenate([sm[:, ROW_WC:ROW_WC + SGU_BLOCK], sm[:, ROW_WC + SGU_BLOCK:ROW_WC + 2 * SGU_BLOCK]], axis=2)
    return dict(
        conv_a_b=sm[:, 0], norm_a_g=sm[:, 1], norm_a_b=sm[:, 2], pool_scale=sm[:, 3], sgu_ln_g=sm[:, 4],
        sgu_ln_b=sm[:, 5], conv_b_w=sm[:, ROW_CBW:ROW_CBW + KB], conv_a_w=sm[:, ROW_CAW:ROW_CAW + KA],
        pool_w=jnp.transpose(sm[:, ROW_PW:ROW_PW + HEAD].reshape(L, HEAD, 4, HEAD), (0, 2, 1, 3)),
        ln_g=sm[:, ROW_LNG:ROW_LNG + 4].reshape(L, D_MODEL), ln_b=sm[:, ROW_LNB:ROW_LNB + 4].reshape(L, D_MODEL),
        b_out=sm[:, ROW_BOUT:ROW_BOUT + 4].reshape(L, D_MODEL),
        b_in=sm[:, ROW_BIN:ROW_BIN + N_SLICES].reshape(L, IN_WIDTH),
        sgu_w=jnp.transpose(owc.reshape(L, SGU_BLOCK, 4, SGU_BLOCK), (0, 2, 1, 3)),
        sgu_bias=jnp.transpose(sm[:, ROW_SB:ROW_SB + SGU_BLOCK, 0:4], (0, 2, 1)))


def _step(p, m, v, x, target, *, tile_f, tile_b, tk):
    L = p["ln_g"].shape[0]
    xi, yi, ci = _place()
    me_k = 2 * xi + yi
    hi_rows, ho_rows = D_MODEL // 2, GROUP // 2

    cw = jnp.concatenate([p["conv_a_w"], p["conv_b_w"]], axis=1).reshape(-1, 128)
    cw_rows = cw.shape[0]
    cw = _pad_rows(cw, 72)
    wi16 = p["w_in"].astype(BF16)
    wo16 = p["w_out"].astype(BF16)
    wig0, wog0, cwg = _gather_weights(wi16[0:1], wo16[0:1], cw)
    cwg = cwg[:, :cw_rows].reshape(N_CHIPS, L, KA + KB, HEAD)
    conv_full = jnp.transpose(cwg, (1, 2, 0, 3)).reshape(L, KA + KB, GROUP)
    seg, e4 = _indicator_consts()
    consts = [_layer_consts(p, conv_full, l) for l in range(L)]

    hcur = x
    saved, wig, wog = [], [wig0[0]], [wog0[0]]
    for l in range(L):
        k = consts[l]
        nxt = (wi16[l + 1], wo16[l + 1]) if l + 1 < L else None
        outs = _fwd_layer(hcur, wig[l], k["bin"], k["caw"], k["cbw"], k["s256"], seg, k["pw"], k["wm"], k["sb"], wog[l],
                          k["v1024"], tile=tile_f, nxt=nxt, target=None if nxt is not None else target)
        y, xb, h, aux, mixb, z = outs[0:6]
        if nxt is not None:
            wig.append(outs[6])
            wog.append(outs[7])
        saved.append((xb, h, aux, mixb, z))
        hcur = y

    dy = hcur
    loss_local = outs[6][0, 0]

    gwi = lax.empty((L, N_CHIPS, D_MODEL, COLS), F32)
    gwo = lax.empty((L, N_CHIPS, GROUP, D_MODEL), F32)
    gwi16 = lax.empty((L, N_CHIPS, D_MODEL, COLS), BF16)
    gwo16 = lax.empty((L, N_CHIPS, GROUP, D_MODEL), BF16)
    ri = lax.empty((L, N_CHIPS, hi_rows, COLS), BF16)
    ro = lax.empty((L, N_CHIPS, ho_rows, D_MODEL), BF16)
    p_i = lax.empty((L, N_CHIPS, hi_rows, COLS), BF16)
    p_o = lax.empty((L, N_CHIPS, ho_rows, D_MODEL), BF16)
    q_i = lax.empty((3, L, hi_rows, COLS), BF16)
    q_o = lax.empty((3, L, ho_rows, D_MODEL), BF16)
    r_sm = [None] * L
    pending = None
    for l in reversed(range(L)):
        k = consts[l]
        xb, h, aux, mixb, z = saved[l]
        exch = None if pending is None else (pending[0], p_i, p_o, pending[1], q_i, q_o)
        outs = _bwd_layer(dy, z, h, aux, wig[l], k["caw"], k["cbw"], k["s256"], seg, k["pw"], k["wm"], k["wmt"],
                          k["sb"], wog[l], k["v1024"], e4, tile=tile_b, exch=exch)
        dy, dhb, dzb, osm = outs[0:4]
        if l == L - 1:
            osm = osm.at[ROW_LOSS, 0].set(loss_local)
        if exch is not None:
            q_i, q_o, r_sm[l + 1] = outs[4:7]
        larr = jnp.full((1,), l, jnp.int32)
        gwi, gwi16 = _dw_in(larr, xb, dhb, gwi, gwi16, tk=tk)
        gwo, gwo16 = _dw_out(larr, mixb, dzb, gwo, gwo16, tk=tk)
        ri, ro = _swap_halves(larr, gwi16, gwo16, ri, ro)
        cl_arr = jnp.stack([ci, jnp.int32(l)]).astype(jnp.int32)
        p_i = _add_halves(cl_arr, gwi, ri, p_i, rows=hi_rows, cols=COLS, tr=256, name="add_halves_in")
        p_o = _add_halves(cl_arr, gwo, ro, p_o, rows=ho_rows, cols=D_MODEL, tr=128, name="add_halves_out")
        pending = (larr, osm)
    grad_x = dy
    q_i, q_o, r_sm[0] = _exchange_last(pending[0], p_i, p_o, pending[1], q_i, q_o)

    summed = _sum_small(r_sm)
    loss = summed[L - 1, ROW_LOSS, 0]
    grads = _unpack_small(summed)
    for n in ("conv_a_w", "conv_b_w"):
        grads[n] = lax.dynamic_slice_in_dim(grads[n], me_k * HEAD, HEAD, axis=2)

    kc_arr = jnp.stack([me_k, ci]).astype(jnp.int32)
    g_i = _sum_chunks(kc_arr, p_i, q_i, rows=hi_rows, cols=COLS, tr=256, name="sum_chunks_in")
    g_o = _sum_chunks(kc_arr, p_o, q_o, rows=ho_rows, cols=D_MODEL, tr=128, name="sum_chunks_out")
    g_i, g_o = _share_result(g_i, g_o)
    grads["w_in"] = g_i
    grads["w_out"] = g_o

    delta, new_m, new_v = {}, {}, {}
    for n in WEIGHTS:
        shp = p[n].shape
        if n in ("w_in", "w_out"):
            two_d = (shp[0] * shp[1], shp[2])
            tr = 512 if n == "w_in" else 256
        else:
            two_d = (-1, shp[-1])
            tr = None
        args = [a.reshape(two_d) for a in (p[n], grads[n], m[n], v[n])]
        outs = _adamw(*args, rows_per_step=tr or args[0].shape[0], name="adamw_" + n, copy_g=tr is not None)
        delta[n], new_m[n], new_v[n] = (a.reshape(shp) for a in outs[0:3])
        if tr is not None:
            grads[n] = outs[3].reshape(shp)

    return (loss, grad_x[None], *[grads[n] for n in WEIGHTS], *[delta[n] for n in WEIGHTS],
            *[new_m[n] for n in WEIGHTS], *[new_v[n] for n in WEIGHTS])


def kernel(x, ln_g, ln_b, w_in, b_in, conv_a_w, conv_a_b, norm_a_g, norm_a_b, conv_b_w, pool_w, pool_scale, sgu_ln_g, sgu_ln_b, sgu_w, sgu_bias, w_out, b_out, loss_target, m_ln_g, m_ln_b, m_w_in, m_b_in, m_conv_a_w, m_conv_a_b, m_norm_a_g, m_norm_a_b, m_conv_b_w, m_pool_w, m_pool_scale, m_sgu_ln_g, m_sgu_ln_b, m_sgu_w, m_sgu_bias, m_w_out, m_b_out, v_ln_g, v_ln_b, v_w_in, v_b_in, v_conv_a_w, v_conv_a_b, v_norm_a_g, v_norm_a_b, v_conv_b_w, v_pool_w, v_pool_scale, v_sgu_ln_g, v_sgu_ln_b, v_sgu_w, v_sgu_bias, v_w_out, v_b_out):
    p = dict(ln_g=ln_g, ln_b=ln_b, w_in=w_in, b_in=b_in, conv_a_w=conv_a_w, conv_a_b=conv_a_b, norm_a_g=norm_a_g,
             norm_a_b=norm_a_b, conv_b_w=conv_b_w, pool_w=pool_w, pool_scale=pool_scale, sgu_ln_g=sgu_ln_g,
             sgu_ln_b=sgu_ln_b, sgu_w=sgu_w, sgu_bias=sgu_bias, w_out=w_out, b_out=b_out)
    m = dict(ln_g=m_ln_g, ln_b=m_ln_b, w_in=m_w_in, b_in=m_b_in, conv_a_w=m_conv_a_w, conv_a_b=m_conv_a_b,
             norm_a_g=m_norm_a_g, norm_a_b=m_norm_a_b, conv_b_w=m_conv_b_w, pool_w=m_pool_w, pool_scale=m_pool_scale,
             sgu_ln_g=m_sgu_ln_g, sgu_ln_b=m_sgu_ln_b, sgu_w=m_sgu_w, sgu_bias=m_sgu_bias, w_out=m_w_out, b_out=m_b_out)
    v = dict(ln_g=v_ln_g, ln_b=v_ln_b, w_in=v_w_in, b_in=v_b_in, conv_a_w=v_conv_a_w, conv_a_b=v_conv_a_b,
             norm_a_g=v_norm_a_g, norm_a_b=v_norm_a_b, conv_b_w=v_conv_b_w, pool_w=v_pool_w, pool_scale=v_pool_scale,
             sgu_ln_g=v_sgu_ln_g, sgu_ln_b=v_sgu_ln_b, sgu_w=v_sgu_w, sgu_bias=v_sgu_bias, w_out=v_w_out, b_out=v_b_out)
    return _step(p, m, v, x[0], loss_target[0], tile_f=256, tile_b=256, tk=2048)
```

```python
import functools

import jax
import jax.numpy as jnp
from jax import lax
from jax.experimental import pallas as pl
from jax.experimental.pallas import tpu as pltpu

F32 = jnp.float32
BF16 = jnp.bfloat16
MESH = pl.DeviceIdType.MESH

D_MODEL = 1024
GROUP = 256
HEAD = 64
N_SLICES = 12
IN_WIDTH = N_SLICES * GROUP
N_CHIPS = 4
COLS = IN_WIDTH // N_CHIPS
KA = 31
KB = 3
HALO_A, HALO_B, HALO_C = 32, 8, 16
POOL_WINDOWS = (2, 4, 8, 16)
SGU_BLOCK = 128
CHUNK = 64
LN_EPS = 1e-5
ROWS = 64
V7X_VMEM_BYTES = 64 * 1024 * 1024
VMEM_LIMIT = 56 * 1024 * 1024

ADAM_LR, ADAM_B1, ADAM_B2, ADAM_EPS, ADAM_WD, ADAM_STEP = 0.001, 0.9, 0.999, 1e-08, 0.01, 10


ANY = pl.BlockSpec(memory_space=pl.ANY)


def _vmem_params(**kw):
    return pltpu.CompilerParams(vmem_limit_bytes=VMEM_LIMIT, **kw)


def _place():
    return lax.axis_index("x"), lax.axis_index("y"), lax.axis_index("c")


def _other_chips(x, y):
    return [(1 - x, y, 2 * (1 - x) + y), (x, 1 - y, 2 * x + (1 - y)), (1 - x, 1 - y, 2 * (1 - x) + (1 - y))]


def _sig(v):
    return 0.5 * jnp.tanh(0.5 * v) + 0.5


def _dot(a, b):
    return jnp.dot(a, b, preferred_element_type=F32)


def _dot_nt(a, b):
    return lax.dot_general(a, b, (((1,), (1,)), ((), ())), preferred_element_type=F32)


def _dot_tn(a, b):
    return lax.dot_general(a, b, (((0,), (0,)), ((), ())), preferred_element_type=F32)


def _segdot(v, m):
    hi = v.astype(BF16)
    lo = (v - hi.astype(F32)).astype(BF16)
    return _dot(hi, m) + _dot(lo, m)


def _colsum(v):
    return jnp.sum(v, axis=0, keepdims=True)


def _rowmean(v):
    return jnp.mean(v, axis=-1, keepdims=True)


def _lane_group(n):
    return lax.broadcasted_iota(jnp.int32, (1, n), 1) // HEAD


def _pool_cnt(tile, t_rows):
    pos = tile * t_rows + lax.broadcasted_iota(jnp.int32, (t_rows, GROUP), 0) + 1
    grp = lax.broadcasted_iota(jnp.int32, (t_rows, GROUP), 1) // HEAD
    win = jnp.where(grp == 0, 2, jnp.where(grp == 1, 4, jnp.where(grp == 2, 8, 16)))
    return jnp.minimum(pos, win).astype(F32)


def _sgu_masks(wm_ref, wmt_ref, wm_s, wmt_s):
    r = lax.broadcasted_iota(jnp.int32, (SGU_BLOCK, 4 * SGU_BLOCK), 0) // CHUNK
    c = (lax.broadcasted_iota(jnp.int32, (SGU_BLOCK, 4 * SGU_BLOCK), 1) % SGU_BLOCK) // CHUNK
    wm_s[...] = jnp.where(c <= r, wm_ref[...], 0.0).astype(BF16)
    if wmt_ref is not None:
        rt = (lax.broadcasted_iota(jnp.int32, (4 * SGU_BLOCK, SGU_BLOCK), 0) % SGU_BLOCK) // CHUNK
        ct = lax.broadcasted_iota(jnp.int32, (4 * SGU_BLOCK, SGU_BLOCK), 1) // CHUNK
        wmt_s[...] = jnp.where(rt <= ct, wmt_ref[...], 0.0).astype(BF16)


def _vstack(v_blk):
    grp = _lane_group(GROUP)
    return jnp.concatenate([jnp.where(grp == h, v_blk, 0.0) for h in range(4)], axis=0).astype(BF16)


def _gather_next(step, nt, nwi, nwo, gwi, gwo, send_sems, recv_sems, loc_sems):
    x, y, c = _place()
    me_k = 2 * x + y
    sibling = (x, y, 1 - c)
    chips = _other_chips(x, y)
    hi, ho = D_MODEL // 2, GROUP // 2

    def rc(src, dst, sem, to):
        return pltpu.make_async_remote_copy(src_ref=src, dst_ref=dst, send_sem=send_sems.at[sem],
                                            recv_sem=recv_sems.at[sem], device_id=to, device_id_type=MESH)

    def blk(ref, k, n, cc):
        return ref.at[k, pl.ds(cc * n, n), :]

    def ici(r):
        px, py, _ = chips[r]
        to = (px, py, c)
        return [rc(nwi.at[pl.ds(c * hi, hi), :], blk(gwi, me_k, hi, c), 2 * r, to),
                rc(nwo.at[pl.ds(c * ho, ho), :], blk(gwo, me_k, ho, c), 2 * r + 1, to)]

    def landed(r, cc, base):
        pk = chips[r][2]
        return [rc(blk(gwi, pk, hi, cc), blk(gwi, pk, hi, cc), base + 2 * r, sibling),
                rc(blk(gwo, pk, ho, cc), blk(gwo, pk, ho, cc), base + 2 * r + 1, sibling)]

    def local():
        return [pltpu.make_async_copy(nwi, gwi.at[me_k], loc_sems.at[0]),
                pltpu.make_async_copy(nwo, gwo.at[me_k], loc_sems.at[1])]

    @pl.when(step == 0)
    def _():
        for cp in local():
            cp.start()
        for r in range(3):
            for cp in ici(r):
                cp.start()

    @pl.when(step == (3 * nt) // 4)
    def _():
        for r in range(3):
            for got, fwd in zip(landed(r, c, 0), landed(r, c, 6)):
                got.wait_recv()
                fwd.start()

    @pl.when(step == nt - 1)
    def _():
        for r in range(3):
            for got in landed(r, 1 - c, 6):
                got.wait_recv()
        for r in range(3):
            for cp in ici(r) + landed(r, c, 6):
                cp.wait_send()
        for cp in local():
            cp.wait()


def _fwd_layer(x, wi, bin_, caw, cbw, s256, seg, pw, wm, sb, wo, v1024, *, tile, nxt=None, target=None):
    assert nxt is None or target is None
    S = x.shape[0]
    T = tile
    nt = S // T
    alpha = float((2.0 * 4) ** 0.25)
    n_in = 12 + (2 if nxt is not None else 0) + (1 if target is not None else 0)
    n_out = 6 + (2 if nxt is not None else 0) + (1 if target is not None else 0)

    def body(*refs):
        (x_ref, wi_ref, bin_ref, caw_ref, cbw_ref, s256_ref, seg_ref, pw_ref, wm_ref, sb_ref, wo_ref,
         v1024_ref) = refs[0:12]
        y_ref, xb_ref, h_ref, aux_ref, mix_ref, z_ref = refs[n_in:n_in + 6]
        abuf, bbuf, cbuf, wm_s, shf = refs[n_in + n_out:n_in + n_out + 5]
        i = pl.program_id(0)
        if nxt is not None:
            _gather_next(i, nt, refs[12], refs[13], refs[n_in + 6], refs[n_in + 7], *refs[n_in + n_out + 5:])

        @pl.when(i == 0)
        def _():
            abuf[0:HALO_A, :] = jnp.zeros((HALO_A, GROUP), F32)
            bbuf[0:HALO_B, :] = jnp.zeros((HALO_B, GROUP), F32)
            cbuf[0:HALO_C, :] = jnp.zeros((HALO_C, GROUP), F32)
            _sgu_masks(wm_ref, None, wm_s, None)

        x = x_ref[...]
        xb = x.astype(BF16)
        xb_ref[...] = xb
        for k in range(N_CHIPS):
            h_ref[:, COLS * k:COLS * (k + 1)] = _dot(xb, wi_ref[k]) + bin_ref[:, COLS * k:COLS * (k + 1)]

        def hs(j):
            return h_ref[:, GROUP * j:GROUP * (j + 1)]

        abuf[HALO_A:HALO_A + T, :] = hs(0) * _sig(hs(1))
        span = T + HALO_A - 8
        for p in range(1, 8):
            shf[p - 1, :, :] = abuf[p:p + span, :]
        for r0 in range(0, T, ROWS):
            acc = None
            for k in range(KA):
                off = HALO_A - (KA - 1) + k
                p, q8 = off % 8, off - off % 8
                win = abuf[r0 + q8:r0 + q8 + ROWS, :] if p == 0 else shf[p - 1, r0 + q8:r0 + q8 + ROWS, :]
                term = caw_ref[k:k + 1, :] * win
                acc = term if acc is None else acc + term
            aux_ref[r0:r0 + ROWS, 0:GROUP] = acc + s256_ref[0:1, :]
        abuf[0:HALO_A, :] = abuf[T:T + HALO_A, :]
        a1 = aux_ref[:, 0:GROUP]
        segm = seg_ref[...]
        cen = a1 - _segdot(a1, segm)
        var = _segdot(cen * cen, segm)
        a2 = cen * lax.rsqrt(var + LN_EPS) * s256_ref[1:2, :] + s256_ref[2:3, :]
        az = hs(2)
        mix_ref[:, 0:GROUP] = (a2 * _sig(a2) * (az * _sig(az))).astype(BF16)

        bbuf[HALO_B:HALO_B + T, :] = hs(4) * hs(5)
        for r0 in range(0, T, ROWS):
            acc = None
            for k in range(KB):
                off = HALO_B - (KB - 1) + k + r0
                term = cbw_ref[k:k + 1, :] * bbuf[off:off + ROWS, :]
                acc = term if acc is None else acc + term
            aux_ref[r0:r0 + ROWS, GROUP:2 * GROUP] = acc
        bbuf[0:HALO_B, :] = bbuf[T:T + HALO_B, :]
        bz = hs(6)
        mix_ref[:, GROUP:2 * GROUP] = (hs(3) * aux_ref[:, GROUP:2 * GROUP] * (bz * _sig(bz))).astype(BF16)

        ch = hs(7)
        cbuf[HALO_C:HALO_C + T, :] = ch
        hi_lane = (lax.broadcasted_iota(jnp.int32, (1, 128), 1) // HEAD) == 1
        for r0 in range(0, T, ROWS):
            def win(col, j0, j1):
                s = None
                for j in range(j0, j1):
                    off = HALO_C - j + r0
                    term = cbuf[off:off + ROWS, 128 * col:128 * (col + 1)]
                    s = term if s is None else s + term
                return s
            w0 = win(0, 0, 2) + jnp.where(hi_lane, win(0, 2, 4), 0.0)
            w1 = win(1, 0, 8) + jnp.where(hi_lane, win(1, 8, 16), 0.0)
            aux_ref[r0:r0 + ROWS, 2 * GROUP:2 * GROUP + 128] = w0
            aux_ref[r0:r0 + ROWS, 2 * GROUP + 128:3 * GROUP] = w1
        cbuf[0:HALO_C, :] = cbuf[T:T + HALO_C, :]
        pooled = aux_ref[:, 2 * GROUP:3 * GROUP] / _pool_cnt(i, T) - ch
        aux_ref[:, 2 * GROUP:3 * GROUP] = pooled
        q = _dot(pooled.astype(BF16), pw_ref[...])
        cz = hs(8)
        mix_ref[:, 2 * GROUP:3 * GROUP] = (q * s256_ref[3:4, :] * (cz * _sig(cz))).astype(BF16)

        dv = hs(10)
        cen = dv - _rowmean(dv)
        var = _rowmean(cen * cen)
        v = cen * lax.rsqrt(var + LN_EPS) * s256_ref[4:5, :] + s256_ref[5:6, :]
        sps = []
        for n in range(T // SGU_BLOCK):
            vb = v[n * SGU_BLOCK:(n + 1) * SGU_BLOCK, :]
            sps.append(_dot(wm_s[...], _vstack(vb)) + sb_ref[...])
        sp = jnp.concatenate(sps, axis=0)
        dz = hs(11)
        mix_ref[:, 3 * GROUP:4 * GROUP] = (hs(9) * sp * (dz * _sig(dz))).astype(BF16)

        out = v1024_ref[0:1, :]
        for k in range(N_CHIPS):
            out = out + _dot(mix_ref[:, GROUP * k:GROUP * (k + 1)], wo_ref[k])
        z = alpha * x + out
        z_ref[...] = z
        cen = z - _rowmean(z)
        var = _rowmean(cen * cen)
        y = cen * lax.rsqrt(var + LN_EPS) * v1024_ref[1:2, :] + v1024_ref[2:3, :]
        if target is None:
            y_ref[...] = y
        else:
            t_ref, loss_ref = refs[12], refs[n_in + 6]

            @pl.when(i == 0)
            def _():
                loss_ref[...] = jnp.zeros_like(loss_ref)
            err = y - t_ref[...]
            y_ref[...] = err * (1.0 / D_MODEL)
            loss_ref[...] += jnp.sum(_colsum(err * err), axis=1, keepdims=True) * (0.5 / D_MODEL)

    def full(a):
        nd = a.ndim
        return pl.BlockSpec(a.shape, lambda i, _n=nd: (0,) * _n)

    def rows(width):
        return pl.BlockSpec((T, width), lambda i: (i, 0))

    consts = (wi, bin_, caw, cbw, s256, seg, pw, wm, sb, wo, v1024)
    in_specs = [rows(D_MODEL)] + [full(a) for a in consts]
    out_specs = [rows(D_MODEL), rows(D_MODEL), rows(IN_WIDTH), rows(3 * GROUP), rows(D_MODEL), rows(D_MODEL)]
    out_shape = [jax.ShapeDtypeStruct((S, D_MODEL), F32), jax.ShapeDtypeStruct((S, D_MODEL), BF16),
                 jax.ShapeDtypeStruct((S, IN_WIDTH), F32), jax.ShapeDtypeStruct((S, 3 * GROUP), F32),
                 jax.ShapeDtypeStruct((S, D_MODEL), BF16), jax.ShapeDtypeStruct((S, D_MODEL), F32)]
    scratch = [pltpu.VMEM((T + HALO_A, GROUP), F32), pltpu.VMEM((T + HALO_B, GROUP), F32),
               pltpu.VMEM((T + HALO_C, GROUP), F32), pltpu.VMEM((SGU_BLOCK, 4 * SGU_BLOCK), BF16),
               pltpu.VMEM((7, T + HALO_A - 8, GROUP), F32)]
    extra = ()
    if nxt is not None:
        extra = tuple(nxt)
        in_specs += [ANY, ANY]
        out_specs += [ANY, ANY]
        out_shape += [jax.ShapeDtypeStruct((N_CHIPS, D_MODEL, COLS), BF16),
                      jax.ShapeDtypeStruct((N_CHIPS, GROUP, D_MODEL), BF16)]
        scratch += [pltpu.SemaphoreType.DMA((12,)), pltpu.SemaphoreType.DMA((12,)), pltpu.SemaphoreType.DMA((2,))]
    if target is not None:
        extra = (target,)
        in_specs += [rows(D_MODEL)]
        out_specs += [pl.BlockSpec((8, 128), lambda i: (0, 0))]
        out_shape += [jax.ShapeDtypeStruct((8, 128), F32)]
    return pl.pallas_call(
        body, name=("fwd_layer_loss" if target is not None else "fwd_layer") if nxt is None else "fwd_layer_gather",
        grid=(nt,), in_specs=in_specs, out_specs=out_specs, out_shape=out_shape, scratch_shapes=scratch,
        compiler_params=_vmem_params(dimension_semantics=("arbitrary",), has_side_effects=nxt is not None),
    )(x, *consts, *extra)


ROW_CBW = 8
ROW_CAW = 16
ROW_LOSS = 7
ROW_PW = 48
ROW_LNG = 112
ROW_LNB = 116
ROW_BOUT = 120
ROW_BIN = 124
ROW_WC = 136
ROW_SB = 392
SM_ROWS = 520
N_DEV = 8


def _exchange_comm(start, finish, l, p_i, p_o, sm, r_i, r_o, r_sm, send_sems, recv_sems, loc_sem):
    x, y, c = _place()
    me = 4 * x + 2 * y + c
    chips = _other_chips(x, y)

    def rc(src, dst, sem, to):
        return pltpu.make_async_remote_copy(src_ref=src, dst_ref=dst, send_sem=send_sems.at[sem],
                                            recv_sem=recv_sems.at[sem], device_id=to, device_id_type=MESH)

    def big(r):
        px, py, pk = chips[r]
        to = (px, py, c)
        return [rc(p_i.at[l, pk], r_i.at[r, l], 2 * r, to), rc(p_o.at[l, pk], r_o.at[r, l], 2 * r + 1, to)]

    def peer(rel):
        px = 1 - x if rel & 4 else x
        py = 1 - y if rel & 2 else y
        pc = 1 - c if rel & 1 else c
        return (px, py, pc), 4 * px + 2 * py + pc

    def small_out(rel):
        to, _ = peer(rel)
        return rc(sm, r_sm.at[me], 5 + rel, to)

    def small_in(rel):
        to, idx = peer(rel)
        return rc(sm, r_sm.at[idx], 5 + rel, to)

    def local():
        return pltpu.make_async_copy(sm, r_sm.at[me], loc_sem.at[0])

    with_big, with_small = p_i is not None, sm is not None

    @pl.when(start)
    def _():
        if with_small:
            local().start()
        if with_big:
            for r in range(3):
                for cp in big(r):
                    cp.start()
        if with_small:
            for rel in range(1, N_DEV):
                small_out(rel).start()

    @pl.when(finish)
    def _():
        if with_big:
            for r in range(3):
                for cp in big(r):
                    cp.wait()
        if with_small:
            for rel in range(1, N_DEV):
                small_in(rel).wait_recv()
                small_out(rel).wait_send()
            local().wait()


RC = 32
RC_WIDE = 16
ACC_ROWS = 136


def _rsum8(v):
    r = v[0:8]
    for j in range(1, v.shape[0] // 8):
        r = r + v[8 * j:8 * j + 8]
    return r


def _bwd_layer(dy, z, h, aux, wi, caw, cbw, s256, seg, pw, wm, wmt, sb, wo, v1024, e4, *, tile, exch=None):
    S = dy.shape[0]
    T = tile
    nt = S // T
    nblk = T // SGU_BLOCK
    alpha = float((2.0 * 4) ** 0.25)
    n_in = 16 + (6 if exch is not None else 0)
    n_out = 4 + (3 if exch is not None else 0)
    slab = pltpu.VMEM((T, GROUP), F32)
    scratch = dict(
        dbuf=pltpu.VMEM((T + HALO_A, GROUP), F32), ebuf=pltpu.VMEM((T + HALO_B, GROUP), F32),
        fbuf=pltpu.VMEM((T + HALO_C, GROUP), F32), sh=pltpu.VMEM((7, T + HALO_A - 8, GROUP), F32),
        wm_s=pltpu.VMEM((SGU_BLOCK, 4 * SGU_BLOCK), BF16), wmt_s=pltpu.VMEM((4 * SGU_BLOCK, SGU_BLOCK), BF16),
        dsp_acc=pltpu.VMEM((SGU_BLOCK, GROUP), F32), pw_acc=pltpu.VMEM((GROUP, GROUP), F32),
        acc_s=pltpu.VMEM((8 * ACC_ROWS, GROUP), F32), acc_w=pltpu.VMEM((24, D_MODEL), F32),
        dmix_s=pltpu.VMEM((T, D_MODEL), F32), vst_s=pltpu.VMEM((nblk, 4 * SGU_BLOCK, GROUP), BF16),
        dq_s=pltpu.VMEM((T, GROUP), BF16), dxt_s=pltpu.VMEM((D_MODEL, T), F32),
        mean_s=slab, t1_s=slab, t2_s=slab, q_s=slab, xv_s=slab, rv_s=slab, v_s=slab, sp_s=slab, a0_s=slab, sg_s=slab,
        xh_s=slab, ra_s=slab, ub_s=slab, dsp_s=slab, m1_s=slab, m2_s=slab, dpool_s=slab, dvd_s=slab, u_s=slab,
        du_s=slab, cw_s=slab)
    names = list(scratch)

    def body(*refs):
        (dy_ref, z_ref, h_ref, aux_ref, wi_ref, caw_ref, cbw_ref, s256_ref, seg_ref, pw_ref, wm_ref, wmt_ref,
         sb_ref, wo_ref, v1024_ref, e4_ref) = refs[0:16]
        dx_ref, dhb_ref, dzb_ref, osm_ref = refs[n_in:n_in + 4]
        k0 = n_in + n_out
        sc = dict(zip(names, refs[k0:k0 + len(names)]))
        dbuf, ebuf, fbuf, sh = sc["dbuf"], sc["ebuf"], sc["fbuf"], sc["sh"]
        wm_s, wmt_s, dsp_acc, pw_acc, acc_s, acc_w = (sc[n] for n in ("wm_s", "wmt_s", "dsp_acc", "pw_acc", "acc_s",
                                                                        "acc_w"))
        dmix_s, vst_s, dq_s = sc["dmix_s"], sc["vst_s"], sc["dq_s"]
        i = pl.program_id(0)
        tile_idx = nt - 1 - i
        if exch is not None:
            l_ref, p_i, p_o, sm = refs[16:20]
            r_i, r_o, r_sm = refs[n_in + 4:n_in + 7]
            _exchange_comm(i == 0, i == nt - 1, l_ref[0], p_i, p_o, sm, r_i, r_o, r_sm, *refs[k0 + len(names):])

        @pl.when(i == 0)
        def _():
            dbuf[T:T + HALO_A, :] = jnp.zeros((HALO_A, GROUP), F32)
            ebuf[T:T + HALO_B, :] = jnp.zeros((HALO_B, GROUP), F32)
            fbuf[T:T + HALO_C, :] = jnp.zeros((HALO_C, GROUP), F32)
            _sgu_masks(wm_ref, wmt_ref, wm_s, wmt_s)
            osm_ref[...] = jnp.zeros_like(osm_ref)
            dsp_acc[...] = jnp.zeros_like(dsp_acc)
            pw_acc[...] = jnp.zeros_like(pw_acc)
            acc_s[...] = jnp.zeros_like(acc_s)
            acc_w[...] = jnp.zeros_like(acc_w)

        def chunks(rc, fn):
            for c in range(T // rc):
                fn(pl.ds(c * rc, rc))

        def hs(j, rows):
            return h_ref[rows, GROUP * j:GROUP * (j + 1)]

        def acc_add(row, val):
            acc_s[8 * row:8 * row + 8, :] += _rsum8(val)

        def put_dh(j, rows, val):
            acc_add(ROW_BIN + j, val)
            dhb_ref[rows, GROUP * j:GROUP * (j + 1)] = val.astype(BF16)

        def dsilu(v, s):
            return s * (1.0 + v * (1.0 - s))

        def vec(r):
            return s256_ref[r:r + 1, :]

        def ln_bwd(rows):
            dyc = dy_ref[rows, :]
            zc = z_ref[rows, :]
            cen = zc - _rowmean(zc)
            rstd = lax.rsqrt(_rowmean(cen * cen) + LN_EPS)
            xhat = cen * rstd
            acc_w[0:8, :] += _rsum8(dyc * xhat)
            acc_w[8:16, :] += _rsum8(dyc)
            gdy = dyc * v1024_ref[1:2, :]
            dz = rstd * (gdy - _rowmean(gdy) - xhat * _rowmean(gdy * xhat))
            acc_w[16:24, :] += _rsum8(dz)
            dzb_ref[rows, :] = dz.astype(BF16)
            dx_ref[rows, :] = alpha * dz
        chunks(RC_WIDE, ln_bwd)

        segm = seg_ref[...]
        dzb = dzb_ref[...]
        for k in range(N_CHIPS):
            dmix_s[:, GROUP * k:GROUP * (k + 1)] = _dot_nt(dzb, wo_ref[k])
        sc["mean_s"][...] = _segdot(aux_ref[:, 0:GROUP], segm)
        pooled_b = aux_ref[:, 2 * GROUP:3 * GROUP].astype(BF16)
        sc["q_s"][...] = _dot(pooled_b, pw_ref[...])

        def centre(rows):
            cen = aux_ref[rows, 0:GROUP] - sc["mean_s"][rows, :]
            sc["t1_s"][rows, :] = cen * cen
            dv_in = hs(10, rows)
            cen_v = dv_in - _rowmean(dv_in)
            rstd_v = lax.rsqrt(_rowmean(cen_v * cen_v) + LN_EPS)
            xv = cen_v * rstd_v
            sc["xv_s"][rows, :] = xv
            sc["rv_s"][rows, :] = jnp.broadcast_to(rstd_v, xv.shape)
            sc["v_s"][rows, :] = xv * vec(4) + vec(5)
        chunks(RC, centre)

        sc["t2_s"][...] = _segdot(sc["t1_s"][...], segm)
        for n in range(nblk):
            blk = slice(n * SGU_BLOCK, (n + 1) * SGU_BLOCK)
            vst_s[n] = _vstack(sc["v_s"][blk, :])
            sc["sp_s"][blk, :] = _dot(wm_s[...], vst_s[n]) + sb_ref[...]

        def mixers(rows):
            a_val, a_glu, a_z = hs(0, rows), hs(1, rows), hs(2, rows)
            sg = _sig(a_glu)
            sc["a0_s"][rows, :] = a_val * sg
            sc["sg_s"][rows, :] = sg
            rstd_a = lax.rsqrt(sc["t2_s"][rows, :] + LN_EPS)
            xh = (aux_ref[rows, 0:GROUP] - sc["mean_s"][rows, :]) * rstd_a
            a2 = xh * vec(1) + vec(2)
            s2 = _sig(a2)
            sz = _sig(a_z)
            dya = dmix_s[rows, 0:GROUP]
            put_dh(2, rows, dya * (a2 * s2) * dsilu(a_z, sz))
            d_a2 = dya * (a_z * sz) * dsilu(a2, s2)
            acc_add(1, d_a2 * xh)
            acc_add(2, d_a2)
            gd = d_a2 * vec(1)
            sc["t1_s"][rows, :] = gd
            sc["t2_s"][rows, :] = gd * xh
            sc["xh_s"][rows, :] = xh
            sc["ra_s"][rows, :] = rstd_a
            b_b, b_c, b_h, b_z = hs(3, rows), hs(4, rows), hs(5, rows), hs(6, rows)
            cb = aux_ref[rows, GROUP:2 * GROUP]
            sz = _sig(b_z)
            dyb = dmix_s[rows, GROUP:2 * GROUP]
            put_dh(3, rows, dyb * cb * (b_z * sz))
            put_dh(6, rows, dyb * b_b * cb * dsilu(b_z, sz))
            ebuf[rows, :] = dyb * b_b * (b_z * sz)
            sc["ub_s"][rows, :] = b_c * b_h
            c_z = hs(8, rows)
            q = sc["q_s"][rows, :]
            sz = _sig(c_z)
            dyc = dmix_s[rows, 2 * GROUP:3 * GROUP]
            acc_add(3, dyc * q * (c_z * sz))
            put_dh(8, rows, dyc * q * vec(3) * dsilu(c_z, sz))
            dq_s[rows, :] = (dyc * vec(3) * (c_z * sz)).astype(BF16)
            d_u, d_z = hs(9, rows), hs(11, rows)
            sp = sc["sp_s"][rows, :]
            sz = _sig(d_z)
            dyd = dmix_s[rows, 3 * GROUP:4 * GROUP]
            put_dh(9, rows, dyd * sp * (d_z * sz))
            put_dh(11, rows, dyd * d_u * sp * dsilu(d_z, sz))
            sc["dsp_s"][rows, :] = dyd * d_u * (d_z * sz)
        chunks(RC, mixers)

        sc["m1_s"][...] = _segdot(sc["t1_s"][...], segm)
        sc["m2_s"][...] = _segdot(sc["t2_s"][...], segm)
        d_q = dq_s[...]
        pw_acc[...] += _dot_tn(pooled_b, d_q)
        sc["dpool_s"][...] = _dot_nt(d_q, pw_ref[...])
        grp = _lane_group(GROUP)
        for n in range(nblk):
            blk = slice(n * SGU_BLOCK, (n + 1) * SGU_BLOCK)
            dspb = sc["dsp_s"][blk, :]
            dsp_acc[...] += dspb
            dspb16 = dspb.astype(BF16)
            dvst = _dot(wmt_s[...], dspb16)
            dvb = None
            for hh in range(4):
                part = jnp.where(grp == hh, dvst[hh * SGU_BLOCK:(hh + 1) * SGU_BLOCK, :], 0.0)
                dvb = part if dvb is None else dvb + part
            sc["dvd_s"][blk, :] = dvb
            dwc = _dot_nt(dspb16, vst_s[n])
            osm_ref[ROW_WC:ROW_WC + SGU_BLOCK, :] += dwc[:, 0:GROUP]
            osm_ref[ROW_WC + SGU_BLOCK:ROW_WC + 2 * SGU_BLOCK, :] += dwc[:, GROUP:2 * GROUP]

        def ln_sums(rows):
            xh = sc["xh_s"][rows, :]
            d_a1 = sc["ra_s"][rows, :] * (sc["t1_s"][rows, :] - sc["m1_s"][rows, :] - xh * sc["m2_s"][rows, :])
            acc_add(0, d_a1)
            dbuf[rows, :] = d_a1
            pos = tile_idx * T + rows.start + lax.broadcasted_iota(jnp.int32, (RC, GROUP), 0) + 1
            lane = lax.broadcasted_iota(jnp.int32, (RC, GROUP), 1) // HEAD
            win = jnp.where(lane == 0, 2, jnp.where(lane == 1, 4, jnp.where(lane == 2, 8, 16)))
            fbuf[rows, :] = sc["dpool_s"][rows, :] / jnp.minimum(pos, win).astype(F32)
            d_v = sc["dvd_s"][rows, :]
            xv = sc["xv_s"][rows, :]
            acc_add(4, d_v * xv)
            acc_add(5, d_v)
            gd = d_v * vec(4)
            put_dh(10, rows, sc["rv_s"][rows, :] * (gd - _rowmean(gd) - xv * _rowmean(gd * xv)))
        chunks(RC, ln_sums)

        span = T + HALO_A - 8
        for p in range(1, 8):
            sh[p - 1, :, :] = dbuf[p:p + span, :]

        for r0 in range(0, T, ROWS):
            uc = sc["ub_s"][r0:r0 + ROWS, :]
            acc = None
            for k in range(KB):
                off = (KB - 1) - k + r0
                w = ebuf[off:off + ROWS, :]
                term = cbw_ref[k:k + 1, :] * w
                acc = term if acc is None else acc + term
                acc_add(ROW_CBW + k, uc * w)
            sc["du_s"][r0:r0 + ROWS, :] = acc
        ebuf[T:T + HALO_B, :] = ebuf[0:HALO_B, :]

        hi_lane = (lax.broadcasted_iota(jnp.int32, (1, 128), 1) // HEAD) == 1
        for r0 in range(0, T, ROWS):
            def win(col, j0, j1):
                s = None
                for j in range(j0, j1):
                    term = fbuf[r0 + j:r0 + j + ROWS, 128 * col:128 * (col + 1)]
                    s = term if s is None else s + term
                return s
            sc["cw_s"][r0:r0 + ROWS, 0:128] = win(0, 0, 2) + jnp.where(hi_lane, win(0, 2, 4), 0.0)
            sc["cw_s"][r0:r0 + ROWS, 128:256] = win(1, 0, 8) + jnp.where(hi_lane, win(1, 8, 16), 0.0)
        fbuf[T:T + HALO_C, :] = fbuf[0:HALO_C, :]

        def rest_bc(rows):
            d_u = sc["du_s"][rows, :]
            put_dh(4, rows, d_u * hs(5, rows))
            put_dh(5, rows, d_u * hs(4, rows))
            put_dh(7, rows, sc["cw_s"][rows, :] - sc["dpool_s"][rows, :])
        chunks(RC, rest_bc)

        dxt_s = sc["dxt_s"]

        def dx_term(k):
            term = _dot_nt(wi_ref[k], dhb_ref[:, COLS * k:COLS * (k + 1)])
            if k == 1:
                dxt_s[...] = term
            else:
                dxt_s[...] += term

        def conv_a(rows):
            a0c = sc["a0_s"][rows, :]
            acc = None
            for k in range(KA):
                off = (KA - 1) - k
                p, q8 = off % 8, off - off % 8
                w = dbuf[pl.ds(rows.start + q8, RC), :] if p == 0 else sh[p - 1, pl.ds(rows.start + q8, RC), :]
                term = caw_ref[k:k + 1, :] * w
                acc = term if acc is None else acc + term
                acc_add(ROW_CAW + k, a0c * w)
            sc["u_s"][rows, :] = acc
        n_chunks = T // RC
        after = {(n_chunks * j) // 3: j + 1 for j in range(3)}
        for c in range(n_chunks):
            conv_a(pl.ds(c * RC, RC))
            if c in after:
                dx_term(after[c])
        dbuf[T:T + HALO_A, :] = dbuf[0:HALO_A, :]

        def rest_a(rows):
            d_a0 = sc["u_s"][rows, :]
            sg = sc["sg_s"][rows, :]
            put_dh(0, rows, d_a0 * sg)
            put_dh(1, rows, d_a0 * hs(0, rows) * sg * (1.0 - sg))
        chunks(RC, rest_a)
        dx_term(0)
        dx_ref[...] += dxt_s[...].T

        @pl.when(i == nt - 1)
        def _():
            for row in list(range(6)) + list(range(ROW_CBW, ROW_CBW + KB)) + list(range(ROW_CAW, ROW_CAW + KA)) + list(
                    range(ROW_BIN, ROW_BIN + N_SLICES)):
                osm_ref[row:row + 1, :] = _colsum(acc_s[8 * row:8 * row + 8, :])
            for j, row in enumerate((ROW_LNG, ROW_LNB, ROW_BOUT)):
                cs = _colsum(acc_w[8 * j:8 * j + 8, :])
                for q in range(D_MODEL // GROUP):
                    osm_ref[row + q:row + q + 1, :] = cs[:, GROUP * q:GROUP * (q + 1)]
            r = lax.broadcasted_iota(jnp.int32, (SGU_BLOCK, GROUP), 0) // CHUNK
            c = (lax.broadcasted_iota(jnp.int32, (SGU_BLOCK, GROUP), 1) % SGU_BLOCK) // CHUNK
            for half in range(2):
                rows_ = slice(ROW_WC + half * SGU_BLOCK, ROW_WC + (half + 1) * SGU_BLOCK)
                osm_ref[rows_, :] = jnp.where(c <= r, osm_ref[rows_, :], 0.0)
            osm_ref[ROW_SB:ROW_SB + SGU_BLOCK, 0:128] = _segdot(dsp_acc[...], e4_ref[...])
            for g in range(4):
                osm_ref[ROW_PW:ROW_PW + HEAD, HEAD * g:HEAD * (g + 1)] = (
                    pw_acc[HEAD * g:HEAD * (g + 1), HEAD * g:HEAD * (g + 1)])

    def full(a):
        nd = a.ndim
        return pl.BlockSpec(a.shape, lambda i, _n=nd: (0,) * _n)

    def rows(width):
        return pl.BlockSpec((T, width), lambda i: (nt - 1 - i, 0))

    def acc(shape):
        return pl.BlockSpec(shape, lambda i: (0, 0))

    consts = (wi, caw, cbw, s256, seg, pw, wm, wmt, sb, wo, v1024, e4)
    in_specs = [rows(D_MODEL), rows(D_MODEL), rows(IN_WIDTH), rows(3 * GROUP)] + [full(a) for a in consts]
    out_specs = [rows(D_MODEL), rows(IN_WIDTH), rows(D_MODEL), acc((SM_ROWS, GROUP))]
    out_shape = [jax.ShapeDtypeStruct((S, D_MODEL), F32), jax.ShapeDtypeStruct((S, IN_WIDTH), BF16),
                 jax.ShapeDtypeStruct((S, D_MODEL), BF16), jax.ShapeDtypeStruct((SM_ROWS, GROUP), F32)]
    scratch_shapes = list(scratch.values())
    extra, aliases = (), {}
    if exch is not None:
        extra = tuple(exch)
        r_i, r_o = exch[4], exch[5]
        in_specs += [pl.BlockSpec(memory_space=pltpu.SMEM)] + [ANY] * 5
        out_specs += [ANY] * 3
        out_shape += [jax.ShapeDtypeStruct(r_i.shape, r_i.dtype), jax.ShapeDtypeStruct(r_o.shape, r_o.dtype),
                      jax.ShapeDtypeStruct((N_DEV, SM_ROWS, GROUP), F32)]
        scratch_shapes += [pltpu.SemaphoreType.DMA((13,)), pltpu.SemaphoreType.DMA((13,)),
                           pltpu.SemaphoreType.DMA((1,))]
        aliases = {20: 4, 21: 5}
    return pl.pallas_call(
        body, name="bwd_layer" if exch is None else "bwd_layer_exchange",
        grid=(nt,), in_specs=in_specs, out_specs=out_specs, out_shape=out_shape, scratch_shapes=scratch_shapes,
        input_output_aliases=aliases,
        compiler_params=_vmem_params(dimension_semantics=("arbitrary",), has_side_effects=exch is not None),
    )(dy, z, h, aux, *consts, *extra)


def _bwd_layer_slabwise(dy, z, h, aux, wi, caw, cbw, s256, seg, pw, wm, wmt, sb, wo, v1024, e4, *, tile, exch=None):
    S = dy.shape[0]
    T = tile
    nt = S // T
    alpha = float((2.0 * 4) ** 0.25)
    n_in = 16 + (6 if exch is not None else 0)
    n_out = 4 + (3 if exch is not None else 0)

    def body(*refs):
        (dy_ref, z_ref, h_ref, aux_ref, wi_ref, caw_ref, cbw_ref, s256_ref, seg_ref, pw_ref, wm_ref, wmt_ref,
         sb_ref, wo_ref, v1024_ref, e4_ref) = refs[0:16]
        dx_ref, dhb_ref, dzb_ref, osm_ref = refs[n_in:n_in + 4]
        dbuf, ebuf, fbuf, a0_s, u_s, wm_s, wmt_s, dsp_acc, pw_acc = refs[n_in + n_out:n_in + n_out + 9]
        i = pl.program_id(0)
        tile_idx = nt - 1 - i
        if exch is not None:
            l_ref, p_i, p_o, sm = refs[16:20]
            r_i, r_o, r_sm = refs[n_in + 4:n_in + 7]
            _exchange_comm(i == 0, i == nt - 1, l_ref[0], p_i, p_o, sm, r_i, r_o, r_sm, *refs[n_in + n_out + 9:])

        @pl.when(i == 0)
        def _():
            dbuf[T:T + HALO_A, :] = jnp.zeros((HALO_A, GROUP), F32)
            ebuf[T:T + HALO_B, :] = jnp.zeros((HALO_B, GROUP), F32)
            fbuf[T:T + HALO_C, :] = jnp.zeros((HALO_C, GROUP), F32)
            _sgu_masks(wm_ref, wmt_ref, wm_s, wmt_s)
            osm_ref[...] = jnp.zeros_like(osm_ref)
            dsp_acc[...] = jnp.zeros_like(dsp_acc)
            pw_acc[...] = jnp.zeros_like(pw_acc)

        def hs(j):
            return h_ref[:, GROUP * j:GROUP * (j + 1)]

        def acc_row(row, val):
            osm_ref[row:row + 1, :] += _colsum(val)

        def acc_wide(row, val):
            cs = _colsum(val)
            for j in range(D_MODEL // GROUP):
                osm_ref[row + j:row + j + 1, :] += cs[:, GROUP * j:GROUP * (j + 1)]

        def put_dh(j, val):
            acc_row(ROW_BIN + j, val)
            dhb_ref[:, GROUP * j:GROUP * (j + 1)] = val.astype(BF16)

        def dsilu(v, s):
            return s * (1.0 + v * (1.0 - s))

        dy = dy_ref[...]
        z = z_ref[...]
        cen = z - _rowmean(z)
        rstd = lax.rsqrt(_rowmean(cen * cen) + LN_EPS)
        xhat = cen * rstd
        acc_wide(ROW_LNG, dy * xhat)
        acc_wide(ROW_LNB, dy)
        gdy = dy * v1024_ref[1:2, :]
        dz = rstd * (gdy - _rowmean(gdy) - xhat * _rowmean(gdy * xhat))
        acc_wide(ROW_BOUT, dz)
        dzb = dz.astype(BF16)
        dzb_ref[...] = dzb

        def dmix(k):
            return _dot_nt(dzb, wo_ref[k])

        segm = seg_ref[...]

        a_val, a_glu, a_z = hs(0), hs(1), hs(2)
        sg = _sig(a_glu)
        a0_s[...] = a_val * sg
        a1 = aux_ref[:, 0:GROUP]
        cen = a1 - _segdot(a1, segm)
        rstd_a = lax.rsqrt(_segdot(cen * cen, segm) + LN_EPS)
        xh = cen * rstd_a
        a2 = xh * s256_ref[1:2, :] + s256_ref[2:3, :]
        s2 = _sig(a2)
        sz = _sig(a_z)
        dya = dmix(0)
        put_dh(2, dya * (a2 * s2) * dsilu(a_z, sz))
        d_a2 = dya * (a_z * sz) * dsilu(a2, s2)
        acc_row(1, d_a2 * xh)
        acc_row(2, d_a2)
        gd = d_a2 * s256_ref[1:2, :]
        d_a1 = rstd_a * (gd - _segdot(gd, segm) - xh * _segdot(gd * xh, segm))
        acc_row(0, d_a1)
        dbuf[0:T, :] = d_a1
        for r0 in range(0, T, ROWS):
            a0c = a0_s[r0:r0 + ROWS, :]
            acc = None
            for k in range(KA):
                off = (KA - 1) - k + r0
                w = dbuf[off:off + ROWS, :]
                term = caw_ref[k:k + 1, :] * w
                acc = term if acc is None else acc + term
                acc_row(ROW_CAW + k, a0c * w)
            u_s[r0:r0 + ROWS, :] = acc
        dbuf[T:T + HALO_A, :] = dbuf[0:HALO_A, :]
        d_a0 = u_s[...]
        put_dh(0, d_a0 * sg)
        put_dh(1, d_a0 * a_val * sg * (1.0 - sg))

        b_b, b_c, b_h, b_z = hs(3), hs(4), hs(5), hs(6)
        cb = aux_ref[:, GROUP:2 * GROUP]
        sz = _sig(b_z)
        dyb = dmix(1)
        put_dh(3, dyb * cb * (b_z * sz))
        put_dh(6, dyb * b_b * cb * dsilu(b_z, sz))
        ebuf[0:T, :] = dyb * b_b * (b_z * sz)
        a0_s[...] = b_c * b_h
        for r0 in range(0, T, ROWS):
            uc = a0_s[r0:r0 + ROWS, :]
            acc = None
            for k in range(KB):
                off = (KB - 1) - k + r0
                w = ebuf[off:off + ROWS, :]
                term = cbw_ref[k:k + 1, :] * w
                acc = term if acc is None else acc + term
                acc_row(ROW_CBW + k, uc * w)
            u_s[r0:r0 + ROWS, :] = acc
        ebuf[T:T + HALO_B, :] = ebuf[0:HALO_B, :]
        d_u = u_s[...]
        put_dh(4, d_u * b_h)
        put_dh(5, d_u * b_c)

        c_z = hs(8)
        pooled = aux_ref[:, 2 * GROUP:3 * GROUP]
        pooled_b = pooled.astype(BF16)
        q = _dot(pooled_b, pw_ref[...])
        sz = _sig(c_z)
        dyc = dmix(2)
        ps = s256_ref[3:4, :]
        acc_row(3, dyc * q * (c_z * sz))
        put_dh(8, dyc * q * ps * dsilu(c_z, sz))
        d_q = (dyc * ps * (c_z * sz)).astype(BF16)
        pw_acc[...] += _dot_tn(pooled_b, d_q)
        d_pooled = _dot_nt(d_q, pw_ref[...])
        fbuf[0:T, :] = d_pooled / _pool_cnt(tile_idx, T)
        hi_lane = (lax.broadcasted_iota(jnp.int32, (1, 128), 1) // HEAD) == 1
        for r0 in range(0, T, ROWS):
            def win(col, j0, j1):
                s = None
                for j in range(j0, j1):
                    term = fbuf[r0 + j:r0 + j + ROWS, 128 * col:128 * (col + 1)]
                    s = term if s is None else s + term
                return s
            u_s[r0:r0 + ROWS, 0:128] = win(0, 0, 2) + jnp.where(hi_lane, win(0, 2, 4), 0.0)
            u_s[r0:r0 + ROWS, 128:256] = win(1, 0, 8) + jnp.where(hi_lane, win(1, 8, 16), 0.0)
        fbuf[T:T + HALO_C, :] = fbuf[0:HALO_C, :]
        put_dh(7, u_s[...] - d_pooled)

        d_u_, d_v_, d_z_ = hs(9), hs(10), hs(11)
        cen = d_v_ - _rowmean(d_v_)
        rstd_v = lax.rsqrt(_rowmean(cen * cen) + LN_EPS)
        xv = cen * rstd_v
        v = xv * s256_ref[4:5, :] + s256_ref[5:6, :]
        sz = _sig(d_z_)
        dyd = dmix(3)
        d_sp = dyd * d_u_ * (d_z_ * sz)
        grp = _lane_group(GROUP)
        sps, dvs = [], []
        for n in range(T // SGU_BLOCK):
            blk = slice(n * SGU_BLOCK, (n + 1) * SGU_BLOCK)
            vst = _vstack(v[blk, :])
            sps.append(_dot(wm_s[...], vst) + sb_ref[...])
            dspb = d_sp[blk, :]
            dsp_acc[...] += dspb
            dspb16 = dspb.astype(BF16)
            dvst = _dot(wmt_s[...], dspb16)
            dvb = None
            for hh in range(4):
                part = jnp.where(grp == hh, dvst[hh * SGU_BLOCK:(hh + 1) * SGU_BLOCK, :], 0.0)
                dvb = part if dvb is None else dvb + part
            dvs.append(dvb)
            dwc = _dot_nt(dspb16, vst)
            osm_ref[ROW_WC:ROW_WC + SGU_BLOCK, :] += dwc[:, 0:GROUP]
            osm_ref[ROW_WC + SGU_BLOCK:ROW_WC + 2 * SGU_BLOCK, :] += dwc[:, GROUP:2 * GROUP]
        sp = jnp.concatenate(sps, axis=0)
        d_v = jnp.concatenate(dvs, axis=0)
        put_dh(9, dyd * sp * (d_z_ * sz))
        put_dh(11, dyd * d_u_ * sp * dsilu(d_z_, sz))
        acc_row(4, d_v * xv)
        acc_row(5, d_v)
        gd = d_v * s256_ref[4:5, :]
        put_dh(10, rstd_v * (gd - _rowmean(gd) - xv * _rowmean(gd * xv)))

        dx = alpha * dz
        for k in range(N_CHIPS):
            dx = dx + _dot_nt(dhb_ref[:, COLS * k:COLS * (k + 1)], wi_ref[k])
        dx_ref[...] = dx

        @pl.when(i == nt - 1)
        def _():
            r = lax.broadcasted_iota(jnp.int32, (SGU_BLOCK, GROUP), 0) // CHUNK
            c = (lax.broadcasted_iota(jnp.int32, (SGU_BLOCK, GROUP), 1) % SGU_BLOCK) // CHUNK
            for half in range(2):
                rows_ = slice(ROW_WC + half * SGU_BLOCK, ROW_WC + (half + 1) * SGU_BLOCK)
                osm_ref[rows_, :] = jnp.where(c <= r, osm_ref[rows_, :], 0.0)
            osm_ref[ROW_SB:ROW_SB + SGU_BLOCK, 0:128] = _segdot(dsp_acc[...], e4_ref[...])
            for g in range(4):
                osm_ref[ROW_PW:ROW_PW + HEAD, HEAD * g:HEAD * (g + 1)] = (
                    pw_acc[HEAD * g:HEAD * (g + 1), HEAD * g:HEAD * (g + 1)])

    def full(a):
        nd = a.ndim
        return pl.BlockSpec(a.shape, lambda i, _n=nd: (0,) * _n)

    def rows(width):
        return pl.BlockSpec((T, width), lambda i: (nt - 1 - i, 0))

    def acc(shape):
        return pl.BlockSpec(shape, lambda i: (0, 0))

    consts = (wi, caw, cbw, s256, seg, pw, wm, wmt, sb, wo, v1024, e4)
    in_specs = [rows(D_MODEL), rows(D_MODEL), rows(IN_WIDTH), rows(3 * GROUP)] + [full(a) for a in consts]
    out_specs = [rows(D_MODEL), rows(IN_WIDTH), rows(D_MODEL), acc((SM_ROWS, GROUP))]
    out_shape = [jax.ShapeDtypeStruct((S, D_MODEL), F32), jax.ShapeDtypeStruct((S, IN_WIDTH), BF16),
                 jax.ShapeDtypeStruct((S, D_MODEL), BF16), jax.ShapeDtypeStruct((SM_ROWS, GROUP), F32)]
    scratch = [pltpu.VMEM((T + HALO_A, GROUP), F32), pltpu.VMEM((T + HALO_B, GROUP), F32),
               pltpu.VMEM((T + HALO_C, GROUP), F32), pltpu.VMEM((T, GROUP), F32), pltpu.VMEM((T, GROUP), F32),
               pltpu.VMEM((SGU_BLOCK, 4 * SGU_BLOCK), BF16), pltpu.VMEM((4 * SGU_BLOCK, SGU_BLOCK), BF16),
               pltpu.VMEM((SGU_BLOCK, GROUP), F32), pltpu.VMEM((GROUP, GROUP), F32)]
    extra, aliases = (), {}
    if exch is not None:
        extra = tuple(exch)
        r_i, r_o = exch[4], exch[5]
        in_specs += [pl.BlockSpec(memory_space=pltpu.SMEM)] + [ANY] * 5
        out_specs += [ANY] * 3
        out_shape += [jax.ShapeDtypeStruct(r_i.shape, r_i.dtype), jax.ShapeDtypeStruct(r_o.shape, r_o.dtype),
                      jax.ShapeDtypeStruct((N_DEV, SM_ROWS, GROUP), F32)]
        scratch += [pltpu.SemaphoreType.DMA((13,)), pltpu.SemaphoreType.DMA((13,)), pltpu.SemaphoreType.DMA((1,))]
        aliases = {20: 4, 21: 5}
    return pl.pallas_call(
        body, name="bwd_layer" if exch is None else "bwd_layer_exchange",
        grid=(nt,), in_specs=in_specs, out_specs=out_specs, out_shape=out_shape, scratch_shapes=scratch,
        input_output_aliases=aliases,
        compiler_params=_vmem_params(dimension_semantics=("arbitrary",), has_side_effects=exch is not None),
    )(dy, z, h, aux, *consts, *extra)


def _dw_in(layer, xb, dhb, slab, slab16, *, tk, small=None):
    S = xb.shape[0]
    ns = S // tk

    def body(*refs):
        l_ref, a_ref, b_ref = refs[0:3]
        o_ref, o16_ref = refs[n_in:n_in + 2]
        if small is not None:
            first = (pl.program_id(0) == 0) & (pl.program_id(1) == 0)
            last = (pl.program_id(0) == N_CHIPS - 1) & (pl.program_id(1) == ns - 1)
            _exchange_comm(first, last, None, None, None, refs[5], None, None, refs[n_in + 2], *refs[n_in + 3:])

        @pl.when(pl.program_id(1) == 0)
        def _():
            o_ref[...] = jnp.zeros_like(o_ref)
        o_ref[...] += _dot_tn(a_ref[...], b_ref[...])

        @pl.when(pl.program_id(1) == ns - 1)
        def _():
            o16_ref[...] = o_ref[...].astype(BF16)

    o_spec = pl.BlockSpec((None, None, D_MODEL, COLS), lambda j, s, l: (l[0], j, 0, 0))
    in_specs = [pl.BlockSpec((tk, D_MODEL), lambda j, s, l: (s, 0)), pl.BlockSpec((tk, COLS), lambda j, s, l: (s, j)),
                ANY, ANY]
    out_specs = [o_spec, o_spec]
    out_shape = [jax.ShapeDtypeStruct(slab.shape, F32), jax.ShapeDtypeStruct(slab.shape, BF16)]
    scratch, extra = [], ()
    if small is not None:
        extra = (small,)
        in_specs += [ANY]
        out_specs += [ANY]
        out_shape += [jax.ShapeDtypeStruct((N_DEV, SM_ROWS, GROUP), F32)]
        scratch = [pltpu.SemaphoreType.DMA((13,)), pltpu.SemaphoreType.DMA((13,)), pltpu.SemaphoreType.DMA((1,))]
    n_in = 5 + len(extra)
    grid_spec = pltpu.PrefetchScalarGridSpec(
        num_scalar_prefetch=1, grid=(N_CHIPS, ns), in_specs=in_specs, out_specs=out_specs, scratch_shapes=scratch)
    return pl.pallas_call(
        body, name="dw_in" if small is None else "dw_in_exchange", grid_spec=grid_spec, out_shape=out_shape,
        input_output_aliases={3: 0, 4: 1},
        compiler_params=_vmem_params(dimension_semantics=("arbitrary", "arbitrary"), has_side_effects=small is not None),
    )(layer, xb, dhb, slab, slab16, *extra)


def _dw_out(layer, mixb, dzb, slab, slab16, *, tk):
    S = mixb.shape[0]
    ns = S // tk

    def body(l_ref, a_ref, b_ref, slab_ref, slab16_ref, o_ref, o16_ref):
        del l_ref, slab_ref, slab16_ref

        @pl.when(pl.program_id(0) == 0)
        def _():
            o_ref[...] = jnp.zeros_like(o_ref)
        o_ref[...] += _dot_tn(a_ref[...], b_ref[...]).reshape(N_CHIPS, GROUP, D_MODEL)

        @pl.when(pl.program_id(0) == ns - 1)
        def _():
            o16_ref[...] = o_ref[...].astype(BF16)

    o_spec = pl.BlockSpec((None, N_CHIPS, GROUP, D_MODEL), lambda s, l: (l[0], 0, 0, 0))
    grid_spec = pltpu.PrefetchScalarGridSpec(
        num_scalar_prefetch=1, grid=(ns,),
        in_specs=[pl.BlockSpec((tk, D_MODEL), lambda s, l: (s, 0)), pl.BlockSpec((tk, D_MODEL), lambda s, l: (s, 0)),
                  ANY, ANY],
        out_specs=[o_spec, o_spec])
    return pl.pallas_call(
        body, name="dw_out", grid_spec=grid_spec,
        out_shape=[jax.ShapeDtypeStruct(slab.shape, F32), jax.ShapeDtypeStruct(slab.shape, BF16)],
        input_output_aliases={3: 0, 4: 1},
        compiler_params=_vmem_params(dimension_semantics=("arbitrary",)),
    )(layer, mixb, dzb, slab, slab16)


def _adamw_math(w, g, m, v):
    nm = ADAM_B1 * m + (1.0 - ADAM_B1) * g
    nv = ADAM_B2 * v + (1.0 - ADAM_B2) * (g * g)
    c1 = 1.0 - ADAM_B1 ** ADAM_STEP
    c2 = 1.0 - ADAM_B2 ** ADAM_STEP
    return -ADAM_LR * ((nm / c1) / (jnp.sqrt(nv / c2) + ADAM_EPS) + ADAM_WD * w), nm, nv


def _adamw_small(ws, gs, ms, vs):
    n = len(ws)

    def body(*refs):
        for j in range(n):
            d, nm, nv = _adamw_math(*(refs[k * n + j][...] for k in range(4)))
            refs[4 * n + j][...] = d
            refs[5 * n + j][...] = nm
            refs[6 * n + j][...] = nv

    shapes = [jax.ShapeDtypeStruct(w.shape, F32) for w in ws]
    outs = pl.pallas_call(body, name="adamw_small", out_shape=shapes * 3, compiler_params=_vmem_params())(
        *ws, *gs, *ms, *vs)
    return outs[0:n], outs[n:2 * n], outs[2 * n:3 * n]


def _adamw(w, g, m, v, *, rows_per_step, name, copy_g=False):
    R, C = w.shape
    tr = rows_per_step

    def body(w_ref, g_ref, m_ref, v_ref, d_ref, nm_ref, nv_ref, *g_out):
        g_ = g_ref[...]
        d_ref[...], nm_ref[...], nv_ref[...] = _adamw_math(w_ref[...], g_, m_ref[...], v_ref[...])
        if copy_g:
            g_out[0][...] = g_

    spec = pl.BlockSpec((tr, C), lambda i: (i, 0))
    n_out = 4 if copy_g else 3
    return pl.pallas_call(
        body, name=name, grid=(R // tr,),
        in_specs=[spec] * 4, out_specs=[spec] * n_out,
        out_shape=[jax.ShapeDtypeStruct((R, C), F32)] * n_out,
        compiler_params=_vmem_params(dimension_semantics=("arbitrary",)),
    )(w, g, m, v)


def _gather_weights(wi16, wo16, cw):
    L = wi16.shape[0]
    hi_rows, ho_rows = D_MODEL // 2, GROUP // 2
    n_ici = 2 * L + 1
    n_fwd = 2 * L

    def body(wi_ref, wo_ref, cw_ref, *rest):
        wig = rest[0:L]
        wog = rest[L:2 * L]
        cwg = rest[2 * L]
        send_sems, recv_sems, loc_sems = rest[2 * L + 1:]
        x, y, c = _place()
        me_k = 2 * x + y
        sibling = (x, y, 1 - c)
        chips = _other_chips(x, y)

        def half_i(ref, blk):
            return ref.at[blk, pl.ds(c * hi_rows, hi_rows), :]

        def half_o(ref, blk):
            return ref.at[blk, pl.ds(c * ho_rows, ho_rows), :]

        def other_half_i(ref, blk):
            return ref.at[blk, pl.ds((1 - c) * hi_rows, hi_rows), :]

        def other_half_o(ref, blk):
            return ref.at[blk, pl.ds((1 - c) * ho_rows, ho_rows), :]

        local = []
        for l in range(L):
            local.append(pltpu.make_async_copy(wi_ref.at[l], wig[l].at[me_k], loc_sems.at[2 * l]))
            local.append(pltpu.make_async_copy(wo_ref.at[l], wog[l].at[me_k], loc_sems.at[2 * l + 1]))
        local.append(pltpu.make_async_copy(cw_ref, cwg.at[me_k], loc_sems.at[2 * L]))
        for cp in local:
            cp.start()

        def remote(src, dst, sem, to):
            return pltpu.make_async_remote_copy(src_ref=src, dst_ref=dst, send_sem=send_sems.at[sem],
                                                recv_sem=recv_sems.at[sem], device_id=to, device_id_type=MESH)

        sends = []
        for r, (px, py, _) in enumerate(chips):
            to = (px, py, c)
            for l in range(L):
                sends.append(remote(half_i(wi_ref, l), half_i(wig[l], me_k), r * n_ici + 2 * l, to))
                sends.append(remote(half_o(wo_ref, l), half_o(wog[l], me_k), r * n_ici + 2 * l + 1, to))
            sends.append(remote(cw_ref, cwg.at[me_k], r * n_ici + 2 * L, to))
        for cp in sends:
            cp.start()

        base = 3 * n_ici
        fwds = []
        for r, (px, py, pk) in enumerate(chips):
            for l in range(L):
                remote(half_i(wig[l], pk), half_i(wig[l], pk), r * n_ici + 2 * l, sibling).wait_recv()
                f = remote(half_i(wig[l], pk), half_i(wig[l], pk), base + r * n_fwd + 2 * l, sibling)
                f.start()
                fwds.append(f)
                remote(half_o(wog[l], pk), half_o(wog[l], pk), r * n_ici + 2 * l + 1, sibling).wait_recv()
                f = remote(half_o(wog[l], pk), half_o(wog[l], pk), base + r * n_fwd + 2 * l + 1, sibling)
                f.start()
                fwds.append(f)
            remote(cwg.at[pk], cwg.at[pk], r * n_ici + 2 * L, sibling).wait_recv()
        for r, (px, py, pk) in enumerate(chips):
            for l in range(L):
                remote(other_half_i(wig[l], pk), other_half_i(wig[l], pk), base + r * n_fwd + 2 * l, sibling).wait_recv()
                remote(other_half_o(wog[l], pk), other_half_o(wog[l], pk), base + r * n_fwd + 2 * l + 1, sibling).wait_recv()
        for cp in sends + fwds:
            cp.wait_send()
        for cp in local:
            cp.wait()

    n_sem = 3 * n_ici + 3 * n_fwd
    out_shape = ([jax.ShapeDtypeStruct((N_CHIPS, D_MODEL, COLS), BF16)] * L
                 + [jax.ShapeDtypeStruct((N_CHIPS, GROUP, D_MODEL), BF16)] * L
                 + [jax.ShapeDtypeStruct((N_CHIPS,) + cw.shape, F32)])
    outs = pl.pallas_call(
        body, name="gather_weights",
        in_specs=[ANY, ANY, ANY], out_specs=[ANY] * (2 * L + 1), out_shape=out_shape,
        scratch_shapes=[pltpu.SemaphoreType.DMA((n_sem,)), pltpu.SemaphoreType.DMA((n_sem,)),
                        pltpu.SemaphoreType.DMA((2 * L + 1,))],
        compiler_params=pltpu.CompilerParams(has_side_effects=True),
    )(wi16, wo16, cw)
    return outs[0:L], outs[L:2 * L], outs[2 * L]


def _swap_halves(l_arr, gwi, gwo, ri, ro):
    hi_rows, ho_rows = D_MODEL // 2, GROUP // 2

    def body(l_ref, gwi_ref, gwo_ref, ri_in, ro_in, ri_ref, ro_ref, send_sems, recv_sems):
        del ri_in, ro_in
        x, y, c = _place()
        l = l_ref[0]
        sibling = (x, y, 1 - c)
        cps = [
            pltpu.make_async_remote_copy(src_ref=gwi_ref.at[l, :, pl.ds((1 - c) * hi_rows, hi_rows), :],
                                         dst_ref=ri_ref.at[l], send_sem=send_sems.at[0], recv_sem=recv_sems.at[0],
                                         device_id=sibling, device_id_type=MESH),
            pltpu.make_async_remote_copy(src_ref=gwo_ref.at[l, :, pl.ds((1 - c) * ho_rows, ho_rows), :],
                                         dst_ref=ro_ref.at[l], send_sem=send_sems.at[1], recv_sem=recv_sems.at[1],
                                         device_id=sibling, device_id_type=MESH),
        ]
        for cp in cps:
            cp.start()
        for cp in cps:
            cp.wait()

    return pl.pallas_call(
        body, name="swap_halves",
        in_specs=[pl.BlockSpec(memory_space=pltpu.SMEM), ANY, ANY, ANY, ANY], out_specs=[ANY, ANY],
        out_shape=[jax.ShapeDtypeStruct(ri.shape, ri.dtype), jax.ShapeDtypeStruct(ro.shape, ro.dtype)],
        input_output_aliases={3: 0, 4: 1},
        scratch_shapes=[pltpu.SemaphoreType.DMA((2,)), pltpu.SemaphoreType.DMA((2,))],
        compiler_params=pltpu.CompilerParams(has_side_effects=True),
    )(l_arr, gwi, gwo, ri, ro)


def _add_halves(cl_arr, g, r, p, *, rows, cols, tr, name):
    nb = rows // tr

    def body(cl_ref, g_ref, r_ref, p_in, o_ref):
        del cl_ref, p_in
        o_ref[...] = (g_ref[...] + r_ref[...].astype(F32)).astype(o_ref.dtype)

    grid_spec = pltpu.PrefetchScalarGridSpec(
        num_scalar_prefetch=1, grid=(N_CHIPS, nb),
        in_specs=[pl.BlockSpec((None, None, tr, cols), lambda k, i, cl: (cl[1], k, cl[0] * nb + i, 0)),
                  pl.BlockSpec((None, None, tr, cols), lambda k, i, cl: (cl[1], k, i, 0)), ANY],
        out_specs=pl.BlockSpec((None, None, tr, cols), lambda k, i, cl: (cl[1], k, i, 0)))
    return pl.pallas_call(
        body, name=name, grid_spec=grid_spec,
        out_shape=jax.ShapeDtypeStruct(p.shape, p.dtype),
        input_output_aliases={3: 0},
        compiler_params=_vmem_params(dimension_semantics=("arbitrary",) * 2),
    )(cl_arr, g, r, p)


def _exchange_last(l_arr, p_i, p_o, r_i, r_o):
    def body(l_ref, p_i_ref, p_o_ref, ri_in, ro_in, ri_ref, ro_ref, send_sems, recv_sems):
        del ri_in, ro_in
        always = l_ref[0] >= 0
        _exchange_comm(always, always, l_ref[0], p_i_ref, p_o_ref, None, ri_ref, ro_ref, None,
                       send_sems, recv_sems, None)

    return pl.pallas_call(
        body, name="exchange_last",
        in_specs=[pl.BlockSpec(memory_space=pltpu.SMEM)] + [ANY] * 4, out_specs=[ANY] * 2,
        out_shape=[jax.ShapeDtypeStruct(r_i.shape, r_i.dtype), jax.ShapeDtypeStruct(r_o.shape, r_o.dtype)],
        input_output_aliases={3: 0, 4: 1},
        scratch_shapes=[pltpu.SemaphoreType.DMA((13,)), pltpu.SemaphoreType.DMA((13,))],
        compiler_params=pltpu.CompilerParams(has_side_effects=True),
    )(l_arr, p_i, p_o, r_i, r_o)


def _sum_small(r_sms):
    L = len(r_sms)

    def body(*refs):
        o_ref = refs[L]
        for l in range(L):
            acc = refs[l][0]
            for d in range(1, N_DEV):
                acc = acc + refs[l][d]
            o_ref[l] = acc

    return pl.pallas_call(
        body, name="sum_small",
        out_shape=jax.ShapeDtypeStruct((L,) + r_sms[0].shape[1:], F32),
        compiler_params=_vmem_params(),
    )(*r_sms)


def _sum_chunks(kc_arr, p, r, *, rows, cols, tr, name):
    L = p.shape[0]
    nb = rows // tr

    def body(kc_ref, p_ref, r0_ref, r1_ref, r2_ref, o_ref):
        del kc_ref
        f = lambda ref: ref[...].astype(F32)
        o_ref[...] = ((f(p_ref) + f(r0_ref)) + f(r1_ref)) + f(r2_ref)

    def rspec(j):
        return pl.BlockSpec((None, None, tr, cols), lambda l, i, kc, _j=j: (_j, l, i, 0))

    grid_spec = pltpu.PrefetchScalarGridSpec(
        num_scalar_prefetch=1, grid=(L, nb),
        in_specs=[pl.BlockSpec((None, None, tr, cols), lambda l, i, kc: (l, kc[0], i, 0)), rspec(0), rspec(1), rspec(2)],
        out_specs=pl.BlockSpec((None, tr, cols), lambda l, i, kc: (l, kc[1] * nb + i, 0)))
    return pl.pallas_call(
        body, name=name, grid_spec=grid_spec,
        out_shape=jax.ShapeDtypeStruct((L, 2 * rows, cols), F32),
        compiler_params=_vmem_params(dimension_semantics=("arbitrary",) * 2),
    )(kc_arr, p, r, r, r)


def _share_result(gi, go):
    hi_rows, ho_rows = gi.shape[1] // 2, go.shape[1] // 2

    def body(gi_ref, go_ref, oi_ref, oo_ref, send_sems, recv_sems):
        del gi_ref, go_ref
        x, y, c = _place()
        sibling = (x, y, 1 - c)
        cps = []
        for j, (ref, n) in enumerate(((oi_ref, hi_rows), (oo_ref, ho_rows))):
            mine = ref.at[:, pl.ds(c * n, n), :]
            cps.append(pltpu.make_async_remote_copy(src_ref=mine, dst_ref=mine, send_sem=send_sems.at[j],
                                                    recv_sem=recv_sems.at[j], device_id=sibling, device_id_type=MESH))
        for cp in cps:
            cp.start()
        for j, (ref, n) in enumerate(((oi_ref, hi_rows), (oo_ref, ho_rows))):
            theirs = ref.at[:, pl.ds((1 - c) * n, n), :]
            pltpu.make_async_remote_copy(src_ref=theirs, dst_ref=theirs, send_sem=send_sems.at[j],
                                         recv_sem=recv_sems.at[j], device_id=sibling, device_id_type=MESH).wait_recv()
        for cp in cps:
            cp.wait_send()

    return pl.pallas_call(
        body, name="share_result",
        in_specs=[ANY, ANY], out_specs=[ANY, ANY],
        out_shape=[jax.ShapeDtypeStruct(gi.shape, F32), jax.ShapeDtypeStruct(go.shape, F32)],
        input_output_aliases={0: 0, 1: 1},
        scratch_shapes=[pltpu.SemaphoreType.DMA((2,)), pltpu.SemaphoreType.DMA((2,))],
        compiler_params=pltpu.CompilerParams(has_side_effects=True),
    )(gi, go)


WEIGHTS = ("ln_g", "ln_b", "w_in", "b_in", "conv_a_w", "conv_a_b", "norm_a_g", "norm_a_b", "conv_b_w", "pool_w",
           "pool_scale", "sgu_ln_g", "sgu_ln_b", "sgu_w", "sgu_bias", "w_out", "b_out")


def _pad_rows(a, rows):
    return jnp.pad(a, ((0, rows - a.shape[0]), (0, 0)))


def _indicator_consts():
    seg = jnp.where((jnp.arange(GROUP)[:, None] // HEAD) == (jnp.arange(GROUP)[None, :] // HEAD),
                    1.0 / HEAD, 0.0).astype(BF16)
    e4 = ((jnp.arange(GROUP)[:, None] // HEAD) == jnp.arange(128)[None, :]).astype(BF16)
    return seg, e4


def _layer_consts(p, conv_full, l):
    same_head = jnp.eye(4, dtype=F32)[:, None, :, None] > 0
    caw = _pad_rows(conv_full[l, :KA], 32)
    cbw = _pad_rows(conv_full[l, KA:], 8)
    s256 = _pad_rows(jnp.stack([p["conv_a_b"][l], p["norm_a_g"][l], p["norm_a_b"][l], p["pool_scale"][l],
                                p["sgu_ln_g"][l], p["sgu_ln_b"][l]]), 8)
    pw = jnp.where(same_head, p["pool_w"][l][:, :, None, :], 0.0).reshape(GROUP, GROUP).astype(BF16)
    wm = jnp.transpose(p["sgu_w"][l], (1, 0, 2)).reshape(SGU_BLOCK, 4 * SGU_BLOCK)
    wmt = jnp.transpose(p["sgu_w"][l], (0, 2, 1)).reshape(4 * SGU_BLOCK, SGU_BLOCK)
    sb = jnp.repeat(p["sgu_bias"][l].T, HEAD, axis=1)
    v1024 = _pad_rows(jnp.stack([p["b_out"][l], p["ln_g"][l], p["ln_b"][l]]), 8)
    return dict(caw=caw, cbw=cbw, s256=s256, pw=pw, wm=wm, wmt=wmt, sb=sb, v1024=v1024, bin=p["b_in"][l][None, :])


def _unpack_small(sm):
    L = sm.shape[0]
    owc = jnp.concatenate([sm[:, ROW_WC:ROW_WC + SGU_BLOCK], sm[:, ROW_WC + SGU_BLOCK:ROW_WC + 2 * SGU_BLOCK]], axis=2)
    return dict(
        conv_a_b=sm[:, 0], norm_a_g=sm[:, 1], norm_a_b=sm[:, 2], pool_scale=sm[:, 3], sgu_ln_g=sm[:, 4],
        sgu_ln_b=sm[:, 5], conv_b_w=sm[:, ROW_CBW:ROW_CBW + KB], conv_a_w=sm[:, ROW_CAW:ROW_CAW + KA],
        pool_w=jnp.transpose(sm[:, ROW_PW:ROW_PW + HEAD].reshape(L, HEAD, 4, HEAD), (0, 2, 1, 3)),
        ln_g=sm[:, ROW_LNG:ROW_LNG + 4].reshape(L, D_MODEL), ln_b=sm[:, ROW_LNB:ROW_LNB + 4].reshape(L, D_MODEL),
        b_out=sm[:, ROW_BOUT:ROW_BOUT + 4].reshape(L, D_MODEL),
        b_in=sm[:, ROW_BIN:ROW_BIN + N_SLICES].reshape(L, IN_WIDTH),
        sgu_w=jnp.transpose(owc.reshape(L, SGU_BLOCK, 4, SGU_BLOCK), (0, 2, 1, 3)),
        sgu_bias=jnp.transpose(sm[:, ROW_SB:ROW_SB + SGU_BLOCK, 0:4], (0, 2, 1)))


def _step(p, m, v, x, target, *, tile_f, tile_b, tk):
    L = p["ln_g"].shape[0]
    xi, yi, ci = _place()
    me_k = 2 * xi + yi
    hi_rows, ho_rows = D_MODEL // 2, GROUP // 2

    cw = jnp.concatenate([p["conv_a_w"], p["conv_b_w"]], axis=1).reshape(-1, 128)
    cw_rows = cw.shape[0]
    cw = _pad_rows(cw, 72)
    wi16 = p["w_in"].astype(BF16)
    wo16 = p["w_out"].astype(BF16)
    wig0, wog0, cwg = _gather_weights(wi16[0:1], wo16[0:1], cw)
    cwg = cwg[:, :cw_rows].reshape(N_CHIPS, L, KA + KB, HEAD)
    conv_full = jnp.transpose(cwg, (1, 2, 0, 3)).reshape(L, KA + KB, GROUP)
    seg, e4 = _indicator_consts()
    consts = [_layer_consts(p, conv_full, l) for l in range(L)]

    hcur = x
    saved, wig, wog = [], [wig0[0]], [wog0[0]]
    for l in range(L):
        k = consts[l]
        nxt = (wi16[l + 1], wo16[l + 1]) if l + 1 < L else None
        outs = _fwd_layer(hcur, wig[l], k["bin"], k["caw"], k["cbw"], k["s256"], seg, k["pw"], k["wm"], k["sb"], wog[l],
                          k["v1024"], tile=tile_f, nxt=nxt, target=None if nxt is not None else target)
        y, xb, h, aux, mixb, z = outs[0:6]
        if nxt is not None:
            wig.append(outs[6])
            wog.append(outs[7])
        saved.append((xb, h, aux, mixb, z))
        hcur = y

    dy = hcur
    loss_local = outs[6][0, 0]

    gwi = lax.empty((L, N_CHIPS, D_MODEL, COLS), F32)
    gwo = lax.empty((L, N_CHIPS, GROUP, D_MODEL), F32)
    gwi16 = lax.empty((L, N_CHIPS, D_MODEL, COLS), BF16)
    gwo16 = lax.empty((L, N_CHIPS, GROUP, D_MODEL), BF16)
    ri = lax.empty((L, N_CHIPS, hi_rows, COLS), BF16)
    ro = lax.empty((L, N_CHIPS, ho_rows, D_MODEL), BF16)
    p_i = lax.empty((L, N_CHIPS, hi_rows, COLS), BF16)
    p_o = lax.empty((L, N_CHIPS, ho_rows, D_MODEL), BF16)
    q_i = lax.empty((3, L, hi_rows, COLS), BF16)
    q_o = lax.empty((3, L, ho_rows, D_MODEL), BF16)
    r_sm = [None] * L
    pending = None
    for l in reversed(range(L)):
        k = consts[l]
        xb, h, aux, mixb, z = saved[l]
        exch = None if pending is None else (pending[0], p_i, p_o, pending[1], q_i, q_o)
        outs = _bwd_layer(dy, z, h, aux, wig[l], k["caw"], k["cbw"], k["s256"], seg, k["pw"], k["wm"], k["wmt"],
                          k["sb"], wog[l], k["v1024"], e4, tile=tile_b, exch=exch)
        dy, dhb, dzb, osm = outs[0:4]
        if l == L - 1:
            osm = osm.at[ROW_LOSS, 0].set(loss_local)
        if exch is not None:
            q_i, q_o, r_sm[l + 1] = outs[4:7]
        larr = jnp.full((1,), l, jnp.int32)
        if l > 0:
            gwi, gwi16 = _dw_in(larr, xb, dhb, gwi, gwi16, tk=tk)
        else:
            gwi, gwi16, r_sm[0] = _dw_in(larr, xb, dhb, gwi, gwi16, tk=tk, small=osm)
        gwo, gwo16 = _dw_out(larr, mixb, dzb, gwo, gwo16, tk=tk)
        ri, ro = _swap_halves(larr, gwi16, gwo16, ri, ro)
        cl_arr = jnp.stack([ci, jnp.int32(l)]).astype(jnp.int32)
        p_i = _add_halves(cl_arr, gwi, ri, p_i, rows=hi_rows, cols=COLS, tr=256, name="add_halves_in")
        p_o = _add_halves(cl_arr, gwo, ro, p_o, rows=ho_rows, cols=D_MODEL, tr=128, name="add_halves_out")
        pending = (larr, osm)
    grad_x = dy
    q_i, q_o = _exchange_last(pending[0], p_i, p_o, q_i, q_o)

    summed = _sum_small(r_sm)
    loss = summed[L - 1, ROW_LOSS, 0]
    grads = _unpack_small(summed)
    for n in ("conv_a_w", "conv_b_w"):
        grads[n] = lax.dynamic_slice_in_dim(grads[n], me_k * HEAD, HEAD, axis=2)

    kc_arr = jnp.stack([me_k, ci]).astype(jnp.int32)
    g_i = _sum_chunks(kc_arr, p_i, q_i, rows=hi_rows, cols=COLS, tr=256, name="sum_chunks_in")
    g_o = _sum_chunks(kc_arr, p_o, q_o, rows=ho_rows, cols=D_MODEL, tr=128, name="sum_chunks_out")
    g_i, g_o = _share_result(g_i, g_o)
    grads["w_in"] = g_i
    grads["w_out"] = g_o

    delta, new_m, new_v = {}, {}, {}
    for n, tr in (("w_in", 512), ("w_out", 256)):
        shp = p[n].shape
        args = [a.reshape(shp[0] * shp[1], shp[2]) for a in (p[n], grads[n], m[n], v[n])]
        outs = _adamw(*args, rows_per_step=tr, name="adamw_" + n, copy_g=True)
        delta[n], new_m[n], new_v[n], grads[n] = (a.reshape(shp) for a in outs)
    small = [n for n in WEIGHTS if n not in ("w_in", "w_out")]
    flat = [[a[n].reshape(-1, a[n].shape[-1]) for n in small] for a in (p, grads, m, v)]
    outs = _adamw_small(*flat)
    for j, n in enumerate(small):
        delta[n], new_m[n], new_v[n] = (o[j].reshape(p[n].shape) for o in outs)

    return (loss, grad_x[None], *[grads[n] for n in WEIGHTS], *[delta[n] for n in WEIGHTS],
            *[new_m[n] for n in WEIGHTS], *[new_v[n] for n in WEIGHTS])


def kernel(x, ln_g, ln_b, w_in, b_in, conv_a_w, conv_a_b, norm_a_g, norm_a_b, conv_b_w, pool_w, pool_scale, sgu_ln_g, sgu_ln_b, sgu_w, sgu_bias, w_out, b_out, loss_target, m_ln_g, m_ln_b, m_w_in, m_b_in, m_conv_a_w, m_conv_a_b, m_norm_a_g, m_norm_a_b, m_conv_b_w, m_pool_w, m_pool_scale, m_sgu_ln_g, m_sgu_ln_b, m_sgu_w, m_sgu_bias, m_w_out, m_b_out, v_ln_g, v_ln_b, v_w_in, v_b_in, v_conv_a_w, v_conv_a_b, v_norm_a_g, v_norm_a_b, v_conv_b_w, v_pool_w, v_pool_scale, v_sgu_ln_g, v_sgu_ln_b, v_sgu_w, v_sgu_bias, v_w_out, v_b_out):
    p = dict(ln_g=ln_g, ln_b=ln_b, w_in=w_in, b_in=b_in, conv_a_w=conv_a_w, conv_a_b=conv_a_b, norm_a_g=norm_a_g,
             norm_a_b=norm_a_b, conv_b_w=conv_b_w, pool_w=pool_w, pool_scale=pool_scale, sgu_ln_g=sgu_ln_g,
             sgu_ln_b=sgu_ln_b, sgu_w=sgu_w, sgu_bias=sgu_bias, w_out=w_out, b_out=b_out)
    m = dict(ln_g=m_ln_g, ln_b=m_ln_b, w_in=m_w_in, b_in=m_b_in, conv_a_w=m_conv_a_w, conv_a_b=m_conv_a_b,
             norm_a_g=m_norm_a_g, norm_a_b=m_norm_a_b, conv_b_w=m_conv_b_w, pool_w=m_pool_w, pool_scale=m_pool_scale,
             sgu_ln_g=m_sgu_ln_g, sgu_ln_b=m_sgu_ln_b, sgu_w=m_sgu_w, sgu_bias=m_sgu_bias, w_out=m_w_out, b_out=m_b_out)
    v = dict(ln_g=v_ln_g, ln_b=v_ln_b, w_in=v_w_in, b_in=v_b_in, conv_a_w=v_conv_a_w, conv_a_b=v_conv_a_b,
             norm_a_g=v_norm_a_g, norm_a_b=v_norm_a_b, conv_b_w=v_conv_b_w, pool_w=v_pool_w, pool_scale=v_pool_scale,
             sgu_ln_g=v_sgu_ln_g, sgu_ln_b=v_sgu_ln_b, sgu_w=v_sgu_w, sgu_bias=v_sgu_bias, w_out=v_w_out, b_out=v_b_out)
    return _step(p, m, v, x[0], loss_target[0], tile_f=256, tile_b=256, tk=2048)
```

```python
import functools

import jax
import jax.numpy as jnp
from jax import lax
from jax.experimental import pallas as pl
from jax.experimental.pallas import tpu as pltpu

F32 = jnp.float32
BF16 = jnp.bfloat16
MESH = pl.DeviceIdType.MESH

D_MODEL = 1024
GROUP = 256
HEAD = 64
N_SLICES = 12
IN_WIDTH = N_SLICES * GROUP
N_CHIPS = 4
COLS = IN_WIDTH // N_CHIPS
KA = 31
KB = 3
HALO_A, HALO_B, HALO_C = 32, 8, 16
POOL_WINDOWS = (2, 4, 8, 16)
SGU_BLOCK = 128
CHUNK = 64
LN_EPS = 1e-5
ROWS = 64
V7X_VMEM_BYTES = 64 * 1024 * 1024
VMEM_LIMIT = 56 * 1024 * 1024

ADAM_LR, ADAM_B1, ADAM_B2, ADAM_EPS, ADAM_WD, ADAM_STEP = 0.001, 0.9, 0.999, 1e-08, 0.01, 10


ANY = pl.BlockSpec(memory_space=pl.ANY)


def _vmem_params(**kw):
    return pltpu.CompilerParams(vmem_limit_bytes=VMEM_LIMIT, **kw)


def _place():
    return lax.axis_index("x"), lax.axis_index("y"), lax.axis_index("c")


def _other_chips(x, y):
    return [(1 - x, y, 2 * (1 - x) + y), (x, 1 - y, 2 * x + (1 - y)), (1 - x, 1 - y, 2 * (1 - x) + (1 - y))]


def _sig(v):
    return 0.5 * jnp.tanh(0.5 * v) + 0.5


def _dot(a, b):
    return jnp.dot(a, b, preferred_element_type=F32)


def _dot_nt(a, b):
    return lax.dot_general(a, b, (((1,), (1,)), ((), ())), preferred_element_type=F32)


def _dot_tn(a, b):
    return lax.dot_general(a, b, (((0,), (0,)), ((), ())), preferred_element_type=F32)


def _segdot(v, m):
    hi = v.astype(BF16)
    lo = (v - hi.astype(F32)).astype(BF16)
    return _dot(hi, m) + _dot(lo, m)


def _colsum(v):
    return jnp.sum(v, axis=0, keepdims=True)


def _rowmean(v):
    return jnp.mean(v, axis=-1, keepdims=True)


def _lane_group(n):
    return lax.broadcasted_iota(jnp.int32, (1, n), 1) // HEAD


def _pool_cnt(tile, t_rows):
    pos = tile * t_rows + lax.broadcasted_iota(jnp.int32, (t_rows, GROUP), 0) + 1
    grp = lax.broadcasted_iota(jnp.int32, (t_rows, GROUP), 1) // HEAD
    win = jnp.where(grp == 0, 2, jnp.where(grp == 1, 4, jnp.where(grp == 2, 8, 16)))
    return jnp.minimum(pos, win).astype(F32)


def _sgu_masks(wm_ref, wmt_ref, wm_s, wmt_s):
    r = lax.broadcasted_iota(jnp.int32, (SGU_BLOCK, 4 * SGU_BLOCK), 0) // CHUNK
    c = (lax.broadcasted_iota(jnp.int32, (SGU_BLOCK, 4 * SGU_BLOCK), 1) % SGU_BLOCK) // CHUNK
    wm_s[...] = jnp.where(c <= r, wm_ref[...], 0.0).astype(BF16)
    if wmt_ref is not None:
        rt = (lax.broadcasted_iota(jnp.int32, (4 * SGU_BLOCK, SGU_BLOCK), 0) % SGU_BLOCK) // CHUNK
        ct = lax.broadcasted_iota(jnp.int32, (4 * SGU_BLOCK, SGU_BLOCK), 1) // CHUNK
        wmt_s[...] = jnp.where(rt <= ct, wmt_ref[...], 0.0).astype(BF16)


def _vstack(v_blk):
    grp = _lane_group(GROUP)
    return jnp.concatenate([jnp.where(grp == h, v_blk, 0.0) for h in range(4)], axis=0).astype(BF16)


def _gather_next(step, nt, nwi, nwo, gwi, gwo, send_sems, recv_sems, loc_sems):
    x, y, c = _place()
    me_k = 2 * x + y
    sibling = (x, y, 1 - c)
    chips = _other_chips(x, y)
    hi, ho = D_MODEL // 2, GROUP // 2

    def rc(src, dst, sem, to):
        return pltpu.make_async_remote_copy(src_ref=src, dst_ref=dst, send_sem=send_sems.at[sem],
                                            recv_sem=recv_sems.at[sem], device_id=to, device_id_type=MESH)

    def blk(ref, k, n, cc):
        return ref.at[k, pl.ds(cc * n, n), :]

    def ici(r):
        px, py, _ = chips[r]
        to = (px, py, c)
        return [rc(nwi.at[pl.ds(c * hi, hi), :], blk(gwi, me_k, hi, c), 2 * r, to),
                rc(nwo.at[pl.ds(c * ho, ho), :], blk(gwo, me_k, ho, c), 2 * r + 1, to)]

    def landed(r, cc, base):
        pk = chips[r][2]
        return [rc(blk(gwi, pk, hi, cc), blk(gwi, pk, hi, cc), base + 2 * r, sibling),
                rc(blk(gwo, pk, ho, cc), blk(gwo, pk, ho, cc), base + 2 * r + 1, sibling)]

    def local():
        return [pltpu.make_async_copy(nwi, gwi.at[me_k], loc_sems.at[0]),
                pltpu.make_async_copy(nwo, gwo.at[me_k], loc_sems.at[1])]

    @pl.when(step == 0)
    def _():
        for cp in local():
            cp.start()
        for r in range(3):
            for cp in ici(r):
                cp.start()

    @pl.when(step == (3 * nt) // 4)
    def _():
        for r in range(3):
            for got, fwd in zip(landed(r, c, 0), landed(r, c, 6)):
                got.wait_recv()
                fwd.start()

    @pl.when(step == nt - 1)
    def _():
        for r in range(3):
            for got in landed(r, 1 - c, 6):
                got.wait_recv()
        for r in range(3):
            for cp in ici(r) + landed(r, c, 6):
                cp.wait_send()
        for cp in local():
            cp.wait()


def _fwd_layer(x, wi, bin_, caw, cbw, s256, seg, pw, wm, sb, wo, v1024, *, tile, nxt=None, target=None):
    assert nxt is None or target is None
    S = x.shape[0]
    T = tile
    nt = S // T
    alpha = float((2.0 * 4) ** 0.25)
    n_in = 12 + (2 if nxt is not None else 0) + (1 if target is not None else 0)
    n_out = 6 + (2 if nxt is not None else 0) + (1 if target is not None else 0)

    def body(*refs):
        (x_ref, wi_ref, bin_ref, caw_ref, cbw_ref, s256_ref, seg_ref, pw_ref, wm_ref, sb_ref, wo_ref,
         v1024_ref) = refs[0:12]
        y_ref, xb_ref, h_ref, aux_ref, mix_ref, z_ref = refs[n_in:n_in + 6]
        abuf, bbuf, cbuf, wm_s, shf = refs[n_in + n_out:n_in + n_out + 5]
        i = pl.program_id(0)
        if nxt is not None:
            _gather_next(i, nt, refs[12], refs[13], refs[n_in + 6], refs[n_in + 7], *refs[n_in + n_out + 5:])

        @pl.when(i == 0)
        def _():
            abuf[0:HALO_A, :] = jnp.zeros((HALO_A, GROUP), F32)
            bbuf[0:HALO_B, :] = jnp.zeros((HALO_B, GROUP), F32)
            cbuf[0:HALO_C, :] = jnp.zeros((HALO_C, GROUP), F32)
            _sgu_masks(wm_ref, None, wm_s, None)

        x = x_ref[...]
        xb = x.astype(BF16)
        xb_ref[...] = xb
        for k in range(N_CHIPS):
            h_ref[:, COLS * k:COLS * (k + 1)] = _dot(xb, wi_ref[k]) + bin_ref[:, COLS * k:COLS * (k + 1)]

        def hs(j):
            return h_ref[:, GROUP * j:GROUP * (j + 1)]

        abuf[HALO_A:HALO_A + T, :] = hs(0) * _sig(hs(1))
        span = T + HALO_A - 8
        for p in range(1, 8):
            shf[p - 1, :, :] = abuf[p:p + span, :]
        for r0 in range(0, T, ROWS):
            acc = None
            for k in range(KA):
                off = HALO_A - (KA - 1) + k
                p, q8 = off % 8, off - off % 8
                win = abuf[r0 + q8:r0 + q8 + ROWS, :] if p == 0 else shf[p - 1, r0 + q8:r0 + q8 + ROWS, :]
                term = caw_ref[k:k + 1, :] * win
                acc = term if acc is None else acc + term
            aux_ref[r0:r0 + ROWS, 0:GROUP] = acc + s256_ref[0:1, :]
        abuf[0:HALO_A, :] = abuf[T:T + HALO_A, :]
        a1 = aux_ref[:, 0:GROUP]
        segm = seg_ref[...]
        cen = a1 - _segdot(a1, segm)
        var = _segdot(cen * cen, segm)
        a2 = cen * lax.rsqrt(var + LN_EPS) * s256_ref[1:2, :] + s256_ref[2:3, :]
        az = hs(2)
        mix_ref[:, 0:GROUP] = (a2 * _sig(a2) * (az * _sig(az))).astype(BF16)

        bbuf[HALO_B:HALO_B + T, :] = hs(4) * hs(5)
        for r0 in range(0, T, ROWS):
            acc = None
            for k in range(KB):
                off = HALO_B - (KB - 1) + k + r0
                term = cbw_ref[k:k + 1, :] * bbuf[off:off + ROWS, :]
                acc = term if acc is None else acc + term
            aux_ref[r0:r0 + ROWS, GROUP:2 * GROUP] = acc
        bbuf[0:HALO_B, :] = bbuf[T:T + HALO_B, :]
        bz = hs(6)
        mix_ref[:, GROUP:2 * GROUP] = (hs(3) * aux_ref[:, GROUP:2 * GROUP] * (bz * _sig(bz))).astype(BF16)

        ch = hs(7)
        cbuf[HALO_C:HALO_C + T, :] = ch
        hi_lane = (lax.broadcasted_iota(jnp.int32, (1, 128), 1) // HEAD) == 1
        for r0 in range(0, T, ROWS):
            def win(col, j0, j1):
                s = None
                for j in range(j0, j1):
                    off = HALO_C - j + r0
                    term = cbuf[off:off + ROWS, 128 * col:128 * (col + 1)]
                    s = term if s is None else s + term
                return s
            w0 = win(0, 0, 2) + jnp.where(hi_lane, win(0, 2, 4), 0.0)
            w1 = win(1, 0, 8) + jnp.where(hi_lane, win(1, 8, 16), 0.0)
            aux_ref[r0:r0 + ROWS, 2 * GROUP:2 * GROUP + 128] = w0
            aux_ref[r0:r0 + ROWS, 2 * GROUP + 128:3 * GROUP] = w1
        cbuf[0:HALO_C, :] = cbuf[T:T + HALO_C, :]
        pooled = aux_ref[:, 2 * GROUP:3 * GROUP] / _pool_cnt(i, T) - ch
        aux_ref[:, 2 * GROUP:3 * GROUP] = pooled
        q = _dot(pooled.astype(BF16), pw_ref[...])
        cz = hs(8)
        mix_ref[:, 2 * GROUP:3 * GROUP] = (q * s256_ref[3:4, :] * (cz * _sig(cz))).astype(BF16)

        dv = hs(10)
        cen = dv - _rowmean(dv)
        var = _rowmean(cen * cen)
        v = cen * lax.rsqrt(var + LN_EPS) * s256_ref[4:5, :] + s256_ref[5:6, :]
        sps = []
        for n in range(T // SGU_BLOCK):
            vb = v[n * SGU_BLOCK:(n + 1) * SGU_BLOCK, :]
            sps.append(_dot(wm_s[...], _vstack(vb)) + sb_ref[...])
        sp = jnp.concatenate(sps, axis=0)
        dz = hs(11)
        mix_ref[:, 3 * GROUP:4 * GROUP] = (hs(9) * sp * (dz * _sig(dz))).astype(BF16)

        out = v1024_ref[0:1, :]
        for k in range(N_CHIPS):
            out = out + _dot(mix_ref[:, GROUP * k:GROUP * (k + 1)], wo_ref[k])
        z = alpha * x + out
        z_ref[...] = z
        cen = z - _rowmean(z)
        var = _rowmean(cen * cen)
        y = cen * lax.rsqrt(var + LN_EPS) * v1024_ref[1:2, :] + v1024_ref[2:3, :]
        if target is None:
            y_ref[...] = y
        else:
            t_ref, loss_ref = refs[12], refs[n_in + 6]

            @pl.when(i == 0)
            def _():
                loss_ref[...] = jnp.zeros_like(loss_ref)
            err = y - t_ref[...]
            y_ref[...] = err * (1.0 / D_MODEL)
            loss_ref[...] += jnp.sum(_colsum(err * err), axis=1, keepdims=True) * (0.5 / D_MODEL)

    def full(a):
        nd = a.ndim
        return pl.BlockSpec(a.shape, lambda i, _n=nd: (0,) * _n)

    def rows(width):
        return pl.BlockSpec((T, width), lambda i: (i, 0))

    consts = (wi, bin_, caw, cbw, s256, seg, pw, wm, sb, wo, v1024)
    in_specs = [rows(D_MODEL)] + [full(a) for a in consts]
    out_specs = [rows(D_MODEL), rows(D_MODEL), rows(IN_WIDTH), rows(3 * GROUP), rows(D_MODEL), rows(D_MODEL)]
    out_shape = [jax.ShapeDtypeStruct((S, D_MODEL), F32), jax.ShapeDtypeStruct((S, D_MODEL), BF16),
                 jax.ShapeDtypeStruct((S, IN_WIDTH), F32), jax.ShapeDtypeStruct((S, 3 * GROUP), F32),
                 jax.ShapeDtypeStruct((S, D_MODEL), BF16), jax.ShapeDtypeStruct((S, D_MODEL), F32)]
    scratch = [pltpu.VMEM((T + HALO_A, GROUP), F32), pltpu.VMEM((T + HALO_B, GROUP), F32),
               pltpu.VMEM((T + HALO_C, GROUP), F32), pltpu.VMEM((SGU_BLOCK, 4 * SGU_BLOCK), BF16),
               pltpu.VMEM((7, T + HALO_A - 8, GROUP), F32)]
    extra = ()
    if nxt is not None:
        extra = tuple(nxt)
        in_specs += [ANY, ANY]
        out_specs += [ANY, ANY]
        out_shape += [jax.ShapeDtypeStruct((N_CHIPS, D_MODEL, COLS), BF16),
                      jax.ShapeDtypeStruct((N_CHIPS, GROUP, D_MODEL), BF16)]
        scratch += [pltpu.SemaphoreType.DMA((12,)), pltpu.SemaphoreType.DMA((12,)), pltpu.SemaphoreType.DMA((2,))]
    if target is not None:
        extra = (target,)
        in_specs += [rows(D_MODEL)]
        out_specs += [pl.BlockSpec((8, 128), lambda i: (0, 0))]
        out_shape += [jax.ShapeDtypeStruct((8, 128), F32)]
    return pl.pallas_call(
        body, name=("fwd_layer_loss" if target is not None else "fwd_layer") if nxt is None else "fwd_layer_gather",
        grid=(nt,), in_specs=in_specs, out_specs=out_specs, out_shape=out_shape, scratch_shapes=scratch,
        compiler_params=_vmem_params(dimension_semantics=("arbitrary",), has_side_effects=nxt is not None),
    )(x, *consts, *extra)


ROW_CBW = 8
ROW_CAW = 16
ROW_LOSS = 7
ROW_PW = 48
ROW_LNG = 112
ROW_LNB = 116
ROW_BOUT = 120
ROW_BIN = 124
ROW_WC = 136
ROW_SB = 392
SM_ROWS = 400
N_DEV = 8


def _exchange_comm(start, finish, l, p_i, p_o, sm, r_i, r_o, r_sm, send_sems, recv_sems, loc_sem):
    x, y, c = _place()
    me = 4 * x + 2 * y + c
    chips = _other_chips(x, y)

    def rc(src, dst, sem, to):
        return pltpu.make_async_remote_copy(src_ref=src, dst_ref=dst, send_sem=send_sems.at[sem],
                                            recv_sem=recv_sems.at[sem], device_id=to, device_id_type=MESH)

    def big(r):
        px, py, pk = chips[r]
        to = (px, py, c)
        return [rc(p_i.at[l, pk], r_i.at[r, l], 2 * r, to), rc(p_o.at[l, pk], r_o.at[r, l], 2 * r + 1, to)]

    def peer(rel):
        px = 1 - x if rel & 4 else x
        py = 1 - y if rel & 2 else y
        pc = 1 - c if rel & 1 else c
        return (px, py, pc), 4 * px + 2 * py + pc

    def small_out(rel):
        to, _ = peer(rel)
        return rc(sm, r_sm.at[me], 5 + rel, to)

    def small_in(rel):
        to, idx = peer(rel)
        return rc(sm, r_sm.at[idx], 5 + rel, to)

    def local():
        return pltpu.make_async_copy(sm, r_sm.at[me], loc_sem.at[0])

    with_big, with_small = p_i is not None, sm is not None

    @pl.when(start)
    def _():
        if with_small:
            local().start()
        if with_big:
            for r in range(3):
                for cp in big(r):
                    cp.start()
        if with_small:
            for rel in range(1, N_DEV):
                small_out(rel).start()

    @pl.when(finish)
    def _():
        if with_big:
            for r in range(3):
                for cp in big(r):
                    cp.wait()
        if with_small:
            for rel in range(1, N_DEV):
                small_in(rel).wait_recv()
                small_out(rel).wait_send()
            local().wait()


RC = 32
RC_WIDE = 16
ACC_ROWS = 136


def _rsum8(v):
    r = v[0:8]
    for j in range(1, v.shape[0] // 8):
        r = r + v[8 * j:8 * j + 8]
    return r


def _bwd_layer(dy, z, h, aux, wi, caw, cbw, s256, seg, pw, wm, wmt, sb, wo, v1024, e4, *, tile, exch=None):
    S = dy.shape[0]
    T = tile
    nt = S // T
    nblk = T // SGU_BLOCK
    alpha = float((2.0 * 4) ** 0.25)
    n_in = 16 + (6 if exch is not None else 0)
    n_out = 4 + (3 if exch is not None else 0)
    slab = pltpu.VMEM((T, GROUP), F32)
    scratch = dict(
        dbuf=pltpu.VMEM((T + HALO_A, GROUP), F32), ebuf=pltpu.VMEM((T + HALO_B, GROUP), F32),
        fbuf=pltpu.VMEM((T + HALO_C, GROUP), F32), sh=pltpu.VMEM((7, T + HALO_A - 8, GROUP), F32),
        wm_s=pltpu.VMEM((SGU_BLOCK, 4 * SGU_BLOCK), BF16), wmt_s=pltpu.VMEM((4 * SGU_BLOCK, SGU_BLOCK), BF16),
        dsp_acc=pltpu.VMEM((SGU_BLOCK, GROUP), F32), pw_acc=pltpu.VMEM((GROUP, GROUP), F32),
        acc_s=pltpu.VMEM((8 * ACC_ROWS, GROUP), F32), acc_w=pltpu.VMEM((24, D_MODEL), F32),
        dmix_s=pltpu.VMEM((T, D_MODEL), F32), vst_s=pltpu.VMEM((nblk, 4 * SGU_BLOCK, GROUP), BF16),
        dq_s=pltpu.VMEM((T, GROUP), BF16), dxt_s=pltpu.VMEM((D_MODEL, T), F32),
        mean_s=slab, t1_s=slab, t2_s=slab, q_s=slab, xv_s=slab, rv_s=slab, v_s=slab, sp_s=slab, a0_s=slab, sg_s=slab,
        xh_s=slab, ra_s=slab, ub_s=slab, dsp_s=slab, m1_s=slab, m2_s=slab, dpool_s=slab, dvd_s=slab, u_s=slab,
        du_s=slab, cw_s=slab)
    names = list(scratch)

    def body(*refs):
        (dy_ref, z_ref, h_ref, aux_ref, wi_ref, caw_ref, cbw_ref, s256_ref, seg_ref, pw_ref, wm_ref, wmt_ref,
         sb_ref, wo_ref, v1024_ref, e4_ref) = refs[0:16]
        dx_ref, dhb_ref, dzb_ref, osm_ref = refs[n_in:n_in + 4]
        k0 = n_in + n_out
        sc = dict(zip(names, refs[k0:k0 + len(names)]))
        dbuf, ebuf, fbuf, sh = sc["dbuf"], sc["ebuf"], sc["fbuf"], sc["sh"]
        wm_s, wmt_s, dsp_acc, pw_acc, acc_s, acc_w = (sc[n] for n in ("wm_s", "wmt_s", "dsp_acc", "pw_acc", "acc_s",
                                                                        "acc_w"))
        dmix_s, vst_s, dq_s = sc["dmix_s"], sc["vst_s"], sc["dq_s"]
        i = pl.program_id(0)
        tile_idx = nt - 1 - i
        if exch is not None:
            l_ref, p_i, p_o, sm = refs[16:20]
            r_i, r_o, r_sm = refs[n_in + 4:n_in + 7]
            _exchange_comm(i == 0, i == nt - 1, l_ref[0], p_i, p_o, sm, r_i, r_o, r_sm, *refs[k0 + len(names):])

        @pl.when(i == 0)
        def _():
            dbuf[T:T + HALO_A, :] = jnp.zeros((HALO_A, GROUP), F32)
            ebuf[T:T + HALO_B, :] = jnp.zeros((HALO_B, GROUP), F32)
            fbuf[T:T + HALO_C, :] = jnp.zeros((HALO_C, GROUP), F32)
            _sgu_masks(wm_ref, wmt_ref, wm_s, wmt_s)
            osm_ref[...] = jnp.zeros_like(osm_ref)
            dsp_acc[...] = jnp.zeros_like(dsp_acc)
            pw_acc[...] = jnp.zeros_like(pw_acc)
            acc_s[...] = jnp.zeros_like(acc_s)
            acc_w[...] = jnp.zeros_like(acc_w)

        def chunks(rc, fn):
            for c in range(T // rc):
                fn(pl.ds(c * rc, rc))

        def hs(j, rows):
            return h_ref[rows, GROUP * j:GROUP * (j + 1)]

        def acc_add(row, val):
            acc_s[8 * row:8 * row + 8, :] += _rsum8(val)

        def put_dh(j, rows, val):
            acc_add(ROW_BIN + j, val)
            dhb_ref[rows, GROUP * j:GROUP * (j + 1)] = val.astype(BF16)

        def dsilu(v, s):
            return s * (1.0 + v * (1.0 - s))

        def vec(r):
            return s256_ref[r:r + 1, :]

        def ln_bwd(rows):
            dyc = dy_ref[rows, :]
            zc = z_ref[rows, :]
            cen = zc - _rowmean(zc)
            rstd = lax.rsqrt(_rowmean(cen * cen) + LN_EPS)
            xhat = cen * rstd
            acc_w[0:8, :] += _rsum8(dyc * xhat)
            acc_w[8:16, :] += _rsum8(dyc)
            gdy = dyc * v1024_ref[1:2, :]
            dz = rstd * (gdy - _rowmean(gdy) - xhat * _rowmean(gdy * xhat))
            acc_w[16:24, :] += _rsum8(dz)
            dzb_ref[rows, :] = dz.astype(BF16)
            dx_ref[rows, :] = alpha * dz
        chunks(RC_WIDE, ln_bwd)

        segm = seg_ref[...]
        dzb = dzb_ref[...]
        for k in range(N_CHIPS):
            dmix_s[:, GROUP * k:GROUP * (k + 1)] = _dot_nt(dzb, wo_ref[k])
        sc["mean_s"][...] = _segdot(aux_ref[:, 0:GROUP], segm)
        pooled_b = aux_ref[:, 2 * GROUP:3 * GROUP].astype(BF16)
        sc["q_s"][...] = _dot(pooled_b, pw_ref[...])

        def centre(rows):
            cen = aux_ref[rows, 0:GROUP] - sc["mean_s"][rows, :]
            sc["t1_s"][rows, :] = cen * cen
            dv_in = hs(10, rows)
            cen_v = dv_in - _rowmean(dv_in)
            rstd_v = lax.rsqrt(_rowmean(cen_v * cen_v) + LN_EPS)
            xv = cen_v * rstd_v
            sc["xv_s"][rows, :] = xv
            sc["rv_s"][rows, :] = jnp.broadcast_to(rstd_v, xv.shape)
            sc["v_s"][rows, :] = xv * vec(4) + vec(5)
        chunks(RC, centre)

        sc["t2_s"][...] = _segdot(sc["t1_s"][...], segm)
        for n in range(nblk):
            blk = slice(n * SGU_BLOCK, (n + 1) * SGU_BLOCK)
            vst_s[n] = _vstack(sc["v_s"][blk, :])
            sc["sp_s"][blk, :] = _dot(wm_s[...], vst_s[n]) + sb_ref[...]

        def mixers(rows):
            a_val, a_glu, a_z = hs(0, rows), hs(1, rows), hs(2, rows)
            sg = _sig(a_glu)
            sc["a0_s"][rows, :] = a_val * sg
            sc["sg_s"][rows, :] = sg
            rstd_a = lax.rsqrt(sc["t2_s"][rows, :] + LN_EPS)
            xh = (aux_ref[rows, 0:GROUP] - sc["mean_s"][rows, :]) * rstd_a
            a2 = xh * vec(1) + vec(2)
            s2 = _sig(a2)
            sz = _sig(a_z)
            dya = dmix_s[rows, 0:GROUP]
            put_dh(2, rows, dya * (a2 * s2) * dsilu(a_z, sz))
            d_a2 = dya * (a_z * sz) * dsilu(a2, s2)
            acc_add(1, d_a2 * xh)
            acc_add(2, d_a2)
            gd = d_a2 * vec(1)
            sc["t1_s"][rows, :] = gd
            sc["t2_s"][rows, :] = gd * xh
            sc["xh_s"][rows, :] = xh
            sc["ra_s"][rows, :] = rstd_a
            b_b, b_c, b_h, b_z = hs(3, rows), hs(4, rows), hs(5, rows), hs(6, rows)
            cb = aux_ref[rows, GROUP:2 * GROUP]
            sz = _sig(b_z)
            dyb = dmix_s[rows, GROUP:2 * GROUP]
            put_dh(3, rows, dyb * cb * (b_z * sz))
            put_dh(6, rows, dyb * b_b * cb * dsilu(b_z, sz))
            ebuf[rows, :] = dyb * b_b * (b_z * sz)
            sc["ub_s"][rows, :] = b_c * b_h
            c_z = hs(8, rows)
            q = sc["q_s"][rows, :]
            sz = _sig(c_z)
            dyc = dmix_s[rows, 2 * GROUP:3 * GROUP]
            acc_add(3, dyc * q * (c_z * sz))
            put_dh(8, rows, dyc * q * vec(3) * dsilu(c_z, sz))
            dq_s[rows, :] = (dyc * vec(3) * (c_z * sz)).astype(BF16)
            d_u, d_z = hs(9, rows), hs(11, rows)
            sp = sc["sp_s"][rows, :]
            sz = _sig(d_z)
            dyd = dmix_s[rows, 3 * GROUP:4 * GROUP]
            put_dh(9, rows, dyd * sp * (d_z * sz))
            put_dh(11, rows, dyd * d_u * sp * dsilu(d_z, sz))
            sc["dsp_s"][rows, :] = dyd * d_u * (d_z * sz)
        chunks(RC, mixers)

        sc["m1_s"][...] = _segdot(sc["t1_s"][...], segm)
        sc["m2_s"][...] = _segdot(sc["t2_s"][...], segm)
        d_q = dq_s[...]
        pw_acc[...] += _dot_tn(pooled_b, d_q)
        sc["dpool_s"][...] = _dot_nt(d_q, pw_ref[...])
        grp = _lane_group(GROUP)
        for n in range(nblk):
            blk = slice(n * SGU_BLOCK, (n + 1) * SGU_BLOCK)
            dspb = sc["dsp_s"][blk, :]
            dsp_acc[...] += dspb
            dspb16 = dspb.astype(BF16)
            dvst = _dot(wmt_s[...], dspb16)
            dvb = None
            for hh in range(4):
                part = jnp.where(grp == hh, dvst[hh * SGU_BLOCK:(hh + 1) * SGU_BLOCK, :], 0.0)
                dvb = part if dvb is None else dvb + part
            sc["dvd_s"][blk, :] = dvb
            dwc = _dot_nt(dspb16, vst_s[n])
            osm_ref[ROW_WC:ROW_WC + SGU_BLOCK, :] += dwc[:, 0:GROUP]
            osm_ref[ROW_WC + SGU_BLOCK:ROW_WC + 2 * SGU_BLOCK, :] += dwc[:, GROUP:2 * GROUP]

        def ln_sums(rows):
            xh = sc["xh_s"][rows, :]
            d_a1 = sc["ra_s"][rows, :] * (sc["t1_s"][rows, :] - sc["m1_s"][rows, :] - xh * sc["m2_s"][rows, :])
            acc_add(0, d_a1)
            dbuf[rows, :] = d_a1
            pos = tile_idx * T + rows.start + lax.broadcasted_iota(jnp.int32, (RC, GROUP), 0) + 1
            lane = lax.broadcasted_iota(jnp.int32, (RC, GROUP), 1) // HEAD
            win = jnp.where(lane == 0, 2, jnp.where(lane == 1, 4, jnp.where(lane == 2, 8, 16)))
            fbuf[rows, :] = sc["dpool_s"][rows, :] / jnp.minimum(pos, win).astype(F32)
            d_v = sc["dvd_s"][rows, :]
            xv = sc["xv_s"][rows, :]
            acc_add(4, d_v * xv)
            acc_add(5, d_v)
            gd = d_v * vec(4)
            put_dh(10, rows, sc["rv_s"][rows, :] * (gd - _rowmean(gd) - xv * _rowmean(gd * xv)))
        chunks(RC, ln_sums)

        span = T + HALO_A - 8
        for p in range(1, 8):
            sh[p - 1, :, :] = dbuf[p:p + span, :]

        for r0 in range(0, T, ROWS):
            uc = sc["ub_s"][r0:r0 + ROWS, :]
            acc = None
            for k in range(KB):
                off = (KB - 1) - k + r0
                w = ebuf[off:off + ROWS, :]
                term = cbw_ref[k:k + 1, :] * w
                acc = term if acc is None else acc + term
                acc_add(ROW_CBW + k, uc * w)
            sc["du_s"][r0:r0 + ROWS, :] = acc
        ebuf[T:T + HALO_B, :] = ebuf[0:HALO_B, :]

        hi_lane = (lax.broadcasted_iota(jnp.int32, (1, 128), 1) // HEAD) == 1
        for r0 in range(0, T, ROWS):
            def win(col, j0, j1):
                s = None
                for j in range(j0, j1):
                    term = fbuf[r0 + j:r0 + j + ROWS, 128 * col:128 * (col + 1)]
                    s = term if s is None else s + term
                return s
            sc["cw_s"][r0:r0 + ROWS, 0:128] = win(0, 0, 2) + jnp.where(hi_lane, win(0, 2, 4), 0.0)
            sc["cw_s"][r0:r0 + ROWS, 128:256] = win(1, 0, 8) + jnp.where(hi_lane, win(1, 8, 16), 0.0)
        fbuf[T:T + HALO_C, :] = fbuf[0:HALO_C, :]

        def rest_bc(rows):
            d_u = sc["du_s"][rows, :]
            put_dh(4, rows, d_u * hs(5, rows))
            put_dh(5, rows, d_u * hs(4, rows))
            put_dh(7, rows, sc["cw_s"][rows, :] - sc["dpool_s"][rows, :])
        chunks(RC, rest_bc)

        dxt_s = sc["dxt_s"]

        def dx_term(k):
            term = _dot_nt(wi_ref[k], dhb_ref[:, COLS * k:COLS * (k + 1)])
            if k == 1:
                dxt_s[...] = term
            else:
                dxt_s[...] += term

        def conv_a(rows):
            a0c = sc["a0_s"][rows, :]
            acc = None
            for k in range(KA):
                off = (KA - 1) - k
                p, q8 = off % 8, off - off % 8
                w = dbuf[pl.ds(rows.start + q8, RC), :] if p == 0 else sh[p - 1, pl.ds(rows.start + q8, RC), :]
                term = caw_ref[k:k + 1, :] * w
                acc = term if acc is None else acc + term
                acc_add(ROW_CAW + k, a0c * w)
            sc["u_s"][rows, :] = acc
        n_chunks = T // RC
        after = {(n_chunks * j) // 3: j + 1 for j in range(3)}
        for c in range(n_chunks):
            conv_a(pl.ds(c * RC, RC))
            if c in after:
                dx_term(after[c])
        dbuf[T:T + HALO_A, :] = dbuf[0:HALO_A, :]

        def rest_a(rows):
            d_a0 = sc["u_s"][rows, :]
            sg = sc["sg_s"][rows, :]
            put_dh(0, rows, d_a0 * sg)
            put_dh(1, rows, d_a0 * hs(0, rows) * sg * (1.0 - sg))
        chunks(RC, rest_a)
        dx_term(0)
        dx_ref[...] += dxt_s[...].T

        @pl.when(i == nt - 1)
        def _():
            for row in list(range(6)) + list(range(ROW_CBW, ROW_CBW + KB)) + list(range(ROW_CAW, ROW_CAW + KA)) + list(
                    range(ROW_BIN, ROW_BIN + N_SLICES)):
                osm_ref[row:row + 1, :] = _colsum(acc_s[8 * row:8 * row + 8, :])
            for j, row in enumerate((ROW_LNG, ROW_LNB, ROW_BOUT)):
                cs = _colsum(acc_w[8 * j:8 * j + 8, :])
                for q in range(D_MODEL // GROUP):
                    osm_ref[row + q:row + q + 1, :] = cs[:, GROUP * q:GROUP * (q + 1)]
            r = lax.broadcasted_iota(jnp.int32, (SGU_BLOCK, GROUP), 0) // CHUNK
            c = (lax.broadcasted_iota(jnp.int32, (SGU_BLOCK, GROUP), 1) % SGU_BLOCK) // CHUNK
            for half in range(2):
                rows_ = slice(ROW_WC + half * SGU_BLOCK, ROW_WC + (half + 1) * SGU_BLOCK)
                osm_ref[rows_, :] = jnp.where(c <= r, osm_ref[rows_, :], 0.0)
            sb_t = _segdot(dsp_acc[...], e4_ref[...]).T
            osm_ref[ROW_SB:ROW_SB + 8, 0:SGU_BLOCK] = sb_t[0:8, :]
            for g in range(4):
                osm_ref[ROW_PW:ROW_PW + HEAD, HEAD * g:HEAD * (g + 1)] = (
                    pw_acc[HEAD * g:HEAD * (g + 1), HEAD * g:HEAD * (g + 1)])

    def full(a):
        nd = a.ndim
        return pl.BlockSpec(a.shape, lambda i, _n=nd: (0,) * _n)

    def rows(width):
        return pl.BlockSpec((T, width), lambda i: (nt - 1 - i, 0))

    def acc(shape):
        return pl.BlockSpec(shape, lambda i: (0, 0))

    consts = (wi, caw, cbw, s256, seg, pw, wm, wmt, sb, wo, v1024, e4)
    in_specs = [rows(D_MODEL), rows(D_MODEL), rows(IN_WIDTH), rows(3 * GROUP)] + [full(a) for a in consts]
    out_specs = [rows(D_MODEL), rows(IN_WIDTH), rows(D_MODEL), acc((SM_ROWS, GROUP))]
    out_shape = [jax.ShapeDtypeStruct((S, D_MODEL), F32), jax.ShapeDtypeStruct((S, IN_WIDTH), BF16),
                 jax.ShapeDtypeStruct((S, D_MODEL), BF16), jax.ShapeDtypeStruct((SM_ROWS, GROUP), F32)]
    scratch_shapes = list(scratch.values())
    extra, aliases = (), {}
    if exch is not None:
        extra = tuple(exch)
        r_i, r_o = exch[4], exch[5]
        in_specs += [pl.BlockSpec(memory_space=pltpu.SMEM)] + [ANY] * 5
        out_specs += [ANY] * 3
        out_shape += [jax.ShapeDtypeStruct(r_i.shape, r_i.dtype), jax.ShapeDtypeStruct(r_o.shape, r_o.dtype),
                      jax.ShapeDtypeStruct((N_DEV, SM_ROWS, GROUP), F32)]
        scratch_shapes += [pltpu.SemaphoreType.DMA((13,)), pltpu.SemaphoreType.DMA((13,)),
                           pltpu.SemaphoreType.DMA((1,))]
        aliases = {20: 4, 21: 5}
    return pl.pallas_call(
        body, name="bwd_layer" if exch is None else "bwd_layer_exchange",
        grid=(nt,), in_specs=in_specs, out_specs=out_specs, out_shape=out_shape, scratch_shapes=scratch_shapes,
        input_output_aliases=aliases,
        compiler_params=_vmem_params(dimension_semantics=("arbitrary",), has_side_effects=exch is not None),
    )(dy, z, h, aux, *consts, *extra)


def _bwd_layer_slabwise(dy, z, h, aux, wi, caw, cbw, s256, seg, pw, wm, wmt, sb, wo, v1024, e4, *, tile, exch=None):
    S = dy.shape[0]
    T = tile
    nt = S // T
    alpha = float((2.0 * 4) ** 0.25)
    n_in = 16 + (6 if exch is not None else 0)
    n_out = 4 + (3 if exch is not None else 0)

    def body(*refs):
        (dy_ref, z_ref, h_ref, aux_ref, wi_ref, caw_ref, cbw_ref, s256_ref, seg_ref, pw_ref, wm_ref, wmt_ref,
         sb_ref, wo_ref, v1024_ref, e4_ref) = refs[0:16]
        dx_ref, dhb_ref, dzb_ref, osm_ref = refs[n_in:n_in + 4]
        dbuf, ebuf, fbuf, a0_s, u_s, wm_s, wmt_s, dsp_acc, pw_acc = refs[n_in + n_out:n_in + n_out + 9]
        i = pl.program_id(0)
        tile_idx = nt - 1 - i
        if exch is not None:
            l_ref, p_i, p_o, sm = refs[16:20]
            r_i, r_o, r_sm = refs[n_in + 4:n_in + 7]
            _exchange_comm(i == 0, i == nt - 1, l_ref[0], p_i, p_o, sm, r_i, r_o, r_sm, *refs[n_in + n_out + 9:])

        @pl.when(i == 0)
        def _():
            dbuf[T:T + HALO_A, :] = jnp.zeros((HALO_A, GROUP), F32)
            ebuf[T:T + HALO_B, :] = jnp.zeros((HALO_B, GROUP), F32)
            fbuf[T:T + HALO_C, :] = jnp.zeros((HALO_C, GROUP), F32)
            _sgu_masks(wm_ref, wmt_ref, wm_s, wmt_s)
            osm_ref[...] = jnp.zeros_like(osm_ref)
            dsp_acc[...] = jnp.zeros_like(dsp_acc)
            pw_acc[...] = jnp.zeros_like(pw_acc)

        def hs(j):
            return h_ref[:, GROUP * j:GROUP * (j + 1)]

        def acc_row(row, val):
            osm_ref[row:row + 1, :] += _colsum(val)

        def acc_wide(row, val):
            cs = _colsum(val)
            for j in range(D_MODEL // GROUP):
                osm_ref[row + j:row + j + 1, :] += cs[:, GROUP * j:GROUP * (j + 1)]

        def put_dh(j, val):
            acc_row(ROW_BIN + j, val)
            dhb_ref[:, GROUP * j:GROUP * (j + 1)] = val.astype(BF16)

        def dsilu(v, s):
            return s * (1.0 + v * (1.0 - s))

        dy = dy_ref[...]
        z = z_ref[...]
        cen = z - _rowmean(z)
        rstd = lax.rsqrt(_rowmean(cen * cen) + LN_EPS)
        xhat = cen * rstd
        acc_wide(ROW_LNG, dy * xhat)
        acc_wide(ROW_LNB, dy)
        gdy = dy * v1024_ref[1:2, :]
        dz = rstd * (gdy - _rowmean(gdy) - xhat * _rowmean(gdy * xhat))
        acc_wide(ROW_BOUT, dz)
        dzb = dz.astype(BF16)
        dzb_ref[...] = dzb

        def dmix(k):
            return _dot_nt(dzb, wo_ref[k])

        segm = seg_ref[...]

        a_val, a_glu, a_z = hs(0), hs(1), hs(2)
        sg = _sig(a_glu)
        a0_s[...] = a_val * sg
        a1 = aux_ref[:, 0:GROUP]
        cen = a1 - _segdot(a1, segm)
        rstd_a = lax.rsqrt(_segdot(cen * cen, segm) + LN_EPS)
        xh = cen * rstd_a
        a2 = xh * s256_ref[1:2, :] + s256_ref[2:3, :]
        s2 = _sig(a2)
        sz = _sig(a_z)
        dya = dmix(0)
        put_dh(2, dya * (a2 * s2) * dsilu(a_z, sz))
        d_a2 = dya * (a_z * sz) * dsilu(a2, s2)
        acc_row(1, d_a2 * xh)
        acc_row(2, d_a2)
        gd = d_a2 * s256_ref[1:2, :]
        d_a1 = rstd_a * (gd - _segdot(gd, segm) - xh * _segdot(gd * xh, segm))
        acc_row(0, d_a1)
        dbuf[0:T, :] = d_a1
        for r0 in range(0, T, ROWS):
            a0c = a0_s[r0:r0 + ROWS, :]
            acc = None
            for k in range(KA):
                off = (KA - 1) - k + r0
                w = dbuf[off:off + ROWS, :]
                term = caw_ref[k:k + 1, :] * w
                acc = term if acc is None else acc + term
                acc_row(ROW_CAW + k, a0c * w)
            u_s[r0:r0 + ROWS, :] = acc
        dbuf[T:T + HALO_A, :] = dbuf[0:HALO_A, :]
        d_a0 = u_s[...]
        put_dh(0, d_a0 * sg)
        put_dh(1, d_a0 * a_val * sg * (1.0 - sg))

        b_b, b_c, b_h, b_z = hs(3), hs(4), hs(5), hs(6)
        cb = aux_ref[:, GROUP:2 * GROUP]
        sz = _sig(b_z)
        dyb = dmix(1)
        put_dh(3, dyb * cb * (b_z * sz))
        put_dh(6, dyb * b_b * cb * dsilu(b_z, sz))
        ebuf[0:T, :] = dyb * b_b * (b_z * sz)
        a0_s[...] = b_c * b_h
        for r0 in range(0, T, ROWS):
            uc = a0_s[r0:r0 + ROWS, :]
            acc = None
            for k in range(KB):
                off = (KB - 1) - k + r0
                w = ebuf[off:off + ROWS, :]
                term = cbw_ref[k:k + 1, :] * w
                acc = term if acc is None else acc + term
                acc_row(ROW_CBW + k, uc * w)
            u_s[r0:r0 + ROWS, :] = acc
        ebuf[T:T + HALO_B, :] = ebuf[0:HALO_B, :]
        d_u = u_s[...]
        put_dh(4, d_u * b_h)
        put_dh(5, d_u * b_c)

        c_z = hs(8)
        pooled = aux_ref[:, 2 * GROUP:3 * GROUP]
        pooled_b = pooled.astype(BF16)
        q = _dot(pooled_b, pw_ref[...])
        sz = _sig(c_z)
        dyc = dmix(2)
        ps = s256_ref[3:4, :]
        acc_row(3, dyc * q * (c_z * sz))
        put_dh(8, dyc * q * ps * dsilu(c_z, sz))
        d_q = (dyc * ps * (c_z * sz)).astype(BF16)
        pw_acc[...] += _dot_tn(pooled_b, d_q)
        d_pooled = _dot_nt(d_q, pw_ref[...])
        fbuf[0:T, :] = d_pooled / _pool_cnt(tile_idx, T)
        hi_lane = (lax.broadcasted_iota(jnp.int32, (1, 128), 1) // HEAD) == 1
        for r0 in range(0, T, ROWS):
            def win(col, j0, j1):
                s = None
                for j in range(j0, j1):
                    term = fbuf[r0 + j:r0 + j + ROWS, 128 * col:128 * (col + 1)]
                    s = term if s is None else s + term
                return s
            u_s[r0:r0 + ROWS, 0:128] = win(0, 0, 2) + jnp.where(hi_lane, win(0, 2, 4), 0.0)
            u_s[r0:r0 + ROWS, 128:256] = win(1, 0, 8) + jnp.where(hi_lane, win(1, 8, 16), 0.0)
        fbuf[T:T + HALO_C, :] = fbuf[0:HALO_C, :]
        put_dh(7, u_s[...] - d_pooled)

        d_u_, d_v_, d_z_ = hs(9), hs(10), hs(11)
        cen = d_v_ - _rowmean(d_v_)
        rstd_v = lax.rsqrt(_rowmean(cen * cen) + LN_EPS)
        xv = cen * rstd_v
        v = xv * s256_ref[4:5, :] + s256_ref[5:6, :]
        sz = _sig(d_z_)
        dyd = dmix(3)
        d_sp = dyd * d_u_ * (d_z_ * sz)
        grp = _lane_group(GROUP)
        sps, dvs = [], []
        for n in range(T // SGU_BLOCK):
            blk = slice(n * SGU_BLOCK, (n + 1) * SGU_BLOCK)
            vst = _vstack(v[blk, :])
            sps.append(_dot(wm_s[...], vst) + sb_ref[...])
            dspb = d_sp[blk, :]
            dsp_acc[...] += dspb
            dspb16 = dspb.astype(BF16)
            dvst = _dot(wmt_s[...], dspb16)
            dvb = None
            for hh in range(4):
                part = jnp.where(grp == hh, dvst[hh * SGU_BLOCK:(hh + 1) * SGU_BLOCK, :], 0.0)
                dvb = part if dvb is None else dvb + part
            dvs.append(dvb)
            dwc = _dot_nt(dspb16, vst)
            osm_ref[ROW_WC:ROW_WC + SGU_BLOCK, :] += dwc[:, 0:GROUP]
            osm_ref[ROW_WC + SGU_BLOCK:ROW_WC + 2 * SGU_BLOCK, :] += dwc[:, GROUP:2 * GROUP]
        sp = jnp.concatenate(sps, axis=0)
        d_v = jnp.concatenate(dvs, axis=0)
        put_dh(9, dyd * sp * (d_z_ * sz))
        put_dh(11, dyd * d_u_ * sp * dsilu(d_z_, sz))
        acc_row(4, d_v * xv)
        acc_row(5, d_v)
        gd = d_v * s256_ref[4:5, :]
        put_dh(10, rstd_v * (gd - _rowmean(gd) - xv * _rowmean(gd * xv)))

        dx = alpha * dz
        for k in range(N_CHIPS):
            dx = dx + _dot_nt(dhb_ref[:, COLS * k:COLS * (k + 1)], wi_ref[k])
        dx_ref[...] = dx

        @pl.when(i == nt - 1)
        def _():
            r = lax.broadcasted_iota(jnp.int32, (SGU_BLOCK, GROUP), 0) // CHUNK
            c = (lax.broadcasted_iota(jnp.int32, (SGU_BLOCK, GROUP), 1) % SGU_BLOCK) // CHUNK
            for half in range(2):
                rows_ = slice(ROW_WC + half * SGU_BLOCK, ROW_WC + (half + 1) * SGU_BLOCK)
                osm_ref[rows_, :] = jnp.where(c <= r, osm_ref[rows_, :], 0.0)
            osm_ref[ROW_SB:ROW_SB + SGU_BLOCK, 0:128] = _segdot(dsp_acc[...], e4_ref[...])
            for g in range(4):
                osm_ref[ROW_PW:ROW_PW + HEAD, HEAD * g:HEAD * (g + 1)] = (
                    pw_acc[HEAD * g:HEAD * (g + 1), HEAD * g:HEAD * (g + 1)])

    def full(a):
        nd = a.ndim
        return pl.BlockSpec(a.shape, lambda i, _n=nd: (0,) * _n)

    def rows(width):
        return pl.BlockSpec((T, width), lambda i: (nt - 1 - i, 0))

    def acc(shape):
        return pl.BlockSpec(shape, lambda i: (0, 0))

    consts = (wi, caw, cbw, s256, seg, pw, wm, wmt, sb, wo, v1024, e4)
    in_specs = [rows(D_MODEL), rows(D_MODEL), rows(IN_WIDTH), rows(3 * GROUP)] + [full(a) for a in consts]
    out_specs = [rows(D_MODEL), rows(IN_WIDTH), rows(D_MODEL), acc((SM_ROWS, GROUP))]
    out_shape = [jax.ShapeDtypeStruct((S, D_MODEL), F32), jax.ShapeDtypeStruct((S, IN_WIDTH), BF16),
                 jax.ShapeDtypeStruct((S, D_MODEL), BF16), jax.ShapeDtypeStruct((SM_ROWS, GROUP), F32)]
    scratch = [pltpu.VMEM((T + HALO_A, GROUP), F32), pltpu.VMEM((T + HALO_B, GROUP), F32),
               pltpu.VMEM((T + HALO_C, GROUP), F32), pltpu.VMEM((T, GROUP), F32), pltpu.VMEM((T, GROUP), F32),
               pltpu.VMEM((SGU_BLOCK, 4 * SGU_BLOCK), BF16), pltpu.VMEM((4 * SGU_BLOCK, SGU_BLOCK), BF16),
               pltpu.VMEM((SGU_BLOCK, GROUP), F32), pltpu.VMEM((GROUP, GROUP), F32)]
    extra, aliases = (), {}
    if exch is not None:
        extra = tuple(exch)
        r_i, r_o = exch[4], exch[5]
        in_specs += [pl.BlockSpec(memory_space=pltpu.SMEM)] + [ANY] * 5
        out_specs += [ANY] * 3
        out_shape += [jax.ShapeDtypeStruct(r_i.shape, r_i.dtype), jax.ShapeDtypeStruct(r_o.shape, r_o.dtype),
                      jax.ShapeDtypeStruct((N_DEV, SM_ROWS, GROUP), F32)]
        scratch += [pltpu.SemaphoreType.DMA((13,)), pltpu.SemaphoreType.DMA((13,)), pltpu.SemaphoreType.DMA((1,))]
        aliases = {20: 4, 21: 5}
    return pl.pallas_call(
        body, name="bwd_layer" if exch is None else "bwd_layer_exchange",
        grid=(nt,), in_specs=in_specs, out_specs=out_specs, out_shape=out_shape, scratch_shapes=scratch,
        input_output_aliases=aliases,
        compiler_params=_vmem_params(dimension_semantics=("arbitrary",), has_side_effects=exch is not None),
    )(dy, z, h, aux, *consts, *extra)


def _dw_in(layer, xb, dhb, slab, slab16, *, tk, small=None):
    S = xb.shape[0]
    ns = S // tk

    def body(*refs):
        l_ref, a_ref, b_ref = refs[0:3]
        o_ref, o16_ref = refs[n_in:n_in + 2]
        if small is not None:
            first = (pl.program_id(0) == 0) & (pl.program_id(1) == 0)
            last = (pl.program_id(0) == N_CHIPS - 1) & (pl.program_id(1) == ns - 1)
            _exchange_comm(first, last, None, None, None, refs[5], None, None, refs[n_in + 2], *refs[n_in + 3:])

        @pl.when(pl.program_id(1) == 0)
        def _():
            o_ref[...] = jnp.zeros_like(o_ref)
        o_ref[...] += _dot_tn(a_ref[...], b_ref[...])

        @pl.when(pl.program_id(1) == ns - 1)
        def _():
            o16_ref[...] = o_ref[...].astype(BF16)

    o_spec = pl.BlockSpec((None, None, D_MODEL, COLS), lambda j, s, l: (l[0], j, 0, 0))
    in_specs = [pl.BlockSpec((tk, D_MODEL), lambda j, s, l: (s, 0)), pl.BlockSpec((tk, COLS), lambda j, s, l: (s, j)),
                ANY, ANY]
    out_specs = [o_spec, o_spec]
    out_shape = [jax.ShapeDtypeStruct(slab.shape, F32), jax.ShapeDtypeStruct(slab.shape, BF16)]
    scratch, extra = [], ()
    if small is not None:
        extra = (small,)
        in_specs += [ANY]
        out_specs += [ANY]
        out_shape += [jax.ShapeDtypeStruct((N_DEV, SM_ROWS, GROUP), F32)]
        scratch = [pltpu.SemaphoreType.DMA((13,)), pltpu.SemaphoreType.DMA((13,)), pltpu.SemaphoreType.DMA((1,))]
    n_in = 5 + len(extra)
    grid_spec = pltpu.PrefetchScalarGridSpec(
        num_scalar_prefetch=1, grid=(N_CHIPS, ns), in_specs=in_specs, out_specs=out_specs, scratch_shapes=scratch)
    return pl.pallas_call(
        body, name="dw_in" if small is None else "dw_in_exchange", grid_spec=grid_spec, out_shape=out_shape,
        input_output_aliases={3: 0, 4: 1},
        compiler_params=_vmem_params(dimension_semantics=("arbitrary", "arbitrary"), has_side_effects=small is not None),
    )(layer, xb, dhb, slab, slab16, *extra)


def _dw_out(layer, mixb, dzb, slab, slab16, *, tk):
    S = mixb.shape[0]
    ns = S // tk

    def body(l_ref, a_ref, b_ref, slab_ref, slab16_ref, o_ref, o16_ref):
        del l_ref, slab_ref, slab16_ref

        @pl.when(pl.program_id(0) == 0)
        def _():
            o_ref[...] = jnp.zeros_like(o_ref)
        o_ref[...] += _dot_tn(a_ref[...], b_ref[...]).reshape(N_CHIPS, GROUP, D_MODEL)

        @pl.when(pl.program_id(0) == ns - 1)
        def _():
            o16_ref[...] = o_ref[...].astype(BF16)

    o_spec = pl.BlockSpec((None, N_CHIPS, GROUP, D_MODEL), lambda s, l: (l[0], 0, 0, 0))
    grid_spec = pltpu.PrefetchScalarGridSpec(
        num_scalar_prefetch=1, grid=(ns,),
        in_specs=[pl.BlockSpec((tk, D_MODEL), lambda s, l: (s, 0)), pl.BlockSpec((tk, D_MODEL), lambda s, l: (s, 0)),
                  ANY, ANY],
        out_specs=[o_spec, o_spec])
    return pl.pallas_call(
        body, name="dw_out", grid_spec=grid_spec,
        out_shape=[jax.ShapeDtypeStruct(slab.shape, F32), jax.ShapeDtypeStruct(slab.shape, BF16)],
        input_output_aliases={3: 0, 4: 1},
        compiler_params=_vmem_params(dimension_semantics=("arbitrary",)),
    )(layer, mixb, dzb, slab, slab16)


def _adamw_math(w, g, m, v):
    nm = ADAM_B1 * m + (1.0 - ADAM_B1) * g
    nv = ADAM_B2 * v + (1.0 - ADAM_B2) * (g * g)
    c1 = 1.0 - ADAM_B1 ** ADAM_STEP
    c2 = 1.0 - ADAM_B2 ** ADAM_STEP
    return -ADAM_LR * ((nm / c1) / (jnp.sqrt(nv / c2) + ADAM_EPS) + ADAM_WD * w), nm, nv


def _adamw_small(ws, gs, ms, vs):
    n = len(ws)

    def body(*refs):
        for j in range(n):
            d, nm, nv = _adamw_math(*(refs[k * n + j][...] for k in range(4)))
            refs[4 * n + j][...] = d
            refs[5 * n + j][...] = nm
            refs[6 * n + j][...] = nv

    shapes = [jax.ShapeDtypeStruct(w.shape, F32) for w in ws]
    outs = pl.pallas_call(body, name="adamw_small", out_shape=shapes * 3, compiler_params=_vmem_params())(
        *ws, *gs, *ms, *vs)
    return outs[0:n], outs[n:2 * n], outs[2 * n:3 * n]


def _adamw(w, g, m, v, *, rows_per_step, name, copy_g=False):
    R, C = w.shape
    tr = rows_per_step

    def body(w_ref, g_ref, m_ref, v_ref, d_ref, nm_ref, nv_ref, *g_out):
        g_ = g_ref[...]
        d_ref[...], nm_ref[...], nv_ref[...] = _adamw_math(w_ref[...], g_, m_ref[...], v_ref[...])
        if copy_g:
            g_out[0][...] = g_

    spec = pl.BlockSpec((tr, C), lambda i: (i, 0))
    n_out = 4 if copy_g else 3
    return pl.pallas_call(
        body, name=name, grid=(R // tr,),
        in_specs=[spec] * 4, out_specs=[spec] * n_out,
        out_shape=[jax.ShapeDtypeStruct((R, C), F32)] * n_out,
        compiler_params=_vmem_params(dimension_semantics=("arbitrary",)),
    )(w, g, m, v)


def _gather_weights(wi16, wo16, cw):
    L = wi16.shape[0]
    hi_rows, ho_rows = D_MODEL // 2, GROUP // 2
    n_ici = 2 * L + 1
    n_fwd = 2 * L

    def body(wi_ref, wo_ref, cw_ref, *rest):
        wig = rest[0:L]
        wog = rest[L:2 * L]
        cwg = rest[2 * L]
        send_sems, recv_sems, loc_sems = rest[2 * L + 1:]
        x, y, c = _place()
        me_k = 2 * x + y
        sibling = (x, y, 1 - c)
        chips = _other_chips(x, y)

        def half_i(ref, blk):
            return ref.at[blk, pl.ds(c * hi_rows, hi_rows), :]

        def half_o(ref, blk):
            return ref.at[blk, pl.ds(c * ho_rows, ho_rows), :]

        def other_half_i(ref, blk):
            return ref.at[blk, pl.ds((1 - c) * hi_rows, hi_rows), :]

        def other_half_o(ref, blk):
            return ref.at[blk, pl.ds((1 - c) * ho_rows, ho_rows), :]

        local = []
        for l in range(L):
            local.append(pltpu.make_async_copy(wi_ref.at[l], wig[l].at[me_k], loc_sems.at[2 * l]))
            local.append(pltpu.make_async_copy(wo_ref.at[l], wog[l].at[me_k], loc_sems.at[2 * l + 1]))
        local.append(pltpu.make_async_copy(cw_ref, cwg.at[me_k], loc_sems.at[2 * L]))
        for cp in local:
            cp.start()

        def remote(src, dst, sem, to):
            return pltpu.make_async_remote_copy(src_ref=src, dst_ref=dst, send_sem=send_sems.at[sem],
                                                recv_sem=recv_sems.at[sem], device_id=to, device_id_type=MESH)

        sends = []
        for r, (px, py, _) in enumerate(chips):
            to = (px, py, c)
            for l in range(L):
                sends.append(remote(half_i(wi_ref, l), half_i(wig[l], me_k), r * n_ici + 2 * l, to))
                sends.append(remote(half_o(wo_ref, l), half_o(wog[l], me_k), r * n_ici + 2 * l + 1, to))
            sends.append(remote(cw_ref, cwg.at[me_k], r * n_ici + 2 * L, to))
        for cp in sends:
            cp.start()

        base = 3 * n_ici
        fwds = []
        for r, (px, py, pk) in enumerate(chips):
            for l in range(L):
                remote(half_i(wig[l], pk), half_i(wig[l], pk), r * n_ici + 2 * l, sibling).wait_recv()
                f = remote(half_i(wig[l], pk), half_i(wig[l], pk), base + r * n_fwd + 2 * l, sibling)
                f.start()
                fwds.append(f)
                remote(half_o(wog[l], pk), half_o(wog[l], pk), r * n_ici + 2 * l + 1, sibling).wait_recv()
                f = remote(half_o(wog[l], pk), half_o(wog[l], pk), base + r * n_fwd + 2 * l + 1, sibling)
                f.start()
                fwds.append(f)
            remote(cwg.at[pk], cwg.at[pk], r * n_ici + 2 * L, sibling).wait_recv()
        for r, (px, py, pk) in enumerate(chips):
            for l in range(L):
                remote(other_half_i(wig[l], pk), other_half_i(wig[l], pk), base + r * n_fwd + 2 * l, sibling).wait_recv()
                remote(other_half_o(wog[l], pk), other_half_o(wog[l], pk), base + r * n_fwd + 2 * l + 1, sibling).wait_recv()
        for cp in sends + fwds:
            cp.wait_send()
        for cp in local:
            cp.wait()

    n_sem = 3 * n_ici + 3 * n_fwd
    out_shape = ([jax.ShapeDtypeStruct((N_CHIPS, D_MODEL, COLS), BF16)] * L
                 + [jax.ShapeDtypeStruct((N_CHIPS, GROUP, D_MODEL), BF16)] * L
                 + [jax.ShapeDtypeStruct((N_CHIPS,) + cw.shape, F32)])
    outs = pl.pallas_call(
        body, name="gather_weights",
        in_specs=[ANY, ANY, ANY], out_specs=[ANY] * (2 * L + 1), out_shape=out_shape,
        scratch_shapes=[pltpu.SemaphoreType.DMA((n_sem,)), pltpu.SemaphoreType.DMA((n_sem,)),
                        pltpu.SemaphoreType.DMA((2 * L + 1,))],
        compiler_params=pltpu.CompilerParams(has_side_effects=True),
    )(wi16, wo16, cw)
    return outs[0:L], outs[L:2 * L], outs[2 * L]


def _swap_halves(l_arr, gwi, gwo, ri, ro):
    hi_rows, ho_rows = D_MODEL // 2, GROUP // 2

    def body(l_ref, gwi_ref, gwo_ref, ri_in, ro_in, ri_ref, ro_ref, send_sems, recv_sems):
        del ri_in, ro_in
        x, y, c = _place()
        l = l_ref[0]
        sibling = (x, y, 1 - c)
        cps = [
            pltpu.make_async_remote_copy(src_ref=gwi_ref.at[l, :, pl.ds((1 - c) * hi_rows, hi_rows), :],
                                         dst_ref=ri_ref.at[l], send_sem=send_sems.at[0], recv_sem=recv_sems.at[0],
                                         device_id=sibling, device_id_type=MESH),
            pltpu.make_async_remote_copy(src_ref=gwo_ref.at[l, :, pl.ds((1 - c) * ho_rows, ho_rows), :],
                                         dst_ref=ro_ref.at[l], send_sem=send_sems.at[1], recv_sem=recv_sems.at[1],
                                         device_id=sibling, device_id_type=MESH),
        ]
        for cp in cps:
            cp.start()
        for cp in cps:
            cp.wait()

    return pl.pallas_call(
        body, name="swap_halves",
        in_specs=[pl.BlockSpec(memory_space=pltpu.SMEM), ANY, ANY, ANY, ANY], out_specs=[ANY, ANY],
        out_shape=[jax.ShapeDtypeStruct(ri.shape, ri.dtype), jax.ShapeDtypeStruct(ro.shape, ro.dtype)],
        input_output_aliases={3: 0, 4: 1},
        scratch_shapes=[pltpu.SemaphoreType.DMA((2,)), pltpu.SemaphoreType.DMA((2,))],
        compiler_params=pltpu.CompilerParams(has_side_effects=True),
    )(l_arr, gwi, gwo, ri, ro)


def _add_halves(cl_arr, g, r, p, *, rows, cols, tr, name):
    nb = rows // tr

    def body(cl_ref, g_ref, r_ref, p_in, o_ref):
        del cl_ref, p_in
        o_ref[...] = (g_ref[...] + r_ref[...].astype(F32)).astype(o_ref.dtype)

    grid_spec = pltpu.PrefetchScalarGridSpec(
        num_scalar_prefetch=1, grid=(N_CHIPS, nb),
        in_specs=[pl.BlockSpec((None, None, tr, cols), lambda k, i, cl: (cl[1], k, cl[0] * nb + i, 0)),
                  pl.BlockSpec((None, None, tr, cols), lambda k, i, cl: (cl[1], k, i, 0)), ANY],
        out_specs=pl.BlockSpec((None, None, tr, cols), lambda k, i, cl: (cl[1], k, i, 0)))
    return pl.pallas_call(
        body, name=name, grid_spec=grid_spec,
        out_shape=jax.ShapeDtypeStruct(p.shape, p.dtype),
        input_output_aliases={3: 0},
        compiler_params=_vmem_params(dimension_semantics=("arbitrary",) * 2),
    )(cl_arr, g, r, p)


def _exchange_last(l_arr, p_i, p_o, r_i, r_o):
    def body(l_ref, p_i_ref, p_o_ref, ri_in, ro_in, ri_ref, ro_ref, send_sems, recv_sems):
        del ri_in, ro_in
        always = l_ref[0] >= 0
        _exchange_comm(always, always, l_ref[0], p_i_ref, p_o_ref, None, ri_ref, ro_ref, None,
                       send_sems, recv_sems, None)

    return pl.pallas_call(
        body, name="exchange_last",
        in_specs=[pl.BlockSpec(memory_space=pltpu.SMEM)] + [ANY] * 4, out_specs=[ANY] * 2,
        out_shape=[jax.ShapeDtypeStruct(r_i.shape, r_i.dtype), jax.ShapeDtypeStruct(r_o.shape, r_o.dtype)],
        input_output_aliases={3: 0, 4: 1},
        scratch_shapes=[pltpu.SemaphoreType.DMA((13,)), pltpu.SemaphoreType.DMA((13,))],
        compiler_params=pltpu.CompilerParams(has_side_effects=True),
    )(l_arr, p_i, p_o, r_i, r_o)


def _sum_small(r_sms):
    L = len(r_sms)

    def body(*refs):
        o_ref = refs[L]
        for l in range(L):
            acc = refs[l][0]
            for d in range(1, N_DEV):
                acc = acc + refs[l][d]
            o_ref[l] = acc

    return pl.pallas_call(
        body, name="sum_small",
        out_shape=jax.ShapeDtypeStruct((L,) + r_sms[0].shape[1:], F32),
        compiler_params=_vmem_params(),
    )(*r_sms)


def _sum_chunks(kc_arr, p, r, *, rows, cols, tr, name):
    L = p.shape[0]
    nb = rows // tr

    def body(kc_ref, p_ref, r0_ref, r1_ref, r2_ref, o_ref):
        del kc_ref
        f = lambda ref: ref[...].astype(F32)
        o_ref[...] = ((f(p_ref) + f(r0_ref)) + f(r1_ref)) + f(r2_ref)

    def rspec(j):
        return pl.BlockSpec((None, None, tr, cols), lambda l, i, kc, _j=j: (_j, l, i, 0))

    grid_spec = pltpu.PrefetchScalarGridSpec(
        num_scalar_prefetch=1, grid=(L, nb),
        in_specs=[pl.BlockSpec((None, None, tr, cols), lambda l, i, kc: (l, kc[0], i, 0)), rspec(0), rspec(1), rspec(2)],
        out_specs=pl.BlockSpec((None, tr, cols), lambda l, i, kc: (l, kc[1] * nb + i, 0)))
    return pl.pallas_call(
        body, name=name, grid_spec=grid_spec,
        out_shape=jax.ShapeDtypeStruct((L, 2 * rows, cols), F32),
        compiler_params=_vmem_params(dimension_semantics=("arbitrary",) * 2),
    )(kc_arr, p, r, r, r)


def _share_result(gi, go):
    hi_rows, ho_rows = gi.shape[1] // 2, go.shape[1] // 2

    def body(gi_ref, go_ref, oi_ref, oo_ref, send_sems, recv_sems):
        del gi_ref, go_ref
        x, y, c = _place()
        sibling = (x, y, 1 - c)
        cps = []
        for j, (ref, n) in enumerate(((oi_ref, hi_rows), (oo_ref, ho_rows))):
            mine = ref.at[:, pl.ds(c * n, n), :]
            cps.append(pltpu.make_async_remote_copy(src_ref=mine, dst_ref=mine, send_sem=send_sems.at[j],
                                                    recv_sem=recv_sems.at[j], device_id=sibling, device_id_type=MESH))
        for cp in cps:
            cp.start()
        for j, (ref, n) in enumerate(((oi_ref, hi_rows), (oo_ref, ho_rows))):
            theirs = ref.at[:, pl.ds((1 - c) * n, n), :]
            pltpu.make_async_remote_copy(src_ref=theirs, dst_ref=theirs, send_sem=send_sems.at[j],
                                         recv_sem=recv_sems.at[j], device_id=sibling, device_id_type=MESH).wait_recv()
        for cp in cps:
            cp.wait_send()

    return pl.pallas_call(
        body, name="share_result",
        in_specs=[ANY, ANY], out_specs=[ANY, ANY],
        out_shape=[jax.ShapeDtypeStruct(gi.shape, F32), jax.ShapeDtypeStruct(go.shape, F32)],
        input_output_aliases={0: 0, 1: 1},
        scratch_shapes=[pltpu.SemaphoreType.DMA((2,)), pltpu.SemaphoreType.DMA((2,))],
        compiler_params=pltpu.CompilerParams(has_side_effects=True),
    )(gi, go)


WEIGHTS = ("ln_g", "ln_b", "w_in", "b_in", "conv_a_w", "conv_a_b", "norm_a_g", "norm_a_b", "conv_b_w", "pool_w",
           "pool_scale", "sgu_ln_g", "sgu_ln_b", "sgu_w", "sgu_bias", "w_out", "b_out")


def _pad_rows(a, rows):
    return jnp.pad(a, ((0, rows - a.shape[0]), (0, 0)))


def _indicator_consts():
    seg = jnp.where((jnp.arange(GROUP)[:, None] // HEAD) == (jnp.arange(GROUP)[None, :] // HEAD),
                    1.0 / HEAD, 0.0).astype(BF16)
    e4 = ((jnp.arange(GROUP)[:, None] // HEAD) == jnp.arange(128)[None, :]).astype(BF16)
    return seg, e4


def _layer_consts(p, conv_full, l):
    same_head = jnp.eye(4, dtype=F32)[:, None, :, None] > 0
    caw = _pad_rows(conv_full[l, :KA], 32)
    cbw = _pad_rows(conv_full[l, KA:], 8)
    s256 = _pad_rows(jnp.stack([p["conv_a_b"][l], p["norm_a_g"][l], p["norm_a_b"][l], p["pool_scale"][l],
                                p["sgu_ln_g"][l], p["sgu_ln_b"][l]]), 8)
    pw = jnp.where(same_head, p["pool_w"][l][:, :, None, :], 0.0).reshape(GROUP, GROUP).astype(BF16)
    wm = jnp.transpose(p["sgu_w"][l], (1, 0, 2)).reshape(SGU_BLOCK, 4 * SGU_BLOCK)
    wmt = jnp.transpose(p["sgu_w"][l], (0, 2, 1)).reshape(4 * SGU_BLOCK, SGU_BLOCK)
    sb = jnp.repeat(p["sgu_bias"][l].T, HEAD, axis=1)
    v1024 = _pad_rows(jnp.stack([p["b_out"][l], p["ln_g"][l], p["ln_b"][l]]), 8)
    return dict(caw=caw, cbw=cbw, s256=s256, pw=pw, wm=wm, wmt=wmt, sb=sb, v1024=v1024, bin=p["b_in"][l][None, :])


def _unpack_small(sm):
    L = sm.shape[0]
    owc = jnp.concatenate([sm[:, ROW_WC:ROW_WC + SGU_BLOCK], sm[:, ROW_WC + SGU_BLOCK:ROW_WC + 2 * SGU_BLOCK]], axis=2)
    return dict(
        conv_a_b=sm[:, 0], norm_a_g=sm[:, 1], norm_a_b=sm[:, 2], pool_scale=sm[:, 3], sgu_ln_g=sm[:, 4],
        sgu_ln_b=sm[:, 5], conv_b_w=sm[:, ROW_CBW:ROW_CBW + KB], conv_a_w=sm[:, ROW_CAW:ROW_CAW + KA],
        pool_w=jnp.transpose(sm[:, ROW_PW:ROW_PW + HEAD].reshape(L, HEAD, 4, HEAD), (0, 2, 1, 3)),
        ln_g=sm[:, ROW_LNG:ROW_LNG + 4].reshape(L, D_MODEL), ln_b=sm[:, ROW_LNB:ROW_LNB + 4].reshape(L, D_MODEL),
        b_out=sm[:, ROW_BOUT:ROW_BOUT + 4].reshape(L, D_MODEL),
        b_in=sm[:, ROW_BIN:ROW_BIN + N_SLICES].reshape(L, IN_WIDTH),
        sgu_w=jnp.transpose(owc.reshape(L, SGU_BLOCK, 4, SGU_BLOCK), (0, 2, 1, 3)),
        sgu_bias=sm[:, ROW_SB:ROW_SB + 4, 0:SGU_BLOCK])


def _step(p, m, v, x, target, *, tile_f, tile_b, tk):
    L = p["ln_g"].shape[0]
    xi, yi, ci = _place()
    me_k = 2 * xi + yi
    hi_rows, ho_rows = D_MODEL // 2, GROUP // 2

    cw = jnp.concatenate([p["conv_a_w"], p["conv_b_w"]], axis=1).reshape(-1, 128)
    cw_rows = cw.shape[0]
    cw = _pad_rows(cw, 72)
    wi16 = p["w_in"].astype(BF16)
    wo16 = p["w_out"].astype(BF16)
    wig0, wog0, cwg = _gather_weights(wi16[0:1], wo16[0:1], cw)
    cwg = cwg[:, :cw_rows].reshape(N_CHIPS, L, KA + KB, HEAD)
    conv_full = jnp.transpose(cwg, (1, 2, 0, 3)).reshape(L, KA + KB, GROUP)
    seg, e4 = _indicator_consts()
    consts = [_layer_consts(p, conv_full, l) for l in range(L)]

    hcur = x
    saved, wig, wog = [], [wig0[0]], [wog0[0]]
    for l in range(L):
        k = consts[l]
        nxt = (wi16[l + 1], wo16[l + 1]) if l + 1 < L else None
        outs = _fwd_layer(hcur, wig[l], k["bin"], k["caw"], k["cbw"], k["s256"], seg, k["pw"], k["wm"], k["sb"], wog[l],
                          k["v1024"], tile=tile_f, nxt=nxt, target=None if nxt is not None else target)
        y, xb, h, aux, mixb, z = outs[0:6]
        if nxt is not None:
            wig.append(outs[6])
            wog.append(outs[7])
        saved.append((xb, h, aux, mixb, z))
        hcur = y

    dy = hcur
    loss_local = outs[6][0, 0]

    gwi = lax.empty((L, N_CHIPS, D_MODEL, COLS), F32)
    gwo = lax.empty((L, N_CHIPS, GROUP, D_MODEL), F32)
    gwi16 = lax.empty((L, N_CHIPS, D_MODEL, COLS), BF16)
    gwo16 = lax.empty((L, N_CHIPS, GROUP, D_MODEL), BF16)
    ri = lax.empty((L, N_CHIPS, hi_rows, COLS), BF16)
    ro = lax.empty((L, N_CHIPS, ho_rows, D_MODEL), BF16)
    p_i = lax.empty((L, N_CHIPS, hi_rows, COLS), BF16)
    p_o = lax.empty((L, N_CHIPS, ho_rows, D_MODEL), BF16)
    q_i = lax.empty((3, L, hi_rows, COLS), BF16)
    q_o = lax.empty((3, L, ho_rows, D_MODEL), BF16)
    r_sm = [None] * L
    pending = None
    for l in reversed(range(L)):
        k = consts[l]
        xb, h, aux, mixb, z = saved[l]
        exch = None if pending is None else (pending[0], p_i, p_o, pending[1], q_i, q_o)
        outs = _bwd_layer(dy, z, h, aux, wig[l], k["caw"], k["cbw"], k["s256"], seg, k["pw"], k["wm"], k["wmt"],
                          k["sb"], wog[l], k["v1024"], e4, tile=tile_b, exch=exch)
        dy, dhb, dzb, osm = outs[0:4]
        if l == L - 1:
            osm = osm.at[ROW_LOSS, 0].set(loss_local)
        if exch is not None:
            q_i, q_o, r_sm[l + 1] = outs[4:7]
        larr = jnp.full((1,), l, jnp.int32)
        if l > 0:
            gwi, gwi16 = _dw_in(larr, xb, dhb, gwi, gwi16, tk=tk)
        else:
            gwi, gwi16, r_sm[0] = _dw_in(larr, xb, dhb, gwi, gwi16, tk=tk, small=osm)
        gwo, gwo16 = _dw_out(larr, mixb, dzb, gwo, gwo16, tk=tk)
        ri, ro = _swap_halves(larr, gwi16, gwo16, ri, ro)
        cl_arr = jnp.stack([ci, jnp.int32(l)]).astype(jnp.int32)
        p_i = _add_halves(cl_arr, gwi, ri, p_i, rows=hi_rows, cols=COLS, tr=256, name="add_halves_in")
        p_o = _add_halves(cl_arr, gwo, ro, p_o, rows=ho_rows, cols=D_MODEL, tr=128, name="add_halves_out")
        pending = (larr, osm)
    grad_x = dy
    q_i, q_o = _exchange_last(pending[0], p_i, p_o, q_i, q_o)

    summed = _sum_small(r_sm)
    loss = summed[L - 1, ROW_LOSS, 0]
    grads = _unpack_small(summed)
    for n in ("conv_a_w", "conv_b_w"):
        grads[n] = lax.dynamic_slice_in_dim(grads[n], me_k * HEAD, HEAD, axis=2)

    kc_arr = jnp.stack([me_k, ci]).astype(jnp.int32)
    g_i = _sum_chunks(kc_arr, p_i, q_i, rows=hi_rows, cols=COLS, tr=256, name="sum_chunks_in")
    g_o = _sum_chunks(kc_arr, p_o, q_o, rows=ho_rows, cols=D_MODEL, tr=128, name="sum_chunks_out")
    g_i, g_o = _share_result(g_i, g_o)
    grads["w_in"] = g_i
    grads["w_out"] = g_o

    delta, new_m, new_v = {}, {}, {}
    for n, tr in (("w_in", 512), ("w_out", 256)):
        shp = p[n].shape
        args = [a.reshape(shp[0] * shp[1], shp[2]) for a in (p[n], grads[n], m[n], v[n])]
        outs = _adamw(*args, rows_per_step=tr, name="adamw_" + n, copy_g=True)
        delta[n], new_m[n], new_v[n], grads[n] = (a.reshape(shp) for a in outs)
    small = [n for n in WEIGHTS if n not in ("w_in", "w_out")]
    flat = [[a[n].reshape(-1, a[n].shape[-1]) for n in small] for a in (p, grads, m, v)]
    outs = _adamw_small(*flat)
    for j, n in enumerate(small):
        delta[n], new_m[n], new_v[n] = (o[j].reshape(p[n].shape) for o in outs)

    return (loss, grad_x[None], *[grads[n] for n in WEIGHTS], *[delta[n] for n in WEIGHTS],
            *[new_m[n] for n in WEIGHTS], *[new_v[n] for n in WEIGHTS])


def kernel(x, ln_g, ln_b, w_in, b_in, conv_a_w, conv_a_b, norm_a_g, norm_a_b, conv_b_w, pool_w, pool_scale, sgu_ln_g, sgu_ln_b, sgu_w, sgu_bias, w_out, b_out, loss_target, m_ln_g, m_ln_b, m_w_in, m_b_in, m_conv_a_w, m_conv_a_b, m_norm_a_g, m_norm_a_b, m_conv_b_w, m_pool_w, m_pool_scale, m_sgu_ln_g, m_sgu_ln_b, m_sgu_w, m_sgu_bias, m_w_out, m_b_out, v_ln_g, v_ln_b, v_w_in, v_b_in, v_conv_a_w, v_conv_a_b, v_norm_a_g, v_norm_a_b, v_conv_b_w, v_pool_w, v_pool_scale, v_sgu_ln_g, v_sgu_ln_b, v_sgu_w, v_sgu_bias, v_w_out, v_b_out):
    p = dict(ln_g=ln_g, ln_b=ln_b, w_in=w_in, b_in=b_in, conv_a_w=conv_a_w, conv_a_b=conv_a_b, norm_a_g=norm_a_g,
             norm_a_b=norm_a_b, conv_b_w=conv_b_w, pool_w=pool_w, pool_scale=pool_scale, sgu_ln_g=sgu_ln_g,
             sgu_ln_b=sgu_ln_b, sgu_w=sgu_w, sgu_bias=sgu_bias, w_out=w_out, b_out=b_out)
    m = dict(ln_g=m_ln_g, ln_b=m_ln_b, w_in=m_w_in, b_in=m_b_in, conv_a_w=m_conv_a_w, conv_a_b=m_conv_a_b,
             norm_a_g=m_norm_a_g, norm_a_b=m_norm_a_b, conv_b_w=m_conv_b_w, pool_w=m_pool_w, pool_scale=m_pool_scale,
             sgu_ln_g=m_sgu_ln_g, sgu_ln_b=m_sgu_ln_b, sgu_w=m_sgu_w, sgu_bias=m_sgu_bias, w_out=m_w_out, b_out=m_b_out)
    v = dict(ln_g=v_ln_g, ln_b=v_ln_b, w_in=v_w_in, b_in=v_b_in, conv_a_w=v_conv_a_w, conv_a_b=v_conv_a_b,
             norm_a_g=v_norm_a_g, norm_a_b=v_norm_a_b, conv_b_w=v_conv_b_w, pool_w=v_pool_w, pool_scale=v_pool_scale,
             sgu_ln_g=v_sgu_ln_g, sgu_ln_b=v_sgu_ln_b, sgu_w=v_sgu_w, sgu_bias=v_sgu_bias, w_out=v_w_out, b_out=v_b_out)
    return _step(p, m, v, x[0], loss_target[0], tile_f=256, tile_b=256, tk=2048)
```

```python
import jax
import jax.numpy as jnp
from jax import lax
from jax.experimental import pallas as pl
from jax.experimental.pallas import tpu as pltpu

F32 = jnp.float32
BF16 = jnp.bfloat16
MESH = pl.DeviceIdType.MESH

D_MODEL = 1024
GROUP = 256
HEAD = 64
N_SLICES = 12
IN_WIDTH = N_SLICES * GROUP
N_CHIPS = 4
COLS = IN_WIDTH // N_CHIPS
KA = 31
KB = 3
HALO_A, HALO_B, HALO_C = 32, 8, 16
SGU_BLOCK = 128
CHUNK = 64
LN_EPS = 1e-5
ROWS = 64
V7X_VMEM_BYTES = 64 * 1024 * 1024
VMEM_LIMIT = V7X_VMEM_BYTES - 8 * 1024 * 1024

ADAM_LR, ADAM_B1, ADAM_B2, ADAM_EPS, ADAM_WD, ADAM_STEP = 0.001, 0.9, 0.999, 1e-08, 0.01, 10


ANY = pl.BlockSpec(memory_space=pl.ANY)


def _vmem_params(**kw):
    return pltpu.CompilerParams(vmem_limit_bytes=VMEM_LIMIT, **kw)


def _whole(a):
    return pl.BlockSpec(a.shape, lambda i, l, _n=a.ndim: (0,) * _n)


def _of_layer(a):
    return pl.BlockSpec((None,) + a.shape[1:], lambda i, l, _n=a.ndim: (l[0],) + (0,) * (_n - 1))


def _place():
    return lax.axis_index("x"), lax.axis_index("y"), lax.axis_index("c")


def _other_chips(x, y):
    return [(1 - x, y, 2 * (1 - x) + y), (x, 1 - y, 2 * x + (1 - y)), (1 - x, 1 - y, 2 * (1 - x) + (1 - y))]


def _sig(v):
    return 0.5 * jnp.tanh(0.5 * v) + 0.5


def _dot(a, b):
    return jnp.dot(a, b, preferred_element_type=F32)


def _dot_nt(a, b):
    return lax.dot_general(a, b, (((1,), (1,)), ((), ())), preferred_element_type=F32)


def _dot_tn(a, b):
    return lax.dot_general(a, b, (((0,), (0,)), ((), ())), preferred_element_type=F32)


def _segdot(v, m):
    hi = v.astype(BF16)
    lo = (v - hi.astype(F32)).astype(BF16)
    return _dot(hi, m) + _dot(lo, m)


def _colsum(v):
    return jnp.sum(v, axis=0, keepdims=True)


def _rowmean(v):
    return jnp.mean(v, axis=-1, keepdims=True)


def _lane_group(n):
    return lax.broadcasted_iota(jnp.int32, (1, n), 1) // HEAD


def _pool_cnt(tile, t_rows):
    pos = tile * t_rows + lax.broadcasted_iota(jnp.int32, (t_rows, GROUP), 0) + 1
    grp = lax.broadcasted_iota(jnp.int32, (t_rows, GROUP), 1) // HEAD
    win = jnp.where(grp == 0, 2, jnp.where(grp == 1, 4, jnp.where(grp == 2, 8, 16)))
    return jnp.minimum(pos, win).astype(F32)


def _sgu_masks(wm_ref, wmt_ref, wm_s, wmt_s):
    r = lax.broadcasted_iota(jnp.int32, (SGU_BLOCK, 4 * SGU_BLOCK), 0) // CHUNK
    c = (lax.broadcasted_iota(jnp.int32, (SGU_BLOCK, 4 * SGU_BLOCK), 1) % SGU_BLOCK) // CHUNK
    wm_s[...] = jnp.where(c <= r, wm_ref[...], 0.0).astype(BF16)
    if wmt_ref is not None:
        rt = (lax.broadcasted_iota(jnp.int32, (4 * SGU_BLOCK, SGU_BLOCK), 0) % SGU_BLOCK) // CHUNK
        ct = lax.broadcasted_iota(jnp.int32, (4 * SGU_BLOCK, SGU_BLOCK), 1) // CHUNK
        wmt_s[...] = jnp.where(rt <= ct, wmt_ref[...], 0.0).astype(BF16)


def _vstack(v_blk):
    grp = _lane_group(GROUP)
    return jnp.concatenate([jnp.where(grp == h, v_blk, 0.0) for h in range(4)], axis=0).astype(BF16)


def _gather_next(step, nt, nwi, nwo, gwi, gwo, send_sems, recv_sems, loc_sems):
    x, y, c = _place()
    me_k = 2 * x + y
    sibling = (x, y, 1 - c)
    chips = _other_chips(x, y)
    hi, ho = D_MODEL // 2, GROUP // 2

    def rc(src, dst, sem, to):
        return pltpu.make_async_remote_copy(src_ref=src, dst_ref=dst, send_sem=send_sems.at[sem],
                                            recv_sem=recv_sems.at[sem], device_id=to, device_id_type=MESH)

    def blk(ref, k, n, cc):
        return ref.at[k, pl.ds(cc * n, n), :]

    def ici(r):
        px, py, _ = chips[r]
        to = (px, py, c)
        return [rc(nwi.at[pl.ds(c * hi, hi), :], blk(gwi, me_k, hi, c), 2 * r, to),
                rc(nwo.at[pl.ds(c * ho, ho), :], blk(gwo, me_k, ho, c), 2 * r + 1, to)]

    def landed(r, cc, base):
        pk = chips[r][2]
        return [rc(blk(gwi, pk, hi, cc), blk(gwi, pk, hi, cc), base + 2 * r, sibling),
                rc(blk(gwo, pk, ho, cc), blk(gwo, pk, ho, cc), base + 2 * r + 1, sibling)]

    def local():
        return [pltpu.make_async_copy(nwi, gwi.at[me_k], loc_sems.at[0]),
                pltpu.make_async_copy(nwo, gwo.at[me_k], loc_sems.at[1])]

    @pl.when(step == 0)
    def _():
        for cp in local():
            cp.start()
        for r in range(3):
            for cp in ici(r):
                cp.start()

    @pl.when(step == (3 * nt) // 4)
    def _():
        for r in range(3):
            for got, fwd in zip(landed(r, c, 0), landed(r, c, 6)):
                got.wait_recv()
                fwd.start()

    @pl.when(step == nt - 1)
    def _():
        for r in range(3):
            for got in landed(r, 1 - c, 6):
                got.wait_recv()
        for r in range(3):
            for cp in ici(r) + landed(r, c, 6):
                cp.wait_send()
        for cp in local():
            cp.wait()


def _fwd_layer(larr, x, wi, bin_, caw, cbw, s256, seg, pw, wm, sb, wo, v1024, *, tile, nxt=None, target=None):
    assert nxt is None or target is None
    S = x.shape[0]
    T = tile
    nt = S // T
    alpha = float((2.0 * 4) ** 0.25)
    n_in = 13 + (2 if nxt is not None else 0) + (1 if target is not None else 0)
    n_out = 6 + (2 if nxt is not None else 0) + (1 if target is not None else 0)

    def body(*refs):
        l_ref = refs[0]
        (x_ref, wi_ref, bin_ref, caw_ref, cbw_ref, s256_ref, seg_ref, pw_ref, wm_ref, sb_ref, wo_ref,
         v1024_ref) = refs[1:13]
        y_ref, xb_ref, h_ref, aux_ref, mix_ref, z_ref = refs[n_in:n_in + 6]
        abuf, bbuf, cbuf, wm_s, shf = refs[n_in + n_out:n_in + n_out + 5]
        i = pl.program_id(0)
        if nxt is not None:
            _gather_next(i, nt, refs[13].at[l_ref[0] + 1], refs[14].at[l_ref[0] + 1], refs[n_in + 6], refs[n_in + 7],
                         *refs[n_in + n_out + 5:])

        @pl.when(i == 0)
        def _():
            abuf[0:HALO_A, :] = jnp.zeros((HALO_A, GROUP), F32)
            bbuf[0:HALO_B, :] = jnp.zeros((HALO_B, GROUP), F32)
            cbuf[0:HALO_C, :] = jnp.zeros((HALO_C, GROUP), F32)
            _sgu_masks(wm_ref, None, wm_s, None)

        x = x_ref[...]
        xb = x.astype(BF16)
        xb_ref[...] = xb
        for k in range(N_CHIPS):
            h_ref[:, COLS * k:COLS * (k + 1)] = _dot(xb, wi_ref[k]) + bin_ref[:, COLS * k:COLS * (k + 1)]

        def hs(j):
            return h_ref[:, GROUP * j:GROUP * (j + 1)]

        abuf[HALO_A:HALO_A + T, :] = hs(0) * _sig(hs(1))
        span = T + HALO_A - 8
        for p in range(1, 8):
            shf[p - 1, :, :] = abuf[p:p + span, :]
        for r0 in range(0, T, ROWS):
            acc = None
            for k in range(KA):
                off = HALO_A - (KA - 1) + k
                p, q8 = off % 8, off - off % 8
                win = abuf[r0 + q8:r0 + q8 + ROWS, :] if p == 0 else shf[p - 1, r0 + q8:r0 + q8 + ROWS, :]
                term = caw_ref[k:k + 1, :] * win
                acc = term if acc is None else acc + term
            aux_ref[r0:r0 + ROWS, 0:GROUP] = acc + s256_ref[0:1, :]
        abuf[0:HALO_A, :] = abuf[T:T + HALO_A, :]
        a1 = aux_ref[:, 0:GROUP]
        segm = seg_ref[...]
        cen = a1 - _segdot(a1, segm)
        var = _segdot(cen * cen, segm)
        a2 = cen * lax.rsqrt(var + LN_EPS) * s256_ref[1:2, :] + s256_ref[2:3, :]
        az = hs(2)
        mix_ref[:, 0:GROUP] = (a2 * _sig(a2) * (az * _sig(az))).astype(BF16)

        bbuf[HALO_B:HALO_B + T, :] = hs(4) * hs(5)
        for r0 in range(0, T, ROWS):
            acc = None
            for k in range(KB):
                off = HALO_B - (KB - 1) + k + r0
                term = cbw_ref[k:k + 1, :] * bbuf[off:off + ROWS, :]
                acc = term if acc is None else acc + term
            aux_ref[r0:r0 + ROWS, GROUP:2 * GROUP] = acc
        bbuf[0:HALO_B, :] = bbuf[T:T + HALO_B, :]
        bz = hs(6)
        mix_ref[:, GROUP:2 * GROUP] = (hs(3) * aux_ref[:, GROUP:2 * GROUP] * (bz * _sig(bz))).astype(BF16)

        ch = hs(7)
        cbuf[HALO_C:HALO_C + T, :] = ch
        hi_lane = (lax.broadcasted_iota(jnp.int32, (1, 128), 1) // HEAD) == 1
        for r0 in range(0, T, ROWS):
            def win(col, j0, j1):
                s = None
                for j in range(j0, j1):
                    off = HALO_C - j + r0
                    term = cbuf[off:off + ROWS, 128 * col:128 * (col + 1)]
                    s = term if s is None else s + term
                return s
            w0 = win(0, 0, 2) + jnp.where(hi_lane, win(0, 2, 4), 0.0)
            w1 = win(1, 0, 8) + jnp.where(hi_lane, win(1, 8, 16), 0.0)
            aux_ref[r0:r0 + ROWS, 2 * GROUP:2 * GROUP + 128] = w0
            aux_ref[r0:r0 + ROWS, 2 * GROUP + 128:3 * GROUP] = w1
        cbuf[0:HALO_C, :] = cbuf[T:T + HALO_C, :]
        pooled = aux_ref[:, 2 * GROUP:3 * GROUP] / _pool_cnt(i, T) - ch
        aux_ref[:, 2 * GROUP:3 * GROUP] = pooled
        q = _dot(pooled.astype(BF16), pw_ref[...])
        cz = hs(8)
        mix_ref[:, 2 * GROUP:3 * GROUP] = (q * s256_ref[3:4, :] * (cz * _sig(cz))).astype(BF16)

        dv = hs(10)
        cen = dv - _rowmean(dv)
        var = _rowmean(cen * cen)
        v = cen * lax.rsqrt(var + LN_EPS) * s256_ref[4:5, :] + s256_ref[5:6, :]
        sps = []
        for n in range(T // SGU_BLOCK):
            vb = v[n * SGU_BLOCK:(n + 1) * SGU_BLOCK, :]
            sps.append(_dot(wm_s[...], _vstack(vb)) + sb_ref[...])
        sp = jnp.concatenate(sps, axis=0)
        dz = hs(11)
        mix_ref[:, 3 * GROUP:4 * GROUP] = (hs(9) * sp * (dz * _sig(dz))).astype(BF16)

        out = v1024_ref[0:1, :]
        for k in range(N_CHIPS):
            out = out + _dot(mix_ref[:, GROUP * k:GROUP * (k + 1)], wo_ref[k])
        z = alpha * x + out
        z_ref[...] = z
        cen = z - _rowmean(z)
        var = _rowmean(cen * cen)
        y = cen * lax.rsqrt(var + LN_EPS) * v1024_ref[1:2, :] + v1024_ref[2:3, :]
        if target is None:
            y_ref[...] = y
        else:
            t_ref, loss_ref = refs[13], refs[n_in + 6]

            @pl.when(i == 0)
            def _():
                loss_ref[...] = jnp.zeros_like(loss_ref)
            err = y - t_ref[...]
            y_ref[...] = err * (1.0 / D_MODEL)
            loss_ref[...] += jnp.sum(_colsum(err * err), axis=1, keepdims=True) * (0.5 / D_MODEL)

    def rows(width):
        return pl.BlockSpec((T, width), lambda i, l: (i, 0))

    consts = (wi, bin_, caw, cbw, s256, seg, pw, wm, sb, wo, v1024)
    in_specs = [rows(D_MODEL)] + [_whole(a) if a is wi or a is seg or a is wo else _of_layer(a) for a in consts]
    out_specs = [rows(D_MODEL), rows(D_MODEL), rows(IN_WIDTH), rows(3 * GROUP), rows(D_MODEL), rows(D_MODEL)]
    out_shape = [jax.ShapeDtypeStruct((S, D_MODEL), F32), jax.ShapeDtypeStruct((S, D_MODEL), BF16),
                 jax.ShapeDtypeStruct((S, IN_WIDTH), F32), jax.ShapeDtypeStruct((S, 3 * GROUP), F32),
                 jax.ShapeDtypeStruct((S, D_MODEL), BF16), jax.ShapeDtypeStruct((S, D_MODEL), F32)]
    scratch = [pltpu.VMEM((T + HALO_A, GROUP), F32), pltpu.VMEM((T + HALO_B, GROUP), F32),
               pltpu.VMEM((T + HALO_C, GROUP), F32), pltpu.VMEM((SGU_BLOCK, 4 * SGU_BLOCK), BF16),
               pltpu.VMEM((7, T + HALO_A - 8, GROUP), F32)]
    extra = ()
    if nxt is not None:
        extra = tuple(nxt)
        in_specs += [ANY, ANY]
        out_specs += [ANY, ANY]
        out_shape += [jax.ShapeDtypeStruct((N_CHIPS, D_MODEL, COLS), BF16),
                      jax.ShapeDtypeStruct((N_CHIPS, GROUP, D_MODEL), BF16)]
        scratch += [pltpu.SemaphoreType.DMA((12,)), pltpu.SemaphoreType.DMA((12,)), pltpu.SemaphoreType.DMA((2,))]
    if target is not None:
        extra = (target,)
        in_specs += [rows(D_MODEL)]
        out_specs += [pl.BlockSpec((8, 128), lambda i, l: (0, 0))]
        out_shape += [jax.ShapeDtypeStruct((8, 128), F32)]
    grid_spec = pltpu.PrefetchScalarGridSpec(num_scalar_prefetch=1, grid=(nt,), in_specs=in_specs,
                                             out_specs=out_specs, scratch_shapes=scratch)
    return pl.pallas_call(
        body, name=("fwd_layer_loss" if target is not None else "fwd_layer") if nxt is None else "fwd_layer_gather",
        grid_spec=grid_spec, out_shape=out_shape,
        compiler_params=_vmem_params(dimension_semantics=("arbitrary",), has_side_effects=nxt is not None),
    )(larr, x, *consts, *extra)


ROW_CBW = 8
ROW_CAW = 16
ROW_LOSS = 7
ROW_PW = 48
ROW_LNG = 112
ROW_LNB = 116
ROW_BOUT = 120
ROW_BIN = 124
ROW_WC = 136
ROW_SB = 392
SM_ROWS = 400
N_DEV = 8


def _exchange_comm(start, finish, l, p_i, p_o, sm, r_i, r_o, r_sm, send_sems, recv_sems, loc_sem):
    x, y, c = _place()
    me = 4 * x + 2 * y + c
    chips = _other_chips(x, y)

    def rc(src, dst, sem, to):
        return pltpu.make_async_remote_copy(src_ref=src, dst_ref=dst, send_sem=send_sems.at[sem],
                                            recv_sem=recv_sems.at[sem], device_id=to, device_id_type=MESH)

    def big(r):
        px, py, pk = chips[r]
        to = (px, py, c)
        return [rc(p_i.at[l, pk], r_i.at[r, l], 2 * r, to), rc(p_o.at[l, pk], r_o.at[r, l], 2 * r + 1, to)]

    def peer(rel):
        px = 1 - x if rel & 4 else x
        py = 1 - y if rel & 2 else y
        pc = 1 - c if rel & 1 else c
        return (px, py, pc), 4 * px + 2 * py + pc

    def small_out(rel):
        to, _ = peer(rel)
        return rc(sm, r_sm.at[me], 5 + rel, to)

    def small_in(rel):
        to, idx = peer(rel)
        return rc(sm, r_sm.at[idx], 5 + rel, to)

    def local():
        return pltpu.make_async_copy(sm, r_sm.at[me], loc_sem.at[0])

    with_big, with_small = p_i is not None, sm is not None

    @pl.when(start)
    def _():
        if with_small:
            local().start()
        if with_big:
            for r in range(3):
                for cp in big(r):
                    cp.start()
        if with_small:
            for rel in range(1, N_DEV):
                small_out(rel).start()

    @pl.when(finish)
    def _():
        if with_big:
            for r in range(3):
                for cp in big(r):
                    cp.wait()
        if with_small:
            for rel in range(1, N_DEV):
                small_in(rel).wait_recv()
                small_out(rel).wait_send()
            local().wait()


RC = 32
RC_WIDE = 16
ACC_ROWS = 136


def _rsum8(v):
    r = v[0:8]
    for j in range(1, v.shape[0] // 8):
        r = r + v[8 * j:8 * j + 8]
    return r


def _bwd_layer(larr, dy, z, h, aux, wi, caw, cbw, s256, seg, pw, wm, wmt, sb, wo, v1024, e4, *, tile, exch=None):
    S = dy.shape[0]
    T = tile
    nt = S // T
    nblk = T // SGU_BLOCK
    alpha = float((2.0 * 4) ** 0.25)
    n_in = 17 + (5 if exch is not None else 0)
    n_out = 4 + (3 if exch is not None else 0)
    slab = pltpu.VMEM((T, GROUP), F32)
    scratch = dict(
        dbuf=pltpu.VMEM((T + HALO_A, GROUP), F32), ebuf=pltpu.VMEM((T + HALO_B, GROUP), F32),
        fbuf=pltpu.VMEM((T + HALO_C, GROUP), F32), sh=pltpu.VMEM((7, T + HALO_A - 8, GROUP), F32),
        wm_s=pltpu.VMEM((SGU_BLOCK, 4 * SGU_BLOCK), BF16), wmt_s=pltpu.VMEM((4 * SGU_BLOCK, SGU_BLOCK), BF16),
        dsp_acc=pltpu.VMEM((SGU_BLOCK, GROUP), F32), pw_acc=pltpu.VMEM((GROUP, GROUP), F32),
        acc_s=pltpu.VMEM((8 * ACC_ROWS, GROUP), F32), acc_w=pltpu.VMEM((24, D_MODEL), F32),
        dmix_s=pltpu.VMEM((T, D_MODEL), F32), vst_s=pltpu.VMEM((nblk, 4 * SGU_BLOCK, GROUP), BF16),
        dq_s=pltpu.VMEM((T, GROUP), BF16), dxt_s=pltpu.VMEM((D_MODEL, T), F32),
        mean_s=slab, t1_s=slab, t2_s=slab, q_s=slab, xv_s=slab, rv_s=slab, v_s=slab, sp_s=slab, a0_s=slab, sg_s=slab,
        xh_s=slab, ra_s=slab, ub_s=slab, dsp_s=slab, m1_s=slab, m2_s=slab, dpool_s=slab, dvd_s=slab, u_s=slab,
        du_s=slab, cw_s=slab)
    names = list(scratch)

    def body(*refs):
        (dy_ref, z_ref, h_ref, aux_ref, wi_ref, caw_ref, cbw_ref, s256_ref, seg_ref, pw_ref, wm_ref, wmt_ref,
         sb_ref, wo_ref, v1024_ref, e4_ref) = refs[1:17]
        dx_ref, dhb_ref, dzb_ref, osm_ref = refs[n_in:n_in + 4]
        k0 = n_in + n_out
        sc = dict(zip(names, refs[k0:k0 + len(names)]))
        dbuf, ebuf, fbuf, sh = sc["dbuf"], sc["ebuf"], sc["fbuf"], sc["sh"]
        wm_s, wmt_s, dsp_acc, pw_acc, acc_s, acc_w = (sc[n] for n in ("wm_s", "wmt_s", "dsp_acc", "pw_acc", "acc_s",
                                                                        "acc_w"))
        dmix_s, vst_s, dq_s = sc["dmix_s"], sc["vst_s"], sc["dq_s"]
        i = pl.program_id(0)
        tile_idx = nt - 1 - i
        if exch is not None:
            p_i, p_o, sm = refs[17:20]
            r_i, r_o, r_sm = refs[n_in + 4:n_in + 7]
            _exchange_comm(i == 0, i == nt - 1, refs[0][0] + 1, p_i, p_o, sm, r_i, r_o, r_sm, *refs[k0 + len(names):])

        @pl.when(i == 0)
        def _():
            dbuf[T:T + HALO_A, :] = jnp.zeros((HALO_A, GROUP), F32)
            ebuf[T:T + HALO_B, :] = jnp.zeros((HALO_B, GROUP), F32)
            fbuf[T:T + HALO_C, :] = jnp.zeros((HALO_C, GROUP), F32)
            _sgu_masks(wm_ref, wmt_ref, wm_s, wmt_s)
            osm_ref[...] = jnp.zeros_like(osm_ref)
            dsp_acc[...] = jnp.zeros_like(dsp_acc)
            pw_acc[...] = jnp.zeros_like(pw_acc)
            acc_s[...] = jnp.zeros_like(acc_s)
            acc_w[...] = jnp.zeros_like(acc_w)

        def chunks(rc, fn):
            for c in range(T // rc):
                fn(pl.ds(c * rc, rc))

        def hs(j, rows):
            return h_ref[rows, GROUP * j:GROUP * (j + 1)]

        def acc_add(row, val):
            acc_s[8 * row:8 * row + 8, :] += _rsum8(val)

        def put_dh(j, rows, val):
            acc_add(ROW_BIN + j, val)
            dhb_ref[rows, GROUP * j:GROUP * (j + 1)] = val.astype(BF16)

        def dsilu(v, s):
            return s * (1.0 + v * (1.0 - s))

        def vec(r):
            return s256_ref[r:r + 1, :]

        def ln_bwd(rows):
            dyc = dy_ref[rows, :]
            zc = z_ref[rows, :]
            cen = zc - _rowmean(zc)
            rstd = lax.rsqrt(_rowmean(cen * cen) + LN_EPS)
            xhat = cen * rstd
            acc_w[0:8, :] += _rsum8(dyc * xhat)
            acc_w[8:16, :] += _rsum8(dyc)
            gdy = dyc * v1024_ref[1:2, :]
            dz = rstd * (gdy - _rowmean(gdy) - xhat * _rowmean(gdy * xhat))
            acc_w[16:24, :] += _rsum8(dz)
            dzb_ref[rows, :] = dz.astype(BF16)
            dx_ref[rows, :] = alpha * dz
        chunks(RC_WIDE, ln_bwd)

        segm = seg_ref[...]
        dzb = dzb_ref[...]
        for k in range(N_CHIPS):
            dmix_s[:, GROUP * k:GROUP * (k + 1)] = _dot_nt(dzb, wo_ref[k])
        sc["mean_s"][...] = _segdot(aux_ref[:, 0:GROUP], segm)
        pooled_b = aux_ref[:, 2 * GROUP:3 * GROUP].astype(BF16)
        sc["q_s"][...] = _dot(pooled_b, pw_ref[...])

        def centre(rows):
            cen = aux_ref[rows, 0:GROUP] - sc["mean_s"][rows, :]
            sc["t1_s"][rows, :] = cen * cen
            dv_in = hs(10, rows)
            cen_v = dv_in - _rowmean(dv_in)
            rstd_v = lax.rsqrt(_rowmean(cen_v * cen_v) + LN_EPS)
            xv = cen_v * rstd_v
            sc["xv_s"][rows, :] = xv
            sc["rv_s"][rows, :] = jnp.broadcast_to(rstd_v, xv.shape)
            sc["v_s"][rows, :] = xv * vec(4) + vec(5)
        chunks(RC, centre)

        sc["t2_s"][...] = _segdot(sc["t1_s"][...], segm)
        for n in range(nblk):
            blk = slice(n * SGU_BLOCK, (n + 1) * SGU_BLOCK)
            vst_s[n] = _vstack(sc["v_s"][blk, :])
            sc["sp_s"][blk, :] = _dot(wm_s[...], vst_s[n]) + sb_ref[...]

        def mixers(rows):
            a_val, a_glu, a_z = hs(0, rows), hs(1, rows), hs(2, rows)
            sg = _sig(a_glu)
            sc["a0_s"][rows, :] = a_val * sg
            sc["sg_s"][rows, :] = sg
            rstd_a = lax.rsqrt(sc["t2_s"][rows, :] + LN_EPS)
            xh = (aux_ref[rows, 0:GROUP] - sc["mean_s"][rows, :]) * rstd_a
            a2 = xh * vec(1) + vec(2)
            s2 = _sig(a2)
            sz = _sig(a_z)
            dya = dmix_s[rows, 0:GROUP]
            put_dh(2, rows, dya * (a2 * s2) * dsilu(a_z, sz))
            d_a2 = dya * (a_z * sz) * dsilu(a2, s2)
            acc_add(1, d_a2 * xh)
            acc_add(2, d_a2)
            gd = d_a2 * vec(1)
            sc["t1_s"][rows, :] = gd
            sc["t2_s"][rows, :] = gd * xh
            sc["xh_s"][rows, :] = xh
            sc["ra_s"][rows, :] = rstd_a
            b_b, b_c, b_h, b_z = hs(3, rows), hs(4, rows), hs(5, rows), hs(6, rows)
            cb = aux_ref[rows, GROUP:2 * GROUP]
            sz = _sig(b_z)
            dyb = dmix_s[rows, GROUP:2 * GROUP]
            put_dh(3, rows, dyb * cb * (b_z * sz))
            put_dh(6, rows, dyb * b_b * cb * dsilu(b_z, sz))
            ebuf[rows, :] = dyb * b_b * (b_z * sz)
            sc["ub_s"][rows, :] = b_c * b_h
            c_z = hs(8, rows)
            q = sc["q_s"][rows, :]
            sz = _sig(c_z)
            dyc = dmix_s[rows, 2 * GROUP:3 * GROUP]
            acc_add(3, dyc * q * (c_z * sz))
            put_dh(8, rows, dyc * q * vec(3) * dsilu(c_z, sz))
            dq_s[rows, :] = (dyc * vec(3) * (c_z * sz)).astype(BF16)
            d_u, d_z = hs(9, rows), hs(11, rows)
            sp = sc["sp_s"][rows, :]
            sz = _sig(d_z)
            dyd = dmix_s[rows, 3 * GROUP:4 * GROUP]
            put_dh(9, rows, dyd * sp * (d_z * sz))
            put_dh(11, rows, dyd * d_u * sp * dsilu(d_z, sz))
            sc["dsp_s"][rows, :] = dyd * d_u * (d_z * sz)
        chunks(RC, mixers)

        sc["m1_s"][...] = _segdot(sc["t1_s"][...], segm)
        sc["m2_s"][...] = _segdot(sc["t2_s"][...], segm)
        d_q = dq_s[...]
        pw_acc[...] += _dot_tn(pooled_b, d_q)
        sc["dpool_s"][...] = _dot_nt(d_q, pw_ref[...])
        grp = _lane_group(GROUP)
        for n in range(nblk):
            blk = slice(n * SGU_BLOCK, (n + 1) * SGU_BLOCK)
            dspb = sc["dsp_s"][blk, :]
            dsp_acc[...] += dspb
            dspb16 = dspb.astype(BF16)
            dvst = _dot(wmt_s[...], dspb16)
            dvb = None
            for hh in range(4):
                part = jnp.where(grp == hh, dvst[hh * SGU_BLOCK:(hh + 1) * SGU_BLOCK, :], 0.0)
                dvb = part if dvb is None else dvb + part
            sc["dvd_s"][blk, :] = dvb
            dwc = _dot_nt(dspb16, vst_s[n])
            osm_ref[ROW_WC:ROW_WC + SGU_BLOCK, :] += dwc[:, 0:GROUP]
            osm_ref[ROW_WC + SGU_BLOCK:ROW_WC + 2 * SGU_BLOCK, :] += dwc[:, GROUP:2 * GROUP]

        def ln_sums(rows):
            xh = sc["xh_s"][rows, :]
            d_a1 = sc["ra_s"][rows, :] * (sc["t1_s"][rows, :] - sc["m1_s"][rows, :] - xh * sc["m2_s"][rows, :])
            acc_add(0, d_a1)
            dbuf[rows, :] = d_a1
            pos = tile_idx * T + rows.start + lax.broadcasted_iota(jnp.int32, (RC, GROUP), 0) + 1
            lane = lax.broadcasted_iota(jnp.int32, (RC, GROUP), 1) // HEAD
            win = jnp.where(lane == 0, 2, jnp.where(lane == 1, 4, jnp.where(lane == 2, 8, 16)))
            fbuf[rows, :] = sc["dpool_s"][rows, :] / jnp.minimum(pos, win).astype(F32)
            d_v = sc["dvd_s"][rows, :]
            xv = sc["xv_s"][rows, :]
            acc_add(4, d_v * xv)
            acc_add(5, d_v)
            gd = d_v * vec(4)
            put_dh(10, rows, sc["rv_s"][rows, :] * (gd - _rowmean(gd) - xv * _rowmean(gd * xv)))
        chunks(RC, ln_sums)

        span = T + HALO_A - 8
        for p in range(1, 8):
            sh[p - 1, :, :] = dbuf[p:p + span, :]

        for r0 in range(0, T, ROWS):
            uc = sc["ub_s"][r0:r0 + ROWS, :]
            acc = None
            for k in range(KB):
                off = (KB - 1) - k + r0
                w = ebuf[off:off + ROWS, :]
                term = cbw_ref[k:k + 1, :] * w
                acc = term if acc is None else acc + term
                acc_add(ROW_CBW + k, uc * w)
            sc["du_s"][r0:r0 + ROWS, :] = acc
        ebuf[T:T + HALO_B, :] = ebuf[0:HALO_B, :]

        hi_lane = (lax.broadcasted_iota(jnp.int32, (1, 128), 1) // HEAD) == 1
        for r0 in range(0, T, ROWS):
            def win(col, j0, j1):
                s = None
                for j in range(j0, j1):
                    term = fbuf[r0 + j:r0 + j + ROWS, 128 * col:128 * (col + 1)]
                    s = term if s is None else s + term
                return s
            sc["cw_s"][r0:r0 + ROWS, 0:128] = win(0, 0, 2) + jnp.where(hi_lane, win(0, 2, 4), 0.0)
            sc["cw_s"][r0:r0 + ROWS, 128:256] = win(1, 0, 8) + jnp.where(hi_lane, win(1, 8, 16), 0.0)
        fbuf[T:T + HALO_C, :] = fbuf[0:HALO_C, :]

        def rest_bc(rows):
            d_u = sc["du_s"][rows, :]
            put_dh(4, rows, d_u * hs(5, rows))
            put_dh(5, rows, d_u * hs(4, rows))
            put_dh(7, rows, sc["cw_s"][rows, :] - sc["dpool_s"][rows, :])
        chunks(RC, rest_bc)

        dxt_s = sc["dxt_s"]

        def dx_term(k):
            term = _dot_nt(wi_ref[k], dhb_ref[:, COLS * k:COLS * (k + 1)])
            if k == 1:
                dxt_s[...] = term
            else:
                dxt_s[...] += term

        def conv_a(rows):
            a0c = sc["a0_s"][rows, :]
            acc = None
            for k in range(KA):
                off = (KA - 1) - k
                p, q8 = off % 8, off - off % 8
                w = dbuf[pl.ds(rows.start + q8, RC), :] if p == 0 else sh[p - 1, pl.ds(rows.start + q8, RC), :]
                term = caw_ref[k:k + 1, :] * w
                acc = term if acc is None else acc + term
                acc_add(ROW_CAW + k, a0c * w)
            sc["u_s"][rows, :] = acc
        n_chunks = T // RC
        after = {(n_chunks * j) // 3: j + 1 for j in range(3)}
        for c in range(n_chunks):
            conv_a(pl.ds(c * RC, RC))
            if c in after:
                dx_term(after[c])
        dbuf[T:T + HALO_A, :] = dbuf[0:HALO_A, :]

        def rest_a(rows):
            d_a0 = sc["u_s"][rows, :]
            sg = sc["sg_s"][rows, :]
            put_dh(0, rows, d_a0 * sg)
            put_dh(1, rows, d_a0 * hs(0, rows) * sg * (1.0 - sg))
        chunks(RC, rest_a)
        dx_term(0)
        dx_ref[...] += dxt_s[...].T

        @pl.when(i == nt - 1)
        def _():
            for row in list(range(6)) + list(range(ROW_CBW, ROW_CBW + KB)) + list(range(ROW_CAW, ROW_CAW + KA)) + list(
                    range(ROW_BIN, ROW_BIN + N_SLICES)):
                osm_ref[row:row + 1, :] = _colsum(acc_s[8 * row:8 * row + 8, :])
            for j, row in enumerate((ROW_LNG, ROW_LNB, ROW_BOUT)):
                cs = _colsum(acc_w[8 * j:8 * j + 8, :])
                for q in range(D_MODEL // GROUP):
                    osm_ref[row + q:row + q + 1, :] = cs[:, GROUP * q:GROUP * (q + 1)]
            r = lax.broadcasted_iota(jnp.int32, (SGU_BLOCK, GROUP), 0) // CHUNK
            c = (lax.broadcasted_iota(jnp.int32, (SGU_BLOCK, GROUP), 1) % SGU_BLOCK) // CHUNK
            for half in range(2):
                rows_ = slice(ROW_WC + half * SGU_BLOCK, ROW_WC + (half + 1) * SGU_BLOCK)
                osm_ref[rows_, :] = jnp.where(c <= r, osm_ref[rows_, :], 0.0)
            sb_t = _segdot(dsp_acc[...], e4_ref[...]).T
            osm_ref[ROW_SB:ROW_SB + 8, 0:SGU_BLOCK] = sb_t[0:8, :]
            for g in range(4):
                osm_ref[ROW_PW:ROW_PW + HEAD, HEAD * g:HEAD * (g + 1)] = (
                    pw_acc[HEAD * g:HEAD * (g + 1), HEAD * g:HEAD * (g + 1)])

    def rows(width):
        return pl.BlockSpec((T, width), lambda i, l: (nt - 1 - i, 0))

    consts = (wi, caw, cbw, s256, seg, pw, wm, wmt, sb, wo, v1024, e4)
    unstacked = (wi, seg, wo, e4)
    in_specs = [rows(D_MODEL), rows(D_MODEL), rows(IN_WIDTH), rows(3 * GROUP)] + [
        _whole(a) if any(a is u for u in unstacked) else _of_layer(a) for a in consts]
    out_specs = [rows(D_MODEL), rows(IN_WIDTH), rows(D_MODEL), pl.BlockSpec((SM_ROWS, GROUP), lambda i, l: (0, 0))]
    out_shape = [jax.ShapeDtypeStruct((S, D_MODEL), F32), jax.ShapeDtypeStruct((S, IN_WIDTH), BF16),
                 jax.ShapeDtypeStruct((S, D_MODEL), BF16), jax.ShapeDtypeStruct((SM_ROWS, GROUP), F32)]
    scratch_shapes = list(scratch.values())
    extra, aliases = (), {}
    if exch is not None:
        extra = tuple(exch)
        r_i, r_o = exch[3], exch[4]
        in_specs += [ANY] * 5
        out_specs += [ANY] * 3
        out_shape += [jax.ShapeDtypeStruct(r_i.shape, r_i.dtype), jax.ShapeDtypeStruct(r_o.shape, r_o.dtype),
                      jax.ShapeDtypeStruct((N_DEV, SM_ROWS, GROUP), F32)]
        scratch_shapes += [pltpu.SemaphoreType.DMA((13,)), pltpu.SemaphoreType.DMA((13,)),
                           pltpu.SemaphoreType.DMA((1,))]
        aliases = {20: 4, 21: 5}
    grid_spec = pltpu.PrefetchScalarGridSpec(num_scalar_prefetch=1, grid=(nt,), in_specs=in_specs,
                                             out_specs=out_specs, scratch_shapes=scratch_shapes)
    return pl.pallas_call(
        body, name="bwd_layer" if exch is None else "bwd_layer_exchange",
        grid_spec=grid_spec, out_shape=out_shape, input_output_aliases=aliases,
        compiler_params=_vmem_params(dimension_semantics=("arbitrary",), has_side_effects=exch is not None),
    )(larr, dy, z, h, aux, *consts, *extra)


def _dw_in(layer, xb, dhb, slab, slab16, *, tk, small=None):
    S = xb.shape[0]
    ns = S // tk

    def body(*refs):
        l_ref, a_ref, b_ref = refs[0:3]
        o_ref, o16_ref = refs[n_in:n_in + 2]
        if small is not None:
            first = (pl.program_id(0) == 0) & (pl.program_id(1) == 0)
            last = (pl.program_id(0) == N_CHIPS - 1) & (pl.program_id(1) == ns - 1)
            _exchange_comm(first, last, None, None, None, refs[5], None, None, refs[n_in + 2], *refs[n_in + 3:])

        @pl.when(pl.program_id(1) == 0)
        def _():
            o_ref[...] = jnp.zeros_like(o_ref)
        o_ref[...] += _dot_tn(a_ref[...], b_ref[...])

        @pl.when(pl.program_id(1) == ns - 1)
        def _():
            o16_ref[...] = o_ref[...].astype(BF16)

    o_spec = pl.BlockSpec((None, None, D_MODEL, COLS), lambda j, s, l: (l[0], j, 0, 0))
    in_specs = [pl.BlockSpec((tk, D_MODEL), lambda j, s, l: (s, 0)), pl.BlockSpec((tk, COLS), lambda j, s, l: (s, j)),
                ANY, ANY]
    out_specs = [o_spec, o_spec]
    out_shape = [jax.ShapeDtypeStruct(slab.shape, F32), jax.ShapeDtypeStruct(slab.shape, BF16)]
    scratch, extra = [], ()
    if small is not None:
        extra = (small,)
        in_specs += [ANY]
        out_specs += [ANY]
        out_shape += [jax.ShapeDtypeStruct((N_DEV, SM_ROWS, GROUP), F32)]
        scratch = [pltpu.SemaphoreType.DMA((13,)), pltpu.SemaphoreType.DMA((13,)), pltpu.SemaphoreType.DMA((1,))]
    n_in = 5 + len(extra)
    grid_spec = pltpu.PrefetchScalarGridSpec(
        num_scalar_prefetch=1, grid=(N_CHIPS, ns), in_specs=in_specs, out_specs=out_specs, scratch_shapes=scratch)
    return pl.pallas_call(
        body, name="dw_in" if small is None else "dw_in_exchange", grid_spec=grid_spec, out_shape=out_shape,
        input_output_aliases={3: 0, 4: 1},
        compiler_params=_vmem_params(dimension_semantics=("arbitrary", "arbitrary"), has_side_effects=small is not None),
    )(layer, xb, dhb, slab, slab16, *extra)


def _dw_out(layer, mixb, dzb, slab, slab16, *, tk):
    S = mixb.shape[0]
    ns = S // tk

    def body(l_ref, a_ref, b_ref, slab_ref, slab16_ref, o_ref, o16_ref):
        del l_ref, slab_ref, slab16_ref

        @pl.when(pl.program_id(0) == 0)
        def _():
            o_ref[...] = jnp.zeros_like(o_ref)
        o_ref[...] += _dot_tn(a_ref[...], b_ref[...]).reshape(N_CHIPS, GROUP, D_MODEL)

        @pl.when(pl.program_id(0) == ns - 1)
        def _():
            o16_ref[...] = o_ref[...].astype(BF16)

    o_spec = pl.BlockSpec((None, N_CHIPS, GROUP, D_MODEL), lambda s, l: (l[0], 0, 0, 0))
    grid_spec = pltpu.PrefetchScalarGridSpec(
        num_scalar_prefetch=1, grid=(ns,),
        in_specs=[pl.BlockSpec((tk, D_MODEL), lambda s, l: (s, 0)), pl.BlockSpec((tk, D_MODEL), lambda s, l: (s, 0)),
                  ANY, ANY],
        out_specs=[o_spec, o_spec])
    return pl.pallas_call(
        body, name="dw_out", grid_spec=grid_spec,
        out_shape=[jax.ShapeDtypeStruct(slab.shape, F32), jax.ShapeDtypeStruct(slab.shape, BF16)],
        input_output_aliases={3: 0, 4: 1},
        compiler_params=_vmem_params(dimension_semantics=("arbitrary",)),
    )(layer, mixb, dzb, slab, slab16)


def _adamw_math(w, g, m, v):
    nm = ADAM_B1 * m + (1.0 - ADAM_B1) * g
    nv = ADAM_B2 * v + (1.0 - ADAM_B2) * (g * g)
    c1 = 1.0 - ADAM_B1 ** ADAM_STEP
    c2 = 1.0 - ADAM_B2 ** ADAM_STEP
    return -ADAM_LR * ((nm / c1) / (jnp.sqrt(nv / c2) + ADAM_EPS) + ADAM_WD * w), nm, nv


def _adamw_small(ws, gs, ms, vs):
    n = len(ws)

    def body(*refs):
        for j in range(n):
            d, nm, nv = _adamw_math(*(refs[k * n + j][...] for k in range(4)))
            refs[4 * n + j][...] = d
            refs[5 * n + j][...] = nm
            refs[6 * n + j][...] = nv

    shapes = [jax.ShapeDtypeStruct(w.shape, F32) for w in ws]
    outs = pl.pallas_call(body, name="adamw_small", out_shape=shapes * 3, compiler_params=_vmem_params())(
        *ws, *gs, *ms, *vs)
    return outs[0:n], outs[n:2 * n], outs[2 * n:3 * n]


def _adamw(w, g, m, v, *, rows_per_step, name, copy_g=False):
    R, C = w.shape
    tr = rows_per_step

    def body(w_ref, g_ref, m_ref, v_ref, d_ref, nm_ref, nv_ref, *g_out):
        g_ = g_ref[...]
        d_ref[...], nm_ref[...], nv_ref[...] = _adamw_math(w_ref[...], g_, m_ref[...], v_ref[...])
        if copy_g:
            g_out[0][...] = g_

    spec = pl.BlockSpec((tr, C), lambda i: (i, 0))
    n_out = 4 if copy_g else 3
    return pl.pallas_call(
        body, name=name, grid=(R // tr,),
        in_specs=[spec] * 4, out_specs=[spec] * n_out,
        out_shape=[jax.ShapeDtypeStruct((R, C), F32)] * n_out,
        compiler_params=_vmem_params(dimension_semantics=("arbitrary",)),
    )(w, g, m, v)


def _gather_weights(wi16, wo16, cw):
    L = wi16.shape[0]
    hi_rows, ho_rows = D_MODEL // 2, GROUP // 2
    n_ici = 2 * L + 1
    n_fwd = 2 * L

    def body(wi_ref, wo_ref, cw_ref, *rest):
        wig = rest[0:L]
        wog = rest[L:2 * L]
        cwg = rest[2 * L]
        send_sems, recv_sems, loc_sems = rest[2 * L + 1:]
        x, y, c = _place()
        me_k = 2 * x + y
        sibling = (x, y, 1 - c)
        chips = _other_chips(x, y)

        def half_i(ref, blk):
            return ref.at[blk, pl.ds(c * hi_rows, hi_rows), :]

        def half_o(ref, blk):
            return ref.at[blk, pl.ds(c * ho_rows, ho_rows), :]

        def other_half_i(ref, blk):
            return ref.at[blk, pl.ds((1 - c) * hi_rows, hi_rows), :]

        def other_half_o(ref, blk):
            return ref.at[blk, pl.ds((1 - c) * ho_rows, ho_rows), :]

        local = []
        for l in range(L):
            local.append(pltpu.make_async_copy(wi_ref.at[l], wig[l].at[me_k], loc_sems.at[2 * l]))
            local.append(pltpu.make_async_copy(wo_ref.at[l], wog[l].at[me_k], loc_sems.at[2 * l + 1]))
        local.append(pltpu.make_async_copy(cw_ref, cwg.at[me_k], loc_sems.at[2 * L]))
        for cp in local:
            cp.start()

        def remote(src, dst, sem, to):
            return pltpu.make_async_remote_copy(src_ref=src, dst_ref=dst, send_sem=send_sems.at[sem],
                                                recv_sem=recv_sems.at[sem], device_id=to, device_id_type=MESH)

        sends = []
        for r, (px, py, _) in enumerate(chips):
            to = (px, py, c)
            for l in range(L):
                sends.append(remote(half_i(wi_ref, l), half_i(wig[l], me_k), r * n_ici + 2 * l, to))
                sends.append(remote(half_o(wo_ref, l), half_o(wog[l], me_k), r * n_ici + 2 * l + 1, to))
            sends.append(remote(cw_ref, cwg.at[me_k], r * n_ici + 2 * L, to))
        for cp in sends:
            cp.start()

        base = 3 * n_ici
        fwds = []
        for r, (px, py, pk) in enumerate(chips):
            for l in range(L):
                remote(half_i(wig[l], pk), half_i(wig[l], pk), r * n_ici + 2 * l, sibling).wait_recv()
                f = remote(half_i(wig[l], pk), half_i(wig[l], pk), base + r * n_fwd + 2 * l, sibling)
                f.start()
                fwds.append(f)
                remote(half_o(wog[l], pk), half_o(wog[l], pk), r * n_ici + 2 * l + 1, sibling).wait_recv()
                f = remote(half_o(wog[l], pk), half_o(wog[l], pk), base + r * n_fwd + 2 * l + 1, sibling)
                f.start()
                fwds.append(f)
            remote(cwg.at[pk], cwg.at[pk], r * n_ici + 2 * L, sibling).wait_recv()
        for r, (px, py, pk) in enumerate(chips):
            for l in range(L):
                remote(other_half_i(wig[l], pk), other_half_i(wig[l], pk), base + r * n_fwd + 2 * l, sibling).wait_recv()
                remote(other_half_o(wog[l], pk), other_half_o(wog[l], pk), base + r * n_fwd + 2 * l + 1, sibling).wait_recv()
        for cp in sends + fwds:
            cp.wait_send()
        for cp in local:
            cp.wait()

    n_sem = 3 * n_ici + 3 * n_fwd
    out_shape = ([jax.ShapeDtypeStruct((N_CHIPS, D_MODEL, COLS), BF16)] * L
                 + [jax.ShapeDtypeStruct((N_CHIPS, GROUP, D_MODEL), BF16)] * L
                 + [jax.ShapeDtypeStruct((N_CHIPS,) + cw.shape, F32)])
    outs = pl.pallas_call(
        body, name="gather_weights",
        in_specs=[ANY, ANY, ANY], out_specs=[ANY] * (2 * L + 1), out_shape=out_shape,
        scratch_shapes=[pltpu.SemaphoreType.DMA((n_sem,)), pltpu.SemaphoreType.DMA((n_sem,)),
                        pltpu.SemaphoreType.DMA((2 * L + 1,))],
        compiler_params=pltpu.CompilerParams(has_side_effects=True),
    )(wi16, wo16, cw)
    return outs[0:L], outs[L:2 * L], outs[2 * L]


def _swap_halves(l_arr, gwi, gwo, ri, ro):
    hi_rows, ho_rows = D_MODEL // 2, GROUP // 2

    def body(l_ref, gwi_ref, gwo_ref, ri_in, ro_in, ri_ref, ro_ref, send_sems, recv_sems):
        del ri_in, ro_in
        x, y, c = _place()
        l = l_ref[0]
        sibling = (x, y, 1 - c)
        cps = [
            pltpu.make_async_remote_copy(src_ref=gwi_ref.at[l, :, pl.ds((1 - c) * hi_rows, hi_rows), :],
                                         dst_ref=ri_ref.at[l], send_sem=send_sems.at[0], recv_sem=recv_sems.at[0],
                                         device_id=sibling, device_id_type=MESH),
            pltpu.make_async_remote_copy(src_ref=gwo_ref.at[l, :, pl.ds((1 - c) * ho_rows, ho_rows), :],
                                         dst_ref=ro_ref.at[l], send_sem=send_sems.at[1], recv_sem=recv_sems.at[1],
                                         device_id=sibling, device_id_type=MESH),
        ]
        for cp in cps:
            cp.start()
        for cp in cps:
            cp.wait()

    return pl.pallas_call(
        body, name="swap_halves",
        in_specs=[pl.BlockSpec(memory_space=pltpu.SMEM), ANY, ANY, ANY, ANY], out_specs=[ANY, ANY],
        out_shape=[jax.ShapeDtypeStruct(ri.shape, ri.dtype), jax.ShapeDtypeStruct(ro.shape, ro.dtype)],
        input_output_aliases={3: 0, 4: 1},
        scratch_shapes=[pltpu.SemaphoreType.DMA((2,)), pltpu.SemaphoreType.DMA((2,))],
        compiler_params=pltpu.CompilerParams(has_side_effects=True),
    )(l_arr, gwi, gwo, ri, ro)


def _add_halves(cl_arr, g, r, p, *, rows, cols, tr, name):
    nb = rows // tr

    def body(cl_ref, g_ref, r_ref, p_in, o_ref):
        del cl_ref, p_in
        o_ref[...] = (g_ref[...] + r_ref[...].astype(F32)).astype(o_ref.dtype)

    grid_spec = pltpu.PrefetchScalarGridSpec(
        num_scalar_prefetch=1, grid=(N_CHIPS, nb),
        in_specs=[pl.BlockSpec((None, None, tr, cols), lambda k, i, cl: (cl[1], k, cl[0] * nb + i, 0)),
                  pl.BlockSpec((None, None, tr, cols), lambda k, i, cl: (cl[1], k, i, 0)), ANY],
        out_specs=pl.BlockSpec((None, None, tr, cols), lambda k, i, cl: (cl[1], k, i, 0)))
    return pl.pallas_call(
        body, name=name, grid_spec=grid_spec,
        out_shape=jax.ShapeDtypeStruct(p.shape, p.dtype),
        input_output_aliases={3: 0},
        compiler_params=_vmem_params(dimension_semantics=("arbitrary",) * 2),
    )(cl_arr, g, r, p)


def _exchange_last(l_arr, p_i, p_o, r_i, r_o):
    def body(l_ref, p_i_ref, p_o_ref, ri_in, ro_in, ri_ref, ro_ref, send_sems, recv_sems):
        del ri_in, ro_in
        always = l_ref[0] >= 0
        _exchange_comm(always, always, l_ref[0], p_i_ref, p_o_ref, None, ri_ref, ro_ref, None,
                       send_sems, recv_sems, None)

    return pl.pallas_call(
        body, name="exchange_last",
        in_specs=[pl.BlockSpec(memory_space=pltpu.SMEM)] + [ANY] * 4, out_specs=[ANY] * 2,
        out_shape=[jax.ShapeDtypeStruct(r_i.shape, r_i.dtype), jax.ShapeDtypeStruct(r_o.shape, r_o.dtype)],
        input_output_aliases={3: 0, 4: 1},
        scratch_shapes=[pltpu.SemaphoreType.DMA((13,)), pltpu.SemaphoreType.DMA((13,))],
        compiler_params=pltpu.CompilerParams(has_side_effects=True),
    )(l_arr, p_i, p_o, r_i, r_o)


def _sum_small(r_sms):
    L = len(r_sms)

    def body(*refs):
        o_ref = refs[L]
        for l in range(L):
            acc = refs[l][0]
            for d in range(1, N_DEV):
                acc = acc + refs[l][d]
            o_ref[l] = acc

    return pl.pallas_call(
        body, name="sum_small",
        out_shape=jax.ShapeDtypeStruct((L,) + r_sms[0].shape[1:], F32),
        compiler_params=_vmem_params(),
    )(*r_sms)


def _sum_chunks(kc_arr, p, r, *, rows, cols, tr, name):
    L = p.shape[0]
    nb = rows // tr

    def body(kc_ref, p_ref, r0_ref, r1_ref, r2_ref, o_ref):
        del kc_ref
        f = lambda ref: ref[...].astype(F32)
        o_ref[...] = ((f(p_ref) + f(r0_ref)) + f(r1_ref)) + f(r2_ref)

    def rspec(j):
        return pl.BlockSpec((None, None, tr, cols), lambda l, i, kc, _j=j: (_j, l, i, 0))

    grid_spec = pltpu.PrefetchScalarGridSpec(
        num_scalar_prefetch=1, grid=(L, nb),
        in_specs=[pl.BlockSpec((None, None, tr, cols), lambda l, i, kc: (l, kc[0], i, 0)), rspec(0), rspec(1), rspec(2)],
        out_specs=pl.BlockSpec((None, tr, cols), lambda l, i, kc: (l, kc[1] * nb + i, 0)))
    return pl.pallas_call(
        body, name=name, grid_spec=grid_spec,
        out_shape=jax.ShapeDtypeStruct((L, 2 * rows, cols), F32),
        compiler_params=_vmem_params(dimension_semantics=("arbitrary",) * 2),
    )(kc_arr, p, r, r, r)


def _share_result(gi, go):
    hi_rows, ho_rows = gi.shape[1] // 2, go.shape[1] // 2

    def body(gi_ref, go_ref, oi_ref, oo_ref, send_sems, recv_sems):
        del gi_ref, go_ref
        x, y, c = _place()
        sibling = (x, y, 1 - c)
        cps = []
        for j, (ref, n) in enumerate(((oi_ref, hi_rows), (oo_ref, ho_rows))):
            mine = ref.at[:, pl.ds(c * n, n), :]
            cps.append(pltpu.make_async_remote_copy(src_ref=mine, dst_ref=mine, send_sem=send_sems.at[j],
                                                    recv_sem=recv_sems.at[j], device_id=sibling, device_id_type=MESH))
        for cp in cps:
            cp.start()
        for j, (ref, n) in enumerate(((oi_ref, hi_rows), (oo_ref, ho_rows))):
            theirs = ref.at[:, pl.ds((1 - c) * n, n), :]
            pltpu.make_async_remote_copy(src_ref=theirs, dst_ref=theirs, send_sem=send_sems.at[j],
                                         recv_sem=recv_sems.at[j], device_id=sibling, device_id_type=MESH).wait_recv()
        for cp in cps:
            cp.wait_send()

    return pl.pallas_call(
        body, name="share_result",
        in_specs=[ANY, ANY], out_specs=[ANY, ANY],
        out_shape=[jax.ShapeDtypeStruct(gi.shape, F32), jax.ShapeDtypeStruct(go.shape, F32)],
        input_output_aliases={0: 0, 1: 1},
        scratch_shapes=[pltpu.SemaphoreType.DMA((2,)), pltpu.SemaphoreType.DMA((2,))],
        compiler_params=pltpu.CompilerParams(has_side_effects=True),
    )(gi, go)


WEIGHTS = ("ln_g", "ln_b", "w_in", "b_in", "conv_a_w", "conv_a_b", "norm_a_g", "norm_a_b", "conv_b_w", "pool_w",
           "pool_scale", "sgu_ln_g", "sgu_ln_b", "sgu_w", "sgu_bias", "w_out", "b_out")


def _pad_rows(a, rows):
    return jnp.pad(a, ((0, rows - a.shape[0]), (0, 0)))


def _indicator_consts():
    seg = jnp.where((jnp.arange(GROUP)[:, None] // HEAD) == (jnp.arange(GROUP)[None, :] // HEAD),
                    1.0 / HEAD, 0.0).astype(BF16)
    e4 = ((jnp.arange(GROUP)[:, None] // HEAD) == jnp.arange(128)[None, :]).astype(BF16)
    return seg, e4


def _layer_consts(p, conv_full):
    L = conv_full.shape[0]
    same_head = jnp.eye(4, dtype=F32)[:, None, :, None] > 0

    def rows_to(a, rows):
        return jnp.pad(a, ((0, 0), (0, rows - a.shape[1]), (0, 0)))

    s256 = jnp.stack([p[n] for n in ("conv_a_b", "norm_a_g", "norm_a_b", "pool_scale", "sgu_ln_g", "sgu_ln_b")], axis=1)
    pw = jnp.where(same_head, p["pool_w"][:, :, :, None, :], 0.0).reshape(L, GROUP, GROUP)
    return dict(
        caw=rows_to(conv_full[:, :KA], 32), cbw=rows_to(conv_full[:, KA:], 8), s256=rows_to(s256, 8),
        pw=pw.astype(BF16),
        wm=jnp.transpose(p["sgu_w"], (0, 2, 1, 3)).reshape(L, SGU_BLOCK, 4 * SGU_BLOCK),
        wmt=jnp.transpose(p["sgu_w"], (0, 1, 3, 2)).reshape(L, 4 * SGU_BLOCK, SGU_BLOCK),
        sb=jnp.repeat(jnp.transpose(p["sgu_bias"], (0, 2, 1)), HEAD, axis=2),
        v1024=rows_to(jnp.stack([p["b_out"], p["ln_g"], p["ln_b"]], axis=1), 8),
        bin=p["b_in"][:, None, :])


def _unpack_small(sm):
    L = sm.shape[0]
    owc = jnp.concatenate([sm[:, ROW_WC:ROW_WC + SGU_BLOCK], sm[:, ROW_WC + SGU_BLOCK:ROW_WC + 2 * SGU_BLOCK]], axis=2)
    return dict(
        conv_a_b=sm[:, 0], norm_a_g=sm[:, 1], norm_a_b=sm[:, 2], pool_scale=sm[:, 3], sgu_ln_g=sm[:, 4],
        sgu_ln_b=sm[:, 5], conv_b_w=sm[:, ROW_CBW:ROW_CBW + KB], conv_a_w=sm[:, ROW_CAW:ROW_CAW + KA],
        pool_w=jnp.transpose(sm[:, ROW_PW:ROW_PW + HEAD].reshape(L, HEAD, 4, HEAD), (0, 2, 1, 3)),
        ln_g=sm[:, ROW_LNG:ROW_LNG + 4].reshape(L, D_MODEL), ln_b=sm[:, ROW_LNB:ROW_LNB + 4].reshape(L, D_MODEL),
        b_out=sm[:, ROW_BOUT:ROW_BOUT + 4].reshape(L, D_MODEL),
        b_in=sm[:, ROW_BIN:ROW_BIN + N_SLICES].reshape(L, IN_WIDTH),
        sgu_w=jnp.transpose(owc.reshape(L, SGU_BLOCK, 4, SGU_BLOCK), (0, 2, 1, 3)),
        sgu_bias=sm[:, ROW_SB:ROW_SB + 4, 0:SGU_BLOCK])


def _step(p, m, v, x, target, *, tile_f, tile_b, tk):
    L = p["ln_g"].shape[0]
    xi, yi, ci = _place()
    me_k = 2 * xi + yi
    hi_rows, ho_rows = D_MODEL // 2, GROUP // 2

    cw = jnp.concatenate([p["conv_a_w"], p["conv_b_w"]], axis=1).reshape(-1, 128)
    cw_rows = cw.shape[0]
    cw = _pad_rows(cw, 72)
    wi16 = p["w_in"].astype(BF16)
    wo16 = p["w_out"].astype(BF16)
    wig0, wog0, cwg = _gather_weights(wi16[0:1], wo16[0:1], cw)
    cwg = cwg[:, :cw_rows].reshape(N_CHIPS, L, KA + KB, HEAD)
    conv_full = jnp.transpose(cwg, (1, 2, 0, 3)).reshape(L, KA + KB, GROUP)
    seg, e4 = _indicator_consts()
    k = _layer_consts(p, conv_full)
    layer = [jnp.full((1,), l, jnp.int32) for l in range(L)]

    hcur = x
    saved, wig, wog = [], [wig0[0]], [wog0[0]]
    for l in range(L):
        nxt = (wi16, wo16) if l + 1 < L else None
        outs = _fwd_layer(layer[l], hcur, wig[l], k["bin"], k["caw"], k["cbw"], k["s256"], seg, k["pw"], k["wm"], k["sb"],
                          wog[l], k["v1024"], tile=tile_f, nxt=nxt, target=None if nxt is not None else target)
        y, xb, h, aux, mixb, z = outs[0:6]
        if nxt is not None:
            wig.append(outs[6])
            wog.append(outs[7])
        saved.append((xb, h, aux, mixb, z))
        hcur = y

    dy = hcur
    loss_local = outs[6][0, 0]

    gwi = lax.empty((L, N_CHIPS, D_MODEL, COLS), F32)
    gwo = lax.empty((L, N_CHIPS, GROUP, D_MODEL), F32)
    gwi16 = lax.empty((L, N_CHIPS, D_MODEL, COLS), BF16)
    gwo16 = lax.empty((L, N_CHIPS, GROUP, D_MODEL), BF16)
    ri = lax.empty((L, N_CHIPS, hi_rows, COLS), BF16)
    ro = lax.empty((L, N_CHIPS, ho_rows, D_MODEL), BF16)
    p_i = lax.empty((L, N_CHIPS, hi_rows, COLS), BF16)
    p_o = lax.empty((L, N_CHIPS, ho_rows, D_MODEL), BF16)
    q_i = lax.empty((3, L, hi_rows, COLS), BF16)
    q_o = lax.empty((3, L, ho_rows, D_MODEL), BF16)
    r_sm = [None] * L
    pending = None
    for l in reversed(range(L)):
        xb, h, aux, mixb, z = saved[l]
        exch = None if pending is None else (p_i, p_o, pending, q_i, q_o)
        outs = _bwd_layer(layer[l], dy, z, h, aux, wig[l], k["caw"], k["cbw"], k["s256"], seg, k["pw"], k["wm"],
                          k["wmt"], k["sb"], wog[l], k["v1024"], e4, tile=tile_b, exch=exch)
        dy, dhb, dzb, osm = outs[0:4]
        if l == L - 1:
            osm = osm.at[ROW_LOSS, 0].set(loss_local)
        if exch is not None:
            q_i, q_o, r_sm[l + 1] = outs[4:7]
        larr = layer[l]
        if l > 0:
            gwi, gwi16 = _dw_in(larr, xb, dhb, gwi, gwi16, tk=tk)
        else:
            gwi, gwi16, r_sm[0] = _dw_in(larr, xb, dhb, gwi, gwi16, tk=tk, small=osm)
        gwo, gwo16 = _dw_out(larr, mixb, dzb, gwo, gwo16, tk=tk)
        ri, ro = _swap_halves(larr, gwi16, gwo16, ri, ro)
        cl_arr = jnp.stack([ci, jnp.int32(l)]).astype(jnp.int32)
        p_i = _add_halves(cl_arr, gwi, ri, p_i, rows=hi_rows, cols=COLS, tr=256, name="add_halves_in")
        p_o = _add_halves(cl_arr, gwo, ro, p_o, rows=ho_rows, cols=D_MODEL, tr=128, name="add_halves_out")
        pending = osm
    grad_x = dy
    q_i, q_o = _exchange_last(layer[0], p_i, p_o, q_i, q_o)

    summed = _sum_small(r_sm)
    loss = summed[L - 1, ROW_LOSS, 0]
    grads = _unpack_small(summed)
    for n in ("conv_a_w", "conv_b_w"):
        grads[n] = lax.dynamic_slice_in_dim(grads[n], me_k * HEAD, HEAD, axis=2)

    kc_arr = jnp.stack([me_k, ci]).astype(jnp.int32)
    g_i = _sum_chunks(kc_arr, p_i, q_i, rows=hi_rows, cols=COLS, tr=256, name="sum_chunks_in")
    g_o = _sum_chunks(kc_arr, p_o, q_o, rows=ho_rows, cols=D_MODEL, tr=128, name="sum_chunks_out")
    g_i, g_o = _share_result(g_i, g_o)
    grads["w_in"] = g_i
    grads["w_out"] = g_o

    delta, new_m, new_v = {}, {}, {}
    for n, tr in (("w_in", 512), ("w_out", 256)):
        shp = p[n].shape
        args = [a.reshape(shp[0] * shp[1], shp[2]) for a in (p[n], grads[n], m[n], v[n])]
        outs = _adamw(*args, rows_per_step=tr, name="adamw_" + n, copy_g=True)
        delta[n], new_m[n], new_v[n], grads[n] = (a.reshape(shp) for a in outs)
    small = [n for n in WEIGHTS if n not in ("w_in", "w_out")]
    flat = [[a[n].reshape(-1, a[n].shape[-1]) for n in small] for a in (p, grads, m, v)]
    outs = _adamw_small(*flat)
    for j, n in enumerate(small):
        delta[n], new_m[n], new_v[n] = (o[j].reshape(p[n].shape) for o in outs)

    return (loss, grad_x[None], *[grads[n] for n in WEIGHTS], *[delta[n] for n in WEIGHTS],
            *[new_m[n] for n in WEIGHTS], *[new_v[n] for n in WEIGHTS])


def kernel(x, ln_g, ln_b, w_in, b_in, conv_a_w, conv_a_b, norm_a_g, norm_a_b, conv_b_w, pool_w, pool_scale, sgu_ln_g, sgu_ln_b, sgu_w, sgu_bias, w_out, b_out, loss_target, m_ln_g, m_ln_b, m_w_in, m_b_in, m_conv_a_w, m_conv_a_b, m_norm_a_g, m_norm_a_b, m_conv_b_w, m_pool_w, m_pool_scale, m_sgu_ln_g, m_sgu_ln_b, m_sgu_w, m_sgu_bias, m_w_out, m_b_out, v_ln_g, v_ln_b, v_w_in, v_b_in, v_conv_a_w, v_conv_a_b, v_norm_a_g, v_norm_a_b, v_conv_b_w, v_pool_w, v_pool_scale, v_sgu_ln_g, v_sgu_ln_b, v_sgu_w, v_sgu_bias, v_w_out, v_b_out):
    p = dict(ln_g=ln_g, ln_b=ln_b, w_in=w_in, b_in=b_in, conv_a_w=conv_a_w, conv_a_b=conv_a_b, norm_a_g=norm_a_g,
             norm_a_b=norm_a_b, conv_b_w=conv_b_w, pool_w=pool_w, pool_scale=pool_scale, sgu_ln_g=sgu_ln_g,
             sgu_ln_b=sgu_ln_b, sgu_w=sgu_w, sgu_bias=sgu_bias, w_out=w_out, b_out=b_out)
    m = dict(ln_g=m_ln_g, ln_b=m_ln_b, w_in=m_w_in, b_in=m_b_in, conv_a_w=m_conv_a_w, conv_a_b=m_conv_a_b,
             norm_a_g=m_norm_a_g, norm_a_b=m_norm_a_b, conv_b_w=m_conv_b_w, pool_w=m_pool_w, pool_scale=m_pool_scale,
             sgu_ln_g=m_sgu_ln_g, sgu_ln_b=m_sgu_ln_b, sgu_w=m_sgu_w, sgu_bias=m_sgu_bias, w_out=m_w_out, b_out=m_b_out)
    v = dict(ln_g=v_ln_g, ln_b=v_ln_b, w_in=v_w_in, b_in=v_b_in, conv_a_w=v_conv_a_w, conv_a_b=v_conv_a_b,
             norm_a_g=v_norm_a_g, norm_a_b=v_norm_a_b, conv_b_w=v_conv_b_w, pool_w=v_pool_w, pool_scale=v_pool_scale,
             sgu_ln_g=v_sgu_ln_g, sgu_ln_b=v_sgu_ln_b, sgu_w=v_sgu_w, sgu_bias=v_sgu_bias, w_out=v_w_out, b_out=v_b_out)
    return _step(p, m, v, x[0], loss_target[0], tile_f=256, tile_b=256, tk=2048)
```

```python
import jax
import jax.numpy as jnp
from jax import lax
from jax.experimental import pallas as pl
from jax.experimental.pallas import tpu as pltpu

F32 = jnp.float32
BF16 = jnp.bfloat16
MESH = pl.DeviceIdType.MESH

D_MODEL = 1024
GROUP = 256
HEAD = 64
N_SLICES = 12
IN_WIDTH = N_SLICES * GROUP
N_CHIPS = 4
COLS = IN_WIDTH // N_CHIPS
KA = 31
KB = 3
HALO_A, HALO_B, HALO_C = 32, 8, 16
SGU_BLOCK = 128
CHUNK = 64
LN_EPS = 1e-5
ROWS = 64
V7X_VMEM_BYTES = 64 * 1024 * 1024
VMEM_LIMIT = V7X_VMEM_BYTES - 8 * 1024 * 1024

ADAM_LR, ADAM_B1, ADAM_B2, ADAM_EPS, ADAM_WD, ADAM_STEP = 0.001, 0.9, 0.999, 1e-08, 0.01, 10


ANY = pl.BlockSpec(memory_space=pl.ANY)


def _vmem_params(**kw):
    return pltpu.CompilerParams(vmem_limit_bytes=VMEM_LIMIT, **kw)


def _whole(a):
    return pl.BlockSpec(a.shape, lambda i, l, _n=a.ndim: (0,) * _n)


def _of_layer(a):
    return pl.BlockSpec((None,) + a.shape[1:], lambda i, l, _n=a.ndim: (l[0],) + (0,) * (_n - 1))


def _place():
    return lax.axis_index("x"), lax.axis_index("y"), lax.axis_index("c")


def _other_chips(x, y):
    return [(1 - x, y, 2 * (1 - x) + y), (x, 1 - y, 2 * x + (1 - y)), (1 - x, 1 - y, 2 * (1 - x) + (1 - y))]


def _sig(v):
    return 0.5 * jnp.tanh(0.5 * v) + 0.5


def _dot(a, b):
    return jnp.dot(a, b, preferred_element_type=F32)


def _dot_nt(a, b):
    return lax.dot_general(a, b, (((1,), (1,)), ((), ())), preferred_element_type=F32)


def _dot_tn(a, b):
    return lax.dot_general(a, b, (((0,), (0,)), ((), ())), preferred_element_type=F32)


def _segdot(v, m):
    hi = v.astype(BF16)
    lo = (v - hi.astype(F32)).astype(BF16)
    return _dot(hi, m) + _dot(lo, m)


def _colsum(v):
    return jnp.sum(v, axis=0, keepdims=True)


def _rowmean(v):
    return jnp.mean(v, axis=-1, keepdims=True)


def _lane_group(n):
    return lax.broadcasted_iota(jnp.int32, (1, n), 1) // HEAD


def _pool_cnt(tile, t_rows):
    pos = tile * t_rows + lax.broadcasted_iota(jnp.int32, (t_rows, GROUP), 0) + 1
    grp = lax.broadcasted_iota(jnp.int32, (t_rows, GROUP), 1) // HEAD
    win = jnp.where(grp == 0, 2, jnp.where(grp == 1, 4, jnp.where(grp == 2, 8, 16)))
    return jnp.minimum(pos, win).astype(F32)


def _sgu_masks(wm_ref, wmt_ref, wm_s, wmt_s):
    r = lax.broadcasted_iota(jnp.int32, (SGU_BLOCK, 4 * SGU_BLOCK), 0) // CHUNK
    c = (lax.broadcasted_iota(jnp.int32, (SGU_BLOCK, 4 * SGU_BLOCK), 1) % SGU_BLOCK) // CHUNK
    wm_s[...] = jnp.where(c <= r, wm_ref[...], 0.0).astype(BF16)
    if wmt_ref is not None:
        rt = (lax.broadcasted_iota(jnp.int32, (4 * SGU_BLOCK, SGU_BLOCK), 0) % SGU_BLOCK) // CHUNK
        ct = lax.broadcasted_iota(jnp.int32, (4 * SGU_BLOCK, SGU_BLOCK), 1) // CHUNK
        wmt_s[...] = jnp.where(rt <= ct, wmt_ref[...], 0.0).astype(BF16)


def _vstack(v_blk):
    grp = _lane_group(GROUP)
    return jnp.concatenate([jnp.where(grp == h, v_blk, 0.0) for h in range(4)], axis=0).astype(BF16)


def _gather_next(step, nt, nwi, nwo, gwi, gwo, send_sems, recv_sems, loc_sems):
    x, y, c = _place()
    me_k = 2 * x + y
    sibling = (x, y, 1 - c)
    chips = _other_chips(x, y)
    hi, ho = D_MODEL // 2, GROUP // 2

    def rc(src, dst, sem, to):
        return pltpu.make_async_remote_copy(src_ref=src, dst_ref=dst, send_sem=send_sems.at[sem],
                                            recv_sem=recv_sems.at[sem], device_id=to, device_id_type=MESH)

    def blk(ref, k, n, cc):
        return ref.at[k, pl.ds(cc * n, n), :]

    def ici(r):
        px, py, _ = chips[r]
        to = (px, py, c)
        return [rc(nwi.at[pl.ds(c * hi, hi), :], blk(gwi, me_k, hi, c), 2 * r, to),
                rc(nwo.at[pl.ds(c * ho, ho), :], blk(gwo, me_k, ho, c), 2 * r + 1, to)]

    def landed(r, cc, base):
        pk = chips[r][2]
        return [rc(blk(gwi, pk, hi, cc), blk(gwi, pk, hi, cc), base + 2 * r, sibling),
                rc(blk(gwo, pk, ho, cc), blk(gwo, pk, ho, cc), base + 2 * r + 1, sibling)]

    def local():
        return [pltpu.make_async_copy(nwi, gwi.at[me_k], loc_sems.at[0]),
                pltpu.make_async_copy(nwo, gwo.at[me_k], loc_sems.at[1])]

    @pl.when(step == 0)
    def _():
        for cp in local():
            cp.start()
        for r in range(3):
            for cp in ici(r):
                cp.start()

    @pl.when(step == (3 * nt) // 4)
    def _():
        for r in range(3):
            for got, fwd in zip(landed(r, c, 0), landed(r, c, 6)):
                got.wait_recv()
                fwd.start()

    @pl.when(step == nt - 1)
    def _():
        for r in range(3):
            for got in landed(r, 1 - c, 6):
                got.wait_recv()
        for r in range(3):
            for cp in ici(r) + landed(r, c, 6):
                cp.wait_send()
        for cp in local():
            cp.wait()


def _fwd_layer(larr, x, wi, bin_, caw, cbw, s256, seg, pw, wm, sb, wo, v1024, *, tile, nxt=None, target=None):
    assert nxt is None or target is None
    S = x.shape[0]
    T = tile
    nt = S // T
    alpha = float((2.0 * 4) ** 0.25)
    n_in = 13 + (2 if nxt is not None else 0) + (1 if target is not None else 0)
    n_out = 6 + (2 if nxt is not None else 0) + (1 if target is not None else 0)

    def body(*refs):
        l_ref = refs[0]
        (x_ref, wi_ref, bin_ref, caw_ref, cbw_ref, s256_ref, seg_ref, pw_ref, wm_ref, sb_ref, wo_ref,
         v1024_ref) = refs[1:13]
        y_ref, xb_ref, h_ref, aux_ref, mix_ref, z_ref = refs[n_in:n_in + 6]
        abuf, bbuf, cbuf, wm_s, shf = refs[n_in + n_out:n_in + n_out + 5]
        i = pl.program_id(0)
        if nxt is not None:
            _gather_next(i, nt, refs[13].at[l_ref[0] + 1], refs[14].at[l_ref[0] + 1], refs[n_in + 6], refs[n_in + 7],
                         *refs[n_in + n_out + 5:])

        @pl.when(i == 0)
        def _():
            abuf[0:HALO_A, :] = jnp.zeros((HALO_A, GROUP), F32)
            bbuf[0:HALO_B, :] = jnp.zeros((HALO_B, GROUP), F32)
            cbuf[0:HALO_C, :] = jnp.zeros((HALO_C, GROUP), F32)
            _sgu_masks(wm_ref, None, wm_s, None)

        x = x_ref[...]
        xb = x.astype(BF16)
        xb_ref[...] = xb
        for k in range(N_CHIPS):
            h_ref[:, COLS * k:COLS * (k + 1)] = _dot(xb, wi_ref[k]) + bin_ref[:, COLS * k:COLS * (k + 1)]

        def hs(j):
            return h_ref[:, GROUP * j:GROUP * (j + 1)]

        abuf[HALO_A:HALO_A + T, :] = hs(0) * _sig(hs(1))
        span = T + HALO_A - 8
        for p in range(1, 8):
            shf[p - 1, :, :] = abuf[p:p + span, :]
        for r0 in range(0, T, ROWS):
            acc = None
            for k in range(KA):
                off = HALO_A - (KA - 1) + k
                p, q8 = off % 8, off - off % 8
                win = abuf[r0 + q8:r0 + q8 + ROWS, :] if p == 0 else shf[p - 1, r0 + q8:r0 + q8 + ROWS, :]
                term = caw_ref[k:k + 1, :] * win
                acc = term if acc is None else acc + term
            aux_ref[r0:r0 + ROWS, 0:GROUP] = acc + s256_ref[0:1, :]
        abuf[0:HALO_A, :] = abuf[T:T + HALO_A, :]
        a1 = aux_ref[:, 0:GROUP]
        segm = seg_ref[...]
        cen = a1 - _segdot(a1, segm)
        var = _segdot(cen * cen, segm)
        a2 = cen * lax.rsqrt(var + LN_EPS) * s256_ref[1:2, :] + s256_ref[2:3, :]
        az = hs(2)
        mix_ref[:, 0:GROUP] = (a2 * _sig(a2) * (az * _sig(az))).astype(BF16)

        bbuf[HALO_B:HALO_B + T, :] = hs(4) * hs(5)
        for r0 in range(0, T, ROWS):
            acc = None
            for k in range(KB):
                off = HALO_B - (KB - 1) + k + r0
                term = cbw_ref[k:k + 1, :] * bbuf[off:off + ROWS, :]
                acc = term if acc is None else acc + term
            aux_ref[r0:r0 + ROWS, GROUP:2 * GROUP] = acc
        bbuf[0:HALO_B, :] = bbuf[T:T + HALO_B, :]
        bz = hs(6)
        mix_ref[:, GROUP:2 * GROUP] = (hs(3) * aux_ref[:, GROUP:2 * GROUP] * (bz * _sig(bz))).astype(BF16)

        ch = hs(7)
        cbuf[HALO_C:HALO_C + T, :] = ch
        hi_lane = (lax.broadcasted_iota(jnp.int32, (1, 128), 1) // HEAD) == 1
        for r0 in range(0, T, ROWS):
            def win(col, j0, j1):
                s = None
                for j in range(j0, j1):
                    off = HALO_C - j + r0
                    term = cbuf[off:off + ROWS, 128 * col:128 * (col + 1)]
                    s = term if s is None else s + term
                return s
            w0 = win(0, 0, 2) + jnp.where(hi_lane, win(0, 2, 4), 0.0)
            w1 = win(1, 0, 8) + jnp.where(hi_lane, win(1, 8, 16), 0.0)
            aux_ref[r0:r0 + ROWS, 2 * GROUP:2 * GROUP + 128] = w0
            aux_ref[r0:r0 + ROWS, 2 * GROUP + 128:3 * GROUP] = w1
        cbuf[0:HALO_C, :] = cbuf[T:T + HALO_C, :]
        pooled = aux_ref[:, 2 * GROUP:3 * GROUP] / _pool_cnt(i, T) - ch
        aux_ref[:, 2 * GROUP:3 * GROUP] = pooled
        q = _dot(pooled.astype(BF16), pw_ref[...])
        cz = hs(8)
        mix_ref[:, 2 * GROUP:3 * GROUP] = (q * s256_ref[3:4, :] * (cz * _sig(cz))).astype(BF16)

        dv = hs(10)
        cen = dv - _rowmean(dv)
        var = _rowmean(cen * cen)
        v = cen * lax.rsqrt(var + LN_EPS) * s256_ref[4:5, :] + s256_ref[5:6, :]
        sps = []
        for n in range(T // SGU_BLOCK):
            vb = v[n * SGU_BLOCK:(n + 1) * SGU_BLOCK, :]
            sps.append(_dot(wm_s[...], _vstack(vb)) + sb_ref[...])
        sp = jnp.concatenate(sps, axis=0)
        dz = hs(11)
        mix_ref[:, 3 * GROUP:4 * GROUP] = (hs(9) * sp * (dz * _sig(dz))).astype(BF16)

        out = v1024_ref[0:1, :]
        for k in range(N_CHIPS):
            out = out + _dot(mix_ref[:, GROUP * k:GROUP * (k + 1)], wo_ref[k])
        z = alpha * x + out
        z_ref[...] = z
        cen = z - _rowmean(z)
        var = _rowmean(cen * cen)
        y = cen * lax.rsqrt(var + LN_EPS) * v1024_ref[1:2, :] + v1024_ref[2:3, :]
        if target is None:
            y_ref[...] = y
        else:
            t_ref, loss_ref = refs[13], refs[n_in + 6]

            @pl.when(i == 0)
            def _():
                loss_ref[...] = jnp.zeros_like(loss_ref)
            err = y - t_ref[...]
            y_ref[...] = err * (1.0 / D_MODEL)
            loss_ref[...] += jnp.sum(_colsum(err * err), axis=1, keepdims=True) * (0.5 / D_MODEL)

    def rows(width):
        return pl.BlockSpec((T, width), lambda i, l: (i, 0))

    consts = (wi, bin_, caw, cbw, s256, seg, pw, wm, sb, wo, v1024)
    in_specs = [rows(D_MODEL)] + [_whole(a) if a is wi or a is seg or a is wo else _of_layer(a) for a in consts]
    out_specs = [rows(D_MODEL), rows(D_MODEL), rows(IN_WIDTH), rows(3 * GROUP), rows(D_MODEL), rows(D_MODEL)]
    out_shape = [jax.ShapeDtypeStruct((S, D_MODEL), F32), jax.ShapeDtypeStruct((S, D_MODEL), BF16),
                 jax.ShapeDtypeStruct((S, IN_WIDTH), F32), jax.ShapeDtypeStruct((S, 3 * GROUP), F32),
                 jax.ShapeDtypeStruct((S, D_MODEL), BF16), jax.ShapeDtypeStruct((S, D_MODEL), F32)]
    scratch = [pltpu.VMEM((T + HALO_A, GROUP), F32), pltpu.VMEM((T + HALO_B, GROUP), F32),
               pltpu.VMEM((T + HALO_C, GROUP), F32), pltpu.VMEM((SGU_BLOCK, 4 * SGU_BLOCK), BF16),
               pltpu.VMEM((7, T + HALO_A - 8, GROUP), F32)]
    extra = ()
    if nxt is not None:
        extra = tuple(nxt)
        in_specs += [ANY, ANY]
        out_specs += [ANY, ANY]
        out_shape += [jax.ShapeDtypeStruct((N_CHIPS, D_MODEL, COLS), BF16),
                      jax.ShapeDtypeStruct((N_CHIPS, GROUP, D_MODEL), BF16)]
        scratch += [pltpu.SemaphoreType.DMA((12,)), pltpu.SemaphoreType.DMA((12,)), pltpu.SemaphoreType.DMA((2,))]
    if target is not None:
        extra = (target,)
        in_specs += [rows(D_MODEL)]
        out_specs += [pl.BlockSpec((8, 128), lambda i, l: (0, 0))]
        out_shape += [jax.ShapeDtypeStruct((8, 128), F32)]
    grid_spec = pltpu.PrefetchScalarGridSpec(num_scalar_prefetch=1, grid=(nt,), in_specs=in_specs,
                                             out_specs=out_specs, scratch_shapes=scratch)
    return pl.pallas_call(
        body, name=("fwd_layer_loss" if target is not None else "fwd_layer") if nxt is None else "fwd_layer_gather",
        grid_spec=grid_spec, out_shape=out_shape,
        compiler_params=_vmem_params(dimension_semantics=("arbitrary",), has_side_effects=nxt is not None),
    )(larr, x, *consts, *extra)


ROW_CBW = 8
ROW_CAW = 16
ROW_LOSS = 7
ROW_PW = 48
ROW_LNG = 112
ROW_LNB = 116
ROW_BOUT = 120
ROW_BIN = 124
ROW_WC = 136
ROW_SB = 392
SM_ROWS = 400
N_DEV = 8


def _exchange_comm(start, finish, l, p_i, p_o, sm, r_i, r_o, r_sm, send_sems, recv_sems, loc_sem):
    x, y, c = _place()
    me = 4 * x + 2 * y + c
    chips = _other_chips(x, y)

    def rc(src, dst, sem, to):
        return pltpu.make_async_remote_copy(src_ref=src, dst_ref=dst, send_sem=send_sems.at[sem],
                                            recv_sem=recv_sems.at[sem], device_id=to, device_id_type=MESH)

    def big(r):
        px, py, pk = chips[r]
        to = (px, py, c)
        return [rc(p_i.at[l, pk], r_i.at[r, l], 2 * r, to), rc(p_o.at[l, pk], r_o.at[r, l], 2 * r + 1, to)]

    def peer(rel):
        px = 1 - x if rel & 4 else x
        py = 1 - y if rel & 2 else y
        pc = 1 - c if rel & 1 else c
        return (px, py, pc), 4 * px + 2 * py + pc

    def small_out(rel):
        to, _ = peer(rel)
        return rc(sm, r_sm.at[me], 5 + rel, to)

    def small_in(rel):
        to, idx = peer(rel)
        return rc(sm, r_sm.at[idx], 5 + rel, to)

    def local():
        return pltpu.make_async_copy(sm, r_sm.at[me], loc_sem.at[0])

    with_big, with_small = p_i is not None, sm is not None

    @pl.when(start)
    def _():
        if with_small:
            local().start()
        if with_big:
            for r in range(3):
                for cp in big(r):
                    cp.start()
        if with_small:
            for rel in range(1, N_DEV):
                small_out(rel).start()

    @pl.when(finish)
    def _():
        if with_big:
            for r in range(3):
                for cp in big(r):
                    cp.wait()
        if with_small:
            for rel in range(1, N_DEV):
                small_in(rel).wait_recv()
                small_out(rel).wait_send()
            local().wait()


RC = 32
RC_WIDE = 16
ACC_ROWS = 136


def _rsum8(v):
    r = v[0:8]
    for j in range(1, v.shape[0] // 8):
        r = r + v[8 * j:8 * j + 8]
    return r


def _bwd_layer(larr, dy, z, h, aux, wi, caw, cbw, s256, seg, pw, wm, wmt, sb, wo, v1024, e4, *, tile, exch=None):
    S = dy.shape[0]
    T = tile
    nt = S // T
    nblk = T // SGU_BLOCK
    alpha = float((2.0 * 4) ** 0.25)
    n_in = 17 + (5 if exch is not None else 0)
    n_out = 4 + (3 if exch is not None else 0)
    slab = pltpu.VMEM((T, GROUP), F32)
    scratch = dict(
        dbuf=pltpu.VMEM((T + HALO_A, GROUP), F32), ebuf=pltpu.VMEM((T + HALO_B, GROUP), F32),
        fbuf=pltpu.VMEM((T + HALO_C, GROUP), F32), sh=pltpu.VMEM((7, T + HALO_A - 8, GROUP), F32),
        wm_s=pltpu.VMEM((SGU_BLOCK, 4 * SGU_BLOCK), BF16), wmt_s=pltpu.VMEM((4 * SGU_BLOCK, SGU_BLOCK), BF16),
        dsp_acc=pltpu.VMEM((SGU_BLOCK, GROUP), F32), pw_acc=pltpu.VMEM((GROUP, GROUP), F32),
        acc_s=pltpu.VMEM((8 * ACC_ROWS, GROUP), F32), acc_w=pltpu.VMEM((24, D_MODEL), F32),
        dmix_s=pltpu.VMEM((T, D_MODEL), F32), vst_s=pltpu.VMEM((nblk, 4 * SGU_BLOCK, GROUP), BF16),
        dq_s=pltpu.VMEM((T, GROUP), BF16), dxt_s=pltpu.VMEM((D_MODEL, T), F32),
        mean_s=slab, t1_s=slab, t2_s=slab, q_s=slab, xv_s=slab, rv_s=slab, v_s=slab, sp_s=slab, a0_s=slab, sg_s=slab,
        xh_s=slab, ra_s=slab, ub_s=slab, dsp_s=slab, m1_s=slab, m2_s=slab, dpool_s=slab, dvd_s=slab, u_s=slab,
        du_s=slab, cw_s=slab)
    names = list(scratch)

    def body(*refs):
        (dy_ref, z_ref, h_ref, aux_ref, wi_ref, caw_ref, cbw_ref, s256_ref, seg_ref, pw_ref, wm_ref, wmt_ref,
         sb_ref, wo_ref, v1024_ref, e4_ref) = refs[1:17]
        dx_ref, dhb_ref, dzb_ref, osm_ref = refs[n_in:n_in + 4]
        k0 = n_in + n_out
        sc = dict(zip(names, refs[k0:k0 + len(names)]))
        dbuf, ebuf, fbuf, sh = sc["dbuf"], sc["ebuf"], sc["fbuf"], sc["sh"]
        wm_s, wmt_s, dsp_acc, pw_acc, acc_s, acc_w = (sc[n] for n in ("wm_s", "wmt_s", "dsp_acc", "pw_acc", "acc_s",
                                                                        "acc_w"))
        dmix_s, vst_s, dq_s = sc["dmix_s"], sc["vst_s"], sc["dq_s"]
        i = pl.program_id(0)
        tile_idx = nt - 1 - i
        if exch is not None:
            p_i, p_o, sm = refs[17:20]
            r_i, r_o, r_sm = refs[n_in + 4:n_in + 7]
            _exchange_comm(i == 0, i == nt - 1, refs[0][0] + 1, p_i, p_o, sm, r_i, r_o, r_sm, *refs[k0 + len(names):])

        @pl.when(i == 0)
        def _():
            dbuf[T:T + HALO_A, :] = jnp.zeros((HALO_A, GROUP), F32)
            ebuf[T:T + HALO_B, :] = jnp.zeros((HALO_B, GROUP), F32)
            fbuf[T:T + HALO_C, :] = jnp.zeros((HALO_C, GROUP), F32)
            _sgu_masks(wm_ref, wmt_ref, wm_s, wmt_s)
            osm_ref[...] = jnp.zeros_like(osm_ref)
            dsp_acc[...] = jnp.zeros_like(dsp_acc)
            pw_acc[...] = jnp.zeros_like(pw_acc)
            acc_s[...] = jnp.zeros_like(acc_s)
            acc_w[...] = jnp.zeros_like(acc_w)

        def chunks(rc, fn):
            for c in range(T // rc):
                fn(pl.ds(c * rc, rc))

        def hs(j, rows):
            return h_ref[rows, GROUP * j:GROUP * (j + 1)]

        def acc_add(row, val):
            acc_s[8 * row:8 * row + 8, :] += _rsum8(val)

        def put_dh(j, rows, val):
            acc_add(ROW_BIN + j, val)
            dhb_ref[rows, GROUP * j:GROUP * (j + 1)] = val.astype(BF16)

        def dsilu(v, s):
            return s * (1.0 + v * (1.0 - s))

        def vec(r):
            return s256_ref[r:r + 1, :]

        def ln_bwd(rows):
            dyc = dy_ref[rows, :]
            zc = z_ref[rows, :]
            cen = zc - _rowmean(zc)
            rstd = lax.rsqrt(_rowmean(cen * cen) + LN_EPS)
            xhat = cen * rstd
            acc_w[0:8, :] += _rsum8(dyc * xhat)
            acc_w[8:16, :] += _rsum8(dyc)
            gdy = dyc * v1024_ref[1:2, :]
            dz = rstd * (gdy - _rowmean(gdy) - xhat * _rowmean(gdy * xhat))
            acc_w[16:24, :] += _rsum8(dz)
            dzb_ref[rows, :] = dz.astype(BF16)
            dx_ref[rows, :] = alpha * dz
        chunks(RC_WIDE, ln_bwd)

        segm = seg_ref[...]
        dzb = dzb_ref[...]
        for k in range(N_CHIPS):
            dmix_s[:, GROUP * k:GROUP * (k + 1)] = _dot_nt(dzb, wo_ref[k])
        sc["mean_s"][...] = _segdot(aux_ref[:, 0:GROUP], segm)
        pooled_b = aux_ref[:, 2 * GROUP:3 * GROUP].astype(BF16)
        sc["q_s"][...] = _dot(pooled_b, pw_ref[...])

        def centre(rows):
            cen = aux_ref[rows, 0:GROUP] - sc["mean_s"][rows, :]
            sc["t1_s"][rows, :] = cen * cen
            dv_in = hs(10, rows)
            cen_v = dv_in - _rowmean(dv_in)
            rstd_v = lax.rsqrt(_rowmean(cen_v * cen_v) + LN_EPS)
            xv = cen_v * rstd_v
            sc["xv_s"][rows, :] = xv
            sc["rv_s"][rows, :] = jnp.broadcast_to(rstd_v, xv.shape)
            sc["v_s"][rows, :] = xv * vec(4) + vec(5)
        chunks(RC, centre)

        sc["t2_s"][...] = _segdot(sc["t1_s"][...], segm)
        for n in range(nblk):
            blk = slice(n * SGU_BLOCK, (n + 1) * SGU_BLOCK)
            vst_s[n] = _vstack(sc["v_s"][blk, :])
            sc["sp_s"][blk, :] = _dot(wm_s[...], vst_s[n]) + sb_ref[...]

        def mixers(rows):
            a_val, a_glu, a_z = hs(0, rows), hs(1, rows), hs(2, rows)
            sg = _sig(a_glu)
            sc["a0_s"][rows, :] = a_val * sg
            sc["sg_s"][rows, :] = sg
            rstd_a = lax.rsqrt(sc["t2_s"][rows, :] + LN_EPS)
            xh = (aux_ref[rows, 0:GROUP] - sc["mean_s"][rows, :]) * rstd_a
            a2 = xh * vec(1) + vec(2)
            s2 = _sig(a2)
            sz = _sig(a_z)
            dya = dmix_s[rows, 0:GROUP]
            put_dh(2, rows, dya * (a2 * s2) * dsilu(a_z, sz))
            d_a2 = dya * (a_z * sz) * dsilu(a2, s2)
            acc_add(1, d_a2 * xh)
            acc_add(2, d_a2)
            gd = d_a2 * vec(1)
            sc["t1_s"][rows, :] = gd
            sc["t2_s"][rows, :] = gd * xh
            sc["xh_s"][rows, :] = xh
            sc["ra_s"][rows, :] = rstd_a
            b_b, b_c, b_h, b_z = hs(3, rows), hs(4, rows), hs(5, rows), hs(6, rows)
            cb = aux_ref[rows, GROUP:2 * GROUP]
            sz = _sig(b_z)
            dyb = dmix_s[rows, GROUP:2 * GROUP]
            put_dh(3, rows, dyb * cb * (b_z * sz))
            put_dh(6, rows, dyb * b_b * cb * dsilu(b_z, sz))
            ebuf[rows, :] = dyb * b_b * (b_z * sz)
            sc["ub_s"][rows, :] = b_c * b_h
            c_z = hs(8, rows)
            q = sc["q_s"][rows, :]
            sz = _sig(c_z)
            dyc = dmix_s[rows, 2 * GROUP:3 * GROUP]
            acc_add(3, dyc * q * (c_z * sz))
            put_dh(8, rows, dyc * q * vec(3) * dsilu(c_z, sz))
            dq_s[rows, :] = (dyc * vec(3) * (c_z * sz)).astype(BF16)
            d_u, d_z = hs(9, rows), hs(11, rows)
            sp = sc["sp_s"][rows, :]
            sz = _sig(d_z)
            dyd = dmix_s[rows, 3 * GROUP:4 * GROUP]
            put_dh(9, rows, dyd * sp * (d_z * sz))
            put_dh(11, rows, dyd * d_u * sp * dsilu(d_z, sz))
            sc["dsp_s"][rows, :] = dyd * d_u * (d_z * sz)
        chunks(RC, mixers)

        sc["m1_s"][...] = _segdot(sc["t1_s"][...], segm)
        sc["m2_s"][...] = _segdot(sc["t2_s"][...], segm)
        d_q = dq_s[...]
        pw_acc[...] += _dot_tn(pooled_b, d_q)
        sc["dpool_s"][...] = _dot_nt(d_q, pw_ref[...])
        grp = _lane_group(GROUP)
        for n in range(nblk):
            blk = slice(n * SGU_BLOCK, (n + 1) * SGU_BLOCK)
            dspb = sc["dsp_s"][blk, :]
            dsp_acc[...] += dspb
            dspb16 = dspb.astype(BF16)
            dvst = _dot(wmt_s[...], dspb16)
            dvb = None
            for hh in range(4):
                part = jnp.where(grp == hh, dvst[hh * SGU_BLOCK:(hh + 1) * SGU_BLOCK, :], 0.0)
                dvb = part if dvb is None else dvb + part
            sc["dvd_s"][blk, :] = dvb
            dwc = _dot_nt(dspb16, vst_s[n])
            osm_ref[ROW_WC:ROW_WC + SGU_BLOCK, :] += dwc[:, 0:GROUP]
            osm_ref[ROW_WC + SGU_BLOCK:ROW_WC + 2 * SGU_BLOCK, :] += dwc[:, GROUP:2 * GROUP]

        def ln_sums(rows):
            xh = sc["xh_s"][rows, :]
            d_a1 = sc["ra_s"][rows, :] * (sc["t1_s"][rows, :] - sc["m1_s"][rows, :] - xh * sc["m2_s"][rows, :])
            acc_add(0, d_a1)
            dbuf[rows, :] = d_a1
            pos = tile_idx * T + rows.start + lax.broadcasted_iota(jnp.int32, (RC, GROUP), 0) + 1
            lane = lax.broadcasted_iota(jnp.int32, (RC, GROUP), 1) // HEAD
            win = jnp.where(lane == 0, 2, jnp.where(lane == 1, 4, jnp.where(lane == 2, 8, 16)))
            fbuf[rows, :] = sc["dpool_s"][rows, :] / jnp.minimum(pos, win).astype(F32)
            d_v = sc["dvd_s"][rows, :]
            xv = sc["xv_s"][rows, :]
            acc_add(4, d_v * xv)
            acc_add(5, d_v)
            gd = d_v * vec(4)
            put_dh(10, rows, sc["rv_s"][rows, :] * (gd - _rowmean(gd) - xv * _rowmean(gd * xv)))
        chunks(RC, ln_sums)

        span = T + HALO_A - 8
        for p in range(1, 8):
            sh[p - 1, :, :] = dbuf[p:p + span, :]

        for r0 in range(0, T, ROWS):
            uc = sc["ub_s"][r0:r0 + ROWS, :]
            acc = None
            for k in range(KB):
                off = (KB - 1) - k + r0
                w = ebuf[off:off + ROWS, :]
                term = cbw_ref[k:k + 1, :] * w
                acc = term if acc is None else acc + term
                acc_add(ROW_CBW + k, uc * w)
            sc["du_s"][r0:r0 + ROWS, :] = acc
        ebuf[T:T + HALO_B, :] = ebuf[0:HALO_B, :]

        hi_lane = (lax.broadcasted_iota(jnp.int32, (1, 128), 1) // HEAD) == 1
        for r0 in range(0, T, ROWS):
            def win(col, j0, j1):
                s = None
                for j in range(j0, j1):
                    term = fbuf[r0 + j:r0 + j + ROWS, 128 * col:128 * (col + 1)]
                    s = term if s is None else s + term
                return s
            sc["cw_s"][r0:r0 + ROWS, 0:128] = win(0, 0, 2) + jnp.where(hi_lane, win(0, 2, 4), 0.0)
            sc["cw_s"][r0:r0 + ROWS, 128:256] = win(1, 0, 8) + jnp.where(hi_lane, win(1, 8, 16), 0.0)
        fbuf[T:T + HALO_C, :] = fbuf[0:HALO_C, :]

        def rest_bc(rows):
            d_u = sc["du_s"][rows, :]
            put_dh(4, rows, d_u * hs(5, rows))
            put_dh(5, rows, d_u * hs(4, rows))
            put_dh(7, rows, sc["cw_s"][rows, :] - sc["dpool_s"][rows, :])
        chunks(RC, rest_bc)

        dxt_s = sc["dxt_s"]

        def dx_term(k):
            term = _dot_nt(wi_ref[k], dhb_ref[:, COLS * k:COLS * (k + 1)])
            if k == 1:
                dxt_s[...] = term
            else:
                dxt_s[...] += term

        def conv_a(rows):
            a0c = sc["a0_s"][rows, :]
            acc = None
            for k in range(KA):
                off = (KA - 1) - k
                p, q8 = off % 8, off - off % 8
                w = dbuf[pl.ds(rows.start + q8, RC), :] if p == 0 else sh[p - 1, pl.ds(rows.start + q8, RC), :]
                term = caw_ref[k:k + 1, :] * w
                acc = term if acc is None else acc + term
                acc_add(ROW_CAW + k, a0c * w)
            sc["u_s"][rows, :] = acc
        n_chunks = T // RC
        after = {(n_chunks * j) // 3: j + 1 for j in range(3)}
        for c in range(n_chunks):
            conv_a(pl.ds(c * RC, RC))
            if c in after:
                dx_term(after[c])
        dbuf[T:T + HALO_A, :] = dbuf[0:HALO_A, :]

        def rest_a(rows):
            d_a0 = sc["u_s"][rows, :]
            sg = sc["sg_s"][rows, :]
            put_dh(0, rows, d_a0 * sg)
            put_dh(1, rows, d_a0 * hs(0, rows) * sg * (1.0 - sg))
        chunks(RC, rest_a)
        dx_term(0)
        dx_ref[...] += dxt_s[...].T

        @pl.when(i == nt - 1)
        def _():
            for row in list(range(6)) + list(range(ROW_CBW, ROW_CBW + KB)) + list(range(ROW_CAW, ROW_CAW + KA)) + list(
                    range(ROW_BIN, ROW_BIN + N_SLICES)):
                osm_ref[row:row + 1, :] = _colsum(acc_s[8 * row:8 * row + 8, :])
            for j, row in enumerate((ROW_LNG, ROW_LNB, ROW_BOUT)):
                cs = _colsum(acc_w[8 * j:8 * j + 8, :])
                for q in range(D_MODEL // GROUP):
                    osm_ref[row + q:row + q + 1, :] = cs[:, GROUP * q:GROUP * (q + 1)]
            r = lax.broadcasted_iota(jnp.int32, (SGU_BLOCK, GROUP), 0) // CHUNK
            c = (lax.broadcasted_iota(jnp.int32, (SGU_BLOCK, GROUP), 1) % SGU_BLOCK) // CHUNK
            for half in range(2):
                rows_ = slice(ROW_WC + half * SGU_BLOCK, ROW_WC + (half + 1) * SGU_BLOCK)
                osm_ref[rows_, :] = jnp.where(c <= r, osm_ref[rows_, :], 0.0)
            sb_t = _segdot(dsp_acc[...], e4_ref[...]).T
            osm_ref[ROW_SB:ROW_SB + 8, 0:SGU_BLOCK] = sb_t[0:8, :]
            for g in range(4):
                osm_ref[ROW_PW:ROW_PW + HEAD, HEAD * g:HEAD * (g + 1)] = (
                    pw_acc[HEAD * g:HEAD * (g + 1), HEAD * g:HEAD * (g + 1)])

    def rows(width):
        return pl.BlockSpec((T, width), lambda i, l: (nt - 1 - i, 0))

    consts = (wi, caw, cbw, s256, seg, pw, wm, wmt, sb, wo, v1024, e4)
    unstacked = (wi, seg, wo, e4)
    in_specs = [rows(D_MODEL), rows(D_MODEL), rows(IN_WIDTH), rows(3 * GROUP)] + [
        _whole(a) if any(a is u for u in unstacked) else _of_layer(a) for a in consts]
    out_specs = [rows(D_MODEL), rows(IN_WIDTH), rows(D_MODEL), pl.BlockSpec((SM_ROWS, GROUP), lambda i, l: (0, 0))]
    out_shape = [jax.ShapeDtypeStruct((S, D_MODEL), F32), jax.ShapeDtypeStruct((S, IN_WIDTH), BF16),
                 jax.ShapeDtypeStruct((S, D_MODEL), BF16), jax.ShapeDtypeStruct((SM_ROWS, GROUP), F32)]
    scratch_shapes = list(scratch.values())
    extra, aliases = (), {}
    if exch is not None:
        extra = tuple(exch)
        r_i, r_o = exch[3], exch[4]
        in_specs += [ANY] * 5
        out_specs += [ANY] * 3
        out_shape += [jax.ShapeDtypeStruct(r_i.shape, r_i.dtype), jax.ShapeDtypeStruct(r_o.shape, r_o.dtype),
                      jax.ShapeDtypeStruct((N_DEV, SM_ROWS, GROUP), F32)]
        scratch_shapes += [pltpu.SemaphoreType.DMA((13,)), pltpu.SemaphoreType.DMA((13,)),
                           pltpu.SemaphoreType.DMA((1,))]
        aliases = {20: 4, 21: 5}
    grid_spec = pltpu.PrefetchScalarGridSpec(num_scalar_prefetch=1, grid=(nt,), in_specs=in_specs,
                                             out_specs=out_specs, scratch_shapes=scratch_shapes)
    return pl.pallas_call(
        body, name="bwd_layer" if exch is None else "bwd_layer_exchange",
        grid_spec=grid_spec, out_shape=out_shape, input_output_aliases=aliases,
        compiler_params=_vmem_params(dimension_semantics=("arbitrary",), has_side_effects=exch is not None),
    )(larr, dy, z, h, aux, *consts, *extra)


def _dw_in(layer, xb, dhb, slab, slab16, *, tk, small=None):
    S = xb.shape[0]
    ns = S // tk

    def body(*refs):
        l_ref, a_ref, b_ref = refs[0:3]
        o_ref, o16_ref = refs[n_in:n_in + 2]
        if small is not None:
            first = (pl.program_id(0) == 0) & (pl.program_id(1) == 0)
            last = (pl.program_id(0) == N_CHIPS - 1) & (pl.program_id(1) == ns - 1)
            _exchange_comm(first, last, None, None, None, refs[5], None, None, refs[n_in + 2], *refs[n_in + 3:])

        @pl.when(pl.program_id(1) == 0)
        def _():
            o_ref[...] = jnp.zeros_like(o_ref)
        o_ref[...] += _dot_tn(a_ref[...], b_ref[...])

        @pl.when(pl.program_id(1) == ns - 1)
        def _():
            o16_ref[...] = o_ref[...].astype(BF16)

    o_spec = pl.BlockSpec((None, None, D_MODEL, COLS), lambda j, s, l: (l[0], j, 0, 0))
    in_specs = [pl.BlockSpec((tk, D_MODEL), lambda j, s, l: (s, 0)), pl.BlockSpec((tk, COLS), lambda j, s, l: (s, j)),
                ANY, ANY]
    out_specs = [o_spec, o_spec]
    out_shape = [jax.ShapeDtypeStruct(slab.shape, F32), jax.ShapeDtypeStruct(slab.shape, BF16)]
    scratch, extra = [], ()
    if small is not None:
        extra = (small,)
        in_specs += [ANY]
        out_specs += [ANY]
        out_shape += [jax.ShapeDtypeStruct((N_DEV, SM_ROWS, GROUP), F32)]
        scratch = [pltpu.SemaphoreType.DMA((13,)), pltpu.SemaphoreType.DMA((13,)), pltpu.SemaphoreType.DMA((1,))]
    n_in = 5 + len(extra)
    grid_spec = pltpu.PrefetchScalarGridSpec(
        num_scalar_prefetch=1, grid=(N_CHIPS, ns), in_specs=in_specs, out_specs=out_specs, scratch_shapes=scratch)
    return pl.pallas_call(
        body, name="dw_in" if small is None else "dw_in_exchange", grid_spec=grid_spec, out_shape=out_shape,
        input_output_aliases={3: 0, 4: 1},
        compiler_params=_vmem_params(dimension_semantics=("arbitrary", "arbitrary"), has_side_effects=small is not None),
    )(layer, xb, dhb, slab, slab16, *extra)


def _dw_out(layer, mixb, dzb, slab, slab16, *, tk):
    S = mixb.shape[0]
    ns = S // tk

    def body(l_ref, a_ref, b_ref, slab_ref, slab16_ref, o_ref, o16_ref):
        del l_ref, slab_ref, slab16_ref

        @pl.when(pl.program_id(0) == 0)
        def _():
            o_ref[...] = jnp.zeros_like(o_ref)
        o_ref[...] += _dot_tn(a_ref[...], b_ref[...]).reshape(N_CHIPS, GROUP, D_MODEL)

        @pl.when(pl.program_id(0) == ns - 1)
        def _():
            o16_ref[...] = o_ref[...].astype(BF16)

    o_spec = pl.BlockSpec((None, N_CHIPS, GROUP, D_MODEL), lambda s, l: (l[0], 0, 0, 0))
    grid_spec = pltpu.PrefetchScalarGridSpec(
        num_scalar_prefetch=1, grid=(ns,),
        in_specs=[pl.BlockSpec((tk, D_MODEL), lambda s, l: (s, 0)), pl.BlockSpec((tk, D_MODEL), lambda s, l: (s, 0)),
                  ANY, ANY],
        out_specs=[o_spec, o_spec])
    return pl.pallas_call(
        body, name="dw_out", grid_spec=grid_spec,
        out_shape=[jax.ShapeDtypeStruct(slab.shape, F32), jax.ShapeDtypeStruct(slab.shape, BF16)],
        input_output_aliases={3: 0, 4: 1},
        compiler_params=_vmem_params(dimension_semantics=("arbitrary",)),
    )(layer, mixb, dzb, slab, slab16)


def _adamw_math(w, g, m, v):
    nm = ADAM_B1 * m + (1.0 - ADAM_B1) * g
    nv = ADAM_B2 * v + (1.0 - ADAM_B2) * (g * g)
    c1 = 1.0 - ADAM_B1 ** ADAM_STEP
    c2 = 1.0 - ADAM_B2 ** ADAM_STEP
    return -ADAM_LR * ((nm / c1) / (jnp.sqrt(nv / c2) + ADAM_EPS) + ADAM_WD * w), nm, nv


def _adamw_small(ws, gs, ms, vs):
    n = len(ws)

    def body(*refs):
        for j in range(n):
            d, nm, nv = _adamw_math(*(refs[k * n + j][...] for k in range(4)))
            refs[4 * n + j][...] = d
            refs[5 * n + j][...] = nm
            refs[6 * n + j][...] = nv

    shapes = [jax.ShapeDtypeStruct(w.shape, F32) for w in ws]
    outs = pl.pallas_call(body, name="adamw_small", out_shape=shapes * 3, compiler_params=_vmem_params())(
        *ws, *gs, *ms, *vs)
    return outs[0:n], outs[n:2 * n], outs[2 * n:3 * n]


def _adamw(w, g, m, v, *, rows_per_step, name, copy_g=False):
    R, C = w.shape
    tr = rows_per_step

    def body(w_ref, g_ref, m_ref, v_ref, d_ref, nm_ref, nv_ref, *g_out):
        g_ = g_ref[...]
        d_ref[...], nm_ref[...], nv_ref[...] = _adamw_math(w_ref[...], g_, m_ref[...], v_ref[...])
        if copy_g:
            g_out[0][...] = g_

    spec = pl.BlockSpec((tr, C), lambda i: (i, 0))
    n_out = 4 if copy_g else 3
    return pl.pallas_call(
        body, name=name, grid=(R // tr,),
        in_specs=[spec] * 4, out_specs=[spec] * n_out,
        out_shape=[jax.ShapeDtypeStruct((R, C), F32)] * n_out,
        compiler_params=_vmem_params(dimension_semantics=("arbitrary",)),
    )(w, g, m, v)


def _gather_weights(wi16, wo16, cw):
    L = wi16.shape[0]
    hi_rows, ho_rows = D_MODEL // 2, GROUP // 2
    n_ici = 2 * L + 1
    n_fwd = 2 * L

    def body(wi_ref, wo_ref, cw_ref, *rest):
        wig = rest[0:L]
        wog = rest[L:2 * L]
        cwg = rest[2 * L]
        send_sems, recv_sems, loc_sems = rest[2 * L + 1:]
        x, y, c = _place()
        me_k = 2 * x + y
        sibling = (x, y, 1 - c)
        chips = _other_chips(x, y)

        def half_i(ref, blk):
            return ref.at[blk, pl.ds(c * hi_rows, hi_rows), :]

        def half_o(ref, blk):
            return ref.at[blk, pl.ds(c * ho_rows, ho_rows), :]

        def other_half_i(ref, blk):
            return ref.at[blk, pl.ds((1 - c) * hi_rows, hi_rows), :]

        def other_half_o(ref, blk):
            return ref.at[blk, pl.ds((1 - c) * ho_rows, ho_rows), :]

        local = []
        for l in range(L):
            local.append(pltpu.make_async_copy(wi_ref.at[l], wig[l].at[me_k], loc_sems.at[2 * l]))
            local.append(pltpu.make_async_copy(wo_ref.at[l], wog[l].at[me_k], loc_sems.at[2 * l + 1]))
        local.append(pltpu.make_async_copy(cw_ref, cwg.at[me_k], loc_sems.at[2 * L]))
        for cp in local:
            cp.start()

        def remote(src, dst, sem, to):
            return pltpu.make_async_remote_copy(src_ref=src, dst_ref=dst, send_sem=send_sems.at[sem],
                                                recv_sem=recv_sems.at[sem], device_id=to, device_id_type=MESH)

        sends = []
        for r, (px, py, _) in enumerate(chips):
            to = (px, py, c)
            for l in range(L):
                sends.append(remote(half_i(wi_ref, l), half_i(wig[l], me_k), r * n_ici + 2 * l, to))
                sends.append(remote(half_o(wo_ref, l), half_o(wog[l], me_k), r * n_ici + 2 * l + 1, to))
            sends.append(remote(cw_ref, cwg.at[me_k], r * n_ici + 2 * L, to))
        for cp in sends:
            cp.start()

        base = 3 * n_ici
        fwds = []
        for r, (px, py, pk) in enumerate(chips):
            for l in range(L):
                remote(half_i(wig[l], pk), half_i(wig[l], pk), r * n_ici + 2 * l, sibling).wait_recv()
                f = remote(half_i(wig[l], pk), half_i(wig[l], pk), base + r * n_fwd + 2 * l, sibling)
                f.start()
                fwds.append(f)
                remote(half_o(wog[l], pk), half_o(wog[l], pk), r * n_ici + 2 * l + 1, sibling).wait_recv()
                f = remote(half_o(wog[l], pk), half_o(wog[l], pk), base + r * n_fwd + 2 * l + 1, sibling)
                f.start()
                fwds.append(f)
            remote(cwg.at[pk], cwg.at[pk], r * n_ici + 2 * L, sibling).wait_recv()
        for r, (px, py, pk) in enumerate(chips):
            for l in range(L):
                remote(other_half_i(wig[l], pk), other_half_i(wig[l], pk), base + r * n_fwd + 2 * l, sibling).wait_recv()
                remote(other_half_o(wog[l], pk), other_half_o(wog[l], pk), base + r * n_fwd + 2 * l + 1, sibling).wait_recv()
        for cp in sends + fwds:
            cp.wait_send()
        for cp in local:
            cp.wait()

    n_sem = 3 * n_ici + 3 * n_fwd
    out_shape = ([jax.ShapeDtypeStruct((N_CHIPS, D_MODEL, COLS), BF16)] * L
                 + [jax.ShapeDtypeStruct((N_CHIPS, GROUP, D_MODEL), BF16)] * L
                 + [jax.ShapeDtypeStruct((N_CHIPS,) + cw.shape, F32)])
    outs = pl.pallas_call(
        body, name="gather_weights",
        in_specs=[ANY, ANY, ANY], out_specs=[ANY] * (2 * L + 1), out_shape=out_shape,
        scratch_shapes=[pltpu.SemaphoreType.DMA((n_sem,)), pltpu.SemaphoreType.DMA((n_sem,)),
                        pltpu.SemaphoreType.DMA((2 * L + 1,))],
        compiler_params=pltpu.CompilerParams(has_side_effects=True),
    )(wi16, wo16, cw)
    return outs[0:L], outs[L:2 * L], outs[2 * L]


def _swap_halves(l_arr, gwi, gwo, ri, ro):
    hi_rows, ho_rows = D_MODEL // 2, GROUP // 2

    def body(l_ref, gwi_ref, gwo_ref, ri_in, ro_in, ri_ref, ro_ref, send_sems, recv_sems):
        del ri_in, ro_in
        x, y, c = _place()
        l = l_ref[0]
        sibling = (x, y, 1 - c)
        cps = [
            pltpu.make_async_remote_copy(src_ref=gwi_ref.at[l, :, pl.ds((1 - c) * hi_rows, hi_rows), :],
                                         dst_ref=ri_ref.at[l], send_sem=send_sems.at[0], recv_sem=recv_sems.at[0],
                                         device_id=sibling, device_id_type=MESH),
            pltpu.make_async_remote_copy(src_ref=gwo_ref.at[l, :, pl.ds((1 - c) * ho_rows, ho_rows), :],
                                         dst_ref=ro_ref.at[l], send_sem=send_sems.at[1], recv_sem=recv_sems.at[1],
                                         device_id=sibling, device_id_type=MESH),
        ]
        for cp in cps:
            cp.start()
        for cp in cps:
            cp.wait()

    return pl.pallas_call(
        body, name="swap_halves",
        in_specs=[pl.BlockSpec(memory_space=pltpu.SMEM), ANY, ANY, ANY, ANY], out_specs=[ANY, ANY],
        out_shape=[jax.ShapeDtypeStruct(ri.shape, ri.dtype), jax.ShapeDtypeStruct(ro.shape, ro.dtype)],
        input_output_aliases={3: 0, 4: 1},
        scratch_shapes=[pltpu.SemaphoreType.DMA((2,)), pltpu.SemaphoreType.DMA((2,))],
        compiler_params=pltpu.CompilerParams(has_side_effects=True),
    )(l_arr, gwi, gwo, ri, ro)


def _add_halves(cl_arr, g, r, p, *, rows, cols, tr, name):
    nb = rows // tr

    def body(cl_ref, g_ref, r_ref, p_in, o_ref):
        del cl_ref, p_in
        o_ref[...] = (g_ref[...] + r_ref[...].astype(F32)).astype(o_ref.dtype)

    grid_spec = pltpu.PrefetchScalarGridSpec(
        num_scalar_prefetch=1, grid=(N_CHIPS, nb),
        in_specs=[pl.BlockSpec((None, None, tr, cols), lambda k, i, cl: (cl[1], k, cl[0] * nb + i, 0)),
                  pl.BlockSpec((None, None, tr, cols), lambda k, i, cl: (cl[1], k, i, 0)), ANY],
        out_specs=pl.BlockSpec((None, None, tr, cols), lambda k, i, cl: (cl[1], k, i, 0)))
    return pl.pallas_call(
        body, name=name, grid_spec=grid_spec,
        out_shape=jax.ShapeDtypeStruct(p.shape, p.dtype),
        input_output_aliases={3: 0},
        compiler_params=_vmem_params(dimension_semantics=("arbitrary",) * 2),
    )(cl_arr, g, r, p)


def _exchange_last(l_arr, p_i, p_o, r_i, r_o):
    def body(l_ref, p_i_ref, p_o_ref, ri_in, ro_in, ri_ref, ro_ref, send_sems, recv_sems):
        del ri_in, ro_in
        always = l_ref[0] >= 0
        _exchange_comm(always, always, l_ref[0], p_i_ref, p_o_ref, None, ri_ref, ro_ref, None,
                       send_sems, recv_sems, None)

    return pl.pallas_call(
        body, name="exchange_last",
        in_specs=[pl.BlockSpec(memory_space=pltpu.SMEM)] + [ANY] * 4, out_specs=[ANY] * 2,
        out_shape=[jax.ShapeDtypeStruct(r_i.shape, r_i.dtype), jax.ShapeDtypeStruct(r_o.shape, r_o.dtype)],
        input_output_aliases={3: 0, 4: 1},
        scratch_shapes=[pltpu.SemaphoreType.DMA((13,)), pltpu.SemaphoreType.DMA((13,))],
        compiler_params=pltpu.CompilerParams(has_side_effects=True),
    )(l_arr, p_i, p_o, r_i, r_o)


def _sum_small(r_sms):
    L = len(r_sms)

    def body(*refs):
        o_ref = refs[L]
        for l in range(L):
            acc = refs[l][0]
            for d in range(1, N_DEV):
                acc = acc + refs[l][d]
            o_ref[l] = acc

    return pl.pallas_call(
        body, name="sum_small",
        out_shape=jax.ShapeDtypeStruct((L,) + r_sms[0].shape[1:], F32),
        compiler_params=_vmem_params(),
    )(*r_sms)


def _sum_chunks(kc_arr, p, r, *, rows, cols, tr, name):
    L = p.shape[0]
    nb = rows // tr

    def body(kc_ref, p_ref, r0_ref, r1_ref, r2_ref, o_ref):
        del kc_ref
        f = lambda ref: ref[...].astype(F32)
        o_ref[...] = ((f(p_ref) + f(r0_ref)) + f(r1_ref)) + f(r2_ref)

    def rspec(j):
        return pl.BlockSpec((None, None, tr, cols), lambda l, i, kc, _j=j: (_j, l, i, 0))

    grid_spec = pltpu.PrefetchScalarGridSpec(
        num_scalar_prefetch=1, grid=(L, nb),
        in_specs=[pl.BlockSpec((None, None, tr, cols), lambda l, i, kc: (l, kc[0], i, 0)), rspec(0), rspec(1), rspec(2)],
        out_specs=pl.BlockSpec((None, tr, cols), lambda l, i, kc: (l, kc[1] * nb + i, 0)))
    return pl.pallas_call(
        body, name=name, grid_spec=grid_spec,
        out_shape=jax.ShapeDtypeStruct((L, 2 * rows, cols), F32),
        compiler_params=_vmem_params(dimension_semantics=("arbitrary",) * 2),
    )(kc_arr, p, r, r, r)


def _share_result(gi, go):
    hi_rows, ho_rows = gi.shape[1] // 2, go.shape[1] // 2

    def body(gi_ref, go_ref, oi_ref, oo_ref, send_sems, recv_sems):
        del gi_ref, go_ref
        x, y, c = _place()
        sibling = (x, y, 1 - c)
        cps = []
        for j, (ref, n) in enumerate(((oi_ref, hi_rows), (oo_ref, ho_rows))):
            mine = ref.at[:, pl.ds(c * n, n), :]
            cps.append(pltpu.make_async_remote_copy(src_ref=mine, dst_ref=mine, send_sem=send_sems.at[j],
                                                    recv_sem=recv_sems.at[j], device_id=sibling, device_id_type=MESH))
        for cp in cps:
            cp.start()
        for j, (ref, n) in enumerate(((oi_ref, hi_rows), (oo_ref, ho_rows))):
            theirs = ref.at[:, pl.ds((1 - c) * n, n), :]
            pltpu.make_async_remote_copy(src_ref=theirs, dst_ref=theirs, send_sem=send_sems.at[j],
                                         recv_sem=recv_sems.at[j], device_id=sibling, device_id_type=MESH).wait_recv()
        for cp in cps:
            cp.wait_send()

    return pl.pallas_call(
        body, name="share_result",
        in_specs=[ANY, ANY], out_specs=[ANY, ANY],
        out_shape=[jax.ShapeDtypeStruct(gi.shape, F32), jax.ShapeDtypeStruct(go.shape, F32)],
        input_output_aliases={0: 0, 1: 1},
        scratch_shapes=[pltpu.SemaphoreType.DMA((2,)), pltpu.SemaphoreType.DMA((2,))],
        compiler_params=pltpu.CompilerParams(has_side_effects=True),
    )(gi, go)


WEIGHTS = ("ln_g", "ln_b", "w_in", "b_in", "conv_a_w", "conv_a_b", "norm_a_g", "norm_a_b", "conv_b_w", "pool_w",
           "pool_scale", "sgu_ln_g", "sgu_ln_b", "sgu_w", "sgu_bias", "w_out", "b_out")


def _pad_rows(a, rows):
    return jnp.pad(a, ((0, rows - a.shape[0]), (0, 0)))


def _indicator_consts():
    seg = jnp.where((jnp.arange(GROUP)[:, None] // HEAD) == (jnp.arange(GROUP)[None, :] // HEAD),
                    1.0 / HEAD, 0.0).astype(BF16)
    e4 = ((jnp.arange(GROUP)[:, None] // HEAD) == jnp.arange(128)[None, :]).astype(BF16)
    return seg, e4


def _layer_consts(p, conv_full):
    L = conv_full.shape[0]
    same_head = jnp.eye(4, dtype=F32)[:, None, :, None] > 0

    def rows_to(a, rows):
        return jnp.pad(a, ((0, 0), (0, rows - a.shape[1]), (0, 0)))

    s256 = jnp.stack([p[n] for n in ("conv_a_b", "norm_a_g", "norm_a_b", "pool_scale", "sgu_ln_g", "sgu_ln_b")], axis=1)
    pw = jnp.where(same_head, p["pool_w"][:, :, :, None, :], 0.0).reshape(L, GROUP, GROUP)
    return dict(
        caw=rows_to(conv_full[:, :KA], 32), cbw=rows_to(conv_full[:, KA:], 8), s256=rows_to(s256, 8),
        pw=pw.astype(BF16),
        wm=jnp.transpose(p["sgu_w"], (0, 2, 1, 3)).reshape(L, SGU_BLOCK, 4 * SGU_BLOCK),
        wmt=jnp.transpose(p["sgu_w"], (0, 1, 3, 2)).reshape(L, 4 * SGU_BLOCK, SGU_BLOCK),
        sb=jnp.repeat(jnp.transpose(p["sgu_bias"], (0, 2, 1)), HEAD, axis=2),
        v1024=rows_to(jnp.stack([p["b_out"], p["ln_g"], p["ln_b"]], axis=1), 8),
        bin=p["b_in"][:, None, :])


def _unpack_small(sm):
    L = sm.shape[0]
    owc = jnp.concatenate([sm[:, ROW_WC:ROW_WC + SGU_BLOCK], sm[:, ROW_WC + SGU_BLOCK:ROW_WC + 2 * SGU_BLOCK]], axis=2)
    return dict(
        conv_a_b=sm[:, 0], norm_a_g=sm[:, 1], norm_a_b=sm[:, 2], pool_scale=sm[:, 3], sgu_ln_g=sm[:, 4],
        sgu_ln_b=sm[:, 5], conv_b_w=sm[:, ROW_CBW:ROW_CBW + KB], conv_a_w=sm[:, ROW_CAW:ROW_CAW + KA],
        pool_w=jnp.transpose(sm[:, ROW_PW:ROW_PW + HEAD].reshape(L, HEAD, 4, HEAD), (0, 2, 1, 3)),
        ln_g=sm[:, ROW_LNG:ROW_LNG + 4].reshape(L, D_MODEL), ln_b=sm[:, ROW_LNB:ROW_LNB + 4].reshape(L, D_MODEL),
        b_out=sm[:, ROW_BOUT:ROW_BOUT + 4].reshape(L, D_MODEL),
        b_in=sm[:, ROW_BIN:ROW_BIN + N_SLICES].reshape(L, IN_WIDTH),
        sgu_w=jnp.transpose(owc.reshape(L, SGU_BLOCK, 4, SGU_BLOCK), (0, 2, 1, 3)),
        sgu_bias=sm[:, ROW_SB:ROW_SB + 4, 0:SGU_BLOCK])


def _step(p, m, v, x, target, *, tile_f, tile_b, tk_in, tk_out):
    L = p["ln_g"].shape[0]
    xi, yi, ci = _place()
    me_k = 2 * xi + yi
    hi_rows, ho_rows = D_MODEL // 2, GROUP // 2

    cw = jnp.concatenate([p["conv_a_w"], p["conv_b_w"]], axis=1).reshape(-1, 128)
    cw_rows = cw.shape[0]
    cw = _pad_rows(cw, 72)
    wi16 = p["w_in"].astype(BF16)
    wo16 = p["w_out"].astype(BF16)
    wig0, wog0, cwg = _gather_weights(wi16[0:1], wo16[0:1], cw)
    cwg = cwg[:, :cw_rows].reshape(N_CHIPS, L, KA + KB, HEAD)
    conv_full = jnp.transpose(cwg, (1, 2, 0, 3)).reshape(L, KA + KB, GROUP)
    seg, e4 = _indicator_consts()
    k = _layer_consts(p, conv_full)
    layer = [jnp.full((1,), l, jnp.int32) for l in range(L)]

    hcur = x
    saved, wig, wog = [], [wig0[0]], [wog0[0]]
    for l in range(L):
        nxt = (wi16, wo16) if l + 1 < L else None
        outs = _fwd_layer(layer[l], hcur, wig[l], k["bin"], k["caw"], k["cbw"], k["s256"], seg, k["pw"], k["wm"], k["sb"],
                          wog[l], k["v1024"], tile=tile_f, nxt=nxt, target=None if nxt is not None else target)
        y, xb, h, aux, mixb, z = outs[0:6]
        if nxt is not None:
            wig.append(outs[6])
            wog.append(outs[7])
        saved.append((xb, h, aux, mixb, z))
        hcur = y

    dy = hcur
    loss_local = outs[6][0, 0]

    gwi = lax.empty((L, N_CHIPS, D_MODEL, COLS), F32)
    gwo = lax.empty((L, N_CHIPS, GROUP, D_MODEL), F32)
    gwi16 = lax.empty((L, N_CHIPS, D_MODEL, COLS), BF16)
    gwo16 = lax.empty((L, N_CHIPS, GROUP, D_MODEL), BF16)
    ri = lax.empty((L, N_CHIPS, hi_rows, COLS), BF16)
    ro = lax.empty((L, N_CHIPS, ho_rows, D_MODEL), BF16)
    p_i = lax.empty((L, N_CHIPS, hi_rows, COLS), BF16)
    p_o = lax.empty((L, N_CHIPS, ho_rows, D_MODEL), BF16)
    q_i = lax.empty((3, L, hi_rows, COLS), BF16)
    q_o = lax.empty((3, L, ho_rows, D_MODEL), BF16)
    r_sm = [None] * L
    pending = None
    for l in reversed(range(L)):
        xb, h, aux, mixb, z = saved[l]
        exch = None if pending is None else (p_i, p_o, pending, q_i, q_o)
        outs = _bwd_layer(layer[l], dy, z, h, aux, wig[l], k["caw"], k["cbw"], k["s256"], seg, k["pw"], k["wm"],
                          k["wmt"], k["sb"], wog[l], k["v1024"], e4, tile=tile_b, exch=exch)
        dy, dhb, dzb, osm = outs[0:4]
        if l == L - 1:
            osm = osm.at[ROW_LOSS, 0].set(loss_local)
        if exch is not None:
            q_i, q_o, r_sm[l + 1] = outs[4:7]
        larr = layer[l]
        if l > 0:
            gwi, gwi16 = _dw_in(larr, xb, dhb, gwi, gwi16, tk=tk_in)
        else:
            gwi, gwi16, r_sm[0] = _dw_in(larr, xb, dhb, gwi, gwi16, tk=tk_out, small=osm)
        gwo, gwo16 = _dw_out(larr, mixb, dzb, gwo, gwo16, tk=tk_out)
        ri, ro = _swap_halves(larr, gwi16, gwo16, ri, ro)
        cl_arr = jnp.stack([ci, jnp.int32(l)]).astype(jnp.int32)
        p_i = _add_halves(cl_arr, gwi, ri, p_i, rows=hi_rows, cols=COLS, tr=256, name="add_halves_in")
        p_o = _add_halves(cl_arr, gwo, ro, p_o, rows=ho_rows, cols=D_MODEL, tr=128, name="add_halves_out")
        pending = osm
    grad_x = dy
    q_i, q_o = _exchange_last(layer[0], p_i, p_o, q_i, q_o)

    summed = _sum_small(r_sm)
    loss = summed[L - 1, ROW_LOSS, 0]
    grads = _unpack_small(summed)
    for n in ("conv_a_w", "conv_b_w"):
        grads[n] = lax.dynamic_slice_in_dim(grads[n], me_k * HEAD, HEAD, axis=2)

    kc_arr = jnp.stack([me_k, ci]).astype(jnp.int32)
    g_i = _sum_chunks(kc_arr, p_i, q_i, rows=hi_rows, cols=COLS, tr=256, name="sum_chunks_in")
    g_o = _sum_chunks(kc_arr, p_o, q_o, rows=ho_rows, cols=D_MODEL, tr=128, name="sum_chunks_out")
    g_i, g_o = _share_result(g_i, g_o)
    grads["w_in"] = g_i
    grads["w_out"] = g_o

    delta, new_m, new_v = {}, {}, {}
    for n, tr in (("w_in", 512), ("w_out", 256)):
        shp = p[n].shape
        args = [a.reshape(shp[0] * shp[1], shp[2]) for a in (p[n], grads[n], m[n], v[n])]
        outs = _adamw(*args, rows_per_step=tr, name="adamw_" + n, copy_g=True)
        delta[n], new_m[n], new_v[n], grads[n] = (a.reshape(shp) for a in outs)
    small = [n for n in WEIGHTS if n not in ("w_in", "w_out")]
    flat = [[a[n].reshape(-1, a[n].shape[-1]) for n in small] for a in (p, grads, m, v)]
    outs = _adamw_small(*flat)
    for j, n in enumerate(small):
        delta[n], new_m[n], new_v[n] = (o[j].reshape(p[n].shape) for o in outs)

    return (loss, grad_x[None], *[grads[n] for n in WEIGHTS], *[delta[n] for n in WEIGHTS],
            *[new_m[n] for n in WEIGHTS], *[new_v[n] for n in WEIGHTS])


def kernel(x, ln_g, ln_b, w_in, b_in, conv_a_w, conv_a_b, norm_a_g, norm_a_b, conv_b_w, pool_w, pool_scale, sgu_ln_g, sgu_ln_b, sgu_w, sgu_bias, w_out, b_out, loss_target, m_ln_g, m_ln_b, m_w_in, m_b_in, m_conv_a_w, m_conv_a_b, m_norm_a_g, m_norm_a_b, m_conv_b_w, m_pool_w, m_pool_scale, m_sgu_ln_g, m_sgu_ln_b, m_sgu_w, m_sgu_bias, m_w_out, m_b_out, v_ln_g, v_ln_b, v_w_in, v_b_in, v_conv_a_w, v_conv_a_b, v_norm_a_g, v_norm_a_b, v_conv_b_w, v_pool_w, v_pool_scale, v_sgu_ln_g, v_sgu_ln_b, v_sgu_w, v_sgu_bias, v_w_out, v_b_out):
    p = dict(ln_g=ln_g, ln_b=ln_b, w_in=w_in, b_in=b_in, conv_a_w=conv_a_w, conv_a_b=conv_a_b, norm_a_g=norm_a_g,
             norm_a_b=norm_a_b, conv_b_w=conv_b_w, pool_w=pool_w, pool_scale=pool_scale, sgu_ln_g=sgu_ln_g,
             sgu_ln_b=sgu_ln_b, sgu_w=sgu_w, sgu_bias=sgu_bias, w_out=w_out, b_out=b_out)
    m = dict(ln_g=m_ln_g, ln_b=m_ln_b, w_in=m_w_in, b_in=m_b_in, conv_a_w=m_conv_a_w, conv_a_b=m_conv_a_b,
             norm_a_g=m_norm_a_g, norm_a_b=m_norm_a_b, conv_b_w=m_conv_b_w, pool_w=m_pool_w, pool_scale=m_pool_scale,
             sgu_ln_g=m_sgu_ln_g, sgu_ln_b=m_sgu_ln_b, sgu_w=m_sgu_w, sgu_bias=m_sgu_bias, w_out=m_w_out, b_out=m_b_out)
    v = dict(ln_g=v_ln_g, ln_b=v_ln_b, w_in=v_w_in, b_in=v_b_in, conv_a_w=v_conv_a_w, conv_a_b=v_conv_a_b,
             norm_a_g=v_norm_a_g, norm_a_b=v_norm_a_b, conv_b_w=v_conv_b_w, pool_w=v_pool_w, pool_scale=v_pool_scale,
             sgu_ln_g=v_sgu_ln_g, sgu_ln_b=v_sgu_ln_b, sgu_w=v_sgu_w, sgu_bias=v_sgu_bias, w_out=v_w_out, b_out=v_b_out)
    return _step(p, m, v, x[0], loss_target[0], tile_f=256, tile_b=256, tk_in=4096, tk_out=2048)
```

```python
import jax
import jax.numpy as jnp
from jax import lax
from jax.experimental import pallas as pl
from jax.experimental.pallas import tpu as pltpu

F32 = jnp.float32
BF16 = jnp.bfloat16
MESH = pl.DeviceIdType.MESH

D_MODEL = 1024
GROUP = 256
HEAD = 64
N_SLICES = 12
IN_WIDTH = N_SLICES * GROUP
N_CHIPS = 4
COLS = IN_WIDTH // N_CHIPS
KA = 31
KB = 3
SUBLANES = 8
HALO_A, HALO_B, HALO_C = 32, 8, 16
N_GATHER_SEMS = 12
N_EXCH_SEMS = 13
SGU_BLOCK = 128
CHUNK = 64
LN_EPS = 1e-5
ROWS = 64
V7X_VMEM_BYTES = 64 * 1024 * 1024
VMEM_LIMIT = V7X_VMEM_BYTES - 8 * 1024 * 1024

ADAM_LR, ADAM_B1, ADAM_B2, ADAM_EPS, ADAM_WD, ADAM_STEP = 0.001, 0.9, 0.999, 1e-08, 0.01, 10


ANY = pl.BlockSpec(memory_space=pl.ANY)


def _vmem_params(**kw):
    return pltpu.CompilerParams(vmem_limit_bytes=VMEM_LIMIT, **kw)


def _whole(a):
    return pl.BlockSpec(a.shape, lambda i, l, _n=a.ndim: (0,) * _n)


def _of_layer(a):
    return pl.BlockSpec((None,) + a.shape[1:], lambda i, l, _n=a.ndim: (l[0],) + (0,) * (_n - 1))


def _place():
    return lax.axis_index("x"), lax.axis_index("y"), lax.axis_index("c")


def _other_chips(x, y):
    return [(1 - x, y, 2 * (1 - x) + y), (x, 1 - y, 2 * x + (1 - y)), (1 - x, 1 - y, 2 * (1 - x) + (1 - y))]


def _sig(v):
    return 0.5 * jnp.tanh(0.5 * v) + 0.5


def _dot(a, b):
    return jnp.dot(a, b, preferred_element_type=F32)


def _dot_nt(a, b):
    return lax.dot_general(a, b, (((1,), (1,)), ((), ())), preferred_element_type=F32)


def _dot_tn(a, b):
    return lax.dot_general(a, b, (((0,), (0,)), ((), ())), preferred_element_type=F32)


def _segdot(v, m):
    hi = v.astype(BF16)
    lo = (v - hi.astype(F32)).astype(BF16)
    return _dot(hi, m) + _dot(lo, m)


def _colsum(v):
    return jnp.sum(v, axis=0, keepdims=True)


def _rowmean(v):
    return jnp.mean(v, axis=-1, keepdims=True)


def _lane_group(n):
    return lax.broadcasted_iota(jnp.int32, (1, n), 1) // HEAD


def _pool_cnt(tile, t_rows):
    pos = tile * t_rows + lax.broadcasted_iota(jnp.int32, (t_rows, GROUP), 0) + 1
    grp = lax.broadcasted_iota(jnp.int32, (t_rows, GROUP), 1) // HEAD
    win = jnp.where(grp == 0, 2, jnp.where(grp == 1, 4, jnp.where(grp == 2, 8, 16)))
    return jnp.minimum(pos, win).astype(F32)


def _sgu_masks(wm_ref, wmt_ref, wm_s, wmt_s):
    r = lax.broadcasted_iota(jnp.int32, (SGU_BLOCK, 4 * SGU_BLOCK), 0) // CHUNK
    c = (lax.broadcasted_iota(jnp.int32, (SGU_BLOCK, 4 * SGU_BLOCK), 1) % SGU_BLOCK) // CHUNK
    wm_s[...] = jnp.where(c <= r, wm_ref[...], 0.0).astype(BF16)
    if wmt_ref is not None:
        rt = (lax.broadcasted_iota(jnp.int32, (4 * SGU_BLOCK, SGU_BLOCK), 0) % SGU_BLOCK) // CHUNK
        ct = lax.broadcasted_iota(jnp.int32, (4 * SGU_BLOCK, SGU_BLOCK), 1) // CHUNK
        wmt_s[...] = jnp.where(rt <= ct, wmt_ref[...], 0.0).astype(BF16)


def _vstack(v_blk):
    grp = _lane_group(GROUP)
    return jnp.concatenate([jnp.where(grp == h, v_blk, 0.0) for h in range(4)], axis=0).astype(BF16)


def _gather_next(step, nt, nwi, nwo, gwi, gwo, send_sems, recv_sems, loc_sems):
    x, y, c = _place()
    me_k = 2 * x + y
    sibling = (x, y, 1 - c)
    chips = _other_chips(x, y)
    hi, ho = D_MODEL // 2, GROUP // 2
    fwd_sems = N_GATHER_SEMS // 2

    def rc(src, dst, sem, to):
        return pltpu.make_async_remote_copy(src_ref=src, dst_ref=dst, send_sem=send_sems.at[sem],
                                            recv_sem=recv_sems.at[sem], device_id=to, device_id_type=MESH)

    def blk(ref, k, n, cc):
        return ref.at[k, pl.ds(cc * n, n), :]

    def ici(r):
        px, py, _ = chips[r]
        to = (px, py, c)
        return [rc(nwi.at[pl.ds(c * hi, hi), :], blk(gwi, me_k, hi, c), 2 * r, to),
                rc(nwo.at[pl.ds(c * ho, ho), :], blk(gwo, me_k, ho, c), 2 * r + 1, to)]

    def landed(r, cc, base):
        pk = chips[r][2]
        return [rc(blk(gwi, pk, hi, cc), blk(gwi, pk, hi, cc), base + 2 * r, sibling),
                rc(blk(gwo, pk, ho, cc), blk(gwo, pk, ho, cc), base + 2 * r + 1, sibling)]

    def local():
        return [pltpu.make_async_copy(nwi, gwi.at[me_k], loc_sems.at[0]),
                pltpu.make_async_copy(nwo, gwo.at[me_k], loc_sems.at[1])]

    @pl.when(step == 0)
    def _():
        for cp in local():
            cp.start()
        for r in range(3):
            for cp in ici(r):
                cp.start()

    @pl.when(step == (3 * nt) // 4)
    def _():
        for r in range(3):
            for got, fwd in zip(landed(r, c, 0), landed(r, c, fwd_sems)):
                got.wait_recv()
                fwd.start()

    @pl.when(step == nt - 1)
    def _():
        for r in range(3):
            for got in landed(r, 1 - c, fwd_sems):
                got.wait_recv()
        for r in range(3):
            for cp in ici(r) + landed(r, c, fwd_sems):
                cp.wait_send()
        for cp in local():
            cp.wait()


def _fwd_layer(larr, x, wi, bin_, caw, cbw, s256, seg, pw, wm, sb, wo, v1024, *, tile, nxt=None, target=None):
    assert nxt is None or target is None
    S = x.shape[0]
    T = tile
    nt = S // T
    alpha = float((2.0 * 4) ** 0.25)
    n_in = 13 + (2 if nxt is not None else 0) + (1 if target is not None else 0)
    n_out = 6 + (2 if nxt is not None else 0) + (1 if target is not None else 0)

    def body(*refs):
        l_ref = refs[0]
        (x_ref, wi_ref, bin_ref, caw_ref, cbw_ref, s256_ref, seg_ref, pw_ref, wm_ref, sb_ref, wo_ref,
         v1024_ref) = refs[1:13]
        y_ref, xb_ref, h_ref, aux_ref, mix_ref, z_ref = refs[n_in:n_in + 6]
        abuf, bbuf, cbuf, wm_s, shf = refs[n_in + n_out:n_in + n_out + 5]
        i = pl.program_id(0)
        if nxt is not None:
            _gather_next(i, nt, refs[13].at[l_ref[0] + 1], refs[14].at[l_ref[0] + 1], refs[n_in + 6], refs[n_in + 7],
                         *refs[n_in + n_out + 5:])

        @pl.when(i == 0)
        def _():
            abuf[0:HALO_A, :] = jnp.zeros((HALO_A, GROUP), F32)
            bbuf[0:HALO_B, :] = jnp.zeros((HALO_B, GROUP), F32)
            cbuf[0:HALO_C, :] = jnp.zeros((HALO_C, GROUP), F32)
            _sgu_masks(wm_ref, None, wm_s, None)

        x = x_ref[...]
        xb = x.astype(BF16)
        xb_ref[...] = xb
        for k in range(N_CHIPS):
            h_ref[:, COLS * k:COLS * (k + 1)] = _dot(xb, wi_ref[k]) + bin_ref[:, COLS * k:COLS * (k + 1)]

        def hs(j):
            return h_ref[:, GROUP * j:GROUP * (j + 1)]

        abuf[HALO_A:HALO_A + T, :] = hs(0) * _sig(hs(1))
        span = T + HALO_A - SUBLANES
        for p in range(1, SUBLANES):
            shf[p - 1, :, :] = abuf[p:p + span, :]
        for r0 in range(0, T, ROWS):
            acc = None
            for k in range(KA):
                off = HALO_A - (KA - 1) + k
                p, q8 = off % SUBLANES, off - off % SUBLANES
                win = abuf[r0 + q8:r0 + q8 + ROWS, :] if p == 0 else shf[p - 1, r0 + q8:r0 + q8 + ROWS, :]
                term = caw_ref[k:k + 1, :] * win
                acc = term if acc is None else acc + term
            aux_ref[r0:r0 + ROWS, 0:GROUP] = acc + s256_ref[0:1, :]
        abuf[0:HALO_A, :] = abuf[T:T + HALO_A, :]
        a1 = aux_ref[:, 0:GROUP]
        segm = seg_ref[...]
        cen = a1 - _segdot(a1, segm)
        var = _segdot(cen * cen, segm)
        a2 = cen * lax.rsqrt(var + LN_EPS) * s256_ref[1:2, :] + s256_ref[2:3, :]
        az = hs(2)
        mix_ref[:, 0:GROUP] = (a2 * _sig(a2) * (az * _sig(az))).astype(BF16)

        bbuf[HALO_B:HALO_B + T, :] = hs(4) * hs(5)
        for r0 in range(0, T, ROWS):
            acc = None
            for k in range(KB):
                off = HALO_B - (KB - 1) + k + r0
                term = cbw_ref[k:k + 1, :] * bbuf[off:off + ROWS, :]
                acc = term if acc is None else acc + term
            aux_ref[r0:r0 + ROWS, GROUP:2 * GROUP] = acc
        bbuf[0:HALO_B, :] = bbuf[T:T + HALO_B, :]
        bz = hs(6)
        mix_ref[:, GROUP:2 * GROUP] = (hs(3) * aux_ref[:, GROUP:2 * GROUP] * (bz * _sig(bz))).astype(BF16)

        ch = hs(7)
        cbuf[HALO_C:HALO_C + T, :] = ch
        hi_lane = (lax.broadcasted_iota(jnp.int32, (1, 128), 1) // HEAD) == 1
        for r0 in range(0, T, ROWS):
            def win(col, j0, j1):
                s = None
                for j in range(j0, j1):
                    off = HALO_C - j + r0
                    term = cbuf[off:off + ROWS, 128 * col:128 * (col + 1)]
                    s = term if s is None else s + term
                return s
            w0 = win(0, 0, 2) + jnp.where(hi_lane, win(0, 2, 4), 0.0)
            w1 = win(1, 0, 8) + jnp.where(hi_lane, win(1, 8, 16), 0.0)
            aux_ref[r0:r0 + ROWS, 2 * GROUP:2 * GROUP + 128] = w0
            aux_ref[r0:r0 + ROWS, 2 * GROUP + 128:3 * GROUP] = w1
        cbuf[0:HALO_C, :] = cbuf[T:T + HALO_C, :]
        pooled = aux_ref[:, 2 * GROUP:3 * GROUP] / _pool_cnt(i, T) - ch
        aux_ref[:, 2 * GROUP:3 * GROUP] = pooled
        q = _dot(pooled.astype(BF16), pw_ref[...])
        cz = hs(8)
        mix_ref[:, 2 * GROUP:3 * GROUP] = (q * s256_ref[3:4, :] * (cz * _sig(cz))).astype(BF16)

        dv = hs(10)
        cen = dv - _rowmean(dv)
        var = _rowmean(cen * cen)
        v = cen * lax.rsqrt(var + LN_EPS) * s256_ref[4:5, :] + s256_ref[5:6, :]
        sps = []
        for n in range(T // SGU_BLOCK):
            vb = v[n * SGU_BLOCK:(n + 1) * SGU_BLOCK, :]
            sps.append(_dot(wm_s[...], _vstack(vb)) + sb_ref[...])
        sp = jnp.concatenate(sps, axis=0)
        dz = hs(11)
        mix_ref[:, 3 * GROUP:4 * GROUP] = (hs(9) * sp * (dz * _sig(dz))).astype(BF16)

        out = v1024_ref[0:1, :]
        for k in range(N_CHIPS):
            out = out + _dot(mix_ref[:, GROUP * k:GROUP * (k + 1)], wo_ref[k])
        z = alpha * x + out
        z_ref[...] = z
        cen = z - _rowmean(z)
        var = _rowmean(cen * cen)
        y = cen * lax.rsqrt(var + LN_EPS) * v1024_ref[1:2, :] + v1024_ref[2:3, :]
        if target is None:
            y_ref[...] = y
        else:
            t_ref, loss_ref = refs[13], refs[n_in + 6]

            @pl.when(i == 0)
            def _():
                loss_ref[...] = jnp.zeros_like(loss_ref)
            err = y - t_ref[...]
            y_ref[...] = err * (1.0 / D_MODEL)
            loss_ref[...] += jnp.sum(_colsum(err * err), axis=1, keepdims=True) * (0.5 / D_MODEL)

    def rows(width):
        return pl.BlockSpec((T, width), lambda i, l: (i, 0))

    consts = (wi, bin_, caw, cbw, s256, seg, pw, wm, sb, wo, v1024)
    in_specs = [rows(D_MODEL)] + [_whole(a) if a is wi or a is seg or a is wo else _of_layer(a) for a in consts]
    out_specs = [rows(D_MODEL), rows(D_MODEL), rows(IN_WIDTH), rows(3 * GROUP), rows(D_MODEL), rows(D_MODEL)]
    out_shape = [jax.ShapeDtypeStruct((S, D_MODEL), F32), jax.ShapeDtypeStruct((S, D_MODEL), BF16),
                 jax.ShapeDtypeStruct((S, IN_WIDTH), F32), jax.ShapeDtypeStruct((S, 3 * GROUP), F32),
                 jax.ShapeDtypeStruct((S, D_MODEL), BF16), jax.ShapeDtypeStruct((S, D_MODEL), F32)]
    scratch = [pltpu.VMEM((T + HALO_A, GROUP), F32), pltpu.VMEM((T + HALO_B, GROUP), F32),
               pltpu.VMEM((T + HALO_C, GROUP), F32), pltpu.VMEM((SGU_BLOCK, 4 * SGU_BLOCK), BF16),
               pltpu.VMEM((SUBLANES - 1, T + HALO_A - SUBLANES, GROUP), F32)]
    extra = ()
    if nxt is not None:
        extra = tuple(nxt)
        in_specs += [ANY, ANY]
        out_specs += [ANY, ANY]
        out_shape += [jax.ShapeDtypeStruct((N_CHIPS, D_MODEL, COLS), BF16),
                      jax.ShapeDtypeStruct((N_CHIPS, GROUP, D_MODEL), BF16)]
        scratch += [pltpu.SemaphoreType.DMA((N_GATHER_SEMS,)), pltpu.SemaphoreType.DMA((N_GATHER_SEMS,)),
                    pltpu.SemaphoreType.DMA((2,))]
    if target is not None:
        extra = (target,)
        in_specs += [rows(D_MODEL)]
        out_specs += [pl.BlockSpec((8, 128), lambda i, l: (0, 0))]
        out_shape += [jax.ShapeDtypeStruct((8, 128), F32)]
    grid_spec = pltpu.PrefetchScalarGridSpec(num_scalar_prefetch=1, grid=(nt,), in_specs=in_specs,
                                             out_specs=out_specs, scratch_shapes=scratch)
    return pl.pallas_call(
        body, name=("fwd_layer_loss" if target is not None else "fwd_layer") if nxt is None else "fwd_layer_gather",
        grid_spec=grid_spec, out_shape=out_shape,
        compiler_params=_vmem_params(dimension_semantics=("arbitrary",), has_side_effects=nxt is not None),
    )(larr, x, *consts, *extra)


ROW_CBW = 8
ROW_CAW = 16
ROW_LOSS = 7
ROW_PW = 48
ROW_LNG = 112
ROW_LNB = 116
ROW_BOUT = 120
ROW_BIN = 124
ROW_WC = 136
ROW_SB = 392
SM_ROWS = 400
N_DEV = 8


def _exchange_comm(start, finish, l, p_i, p_o, sm, r_i, r_o, r_sm, send_sems, recv_sems, loc_sem):
    x, y, c = _place()
    me = 4 * x + 2 * y + c
    chips = _other_chips(x, y)

    def rc(src, dst, sem, to):
        return pltpu.make_async_remote_copy(src_ref=src, dst_ref=dst, send_sem=send_sems.at[sem],
                                            recv_sem=recv_sems.at[sem], device_id=to, device_id_type=MESH)

    def big(r):
        px, py, pk = chips[r]
        to = (px, py, c)
        return [rc(p_i.at[l, pk], r_i.at[r, l], 2 * r, to), rc(p_o.at[l, pk], r_o.at[r, l], 2 * r + 1, to)]

    def peer(rel):
        px = 1 - x if rel & 4 else x
        py = 1 - y if rel & 2 else y
        pc = 1 - c if rel & 1 else c
        return (px, py, pc), 4 * px + 2 * py + pc

    def small_out(rel):
        to, _ = peer(rel)
        return rc(sm, r_sm.at[me], N_EXCH_SEMS - N_DEV + rel, to)

    def small_in(rel):
        to, idx = peer(rel)
        return rc(sm, r_sm.at[idx], N_EXCH_SEMS - N_DEV + rel, to)

    def local():
        return pltpu.make_async_copy(sm, r_sm.at[me], loc_sem.at[0])

    with_big, with_small = p_i is not None, sm is not None

    @pl.when(start)
    def _():
        if with_small:
            local().start()
        if with_big:
            for r in range(3):
                for cp in big(r):
                    cp.start()
        if with_small:
            for rel in range(1, N_DEV):
                small_out(rel).start()

    @pl.when(finish)
    def _():
        if with_big:
            for r in range(3):
                for cp in big(r):
                    cp.wait()
        if with_small:
            for rel in range(1, N_DEV):
                small_in(rel).wait_recv()
                small_out(rel).wait_send()
            local().wait()


RC = 32
RC_WIDE = 16
ACC_ROWS = 136


def _rsum8(v):
    r = v[0:8]
    for j in range(1, v.shape[0] // 8):
        r = r + v[8 * j:8 * j + 8]
    return r


def _bwd_layer(larr, dy, z, h, aux, wi, caw, cbw, s256, seg, pw, wm, wmt, sb, wo, v1024, e4, *, tile, exch=None):
    S = dy.shape[0]
    T = tile
    nt = S // T
    nblk = T // SGU_BLOCK
    alpha = float((2.0 * 4) ** 0.25)
    n_in = 17 + (5 if exch is not None else 0)
    n_out = 4 + (3 if exch is not None else 0)
    slab = pltpu.VMEM((T, GROUP), F32)
    scratch = dict(
        dbuf=pltpu.VMEM((T + HALO_A, GROUP), F32), ebuf=pltpu.VMEM((T + HALO_B, GROUP), F32),
        fbuf=pltpu.VMEM((T + HALO_C, GROUP), F32), sh=pltpu.VMEM((SUBLANES - 1, T + HALO_A - SUBLANES, GROUP), F32),
        wm_s=pltpu.VMEM((SGU_BLOCK, 4 * SGU_BLOCK), BF16), wmt_s=pltpu.VMEM((4 * SGU_BLOCK, SGU_BLOCK), BF16),
        dsp_acc=pltpu.VMEM((SGU_BLOCK, GROUP), F32), pw_acc=pltpu.VMEM((GROUP, GROUP), F32),
        acc_s=pltpu.VMEM((8 * ACC_ROWS, GROUP), F32), acc_w=pltpu.VMEM((24, D_MODEL), F32),
        dmix_s=pltpu.VMEM((T, D_MODEL), F32), vst_s=pltpu.VMEM((nblk, 4 * SGU_BLOCK, GROUP), BF16),
        dq_s=pltpu.VMEM((T, GROUP), BF16), dxt_s=pltpu.VMEM((D_MODEL, T), F32),
        mean_s=slab, t1_s=slab, t2_s=slab, q_s=slab, xv_s=slab, rv_s=slab, v_s=slab, sp_s=slab, a0_s=slab, sg_s=slab,
        xh_s=slab, ra_s=slab, ub_s=slab, dsp_s=slab, m1_s=slab, m2_s=slab, dpool_s=slab, dvd_s=slab, u_s=slab,
        du_s=slab, cw_s=slab)
    names = list(scratch)

    def body(*refs):
        (dy_ref, z_ref, h_ref, aux_ref, wi_ref, caw_ref, cbw_ref, s256_ref, seg_ref, pw_ref, wm_ref, wmt_ref,
         sb_ref, wo_ref, v1024_ref, e4_ref) = refs[1:17]
        dx_ref, dhb_ref, dzb_ref, osm_ref = refs[n_in:n_in + 4]
        k0 = n_in + n_out
        sc = dict(zip(names, refs[k0:k0 + len(names)]))
        dbuf, ebuf, fbuf, sh = sc["dbuf"], sc["ebuf"], sc["fbuf"], sc["sh"]
        wm_s, wmt_s, dsp_acc, pw_acc, acc_s, acc_w = (sc[n] for n in ("wm_s", "wmt_s", "dsp_acc", "pw_acc", "acc_s",
                                                                        "acc_w"))
        dmix_s, vst_s, dq_s = sc["dmix_s"], sc["vst_s"], sc["dq_s"]
        i = pl.program_id(0)
        tile_idx = nt - 1 - i
        if exch is not None:
            p_i, p_o, sm = refs[17:20]
            r_i, r_o, r_sm = refs[n_in + 4:n_in + 7]
            _exchange_comm(i == 0, i == nt - 1, refs[0][0] + 1, p_i, p_o, sm, r_i, r_o, r_sm, *refs[k0 + len(names):])

        @pl.when(i == 0)
        def _():
            dbuf[T:T + HALO_A, :] = jnp.zeros((HALO_A, GROUP), F32)
            ebuf[T:T + HALO_B, :] = jnp.zeros((HALO_B, GROUP), F32)
            fbuf[T:T + HALO_C, :] = jnp.zeros((HALO_C, GROUP), F32)
            _sgu_masks(wm_ref, wmt_ref, wm_s, wmt_s)
            osm_ref[...] = jnp.zeros_like(osm_ref)
            dsp_acc[...] = jnp.zeros_like(dsp_acc)
            pw_acc[...] = jnp.zeros_like(pw_acc)
            acc_s[...] = jnp.zeros_like(acc_s)
            acc_w[...] = jnp.zeros_like(acc_w)

        def chunks(rc, fn):
            for c in range(T // rc):
                fn(pl.ds(c * rc, rc))

        def hs(j, rows):
            return h_ref[rows, GROUP * j:GROUP * (j + 1)]

        def acc_add(row, val):
            acc_s[8 * row:8 * row + 8, :] += _rsum8(val)

        def put_dh(j, rows, val):
            acc_add(ROW_BIN + j, val)
            dhb_ref[rows, GROUP * j:GROUP * (j + 1)] = val.astype(BF16)

        def dsilu(v, s):
            return s * (1.0 + v * (1.0 - s))

        def vec(r):
            return s256_ref[r:r + 1, :]

        def ln_bwd(rows):
            dyc = dy_ref[rows, :]
            zc = z_ref[rows, :]
            cen = zc - _rowmean(zc)
            rstd = lax.rsqrt(_rowmean(cen * cen) + LN_EPS)
            xhat = cen * rstd
            acc_w[0:8, :] += _rsum8(dyc * xhat)
            acc_w[8:16, :] += _rsum8(dyc)
            gdy = dyc * v1024_ref[1:2, :]
            dz = rstd * (gdy - _rowmean(gdy) - xhat * _rowmean(gdy * xhat))
            acc_w[16:24, :] += _rsum8(dz)
            dzb_ref[rows, :] = dz.astype(BF16)
            dx_ref[rows, :] = alpha * dz
        chunks(RC_WIDE, ln_bwd)

        segm = seg_ref[...]
        dzb = dzb_ref[...]
        for k in range(N_CHIPS):
            dmix_s[:, GROUP * k:GROUP * (k + 1)] = _dot_nt(dzb, wo_ref[k])
        sc["mean_s"][...] = _segdot(aux_ref[:, 0:GROUP], segm)
        pooled_b = aux_ref[:, 2 * GROUP:3 * GROUP].astype(BF16)
        sc["q_s"][...] = _dot(pooled_b, pw_ref[...])

        def centre(rows):
            cen = aux_ref[rows, 0:GROUP] - sc["mean_s"][rows, :]
            sc["t1_s"][rows, :] = cen * cen
            dv_in = hs(10, rows)
            cen_v = dv_in - _rowmean(dv_in)
            rstd_v = lax.rsqrt(_rowmean(cen_v * cen_v) + LN_EPS)
            xv = cen_v * rstd_v
            sc["xv_s"][rows, :] = xv
            sc["rv_s"][rows, :] = jnp.broadcast_to(rstd_v, xv.shape)
            sc["v_s"][rows, :] = xv * vec(4) + vec(5)
        chunks(RC, centre)

        sc["t2_s"][...] = _segdot(sc["t1_s"][...], segm)
        for n in range(nblk):
            blk = slice(n * SGU_BLOCK, (n + 1) * SGU_BLOCK)
            vst_s[n] = _vstack(sc["v_s"][blk, :])
            sc["sp_s"][blk, :] = _dot(wm_s[...], vst_s[n]) + sb_ref[...]

        def mixers(rows):
            a_val, a_glu, a_z = hs(0, rows), hs(1, rows), hs(2, rows)
            sg = _sig(a_glu)
            sc["a0_s"][rows, :] = a_val * sg
            sc["sg_s"][rows, :] = sg
            rstd_a = lax.rsqrt(sc["t2_s"][rows, :] + LN_EPS)
            xh = (aux_ref[rows, 0:GROUP] - sc["mean_s"][rows, :]) * rstd_a
            a2 = xh * vec(1) + vec(2)
            s2 = _sig(a2)
            sz = _sig(a_z)
            dya = dmix_s[rows, 0:GROUP]
            put_dh(2, rows, dya * (a2 * s2) * dsilu(a_z, sz))
            d_a2 = dya * (a_z * sz) * dsilu(a2, s2)
            acc_add(1, d_a2 * xh)
            acc_add(2, d_a2)
            gd = d_a2 * vec(1)
            sc["t1_s"][rows, :] = gd
            sc["t2_s"][rows, :] = gd * xh
            sc["xh_s"][rows, :] = xh
            sc["ra_s"][rows, :] = rstd_a
            b_b, b_c, b_h, b_z = hs(3, rows), hs(4, rows), hs(5, rows), hs(6, rows)
            cb = aux_ref[rows, GROUP:2 * GROUP]
            sz = _sig(b_z)
            dyb = dmix_s[rows, GROUP:2 * GROUP]
            put_dh(3, rows, dyb * cb * (b_z * sz))
            put_dh(6, rows, dyb * b_b * cb * dsilu(b_z, sz))
            ebuf[rows, :] = dyb * b_b * (b_z * sz)
            sc["ub_s"][rows, :] = b_c * b_h
            c_z = hs(8, rows)
            q = sc["q_s"][rows, :]
            sz = _sig(c_z)
            dyc = dmix_s[rows, 2 * GROUP:3 * GROUP]
            acc_add(3, dyc * q * (c_z * sz))
            put_dh(8, rows, dyc * q * vec(3) * dsilu(c_z, sz))
            dq_s[rows, :] = (dyc * vec(3) * (c_z * sz)).astype(BF16)
            d_u, d_z = hs(9, rows), hs(11, rows)
            sp = sc["sp_s"][rows, :]
            sz = _sig(d_z)
            dyd = dmix_s[rows, 3 * GROUP:4 * GROUP]
            put_dh(9, rows, dyd * sp * (d_z * sz))
            put_dh(11, rows, dyd * d_u * sp * dsilu(d_z, sz))
            sc["dsp_s"][rows, :] = dyd * d_u * (d_z * sz)
        chunks(RC, mixers)

        sc["m1_s"][...] = _segdot(sc["t1_s"][...], segm)
        sc["m2_s"][...] = _segdot(sc["t2_s"][...], segm)
        d_q = dq_s[...]
        pw_acc[...] += _dot_tn(pooled_b, d_q)
        sc["dpool_s"][...] = _dot_nt(d_q, pw_ref[...])
        grp = _lane_group(GROUP)
        for n in range(nblk):
            blk = slice(n * SGU_BLOCK, (n + 1) * SGU_BLOCK)
            dspb = sc["dsp_s"][blk, :]
            dsp_acc[...] += dspb
            dspb16 = dspb.astype(BF16)
            dvst = _dot(wmt_s[...], dspb16)
            dvb = None
            for hh in range(4):
                part = jnp.where(grp == hh, dvst[hh * SGU_BLOCK:(hh + 1) * SGU_BLOCK, :], 0.0)
                dvb = part if dvb is None else dvb + part
            sc["dvd_s"][blk, :] = dvb
            dwc = _dot_nt(dspb16, vst_s[n])
            osm_ref[ROW_WC:ROW_WC + SGU_BLOCK, :] += dwc[:, 0:GROUP]
            osm_ref[ROW_WC + SGU_BLOCK:ROW_WC + 2 * SGU_BLOCK, :] += dwc[:, GROUP:2 * GROUP]

        def ln_sums(rows):
            xh = sc["xh_s"][rows, :]
            d_a1 = sc["ra_s"][rows, :] * (sc["t1_s"][rows, :] - sc["m1_s"][rows, :] - xh * sc["m2_s"][rows, :])
            acc_add(0, d_a1)
            dbuf[rows, :] = d_a1
            pos = tile_idx * T + rows.start + lax.broadcasted_iota(jnp.int32, (RC, GROUP), 0) + 1
            lane = lax.broadcasted_iota(jnp.int32, (RC, GROUP), 1) // HEAD
            win = jnp.where(lane == 0, 2, jnp.where(lane == 1, 4, jnp.where(lane == 2, 8, 16)))
            fbuf[rows, :] = sc["dpool_s"][rows, :] / jnp.minimum(pos, win).astype(F32)
            d_v = sc["dvd_s"][rows, :]
            xv = sc["xv_s"][rows, :]
            acc_add(4, d_v * xv)
            acc_add(5, d_v)
            gd = d_v * vec(4)
            put_dh(10, rows, sc["rv_s"][rows, :] * (gd - _rowmean(gd) - xv * _rowmean(gd * xv)))
        chunks(RC, ln_sums)

        span = T + HALO_A - SUBLANES
        for p in range(1, SUBLANES):
            sh[p - 1, :, :] = dbuf[p:p + span, :]

        for r0 in range(0, T, ROWS):
            uc = sc["ub_s"][r0:r0 + ROWS, :]
            acc = None
            for k in range(KB):
                off = (KB - 1) - k + r0
                w = ebuf[off:off + ROWS, :]
                term = cbw_ref[k:k + 1, :] * w
                acc = term if acc is None else acc + term
                acc_add(ROW_CBW + k, uc * w)
            sc["du_s"][r0:r0 + ROWS, :] = acc
        ebuf[T:T + HALO_B, :] = ebuf[0:HALO_B, :]

        hi_lane = (lax.broadcasted_iota(jnp.int32, (1, 128), 1) // HEAD) == 1
        for r0 in range(0, T, ROWS):
            def win(col, j0, j1):
                s = None
                for j in range(j0, j1):
                    term = fbuf[r0 + j:r0 + j + ROWS, 128 * col:128 * (col + 1)]
                    s = term if s is None else s + term
                return s
            sc["cw_s"][r0:r0 + ROWS, 0:128] = win(0, 0, 2) + jnp.where(hi_lane, win(0, 2, 4), 0.0)
            sc["cw_s"][r0:r0 + ROWS, 128:256] = win(1, 0, 8) + jnp.where(hi_lane, win(1, 8, 16), 0.0)
        fbuf[T:T + HALO_C, :] = fbuf[0:HALO_C, :]

        def rest_bc(rows):
            d_u = sc["du_s"][rows, :]
            put_dh(4, rows, d_u * hs(5, rows))
            put_dh(5, rows, d_u * hs(4, rows))
            put_dh(7, rows, sc["cw_s"][rows, :] - sc["dpool_s"][rows, :])
        chunks(RC, rest_bc)

        dxt_s = sc["dxt_s"]

        def dx_term(k):
            term = _dot_nt(wi_ref[k], dhb_ref[:, COLS * k:COLS * (k + 1)])
            if k == 1:
                dxt_s[...] = term
            else:
                dxt_s[...] += term

        def conv_a(rows):
            a0c = sc["a0_s"][rows, :]
            acc = None
            for k in range(KA):
                off = (KA - 1) - k
                p, q8 = off % SUBLANES, off - off % SUBLANES
                w = dbuf[pl.ds(rows.start + q8, RC), :] if p == 0 else sh[p - 1, pl.ds(rows.start + q8, RC), :]
                term = caw_ref[k:k + 1, :] * w
                acc = term if acc is None else acc + term
                acc_add(ROW_CAW + k, a0c * w)
            sc["u_s"][rows, :] = acc
        n_chunks = T // RC
        after = {(n_chunks * j) // 3: j + 1 for j in range(3)}
        for c in range(n_chunks):
            conv_a(pl.ds(c * RC, RC))
            if c in after:
                dx_term(after[c])
        dbuf[T:T + HALO_A, :] = dbuf[0:HALO_A, :]

        def rest_a(rows):
            d_a0 = sc["u_s"][rows, :]
            sg = sc["sg_s"][rows, :]
            put_dh(0, rows, d_a0 * sg)
            put_dh(1, rows, d_a0 * hs(0, rows) * sg * (1.0 - sg))
        chunks(RC, rest_a)
        dx_term(0)
        dx_ref[...] += dxt_s[...].T

        @pl.when(i == nt - 1)
        def _():
            for row in list(range(6)) + list(range(ROW_CBW, ROW_CBW + KB)) + list(range(ROW_CAW, ROW_CAW + KA)) + list(
                    range(ROW_BIN, ROW_BIN + N_SLICES)):
                osm_ref[row:row + 1, :] = _colsum(acc_s[8 * row:8 * row + 8, :])
            for j, row in enumerate((ROW_LNG, ROW_LNB, ROW_BOUT)):
                cs = _colsum(acc_w[8 * j:8 * j + 8, :])
                for q in range(D_MODEL // GROUP):
                    osm_ref[row + q:row + q + 1, :] = cs[:, GROUP * q:GROUP * (q + 1)]
            r = lax.broadcasted_iota(jnp.int32, (SGU_BLOCK, GROUP), 0) // CHUNK
            c = (lax.broadcasted_iota(jnp.int32, (SGU_BLOCK, GROUP), 1) % SGU_BLOCK) // CHUNK
            for half in range(2):
                rows_ = slice(ROW_WC + half * SGU_BLOCK, ROW_WC + (half + 1) * SGU_BLOCK)
                osm_ref[rows_, :] = jnp.where(c <= r, osm_ref[rows_, :], 0.0)
            sb_t = _segdot(dsp_acc[...], e4_ref[...]).T
            osm_ref[ROW_SB:ROW_SB + 8, 0:SGU_BLOCK] = sb_t[0:8, :]
            for g in range(4):
                osm_ref[ROW_PW:ROW_PW + HEAD, HEAD * g:HEAD * (g + 1)] = (
                    pw_acc[HEAD * g:HEAD * (g + 1), HEAD * g:HEAD * (g + 1)])

    def rows(width):
        return pl.BlockSpec((T, width), lambda i, l: (nt - 1 - i, 0))

    consts = (wi, caw, cbw, s256, seg, pw, wm, wmt, sb, wo, v1024, e4)
    unstacked = (wi, seg, wo, e4)
    in_specs = [rows(D_MODEL), rows(D_MODEL), rows(IN_WIDTH), rows(3 * GROUP)] + [
        _whole(a) if any(a is u for u in unstacked) else _of_layer(a) for a in consts]
    out_specs = [rows(D_MODEL), rows(IN_WIDTH), rows(D_MODEL), pl.BlockSpec((SM_ROWS, GROUP), lambda i, l: (0, 0))]
    out_shape = [jax.ShapeDtypeStruct((S, D_MODEL), F32), jax.ShapeDtypeStruct((S, IN_WIDTH), BF16),
                 jax.ShapeDtypeStruct((S, D_MODEL), BF16), jax.ShapeDtypeStruct((SM_ROWS, GROUP), F32)]
    scratch_shapes = list(scratch.values())
    extra, aliases = (), {}
    if exch is not None:
        extra = tuple(exch)
        r_i, r_o = exch[3], exch[4]
        in_specs += [ANY] * 5
        out_specs += [ANY] * 3
        out_shape += [jax.ShapeDtypeStruct(r_i.shape, r_i.dtype), jax.ShapeDtypeStruct(r_o.shape, r_o.dtype),
                      jax.ShapeDtypeStruct((N_DEV, SM_ROWS, GROUP), F32)]
        scratch_shapes += [pltpu.SemaphoreType.DMA((N_EXCH_SEMS,)), pltpu.SemaphoreType.DMA((N_EXCH_SEMS,)),
                           pltpu.SemaphoreType.DMA((1,))]
        aliases = {20: 4, 21: 5}
    grid_spec = pltpu.PrefetchScalarGridSpec(num_scalar_prefetch=1, grid=(nt,), in_specs=in_specs,
                                             out_specs=out_specs, scratch_shapes=scratch_shapes)
    return pl.pallas_call(
        body, name="bwd_layer" if exch is None else "bwd_layer_exchange",
        grid_spec=grid_spec, out_shape=out_shape, input_output_aliases=aliases,
        compiler_params=_vmem_params(dimension_semantics=("arbitrary",), has_side_effects=exch is not None),
    )(larr, dy, z, h, aux, *consts, *extra)


def _dw_in(layer, xb, dhb, slab, slab16, *, tk, small=None):
    S = xb.shape[0]
    ns = S // tk

    def body(*refs):
        l_ref, a_ref, b_ref = refs[0:3]
        o_ref, o16_ref = refs[n_in:n_in + 2]
        if small is not None:
            first = (pl.program_id(0) == 0) & (pl.program_id(1) == 0)
            last = (pl.program_id(0) == N_CHIPS - 1) & (pl.program_id(1) == ns - 1)
            _exchange_comm(first, last, None, None, None, refs[5], None, None, refs[n_in + 2], *refs[n_in + 3:])

        @pl.when(pl.program_id(1) == 0)
        def _():
            o_ref[...] = jnp.zeros_like(o_ref)
        o_ref[...] += _dot_tn(a_ref[...], b_ref[...])

        @pl.when(pl.program_id(1) == ns - 1)
        def _():
            o16_ref[...] = o_ref[...].astype(BF16)

    o_spec = pl.BlockSpec((None, None, D_MODEL, COLS), lambda j, s, l: (l[0], j, 0, 0))
    in_specs = [pl.BlockSpec((tk, D_MODEL), lambda j, s, l: (s, 0)), pl.BlockSpec((tk, COLS), lambda j, s, l: (s, j)),
                ANY, ANY]
    out_specs = [o_spec, o_spec]
    out_shape = [jax.ShapeDtypeStruct(slab.shape, F32), jax.ShapeDtypeStruct(slab.shape, BF16)]
    scratch, extra = [], ()
    if small is not None:
        extra = (small,)
        in_specs += [ANY]
        out_specs += [ANY]
        out_shape += [jax.ShapeDtypeStruct((N_DEV, SM_ROWS, GROUP), F32)]
        scratch = [pltpu.SemaphoreType.DMA((N_EXCH_SEMS,)), pltpu.SemaphoreType.DMA((N_EXCH_SEMS,)), pltpu.SemaphoreType.DMA((1,))]
    n_in = 5 + len(extra)
    grid_spec = pltpu.PrefetchScalarGridSpec(
        num_scalar_prefetch=1, grid=(N_CHIPS, ns), in_specs=in_specs, out_specs=out_specs, scratch_shapes=scratch)
    return pl.pallas_call(
        body, name="dw_in" if small is None else "dw_in_exchange", grid_spec=grid_spec, out_shape=out_shape,
        input_output_aliases={3: 0, 4: 1},
        compiler_params=_vmem_params(dimension_semantics=("arbitrary", "arbitrary"), has_side_effects=small is not None),
    )(layer, xb, dhb, slab, slab16, *extra)


def _dw_out(layer, mixb, dzb, slab, slab16, *, tk):
    S = mixb.shape[0]
    ns = S // tk

    def body(l_ref, a_ref, b_ref, slab_ref, slab16_ref, o_ref, o16_ref):
        del l_ref, slab_ref, slab16_ref

        @pl.when(pl.program_id(0) == 0)
        def _():
            o_ref[...] = jnp.zeros_like(o_ref)
        o_ref[...] += _dot_tn(a_ref[...], b_ref[...]).reshape(N_CHIPS, GROUP, D_MODEL)

        @pl.when(pl.program_id(0) == ns - 1)
        def _():
            o16_ref[...] = o_ref[...].astype(BF16)

    o_spec = pl.BlockSpec((None, N_CHIPS, GROUP, D_MODEL), lambda s, l: (l[0], 0, 0, 0))
    grid_spec = pltpu.PrefetchScalarGridSpec(
        num_scalar_prefetch=1, grid=(ns,),
        in_specs=[pl.BlockSpec((tk, D_MODEL), lambda s, l: (s, 0)), pl.BlockSpec((tk, D_MODEL), lambda s, l: (s, 0)),
                  ANY, ANY],
        out_specs=[o_spec, o_spec])
    return pl.pallas_call(
        body, name="dw_out", grid_spec=grid_spec,
        out_shape=[jax.ShapeDtypeStruct(slab.shape, F32), jax.ShapeDtypeStruct(slab.shape, BF16)],
        input_output_aliases={3: 0, 4: 1},
        compiler_params=_vmem_params(dimension_semantics=("arbitrary",)),
    )(layer, mixb, dzb, slab, slab16)


def _adamw_math(w, g, m, v):
    nm = ADAM_B1 * m + (1.0 - ADAM_B1) * g
    nv = ADAM_B2 * v + (1.0 - ADAM_B2) * (g * g)
    c1 = 1.0 - ADAM_B1 ** ADAM_STEP
    c2 = 1.0 - ADAM_B2 ** ADAM_STEP
    return -ADAM_LR * ((nm / c1) / (jnp.sqrt(nv / c2) + ADAM_EPS) + ADAM_WD * w), nm, nv


def _adamw_small(ws, gs, ms, vs):
    n = len(ws)

    def body(*refs):
        for j in range(n):
            d, nm, nv = _adamw_math(*(refs[k * n + j][...] for k in range(4)))
            refs[4 * n + j][...] = d
            refs[5 * n + j][...] = nm
            refs[6 * n + j][...] = nv

    shapes = [jax.ShapeDtypeStruct(w.shape, F32) for w in ws]
    outs = pl.pallas_call(body, name="adamw_small", out_shape=shapes * 3, compiler_params=_vmem_params())(
        *ws, *gs, *ms, *vs)
    return outs[0:n], outs[n:2 * n], outs[2 * n:3 * n]


def _adamw(w, g, m, v, *, rows_per_step, name, copy_g=False):
    R, C = w.shape
    tr = rows_per_step

    def body(w_ref, g_ref, m_ref, v_ref, d_ref, nm_ref, nv_ref, *g_out):
        g_ = g_ref[...]
        d_ref[...], nm_ref[...], nv_ref[...] = _adamw_math(w_ref[...], g_, m_ref[...], v_ref[...])
        if copy_g:
            g_out[0][...] = g_

    spec = pl.BlockSpec((tr, C), lambda i: (i, 0))
    n_out = 4 if copy_g else 3
    return pl.pallas_call(
        body, name=name, grid=(R // tr,),
        in_specs=[spec] * 4, out_specs=[spec] * n_out,
        out_shape=[jax.ShapeDtypeStruct((R, C), F32)] * n_out,
        compiler_params=_vmem_params(dimension_semantics=("arbitrary",)),
    )(w, g, m, v)


def _gather_weights(wi16, wo16, cw):
    L = wi16.shape[0]
    hi_rows, ho_rows = D_MODEL // 2, GROUP // 2
    n_ici = 2 * L + 1
    n_fwd = 2 * L

    def body(wi_ref, wo_ref, cw_ref, *rest):
        wig = rest[0:L]
        wog = rest[L:2 * L]
        cwg = rest[2 * L]
        send_sems, recv_sems, loc_sems = rest[2 * L + 1:]
        x, y, c = _place()
        me_k = 2 * x + y
        sibling = (x, y, 1 - c)
        chips = _other_chips(x, y)

        def half_i(ref, blk):
            return ref.at[blk, pl.ds(c * hi_rows, hi_rows), :]

        def half_o(ref, blk):
            return ref.at[blk, pl.ds(c * ho_rows, ho_rows), :]

        def other_half_i(ref, blk):
            return ref.at[blk, pl.ds((1 - c) * hi_rows, hi_rows), :]

        def other_half_o(ref, blk):
            return ref.at[blk, pl.ds((1 - c) * ho_rows, ho_rows), :]

        local = []
        for l in range(L):
            local.append(pltpu.make_async_copy(wi_ref.at[l], wig[l].at[me_k], loc_sems.at[2 * l]))
            local.append(pltpu.make_async_copy(wo_ref.at[l], wog[l].at[me_k], loc_sems.at[2 * l + 1]))
        local.append(pltpu.make_async_copy(cw_ref, cwg.at[me_k], loc_sems.at[2 * L]))
        for cp in local:
            cp.start()

        def remote(src, dst, sem, to):
            return pltpu.make_async_remote_copy(src_ref=src, dst_ref=dst, send_sem=send_sems.at[sem],
                                                recv_sem=recv_sems.at[sem], device_id=to, device_id_type=MESH)

        sends = []
        for r, (px, py, _) in enumerate(chips):
            to = (px, py, c)
            for l in range(L):
                sends.append(remote(half_i(wi_ref, l), half_i(wig[l], me_k), r * n_ici + 2 * l, to))
                sends.append(remote(half_o(wo_ref, l), half_o(wog[l], me_k), r * n_ici + 2 * l + 1, to))
            sends.append(remote(cw_ref, cwg.at[me_k], r * n_ici + 2 * L, to))
        for cp in sends:
            cp.start()

        base = 3 * n_ici
        fwds = []
        for r, (px, py, pk) in enumerate(chips):
            for l in range(L):
                remote(half_i(wig[l], pk), half_i(wig[l], pk), r * n_ici + 2 * l, sibling).wait_recv()
                f = remote(half_i(wig[l], pk), half_i(wig[l], pk), base + r * n_fwd + 2 * l, sibling)
                f.start()
                fwds.append(f)
                remote(half_o(wog[l], pk), half_o(wog[l], pk), r * n_ici + 2 * l + 1, sibling).wait_recv()
                f = remote(half_o(wog[l], pk), half_o(wog[l], pk), base + r * n_fwd + 2 * l + 1, sibling)
                f.start()
                fwds.append(f)
            remote(cwg.at[pk], cwg.at[pk], r * n_ici + 2 * L, sibling).wait_recv()
        for r, (px, py, pk) in enumerate(chips):
            for l in range(L):
                remote(other_half_i(wig[l], pk), other_half_i(wig[l], pk), base + r * n_fwd + 2 * l, sibling).wait_recv()
                remote(other_half_o(wog[l], pk), other_half_o(wog[l], pk), base + r * n_fwd + 2 * l + 1, sibling).wait_recv()
        for cp in sends + fwds:
            cp.wait_send()
        for cp in local:
            cp.wait()

    n_sem = 3 * n_ici + 3 * n_fwd
    out_shape = ([jax.ShapeDtypeStruct((N_CHIPS, D_MODEL, COLS), BF16)] * L
                 + [jax.ShapeDtypeStruct((N_CHIPS, GROUP, D_MODEL), BF16)] * L
                 + [jax.ShapeDtypeStruct((N_CHIPS,) + cw.shape, F32)])
    outs = pl.pallas_call(
        body, name="gather_weights",
        in_specs=[ANY, ANY, ANY], out_specs=[ANY] * (2 * L + 1), out_shape=out_shape,
        scratch_shapes=[pltpu.SemaphoreType.DMA((n_sem,)), pltpu.SemaphoreType.DMA((n_sem,)),
                        pltpu.SemaphoreType.DMA((2 * L + 1,))],
        compiler_params=pltpu.CompilerParams(has_side_effects=True),
    )(wi16, wo16, cw)
    return outs[0:L], outs[L:2 * L], outs[2 * L]


def _swap_halves(l_arr, gwi, gwo, ri, ro):
    hi_rows, ho_rows = D_MODEL // 2, GROUP // 2

    def body(l_ref, gwi_ref, gwo_ref, ri_in, ro_in, ri_ref, ro_ref, send_sems, recv_sems):
        del ri_in, ro_in
        x, y, c = _place()
        l = l_ref[0]
        sibling = (x, y, 1 - c)
        cps = [
            pltpu.make_async_remote_copy(src_ref=gwi_ref.at[l, :, pl.ds((1 - c) * hi_rows, hi_rows), :],
                                         dst_ref=ri_ref.at[l], send_sem=send_sems.at[0], recv_sem=recv_sems.at[0],
                                         device_id=sibling, device_id_type=MESH),
            pltpu.make_async_remote_copy(src_ref=gwo_ref.at[l, :, pl.ds((1 - c) * ho_rows, ho_rows), :],
                                         dst_ref=ro_ref.at[l], send_sem=send_sems.at[1], recv_sem=recv_sems.at[1],
                                         device_id=sibling, device_id_type=MESH),
        ]
        for cp in cps:
            cp.start()
        for cp in cps:
            cp.wait()

    return pl.pallas_call(
        body, name="swap_halves",
        in_specs=[pl.BlockSpec(memory_space=pltpu.SMEM), ANY, ANY, ANY, ANY], out_specs=[ANY, ANY],
        out_shape=[jax.ShapeDtypeStruct(ri.shape, ri.dtype), jax.ShapeDtypeStruct(ro.shape, ro.dtype)],
        input_output_aliases={3: 0, 4: 1},
        scratch_shapes=[pltpu.SemaphoreType.DMA((2,)), pltpu.SemaphoreType.DMA((2,))],
        compiler_params=pltpu.CompilerParams(has_side_effects=True),
    )(l_arr, gwi, gwo, ri, ro)


def _add_halves(cl_arr, g_i, r_i, p_i, g_o, r_o, p_o, *, nb):
    def body(cl_ref, gi_ref, ri_ref, pi_in, go_ref, ro_ref, po_in, oi_ref, oo_ref):
        del cl_ref, pi_in, po_in
        oi_ref[...] = (gi_ref[...] + ri_ref[...].astype(F32)).astype(oi_ref.dtype)
        oo_ref[...] = (go_ref[...] + ro_ref[...].astype(F32)).astype(oo_ref.dtype)

    def specs(r):
        tr, cols = r.shape[2] // nb, r.shape[3]
        mine = pl.BlockSpec((None, None, tr, cols), lambda k, i, cl: (cl[1], k, cl[0] * nb + i, 0))
        same = pl.BlockSpec((None, None, tr, cols), lambda k, i, cl: (cl[1], k, i, 0))
        return mine, same

    (gi_s, ri_s), (go_s, ro_s) = specs(r_i), specs(r_o)
    grid_spec = pltpu.PrefetchScalarGridSpec(
        num_scalar_prefetch=1, grid=(N_CHIPS, nb),
        in_specs=[gi_s, ri_s, ANY, go_s, ro_s, ANY], out_specs=[ri_s, ro_s])
    return pl.pallas_call(
        body, name="add_halves", grid_spec=grid_spec,
        out_shape=[jax.ShapeDtypeStruct(p_i.shape, p_i.dtype), jax.ShapeDtypeStruct(p_o.shape, p_o.dtype)],
        input_output_aliases={3: 0, 6: 1},
        compiler_params=_vmem_params(dimension_semantics=("arbitrary",) * 2),
    )(cl_arr, g_i, r_i, p_i, g_o, r_o, p_o)


def _exchange_last(l_arr, p_i, p_o, r_i, r_o):
    def body(l_ref, p_i_ref, p_o_ref, ri_in, ro_in, ri_ref, ro_ref, send_sems, recv_sems):
        del ri_in, ro_in
        always = l_ref[0] >= 0
        _exchange_comm(always, always, l_ref[0], p_i_ref, p_o_ref, None, ri_ref, ro_ref, None,
                       send_sems, recv_sems, None)

    return pl.pallas_call(
        body, name="exchange_last",
        in_specs=[pl.BlockSpec(memory_space=pltpu.SMEM)] + [ANY] * 4, out_specs=[ANY] * 2,
        out_shape=[jax.ShapeDtypeStruct(r_i.shape, r_i.dtype), jax.ShapeDtypeStruct(r_o.shape, r_o.dtype)],
        input_output_aliases={3: 0, 4: 1},
        scratch_shapes=[pltpu.SemaphoreType.DMA((N_EXCH_SEMS,)), pltpu.SemaphoreType.DMA((N_EXCH_SEMS,))],
        compiler_params=pltpu.CompilerParams(has_side_effects=True),
    )(l_arr, p_i, p_o, r_i, r_o)


def _sum_small(r_sms):
    L = len(r_sms)

    def body(*refs):
        o_ref = refs[L]
        for l in range(L):
            acc = refs[l][0]
            for d in range(1, N_DEV):
                acc = acc + refs[l][d]
            o_ref[l] = acc

    return pl.pallas_call(
        body, name="sum_small",
        out_shape=jax.ShapeDtypeStruct((L,) + r_sms[0].shape[1:], F32),
        compiler_params=_vmem_params(),
    )(*r_sms)


def _sum_chunks(kc_arr, p_i, q_i, p_o, q_o, *, nb):
    L = p_i.shape[0]

    def body(kc_ref, pi_ref, a0, a1, a2, po_ref, b0, b1, b2, oi_ref, oo_ref):
        del kc_ref
        f = lambda ref: ref[...].astype(F32)
        oi_ref[...] = ((f(pi_ref) + f(a0)) + f(a1)) + f(a2)
        oo_ref[...] = ((f(po_ref) + f(b0)) + f(b1)) + f(b2)

    def specs(p):
        tr, cols = p.shape[2] // nb, p.shape[3]
        chunk = pl.BlockSpec((None, None, tr, cols), lambda l, i, kc: (l, kc[0], i, 0))
        got = [pl.BlockSpec((None, None, tr, cols), lambda l, i, kc, _j=j: (_j, l, i, 0)) for j in range(3)]
        out = pl.BlockSpec((None, tr, cols), lambda l, i, kc: (l, kc[1] * nb + i, 0))
        return [chunk] + got, out

    (in_i, out_i), (in_o, out_o) = specs(p_i), specs(p_o)
    grid_spec = pltpu.PrefetchScalarGridSpec(num_scalar_prefetch=1, grid=(L, nb), in_specs=in_i + in_o,
                                             out_specs=[out_i, out_o])
    return pl.pallas_call(
        body, name="sum_chunks", grid_spec=grid_spec,
        out_shape=[jax.ShapeDtypeStruct((L, 2 * p.shape[2], p.shape[3]), F32) for p in (p_i, p_o)],
        compiler_params=_vmem_params(dimension_semantics=("arbitrary",) * 2),
    )(kc_arr, p_i, q_i, q_i, q_i, p_o, q_o, q_o, q_o)


def _share_result(gi, go):
    hi_rows, ho_rows = gi.shape[1] // 2, go.shape[1] // 2

    def body(gi_ref, go_ref, oi_ref, oo_ref, send_sems, recv_sems):
        del gi_ref, go_ref
        x, y, c = _place()
        sibling = (x, y, 1 - c)
        cps = []
        for j, (ref, n) in enumerate(((oi_ref, hi_rows), (oo_ref, ho_rows))):
            mine = ref.at[:, pl.ds(c * n, n), :]
            cps.append(pltpu.make_async_remote_copy(src_ref=mine, dst_ref=mine, send_sem=send_sems.at[j],
                                                    recv_sem=recv_sems.at[j], device_id=sibling, device_id_type=MESH))
        for cp in cps:
            cp.start()
        for j, (ref, n) in enumerate(((oi_ref, hi_rows), (oo_ref, ho_rows))):
            theirs = ref.at[:, pl.ds((1 - c) * n, n), :]
            pltpu.make_async_remote_copy(src_ref=theirs, dst_ref=theirs, send_sem=send_sems.at[j],
                                         recv_sem=recv_sems.at[j], device_id=sibling, device_id_type=MESH).wait_recv()
        for cp in cps:
            cp.wait_send()

    return pl.pallas_call(
        body, name="share_result",
        in_specs=[ANY, ANY], out_specs=[ANY, ANY],
        out_shape=[jax.ShapeDtypeStruct(gi.shape, F32), jax.ShapeDtypeStruct(go.shape, F32)],
        input_output_aliases={0: 0, 1: 1},
        scratch_shapes=[pltpu.SemaphoreType.DMA((2,)), pltpu.SemaphoreType.DMA((2,))],
        compiler_params=pltpu.CompilerParams(has_side_effects=True),
    )(gi, go)


WEIGHTS = ("ln_g", "ln_b", "w_in", "b_in", "conv_a_w", "conv_a_b", "norm_a_g", "norm_a_b", "conv_b_w", "pool_w",
           "pool_scale", "sgu_ln_g", "sgu_ln_b", "sgu_w", "sgu_bias", "w_out", "b_out")


def _pad_rows(a, rows):
    return jnp.pad(a, ((0, rows - a.shape[0]), (0, 0)))


def _indicator_consts():
    seg = jnp.where((jnp.arange(GROUP)[:, None] // HEAD) == (jnp.arange(GROUP)[None, :] // HEAD),
                    1.0 / HEAD, 0.0).astype(BF16)
    e4 = ((jnp.arange(GROUP)[:, None] // HEAD) == jnp.arange(128)[None, :]).astype(BF16)
    return seg, e4


def _layer_consts(p, conv_full):
    L = conv_full.shape[0]
    same_head = jnp.eye(4, dtype=F32)[:, None, :, None] > 0

    def rows_to(a, rows):
        return jnp.pad(a, ((0, 0), (0, rows - a.shape[1]), (0, 0)))

    s256 = jnp.stack([p[n] for n in ("conv_a_b", "norm_a_g", "norm_a_b", "pool_scale", "sgu_ln_g", "sgu_ln_b")], axis=1)
    pw = jnp.where(same_head, p["pool_w"][:, :, :, None, :], 0.0).reshape(L, GROUP, GROUP)
    return dict(
        caw=rows_to(conv_full[:, :KA], 32), cbw=rows_to(conv_full[:, KA:], 8), s256=rows_to(s256, 8),
        pw=pw.astype(BF16),
        wm=jnp.transpose(p["sgu_w"], (0, 2, 1, 3)).reshape(L, SGU_BLOCK, 4 * SGU_BLOCK),
        wmt=jnp.transpose(p["sgu_w"], (0, 1, 3, 2)).reshape(L, 4 * SGU_BLOCK, SGU_BLOCK),
        sb=jnp.repeat(jnp.transpose(p["sgu_bias"], (0, 2, 1)), HEAD, axis=2),
        v1024=rows_to(jnp.stack([p["b_out"], p["ln_g"], p["ln_b"]], axis=1), 8),
        bin=p["b_in"][:, None, :])


def _unpack_small(sm):
    L = sm.shape[0]
    owc = jnp.concatenate([sm[:, ROW_WC:ROW_WC + SGU_BLOCK], sm[:, ROW_WC + SGU_BLOCK:ROW_WC + 2 * SGU_BLOCK]], axis=2)
    return dict(
        conv_a_b=sm[:, 0], norm_a_g=sm[:, 1], norm_a_b=sm[:, 2], pool_scale=sm[:, 3], sgu_ln_g=sm[:, 4],
        sgu_ln_b=sm[:, 5], conv_b_w=sm[:, ROW_CBW:ROW_CBW + KB], conv_a_w=sm[:, ROW_CAW:ROW_CAW + KA],
        pool_w=jnp.transpose(sm[:, ROW_PW:ROW_PW + HEAD].reshape(L, HEAD, 4, HEAD), (0, 2, 1, 3)),
        ln_g=sm[:, ROW_LNG:ROW_LNG + 4].reshape(L, D_MODEL), ln_b=sm[:, ROW_LNB:ROW_LNB + 4].reshape(L, D_MODEL),
        b_out=sm[:, ROW_BOUT:ROW_BOUT + 4].reshape(L, D_MODEL),
        b_in=sm[:, ROW_BIN:ROW_BIN + N_SLICES].reshape(L, IN_WIDTH),
        sgu_w=jnp.transpose(owc.reshape(L, SGU_BLOCK, 4, SGU_BLOCK), (0, 2, 1, 3)),
        sgu_bias=sm[:, ROW_SB:ROW_SB + 4, 0:SGU_BLOCK])


def _step(p, m, v, x, target, *, tile_f, tile_b, tk_in, tk_out):
    L = p["ln_g"].shape[0]
    xi, yi, ci = _place()
    me_k = 2 * xi + yi
    hi_rows, ho_rows = D_MODEL // 2, GROUP // 2

    cw = jnp.concatenate([p["conv_a_w"], p["conv_b_w"]], axis=1).reshape(-1, 128)
    cw_rows = cw.shape[0]
    cw = _pad_rows(cw, -(-cw_rows // SUBLANES) * SUBLANES)
    wi16 = p["w_in"].astype(BF16)
    wo16 = p["w_out"].astype(BF16)
    wig0, wog0, cwg = _gather_weights(wi16[0:1], wo16[0:1], cw)
    cwg = cwg[:, :cw_rows].reshape(N_CHIPS, L, KA + KB, HEAD)
    conv_full = jnp.transpose(cwg, (1, 2, 0, 3)).reshape(L, KA + KB, GROUP)
    seg, e4 = _indicator_consts()
    k = _layer_consts(p, conv_full)
    layer = [jnp.full((1,), l, jnp.int32) for l in range(L)]

    hcur = x
    saved, wig, wog = [], [wig0[0]], [wog0[0]]
    for l in range(L):
        nxt = (wi16, wo16) if l + 1 < L else None
        outs = _fwd_layer(layer[l], hcur, wig[l], k["bin"], k["caw"], k["cbw"], k["s256"], seg, k["pw"], k["wm"], k["sb"],
                          wog[l], k["v1024"], tile=tile_f, nxt=nxt, target=None if nxt is not None else target)
        y, xb, h, aux, mixb, z = outs[0:6]
        if nxt is not None:
            wig.append(outs[6])
            wog.append(outs[7])
        saved.append((xb, h, aux, mixb, z))
        hcur = y

    dy = hcur
    loss_local = outs[6][0, 0]

    gwi = lax.empty((L, N_CHIPS, D_MODEL, COLS), F32)
    gwo = lax.empty((L, N_CHIPS, GROUP, D_MODEL), F32)
    gwi16 = lax.empty((L, N_CHIPS, D_MODEL, COLS), BF16)
    gwo16 = lax.empty((L, N_CHIPS, GROUP, D_MODEL), BF16)
    ri = lax.empty((L, N_CHIPS, hi_rows, COLS), BF16)
    ro = lax.empty((L, N_CHIPS, ho_rows, D_MODEL), BF16)
    p_i = lax.empty((L, N_CHIPS, hi_rows, COLS), BF16)
    p_o = lax.empty((L, N_CHIPS, ho_rows, D_MODEL), BF16)
    q_i = lax.empty((3, L, hi_rows, COLS), BF16)
    q_o = lax.empty((3, L, ho_rows, D_MODEL), BF16)
    r_sm = [None] * L
    pending = None
    for l in reversed(range(L)):
        xb, h, aux, mixb, z = saved[l]
        exch = None if pending is None else (p_i, p_o, pending, q_i, q_o)
        outs = _bwd_layer(layer[l], dy, z, h, aux, wig[l], k["caw"], k["cbw"], k["s256"], seg, k["pw"], k["wm"],
                          k["wmt"], k["sb"], wog[l], k["v1024"], e4, tile=tile_b, exch=exch)
        dy, dhb, dzb, osm = outs[0:4]
        if l == L - 1:
            osm = osm.at[ROW_LOSS, 0].set(loss_local)
        if exch is not None:
            q_i, q_o, r_sm[l + 1] = outs[4:7]
        larr = layer[l]
        if l > 0:
            gwi, gwi16 = _dw_in(larr, xb, dhb, gwi, gwi16, tk=tk_in)
        else:
            gwi, gwi16, r_sm[0] = _dw_in(larr, xb, dhb, gwi, gwi16, tk=tk_out, small=osm)
        gwo, gwo16 = _dw_out(larr, mixb, dzb, gwo, gwo16, tk=tk_out)
        ri, ro = _swap_halves(larr, gwi16, gwo16, ri, ro)
        cl_arr = jnp.stack([ci, jnp.int32(l)]).astype(jnp.int32)
        p_i, p_o = _add_halves(cl_arr, gwi, ri, p_i, gwo, ro, p_o, nb=2)
        pending = osm
    grad_x = dy
    q_i, q_o = _exchange_last(layer[0], p_i, p_o, q_i, q_o)

    summed = _sum_small(r_sm)
    loss = summed[L - 1, ROW_LOSS, 0]
    grads = _unpack_small(summed)
    for n in ("conv_a_w", "conv_b_w"):
        grads[n] = lax.dynamic_slice_in_dim(grads[n], me_k * HEAD, HEAD, axis=2)

    kc_arr = jnp.stack([me_k, ci]).astype(jnp.int32)
    g_i, g_o = _sum_chunks(kc_arr, p_i, q_i, p_o, q_o, nb=2)
    g_i, g_o = _share_result(g_i, g_o)
    grads["w_in"] = g_i
    grads["w_out"] = g_o

    delta, new_m, new_v = {}, {}, {}
    for n, tr in (("w_in", 512), ("w_out", 256)):
        shp = p[n].shape
        args = [a.reshape(shp[0] * shp[1], shp[2]) for a in (p[n], grads[n], m[n], v[n])]
        outs = _adamw(*args, rows_per_step=tr, name="adamw_" + n, copy_g=True)
        delta[n], new_m[n], new_v[n], grads[n] = (a.reshape(shp) for a in outs)
    small = [n for n in WEIGHTS if n not in ("w_in", "w_out")]
    flat = [[a[n].reshape(-1, a[n].shape[-1]) for n in small] for a in (p, grads, m, v)]
    outs = _adamw_small(*flat)
    for j, n in enumerate(small):
        delta[n], new_m[n], new_v[n] = (o[j].reshape(p[n].shape) for o in outs)

    return (loss, grad_x[None], *[grads[n] for n in WEIGHTS], *[delta[n] for n in WEIGHTS],
            *[new_m[n] for n in WEIGHTS], *[new_v[n] for n in WEIGHTS])


def kernel(x, ln_g, ln_b, w_in, b_in, conv_a_w, conv_a_b, norm_a_g, norm_a_b, conv_b_w, pool_w, pool_scale, sgu_ln_g, sgu_ln_b, sgu_w, sgu_bias, w_out, b_out, loss_target, m_ln_g, m_ln_b, m_w_in, m_b_in, m_conv_a_w, m_conv_a_b, m_norm_a_g, m_norm_a_b, m_conv_b_w, m_pool_w, m_pool_scale, m_sgu_ln_g, m_sgu_ln_b, m_sgu_w, m_sgu_bias, m_w_out, m_b_out, v_ln_g, v_ln_b, v_w_in, v_b_in, v_conv_a_w, v_conv_a_b, v_norm_a_g, v_norm_a_b, v_conv_b_w, v_pool_w, v_pool_scale, v_sgu_ln_g, v_sgu_ln_b, v_sgu_w, v_sgu_bias, v_w_out, v_b_out):
    p = dict(ln_g=ln_g, ln_b=ln_b, w_in=w_in, b_in=b_in, conv_a_w=conv_a_w, conv_a_b=conv_a_b, norm_a_g=norm_a_g,
             norm_a_b=norm_a_b, conv_b_w=conv_b_w, pool_w=pool_w, pool_scale=pool_scale, sgu_ln_g=sgu_ln_g,
             sgu_ln_b=sgu_ln_b, sgu_w=sgu_w, sgu_bias=sgu_bias, w_out=w_out, b_out=b_out)
    m = dict(ln_g=m_ln_g, ln_b=m_ln_b, w_in=m_w_in, b_in=m_b_in, conv_a_w=m_conv_a_w, conv_a_b=m_conv_a_b,
             norm_a_g=m_norm_a_g, norm_a_b=m_norm_a_b, conv_b_w=m_conv_b_w, pool_w=m_pool_w, pool_scale=m_pool_scale,
             sgu_ln_g=m_sgu_ln_g, sgu_ln_b=m_sgu_ln_b, sgu_w=m_sgu_w, sgu_bias=m_sgu_bias, w_out=m_w_out, b_out=m_b_out)
    v = dict(ln_g=v_ln_g, ln_b=v_ln_b, w_in=v_w_in, b_in=v_b_in, conv_a_w=v_conv_a_w, conv_a_b=v_conv_a_b,
             norm_a_g=v_norm_a_g, norm_a_b=v_norm_a_b, conv_b_w=v_conv_b_w, pool_w=v_pool_w, pool_scale=v_pool_scale,
             sgu_ln_g=v_sgu_ln_g, sgu_ln_b=v_sgu_ln_b, sgu_w=v_sgu_w, sgu_bias=v_sgu_bias, w_out=v_w_out, b_out=v_b_out)
    return _step(p, m, v, x[0], loss_target[0], tile_f=256, tile_b=256, tk_in=4096, tk_out=2048)
```

```python
import jax
import jax.numpy as jnp
from jax import lax
from jax.experimental import pallas as pl
from jax.experimental.pallas import tpu as pltpu

F32 = jnp.float32
BF16 = jnp.bfloat16
MESH = pl.DeviceIdType.MESH

D_MODEL = 1024
GROUP = 256
HEAD = 64
N_SLICES = 12
IN_WIDTH = N_SLICES * GROUP
N_CHIPS = 4
COLS = IN_WIDTH // N_CHIPS
KA = 31
KB = 3
SUBLANES = 8
HALO_A, HALO_B, HALO_C = 32, 8, 16
N_GATHER_SEMS = 12
N_EXCH_SEMS = 13
SGU_BLOCK = 128
CHUNK = 64
LN_EPS = 1e-5
ROWS = 64
V7X_VMEM_BYTES = 64 * 1024 * 1024
VMEM_LIMIT = V7X_VMEM_BYTES - 8 * 1024 * 1024

ADAM_LR, ADAM_B1, ADAM_B2, ADAM_EPS, ADAM_WD, ADAM_STEP = 0.001, 0.9, 0.999, 1e-08, 0.01, 10


ANY = pl.BlockSpec(memory_space=pl.ANY)


def _vmem_params(**kw):
    return pltpu.CompilerParams(vmem_limit_bytes=VMEM_LIMIT, **kw)


def _whole(a):
    return pl.BlockSpec(a.shape, lambda i, l, _n=a.ndim: (0,) * _n)


def _of_layer(a):
    return pl.BlockSpec((None,) + a.shape[1:], lambda i, l, _n=a.ndim: (l[0],) + (0,) * (_n - 1))


def _place():
    return lax.axis_index("x"), lax.axis_index("y"), lax.axis_index("c")


def _other_chips(x, y):
    return [(1 - x, y, 2 * (1 - x) + y), (x, 1 - y, 2 * x + (1 - y)), (1 - x, 1 - y, 2 * (1 - x) + (1 - y))]


def _sig(v):
    return 0.5 * jnp.tanh(0.5 * v) + 0.5


def _dot(a, b):
    return jnp.dot(a, b, preferred_element_type=F32)


def _dot_nt(a, b):
    return lax.dot_general(a, b, (((1,), (1,)), ((), ())), preferred_element_type=F32)


def _dot_tn(a, b):
    return lax.dot_general(a, b, (((0,), (0,)), ((), ())), preferred_element_type=F32)


def _segdot(v, m):
    hi = v.astype(BF16)
    lo = (v - hi.astype(F32)).astype(BF16)
    return _dot(hi, m) + _dot(lo, m)


def _colsum(v):
    return jnp.sum(v, axis=0, keepdims=True)


def _rowmean(v):
    return jnp.mean(v, axis=-1, keepdims=True)


def _lane_group(n):
    return lax.broadcasted_iota(jnp.int32, (1, n), 1) // HEAD


def _pool_cnt(tile, t_rows):
    pos = tile * t_rows + lax.broadcasted_iota(jnp.int32, (t_rows, GROUP), 0) + 1
    grp = lax.broadcasted_iota(jnp.int32, (t_rows, GROUP), 1) // HEAD
    win = jnp.where(grp == 0, 2, jnp.where(grp == 1, 4, jnp.where(grp == 2, 8, 16)))
    return jnp.minimum(pos, win).astype(F32)


def _sgu_masks(wm_ref, wmt_ref, wm_s, wmt_s):
    r = lax.broadcasted_iota(jnp.int32, (SGU_BLOCK, 4 * SGU_BLOCK), 0) // CHUNK
    c = (lax.broadcasted_iota(jnp.int32, (SGU_BLOCK, 4 * SGU_BLOCK), 1) % SGU_BLOCK) // CHUNK
    wm_s[...] = jnp.where(c <= r, wm_ref[...], 0.0).astype(BF16)
    if wmt_ref is not None:
        rt = (lax.broadcasted_iota(jnp.int32, (4 * SGU_BLOCK, SGU_BLOCK), 0) % SGU_BLOCK) // CHUNK
        ct = lax.broadcasted_iota(jnp.int32, (4 * SGU_BLOCK, SGU_BLOCK), 1) // CHUNK
        wmt_s[...] = jnp.where(rt <= ct, wmt_ref[...], 0.0).astype(BF16)


def _vstack(v_blk):
    grp = _lane_group(GROUP)
    return jnp.concatenate([jnp.where(grp == h, v_blk, 0.0) for h in range(4)], axis=0).astype(BF16)


def _gather_next(step, nt, nwi, nwo, gwi, gwo, send_sems, recv_sems, loc_sems):
    x, y, c = _place()
    me_k = 2 * x + y
    sibling = (x, y, 1 - c)
    chips = _other_chips(x, y)
    hi, ho = D_MODEL // 2, GROUP // 2
    fwd_sems = N_GATHER_SEMS // 2

    def rc(src, dst, sem, to):
        return pltpu.make_async_remote_copy(src_ref=src, dst_ref=dst, send_sem=send_sems.at[sem],
                                            recv_sem=recv_sems.at[sem], device_id=to, device_id_type=MESH)

    def blk(ref, k, n, cc):
        return ref.at[k, pl.ds(cc * n, n), :]

    def ici(r):
        px, py, _ = chips[r]
        to = (px, py, c)
        return [rc(nwi.at[pl.ds(c * hi, hi), :], blk(gwi, me_k, hi, c), 2 * r, to),
                rc(nwo.at[pl.ds(c * ho, ho), :], blk(gwo, me_k, ho, c), 2 * r + 1, to)]

    def landed(r, cc, base):
        pk = chips[r][2]
        return [rc(blk(gwi, pk, hi, cc), blk(gwi, pk, hi, cc), base + 2 * r, sibling),
                rc(blk(gwo, pk, ho, cc), blk(gwo, pk, ho, cc), base + 2 * r + 1, sibling)]

    def local():
        return [pltpu.make_async_copy(nwi, gwi.at[me_k], loc_sems.at[0]),
                pltpu.make_async_copy(nwo, gwo.at[me_k], loc_sems.at[1])]

    @pl.when(step == 0)
    def _():
        for cp in local():
            cp.start()
        for r in range(3):
            for cp in ici(r):
                cp.start()

    @pl.when(step == (3 * nt) // 4)
    def _():
        for r in range(3):
            for got, fwd in zip(landed(r, c, 0), landed(r, c, fwd_sems)):
                got.wait_recv()
                fwd.start()

    @pl.when(step == nt - 1)
    def _():
        for r in range(3):
            for got in landed(r, 1 - c, fwd_sems):
                got.wait_recv()
        for r in range(3):
            for cp in ici(r) + landed(r, c, fwd_sems):
                cp.wait_send()
        for cp in local():
            cp.wait()


def _fwd_layer(larr, x, wi, bin_, caw, cbw, s256, seg, pw, wm, sb, wo, v1024, *, tile, nxt=None, target=None):
    assert nxt is None or target is None
    S = x.shape[0]
    T = tile
    nt = S // T
    alpha = float((2.0 * 4) ** 0.25)
    n_in = 13 + (2 if nxt is not None else 0) + (1 if target is not None else 0)
    n_out = 6 + (2 if nxt is not None else 0) + (1 if target is not None else 0)

    def body(*refs):
        l_ref = refs[0]
        (x_ref, wi_ref, bin_ref, caw_ref, cbw_ref, s256_ref, seg_ref, pw_ref, wm_ref, sb_ref, wo_ref,
         v1024_ref) = refs[1:13]
        y_ref, xb_ref, h_ref, aux_ref, mix_ref, z_ref = refs[n_in:n_in + 6]
        abuf, bbuf, cbuf, wm_s, shf = refs[n_in + n_out:n_in + n_out + 5]
        i = pl.program_id(0)
        if nxt is not None:
            _gather_next(i, nt, refs[13].at[l_ref[0] + 1], refs[14].at[l_ref[0] + 1], refs[n_in + 6], refs[n_in + 7],
                         *refs[n_in + n_out + 5:])

        @pl.when(i == 0)
        def _():
            abuf[0:HALO_A, :] = jnp.zeros((HALO_A, GROUP), F32)
            bbuf[0:HALO_B, :] = jnp.zeros((HALO_B, GROUP), F32)
            cbuf[0:HALO_C, :] = jnp.zeros((HALO_C, GROUP), F32)
            _sgu_masks(wm_ref, None, wm_s, None)

        x = x_ref[...]
        xb = x.astype(BF16)
        xb_ref[...] = xb
        for k in range(N_CHIPS):
            h_ref[:, COLS * k:COLS * (k + 1)] = _dot(xb, wi_ref[k]) + bin_ref[:, COLS * k:COLS * (k + 1)]

        def hs(j):
            return h_ref[:, GROUP * j:GROUP * (j + 1)]

        abuf[HALO_A:HALO_A + T, :] = hs(0) * _sig(hs(1))
        span = T + HALO_A - SUBLANES
        for p in range(1, SUBLANES):
            shf[p - 1, :, :] = abuf[p:p + span, :]
        for r0 in range(0, T, ROWS):
            acc = None
            for k in range(KA):
                off = HALO_A - (KA - 1) + k
                p, q8 = off % SUBLANES, off - off % SUBLANES
                win = abuf[r0 + q8:r0 + q8 + ROWS, :] if p == 0 else shf[p - 1, r0 + q8:r0 + q8 + ROWS, :]
                term = caw_ref[k:k + 1, :] * win
                acc = term if acc is None else acc + term
            aux_ref[r0:r0 + ROWS, 0:GROUP] = acc + s256_ref[0:1, :]
        abuf[0:HALO_A, :] = abuf[T:T + HALO_A, :]
        a1 = aux_ref[:, 0:GROUP]
        segm = seg_ref[...]
        cen = a1 - _segdot(a1, segm)
        var = _segdot(cen * cen, segm)
        a2 = cen * lax.rsqrt(var + LN_EPS) * s256_ref[1:2, :] + s256_ref[2:3, :]
        az = hs(2)
        mix_ref[:, 0:GROUP] = (a2 * _sig(a2) * (az * _sig(az))).astype(BF16)

        bbuf[HALO_B:HALO_B + T, :] = hs(4) * hs(5)
        for r0 in range(0, T, ROWS):
            acc = None
            for k in range(KB):
                off = HALO_B - (KB - 1) + k + r0
                term = cbw_ref[k:k + 1, :] * bbuf[off:off + ROWS, :]
                acc = term if acc is None else acc + term
            aux_ref[r0:r0 + ROWS, GROUP:2 * GROUP] = acc
        bbuf[0:HALO_B, :] = bbuf[T:T + HALO_B, :]
        bz = hs(6)
        mix_ref[:, GROUP:2 * GROUP] = (hs(3) * aux_ref[:, GROUP:2 * GROUP] * (bz * _sig(bz))).astype(BF16)

        ch = hs(7)
        cbuf[HALO_C:HALO_C + T, :] = ch
        hi_lane = (lax.broadcasted_iota(jnp.int32, (1, 128), 1) // HEAD) == 1
        for r0 in range(0, T, ROWS):
            def win(col, j0, j1):
                s = None
                for j in range(j0, j1):
                    off = HALO_C - j + r0
                    term = cbuf[off:off + ROWS, 128 * col:128 * (col + 1)]
                    s = term if s is None else s + term
                return s
            w0 = win(0, 0, 2) + jnp.where(hi_lane, win(0, 2, 4), 0.0)
            w1 = win(1, 0, 8) + jnp.where(hi_lane, win(1, 8, 16), 0.0)
            aux_ref[r0:r0 + ROWS, 2 * GROUP:2 * GROUP + 128] = w0
            aux_ref[r0:r0 + ROWS, 2 * GROUP + 128:3 * GROUP] = w1
        cbuf[0:HALO_C, :] = cbuf[T:T + HALO_C, :]
        pooled = aux_ref[:, 2 * GROUP:3 * GROUP] / _pool_cnt(i, T) - ch
        aux_ref[:, 2 * GROUP:3 * GROUP] = pooled
        q = _dot(pooled.astype(BF16), pw_ref[...])
        cz = hs(8)
        mix_ref[:, 2 * GROUP:3 * GROUP] = (q * s256_ref[3:4, :] * (cz * _sig(cz))).astype(BF16)

        dv = hs(10)
        cen = dv - _rowmean(dv)
        var = _rowmean(cen * cen)
        v = cen * lax.rsqrt(var + LN_EPS) * s256_ref[4:5, :] + s256_ref[5:6, :]
        sps = []
        for n in range(T // SGU_BLOCK):
            vb = v[n * SGU_BLOCK:(n + 1) * SGU_BLOCK, :]
            sps.append(_dot(wm_s[...], _vstack(vb)) + sb_ref[...])
        sp = jnp.concatenate(sps, axis=0)
        dz = hs(11)
        mix_ref[:, 3 * GROUP:4 * GROUP] = (hs(9) * sp * (dz * _sig(dz))).astype(BF16)

        out = v1024_ref[0:1, :]
        for k in range(N_CHIPS):
            out = out + _dot(mix_ref[:, GROUP * k:GROUP * (k + 1)], wo_ref[k])
        z = alpha * x + out
        z_ref[...] = z
        cen = z - _rowmean(z)
        var = _rowmean(cen * cen)
        y = cen * lax.rsqrt(var + LN_EPS) * v1024_ref[1:2, :] + v1024_ref[2:3, :]
        if target is None:
            y_ref[...] = y
        else:
            t_ref, loss_ref = refs[13], refs[n_in + 6]

            @pl.when(i == 0)
            def _():
                loss_ref[...] = jnp.zeros_like(loss_ref)
            err = y - t_ref[...]
            y_ref[...] = err * (1.0 / D_MODEL)
            loss_ref[...] += jnp.sum(_colsum(err * err), axis=1, keepdims=True) * (0.5 / D_MODEL)

    def rows(width):
        return pl.BlockSpec((T, width), lambda i, l: (i, 0))

    consts = (wi, bin_, caw, cbw, s256, seg, pw, wm, sb, wo, v1024)
    in_specs = [rows(D_MODEL)] + [_whole(a) if a is wi or a is seg or a is wo else _of_layer(a) for a in consts]
    out_specs = [rows(D_MODEL), rows(D_MODEL), rows(IN_WIDTH), rows(3 * GROUP), rows(D_MODEL), rows(D_MODEL)]
    out_shape = [jax.ShapeDtypeStruct((S, D_MODEL), F32), jax.ShapeDtypeStruct((S, D_MODEL), BF16),
                 jax.ShapeDtypeStruct((S, IN_WIDTH), F32), jax.ShapeDtypeStruct((S, 3 * GROUP), F32),
                 jax.ShapeDtypeStruct((S, D_MODEL), BF16), jax.ShapeDtypeStruct((S, D_MODEL), F32)]
    scratch = [pltpu.VMEM((T + HALO_A, GROUP), F32), pltpu.VMEM((T + HALO_B, GROUP), F32),
               pltpu.VMEM((T + HALO_C, GROUP), F32), pltpu.VMEM((SGU_BLOCK, 4 * SGU_BLOCK), BF16),
               pltpu.VMEM((SUBLANES - 1, T + HALO_A - SUBLANES, GROUP), F32)]
    extra = ()
    if nxt is not None:
        extra = tuple(nxt)
        in_specs += [ANY, ANY]
        out_specs += [ANY, ANY]
        out_shape += [jax.ShapeDtypeStruct((N_CHIPS, D_MODEL, COLS), BF16),
                      jax.ShapeDtypeStruct((N_CHIPS, GROUP, D_MODEL), BF16)]
        scratch += [pltpu.SemaphoreType.DMA((N_GATHER_SEMS,)), pltpu.SemaphoreType.DMA((N_GATHER_SEMS,)),
                    pltpu.SemaphoreType.DMA((2,))]
    if target is not None:
        extra = (target,)
        in_specs += [rows(D_MODEL)]
        out_specs += [pl.BlockSpec((8, 128), lambda i, l: (0, 0))]
        out_shape += [jax.ShapeDtypeStruct((8, 128), F32)]
    grid_spec = pltpu.PrefetchScalarGridSpec(num_scalar_prefetch=1, grid=(nt,), in_specs=in_specs,
                                             out_specs=out_specs, scratch_shapes=scratch)
    return pl.pallas_call(
        body, name=("fwd_layer_loss" if target is not None else "fwd_layer") if nxt is None else "fwd_layer_gather",
        grid_spec=grid_spec, out_shape=out_shape,
        compiler_params=_vmem_params(dimension_semantics=("arbitrary",), has_side_effects=nxt is not None),
    )(larr, x, *consts, *extra)


ROW_CBW = 8
ROW_CAW = 16
ROW_LOSS = 7
ROW_PW = 48
ROW_LNG = 112
ROW_LNB = 116
ROW_BOUT = 120
ROW_BIN = 124
ROW_WC = 136
ROW_SB = 392
SM_ROWS = 400
N_DEV = 8


def _exchange_comm(start, finish, l, p_i, p_o, sm, r_i, r_o, r_sm, send_sems, recv_sems, loc_sem):
    x, y, c = _place()
    me = 4 * x + 2 * y + c
    chips = _other_chips(x, y)

    def rc(src, dst, sem, to):
        return pltpu.make_async_remote_copy(src_ref=src, dst_ref=dst, send_sem=send_sems.at[sem],
                                            recv_sem=recv_sems.at[sem], device_id=to, device_id_type=MESH)

    def big(r):
        px, py, pk = chips[r]
        to = (px, py, c)
        return [rc(p_i.at[l, pk], r_i.at[r, l], 2 * r, to), rc(p_o.at[l, pk], r_o.at[r, l], 2 * r + 1, to)]

    def peer(rel):
        px = 1 - x if rel & 4 else x
        py = 1 - y if rel & 2 else y
        pc = 1 - c if rel & 1 else c
        return (px, py, pc), 4 * px + 2 * py + pc

    def small_out(rel):
        to, _ = peer(rel)
        return rc(sm, r_sm.at[me], N_EXCH_SEMS - N_DEV + rel, to)

    def small_in(rel):
        to, idx = peer(rel)
        return rc(sm, r_sm.at[idx], N_EXCH_SEMS - N_DEV + rel, to)

    def local():
        return pltpu.make_async_copy(sm, r_sm.at[me], loc_sem.at[0])

    with_big, with_small = p_i is not None, sm is not None

    @pl.when(start)
    def _():
        if with_small:
            local().start()
        if with_big:
            for r in range(3):
                for cp in big(r):
                    cp.start()
        if with_small:
            for rel in range(1, N_DEV):
                small_out(rel).start()

    @pl.when(finish)
    def _():
        if with_big:
            for r in range(3):
                for cp in big(r):
                    cp.wait()
        if with_small:
            for rel in range(1, N_DEV):
                small_in(rel).wait_recv()
                small_out(rel).wait_send()
            local().wait()


RC = 32
RC_WIDE = 16
ACC_ROWS = 136


def _rsum8(v):
    r = v[0:8]
    for j in range(1, v.shape[0] // 8):
        r = r + v[8 * j:8 * j + 8]
    return r


def _bwd_layer(larr, dy, z, h, aux, wi, caw, cbw, s256, seg, pw, wm, wmt, sb, wo, v1024, e4, *, tile, exch=None):
    S = dy.shape[0]
    T = tile
    nt = S // T
    nblk = T // SGU_BLOCK
    alpha = float((2.0 * 4) ** 0.25)
    n_in = 17 + (5 if exch is not None else 0)
    n_out = 4 + (3 if exch is not None else 0)
    slab = pltpu.VMEM((T, GROUP), F32)
    scratch = dict(
        dbuf=pltpu.VMEM((T + HALO_A, GROUP), F32), ebuf=pltpu.VMEM((T + HALO_B, GROUP), F32),
        fbuf=pltpu.VMEM((T + HALO_C, GROUP), F32), sh=pltpu.VMEM((SUBLANES - 1, T + HALO_A - SUBLANES, GROUP), F32),
        wm_s=pltpu.VMEM((SGU_BLOCK, 4 * SGU_BLOCK), BF16), wmt_s=pltpu.VMEM((4 * SGU_BLOCK, SGU_BLOCK), BF16),
        dsp_acc=pltpu.VMEM((SGU_BLOCK, GROUP), F32), pw_acc=pltpu.VMEM((GROUP, GROUP), F32),
        acc_s=pltpu.VMEM((8 * ACC_ROWS, GROUP), F32), acc_w=pltpu.VMEM((24, D_MODEL), F32),
        dmix_s=pltpu.VMEM((T, D_MODEL), F32), vst_s=pltpu.VMEM((nblk, 4 * SGU_BLOCK, GROUP), BF16),
        dq_s=pltpu.VMEM((T, GROUP), BF16), dxt_s=pltpu.VMEM((D_MODEL, T), F32),
        mean_s=slab, t1_s=slab, t2_s=slab, q_s=slab, xv_s=slab, rv_s=slab, v_s=slab, sp_s=slab, a0_s=slab, sg_s=slab,
        xh_s=slab, ra_s=slab, ub_s=slab, dsp_s=slab, m1_s=slab, m2_s=slab, dpool_s=slab, dvd_s=slab, u_s=slab,
        du_s=slab, cw_s=slab)
    names = list(scratch)

    def body(*refs):
        (dy_ref, z_ref, h_ref, aux_ref, wi_ref, caw_ref, cbw_ref, s256_ref, seg_ref, pw_ref, wm_ref, wmt_ref,
         sb_ref, wo_ref, v1024_ref, e4_ref) = refs[1:17]
        dx_ref, dhb_ref, dzb_ref, osm_ref = refs[n_in:n_in + 4]
        k0 = n_in + n_out
        sc = dict(zip(names, refs[k0:k0 + len(names)]))
        dbuf, ebuf, fbuf, sh = sc["dbuf"], sc["ebuf"], sc["fbuf"], sc["sh"]
        wm_s, wmt_s, dsp_acc, pw_acc, acc_s, acc_w = (sc[n] for n in ("wm_s", "wmt_s", "dsp_acc", "pw_acc", "acc_s",
                                                                        "acc_w"))
        dmix_s, vst_s, dq_s = sc["dmix_s"], sc["vst_s"], sc["dq_s"]
        i = pl.program_id(0)
        tile_idx = nt - 1 - i
        if exch is not None:
            p_i, p_o, sm = refs[17:20]
            r_i, r_o, r_sm = refs[n_in + 4:n_in + 7]
            _exchange_comm(i == 0, i == nt - 1, refs[0][0] + 1, p_i, p_o, sm, r_i, r_o, r_sm, *refs[k0 + len(names):])

        @pl.when(i == 0)
        def _():
            dbuf[T:T + HALO_A, :] = jnp.zeros((HALO_A, GROUP), F32)
            ebuf[T:T + HALO_B, :] = jnp.zeros((HALO_B, GROUP), F32)
            fbuf[T:T + HALO_C, :] = jnp.zeros((HALO_C, GROUP), F32)
            _sgu_masks(wm_ref, wmt_ref, wm_s, wmt_s)
            osm_ref[...] = jnp.zeros_like(osm_ref)
            dsp_acc[...] = jnp.zeros_like(dsp_acc)
            pw_acc[...] = jnp.zeros_like(pw_acc)
            acc_s[...] = jnp.zeros_like(acc_s)
            acc_w[...] = jnp.zeros_like(acc_w)

        def chunks(rc, fn):
            for c in range(T // rc):
                fn(pl.ds(c * rc, rc))

        def hs(j, rows):
            return h_ref[rows, GROUP * j:GROUP * (j + 1)]

        def acc_add(row, val):
            acc_s[8 * row:8 * row + 8, :] += _rsum8(val)

        def put_dh(j, rows, val):
            acc_add(ROW_BIN + j, val)
            dhb_ref[rows, GROUP * j:GROUP * (j + 1)] = val.astype(BF16)

        def dsilu(v, s):
            return s * (1.0 + v * (1.0 - s))

        def vec(r):
            return s256_ref[r:r + 1, :]

        def ln_bwd(rows):
            dyc = dy_ref[rows, :]
            zc = z_ref[rows, :]
            cen = zc - _rowmean(zc)
            rstd = lax.rsqrt(_rowmean(cen * cen) + LN_EPS)
            xhat = cen * rstd
            acc_w[0:8, :] += _rsum8(dyc * xhat)
            acc_w[8:16, :] += _rsum8(dyc)
            gdy = dyc * v1024_ref[1:2, :]
            dz = rstd * (gdy - _rowmean(gdy) - xhat * _rowmean(gdy * xhat))
            acc_w[16:24, :] += _rsum8(dz)
            dzb_ref[rows, :] = dz.astype(BF16)
            dx_ref[rows, :] = alpha * dz
        chunks(RC_WIDE, ln_bwd)

        segm = seg_ref[...]
        dzb = dzb_ref[...]
        for k in range(N_CHIPS):
            dmix_s[:, GROUP * k:GROUP * (k + 1)] = _dot_nt(dzb, wo_ref[k])
        sc["mean_s"][...] = _segdot(aux_ref[:, 0:GROUP], segm)
        pooled_b = aux_ref[:, 2 * GROUP:3 * GROUP].astype(BF16)
        sc["q_s"][...] = _dot(pooled_b, pw_ref[...])

        def centre(rows):
            cen = aux_ref[rows, 0:GROUP] - sc["mean_s"][rows, :]
            sc["t1_s"][rows, :] = cen * cen
            dv_in = hs(10, rows)
            cen_v = dv_in - _rowmean(dv_in)
            rstd_v = lax.rsqrt(_rowmean(cen_v * cen_v) + LN_EPS)
            xv = cen_v * rstd_v
            sc["xv_s"][rows, :] = xv
            sc["rv_s"][rows, :] = jnp.broadcast_to(rstd_v, xv.shape)
            sc["v_s"][rows, :] = xv * vec(4) + vec(5)
        chunks(RC, centre)

        sc["t2_s"][...] = _segdot(sc["t1_s"][...], segm)
        for n in range(nblk):
            blk = slice(n * SGU_BLOCK, (n + 1) * SGU_BLOCK)
            vst_s[n] = _vstack(sc["v_s"][blk, :])
            sc["sp_s"][blk, :] = _dot(wm_s[...], vst_s[n]) + sb_ref[...]

        def mixers(rows):
            a_val, a_glu, a_z = hs(0, rows), hs(1, rows), hs(2, rows)
            sg = _sig(a_glu)
            sc["a0_s"][rows, :] = a_val * sg
            sc["sg_s"][rows, :] = sg
            rstd_a = lax.rsqrt(sc["t2_s"][rows, :] + LN_EPS)
            xh = (aux_ref[rows, 0:GROUP] - sc["mean_s"][rows, :]) * rstd_a
            a2 = xh * vec(1) + vec(2)
            s2 = _sig(a2)
            sz = _sig(a_z)
            dya = dmix_s[rows, 0:GROUP]
            put_dh(2, rows, dya * (a2 * s2) * dsilu(a_z, sz))
            d_a2 = dya * (a_z * sz) * dsilu(a2, s2)
            acc_add(1, d_a2 * xh)
            acc_add(2, d_a2)
            gd = d_a2 * vec(1)
            sc["t1_s"][rows, :] = gd
            sc["t2_s"][rows, :] = gd * xh
            sc["xh_s"][rows, :] = xh
            sc["ra_s"][rows, :] = rstd_a
            b_b, b_c, b_h, b_z = hs(3, rows), hs(4, rows), hs(5, rows), hs(6, rows)
            cb = aux_ref[rows, GROUP:2 * GROUP]
            sz = _sig(b_z)
            dyb = dmix_s[rows, GROUP:2 * GROUP]
            put_dh(3, rows, dyb * cb * (b_z * sz))
            put_dh(6, rows, dyb * b_b * cb * dsilu(b_z, sz))
            ebuf[rows, :] = dyb * b_b * (b_z * sz)
            sc["ub_s"][rows, :] = b_c * b_h
            c_z = hs(8, rows)
            q = sc["q_s"][rows, :]
            sz = _sig(c_z)
            dyc = dmix_s[rows, 2 * GROUP:3 * GROUP]
            acc_add(3, dyc * q * (c_z * sz))
            put_dh(8, rows, dyc * q * vec(3) * dsilu(c_z, sz))
            dq_s[rows, :] = (dyc * vec(3) * (c_z * sz)).astype(BF16)
            d_u, d_z = hs(9, rows), hs(11, rows)
            sp = sc["sp_s"][rows, :]
            sz = _sig(d_z)
            dyd = dmix_s[rows, 3 * GROUP:4 * GROUP]
            put_dh(9, rows, dyd * sp * (d_z * sz))
            put_dh(11, rows, dyd * d_u * sp * dsilu(d_z, sz))
            sc["dsp_s"][rows, :] = dyd * d_u * (d_z * sz)
        chunks(RC, mixers)

        sc["m1_s"][...] = _segdot(sc["t1_s"][...], segm)
        sc["m2_s"][...] = _segdot(sc["t2_s"][...], segm)
        d_q = dq_s[...]
        pw_acc[...] += _dot_tn(pooled_b, d_q)
        sc["dpool_s"][...] = _dot_nt(d_q, pw_ref[...])
        grp = _lane_group(GROUP)
        for n in range(nblk):
            blk = slice(n * SGU_BLOCK, (n + 1) * SGU_BLOCK)
            dspb = sc["dsp_s"][blk, :]
            dsp_acc[...] += dspb
            dspb16 = dspb.astype(BF16)
            dvst = _dot(wmt_s[...], dspb16)
            dvb = None
            for hh in range(4):
                part = jnp.where(grp == hh, dvst[hh * SGU_BLOCK:(hh + 1) * SGU_BLOCK, :], 0.0)
                dvb = part if dvb is None else dvb + part
            sc["dvd_s"][blk, :] = dvb
            dwc = _dot_nt(dspb16, vst_s[n])
            osm_ref[ROW_WC:ROW_WC + SGU_BLOCK, :] += dwc[:, 0:GROUP]
            osm_ref[ROW_WC + SGU_BLOCK:ROW_WC + 2 * SGU_BLOCK, :] += dwc[:, GROUP:2 * GROUP]

        def ln_sums(rows):
            xh = sc["xh_s"][rows, :]
            d_a1 = sc["ra_s"][rows, :] * (sc["t1_s"][rows, :] - sc["m1_s"][rows, :] - xh * sc["m2_s"][rows, :])
            acc_add(0, d_a1)
            dbuf[rows, :] = d_a1
            pos = tile_idx * T + rows.start + lax.broadcasted_iota(jnp.int32, (RC, GROUP), 0) + 1
            lane = lax.broadcasted_iota(jnp.int32, (RC, GROUP), 1) // HEAD
            win = jnp.where(lane == 0, 2, jnp.where(lane == 1, 4, jnp.where(lane == 2, 8, 16)))
            fbuf[rows, :] = sc["dpool_s"][rows, :] / jnp.minimum(pos, win).astype(F32)
            d_v = sc["dvd_s"][rows, :]
            xv = sc["xv_s"][rows, :]
            acc_add(4, d_v * xv)
            acc_add(5, d_v)
            gd = d_v * vec(4)
            put_dh(10, rows, sc["rv_s"][rows, :] * (gd - _rowmean(gd) - xv * _rowmean(gd * xv)))
        chunks(RC, ln_sums)

        span = T + HALO_A - SUBLANES
        for p in range(1, SUBLANES):
            sh[p - 1, :, :] = dbuf[p:p + span, :]

        for r0 in range(0, T, ROWS):
            uc = sc["ub_s"][r0:r0 + ROWS, :]
            acc = None
            for k in range(KB):
                off = (KB - 1) - k + r0
                w = ebuf[off:off + ROWS, :]
                term = cbw_ref[k:k + 1, :] * w
                acc = term if acc is None else acc + term
                acc_add(ROW_CBW + k, uc * w)
            sc["du_s"][r0:r0 + ROWS, :] = acc
        ebuf[T:T + HALO_B, :] = ebuf[0:HALO_B, :]

        hi_lane = (lax.broadcasted_iota(jnp.int32, (1, 128), 1) // HEAD) == 1
        for r0 in range(0, T, ROWS):
            def win(col, j0, j1):
                s = None
                for j in range(j0, j1):
                    term = fbuf[r0 + j:r0 + j + ROWS, 128 * col:128 * (col + 1)]
                    s = term if s is None else s + term
                return s
            sc["cw_s"][r0:r0 + ROWS, 0:128] = win(0, 0, 2) + jnp.where(hi_lane, win(0, 2, 4), 0.0)
            sc["cw_s"][r0:r0 + ROWS, 128:256] = win(1, 0, 8) + jnp.where(hi_lane, win(1, 8, 16), 0.0)
        fbuf[T:T + HALO_C, :] = fbuf[0:HALO_C, :]

        def rest_bc(rows):
            d_u = sc["du_s"][rows, :]
            put_dh(4, rows, d_u * hs(5, rows))
            put_dh(5, rows, d_u * hs(4, rows))
            put_dh(7, rows, sc["cw_s"][rows, :] - sc["dpool_s"][rows, :])
        chunks(RC, rest_bc)

        dxt_s = sc["dxt_s"]

        def dx_term(k):
            term = _dot_nt(wi_ref[k], dhb_ref[:, COLS * k:COLS * (k + 1)])
            if k == 1:
                dxt_s[...] = term
            else:
                dxt_s[...] += term

        def conv_a(rows):
            a0c = sc["a0_s"][rows, :]
            acc = None
            for k in range(KA):
                off = (KA - 1) - k
                p, q8 = off % SUBLANES, off - off % SUBLANES
                w = dbuf[pl.ds(rows.start + q8, RC), :] if p == 0 else sh[p - 1, pl.ds(rows.start + q8, RC), :]
                term = caw_ref[k:k + 1, :] * w
                acc = term if acc is None else acc + term
                acc_add(ROW_CAW + k, a0c * w)
            sc["u_s"][rows, :] = acc
        n_chunks = T // RC
        after = {(n_chunks * j) // 3: j + 1 for j in range(3)}
        for c in range(n_chunks):
            conv_a(pl.ds(c * RC, RC))
            if c in after:
                dx_term(after[c])
        dbuf[T:T + HALO_A, :] = dbuf[0:HALO_A, :]

        def rest_a(rows):
            d_a0 = sc["u_s"][rows, :]
            sg = sc["sg_s"][rows, :]
            put_dh(0, rows, d_a0 * sg)
            put_dh(1, rows, d_a0 * hs(0, rows) * sg * (1.0 - sg))
        chunks(RC, rest_a)
        dx_term(0)
        dx_ref[...] += dxt_s[...].T

        @pl.when(i == nt - 1)
        def _():
            for row in list(range(6)) + list(range(ROW_CBW, ROW_CBW + KB)) + list(range(ROW_CAW, ROW_CAW + KA)) + list(
                    range(ROW_BIN, ROW_BIN + N_SLICES)):
                osm_ref[row:row + 1, :] = _colsum(acc_s[8 * row:8 * row + 8, :])
            for j, row in enumerate((ROW_LNG, ROW_LNB, ROW_BOUT)):
                cs = _colsum(acc_w[8 * j:8 * j + 8, :])
                for q in range(D_MODEL // GROUP):
                    osm_ref[row + q:row + q + 1, :] = cs[:, GROUP * q:GROUP * (q + 1)]
            r = lax.broadcasted_iota(jnp.int32, (SGU_BLOCK, GROUP), 0) // CHUNK
            c = (lax.broadcasted_iota(jnp.int32, (SGU_BLOCK, GROUP), 1) % SGU_BLOCK) // CHUNK
            for half in range(2):
                rows_ = slice(ROW_WC + half * SGU_BLOCK, ROW_WC + (half + 1) * SGU_BLOCK)
                osm_ref[rows_, :] = jnp.where(c <= r, osm_ref[rows_, :], 0.0)
            sb_t = _segdot(dsp_acc[...], e4_ref[...]).T
            osm_ref[ROW_SB:ROW_SB + 8, 0:SGU_BLOCK] = sb_t[0:8, :]
            for g in range(4):
                osm_ref[ROW_PW:ROW_PW + HEAD, HEAD * g:HEAD * (g + 1)] = (
                    pw_acc[HEAD * g:HEAD * (g + 1), HEAD * g:HEAD * (g + 1)])

    def rows(width):
        return pl.BlockSpec((T, width), lambda i, l: (nt - 1 - i, 0))

    consts = (wi, caw, cbw, s256, seg, pw, wm, wmt, sb, wo, v1024, e4)
    unstacked = (wi, seg, wo, e4)
    in_specs = [rows(D_MODEL), rows(D_MODEL), rows(IN_WIDTH), rows(3 * GROUP)] + [
        _whole(a) if any(a is u for u in unstacked) else _of_layer(a) for a in consts]
    out_specs = [rows(D_MODEL), rows(IN_WIDTH), rows(D_MODEL), pl.BlockSpec((SM_ROWS, GROUP), lambda i, l: (0, 0))]
    out_shape = [jax.ShapeDtypeStruct((S, D_MODEL), F32), jax.ShapeDtypeStruct((S, IN_WIDTH), BF16),
                 jax.ShapeDtypeStruct((S, D_MODEL), BF16), jax.ShapeDtypeStruct((SM_ROWS, GROUP), F32)]
    scratch_shapes = list(scratch.values())
    extra, aliases = (), {}
    if exch is not None:
        extra = tuple(exch)
        r_i, r_o = exch[3], exch[4]
        in_specs += [ANY] * 5
        out_specs += [ANY] * 3
        out_shape += [jax.ShapeDtypeStruct(r_i.shape, r_i.dtype), jax.ShapeDtypeStruct(r_o.shape, r_o.dtype),
                      jax.ShapeDtypeStruct((N_DEV, SM_ROWS, GROUP), F32)]
        scratch_shapes += [pltpu.SemaphoreType.DMA((N_EXCH_SEMS,)), pltpu.SemaphoreType.DMA((N_EXCH_SEMS,)),
                           pltpu.SemaphoreType.DMA((1,))]
        aliases = {20: 4, 21: 5}
    grid_spec = pltpu.PrefetchScalarGridSpec(num_scalar_prefetch=1, grid=(nt,), in_specs=in_specs,
                                             out_specs=out_specs, scratch_shapes=scratch_shapes)
    return pl.pallas_call(
        body, name="bwd_layer" if exch is None else "bwd_layer_exchange",
        grid_spec=grid_spec, out_shape=out_shape, input_output_aliases=aliases,
        compiler_params=_vmem_params(dimension_semantics=("arbitrary",), has_side_effects=exch is not None),
    )(larr, dy, z, h, aux, *consts, *extra)


def _dw_in(layer, xb, dhb, slab, slab16, *, tk, small=None):
    S = xb.shape[0]
    ns = S // tk

    def body(*refs):
        l_ref, a_ref, b_ref = refs[0:3]
        o_ref, o16_ref = refs[n_in:n_in + 2]
        if small is not None:
            first = (pl.program_id(0) == 0) & (pl.program_id(1) == 0)
            last = (pl.program_id(0) == N_CHIPS - 1) & (pl.program_id(1) == ns - 1)
            _exchange_comm(first, last, None, None, None, refs[5], None, None, refs[n_in + 2], *refs[n_in + 3:])

        @pl.when(pl.program_id(1) == 0)
        def _():
            o_ref[...] = jnp.zeros_like(o_ref)
        o_ref[...] += _dot_tn(a_ref[...], b_ref[...])

        @pl.when(pl.program_id(1) == ns - 1)
        def _():
            o16_ref[...] = o_ref[...].astype(BF16)

    o_spec = pl.BlockSpec((None, None, D_MODEL, COLS), lambda j, s, l: (l[0], j, 0, 0))
    in_specs = [pl.BlockSpec((tk, D_MODEL), lambda j, s, l: (s, 0)), pl.BlockSpec((tk, COLS), lambda j, s, l: (s, j)),
                ANY, ANY]
    out_specs = [o_spec, o_spec]
    out_shape = [jax.ShapeDtypeStruct(slab.shape, F32), jax.ShapeDtypeStruct(slab.shape, BF16)]
    scratch, extra = [], ()
    if small is not None:
        extra = (small,)
        in_specs += [ANY]
        out_specs += [ANY]
        out_shape += [jax.ShapeDtypeStruct((N_DEV, SM_ROWS, GROUP), F32)]
        scratch = [pltpu.SemaphoreType.DMA((N_EXCH_SEMS,)), pltpu.SemaphoreType.DMA((N_EXCH_SEMS,)), pltpu.SemaphoreType.DMA((1,))]
    n_in = 5 + len(extra)
    grid_spec = pltpu.PrefetchScalarGridSpec(
        num_scalar_prefetch=1, grid=(N_CHIPS, ns), in_specs=in_specs, out_specs=out_specs, scratch_shapes=scratch)
    return pl.pallas_call(
        body, name="dw_in" if small is None else "dw_in_exchange", grid_spec=grid_spec, out_shape=out_shape,
        input_output_aliases={3: 0, 4: 1},
        compiler_params=_vmem_params(dimension_semantics=("arbitrary", "arbitrary"), has_side_effects=small is not None),
    )(layer, xb, dhb, slab, slab16, *extra)


def _dw_out(layer, mixb, dzb, slab, slab16, *, tk, carry=None):
    S = mixb.shape[0]
    ns = S // tk

    def body(*refs):
        a_ref, b_ref = refs[1:3]
        o_ref, o16_ref = refs[n_in:n_in + 2]
        if carry is not None:
            cp = pltpu.make_async_copy(refs[5], refs[n_in + 2], refs[n_in + 3].at[0])

            @pl.when(pl.program_id(0) == 0)
            def _():
                cp.start()

        @pl.when(pl.program_id(0) == 0)
        def _():
            o_ref[...] = jnp.zeros_like(o_ref)
        o_ref[...] += _dot_tn(a_ref[...], b_ref[...]).reshape(N_CHIPS, GROUP, D_MODEL)

        @pl.when(pl.program_id(0) == ns - 1)
        def _():
            o16_ref[...] = o_ref[...].astype(BF16)
            if carry is not None:
                cp.wait()

    o_spec = pl.BlockSpec((None, N_CHIPS, GROUP, D_MODEL), lambda s, l: (l[0], 0, 0, 0))
    in_specs = [pl.BlockSpec((tk, D_MODEL), lambda s, l: (s, 0)), pl.BlockSpec((tk, D_MODEL), lambda s, l: (s, 0)),
                ANY, ANY]
    out_specs = [o_spec, o_spec]
    out_shape = [jax.ShapeDtypeStruct(slab.shape, F32), jax.ShapeDtypeStruct(slab.shape, BF16)]
    scratch, extra = [], ()
    if carry is not None:
        extra = (carry,)
        in_specs += [ANY]
        out_specs += [ANY]
        out_shape += [jax.ShapeDtypeStruct(carry.shape, carry.dtype)]
        scratch = [pltpu.SemaphoreType.DMA((1,))]
    n_in = 5 + len(extra)
    grid_spec = pltpu.PrefetchScalarGridSpec(num_scalar_prefetch=1, grid=(ns,), in_specs=in_specs,
                                             out_specs=out_specs, scratch_shapes=scratch)
    return pl.pallas_call(
        body, name="dw_out" if carry is None else "dw_out_carry", grid_spec=grid_spec, out_shape=out_shape,
        input_output_aliases={3: 0, 4: 1},
        compiler_params=_vmem_params(dimension_semantics=("arbitrary",)),
    )(layer, mixb, dzb, slab, slab16, *extra)


def _adamw_math(w, g, m, v):
    nm = ADAM_B1 * m + (1.0 - ADAM_B1) * g
    nv = ADAM_B2 * v + (1.0 - ADAM_B2) * (g * g)
    c1 = 1.0 - ADAM_B1 ** ADAM_STEP
    c2 = 1.0 - ADAM_B2 ** ADAM_STEP
    return -ADAM_LR * ((nm / c1) / (jnp.sqrt(nv / c2) + ADAM_EPS) + ADAM_WD * w), nm, nv


def _adamw_small(ws, gs, ms, vs):
    n = len(ws)

    def body(*refs):
        for j in range(n):
            d, nm, nv = _adamw_math(*(refs[k * n + j][...] for k in range(4)))
            refs[4 * n + j][...] = d
            refs[5 * n + j][...] = nm
            refs[6 * n + j][...] = nv

    shapes = [jax.ShapeDtypeStruct(w.shape, F32) for w in ws]
    outs = pl.pallas_call(body, name="adamw_small", out_shape=shapes * 3, compiler_params=_vmem_params())(
        *ws, *gs, *ms, *vs)
    return outs[0:n], outs[n:2 * n], outs[2 * n:3 * n]


def _adamw(w, g, m, v, *, rows_per_step, name, copy_g=False):
    R, C = w.shape
    tr = rows_per_step

    def body(w_ref, g_ref, m_ref, v_ref, d_ref, nm_ref, nv_ref, *g_out):
        g_ = g_ref[...]
        d_ref[...], nm_ref[...], nv_ref[...] = _adamw_math(w_ref[...], g_, m_ref[...], v_ref[...])
        if copy_g:
            g_out[0][...] = g_

    spec = pl.BlockSpec((tr, C), lambda i: (i, 0))
    n_out = 4 if copy_g else 3
    return pl.pallas_call(
        body, name=name, grid=(R // tr,),
        in_specs=[spec] * 4, out_specs=[spec] * n_out,
        out_shape=[jax.ShapeDtypeStruct((R, C), F32)] * n_out,
        compiler_params=_vmem_params(dimension_semantics=("arbitrary",)),
    )(w, g, m, v)


def _gather_weights(wi16, wo16, cw):
    L = wi16.shape[0]
    hi_rows, ho_rows = D_MODEL // 2, GROUP // 2
    n_ici = 2 * L + 1
    n_fwd = 2 * L

    def body(wi_ref, wo_ref, cw_ref, *rest):
        wig = rest[0:L]
        wog = rest[L:2 * L]
        cwg = rest[2 * L]
        send_sems, recv_sems, loc_sems = rest[2 * L + 1:]
        x, y, c = _place()
        me_k = 2 * x + y
        sibling = (x, y, 1 - c)
        chips = _other_chips(x, y)

        def half_i(ref, blk):
            return ref.at[blk, pl.ds(c * hi_rows, hi_rows), :]

        def half_o(ref, blk):
            return ref.at[blk, pl.ds(c * ho_rows, ho_rows), :]

        def other_half_i(ref, blk):
            return ref.at[blk, pl.ds((1 - c) * hi_rows, hi_rows), :]

        def other_half_o(ref, blk):
            return ref.at[blk, pl.ds((1 - c) * ho_rows, ho_rows), :]

        local = []
        for l in range(L):
            local.append(pltpu.make_async_copy(wi_ref.at[l], wig[l].at[me_k], loc_sems.at[2 * l]))
            local.append(pltpu.make_async_copy(wo_ref.at[l], wog[l].at[me_k], loc_sems.at[2 * l + 1]))
        local.append(pltpu.make_async_copy(cw_ref, cwg.at[me_k], loc_sems.at[2 * L]))
        for cp in local:
            cp.start()

        def remote(src, dst, sem, to):
            return pltpu.make_async_remote_copy(src_ref=src, dst_ref=dst, send_sem=send_sems.at[sem],
                                                recv_sem=recv_sems.at[sem], device_id=to, device_id_type=MESH)

        sends = []
        for r, (px, py, _) in enumerate(chips):
            to = (px, py, c)
            for l in range(L):
                sends.append(remote(half_i(wi_ref, l), half_i(wig[l], me_k), r * n_ici + 2 * l, to))
                sends.append(remote(half_o(wo_ref, l), half_o(wog[l], me_k), r * n_ici + 2 * l + 1, to))
            sends.append(remote(cw_ref, cwg.at[me_k], r * n_ici + 2 * L, to))
        for cp in sends:
            cp.start()

        base = 3 * n_ici
        fwds = []
        for r, (px, py, pk) in enumerate(chips):
            for l in range(L):
                remote(half_i(wig[l], pk), half_i(wig[l], pk), r * n_ici + 2 * l, sibling).wait_recv()
                f = remote(half_i(wig[l], pk), half_i(wig[l], pk), base + r * n_fwd + 2 * l, sibling)
                f.start()
                fwds.append(f)
                remote(half_o(wog[l], pk), half_o(wog[l], pk), r * n_ici + 2 * l + 1, sibling).wait_recv()
                f = remote(half_o(wog[l], pk), half_o(wog[l], pk), base + r * n_fwd + 2 * l + 1, sibling)
                f.start()
                fwds.append(f)
            remote(cwg.at[pk], cwg.at[pk], r * n_ici + 2 * L, sibling).wait_recv()
        for r, (px, py, pk) in enumerate(chips):
            for l in range(L):
                remote(other_half_i(wig[l], pk), other_half_i(wig[l], pk), base + r * n_fwd + 2 * l, sibling).wait_recv()
                remote(other_half_o(wog[l], pk), other_half_o(wog[l], pk), base + r * n_fwd + 2 * l + 1, sibling).wait_recv()
        for cp in sends + fwds:
            cp.wait_send()
        for cp in local:
            cp.wait()

    n_sem = 3 * n_ici + 3 * n_fwd
    out_shape = ([jax.ShapeDtypeStruct((N_CHIPS, D_MODEL, COLS), BF16)] * L
                 + [jax.ShapeDtypeStruct((N_CHIPS, GROUP, D_MODEL), BF16)] * L
                 + [jax.ShapeDtypeStruct((N_CHIPS,) + cw.shape, F32)])
    outs = pl.pallas_call(
        body, name="gather_weights",
        in_specs=[ANY, ANY, ANY], out_specs=[ANY] * (2 * L + 1), out_shape=out_shape,
        scratch_shapes=[pltpu.SemaphoreType.DMA((n_sem,)), pltpu.SemaphoreType.DMA((n_sem,)),
                        pltpu.SemaphoreType.DMA((2 * L + 1,))],
        compiler_params=pltpu.CompilerParams(has_side_effects=True),
    )(wi16, wo16, cw)
    return outs[0:L], outs[L:2 * L], outs[2 * L]


def _swap_halves(l_arr, gwi, gwo, ri, ro):
    hi_rows, ho_rows = D_MODEL // 2, GROUP // 2

    def body(l_ref, gwi_ref, gwo_ref, ri_in, ro_in, ri_ref, ro_ref, send_sems, recv_sems):
        del ri_in, ro_in
        x, y, c = _place()
        l = l_ref[0]
        sibling = (x, y, 1 - c)
        cps = [
            pltpu.make_async_remote_copy(src_ref=gwi_ref.at[l, :, pl.ds((1 - c) * hi_rows, hi_rows), :],
                                         dst_ref=ri_ref.at[l], send_sem=send_sems.at[0], recv_sem=recv_sems.at[0],
                                         device_id=sibling, device_id_type=MESH),
            pltpu.make_async_remote_copy(src_ref=gwo_ref.at[l, :, pl.ds((1 - c) * ho_rows, ho_rows), :],
                                         dst_ref=ro_ref.at[l], send_sem=send_sems.at[1], recv_sem=recv_sems.at[1],
                                         device_id=sibling, device_id_type=MESH),
        ]
        for cp in cps:
            cp.start()
        for cp in cps:
            cp.wait()

    return pl.pallas_call(
        body, name="swap_halves",
        in_specs=[pl.BlockSpec(memory_space=pltpu.SMEM), ANY, ANY, ANY, ANY], out_specs=[ANY, ANY],
        out_shape=[jax.ShapeDtypeStruct(ri.shape, ri.dtype), jax.ShapeDtypeStruct(ro.shape, ro.dtype)],
        input_output_aliases={3: 0, 4: 1},
        scratch_shapes=[pltpu.SemaphoreType.DMA((2,)), pltpu.SemaphoreType.DMA((2,))],
        compiler_params=pltpu.CompilerParams(has_side_effects=True),
    )(l_arr, gwi, gwo, ri, ro)


def _add_halves(cl_arr, g_i, r_i, p_i, g_o, r_o, p_o, *, nb):
    def body(cl_ref, gi_ref, ri_ref, pi_in, go_ref, ro_ref, po_in, oi_ref, oo_ref):
        del cl_ref, pi_in, po_in
        oi_ref[...] = (gi_ref[...] + ri_ref[...].astype(F32)).astype(oi_ref.dtype)
        oo_ref[...] = (go_ref[...] + ro_ref[...].astype(F32)).astype(oo_ref.dtype)

    def specs(r):
        tr, cols = r.shape[2] // nb, r.shape[3]
        mine = pl.BlockSpec((None, None, tr, cols), lambda k, i, cl: (cl[1], k, cl[0] * nb + i, 0))
        same = pl.BlockSpec((None, None, tr, cols), lambda k, i, cl: (cl[1], k, i, 0))
        return mine, same

    (gi_s, ri_s), (go_s, ro_s) = specs(r_i), specs(r_o)
    grid_spec = pltpu.PrefetchScalarGridSpec(
        num_scalar_prefetch=1, grid=(N_CHIPS, nb),
        in_specs=[gi_s, ri_s, ANY, go_s, ro_s, ANY], out_specs=[ri_s, ro_s])
    return pl.pallas_call(
        body, name="add_halves", grid_spec=grid_spec,
        out_shape=[jax.ShapeDtypeStruct(p_i.shape, p_i.dtype), jax.ShapeDtypeStruct(p_o.shape, p_o.dtype)],
        input_output_aliases={3: 0, 6: 1},
        compiler_params=_vmem_params(dimension_semantics=("arbitrary",) * 2),
    )(cl_arr, g_i, r_i, p_i, g_o, r_o, p_o)


def _swap_add(cl_arr, g_i, g16_i, p_i, g_o, g16_o, p_o):
    hi, ho = p_i.shape[2], p_o.shape[2]

    def body(cl_ref, gi_ref, gi16_ref, pi_in, go_ref, go16_ref, po_in, oi_ref, oo_ref, ri_v, ro_v, send_sems,
             recv_sems):
        del pi_in, po_in
        k = pl.program_id(0)
        x, y, c = _place()
        l = cl_ref[1]

        def copies(kk):
            pair = ((gi16_ref, hi, ri_v), (go16_ref, ho, ro_v))
            return [pltpu.make_async_remote_copy(
                src_ref=src.at[l, kk, pl.ds((1 - c) * n, n), :], dst_ref=dst.at[kk], send_sem=send_sems.at[2 * kk + j],
                recv_sem=recv_sems.at[2 * kk + j], device_id=(x, y, 1 - c), device_id_type=MESH)
                for j, (src, n, dst) in enumerate(pair)]

        @pl.when(k == 0)
        def _():
            for kk in range(N_CHIPS):
                for cp in copies(kk):
                    cp.start()

        for cp in copies(k):
            cp.wait_recv()
        oi_ref[...] = (gi_ref[...] + ri_v[k].astype(F32)).astype(oi_ref.dtype)
        oo_ref[...] = (go_ref[...] + ro_v[k].astype(F32)).astype(oo_ref.dtype)

        @pl.when(k == N_CHIPS - 1)
        def _():
            for kk in range(N_CHIPS):
                for cp in copies(kk):
                    cp.wait_send()

    def specs(p):
        rows, cols = p.shape[2], p.shape[3]
        mine = pl.BlockSpec((None, None, rows, cols), lambda k, cl: (cl[1], k, cl[0], 0))
        out = pl.BlockSpec((None, None, rows, cols), lambda k, cl: (cl[1], k, 0, 0))
        return mine, out

    (gi_s, pi_s), (go_s, po_s) = specs(p_i), specs(p_o)
    grid_spec = pltpu.PrefetchScalarGridSpec(
        num_scalar_prefetch=1, grid=(N_CHIPS,),
        in_specs=[gi_s, ANY, ANY, go_s, ANY, ANY], out_specs=[pi_s, po_s],
        scratch_shapes=[pltpu.VMEM((N_CHIPS, hi, p_i.shape[3]), BF16), pltpu.VMEM((N_CHIPS, ho, p_o.shape[3]), BF16),
                        pltpu.SemaphoreType.DMA((2 * N_CHIPS,)), pltpu.SemaphoreType.DMA((2 * N_CHIPS,))])
    return pl.pallas_call(
        body, name="swap_add", grid_spec=grid_spec,
        out_shape=[jax.ShapeDtypeStruct(p_i.shape, p_i.dtype), jax.ShapeDtypeStruct(p_o.shape, p_o.dtype)],
        input_output_aliases={3: 0, 6: 1},
        compiler_params=_vmem_params(dimension_semantics=("arbitrary",), has_side_effects=True),
    )(cl_arr, g_i, g16_i, p_i, g_o, g16_o, p_o)


def _exchange_last(l_arr, p_i, p_o, r_i, r_o):
    def body(l_ref, p_i_ref, p_o_ref, ri_in, ro_in, ri_ref, ro_ref, send_sems, recv_sems):
        del ri_in, ro_in
        always = l_ref[0] >= 0
        _exchange_comm(always, always, l_ref[0], p_i_ref, p_o_ref, None, ri_ref, ro_ref, None,
                       send_sems, recv_sems, None)

    return pl.pallas_call(
        body, name="exchange_last",
        in_specs=[pl.BlockSpec(memory_space=pltpu.SMEM)] + [ANY] * 4, out_specs=[ANY] * 2,
        out_shape=[jax.ShapeDtypeStruct(r_i.shape, r_i.dtype), jax.ShapeDtypeStruct(r_o.shape, r_o.dtype)],
        input_output_aliases={3: 0, 4: 1},
        scratch_shapes=[pltpu.SemaphoreType.DMA((N_EXCH_SEMS,)), pltpu.SemaphoreType.DMA((N_EXCH_SEMS,))],
        compiler_params=pltpu.CompilerParams(has_side_effects=True),
    )(l_arr, p_i, p_o, r_i, r_o)


def _sum_small(r_sms):
    L = len(r_sms)

    def body(*refs):
        o_ref = refs[L]
        for l in range(L):
            acc = refs[l][0]
            for d in range(1, N_DEV):
                acc = acc + refs[l][d]
            o_ref[l] = acc

    return pl.pallas_call(
        body, name="sum_small",
        out_shape=jax.ShapeDtypeStruct((L,) + r_sms[0].shape[1:], F32),
        compiler_params=_vmem_params(),
    )(*r_sms)


def _sum_chunks(kc_arr, p_i, q_i, p_o, q_o, *, nb):
    L = p_i.shape[0]

    def body(kc_ref, pi_ref, a0, a1, a2, po_ref, b0, b1, b2, oi_ref, oo_ref):
        del kc_ref
        f = lambda ref: ref[...].astype(F32)
        oi_ref[...] = ((f(pi_ref) + f(a0)) + f(a1)) + f(a2)
        oo_ref[...] = ((f(po_ref) + f(b0)) + f(b1)) + f(b2)

    def specs(p):
        tr, cols = p.shape[2] // nb, p.shape[3]
        chunk = pl.BlockSpec((None, None, tr, cols), lambda l, i, kc: (l, kc[0], i, 0))
        got = [pl.BlockSpec((None, None, tr, cols), lambda l, i, kc, _j=j: (_j, l, i, 0)) for j in range(3)]
        out = pl.BlockSpec((None, tr, cols), lambda l, i, kc: (l, kc[1] * nb + i, 0))
        return [chunk] + got, out

    (in_i, out_i), (in_o, out_o) = specs(p_i), specs(p_o)
    grid_spec = pltpu.PrefetchScalarGridSpec(num_scalar_prefetch=1, grid=(L, nb), in_specs=in_i + in_o,
                                             out_specs=[out_i, out_o])
    return pl.pallas_call(
        body, name="sum_chunks", grid_spec=grid_spec,
        out_shape=[jax.ShapeDtypeStruct((L, 2 * p.shape[2], p.shape[3]), F32) for p in (p_i, p_o)],
        compiler_params=_vmem_params(dimension_semantics=("arbitrary",) * 2),
    )(kc_arr, p_i, q_i, q_i, q_i, p_o, q_o, q_o, q_o)


def _share_result(gi, go):
    hi_rows, ho_rows = gi.shape[1] // 2, go.shape[1] // 2

    def body(gi_ref, go_ref, oi_ref, oo_ref, send_sems, recv_sems):
        del gi_ref, go_ref
        x, y, c = _place()
        sibling = (x, y, 1 - c)
        cps = []
        for j, (ref, n) in enumerate(((oi_ref, hi_rows), (oo_ref, ho_rows))):
            mine = ref.at[:, pl.ds(c * n, n), :]
            cps.append(pltpu.make_async_remote_copy(src_ref=mine, dst_ref=mine, send_sem=send_sems.at[j],
                                                    recv_sem=recv_sems.at[j], device_id=sibling, device_id_type=MESH))
        for cp in cps:
            cp.start()
        for j, (ref, n) in enumerate(((oi_ref, hi_rows), (oo_ref, ho_rows))):
            theirs = ref.at[:, pl.ds((1 - c) * n, n), :]
            pltpu.make_async_remote_copy(src_ref=theirs, dst_ref=theirs, send_sem=send_sems.at[j],
                                         recv_sem=recv_sems.at[j], device_id=sibling, device_id_type=MESH).wait_recv()
        for cp in cps:
            cp.wait_send()

    return pl.pallas_call(
        body, name="share_result",
        in_specs=[ANY, ANY], out_specs=[ANY, ANY],
        out_shape=[jax.ShapeDtypeStruct(gi.shape, F32), jax.ShapeDtypeStruct(go.shape, F32)],
        input_output_aliases={0: 0, 1: 1},
        scratch_shapes=[pltpu.SemaphoreType.DMA((2,)), pltpu.SemaphoreType.DMA((2,))],
        compiler_params=pltpu.CompilerParams(has_side_effects=True),
    )(gi, go)


WEIGHTS = ("ln_g", "ln_b", "w_in", "b_in", "conv_a_w", "conv_a_b", "norm_a_g", "norm_a_b", "conv_b_w", "pool_w",
           "pool_scale", "sgu_ln_g", "sgu_ln_b", "sgu_w", "sgu_bias", "w_out", "b_out")


def _pad_rows(a, rows):
    return jnp.pad(a, ((0, rows - a.shape[0]), (0, 0)))


def _indicator_consts():
    seg = jnp.where((jnp.arange(GROUP)[:, None] // HEAD) == (jnp.arange(GROUP)[None, :] // HEAD),
                    1.0 / HEAD, 0.0).astype(BF16)
    e4 = ((jnp.arange(GROUP)[:, None] // HEAD) == jnp.arange(128)[None, :]).astype(BF16)
    return seg, e4


def _layer_consts(p, conv_full):
    L = conv_full.shape[0]
    same_head = jnp.eye(4, dtype=F32)[:, None, :, None] > 0

    def rows_to(a, rows):
        return jnp.pad(a, ((0, 0), (0, rows - a.shape[1]), (0, 0)))

    s256 = jnp.stack([p[n] for n in ("conv_a_b", "norm_a_g", "norm_a_b", "pool_scale", "sgu_ln_g", "sgu_ln_b")], axis=1)
    pw = jnp.where(same_head, p["pool_w"][:, :, :, None, :], 0.0).reshape(L, GROUP, GROUP)
    return dict(
        caw=rows_to(conv_full[:, :KA], 32), cbw=rows_to(conv_full[:, KA:], 8), s256=rows_to(s256, 8),
        pw=pw.astype(BF16),
        wm=jnp.transpose(p["sgu_w"], (0, 2, 1, 3)).reshape(L, SGU_BLOCK, 4 * SGU_BLOCK),
        wmt=jnp.transpose(p["sgu_w"], (0, 1, 3, 2)).reshape(L, 4 * SGU_BLOCK, SGU_BLOCK),
        sb=jnp.repeat(jnp.transpose(p["sgu_bias"], (0, 2, 1)), HEAD, axis=2),
        v1024=rows_to(jnp.stack([p["b_out"], p["ln_g"], p["ln_b"]], axis=1), 8),
        bin=p["b_in"][:, None, :])


def _unpack_small(sm):
    L = sm.shape[0]
    owc = jnp.concatenate([sm[:, ROW_WC:ROW_WC + SGU_BLOCK], sm[:, ROW_WC + SGU_BLOCK:ROW_WC + 2 * SGU_BLOCK]], axis=2)
    return dict(
        conv_a_b=sm[:, 0], norm_a_g=sm[:, 1], norm_a_b=sm[:, 2], pool_scale=sm[:, 3], sgu_ln_g=sm[:, 4],
        sgu_ln_b=sm[:, 5], conv_b_w=sm[:, ROW_CBW:ROW_CBW + KB], conv_a_w=sm[:, ROW_CAW:ROW_CAW + KA],
        pool_w=jnp.transpose(sm[:, ROW_PW:ROW_PW + HEAD].reshape(L, HEAD, 4, HEAD), (0, 2, 1, 3)),
        ln_g=sm[:, ROW_LNG:ROW_LNG + 4].reshape(L, D_MODEL), ln_b=sm[:, ROW_LNB:ROW_LNB + 4].reshape(L, D_MODEL),
        b_out=sm[:, ROW_BOUT:ROW_BOUT + 4].reshape(L, D_MODEL),
        b_in=sm[:, ROW_BIN:ROW_BIN + N_SLICES].reshape(L, IN_WIDTH),
        sgu_w=jnp.transpose(owc.reshape(L, SGU_BLOCK, 4, SGU_BLOCK), (0, 2, 1, 3)),
        sgu_bias=sm[:, ROW_SB:ROW_SB + 4, 0:SGU_BLOCK])


def _step(p, m, v, x, target, *, tile_f, tile_b, tk_in, tk_out):
    L = p["ln_g"].shape[0]
    xi, yi, ci = _place()
    me_k = 2 * xi + yi
    hi_rows, ho_rows = D_MODEL // 2, GROUP // 2

    cw = jnp.concatenate([p["conv_a_w"], p["conv_b_w"]], axis=1).reshape(-1, 128)
    cw_rows = cw.shape[0]
    cw = _pad_rows(cw, -(-cw_rows // SUBLANES) * SUBLANES)
    wi16 = p["w_in"].astype(BF16)
    wo16 = p["w_out"].astype(BF16)
    wig0, wog0, cwg = _gather_weights(wi16[0:1], wo16[0:1], cw)
    cwg = cwg[:, :cw_rows].reshape(N_CHIPS, L, KA + KB, HEAD)
    conv_full = jnp.transpose(cwg, (1, 2, 0, 3)).reshape(L, KA + KB, GROUP)
    seg, e4 = _indicator_consts()
    k = _layer_consts(p, conv_full)
    layer = [jnp.full((1,), l, jnp.int32) for l in range(L)]

    hcur = x
    saved, wig, wog = [], [wig0[0]], [wog0[0]]
    for l in range(L):
        nxt = (wi16, wo16) if l + 1 < L else None
        outs = _fwd_layer(layer[l], hcur, wig[l], k["bin"], k["caw"], k["cbw"], k["s256"], seg, k["pw"], k["wm"], k["sb"],
                          wog[l], k["v1024"], tile=tile_f, nxt=nxt, target=None if nxt is not None else target)
        y, xb, h, aux, mixb, z = outs[0:6]
        if nxt is not None:
            wig.append(outs[6])
            wog.append(outs[7])
        saved.append((xb, h, aux, mixb, z))
        hcur = y

    dy = hcur
    loss_local = outs[6][0, 0]

    gwi = lax.empty((L, N_CHIPS, D_MODEL, COLS), F32)
    gwo = lax.empty((L, N_CHIPS, GROUP, D_MODEL), F32)
    gwi16 = lax.empty((L, N_CHIPS, D_MODEL, COLS), BF16)
    gwo16 = lax.empty((L, N_CHIPS, GROUP, D_MODEL), BF16)
    ri = lax.empty((L, N_CHIPS, hi_rows, COLS), BF16)
    ro = lax.empty((L, N_CHIPS, ho_rows, D_MODEL), BF16)
    p_i = lax.empty((L, N_CHIPS, hi_rows, COLS), BF16)
    p_o = lax.empty((L, N_CHIPS, ho_rows, D_MODEL), BF16)
    q_i = lax.empty((3, L, hi_rows, COLS), BF16)
    q_o = lax.empty((3, L, ho_rows, D_MODEL), BF16)
    r_sm = [None] * L
    pending = None
    for l in reversed(range(L)):
        xb, h, aux, mixb, z = saved[l]
        exch = None if pending is None else (p_i, p_o, pending, q_i, q_o)
        outs = _bwd_layer(layer[l], dy, z, h, aux, wig[l], k["caw"], k["cbw"], k["s256"], seg, k["pw"], k["wm"],
                          k["wmt"], k["sb"], wog[l], k["v1024"], e4, tile=tile_b, exch=exch)
        dy, dhb, dzb, osm = outs[0:4]
        if l == L - 1:
            osm = osm.at[ROW_LOSS, 0].set(loss_local)
        if exch is not None:
            q_i, q_o, r_sm[l + 1] = outs[4:7]
        larr = layer[l]
        if l > 0:
            gwi, gwi16 = _dw_in(larr, xb, dhb, gwi, gwi16, tk=tk_in)
        else:
            gwi, gwi16, r_sm[0] = _dw_in(larr, xb, dhb, gwi, gwi16, tk=tk_out, small=osm)
        if l > 0:
            gwo, gwo16 = _dw_out(larr, mixb, dzb, gwo, gwo16, tk=tk_out)
        else:
            gwo, gwo16, dy = _dw_out(larr, mixb, dzb, gwo, gwo16, tk=tk_out, carry=dy)
        cl_arr = jnp.stack([ci, jnp.int32(l)]).astype(jnp.int32)
        p_i, p_o = _swap_add(cl_arr, gwi, gwi16, p_i, gwo, gwo16, p_o)
        pending = osm
    grad_x = dy
    q_i, q_o = _exchange_last(layer[0], p_i, p_o, q_i, q_o)

    summed = _sum_small(r_sm)
    loss = summed[L - 1, ROW_LOSS, 0]
    grads = _unpack_small(summed)
    for n in ("conv_a_w", "conv_b_w"):
        grads[n] = lax.dynamic_slice_in_dim(grads[n], me_k * HEAD, HEAD, axis=2)

    kc_arr = jnp.stack([me_k, ci]).astype(jnp.int32)
    g_i, g_o = _sum_chunks(kc_arr, p_i, q_i, p_o, q_o, nb=2)
    g_i, g_o = _share_result(g_i, g_o)
    grads["w_in"] = g_i
    grads["w_out"] = g_o

    delta, new_m, new_v = {}, {}, {}
    for n, tr in (("w_in", 512), ("w_out", 256)):
        shp = p[n].shape
        args = [a.reshape(shp[0] * shp[1], shp[2]) for a in (p[n], grads[n], m[n], v[n])]
        outs = _adamw(*args, rows_per_step=tr, name="adamw_" + n, copy_g=True)
        delta[n], new_m[n], new_v[n], grads[n] = (a.reshape(shp) for a in outs)
    small = [n for n in WEIGHTS if n not in ("w_in", "w_out")]
    flat = [[a[n].reshape(-1, a[n].shape[-1]) for n in small] for a in (p, grads, m, v)]
    outs = _adamw_small(*flat)
    for j, n in enumerate(small):
        delta[n], new_m[n], new_v[n] = (o[j].reshape(p[n].shape) for o in outs)

    return (loss, grad_x[None], *[grads[n] for n in WEIGHTS], *[delta[n] for n in WEIGHTS],
            *[new_m[n] for n in WEIGHTS], *[new_v[n] for n in WEIGHTS])


def kernel(x, ln_g, ln_b, w_in, b_in, conv_a_w, conv_a_b, norm_a_g, norm_a_b, conv_b_w, pool_w, pool_scale, sgu_ln_g, sgu_ln_b, sgu_w, sgu_bias, w_out, b_out, loss_target, m_ln_g, m_ln_b, m_w_in, m_b_in, m_conv_a_w, m_conv_a_b, m_norm_a_g, m_norm_a_b, m_conv_b_w, m_pool_w, m_pool_scale, m_sgu_ln_g, m_sgu_ln_b, m_sgu_w, m_sgu_bias, m_w_out, m_b_out, v_ln_g, v_ln_b, v_w_in, v_b_in, v_conv_a_w, v_conv_a_b, v_norm_a_g, v_norm_a_b, v_conv_b_w, v_pool_w, v_pool_scale, v_sgu_ln_g, v_sgu_ln_b, v_sgu_w, v_sgu_bias, v_w_out, v_b_out):
    p = dict(ln_g=ln_g, ln_b=ln_b, w_in=w_in, b_in=b_in, conv_a_w=conv_a_w, conv_a_b=conv_a_b, norm_a_g=norm_a_g,
             norm_a_b=norm_a_b, conv_b_w=conv_b_w, pool_w=pool_w, pool_scale=pool_scale, sgu_ln_g=sgu_ln_g,
             sgu_ln_b=sgu_ln_b, sgu_w=sgu_w, sgu_bias=sgu_bias, w_out=w_out, b_out=b_out)
    m = dict(ln_g=m_ln_g, ln_b=m_ln_b, w_in=m_w_in, b_in=m_b_in, conv_a_w=m_conv_a_w, conv_a_b=m_conv_a_b,
             norm_a_g=m_norm_a_g, norm_a_b=m_norm_a_b, conv_b_w=m_conv_b_w, pool_w=m_pool_w, pool_scale=m_pool_scale,
             sgu_ln_g=m_sgu_ln_g, sgu_ln_b=m_sgu_ln_b, sgu_w=m_sgu_w, sgu_bias=m_sgu_bias, w_out=m_w_out, b_out=m_b_out)
    v = dict(ln_g=v_ln_g, ln_b=v_ln_b, w_in=v_w_in, b_in=v_b_in, conv_a_w=v_conv_a_w, conv_a_b=v_conv_a_b,
             norm_a_g=v_norm_a_g, norm_a_b=v_norm_a_b, conv_b_w=v_conv_b_w, pool_w=v_pool_w, pool_scale=v_pool_scale,
             sgu_ln_g=v_sgu_ln_g, sgu_ln_b=v_sgu_ln_b, sgu_w=v_sgu_w, sgu_bias=v_sgu_bias, w_out=v_w_out, b_out=v_b_out)
    return _step(p, m, v, x[0], loss_target[0], tile_f=256, tile_b=256, tk_in=4096, tk_out=2048)
```

```python
import jax
import jax.numpy as jnp
from jax import lax
from jax.experimental import pallas as pl
from jax.experimental.pallas import tpu as pltpu

F32 = jnp.float32
BF16 = jnp.bfloat16
MESH = pl.DeviceIdType.MESH

D_MODEL = 1024
GROUP = 256
HEAD = 64
N_SLICES = 12
IN_WIDTH = N_SLICES * GROUP
N_CHIPS = 4
COLS = IN_WIDTH // N_CHIPS
KA = 31
KB = 3
SUBLANES = 8
HALO_A, HALO_B, HALO_C = 32, 8, 16
N_GATHER_SEMS = 12
N_EXCH_SEMS = 13
SGU_BLOCK = 128
CHUNK = 64
LN_EPS = 1e-5
ROWS = 64
V7X_VMEM_BYTES = 64 * 1024 * 1024
VMEM_LIMIT = V7X_VMEM_BYTES - 8 * 1024 * 1024

ADAM_LR, ADAM_B1, ADAM_B2, ADAM_EPS, ADAM_WD, ADAM_STEP = 0.001, 0.9, 0.999, 1e-08, 0.01, 10


ANY = pl.BlockSpec(memory_space=pl.ANY)


def _vmem_params(**kw):
    return pltpu.CompilerParams(vmem_limit_bytes=VMEM_LIMIT, **kw)


def _whole(a):
    return pl.BlockSpec(a.shape, lambda i, l, _n=a.ndim: (0,) * _n)


def _of_layer(a):
    return pl.BlockSpec((None,) + a.shape[1:], lambda i, l, _n=a.ndim: (l[0],) + (0,) * (_n - 1))


def _place():
    return lax.axis_index("x"), lax.axis_index("y"), lax.axis_index("c")


def _other_chips(x, y):
    return [(1 - x, y, 2 * (1 - x) + y), (x, 1 - y, 2 * x + (1 - y)), (1 - x, 1 - y, 2 * (1 - x) + (1 - y))]


def _sig(v):
    return 0.5 * jnp.tanh(0.5 * v) + 0.5


def _dot(a, b):
    return jnp.dot(a, b, preferred_element_type=F32)


def _dot_nt(a, b):
    return lax.dot_general(a, b, (((1,), (1,)), ((), ())), preferred_element_type=F32)


def _dot_tn(a, b):
    return lax.dot_general(a, b, (((0,), (0,)), ((), ())), preferred_element_type=F32)


def _segdot(v, m):
    hi = v.astype(BF16)
    lo = (v - hi.astype(F32)).astype(BF16)
    return _dot(hi, m) + _dot(lo, m)


def _colsum(v):
    return jnp.sum(v, axis=0, keepdims=True)


def _rowmean(v):
    return jnp.mean(v, axis=-1, keepdims=True)


def _lane_group(n):
    return lax.broadcasted_iota(jnp.int32, (1, n), 1) // HEAD


def _pool_cnt(tile, t_rows):
    pos = tile * t_rows + lax.broadcasted_iota(jnp.int32, (t_rows, GROUP), 0) + 1
    grp = lax.broadcasted_iota(jnp.int32, (t_rows, GROUP), 1) // HEAD
    win = jnp.where(grp == 0, 2, jnp.where(grp == 1, 4, jnp.where(grp == 2, 8, 16)))
    return jnp.minimum(pos, win).astype(F32)


def _sgu_masks(wm_ref, wmt_ref, wm_s, wmt_s):
    r = lax.broadcasted_iota(jnp.int32, (SGU_BLOCK, 4 * SGU_BLOCK), 0) // CHUNK
    c = (lax.broadcasted_iota(jnp.int32, (SGU_BLOCK, 4 * SGU_BLOCK), 1) % SGU_BLOCK) // CHUNK
    wm_s[...] = jnp.where(c <= r, wm_ref[...], 0.0).astype(BF16)
    if wmt_ref is not None:
        rt = (lax.broadcasted_iota(jnp.int32, (4 * SGU_BLOCK, SGU_BLOCK), 0) % SGU_BLOCK) // CHUNK
        ct = lax.broadcasted_iota(jnp.int32, (4 * SGU_BLOCK, SGU_BLOCK), 1) // CHUNK
        wmt_s[...] = jnp.where(rt <= ct, wmt_ref[...], 0.0).astype(BF16)


def _vstack(v_blk):
    grp = _lane_group(GROUP)
    return jnp.concatenate([jnp.where(grp == h, v_blk, 0.0) for h in range(4)], axis=0).astype(BF16)


def _gather_next(step, nt, nwi, nwo, gwi, gwo, send_sems, recv_sems, loc_sems, vwi, vwo):
    x, y, c = _place()
    me_k = 2 * x + y
    sibling = (x, y, 1 - c)
    chips = _other_chips(x, y)
    hi, ho = D_MODEL // 2, GROUP // 2
    fwd_sems = N_GATHER_SEMS // 2

    def rc(src, dst, sem, to):
        return pltpu.make_async_remote_copy(src_ref=src, dst_ref=dst, send_sem=send_sems.at[sem],
                                            recv_sem=recv_sems.at[sem], device_id=to, device_id_type=MESH)

    def blk(ref, k, n, cc):
        return ref.at[k, pl.ds(cc * n, n), :]

    def ici(r):
        px, py, _ = chips[r]
        to = (px, py, c)
        return [rc(nwi.at[pl.ds(c * hi, hi), :], blk(gwi, me_k, hi, c), 2 * r, to),
                rc(nwo.at[pl.ds(c * ho, ho), :], blk(gwo, me_k, ho, c), 2 * r + 1, to)]

    def landed(r, cc, base):
        pk = chips[r][2]
        return [rc(blk(gwi, pk, hi, cc), blk(gwi, pk, hi, cc), base + 2 * r, sibling),
                rc(blk(gwo, pk, ho, cc), blk(gwo, pk, ho, cc), base + 2 * r + 1, sibling)]

    def stage_in():
        return [pltpu.make_async_copy(nwi, vwi, loc_sems.at[0]), pltpu.make_async_copy(nwo, vwo, loc_sems.at[1])]

    def local():
        return [pltpu.make_async_copy(vwi, gwi.at[me_k], loc_sems.at[2]),
                pltpu.make_async_copy(vwo, gwo.at[me_k], loc_sems.at[3])]

    @pl.when(step == 0)
    def _():
        for cp in stage_in():
            cp.start()
        for r in range(3):
            for cp in ici(r):
                cp.start()

    @pl.when(step == 1)
    def _():
        for cp in stage_in():
            cp.wait()
        for cp in local():
            cp.start()

    @pl.when(step == (3 * nt) // 4)
    def _():
        for r in range(3):
            for got, fwd in zip(landed(r, c, 0), landed(r, c, fwd_sems)):
                got.wait_recv()
                fwd.start()

    @pl.when(step == nt - 1)
    def _():
        for r in range(3):
            for got in landed(r, 1 - c, fwd_sems):
                got.wait_recv()
        for r in range(3):
            for cp in ici(r) + landed(r, c, fwd_sems):
                cp.wait_send()
        for cp in local():
            cp.wait()


def _fwd_layer(larr, x, wi, bin_, caw, cbw, s256, seg, pw, wm, sb, wo, v1024, *, tile, nxt=None, target=None):
    assert nxt is None or target is None
    S = x.shape[0]
    T = tile
    nt = S // T
    alpha = float((2.0 * 4) ** 0.25)
    n_in = 13 + (2 if nxt is not None else 0) + (1 if target is not None else 0)
    n_out = 6 + (2 if nxt is not None else 0) + (1 if target is not None else 0)

    def body(*refs):
        l_ref = refs[0]
        (x_ref, wi_ref, bin_ref, caw_ref, cbw_ref, s256_ref, seg_ref, pw_ref, wm_ref, sb_ref, wo_ref,
         v1024_ref) = refs[1:13]
        y_ref, xb_ref, h_ref, aux_ref, mix_ref, z_ref = refs[n_in:n_in + 6]
        abuf, bbuf, cbuf, wm_s, shf = refs[n_in + n_out:n_in + n_out + 5]
        i = pl.program_id(0)
        if nxt is not None:
            _gather_next(i, nt, refs[13].at[l_ref[0] + 1], refs[14].at[l_ref[0] + 1], refs[n_in + 6], refs[n_in + 7],
                         *refs[n_in + n_out + 5:])

        @pl.when(i == 0)
        def _():
            abuf[0:HALO_A, :] = jnp.zeros((HALO_A, GROUP), F32)
            bbuf[0:HALO_B, :] = jnp.zeros((HALO_B, GROUP), F32)
            cbuf[0:HALO_C, :] = jnp.zeros((HALO_C, GROUP), F32)
            _sgu_masks(wm_ref, None, wm_s, None)

        x = x_ref[...]
        xb = x.astype(BF16)
        xb_ref[...] = xb
        for k in range(N_CHIPS):
            h_ref[:, COLS * k:COLS * (k + 1)] = _dot(xb, wi_ref[k]) + bin_ref[:, COLS * k:COLS * (k + 1)]

        def hs(j):
            return h_ref[:, GROUP * j:GROUP * (j + 1)]

        abuf[HALO_A:HALO_A + T, :] = hs(0) * _sig(hs(1))
        span = T + HALO_A - SUBLANES
        for p in range(1, SUBLANES):
            shf[p - 1, :, :] = abuf[p:p + span, :]
        for r0 in range(0, T, ROWS):
            acc = None
            for k in range(KA):
                off = HALO_A - (KA - 1) + k
                p, q8 = off % SUBLANES, off - off % SUBLANES
                win = abuf[r0 + q8:r0 + q8 + ROWS, :] if p == 0 else shf[p - 1, r0 + q8:r0 + q8 + ROWS, :]
                term = caw_ref[k:k + 1, :] * win
                acc = term if acc is None else acc + term
            aux_ref[r0:r0 + ROWS, 0:GROUP] = acc + s256_ref[0:1, :]
        abuf[0:HALO_A, :] = abuf[T:T + HALO_A, :]
        a1 = aux_ref[:, 0:GROUP]
        segm = seg_ref[...]
        cen = a1 - _segdot(a1, segm)
        var = _segdot(cen * cen, segm)
        a2 = cen * lax.rsqrt(var + LN_EPS) * s256_ref[1:2, :] + s256_ref[2:3, :]
        az = hs(2)
        mix_ref[:, 0:GROUP] = (a2 * _sig(a2) * (az * _sig(az))).astype(BF16)

        bbuf[HALO_B:HALO_B + T, :] = hs(4) * hs(5)
        for r0 in range(0, T, ROWS):
            acc = None
            for k in range(KB):
                off = HALO_B - (KB - 1) + k + r0
                term = cbw_ref[k:k + 1, :] * bbuf[off:off + ROWS, :]
                acc = term if acc is None else acc + term
            aux_ref[r0:r0 + ROWS, GROUP:2 * GROUP] = acc
        bbuf[0:HALO_B, :] = bbuf[T:T + HALO_B, :]
        bz = hs(6)
        mix_ref[:, GROUP:2 * GROUP] = (hs(3) * aux_ref[:, GROUP:2 * GROUP] * (bz * _sig(bz))).astype(BF16)

        ch = hs(7)
        cbuf[HALO_C:HALO_C + T, :] = ch
        hi_lane = (lax.broadcasted_iota(jnp.int32, (1, 128), 1) // HEAD) == 1
        for r0 in range(0, T, ROWS):
            def win(col, j0, j1):
                s = None
                for j in range(j0, j1):
                    off = HALO_C - j + r0
                    term = cbuf[off:off + ROWS, 128 * col:128 * (col + 1)]
                    s = term if s is None else s + term
                return s
            w0 = win(0, 0, 2) + jnp.where(hi_lane, win(0, 2, 4), 0.0)
            w1 = win(1, 0, 8) + jnp.where(hi_lane, win(1, 8, 16), 0.0)
            aux_ref[r0:r0 + ROWS, 2 * GROUP:2 * GROUP + 128] = w0
            aux_ref[r0:r0 + ROWS, 2 * GROUP + 128:3 * GROUP] = w1
        cbuf[0:HALO_C, :] = cbuf[T:T + HALO_C, :]
        pooled = aux_ref[:, 2 * GROUP:3 * GROUP] / _pool_cnt(i, T) - ch
        aux_ref[:, 2 * GROUP:3 * GROUP] = pooled
        q = _dot(pooled.astype(BF16), pw_ref[...])
        cz = hs(8)
        mix_ref[:, 2 * GROUP:3 * GROUP] = (q * s256_ref[3:4, :] * (cz * _sig(cz))).astype(BF16)

        dv = hs(10)
        cen = dv - _rowmean(dv)
        var = _rowmean(cen * cen)
        v = cen * lax.rsqrt(var + LN_EPS) * s256_ref[4:5, :] + s256_ref[5:6, :]
        sps = []
        for n in range(T // SGU_BLOCK):
            vb = v[n * SGU_BLOCK:(n + 1) * SGU_BLOCK, :]
            sps.append(_dot(wm_s[...], _vstack(vb)) + sb_ref[...])
        sp = jnp.concatenate(sps, axis=0)
        dz = hs(11)
        mix_ref[:, 3 * GROUP:4 * GROUP] = (hs(9) * sp * (dz * _sig(dz))).astype(BF16)

        out = v1024_ref[0:1, :]
        for k in range(N_CHIPS):
            out = out + _dot(mix_ref[:, GROUP * k:GROUP * (k + 1)], wo_ref[k])
        z = alpha * x + out
        z_ref[...] = z
        cen = z - _rowmean(z)
        var = _rowmean(cen * cen)
        y = cen * lax.rsqrt(var + LN_EPS) * v1024_ref[1:2, :] + v1024_ref[2:3, :]
        if target is None:
            y_ref[...] = y
        else:
            t_ref, loss_ref = refs[13], refs[n_in + 6]

            @pl.when(i == 0)
            def _():
                loss_ref[...] = jnp.zeros_like(loss_ref)
            err = y - t_ref[...]
            y_ref[...] = err * (1.0 / D_MODEL)
            loss_ref[...] += jnp.sum(_colsum(err * err), axis=1, keepdims=True) * (0.5 / D_MODEL)

    def rows(width):
        return pl.BlockSpec((T, width), lambda i, l: (i, 0))

    consts = (wi, bin_, caw, cbw, s256, seg, pw, wm, sb, wo, v1024)
    in_specs = [rows(D_MODEL)] + [_whole(a) if a is wi or a is seg or a is wo else _of_layer(a) for a in consts]
    out_specs = [rows(D_MODEL), rows(D_MODEL), rows(IN_WIDTH), rows(3 * GROUP), rows(D_MODEL), rows(D_MODEL)]
    out_shape = [jax.ShapeDtypeStruct((S, D_MODEL), F32), jax.ShapeDtypeStruct((S, D_MODEL), BF16),
                 jax.ShapeDtypeStruct((S, IN_WIDTH), F32), jax.ShapeDtypeStruct((S, 3 * GROUP), F32),
                 jax.ShapeDtypeStruct((S, D_MODEL), BF16), jax.ShapeDtypeStruct((S, D_MODEL), F32)]
    scratch = [pltpu.VMEM((T + HALO_A, GROUP), F32), pltpu.VMEM((T + HALO_B, GROUP), F32),
               pltpu.VMEM((T + HALO_C, GROUP), F32), pltpu.VMEM((SGU_BLOCK, 4 * SGU_BLOCK), BF16),
               pltpu.VMEM((SUBLANES - 1, T + HALO_A - SUBLANES, GROUP), F32)]
    extra = ()
    if nxt is not None:
        extra = tuple(nxt)
        in_specs += [ANY, ANY]
        out_specs += [ANY, ANY]
        out_shape += [jax.ShapeDtypeStruct((N_CHIPS, D_MODEL, COLS), BF16),
                      jax.ShapeDtypeStruct((N_CHIPS, GROUP, D_MODEL), BF16)]
        scratch += [pltpu.SemaphoreType.DMA((N_GATHER_SEMS,)), pltpu.SemaphoreType.DMA((N_GATHER_SEMS,)),
                    pltpu.SemaphoreType.DMA((4,)), pltpu.VMEM((D_MODEL, COLS), BF16), pltpu.VMEM((GROUP, D_MODEL), BF16)]
    if target is not None:
        extra = (target,)
        in_specs += [rows(D_MODEL)]
        out_specs += [pl.BlockSpec((8, 128), lambda i, l: (0, 0))]
        out_shape += [jax.ShapeDtypeStruct((8, 128), F32)]
    grid_spec = pltpu.PrefetchScalarGridSpec(num_scalar_prefetch=1, grid=(nt,), in_specs=in_specs,
                                             out_specs=out_specs, scratch_shapes=scratch)
    return pl.pallas_call(
        body, name=("fwd_layer_loss" if target is not None else "fwd_layer") if nxt is None else "fwd_layer_gather",
        grid_spec=grid_spec, out_shape=out_shape,
        compiler_params=_vmem_params(dimension_semantics=("arbitrary",), has_side_effects=nxt is not None),
    )(larr, x, *consts, *extra)


ROW_CBW = 8
ROW_CAW = 16
ROW_LOSS = 7
ROW_PW = 48
ROW_LNG = 112
ROW_LNB = 116
ROW_BOUT = 120
ROW_BIN = 124
ROW_WC = 136
ROW_SB = 392
SM_ROWS = 400
N_DEV = 8


def _exchange_comm(start, finish, l, p_i, p_o, sm, r_i, r_o, r_sm, send_sems, recv_sems, loc_sem):
    x, y, c = _place()
    me = 4 * x + 2 * y + c
    chips = _other_chips(x, y)

    def rc(src, dst, sem, to):
        return pltpu.make_async_remote_copy(src_ref=src, dst_ref=dst, send_sem=send_sems.at[sem],
                                            recv_sem=recv_sems.at[sem], device_id=to, device_id_type=MESH)

    def big(r):
        px, py, pk = chips[r]
        to = (px, py, c)
        return [rc(p_i.at[l, pk], r_i.at[r, l], 2 * r, to), rc(p_o.at[l, pk], r_o.at[r, l], 2 * r + 1, to)]

    def peer(rel):
        px = 1 - x if rel & 4 else x
        py = 1 - y if rel & 2 else y
        pc = 1 - c if rel & 1 else c
        return (px, py, pc), 4 * px + 2 * py + pc

    def small_out(rel):
        to, _ = peer(rel)
        return rc(sm, r_sm.at[me], N_EXCH_SEMS - N_DEV + rel, to)

    def small_in(rel):
        to, idx = peer(rel)
        return rc(sm, r_sm.at[idx], N_EXCH_SEMS - N_DEV + rel, to)

    def local():
        return pltpu.make_async_copy(sm, r_sm.at[me], loc_sem.at[0])

    with_big, with_small = p_i is not None, sm is not None

    @pl.when(start)
    def _():
        if with_small:
            local().start()
        if with_big:
            for r in range(3):
                for cp in big(r):
                    cp.start()
        if with_small:
            for rel in range(1, N_DEV):
                small_out(rel).start()

    @pl.when(finish)
    def _():
        if with_big:
            for r in range(3):
                for cp in big(r):
                    cp.wait()
        if with_small:
            for rel in range(1, N_DEV):
                small_in(rel).wait_recv()
                small_out(rel).wait_send()
            local().wait()


RC = 32
RC_WIDE = 16
ACC_ROWS = 136


def _rsum8(v):
    r = v[0:8]
    for j in range(1, v.shape[0] // 8):
        r = r + v[8 * j:8 * j + 8]
    return r


def _bwd_layer(larr, dy, z, h, aux, wi, caw, cbw, s256, seg, pw, wm, wmt, sb, wo, v1024, e4, *, tile, exch=None):
    S = dy.shape[0]
    T = tile
    nt = S // T
    nblk = T // SGU_BLOCK
    alpha = float((2.0 * 4) ** 0.25)
    n_in = 17 + (5 if exch is not None else 0)
    n_out = 4 + (3 if exch is not None else 0)
    slab = pltpu.VMEM((T, GROUP), F32)
    scratch = dict(
        dbuf=pltpu.VMEM((T + HALO_A, GROUP), F32), ebuf=pltpu.VMEM((T + HALO_B, GROUP), F32),
        fbuf=pltpu.VMEM((T + HALO_C, GROUP), F32), sh=pltpu.VMEM((SUBLANES - 1, T + HALO_A - SUBLANES, GROUP), F32),
        wm_s=pltpu.VMEM((SGU_BLOCK, 4 * SGU_BLOCK), BF16), wmt_s=pltpu.VMEM((4 * SGU_BLOCK, SGU_BLOCK), BF16),
        dsp_acc=pltpu.VMEM((SGU_BLOCK, GROUP), F32), pw_acc=pltpu.VMEM((GROUP, GROUP), F32),
        acc_s=pltpu.VMEM((8 * ACC_ROWS, GROUP), F32), acc_w=pltpu.VMEM((24, D_MODEL), F32),
        dmix_s=pltpu.VMEM((T, D_MODEL), F32), vst_s=pltpu.VMEM((nblk, 4 * SGU_BLOCK, GROUP), BF16),
        dq_s=pltpu.VMEM((T, GROUP), BF16), dxt_s=pltpu.VMEM((D_MODEL, T), F32),
        mean_s=slab, t1_s=slab, t2_s=slab, q_s=slab, xv_s=slab, rv_s=slab, v_s=slab, sp_s=slab, a0_s=slab, sg_s=slab,
        xh_s=slab, ra_s=slab, ub_s=slab, dsp_s=slab, m1_s=slab, m2_s=slab, dpool_s=slab, dvd_s=slab, u_s=slab,
        du_s=slab, cw_s=slab)
    names = list(scratch)

    def body(*refs):
        (dy_ref, z_ref, h_ref, aux_ref, wi_ref, caw_ref, cbw_ref, s256_ref, seg_ref, pw_ref, wm_ref, wmt_ref,
         sb_ref, wo_ref, v1024_ref, e4_ref) = refs[1:17]
        dx_ref, dhb_ref, dzb_ref, osm_ref = refs[n_in:n_in + 4]
        k0 = n_in + n_out
        sc = dict(zip(names, refs[k0:k0 + len(names)]))
        dbuf, ebuf, fbuf, sh = sc["dbuf"], sc["ebuf"], sc["fbuf"], sc["sh"]
        wm_s, wmt_s, dsp_acc, pw_acc, acc_s, acc_w = (sc[n] for n in ("wm_s", "wmt_s", "dsp_acc", "pw_acc", "acc_s",
                                                                        "acc_w"))
        dmix_s, vst_s, dq_s = sc["dmix_s"], sc["vst_s"], sc["dq_s"]
        i = pl.program_id(0)
        tile_idx = nt - 1 - i
        if exch is not None:
            p_i, p_o, sm = refs[17:20]
            r_i, r_o, r_sm = refs[n_in + 4:n_in + 7]
            _exchange_comm(i == 0, i == nt - 1, refs[0][0] + 1, p_i, p_o, sm, r_i, r_o, r_sm, *refs[k0 + len(names):])

        @pl.when(i == 0)
        def _():
            dbuf[T:T + HALO_A, :] = jnp.zeros((HALO_A, GROUP), F32)
            ebuf[T:T + HALO_B, :] = jnp.zeros((HALO_B, GROUP), F32)
            fbuf[T:T + HALO_C, :] = jnp.zeros((HALO_C, GROUP), F32)
            _sgu_masks(wm_ref, wmt_ref, wm_s, wmt_s)
            osm_ref[...] = jnp.zeros_like(osm_ref)
            dsp_acc[...] = jnp.zeros_like(dsp_acc)
            pw_acc[...] = jnp.zeros_like(pw_acc)
            acc_s[...] = jnp.zeros_like(acc_s)
            acc_w[...] = jnp.zeros_like(acc_w)

        def chunks(rc, fn):
            for c in range(T // rc):
                fn(pl.ds(c * rc, rc))

        def hs(j, rows):
            return h_ref[rows, GROUP * j:GROUP * (j + 1)]

        def acc_add(row, val):
            acc_s[8 * row:8 * row + 8, :] += _rsum8(val)

        def put_dh(j, rows, val):
            acc_add(ROW_BIN + j, val)
            dhb_ref[rows, GROUP * j:GROUP * (j + 1)] = val.astype(BF16)

        def dsilu(v, s):
            return s * (1.0 + v * (1.0 - s))

        def vec(r):
            return s256_ref[r:r + 1, :]

        def ln_bwd(rows):
            dyc = dy_ref[rows, :]
            zc = z_ref[rows, :]
            cen = zc - _rowmean(zc)
            rstd = lax.rsqrt(_rowmean(cen * cen) + LN_EPS)
            xhat = cen * rstd
            acc_w[0:8, :] += _rsum8(dyc * xhat)
            acc_w[8:16, :] += _rsum8(dyc)
            gdy = dyc * v1024_ref[1:2, :]
            dz = rstd * (gdy - _rowmean(gdy) - xhat * _rowmean(gdy * xhat))
            acc_w[16:24, :] += _rsum8(dz)
            dzb_ref[rows, :] = dz.astype(BF16)
            dx_ref[rows, :] = alpha * dz
        chunks(RC_WIDE, ln_bwd)

        segm = seg_ref[...]
        dzb = dzb_ref[...]
        for k in range(N_CHIPS):
            dmix_s[:, GROUP * k:GROUP * (k + 1)] = _dot_nt(dzb, wo_ref[k])
        sc["mean_s"][...] = _segdot(aux_ref[:, 0:GROUP], segm)
        pooled_b = aux_ref[:, 2 * GROUP:3 * GROUP].astype(BF16)
        sc["q_s"][...] = _dot(pooled_b, pw_ref[...])

        def centre(rows):
            cen = aux_ref[rows, 0:GROUP] - sc["mean_s"][rows, :]
            sc["t1_s"][rows, :] = cen * cen
            dv_in = hs(10, rows)
            cen_v = dv_in - _rowmean(dv_in)
            rstd_v = lax.rsqrt(_rowmean(cen_v * cen_v) + LN_EPS)
            xv = cen_v * rstd_v
            sc["xv_s"][rows, :] = xv
            sc["rv_s"][rows, :] = jnp.broadcast_to(rstd_v, xv.shape)
            sc["v_s"][rows, :] = xv * vec(4) + vec(5)
        chunks(RC, centre)

        sc["t2_s"][...] = _segdot(sc["t1_s"][...], segm)
        for n in range(nblk):
            blk = slice(n * SGU_BLOCK, (n + 1) * SGU_BLOCK)
            vst_s[n] = _vstack(sc["v_s"][blk, :])
            sc["sp_s"][blk, :] = _dot(wm_s[...], vst_s[n]) + sb_ref[...]

        def mixers(rows):
            a_val, a_glu, a_z = hs(0, rows), hs(1, rows), hs(2, rows)
            sg = _sig(a_glu)
            sc["a0_s"][rows, :] = a_val * sg
            sc["sg_s"][rows, :] = sg
            rstd_a = lax.rsqrt(sc["t2_s"][rows, :] + LN_EPS)
            xh = (aux_ref[rows, 0:GROUP] - sc["mean_s"][rows, :]) * rstd_a
            a2 = xh * vec(1) + vec(2)
            s2 = _sig(a2)
            sz = _sig(a_z)
            dya = dmix_s[rows, 0:GROUP]
            put_dh(2, rows, dya * (a2 * s2) * dsilu(a_z, sz))
            d_a2 = dya * (a_z * sz) * dsilu(a2, s2)
            acc_add(1, d_a2 * xh)
            acc_add(2, d_a2)
            gd = d_a2 * vec(1)
            sc["t1_s"][rows, :] = gd
            sc["t2_s"][rows, :] = gd * xh
            sc["xh_s"][rows, :] = xh
            sc["ra_s"][rows, :] = rstd_a
            b_b, b_c, b_h, b_z = hs(3, rows), hs(4, rows), hs(5, rows), hs(6, rows)
            cb = aux_ref[rows, GROUP:2 * GROUP]
            sz = _sig(b_z)
            dyb = dmix_s[rows, GROUP:2 * GROUP]
            put_dh(3, rows, dyb * cb * (b_z * sz))
            put_dh(6, rows, dyb * b_b * cb * dsilu(b_z, sz))
            ebuf[rows, :] = dyb * b_b * (b_z * sz)
            sc["ub_s"][rows, :] = b_c * b_h
            c_z = hs(8, rows)
            q = sc["q_s"][rows, :]
            sz = _sig(c_z)
            dyc = dmix_s[rows, 2 * GROUP:3 * GROUP]
            acc_add(3, dyc * q * (c_z * sz))
            put_dh(8, rows, dyc * q * vec(3) * dsilu(c_z, sz))
            dq_s[rows, :] = (dyc * vec(3) * (c_z * sz)).astype(BF16)
            d_u, d_z = hs(9, rows), hs(11, rows)
            sp = sc["sp_s"][rows, :]
            sz = _sig(d_z)
            dyd = dmix_s[rows, 3 * GROUP:4 * GROUP]
            put_dh(9, rows, dyd * sp * (d_z * sz))
            put_dh(11, rows, dyd * d_u * sp * dsilu(d_z, sz))
            sc["dsp_s"][rows, :] = dyd * d_u * (d_z * sz)
        chunks(RC, mixers)

        sc["m1_s"][...] = _segdot(sc["t1_s"][...], segm)
        sc["m2_s"][...] = _segdot(sc["t2_s"][...], segm)
        d_q = dq_s[...]
        pw_acc[...] += _dot_tn(pooled_b, d_q)
        sc["dpool_s"][...] = _dot_nt(d_q, pw_ref[...])
        grp = _lane_group(GROUP)
        for n in range(nblk):
            blk = slice(n * SGU_BLOCK, (n + 1) * SGU_BLOCK)
            dspb = sc["dsp_s"][blk, :]
            dsp_acc[...] += dspb
            dspb16 = dspb.astype(BF16)
            dvst = _dot(wmt_s[...], dspb16)
            dvb = None
            for hh in range(4):
                part = jnp.where(grp == hh, dvst[hh * SGU_BLOCK:(hh + 1) * SGU_BLOCK, :], 0.0)
                dvb = part if dvb is None else dvb + part
            sc["dvd_s"][blk, :] = dvb
            dwc = _dot_nt(dspb16, vst_s[n])
            osm_ref[ROW_WC:ROW_WC + SGU_BLOCK, :] += dwc[:, 0:GROUP]
            osm_ref[ROW_WC + SGU_BLOCK:ROW_WC + 2 * SGU_BLOCK, :] += dwc[:, GROUP:2 * GROUP]

        def ln_sums(rows):
            xh = sc["xh_s"][rows, :]
            d_a1 = sc["ra_s"][rows, :] * (sc["t1_s"][rows, :] - sc["m1_s"][rows, :] - xh * sc["m2_s"][rows, :])
            acc_add(0, d_a1)
            dbuf[rows, :] = d_a1
            pos = tile_idx * T + rows.start + lax.broadcasted_iota(jnp.int32, (RC, GROUP), 0) + 1
            lane = lax.broadcasted_iota(jnp.int32, (RC, GROUP), 1) // HEAD
            win = jnp.where(lane == 0, 2, jnp.where(lane == 1, 4, jnp.where(lane == 2, 8, 16)))
            fbuf[rows, :] = sc["dpool_s"][rows, :] / jnp.minimum(pos, win).astype(F32)
            d_v = sc["dvd_s"][rows, :]
            xv = sc["xv_s"][rows, :]
            acc_add(4, d_v * xv)
            acc_add(5, d_v)
            gd = d_v * vec(4)
            put_dh(10, rows, sc["rv_s"][rows, :] * (gd - _rowmean(gd) - xv * _rowmean(gd * xv)))
        chunks(RC, ln_sums)

        span = T + HALO_A - SUBLANES
        for p in range(1, SUBLANES):
            sh[p - 1, :, :] = dbuf[p:p + span, :]

        for r0 in range(0, T, ROWS):
            uc = sc["ub_s"][r0:r0 + ROWS, :]
            acc = None
            for k in range(KB):
                off = (KB - 1) - k + r0
                w = ebuf[off:off + ROWS, :]
                term = cbw_ref[k:k + 1, :] * w
                acc = term if acc is None else acc + term
                acc_add(ROW_CBW + k, uc * w)
            sc["du_s"][r0:r0 + ROWS, :] = acc
        ebuf[T:T + HALO_B, :] = ebuf[0:HALO_B, :]

        hi_lane = (lax.broadcasted_iota(jnp.int32, (1, 128), 1) // HEAD) == 1
        for r0 in range(0, T, ROWS):
            def win(col, j0, j1):
                s = None
                for j in range(j0, j1):
                    term = fbuf[r0 + j:r0 + j + ROWS, 128 * col:128 * (col + 1)]
                    s = term if s is None else s + term
                return s
            sc["cw_s"][r0:r0 + ROWS, 0:128] = win(0, 0, 2) + jnp.where(hi_lane, win(0, 2, 4), 0.0)
            sc["cw_s"][r0:r0 + ROWS, 128:256] = win(1, 0, 8) + jnp.where(hi_lane, win(1, 8, 16), 0.0)
        fbuf[T:T + HALO_C, :] = fbuf[0:HALO_C, :]

        def rest_bc(rows):
            d_u = sc["du_s"][rows, :]
            put_dh(4, rows, d_u * hs(5, rows))
            put_dh(5, rows, d_u * hs(4, rows))
            put_dh(7, rows, sc["cw_s"][rows, :] - sc["dpool_s"][rows, :])
        chunks(RC, rest_bc)

        dxt_s = sc["dxt_s"]

        def dx_term(k):
            term = _dot_nt(wi_ref[k], dhb_ref[:, COLS * k:COLS * (k + 1)])
            if k == 1:
                dxt_s[...] = term
            else:
                dxt_s[...] += term

        def conv_a(rows):
            a0c = sc["a0_s"][rows, :]
            acc = None
            for k in range(KA):
                off = (KA - 1) - k
                p, q8 = off % SUBLANES, off - off % SUBLANES
                w = dbuf[pl.ds(rows.start + q8, RC), :] if p == 0 else sh[p - 1, pl.ds(rows.start + q8, RC), :]
                term = caw_ref[k:k + 1, :] * w
                acc = term if acc is None else acc + term
                acc_add(ROW_CAW + k, a0c * w)
            sc["u_s"][rows, :] = acc
        n_chunks = T // RC
        after = {(n_chunks * j) // 3: j + 1 for j in range(3)}
        for c in range(n_chunks):
            conv_a(pl.ds(c * RC, RC))
            if c in after:
                dx_term(after[c])
        dbuf[T:T + HALO_A, :] = dbuf[0:HALO_A, :]

        def rest_a(rows):
            d_a0 = sc["u_s"][rows, :]
            sg = sc["sg_s"][rows, :]
            put_dh(0, rows, d_a0 * sg)
            put_dh(1, rows, d_a0 * hs(0, rows) * sg * (1.0 - sg))
        chunks(RC, rest_a)
        dx_term(0)
        dx_ref[...] += dxt_s[...].T

        @pl.when(i == nt - 1)
        def _():
            for row in list(range(6)) + list(range(ROW_CBW, ROW_CBW + KB)) + list(range(ROW_CAW, ROW_CAW + KA)) + list(
                    range(ROW_BIN, ROW_BIN + N_SLICES)):
                osm_ref[row:row + 1, :] = _colsum(acc_s[8 * row:8 * row + 8, :])
            for j, row in enumerate((ROW_LNG, ROW_LNB, ROW_BOUT)):
                cs = _colsum(acc_w[8 * j:8 * j + 8, :])
                for q in range(D_MODEL // GROUP):
                    osm_ref[row + q:row + q + 1, :] = cs[:, GROUP * q:GROUP * (q + 1)]
            r = lax.broadcasted_iota(jnp.int32, (SGU_BLOCK, GROUP), 0) // CHUNK
            c = (lax.broadcasted_iota(jnp.int32, (SGU_BLOCK, GROUP), 1) % SGU_BLOCK) // CHUNK
            for half in range(2):
                rows_ = slice(ROW_WC + half * SGU_BLOCK, ROW_WC + (half + 1) * SGU_BLOCK)
                osm_ref[rows_, :] = jnp.where(c <= r, osm_ref[rows_, :], 0.0)
            sb_t = _segdot(dsp_acc[...], e4_ref[...]).T
            osm_ref[ROW_SB:ROW_SB + 8, 0:SGU_BLOCK] = sb_t[0:8, :]
            for g in range(4):
                osm_ref[ROW_PW:ROW_PW + HEAD, HEAD * g:HEAD * (g + 1)] = (
                    pw_acc[HEAD * g:HEAD * (g + 1), HEAD * g:HEAD * (g + 1)])

    def rows(width):
        return pl.BlockSpec((T, width), lambda i, l: (nt - 1 - i, 0))

    consts = (wi, caw, cbw, s256, seg, pw, wm, wmt, sb, wo, v1024, e4)
    unstacked = (wi, seg, wo, e4)
    in_specs = [rows(D_MODEL), rows(D_MODEL), rows(IN_WIDTH), rows(3 * GROUP)] + [
        _whole(a) if any(a is u for u in unstacked) else _of_layer(a) for a in consts]
    out_specs = [rows(D_MODEL), rows(IN_WIDTH), rows(D_MODEL), pl.BlockSpec((SM_ROWS, GROUP), lambda i, l: (0, 0))]
    out_shape = [jax.ShapeDtypeStruct((S, D_MODEL), F32), jax.ShapeDtypeStruct((S, IN_WIDTH), BF16),
                 jax.ShapeDtypeStruct((S, D_MODEL), BF16), jax.ShapeDtypeStruct((SM_ROWS, GROUP), F32)]
    scratch_shapes = list(scratch.values())
    extra, aliases = (), {}
    if exch is not None:
        extra = tuple(exch)
        r_i, r_o = exch[3], exch[4]
        in_specs += [ANY] * 5
        out_specs += [ANY] * 3
        out_shape += [jax.ShapeDtypeStruct(r_i.shape, r_i.dtype), jax.ShapeDtypeStruct(r_o.shape, r_o.dtype),
                      jax.ShapeDtypeStruct((N_DEV, SM_ROWS, GROUP), F32)]
        scratch_shapes += [pltpu.SemaphoreType.DMA((N_EXCH_SEMS,)), pltpu.SemaphoreType.DMA((N_EXCH_SEMS,)),
                           pltpu.SemaphoreType.DMA((1,))]
        aliases = {20: 4, 21: 5}
    grid_spec = pltpu.PrefetchScalarGridSpec(num_scalar_prefetch=1, grid=(nt,), in_specs=in_specs,
                                             out_specs=out_specs, scratch_shapes=scratch_shapes)
    return pl.pallas_call(
        body, name="bwd_layer" if exch is None else "bwd_layer_exchange",
        grid_spec=grid_spec, out_shape=out_shape, input_output_aliases=aliases,
        compiler_params=_vmem_params(dimension_semantics=("arbitrary",), has_side_effects=exch is not None),
    )(larr, dy, z, h, aux, *consts, *extra)


def _dw_in(layer, xb, dhb, slab, slab16, *, tk, small=None):
    S = xb.shape[0]
    ns = S // tk

    def body(*refs):
        l_ref, a_ref, b_ref = refs[0:3]
        o_ref, o16_ref = refs[n_in:n_in + 2]
        if small is not None:
            first = (pl.program_id(0) == 0) & (pl.program_id(1) == 0)
            last = (pl.program_id(0) == N_CHIPS - 1) & (pl.program_id(1) == ns - 1)
            _exchange_comm(first, last, None, None, None, refs[5], None, None, refs[n_in + 2], *refs[n_in + 3:])

        @pl.when(pl.program_id(1) == 0)
        def _():
            o_ref[...] = jnp.zeros_like(o_ref)
        o_ref[...] += _dot_tn(a_ref[...], b_ref[...])

        @pl.when(pl.program_id(1) == ns - 1)
        def _():
            o16_ref[...] = o_ref[...].astype(BF16)

    o_spec = pl.BlockSpec((None, None, D_MODEL, COLS), lambda j, s, l: (l[0], j, 0, 0))
    in_specs = [pl.BlockSpec((tk, D_MODEL), lambda j, s, l: (s, 0)), pl.BlockSpec((tk, COLS), lambda j, s, l: (s, j)),
                ANY, ANY]
    out_specs = [o_spec, o_spec]
    out_shape = [jax.ShapeDtypeStruct(slab.shape, F32), jax.ShapeDtypeStruct(slab.shape, BF16)]
    scratch, extra = [], ()
    if small is not None:
        extra = (small,)
        in_specs += [ANY]
        out_specs += [ANY]
        out_shape += [jax.ShapeDtypeStruct((N_DEV, SM_ROWS, GROUP), F32)]
        scratch = [pltpu.SemaphoreType.DMA((N_EXCH_SEMS,)), pltpu.SemaphoreType.DMA((N_EXCH_SEMS,)), pltpu.SemaphoreType.DMA((1,))]
    n_in = 5 + len(extra)
    grid_spec = pltpu.PrefetchScalarGridSpec(
        num_scalar_prefetch=1, grid=(N_CHIPS, ns), in_specs=in_specs, out_specs=out_specs, scratch_shapes=scratch)
    return pl.pallas_call(
        body, name="dw_in" if small is None else "dw_in_exchange", grid_spec=grid_spec, out_shape=out_shape,
        input_output_aliases={3: 0, 4: 1},
        compiler_params=_vmem_params(dimension_semantics=("arbitrary", "arbitrary"), has_side_effects=small is not None),
    )(layer, xb, dhb, slab, slab16, *extra)


def _dw_out(layer, mixb, dzb, slab, slab16, *, tk):
    S = mixb.shape[0]
    ns = S // tk

    def body(l_ref, a_ref, b_ref, slab_ref, slab16_ref, o_ref, o16_ref):
        del l_ref, slab_ref, slab16_ref

        @pl.when(pl.program_id(0) == 0)
        def _():
            o_ref[...] = jnp.zeros_like(o_ref)
        o_ref[...] += _dot_tn(a_ref[...], b_ref[...]).reshape(N_CHIPS, GROUP, D_MODEL)

        @pl.when(pl.program_id(0) == ns - 1)
        def _():
            o16_ref[...] = o_ref[...].astype(BF16)

    o_spec = pl.BlockSpec((None, N_CHIPS, GROUP, D_MODEL), lambda s, l: (l[0], 0, 0, 0))
    grid_spec = pltpu.PrefetchScalarGridSpec(
        num_scalar_prefetch=1, grid=(ns,),
        in_specs=[pl.BlockSpec((tk, D_MODEL), lambda s, l: (s, 0)), pl.BlockSpec((tk, D_MODEL), lambda s, l: (s, 0)),
                  ANY, ANY],
        out_specs=[o_spec, o_spec])
    return pl.pallas_call(
        body, name="dw_out", grid_spec=grid_spec,
        out_shape=[jax.ShapeDtypeStruct(slab.shape, F32), jax.ShapeDtypeStruct(slab.shape, BF16)],
        input_output_aliases={3: 0, 4: 1},
        compiler_params=_vmem_params(dimension_semantics=("arbitrary",)),
    )(layer, mixb, dzb, slab, slab16)


def _adamw_math(w, g, m, v):
    nm = ADAM_B1 * m + (1.0 - ADAM_B1) * g
    nv = ADAM_B2 * v + (1.0 - ADAM_B2) * (g * g)
    c1 = 1.0 - ADAM_B1 ** ADAM_STEP
    c2 = 1.0 - ADAM_B2 ** ADAM_STEP
    return -ADAM_LR * ((nm / c1) / (jnp.sqrt(nv / c2) + ADAM_EPS) + ADAM_WD * w), nm, nv


def _adamw_small(ws, gs, ms, vs):
    n = len(ws)

    def body(*refs):
        for j in range(n):
            d, nm, nv = _adamw_math(*(refs[k * n + j][...] for k in range(4)))
            refs[4 * n + j][...] = d
            refs[5 * n + j][...] = nm
            refs[6 * n + j][...] = nv

    shapes = [jax.ShapeDtypeStruct(w.shape, F32) for w in ws]
    outs = pl.pallas_call(body, name="adamw_small", out_shape=shapes * 3, compiler_params=_vmem_params())(
        *ws, *gs, *ms, *vs)
    return outs[0:n], outs[n:2 * n], outs[2 * n:3 * n]


def _adamw(w, g, m, v, *, rows_per_step, name, copy_g=False):
    R, C = w.shape
    tr = rows_per_step

    def body(w_ref, g_ref, m_ref, v_ref, d_ref, nm_ref, nv_ref, *g_out):
        g_ = g_ref[...]
        d_ref[...], nm_ref[...], nv_ref[...] = _adamw_math(w_ref[...], g_, m_ref[...], v_ref[...])
        if copy_g:
            g_out[0][...] = g_

    spec = pl.BlockSpec((tr, C), lambda i: (i, 0))
    n_out = 4 if copy_g else 3
    return pl.pallas_call(
        body, name=name, grid=(R // tr,),
        in_specs=[spec] * 4, out_specs=[spec] * n_out,
        out_shape=[jax.ShapeDtypeStruct((R, C), F32)] * n_out,
        compiler_params=_vmem_params(dimension_semantics=("arbitrary",)),
    )(w, g, m, v)


def _gather_weights(wi16, wo16, cw):
    L = wi16.shape[0]
    hi_rows, ho_rows = D_MODEL // 2, GROUP // 2
    n_ici = 2 * L + 1
    n_fwd = 2 * L

    def body(wi_ref, wo_ref, cw_ref, *rest):
        wig = rest[0:L]
        wog = rest[L:2 * L]
        cwg = rest[2 * L]
        send_sems, recv_sems, loc_sems, vwi, vwo, vcw = rest[2 * L + 1:]
        x, y, c = _place()
        me_k = 2 * x + y
        sibling = (x, y, 1 - c)
        chips = _other_chips(x, y)

        def half_i(ref, blk):
            return ref.at[blk, pl.ds(c * hi_rows, hi_rows), :]

        def half_o(ref, blk):
            return ref.at[blk, pl.ds(c * ho_rows, ho_rows), :]

        def other_half_i(ref, blk):
            return ref.at[blk, pl.ds((1 - c) * hi_rows, hi_rows), :]

        def other_half_o(ref, blk):
            return ref.at[blk, pl.ds((1 - c) * ho_rows, ho_rows), :]

        stage_in = [pltpu.make_async_copy(wi_ref, vwi, loc_sems.at[0]), pltpu.make_async_copy(wo_ref, vwo, loc_sems.at[1]),
                    pltpu.make_async_copy(cw_ref, vcw, loc_sems.at[2])]
        local = []
        for l in range(L):
            local.append(pltpu.make_async_copy(vwi.at[l], wig[l].at[me_k], loc_sems.at[3 + 2 * l]))
            local.append(pltpu.make_async_copy(vwo.at[l], wog[l].at[me_k], loc_sems.at[3 + 2 * l + 1]))
        local.append(pltpu.make_async_copy(vcw, cwg.at[me_k], loc_sems.at[3 + 2 * L]))
        for cp in stage_in:
            cp.start()

        def remote(src, dst, sem, to):
            return pltpu.make_async_remote_copy(src_ref=src, dst_ref=dst, send_sem=send_sems.at[sem],
                                                recv_sem=recv_sems.at[sem], device_id=to, device_id_type=MESH)

        sends = []
        for r, (px, py, _) in enumerate(chips):
            to = (px, py, c)
            for l in range(L):
                sends.append(remote(half_i(wi_ref, l), half_i(wig[l], me_k), r * n_ici + 2 * l, to))
                sends.append(remote(half_o(wo_ref, l), half_o(wog[l], me_k), r * n_ici + 2 * l + 1, to))
            sends.append(remote(cw_ref, cwg.at[me_k], r * n_ici + 2 * L, to))
        for cp in sends:
            cp.start()
        for cp in stage_in:
            cp.wait()
        for cp in local:
            cp.start()

        base = 3 * n_ici
        fwds = []
        for r, (px, py, pk) in enumerate(chips):
            for l in range(L):
                remote(half_i(wig[l], pk), half_i(wig[l], pk), r * n_ici + 2 * l, sibling).wait_recv()
                f = remote(half_i(wig[l], pk), half_i(wig[l], pk), base + r * n_fwd + 2 * l, sibling)
                f.start()
                fwds.append(f)
                remote(half_o(wog[l], pk), half_o(wog[l], pk), r * n_ici + 2 * l + 1, sibling).wait_recv()
                f = remote(half_o(wog[l], pk), half_o(wog[l], pk), base + r * n_fwd + 2 * l + 1, sibling)
                f.start()
                fwds.append(f)
            remote(cwg.at[pk], cwg.at[pk], r * n_ici + 2 * L, sibling).wait_recv()
        for r, (px, py, pk) in enumerate(chips):
            for l in range(L):
                remote(other_half_i(wig[l], pk), other_half_i(wig[l], pk), base + r * n_fwd + 2 * l, sibling).wait_recv()
                remote(other_half_o(wog[l], pk), other_half_o(wog[l], pk), base + r * n_fwd + 2 * l + 1, sibling).wait_recv()
        for cp in sends + fwds:
            cp.wait_send()
        for cp in local:
            cp.wait()

    n_sem = 3 * n_ici + 3 * n_fwd
    out_shape = ([jax.ShapeDtypeStruct((N_CHIPS, D_MODEL, COLS), BF16)] * L
                 + [jax.ShapeDtypeStruct((N_CHIPS, GROUP, D_MODEL), BF16)] * L
                 + [jax.ShapeDtypeStruct((N_CHIPS,) + cw.shape, F32)])
    outs = pl.pallas_call(
        body, name="gather_weights",
        in_specs=[ANY, ANY, ANY], out_specs=[ANY] * (2 * L + 1), out_shape=out_shape,
        scratch_shapes=[pltpu.SemaphoreType.DMA((n_sem,)), pltpu.SemaphoreType.DMA((n_sem,)),
                        pltpu.SemaphoreType.DMA((2 * L + 4,)), pltpu.VMEM(wi16.shape, BF16), pltpu.VMEM(wo16.shape, BF16),
                        pltpu.VMEM(cw.shape, F32)],
        compiler_params=_vmem_params(has_side_effects=True),
    )(wi16, wo16, cw)
    return outs[0:L], outs[L:2 * L], outs[2 * L]


def _swap_halves(l_arr, gwi, gwo, ri, ro):
    hi_rows, ho_rows = D_MODEL // 2, GROUP // 2

    def body(l_ref, gwi_ref, gwo_ref, ri_in, ro_in, ri_ref, ro_ref, send_sems, recv_sems):
        del ri_in, ro_in
        x, y, c = _place()
        l = l_ref[0]
        sibling = (x, y, 1 - c)
        cps = [
            pltpu.make_async_remote_copy(src_ref=gwi_ref.at[l, :, pl.ds((1 - c) * hi_rows, hi_rows), :],
                                         dst_ref=ri_ref.at[l], send_sem=send_sems.at[0], recv_sem=recv_sems.at[0],
                                         device_id=sibling, device_id_type=MESH),
            pltpu.make_async_remote_copy(src_ref=gwo_ref.at[l, :, pl.ds((1 - c) * ho_rows, ho_rows), :],
                                         dst_ref=ro_ref.at[l], send_sem=send_sems.at[1], recv_sem=recv_sems.at[1],
                                         device_id=sibling, device_id_type=MESH),
        ]
        for cp in cps:
            cp.start()
        for cp in cps:
            cp.wait()

    return pl.pallas_call(
        body, name="swap_halves",
        in_specs=[pl.BlockSpec(memory_space=pltpu.SMEM), ANY, ANY, ANY, ANY], out_specs=[ANY, ANY],
        out_shape=[jax.ShapeDtypeStruct(ri.shape, ri.dtype), jax.ShapeDtypeStruct(ro.shape, ro.dtype)],
        input_output_aliases={3: 0, 4: 1},
        scratch_shapes=[pltpu.SemaphoreType.DMA((2,)), pltpu.SemaphoreType.DMA((2,))],
        compiler_params=pltpu.CompilerParams(has_side_effects=True),
    )(l_arr, gwi, gwo, ri, ro)


def _add_halves(cl_arr, g_i, r_i, p_i, g_o, r_o, p_o, *, nb):
    def body(cl_ref, gi_ref, ri_ref, pi_in, go_ref, ro_ref, po_in, oi_ref, oo_ref):
        del cl_ref, pi_in, po_in
        oi_ref[...] = (gi_ref[...] + ri_ref[...].astype(F32)).astype(oi_ref.dtype)
        oo_ref[...] = (go_ref[...] + ro_ref[...].astype(F32)).astype(oo_ref.dtype)

    def specs(r):
        tr, cols = r.shape[2] // nb, r.shape[3]
        mine = pl.BlockSpec((None, None, tr, cols), lambda k, i, cl: (cl[1], k, cl[0] * nb + i, 0))
        same = pl.BlockSpec((None, None, tr, cols), lambda k, i, cl: (cl[1], k, i, 0))
        return mine, same

    (gi_s, ri_s), (go_s, ro_s) = specs(r_i), specs(r_o)
    grid_spec = pltpu.PrefetchScalarGridSpec(
        num_scalar_prefetch=1, grid=(N_CHIPS, nb),
        in_specs=[gi_s, ri_s, ANY, go_s, ro_s, ANY], out_specs=[ri_s, ro_s])
    return pl.pallas_call(
        body, name="add_halves", grid_spec=grid_spec,
        out_shape=[jax.ShapeDtypeStruct(p_i.shape, p_i.dtype), jax.ShapeDtypeStruct(p_o.shape, p_o.dtype)],
        input_output_aliases={3: 0, 6: 1},
        compiler_params=_vmem_params(dimension_semantics=("arbitrary",) * 2),
    )(cl_arr, g_i, r_i, p_i, g_o, r_o, p_o)


def _swap_add(cl_arr, g_i, g16_i, p_i, g_o, g16_o, p_o):
    hi, ho = p_i.shape[2], p_o.shape[2]

    def body(cl_ref, gi_ref, gi16_ref, pi_in, go_ref, go16_ref, po_in, oi_ref, oo_ref, ri_v, ro_v, send_sems,
             recv_sems):
        del pi_in, po_in
        k = pl.program_id(0)
        x, y, c = _place()
        l = cl_ref[1]

        def copies(kk):
            pair = ((gi16_ref, hi, ri_v), (go16_ref, ho, ro_v))
            return [pltpu.make_async_remote_copy(
                src_ref=src.at[l, kk, pl.ds((1 - c) * n, n), :], dst_ref=dst.at[kk], send_sem=send_sems.at[2 * kk + j],
                recv_sem=recv_sems.at[2 * kk + j], device_id=(x, y, 1 - c), device_id_type=MESH)
                for j, (src, n, dst) in enumerate(pair)]

        @pl.when(k == 0)
        def _():
            for kk in range(N_CHIPS):
                for cp in copies(kk):
                    cp.start()

        for cp in copies(k):
            cp.wait_recv()
        oi_ref[...] = (gi_ref[...] + ri_v[k].astype(F32)).astype(oi_ref.dtype)
        oo_ref[...] = (go_ref[...] + ro_v[k].astype(F32)).astype(oo_ref.dtype)

        @pl.when(k == N_CHIPS - 1)
        def _():
            for kk in range(N_CHIPS):
                for cp in copies(kk):
                    cp.wait_send()

    def specs(p):
        rows, cols = p.shape[2], p.shape[3]
        mine = pl.BlockSpec((None, None, rows, cols), lambda k, cl: (cl[1], k, cl[0], 0))
        out = pl.BlockSpec((None, None, rows, cols), lambda k, cl: (cl[1], k, 0, 0))
        return mine, out

    (gi_s, pi_s), (go_s, po_s) = specs(p_i), specs(p_o)
    grid_spec = pltpu.PrefetchScalarGridSpec(
        num_scalar_prefetch=1, grid=(N_CHIPS,),
        in_specs=[gi_s, ANY, ANY, go_s, ANY, ANY], out_specs=[pi_s, po_s],
        scratch_shapes=[pltpu.VMEM((N_CHIPS, hi, p_i.shape[3]), BF16), pltpu.VMEM((N_CHIPS, ho, p_o.shape[3]), BF16),
                        pltpu.SemaphoreType.DMA((2 * N_CHIPS,)), pltpu.SemaphoreType.DMA((2 * N_CHIPS,))])
    return pl.pallas_call(
        body, name="swap_add", grid_spec=grid_spec,
        out_shape=[jax.ShapeDtypeStruct(p_i.shape, p_i.dtype), jax.ShapeDtypeStruct(p_o.shape, p_o.dtype)],
        input_output_aliases={3: 0, 6: 1},
        compiler_params=_vmem_params(dimension_semantics=("arbitrary",), has_side_effects=True),
    )(cl_arr, g_i, g16_i, p_i, g_o, g16_o, p_o)


def _exchange_last(l_arr, p_i, p_o, r_i, r_o):
    def body(l_ref, p_i_ref, p_o_ref, ri_in, ro_in, ri_ref, ro_ref, send_sems, recv_sems):
        del ri_in, ro_in
        always = l_ref[0] >= 0
        _exchange_comm(always, always, l_ref[0], p_i_ref, p_o_ref, None, ri_ref, ro_ref, None,
                       send_sems, recv_sems, None)

    return pl.pallas_call(
        body, name="exchange_last",
        in_specs=[pl.BlockSpec(memory_space=pltpu.SMEM)] + [ANY] * 4, out_specs=[ANY] * 2,
        out_shape=[jax.ShapeDtypeStruct(r_i.shape, r_i.dtype), jax.ShapeDtypeStruct(r_o.shape, r_o.dtype)],
        input_output_aliases={3: 0, 4: 1},
        scratch_shapes=[pltpu.SemaphoreType.DMA((N_EXCH_SEMS,)), pltpu.SemaphoreType.DMA((N_EXCH_SEMS,))],
        compiler_params=pltpu.CompilerParams(has_side_effects=True),
    )(l_arr, p_i, p_o, r_i, r_o)


def _sum_small(r_sms):
    L = len(r_sms)

    def body(*refs):
        o_ref = refs[L]
        for l in range(L):
            acc = refs[l][0]
            for d in range(1, N_DEV):
                acc = acc + refs[l][d]
            o_ref[l] = acc

    return pl.pallas_call(
        body, name="sum_small",
        out_shape=jax.ShapeDtypeStruct((L,) + r_sms[0].shape[1:], F32),
        compiler_params=_vmem_params(),
    )(*r_sms)


def _sum_chunks(kc_arr, p_i, q_i, p_o, q_o, *, nb):
    L = p_i.shape[0]

    def body(kc_ref, pi_ref, a0, a1, a2, po_ref, b0, b1, b2, oi_ref, oo_ref):
        del kc_ref
        f = lambda ref: ref[...].astype(F32)
        oi_ref[...] = ((f(pi_ref) + f(a0)) + f(a1)) + f(a2)
        oo_ref[...] = ((f(po_ref) + f(b0)) + f(b1)) + f(b2)

    def specs(p):
        tr, cols = p.shape[2] // nb, p.shape[3]
        chunk = pl.BlockSpec((None, None, tr, cols), lambda l, i, kc: (l, kc[0], i, 0))
        got = [pl.BlockSpec((None, None, tr, cols), lambda l, i, kc, _j=j: (_j, l, i, 0)) for j in range(3)]
        out = pl.BlockSpec((None, tr, cols), lambda l, i, kc: (l, kc[1] * nb + i, 0))
        return [chunk] + got, out

    (in_i, out_i), (in_o, out_o) = specs(p_i), specs(p_o)
    grid_spec = pltpu.PrefetchScalarGridSpec(num_scalar_prefetch=1, grid=(L, nb), in_specs=in_i + in_o,
                                             out_specs=[out_i, out_o])
    return pl.pallas_call(
        body, name="sum_chunks", grid_spec=grid_spec,
        out_shape=[jax.ShapeDtypeStruct((L, 2 * p.shape[2], p.shape[3]), F32) for p in (p_i, p_o)],
        compiler_params=_vmem_params(dimension_semantics=("arbitrary",) * 2),
    )(kc_arr, p_i, q_i, q_i, q_i, p_o, q_o, q_o, q_o)


def _share_result(gi, go):
    hi_rows, ho_rows = gi.shape[1] // 2, go.shape[1] // 2

    def body(gi_ref, go_ref, oi_ref, oo_ref, send_sems, recv_sems):
        del gi_ref, go_ref
        x, y, c = _place()
        sibling = (x, y, 1 - c)
        cps = []
        for j, (ref, n) in enumerate(((oi_ref, hi_rows), (oo_ref, ho_rows))):
            mine = ref.at[:, pl.ds(c * n, n), :]
            cps.append(pltpu.make_async_remote_copy(src_ref=mine, dst_ref=mine, send_sem=send_sems.at[j],
                                                    recv_sem=recv_sems.at[j], device_id=sibling, device_id_type=MESH))
        for cp in cps:
            cp.start()
        for j, (ref, n) in enumerate(((oi_ref, hi_rows), (oo_ref, ho_rows))):
            theirs = ref.at[:, pl.ds((1 - c) * n, n), :]
            pltpu.make_async_remote_copy(src_ref=theirs, dst_ref=theirs, send_sem=send_sems.at[j],
                                         recv_sem=recv_sems.at[j], device_id=sibling, device_id_type=MESH).wait_recv()
        for cp in cps:
            cp.wait_send()

    return pl.pallas_call(
        body, name="share_result",
        in_specs=[ANY, ANY], out_specs=[ANY, ANY],
        out_shape=[jax.ShapeDtypeStruct(gi.shape, F32), jax.ShapeDtypeStruct(go.shape, F32)],
        input_output_aliases={0: 0, 1: 1},
        scratch_shapes=[pltpu.SemaphoreType.DMA((2,)), pltpu.SemaphoreType.DMA((2,))],
        compiler_params=pltpu.CompilerParams(has_side_effects=True),
    )(gi, go)


WEIGHTS = ("ln_g", "ln_b", "w_in", "b_in", "conv_a_w", "conv_a_b", "norm_a_g", "norm_a_b", "conv_b_w", "pool_w",
           "pool_scale", "sgu_ln_g", "sgu_ln_b", "sgu_w", "sgu_bias", "w_out", "b_out")


def _pad_rows(a, rows):
    return jnp.pad(a, ((0, rows - a.shape[0]), (0, 0)))


def _indicator_consts():
    seg = jnp.where((jnp.arange(GROUP)[:, None] // HEAD) == (jnp.arange(GROUP)[None, :] // HEAD),
                    1.0 / HEAD, 0.0).astype(BF16)
    e4 = ((jnp.arange(GROUP)[:, None] // HEAD) == jnp.arange(128)[None, :]).astype(BF16)
    return seg, e4


def _layer_consts(p, conv_full):
    L = conv_full.shape[0]
    same_head = jnp.eye(4, dtype=F32)[:, None, :, None] > 0

    def rows_to(a, rows):
        return jnp.pad(a, ((0, 0), (0, rows - a.shape[1]), (0, 0)))

    s256 = jnp.stack([p[n] for n in ("conv_a_b", "norm_a_g", "norm_a_b", "pool_scale", "sgu_ln_g", "sgu_ln_b")], axis=1)
    pw = jnp.where(same_head, p["pool_w"][:, :, :, None, :], 0.0).reshape(L, GROUP, GROUP)
    return dict(
        caw=rows_to(conv_full[:, :KA], 32), cbw=rows_to(conv_full[:, KA:], 8), s256=rows_to(s256, 8),
        pw=pw.astype(BF16),
        wm=jnp.transpose(p["sgu_w"], (0, 2, 1, 3)).reshape(L, SGU_BLOCK, 4 * SGU_BLOCK),
        wmt=jnp.transpose(p["sgu_w"], (0, 1, 3, 2)).reshape(L, 4 * SGU_BLOCK, SGU_BLOCK),
        sb=jnp.repeat(jnp.transpose(p["sgu_bias"], (0, 2, 1)), HEAD, axis=2),
        v1024=rows_to(jnp.stack([p["b_out"], p["ln_g"], p["ln_b"]], axis=1), 8),
        bin=p["b_in"][:, None, :])


def _unpack_small(sm):
    L = sm.shape[0]
    owc = jnp.concatenate([sm[:, ROW_WC:ROW_WC + SGU_BLOCK], sm[:, ROW_WC + SGU_BLOCK:ROW_WC + 2 * SGU_BLOCK]], axis=2)
    return dict(
        conv_a_b=sm[:, 0], norm_a_g=sm[:, 1], norm_a_b=sm[:, 2], pool_scale=sm[:, 3], sgu_ln_g=sm[:, 4],
        sgu_ln_b=sm[:, 5], conv_b_w=sm[:, ROW_CBW:ROW_CBW + KB], conv_a_w=sm[:, ROW_CAW:ROW_CAW + KA],
        pool_w=jnp.transpose(sm[:, ROW_PW:ROW_PW + HEAD].reshape(L, HEAD, 4, HEAD), (0, 2, 1, 3)),
        ln_g=sm[:, ROW_LNG:ROW_LNG + 4].reshape(L, D_MODEL), ln_b=sm[:, ROW_LNB:ROW_LNB + 4].reshape(L, D_MODEL),
        b_out=sm[:, ROW_BOUT:ROW_BOUT + 4].reshape(L, D_MODEL),
        b_in=sm[:, ROW_BIN:ROW_BIN + N_SLICES].reshape(L, IN_WIDTH),
        sgu_w=jnp.transpose(owc.reshape(L, SGU_BLOCK, 4, SGU_BLOCK), (0, 2, 1, 3)),
        sgu_bias=sm[:, ROW_SB:ROW_SB + 4, 0:SGU_BLOCK])


def _step(p, m, v, x, target, *, tile_f, tile_b, tk_in, tk_out):
    L = p["ln_g"].shape[0]
    xi, yi, ci = _place()
    me_k = 2 * xi + yi
    hi_rows, ho_rows = D_MODEL // 2, GROUP // 2

    cw = jnp.concatenate([p["conv_a_w"], p["conv_b_w"]], axis=1).reshape(-1, 128)
    cw_rows = cw.shape[0]
    cw = _pad_rows(cw, -(-cw_rows // SUBLANES) * SUBLANES)
    wi16 = p["w_in"].astype(BF16)
    wo16 = p["w_out"].astype(BF16)
    wig0, wog0, cwg = _gather_weights(wi16[0:1], wo16[0:1], cw)
    cwg = cwg[:, :cw_rows].reshape(N_CHIPS, L, KA + KB, HEAD)
    conv_full = jnp.transpose(cwg, (1, 2, 0, 3)).reshape(L, KA + KB, GROUP)
    seg, e4 = _indicator_consts()
    k = _layer_consts(p, conv_full)
    layer = [jnp.full((1,), l, jnp.int32) for l in range(L)]

    hcur = x
    saved, wig, wog = [], [wig0[0]], [wog0[0]]
    for l in range(L):
        nxt = (wi16, wo16) if l + 1 < L else None
        outs = _fwd_layer(layer[l], hcur, wig[l], k["bin"], k["caw"], k["cbw"], k["s256"], seg, k["pw"], k["wm"], k["sb"],
                          wog[l], k["v1024"], tile=tile_f, nxt=nxt, target=None if nxt is not None else target)
        y, xb, h, aux, mixb, z = outs[0:6]
        if nxt is not None:
            wig.append(outs[6])
            wog.append(outs[7])
        saved.append((xb, h, aux, mixb, z))
        hcur = y

    dy = hcur
    loss_local = outs[6][0, 0]

    gwi = lax.empty((L, N_CHIPS, D_MODEL, COLS), F32)
    gwo = lax.empty((L, N_CHIPS, GROUP, D_MODEL), F32)
    gwi16 = lax.empty((L, N_CHIPS, D_MODEL, COLS), BF16)
    gwo16 = lax.empty((L, N_CHIPS, GROUP, D_MODEL), BF16)
    ri = lax.empty((L, N_CHIPS, hi_rows, COLS), BF16)
    ro = lax.empty((L, N_CHIPS, ho_rows, D_MODEL), BF16)
    p_i = lax.empty((L, N_CHIPS, hi_rows, COLS), BF16)
    p_o = lax.empty((L, N_CHIPS, ho_rows, D_MODEL), BF16)
    q_i = lax.empty((3, L, hi_rows, COLS), BF16)
    q_o = lax.empty((3, L, ho_rows, D_MODEL), BF16)
    r_sm = [None] * L
    pending = None
    for l in reversed(range(L)):
        xb, h, aux, mixb, z = saved[l]
        exch = None if pending is None else (p_i, p_o, pending, q_i, q_o)
        outs = _bwd_layer(layer[l], dy, z, h, aux, wig[l], k["caw"], k["cbw"], k["s256"], seg, k["pw"], k["wm"],
                          k["wmt"], k["sb"], wog[l], k["v1024"], e4, tile=tile_b, exch=exch)
        dy, dhb, dzb, osm = outs[0:4]
        if l == L - 1:
            osm = osm.at[ROW_LOSS, 0].set(loss_local)
        if exch is not None:
            q_i, q_o, r_sm[l + 1] = outs[4:7]
        larr = layer[l]
        if l > 0:
            gwi, gwi16 = _dw_in(larr, xb, dhb, gwi, gwi16, tk=tk_in)
        else:
            gwi, gwi16, r_sm[0] = _dw_in(larr, xb, dhb, gwi, gwi16, tk=tk_out, small=osm)
        gwo, gwo16 = _dw_out(larr, mixb, dzb, gwo, gwo16, tk=tk_out)
        cl_arr = jnp.stack([ci, jnp.int32(l)]).astype(jnp.int32)
        p_i, p_o = _swap_add(cl_arr, gwi, gwi16, p_i, gwo, gwo16, p_o)
        pending = osm
    grad_x = dy
    q_i, q_o = _exchange_last(layer[0], p_i, p_o, q_i, q_o)

    summed = _sum_small(r_sm)
    loss = summed[L - 1, ROW_LOSS, 0]
    grads = _unpack_small(summed)
    for n in ("conv_a_w", "conv_b_w"):
        grads[n] = lax.dynamic_slice_in_dim(grads[n], me_k * HEAD, HEAD, axis=2)

    kc_arr = jnp.stack([me_k, ci]).astype(jnp.int32)
    g_i, g_o = _sum_chunks(kc_arr, p_i, q_i, p_o, q_o, nb=2)
    g_i, g_o = _share_result(g_i, g_o)
    grads["w_in"] = g_i
    grads["w_out"] = g_o

    delta, new_m, new_v = {}, {}, {}
    for n, tr in (("w_in", 512), ("w_out", 256)):
        shp = p[n].shape
        args = [a.reshape(shp[0] * shp[1], shp[2]) for a in (p[n], grads[n], m[n], v[n])]
        outs = _adamw(*args, rows_per_step=tr, name="adamw_" + n, copy_g=True)
        delta[n], new_m[n], new_v[n], grads[n] = (a.reshape(shp) for a in outs)
    small = [n for n in WEIGHTS if n not in ("w_in", "w_out")]
    flat = [[a[n].reshape(-1, a[n].shape[-1]) for n in small] for a in (p, grads, m, v)]
    outs = _adamw_small(*flat)
    for j, n in enumerate(small):
        delta[n], new_m[n], new_v[n] = (o[j].reshape(p[n].shape) for o in outs)

    return (loss, grad_x[None], *[grads[n] for n in WEIGHTS], *[delta[n] for n in WEIGHTS],
            *[new_m[n] for n in WEIGHTS], *[new_v[n] for n in WEIGHTS])


def kernel(x, ln_g, ln_b, w_in, b_in, conv_a_w, conv_a_b, norm_a_g, norm_a_b, conv_b_w, pool_w, pool_scale, sgu_ln_g, sgu_ln_b, sgu_w, sgu_bias, w_out, b_out, loss_target, m_ln_g, m_ln_b, m_w_in, m_b_in, m_conv_a_w, m_conv_a_b, m_norm_a_g, m_norm_a_b, m_conv_b_w, m_pool_w, m_pool_scale, m_sgu_ln_g, m_sgu_ln_b, m_sgu_w, m_sgu_bias, m_w_out, m_b_out, v_ln_g, v_ln_b, v_w_in, v_b_in, v_conv_a_w, v_conv_a_b, v_norm_a_g, v_norm_a_b, v_conv_b_w, v_pool_w, v_pool_scale, v_sgu_ln_g, v_sgu_ln_b, v_sgu_w, v_sgu_bias, v_w_out, v_b_out):
    p = dict(ln_g=ln_g, ln_b=ln_b, w_in=w_in, b_in=b_in, conv_a_w=conv_a_w, conv_a_b=conv_a_b, norm_a_g=norm_a_g,
             norm_a_b=norm_a_b, conv_b_w=conv_b_w, pool_w=pool_w, pool_scale=pool_scale, sgu_ln_g=sgu_ln_g,
             sgu_ln_b=sgu_ln_b, sgu_w=sgu_w, sgu_bias=sgu_bias, w_out=w_out, b_out=b_out)
    m = dict(ln_g=m_ln_g, ln_b=m_ln_b, w_in=m_w_in, b_in=m_b_in, conv_a_w=m_conv_a_w, conv_a_b=m_conv_a_b,
             norm_a_g=m_norm_a_g, norm_a_b=m_norm_a_b, conv_b_w=m_conv_b_w, pool_w=m_pool_w, pool_scale=m_pool_scale,
             sgu_ln_g=m_sgu_ln_g, sgu_ln_b=m_sgu_ln_b, sgu_w=m_sgu_w, sgu_bias=m_sgu_bias, w_out=m_w_out, b_out=m_b_out)
    v = dict(ln_g=v_ln_g, ln_b=v_ln_b, w_in=v_w_in, b_in=v_b_in, conv_a_w=v_conv_a_w, conv_a_b=v_conv_a_b,
             norm_a_g=v_norm_a_g, norm_a_b=v_norm_a_b, conv_b_w=v_conv_b_w, pool_w=v_pool_w, pool_scale=v_pool_scale,
             sgu_ln_g=v_sgu_ln_g, sgu_ln_b=v_sgu_ln_b, sgu_w=v_sgu_w, sgu_bias=v_sgu_bias, w_out=v_w_out, b_out=v_b_out)
    return _step(p, m, v, x[0], loss_target[0], tile_f=256, tile_b=256, tk_in=4096, tk_out=2048)
```

```python
import jax
import jax.numpy as jnp
from jax import lax
from jax.experimental import pallas as pl
from jax.experimental.pallas import tpu as pltpu

F32 = jnp.float32
BF16 = jnp.bfloat16
MESH = pl.DeviceIdType.MESH

D_MODEL = 1024
GROUP = 256
HEAD = 64
N_SLICES = 12
IN_WIDTH = N_SLICES * GROUP
N_CHIPS = 4
COLS = IN_WIDTH // N_CHIPS
KA = 31
KB = 3
SUBLANES = 8
HALO_A, HALO_B, HALO_C = 32, 8, 16
N_GATHER_SEMS = 12
N_EXCH_SEMS = 13
SGU_BLOCK = 128
CHUNK = 64
LN_EPS = 1e-5
ROWS = 64
V7X_VMEM_BYTES = 64 * 1024 * 1024
VMEM_LIMIT = V7X_VMEM_BYTES - 8 * 1024 * 1024

ADAM_LR, ADAM_B1, ADAM_B2, ADAM_EPS, ADAM_WD, ADAM_STEP = 0.001, 0.9, 0.999, 1e-08, 0.01, 10


ANY = pl.BlockSpec(memory_space=pl.ANY)


def _vmem_params(**kw):
    return pltpu.CompilerParams(vmem_limit_bytes=VMEM_LIMIT, **kw)


def _whole(a):
    return pl.BlockSpec(a.shape, lambda i, l, _n=a.ndim: (0,) * _n)


def _of_layer(a):
    return pl.BlockSpec((None,) + a.shape[1:], lambda i, l, _n=a.ndim: (l[0],) + (0,) * (_n - 1))


def _place():
    return lax.axis_index("x"), lax.axis_index("y"), lax.axis_index("c")


def _other_chips(x, y):
    return [(1 - x, y, 2 * (1 - x) + y), (x, 1 - y, 2 * x + (1 - y)), (1 - x, 1 - y, 2 * (1 - x) + (1 - y))]


def _sig(v):
    return 0.5 * jnp.tanh(0.5 * v) + 0.5


def _dot(a, b):
    return jnp.dot(a, b, preferred_element_type=F32)


def _dot_nt(a, b):
    return lax.dot_general(a, b, (((1,), (1,)), ((), ())), preferred_element_type=F32)


def _dot_tn(a, b):
    return lax.dot_general(a, b, (((0,), (0,)), ((), ())), preferred_element_type=F32)


def _segdot(v, m):
    hi = v.astype(BF16)
    lo = (v - hi.astype(F32)).astype(BF16)
    return _dot(hi, m) + _dot(lo, m)


def _colsum(v):
    return jnp.sum(v, axis=0, keepdims=True)


def _rowmean(v):
    return jnp.mean(v, axis=-1, keepdims=True)


def _lane_group(n):
    return lax.broadcasted_iota(jnp.int32, (1, n), 1) // HEAD


def _pool_cnt(tile, t_rows):
    pos = tile * t_rows + lax.broadcasted_iota(jnp.int32, (t_rows, GROUP), 0) + 1
    grp = lax.broadcasted_iota(jnp.int32, (t_rows, GROUP), 1) // HEAD
    win = jnp.where(grp == 0, 2, jnp.where(grp == 1, 4, jnp.where(grp == 2, 8, 16)))
    return jnp.minimum(pos, win).astype(F32)


def _sgu_masks(wm_ref, wmt_ref, wm_s, wmt_s):
    r = lax.broadcasted_iota(jnp.int32, (SGU_BLOCK, 4 * SGU_BLOCK), 0) // CHUNK
    c = (lax.broadcasted_iota(jnp.int32, (SGU_BLOCK, 4 * SGU_BLOCK), 1) % SGU_BLOCK) // CHUNK
    wm_s[...] = jnp.where(c <= r, wm_ref[...], 0.0).astype(BF16)
    if wmt_ref is not None:
        rt = (lax.broadcasted_iota(jnp.int32, (4 * SGU_BLOCK, SGU_BLOCK), 0) % SGU_BLOCK) // CHUNK
        ct = lax.broadcasted_iota(jnp.int32, (4 * SGU_BLOCK, SGU_BLOCK), 1) // CHUNK
        wmt_s[...] = jnp.where(rt <= ct, wmt_ref[...], 0.0).astype(BF16)


def _vstack(v_blk):
    grp = _lane_group(GROUP)
    return jnp.concatenate([jnp.where(grp == h, v_blk, 0.0) for h in range(4)], axis=0).astype(BF16)


def _gather_next(step, nt, nwi, nwo, gwi, gwo, send_sems, recv_sems, loc_sems, vwi, vwo):
    x, y, c = _place()
    me_k = 2 * x + y
    sibling = (x, y, 1 - c)
    chips = _other_chips(x, y)
    hi, ho = D_MODEL // 2, GROUP // 2
    fwd_sems = N_GATHER_SEMS // 2

    def rc(src, dst, sem, to):
        return pltpu.make_async_remote_copy(src_ref=src, dst_ref=dst, send_sem=send_sems.at[sem],
                                            recv_sem=recv_sems.at[sem], device_id=to, device_id_type=MESH)

    def blk(ref, k, n, cc):
        return ref.at[k, pl.ds(cc * n, n), :]

    def ici(r):
        px, py, _ = chips[r]
        to = (px, py, c)
        return [rc(nwi.at[pl.ds(c * hi, hi), :], blk(gwi, me_k, hi, c), 2 * r, to),
                rc(nwo.at[pl.ds(c * ho, ho), :], blk(gwo, me_k, ho, c), 2 * r + 1, to)]

    def landed(r, cc, base):
        pk = chips[r][2]
        return [rc(blk(gwi, pk, hi, cc), blk(gwi, pk, hi, cc), base + 2 * r, sibling),
                rc(blk(gwo, pk, ho, cc), blk(gwo, pk, ho, cc), base + 2 * r + 1, sibling)]

    def stage_in():
        return [pltpu.make_async_copy(nwi, vwi, loc_sems.at[0]), pltpu.make_async_copy(nwo, vwo, loc_sems.at[1])]

    def local():
        return [pltpu.make_async_copy(vwi, gwi.at[me_k], loc_sems.at[2]),
                pltpu.make_async_copy(vwo, gwo.at[me_k], loc_sems.at[3])]

    @pl.when(step == 0)
    def _():
        for cp in stage_in():
            cp.start()
        for r in range(3):
            for cp in ici(r):
                cp.start()

    @pl.when(step == 1)
    def _():
        for cp in stage_in():
            cp.wait()
        for cp in local():
            cp.start()

    @pl.when(step == (3 * nt) // 4)
    def _():
        for r in range(3):
            for got, fwd in zip(landed(r, c, 0), landed(r, c, fwd_sems)):
                got.wait_recv()
                fwd.start()

    @pl.when(step == nt - 1)
    def _():
        for r in range(3):
            for got in landed(r, 1 - c, fwd_sems):
                got.wait_recv()
        for r in range(3):
            for cp in ici(r) + landed(r, c, fwd_sems):
                cp.wait_send()
        for cp in local():
            cp.wait()


def _fwd_layer(larr, x, wi, bin_, caw, cbw, s256, seg, pw, wm, sb, wo, v1024, *, tile, nxt=None, target=None):
    assert nxt is None or target is None
    S = x.shape[0]
    T = tile
    nt = S // T
    alpha = float((2.0 * 4) ** 0.25)
    n_in = 13 + (2 if nxt is not None else 0) + (1 if target is not None else 0)
    n_out = 6 + (2 if nxt is not None else 0) + (1 if target is not None else 0)

    def body(*refs):
        l_ref = refs[0]
        (x_ref, wi_ref, bin_ref, caw_ref, cbw_ref, s256_ref, seg_ref, pw_ref, wm_ref, sb_ref, wo_ref,
         v1024_ref) = refs[1:13]
        y_ref, xb_ref, h_ref, aux_ref, mix_ref, z_ref = refs[n_in:n_in + 6]
        abuf, bbuf, cbuf, wm_s, shf = refs[n_in + n_out:n_in + n_out + 5]
        i = pl.program_id(0)
        if nxt is not None:
            _gather_next(i, nt, refs[13].at[l_ref[0] + 1], refs[14].at[l_ref[0] + 1], refs[n_in + 6], refs[n_in + 7],
                         *refs[n_in + n_out + 5:])

        @pl.when(i == 0)
        def _():
            abuf[0:HALO_A, :] = jnp.zeros((HALO_A, GROUP), F32)
            bbuf[0:HALO_B, :] = jnp.zeros((HALO_B, GROUP), F32)
            cbuf[0:HALO_C, :] = jnp.zeros((HALO_C, GROUP), F32)
            _sgu_masks(wm_ref, None, wm_s, None)

        x = x_ref[...]
        xb = x.astype(BF16)
        xb_ref[...] = xb
        for k in range(N_CHIPS):
            h_ref[:, COLS * k:COLS * (k + 1)] = _dot(xb, wi_ref[k]) + bin_ref[:, COLS * k:COLS * (k + 1)]

        def hs(j):
            return h_ref[:, GROUP * j:GROUP * (j + 1)]

        abuf[HALO_A:HALO_A + T, :] = hs(0) * _sig(hs(1))
        span = T + HALO_A - SUBLANES
        for p in range(1, SUBLANES):
            shf[p - 1, :, :] = abuf[p:p + span, :]
        for r0 in range(0, T, ROWS):
            acc = None
            for k in range(KA):
                off = HALO_A - (KA - 1) + k
                p, q8 = off % SUBLANES, off - off % SUBLANES
                win = abuf[r0 + q8:r0 + q8 + ROWS, :] if p == 0 else shf[p - 1, r0 + q8:r0 + q8 + ROWS, :]
                term = caw_ref[k:k + 1, :] * win
                acc = term if acc is None else acc + term
            aux_ref[r0:r0 + ROWS, 0:GROUP] = acc + s256_ref[0:1, :]
        abuf[0:HALO_A, :] = abuf[T:T + HALO_A, :]
        a1 = aux_ref[:, 0:GROUP]
        segm = seg_ref[...]
        cen = a1 - _segdot(a1, segm)
        var = _segdot(cen * cen, segm)
        a2 = cen * lax.rsqrt(var + LN_EPS) * s256_ref[1:2, :] + s256_ref[2:3, :]
        az = hs(2)
        mix_ref[:, 0:GROUP] = (a2 * _sig(a2) * (az * _sig(az))).astype(BF16)

        bbuf[HALO_B:HALO_B + T, :] = hs(4) * hs(5)
        for r0 in range(0, T, ROWS):
            acc = None
            for k in range(KB):
                off = HALO_B - (KB - 1) + k + r0
                term = cbw_ref[k:k + 1, :] * bbuf[off:off + ROWS, :]
                acc = term if acc is None else acc + term
            aux_ref[r0:r0 + ROWS, GROUP:2 * GROUP] = acc
        bbuf[0:HALO_B, :] = bbuf[T:T + HALO_B, :]
        bz = hs(6)
        mix_ref[:, GROUP:2 * GROUP] = (hs(3) * aux_ref[:, GROUP:2 * GROUP] * (bz * _sig(bz))).astype(BF16)

        ch = hs(7)
        cbuf[HALO_C:HALO_C + T, :] = ch
        hi_lane = (lax.broadcasted_iota(jnp.int32, (1, 128), 1) // HEAD) == 1
        for r0 in range(0, T, ROWS):
            def win(col, j0, j1):
                s = None
                for j in range(j0, j1):
                    off = HALO_C - j + r0
                    term = cbuf[off:off + ROWS, 128 * col:128 * (col + 1)]
                    s = term if s is None else s + term
                return s
            w0 = win(0, 0, 2) + jnp.where(hi_lane, win(0, 2, 4), 0.0)
            w1 = win(1, 0, 8) + jnp.where(hi_lane, win(1, 8, 16), 0.0)
            aux_ref[r0:r0 + ROWS, 2 * GROUP:2 * GROUP + 128] = w0
            aux_ref[r0:r0 + ROWS, 2 * GROUP + 128:3 * GROUP] = w1
        cbuf[0:HALO_C, :] = cbuf[T:T + HALO_C, :]
        pooled = aux_ref[:, 2 * GROUP:3 * GROUP] / _pool_cnt(i, T) - ch
        aux_ref[:, 2 * GROUP:3 * GROUP] = pooled
        q = _dot(pooled.astype(BF16), pw_ref[...])
        cz = hs(8)
        mix_ref[:, 2 * GROUP:3 * GROUP] = (q * s256_ref[3:4, :] * (cz * _sig(cz))).astype(BF16)

        dv = hs(10)
        cen = dv - _rowmean(dv)
        var = _rowmean(cen * cen)
        v = cen * lax.rsqrt(var + LN_EPS) * s256_ref[4:5, :] + s256_ref[5:6, :]
        sps = []
        for n in range(T // SGU_BLOCK):
            vb = v[n * SGU_BLOCK:(n + 1) * SGU_BLOCK, :]
            sps.append(_dot(wm_s[...], _vstack(vb)) + sb_ref[...])
        sp = jnp.concatenate(sps, axis=0)
        dz = hs(11)
        mix_ref[:, 3 * GROUP:4 * GROUP] = (hs(9) * sp * (dz * _sig(dz))).astype(BF16)

        out = v1024_ref[0:1, :]
        for k in range(N_CHIPS):
            out = out + _dot(mix_ref[:, GROUP * k:GROUP * (k + 1)], wo_ref[k])
        z = alpha * x + out
        z_ref[...] = z
        cen = z - _rowmean(z)
        var = _rowmean(cen * cen)
        y = cen * lax.rsqrt(var + LN_EPS) * v1024_ref[1:2, :] + v1024_ref[2:3, :]
        if target is None:
            y_ref[...] = y
        else:
            t_ref, loss_ref = refs[13], refs[n_in + 6]

            @pl.when(i == 0)
            def _():
                loss_ref[...] = jnp.zeros_like(loss_ref)
            err = y - t_ref[...]
            y_ref[...] = err * (1.0 / D_MODEL)
            loss_ref[...] += jnp.sum(_colsum(err * err), axis=1, keepdims=True) * (0.5 / D_MODEL)

    def rows(width):
        return pl.BlockSpec((T, width), lambda i, l: (i, 0))

    consts = (wi, bin_, caw, cbw, s256, seg, pw, wm, sb, wo, v1024)
    in_specs = [rows(D_MODEL)] + [_whole(a) if a is wi or a is seg or a is wo else _of_layer(a) for a in consts]
    out_specs = [rows(D_MODEL), rows(D_MODEL), rows(IN_WIDTH), rows(3 * GROUP), rows(D_MODEL), rows(D_MODEL)]
    out_shape = [jax.ShapeDtypeStruct((S, D_MODEL), F32), jax.ShapeDtypeStruct((S, D_MODEL), BF16),
                 jax.ShapeDtypeStruct((S, IN_WIDTH), F32), jax.ShapeDtypeStruct((S, 3 * GROUP), F32),
                 jax.ShapeDtypeStruct((S, D_MODEL), BF16), jax.ShapeDtypeStruct((S, D_MODEL), F32)]
    scratch = [pltpu.VMEM((T + HALO_A, GROUP), F32), pltpu.VMEM((T + HALO_B, GROUP), F32),
               pltpu.VMEM((T + HALO_C, GROUP), F32), pltpu.VMEM((SGU_BLOCK, 4 * SGU_BLOCK), BF16),
               pltpu.VMEM((SUBLANES - 1, T + HALO_A - SUBLANES, GROUP), F32)]
    extra = ()
    if nxt is not None:
        extra = tuple(nxt)
        in_specs += [ANY, ANY]
        out_specs += [ANY, ANY]
        out_shape += [jax.ShapeDtypeStruct((N_CHIPS, D_MODEL, COLS), BF16),
                      jax.ShapeDtypeStruct((N_CHIPS, GROUP, D_MODEL), BF16)]
        scratch += [pltpu.SemaphoreType.DMA((N_GATHER_SEMS,)), pltpu.SemaphoreType.DMA((N_GATHER_SEMS,)),
                    pltpu.SemaphoreType.DMA((4,)), pltpu.VMEM((D_MODEL, COLS), BF16), pltpu.VMEM((GROUP, D_MODEL), BF16)]
    if target is not None:
        extra = (target,)
        in_specs += [rows(D_MODEL)]
        out_specs += [pl.BlockSpec((8, 128), lambda i, l: (0, 0))]
        out_shape += [jax.ShapeDtypeStruct((8, 128), F32)]
    grid_spec = pltpu.PrefetchScalarGridSpec(num_scalar_prefetch=1, grid=(nt,), in_specs=in_specs,
                                             out_specs=out_specs, scratch_shapes=scratch)
    return pl.pallas_call(
        body, name=("fwd_layer_loss" if target is not None else "fwd_layer") if nxt is None else "fwd_layer_gather",
        grid_spec=grid_spec, out_shape=out_shape,
        compiler_params=_vmem_params(dimension_semantics=("arbitrary",), has_side_effects=nxt is not None),
    )(larr, x, *consts, *extra)


ROW_CBW = 8
ROW_CAW = 16
ROW_LOSS = 7
ROW_PW = 48
ROW_LNG = 112
ROW_LNB = 116
ROW_BOUT = 120
ROW_BIN = 124
ROW_WC = 136
ROW_SB = 392
SM_ROWS = 400
N_DEV = 8


def _exchange_comm(start, finish, l, p_i, p_o, sm, r_i, r_o, r_sm, send_sems, recv_sems, loc_sem=None, vsm=None):
    x, y, c = _place()
    me = 4 * x + 2 * y + c
    chips = _other_chips(x, y)

    def rc(src, dst, sem, to):
        return pltpu.make_async_remote_copy(src_ref=src, dst_ref=dst, send_sem=send_sems.at[sem],
                                            recv_sem=recv_sems.at[sem], device_id=to, device_id_type=MESH)

    def big(r):
        px, py, pk = chips[r]
        to = (px, py, c)
        return [rc(p_i.at[l, pk], r_i.at[r, l], 2 * r, to), rc(p_o.at[l, pk], r_o.at[r, l], 2 * r + 1, to)]

    def peer(rel):
        px = 1 - x if rel & 4 else x
        py = 1 - y if rel & 2 else y
        pc = 1 - c if rel & 1 else c
        return (px, py, pc), 4 * px + 2 * py + pc

    def small_out(rel):
        to, _ = peer(rel)
        return rc(sm, r_sm.at[me], N_EXCH_SEMS - N_DEV + rel, to)

    def small_in(rel):
        to, idx = peer(rel)
        return rc(sm, r_sm.at[idx], N_EXCH_SEMS - N_DEV + rel, to)

    def stage_in():
        return pltpu.make_async_copy(sm, vsm, loc_sem.at[0])

    def local():
        return pltpu.make_async_copy(vsm, r_sm.at[me], loc_sem.at[1])

    with_big, with_small = p_i is not None, sm is not None

    @pl.when(start)
    def _():
        if with_small:
            stage_in().start()
        if with_big:
            for r in range(3):
                for cp in big(r):
                    cp.start()
        if with_small:
            for rel in range(1, N_DEV):
                small_out(rel).start()

    @pl.when(finish)
    def _():
        if with_big:
            for r in range(3):
                for cp in big(r):
                    cp.wait()
        if with_small:
            stage_in().wait()
            local().start()
            for rel in range(1, N_DEV):
                small_in(rel).wait_recv()
                small_out(rel).wait_send()
            local().wait()


RC = 32
RC_WIDE = 16
ACC_ROWS = 136


def _rsum8(v):
    r = v[0:8]
    for j in range(1, v.shape[0] // 8):
        r = r + v[8 * j:8 * j + 8]
    return r


def _bwd_layer(larr, dy, z, h, aux, wi, caw, cbw, s256, seg, pw, wm, wmt, sb, wo, v1024, e4, *, tile, exch=None):
    S = dy.shape[0]
    T = tile
    nt = S // T
    nblk = T // SGU_BLOCK
    alpha = float((2.0 * 4) ** 0.25)
    n_in = 17 + (5 if exch is not None else 0)
    n_out = 4 + (3 if exch is not None else 0)
    slab = pltpu.VMEM((T, GROUP), F32)
    scratch = dict(
        dbuf=pltpu.VMEM((T + HALO_A, GROUP), F32), ebuf=pltpu.VMEM((T + HALO_B, GROUP), F32),
        fbuf=pltpu.VMEM((T + HALO_C, GROUP), F32), sh=pltpu.VMEM((SUBLANES - 1, T + HALO_A - SUBLANES, GROUP), F32),
        wm_s=pltpu.VMEM((SGU_BLOCK, 4 * SGU_BLOCK), BF16), wmt_s=pltpu.VMEM((4 * SGU_BLOCK, SGU_BLOCK), BF16),
        dsp_acc=pltpu.VMEM((SGU_BLOCK, GROUP), F32), pw_acc=pltpu.VMEM((GROUP, GROUP), F32),
        acc_s=pltpu.VMEM((8 * ACC_ROWS, GROUP), F32), acc_w=pltpu.VMEM((24, D_MODEL), F32),
        dmix_s=pltpu.VMEM((T, D_MODEL), F32), vst_s=pltpu.VMEM((nblk, 4 * SGU_BLOCK, GROUP), BF16),
        dq_s=pltpu.VMEM((T, GROUP), BF16), dxt_s=pltpu.VMEM((D_MODEL, T), F32),
        mean_s=slab, t1_s=slab, t2_s=slab, q_s=slab, xv_s=slab, rv_s=slab, v_s=slab, sp_s=slab, a0_s=slab, sg_s=slab,
        xh_s=slab, ra_s=slab, ub_s=slab, dsp_s=slab, m1_s=slab, m2_s=slab, dpool_s=slab, dvd_s=slab, u_s=slab,
        du_s=slab, cw_s=slab)
    names = list(scratch)

    def body(*refs):
        (dy_ref, z_ref, h_ref, aux_ref, wi_ref, caw_ref, cbw_ref, s256_ref, seg_ref, pw_ref, wm_ref, wmt_ref,
         sb_ref, wo_ref, v1024_ref, e4_ref) = refs[1:17]
        dx_ref, dhb_ref, dzb_ref, osm_ref = refs[n_in:n_in + 4]
        k0 = n_in + n_out
        sc = dict(zip(names, refs[k0:k0 + len(names)]))
        dbuf, ebuf, fbuf, sh = sc["dbuf"], sc["ebuf"], sc["fbuf"], sc["sh"]
        wm_s, wmt_s, dsp_acc, pw_acc, acc_s, acc_w = (sc[n] for n in ("wm_s", "wmt_s", "dsp_acc", "pw_acc", "acc_s",
                                                                        "acc_w"))
        dmix_s, vst_s, dq_s = sc["dmix_s"], sc["vst_s"], sc["dq_s"]
        i = pl.program_id(0)
        tile_idx = nt - 1 - i
        if exch is not None:
            p_i, p_o, sm = refs[17:20]
            r_i, r_o, r_sm = refs[n_in + 4:n_in + 7]
            _exchange_comm(i == 0, i == nt - 1, refs[0][0] + 1, p_i, p_o, sm, r_i, r_o, r_sm, *refs[k0 + len(names):])

        @pl.when(i == 0)
        def _():
            dbuf[T:T + HALO_A, :] = jnp.zeros((HALO_A, GROUP), F32)
            ebuf[T:T + HALO_B, :] = jnp.zeros((HALO_B, GROUP), F32)
            fbuf[T:T + HALO_C, :] = jnp.zeros((HALO_C, GROUP), F32)
            _sgu_masks(wm_ref, wmt_ref, wm_s, wmt_s)
            osm_ref[...] = jnp.zeros_like(osm_ref)
            dsp_acc[...] = jnp.zeros_like(dsp_acc)
            pw_acc[...] = jnp.zeros_like(pw_acc)
            acc_s[...] = jnp.zeros_like(acc_s)
            acc_w[...] = jnp.zeros_like(acc_w)

        def chunks(rc, fn):
            for c in range(T // rc):
                fn(pl.ds(c * rc, rc))

        def hs(j, rows):
            return h_ref[rows, GROUP * j:GROUP * (j + 1)]

        def acc_add(row, val):
            acc_s[8 * row:8 * row + 8, :] += _rsum8(val)

        def put_dh(j, rows, val):
            acc_add(ROW_BIN + j, val)
            dhb_ref[rows, GROUP * j:GROUP * (j + 1)] = val.astype(BF16)

        def dsilu(v, s):
            return s * (1.0 + v * (1.0 - s))

        def vec(r):
            return s256_ref[r:r + 1, :]

        def ln_bwd(rows):
            dyc = dy_ref[rows, :]
            zc = z_ref[rows, :]
            cen = zc - _rowmean(zc)
            rstd = lax.rsqrt(_rowmean(cen * cen) + LN_EPS)
            xhat = cen * rstd
            acc_w[0:8, :] += _rsum8(dyc * xhat)
            acc_w[8:16, :] += _rsum8(dyc)
            gdy = dyc * v1024_ref[1:2, :]
            dz = rstd * (gdy - _rowmean(gdy) - xhat * _rowmean(gdy * xhat))
            acc_w[16:24, :] += _rsum8(dz)
            dzb_ref[rows, :] = dz.astype(BF16)
            dx_ref[rows, :] = alpha * dz
        chunks(RC_WIDE, ln_bwd)

        segm = seg_ref[...]
        dzb = dzb_ref[...]
        for k in range(N_CHIPS):
            dmix_s[:, GROUP * k:GROUP * (k + 1)] = _dot_nt(dzb, wo_ref[k])
        sc["mean_s"][...] = _segdot(aux_ref[:, 0:GROUP], segm)
        pooled_b = aux_ref[:, 2 * GROUP:3 * GROUP].astype(BF16)
        sc["q_s"][...] = _dot(pooled_b, pw_ref[...])

        def centre(rows):
            cen = aux_ref[rows, 0:GROUP] - sc["mean_s"][rows, :]
            sc["t1_s"][rows, :] = cen * cen
            dv_in = hs(10, rows)
            cen_v = dv_in - _rowmean(dv_in)
            rstd_v = lax.rsqrt(_rowmean(cen_v * cen_v) + LN_EPS)
            xv = cen_v * rstd_v
            sc["xv_s"][rows, :] = xv
            sc["rv_s"][rows, :] = jnp.broadcast_to(rstd_v, xv.shape)
            sc["v_s"][rows, :] = xv * vec(4) + vec(5)
        chunks(RC, centre)

        sc["t2_s"][...] = _segdot(sc["t1_s"][...], segm)
        for n in range(nblk):
            blk = slice(n * SGU_BLOCK, (n + 1) * SGU_BLOCK)
            vst_s[n] = _vstack(sc["v_s"][blk, :])
            sc["sp_s"][blk, :] = _dot(wm_s[...], vst_s[n]) + sb_ref[...]

        def mixers(rows):
            a_val, a_glu, a_z = hs(0, rows), hs(1, rows), hs(2, rows)
            sg = _sig(a_glu)
            sc["a0_s"][rows, :] = a_val * sg
            sc["sg_s"][rows, :] = sg
            rstd_a = lax.rsqrt(sc["t2_s"][rows, :] + LN_EPS)
            xh = (aux_ref[rows, 0:GROUP] - sc["mean_s"][rows, :]) * rstd_a
            a2 = xh * vec(1) + vec(2)
            s2 = _sig(a2)
            sz = _sig(a_z)
            dya = dmix_s[rows, 0:GROUP]
            put_dh(2, rows, dya * (a2 * s2) * dsilu(a_z, sz))
            d_a2 = dya * (a_z * sz) * dsilu(a2, s2)
            acc_add(1, d_a2 * xh)
            acc_add(2, d_a2)
            gd = d_a2 * vec(1)
            sc["t1_s"][rows, :] = gd
            sc["t2_s"][rows, :] = gd * xh
            sc["xh_s"][rows, :] = xh
            sc["ra_s"][rows, :] = rstd_a
            b_b, b_c, b_h, b_z = hs(3, rows), hs(4, rows), hs(5, rows), hs(6, rows)
            cb = aux_ref[rows, GROUP:2 * GROUP]
            sz = _sig(b_z)
            dyb = dmix_s[rows, GROUP:2 * GROUP]
            put_dh(3, rows, dyb * cb * (b_z * sz))
            put_dh(6, rows, dyb * b_b * cb * dsilu(b_z, sz))
            ebuf[rows, :] = dyb * b_b * (b_z * sz)
            sc["ub_s"][rows, :] = b_c * b_h
            c_z = hs(8, rows)
            q = sc["q_s"][rows, :]
            sz = _sig(c_z)
            dyc = dmix_s[rows, 2 * GROUP:3 * GROUP]
            acc_add(3, dyc * q * (c_z * sz))
            put_dh(8, rows, dyc * q * vec(3) * dsilu(c_z, sz))
            dq_s[rows, :] = (dyc * vec(3) * (c_z * sz)).astype(BF16)
            d_u, d_z = hs(9, rows), hs(11, rows)
            sp = sc["sp_s"][rows, :]
            sz = _sig(d_z)
            dyd = dmix_s[rows, 3 * GROUP:4 * GROUP]
            put_dh(9, rows, dyd * sp * (d_z * sz))
            put_dh(11, rows, dyd * d_u * sp * dsilu(d_z, sz))
            sc["dsp_s"][rows, :] = dyd * d_u * (d_z * sz)
        chunks(RC, mixers)

        sc["m1_s"][...] = _segdot(sc["t1_s"][...], segm)
        sc["m2_s"][...] = _segdot(sc["t2_s"][...], segm)
        d_q = dq_s[...]
        pw_acc[...] += _dot_tn(pooled_b, d_q)
        sc["dpool_s"][...] = _dot_nt(d_q, pw_ref[...])
        grp = _lane_group(GROUP)
        for n in range(nblk):
            blk = slice(n * SGU_BLOCK, (n + 1) * SGU_BLOCK)
            dspb = sc["dsp_s"][blk, :]
            dsp_acc[...] += dspb
            dspb16 = dspb.astype(BF16)
            dvst = _dot(wmt_s[...], dspb16)
            dvb = None
            for hh in range(4):
                part = jnp.where(grp == hh, dvst[hh * SGU_BLOCK:(hh + 1) * SGU_BLOCK, :], 0.0)
                dvb = part if dvb is None else dvb + part
            sc["dvd_s"][blk, :] = dvb
            dwc = _dot_nt(dspb16, vst_s[n])
            osm_ref[ROW_WC:ROW_WC + SGU_BLOCK, :] += dwc[:, 0:GROUP]
            osm_ref[ROW_WC + SGU_BLOCK:ROW_WC + 2 * SGU_BLOCK, :] += dwc[:, GROUP:2 * GROUP]

        def ln_sums(rows):
            xh = sc["xh_s"][rows, :]
            d_a1 = sc["ra_s"][rows, :] * (sc["t1_s"][rows, :] - sc["m1_s"][rows, :] - xh * sc["m2_s"][rows, :])
            acc_add(0, d_a1)
            dbuf[rows, :] = d_a1
            pos = tile_idx * T + rows.start + lax.broadcasted_iota(jnp.int32, (RC, GROUP), 0) + 1
            lane = lax.broadcasted_iota(jnp.int32, (RC, GROUP), 1) // HEAD
            win = jnp.where(lane == 0, 2, jnp.where(lane == 1, 4, jnp.where(lane == 2, 8, 16)))
            fbuf[rows, :] = sc["dpool_s"][rows, :] / jnp.minimum(pos, win).astype(F32)
            d_v = sc["dvd_s"][rows, :]
            xv = sc["xv_s"][rows, :]
            acc_add(4, d_v * xv)
            acc_add(5, d_v)
            gd = d_v * vec(4)
            put_dh(10, rows, sc["rv_s"][rows, :] * (gd - _rowmean(gd) - xv * _rowmean(gd * xv)))
        chunks(RC, ln_sums)

        span = T + HALO_A - SUBLANES
        for p in range(1, SUBLANES):
            sh[p - 1, :, :] = dbuf[p:p + span, :]

        for r0 in range(0, T, ROWS):
            uc = sc["ub_s"][r0:r0 + ROWS, :]
            acc = None
            for k in range(KB):
                off = (KB - 1) - k + r0
                w = ebuf[off:off + ROWS, :]
                term = cbw_ref[k:k + 1, :] * w
                acc = term if acc is None else acc + term
                acc_add(ROW_CBW + k, uc * w)
            sc["du_s"][r0:r0 + ROWS, :] = acc
        ebuf[T:T + HALO_B, :] = ebuf[0:HALO_B, :]

        hi_lane = (lax.broadcasted_iota(jnp.int32, (1, 128), 1) // HEAD) == 1
        for r0 in range(0, T, ROWS):
            def win(col, j0, j1):
                s = None
                for j in range(j0, j1):
                    term = fbuf[r0 + j:r0 + j + ROWS, 128 * col:128 * (col + 1)]
                    s = term if s is None else s + term
                return s
            sc["cw_s"][r0:r0 + ROWS, 0:128] = win(0, 0, 2) + jnp.where(hi_lane, win(0, 2, 4), 0.0)
            sc["cw_s"][r0:r0 + ROWS, 128:256] = win(1, 0, 8) + jnp.where(hi_lane, win(1, 8, 16), 0.0)
        fbuf[T:T + HALO_C, :] = fbuf[0:HALO_C, :]

        def rest_bc(rows):
            d_u = sc["du_s"][rows, :]
            put_dh(4, rows, d_u * hs(5, rows))
            put_dh(5, rows, d_u * hs(4, rows))
            put_dh(7, rows, sc["cw_s"][rows, :] - sc["dpool_s"][rows, :])
        chunks(RC, rest_bc)

        dxt_s = sc["dxt_s"]

        def dx_term(k):
            term = _dot_nt(wi_ref[k], dhb_ref[:, COLS * k:COLS * (k + 1)])
            if k == 1:
                dxt_s[...] = term
            else:
                dxt_s[...] += term

        def conv_a(rows):
            a0c = sc["a0_s"][rows, :]
            acc = None
            for k in range(KA):
                off = (KA - 1) - k
                p, q8 = off % SUBLANES, off - off % SUBLANES
                w = dbuf[pl.ds(rows.start + q8, RC), :] if p == 0 else sh[p - 1, pl.ds(rows.start + q8, RC), :]
                term = caw_ref[k:k + 1, :] * w
                acc = term if acc is None else acc + term
                acc_add(ROW_CAW + k, a0c * w)
            sc["u_s"][rows, :] = acc
        n_chunks = T // RC
        after = {(n_chunks * j) // 3: j + 1 for j in range(3)}
        for c in range(n_chunks):
            conv_a(pl.ds(c * RC, RC))
            if c in after:
                dx_term(after[c])
        dbuf[T:T + HALO_A, :] = dbuf[0:HALO_A, :]

        def rest_a(rows):
            d_a0 = sc["u_s"][rows, :]
            sg = sc["sg_s"][rows, :]
            put_dh(0, rows, d_a0 * sg)
            put_dh(1, rows, d_a0 * hs(0, rows) * sg * (1.0 - sg))
        chunks(RC, rest_a)
        dx_term(0)
        dx_ref[...] += dxt_s[...].T

        @pl.when(i == nt - 1)
        def _():
            for row in list(range(6)) + list(range(ROW_CBW, ROW_CBW + KB)) + list(range(ROW_CAW, ROW_CAW + KA)) + list(
                    range(ROW_BIN, ROW_BIN + N_SLICES)):
                osm_ref[row:row + 1, :] = _colsum(acc_s[8 * row:8 * row + 8, :])
            for j, row in enumerate((ROW_LNG, ROW_LNB, ROW_BOUT)):
                cs = _colsum(acc_w[8 * j:8 * j + 8, :])
                for q in range(D_MODEL // GROUP):
                    osm_ref[row + q:row + q + 1, :] = cs[:, GROUP * q:GROUP * (q + 1)]
            r = lax.broadcasted_iota(jnp.int32, (SGU_BLOCK, GROUP), 0) // CHUNK
            c = (lax.broadcasted_iota(jnp.int32, (SGU_BLOCK, GROUP), 1) % SGU_BLOCK) // CHUNK
            for half in range(2):
                rows_ = slice(ROW_WC + half * SGU_BLOCK, ROW_WC + (half + 1) * SGU_BLOCK)
                osm_ref[rows_, :] = jnp.where(c <= r, osm_ref[rows_, :], 0.0)
            sb_t = _segdot(dsp_acc[...], e4_ref[...]).T
            osm_ref[ROW_SB:ROW_SB + 8, 0:SGU_BLOCK] = sb_t[0:8, :]
            for g in range(4):
                osm_ref[ROW_PW:ROW_PW + HEAD, HEAD * g:HEAD * (g + 1)] = (
                    pw_acc[HEAD * g:HEAD * (g + 1), HEAD * g:HEAD * (g + 1)])

    def rows(width):
        return pl.BlockSpec((T, width), lambda i, l: (nt - 1 - i, 0))

    consts = (wi, caw, cbw, s256, seg, pw, wm, wmt, sb, wo, v1024, e4)
    unstacked = (wi, seg, wo, e4)
    in_specs = [rows(D_MODEL), rows(D_MODEL), rows(IN_WIDTH), rows(3 * GROUP)] + [
        _whole(a) if any(a is u for u in unstacked) else _of_layer(a) for a in consts]
    out_specs = [rows(D_MODEL), rows(IN_WIDTH), rows(D_MODEL), pl.BlockSpec((SM_ROWS, GROUP), lambda i, l: (0, 0))]
    out_shape = [jax.ShapeDtypeStruct((S, D_MODEL), F32), jax.ShapeDtypeStruct((S, IN_WIDTH), BF16),
                 jax.ShapeDtypeStruct((S, D_MODEL), BF16), jax.ShapeDtypeStruct((SM_ROWS, GROUP), F32)]
    scratch_shapes = list(scratch.values())
    extra, aliases = (), {}
    if exch is not None:
        extra = tuple(exch)
        r_i, r_o = exch[3], exch[4]
        in_specs += [ANY] * 5
        out_specs += [ANY] * 3
        out_shape += [jax.ShapeDtypeStruct(r_i.shape, r_i.dtype), jax.ShapeDtypeStruct(r_o.shape, r_o.dtype),
                      jax.ShapeDtypeStruct((N_DEV, SM_ROWS, GROUP), F32)]
        scratch_shapes += [pltpu.SemaphoreType.DMA((N_EXCH_SEMS,)), pltpu.SemaphoreType.DMA((N_EXCH_SEMS,)),
                           pltpu.SemaphoreType.DMA((2,)), pltpu.VMEM((SM_ROWS, GROUP), F32)]
        aliases = {20: 4, 21: 5}
    grid_spec = pltpu.PrefetchScalarGridSpec(num_scalar_prefetch=1, grid=(nt,), in_specs=in_specs,
                                             out_specs=out_specs, scratch_shapes=scratch_shapes)
    return pl.pallas_call(
        body, name="bwd_layer" if exch is None else "bwd_layer_exchange",
        grid_spec=grid_spec, out_shape=out_shape, input_output_aliases=aliases,
        compiler_params=_vmem_params(dimension_semantics=("arbitrary",), has_side_effects=exch is not None),
    )(larr, dy, z, h, aux, *consts, *extra)


def _dw_in(layer, xb, dhb, slab, slab16, *, tk, small=None):
    S = xb.shape[0]
    ns = S // tk

    def body(*refs):
        l_ref, a_ref, b_ref = refs[0:3]
        o_ref, o16_ref = refs[n_in:n_in + 2]
        if small is not None:
            first = (pl.program_id(0) == 0) & (pl.program_id(1) == 0)
            last = (pl.program_id(0) == N_CHIPS - 1) & (pl.program_id(1) == ns - 1)
            _exchange_comm(first, last, None, None, None, refs[5], None, None, refs[n_in + 2], *refs[n_in + 3:])

        @pl.when(pl.program_id(1) == 0)
        def _():
            o_ref[...] = jnp.zeros_like(o_ref)
        o_ref[...] += _dot_tn(a_ref[...], b_ref[...])

        @pl.when(pl.program_id(1) == ns - 1)
        def _():
            o16_ref[...] = o_ref[...].astype(BF16)

    o_spec = pl.BlockSpec((None, None, D_MODEL, COLS), lambda j, s, l: (l[0], j, 0, 0))
    in_specs = [pl.BlockSpec((tk, D_MODEL), lambda j, s, l: (s, 0)), pl.BlockSpec((tk, COLS), lambda j, s, l: (s, j)),
                ANY, ANY]
    out_specs = [o_spec, o_spec]
    out_shape = [jax.ShapeDtypeStruct(slab.shape, F32), jax.ShapeDtypeStruct(slab.shape, BF16)]
    scratch, extra = [], ()
    if small is not None:
        extra = (small,)
        in_specs += [ANY]
        out_specs += [ANY]
        out_shape += [jax.ShapeDtypeStruct((N_DEV, SM_ROWS, GROUP), F32)]
        scratch = [pltpu.SemaphoreType.DMA((N_EXCH_SEMS,)), pltpu.SemaphoreType.DMA((N_EXCH_SEMS,)),
                   pltpu.SemaphoreType.DMA((2,)), pltpu.VMEM((SM_ROWS, GROUP), F32)]
    n_in = 5 + len(extra)
    grid_spec = pltpu.PrefetchScalarGridSpec(
        num_scalar_prefetch=1, grid=(N_CHIPS, ns), in_specs=in_specs, out_specs=out_specs, scratch_shapes=scratch)
    return pl.pallas_call(
        body, name="dw_in" if small is None else "dw_in_exchange", grid_spec=grid_spec, out_shape=out_shape,
        input_output_aliases={3: 0, 4: 1},
        compiler_params=_vmem_params(dimension_semantics=("arbitrary", "arbitrary"), has_side_effects=small is not None),
    )(layer, xb, dhb, slab, slab16, *extra)


def _dw_out(layer, mixb, dzb, slab, slab16, *, tk):
    S = mixb.shape[0]
    ns = S // tk

    def body(l_ref, a_ref, b_ref, slab_ref, slab16_ref, o_ref, o16_ref):
        del l_ref, slab_ref, slab16_ref

        @pl.when(pl.program_id(0) == 0)
        def _():
            o_ref[...] = jnp.zeros_like(o_ref)
        o_ref[...] += _dot_tn(a_ref[...], b_ref[...]).reshape(N_CHIPS, GROUP, D_MODEL)

        @pl.when(pl.program_id(0) == ns - 1)
        def _():
            o16_ref[...] = o_ref[...].astype(BF16)

    o_spec = pl.BlockSpec((None, N_CHIPS, GROUP, D_MODEL), lambda s, l: (l[0], 0, 0, 0))
    grid_spec = pltpu.PrefetchScalarGridSpec(
        num_scalar_prefetch=1, grid=(ns,),
        in_specs=[pl.BlockSpec((tk, D_MODEL), lambda s, l: (s, 0)), pl.BlockSpec((tk, D_MODEL), lambda s, l: (s, 0)),
                  ANY, ANY],
        out_specs=[o_spec, o_spec])
    return pl.pallas_call(
        body, name="dw_out", grid_spec=grid_spec,
        out_shape=[jax.ShapeDtypeStruct(slab.shape, F32), jax.ShapeDtypeStruct(slab.shape, BF16)],
        input_output_aliases={3: 0, 4: 1},
        compiler_params=_vmem_params(dimension_semantics=("arbitrary",)),
    )(layer, mixb, dzb, slab, slab16)


def _adamw_math(w, g, m, v):
    nm = ADAM_B1 * m + (1.0 - ADAM_B1) * g
    nv = ADAM_B2 * v + (1.0 - ADAM_B2) * (g * g)
    c1 = 1.0 - ADAM_B1 ** ADAM_STEP
    c2 = 1.0 - ADAM_B2 ** ADAM_STEP
    return -ADAM_LR * ((nm / c1) / (jnp.sqrt(nv / c2) + ADAM_EPS) + ADAM_WD * w), nm, nv


def _adamw_small(ws, gs, ms, vs):
    n = len(ws)

    def body(*refs):
        for j in range(n):
            d, nm, nv = _adamw_math(*(refs[k * n + j][...] for k in range(4)))
            refs[4 * n + j][...] = d
            refs[5 * n + j][...] = nm
            refs[6 * n + j][...] = nv

    shapes = [jax.ShapeDtypeStruct(w.shape, F32) for w in ws]
    outs = pl.pallas_call(body, name="adamw_small", out_shape=shapes * 3, compiler_params=_vmem_params())(
        *ws, *gs, *ms, *vs)
    return outs[0:n], outs[n:2 * n], outs[2 * n:3 * n]


def _adamw(w, g, m, v, *, rows_per_step, name, copy_g=False):
    R, C = w.shape
    tr = rows_per_step

    def body(w_ref, g_ref, m_ref, v_ref, d_ref, nm_ref, nv_ref, *g_out):
        g_ = g_ref[...]
        d_ref[...], nm_ref[...], nv_ref[...] = _adamw_math(w_ref[...], g_, m_ref[...], v_ref[...])
        if copy_g:
            g_out[0][...] = g_

    spec = pl.BlockSpec((tr, C), lambda i: (i, 0))
    n_out = 4 if copy_g else 3
    return pl.pallas_call(
        body, name=name, grid=(R // tr,),
        in_specs=[spec] * 4, out_specs=[spec] * n_out,
        out_shape=[jax.ShapeDtypeStruct((R, C), F32)] * n_out,
        compiler_params=_vmem_params(dimension_semantics=("arbitrary",)),
    )(w, g, m, v)


def _gather_weights(wi16, wo16, cw):
    L = wi16.shape[0]
    hi_rows, ho_rows = D_MODEL // 2, GROUP // 2
    n_ici = 2 * L + 1
    n_fwd = 2 * L

    def body(wi_ref, wo_ref, cw_ref, *rest):
        wig = rest[0:L]
        wog = rest[L:2 * L]
        cwg = rest[2 * L]
        send_sems, recv_sems, loc_sems, vwi, vwo, vcw = rest[2 * L + 1:]
        x, y, c = _place()
        me_k = 2 * x + y
        sibling = (x, y, 1 - c)
        chips = _other_chips(x, y)

        def half_i(ref, blk):
            return ref.at[blk, pl.ds(c * hi_rows, hi_rows), :]

        def half_o(ref, blk):
            return ref.at[blk, pl.ds(c * ho_rows, ho_rows), :]

        def other_half_i(ref, blk):
            return ref.at[blk, pl.ds((1 - c) * hi_rows, hi_rows), :]

        def other_half_o(ref, blk):
            return ref.at[blk, pl.ds((1 - c) * ho_rows, ho_rows), :]

        stage_in = [pltpu.make_async_copy(wi_ref, vwi, loc_sems.at[0]), pltpu.make_async_copy(wo_ref, vwo, loc_sems.at[1]),
                    pltpu.make_async_copy(cw_ref, vcw, loc_sems.at[2])]
        local = []
        for l in range(L):
            local.append(pltpu.make_async_copy(vwi.at[l], wig[l].at[me_k], loc_sems.at[3 + 2 * l]))
            local.append(pltpu.make_async_copy(vwo.at[l], wog[l].at[me_k], loc_sems.at[3 + 2 * l + 1]))
        local.append(pltpu.make_async_copy(vcw, cwg.at[me_k], loc_sems.at[3 + 2 * L]))
        for cp in stage_in:
            cp.start()

        def remote(src, dst, sem, to):
            return pltpu.make_async_remote_copy(src_ref=src, dst_ref=dst, send_sem=send_sems.at[sem],
                                                recv_sem=recv_sems.at[sem], device_id=to, device_id_type=MESH)

        sends = []
        for r, (px, py, _) in enumerate(chips):
            to = (px, py, c)
            for l in range(L):
                sends.append(remote(half_i(wi_ref, l), half_i(wig[l], me_k), r * n_ici + 2 * l, to))
                sends.append(remote(half_o(wo_ref, l), half_o(wog[l], me_k), r * n_ici + 2 * l + 1, to))
            sends.append(remote(cw_ref, cwg.at[me_k], r * n_ici + 2 * L, to))
        for cp in sends:
            cp.start()
        for cp in stage_in:
            cp.wait()
        for cp in local:
            cp.start()

        base = 3 * n_ici
        fwds = []
        for r, (px, py, pk) in enumerate(chips):
            for l in range(L):
                remote(half_i(wig[l], pk), half_i(wig[l], pk), r * n_ici + 2 * l, sibling).wait_recv()
                f = remote(half_i(wig[l], pk), half_i(wig[l], pk), base + r * n_fwd + 2 * l, sibling)
                f.start()
                fwds.append(f)
                remote(half_o(wog[l], pk), half_o(wog[l], pk), r * n_ici + 2 * l + 1, sibling).wait_recv()
                f = remote(half_o(wog[l], pk), half_o(wog[l], pk), base + r * n_fwd + 2 * l + 1, sibling)
                f.start()
                fwds.append(f)
            remote(cwg.at[pk], cwg.at[pk], r * n_ici + 2 * L, sibling).wait_recv()
        for r, (px, py, pk) in enumerate(chips):
            for l in range(L):
                remote(other_half_i(wig[l], pk), other_half_i(wig[l], pk), base + r * n_fwd + 2 * l, sibling).wait_recv()
                remote(other_half_o(wog[l], pk), other_half_o(wog[l], pk), base + r * n_fwd + 2 * l + 1, sibling).wait_recv()
        for cp in sends + fwds:
            cp.wait_send()
        for cp in local:
            cp.wait()

    n_sem = 3 * n_ici + 3 * n_fwd
    out_shape = ([jax.ShapeDtypeStruct((N_CHIPS, D_MODEL, COLS), BF16)] * L
                 + [jax.ShapeDtypeStruct((N_CHIPS, GROUP, D_MODEL), BF16)] * L
                 + [jax.ShapeDtypeStruct((N_CHIPS,) + cw.shape, F32)])
    outs = pl.pallas_call(
        body, name="gather_weights",
        in_specs=[ANY, ANY, ANY], out_specs=[ANY] * (2 * L + 1), out_shape=out_shape,
        scratch_shapes=[pltpu.SemaphoreType.DMA((n_sem,)), pltpu.SemaphoreType.DMA((n_sem,)),
                        pltpu.SemaphoreType.DMA((2 * L + 4,)), pltpu.VMEM(wi16.shape, BF16), pltpu.VMEM(wo16.shape, BF16),
                        pltpu.VMEM(cw.shape, F32)],
        compiler_params=_vmem_params(has_side_effects=True),
    )(wi16, wo16, cw)
    return outs[0:L], outs[L:2 * L], outs[2 * L]


def _swap_halves(l_arr, gwi, gwo, ri, ro):
    hi_rows, ho_rows = D_MODEL // 2, GROUP // 2

    def body(l_ref, gwi_ref, gwo_ref, ri_in, ro_in, ri_ref, ro_ref, send_sems, recv_sems):
        del ri_in, ro_in
        x, y, c = _place()
        l = l_ref[0]
        sibling = (x, y, 1 - c)
        cps = [
            pltpu.make_async_remote_copy(src_ref=gwi_ref.at[l, :, pl.ds((1 - c) * hi_rows, hi_rows), :],
                                         dst_ref=ri_ref.at[l], send_sem=send_sems.at[0], recv_sem=recv_sems.at[0],
                                         device_id=sibling, device_id_type=MESH),
            pltpu.make_async_remote_copy(src_ref=gwo_ref.at[l, :, pl.ds((1 - c) * ho_rows, ho_rows), :],
                                         dst_ref=ro_ref.at[l], send_sem=send_sems.at[1], recv_sem=recv_sems.at[1],
                                         device_id=sibling, device_id_type=MESH),
        ]
        for cp in cps:
            cp.start()
        for cp in cps:
            cp.wait()

    return pl.pallas_call(
        body, name="swap_halves",
        in_specs=[pl.BlockSpec(memory_space=pltpu.SMEM), ANY, ANY, ANY, ANY], out_specs=[ANY, ANY],
        out_shape=[jax.ShapeDtypeStruct(ri.shape, ri.dtype), jax.ShapeDtypeStruct(ro.shape, ro.dtype)],
        input_output_aliases={3: 0, 4: 1},
        scratch_shapes=[pltpu.SemaphoreType.DMA((2,)), pltpu.SemaphoreType.DMA((2,))],
        compiler_params=pltpu.CompilerParams(has_side_effects=True),
    )(l_arr, gwi, gwo, ri, ro)


def _add_halves(cl_arr, g_i, r_i, p_i, g_o, r_o, p_o, *, nb):
    def body(cl_ref, gi_ref, ri_ref, pi_in, go_ref, ro_ref, po_in, oi_ref, oo_ref):
        del cl_ref, pi_in, po_in
        oi_ref[...] = (gi_ref[...] + ri_ref[...].astype(F32)).astype(oi_ref.dtype)
        oo_ref[...] = (go_ref[...] + ro_ref[...].astype(F32)).astype(oo_ref.dtype)

    def specs(r):
        tr, cols = r.shape[2] // nb, r.shape[3]
        mine = pl.BlockSpec((None, None, tr, cols), lambda k, i, cl: (cl[1], k, cl[0] * nb + i, 0))
        same = pl.BlockSpec((None, None, tr, cols), lambda k, i, cl: (cl[1], k, i, 0))
        return mine, same

    (gi_s, ri_s), (go_s, ro_s) = specs(r_i), specs(r_o)
    grid_spec = pltpu.PrefetchScalarGridSpec(
        num_scalar_prefetch=1, grid=(N_CHIPS, nb),
        in_specs=[gi_s, ri_s, ANY, go_s, ro_s, ANY], out_specs=[ri_s, ro_s])
    return pl.pallas_call(
        body, name="add_halves", grid_spec=grid_spec,
        out_shape=[jax.ShapeDtypeStruct(p_i.shape, p_i.dtype), jax.ShapeDtypeStruct(p_o.shape, p_o.dtype)],
        input_output_aliases={3: 0, 6: 1},
        compiler_params=_vmem_params(dimension_semantics=("arbitrary",) * 2),
    )(cl_arr, g_i, r_i, p_i, g_o, r_o, p_o)


def _swap_add(cl_arr, g_i, g16_i, p_i, g_o, g16_o, p_o):
    hi, ho = p_i.shape[2], p_o.shape[2]

    def body(cl_ref, gi_ref, gi16_ref, pi_in, go_ref, go16_ref, po_in, oi_ref, oo_ref, ri_v, ro_v, send_sems,
             recv_sems):
        del pi_in, po_in
        k = pl.program_id(0)
        x, y, c = _place()
        l = cl_ref[1]

        def copies(kk):
            pair = ((gi16_ref, hi, ri_v), (go16_ref, ho, ro_v))
            return [pltpu.make_async_remote_copy(
                src_ref=src.at[l, kk, pl.ds((1 - c) * n, n), :], dst_ref=dst.at[kk], send_sem=send_sems.at[2 * kk + j],
                recv_sem=recv_sems.at[2 * kk + j], device_id=(x, y, 1 - c), device_id_type=MESH)
                for j, (src, n, dst) in enumerate(pair)]

        @pl.when(k == 0)
        def _():
            for kk in range(N_CHIPS):
                for cp in copies(kk):
                    cp.start()

        for cp in copies(k):
            cp.wait_recv()
        oi_ref[...] = (gi_ref[...] + ri_v[k].astype(F32)).astype(oi_ref.dtype)
        oo_ref[...] = (go_ref[...] + ro_v[k].astype(F32)).astype(oo_ref.dtype)

        @pl.when(k == N_CHIPS - 1)
        def _():
            for kk in range(N_CHIPS):
                for cp in copies(kk):
                    cp.wait_send()

    def specs(p):
        rows, cols = p.shape[2], p.shape[3]
        mine = pl.BlockSpec((None, None, rows, cols), lambda k, cl: (cl[1], k, cl[0], 0))
        out = pl.BlockSpec((None, None, rows, cols), lambda k, cl: (cl[1], k, 0, 0))
        return mine, out

    (gi_s, pi_s), (go_s, po_s) = specs(p_i), specs(p_o)
    grid_spec = pltpu.PrefetchScalarGridSpec(
        num_scalar_prefetch=1, grid=(N_CHIPS,),
        in_specs=[gi_s, ANY, ANY, go_s, ANY, ANY], out_specs=[pi_s, po_s],
        scratch_shapes=[pltpu.VMEM((N_CHIPS, hi, p_i.shape[3]), BF16), pltpu.VMEM((N_CHIPS, ho, p_o.shape[3]), BF16),
                        pltpu.SemaphoreType.DMA((2 * N_CHIPS,)), pltpu.SemaphoreType.DMA((2 * N_CHIPS,))])
    return pl.pallas_call(
        body, name="swap_add", grid_spec=grid_spec,
        out_shape=[jax.ShapeDtypeStruct(p_i.shape, p_i.dtype), jax.ShapeDtypeStruct(p_o.shape, p_o.dtype)],
        input_output_aliases={3: 0, 6: 1},
        compiler_params=_vmem_params(dimension_semantics=("arbitrary",), has_side_effects=True),
    )(cl_arr, g_i, g16_i, p_i, g_o, g16_o, p_o)


def _exchange_last(l_arr, p_i, p_o, r_i, r_o):
    def body(l_ref, p_i_ref, p_o_ref, ri_in, ro_in, ri_ref, ro_ref, send_sems, recv_sems):
        del ri_in, ro_in
        always = l_ref[0] >= 0
        _exchange_comm(always, always, l_ref[0], p_i_ref, p_o_ref, None, ri_ref, ro_ref, None,
                       send_sems, recv_sems)

    return pl.pallas_call(
        body, name="exchange_last",
        in_specs=[pl.BlockSpec(memory_space=pltpu.SMEM)] + [ANY] * 4, out_specs=[ANY] * 2,
        out_shape=[jax.ShapeDtypeStruct(r_i.shape, r_i.dtype), jax.ShapeDtypeStruct(r_o.shape, r_o.dtype)],
        input_output_aliases={3: 0, 4: 1},
        scratch_shapes=[pltpu.SemaphoreType.DMA((N_EXCH_SEMS,)), pltpu.SemaphoreType.DMA((N_EXCH_SEMS,))],
        compiler_params=pltpu.CompilerParams(has_side_effects=True),
    )(l_arr, p_i, p_o, r_i, r_o)


def _sum_small(r_sms):
    L = len(r_sms)

    def body(*refs):
        o_ref = refs[L]
        for l in range(L):
            acc = refs[l][0]
            for d in range(1, N_DEV):
                acc = acc + refs[l][d]
            o_ref[l] = acc

    return pl.pallas_call(
        body, name="sum_small",
        out_shape=jax.ShapeDtypeStruct((L,) + r_sms[0].shape[1:], F32),
        compiler_params=_vmem_params(),
    )(*r_sms)


def _sum_chunks(kc_arr, p_i, q_i, p_o, q_o, *, nb):
    L = p_i.shape[0]

    def body(kc_ref, pi_ref, a0, a1, a2, po_ref, b0, b1, b2, oi_ref, oo_ref):
        del kc_ref
        f = lambda ref: ref[...].astype(F32)
        oi_ref[...] = ((f(pi_ref) + f(a0)) + f(a1)) + f(a2)
        oo_ref[...] = ((f(po_ref) + f(b0)) + f(b1)) + f(b2)

    def specs(p):
        tr, cols = p.shape[2] // nb, p.shape[3]
        chunk = pl.BlockSpec((None, None, tr, cols), lambda l, i, kc: (l, kc[0], i, 0))
        got = [pl.BlockSpec((None, None, tr, cols), lambda l, i, kc, _j=j: (_j, l, i, 0)) for j in range(3)]
        out = pl.BlockSpec((None, tr, cols), lambda l, i, kc: (l, kc[1] * nb + i, 0))
        return [chunk] + got, out

    (in_i, out_i), (in_o, out_o) = specs(p_i), specs(p_o)
    grid_spec = pltpu.PrefetchScalarGridSpec(num_scalar_prefetch=1, grid=(L, nb), in_specs=in_i + in_o,
                                             out_specs=[out_i, out_o])
    return pl.pallas_call(
        body, name="sum_chunks", grid_spec=grid_spec,
        out_shape=[jax.ShapeDtypeStruct((L, 2 * p.shape[2], p.shape[3]), F32) for p in (p_i, p_o)],
        compiler_params=_vmem_params(dimension_semantics=("arbitrary",) * 2),
    )(kc_arr, p_i, q_i, q_i, q_i, p_o, q_o, q_o, q_o)


def _share_result(gi, go):
    hi_rows, ho_rows = gi.shape[1] // 2, go.shape[1] // 2

    def body(gi_ref, go_ref, oi_ref, oo_ref, send_sems, recv_sems):
        del gi_ref, go_ref
        x, y, c = _place()
        sibling = (x, y, 1 - c)
        cps = []
        for j, (ref, n) in enumerate(((oi_ref, hi_rows), (oo_ref, ho_rows))):
            mine = ref.at[:, pl.ds(c * n, n), :]
            cps.append(pltpu.make_async_remote_copy(src_ref=mine, dst_ref=mine, send_sem=send_sems.at[j],
                                                    recv_sem=recv_sems.at[j], device_id=sibling, device_id_type=MESH))
        for cp in cps:
            cp.start()
        for j, (ref, n) in enumerate(((oi_ref, hi_rows), (oo_ref, ho_rows))):
            theirs = ref.at[:, pl.ds((1 - c) * n, n), :]
            pltpu.make_async_remote_copy(src_ref=theirs, dst_ref=theirs, send_sem=send_sems.at[j],
                                         recv_sem=recv_sems.at[j], device_id=sibling, device_id_type=MESH).wait_recv()
        for cp in cps:
            cp.wait_send()

    return pl.pallas_call(
        body, name="share_result",
        in_specs=[ANY, ANY], out_specs=[ANY, ANY],
        out_shape=[jax.ShapeDtypeStruct(gi.shape, F32), jax.ShapeDtypeStruct(go.shape, F32)],
        input_output_aliases={0: 0, 1: 1},
        scratch_shapes=[pltpu.SemaphoreType.DMA((2,)), pltpu.SemaphoreType.DMA((2,))],
        compiler_params=pltpu.CompilerParams(has_side_effects=True),
    )(gi, go)


WEIGHTS = ("ln_g", "ln_b", "w_in", "b_in", "conv_a_w", "conv_a_b", "norm_a_g", "norm_a_b", "conv_b_w", "pool_w",
           "pool_scale", "sgu_ln_g", "sgu_ln_b", "sgu_w", "sgu_bias", "w_out", "b_out")


def _pad_rows(a, rows):
    return jnp.pad(a, ((0, rows - a.shape[0]), (0, 0)))


def _indicator_consts():
    seg = jnp.where((jnp.arange(GROUP)[:, None] // HEAD) == (jnp.arange(GROUP)[None, :] // HEAD),
                    1.0 / HEAD, 0.0).astype(BF16)
    e4 = ((jnp.arange(GROUP)[:, None] // HEAD) == jnp.arange(128)[None, :]).astype(BF16)
    return seg, e4


def _layer_consts(p, conv_full):
    L = conv_full.shape[0]
    same_head = jnp.eye(4, dtype=F32)[:, None, :, None] > 0

    def rows_to(a, rows):
        return jnp.pad(a, ((0, 0), (0, rows - a.shape[1]), (0, 0)))

    s256 = jnp.stack([p[n] for n in ("conv_a_b", "norm_a_g", "norm_a_b", "pool_scale", "sgu_ln_g", "sgu_ln_b")], axis=1)
    pw = jnp.where(same_head, p["pool_w"][:, :, :, None, :], 0.0).reshape(L, GROUP, GROUP)
    return dict(
        caw=rows_to(conv_full[:, :KA], 32), cbw=rows_to(conv_full[:, KA:], 8), s256=rows_to(s256, 8),
        pw=pw.astype(BF16),
        wm=jnp.transpose(p["sgu_w"], (0, 2, 1, 3)).reshape(L, SGU_BLOCK, 4 * SGU_BLOCK),
        wmt=jnp.transpose(p["sgu_w"], (0, 1, 3, 2)).reshape(L, 4 * SGU_BLOCK, SGU_BLOCK),
        sb=jnp.repeat(jnp.transpose(p["sgu_bias"], (0, 2, 1)), HEAD, axis=2),
        v1024=rows_to(jnp.stack([p["b_out"], p["ln_g"], p["ln_b"]], axis=1), 8),
        bin=p["b_in"][:, None, :])


def _unpack_small(sm):
    L = sm.shape[0]
    owc = jnp.concatenate([sm[:, ROW_WC:ROW_WC + SGU_BLOCK], sm[:, ROW_WC + SGU_BLOCK:ROW_WC + 2 * SGU_BLOCK]], axis=2)
    return dict(
        conv_a_b=sm[:, 0], norm_a_g=sm[:, 1], norm_a_b=sm[:, 2], pool_scale=sm[:, 3], sgu_ln_g=sm[:, 4],
        sgu_ln_b=sm[:, 5], conv_b_w=sm[:, ROW_CBW:ROW_CBW + KB], conv_a_w=sm[:, ROW_CAW:ROW_CAW + KA],
        pool_w=jnp.transpose(sm[:, ROW_PW:ROW_PW + HEAD].reshape(L, HEAD, 4, HEAD), (0, 2, 1, 3)),
        ln_g=sm[:, ROW_LNG:ROW_LNG + 4].reshape(L, D_MODEL), ln_b=sm[:, ROW_LNB:ROW_LNB + 4].reshape(L, D_MODEL),
        b_out=sm[:, ROW_BOUT:ROW_BOUT + 4].reshape(L, D_MODEL),
        b_in=sm[:, ROW_BIN:ROW_BIN + N_SLICES].reshape(L, IN_WIDTH),
        sgu_w=jnp.transpose(owc.reshape(L, SGU_BLOCK, 4, SGU_BLOCK), (0, 2, 1, 3)),
        sgu_bias=sm[:, ROW_SB:ROW_SB + 4, 0:SGU_BLOCK])


def _step(p, m, v, x, target, *, tile_f, tile_b, tk_in, tk_out):
    L = p["ln_g"].shape[0]
    xi, yi, ci = _place()
    me_k = 2 * xi + yi
    hi_rows, ho_rows = D_MODEL // 2, GROUP // 2

    cw = jnp.concatenate([p["conv_a_w"], p["conv_b_w"]], axis=1).reshape(-1, 128)
    cw_rows = cw.shape[0]
    cw = _pad_rows(cw, -(-cw_rows // SUBLANES) * SUBLANES)
    wi16 = p["w_in"].astype(BF16)
    wo16 = p["w_out"].astype(BF16)
    wig0, wog0, cwg = _gather_weights(wi16[0:1], wo16[0:1], cw)
    cwg = cwg[:, :cw_rows].reshape(N_CHIPS, L, KA + KB, HEAD)
    conv_full = jnp.transpose(cwg, (1, 2, 0, 3)).reshape(L, KA + KB, GROUP)
    seg, e4 = _indicator_consts()
    k = _layer_consts(p, conv_full)
    layer = [jnp.full((1,), l, jnp.int32) for l in range(L)]

    hcur = x
    saved, wig, wog = [], [wig0[0]], [wog0[0]]
    for l in range(L):
        nxt = (wi16, wo16) if l + 1 < L else None
        outs = _fwd_layer(layer[l], hcur, wig[l], k["bin"], k["caw"], k["cbw"], k["s256"], seg, k["pw"], k["wm"], k["sb"],
                          wog[l], k["v1024"], tile=tile_f, nxt=nxt, target=None if nxt is not None else target)
        y, xb, h, aux, mixb, z = outs[0:6]
        if nxt is not None:
            wig.append(outs[6])
            wog.append(outs[7])
        saved.append((xb, h, aux, mixb, z))
        hcur = y

    dy = hcur
    loss_local = outs[6][0, 0]

    gwi = lax.empty((L, N_CHIPS, D_MODEL, COLS), F32)
    gwo = lax.empty((L, N_CHIPS, GROUP, D_MODEL), F32)
    gwi16 = lax.empty((L, N_CHIPS, D_MODEL, COLS), BF16)
    gwo16 = lax.empty((L, N_CHIPS, GROUP, D_MODEL), BF16)
    ri = lax.empty((L, N_CHIPS, hi_rows, COLS), BF16)
    ro = lax.empty((L, N_CHIPS, ho_rows, D_MODEL), BF16)
    p_i = lax.empty((L, N_CHIPS, hi_rows, COLS), BF16)
    p_o = lax.empty((L, N_CHIPS, ho_rows, D_MODEL), BF16)
    q_i = lax.empty((3, L, hi_rows, COLS), BF16)
    q_o = lax.empty((3, L, ho_rows, D_MODEL), BF16)
    r_sm = [None] * L
    pending = None
    for l in reversed(range(L)):
        xb, h, aux, mixb, z = saved[l]
        exch = None if pending is None else (p_i, p_o, pending, q_i, q_o)
        outs = _bwd_layer(layer[l], dy, z, h, aux, wig[l], k["caw"], k["cbw"], k["s256"], seg, k["pw"], k["wm"],
                          k["wmt"], k["sb"], wog[l], k["v1024"], e4, tile=tile_b, exch=exch)
        dy, dhb, dzb, osm = outs[0:4]
        if l == L - 1:
            osm = osm.at[ROW_LOSS, 0].set(loss_local)
        if exch is not None:
            q_i, q_o, r_sm[l + 1] = outs[4:7]
        larr = layer[l]
        if l > 0:
            gwi, gwi16 = _dw_in(larr, xb, dhb, gwi, gwi16, tk=tk_in)
        else:
            gwi, gwi16, r_sm[0] = _dw_in(larr, xb, dhb, gwi, gwi16, tk=tk_out, small=osm)
        gwo, gwo16 = _dw_out(larr, mixb, dzb, gwo, gwo16, tk=tk_out)
        cl_arr = jnp.stack([ci, jnp.int32(l)]).astype(jnp.int32)
        p_i, p_o = _swap_add(cl_arr, gwi, gwi16, p_i, gwo, gwo16, p_o)
        pending = osm
    grad_x = dy
    q_i, q_o = _exchange_last(layer[0], p_i, p_o, q_i, q_o)

    summed = _sum_small(r_sm)
    loss = summed[L - 1, ROW_LOSS, 0]
    grads = _unpack_small(summed)
    for n in ("conv_a_w", "conv_b_w"):
        grads[n] = lax.dynamic_slice_in_dim(grads[n], me_k * HEAD, HEAD, axis=2)

    kc_arr = jnp.stack([me_k, ci]).astype(jnp.int32)
    g_i, g_o = _sum_chunks(kc_arr, p_i, q_i, p_o, q_o, nb=2)
    g_i, g_o = _share_result(g_i, g_o)
    grads["w_in"] = g_i
    grads["w_out"] = g_o

    delta, new_m, new_v = {}, {}, {}
    for n, tr in (("w_in", 512), ("w_out", 256)):
        shp = p[n].shape
        args = [a.reshape(shp[0] * shp[1], shp[2]) for a in (p[n], grads[n], m[n], v[n])]
        outs = _adamw(*args, rows_per_step=tr, name="adamw_" + n, copy_g=True)
        delta[n], new_m[n], new_v[n], grads[n] = (a.reshape(shp) for a in outs)
    small = [n for n in WEIGHTS if n not in ("w_in", "w_out")]
    flat = [[a[n].reshape(-1, a[n].shape[-1]) for n in small] for a in (p, grads, m, v)]
    outs = _adamw_small(*flat)
    for j, n in enumerate(small):
        delta[n], new_m[n], new_v[n] = (o[j].reshape(p[n].shape) for o in outs)

    return (loss, grad_x[None], *[grads[n] for n in WEIGHTS], *[delta[n] for n in WEIGHTS],
            *[new_m[n] for n in WEIGHTS], *[new_v[n] for n in WEIGHTS])


def kernel(x, ln_g, ln_b, w_in, b_in, conv_a_w, conv_a_b, norm_a_g, norm_a_b, conv_b_w, pool_w, pool_scale, sgu_ln_g, sgu_ln_b, sgu_w, sgu_bias, w_out, b_out, loss_target, m_ln_g, m_ln_b, m_w_in, m_b_in, m_conv_a_w, m_conv_a_b, m_norm_a_g, m_norm_a_b, m_conv_b_w, m_pool_w, m_pool_scale, m_sgu_ln_g, m_sgu_ln_b, m_sgu_w, m_sgu_bias, m_w_out, m_b_out, v_ln_g, v_ln_b, v_w_in, v_b_in, v_conv_a_w, v_conv_a_b, v_norm_a_g, v_norm_a_b, v_conv_b_w, v_pool_w, v_pool_scale, v_sgu_ln_g, v_sgu_ln_b, v_sgu_w, v_sgu_bias, v_w_out, v_b_out):
    p = dict(ln_g=ln_g, ln_b=ln_b, w_in=w_in, b_in=b_in, conv_a_w=conv_a_w, conv_a_b=conv_a_b, norm_a_g=norm_a_g,
             norm_a_b=norm_a_b, conv_b_w=conv_b_w, pool_w=pool_w, pool_scale=pool_scale, sgu_ln_g=sgu_ln_g,
             sgu_ln_b=sgu_ln_b, sgu_w=sgu_w, sgu_bias=sgu_bias, w_out=w_out, b_out=b_out)
    m = dict(ln_g=m_ln_g, ln_b=m_ln_b, w_in=m_w_in, b_in=m_b_in, conv_a_w=m_conv_a_w, conv_a_b=m_conv_a_b,
             norm_a_g=m_norm_a_g, norm_a_b=m_norm_a_b, conv_b_w=m_conv_b_w, pool_w=m_pool_w, pool_scale=m_pool_scale,
             sgu_ln_g=m_sgu_ln_g, sgu_ln_b=m_sgu_ln_b, sgu_w=m_sgu_w, sgu_bias=m_sgu_bias, w_out=m_w_out, b_out=m_b_out)
    v = dict(ln_g=v_ln_g, ln_b=v_ln_b, w_in=v_w_in, b_in=v_b_in, conv_a_w=v_conv_a_w, conv_a_b=v_conv_a_b,
             norm_a_g=v_norm_a_g, norm_a_b=v_norm_a_b, conv_b_w=v_conv_b_w, pool_w=v_pool_w, pool_scale=v_pool_scale,
             sgu_ln_g=v_sgu_ln_g, sgu_ln_b=v_sgu_ln_b, sgu_w=v_sgu_w, sgu_bias=v_sgu_bias, w_out=v_w_out, b_out=v_b_out)
    return _step(p, m, v, x[0], loss_target[0], tile_f=256, tile_b=256, tk_in=4096, tk_out=2048)
```

```python
import jax
import jax.numpy as jnp
from jax import lax
from jax.experimental import pallas as pl
from jax.experimental.pallas import tpu as pltpu

F32 = jnp.float32
BF16 = jnp.bfloat16
MESH = pl.DeviceIdType.MESH

D_MODEL = 1024
GROUP = 256
HEAD = 64
N_SLICES = 12
IN_WIDTH = N_SLICES * GROUP
N_CHIPS = 4
COLS = IN_WIDTH // N_CHIPS
KA = 31
KB = 3
SUBLANES = 8
HALO_A, HALO_B, HALO_C = 32, 8, 16
N_GATHER_SEMS = 12
N_EXCH_SEMS = 13
SGU_BLOCK = 128
CHUNK = 64
LN_EPS = 1e-5
ROWS = 64
V7X_VMEM_BYTES = 64 * 1024 * 1024
VMEM_LIMIT = V7X_VMEM_BYTES - 8 * 1024 * 1024

ADAM_LR, ADAM_B1, ADAM_B2, ADAM_EPS, ADAM_WD, ADAM_STEP = 0.001, 0.9, 0.999, 1e-08, 0.01, 10


ANY = pl.BlockSpec(memory_space=pl.ANY)


def _vmem_params(**kw):
    return pltpu.CompilerParams(vmem_limit_bytes=VMEM_LIMIT, **kw)


def _whole(a):
    return pl.BlockSpec(a.shape, lambda i, l, _n=a.ndim: (0,) * _n)


def _of_layer(a):
    return pl.BlockSpec((None,) + a.shape[1:], lambda i, l, _n=a.ndim: (l[0],) + (0,) * (_n - 1))


def _place():
    return lax.axis_index("x"), lax.axis_index("y"), lax.axis_index("c")


def _other_chips(x, y):
    return [(1 - x, y, 2 * (1 - x) + y), (x, 1 - y, 2 * x + (1 - y)), (1 - x, 1 - y, 2 * (1 - x) + (1 - y))]


def _sig(v):
    return 0.5 * jnp.tanh(0.5 * v) + 0.5


def _dot(a, b):
    return jnp.dot(a, b, preferred_element_type=F32)


def _dot_nt(a, b):
    return lax.dot_general(a, b, (((1,), (1,)), ((), ())), preferred_element_type=F32)


def _dot_tn(a, b):
    return lax.dot_general(a, b, (((0,), (0,)), ((), ())), preferred_element_type=F32)


def _segdot(v, m):
    hi = v.astype(BF16)
    lo = (v - hi.astype(F32)).astype(BF16)
    return _dot(hi, m) + _dot(lo, m)


def _colsum(v):
    return jnp.sum(v, axis=0, keepdims=True)


def _rowmean(v):
    return jnp.mean(v, axis=-1, keepdims=True)


def _lane_group(n):
    return lax.broadcasted_iota(jnp.int32, (1, n), 1) // HEAD


def _pool_cnt(tile, t_rows):
    pos = tile * t_rows + lax.broadcasted_iota(jnp.int32, (t_rows, GROUP), 0) + 1
    grp = lax.broadcasted_iota(jnp.int32, (t_rows, GROUP), 1) // HEAD
    win = jnp.where(grp == 0, 2, jnp.where(grp == 1, 4, jnp.where(grp == 2, 8, 16)))
    return jnp.minimum(pos, win).astype(F32)


def _sgu_masks(wm_ref, wmt_ref, wm_s, wmt_s):
    r = lax.broadcasted_iota(jnp.int32, (SGU_BLOCK, 4 * SGU_BLOCK), 0) // CHUNK
    c = (lax.broadcasted_iota(jnp.int32, (SGU_BLOCK, 4 * SGU_BLOCK), 1) % SGU_BLOCK) // CHUNK
    wm_s[...] = jnp.where(c <= r, wm_ref[...], 0.0).astype(BF16)
    if wmt_ref is not None:
        rt = (lax.broadcasted_iota(jnp.int32, (4 * SGU_BLOCK, SGU_BLOCK), 0) % SGU_BLOCK) // CHUNK
        ct = lax.broadcasted_iota(jnp.int32, (4 * SGU_BLOCK, SGU_BLOCK), 1) // CHUNK
        wmt_s[...] = jnp.where(rt <= ct, wmt_ref[...], 0.0).astype(BF16)


def _vstack(v_blk):
    grp = _lane_group(GROUP)
    return jnp.concatenate([jnp.where(grp == h, v_blk, 0.0) for h in range(4)], axis=0).astype(BF16)


def _gather_next(step, nt, nwi, nwo, gwi, gwo, send_sems, recv_sems, loc_sems, vwi, vwo):
    x, y, c = _place()
    me_k = 2 * x + y
    sibling = (x, y, 1 - c)
    chips = _other_chips(x, y)
    hi, ho = D_MODEL // 2, GROUP // 2
    fwd_sems = N_GATHER_SEMS // 2

    def rc(src, dst, sem, to):
        return pltpu.make_async_remote_copy(src_ref=src, dst_ref=dst, send_sem=send_sems.at[sem],
                                            recv_sem=recv_sems.at[sem], device_id=to, device_id_type=MESH)

    def blk(ref, k, n, cc):
        return ref.at[k, pl.ds(cc * n, n), :]

    def ici(r):
        px, py, _ = chips[r]
        to = (px, py, c)
        return [rc(nwi.at[pl.ds(c * hi, hi), :], blk(gwi, me_k, hi, c), 2 * r, to),
                rc(nwo.at[pl.ds(c * ho, ho), :], blk(gwo, me_k, ho, c), 2 * r + 1, to)]

    def landed(r, cc, base):
        pk = chips[r][2]
        return [rc(blk(gwi, pk, hi, cc), blk(gwi, pk, hi, cc), base + 2 * r, sibling),
                rc(blk(gwo, pk, ho, cc), blk(gwo, pk, ho, cc), base + 2 * r + 1, sibling)]

    def stage_in():
        return [pltpu.make_async_copy(nwi, vwi, loc_sems.at[0]), pltpu.make_async_copy(nwo, vwo, loc_sems.at[1])]

    def local():
        return [pltpu.make_async_copy(vwi, gwi.at[me_k], loc_sems.at[2]),
                pltpu.make_async_copy(vwo, gwo.at[me_k], loc_sems.at[3])]

    @pl.when(step == 0)
    def _():
        for cp in stage_in():
            cp.start()
        for r in range(3):
            for cp in ici(r):
                cp.start()

    @pl.when(step == 1)
    def _():
        for cp in stage_in():
            cp.wait()
        for cp in local():
            cp.start()

    @pl.when(step == (3 * nt) // 4)
    def _():
        for r in range(3):
            for got, fwd in zip(landed(r, c, 0), landed(r, c, fwd_sems)):
                got.wait_recv()
                fwd.start()

    @pl.when(step == nt - 1)
    def _():
        for r in range(3):
            for got in landed(r, 1 - c, fwd_sems):
                got.wait_recv()
        for r in range(3):
            for cp in ici(r) + landed(r, c, fwd_sems):
                cp.wait_send()
        for cp in local():
            cp.wait()


def _fwd_layer(larr, x, wi, bin_, caw, cbw, s256, seg, pw, wm, sb, wo, v1024, *, tile, nxt=None, target=None):
    assert nxt is None or target is None
    S = x.shape[0]
    T = tile
    nt = S // T
    alpha = float((2.0 * 4) ** 0.25)
    n_in = 13 + (2 if nxt is not None else 0) + (1 if target is not None else 0)
    n_out = 6 + (2 if nxt is not None else 0) + (1 if target is not None else 0)

    def body(*refs):
        l_ref = refs[0]
        (x_ref, wi_ref, bin_ref, caw_ref, cbw_ref, s256_ref, seg_ref, pw_ref, wm_ref, sb_ref, wo_ref,
         v1024_ref) = refs[1:13]
        y_ref, xb_ref, h_ref, aux_ref, mix_ref, z_ref = refs[n_in:n_in + 6]
        abuf, bbuf, cbuf, wm_s, shf = refs[n_in + n_out:n_in + n_out + 5]
        i = pl.program_id(0)
        if nxt is not None:
            _gather_next(i, nt, refs[13].at[l_ref[0] + 1], refs[14].at[l_ref[0] + 1], refs[n_in + 6], refs[n_in + 7],
                         *refs[n_in + n_out + 5:])

        @pl.when(i == 0)
        def _():
            abuf[0:HALO_A, :] = jnp.zeros((HALO_A, GROUP), F32)
            bbuf[0:HALO_B, :] = jnp.zeros((HALO_B, GROUP), F32)
            cbuf[0:HALO_C, :] = jnp.zeros((HALO_C, GROUP), F32)
            _sgu_masks(wm_ref, None, wm_s, None)

        x = x_ref[...]
        xb = x.astype(BF16)
        xb_ref[...] = xb
        for k in range(N_CHIPS):
            h_ref[:, COLS * k:COLS * (k + 1)] = _dot(xb, wi_ref[k]) + bin_ref[:, COLS * k:COLS * (k + 1)]

        def hs(j):
            return h_ref[:, GROUP * j:GROUP * (j + 1)]

        abuf[HALO_A:HALO_A + T, :] = hs(0) * _sig(hs(1))
        span = T + HALO_A - SUBLANES
        for p in range(1, SUBLANES):
            shf[p - 1, :, :] = abuf[p:p + span, :]
        for r0 in range(0, T, ROWS):
            acc = None
            for k in range(KA):
                off = HALO_A - (KA - 1) + k
                p, q8 = off % SUBLANES, off - off % SUBLANES
                win = abuf[r0 + q8:r0 + q8 + ROWS, :] if p == 0 else shf[p - 1, r0 + q8:r0 + q8 + ROWS, :]
                term = caw_ref[k:k + 1, :] * win
                acc = term if acc is None else acc + term
            aux_ref[r0:r0 + ROWS, 0:GROUP] = acc + s256_ref[0:1, :]
        abuf[0:HALO_A, :] = abuf[T:T + HALO_A, :]
        a1 = aux_ref[:, 0:GROUP]
        segm = seg_ref[...]
        cen = a1 - _segdot(a1, segm)
        var = _segdot(cen * cen, segm)
        a2 = cen * lax.rsqrt(var + LN_EPS) * s256_ref[1:2, :] + s256_ref[2:3, :]
        az = hs(2)
        mix_ref[:, 0:GROUP] = (a2 * _sig(a2) * (az * _sig(az))).astype(BF16)

        bbuf[HALO_B:HALO_B + T, :] = hs(4) * hs(5)
        for r0 in range(0, T, ROWS):
            acc = None
            for k in range(KB):
                off = HALO_B - (KB - 1) + k + r0
                term = cbw_ref[k:k + 1, :] * bbuf[off:off + ROWS, :]
                acc = term if acc is None else acc + term
            aux_ref[r0:r0 + ROWS, GROUP:2 * GROUP] = acc
        bbuf[0:HALO_B, :] = bbuf[T:T + HALO_B, :]
        bz = hs(6)
        mix_ref[:, GROUP:2 * GROUP] = (hs(3) * aux_ref[:, GROUP:2 * GROUP] * (bz * _sig(bz))).astype(BF16)

        ch = hs(7)
        cbuf[HALO_C:HALO_C + T, :] = ch
        hi_lane = (lax.broadcasted_iota(jnp.int32, (1, 128), 1) // HEAD) == 1
        for r0 in range(0, T, ROWS):
            def win(col, j0, j1):
                s = None
                for j in range(j0, j1):
                    off = HALO_C - j + r0
                    term = cbuf[off:off + ROWS, 128 * col:128 * (col + 1)]
                    s = term if s is None else s + term
                return s
            w0 = win(0, 0, 2) + jnp.where(hi_lane, win(0, 2, 4), 0.0)
            w1 = win(1, 0, 8) + jnp.where(hi_lane, win(1, 8, 16), 0.0)
            aux_ref[r0:r0 + ROWS, 2 * GROUP:2 * GROUP + 128] = w0
            aux_ref[r0:r0 + ROWS, 2 * GROUP + 128:3 * GROUP] = w1
        cbuf[0:HALO_C, :] = cbuf[T:T + HALO_C, :]
        pooled = aux_ref[:, 2 * GROUP:3 * GROUP] / _pool_cnt(i, T) - ch
        aux_ref[:, 2 * GROUP:3 * GROUP] = pooled
        q = _dot(pooled.astype(BF16), pw_ref[...])
        cz = hs(8)
        mix_ref[:, 2 * GROUP:3 * GROUP] = (q * s256_ref[3:4, :] * (cz * _sig(cz))).astype(BF16)

        dv = hs(10)
        cen = dv - _rowmean(dv)
        var = _rowmean(cen * cen)
        v = cen * lax.rsqrt(var + LN_EPS) * s256_ref[4:5, :] + s256_ref[5:6, :]
        sps = []
        for n in range(T // SGU_BLOCK):
            vb = v[n * SGU_BLOCK:(n + 1) * SGU_BLOCK, :]
            sps.append(_dot(wm_s[...], _vstack(vb)) + sb_ref[...])
        sp = jnp.concatenate(sps, axis=0)
        dz = hs(11)
        mix_ref[:, 3 * GROUP:4 * GROUP] = (hs(9) * sp * (dz * _sig(dz))).astype(BF16)

        out = v1024_ref[0:1, :]
        for k in range(N_CHIPS):
            out = out + _dot(mix_ref[:, GROUP * k:GROUP * (k + 1)], wo_ref[k])
        z = alpha * x + out
        z_ref[...] = z
        cen = z - _rowmean(z)
        var = _rowmean(cen * cen)
        y = cen * lax.rsqrt(var + LN_EPS) * v1024_ref[1:2, :] + v1024_ref[2:3, :]
        if target is None:
            y_ref[...] = y
        else:
            t_ref, loss_ref = refs[13], refs[n_in + 6]

            @pl.when(i == 0)
            def _():
                loss_ref[...] = jnp.zeros_like(loss_ref)
            err = y - t_ref[...]
            y_ref[...] = err * (1.0 / D_MODEL)
            loss_ref[...] += jnp.sum(_colsum(err * err), axis=1, keepdims=True) * (0.5 / D_MODEL)

    def rows(width):
        return pl.BlockSpec((T, width), lambda i, l: (i, 0))

    consts = (wi, bin_, caw, cbw, s256, seg, pw, wm, sb, wo, v1024)
    in_specs = [rows(D_MODEL)] + [_whole(a) if a is wi or a is seg or a is wo else _of_layer(a) for a in consts]
    out_specs = [rows(D_MODEL), rows(D_MODEL), rows(IN_WIDTH), rows(3 * GROUP), rows(D_MODEL), rows(D_MODEL)]
    out_shape = [jax.ShapeDtypeStruct((S, D_MODEL), F32), jax.ShapeDtypeStruct((S, D_MODEL), BF16),
                 jax.ShapeDtypeStruct((S, IN_WIDTH), F32), jax.ShapeDtypeStruct((S, 3 * GROUP), F32),
                 jax.ShapeDtypeStruct((S, D_MODEL), BF16), jax.ShapeDtypeStruct((S, D_MODEL), F32)]
    scratch = [pltpu.VMEM((T + HALO_A, GROUP), F32), pltpu.VMEM((T + HALO_B, GROUP), F32),
               pltpu.VMEM((T + HALO_C, GROUP), F32), pltpu.VMEM((SGU_BLOCK, 4 * SGU_BLOCK), BF16),
               pltpu.VMEM((SUBLANES - 1, T + HALO_A - SUBLANES, GROUP), F32)]
    extra = ()
    if nxt is not None:
        extra = tuple(nxt)
        in_specs += [ANY, ANY]
        out_specs += [ANY, ANY]
        out_shape += [jax.ShapeDtypeStruct((N_CHIPS, D_MODEL, COLS), BF16),
                      jax.ShapeDtypeStruct((N_CHIPS, GROUP, D_MODEL), BF16)]
        scratch += [pltpu.SemaphoreType.DMA((N_GATHER_SEMS,)), pltpu.SemaphoreType.DMA((N_GATHER_SEMS,)),
                    pltpu.SemaphoreType.DMA((4,)), pltpu.VMEM((D_MODEL, COLS), BF16), pltpu.VMEM((GROUP, D_MODEL), BF16)]
    if target is not None:
        extra = (target,)
        in_specs += [rows(D_MODEL)]
        out_specs += [pl.BlockSpec((8, 128), lambda i, l: (0, 0))]
        out_shape += [jax.ShapeDtypeStruct((8, 128), F32)]
    grid_spec = pltpu.PrefetchScalarGridSpec(num_scalar_prefetch=1, grid=(nt,), in_specs=in_specs,
                                             out_specs=out_specs, scratch_shapes=scratch)
    return pl.pallas_call(
        body, name=("fwd_layer_loss" if target is not None else "fwd_layer") if nxt is None else "fwd_layer_gather",
        grid_spec=grid_spec, out_shape=out_shape,
        compiler_params=_vmem_params(dimension_semantics=("arbitrary",), has_side_effects=nxt is not None),
    )(larr, x, *consts, *extra)


ROW_CBW = 8
ROW_CAW = 16
ROW_LOSS = 7
ROW_PW = 48
ROW_LNG = 112
ROW_LNB = 116
ROW_BOUT = 120
ROW_BIN = 124
ROW_WC = 136
ROW_SB = 392
SM_ROWS = 400
N_DEV = 8


def _exchange_comm(start, finish, l, p_i, p_o, sm, r_i, r_o, r_sm, send_sems, recv_sems, loc_sem):
    x, y, c = _place()
    me = 4 * x + 2 * y + c
    chips = _other_chips(x, y)

    def rc(src, dst, sem, to):
        return pltpu.make_async_remote_copy(src_ref=src, dst_ref=dst, send_sem=send_sems.at[sem],
                                            recv_sem=recv_sems.at[sem], device_id=to, device_id_type=MESH)

    def big(r):
        px, py, pk = chips[r]
        to = (px, py, c)
        return [rc(p_i.at[l, pk], r_i.at[r, l], 2 * r, to), rc(p_o.at[l, pk], r_o.at[r, l], 2 * r + 1, to)]

    def peer(rel):
        px = 1 - x if rel & 4 else x
        py = 1 - y if rel & 2 else y
        pc = 1 - c if rel & 1 else c
        return (px, py, pc), 4 * px + 2 * py + pc

    def small_out(rel):
        to, _ = peer(rel)
        return rc(sm, r_sm.at[me], N_EXCH_SEMS - N_DEV + rel, to)

    def small_in(rel):
        to, idx = peer(rel)
        return rc(sm, r_sm.at[idx], N_EXCH_SEMS - N_DEV + rel, to)

    def local():
        return pltpu.make_async_copy(sm, r_sm.at[me], loc_sem.at[0])

    with_big, with_small = p_i is not None, sm is not None

    @pl.when(start)
    def _():
        if with_small:
            local().start()
        if with_big:
            for r in range(3):
                for cp in big(r):
                    cp.start()
        if with_small:
            for rel in range(1, N_DEV):
                small_out(rel).start()

    @pl.when(finish)
    def _():
        if with_big:
            for r in range(3):
                for cp in big(r):
                    cp.wait()
        if with_small:
            for rel in range(1, N_DEV):
                small_in(rel).wait_recv()
                small_out(rel).wait_send()
            local().wait()


RC = 32
RC_WIDE = 16
ACC_ROWS = 136


def _rsum8(v):
    r = v[0:8]
    for j in range(1, v.shape[0] // 8):
        r = r + v[8 * j:8 * j + 8]
    return r


def _bwd_layer(larr, dy, z, h, aux, wi, caw, cbw, s256, seg, pw, wm, wmt, sb, wo, v1024, e4, *, tile, exch=None):
    S = dy.shape[0]
    T = tile
    nt = S // T
    nblk = T // SGU_BLOCK
    alpha = float((2.0 * 4) ** 0.25)
    n_in = 17 + (5 if exch is not None else 0)
    n_out = 4 + (3 if exch is not None else 0)
    slab = pltpu.VMEM((T, GROUP), F32)
    scratch = dict(
        dbuf=pltpu.VMEM((T + HALO_A, GROUP), F32), ebuf=pltpu.VMEM((T + HALO_B, GROUP), F32),
        fbuf=pltpu.VMEM((T + HALO_C, GROUP), F32), sh=pltpu.VMEM((SUBLANES - 1, T + HALO_A - SUBLANES, GROUP), F32),
        wm_s=pltpu.VMEM((SGU_BLOCK, 4 * SGU_BLOCK), BF16), wmt_s=pltpu.VMEM((4 * SGU_BLOCK, SGU_BLOCK), BF16),
        dsp_acc=pltpu.VMEM((SGU_BLOCK, GROUP), F32), pw_acc=pltpu.VMEM((GROUP, GROUP), F32),
        acc_s=pltpu.VMEM((8 * ACC_ROWS, GROUP), F32), acc_w=pltpu.VMEM((24, D_MODEL), F32),
        dmix_s=pltpu.VMEM((T, D_MODEL), F32), vst_s=pltpu.VMEM((nblk, 4 * SGU_BLOCK, GROUP), BF16),
        dq_s=pltpu.VMEM((T, GROUP), BF16), dxt_s=pltpu.VMEM((D_MODEL, T), F32),
        mean_s=slab, t1_s=slab, t2_s=slab, q_s=slab, xv_s=slab, rv_s=slab, v_s=slab, sp_s=slab, a0_s=slab, sg_s=slab,
        xh_s=slab, ra_s=slab, ub_s=slab, dsp_s=slab, m1_s=slab, m2_s=slab, dpool_s=slab, dvd_s=slab, u_s=slab,
        du_s=slab, cw_s=slab)
    names = list(scratch)

    def body(*refs):
        (dy_ref, z_ref, h_ref, aux_ref, wi_ref, caw_ref, cbw_ref, s256_ref, seg_ref, pw_ref, wm_ref, wmt_ref,
         sb_ref, wo_ref, v1024_ref, e4_ref) = refs[1:17]
        dx_ref, dhb_ref, dzb_ref, osm_ref = refs[n_in:n_in + 4]
        k0 = n_in + n_out
        sc = dict(zip(names, refs[k0:k0 + len(names)]))
        dbuf, ebuf, fbuf, sh = sc["dbuf"], sc["ebuf"], sc["fbuf"], sc["sh"]
        wm_s, wmt_s, dsp_acc, pw_acc, acc_s, acc_w = (sc[n] for n in ("wm_s", "wmt_s", "dsp_acc", "pw_acc", "acc_s",
                                                                        "acc_w"))
        dmix_s, vst_s, dq_s = sc["dmix_s"], sc["vst_s"], sc["dq_s"]
        i = pl.program_id(0)
        tile_idx = nt - 1 - i
        if exch is not None:
            p_i, p_o, sm = refs[17:20]
            r_i, r_o, r_sm = refs[n_in + 4:n_in + 7]
            _exchange_comm(i == 0, i == nt - 1, refs[0][0] + 1, p_i, p_o, sm, r_i, r_o, r_sm, *refs[k0 + len(names):])

        @pl.when(i == 0)
        def _():
            dbuf[T:T + HALO_A, :] = jnp.zeros((HALO_A, GROUP), F32)
            ebuf[T:T + HALO_B, :] = jnp.zeros((HALO_B, GROUP), F32)
            fbuf[T:T + HALO_C, :] = jnp.zeros((HALO_C, GROUP), F32)
            _sgu_masks(wm_ref, wmt_ref, wm_s, wmt_s)
            osm_ref[...] = jnp.zeros_like(osm_ref)
            dsp_acc[...] = jnp.zeros_like(dsp_acc)
            pw_acc[...] = jnp.zeros_like(pw_acc)
            acc_s[...] = jnp.zeros_like(acc_s)
            acc_w[...] = jnp.zeros_like(acc_w)

        def chunks(rc, fn):
            for c in range(T // rc):
                fn(pl.ds(c * rc, rc))

        def hs(j, rows):
            return h_ref[rows, GROUP * j:GROUP * (j + 1)]

        def acc_add(row, val):
            acc_s[8 * row:8 * row + 8, :] += _rsum8(val)

        def put_dh(j, rows, val):
            acc_add(ROW_BIN + j, val)
            dhb_ref[rows, GROUP * j:GROUP * (j + 1)] = val.astype(BF16)

        def dsilu(v, s):
            return s * (1.0 + v * (1.0 - s))

        def vec(r):
            return s256_ref[r:r + 1, :]

        def ln_bwd(rows):
            dyc = dy_ref[rows, :]
            zc = z_ref[rows, :]
            cen = zc - _rowmean(zc)
            rstd = lax.rsqrt(_rowmean(cen * cen) + LN_EPS)
            xhat = cen * rstd
            acc_w[0:8, :] += _rsum8(dyc * xhat)
            acc_w[8:16, :] += _rsum8(dyc)
            gdy = dyc * v1024_ref[1:2, :]
            dz = rstd * (gdy - _rowmean(gdy) - xhat * _rowmean(gdy * xhat))
            acc_w[16:24, :] += _rsum8(dz)
            dzb_ref[rows, :] = dz.astype(BF16)
            dx_ref[rows, :] = alpha * dz
        chunks(RC_WIDE, ln_bwd)

        segm = seg_ref[...]
        dzb = dzb_ref[...]
        for k in range(N_CHIPS):
            dmix_s[:, GROUP * k:GROUP * (k + 1)] = _dot_nt(dzb, wo_ref[k])
        sc["mean_s"][...] = _segdot(aux_ref[:, 0:GROUP], segm)
        pooled_b = aux_ref[:, 2 * GROUP:3 * GROUP].astype(BF16)
        sc["q_s"][...] = _dot(pooled_b, pw_ref[...])

        def centre(rows):
            cen = aux_ref[rows, 0:GROUP] - sc["mean_s"][rows, :]
            sc["t1_s"][rows, :] = cen * cen
            dv_in = hs(10, rows)
            cen_v = dv_in - _rowmean(dv_in)
            rstd_v = lax.rsqrt(_rowmean(cen_v * cen_v) + LN_EPS)
            xv = cen_v * rstd_v
            sc["xv_s"][rows, :] = xv
            sc["rv_s"][rows, :] = jnp.broadcast_to(rstd_v, xv.shape)
            sc["v_s"][rows, :] = xv * vec(4) + vec(5)
        chunks(RC, centre)

        sc["t2_s"][...] = _segdot(sc["t1_s"][...], segm)
        for n in range(nblk):
            blk = slice(n * SGU_BLOCK, (n + 1) * SGU_BLOCK)
            vst_s[n] = _vstack(sc["v_s"][blk, :])
            sc["sp_s"][blk, :] = _dot(wm_s[...], vst_s[n]) + sb_ref[...]

        def mixers(rows):
            a_val, a_glu, a_z = hs(0, rows), hs(1, rows), hs(2, rows)
            sg = _sig(a_glu)
            sc["a0_s"][rows, :] = a_val * sg
            sc["sg_s"][rows, :] = sg
            rstd_a = lax.rsqrt(sc["t2_s"][rows, :] + LN_EPS)
            xh = (aux_ref[rows, 0:GROUP] - sc["mean_s"][rows, :]) * rstd_a
            a2 = xh * vec(1) + vec(2)
            s2 = _sig(a2)
            sz = _sig(a_z)
            dya = dmix_s[rows, 0:GROUP]
            put_dh(2, rows, dya * (a2 * s2) * dsilu(a_z, sz))
            d_a2 = dya * (a_z * sz) * dsilu(a2, s2)
            acc_add(1, d_a2 * xh)
            acc_add(2, d_a2)
            gd = d_a2 * vec(1)
            sc["t1_s"][rows, :] = gd
            sc["t2_s"][rows, :] = gd * xh
            sc["xh_s"][rows, :] = xh
            sc["ra_s"][rows, :] = rstd_a
            b_b, b_c, b_h, b_z = hs(3, rows), hs(4, rows), hs(5, rows), hs(6, rows)
            cb = aux_ref[rows, GROUP:2 * GROUP]
            sz = _sig(b_z)
            dyb = dmix_s[rows, GROUP:2 * GROUP]
            put_dh(3, rows, dyb * cb * (b_z * sz))
            put_dh(6, rows, dyb * b_b * cb * dsilu(b_z, sz))
            ebuf[rows, :] = dyb * b_b * (b_z * sz)
            sc["ub_s"][rows, :] = b_c * b_h
            c_z = hs(8, rows)
            q = sc["q_s"][rows, :]
            sz = _sig(c_z)
            dyc = dmix_s[rows, 2 * GROUP:3 * GROUP]
            acc_add(3, dyc * q * (c_z * sz))
            put_dh(8, rows, dyc * q * vec(3) * dsilu(c_z, sz))
            dq_s[rows, :] = (dyc * vec(3) * (c_z * sz)).astype(BF16)
            d_u, d_z = hs(9, rows), hs(11, rows)
            sp = sc["sp_s"][rows, :]
            sz = _sig(d_z)
            dyd = dmix_s[rows, 3 * GROUP:4 * GROUP]
            put_dh(9, rows, dyd * sp * (d_z * sz))
            put_dh(11, rows, dyd * d_u * sp * dsilu(d_z, sz))
            sc["dsp_s"][rows, :] = dyd * d_u * (d_z * sz)
        chunks(RC, mixers)

        sc["m1_s"][...] = _segdot(sc["t1_s"][...], segm)
        sc["m2_s"][...] = _segdot(sc["t2_s"][...], segm)
        d_q = dq_s[...]
        pw_acc[...] += _dot_tn(pooled_b, d_q)
        sc["dpool_s"][...] = _dot_nt(d_q, pw_ref[...])
        grp = _lane_group(GROUP)
        for n in range(nblk):
            blk = slice(n * SGU_BLOCK, (n + 1) * SGU_BLOCK)
            dspb = sc["dsp_s"][blk, :]
            dsp_acc[...] += dspb
            dspb16 = dspb.astype(BF16)
            dvst = _dot(wmt_s[...], dspb16)
            dvb = None
            for hh in range(4):
                part = jnp.where(grp == hh, dvst[hh * SGU_BLOCK:(hh + 1) * SGU_BLOCK, :], 0.0)
                dvb = part if dvb is None else dvb + part
            sc["dvd_s"][blk, :] = dvb
            dwc = _dot_nt(dspb16, vst_s[n])
            osm_ref[ROW_WC:ROW_WC + SGU_BLOCK, :] += dwc[:, 0:GROUP]
            osm_ref[ROW_WC + SGU_BLOCK:ROW_WC + 2 * SGU_BLOCK, :] += dwc[:, GROUP:2 * GROUP]

        def ln_sums(rows):
            xh = sc["xh_s"][rows, :]
            d_a1 = sc["ra_s"][rows, :] * (sc["t1_s"][rows, :] - sc["m1_s"][rows, :] - xh * sc["m2_s"][rows, :])
            acc_add(0, d_a1)
            dbuf[rows, :] = d_a1
            pos = tile_idx * T + rows.start + lax.broadcasted_iota(jnp.int32, (RC, GROUP), 0) + 1
            lane = lax.broadcasted_iota(jnp.int32, (RC, GROUP), 1) // HEAD
            win = jnp.where(lane == 0, 2, jnp.where(lane == 1, 4, jnp.where(lane == 2, 8, 16)))
            fbuf[rows, :] = sc["dpool_s"][rows, :] / jnp.minimum(pos, win).astype(F32)
            d_v = sc["dvd_s"][rows, :]
            xv = sc["xv_s"][rows, :]
            acc_add(4, d_v * xv)
            acc_add(5, d_v)
            gd = d_v * vec(4)
            put_dh(10, rows, sc["rv_s"][rows, :] * (gd - _rowmean(gd) - xv * _rowmean(gd * xv)))
        chunks(RC, ln_sums)

        span = T + HALO_A - SUBLANES
        for p in range(1, SUBLANES):
            sh[p - 1, :, :] = dbuf[p:p + span, :]

        for r0 in range(0, T, ROWS):
            uc = sc["ub_s"][r0:r0 + ROWS, :]
            acc = None
            for k in range(KB):
                off = (KB - 1) - k + r0
                w = ebuf[off:off + ROWS, :]
                term = cbw_ref[k:k + 1, :] * w
                acc = term if acc is None else acc + term
                acc_add(ROW_CBW + k, uc * w)
            sc["du_s"][r0:r0 + ROWS, :] = acc
        ebuf[T:T + HALO_B, :] = ebuf[0:HALO_B, :]

        hi_lane = (lax.broadcasted_iota(jnp.int32, (1, 128), 1) // HEAD) == 1
        for r0 in range(0, T, ROWS):
            def win(col, j0, j1):
                s = None
                for j in range(j0, j1):
                    term = fbuf[r0 + j:r0 + j + ROWS, 128 * col:128 * (col + 1)]
                    s = term if s is None else s + term
                return s
            sc["cw_s"][r0:r0 + ROWS, 0:128] = win(0, 0, 2) + jnp.where(hi_lane, win(0, 2, 4), 0.0)
            sc["cw_s"][r0:r0 + ROWS, 128:256] = win(1, 0, 8) + jnp.where(hi_lane, win(1, 8, 16), 0.0)
        fbuf[T:T + HALO_C, :] = fbuf[0:HALO_C, :]

        def rest_bc(rows):
            d_u = sc["du_s"][rows, :]
            put_dh(4, rows, d_u * hs(5, rows))
            put_dh(5, rows, d_u * hs(4, rows))
            put_dh(7, rows, sc["cw_s"][rows, :] - sc["dpool_s"][rows, :])
        chunks(RC, rest_bc)

        dxt_s = sc["dxt_s"]

        def dx_term(k):
            term = _dot_nt(wi_ref[k], dhb_ref[:, COLS * k:COLS * (k + 1)])
            if k == 1:
                dxt_s[...] = term
            else:
                dxt_s[...] += term

        def conv_a(rows):
            a0c = sc["a0_s"][rows, :]
            acc = None
            for k in range(KA):
                off = (KA - 1) - k
                p, q8 = off % SUBLANES, off - off % SUBLANES
                w = dbuf[pl.ds(rows.start + q8, RC), :] if p == 0 else sh[p - 1, pl.ds(rows.start + q8, RC), :]
                term = caw_ref[k:k + 1, :] * w
                acc = term if acc is None else acc + term
                acc_add(ROW_CAW + k, a0c * w)
            sc["u_s"][rows, :] = acc
        n_chunks = T // RC
        after = {(n_chunks * j) // 3: j + 1 for j in range(3)}
        for c in range(n_chunks):
            conv_a(pl.ds(c * RC, RC))
            if c in after:
                dx_term(after[c])
        dbuf[T:T + HALO_A, :] = dbuf[0:HALO_A, :]

        def rest_a(rows):
            d_a0 = sc["u_s"][rows, :]
            sg = sc["sg_s"][rows, :]
            put_dh(0, rows, d_a0 * sg)
            put_dh(1, rows, d_a0 * hs(0, rows) * sg * (1.0 - sg))
        chunks(RC, rest_a)
        dx_term(0)
        dx_ref[...] += dxt_s[...].T

        @pl.when(i == nt - 1)
        def _():
            for row in list(range(6)) + list(range(ROW_CBW, ROW_CBW + KB)) + list(range(ROW_CAW, ROW_CAW + KA)) + list(
                    range(ROW_BIN, ROW_BIN + N_SLICES)):
                osm_ref[row:row + 1, :] = _colsum(acc_s[8 * row:8 * row + 8, :])
            for j, row in enumerate((ROW_LNG, ROW_LNB, ROW_BOUT)):
                cs = _colsum(acc_w[8 * j:8 * j + 8, :])
                for q in range(D_MODEL // GROUP):
                    osm_ref[row + q:row + q + 1, :] = cs[:, GROUP * q:GROUP * (q + 1)]
            r = lax.broadcasted_iota(jnp.int32, (SGU_BLOCK, GROUP), 0) // CHUNK
            c = (lax.broadcasted_iota(jnp.int32, (SGU_BLOCK, GROUP), 1) % SGU_BLOCK) // CHUNK
            for half in range(2):
                rows_ = slice(ROW_WC + half * SGU_BLOCK, ROW_WC + (half + 1) * SGU_BLOCK)
                osm_ref[rows_, :] = jnp.where(c <= r, osm_ref[rows_, :], 0.0)
            sb_t = _segdot(dsp_acc[...], e4_ref[...]).T
            osm_ref[ROW_SB:ROW_SB + 8, 0:SGU_BLOCK] = sb_t[0:8, :]
            for g in range(4):
                osm_ref[ROW_PW:ROW_PW + HEAD, HEAD * g:HEAD * (g + 1)] = (
                    pw_acc[HEAD * g:HEAD * (g + 1), HEAD * g:HEAD * (g + 1)])

    def rows(width):
        return pl.BlockSpec((T, width), lambda i, l: (nt - 1 - i, 0))

    consts = (wi, caw, cbw, s256, seg, pw, wm, wmt, sb, wo, v1024, e4)
    unstacked = (wi, seg, wo, e4)
    in_specs = [rows(D_MODEL), rows(D_MODEL), rows(IN_WIDTH), rows(3 * GROUP)] + [
        _whole(a) if any(a is u for u in unstacked) else _of_layer(a) for a in consts]
    out_specs = [rows(D_MODEL), rows(IN_WIDTH), rows(D_MODEL), pl.BlockSpec((SM_ROWS, GROUP), lambda i, l: (0, 0))]
    out_shape = [jax.ShapeDtypeStruct((S, D_MODEL), F32), jax.ShapeDtypeStruct((S, IN_WIDTH), BF16),
                 jax.ShapeDtypeStruct((S, D_MODEL), BF16), jax.ShapeDtypeStruct((SM_ROWS, GROUP), F32)]
    scratch_shapes = list(scratch.values())
    extra, aliases = (), {}
    if exch is not None:
        extra = tuple(exch)
        r_i, r_o = exch[3], exch[4]
        in_specs += [ANY] * 5
        out_specs += [ANY] * 3
        out_shape += [jax.ShapeDtypeStruct(r_i.shape, r_i.dtype), jax.ShapeDtypeStruct(r_o.shape, r_o.dtype),
                      jax.ShapeDtypeStruct((N_DEV, SM_ROWS, GROUP), F32)]
        scratch_shapes += [pltpu.SemaphoreType.DMA((N_EXCH_SEMS,)), pltpu.SemaphoreType.DMA((N_EXCH_SEMS,)),
                           pltpu.SemaphoreType.DMA((1,))]
        aliases = {20: 4, 21: 5}
    grid_spec = pltpu.PrefetchScalarGridSpec(num_scalar_prefetch=1, grid=(nt,), in_specs=in_specs,
                                             out_specs=out_specs, scratch_shapes=scratch_shapes)
    return pl.pallas_call(
        body, name="bwd_layer" if exch is None else "bwd_layer_exchange",
        grid_spec=grid_spec, out_shape=out_shape, input_output_aliases=aliases,
        compiler_params=_vmem_params(dimension_semantics=("arbitrary",), has_side_effects=exch is not None),
    )(larr, dy, z, h, aux, *consts, *extra)


def _dw(layer, xb, dhb, mixb, dzb, gwi, gwi16, gwo, gwo16, *, k_steps, small=None):
    S = xb.shape[0]
    tk = S // k_steps
    n_steps = N_CHIPS + k_steps

    def body(*refs):
        x_ref, dh_ref, mix_ref, dz_ref = refs[1:5]
        oi_ref, oi16_ref, oo_ref, oo16_ref = refs[n_in:n_in + 4]
        j = pl.program_id(0)
        if small is not None:
            _exchange_comm(j == 0, j == n_steps - 1, None, None, None, refs[9], None, None, refs[n_in + 4],
                           *refs[n_in + 5:])

        @pl.when(j < N_CHIPS)
        def _():
            acc = _dot_tn(x_ref[...], dh_ref[...])
            oi_ref[...] = acc
            oi16_ref[...] = acc.astype(BF16)

        @pl.when(j == N_CHIPS)
        def _():
            oo_ref[...] = jnp.zeros_like(oo_ref)

        @pl.when(j >= N_CHIPS)
        def _():
            oo_ref[...] += _dot_tn(mix_ref[...], dz_ref[...]).reshape(N_CHIPS, GROUP, D_MODEL)

        @pl.when(j == n_steps - 1)
        def _():
            oo16_ref[...] = oo_ref[...].astype(BF16)

    def col_block(j, l):
        return jnp.minimum(j, N_CHIPS - 1)

    def tok_block(j, l):
        return jnp.maximum(j - N_CHIPS, 0)

    oi_spec = pl.BlockSpec((None, None, D_MODEL, COLS), lambda j, l: (l[0], col_block(j, l), 0, 0))
    oo_spec = pl.BlockSpec((None, N_CHIPS, GROUP, D_MODEL), lambda j, l: (l[0], 0, 0, 0))
    in_specs = [pl.BlockSpec((S, D_MODEL), lambda j, l: (0, 0)),
                pl.BlockSpec((S, COLS), lambda j, l: (0, col_block(j, l))),
                pl.BlockSpec((tk, D_MODEL), lambda j, l: (tok_block(j, l), 0)),
                pl.BlockSpec((tk, D_MODEL), lambda j, l: (tok_block(j, l), 0)), ANY, ANY, ANY, ANY]
    out_specs = [oi_spec, oi_spec, oo_spec, oo_spec]
    out_shape = [jax.ShapeDtypeStruct(gwi.shape, F32), jax.ShapeDtypeStruct(gwi.shape, BF16),
                 jax.ShapeDtypeStruct(gwo.shape, F32), jax.ShapeDtypeStruct(gwo.shape, BF16)]
    scratch, extra = [], ()
    if small is not None:
        extra = (small,)
        in_specs += [ANY]
        out_specs += [ANY]
        out_shape += [jax.ShapeDtypeStruct((N_DEV, SM_ROWS, GROUP), F32)]
        scratch = [pltpu.SemaphoreType.DMA((N_EXCH_SEMS,)), pltpu.SemaphoreType.DMA((N_EXCH_SEMS,)), pltpu.SemaphoreType.DMA((1,))]
    n_in = 9 + len(extra)
    grid_spec = pltpu.PrefetchScalarGridSpec(
        num_scalar_prefetch=1, grid=(n_steps,), in_specs=in_specs, out_specs=out_specs, scratch_shapes=scratch)
    return pl.pallas_call(
        body, name="dw" if small is None else "dw_exchange", grid_spec=grid_spec, out_shape=out_shape,
        input_output_aliases={5: 0, 6: 1, 7: 2, 8: 3},
        compiler_params=_vmem_params(dimension_semantics=("arbitrary",), has_side_effects=small is not None),
    )(layer, xb, dhb, mixb, dzb, gwi, gwi16, gwo, gwo16, *extra)


def _adamw_math(w, g, m, v):
    nm = ADAM_B1 * m + (1.0 - ADAM_B1) * g
    nv = ADAM_B2 * v + (1.0 - ADAM_B2) * (g * g)
    c1 = 1.0 - ADAM_B1 ** ADAM_STEP
    c2 = 1.0 - ADAM_B2 ** ADAM_STEP
    return -ADAM_LR * ((nm / c1) / (jnp.sqrt(nv / c2) + ADAM_EPS) + ADAM_WD * w), nm, nv


def _adamw_small(ws, gs, ms, vs):
    n = len(ws)

    def body(*refs):
        for j in range(n):
            d, nm, nv = _adamw_math(*(refs[k * n + j][...] for k in range(4)))
            refs[4 * n + j][...] = d
            refs[5 * n + j][...] = nm
            refs[6 * n + j][...] = nv

    shapes = [jax.ShapeDtypeStruct(w.shape, F32) for w in ws]
    outs = pl.pallas_call(body, name="adamw_small", out_shape=shapes * 3, compiler_params=_vmem_params())(
        *ws, *gs, *ms, *vs)
    return outs[0:n], outs[n:2 * n], outs[2 * n:3 * n]


def _adamw(w, g, m, v, *, rows_per_step, name, copy_g=False):
    R, C = w.shape
    tr = rows_per_step

    def body(w_ref, g_ref, m_ref, v_ref, d_ref, nm_ref, nv_ref, *g_out):
        g_ = g_ref[...]
        d_ref[...], nm_ref[...], nv_ref[...] = _adamw_math(w_ref[...], g_, m_ref[...], v_ref[...])
        if copy_g:
            g_out[0][...] = g_

    spec = pl.BlockSpec((tr, C), lambda i: (i, 0))
    n_out = 4 if copy_g else 3
    return pl.pallas_call(
        body, name=name, grid=(R // tr,),
        in_specs=[spec] * 4, out_specs=[spec] * n_out,
        out_shape=[jax.ShapeDtypeStruct((R, C), F32)] * n_out,
        compiler_params=_vmem_params(dimension_semantics=("arbitrary",)),
    )(w, g, m, v)


def _gather_weights(wi16, wo16, cw):
    L = wi16.shape[0]
    hi_rows, ho_rows = D_MODEL // 2, GROUP // 2
    n_ici = 2 * L + 1
    n_fwd = 2 * L

    def body(wi_ref, wo_ref, cw_ref, *rest):
        wig = rest[0:L]
        wog = rest[L:2 * L]
        cwg = rest[2 * L]
        send_sems, recv_sems, loc_sems, vwi, vwo, vcw = rest[2 * L + 1:]
        x, y, c = _place()
        me_k = 2 * x + y
        sibling = (x, y, 1 - c)
        chips = _other_chips(x, y)

        def half_i(ref, blk):
            return ref.at[blk, pl.ds(c * hi_rows, hi_rows), :]

        def half_o(ref, blk):
            return ref.at[blk, pl.ds(c * ho_rows, ho_rows), :]

        def other_half_i(ref, blk):
            return ref.at[blk, pl.ds((1 - c) * hi_rows, hi_rows), :]

        def other_half_o(ref, blk):
            return ref.at[blk, pl.ds((1 - c) * ho_rows, ho_rows), :]

        stage_in = [pltpu.make_async_copy(wi_ref, vwi, loc_sems.at[0]), pltpu.make_async_copy(wo_ref, vwo, loc_sems.at[1]),
                    pltpu.make_async_copy(cw_ref, vcw, loc_sems.at[2])]
        local = []
        for l in range(L):
            local.append(pltpu.make_async_copy(vwi.at[l], wig[l].at[me_k], loc_sems.at[3 + 2 * l]))
            local.append(pltpu.make_async_copy(vwo.at[l], wog[l].at[me_k], loc_sems.at[3 + 2 * l + 1]))
        local.append(pltpu.make_async_copy(vcw, cwg.at[me_k], loc_sems.at[3 + 2 * L]))
        for cp in stage_in:
            cp.start()

        def remote(src, dst, sem, to):
            return pltpu.make_async_remote_copy(src_ref=src, dst_ref=dst, send_sem=send_sems.at[sem],
                                                recv_sem=recv_sems.at[sem], device_id=to, device_id_type=MESH)

        sends = []
        for r, (px, py, _) in enumerate(chips):
            to = (px, py, c)
            for l in range(L):
                sends.append(remote(half_i(wi_ref, l), half_i(wig[l], me_k), r * n_ici + 2 * l, to))
                sends.append(remote(half_o(wo_ref, l), half_o(wog[l], me_k), r * n_ici + 2 * l + 1, to))
            sends.append(remote(cw_ref, cwg.at[me_k], r * n_ici + 2 * L, to))
        for cp in sends:
            cp.start()
        for cp in stage_in:
            cp.wait()
        for cp in local:
            cp.start()

        base = 3 * n_ici
        fwds = []
        for r, (px, py, pk) in enumerate(chips):
            for l in range(L):
                remote(half_i(wig[l], pk), half_i(wig[l], pk), r * n_ici + 2 * l, sibling).wait_recv()
                f = remote(half_i(wig[l], pk), half_i(wig[l], pk), base + r * n_fwd + 2 * l, sibling)
                f.start()
                fwds.append(f)
                remote(half_o(wog[l], pk), half_o(wog[l], pk), r * n_ici + 2 * l + 1, sibling).wait_recv()
                f = remote(half_o(wog[l], pk), half_o(wog[l], pk), base + r * n_fwd + 2 * l + 1, sibling)
                f.start()
                fwds.append(f)
            remote(cwg.at[pk], cwg.at[pk], r * n_ici + 2 * L, sibling).wait_recv()
        for r, (px, py, pk) in enumerate(chips):
            for l in range(L):
                remote(other_half_i(wig[l], pk), other_half_i(wig[l], pk), base + r * n_fwd + 2 * l, sibling).wait_recv()
                remote(other_half_o(wog[l], pk), other_half_o(wog[l], pk), base + r * n_fwd + 2 * l + 1, sibling).wait_recv()
        for cp in sends + fwds:
            cp.wait_send()
        for cp in local:
            cp.wait()

    n_sem = 3 * n_ici + 3 * n_fwd
    out_shape = ([jax.ShapeDtypeStruct((N_CHIPS, D_MODEL, COLS), BF16)] * L
                 + [jax.ShapeDtypeStruct((N_CHIPS, GROUP, D_MODEL), BF16)] * L
                 + [jax.ShapeDtypeStruct((N_CHIPS,) + cw.shape, F32)])
    outs = pl.pallas_call(
        body, name="gather_weights",
        in_specs=[ANY, ANY, ANY], out_specs=[ANY] * (2 * L + 1), out_shape=out_shape,
        scratch_shapes=[pltpu.SemaphoreType.DMA((n_sem,)), pltpu.SemaphoreType.DMA((n_sem,)),
                        pltpu.SemaphoreType.DMA((2 * L + 4,)), pltpu.VMEM(wi16.shape, BF16), pltpu.VMEM(wo16.shape, BF16),
                        pltpu.VMEM(cw.shape, F32)],
        compiler_params=_vmem_params(has_side_effects=True),
    )(wi16, wo16, cw)
    return outs[0:L], outs[L:2 * L], outs[2 * L]


def _swap_add(cl_arr, g_i, g16_i, p_i, g_o, g16_o, p_o):
    hi, ho = p_i.shape[2], p_o.shape[2]

    def body(cl_ref, gi_ref, gi16_ref, pi_in, go_ref, go16_ref, po_in, oi_ref, oo_ref, ri_v, ro_v, send_sems,
             recv_sems):
        del pi_in, po_in
        k = pl.program_id(0)
        x, y, c = _place()
        l = cl_ref[1]

        def copies(kk):
            pair = ((gi16_ref, hi, ri_v), (go16_ref, ho, ro_v))
            return [pltpu.make_async_remote_copy(
                src_ref=src.at[l, kk, pl.ds((1 - c) * n, n), :], dst_ref=dst.at[kk], send_sem=send_sems.at[2 * kk + j],
                recv_sem=recv_sems.at[2 * kk + j], device_id=(x, y, 1 - c), device_id_type=MESH)
                for j, (src, n, dst) in enumerate(pair)]

        @pl.when(k == 0)
        def _():
            for kk in range(N_CHIPS):
                for cp in copies(kk):
                    cp.start()

        for cp in copies(k):
            cp.wait_recv()
        oi_ref[...] = (gi_ref[...] + ri_v[k].astype(F32)).astype(oi_ref.dtype)
        oo_ref[...] = (go_ref[...] + ro_v[k].astype(F32)).astype(oo_ref.dtype)

        @pl.when(k == N_CHIPS - 1)
        def _():
            for kk in range(N_CHIPS):
                for cp in copies(kk):
                    cp.wait_send()

    def specs(p):
        rows, cols = p.shape[2], p.shape[3]
        mine = pl.BlockSpec((None, None, rows, cols), lambda k, cl: (cl[1], k, cl[0], 0))
        out = pl.BlockSpec((None, None, rows, cols), lambda k, cl: (cl[1], k, 0, 0))
        return mine, out

    (gi_s, pi_s), (go_s, po_s) = specs(p_i), specs(p_o)
    grid_spec = pltpu.PrefetchScalarGridSpec(
        num_scalar_prefetch=1, grid=(N_CHIPS,),
        in_specs=[gi_s, ANY, ANY, go_s, ANY, ANY], out_specs=[pi_s, po_s],
        scratch_shapes=[pltpu.VMEM((N_CHIPS, hi, p_i.shape[3]), BF16), pltpu.VMEM((N_CHIPS, ho, p_o.shape[3]), BF16),
                        pltpu.SemaphoreType.DMA((2 * N_CHIPS,)), pltpu.SemaphoreType.DMA((2 * N_CHIPS,))])
    return pl.pallas_call(
        body, name="swap_add", grid_spec=grid_spec,
        out_shape=[jax.ShapeDtypeStruct(p_i.shape, p_i.dtype), jax.ShapeDtypeStruct(p_o.shape, p_o.dtype)],
        input_output_aliases={3: 0, 6: 1},
        compiler_params=_vmem_params(dimension_semantics=("arbitrary",), has_side_effects=True),
    )(cl_arr, g_i, g16_i, p_i, g_o, g16_o, p_o)


def _exchange_last(l_arr, p_i, p_o, r_i, r_o):
    def body(l_ref, p_i_ref, p_o_ref, ri_in, ro_in, ri_ref, ro_ref, send_sems, recv_sems):
        del ri_in, ro_in
        always = l_ref[0] >= 0
        _exchange_comm(always, always, l_ref[0], p_i_ref, p_o_ref, None, ri_ref, ro_ref, None,
                       send_sems, recv_sems, None)

    return pl.pallas_call(
        body, name="exchange_last",
        in_specs=[pl.BlockSpec(memory_space=pltpu.SMEM)] + [ANY] * 4, out_specs=[ANY] * 2,
        out_shape=[jax.ShapeDtypeStruct(r_i.shape, r_i.dtype), jax.ShapeDtypeStruct(r_o.shape, r_o.dtype)],
        input_output_aliases={3: 0, 4: 1},
        scratch_shapes=[pltpu.SemaphoreType.DMA((N_EXCH_SEMS,)), pltpu.SemaphoreType.DMA((N_EXCH_SEMS,))],
        compiler_params=pltpu.CompilerParams(has_side_effects=True),
    )(l_arr, p_i, p_o, r_i, r_o)


def _sum_small(r_sms):
    L = len(r_sms)

    def body(*refs):
        o_ref = refs[L]
        for l in range(L):
            acc = refs[l][0]
            for d in range(1, N_DEV):
                acc = acc + refs[l][d]
            o_ref[l] = acc

    return pl.pallas_call(
        body, name="sum_small",
        out_shape=jax.ShapeDtypeStruct((L,) + r_sms[0].shape[1:], F32),
        compiler_params=_vmem_params(),
    )(*r_sms)


def _sum_chunks(kc_arr, p_i, q_i, p_o, q_o, *, nb):
    L = p_i.shape[0]

    def body(kc_ref, pi_ref, a0, a1, a2, po_ref, b0, b1, b2, oi_ref, oo_ref):
        del kc_ref
        f = lambda ref: ref[...].astype(F32)
        oi_ref[...] = ((f(pi_ref) + f(a0)) + f(a1)) + f(a2)
        oo_ref[...] = ((f(po_ref) + f(b0)) + f(b1)) + f(b2)

    def specs(p):
        tr, cols = p.shape[2] // nb, p.shape[3]
        chunk = pl.BlockSpec((None, None, tr, cols), lambda l, i, kc: (l, kc[0], i, 0))
        got = [pl.BlockSpec((None, None, tr, cols), lambda l, i, kc, _j=j: (_j, l, i, 0)) for j in range(3)]
        out = pl.BlockSpec((None, tr, cols), lambda l, i, kc: (l, kc[1] * nb + i, 0))
        return [chunk] + got, out

    (in_i, out_i), (in_o, out_o) = specs(p_i), specs(p_o)
    grid_spec = pltpu.PrefetchScalarGridSpec(num_scalar_prefetch=1, grid=(L, nb), in_specs=in_i + in_o,
                                             out_specs=[out_i, out_o])
    return pl.pallas_call(
        body, name="sum_chunks", grid_spec=grid_spec,
        out_shape=[jax.ShapeDtypeStruct((L, 2 * p.shape[2], p.shape[3]), F32) for p in (p_i, p_o)],
        compiler_params=_vmem_params(dimension_semantics=("arbitrary",) * 2),
    )(kc_arr, p_i, q_i, q_i, q_i, p_o, q_o, q_o, q_o)


def _share_result(gi, go):
    hi_rows, ho_rows = gi.shape[1] // 2, go.shape[1] // 2

    def body(gi_ref, go_ref, oi_ref, oo_ref, send_sems, recv_sems):
        del gi_ref, go_ref
        x, y, c = _place()
        sibling = (x, y, 1 - c)
        cps = []
        for j, (ref, n) in enumerate(((oi_ref, hi_rows), (oo_ref, ho_rows))):
            mine = ref.at[:, pl.ds(c * n, n), :]
            cps.append(pltpu.make_async_remote_copy(src_ref=mine, dst_ref=mine, send_sem=send_sems.at[j],
                                                    recv_sem=recv_sems.at[j], device_id=sibling, device_id_type=MESH))
        for cp in cps:
            cp.start()
        for j, (ref, n) in enumerate(((oi_ref, hi_rows), (oo_ref, ho_rows))):
            theirs = ref.at[:, pl.ds((1 - c) * n, n), :]
            pltpu.make_async_remote_copy(src_ref=theirs, dst_ref=theirs, send_sem=send_sems.at[j],
                                         recv_sem=recv_sems.at[j], device_id=sibling, device_id_type=MESH).wait_recv()
        for cp in cps:
            cp.wait_send()

    return pl.pallas_call(
        body, name="share_result",
        in_specs=[ANY, ANY], out_specs=[ANY, ANY],
        out_shape=[jax.ShapeDtypeStruct(gi.shape, F32), jax.ShapeDtypeStruct(go.shape, F32)],
        input_output_aliases={0: 0, 1: 1},
        scratch_shapes=[pltpu.SemaphoreType.DMA((2,)), pltpu.SemaphoreType.DMA((2,))],
        compiler_params=pltpu.CompilerParams(has_side_effects=True),
    )(gi, go)


WEIGHTS = ("ln_g", "ln_b", "w_in", "b_in", "conv_a_w", "conv_a_b", "norm_a_g", "norm_a_b", "conv_b_w", "pool_w",
           "pool_scale", "sgu_ln_g", "sgu_ln_b", "sgu_w", "sgu_bias", "w_out", "b_out")


def _pad_rows(a, rows):
    return jnp.pad(a, ((0, rows - a.shape[0]), (0, 0)))


def _indicator_consts():
    seg = jnp.where((jnp.arange(GROUP)[:, None] // HEAD) == (jnp.arange(GROUP)[None, :] // HEAD),
                    1.0 / HEAD, 0.0).astype(BF16)
    e4 = ((jnp.arange(GROUP)[:, None] // HEAD) == jnp.arange(128)[None, :]).astype(BF16)
    return seg, e4


def _layer_consts(p, conv_full):
    L = conv_full.shape[0]
    same_head = jnp.eye(4, dtype=F32)[:, None, :, None] > 0

    def rows_to(a, rows):
        return jnp.pad(a, ((0, 0), (0, rows - a.shape[1]), (0, 0)))

    s256 = jnp.stack([p[n] for n in ("conv_a_b", "norm_a_g", "norm_a_b", "pool_scale", "sgu_ln_g", "sgu_ln_b")], axis=1)
    pw = jnp.where(same_head, p["pool_w"][:, :, :, None, :], 0.0).reshape(L, GROUP, GROUP)
    return dict(
        caw=rows_to(conv_full[:, :KA], 32), cbw=rows_to(conv_full[:, KA:], 8), s256=rows_to(s256, 8),
        pw=pw.astype(BF16),
        wm=jnp.transpose(p["sgu_w"], (0, 2, 1, 3)).reshape(L, SGU_BLOCK, 4 * SGU_BLOCK),
        wmt=jnp.transpose(p["sgu_w"], (0, 1, 3, 2)).reshape(L, 4 * SGU_BLOCK, SGU_BLOCK),
        sb=jnp.repeat(jnp.transpose(p["sgu_bias"], (0, 2, 1)), HEAD, axis=2),
        v1024=rows_to(jnp.stack([p["b_out"], p["ln_g"], p["ln_b"]], axis=1), 8),
        bin=p["b_in"][:, None, :])


def _unpack_small(sm):
    L = sm.shape[0]
    owc = jnp.concatenate([sm[:, ROW_WC:ROW_WC + SGU_BLOCK], sm[:, ROW_WC + SGU_BLOCK:ROW_WC + 2 * SGU_BLOCK]], axis=2)
    return dict(
        conv_a_b=sm[:, 0], norm_a_g=sm[:, 1], norm_a_b=sm[:, 2], pool_scale=sm[:, 3], sgu_ln_g=sm[:, 4],
        sgu_ln_b=sm[:, 5], conv_b_w=sm[:, ROW_CBW:ROW_CBW + KB], conv_a_w=sm[:, ROW_CAW:ROW_CAW + KA],
        pool_w=jnp.transpose(sm[:, ROW_PW:ROW_PW + HEAD].reshape(L, HEAD, 4, HEAD), (0, 2, 1, 3)),
        ln_g=sm[:, ROW_LNG:ROW_LNG + 4].reshape(L, D_MODEL), ln_b=sm[:, ROW_LNB:ROW_LNB + 4].reshape(L, D_MODEL),
        b_out=sm[:, ROW_BOUT:ROW_BOUT + 4].reshape(L, D_MODEL),
        b_in=sm[:, ROW_BIN:ROW_BIN + N_SLICES].reshape(L, IN_WIDTH),
        sgu_w=jnp.transpose(owc.reshape(L, SGU_BLOCK, 4, SGU_BLOCK), (0, 2, 1, 3)),
        sgu_bias=sm[:, ROW_SB:ROW_SB + 4, 0:SGU_BLOCK])


def _step(p, m, v, x, target, *, tile_f, tile_b, k_steps):
    L = p["ln_g"].shape[0]
    xi, yi, ci = _place()
    me_k = 2 * xi + yi
    hi_rows, ho_rows = D_MODEL // 2, GROUP // 2

    cw = jnp.concatenate([p["conv_a_w"], p["conv_b_w"]], axis=1).reshape(-1, 128)
    cw_rows = cw.shape[0]
    cw = _pad_rows(cw, -(-cw_rows // SUBLANES) * SUBLANES)
    wi16 = p["w_in"].astype(BF16)
    wo16 = p["w_out"].astype(BF16)
    wig0, wog0, cwg = _gather_weights(wi16[0:1], wo16[0:1], cw)
    cwg = cwg[:, :cw_rows].reshape(N_CHIPS, L, KA + KB, HEAD)
    conv_full = jnp.transpose(cwg, (1, 2, 0, 3)).reshape(L, KA + KB, GROUP)
    seg, e4 = _indicator_consts()
    k = _layer_consts(p, conv_full)
    layer = [jnp.full((1,), l, jnp.int32) for l in range(L)]

    hcur = x
    saved, wig, wog = [], [wig0[0]], [wog0[0]]
    for l in range(L):
        nxt = (wi16, wo16) if l + 1 < L else None
        outs = _fwd_layer(layer[l], hcur, wig[l], k["bin"], k["caw"], k["cbw"], k["s256"], seg, k["pw"], k["wm"], k["sb"],
                          wog[l], k["v1024"], tile=tile_f, nxt=nxt, target=None if nxt is not None else target)
        y, xb, h, aux, mixb, z = outs[0:6]
        if nxt is not None:
            wig.append(outs[6])
            wog.append(outs[7])
        saved.append((xb, h, aux, mixb, z))
        hcur = y

    dy = hcur
    loss_local = outs[6][0, 0]

    gwi = lax.empty((L, N_CHIPS, D_MODEL, COLS), F32)
    gwo = lax.empty((L, N_CHIPS, GROUP, D_MODEL), F32)
    gwi16 = lax.empty((L, N_CHIPS, D_MODEL, COLS), BF16)
    gwo16 = lax.empty((L, N_CHIPS, GROUP, D_MODEL), BF16)
    p_i = lax.empty((L, N_CHIPS, hi_rows, COLS), BF16)
    p_o = lax.empty((L, N_CHIPS, ho_rows, D_MODEL), BF16)
    q_i = lax.empty((3, L, hi_rows, COLS), BF16)
    q_o = lax.empty((3, L, ho_rows, D_MODEL), BF16)
    r_sm = [None] * L
    pending = None
    for l in reversed(range(L)):
        xb, h, aux, mixb, z = saved[l]
        exch = None if pending is None else (p_i, p_o, pending, q_i, q_o)
        outs = _bwd_layer(layer[l], dy, z, h, aux, wig[l], k["caw"], k["cbw"], k["s256"], seg, k["pw"], k["wm"],
                          k["wmt"], k["sb"], wog[l], k["v1024"], e4, tile=tile_b, exch=exch)
        dy, dhb, dzb, osm = outs[0:4]
        if l == L - 1:
            osm = osm.at[ROW_LOSS, 0].set(loss_local)
        if exch is not None:
            q_i, q_o, r_sm[l + 1] = outs[4:7]
        larr = layer[l]
        outs = _dw(larr, xb, dhb, mixb, dzb, gwi, gwi16, gwo, gwo16, k_steps=k_steps, small=osm if l == 0 else None)
        gwi, gwi16, gwo, gwo16 = outs[0:4]
        if l == 0:
            r_sm[0] = outs[4]
        cl_arr = jnp.stack([ci, jnp.int32(l)]).astype(jnp.int32)
        p_i, p_o = _swap_add(cl_arr, gwi, gwi16, p_i, gwo, gwo16, p_o)
        pending = osm
    grad_x = dy
    q_i, q_o = _exchange_last(layer[0], p_i, p_o, q_i, q_o)

    summed = _sum_small(r_sm)
    loss = summed[L - 1, ROW_LOSS, 0]
    grads = _unpack_small(summed)
    for n in ("conv_a_w", "conv_b_w"):
        grads[n] = lax.dynamic_slice_in_dim(grads[n], me_k * HEAD, HEAD, axis=2)

    kc_arr = jnp.stack([me_k, ci]).astype(jnp.int32)
    g_i, g_o = _sum_chunks(kc_arr, p_i, q_i, p_o, q_o, nb=2)
    g_i, g_o = _share_result(g_i, g_o)
    grads["w_in"] = g_i
    grads["w_out"] = g_o

    delta, new_m, new_v = {}, {}, {}
    for n, tr in (("w_in", 512), ("w_out", 256)):
        shp = p[n].shape
        args = [a.reshape(shp[0] * shp[1], shp[2]) for a in (p[n], grads[n], m[n], v[n])]
        outs = _adamw(*args, rows_per_step=tr, name="adamw_" + n, copy_g=True)
        delta[n], new_m[n], new_v[n], grads[n] = (a.reshape(shp) for a in outs)
    small = [n for n in WEIGHTS if n not in ("w_in", "w_out")]
    flat = [[a[n].reshape(-1, a[n].shape[-1]) for n in small] for a in (p, grads, m, v)]
    outs = _adamw_small(*flat)
    for j, n in enumerate(small):
        delta[n], new_m[n], new_v[n] = (o[j].reshape(p[n].shape) for o in outs)

    return (loss, grad_x[None], *[grads[n] for n in WEIGHTS], *[delta[n] for n in WEIGHTS],
            *[new_m[n] for n in WEIGHTS], *[new_v[n] for n in WEIGHTS])


def kernel(x, ln_g, ln_b, w_in, b_in, conv_a_w, conv_a_b, norm_a_g, norm_a_b, conv_b_w, pool_w, pool_scale, sgu_ln_g, sgu_ln_b, sgu_w, sgu_bias, w_out, b_out, loss_target, m_ln_g, m_ln_b, m_w_in, m_b_in, m_conv_a_w, m_conv_a_b, m_norm_a_g, m_norm_a_b, m_conv_b_w, m_pool_w, m_pool_scale, m_sgu_ln_g, m_sgu_ln_b, m_sgu_w, m_sgu_bias, m_w_out, m_b_out, v_ln_g, v_ln_b, v_w_in, v_b_in, v_conv_a_w, v_conv_a_b, v_norm_a_g, v_norm_a_b, v_conv_b_w, v_pool_w, v_pool_scale, v_sgu_ln_g, v_sgu_ln_b, v_sgu_w, v_sgu_bias, v_w_out, v_b_out):
    p = dict(ln_g=ln_g, ln_b=ln_b, w_in=w_in, b_in=b_in, conv_a_w=conv_a_w, conv_a_b=conv_a_b, norm_a_g=norm_a_g,
             norm_a_b=norm_a_b, conv_b_w=conv_b_w, pool_w=pool_w, pool_scale=pool_scale, sgu_ln_g=sgu_ln_g,
             sgu_ln_b=sgu_ln_b, sgu_w=sgu_w, sgu_bias=sgu_bias, w_out=w_out, b_out=b_out)
    m = dict(ln_g=m_ln_g, ln_b=m_ln_b, w_in=m_w_in, b_in=m_b_in, conv_a_w=m_conv_a_w, conv_a_b=m_conv_a_b,
             norm_a_g=m_norm_a_g, norm_a_b=m_norm_a_b, conv_b_w=m_conv_b_w, pool_w=m_pool_w, pool_scale=m_pool_scale,
             sgu_ln_g=m_sgu_ln_g, sgu_ln_b=m_sgu_ln_b, sgu_w=m_sgu_w, sgu_bias=m_sgu_bias, w_out=m_w_out, b_out=m_b_out)
    v = dict(ln_g=v_ln_g, ln_b=v_ln_b, w_in=v_w_in, b_in=v_b_in, conv_a_w=v_conv_a_w, conv_a_b=v_conv_a_b,
             norm_a_g=v_norm_a_g, norm_a_b=v_norm_a_b, conv_b_w=v_conv_b_w, pool_w=v_pool_w, pool_scale=v_pool_scale,
             sgu_ln_g=v_sgu_ln_g, sgu_ln_b=v_sgu_ln_b, sgu_w=v_sgu_w, sgu_bias=v_sgu_bias, w_out=v_w_out, b_out=v_b_out)
    return _step(p, m, v, x[0], loss_target[0], tile_f=256, tile_b=256, k_steps=4)
```

```python
import jax
import jax.numpy as jnp
from jax import lax
from jax.experimental import pallas as pl
from jax.experimental.pallas import tpu as pltpu

F32 = jnp.float32
BF16 = jnp.bfloat16
MESH = pl.DeviceIdType.MESH

D_MODEL = 1024
GROUP = 256
HEAD = 64
N_SLICES = 12
IN_WIDTH = N_SLICES * GROUP
N_CHIPS = 4
COLS = IN_WIDTH // N_CHIPS
KA = 31
KB = 3
SUBLANES = 8
HALO_A, HALO_B, HALO_C = 32, 8, 16
N_GATHER_SEMS = 12
N_EXCH_SEMS = 13
SGU_BLOCK = 128
CHUNK = 64
LN_EPS = 1e-5
ROWS = 64
V7X_VMEM_BYTES = 64 * 1024 * 1024
VMEM_LIMIT = V7X_VMEM_BYTES - 8 * 1024 * 1024

ADAM_LR, ADAM_B1, ADAM_B2, ADAM_EPS, ADAM_WD, ADAM_STEP = 0.001, 0.9, 0.999, 1e-08, 0.01, 10


ANY = pl.BlockSpec(memory_space=pl.ANY)


def _vmem_params(**kw):
    return pltpu.CompilerParams(vmem_limit_bytes=VMEM_LIMIT, **kw)


def _whole(a):
    return pl.BlockSpec(a.shape, lambda i, l, _n=a.ndim: (0,) * _n)


def _of_layer(a):
    return pl.BlockSpec((None,) + a.shape[1:], lambda i, l, _n=a.ndim: (l[0],) + (0,) * (_n - 1))


def _place():
    return lax.axis_index("x"), lax.axis_index("y"), lax.axis_index("c")


def _other_chips(x, y):
    return [(1 - x, y, 2 * (1 - x) + y), (x, 1 - y, 2 * x + (1 - y)), (1 - x, 1 - y, 2 * (1 - x) + (1 - y))]


def _sig(v):
    return 0.5 * jnp.tanh(0.5 * v) + 0.5


def _dot(a, b):
    return jnp.dot(a, b, preferred_element_type=F32)


def _dot_nt(a, b):
    return lax.dot_general(a, b, (((1,), (1,)), ((), ())), preferred_element_type=F32)


def _dot_tn(a, b):
    return lax.dot_general(a, b, (((0,), (0,)), ((), ())), preferred_element_type=F32)


def _segdot(v, m):
    hi = v.astype(BF16)
    lo = (v - hi.astype(F32)).astype(BF16)
    return _dot(hi, m) + _dot(lo, m)


def _colsum(v):
    return jnp.sum(v, axis=0, keepdims=True)


def _rowmean(v):
    return jnp.mean(v, axis=-1, keepdims=True)


def _lane_group(n):
    return lax.broadcasted_iota(jnp.int32, (1, n), 1) // HEAD


def _pool_cnt(tile, t_rows):
    pos = tile * t_rows + lax.broadcasted_iota(jnp.int32, (t_rows, GROUP), 0) + 1
    grp = lax.broadcasted_iota(jnp.int32, (t_rows, GROUP), 1) // HEAD
    win = jnp.where(grp == 0, 2, jnp.where(grp == 1, 4, jnp.where(grp == 2, 8, 16)))
    return jnp.minimum(pos, win).astype(F32)


def _sgu_masks(wm_ref, wmt_ref, wm_s, wmt_s):
    r = lax.broadcasted_iota(jnp.int32, (SGU_BLOCK, 4 * SGU_BLOCK), 0) // CHUNK
    c = (lax.broadcasted_iota(jnp.int32, (SGU_BLOCK, 4 * SGU_BLOCK), 1) % SGU_BLOCK) // CHUNK
    wm_s[...] = jnp.where(c <= r, wm_ref[...], 0.0).astype(BF16)
    if wmt_ref is not None:
        rt = (lax.broadcasted_iota(jnp.int32, (4 * SGU_BLOCK, SGU_BLOCK), 0) % SGU_BLOCK) // CHUNK
        ct = lax.broadcasted_iota(jnp.int32, (4 * SGU_BLOCK, SGU_BLOCK), 1) // CHUNK
        wmt_s[...] = jnp.where(rt <= ct, wmt_ref[...], 0.0).astype(BF16)


def _vstack(v_blk):
    grp = _lane_group(GROUP)
    return jnp.concatenate([jnp.where(grp == h, v_blk, 0.0) for h in range(4)], axis=0).astype(BF16)


def _gather_next(step, nt, nwi, nwo, gwi, gwo, send_sems, recv_sems, loc_sems, vwi, vwo):
    x, y, c = _place()
    me_k = 2 * x + y
    sibling = (x, y, 1 - c)
    chips = _other_chips(x, y)
    hi, ho = D_MODEL // 2, GROUP // 2
    fwd_sems = N_GATHER_SEMS // 2

    def rc(src, dst, sem, to):
        return pltpu.make_async_remote_copy(src_ref=src, dst_ref=dst, send_sem=send_sems.at[sem],
                                            recv_sem=recv_sems.at[sem], device_id=to, device_id_type=MESH)

    def blk(ref, k, n, cc):
        return ref.at[k, pl.ds(cc * n, n), :]

    def ici(r):
        px, py, _ = chips[r]
        to = (px, py, c)
        return [rc(nwi.at[pl.ds(c * hi, hi), :], blk(gwi, me_k, hi, c), 2 * r, to),
                rc(nwo.at[pl.ds(c * ho, ho), :], blk(gwo, me_k, ho, c), 2 * r + 1, to)]

    def landed(r, cc, base):
        pk = chips[r][2]
        return [rc(blk(gwi, pk, hi, cc), blk(gwi, pk, hi, cc), base + 2 * r, sibling),
                rc(blk(gwo, pk, ho, cc), blk(gwo, pk, ho, cc), base + 2 * r + 1, sibling)]

    def stage_in():
        return [pltpu.make_async_copy(nwi, vwi, loc_sems.at[0]), pltpu.make_async_copy(nwo, vwo, loc_sems.at[1])]

    def local():
        return [pltpu.make_async_copy(vwi, gwi.at[me_k], loc_sems.at[2]),
                pltpu.make_async_copy(vwo, gwo.at[me_k], loc_sems.at[3])]

    @pl.when(step == 0)
    def _():
        for cp in stage_in():
            cp.start()
        for r in range(3):
            for cp in ici(r):
                cp.start()

    @pl.when(step == 1)
    def _():
        for cp in stage_in():
            cp.wait()
        for cp in local():
            cp.start()

    @pl.when(step == (3 * nt) // 4)
    def _():
        for r in range(3):
            for got, fwd in zip(landed(r, c, 0), landed(r, c, fwd_sems)):
                got.wait_recv()
                fwd.start()

    @pl.when(step == nt - 1)
    def _():
        for r in range(3):
            for got in landed(r, 1 - c, fwd_sems):
                got.wait_recv()
        for r in range(3):
            for cp in ici(r) + landed(r, c, fwd_sems):
                cp.wait_send()
        for cp in local():
            cp.wait()


def _fwd_layer(larr, x, wi, bin_, caw, cbw, s256, seg, pw, wm, sb, wo, v1024, *, tile, nxt=None, target=None):
    assert nxt is None or target is None
    S = x.shape[0]
    T = tile
    nt = S // T
    alpha = float((2.0 * 4) ** 0.25)
    n_in = 13 + (2 if nxt is not None else 0) + (1 if target is not None else 0)
    n_out = 6 + (2 if nxt is not None else 0) + (1 if target is not None else 0)

    def body(*refs):
        l_ref = refs[0]
        (x_ref, wi_ref, bin_ref, caw_ref, cbw_ref, s256_ref, seg_ref, pw_ref, wm_ref, sb_ref, wo_ref,
         v1024_ref) = refs[1:13]
        y_ref, xb_ref, h_ref, aux_ref, mix_ref, z_ref = refs[n_in:n_in + 6]
        abuf, bbuf, cbuf, wm_s, shf = refs[n_in + n_out:n_in + n_out + 5]
        i = pl.program_id(0)
        if nxt is not None:
            _gather_next(i, nt, refs[13].at[l_ref[0] + 1], refs[14].at[l_ref[0] + 1], refs[n_in + 6], refs[n_in + 7],
                         *refs[n_in + n_out + 5:])

        @pl.when(i == 0)
        def _():
            abuf[0:HALO_A, :] = jnp.zeros((HALO_A, GROUP), F32)
            bbuf[0:HALO_B, :] = jnp.zeros((HALO_B, GROUP), F32)
            cbuf[0:HALO_C, :] = jnp.zeros((HALO_C, GROUP), F32)
            _sgu_masks(wm_ref, None, wm_s, None)

        x = x_ref[...]
        xb = x.astype(BF16)
        xb_ref[...] = xb
        for k in range(N_CHIPS):
            h_ref[:, COLS * k:COLS * (k + 1)] = _dot(xb, wi_ref[k]) + bin_ref[:, COLS * k:COLS * (k + 1)]

        def hs(j):
            return h_ref[:, GROUP * j:GROUP * (j + 1)]

        abuf[HALO_A:HALO_A + T, :] = hs(0) * _sig(hs(1))
        span = T + HALO_A - SUBLANES
        for p in range(1, SUBLANES):
            shf[p - 1, :, :] = abuf[p:p + span, :]
        for r0 in range(0, T, ROWS):
            acc = None
            for k in range(KA):
                off = HALO_A - (KA - 1) + k
                p, q8 = off % SUBLANES, off - off % SUBLANES
                win = abuf[r0 + q8:r0 + q8 + ROWS, :] if p == 0 else shf[p - 1, r0 + q8:r0 + q8 + ROWS, :]
                term = caw_ref[k:k + 1, :] * win
                acc = term if acc is None else acc + term
            aux_ref[r0:r0 + ROWS, 0:GROUP] = acc + s256_ref[0:1, :]
        abuf[0:HALO_A, :] = abuf[T:T + HALO_A, :]
        a1 = aux_ref[:, 0:GROUP]
        segm = seg_ref[...]
        cen = a1 - _segdot(a1, segm)
        var = _segdot(cen * cen, segm)
        a2 = cen * lax.rsqrt(var + LN_EPS) * s256_ref[1:2, :] + s256_ref[2:3, :]
        az = hs(2)
        mix_ref[:, 0:GROUP] = (a2 * _sig(a2) * (az * _sig(az))).astype(BF16)

        bbuf[HALO_B:HALO_B + T, :] = hs(4) * hs(5)
        for r0 in range(0, T, ROWS):
            acc = None
            for k in range(KB):
                off = HALO_B - (KB - 1) + k + r0
                term = cbw_ref[k:k + 1, :] * bbuf[off:off + ROWS, :]
                acc = term if acc is None else acc + term
            aux_ref[r0:r0 + ROWS, GROUP:2 * GROUP] = acc
        bbuf[0:HALO_B, :] = bbuf[T:T + HALO_B, :]
        bz = hs(6)
        mix_ref[:, GROUP:2 * GROUP] = (hs(3) * aux_ref[:, GROUP:2 * GROUP] * (bz * _sig(bz))).astype(BF16)

        ch = hs(7)
        cbuf[HALO_C:HALO_C + T, :] = ch
        hi_lane = (lax.broadcasted_iota(jnp.int32, (1, 128), 1) // HEAD) == 1
        for r0 in range(0, T, ROWS):
            def win(col, j0, j1):
                s = None
                for j in range(j0, j1):
                    off = HALO_C - j + r0
                    term = cbuf[off:off + ROWS, 128 * col:128 * (col + 1)]
                    s = term if s is None else s + term
                return s
            w0 = win(0, 0, 2) + jnp.where(hi_lane, win(0, 2, 4), 0.0)
            w1 = win(1, 0, 8) + jnp.where(hi_lane, win(1, 8, 16), 0.0)
            aux_ref[r0:r0 + ROWS, 2 * GROUP:2 * GROUP + 128] = w0
            aux_ref[r0:r0 + ROWS, 2 * GROUP + 128:3 * GROUP] = w1
        cbuf[0:HALO_C, :] = cbuf[T:T + HALO_C, :]
        pooled = aux_ref[:, 2 * GROUP:3 * GROUP] / _pool_cnt(i, T) - ch
        aux_ref[:, 2 * GROUP:3 * GROUP] = pooled
        q = _dot(pooled.astype(BF16), pw_ref[...])
        cz = hs(8)
        mix_ref[:, 2 * GROUP:3 * GROUP] = (q * s256_ref[3:4, :] * (cz * _sig(cz))).astype(BF16)

        dv = hs(10)
        cen = dv - _rowmean(dv)
        var = _rowmean(cen * cen)
        v = cen * lax.rsqrt(var + LN_EPS) * s256_ref[4:5, :] + s256_ref[5:6, :]
        sps = []
        for n in range(T // SGU_BLOCK):
            vb = v[n * SGU_BLOCK:(n + 1) * SGU_BLOCK, :]
            sps.append(_dot(wm_s[...], _vstack(vb)) + sb_ref[...])
        sp = jnp.concatenate(sps, axis=0)
        dz = hs(11)
        mix_ref[:, 3 * GROUP:4 * GROUP] = (hs(9) * sp * (dz * _sig(dz))).astype(BF16)

        out = v1024_ref[0:1, :]
        for k in range(N_CHIPS):
            out = out + _dot(mix_ref[:, GROUP * k:GROUP * (k + 1)], wo_ref[k])
        z = alpha * x + out
        z_ref[...] = z
        cen = z - _rowmean(z)
        var = _rowmean(cen * cen)
        y = cen * lax.rsqrt(var + LN_EPS) * v1024_ref[1:2, :] + v1024_ref[2:3, :]
        if target is None:
            y_ref[...] = y
        else:
            t_ref, loss_ref = refs[13], refs[n_in + 6]

            @pl.when(i == 0)
            def _():
                loss_ref[...] = jnp.zeros_like(loss_ref)
            err = y - t_ref[...]
            y_ref[...] = err * (1.0 / D_MODEL)
            loss_ref[...] += jnp.sum(_colsum(err * err), axis=1, keepdims=True) * (0.5 / D_MODEL)

    def rows(width):
        return pl.BlockSpec((T, width), lambda i, l: (i, 0))

    consts = (wi, bin_, caw, cbw, s256, seg, pw, wm, sb, wo, v1024)
    in_specs = [rows(D_MODEL)] + [_whole(a) if a is wi or a is seg or a is wo else _of_layer(a) for a in consts]
    out_specs = [rows(D_MODEL), rows(D_MODEL), rows(IN_WIDTH), rows(3 * GROUP), rows(D_MODEL), rows(D_MODEL)]
    out_shape = [jax.ShapeDtypeStruct((S, D_MODEL), F32), jax.ShapeDtypeStruct((S, D_MODEL), BF16),
                 jax.ShapeDtypeStruct((S, IN_WIDTH), F32), jax.ShapeDtypeStruct((S, 3 * GROUP), F32),
                 jax.ShapeDtypeStruct((S, D_MODEL), BF16), jax.ShapeDtypeStruct((S, D_MODEL), F32)]
    scratch = [pltpu.VMEM((T + HALO_A, GROUP), F32), pltpu.VMEM((T + HALO_B, GROUP), F32),
               pltpu.VMEM((T + HALO_C, GROUP), F32), pltpu.VMEM((SGU_BLOCK, 4 * SGU_BLOCK), BF16),
               pltpu.VMEM((SUBLANES - 1, T + HALO_A - SUBLANES, GROUP), F32)]
    extra = ()
    if nxt is not None:
        extra = tuple(nxt)
        in_specs += [ANY, ANY]
        out_specs += [ANY, ANY]
        out_shape += [jax.ShapeDtypeStruct((N_CHIPS, D_MODEL, COLS), BF16),
                      jax.ShapeDtypeStruct((N_CHIPS, GROUP, D_MODEL), BF16)]
        scratch += [pltpu.SemaphoreType.DMA((N_GATHER_SEMS,)), pltpu.SemaphoreType.DMA((N_GATHER_SEMS,)),
                    pltpu.SemaphoreType.DMA((4,)), pltpu.VMEM((D_MODEL, COLS), BF16), pltpu.VMEM((GROUP, D_MODEL), BF16)]
    if target is not None:
        extra = (target,)
        in_specs += [rows(D_MODEL)]
        out_specs += [pl.BlockSpec((8, 128), lambda i, l: (0, 0))]
        out_shape += [jax.ShapeDtypeStruct((8, 128), F32)]
    grid_spec = pltpu.PrefetchScalarGridSpec(num_scalar_prefetch=1, grid=(nt,), in_specs=in_specs,
                                             out_specs=out_specs, scratch_shapes=scratch)
    return pl.pallas_call(
        body, name=("fwd_layer_loss" if target is not None else "fwd_layer") if nxt is None else "fwd_layer_gather",
        grid_spec=grid_spec, out_shape=out_shape,
        compiler_params=_vmem_params(dimension_semantics=("arbitrary",), has_side_effects=nxt is not None),
    )(larr, x, *consts, *extra)


ROW_CBW = 8
ROW_CAW = 16
ROW_LOSS = 7
ROW_PW = 48
ROW_LNG = 112
ROW_LNB = 116
ROW_BOUT = 120
ROW_BIN = 124
ROW_WC = 136
ROW_SB = 392
SM_ROWS = 400
N_DEV = 8


def _exchange_comm(start, finish, l, p_i, p_o, sm, r_i, r_o, r_sm, send_sems, recv_sems, loc_sem):
    x, y, c = _place()
    me = 4 * x + 2 * y + c
    chips = _other_chips(x, y)

    def rc(src, dst, sem, to):
        return pltpu.make_async_remote_copy(src_ref=src, dst_ref=dst, send_sem=send_sems.at[sem],
                                            recv_sem=recv_sems.at[sem], device_id=to, device_id_type=MESH)

    def big(r):
        px, py, pk = chips[r]
        to = (px, py, c)
        return [rc(p_i.at[l, pk], r_i.at[r, l], 2 * r, to), rc(p_o.at[l, pk], r_o.at[r, l], 2 * r + 1, to)]

    def peer(rel):
        px = 1 - x if rel & 4 else x
        py = 1 - y if rel & 2 else y
        pc = 1 - c if rel & 1 else c
        return (px, py, pc), 4 * px + 2 * py + pc

    def small_out(rel):
        to, _ = peer(rel)
        return rc(sm, r_sm.at[me], N_EXCH_SEMS - N_DEV + rel, to)

    def small_in(rel):
        to, idx = peer(rel)
        return rc(sm, r_sm.at[idx], N_EXCH_SEMS - N_DEV + rel, to)

    def local():
        return pltpu.make_async_copy(sm, r_sm.at[me], loc_sem.at[0])

    with_big, with_small = p_i is not None, sm is not None

    @pl.when(start)
    def _():
        if with_small:
            local().start()
        if with_big:
            for r in range(3):
                for cp in big(r):
                    cp.start()
        if with_small:
            for rel in range(1, N_DEV):
                small_out(rel).start()

    @pl.when(finish)
    def _():
        if with_big:
            for r in range(3):
                for cp in big(r):
                    cp.wait()
        if with_small:
            for rel in range(1, N_DEV):
                small_in(rel).wait_recv()
                small_out(rel).wait_send()
            local().wait()


RC = 32
RC_WIDE = 16
ACC_ROWS = 136


def _rsum8(v):
    r = v[0:8]
    for j in range(1, v.shape[0] // 8):
        r = r + v[8 * j:8 * j + 8]
    return r


def _bwd_layer(larr, dy, z, h, aux, wi, caw, cbw, s256, seg, pw, wm, wmt, sb, wo, v1024, e4, *, tile, exch=None):
    S = dy.shape[0]
    T = tile
    nt = S // T
    nblk = T // SGU_BLOCK
    alpha = float((2.0 * 4) ** 0.25)
    n_in = 17 + (5 if exch is not None else 0)
    n_out = 4 + (3 if exch is not None else 0)
    slab = pltpu.VMEM((T, GROUP), F32)
    scratch = dict(
        dbuf=pltpu.VMEM((T + HALO_A, GROUP), F32), ebuf=pltpu.VMEM((T + HALO_B, GROUP), F32),
        fbuf=pltpu.VMEM((T + HALO_C, GROUP), F32), sh=pltpu.VMEM((SUBLANES - 1, T + HALO_A - SUBLANES, GROUP), F32),
        wm_s=pltpu.VMEM((SGU_BLOCK, 4 * SGU_BLOCK), BF16), wmt_s=pltpu.VMEM((4 * SGU_BLOCK, SGU_BLOCK), BF16),
        dsp_acc=pltpu.VMEM((SGU_BLOCK, GROUP), F32), pw_acc=pltpu.VMEM((GROUP, GROUP), F32),
        acc_s=pltpu.VMEM((8 * ACC_ROWS, GROUP), F32), acc_w=pltpu.VMEM((24, D_MODEL), F32),
        dmix_s=pltpu.VMEM((T, D_MODEL), F32), vst_s=pltpu.VMEM((nblk, 4 * SGU_BLOCK, GROUP), BF16),
        dq_s=pltpu.VMEM((T, GROUP), BF16), dxt_s=pltpu.VMEM((D_MODEL, T), F32),
        mean_s=slab, t1_s=slab, t2_s=slab, q_s=slab, xv_s=slab, rv_s=slab, v_s=slab, sp_s=slab, a0_s=slab, sg_s=slab,
        xh_s=slab, ra_s=slab, ub_s=slab, dsp_s=slab, m1_s=slab, m2_s=slab, dpool_s=slab, dvd_s=slab, u_s=slab,
        du_s=slab, cw_s=slab)
    names = list(scratch)

    def body(*refs):
        (dy_ref, z_ref, h_ref, aux_ref, wi_ref, caw_ref, cbw_ref, s256_ref, seg_ref, pw_ref, wm_ref, wmt_ref,
         sb_ref, wo_ref, v1024_ref, e4_ref) = refs[1:17]
        dx_ref, dhb_ref, dzb_ref, osm_ref = refs[n_in:n_in + 4]
        k0 = n_in + n_out
        sc = dict(zip(names, refs[k0:k0 + len(names)]))
        dbuf, ebuf, fbuf, sh = sc["dbuf"], sc["ebuf"], sc["fbuf"], sc["sh"]
        wm_s, wmt_s, dsp_acc, pw_acc, acc_s, acc_w = (sc[n] for n in ("wm_s", "wmt_s", "dsp_acc", "pw_acc", "acc_s",
                                                                        "acc_w"))
        dmix_s, vst_s, dq_s = sc["dmix_s"], sc["vst_s"], sc["dq_s"]
        i = pl.program_id(0)
        tile_idx = nt - 1 - i
        if exch is not None:
            p_i, p_o, sm = refs[17:20]
            r_i, r_o, r_sm = refs[n_in + 4:n_in + 7]
            _exchange_comm(i == 0, i == nt - 1, refs[0][0] + 1, p_i, p_o, sm, r_i, r_o, r_sm, *refs[k0 + len(names):])

        @pl.when(i == 0)
        def _():
            dbuf[T:T + HALO_A, :] = jnp.zeros((HALO_A, GROUP), F32)
            ebuf[T:T + HALO_B, :] = jnp.zeros((HALO_B, GROUP), F32)
            fbuf[T:T + HALO_C, :] = jnp.zeros((HALO_C, GROUP), F32)
            _sgu_masks(wm_ref, wmt_ref, wm_s, wmt_s)
            osm_ref[...] = jnp.zeros_like(osm_ref)
            dsp_acc[...] = jnp.zeros_like(dsp_acc)
            pw_acc[...] = jnp.zeros_like(pw_acc)
            acc_s[...] = jnp.zeros_like(acc_s)
            acc_w[...] = jnp.zeros_like(acc_w)

        def chunks(rc, fn):
            for c in range(T // rc):
                fn(pl.ds(c * rc, rc))

        def hs(j, rows):
            return h_ref[rows, GROUP * j:GROUP * (j + 1)]

        def acc_add(row, val):
            acc_s[8 * row:8 * row + 8, :] += _rsum8(val)

        def put_dh(j, rows, val):
            acc_add(ROW_BIN + j, val)
            dhb_ref[rows, GROUP * j:GROUP * (j + 1)] = val.astype(BF16)

        def dsilu(v, s):
            return s * (1.0 + v * (1.0 - s))

        def vec(r):
            return s256_ref[r:r + 1, :]

        def ln_bwd(rows):
            dyc = dy_ref[rows, :]
            zc = z_ref[rows, :]
            cen = zc - _rowmean(zc)
            rstd = lax.rsqrt(_rowmean(cen * cen) + LN_EPS)
            xhat = cen * rstd
            acc_w[0:8, :] += _rsum8(dyc * xhat)
            acc_w[8:16, :] += _rsum8(dyc)
            gdy = dyc * v1024_ref[1:2, :]
            dz = rstd * (gdy - _rowmean(gdy) - xhat * _rowmean(gdy * xhat))
            acc_w[16:24, :] += _rsum8(dz)
            dzb_ref[rows, :] = dz.astype(BF16)
            dx_ref[rows, :] = alpha * dz
        chunks(RC_WIDE, ln_bwd)

        segm = seg_ref[...]
        dzb = dzb_ref[...]
        for k in range(N_CHIPS):
            dmix_s[:, GROUP * k:GROUP * (k + 1)] = _dot_nt(dzb, wo_ref[k])
        sc["mean_s"][...] = _segdot(aux_ref[:, 0:GROUP], segm)
        pooled_b = aux_ref[:, 2 * GROUP:3 * GROUP].astype(BF16)
        sc["q_s"][...] = _dot(pooled_b, pw_ref[...])

        def centre(rows):
            cen = aux_ref[rows, 0:GROUP] - sc["mean_s"][rows, :]
            sc["t1_s"][rows, :] = cen * cen
            dv_in = hs(10, rows)
            cen_v = dv_in - _rowmean(dv_in)
            rstd_v = lax.rsqrt(_rowmean(cen_v * cen_v) + LN_EPS)
            xv = cen_v * rstd_v
            sc["xv_s"][rows, :] = xv
            sc["rv_s"][rows, :] = jnp.broadcast_to(rstd_v, xv.shape)
            sc["v_s"][rows, :] = xv * vec(4) + vec(5)
        chunks(RC, centre)

        sc["t2_s"][...] = _segdot(sc["t1_s"][...], segm)
        for n in range(nblk):
            blk = slice(n * SGU_BLOCK, (n + 1) * SGU_BLOCK)
            vst_s[n] = _vstack(sc["v_s"][blk, :])
            sc["sp_s"][blk, :] = _dot(wm_s[...], vst_s[n]) + sb_ref[...]

        def mixers(rows):
            a_val, a_glu, a_z = hs(0, rows), hs(1, rows), hs(2, rows)
            sg = _sig(a_glu)
            sc["a0_s"][rows, :] = a_val * sg
            sc["sg_s"][rows, :] = sg
            rstd_a = lax.rsqrt(sc["t2_s"][rows, :] + LN_EPS)
            xh = (aux_ref[rows, 0:GROUP] - sc["mean_s"][rows, :]) * rstd_a
            a2 = xh * vec(1) + vec(2)
            s2 = _sig(a2)
            sz = _sig(a_z)
            dya = dmix_s[rows, 0:GROUP]
            put_dh(2, rows, dya * (a2 * s2) * dsilu(a_z, sz))
            d_a2 = dya * (a_z * sz) * dsilu(a2, s2)
            acc_add(1, d_a2 * xh)
            acc_add(2, d_a2)
            gd = d_a2 * vec(1)
            sc["t1_s"][rows, :] = gd
            sc["t2_s"][rows, :] = gd * xh
            sc["xh_s"][rows, :] = xh
            sc["ra_s"][rows, :] = rstd_a
            b_b, b_c, b_h, b_z = hs(3, rows), hs(4, rows), hs(5, rows), hs(6, rows)
            cb = aux_ref[rows, GROUP:2 * GROUP]
            sz = _sig(b_z)
            dyb = dmix_s[rows, GROUP:2 * GROUP]
            put_dh(3, rows, dyb * cb * (b_z * sz))
            put_dh(6, rows, dyb * b_b * cb * dsilu(b_z, sz))
            ebuf[rows, :] = dyb * b_b * (b_z * sz)
            sc["ub_s"][rows, :] = b_c * b_h
            c_z = hs(8, rows)
            q = sc["q_s"][rows, :]
            sz = _sig(c_z)
            dyc = dmix_s[rows, 2 * GROUP:3 * GROUP]
            acc_add(3, dyc * q * (c_z * sz))
            put_dh(8, rows, dyc * q * vec(3) * dsilu(c_z, sz))
            dq_s[rows, :] = (dyc * vec(3) * (c_z * sz)).astype(BF16)
            d_u, d_z = hs(9, rows), hs(11, rows)
            sp = sc["sp_s"][rows, :]
            sz = _sig(d_z)
            dyd = dmix_s[rows, 3 * GROUP:4 * GROUP]
            put_dh(9, rows, dyd * sp * (d_z * sz))
            put_dh(11, rows, dyd * d_u * sp * dsilu(d_z, sz))
            sc["dsp_s"][rows, :] = dyd * d_u * (d_z * sz)
        chunks(RC, mixers)

        sc["m1_s"][...] = _segdot(sc["t1_s"][...], segm)
        sc["m2_s"][...] = _segdot(sc["t2_s"][...], segm)
        d_q = dq_s[...]
        pw_acc[...] += _dot_tn(pooled_b, d_q)
        sc["dpool_s"][...] = _dot_nt(d_q, pw_ref[...])
        grp = _lane_group(GROUP)
        for n in range(nblk):
            blk = slice(n * SGU_BLOCK, (n + 1) * SGU_BLOCK)
            dspb = sc["dsp_s"][blk, :]
            dsp_acc[...] += dspb
            dspb16 = dspb.astype(BF16)
            dvst = _dot(wmt_s[...], dspb16)
            dvb = None
            for hh in range(4):
                part = jnp.where(grp == hh, dvst[hh * SGU_BLOCK:(hh + 1) * SGU_BLOCK, :], 0.0)
                dvb = part if dvb is None else dvb + part
            sc["dvd_s"][blk, :] = dvb
            dwc = _dot_nt(dspb16, vst_s[n])
            osm_ref[ROW_WC:ROW_WC + SGU_BLOCK, :] += dwc[:, 0:GROUP]
            osm_ref[ROW_WC + SGU_BLOCK:ROW_WC + 2 * SGU_BLOCK, :] += dwc[:, GROUP:2 * GROUP]

        def ln_sums(rows):
            xh = sc["xh_s"][rows, :]
            d_a1 = sc["ra_s"][rows, :] * (sc["t1_s"][rows, :] - sc["m1_s"][rows, :] - xh * sc["m2_s"][rows, :])
            acc_add(0, d_a1)
            dbuf[rows, :] = d_a1
            pos = tile_idx * T + rows.start + lax.broadcasted_iota(jnp.int32, (RC, GROUP), 0) + 1
            lane = lax.broadcasted_iota(jnp.int32, (RC, GROUP), 1) // HEAD
            win = jnp.where(lane == 0, 2, jnp.where(lane == 1, 4, jnp.where(lane == 2, 8, 16)))
            fbuf[rows, :] = sc["dpool_s"][rows, :] / jnp.minimum(pos, win).astype(F32)
            d_v = sc["dvd_s"][rows, :]
            xv = sc["xv_s"][rows, :]
            acc_add(4, d_v * xv)
            acc_add(5, d_v)
            gd = d_v * vec(4)
            put_dh(10, rows, sc["rv_s"][rows, :] * (gd - _rowmean(gd) - xv * _rowmean(gd * xv)))
        chunks(RC, ln_sums)

        span = T + HALO_A - SUBLANES
        for p in range(1, SUBLANES):
            sh[p - 1, :, :] = dbuf[p:p + span, :]

        for r0 in range(0, T, ROWS):
            uc = sc["ub_s"][r0:r0 + ROWS, :]
            acc = None
            for k in range(KB):
                off = (KB - 1) - k + r0
                w = ebuf[off:off + ROWS, :]
                term = cbw_ref[k:k + 1, :] * w
                acc = term if acc is None else acc + term
                acc_add(ROW_CBW + k, uc * w)
            sc["du_s"][r0:r0 + ROWS, :] = acc
        ebuf[T:T + HALO_B, :] = ebuf[0:HALO_B, :]

        hi_lane = (lax.broadcasted_iota(jnp.int32, (1, 128), 1) // HEAD) == 1
        for r0 in range(0, T, ROWS):
            def win(col, j0, j1):
                s = None
                for j in range(j0, j1):
                    term = fbuf[r0 + j:r0 + j + ROWS, 128 * col:128 * (col + 1)]
                    s = term if s is None else s + term
                return s
            sc["cw_s"][r0:r0 + ROWS, 0:128] = win(0, 0, 2) + jnp.where(hi_lane, win(0, 2, 4), 0.0)
            sc["cw_s"][r0:r0 + ROWS, 128:256] = win(1, 0, 8) + jnp.where(hi_lane, win(1, 8, 16), 0.0)
        fbuf[T:T + HALO_C, :] = fbuf[0:HALO_C, :]

        def rest_bc(rows):
            d_u = sc["du_s"][rows, :]
            put_dh(4, rows, d_u * hs(5, rows))
            put_dh(5, rows, d_u * hs(4, rows))
            put_dh(7, rows, sc["cw_s"][rows, :] - sc["dpool_s"][rows, :])
        chunks(RC, rest_bc)

        dxt_s = sc["dxt_s"]

        def dx_term(k):
            term = _dot_nt(wi_ref[k], dhb_ref[:, COLS * k:COLS * (k + 1)])
            if k == 1:
                dxt_s[...] = term
            else:
                dxt_s[...] += term

        def conv_a(rows):
            a0c = sc["a0_s"][rows, :]
            acc = None
            for k in range(KA):
                off = (KA - 1) - k
                p, q8 = off % SUBLANES, off - off % SUBLANES
                w = dbuf[pl.ds(rows.start + q8, RC), :] if p == 0 else sh[p - 1, pl.ds(rows.start + q8, RC), :]
                term = caw_ref[k:k + 1, :] * w
                acc = term if acc is None else acc + term
                acc_add(ROW_CAW + k, a0c * w)
            sc["u_s"][rows, :] = acc
        n_chunks = T // RC
        after = {(n_chunks * j) // 3: j + 1 for j in range(3)}
        for c in range(n_chunks):
            conv_a(pl.ds(c * RC, RC))
            if c in after:
                dx_term(after[c])
        dbuf[T:T + HALO_A, :] = dbuf[0:HALO_A, :]

        def rest_a(rows):
            d_a0 = sc["u_s"][rows, :]
            sg = sc["sg_s"][rows, :]
            put_dh(0, rows, d_a0 * sg)
            put_dh(1, rows, d_a0 * hs(0, rows) * sg * (1.0 - sg))
        chunks(RC, rest_a)
        dx_term(0)
        dx_ref[...] += dxt_s[...].T

        @pl.when(i == nt - 1)
        def _():
            for row in list(range(6)) + list(range(ROW_CBW, ROW_CBW + KB)) + list(range(ROW_CAW, ROW_CAW + KA)) + list(
                    range(ROW_BIN, ROW_BIN + N_SLICES)):
                osm_ref[row:row + 1, :] = _colsum(acc_s[8 * row:8 * row + 8, :])
            for j, row in enumerate((ROW_LNG, ROW_LNB, ROW_BOUT)):
                cs = _colsum(acc_w[8 * j:8 * j + 8, :])
                for q in range(D_MODEL // GROUP):
                    osm_ref[row + q:row + q + 1, :] = cs[:, GROUP * q:GROUP * (q + 1)]
            r = lax.broadcasted_iota(jnp.int32, (SGU_BLOCK, GROUP), 0) // CHUNK
            c = (lax.broadcasted_iota(jnp.int32, (SGU_BLOCK, GROUP), 1) % SGU_BLOCK) // CHUNK
            for half in range(2):
                rows_ = slice(ROW_WC + half * SGU_BLOCK, ROW_WC + (half + 1) * SGU_BLOCK)
                osm_ref[rows_, :] = jnp.where(c <= r, osm_ref[rows_, :], 0.0)
            sb_t = _segdot(dsp_acc[...], e4_ref[...]).T
            osm_ref[ROW_SB:ROW_SB + 8, 0:SGU_BLOCK] = sb_t[0:8, :]
            for g in range(4):
                osm_ref[ROW_PW:ROW_PW + HEAD, HEAD * g:HEAD * (g + 1)] = (
                    pw_acc[HEAD * g:HEAD * (g + 1), HEAD * g:HEAD * (g + 1)])

    def rows(width):
        return pl.BlockSpec((T, width), lambda i, l: (nt - 1 - i, 0))

    consts = (wi, caw, cbw, s256, seg, pw, wm, wmt, sb, wo, v1024, e4)
    unstacked = (wi, seg, wo, e4)
    in_specs = [rows(D_MODEL), rows(D_MODEL), rows(IN_WIDTH), rows(3 * GROUP)] + [
        _whole(a) if any(a is u for u in unstacked) else _of_layer(a) for a in consts]
    out_specs = [rows(D_MODEL), rows(IN_WIDTH), rows(D_MODEL), pl.BlockSpec((SM_ROWS, GROUP), lambda i, l: (0, 0))]
    out_shape = [jax.ShapeDtypeStruct((S, D_MODEL), F32), jax.ShapeDtypeStruct((S, IN_WIDTH), BF16),
                 jax.ShapeDtypeStruct((S, D_MODEL), BF16), jax.ShapeDtypeStruct((SM_ROWS, GROUP), F32)]
    scratch_shapes = list(scratch.values())
    extra, aliases = (), {}
    if exch is not None:
        extra = tuple(exch)
        r_i, r_o = exch[3], exch[4]
        in_specs += [ANY] * 5
        out_specs += [ANY] * 3
        out_shape += [jax.ShapeDtypeStruct(r_i.shape, r_i.dtype), jax.ShapeDtypeStruct(r_o.shape, r_o.dtype),
                      jax.ShapeDtypeStruct((N_DEV, SM_ROWS, GROUP), F32)]
        scratch_shapes += [pltpu.SemaphoreType.DMA((N_EXCH_SEMS,)), pltpu.SemaphoreType.DMA((N_EXCH_SEMS,)),
                           pltpu.SemaphoreType.DMA((1,))]
        aliases = {20: 4, 21: 5}
    grid_spec = pltpu.PrefetchScalarGridSpec(num_scalar_prefetch=1, grid=(nt,), in_specs=in_specs,
                                             out_specs=out_specs, scratch_shapes=scratch_shapes)
    return pl.pallas_call(
        body, name="bwd_layer" if exch is None else "bwd_layer_exchange",
        grid_spec=grid_spec, out_shape=out_shape, input_output_aliases=aliases,
        compiler_params=_vmem_params(dimension_semantics=("arbitrary",), has_side_effects=exch is not None),
    )(larr, dy, z, h, aux, *consts, *extra)


def _dw(layer, xb, dhb, mixb, dzb, gwi, gwi16, gwo, gwo16, *, k_steps, small=None):
    S = xb.shape[0]
    tk = S // k_steps
    n_steps = N_CHIPS + k_steps

    def body(*refs):
        x_ref, dh_ref, mix_ref, dz_ref = refs[1:5]
        oi_ref, oi16_ref, oo_ref, oo16_ref = refs[n_in:n_in + 4]
        j = pl.program_id(0)
        if small is not None:
            _exchange_comm(j == 0, j == n_steps - 1, None, None, None, refs[9], None, None, refs[n_in + 4],
                           *refs[n_in + 5:])

        @pl.when(j < N_CHIPS)
        def _():
            acc = _dot_tn(x_ref[...], dh_ref[...])
            oi_ref[...] = acc
            oi16_ref[...] = acc.astype(BF16)

        @pl.when(j == N_CHIPS)
        def _():
            oo_ref[...] = jnp.zeros_like(oo_ref)

        @pl.when(j >= N_CHIPS)
        def _():
            oo_ref[...] += _dot_tn(mix_ref[...], dz_ref[...]).reshape(N_CHIPS, GROUP, D_MODEL)

        @pl.when(j == n_steps - 1)
        def _():
            oo16_ref[...] = oo_ref[...].astype(BF16)

    def col_block(j, l):
        return jnp.minimum(j, N_CHIPS - 1)

    def tok_block(j, l):
        return jnp.maximum(j - N_CHIPS, 0)

    oi_spec = pl.BlockSpec((None, None, D_MODEL, COLS), lambda j, l: (l[0], col_block(j, l), 0, 0))
    oo_spec = pl.BlockSpec((None, N_CHIPS, GROUP, D_MODEL), lambda j, l: (l[0], 0, 0, 0))
    in_specs = [pl.BlockSpec((S, D_MODEL), lambda j, l: (0, 0)),
                pl.BlockSpec((S, COLS), lambda j, l: (0, col_block(j, l))),
                pl.BlockSpec((tk, D_MODEL), lambda j, l: (tok_block(j, l), 0)),
                pl.BlockSpec((tk, D_MODEL), lambda j, l: (tok_block(j, l), 0)), ANY, ANY, ANY, ANY]
    out_specs = [oi_spec, oi_spec, oo_spec, oo_spec]
    out_shape = [jax.ShapeDtypeStruct(gwi.shape, F32), jax.ShapeDtypeStruct(gwi.shape, BF16),
                 jax.ShapeDtypeStruct(gwo.shape, F32), jax.ShapeDtypeStruct(gwo.shape, BF16)]
    scratch, extra = [], ()
    if small is not None:
        extra = (small,)
        in_specs += [ANY]
        out_specs += [ANY]
        out_shape += [jax.ShapeDtypeStruct((N_DEV, SM_ROWS, GROUP), F32)]
        scratch = [pltpu.SemaphoreType.DMA((N_EXCH_SEMS,)), pltpu.SemaphoreType.DMA((N_EXCH_SEMS,)), pltpu.SemaphoreType.DMA((1,))]
    n_in = 9 + len(extra)
    grid_spec = pltpu.PrefetchScalarGridSpec(
        num_scalar_prefetch=1, grid=(n_steps,), in_specs=in_specs, out_specs=out_specs, scratch_shapes=scratch)
    return pl.pallas_call(
        body, name="dw" if small is None else "dw_exchange", grid_spec=grid_spec, out_shape=out_shape,
        input_output_aliases={5: 0, 6: 1, 7: 2, 8: 3},
        compiler_params=_vmem_params(dimension_semantics=("arbitrary",), has_side_effects=small is not None),
    )(layer, xb, dhb, mixb, dzb, gwi, gwi16, gwo, gwo16, *extra)


def _adamw_math(w, g, m, v):
    nm = ADAM_B1 * m + (1.0 - ADAM_B1) * g
    nv = ADAM_B2 * v + (1.0 - ADAM_B2) * (g * g)
    c1 = 1.0 - ADAM_B1 ** ADAM_STEP
    c2 = 1.0 - ADAM_B2 ** ADAM_STEP
    return -ADAM_LR * ((nm / c1) / (jnp.sqrt(nv / c2) + ADAM_EPS) + ADAM_WD * w), nm, nv


def _adamw_small(ws, gs, ms, vs):
    n = len(ws)

    def body(*refs):
        for j in range(n):
            d, nm, nv = _adamw_math(*(refs[k * n + j][...] for k in range(4)))
            refs[4 * n + j][...] = d
            refs[5 * n + j][...] = nm
            refs[6 * n + j][...] = nv

    shapes = [jax.ShapeDtypeStruct(w.shape, F32) for w in ws]
    outs = pl.pallas_call(body, name="adamw_small", out_shape=shapes * 3, compiler_params=_vmem_params())(
        *ws, *gs, *ms, *vs)
    return outs[0:n], outs[n:2 * n], outs[2 * n:3 * n]


def _adamw(w, g, m, v, *, rows_per_step, name, copy_g=False):
    R, C = w.shape
    tr = rows_per_step

    def body(w_ref, g_ref, m_ref, v_ref, d_ref, nm_ref, nv_ref, *g_out):
        g_ = g_ref[...]
        d_ref[...], nm_ref[...], nv_ref[...] = _adamw_math(w_ref[...], g_, m_ref[...], v_ref[...])
        if copy_g:
            g_out[0][...] = g_

    spec = pl.BlockSpec((tr, C), lambda i: (i, 0))
    n_out = 4 if copy_g else 3
    return pl.pallas_call(
        body, name=name, grid=(R // tr,),
        in_specs=[spec] * 4, out_specs=[spec] * n_out,
        out_shape=[jax.ShapeDtypeStruct((R, C), F32)] * n_out,
        compiler_params=_vmem_params(dimension_semantics=("arbitrary",)),
    )(w, g, m, v)


def _gather_weights(wi16, wo16, cw):
    L = wi16.shape[0]
    hi_rows, ho_rows = D_MODEL // 2, GROUP // 2
    n_ici = 2 * L + 1
    n_fwd = 2 * L

    def body(wi_ref, wo_ref, cw_ref, *rest):
        wig = rest[0:L]
        wog = rest[L:2 * L]
        cwg = rest[2 * L]
        send_sems, recv_sems, loc_sems, vwi, vwo, vcw = rest[2 * L + 1:]
        x, y, c = _place()
        me_k = 2 * x + y
        sibling = (x, y, 1 - c)
        chips = _other_chips(x, y)

        def half_i(ref, blk):
            return ref.at[blk, pl.ds(c * hi_rows, hi_rows), :]

        def half_o(ref, blk):
            return ref.at[blk, pl.ds(c * ho_rows, ho_rows), :]

        def other_half_i(ref, blk):
            return ref.at[blk, pl.ds((1 - c) * hi_rows, hi_rows), :]

        def other_half_o(ref, blk):
            return ref.at[blk, pl.ds((1 - c) * ho_rows, ho_rows), :]

        stage_in = [pltpu.make_async_copy(wi_ref, vwi, loc_sems.at[0]), pltpu.make_async_copy(wo_ref, vwo, loc_sems.at[1]),
                    pltpu.make_async_copy(cw_ref, vcw, loc_sems.at[2])]
        local = []
        for l in range(L):
            local.append(pltpu.make_async_copy(vwi.at[l], wig[l].at[me_k], loc_sems.at[3 + 2 * l]))
            local.append(pltpu.make_async_copy(vwo.at[l], wog[l].at[me_k], loc_sems.at[3 + 2 * l + 1]))
        local.append(pltpu.make_async_copy(vcw, cwg.at[me_k], loc_sems.at[3 + 2 * L]))
        for cp in stage_in:
            cp.start()

        def remote(src, dst, sem, to):
            return pltpu.make_async_remote_copy(src_ref=src, dst_ref=dst, send_sem=send_sems.at[sem],
                                                recv_sem=recv_sems.at[sem], device_id=to, device_id_type=MESH)

        sends = []
        for r, (px, py, _) in enumerate(chips):
            to = (px, py, c)
            for l in range(L):
                sends.append(remote(half_i(wi_ref, l), half_i(wig[l], me_k), r * n_ici + 2 * l, to))
                sends.append(remote(half_o(wo_ref, l), half_o(wog[l], me_k), r * n_ici + 2 * l + 1, to))
            sends.append(remote(cw_ref, cwg.at[me_k], r * n_ici + 2 * L, to))
        for cp in sends:
            cp.start()
        for cp in stage_in:
            cp.wait()
        for cp in local:
            cp.start()

        base = 3 * n_ici
        fwds = []
        for r, (px, py, pk) in enumerate(chips):
            for l in range(L):
                remote(half_i(wig[l], pk), half_i(wig[l], pk), r * n_ici + 2 * l, sibling).wait_recv()
                f = remote(half_i(wig[l], pk), half_i(wig[l], pk), base + r * n_fwd + 2 * l, sibling)
                f.start()
                fwds.append(f)
                remote(half_o(wog[l], pk), half_o(wog[l], pk), r * n_ici + 2 * l + 1, sibling).wait_recv()
                f = remote(half_o(wog[l], pk), half_o(wog[l], pk), base + r * n_fwd + 2 * l + 1, sibling)
                f.start()
                fwds.append(f)
            remote(cwg.at[pk], cwg.at[pk], r * n_ici + 2 * L, sibling).wait_recv()
        for r, (px, py, pk) in enumerate(chips):
            for l in range(L):
                remote(other_half_i(wig[l], pk), other_half_i(wig[l], pk), base + r * n_fwd + 2 * l, sibling).wait_recv()
                remote(other_half_o(wog[l], pk), other_half_o(wog[l], pk), base + r * n_fwd + 2 * l + 1, sibling).wait_recv()
        for cp in sends + fwds:
            cp.wait_send()
        for cp in local:
            cp.wait()

    n_sem = 3 * n_ici + 3 * n_fwd
    out_shape = ([jax.ShapeDtypeStruct((N_CHIPS, D_MODEL, COLS), BF16)] * L
                 + [jax.ShapeDtypeStruct((N_CHIPS, GROUP, D_MODEL), BF16)] * L
                 + [jax.ShapeDtypeStruct((N_CHIPS,) + cw.shape, F32)])
    outs = pl.pallas_call(
        body, name="gather_weights",
        in_specs=[ANY, ANY, ANY], out_specs=[ANY] * (2 * L + 1), out_shape=out_shape,
        scratch_shapes=[pltpu.SemaphoreType.DMA((n_sem,)), pltpu.SemaphoreType.DMA((n_sem,)),
                        pltpu.SemaphoreType.DMA((2 * L + 4,)), pltpu.VMEM(wi16.shape, BF16), pltpu.VMEM(wo16.shape, BF16),
                        pltpu.VMEM(cw.shape, F32)],
        compiler_params=_vmem_params(has_side_effects=True),
    )(wi16, wo16, cw)
    return outs[0:L], outs[L:2 * L], outs[2 * L]


def _swap_add(cl_arr, g_i, g16_i, p_i, g_o, g16_o, p_o, *, send_on=None):
    hi, ho = p_i.shape[2], p_o.shape[2]
    n_in = 7 + (2 if send_on is not None else 0)
    n_out = 2 + (2 if send_on is not None else 0)

    def body(*refs):
        cl_ref, gi_ref, gi16_ref, _, go_ref, go16_ref = refs[0:6]
        oi_ref, oo_ref = refs[n_in:n_in + 2]
        ri_v, ro_v, send_sems, recv_sems = refs[n_in + n_out:n_in + n_out + 4]
        k = pl.program_id(0)
        x, y, c = _place()
        l = cl_ref[1]

        def copies(kk):
            pair = ((gi16_ref, hi, ri_v), (go16_ref, ho, ro_v))
            return [pltpu.make_async_remote_copy(
                src_ref=src.at[l, kk, pl.ds((1 - c) * n, n), :], dst_ref=dst.at[kk], send_sem=send_sems.at[2 * kk + j],
                recv_sem=recv_sems.at[2 * kk + j], device_id=(x, y, 1 - c), device_id_type=MESH)
                for j, (src, n, dst) in enumerate(pair)]

        @pl.when(k == 0)
        def _():
            for kk in range(N_CHIPS):
                for cp in copies(kk):
                    cp.start()

        for cp in copies(k):
            cp.wait_recv()
        pi_k = (gi_ref[...] + ri_v[k].astype(F32)).astype(oi_ref.dtype)
        po_k = (go_ref[...] + ro_v[k].astype(F32)).astype(oo_ref.dtype)
        oi_ref[...] = pi_k
        oo_ref[...] = po_k

        if send_on is not None:
            qi_ref, qo_ref = refs[n_in + 2:n_in + 4]
            pv_i, pv_o, out_sems, in_sems = refs[n_in + n_out + 4:]
            pv_i[k] = pi_k
            pv_o[k] = po_k
            chips = _other_chips(x, y)

            def onward(r):
                px, py, pk = chips[r]
                return [pltpu.make_async_remote_copy(
                    src_ref=pv.at[pk], dst_ref=q.at[r, l], send_sem=out_sems.at[2 * r + j], recv_sem=in_sems.at[2 * r + j],
                    device_id=(px, py, c), device_id_type=MESH) for j, (pv, q) in enumerate(((pv_i, qi_ref), (pv_o, qo_ref)))]

            for r in range(3):
                @pl.when(k == chips[r][2])
                def _():
                    for cp in onward(r):
                        cp.start()

        @pl.when(k == N_CHIPS - 1)
        def _():
            for kk in range(N_CHIPS):
                for cp in copies(kk):
                    cp.wait_send()
            if send_on is not None:
                for r in range(3):
                    for cp in onward(r):
                        cp.wait()

    def specs(p):
        rows, cols = p.shape[2], p.shape[3]
        mine = pl.BlockSpec((None, None, rows, cols), lambda k, cl: (cl[1], k, cl[0], 0))
        out = pl.BlockSpec((None, None, rows, cols), lambda k, cl: (cl[1], k, 0, 0))
        return mine, out

    (gi_s, pi_s), (go_s, po_s) = specs(p_i), specs(p_o)
    in_specs = [gi_s, ANY, ANY, go_s, ANY, ANY]
    out_specs = [pi_s, po_s]
    out_shape = [jax.ShapeDtypeStruct(p_i.shape, p_i.dtype), jax.ShapeDtypeStruct(p_o.shape, p_o.dtype)]
    scratch = [pltpu.VMEM((N_CHIPS, hi, p_i.shape[3]), BF16), pltpu.VMEM((N_CHIPS, ho, p_o.shape[3]), BF16),
               pltpu.SemaphoreType.DMA((2 * N_CHIPS,)), pltpu.SemaphoreType.DMA((2 * N_CHIPS,))]
    extra, aliases = (), {3: 0, 6: 1}
    if send_on is not None:
        extra = tuple(send_on)
        in_specs += [ANY, ANY]
        out_specs += [ANY, ANY]
        out_shape += [jax.ShapeDtypeStruct(q.shape, q.dtype) for q in send_on]
        scratch += [pltpu.VMEM((N_CHIPS, hi, p_i.shape[3]), BF16), pltpu.VMEM((N_CHIPS, ho, p_o.shape[3]), BF16),
                    pltpu.SemaphoreType.DMA((6,)), pltpu.SemaphoreType.DMA((6,))]
        aliases = {3: 0, 6: 1, 7: 2, 8: 3}
    grid_spec = pltpu.PrefetchScalarGridSpec(num_scalar_prefetch=1, grid=(N_CHIPS,), in_specs=in_specs,
                                             out_specs=out_specs, scratch_shapes=scratch)
    return pl.pallas_call(
        body, name="swap_add" if send_on is None else "swap_add_send", grid_spec=grid_spec, out_shape=out_shape,
        input_output_aliases=aliases,
        compiler_params=_vmem_params(dimension_semantics=("arbitrary",), has_side_effects=True),
    )(cl_arr, g_i, g16_i, p_i, g_o, g16_o, p_o, *extra)


def _sum_small(r_sms):
    L = len(r_sms)

    def body(*refs):
        o_ref = refs[L]
        for l in range(L):
            acc = refs[l][0]
            for d in range(1, N_DEV):
                acc = acc + refs[l][d]
            o_ref[l] = acc

    return pl.pallas_call(
        body, name="sum_small",
        out_shape=jax.ShapeDtypeStruct((L,) + r_sms[0].shape[1:], F32),
        compiler_params=_vmem_params(),
    )(*r_sms)


def _sum_chunks(kc_arr, p_i, q_i, p_o, q_o, *, nb):
    L = p_i.shape[0]

    def body(kc_ref, pi_ref, a0, a1, a2, po_ref, b0, b1, b2, oi_ref, oo_ref):
        del kc_ref
        f = lambda ref: ref[...].astype(F32)
        oi_ref[...] = ((f(pi_ref) + f(a0)) + f(a1)) + f(a2)
        oo_ref[...] = ((f(po_ref) + f(b0)) + f(b1)) + f(b2)

    def specs(p):
        tr, cols = p.shape[2] // nb, p.shape[3]
        chunk = pl.BlockSpec((None, None, tr, cols), lambda l, i, kc: (l, kc[0], i, 0))
        got = [pl.BlockSpec((None, None, tr, cols), lambda l, i, kc, _j=j: (_j, l, i, 0)) for j in range(3)]
        out = pl.BlockSpec((None, tr, cols), lambda l, i, kc: (l, kc[1] * nb + i, 0))
        return [chunk] + got, out

    (in_i, out_i), (in_o, out_o) = specs(p_i), specs(p_o)
    grid_spec = pltpu.PrefetchScalarGridSpec(num_scalar_prefetch=1, grid=(L, nb), in_specs=in_i + in_o,
                                             out_specs=[out_i, out_o])
    return pl.pallas_call(
        body, name="sum_chunks", grid_spec=grid_spec,
        out_shape=[jax.ShapeDtypeStruct((L, 2 * p.shape[2], p.shape[3]), F32) for p in (p_i, p_o)],
        compiler_params=_vmem_params(dimension_semantics=("arbitrary",) * 2),
    )(kc_arr, p_i, q_i, q_i, q_i, p_o, q_o, q_o, q_o)


def _share_result(gi, go):
    hi_rows, ho_rows = gi.shape[1] // 2, go.shape[1] // 2

    def body(gi_ref, go_ref, oi_ref, oo_ref, send_sems, recv_sems):
        del gi_ref, go_ref
        x, y, c = _place()
        sibling = (x, y, 1 - c)
        cps = []
        for j, (ref, n) in enumerate(((oi_ref, hi_rows), (oo_ref, ho_rows))):
            mine = ref.at[:, pl.ds(c * n, n), :]
            cps.append(pltpu.make_async_remote_copy(src_ref=mine, dst_ref=mine, send_sem=send_sems.at[j],
                                                    recv_sem=recv_sems.at[j], device_id=sibling, device_id_type=MESH))
        for cp in cps:
            cp.start()
        for j, (ref, n) in enumerate(((oi_ref, hi_rows), (oo_ref, ho_rows))):
            theirs = ref.at[:, pl.ds((1 - c) * n, n), :]
            pltpu.make_async_remote_copy(src_ref=theirs, dst_ref=theirs, send_sem=send_sems.at[j],
                                         recv_sem=recv_sems.at[j], device_id=sibling, device_id_type=MESH).wait_recv()
        for cp in cps:
            cp.wait_send()

    return pl.pallas_call(
        body, name="share_result",
        in_specs=[ANY, ANY], out_specs=[ANY, ANY],
        out_shape=[jax.ShapeDtypeStruct(gi.shape, F32), jax.ShapeDtypeStruct(go.shape, F32)],
        input_output_aliases={0: 0, 1: 1},
        scratch_shapes=[pltpu.SemaphoreType.DMA((2,)), pltpu.SemaphoreType.DMA((2,))],
        compiler_params=pltpu.CompilerParams(has_side_effects=True),
    )(gi, go)


WEIGHTS = ("ln_g", "ln_b", "w_in", "b_in", "conv_a_w", "conv_a_b", "norm_a_g", "norm_a_b", "conv_b_w", "pool_w",
           "pool_scale", "sgu_ln_g", "sgu_ln_b", "sgu_w", "sgu_bias", "w_out", "b_out")


def _pad_rows(a, rows):
    return jnp.pad(a, ((0, rows - a.shape[0]), (0, 0)))


def _indicator_consts():
    seg = jnp.where((jnp.arange(GROUP)[:, None] // HEAD) == (jnp.arange(GROUP)[None, :] // HEAD),
                    1.0 / HEAD, 0.0).astype(BF16)
    e4 = ((jnp.arange(GROUP)[:, None] // HEAD) == jnp.arange(128)[None, :]).astype(BF16)
    return seg, e4


def _layer_consts(p, conv_full):
    L = conv_full.shape[0]
    same_head = jnp.eye(4, dtype=F32)[:, None, :, None] > 0

    def rows_to(a, rows):
        return jnp.pad(a, ((0, 0), (0, rows - a.shape[1]), (0, 0)))

    s256 = jnp.stack([p[n] for n in ("conv_a_b", "norm_a_g", "norm_a_b", "pool_scale", "sgu_ln_g", "sgu_ln_b")], axis=1)
    pw = jnp.where(same_head, p["pool_w"][:, :, :, None, :], 0.0).reshape(L, GROUP, GROUP)
    return dict(
        caw=rows_to(conv_full[:, :KA], 32), cbw=rows_to(conv_full[:, KA:], 8), s256=rows_to(s256, 8),
        pw=pw.astype(BF16),
        wm=jnp.transpose(p["sgu_w"], (0, 2, 1, 3)).reshape(L, SGU_BLOCK, 4 * SGU_BLOCK),
        wmt=jnp.transpose(p["sgu_w"], (0, 1, 3, 2)).reshape(L, 4 * SGU_BLOCK, SGU_BLOCK),
        sb=jnp.repeat(jnp.transpose(p["sgu_bias"], (0, 2, 1)), HEAD, axis=2),
        v1024=rows_to(jnp.stack([p["b_out"], p["ln_g"], p["ln_b"]], axis=1), 8),
        bin=p["b_in"][:, None, :])


def _unpack_small(sm):
    L = sm.shape[0]
    owc = jnp.concatenate([sm[:, ROW_WC:ROW_WC + SGU_BLOCK], sm[:, ROW_WC + SGU_BLOCK:ROW_WC + 2 * SGU_BLOCK]], axis=2)
    return dict(
        conv_a_b=sm[:, 0], norm_a_g=sm[:, 1], norm_a_b=sm[:, 2], pool_scale=sm[:, 3], sgu_ln_g=sm[:, 4],
        sgu_ln_b=sm[:, 5], conv_b_w=sm[:, ROW_CBW:ROW_CBW + KB], conv_a_w=sm[:, ROW_CAW:ROW_CAW + KA],
        pool_w=jnp.transpose(sm[:, ROW_PW:ROW_PW + HEAD].reshape(L, HEAD, 4, HEAD), (0, 2, 1, 3)),
        ln_g=sm[:, ROW_LNG:ROW_LNG + 4].reshape(L, D_MODEL), ln_b=sm[:, ROW_LNB:ROW_LNB + 4].reshape(L, D_MODEL),
        b_out=sm[:, ROW_BOUT:ROW_BOUT + 4].reshape(L, D_MODEL),
        b_in=sm[:, ROW_BIN:ROW_BIN + N_SLICES].reshape(L, IN_WIDTH),
        sgu_w=jnp.transpose(owc.reshape(L, SGU_BLOCK, 4, SGU_BLOCK), (0, 2, 1, 3)),
        sgu_bias=sm[:, ROW_SB:ROW_SB + 4, 0:SGU_BLOCK])


def _step(p, m, v, x, target, *, tile_f, tile_b, k_steps):
    L = p["ln_g"].shape[0]
    xi, yi, ci = _place()
    me_k = 2 * xi + yi
    hi_rows, ho_rows = D_MODEL // 2, GROUP // 2

    cw = jnp.concatenate([p["conv_a_w"], p["conv_b_w"]], axis=1).reshape(-1, 128)
    cw_rows = cw.shape[0]
    cw = _pad_rows(cw, -(-cw_rows // SUBLANES) * SUBLANES)
    wi16 = p["w_in"].astype(BF16)
    wo16 = p["w_out"].astype(BF16)
    wig0, wog0, cwg = _gather_weights(wi16[0:1], wo16[0:1], cw)
    cwg = cwg[:, :cw_rows].reshape(N_CHIPS, L, KA + KB, HEAD)
    conv_full = jnp.transpose(cwg, (1, 2, 0, 3)).reshape(L, KA + KB, GROUP)
    seg, e4 = _indicator_consts()
    k = _layer_consts(p, conv_full)
    layer = [jnp.full((1,), l, jnp.int32) for l in range(L)]

    hcur = x
    saved, wig, wog = [], [wig0[0]], [wog0[0]]
    for l in range(L):
        nxt = (wi16, wo16) if l + 1 < L else None
        outs = _fwd_layer(layer[l], hcur, wig[l], k["bin"], k["caw"], k["cbw"], k["s256"], seg, k["pw"], k["wm"], k["sb"],
                          wog[l], k["v1024"], tile=tile_f, nxt=nxt, target=None if nxt is not None else target)
        y, xb, h, aux, mixb, z = outs[0:6]
        if nxt is not None:
            wig.append(outs[6])
            wog.append(outs[7])
        saved.append((xb, h, aux, mixb, z))
        hcur = y

    dy = hcur
    loss_local = outs[6][0, 0]

    gwi = lax.empty((L, N_CHIPS, D_MODEL, COLS), F32)
    gwo = lax.empty((L, N_CHIPS, GROUP, D_MODEL), F32)
    gwi16 = lax.empty((L, N_CHIPS, D_MODEL, COLS), BF16)
    gwo16 = lax.empty((L, N_CHIPS, GROUP, D_MODEL), BF16)
    p_i = lax.empty((L, N_CHIPS, hi_rows, COLS), BF16)
    p_o = lax.empty((L, N_CHIPS, ho_rows, D_MODEL), BF16)
    q_i = lax.empty((3, L, hi_rows, COLS), BF16)
    q_o = lax.empty((3, L, ho_rows, D_MODEL), BF16)
    r_sm = [None] * L
    pending = None
    for l in reversed(range(L)):
        xb, h, aux, mixb, z = saved[l]
        exch = None if pending is None else (p_i, p_o, pending, q_i, q_o)
        outs = _bwd_layer(layer[l], dy, z, h, aux, wig[l], k["caw"], k["cbw"], k["s256"], seg, k["pw"], k["wm"],
                          k["wmt"], k["sb"], wog[l], k["v1024"], e4, tile=tile_b, exch=exch)
        dy, dhb, dzb, osm = outs[0:4]
        if l == L - 1:
            osm = osm.at[ROW_LOSS, 0].set(loss_local)
        if exch is not None:
            q_i, q_o, r_sm[l + 1] = outs[4:7]
        larr = layer[l]
        outs = _dw(larr, xb, dhb, mixb, dzb, gwi, gwi16, gwo, gwo16, k_steps=k_steps, small=osm if l == 0 else None)
        gwi, gwi16, gwo, gwo16 = outs[0:4]
        if l == 0:
            r_sm[0] = outs[4]
        cl_arr = jnp.stack([ci, jnp.int32(l)]).astype(jnp.int32)
        if l > 0:
            p_i, p_o = _swap_add(cl_arr, gwi, gwi16, p_i, gwo, gwo16, p_o)
        else:
            p_i, p_o, q_i, q_o = _swap_add(cl_arr, gwi, gwi16, p_i, gwo, gwo16, p_o, send_on=(q_i, q_o))
        pending = osm
    grad_x = dy

    summed = _sum_small(r_sm)
    loss = summed[L - 1, ROW_LOSS, 0]
    grads = _unpack_small(summed)
    for n in ("conv_a_w", "conv_b_w"):
        grads[n] = lax.dynamic_slice_in_dim(grads[n], me_k * HEAD, HEAD, axis=2)

    kc_arr = jnp.stack([me_k, ci]).astype(jnp.int32)
    g_i, g_o = _sum_chunks(kc_arr, p_i, q_i, p_o, q_o, nb=2)
    g_i, g_o = _share_result(g_i, g_o)
    grads["w_in"] = g_i
    grads["w_out"] = g_o

    delta, new_m, new_v = {}, {}, {}
    for n, tr in (("w_in", 512), ("w_out", 256)):
        shp = p[n].shape
        args = [a.reshape(shp[0] * shp[1], shp[2]) for a in (p[n], grads[n], m[n], v[n])]
        outs = _adamw(*args, rows_per_step=tr, name="adamw_" + n, copy_g=True)
        delta[n], new_m[n], new_v[n], grads[n] = (a.reshape(shp) for a in outs)
    small = [n for n in WEIGHTS if n not in ("w_in", "w_out")]
    flat = [[a[n].reshape(-1, a[n].shape[-1]) for n in small] for a in (p, grads, m, v)]
    outs = _adamw_small(*flat)
    for j, n in enumerate(small):
        delta[n], new_m[n], new_v[n] = (o[j].reshape(p[n].shape) for o in outs)

    return (loss, grad_x[None], *[grads[n] for n in WEIGHTS], *[delta[n] for n in WEIGHTS],
            *[new_m[n] for n in WEIGHTS], *[new_v[n] for n in WEIGHTS])


def kernel(x, ln_g, ln_b, w_in, b_in, conv_a_w, conv_a_b, norm_a_g, norm_a_b, conv_b_w, pool_w, pool_scale, sgu_ln_g, sgu_ln_b, sgu_w, sgu_bias, w_out, b_out, loss_target, m_ln_g, m_ln_b, m_w_in, m_b_in, m_conv_a_w, m_conv_a_b, m_norm_a_g, m_norm_a_b, m_conv_b_w, m_pool_w, m_pool_scale, m_sgu_ln_g, m_sgu_ln_b, m_sgu_w, m_sgu_bias, m_w_out, m_b_out, v_ln_g, v_ln_b, v_w_in, v_b_in, v_conv_a_w, v_conv_a_b, v_norm_a_g, v_norm_a_b, v_conv_b_w, v_pool_w, v_pool_scale, v_sgu_ln_g, v_sgu_ln_b, v_sgu_w, v_sgu_bias, v_w_out, v_b_out):
    p = dict(ln_g=ln_g, ln_b=ln_b, w_in=w_in, b_in=b_in, conv_a_w=conv_a_w, conv_a_b=conv_a_b, norm_a_g=norm_a_g,
             norm_a_b=norm_a_b, conv_b_w=conv_b_w, pool_w=pool_w, pool_scale=pool_scale, sgu_ln_g=sgu_ln_g,
             sgu_ln_b=sgu_ln_b, sgu_w=sgu_w, sgu_bias=sgu_bias, w_out=w_out, b_out=b_out)
    m = dict(ln_g=m_ln_g, ln_b=m_ln_b, w_in=m_w_in, b_in=m_b_in, conv_a_w=m_conv_a_w, conv_a_b=m_conv_a_b,
             norm_a_g=m_norm_a_g, norm_a_b=m_norm_a_b, conv_b_w=m_conv_b_w, pool_w=m_pool_w, pool_scale=m_pool_scale,
             sgu_ln_g=m_sgu_ln_g, sgu_ln_b=m_sgu_ln_b, sgu_w=m_sgu_w, sgu_bias=m_sgu_bias, w_out=m_w_out, b_out=m_b_out)
    v = dict(ln_g=v_ln_g, ln_b=v_ln_b, w_in=v_w_in, b_in=v_b_in, conv_a_w=v_conv_a_w, conv_a_b=v_conv_a_b,
             norm_a_g=v_norm_a_g, norm_a_b=v_norm_a_b, conv_b_w=v_conv_b_w, pool_w=v_pool_w, pool_scale=v_pool_scale,
             sgu_ln_g=v_sgu_ln_g, sgu_ln_b=v_sgu_ln_b, sgu_w=v_sgu_w, sgu_bias=v_sgu_bias, w_out=v_w_out, b_out=v_b_out)
    return _step(p, m, v, x[0], loss_target[0], tile_f=256, tile_b=256, k_steps=4)
```

```python
import jax
import jax.numpy as jnp
from jax import lax
from jax.experimental import pallas as pl
from jax.experimental.pallas import tpu as pltpu

F32 = jnp.float32
BF16 = jnp.bfloat16
MESH = pl.DeviceIdType.MESH

D_MODEL = 1024
GROUP = 256
HEAD = 64
N_SLICES = 12
IN_WIDTH = N_SLICES * GROUP
N_CHIPS = 4
COLS = IN_WIDTH // N_CHIPS
KA = 31
KB = 3
SUBLANES = 8
HALO_A, HALO_B, HALO_C = 32, 8, 16
N_GATHER_SEMS = 12
N_EXCH_SEMS = 13
SGU_BLOCK = 128
CHUNK = 64
LN_EPS = 1e-5
ROWS = 64
V7X_VMEM_BYTES = 64 * 1024 * 1024
VMEM_LIMIT = V7X_VMEM_BYTES - 8 * 1024 * 1024

ADAM_LR, ADAM_B1, ADAM_B2, ADAM_EPS, ADAM_WD, ADAM_STEP = 0.001, 0.9, 0.999, 1e-08, 0.01, 10


ANY = pl.BlockSpec(memory_space=pl.ANY)


def _vmem_params(**kw):
    return pltpu.CompilerParams(vmem_limit_bytes=VMEM_LIMIT, **kw)


def _whole(a):
    return pl.BlockSpec(a.shape, lambda i, l, _n=a.ndim: (0,) * _n)


def _of_layer(a):
    return pl.BlockSpec((None,) + a.shape[1:], lambda i, l, _n=a.ndim: (l[0],) + (0,) * (_n - 1))


def _place():
    return lax.axis_index("x"), lax.axis_index("y"), lax.axis_index("c")


def _other_chips(x, y):
    return [(1 - x, y, 2 * (1 - x) + y), (x, 1 - y, 2 * x + (1 - y)), (1 - x, 1 - y, 2 * (1 - x) + (1 - y))]


def _sibling_handshake():
    x, y, c = _place()
    barrier = pltpu.get_barrier_semaphore()
    pl.semaphore_signal(barrier, inc=1, device_id=(x, y, 1 - c), device_id_type=MESH)
    pl.semaphore_wait(barrier, 1)


def _sig(v):
    return 0.5 * jnp.tanh(0.5 * v) + 0.5


def _dot(a, b):
    return jnp.dot(a, b, preferred_element_type=F32)


def _dot_nt(a, b):
    return lax.dot_general(a, b, (((1,), (1,)), ((), ())), preferred_element_type=F32)


def _dot_tn(a, b):
    return lax.dot_general(a, b, (((0,), (0,)), ((), ())), preferred_element_type=F32)


def _segdot(v, m):
    hi = v.astype(BF16)
    lo = (v - hi.astype(F32)).astype(BF16)
    return _dot(hi, m) + _dot(lo, m)


def _colsum(v):
    return jnp.sum(v, axis=0, keepdims=True)


def _rowmean(v):
    return jnp.mean(v, axis=-1, keepdims=True)


def _lane_group(n):
    return lax.broadcasted_iota(jnp.int32, (1, n), 1) // HEAD


def _pool_cnt(tile, t_rows):
    pos = tile * t_rows + lax.broadcasted_iota(jnp.int32, (t_rows, GROUP), 0) + 1
    grp = lax.broadcasted_iota(jnp.int32, (t_rows, GROUP), 1) // HEAD
    win = jnp.where(grp == 0, 2, jnp.where(grp == 1, 4, jnp.where(grp == 2, 8, 16)))
    return jnp.minimum(pos, win).astype(F32)


def _sgu_masks(wm_ref, wmt_ref, wm_s, wmt_s):
    r = lax.broadcasted_iota(jnp.int32, (SGU_BLOCK, 4 * SGU_BLOCK), 0) // CHUNK
    c = (lax.broadcasted_iota(jnp.int32, (SGU_BLOCK, 4 * SGU_BLOCK), 1) % SGU_BLOCK) // CHUNK
    wm_s[...] = jnp.where(c <= r, wm_ref[...], 0.0).astype(BF16)
    if wmt_ref is not None:
        rt = (lax.broadcasted_iota(jnp.int32, (4 * SGU_BLOCK, SGU_BLOCK), 0) % SGU_BLOCK) // CHUNK
        ct = lax.broadcasted_iota(jnp.int32, (4 * SGU_BLOCK, SGU_BLOCK), 1) // CHUNK
        wmt_s[...] = jnp.where(rt <= ct, wmt_ref[...], 0.0).astype(BF16)


def _vstack(v_blk):
    grp = _lane_group(GROUP)
    return jnp.concatenate([jnp.where(grp == h, v_blk, 0.0) for h in range(4)], axis=0).astype(BF16)


def _gather_next(step, nt, nwi, nwo, gwi, gwo, send_sems, recv_sems, loc_sems, vwi, vwo):
    x, y, c = _place()
    me_k = 2 * x + y
    sibling = (x, y, 1 - c)
    chips = _other_chips(x, y)
    hi, ho = D_MODEL // 2, GROUP // 2
    fwd_sems = N_GATHER_SEMS // 2

    def rc(src, dst, sem, to):
        return pltpu.make_async_remote_copy(src_ref=src, dst_ref=dst, send_sem=send_sems.at[sem],
                                            recv_sem=recv_sems.at[sem], device_id=to, device_id_type=MESH)

    def blk(ref, k, n, cc):
        return ref.at[k, pl.ds(cc * n, n), :]

    def ici(r):
        px, py, _ = chips[r]
        to = (px, py, c)
        return [rc(nwi.at[pl.ds(c * hi, hi), :], blk(gwi, me_k, hi, c), 2 * r, to),
                rc(nwo.at[pl.ds(c * ho, ho), :], blk(gwo, me_k, ho, c), 2 * r + 1, to)]

    def landed(r, cc, base):
        pk = chips[r][2]
        return [rc(blk(gwi, pk, hi, cc), blk(gwi, pk, hi, cc), base + 2 * r, sibling),
                rc(blk(gwo, pk, ho, cc), blk(gwo, pk, ho, cc), base + 2 * r + 1, sibling)]

    def stage_in():
        return [pltpu.make_async_copy(nwi, vwi, loc_sems.at[0]), pltpu.make_async_copy(nwo, vwo, loc_sems.at[1])]

    def local():
        return [pltpu.make_async_copy(vwi, gwi.at[me_k], loc_sems.at[2]),
                pltpu.make_async_copy(vwo, gwo.at[me_k], loc_sems.at[3])]

    @pl.when(step == 0)
    def _():
        for cp in stage_in():
            cp.start()
        for r in range(3):
            for cp in ici(r):
                cp.start()

    @pl.when(step == 1)
    def _():
        for cp in stage_in():
            cp.wait()
        for cp in local():
            cp.start()

    @pl.when(step == (3 * nt) // 4)
    def _():
        for r in range(3):
            for got, fwd in zip(landed(r, c, 0), landed(r, c, fwd_sems)):
                got.wait_recv()
                fwd.start()

    @pl.when(step == nt - 1)
    def _():
        for r in range(3):
            for got in landed(r, 1 - c, fwd_sems):
                got.wait_recv()
        for r in range(3):
            for cp in ici(r) + landed(r, c, fwd_sems):
                cp.wait_send()
        for cp in local():
            cp.wait()


def _fwd_layer(larr, x, wi, bin_, caw, cbw, s256, seg, pw, wm, sb, wo, v1024, *, tile, nxt=None, target=None):
    assert nxt is None or target is None
    S = x.shape[0]
    T = tile
    nt = S // T
    alpha = float((2.0 * 4) ** 0.25)
    n_in = 13 + (2 if nxt is not None else 0) + (1 if target is not None else 0)
    n_out = 6 + (2 if nxt is not None else 0) + (1 if target is not None else 0)

    def body(*refs):
        l_ref = refs[0]
        (x_ref, wi_ref, bin_ref, caw_ref, cbw_ref, s256_ref, seg_ref, pw_ref, wm_ref, sb_ref, wo_ref,
         v1024_ref) = refs[1:13]
        y_ref, xb_ref, h_ref, aux_ref, mix_ref, z_ref = refs[n_in:n_in + 6]
        abuf, bbuf, cbuf, wm_s, shf = refs[n_in + n_out:n_in + n_out + 5]
        i = pl.program_id(0)
        if nxt is not None:
            _gather_next(i, nt, refs[13].at[l_ref[0] + 1], refs[14].at[l_ref[0] + 1], refs[n_in + 6], refs[n_in + 7],
                         *refs[n_in + n_out + 5:])

        @pl.when(i == 0)
        def _():
            abuf[0:HALO_A, :] = jnp.zeros((HALO_A, GROUP), F32)
            bbuf[0:HALO_B, :] = jnp.zeros((HALO_B, GROUP), F32)
            cbuf[0:HALO_C, :] = jnp.zeros((HALO_C, GROUP), F32)
            _sgu_masks(wm_ref, None, wm_s, None)

        x = x_ref[...]
        xb = x.astype(BF16)
        xb_ref[...] = xb
        for k in range(N_CHIPS):
            h_ref[:, COLS * k:COLS * (k + 1)] = _dot(xb, wi_ref[k]) + bin_ref[:, COLS * k:COLS * (k + 1)]

        def hs(j):
            return h_ref[:, GROUP * j:GROUP * (j + 1)]

        abuf[HALO_A:HALO_A + T, :] = hs(0) * _sig(hs(1))
        span = T + HALO_A - SUBLANES
        for p in range(1, SUBLANES):
            shf[p - 1, :, :] = abuf[p:p + span, :]
        for r0 in range(0, T, ROWS):
            acc = None
            for k in range(KA):
                off = HALO_A - (KA - 1) + k
                p, q8 = off % SUBLANES, off - off % SUBLANES
                win = abuf[r0 + q8:r0 + q8 + ROWS, :] if p == 0 else shf[p - 1, r0 + q8:r0 + q8 + ROWS, :]
                term = caw_ref[k:k + 1, :] * win
                acc = term if acc is None else acc + term
            aux_ref[r0:r0 + ROWS, 0:GROUP] = acc + s256_ref[0:1, :]
        abuf[0:HALO_A, :] = abuf[T:T + HALO_A, :]
        a1 = aux_ref[:, 0:GROUP]
        segm = seg_ref[...]
        cen = a1 - _segdot(a1, segm)
        var = _segdot(cen * cen, segm)
        a2 = cen * lax.rsqrt(var + LN_EPS) * s256_ref[1:2, :] + s256_ref[2:3, :]
        az = hs(2)
        mix_ref[:, 0:GROUP] = (a2 * _sig(a2) * (az * _sig(az))).astype(BF16)

        bbuf[HALO_B:HALO_B + T, :] = hs(4) * hs(5)
        for r0 in range(0, T, ROWS):
            acc = None
            for k in range(KB):
                off = HALO_B - (KB - 1) + k + r0
                term = cbw_ref[k:k + 1, :] * bbuf[off:off + ROWS, :]
                acc = term if acc is None else acc + term
            aux_ref[r0:r0 + ROWS, GROUP:2 * GROUP] = acc
        bbuf[0:HALO_B, :] = bbuf[T:T + HALO_B, :]
        bz = hs(6)
        mix_ref[:, GROUP:2 * GROUP] = (hs(3) * aux_ref[:, GROUP:2 * GROUP] * (bz * _sig(bz))).astype(BF16)

        ch = hs(7)
        cbuf[HALO_C:HALO_C + T, :] = ch
        hi_lane = (lax.broadcasted_iota(jnp.int32, (1, 128), 1) // HEAD) == 1
        for r0 in range(0, T, ROWS):
            def win(col, j0, j1):
                s = None
                for j in range(j0, j1):
                    off = HALO_C - j + r0
                    term = cbuf[off:off + ROWS, 128 * col:128 * (col + 1)]
                    s = term if s is None else s + term
                return s
            w0 = win(0, 0, 2) + jnp.where(hi_lane, win(0, 2, 4), 0.0)
            w1 = win(1, 0, 8) + jnp.where(hi_lane, win(1, 8, 16), 0.0)
            aux_ref[r0:r0 + ROWS, 2 * GROUP:2 * GROUP + 128] = w0
            aux_ref[r0:r0 + ROWS, 2 * GROUP + 128:3 * GROUP] = w1
        cbuf[0:HALO_C, :] = cbuf[T:T + HALO_C, :]
        pooled = aux_ref[:, 2 * GROUP:3 * GROUP] / _pool_cnt(i, T) - ch
        aux_ref[:, 2 * GROUP:3 * GROUP] = pooled
        q = _dot(pooled.astype(BF16), pw_ref[...])
        cz = hs(8)
        mix_ref[:, 2 * GROUP:3 * GROUP] = (q * s256_ref[3:4, :] * (cz * _sig(cz))).astype(BF16)

        dv = hs(10)
        cen = dv - _rowmean(dv)
        var = _rowmean(cen * cen)
        v = cen * lax.rsqrt(var + LN_EPS) * s256_ref[4:5, :] + s256_ref[5:6, :]
        sps = []
        for n in range(T // SGU_BLOCK):
            vb = v[n * SGU_BLOCK:(n + 1) * SGU_BLOCK, :]
            sps.append(_dot(wm_s[...], _vstack(vb)) + sb_ref[...])
        sp = jnp.concatenate(sps, axis=0)
        dz = hs(11)
        mix_ref[:, 3 * GROUP:4 * GROUP] = (hs(9) * sp * (dz * _sig(dz))).astype(BF16)

        out = v1024_ref[0:1, :]
        for k in range(N_CHIPS):
            out = out + _dot(mix_ref[:, GROUP * k:GROUP * (k + 1)], wo_ref[k])
        z = alpha * x + out
        z_ref[...] = z
        cen = z - _rowmean(z)
        var = _rowmean(cen * cen)
        y = cen * lax.rsqrt(var + LN_EPS) * v1024_ref[1:2, :] + v1024_ref[2:3, :]
        if target is None:
            y_ref[...] = y
        else:
            t_ref, loss_ref = refs[13], refs[n_in + 6]

            @pl.when(i == 0)
            def _():
                loss_ref[...] = jnp.zeros_like(loss_ref)
            err = y - t_ref[...]
            y_ref[...] = err * (1.0 / D_MODEL)
            loss_ref[...] += jnp.sum(_colsum(err * err), axis=1, keepdims=True) * (0.5 / D_MODEL)

    def rows(width):
        return pl.BlockSpec((T, width), lambda i, l: (i, 0))

    consts = (wi, bin_, caw, cbw, s256, seg, pw, wm, sb, wo, v1024)
    in_specs = [rows(D_MODEL)] + [_whole(a) if a is wi or a is seg or a is wo else _of_layer(a) for a in consts]
    out_specs = [rows(D_MODEL), rows(D_MODEL), rows(IN_WIDTH), rows(3 * GROUP), rows(D_MODEL), rows(D_MODEL)]
    out_shape = [jax.ShapeDtypeStruct((S, D_MODEL), F32), jax.ShapeDtypeStruct((S, D_MODEL), BF16),
                 jax.ShapeDtypeStruct((S, IN_WIDTH), F32), jax.ShapeDtypeStruct((S, 3 * GROUP), F32),
                 jax.ShapeDtypeStruct((S, D_MODEL), BF16), jax.ShapeDtypeStruct((S, D_MODEL), F32)]
    scratch = [pltpu.VMEM((T + HALO_A, GROUP), F32), pltpu.VMEM((T + HALO_B, GROUP), F32),
               pltpu.VMEM((T + HALO_C, GROUP), F32), pltpu.VMEM((SGU_BLOCK, 4 * SGU_BLOCK), BF16),
               pltpu.VMEM((SUBLANES - 1, T + HALO_A - SUBLANES, GROUP), F32)]
    extra = ()
    if nxt is not None:
        extra = tuple(nxt)
        in_specs += [ANY, ANY]
        out_specs += [ANY, ANY]
        out_shape += [jax.ShapeDtypeStruct((N_CHIPS, D_MODEL, COLS), BF16),
                      jax.ShapeDtypeStruct((N_CHIPS, GROUP, D_MODEL), BF16)]
        scratch += [pltpu.SemaphoreType.DMA((N_GATHER_SEMS,)), pltpu.SemaphoreType.DMA((N_GATHER_SEMS,)),
                    pltpu.SemaphoreType.DMA((4,)), pltpu.VMEM((D_MODEL, COLS), BF16), pltpu.VMEM((GROUP, D_MODEL), BF16)]
    if target is not None:
        extra = (target,)
        in_specs += [rows(D_MODEL)]
        out_specs += [pl.BlockSpec((8, 128), lambda i, l: (0, 0))]
        out_shape += [jax.ShapeDtypeStruct((8, 128), F32)]
    grid_spec = pltpu.PrefetchScalarGridSpec(num_scalar_prefetch=1, grid=(nt,), in_specs=in_specs,
                                             out_specs=out_specs, scratch_shapes=scratch)
    return pl.pallas_call(
        body, name=("fwd_layer_loss" if target is not None else "fwd_layer") if nxt is None else "fwd_layer_gather",
        grid_spec=grid_spec, out_shape=out_shape,
        compiler_params=_vmem_params(dimension_semantics=("arbitrary",), has_side_effects=nxt is not None),
    )(larr, x, *consts, *extra)


ROW_CBW = 8
ROW_CAW = 16
ROW_LOSS = 7
ROW_PW = 48
ROW_LNG = 112
ROW_LNB = 116
ROW_BOUT = 120
ROW_BIN = 124
ROW_WC = 136
ROW_SB = 392
SM_ROWS = 400
N_DEV = 8


def _exchange_comm(start, finish, l, p_i, p_o, sm, r_i, r_o, r_sm, send_sems, recv_sems, loc_sem):
    x, y, c = _place()
    me = 4 * x + 2 * y + c
    chips = _other_chips(x, y)

    def rc(src, dst, sem, to):
        return pltpu.make_async_remote_copy(src_ref=src, dst_ref=dst, send_sem=send_sems.at[sem],
                                            recv_sem=recv_sems.at[sem], device_id=to, device_id_type=MESH)

    def big(r):
        px, py, pk = chips[r]
        to = (px, py, c)
        return [rc(p_i.at[l, pk], r_i.at[r, l], 2 * r, to), rc(p_o.at[l, pk], r_o.at[r, l], 2 * r + 1, to)]

    def peer(rel):
        px = 1 - x if rel & 4 else x
        py = 1 - y if rel & 2 else y
        pc = 1 - c if rel & 1 else c
        return (px, py, pc), 4 * px + 2 * py + pc

    def small_out(rel):
        to, _ = peer(rel)
        return rc(sm, r_sm.at[me], N_EXCH_SEMS - N_DEV + rel, to)

    def small_in(rel):
        to, idx = peer(rel)
        return rc(sm, r_sm.at[idx], N_EXCH_SEMS - N_DEV + rel, to)

    def local():
        return pltpu.make_async_copy(sm, r_sm.at[me], loc_sem.at[0])

    with_big, with_small = p_i is not None, sm is not None

    @pl.when(start)
    def _():
        if with_small:
            local().start()
        if with_big:
            for r in range(3):
                for cp in big(r):
                    cp.start()
        if with_small:
            for rel in range(1, N_DEV):
                small_out(rel).start()

    @pl.when(finish)
    def _():
        if with_big:
            for r in range(3):
                for cp in big(r):
                    cp.wait()
        if with_small:
            for rel in range(1, N_DEV):
                small_in(rel).wait_recv()
                small_out(rel).wait_send()
            local().wait()


RC = 32
RC_WIDE = 16
ACC_ROWS = 136


def _rsum8(v):
    r = v[0:8]
    for j in range(1, v.shape[0] // 8):
        r = r + v[8 * j:8 * j + 8]
    return r


def _bwd_layer(larr, dy, z, h, aux, wi, caw, cbw, s256, seg, pw, wm, wmt, sb, wo, v1024, e4, *, tile, exch=None):
    S = dy.shape[0]
    T = tile
    nt = S // T
    nblk = T // SGU_BLOCK
    alpha = float((2.0 * 4) ** 0.25)
    n_in = 17 + (5 if exch is not None else 0)
    n_out = 4 + (3 if exch is not None else 0)
    slab = pltpu.VMEM((T, GROUP), F32)
    scratch = dict(
        dbuf=pltpu.VMEM((T + HALO_A, GROUP), F32), ebuf=pltpu.VMEM((T + HALO_B, GROUP), F32),
        fbuf=pltpu.VMEM((T + HALO_C, GROUP), F32), sh=pltpu.VMEM((SUBLANES - 1, T + HALO_A - SUBLANES, GROUP), F32),
        wm_s=pltpu.VMEM((SGU_BLOCK, 4 * SGU_BLOCK), BF16), wmt_s=pltpu.VMEM((4 * SGU_BLOCK, SGU_BLOCK), BF16),
        dsp_acc=pltpu.VMEM((SGU_BLOCK, GROUP), F32), pw_acc=pltpu.VMEM((GROUP, GROUP), F32),
        acc_s=pltpu.VMEM((8 * ACC_ROWS, GROUP), F32), acc_w=pltpu.VMEM((24, D_MODEL), F32),
        dmix_s=pltpu.VMEM((T, D_MODEL), F32), vst_s=pltpu.VMEM((nblk, 4 * SGU_BLOCK, GROUP), BF16),
        dq_s=pltpu.VMEM((T, GROUP), BF16), dxt_s=pltpu.VMEM((D_MODEL, T), F32),
        mean_s=slab, t1_s=slab, t2_s=slab, q_s=slab, xv_s=slab, rv_s=slab, v_s=slab, sp_s=slab, a0_s=slab, sg_s=slab,
        xh_s=slab, ra_s=slab, ub_s=slab, dsp_s=slab, m1_s=slab, m2_s=slab, dpool_s=slab, dvd_s=slab, u_s=slab,
        du_s=slab, cw_s=slab)
    names = list(scratch)

    def body(*refs):
        (dy_ref, z_ref, h_ref, aux_ref, wi_ref, caw_ref, cbw_ref, s256_ref, seg_ref, pw_ref, wm_ref, wmt_ref,
         sb_ref, wo_ref, v1024_ref, e4_ref) = refs[1:17]
        dx_ref, dhb_ref, dzb_ref, osm_ref = refs[n_in:n_in + 4]
        k0 = n_in + n_out
        sc = dict(zip(names, refs[k0:k0 + len(names)]))
        dbuf, ebuf, fbuf, sh = sc["dbuf"], sc["ebuf"], sc["fbuf"], sc["sh"]
        wm_s, wmt_s, dsp_acc, pw_acc, acc_s, acc_w = (sc[n] for n in ("wm_s", "wmt_s", "dsp_acc", "pw_acc", "acc_s",
                                                                        "acc_w"))
        dmix_s, vst_s, dq_s = sc["dmix_s"], sc["vst_s"], sc["dq_s"]
        i = pl.program_id(0)
        tile_idx = nt - 1 - i
        if exch is not None:
            p_i, p_o, sm = refs[17:20]
            r_i, r_o, r_sm = refs[n_in + 4:n_in + 7]
            _exchange_comm(i == 0, i == nt - 1, refs[0][0] + 1, p_i, p_o, sm, r_i, r_o, r_sm, *refs[k0 + len(names):])

        @pl.when(i == 0)
        def _():
            dbuf[T:T + HALO_A, :] = jnp.zeros((HALO_A, GROUP), F32)
            ebuf[T:T + HALO_B, :] = jnp.zeros((HALO_B, GROUP), F32)
            fbuf[T:T + HALO_C, :] = jnp.zeros((HALO_C, GROUP), F32)
            _sgu_masks(wm_ref, wmt_ref, wm_s, wmt_s)
            osm_ref[...] = jnp.zeros_like(osm_ref)
            dsp_acc[...] = jnp.zeros_like(dsp_acc)
            pw_acc[...] = jnp.zeros_like(pw_acc)
            acc_s[...] = jnp.zeros_like(acc_s)
            acc_w[...] = jnp.zeros_like(acc_w)

        def chunks(rc, fn):
            for c in range(T // rc):
                fn(pl.ds(c * rc, rc))

        def hs(j, rows):
            return h_ref[rows, GROUP * j:GROUP * (j + 1)]

        def acc_add(row, val):
            acc_s[8 * row:8 * row + 8, :] += _rsum8(val)

        def put_dh(j, rows, val):
            acc_add(ROW_BIN + j, val)
            dhb_ref[rows, GROUP * j:GROUP * (j + 1)] = val.astype(BF16)

        def dsilu(v, s):
            return s * (1.0 + v * (1.0 - s))

        def vec(r):
            return s256_ref[r:r + 1, :]

        def ln_bwd(rows):
            dyc = dy_ref[rows, :]
            zc = z_ref[rows, :]
            cen = zc - _rowmean(zc)
            rstd = lax.rsqrt(_rowmean(cen * cen) + LN_EPS)
            xhat = cen * rstd
            acc_w[0:8, :] += _rsum8(dyc * xhat)
            acc_w[8:16, :] += _rsum8(dyc)
            gdy = dyc * v1024_ref[1:2, :]
            dz = rstd * (gdy - _rowmean(gdy) - xhat * _rowmean(gdy * xhat))
            acc_w[16:24, :] += _rsum8(dz)
            dzb_ref[rows, :] = dz.astype(BF16)
            dx_ref[rows, :] = alpha * dz
        chunks(RC_WIDE, ln_bwd)

        segm = seg_ref[...]
        dzb = dzb_ref[...]
        for k in range(N_CHIPS):
            dmix_s[:, GROUP * k:GROUP * (k + 1)] = _dot_nt(dzb, wo_ref[k])
        sc["mean_s"][...] = _segdot(aux_ref[:, 0:GROUP], segm)
        pooled_b = aux_ref[:, 2 * GROUP:3 * GROUP].astype(BF16)
        sc["q_s"][...] = _dot(pooled_b, pw_ref[...])

        def centre(rows):
            cen = aux_ref[rows, 0:GROUP] - sc["mean_s"][rows, :]
            sc["t1_s"][rows, :] = cen * cen
            dv_in = hs(10, rows)
            cen_v = dv_in - _rowmean(dv_in)
            rstd_v = lax.rsqrt(_rowmean(cen_v * cen_v) + LN_EPS)
            xv = cen_v * rstd_v
            sc["xv_s"][rows, :] = xv
            sc["rv_s"][rows, :] = jnp.broadcast_to(rstd_v, xv.shape)
            sc["v_s"][rows, :] = xv * vec(4) + vec(5)
        chunks(RC, centre)

        sc["t2_s"][...] = _segdot(sc["t1_s"][...], segm)
        for n in range(nblk):
            blk = slice(n * SGU_BLOCK, (n + 1) * SGU_BLOCK)
            vst_s[n] = _vstack(sc["v_s"][blk, :])
            sc["sp_s"][blk, :] = _dot(wm_s[...], vst_s[n]) + sb_ref[...]

        def mixers(rows):
            a_val, a_glu, a_z = hs(0, rows), hs(1, rows), hs(2, rows)
            sg = _sig(a_glu)
            sc["a0_s"][rows, :] = a_val * sg
            sc["sg_s"][rows, :] = sg
            rstd_a = lax.rsqrt(sc["t2_s"][rows, :] + LN_EPS)
            xh = (aux_ref[rows, 0:GROUP] - sc["mean_s"][rows, :]) * rstd_a
            a2 = xh * vec(1) + vec(2)
            s2 = _sig(a2)
            sz = _sig(a_z)
            dya = dmix_s[rows, 0:GROUP]
            put_dh(2, rows, dya * (a2 * s2) * dsilu(a_z, sz))
            d_a2 = dya * (a_z * sz) * dsilu(a2, s2)
            acc_add(1, d_a2 * xh)
            acc_add(2, d_a2)
            gd = d_a2 * vec(1)
            sc["t1_s"][rows, :] = gd
            sc["t2_s"][rows, :] = gd * xh
            sc["xh_s"][rows, :] = xh
            sc["ra_s"][rows, :] = rstd_a
            b_b, b_c, b_h, b_z = hs(3, rows), hs(4, rows), hs(5, rows), hs(6, rows)
            cb = aux_ref[rows, GROUP:2 * GROUP]
            sz = _sig(b_z)
            dyb = dmix_s[rows, GROUP:2 * GROUP]
            put_dh(3, rows, dyb * cb * (b_z * sz))
            put_dh(6, rows, dyb * b_b * cb * dsilu(b_z, sz))
            ebuf[rows, :] = dyb * b_b * (b_z * sz)
            sc["ub_s"][rows, :] = b_c * b_h
            c_z = hs(8, rows)
            q = sc["q_s"][rows, :]
            sz = _sig(c_z)
            dyc = dmix_s[rows, 2 * GROUP:3 * GROUP]
            acc_add(3, dyc * q * (c_z * sz))
            put_dh(8, rows, dyc * q * vec(3) * dsilu(c_z, sz))
            dq_s[rows, :] = (dyc * vec(3) * (c_z * sz)).astype(BF16)
            d_u, d_z = hs(9, rows), hs(11, rows)
            sp = sc["sp_s"][rows, :]
            sz = _sig(d_z)
            dyd = dmix_s[rows, 3 * GROUP:4 * GROUP]
            put_dh(9, rows, dyd * sp * (d_z * sz))
            put_dh(11, rows, dyd * d_u * sp * dsilu(d_z, sz))
            sc["dsp_s"][rows, :] = dyd * d_u * (d_z * sz)
        chunks(RC, mixers)

        sc["m1_s"][...] = _segdot(sc["t1_s"][...], segm)
        sc["m2_s"][...] = _segdot(sc["t2_s"][...], segm)
        d_q = dq_s[...]
        pw_acc[...] += _dot_tn(pooled_b, d_q)
        sc["dpool_s"][...] = _dot_nt(d_q, pw_ref[...])
        grp = _lane_group(GROUP)
        for n in range(nblk):
            blk = slice(n * SGU_BLOCK, (n + 1) * SGU_BLOCK)
            dspb = sc["dsp_s"][blk, :]
            dsp_acc[...] += dspb
            dspb16 = dspb.astype(BF16)
            dvst = _dot(wmt_s[...], dspb16)
            dvb = None
            for hh in range(4):
                part = jnp.where(grp == hh, dvst[hh * SGU_BLOCK:(hh + 1) * SGU_BLOCK, :], 0.0)
                dvb = part if dvb is None else dvb + part
            sc["dvd_s"][blk, :] = dvb
            dwc = _dot_nt(dspb16, vst_s[n])
            osm_ref[ROW_WC:ROW_WC + SGU_BLOCK, :] += dwc[:, 0:GROUP]
            osm_ref[ROW_WC + SGU_BLOCK:ROW_WC + 2 * SGU_BLOCK, :] += dwc[:, GROUP:2 * GROUP]

        def ln_sums(rows):
            xh = sc["xh_s"][rows, :]
            d_a1 = sc["ra_s"][rows, :] * (sc["t1_s"][rows, :] - sc["m1_s"][rows, :] - xh * sc["m2_s"][rows, :])
            acc_add(0, d_a1)
            dbuf[rows, :] = d_a1
            pos = tile_idx * T + rows.start + lax.broadcasted_iota(jnp.int32, (RC, GROUP), 0) + 1
            lane = lax.broadcasted_iota(jnp.int32, (RC, GROUP), 1) // HEAD
            win = jnp.where(lane == 0, 2, jnp.where(lane == 1, 4, jnp.where(lane == 2, 8, 16)))
            fbuf[rows, :] = sc["dpool_s"][rows, :] / jnp.minimum(pos, win).astype(F32)
            d_v = sc["dvd_s"][rows, :]
            xv = sc["xv_s"][rows, :]
            acc_add(4, d_v * xv)
            acc_add(5, d_v)
            gd = d_v * vec(4)
            put_dh(10, rows, sc["rv_s"][rows, :] * (gd - _rowmean(gd) - xv * _rowmean(gd * xv)))
        chunks(RC, ln_sums)

        span = T + HALO_A - SUBLANES
        for p in range(1, SUBLANES):
            sh[p - 1, :, :] = dbuf[p:p + span, :]

        for r0 in range(0, T, ROWS):
            uc = sc["ub_s"][r0:r0 + ROWS, :]
            acc = None
            for k in range(KB):
                off = (KB - 1) - k + r0
                w = ebuf[off:off + ROWS, :]
                term = cbw_ref[k:k + 1, :] * w
                acc = term if acc is None else acc + term
                acc_add(ROW_CBW + k, uc * w)
            sc["du_s"][r0:r0 + ROWS, :] = acc
        ebuf[T:T + HALO_B, :] = ebuf[0:HALO_B, :]

        hi_lane = (lax.broadcasted_iota(jnp.int32, (1, 128), 1) // HEAD) == 1
        for r0 in range(0, T, ROWS):
            def win(col, j0, j1):
                s = None
                for j in range(j0, j1):
                    term = fbuf[r0 + j:r0 + j + ROWS, 128 * col:128 * (col + 1)]
                    s = term if s is None else s + term
                return s
            sc["cw_s"][r0:r0 + ROWS, 0:128] = win(0, 0, 2) + jnp.where(hi_lane, win(0, 2, 4), 0.0)
            sc["cw_s"][r0:r0 + ROWS, 128:256] = win(1, 0, 8) + jnp.where(hi_lane, win(1, 8, 16), 0.0)
        fbuf[T:T + HALO_C, :] = fbuf[0:HALO_C, :]

        def rest_bc(rows):
            d_u = sc["du_s"][rows, :]
            put_dh(4, rows, d_u * hs(5, rows))
            put_dh(5, rows, d_u * hs(4, rows))
            put_dh(7, rows, sc["cw_s"][rows, :] - sc["dpool_s"][rows, :])
        chunks(RC, rest_bc)

        dxt_s = sc["dxt_s"]

        def dx_term(k):
            term = _dot_nt(wi_ref[k], dhb_ref[:, COLS * k:COLS * (k + 1)])
            if k == 1:
                dxt_s[...] = term
            else:
                dxt_s[...] += term

        def conv_a(rows):
            a0c = sc["a0_s"][rows, :]
            acc = None
            for k in range(KA):
                off = (KA - 1) - k
                p, q8 = off % SUBLANES, off - off % SUBLANES
                w = dbuf[pl.ds(rows.start + q8, RC), :] if p == 0 else sh[p - 1, pl.ds(rows.start + q8, RC), :]
                term = caw_ref[k:k + 1, :] * w
                acc = term if acc is None else acc + term
                acc_add(ROW_CAW + k, a0c * w)
            sc["u_s"][rows, :] = acc
        n_chunks = T // RC
        after = {(n_chunks * j) // 3: j + 1 for j in range(3)}
        for c in range(n_chunks):
            conv_a(pl.ds(c * RC, RC))
            if c in after:
                dx_term(after[c])
        dbuf[T:T + HALO_A, :] = dbuf[0:HALO_A, :]

        def rest_a(rows):
            d_a0 = sc["u_s"][rows, :]
            sg = sc["sg_s"][rows, :]
            put_dh(0, rows, d_a0 * sg)
            put_dh(1, rows, d_a0 * hs(0, rows) * sg * (1.0 - sg))
        chunks(RC, rest_a)
        dx_term(0)
        dx_ref[...] += dxt_s[...].T

        @pl.when(i == nt - 1)
        def _():
            for row in list(range(6)) + list(range(ROW_CBW, ROW_CBW + KB)) + list(range(ROW_CAW, ROW_CAW + KA)) + list(
                    range(ROW_BIN, ROW_BIN + N_SLICES)):
                osm_ref[row:row + 1, :] = _colsum(acc_s[8 * row:8 * row + 8, :])
            for j, row in enumerate((ROW_LNG, ROW_LNB, ROW_BOUT)):
                cs = _colsum(acc_w[8 * j:8 * j + 8, :])
                for q in range(D_MODEL // GROUP):
                    osm_ref[row + q:row + q + 1, :] = cs[:, GROUP * q:GROUP * (q + 1)]
            r = lax.broadcasted_iota(jnp.int32, (SGU_BLOCK, GROUP), 0) // CHUNK
            c = (lax.broadcasted_iota(jnp.int32, (SGU_BLOCK, GROUP), 1) % SGU_BLOCK) // CHUNK
            for half in range(2):
                rows_ = slice(ROW_WC + half * SGU_BLOCK, ROW_WC + (half + 1) * SGU_BLOCK)
                osm_ref[rows_, :] = jnp.where(c <= r, osm_ref[rows_, :], 0.0)
            sb_t = _segdot(dsp_acc[...], e4_ref[...]).T
            osm_ref[ROW_SB:ROW_SB + 8, 0:SGU_BLOCK] = sb_t[0:8, :]
            for g in range(4):
                osm_ref[ROW_PW:ROW_PW + HEAD, HEAD * g:HEAD * (g + 1)] = (
                    pw_acc[HEAD * g:HEAD * (g + 1), HEAD * g:HEAD * (g + 1)])

    def rows(width):
        return pl.BlockSpec((T, width), lambda i, l: (nt - 1 - i, 0))

    consts = (wi, caw, cbw, s256, seg, pw, wm, wmt, sb, wo, v1024, e4)
    unstacked = (wi, seg, wo, e4)
    in_specs = [rows(D_MODEL), rows(D_MODEL), rows(IN_WIDTH), rows(3 * GROUP)] + [
        _whole(a) if any(a is u for u in unstacked) else _of_layer(a) for a in consts]
    out_specs = [rows(D_MODEL), rows(IN_WIDTH), rows(D_MODEL), pl.BlockSpec((SM_ROWS, GROUP), lambda i, l: (0, 0))]
    out_shape = [jax.ShapeDtypeStruct((S, D_MODEL), F32), jax.ShapeDtypeStruct((S, IN_WIDTH), BF16),
                 jax.ShapeDtypeStruct((S, D_MODEL), BF16), jax.ShapeDtypeStruct((SM_ROWS, GROUP), F32)]
    scratch_shapes = list(scratch.values())
    extra, aliases = (), {}
    if exch is not None:
        extra = tuple(exch)
        r_i, r_o = exch[3], exch[4]
        in_specs += [ANY] * 5
        out_specs += [ANY] * 3
        out_shape += [jax.ShapeDtypeStruct(r_i.shape, r_i.dtype), jax.ShapeDtypeStruct(r_o.shape, r_o.dtype),
                      jax.ShapeDtypeStruct((N_DEV, SM_ROWS, GROUP), F32)]
        scratch_shapes += [pltpu.SemaphoreType.DMA((N_EXCH_SEMS,)), pltpu.SemaphoreType.DMA((N_EXCH_SEMS,)),
                           pltpu.SemaphoreType.DMA((1,))]
        aliases = {20: 4, 21: 5}
    grid_spec = pltpu.PrefetchScalarGridSpec(num_scalar_prefetch=1, grid=(nt,), in_specs=in_specs,
                                             out_specs=out_specs, scratch_shapes=scratch_shapes)
    return pl.pallas_call(
        body, name="bwd_layer" if exch is None else "bwd_layer_exchange",
        grid_spec=grid_spec, out_shape=out_shape, input_output_aliases=aliases,
        compiler_params=_vmem_params(dimension_semantics=("arbitrary",), has_side_effects=exch is not None),
    )(larr, dy, z, h, aux, *consts, *extra)


def _dw(layer, xb, dhb, mixb, dzb, gwi, gwi16, gwo, gwo16, *, k_steps, small=None):
    S = xb.shape[0]
    tk = S // k_steps
    n_steps = N_CHIPS + k_steps

    def body(*refs):
        x_ref, dh_ref, mix_ref, dz_ref = refs[1:5]
        oi_ref, oi16_ref, oo_ref, oo16_ref = refs[n_in:n_in + 4]
        j = pl.program_id(0)
        if small is not None:
            _exchange_comm(j == 0, j == n_steps - 1, None, None, None, refs[9], None, None, refs[n_in + 4],
                           *refs[n_in + 5:])

        @pl.when(j < N_CHIPS)
        def _():
            acc = _dot_tn(x_ref[...], dh_ref[...])
            oi_ref[...] = acc
            oi16_ref[...] = acc.astype(BF16)

        @pl.when(j == N_CHIPS)
        def _():
            oo_ref[...] = jnp.zeros_like(oo_ref)

        @pl.when(j >= N_CHIPS)
        def _():
            oo_ref[...] += _dot_tn(mix_ref[...], dz_ref[...]).reshape(N_CHIPS, GROUP, D_MODEL)

        @pl.when(j == n_steps - 1)
        def _():
            oo16_ref[...] = oo_ref[...].astype(BF16)

    def col_block(j, l):
        return jnp.minimum(j, N_CHIPS - 1)

    def tok_block(j, l):
        return jnp.maximum(j - N_CHIPS, 0)

    oi_spec = pl.BlockSpec((None, None, D_MODEL, COLS), lambda j, l: (l[0], col_block(j, l), 0, 0))
    oo_spec = pl.BlockSpec((None, N_CHIPS, GROUP, D_MODEL), lambda j, l: (l[0], 0, 0, 0))
    in_specs = [pl.BlockSpec((S, D_MODEL), lambda j, l: (0, 0)),
                pl.BlockSpec((S, COLS), lambda j, l: (0, col_block(j, l))),
                pl.BlockSpec((tk, D_MODEL), lambda j, l: (tok_block(j, l), 0)),
                pl.BlockSpec((tk, D_MODEL), lambda j, l: (tok_block(j, l), 0)), ANY, ANY, ANY, ANY]
    out_specs = [oi_spec, oi_spec, oo_spec, oo_spec]
    out_shape = [jax.ShapeDtypeStruct(gwi.shape, F32), jax.ShapeDtypeStruct(gwi.shape, BF16),
                 jax.ShapeDtypeStruct(gwo.shape, F32), jax.ShapeDtypeStruct(gwo.shape, BF16)]
    scratch, extra = [], ()
    if small is not None:
        extra = (small,)
        in_specs += [ANY]
        out_specs += [ANY]
        out_shape += [jax.ShapeDtypeStruct((N_DEV, SM_ROWS, GROUP), F32)]
        scratch = [pltpu.SemaphoreType.DMA((N_EXCH_SEMS,)), pltpu.SemaphoreType.DMA((N_EXCH_SEMS,)), pltpu.SemaphoreType.DMA((1,))]
    n_in = 9 + len(extra)
    grid_spec = pltpu.PrefetchScalarGridSpec(
        num_scalar_prefetch=1, grid=(n_steps,), in_specs=in_specs, out_specs=out_specs, scratch_shapes=scratch)
    return pl.pallas_call(
        body, name="dw" if small is None else "dw_exchange", grid_spec=grid_spec, out_shape=out_shape,
        input_output_aliases={5: 0, 6: 1, 7: 2, 8: 3},
        compiler_params=_vmem_params(dimension_semantics=("arbitrary",), has_side_effects=small is not None),
    )(layer, xb, dhb, mixb, dzb, gwi, gwi16, gwo, gwo16, *extra)


def _adamw_math(w, g, m, v):
    nm = ADAM_B1 * m + (1.0 - ADAM_B1) * g
    nv = ADAM_B2 * v + (1.0 - ADAM_B2) * (g * g)
    c1 = 1.0 - ADAM_B1 ** ADAM_STEP
    c2 = 1.0 - ADAM_B2 ** ADAM_STEP
    return -ADAM_LR * ((nm / c1) / (jnp.sqrt(nv / c2) + ADAM_EPS) + ADAM_WD * w), nm, nv


def _adamw_small(ws, gs, ms, vs):
    n = len(ws)

    def body(*refs):
        for j in range(n):
            d, nm, nv = _adamw_math(*(refs[k * n + j][...] for k in range(4)))
            refs[4 * n + j][...] = d
            refs[5 * n + j][...] = nm
            refs[6 * n + j][...] = nv

    shapes = [jax.ShapeDtypeStruct(w.shape, F32) for w in ws]
    outs = pl.pallas_call(body, name="adamw_small", out_shape=shapes * 3, compiler_params=_vmem_params())(
        *ws, *gs, *ms, *vs)
    return outs[0:n], outs[n:2 * n], outs[2 * n:3 * n]


def _adamw(w, g, m, v, *, rows_per_step, name, copy_g=False):
    R, C = w.shape
    tr = rows_per_step

    def body(w_ref, g_ref, m_ref, v_ref, d_ref, nm_ref, nv_ref, *g_out):
        g_ = g_ref[...]
        d_ref[...], nm_ref[...], nv_ref[...] = _adamw_math(w_ref[...], g_, m_ref[...], v_ref[...])
        if copy_g:
            g_out[0][...] = g_

    spec = pl.BlockSpec((tr, C), lambda i: (i, 0))
    n_out = 4 if copy_g else 3
    return pl.pallas_call(
        body, name=name, grid=(R // tr,),
        in_specs=[spec] * 4, out_specs=[spec] * n_out,
        out_shape=[jax.ShapeDtypeStruct((R, C), F32)] * n_out,
        compiler_params=_vmem_params(dimension_semantics=("arbitrary",)),
    )(w, g, m, v)


def _gather_weights(wi16, wo16, cw):
    L = wi16.shape[0]
    hi_rows, ho_rows = D_MODEL // 2, GROUP // 2
    n_ici = 2 * L + 1
    n_fwd = 2 * L

    def body(wi_ref, wo_ref, cw_ref, *rest):
        wig = rest[0:L]
        wog = rest[L:2 * L]
        cwg = rest[2 * L]
        send_sems, recv_sems, loc_sems, vwi, vwo, vcw = rest[2 * L + 1:]
        x, y, c = _place()
        me_k = 2 * x + y
        sibling = (x, y, 1 - c)
        chips = _other_chips(x, y)

        def half_i(ref, blk):
            return ref.at[blk, pl.ds(c * hi_rows, hi_rows), :]

        def half_o(ref, blk):
            return ref.at[blk, pl.ds(c * ho_rows, ho_rows), :]

        def other_half_i(ref, blk):
            return ref.at[blk, pl.ds((1 - c) * hi_rows, hi_rows), :]

        def other_half_o(ref, blk):
            return ref.at[blk, pl.ds((1 - c) * ho_rows, ho_rows), :]

        stage_in = [pltpu.make_async_copy(wi_ref, vwi, loc_sems.at[0]), pltpu.make_async_copy(wo_ref, vwo, loc_sems.at[1]),
                    pltpu.make_async_copy(cw_ref, vcw, loc_sems.at[2])]
        local = []
        for l in range(L):
            local.append(pltpu.make_async_copy(vwi.at[l], wig[l].at[me_k], loc_sems.at[3 + 2 * l]))
            local.append(pltpu.make_async_copy(vwo.at[l], wog[l].at[me_k], loc_sems.at[3 + 2 * l + 1]))
        local.append(pltpu.make_async_copy(vcw, cwg.at[me_k], loc_sems.at[3 + 2 * L]))
        for cp in stage_in:
            cp.start()

        def remote(src, dst, sem, to):
            return pltpu.make_async_remote_copy(src_ref=src, dst_ref=dst, send_sem=send_sems.at[sem],
                                                recv_sem=recv_sems.at[sem], device_id=to, device_id_type=MESH)

        sends = []
        for r, (px, py, _) in enumerate(chips):
            to = (px, py, c)
            for l in range(L):
                sends.append(remote(half_i(wi_ref, l), half_i(wig[l], me_k), r * n_ici + 2 * l, to))
                sends.append(remote(half_o(wo_ref, l), half_o(wog[l], me_k), r * n_ici + 2 * l + 1, to))
            sends.append(remote(cw_ref, cwg.at[me_k], r * n_ici + 2 * L, to))
        for cp in sends:
            cp.start()
        for cp in stage_in:
            cp.wait()
        for cp in local:
            cp.start()

        base = 3 * n_ici
        fwds = []
        for r, (px, py, pk) in enumerate(chips):
            for l in range(L):
                remote(half_i(wig[l], pk), half_i(wig[l], pk), r * n_ici + 2 * l, sibling).wait_recv()
                f = remote(half_i(wig[l], pk), half_i(wig[l], pk), base + r * n_fwd + 2 * l, sibling)
                f.start()
                fwds.append(f)
                remote(half_o(wog[l], pk), half_o(wog[l], pk), r * n_ici + 2 * l + 1, sibling).wait_recv()
                f = remote(half_o(wog[l], pk), half_o(wog[l], pk), base + r * n_fwd + 2 * l + 1, sibling)
                f.start()
                fwds.append(f)
            remote(cwg.at[pk], cwg.at[pk], r * n_ici + 2 * L, sibling).wait_recv()
        for r, (px, py, pk) in enumerate(chips):
            for l in range(L):
                remote(other_half_i(wig[l], pk), other_half_i(wig[l], pk), base + r * n_fwd + 2 * l, sibling).wait_recv()
                remote(other_half_o(wog[l], pk), other_half_o(wog[l], pk), base + r * n_fwd + 2 * l + 1, sibling).wait_recv()
        for cp in sends + fwds:
            cp.wait_send()
        for cp in local:
            cp.wait()

    n_sem = 3 * n_ici + 3 * n_fwd
    out_shape = ([jax.ShapeDtypeStruct((N_CHIPS, D_MODEL, COLS), BF16)] * L
                 + [jax.ShapeDtypeStruct((N_CHIPS, GROUP, D_MODEL), BF16)] * L
                 + [jax.ShapeDtypeStruct((N_CHIPS,) + cw.shape, F32)])
    outs = pl.pallas_call(
        body, name="gather_weights",
        in_specs=[ANY, ANY, ANY], out_specs=[ANY] * (2 * L + 1), out_shape=out_shape,
        scratch_shapes=[pltpu.SemaphoreType.DMA((n_sem,)), pltpu.SemaphoreType.DMA((n_sem,)),
                        pltpu.SemaphoreType.DMA((2 * L + 4,)), pltpu.VMEM(wi16.shape, BF16), pltpu.VMEM(wo16.shape, BF16),
                        pltpu.VMEM(cw.shape, F32)],
        compiler_params=_vmem_params(has_side_effects=True),
    )(wi16, wo16, cw)
    return outs[0:L], outs[L:2 * L], outs[2 * L]


def _swap_add(cl_arr, g_i, g16_i, p_i, g_o, g16_o, p_o, *, send_on=None):
    hi, ho = p_i.shape[2], p_o.shape[2]
    n_in = 7 + (2 if send_on is not None else 0)
    n_out = 2 + (2 if send_on is not None else 0)

    def body(*refs):
        cl_ref, gi_ref, gi16_ref, _, go_ref, go16_ref = refs[0:6]
        oi_ref, oo_ref = refs[n_in:n_in + 2]
        ri_v, ro_v, send_sems, recv_sems = refs[n_in + n_out:n_in + n_out + 4]
        k = pl.program_id(0)
        x, y, c = _place()
        l = cl_ref[1]

        def copies(kk):
            pair = ((gi16_ref, hi, ri_v), (go16_ref, ho, ro_v))
            return [pltpu.make_async_remote_copy(
                src_ref=src.at[l, kk, pl.ds((1 - c) * n, n), :], dst_ref=dst.at[kk], send_sem=send_sems.at[2 * kk + j],
                recv_sem=recv_sems.at[2 * kk + j], device_id=(x, y, 1 - c), device_id_type=MESH)
                for j, (src, n, dst) in enumerate(pair)]

        @pl.when(k == 0)
        def _():
            if send_on is None:
                _sibling_handshake()
            for kk in range(N_CHIPS):
                for cp in copies(kk):
                    cp.start()

        for cp in copies(k):
            cp.wait_recv()
        pi_k = (gi_ref[...] + ri_v[k].astype(F32)).astype(oi_ref.dtype)
        po_k = (go_ref[...] + ro_v[k].astype(F32)).astype(oo_ref.dtype)
        oi_ref[...] = pi_k
        oo_ref[...] = po_k

        if send_on is not None:
            qi_ref, qo_ref = refs[n_in + 2:n_in + 4]
            pv_i, pv_o, out_sems, in_sems = refs[n_in + n_out + 4:]
            pv_i[k] = pi_k
            pv_o[k] = po_k
            chips = _other_chips(x, y)

            def onward(r):
                px, py, pk = chips[r]
                return [pltpu.make_async_remote_copy(
                    src_ref=pv.at[pk], dst_ref=q.at[r, l], send_sem=out_sems.at[2 * r + j], recv_sem=in_sems.at[2 * r + j],
                    device_id=(px, py, c), device_id_type=MESH) for j, (pv, q) in enumerate(((pv_i, qi_ref), (pv_o, qo_ref)))]

            for r in range(3):
                @pl.when(k == chips[r][2])
                def _():
                    for cp in onward(r):
                        cp.start()

        @pl.when(k == N_CHIPS - 1)
        def _():
            for kk in range(N_CHIPS):
                for cp in copies(kk):
                    cp.wait_send()
            if send_on is not None:
                for r in range(3):
                    for cp in onward(r):
                        cp.wait()

    def specs(p):
        rows, cols = p.shape[2], p.shape[3]
        mine = pl.BlockSpec((None, None, rows, cols), lambda k, cl: (cl[1], k, cl[0], 0))
        out = pl.BlockSpec((None, None, rows, cols), lambda k, cl: (cl[1], k, 0, 0))
        return mine, out

    (gi_s, pi_s), (go_s, po_s) = specs(p_i), specs(p_o)
    in_specs = [gi_s, ANY, ANY, go_s, ANY, ANY]
    out_specs = [pi_s, po_s]
    out_shape = [jax.ShapeDtypeStruct(p_i.shape, p_i.dtype), jax.ShapeDtypeStruct(p_o.shape, p_o.dtype)]
    scratch = [pltpu.VMEM((N_CHIPS, hi, p_i.shape[3]), BF16), pltpu.VMEM((N_CHIPS, ho, p_o.shape[3]), BF16),
               pltpu.SemaphoreType.DMA((2 * N_CHIPS,)), pltpu.SemaphoreType.DMA((2 * N_CHIPS,))]
    extra, aliases = (), {3: 0, 6: 1}
    if send_on is not None:
        extra = tuple(send_on)
        in_specs += [ANY, ANY]
        out_specs += [ANY, ANY]
        out_shape += [jax.ShapeDtypeStruct(q.shape, q.dtype) for q in send_on]
        scratch += [pltpu.VMEM((N_CHIPS, hi, p_i.shape[3]), BF16), pltpu.VMEM((N_CHIPS, ho, p_o.shape[3]), BF16),
                    pltpu.SemaphoreType.DMA((6,)), pltpu.SemaphoreType.DMA((6,))]
        aliases = {3: 0, 6: 1, 7: 2, 8: 3}
    grid_spec = pltpu.PrefetchScalarGridSpec(num_scalar_prefetch=1, grid=(N_CHIPS,), in_specs=in_specs,
                                             out_specs=out_specs, scratch_shapes=scratch)
    return pl.pallas_call(
        body, name="swap_add" if send_on is None else "swap_add_send", grid_spec=grid_spec, out_shape=out_shape,
        input_output_aliases=aliases,
        compiler_params=_vmem_params(dimension_semantics=("arbitrary",), has_side_effects=True,
                                     **({"collective_id": 1} if send_on is None else {})),
    )(cl_arr, g_i, g16_i, p_i, g_o, g16_o, p_o, *extra)


def _sum_small(r_sms):
    L = len(r_sms)

    def body(*refs):
        o_ref = refs[L]
        for l in range(L):
            acc = refs[l][0]
            for d in range(1, N_DEV):
                acc = acc + refs[l][d]
            o_ref[l] = acc

    return pl.pallas_call(
        body, name="sum_small",
        out_shape=jax.ShapeDtypeStruct((L,) + r_sms[0].shape[1:], F32),
        compiler_params=_vmem_params(),
    )(*r_sms)


def _sum_chunks(kc_arr, p_i, q_i, p_o, q_o, *, nb):
    L = p_i.shape[0]

    def body(kc_ref, pi_ref, a0, a1, a2, po_ref, b0, b1, b2, oi_ref, oo_ref):
        del kc_ref
        f = lambda ref: ref[...].astype(F32)
        oi_ref[...] = ((f(pi_ref) + f(a0)) + f(a1)) + f(a2)
        oo_ref[...] = ((f(po_ref) + f(b0)) + f(b1)) + f(b2)

    def specs(p):
        tr, cols = p.shape[2] // nb, p.shape[3]
        chunk = pl.BlockSpec((None, None, tr, cols), lambda l, i, kc: (l, kc[0], i, 0))
        got = [pl.BlockSpec((None, None, tr, cols), lambda l, i, kc, _j=j: (_j, l, i, 0)) for j in range(3)]
        out = pl.BlockSpec((None, tr, cols), lambda l, i, kc: (l, kc[1] * nb + i, 0))
        return [chunk] + got, out

    (in_i, out_i), (in_o, out_o) = specs(p_i), specs(p_o)
    grid_spec = pltpu.PrefetchScalarGridSpec(num_scalar_prefetch=1, grid=(L, nb), in_specs=in_i + in_o,
                                             out_specs=[out_i, out_o])
    return pl.pallas_call(
        body, name="sum_chunks", grid_spec=grid_spec,
        out_shape=[jax.ShapeDtypeStruct((L, 2 * p.shape[2], p.shape[3]), F32) for p in (p_i, p_o)],
        compiler_params=_vmem_params(dimension_semantics=("arbitrary",) * 2),
    )(kc_arr, p_i, q_i, q_i, q_i, p_o, q_o, q_o, q_o)


def _share_result(gi, go):
    hi_rows, ho_rows = gi.shape[1] // 2, go.shape[1] // 2

    def body(gi_ref, go_ref, oi_ref, oo_ref, send_sems, recv_sems):
        del gi_ref, go_ref
        x, y, c = _place()
        sibling = (x, y, 1 - c)
        cps = []
        for j, (ref, n) in enumerate(((oi_ref, hi_rows), (oo_ref, ho_rows))):
            mine = ref.at[:, pl.ds(c * n, n), :]
            cps.append(pltpu.make_async_remote_copy(src_ref=mine, dst_ref=mine, send_sem=send_sems.at[j],
                                                    recv_sem=recv_sems.at[j], device_id=sibling, device_id_type=MESH))
        _sibling_handshake()
        for cp in cps:
            cp.start()
        for j, (ref, n) in enumerate(((oi_ref, hi_rows), (oo_ref, ho_rows))):
            theirs = ref.at[:, pl.ds((1 - c) * n, n), :]
            pltpu.make_async_remote_copy(src_ref=theirs, dst_ref=theirs, send_sem=send_sems.at[j],
                                         recv_sem=recv_sems.at[j], device_id=sibling, device_id_type=MESH).wait_recv()
        for cp in cps:
            cp.wait_send()

    return pl.pallas_call(
        body, name="share_result",
        in_specs=[ANY, ANY], out_specs=[ANY, ANY],
        out_shape=[jax.ShapeDtypeStruct(gi.shape, F32), jax.ShapeDtypeStruct(go.shape, F32)],
        input_output_aliases={0: 0, 1: 1},
        scratch_shapes=[pltpu.SemaphoreType.DMA((2,)), pltpu.SemaphoreType.DMA((2,))],
        compiler_params=pltpu.CompilerParams(has_side_effects=True, collective_id=0),
    )(gi, go)


WEIGHTS = ("ln_g", "ln_b", "w_in", "b_in", "conv_a_w", "conv_a_b", "norm_a_g", "norm_a_b", "conv_b_w", "pool_w",
           "pool_scale", "sgu_ln_g", "sgu_ln_b", "sgu_w", "sgu_bias", "w_out", "b_out")


def _pad_rows(a, rows):
    return jnp.pad(a, ((0, rows - a.shape[0]), (0, 0)))


def _indicator_consts():
    seg = jnp.where((jnp.arange(GROUP)[:, None] // HEAD) == (jnp.arange(GROUP)[None, :] // HEAD),
                    1.0 / HEAD, 0.0).astype(BF16)
    e4 = ((jnp.arange(GROUP)[:, None] // HEAD) == jnp.arange(128)[None, :]).astype(BF16)
    return seg, e4


def _layer_consts(p, conv_full):
    L = conv_full.shape[0]
    same_head = jnp.eye(4, dtype=F32)[:, None, :, None] > 0

    def rows_to(a, rows):
        return jnp.pad(a, ((0, 0), (0, rows - a.shape[1]), (0, 0)))

    s256 = jnp.stack([p[n] for n in ("conv_a_b", "norm_a_g", "norm_a_b", "pool_scale", "sgu_ln_g", "sgu_ln_b")], axis=1)
    pw = jnp.where(same_head, p["pool_w"][:, :, :, None, :], 0.0).reshape(L, GROUP, GROUP)
    return dict(
        caw=rows_to(conv_full[:, :KA], 32), cbw=rows_to(conv_full[:, KA:], 8), s256=rows_to(s256, 8),
        pw=pw.astype(BF16),
        wm=jnp.transpose(p["sgu_w"], (0, 2, 1, 3)).reshape(L, SGU_BLOCK, 4 * SGU_BLOCK),
        wmt=jnp.transpose(p["sgu_w"], (0, 1, 3, 2)).reshape(L, 4 * SGU_BLOCK, SGU_BLOCK),
        sb=jnp.repeat(jnp.transpose(p["sgu_bias"], (0, 2, 1)), HEAD, axis=2),
        v1024=rows_to(jnp.stack([p["b_out"], p["ln_g"], p["ln_b"]], axis=1), 8),
        bin=p["b_in"][:, None, :])


def _unpack_small(sm):
    L = sm.shape[0]
    owc = jnp.concatenate([sm[:, ROW_WC:ROW_WC + SGU_BLOCK], sm[:, ROW_WC + SGU_BLOCK:ROW_WC + 2 * SGU_BLOCK]], axis=2)
    return dict(
        conv_a_b=sm[:, 0], norm_a_g=sm[:, 1], norm_a_b=sm[:, 2], pool_scale=sm[:, 3], sgu_ln_g=sm[:, 4],
        sgu_ln_b=sm[:, 5], conv_b_w=sm[:, ROW_CBW:ROW_CBW + KB], conv_a_w=sm[:, ROW_CAW:ROW_CAW + KA],
        pool_w=jnp.transpose(sm[:, ROW_PW:ROW_PW + HEAD].reshape(L, HEAD, 4, HEAD), (0, 2, 1, 3)),
        ln_g=sm[:, ROW_LNG:ROW_LNG + 4].reshape(L, D_MODEL), ln_b=sm[:, ROW_LNB:ROW_LNB + 4].reshape(L, D_MODEL),
        b_out=sm[:, ROW_BOUT:ROW_BOUT + 4].reshape(L, D_MODEL),
        b_in=sm[:, ROW_BIN:ROW_BIN + N_SLICES].reshape(L, IN_WIDTH),
        sgu_w=jnp.transpose(owc.reshape(L, SGU_BLOCK, 4, SGU_BLOCK), (0, 2, 1, 3)),
        sgu_bias=sm[:, ROW_SB:ROW_SB + 4, 0:SGU_BLOCK])


def _step(p, m, v, x, target, *, tile_f, tile_b, k_steps):
    L = p["ln_g"].shape[0]
    xi, yi, ci = _place()
    me_k = 2 * xi + yi
    hi_rows, ho_rows = D_MODEL // 2, GROUP // 2

    cw = jnp.concatenate([p["conv_a_w"], p["conv_b_w"]], axis=1).reshape(-1, 128)
    cw_rows = cw.shape[0]
    cw = _pad_rows(cw, -(-cw_rows // SUBLANES) * SUBLANES)
    wi16 = p["w_in"].astype(BF16)
    wo16 = p["w_out"].astype(BF16)
    wig0, wog0, cwg = _gather_weights(wi16[0:1], wo16[0:1], cw)
    cwg = cwg[:, :cw_rows].reshape(N_CHIPS, L, KA + KB, HEAD)
    conv_full = jnp.transpose(cwg, (1, 2, 0, 3)).reshape(L, KA + KB, GROUP)
    seg, e4 = _indicator_consts()
    k = _layer_consts(p, conv_full)
    layer = [jnp.full((1,), l, jnp.int32) for l in range(L)]

    hcur = x
    saved, wig, wog = [], [wig0[0]], [wog0[0]]
    for l in range(L):
        nxt = (wi16, wo16) if l + 1 < L else None
        outs = _fwd_layer(layer[l], hcur, wig[l], k["bin"], k["caw"], k["cbw"], k["s256"], seg, k["pw"], k["wm"], k["sb"],
                          wog[l], k["v1024"], tile=tile_f, nxt=nxt, target=None if nxt is not None else target)
        y, xb, h, aux, mixb, z = outs[0:6]
        if nxt is not None:
            wig.append(outs[6])
            wog.append(outs[7])
        saved.append((xb, h, aux, mixb, z))
        hcur = y

    dy = hcur
    loss_local = outs[6][0, 0]

    gwi = lax.empty((L, N_CHIPS, D_MODEL, COLS), F32)
    gwo = lax.empty((L, N_CHIPS, GROUP, D_MODEL), F32)
    gwi16 = lax.empty((L, N_CHIPS, D_MODEL, COLS), BF16)
    gwo16 = lax.empty((L, N_CHIPS, GROUP, D_MODEL), BF16)
    p_i = lax.empty((L, N_CHIPS, hi_rows, COLS), BF16)
    p_o = lax.empty((L, N_CHIPS, ho_rows, D_MODEL), BF16)
    q_i = lax.empty((3, L, hi_rows, COLS), BF16)
    q_o = lax.empty((3, L, ho_rows, D_MODEL), BF16)
    r_sm = [None] * L
    pending = None
    for l in reversed(range(L)):
        xb, h, aux, mixb, z = saved[l]
        exch = None if pending is None else (p_i, p_o, pending, q_i, q_o)
        outs = _bwd_layer(layer[l], dy, z, h, aux, wig[l], k["caw"], k["cbw"], k["s256"], seg, k["pw"], k["wm"],
                          k["wmt"], k["sb"], wog[l], k["v1024"], e4, tile=tile_b, exch=exch)
        dy, dhb, dzb, osm = outs[0:4]
        if l == L - 1:
            osm = osm.at[ROW_LOSS, 0].set(loss_local)
        if exch is not None:
            q_i, q_o, r_sm[l + 1] = outs[4:7]
        larr = layer[l]
        outs = _dw(larr, xb, dhb, mixb, dzb, gwi, gwi16, gwo, gwo16, k_steps=k_steps, small=osm if l == 0 else None)
        gwi, gwi16, gwo, gwo16 = outs[0:4]
        if l == 0:
            r_sm[0] = outs[4]
        cl_arr = jnp.stack([ci, jnp.int32(l)]).astype(jnp.int32)
        if l > 0:
            p_i, p_o = _swap_add(cl_arr, gwi, gwi16, p_i, gwo, gwo16, p_o)
        else:
            p_i, p_o, q_i, q_o = _swap_add(cl_arr, gwi, gwi16, p_i, gwo, gwo16, p_o, send_on=(q_i, q_o))
        pending = osm
    grad_x = dy

    summed = _sum_small(r_sm)
    loss = summed[L - 1, ROW_LOSS, 0]
    grads = _unpack_small(summed)
    for n in ("conv_a_w", "conv_b_w"):
        grads[n] = lax.dynamic_slice_in_dim(grads[n], me_k * HEAD, HEAD, axis=2)

    kc_arr = jnp.stack([me_k, ci]).astype(jnp.int32)
    g_i, g_o = _sum_chunks(kc_arr, p_i, q_i, p_o, q_o, nb=2)
    g_i, g_o = _share_result(g_i, g_o)
    grads["w_in"] = g_i
    grads["w_out"] = g_o

    delta, new_m, new_v = {}, {}, {}
    for n, tr in (("w_in", 512), ("w_out", 256)):
        shp = p[n].shape
        args = [a.reshape(shp[0] * shp[1], shp[2]) for a in (p[n], grads[n], m[n], v[n])]
        outs = _adamw(*args, rows_per_step=tr, name="adamw_" + n, copy_g=True)
        delta[n], new_m[n], new_v[n], grads[n] = (a.reshape(shp) for a in outs)
    small = [n for n in WEIGHTS if n not in ("w_in", "w_out")]
    flat = [[a[n].reshape(-1, a[n].shape[-1]) for n in small] for a in (p, grads, m, v)]
    outs = _adamw_small(*flat)
    for j, n in enumerate(small):
        delta[n], new_m[n], new_v[n] = (o[j].reshape(p[n].shape) for o in outs)

    return (loss, grad_x[None], *[grads[n] for n in WEIGHTS], *[delta[n] for n in WEIGHTS],
            *[new_m[n] for n in WEIGHTS], *[new_v[n] for n in WEIGHTS])


def kernel(x, ln_g, ln_b, w_in, b_in, conv_a_w, conv_a_b, norm_a_g, norm_a_b, conv_b_w, pool_w, pool_scale, sgu_ln_g, sgu_ln_b, sgu_w, sgu_bias, w_out, b_out, loss_target, m_ln_g, m_ln_b, m_w_in, m_b_in, m_conv_a_w, m_conv_a_b, m_norm_a_g, m_norm_a_b, m_conv_b_w, m_pool_w, m_pool_scale, m_sgu_ln_g, m_sgu_ln_b, m_sgu_w, m_sgu_bias, m_w_out, m_b_out, v_ln_g, v_ln_b, v_w_in, v_b_in, v_conv_a_w, v_conv_a_b, v_norm_a_g, v_norm_a_b, v_conv_b_w, v_pool_w, v_pool_scale, v_sgu_ln_g, v_sgu_ln_b, v_sgu_w, v_sgu_bias, v_w_out, v_b_out):
    p = dict(ln_g=ln_g, ln_b=ln_b, w_in=w_in, b_in=b_in, conv_a_w=conv_a_w, conv_a_b=conv_a_b, norm_a_g=norm_a_g,
             norm_a_b=norm_a_b, conv_b_w=conv_b_w, pool_w=pool_w, pool_scale=pool_scale, sgu_ln_g=sgu_ln_g,
             sgu_ln_b=sgu_ln_b, sgu_w=sgu_w, sgu_bias=sgu_bias, w_out=w_out, b_out=b_out)
    m = dict(ln_g=m_ln_g, ln_b=m_ln_b, w_in=m_w_in, b_in=m_b_in, conv_a_w=m_conv_a_w, conv_a_b=m_conv_a_b,
             norm_a_g=m_norm_a_g, norm_a_b=m_norm_a_b, conv_b_w=m_conv_b_w, pool_w=m_pool_w, pool_scale=m_pool_scale,
             sgu_ln_g=m_sgu_ln_g, sgu_ln_b=m_sgu_ln_b, sgu_w=m_sgu_w, sgu_bias=m_sgu_bias, w_out=m_w_out, b_out=m_b_out)
    v = dict(ln_g=v_ln_g, ln_b=v_ln_b, w_in=v_w_in, b_in=v_b_in, conv_a_w=v_conv_a_w, conv_a_b=v_conv_a_b,
             norm_a_g=v_norm_a_g, norm_a_b=v_norm_a_b, conv_b_w=v_conv_b_w, pool_w=v_pool_w, pool_scale=v_pool_scale,
             sgu_ln_g=v_sgu_ln_g, sgu_ln_b=v_sgu_ln_b, sgu_w=v_sgu_w, sgu_bias=v_sgu_bias, w_out=v_w_out, b_out=v_b_out)
    return _step(p, m, v, x[0], loss_target[0], tile_f=256, tile_b=256, k_steps=4)
```

```python
import jax
import jax.numpy as jnp
from jax import lax
from jax.experimental import pallas as pl
from jax.experimental.pallas import tpu as pltpu

F32 = jnp.float32
BF16 = jnp.bfloat16
MESH = pl.DeviceIdType.MESH

D_MODEL = 1024
GROUP = 256
HEAD = 64
N_SLICES = 12
IN_WIDTH = N_SLICES * GROUP
N_CHIPS = 4
COLS = IN_WIDTH // N_CHIPS
KA = 31
KB = 3
SUBLANES = 8
HALO_A, HALO_B, HALO_C = 32, 8, 16
N_GATHER_SEMS = 12
N_EXCH_SEMS = 13
SGU_BLOCK = 128
CHUNK = 64
LN_EPS = 1e-5
ROWS = 64
V7X_VMEM_BYTES = 64 * 1024 * 1024
VMEM_LIMIT = V7X_VMEM_BYTES - 8 * 1024 * 1024

ADAM_LR, ADAM_B1, ADAM_B2, ADAM_EPS, ADAM_WD, ADAM_STEP = 0.001, 0.9, 0.999, 1e-08, 0.01, 10


ANY = pl.BlockSpec(memory_space=pl.ANY)


def _vmem_params(**kw):
    return pltpu.CompilerParams(vmem_limit_bytes=VMEM_LIMIT, **kw)


def _whole(a):
    return pl.BlockSpec(a.shape, lambda i, l, _n=a.ndim: (0,) * _n)


def _of_layer(a):
    return pl.BlockSpec((None,) + a.shape[1:], lambda i, l, _n=a.ndim: (l[0],) + (0,) * (_n - 1))


def _place():
    return lax.axis_index("x"), lax.axis_index("y"), lax.axis_index("c")


def _other_chips(x, y):
    return [(1 - x, y, 2 * (1 - x) + y), (x, 1 - y, 2 * x + (1 - y)), (1 - x, 1 - y, 2 * (1 - x) + (1 - y))]


PEERS_SIBLING, PEERS_COLUMN, PEERS_ALL = "sibling", "sibling and the same core of the other chips", "all"
COLLECTIVE_ID = dict(share_result=0, swap_add=1, gather_weights=2, fwd_layer_gather=3, swap_add_send=4,
                     bwd_layer_exchange=5, dw_exchange=6)


def _handshake(peers):
    x, y, c = _place()
    if peers == PEERS_SIBLING:
        ids = [(x, y, 1 - c)]
    elif peers == PEERS_COLUMN:
        ids = [(x, y, 1 - c)] + [(px, py, c) for px, py, _ in _other_chips(x, y)]
    else:
        ids = [(1 - x if r & 4 else x, 1 - y if r & 2 else y, 1 - c if r & 1 else c) for r in range(1, 8)]
    barrier = pltpu.get_barrier_semaphore()
    for to in ids:
        pl.semaphore_signal(barrier, inc=1, device_id=to, device_id_type=MESH)
    pl.semaphore_wait(barrier, len(ids))


def _sibling_handshake():
    _handshake(PEERS_SIBLING)


def _sig(v):
    return 0.5 * jnp.tanh(0.5 * v) + 0.5


def _dot(a, b):
    return jnp.dot(a, b, preferred_element_type=F32)


def _dot_nt(a, b):
    return lax.dot_general(a, b, (((1,), (1,)), ((), ())), preferred_element_type=F32)


def _dot_tn(a, b):
    return lax.dot_general(a, b, (((0,), (0,)), ((), ())), preferred_element_type=F32)


def _segdot(v, m):
    hi = v.astype(BF16)
    lo = (v - hi.astype(F32)).astype(BF16)
    return _dot(hi, m) + _dot(lo, m)


def _colsum(v):
    return jnp.sum(v, axis=0, keepdims=True)


def _rowmean(v):
    return jnp.mean(v, axis=-1, keepdims=True)


def _lane_group(n):
    return lax.broadcasted_iota(jnp.int32, (1, n), 1) // HEAD


def _pool_cnt(tile, t_rows):
    pos = tile * t_rows + lax.broadcasted_iota(jnp.int32, (t_rows, GROUP), 0) + 1
    grp = lax.broadcasted_iota(jnp.int32, (t_rows, GROUP), 1) // HEAD
    win = jnp.where(grp == 0, 2, jnp.where(grp == 1, 4, jnp.where(grp == 2, 8, 16)))
    return jnp.minimum(pos, win).astype(F32)


def _sgu_masks(wm_ref, wmt_ref, wm_s, wmt_s):
    r = lax.broadcasted_iota(jnp.int32, (SGU_BLOCK, 4 * SGU_BLOCK), 0) // CHUNK
    c = (lax.broadcasted_iota(jnp.int32, (SGU_BLOCK, 4 * SGU_BLOCK), 1) % SGU_BLOCK) // CHUNK
    wm_s[...] = jnp.where(c <= r, wm_ref[...], 0.0).astype(BF16)
    if wmt_ref is not None:
        rt = (lax.broadcasted_iota(jnp.int32, (4 * SGU_BLOCK, SGU_BLOCK), 0) % SGU_BLOCK) // CHUNK
        ct = lax.broadcasted_iota(jnp.int32, (4 * SGU_BLOCK, SGU_BLOCK), 1) // CHUNK
        wmt_s[...] = jnp.where(rt <= ct, wmt_ref[...], 0.0).astype(BF16)


def _vstack(v_blk):
    grp = _lane_group(GROUP)
    return jnp.concatenate([jnp.where(grp == h, v_blk, 0.0) for h in range(4)], axis=0).astype(BF16)


def _gather_next(step, nt, nwi, nwo, gwi, gwo, send_sems, recv_sems, loc_sems, vwi, vwo):
    x, y, c = _place()
    me_k = 2 * x + y
    sibling = (x, y, 1 - c)
    chips = _other_chips(x, y)
    hi, ho = D_MODEL // 2, GROUP // 2
    fwd_sems = N_GATHER_SEMS // 2

    def rc(src, dst, sem, to):
        return pltpu.make_async_remote_copy(src_ref=src, dst_ref=dst, send_sem=send_sems.at[sem],
                                            recv_sem=recv_sems.at[sem], device_id=to, device_id_type=MESH)

    def blk(ref, k, n, cc):
        return ref.at[k, pl.ds(cc * n, n), :]

    def ici(r):
        px, py, _ = chips[r]
        to = (px, py, c)
        return [rc(nwi.at[pl.ds(c * hi, hi), :], blk(gwi, me_k, hi, c), 2 * r, to),
                rc(nwo.at[pl.ds(c * ho, ho), :], blk(gwo, me_k, ho, c), 2 * r + 1, to)]

    def landed(r, cc, base):
        pk = chips[r][2]
        return [rc(blk(gwi, pk, hi, cc), blk(gwi, pk, hi, cc), base + 2 * r, sibling),
                rc(blk(gwo, pk, ho, cc), blk(gwo, pk, ho, cc), base + 2 * r + 1, sibling)]

    def stage_in():
        return [pltpu.make_async_copy(nwi, vwi, loc_sems.at[0]), pltpu.make_async_copy(nwo, vwo, loc_sems.at[1])]

    def local():
        return [pltpu.make_async_copy(vwi, gwi.at[me_k], loc_sems.at[2]),
                pltpu.make_async_copy(vwo, gwo.at[me_k], loc_sems.at[3])]

    @pl.when(step == 0)
    def _():
        _handshake(PEERS_COLUMN)
        for cp in stage_in():
            cp.start()
        for r in range(3):
            for cp in ici(r):
                cp.start()

    @pl.when(step == 1)
    def _():
        for cp in stage_in():
            cp.wait()
        for cp in local():
            cp.start()

    @pl.when(step == (3 * nt) // 4)
    def _():
        for r in range(3):
            for got, fwd in zip(landed(r, c, 0), landed(r, c, fwd_sems)):
                got.wait_recv()
                fwd.start()

    @pl.when(step == nt - 1)
    def _():
        for r in range(3):
            for got in landed(r, 1 - c, fwd_sems):
                got.wait_recv()
        for r in range(3):
            for cp in ici(r) + landed(r, c, fwd_sems):
                cp.wait_send()
        for cp in local():
            cp.wait()


def _fwd_layer(larr, x, wi, bin_, caw, cbw, s256, seg, pw, wm, sb, wo, v1024, *, tile, nxt=None, target=None):
    assert nxt is None or target is None
    S = x.shape[0]
    T = tile
    nt = S // T
    alpha = float((2.0 * 4) ** 0.25)
    n_in = 13 + (2 if nxt is not None else 0) + (1 if target is not None else 0)
    n_out = 6 + (2 if nxt is not None else 0) + (1 if target is not None else 0)

    def body(*refs):
        l_ref = refs[0]
        (x_ref, wi_ref, bin_ref, caw_ref, cbw_ref, s256_ref, seg_ref, pw_ref, wm_ref, sb_ref, wo_ref,
         v1024_ref) = refs[1:13]
        y_ref, xb_ref, h_ref, aux_ref, mix_ref, z_ref = refs[n_in:n_in + 6]
        abuf, bbuf, cbuf, wm_s, shf = refs[n_in + n_out:n_in + n_out + 5]
        i = pl.program_id(0)
        if nxt is not None:
            _gather_next(i, nt, refs[13].at[l_ref[0] + 1], refs[14].at[l_ref[0] + 1], refs[n_in + 6], refs[n_in + 7],
                         *refs[n_in + n_out + 5:])

        @pl.when(i == 0)
        def _():
            abuf[0:HALO_A, :] = jnp.zeros((HALO_A, GROUP), F32)
            bbuf[0:HALO_B, :] = jnp.zeros((HALO_B, GROUP), F32)
            cbuf[0:HALO_C, :] = jnp.zeros((HALO_C, GROUP), F32)
            _sgu_masks(wm_ref, None, wm_s, None)

        x = x_ref[...]
        xb = x.astype(BF16)
        xb_ref[...] = xb
        for k in range(N_CHIPS):
            h_ref[:, COLS * k:COLS * (k + 1)] = _dot(xb, wi_ref[k]) + bin_ref[:, COLS * k:COLS * (k + 1)]

        def hs(j):
            return h_ref[:, GROUP * j:GROUP * (j + 1)]

        abuf[HALO_A:HALO_A + T, :] = hs(0) * _sig(hs(1))
        span = T + HALO_A - SUBLANES
        for p in range(1, SUBLANES):
            shf[p - 1, :, :] = abuf[p:p + span, :]
        for r0 in range(0, T, ROWS):
            acc = None
            for k in range(KA):
                off = HALO_A - (KA - 1) + k
                p, q8 = off % SUBLANES, off - off % SUBLANES
                win = abuf[r0 + q8:r0 + q8 + ROWS, :] if p == 0 else shf[p - 1, r0 + q8:r0 + q8 + ROWS, :]
                term = caw_ref[k:k + 1, :] * win
                acc = term if acc is None else acc + term
            aux_ref[r0:r0 + ROWS, 0:GROUP] = acc + s256_ref[0:1, :]
        abuf[0:HALO_A, :] = abuf[T:T + HALO_A, :]
        a1 = aux_ref[:, 0:GROUP]
        segm = seg_ref[...]
        cen = a1 - _segdot(a1, segm)
        var = _segdot(cen * cen, segm)
        a2 = cen * lax.rsqrt(var + LN_EPS) * s256_ref[1:2, :] + s256_ref[2:3, :]
        az = hs(2)
        mix_ref[:, 0:GROUP] = (a2 * _sig(a2) * (az * _sig(az))).astype(BF16)

        bbuf[HALO_B:HALO_B + T, :] = hs(4) * hs(5)
        for r0 in range(0, T, ROWS):
            acc = None
            for k in range(KB):
                off = HALO_B - (KB - 1) + k + r0
                term = cbw_ref[k:k + 1, :] * bbuf[off:off + ROWS, :]
                acc = term if acc is None else acc + term
            aux_ref[r0:r0 + ROWS, GROUP:2 * GROUP] = acc
        bbuf[0:HALO_B, :] = bbuf[T:T + HALO_B, :]
        bz = hs(6)
        mix_ref[:, GROUP:2 * GROUP] = (hs(3) * aux_ref[:, GROUP:2 * GROUP] * (bz * _sig(bz))).astype(BF16)

        ch = hs(7)
        cbuf[HALO_C:HALO_C + T, :] = ch
        hi_lane = (lax.broadcasted_iota(jnp.int32, (1, 128), 1) // HEAD) == 1
        for r0 in range(0, T, ROWS):
            def win(col, j0, j1):
                s = None
                for j in range(j0, j1):
                    off = HALO_C - j + r0
                    term = cbuf[off:off + ROWS, 128 * col:128 * (col + 1)]
                    s = term if s is None else s + term
                return s
            w0 = win(0, 0, 2) + jnp.where(hi_lane, win(0, 2, 4), 0.0)
            w1 = win(1, 0, 8) + jnp.where(hi_lane, win(1, 8, 16), 0.0)
            aux_ref[r0:r0 + ROWS, 2 * GROUP:2 * GROUP + 128] = w0
            aux_ref[r0:r0 + ROWS, 2 * GROUP + 128:3 * GROUP] = w1
        cbuf[0:HALO_C, :] = cbuf[T:T + HALO_C, :]
        pooled = aux_ref[:, 2 * GROUP:3 * GROUP] / _pool_cnt(i, T) - ch
        aux_ref[:, 2 * GROUP:3 * GROUP] = pooled
        q = _dot(pooled.astype(BF16), pw_ref[...])
        cz = hs(8)
        mix_ref[:, 2 * GROUP:3 * GROUP] = (q * s256_ref[3:4, :] * (cz * _sig(cz))).astype(BF16)

        dv = hs(10)
        cen = dv - _rowmean(dv)
        var = _rowmean(cen * cen)
        v = cen * lax.rsqrt(var + LN_EPS) * s256_ref[4:5, :] + s256_ref[5:6, :]
        sps = []
        for n in range(T // SGU_BLOCK):
            vb = v[n * SGU_BLOCK:(n + 1) * SGU_BLOCK, :]
            sps.append(_dot(wm_s[...], _vstack(vb)) + sb_ref[...])
        sp = jnp.concatenate(sps, axis=0)
        dz = hs(11)
        mix_ref[:, 3 * GROUP:4 * GROUP] = (hs(9) * sp * (dz * _sig(dz))).astype(BF16)

        out = v1024_ref[0:1, :]
        for k in range(N_CHIPS):
            out = out + _dot(mix_ref[:, GROUP * k:GROUP * (k + 1)], wo_ref[k])
        z = alpha * x + out
        z_ref[...] = z
        cen = z - _rowmean(z)
        var = _rowmean(cen * cen)
        y = cen * lax.rsqrt(var + LN_EPS) * v1024_ref[1:2, :] + v1024_ref[2:3, :]
        if target is None:
            y_ref[...] = y
        else:
            t_ref, loss_ref = refs[13], refs[n_in + 6]

            @pl.when(i == 0)
            def _():
                loss_ref[...] = jnp.zeros_like(loss_ref)
            err = y - t_ref[...]
            y_ref[...] = err * (1.0 / D_MODEL)
            loss_ref[...] += jnp.sum(_colsum(err * err), axis=1, keepdims=True) * (0.5 / D_MODEL)

    def rows(width):
        return pl.BlockSpec((T, width), lambda i, l: (i, 0))

    consts = (wi, bin_, caw, cbw, s256, seg, pw, wm, sb, wo, v1024)
    in_specs = [rows(D_MODEL)] + [_whole(a) if a is wi or a is seg or a is wo else _of_layer(a) for a in consts]
    out_specs = [rows(D_MODEL), rows(D_MODEL), rows(IN_WIDTH), rows(3 * GROUP), rows(D_MODEL), rows(D_MODEL)]
    out_shape = [jax.ShapeDtypeStruct((S, D_MODEL), F32), jax.ShapeDtypeStruct((S, D_MODEL), BF16),
                 jax.ShapeDtypeStruct((S, IN_WIDTH), F32), jax.ShapeDtypeStruct((S, 3 * GROUP), F32),
                 jax.ShapeDtypeStruct((S, D_MODEL), BF16), jax.ShapeDtypeStruct((S, D_MODEL), F32)]
    scratch = [pltpu.VMEM((T + HALO_A, GROUP), F32), pltpu.VMEM((T + HALO_B, GROUP), F32),
               pltpu.VMEM((T + HALO_C, GROUP), F32), pltpu.VMEM((SGU_BLOCK, 4 * SGU_BLOCK), BF16),
               pltpu.VMEM((SUBLANES - 1, T + HALO_A - SUBLANES, GROUP), F32)]
    extra = ()
    if nxt is not None:
        extra = tuple(nxt)
        in_specs += [ANY, ANY]
        out_specs += [ANY, ANY]
        out_shape += [jax.ShapeDtypeStruct((N_CHIPS, D_MODEL, COLS), BF16),
                      jax.ShapeDtypeStruct((N_CHIPS, GROUP, D_MODEL), BF16)]
        scratch += [pltpu.SemaphoreType.DMA((N_GATHER_SEMS,)), pltpu.SemaphoreType.DMA((N_GATHER_SEMS,)),
                    pltpu.SemaphoreType.DMA((4,)), pltpu.VMEM((D_MODEL, COLS), BF16), pltpu.VMEM((GROUP, D_MODEL), BF16)]
    if target is not None:
        extra = (target,)
        in_specs += [rows(D_MODEL)]
        out_specs += [pl.BlockSpec((8, 128), lambda i, l: (0, 0))]
        out_shape += [jax.ShapeDtypeStruct((8, 128), F32)]
    grid_spec = pltpu.PrefetchScalarGridSpec(num_scalar_prefetch=1, grid=(nt,), in_specs=in_specs,
                                             out_specs=out_specs, scratch_shapes=scratch)
    return pl.pallas_call(
        body, name=("fwd_layer_loss" if target is not None else "fwd_layer") if nxt is None else "fwd_layer_gather",
        grid_spec=grid_spec, out_shape=out_shape,
        compiler_params=_vmem_params(dimension_semantics=("arbitrary",), **(
            dict(has_side_effects=True, collective_id=COLLECTIVE_ID["fwd_layer_gather"]) if nxt is not None else {})),
    )(larr, x, *consts, *extra)


ROW_CBW = 8
ROW_CAW = 16
ROW_LOSS = 7
ROW_PW = 48
ROW_LNG = 112
ROW_LNB = 116
ROW_BOUT = 120
ROW_BIN = 124
ROW_WC = 136
ROW_SB = 392
SM_ROWS = 400
N_DEV = 8


def _exchange_comm(start, finish, l, p_i, p_o, sm, r_i, r_o, r_sm, send_sems, recv_sems, loc_sem):
    x, y, c = _place()
    me = 4 * x + 2 * y + c
    chips = _other_chips(x, y)

    def rc(src, dst, sem, to):
        return pltpu.make_async_remote_copy(src_ref=src, dst_ref=dst, send_sem=send_sems.at[sem],
                                            recv_sem=recv_sems.at[sem], device_id=to, device_id_type=MESH)

    def big(r):
        px, py, pk = chips[r]
        to = (px, py, c)
        return [rc(p_i.at[l, pk], r_i.at[r, l], 2 * r, to), rc(p_o.at[l, pk], r_o.at[r, l], 2 * r + 1, to)]

    def peer(rel):
        px = 1 - x if rel & 4 else x
        py = 1 - y if rel & 2 else y
        pc = 1 - c if rel & 1 else c
        return (px, py, pc), 4 * px + 2 * py + pc

    def small_out(rel):
        to, _ = peer(rel)
        return rc(sm, r_sm.at[me], N_EXCH_SEMS - N_DEV + rel, to)

    def small_in(rel):
        to, idx = peer(rel)
        return rc(sm, r_sm.at[idx], N_EXCH_SEMS - N_DEV + rel, to)

    def local():
        return pltpu.make_async_copy(sm, r_sm.at[me], loc_sem.at[0])

    with_big, with_small = p_i is not None, sm is not None

    @pl.when(start)
    def _():
        _handshake(PEERS_ALL)
        if with_small:
            local().start()
        if with_big:
            for r in range(3):
                for cp in big(r):
                    cp.start()
        if with_small:
            for rel in range(1, N_DEV):
                small_out(rel).start()

    @pl.when(finish)
    def _():
        if with_big:
            for r in range(3):
                for cp in big(r):
                    cp.wait()
        if with_small:
            for rel in range(1, N_DEV):
                small_in(rel).wait_recv()
                small_out(rel).wait_send()
            local().wait()


RC = 32
RC_WIDE = 16
ACC_ROWS = 136


def _rsum8(v):
    r = v[0:8]
    for j in range(1, v.shape[0] // 8):
        r = r + v[8 * j:8 * j + 8]
    return r


def _bwd_layer(larr, dy, z, h, aux, wi, caw, cbw, s256, seg, pw, wm, wmt, sb, wo, v1024, e4, *, tile, exch=None):
    S = dy.shape[0]
    T = tile
    nt = S // T
    nblk = T // SGU_BLOCK
    alpha = float((2.0 * 4) ** 0.25)
    n_in = 17 + (5 if exch is not None else 0)
    n_out = 4 + (3 if exch is not None else 0)
    slab = pltpu.VMEM((T, GROUP), F32)
    scratch = dict(
        dbuf=pltpu.VMEM((T + HALO_A, GROUP), F32), ebuf=pltpu.VMEM((T + HALO_B, GROUP), F32),
        fbuf=pltpu.VMEM((T + HALO_C, GROUP), F32), sh=pltpu.VMEM((SUBLANES - 1, T + HALO_A - SUBLANES, GROUP), F32),
        wm_s=pltpu.VMEM((SGU_BLOCK, 4 * SGU_BLOCK), BF16), wmt_s=pltpu.VMEM((4 * SGU_BLOCK, SGU_BLOCK), BF16),
        dsp_acc=pltpu.VMEM((SGU_BLOCK, GROUP), F32), pw_acc=pltpu.VMEM((GROUP, GROUP), F32),
        acc_s=pltpu.VMEM((8 * ACC_ROWS, GROUP), F32), acc_w=pltpu.VMEM((24, D_MODEL), F32),
        dmix_s=pltpu.VMEM((T, D_MODEL), F32), vst_s=pltpu.VMEM((nblk, 4 * SGU_BLOCK, GROUP), BF16),
        dq_s=pltpu.VMEM((T, GROUP), BF16), dxt_s=pltpu.VMEM((D_MODEL, T), F32),
        mean_s=slab, t1_s=slab, t2_s=slab, q_s=slab, xv_s=slab, rv_s=slab, v_s=slab, sp_s=slab, a0_s=slab, sg_s=slab,
        xh_s=slab, ra_s=slab, ub_s=slab, dsp_s=slab, m1_s=slab, m2_s=slab, dpool_s=slab, dvd_s=slab, u_s=slab,
        du_s=slab, cw_s=slab)
    names = list(scratch)

    def body(*refs):
        (dy_ref, z_ref, h_ref, aux_ref, wi_ref, caw_ref, cbw_ref, s256_ref, seg_ref, pw_ref, wm_ref, wmt_ref,
         sb_ref, wo_ref, v1024_ref, e4_ref) = refs[1:17]
        dx_ref, dhb_ref, dzb_ref, osm_ref = refs[n_in:n_in + 4]
        k0 = n_in + n_out
        sc = dict(zip(names, refs[k0:k0 + len(names)]))
        dbuf, ebuf, fbuf, sh = sc["dbuf"], sc["ebuf"], sc["fbuf"], sc["sh"]
        wm_s, wmt_s, dsp_acc, pw_acc, acc_s, acc_w = (sc[n] for n in ("wm_s", "wmt_s", "dsp_acc", "pw_acc", "acc_s",
                                                                        "acc_w"))
        dmix_s, vst_s, dq_s = sc["dmix_s"], sc["vst_s"], sc["dq_s"]
        i = pl.program_id(0)
        tile_idx = nt - 1 - i
        if exch is not None:
            p_i, p_o, sm = refs[17:20]
            r_i, r_o, r_sm = refs[n_in + 4:n_in + 7]
            _exchange_comm(i == 0, i == nt - 1, refs[0][0] + 1, p_i, p_o, sm, r_i, r_o, r_sm, *refs[k0 + len(names):])

        @pl.when(i == 0)
        def _():
            dbuf[T:T + HALO_A, :] = jnp.zeros((HALO_A, GROUP), F32)
            ebuf[T:T + HALO_B, :] = jnp.zeros((HALO_B, GROUP), F32)
            fbuf[T:T + HALO_C, :] = jnp.zeros((HALO_C, GROUP), F32)
            _sgu_masks(wm_ref, wmt_ref, wm_s, wmt_s)
            osm_ref[...] = jnp.zeros_like(osm_ref)
            dsp_acc[...] = jnp.zeros_like(dsp_acc)
            pw_acc[...] = jnp.zeros_like(pw_acc)
            acc_s[...] = jnp.zeros_like(acc_s)
            acc_w[...] = jnp.zeros_like(acc_w)

        def chunks(rc, fn):
            for c in range(T // rc):
                fn(pl.ds(c * rc, rc))

        def hs(j, rows):
            return h_ref[rows, GROUP * j:GROUP * (j + 1)]

        def acc_add(row, val):
            acc_s[8 * row:8 * row + 8, :] += _rsum8(val)

        def put_dh(j, rows, val):
            acc_add(ROW_BIN + j, val)
            dhb_ref[rows, GROUP * j:GROUP * (j + 1)] = val.astype(BF16)

        def dsilu(v, s):
            return s * (1.0 + v * (1.0 - s))

        def vec(r):
            return s256_ref[r:r + 1, :]

        def ln_bwd(rows):
            dyc = dy_ref[rows, :]
            zc = z_ref[rows, :]
            cen = zc - _rowmean(zc)
            rstd = lax.rsqrt(_rowmean(cen * cen) + LN_EPS)
            xhat = cen * rstd
            acc_w[0:8, :] += _rsum8(dyc * xhat)
            acc_w[8:16, :] += _rsum8(dyc)
            gdy = dyc * v1024_ref[1:2, :]
            dz = rstd * (gdy - _rowmean(gdy) - xhat * _rowmean(gdy * xhat))
            acc_w[16:24, :] += _rsum8(dz)
            dzb_ref[rows, :] = dz.astype(BF16)
            dx_ref[rows, :] = alpha * dz
        chunks(RC_WIDE, ln_bwd)

        segm = seg_ref[...]
        dzb = dzb_ref[...]
        for k in range(N_CHIPS):
            dmix_s[:, GROUP * k:GROUP * (k + 1)] = _dot_nt(dzb, wo_ref[k])
        sc["mean_s"][...] = _segdot(aux_ref[:, 0:GROUP], segm)
        pooled_b = aux_ref[:, 2 * GROUP:3 * GROUP].astype(BF16)
        sc["q_s"][...] = _dot(pooled_b, pw_ref[...])

        def centre(rows):
            cen = aux_ref[rows, 0:GROUP] - sc["mean_s"][rows, :]
            sc["t1_s"][rows, :] = cen * cen
            dv_in = hs(10, rows)
            cen_v = dv_in - _rowmean(dv_in)
            rstd_v = lax.rsqrt(_rowmean(cen_v * cen_v) + LN_EPS)
            xv = cen_v * rstd_v
            sc["xv_s"][rows, :] = xv
            sc["rv_s"][rows, :] = jnp.broadcast_to(rstd_v, xv.shape)
            sc["v_s"][rows, :] = xv * vec(4) + vec(5)
        chunks(RC, centre)

        sc["t2_s"][...] = _segdot(sc["t1_s"][...], segm)
        for n in range(nblk):
            blk = slice(n * SGU_BLOCK, (n + 1) * SGU_BLOCK)
            vst_s[n] = _vstack(sc["v_s"][blk, :])
            sc["sp_s"][blk, :] = _dot(wm_s[...], vst_s[n]) + sb_ref[...]

        def mixers(rows):
            a_val, a_glu, a_z = hs(0, rows), hs(1, rows), hs(2, rows)
            sg = _sig(a_glu)
            sc["a0_s"][rows, :] = a_val * sg
            sc["sg_s"][rows, :] = sg
            rstd_a = lax.rsqrt(sc["t2_s"][rows, :] + LN_EPS)
            xh = (aux_ref[rows, 0:GROUP] - sc["mean_s"][rows, :]) * rstd_a
            a2 = xh * vec(1) + vec(2)
            s2 = _sig(a2)
            sz = _sig(a_z)
            dya = dmix_s[rows, 0:GROUP]
            put_dh(2, rows, dya * (a2 * s2) * dsilu(a_z, sz))
            d_a2 = dya * (a_z * sz) * dsilu(a2, s2)
            acc_add(1, d_a2 * xh)
            acc_add(2, d_a2)
            gd = d_a2 * vec(1)
            sc["t1_s"][rows, :] = gd
            sc["t2_s"][rows, :] = gd * xh
            sc["xh_s"][rows, :] = xh
            sc["ra_s"][rows, :] = rstd_a
            b_b, b_c, b_h, b_z = hs(3, rows), hs(4, rows), hs(5, rows), hs(6, rows)
            cb = aux_ref[rows, GROUP:2 * GROUP]
            sz = _sig(b_z)
            dyb = dmix_s[rows, GROUP:2 * GROUP]
            put_dh(3, rows, dyb * cb * (b_z * sz))
            put_dh(6, rows, dyb * b_b * cb * dsilu(b_z, sz))
            ebuf[rows, :] = dyb * b_b * (b_z * sz)
            sc["ub_s"][rows, :] = b_c * b_h
            c_z = hs(8, rows)
            q = sc["q_s"][rows, :]
            sz = _sig(c_z)
            dyc = dmix_s[rows, 2 * GROUP:3 * GROUP]
            acc_add(3, dyc * q * (c_z * sz))
            put_dh(8, rows, dyc * q * vec(3) * dsilu(c_z, sz))
            dq_s[rows, :] = (dyc * vec(3) * (c_z * sz)).astype(BF16)
            d_u, d_z = hs(9, rows), hs(11, rows)
            sp = sc["sp_s"][rows, :]
            sz = _sig(d_z)
            dyd = dmix_s[rows, 3 * GROUP:4 * GROUP]
            put_dh(9, rows, dyd * sp * (d_z * sz))
            put_dh(11, rows, dyd * d_u * sp * dsilu(d_z, sz))
            sc["dsp_s"][rows, :] = dyd * d_u * (d_z * sz)
        chunks(RC, mixers)

        sc["m1_s"][...] = _segdot(sc["t1_s"][...], segm)
        sc["m2_s"][...] = _segdot(sc["t2_s"][...], segm)
        d_q = dq_s[...]
        pw_acc[...] += _dot_tn(pooled_b, d_q)
        sc["dpool_s"][...] = _dot_nt(d_q, pw_ref[...])
        grp = _lane_group(GROUP)
        for n in range(nblk):
            blk = slice(n * SGU_BLOCK, (n + 1) * SGU_BLOCK)
            dspb = sc["dsp_s"][blk, :]
            dsp_acc[...] += dspb
            dspb16 = dspb.astype(BF16)
            dvst = _dot(wmt_s[...], dspb16)
            dvb = None
            for hh in range(4):
                part = jnp.where(grp == hh, dvst[hh * SGU_BLOCK:(hh + 1) * SGU_BLOCK, :], 0.0)
                dvb = part if dvb is None else dvb + part
            sc["dvd_s"][blk, :] = dvb
            dwc = _dot_nt(dspb16, vst_s[n])
            osm_ref[ROW_WC:ROW_WC + SGU_BLOCK, :] += dwc[:, 0:GROUP]
            osm_ref[ROW_WC + SGU_BLOCK:ROW_WC + 2 * SGU_BLOCK, :] += dwc[:, GROUP:2 * GROUP]

        def ln_sums(rows):
            xh = sc["xh_s"][rows, :]
            d_a1 = sc["ra_s"][rows, :] * (sc["t1_s"][rows, :] - sc["m1_s"][rows, :] - xh * sc["m2_s"][rows, :])
            acc_add(0, d_a1)
            dbuf[rows, :] = d_a1
            pos = tile_idx * T + rows.start + lax.broadcasted_iota(jnp.int32, (RC, GROUP), 0) + 1
            lane = lax.broadcasted_iota(jnp.int32, (RC, GROUP), 1) // HEAD
            win = jnp.where(lane == 0, 2, jnp.where(lane == 1, 4, jnp.where(lane == 2, 8, 16)))
            fbuf[rows, :] = sc["dpool_s"][rows, :] / jnp.minimum(pos, win).astype(F32)
            d_v = sc["dvd_s"][rows, :]
            xv = sc["xv_s"][rows, :]
            acc_add(4, d_v * xv)
            acc_add(5, d_v)
            gd = d_v * vec(4)
            put_dh(10, rows, sc["rv_s"][rows, :] * (gd - _rowmean(gd) - xv * _rowmean(gd * xv)))
        chunks(RC, ln_sums)

        span = T + HALO_A - SUBLANES
        for p in range(1, SUBLANES):
            sh[p - 1, :, :] = dbuf[p:p + span, :]

        for r0 in range(0, T, ROWS):
            uc = sc["ub_s"][r0:r0 + ROWS, :]
            acc = None
            for k in range(KB):
                off = (KB - 1) - k + r0
                w = ebuf[off:off + ROWS, :]
                term = cbw_ref[k:k + 1, :] * w
                acc = term if acc is None else acc + term
                acc_add(ROW_CBW + k, uc * w)
            sc["du_s"][r0:r0 + ROWS, :] = acc
        ebuf[T:T + HALO_B, :] = ebuf[0:HALO_B, :]

        hi_lane = (lax.broadcasted_iota(jnp.int32, (1, 128), 1) // HEAD) == 1
        for r0 in range(0, T, ROWS):
            def win(col, j0, j1):
                s = None
                for j in range(j0, j1):
                    term = fbuf[r0 + j:r0 + j + ROWS, 128 * col:128 * (col + 1)]
                    s = term if s is None else s + term
                return s
            sc["cw_s"][r0:r0 + ROWS, 0:128] = win(0, 0, 2) + jnp.where(hi_lane, win(0, 2, 4), 0.0)
            sc["cw_s"][r0:r0 + ROWS, 128:256] = win(1, 0, 8) + jnp.where(hi_lane, win(1, 8, 16), 0.0)
        fbuf[T:T + HALO_C, :] = fbuf[0:HALO_C, :]

        def rest_bc(rows):
            d_u = sc["du_s"][rows, :]
            put_dh(4, rows, d_u * hs(5, rows))
            put_dh(5, rows, d_u * hs(4, rows))
            put_dh(7, rows, sc["cw_s"][rows, :] - sc["dpool_s"][rows, :])
        chunks(RC, rest_bc)

        dxt_s = sc["dxt_s"]

        def dx_term(k):
            term = _dot_nt(wi_ref[k], dhb_ref[:, COLS * k:COLS * (k + 1)])
            if k == 1:
                dxt_s[...] = term
            else:
                dxt_s[...] += term

        def conv_a(rows):
            a0c = sc["a0_s"][rows, :]
            acc = None
            for k in range(KA):
                off = (KA - 1) - k
                p, q8 = off % SUBLANES, off - off % SUBLANES
                w = dbuf[pl.ds(rows.start + q8, RC), :] if p == 0 else sh[p - 1, pl.ds(rows.start + q8, RC), :]
                term = caw_ref[k:k + 1, :] * w
                acc = term if acc is None else acc + term
                acc_add(ROW_CAW + k, a0c * w)
            sc["u_s"][rows, :] = acc
        n_chunks = T // RC
        after = {(n_chunks * j) // 3: j + 1 for j in range(3)}
        for c in range(n_chunks):
            conv_a(pl.ds(c * RC, RC))
            if c in after:
                dx_term(after[c])
        dbuf[T:T + HALO_A, :] = dbuf[0:HALO_A, :]

        def rest_a(rows):
            d_a0 = sc["u_s"][rows, :]
            sg = sc["sg_s"][rows, :]
            put_dh(0, rows, d_a0 * sg)
            put_dh(1, rows, d_a0 * hs(0, rows) * sg * (1.0 - sg))
        chunks(RC, rest_a)
        dx_term(0)
        dx_ref[...] += dxt_s[...].T

        @pl.when(i == nt - 1)
        def _():
            for row in list(range(6)) + list(range(ROW_CBW, ROW_CBW + KB)) + list(range(ROW_CAW, ROW_CAW + KA)) + list(
                    range(ROW_BIN, ROW_BIN + N_SLICES)):
                osm_ref[row:row + 1, :] = _colsum(acc_s[8 * row:8 * row + 8, :])
            for j, row in enumerate((ROW_LNG, ROW_LNB, ROW_BOUT)):
                cs = _colsum(acc_w[8 * j:8 * j + 8, :])
                for q in range(D_MODEL // GROUP):
                    osm_ref[row + q:row + q + 1, :] = cs[:, GROUP * q:GROUP * (q + 1)]
            r = lax.broadcasted_iota(jnp.int32, (SGU_BLOCK, GROUP), 0) // CHUNK
            c = (lax.broadcasted_iota(jnp.int32, (SGU_BLOCK, GROUP), 1) % SGU_BLOCK) // CHUNK
            for half in range(2):
                rows_ = slice(ROW_WC + half * SGU_BLOCK, ROW_WC + (half + 1) * SGU_BLOCK)
                osm_ref[rows_, :] = jnp.where(c <= r, osm_ref[rows_, :], 0.0)
            sb_t = _segdot(dsp_acc[...], e4_ref[...]).T
            osm_ref[ROW_SB:ROW_SB + 8, 0:SGU_BLOCK] = sb_t[0:8, :]
            for g in range(4):
                osm_ref[ROW_PW:ROW_PW + HEAD, HEAD * g:HEAD * (g + 1)] = (
                    pw_acc[HEAD * g:HEAD * (g + 1), HEAD * g:HEAD * (g + 1)])

    def rows(width):
        return pl.BlockSpec((T, width), lambda i, l: (nt - 1 - i, 0))

    consts = (wi, caw, cbw, s256, seg, pw, wm, wmt, sb, wo, v1024, e4)
    unstacked = (wi, seg, wo, e4)
    in_specs = [rows(D_MODEL), rows(D_MODEL), rows(IN_WIDTH), rows(3 * GROUP)] + [
        _whole(a) if any(a is u for u in unstacked) else _of_layer(a) for a in consts]
    out_specs = [rows(D_MODEL), rows(IN_WIDTH), rows(D_MODEL), pl.BlockSpec((SM_ROWS, GROUP), lambda i, l: (0, 0))]
    out_shape = [jax.ShapeDtypeStruct((S, D_MODEL), F32), jax.ShapeDtypeStruct((S, IN_WIDTH), BF16),
                 jax.ShapeDtypeStruct((S, D_MODEL), BF16), jax.ShapeDtypeStruct((SM_ROWS, GROUP), F32)]
    scratch_shapes = list(scratch.values())
    extra, aliases = (), {}
    if exch is not None:
        extra = tuple(exch)
        r_i, r_o = exch[3], exch[4]
        in_specs += [ANY] * 5
        out_specs += [ANY] * 3
        out_shape += [jax.ShapeDtypeStruct(r_i.shape, r_i.dtype), jax.ShapeDtypeStruct(r_o.shape, r_o.dtype),
                      jax.ShapeDtypeStruct((N_DEV, SM_ROWS, GROUP), F32)]
        scratch_shapes += [pltpu.SemaphoreType.DMA((N_EXCH_SEMS,)), pltpu.SemaphoreType.DMA((N_EXCH_SEMS,)),
                           pltpu.SemaphoreType.DMA((1,))]
        aliases = {20: 4, 21: 5}
    grid_spec = pltpu.PrefetchScalarGridSpec(num_scalar_prefetch=1, grid=(nt,), in_specs=in_specs,
                                             out_specs=out_specs, scratch_shapes=scratch_shapes)
    return pl.pallas_call(
        body, name="bwd_layer" if exch is None else "bwd_layer_exchange",
        grid_spec=grid_spec, out_shape=out_shape, input_output_aliases=aliases,
        compiler_params=_vmem_params(dimension_semantics=("arbitrary",), **(
            dict(has_side_effects=True, collective_id=COLLECTIVE_ID["bwd_layer_exchange"]) if exch is not None else {})),
    )(larr, dy, z, h, aux, *consts, *extra)


def _dw(layer, xb, dhb, mixb, dzb, gwi, gwi16, gwo, gwo16, *, k_steps, small=None):
    S = xb.shape[0]
    tk = S // k_steps
    n_steps = N_CHIPS + k_steps

    def body(*refs):
        x_ref, dh_ref, mix_ref, dz_ref = refs[1:5]
        oi_ref, oi16_ref, oo_ref, oo16_ref = refs[n_in:n_in + 4]
        j = pl.program_id(0)
        if small is not None:
            _exchange_comm(j == 0, j == n_steps - 1, None, None, None, refs[9], None, None, refs[n_in + 4],
                           *refs[n_in + 5:])

        @pl.when(j < N_CHIPS)
        def _():
            acc = _dot_tn(x_ref[...], dh_ref[...])
            oi_ref[...] = acc
            oi16_ref[...] = acc.astype(BF16)

        @pl.when(j == N_CHIPS)
        def _():
            oo_ref[...] = jnp.zeros_like(oo_ref)

        @pl.when(j >= N_CHIPS)
        def _():
            oo_ref[...] += _dot_tn(mix_ref[...], dz_ref[...]).reshape(N_CHIPS, GROUP, D_MODEL)

        @pl.when(j == n_steps - 1)
        def _():
            oo16_ref[...] = oo_ref[...].astype(BF16)

    def col_block(j, l):
        return jnp.minimum(j, N_CHIPS - 1)

    def tok_block(j, l):
        return jnp.maximum(j - N_CHIPS, 0)

    oi_spec = pl.BlockSpec((None, None, D_MODEL, COLS), lambda j, l: (l[0], col_block(j, l), 0, 0))
    oo_spec = pl.BlockSpec((None, N_CHIPS, GROUP, D_MODEL), lambda j, l: (l[0], 0, 0, 0))
    in_specs = [pl.BlockSpec((S, D_MODEL), lambda j, l: (0, 0)),
                pl.BlockSpec((S, COLS), lambda j, l: (0, col_block(j, l))),
                pl.BlockSpec((tk, D_MODEL), lambda j, l: (tok_block(j, l), 0)),
                pl.BlockSpec((tk, D_MODEL), lambda j, l: (tok_block(j, l), 0)), ANY, ANY, ANY, ANY]
    out_specs = [oi_spec, oi_spec, oo_spec, oo_spec]
    out_shape = [jax.ShapeDtypeStruct(gwi.shape, F32), jax.ShapeDtypeStruct(gwi.shape, BF16),
                 jax.ShapeDtypeStruct(gwo.shape, F32), jax.ShapeDtypeStruct(gwo.shape, BF16)]
    scratch, extra = [], ()
    if small is not None:
        extra = (small,)
        in_specs += [ANY]
        out_specs += [ANY]
        out_shape += [jax.ShapeDtypeStruct((N_DEV, SM_ROWS, GROUP), F32)]
        scratch = [pltpu.SemaphoreType.DMA((N_EXCH_SEMS,)), pltpu.SemaphoreType.DMA((N_EXCH_SEMS,)), pltpu.SemaphoreType.DMA((1,))]
    n_in = 9 + len(extra)
    grid_spec = pltpu.PrefetchScalarGridSpec(
        num_scalar_prefetch=1, grid=(n_steps,), in_specs=in_specs, out_specs=out_specs, scratch_shapes=scratch)
    return pl.pallas_call(
        body, name="dw" if small is None else "dw_exchange", grid_spec=grid_spec, out_shape=out_shape,
        input_output_aliases={5: 0, 6: 1, 7: 2, 8: 3},
        compiler_params=_vmem_params(dimension_semantics=("arbitrary",), **(
            dict(has_side_effects=True, collective_id=COLLECTIVE_ID["dw_exchange"]) if small is not None else {})),
    )(layer, xb, dhb, mixb, dzb, gwi, gwi16, gwo, gwo16, *extra)


def _adamw_math(w, g, m, v):
    nm = ADAM_B1 * m + (1.0 - ADAM_B1) * g
    nv = ADAM_B2 * v + (1.0 - ADAM_B2) * (g * g)
    c1 = 1.0 - ADAM_B1 ** ADAM_STEP
    c2 = 1.0 - ADAM_B2 ** ADAM_STEP
    return -ADAM_LR * ((nm / c1) / (jnp.sqrt(nv / c2) + ADAM_EPS) + ADAM_WD * w), nm, nv


def _adamw_small(ws, gs, ms, vs):
    n = len(ws)

    def body(*refs):
        for j in range(n):
            d, nm, nv = _adamw_math(*(refs[k * n + j][...] for k in range(4)))
            refs[4 * n + j][...] = d
            refs[5 * n + j][...] = nm
            refs[6 * n + j][...] = nv

    shapes = [jax.ShapeDtypeStruct(w.shape, F32) for w in ws]
    outs = pl.pallas_call(body, name="adamw_small", out_shape=shapes * 3, compiler_params=_vmem_params())(
        *ws, *gs, *ms, *vs)
    return outs[0:n], outs[n:2 * n], outs[2 * n:3 * n]


def _adamw(w, g, m, v, *, rows_per_step, name, copy_g=False):
    R, C = w.shape
    tr = rows_per_step

    def body(w_ref, g_ref, m_ref, v_ref, d_ref, nm_ref, nv_ref, *g_out):
        g_ = g_ref[...]
        d_ref[...], nm_ref[...], nv_ref[...] = _adamw_math(w_ref[...], g_, m_ref[...], v_ref[...])
        if copy_g:
            g_out[0][...] = g_

    spec = pl.BlockSpec((tr, C), lambda i: (i, 0))
    n_out = 4 if copy_g else 3
    return pl.pallas_call(
        body, name=name, grid=(R // tr,),
        in_specs=[spec] * 4, out_specs=[spec] * n_out,
        out_shape=[jax.ShapeDtypeStruct((R, C), F32)] * n_out,
        compiler_params=_vmem_params(dimension_semantics=("arbitrary",)),
    )(w, g, m, v)


def _gather_weights(wi16, wo16, cw):
    L = wi16.shape[0]
    hi_rows, ho_rows = D_MODEL // 2, GROUP // 2
    n_ici = 2 * L + 1
    n_fwd = 2 * L

    def body(wi_ref, wo_ref, cw_ref, *rest):
        wig = rest[0:L]
        wog = rest[L:2 * L]
        cwg = rest[2 * L]
        send_sems, recv_sems, loc_sems, vwi, vwo, vcw = rest[2 * L + 1:]
        x, y, c = _place()
        me_k = 2 * x + y
        sibling = (x, y, 1 - c)
        chips = _other_chips(x, y)

        def half_i(ref, blk):
            return ref.at[blk, pl.ds(c * hi_rows, hi_rows), :]

        def half_o(ref, blk):
            return ref.at[blk, pl.ds(c * ho_rows, ho_rows), :]

        def other_half_i(ref, blk):
            return ref.at[blk, pl.ds((1 - c) * hi_rows, hi_rows), :]

        def other_half_o(ref, blk):
            return ref.at[blk, pl.ds((1 - c) * ho_rows, ho_rows), :]

        stage_in = [pltpu.make_async_copy(wi_ref, vwi, loc_sems.at[0]), pltpu.make_async_copy(wo_ref, vwo, loc_sems.at[1]),
                    pltpu.make_async_copy(cw_ref, vcw, loc_sems.at[2])]
        local = []
        for l in range(L):
            local.append(pltpu.make_async_copy(vwi.at[l], wig[l].at[me_k], loc_sems.at[3 + 2 * l]))
            local.append(pltpu.make_async_copy(vwo.at[l], wog[l].at[me_k], loc_sems.at[3 + 2 * l + 1]))
        local.append(pltpu.make_async_copy(vcw, cwg.at[me_k], loc_sems.at[3 + 2 * L]))
        _handshake(PEERS_COLUMN)
        for cp in stage_in:
            cp.start()

        def remote(src, dst, sem, to):
            return pltpu.make_async_remote_copy(src_ref=src, dst_ref=dst, send_sem=send_sems.at[sem],
                                                recv_sem=recv_sems.at[sem], device_id=to, device_id_type=MESH)

        sends = []
        for r, (px, py, _) in enumerate(chips):
            to = (px, py, c)
            for l in range(L):
                sends.append(remote(half_i(wi_ref, l), half_i(wig[l], me_k), r * n_ici + 2 * l, to))
                sends.append(remote(half_o(wo_ref, l), half_o(wog[l], me_k), r * n_ici + 2 * l + 1, to))
            sends.append(remote(cw_ref, cwg.at[me_k], r * n_ici + 2 * L, to))
        for cp in sends:
            cp.start()
        for cp in stage_in:
            cp.wait()
        for cp in local:
            cp.start()

        base = 3 * n_ici
        fwds = []
        for r, (px, py, pk) in enumerate(chips):
            for l in range(L):
                remote(half_i(wig[l], pk), half_i(wig[l], pk), r * n_ici + 2 * l, sibling).wait_recv()
                f = remote(half_i(wig[l], pk), half_i(wig[l], pk), base + r * n_fwd + 2 * l, sibling)
                f.start()
                fwds.append(f)
                remote(half_o(wog[l], pk), half_o(wog[l], pk), r * n_ici + 2 * l + 1, sibling).wait_recv()
                f = remote(half_o(wog[l], pk), half_o(wog[l], pk), base + r * n_fwd + 2 * l + 1, sibling)
                f.start()
                fwds.append(f)
            remote(cwg.at[pk], cwg.at[pk], r * n_ici + 2 * L, sibling).wait_recv()
        for r, (px, py, pk) in enumerate(chips):
            for l in range(L):
                remote(other_half_i(wig[l], pk), other_half_i(wig[l], pk), base + r * n_fwd + 2 * l, sibling).wait_recv()
                remote(other_half_o(wog[l], pk), other_half_o(wog[l], pk), base + r * n_fwd + 2 * l + 1, sibling).wait_recv()
        for cp in sends + fwds:
            cp.wait_send()
        for cp in local:
            cp.wait()

    n_sem = 3 * n_ici + 3 * n_fwd
    out_shape = ([jax.ShapeDtypeStruct((N_CHIPS, D_MODEL, COLS), BF16)] * L
                 + [jax.ShapeDtypeStruct((N_CHIPS, GROUP, D_MODEL), BF16)] * L
                 + [jax.ShapeDtypeStruct((N_CHIPS,) + cw.shape, F32)])
    outs = pl.pallas_call(
        body, name="gather_weights",
        in_specs=[ANY, ANY, ANY], out_specs=[ANY] * (2 * L + 1), out_shape=out_shape,
        scratch_shapes=[pltpu.SemaphoreType.DMA((n_sem,)), pltpu.SemaphoreType.DMA((n_sem,)),
                        pltpu.SemaphoreType.DMA((2 * L + 4,)), pltpu.VMEM(wi16.shape, BF16), pltpu.VMEM(wo16.shape, BF16),
                        pltpu.VMEM(cw.shape, F32)],
        compiler_params=_vmem_params(has_side_effects=True, collective_id=COLLECTIVE_ID["gather_weights"]),
    )(wi16, wo16, cw)
    return outs[0:L], outs[L:2 * L], outs[2 * L]


def _swap_add(cl_arr, g_i, g16_i, p_i, g_o, g16_o, p_o, *, send_on=None):
    hi, ho = p_i.shape[2], p_o.shape[2]
    n_in = 7 + (2 if send_on is not None else 0)
    n_out = 2 + (2 if send_on is not None else 0)

    def body(*refs):
        cl_ref, gi_ref, gi16_ref, _, go_ref, go16_ref = refs[0:6]
        oi_ref, oo_ref = refs[n_in:n_in + 2]
        ri_v, ro_v, send_sems, recv_sems = refs[n_in + n_out:n_in + n_out + 4]
        k = pl.program_id(0)
        x, y, c = _place()
        l = cl_ref[1]

        def copies(kk):
            pair = ((gi16_ref, hi, ri_v), (go16_ref, ho, ro_v))
            return [pltpu.make_async_remote_copy(
                src_ref=src.at[l, kk, pl.ds((1 - c) * n, n), :], dst_ref=dst.at[kk], send_sem=send_sems.at[2 * kk + j],
                recv_sem=recv_sems.at[2 * kk + j], device_id=(x, y, 1 - c), device_id_type=MESH)
                for j, (src, n, dst) in enumerate(pair)]

        @pl.when(k == 0)
        def _():
            _handshake(PEERS_SIBLING if send_on is None else PEERS_COLUMN)
            for kk in range(N_CHIPS):
                for cp in copies(kk):
                    cp.start()

        for cp in copies(k):
            cp.wait_recv()
        pi_k = (gi_ref[...] + ri_v[k].astype(F32)).astype(oi_ref.dtype)
        po_k = (go_ref[...] + ro_v[k].astype(F32)).astype(oo_ref.dtype)
        oi_ref[...] = pi_k
        oo_ref[...] = po_k

        if send_on is not None:
            qi_ref, qo_ref = refs[n_in + 2:n_in + 4]
            pv_i, pv_o, out_sems, in_sems = refs[n_in + n_out + 4:]
            pv_i[k] = pi_k
            pv_o[k] = po_k
            chips = _other_chips(x, y)

            def onward(r):
                px, py, pk = chips[r]
                return [pltpu.make_async_remote_copy(
                    src_ref=pv.at[pk], dst_ref=q.at[r, l], send_sem=out_sems.at[2 * r + j], recv_sem=in_sems.at[2 * r + j],
                    device_id=(px, py, c), device_id_type=MESH) for j, (pv, q) in enumerate(((pv_i, qi_ref), (pv_o, qo_ref)))]

            for r in range(3):
                @pl.when(k == chips[r][2])
                def _():
                    for cp in onward(r):
                        cp.start()

        @pl.when(k == N_CHIPS - 1)
        def _():
            for kk in range(N_CHIPS):
                for cp in copies(kk):
                    cp.wait_send()
            if send_on is not None:
                for r in range(3):
                    for cp in onward(r):
                        cp.wait()

    def specs(p):
        rows, cols = p.shape[2], p.shape[3]
        mine = pl.BlockSpec((None, None, rows, cols), lambda k, cl: (cl[1], k, cl[0], 0))
        out = pl.BlockSpec((None, None, rows, cols), lambda k, cl: (cl[1], k, 0, 0))
        return mine, out

    (gi_s, pi_s), (go_s, po_s) = specs(p_i), specs(p_o)
    in_specs = [gi_s, ANY, ANY, go_s, ANY, ANY]
    out_specs = [pi_s, po_s]
    out_shape = [jax.ShapeDtypeStruct(p_i.shape, p_i.dtype), jax.ShapeDtypeStruct(p_o.shape, p_o.dtype)]
    scratch = [pltpu.VMEM((N_CHIPS, hi, p_i.shape[3]), BF16), pltpu.VMEM((N_CHIPS, ho, p_o.shape[3]), BF16),
               pltpu.SemaphoreType.DMA((2 * N_CHIPS,)), pltpu.SemaphoreType.DMA((2 * N_CHIPS,))]
    extra, aliases = (), {3: 0, 6: 1}
    if send_on is not None:
        extra = tuple(send_on)
        in_specs += [ANY, ANY]
        out_specs += [ANY, ANY]
        out_shape += [jax.ShapeDtypeStruct(q.shape, q.dtype) for q in send_on]
        scratch += [pltpu.VMEM((N_CHIPS, hi, p_i.shape[3]), BF16), pltpu.VMEM((N_CHIPS, ho, p_o.shape[3]), BF16),
                    pltpu.SemaphoreType.DMA((6,)), pltpu.SemaphoreType.DMA((6,))]
        aliases = {3: 0, 6: 1, 7: 2, 8: 3}
    grid_spec = pltpu.PrefetchScalarGridSpec(num_scalar_prefetch=1, grid=(N_CHIPS,), in_specs=in_specs,
                                             out_specs=out_specs, scratch_shapes=scratch)
    return pl.pallas_call(
        body, name="swap_add" if send_on is None else "swap_add_send", grid_spec=grid_spec, out_shape=out_shape,
        input_output_aliases=aliases,
        compiler_params=_vmem_params(dimension_semantics=("arbitrary",), has_side_effects=True,
                                     collective_id=COLLECTIVE_ID["swap_add" if send_on is None else "swap_add_send"]),
    )(cl_arr, g_i, g16_i, p_i, g_o, g16_o, p_o, *extra)


def _sum_small(r_sms):
    L = len(r_sms)

    def body(*refs):
        o_ref = refs[L]
        for l in range(L):
            acc = refs[l][0]
            for d in range(1, N_DEV):
                acc = acc + refs[l][d]
            o_ref[l] = acc

    return pl.pallas_call(
        body, name="sum_small",
        out_shape=jax.ShapeDtypeStruct((L,) + r_sms[0].shape[1:], F32),
        compiler_params=_vmem_params(),
    )(*r_sms)


def _sum_chunks(kc_arr, p_i, q_i, p_o, q_o, *, nb):
    L = p_i.shape[0]

    def body(kc_ref, pi_ref, a0, a1, a2, po_ref, b0, b1, b2, oi_ref, oo_ref):
        del kc_ref
        f = lambda ref: ref[...].astype(F32)
        oi_ref[...] = ((f(pi_ref) + f(a0)) + f(a1)) + f(a2)
        oo_ref[...] = ((f(po_ref) + f(b0)) + f(b1)) + f(b2)

    def specs(p):
        tr, cols = p.shape[2] // nb, p.shape[3]
        chunk = pl.BlockSpec((None, None, tr, cols), lambda l, i, kc: (l, kc[0], i, 0))
        got = [pl.BlockSpec((None, None, tr, cols), lambda l, i, kc, _j=j: (_j, l, i, 0)) for j in range(3)]
        out = pl.BlockSpec((None, tr, cols), lambda l, i, kc: (l, kc[1] * nb + i, 0))
        return [chunk] + got, out

    (in_i, out_i), (in_o, out_o) = specs(p_i), specs(p_o)
    grid_spec = pltpu.PrefetchScalarGridSpec(num_scalar_prefetch=1, grid=(L, nb), in_specs=in_i + in_o,
                                             out_specs=[out_i, out_o])
    return pl.pallas_call(
        body, name="sum_chunks", grid_spec=grid_spec,
        out_shape=[jax.ShapeDtypeStruct((L, 2 * p.shape[2], p.shape[3]), F32) for p in (p_i, p_o)],
        compiler_params=_vmem_params(dimension_semantics=("arbitrary",) * 2),
    )(kc_arr, p_i, q_i, q_i, q_i, p_o, q_o, q_o, q_o)


def _share_result(gi, go):
    hi_rows, ho_rows = gi.shape[1] // 2, go.shape[1] // 2

    def body(gi_ref, go_ref, oi_ref, oo_ref, send_sems, recv_sems):
        del gi_ref, go_ref
        x, y, c = _place()
        sibling = (x, y, 1 - c)
        cps = []
        for j, (ref, n) in enumerate(((oi_ref, hi_rows), (oo_ref, ho_rows))):
            mine = ref.at[:, pl.ds(c * n, n), :]
            cps.append(pltpu.make_async_remote_copy(src_ref=mine, dst_ref=mine, send_sem=send_sems.at[j],
                                                    recv_sem=recv_sems.at[j], device_id=sibling, device_id_type=MESH))
        _sibling_handshake()
        for cp in cps:
            cp.start()
        for j, (ref, n) in enumerate(((oi_ref, hi_rows), (oo_ref, ho_rows))):
            theirs = ref.at[:, pl.ds((1 - c) * n, n), :]
            pltpu.make_async_remote_copy(src_ref=theirs, dst_ref=theirs, send_sem=send_sems.at[j],
                                         recv_sem=recv_sems.at[j], device_id=sibling, device_id_type=MESH).wait_recv()
        for cp in cps:
            cp.wait_send()

    return pl.pallas_call(
        body, name="share_result",
        in_specs=[ANY, ANY], out_specs=[ANY, ANY],
        out_shape=[jax.ShapeDtypeStruct(gi.shape, F32), jax.ShapeDtypeStruct(go.shape, F32)],
        input_output_aliases={0: 0, 1: 1},
        scratch_shapes=[pltpu.SemaphoreType.DMA((2,)), pltpu.SemaphoreType.DMA((2,))],
        compiler_params=pltpu.CompilerParams(has_side_effects=True, collective_id=COLLECTIVE_ID["share_result"]),
    )(gi, go)


WEIGHTS = ("ln_g", "ln_b", "w_in", "b_in", "conv_a_w", "conv_a_b", "norm_a_g", "norm_a_b", "conv_b_w", "pool_w",
           "pool_scale", "sgu_ln_g", "sgu_ln_b", "sgu_w", "sgu_bias", "w_out", "b_out")


def _pad_rows(a, rows):
    return jnp.pad(a, ((0, rows - a.shape[0]), (0, 0)))


def _indicator_consts():
    seg = jnp.where((jnp.arange(GROUP)[:, None] // HEAD) == (jnp.arange(GROUP)[None, :] // HEAD),
                    1.0 / HEAD, 0.0).astype(BF16)
    e4 = ((jnp.arange(GROUP)[:, None] // HEAD) == jnp.arange(128)[None, :]).astype(BF16)
    return seg, e4


def _layer_consts(p, conv_full):
    L = conv_full.shape[0]
    same_head = jnp.eye(4, dtype=F32)[:, None, :, None] > 0

    def rows_to(a, rows):
        return jnp.pad(a, ((0, 0), (0, rows - a.shape[1]), (0, 0)))

    s256 = jnp.stack([p[n] for n in ("conv_a_b", "norm_a_g", "norm_a_b", "pool_scale", "sgu_ln_g", "sgu_ln_b")], axis=1)
    pw = jnp.where(same_head, p["pool_w"][:, :, :, None, :], 0.0).reshape(L, GROUP, GROUP)
    return dict(
        caw=rows_to(conv_full[:, :KA], 32), cbw=rows_to(conv_full[:, KA:], 8), s256=rows_to(s256, 8),
        pw=pw.astype(BF16),
        wm=jnp.transpose(p["sgu_w"], (0, 2, 1, 3)).reshape(L, SGU_BLOCK, 4 * SGU_BLOCK),
        wmt=jnp.transpose(p["sgu_w"], (0, 1, 3, 2)).reshape(L, 4 * SGU_BLOCK, SGU_BLOCK),
        sb=jnp.repeat(jnp.transpose(p["sgu_bias"], (0, 2, 1)), HEAD, axis=2),
        v1024=rows_to(jnp.stack([p["b_out"], p["ln_g"], p["ln_b"]], axis=1), 8),
        bin=p["b_in"][:, None, :])


def _unpack_small(sm):
    L = sm.shape[0]
    owc = jnp.concatenate([sm[:, ROW_WC:ROW_WC + SGU_BLOCK], sm[:, ROW_WC + SGU_BLOCK:ROW_WC + 2 * SGU_BLOCK]], axis=2)
    return dict(
        conv_a_b=sm[:, 0], norm_a_g=sm[:, 1], norm_a_b=sm[:, 2], pool_scale=sm[:, 3], sgu_ln_g=sm[:, 4],
        sgu_ln_b=sm[:, 5], conv_b_w=sm[:, ROW_CBW:ROW_CBW + KB], conv_a_w=sm[:, ROW_CAW:ROW_CAW + KA],
        pool_w=jnp.transpose(sm[:, ROW_PW:ROW_PW + HEAD].reshape(L, HEAD, 4, HEAD), (0, 2, 1, 3)),
        ln_g=sm[:, ROW_LNG:ROW_LNG + 4].reshape(L, D_MODEL), ln_b=sm[:, ROW_LNB:ROW_LNB + 4].reshape(L, D_MODEL),
        b_out=sm[:, ROW_BOUT:ROW_BOUT + 4].reshape(L, D_MODEL),
        b_in=sm[:, ROW_BIN:ROW_BIN + N_SLICES].reshape(L, IN_WIDTH),
        sgu_w=jnp.transpose(owc.reshape(L, SGU_BLOCK, 4, SGU_BLOCK), (0, 2, 1, 3)),
        sgu_bias=sm[:, ROW_SB:ROW_SB + 4, 0:SGU_BLOCK])


def _step(p, m, v, x, target, *, tile_f, tile_b, k_steps):
    L = p["ln_g"].shape[0]
    xi, yi, ci = _place()
    me_k = 2 * xi + yi
    hi_rows, ho_rows = D_MODEL // 2, GROUP // 2

    cw = jnp.concatenate([p["conv_a_w"], p["conv_b_w"]], axis=1).reshape(-1, 128)
    cw_rows = cw.shape[0]
    cw = _pad_rows(cw, -(-cw_rows // SUBLANES) * SUBLANES)
    wi16 = p["w_in"].astype(BF16)
    wo16 = p["w_out"].astype(BF16)
    wig0, wog0, cwg = _gather_weights(wi16[0:1], wo16[0:1], cw)
    cwg = cwg[:, :cw_rows].reshape(N_CHIPS, L, KA + KB, HEAD)
    conv_full = jnp.transpose(cwg, (1, 2, 0, 3)).reshape(L, KA + KB, GROUP)
    seg, e4 = _indicator_consts()
    k = _layer_consts(p, conv_full)
    layer = [jnp.full((1,), l, jnp.int32) for l in range(L)]

    hcur = x
    saved, wig, wog = [], [wig0[0]], [wog0[0]]
    for l in range(L):
        nxt = (wi16, wo16) if l + 1 < L else None
        outs = _fwd_layer(layer[l], hcur, wig[l], k["bin"], k["caw"], k["cbw"], k["s256"], seg, k["pw"], k["wm"], k["sb"],
                          wog[l], k["v1024"], tile=tile_f, nxt=nxt, target=None if nxt is not None else target)
        y, xb, h, aux, mixb, z = outs[0:6]
        if nxt is not None:
            wig.append(outs[6])
            wog.append(outs[7])
        saved.append((xb, h, aux, mixb, z))
        hcur = y

    dy = hcur
    loss_local = outs[6][0, 0]

    gwi = lax.empty((L, N_CHIPS, D_MODEL, COLS), F32)
    gwo = lax.empty((L, N_CHIPS, GROUP, D_MODEL), F32)
    gwi16 = lax.empty((L, N_CHIPS, D_MODEL, COLS), BF16)
    gwo16 = lax.empty((L, N_CHIPS, GROUP, D_MODEL), BF16)
    p_i = lax.empty((L, N_CHIPS, hi_rows, COLS), BF16)
    p_o = lax.empty((L, N_CHIPS, ho_rows, D_MODEL), BF16)
    q_i = lax.empty((3, L, hi_rows, COLS), BF16)
    q_o = lax.empty((3, L, ho_rows, D_MODEL), BF16)
    r_sm = [None] * L
    pending = None
    for l in reversed(range(L)):
        xb, h, aux, mixb, z = saved[l]
        exch = None if pending is None else (p_i, p_o, pending, q_i, q_o)
        outs = _bwd_layer(layer[l], dy, z, h, aux, wig[l], k["caw"], k["cbw"], k["s256"], seg, k["pw"], k["wm"],
                          k["wmt"], k["sb"], wog[l], k["v1024"], e4, tile=tile_b, exch=exch)
        dy, dhb, dzb, osm = outs[0:4]
        if l == L - 1:
            osm = osm.at[ROW_LOSS, 0].set(loss_local)
        if exch is not None:
            q_i, q_o, r_sm[l + 1] = outs[4:7]
        larr = layer[l]
        outs = _dw(larr, xb, dhb, mixb, dzb, gwi, gwi16, gwo, gwo16, k_steps=k_steps, small=osm if l == 0 else None)
        gwi, gwi16, gwo, gwo16 = outs[0:4]
        if l == 0:
            r_sm[0] = outs[4]
        cl_arr = jnp.stack([ci, jnp.int32(l)]).astype(jnp.int32)
        if l > 0:
            p_i, p_o = _swap_add(cl_arr, gwi, gwi16, p_i, gwo, gwo16, p_o)
        else:
            p_i, p_o, q_i, q_o = _swap_add(cl_arr, gwi, gwi16, p_i, gwo, gwo16, p_o, send_on=(q_i, q_o))
        pending = osm
    grad_x = dy

    summed = _sum_small(r_sm)
    loss = summed[L - 1, ROW_LOSS, 0]
    grads = _unpack_small(summed)
    for n in ("conv_a_w", "conv_b_w"):
        grads[n] = lax.dynamic_slice_in_dim(grads[n], me_k * HEAD, HEAD, axis=2)

    kc_arr = jnp.stack([me_k, ci]).astype(jnp.int32)
    g_i, g_o = _sum_chunks(kc_arr, p_i, q_i, p_o, q_o, nb=2)
    g_i, g_o = _share_result(g_i, g_o)
    grads["w_in"] = g_i
    grads["w_out"] = g_o

    delta, new_m, new_v = {}, {}, {}
    for n, tr in (("w_in", 512), ("w_out", 256)):
        shp = p[n].shape
        args = [a.reshape(shp[0] * shp[1], shp[2]) for a in (p[n], grads[n], m[n], v[n])]
        outs = _adamw(*args, rows_per_step=tr, name="adamw_" + n, copy_g=True)
        delta[n], new_m[n], new_v[n], grads[n] = (a.reshape(shp) for a in outs)
    small = [n for n in WEIGHTS if n not in ("w_in", "w_out")]
    flat = [[a[n].reshape(-1, a[n].shape[-1]) for n in small] for a in (p, grads, m, v)]
    outs = _adamw_small(*flat)
    for j, n in enumerate(small):
        delta[n], new_m[n], new_v[n] = (o[j].reshape(p[n].shape) for o in outs)

    return (loss, grad_x[None], *[grads[n] for n in WEIGHTS], *[delta[n] for n in WEIGHTS],
            *[new_m[n] for n in WEIGHTS], *[new_v[n] for n in WEIGHTS])


def kernel(x, ln_g, ln_b, w_in, b_in, conv_a_w, conv_a_b, norm_a_g, norm_a_b, conv_b_w, pool_w, pool_scale, sgu_ln_g, sgu_ln_b, sgu_w, sgu_bias, w_out, b_out, loss_target, m_ln_g, m_ln_b, m_w_in, m_b_in, m_conv_a_w, m_conv_a_b, m_norm_a_g, m_norm_a_b, m_conv_b_w, m_pool_w, m_pool_scale, m_sgu_ln_g, m_sgu_ln_b, m_sgu_w, m_sgu_bias, m_w_out, m_b_out, v_ln_g, v_ln_b, v_w_in, v_b_in, v_conv_a_w, v_conv_a_b, v_norm_a_g, v_norm_a_b, v_conv_b_w, v_pool_w, v_pool_scale, v_sgu_ln_g, v_sgu_ln_b, v_sgu_w, v_sgu_bias, v_w_out, v_b_out):
    p = dict(ln_g=ln_g, ln_b=ln_b, w_in=w_in, b_in=b_in, conv_a_w=conv_a_w, conv_a_b=conv_a_b, norm_a_g=norm_a_g,
             norm_a_b=norm_a_b, conv_b_w=conv_b_w, pool_w=pool_w, pool_scale=pool_scale, sgu_ln_g=sgu_ln_g,
             sgu_ln_b=sgu_ln_b, sgu_w=sgu_w, sgu_bias=sgu_bias, w_out=w_out, b_out=b_out)
    m = dict(ln_g=m_ln_g, ln_b=m_ln_b, w_in=m_w_in, b_in=m_b_in, conv_a_w=m_conv_a_w, conv_a_b=m_conv_a_b,
             norm_a_g=m_norm_a_g, norm_a_b=m_norm_a_b, conv_b_w=m_conv_b_w, pool_w=m_pool_w, pool_scale=m_pool_scale,
             sgu_ln_g=m_sgu_ln_g, sgu_ln_b=m_sgu_ln_b, sgu_w=m_sgu_w, sgu_bias=m_sgu_bias, w_out=m_w_out, b_out=m_b_out)
    v = dict(ln_g=v_ln_g, ln_b=v_ln_b, w_in=v_w_in, b_in=v_b_in, conv_a_w=v_conv_a_w, conv_a_b=v_conv_a_b,
             norm_a_g=v_norm_a_g, norm_a_b=v_norm_a_b, conv_b_w=v_conv_b_w, pool_w=v_pool_w, pool_scale=v_pool_scale,
             sgu_ln_g=v_sgu_ln_g, sgu_ln_b=v_sgu_ln_b, sgu_w=v_sgu_w, sgu_bias=v_sgu_bias, w_out=v_w_out, b_out=v_b_out)
    return _step(p, m, v, x[0], loss_target[0], tile_f=256, tile_b=256, k_steps=4)
```

```python
import jax
import jax.numpy as jnp
from jax import lax
from jax.experimental import pallas as pl
from jax.experimental.pallas import tpu as pltpu

F32 = jnp.float32
BF16 = jnp.bfloat16
MESH = pl.DeviceIdType.MESH

D_MODEL = 1024
GROUP = 256
HEAD = 64
N_SLICES = 12
IN_WIDTH = N_SLICES * GROUP
N_CHIPS = 4
COLS = IN_WIDTH // N_CHIPS
KA = 31
KB = 3
SUBLANES = 8
HALO_A, HALO_B, HALO_C = 32, 8, 16
N_GATHER_SEMS = 12
N_EXCH_SEMS = 13
SGU_BLOCK = 128
CHUNK = 64
LN_EPS = 1e-5
ROWS = 64
V7X_VMEM_BYTES = 64 * 1024 * 1024
VMEM_LIMIT = V7X_VMEM_BYTES - 8 * 1024 * 1024

ADAM_LR, ADAM_B1, ADAM_B2, ADAM_EPS, ADAM_WD, ADAM_STEP = 0.001, 0.9, 0.999, 1e-08, 0.01, 10


ANY = pl.BlockSpec(memory_space=pl.ANY)


def _vmem_params(**kw):
    return pltpu.CompilerParams(vmem_limit_bytes=VMEM_LIMIT, **kw)


def _whole(a):
    return pl.BlockSpec(a.shape, lambda i, l, _n=a.ndim: (0,) * _n)


def _of_layer(a):
    return pl.BlockSpec((None,) + a.shape[1:], lambda i, l, _n=a.ndim: (l[0],) + (0,) * (_n - 1))


def _place():
    return lax.axis_index("x"), lax.axis_index("y"), lax.axis_index("c")


def _other_chips(x, y):
    return [(1 - x, y, 2 * (1 - x) + y), (x, 1 - y, 2 * x + (1 - y)), (1 - x, 1 - y, 2 * (1 - x) + (1 - y))]


PEERS_SIBLING, PEERS_COLUMN, PEERS_ALL = "sibling", "sibling and the same core of the other chips", "all"
COLLECTIVE_ID = dict(sum_share=0, swap_add=1, gather_weights=2, fwd_layer_gather=3, swap_add_send=4,
                     bwd_layer_exchange=5, dw_exchange=6)


def _handshake(peers):
    x, y, c = _place()
    if peers == PEERS_SIBLING:
        ids = [(x, y, 1 - c)]
    elif peers == PEERS_COLUMN:
        ids = [(x, y, 1 - c)] + [(px, py, c) for px, py, _ in _other_chips(x, y)]
    else:
        ids = [(1 - x if r & 4 else x, 1 - y if r & 2 else y, 1 - c if r & 1 else c) for r in range(1, 8)]
    barrier = pltpu.get_barrier_semaphore()
    for to in ids:
        pl.semaphore_signal(barrier, inc=1, device_id=to, device_id_type=MESH)
    pl.semaphore_wait(barrier, len(ids))


def _sig(v):
    return 0.5 * jnp.tanh(0.5 * v) + 0.5


def _dot(a, b):
    return jnp.dot(a, b, preferred_element_type=F32)


def _dot_nt(a, b):
    return lax.dot_general(a, b, (((1,), (1,)), ((), ())), preferred_element_type=F32)


def _dot_tn(a, b):
    return lax.dot_general(a, b, (((0,), (0,)), ((), ())), preferred_element_type=F32)


def _segdot(v, m):
    hi = v.astype(BF16)
    lo = (v - hi.astype(F32)).astype(BF16)
    return _dot(hi, m) + _dot(lo, m)


def _colsum(v):
    return jnp.sum(v, axis=0, keepdims=True)


def _rowmean(v):
    return jnp.mean(v, axis=-1, keepdims=True)


def _lane_group(n):
    return lax.broadcasted_iota(jnp.int32, (1, n), 1) // HEAD


def _pool_cnt(tile, t_rows):
    pos = tile * t_rows + lax.broadcasted_iota(jnp.int32, (t_rows, GROUP), 0) + 1
    grp = lax.broadcasted_iota(jnp.int32, (t_rows, GROUP), 1) // HEAD
    win = jnp.where(grp == 0, 2, jnp.where(grp == 1, 4, jnp.where(grp == 2, 8, 16)))
    return jnp.minimum(pos, win).astype(F32)


def _sgu_masks(wm_ref, wmt_ref, wm_s, wmt_s):
    r = lax.broadcasted_iota(jnp.int32, (SGU_BLOCK, 4 * SGU_BLOCK), 0) // CHUNK
    c = (lax.broadcasted_iota(jnp.int32, (SGU_BLOCK, 4 * SGU_BLOCK), 1) % SGU_BLOCK) // CHUNK
    wm_s[...] = jnp.where(c <= r, wm_ref[...], 0.0).astype(BF16)
    if wmt_ref is not None:
        rt = (lax.broadcasted_iota(jnp.int32, (4 * SGU_BLOCK, SGU_BLOCK), 0) % SGU_BLOCK) // CHUNK
        ct = lax.broadcasted_iota(jnp.int32, (4 * SGU_BLOCK, SGU_BLOCK), 1) // CHUNK
        wmt_s[...] = jnp.where(rt <= ct, wmt_ref[...], 0.0).astype(BF16)


def _vstack(v_blk):
    grp = _lane_group(GROUP)
    return jnp.concatenate([jnp.where(grp == h, v_blk, 0.0) for h in range(4)], axis=0).astype(BF16)


def _gather_next(step, nt, nwi, nwo, gwi, gwo, send_sems, recv_sems, loc_sems, vwi, vwo):
    x, y, c = _place()
    me_k = 2 * x + y
    sibling = (x, y, 1 - c)
    chips = _other_chips(x, y)
    hi, ho = D_MODEL // 2, GROUP // 2
    fwd_sems = N_GATHER_SEMS // 2

    def rc(src, dst, sem, to):
        return pltpu.make_async_remote_copy(src_ref=src, dst_ref=dst, send_sem=send_sems.at[sem],
                                            recv_sem=recv_sems.at[sem], device_id=to, device_id_type=MESH)

    def blk(ref, k, n, cc):
        return ref.at[k, pl.ds(cc * n, n), :]

    def ici(r):
        px, py, _ = chips[r]
        to = (px, py, c)
        return [rc(nwi.at[pl.ds(c * hi, hi), :], blk(gwi, me_k, hi, c), 2 * r, to),
                rc(nwo.at[pl.ds(c * ho, ho), :], blk(gwo, me_k, ho, c), 2 * r + 1, to)]

    def landed(r, cc, base):
        pk = chips[r][2]
        return [rc(blk(gwi, pk, hi, cc), blk(gwi, pk, hi, cc), base + 2 * r, sibling),
                rc(blk(gwo, pk, ho, cc), blk(gwo, pk, ho, cc), base + 2 * r + 1, sibling)]

    def stage_in():
        return [pltpu.make_async_copy(nwi, vwi, loc_sems.at[0]), pltpu.make_async_copy(nwo, vwo, loc_sems.at[1])]

    def local():
        return [pltpu.make_async_copy(vwi, gwi.at[me_k], loc_sems.at[2]),
                pltpu.make_async_copy(vwo, gwo.at[me_k], loc_sems.at[3])]

    @pl.when(step == 0)
    def _():
        _handshake(PEERS_COLUMN)
        for cp in stage_in():
            cp.start()
        for r in range(3):
            for cp in ici(r):
                cp.start()

    @pl.when(step == 1)
    def _():
        for cp in stage_in():
            cp.wait()
        for cp in local():
            cp.start()

    @pl.when(step == (3 * nt) // 4)
    def _():
        for r in range(3):
            for got, fwd in zip(landed(r, c, 0), landed(r, c, fwd_sems)):
                got.wait_recv()
                fwd.start()

    @pl.when(step == nt - 1)
    def _():
        for r in range(3):
            for got in landed(r, 1 - c, fwd_sems):
                got.wait_recv()
        for r in range(3):
            for cp in ici(r) + landed(r, c, fwd_sems):
                cp.wait_send()
        for cp in local():
            cp.wait()


def _fwd_layer(larr, x, wi, bin_, caw, cbw, s256, seg, pw, wm, sb, wo, v1024, *, tile, nxt=None, target=None):
    assert nxt is None or target is None
    S = x.shape[0]
    T = tile
    nt = S // T
    alpha = float((2.0 * 4) ** 0.25)
    n_in = 13 + (2 if nxt is not None else 0) + (1 if target is not None else 0)
    n_out = 6 + (2 if nxt is not None else 0) + (1 if target is not None else 0)

    def body(*refs):
        l_ref = refs[0]
        (x_ref, wi_ref, bin_ref, caw_ref, cbw_ref, s256_ref, seg_ref, pw_ref, wm_ref, sb_ref, wo_ref,
         v1024_ref) = refs[1:13]
        y_ref, xb_ref, h_ref, aux_ref, mix_ref, z_ref = refs[n_in:n_in + 6]
        abuf, bbuf, cbuf, wm_s, shf = refs[n_in + n_out:n_in + n_out + 5]
        i = pl.program_id(0)
        if nxt is not None:
            _gather_next(i, nt, refs[13].at[l_ref[0] + 1], refs[14].at[l_ref[0] + 1], refs[n_in + 6], refs[n_in + 7],
                         *refs[n_in + n_out + 5:])

        @pl.when(i == 0)
        def _():
            abuf[0:HALO_A, :] = jnp.zeros((HALO_A, GROUP), F32)
            bbuf[0:HALO_B, :] = jnp.zeros((HALO_B, GROUP), F32)
            cbuf[0:HALO_C, :] = jnp.zeros((HALO_C, GROUP), F32)
            _sgu_masks(wm_ref, None, wm_s, None)

        x = x_ref[...]
        xb = x.astype(BF16)
        xb_ref[...] = xb
        for k in range(N_CHIPS):
            h_ref[:, COLS * k:COLS * (k + 1)] = _dot(xb, wi_ref[k]) + bin_ref[:, COLS * k:COLS * (k + 1)]

        def hs(j):
            return h_ref[:, GROUP * j:GROUP * (j + 1)]

        abuf[HALO_A:HALO_A + T, :] = hs(0) * _sig(hs(1))
        span = T + HALO_A - SUBLANES
        for p in range(1, SUBLANES):
            shf[p - 1, :, :] = abuf[p:p + span, :]
        for r0 in range(0, T, ROWS):
            acc = None
            for k in range(KA):
                off = HALO_A - (KA - 1) + k
                p, q8 = off % SUBLANES, off - off % SUBLANES
                win = abuf[r0 + q8:r0 + q8 + ROWS, :] if p == 0 else shf[p - 1, r0 + q8:r0 + q8 + ROWS, :]
                term = caw_ref[k:k + 1, :] * win
                acc = term if acc is None else acc + term
            aux_ref[r0:r0 + ROWS, 0:GROUP] = acc + s256_ref[0:1, :]
        abuf[0:HALO_A, :] = abuf[T:T + HALO_A, :]
        a1 = aux_ref[:, 0:GROUP]
        segm = seg_ref[...]
        cen = a1 - _segdot(a1, segm)
        var = _segdot(cen * cen, segm)
        a2 = cen * lax.rsqrt(var + LN_EPS) * s256_ref[1:2, :] + s256_ref[2:3, :]
        az = hs(2)
        mix_ref[:, 0:GROUP] = (a2 * _sig(a2) * (az * _sig(az))).astype(BF16)

        bbuf[HALO_B:HALO_B + T, :] = hs(4) * hs(5)
        for r0 in range(0, T, ROWS):
            acc = None
            for k in range(KB):
                off = HALO_B - (KB - 1) + k + r0
                term = cbw_ref[k:k + 1, :] * bbuf[off:off + ROWS, :]
                acc = term if acc is None else acc + term
            aux_ref[r0:r0 + ROWS, GROUP:2 * GROUP] = acc
        bbuf[0:HALO_B, :] = bbuf[T:T + HALO_B, :]
        bz = hs(6)
        mix_ref[:, GROUP:2 * GROUP] = (hs(3) * aux_ref[:, GROUP:2 * GROUP] * (bz * _sig(bz))).astype(BF16)

        ch = hs(7)
        cbuf[HALO_C:HALO_C + T, :] = ch
        hi_lane = (lax.broadcasted_iota(jnp.int32, (1, 128), 1) // HEAD) == 1
        for r0 in range(0, T, ROWS):
            def win(col, j0, j1):
                s = None
                for j in range(j0, j1):
                    off = HALO_C - j + r0
                    term = cbuf[off:off + ROWS, 128 * col:128 * (col + 1)]
                    s = term if s is None else s + term
                return s
            w0 = win(0, 0, 2) + jnp.where(hi_lane, win(0, 2, 4), 0.0)
            w1 = win(1, 0, 8) + jnp.where(hi_lane, win(1, 8, 16), 0.0)
            aux_ref[r0:r0 + ROWS, 2 * GROUP:2 * GROUP + 128] = w0
            aux_ref[r0:r0 + ROWS, 2 * GROUP + 128:3 * GROUP] = w1
        cbuf[0:HALO_C, :] = cbuf[T:T + HALO_C, :]
        pooled = aux_ref[:, 2 * GROUP:3 * GROUP] / _pool_cnt(i, T) - ch
        aux_ref[:, 2 * GROUP:3 * GROUP] = pooled
        q = _dot(pooled.astype(BF16), pw_ref[...])
        cz = hs(8)
        mix_ref[:, 2 * GROUP:3 * GROUP] = (q * s256_ref[3:4, :] * (cz * _sig(cz))).astype(BF16)

        dv = hs(10)
        cen = dv - _rowmean(dv)
        var = _rowmean(cen * cen)
        v = cen * lax.rsqrt(var + LN_EPS) * s256_ref[4:5, :] + s256_ref[5:6, :]
        sps = []
        for n in range(T // SGU_BLOCK):
            vb = v[n * SGU_BLOCK:(n + 1) * SGU_BLOCK, :]
            sps.append(_dot(wm_s[...], _vstack(vb)) + sb_ref[...])
        sp = jnp.concatenate(sps, axis=0)
        dz = hs(11)
        mix_ref[:, 3 * GROUP:4 * GROUP] = (hs(9) * sp * (dz * _sig(dz))).astype(BF16)

        out = v1024_ref[0:1, :]
        for k in range(N_CHIPS):
            out = out + _dot(mix_ref[:, GROUP * k:GROUP * (k + 1)], wo_ref[k])
        z = alpha * x + out
        z_ref[...] = z
        cen = z - _rowmean(z)
        var = _rowmean(cen * cen)
        y = cen * lax.rsqrt(var + LN_EPS) * v1024_ref[1:2, :] + v1024_ref[2:3, :]
        if target is None:
            y_ref[...] = y
        else:
            t_ref, loss_ref = refs[13], refs[n_in + 6]

            @pl.when(i == 0)
            def _():
                loss_ref[...] = jnp.zeros_like(loss_ref)
            err = y - t_ref[...]
            y_ref[...] = err * (1.0 / D_MODEL)
            loss_ref[...] += jnp.sum(_colsum(err * err), axis=1, keepdims=True) * (0.5 / D_MODEL)

    def rows(width):
        return pl.BlockSpec((T, width), lambda i, l: (i, 0))

    consts = (wi, bin_, caw, cbw, s256, seg, pw, wm, sb, wo, v1024)
    in_specs = [rows(D_MODEL)] + [_whole(a) if a is wi or a is seg or a is wo else _of_layer(a) for a in consts]
    out_specs = [rows(D_MODEL), rows(D_MODEL), rows(IN_WIDTH), rows(3 * GROUP), rows(D_MODEL), rows(D_MODEL)]
    out_shape = [jax.ShapeDtypeStruct((S, D_MODEL), F32), jax.ShapeDtypeStruct((S, D_MODEL), BF16),
                 jax.ShapeDtypeStruct((S, IN_WIDTH), F32), jax.ShapeDtypeStruct((S, 3 * GROUP), F32),
                 jax.ShapeDtypeStruct((S, D_MODEL), BF16), jax.ShapeDtypeStruct((S, D_MODEL), F32)]
    scratch = [pltpu.VMEM((T + HALO_A, GROUP), F32), pltpu.VMEM((T + HALO_B, GROUP), F32),
               pltpu.VMEM((T + HALO_C, GROUP), F32), pltpu.VMEM((SGU_BLOCK, 4 * SGU_BLOCK), BF16),
               pltpu.VMEM((SUBLANES - 1, T + HALO_A - SUBLANES, GROUP), F32)]
    extra = ()
    if nxt is not None:
        extra = tuple(nxt)
        in_specs += [ANY, ANY]
        out_specs += [ANY, ANY]
        out_shape += [jax.ShapeDtypeStruct((N_CHIPS, D_MODEL, COLS), BF16),
                      jax.ShapeDtypeStruct((N_CHIPS, GROUP, D_MODEL), BF16)]
        scratch += [pltpu.SemaphoreType.DMA((N_GATHER_SEMS,)), pltpu.SemaphoreType.DMA((N_GATHER_SEMS,)),
                    pltpu.SemaphoreType.DMA((4,)), pltpu.VMEM((D_MODEL, COLS), BF16), pltpu.VMEM((GROUP, D_MODEL), BF16)]
    if target is not None:
        extra = (target,)
        in_specs += [rows(D_MODEL)]
        out_specs += [pl.BlockSpec((8, 128), lambda i, l: (0, 0))]
        out_shape += [jax.ShapeDtypeStruct((8, 128), F32)]
    grid_spec = pltpu.PrefetchScalarGridSpec(num_scalar_prefetch=1, grid=(nt,), in_specs=in_specs,
                                             out_specs=out_specs, scratch_shapes=scratch)
    return pl.pallas_call(
        body, name=("fwd_layer_loss" if target is not None else "fwd_layer") if nxt is None else "fwd_layer_gather",
        grid_spec=grid_spec, out_shape=out_shape,
        compiler_params=_vmem_params(dimension_semantics=("arbitrary",), **(
            dict(has_side_effects=True, collective_id=COLLECTIVE_ID["fwd_layer_gather"]) if nxt is not None else {})),
    )(larr, x, *consts, *extra)


ROW_CBW = 8
ROW_CAW = 16
ROW_LOSS = 7
ROW_PW = 48
ROW_LNG = 112
ROW_LNB = 116
ROW_BOUT = 120
ROW_BIN = 124
ROW_WC = 136
ROW_SB = 392
SM_ROWS = 400
N_DEV = 8


def _exchange_comm(start, finish, l, p_i, p_o, sm, r_i, r_o, r_sm, send_sems, recv_sems, loc_sem):
    x, y, c = _place()
    me = 4 * x + 2 * y + c
    chips = _other_chips(x, y)

    def rc(src, dst, sem, to):
        return pltpu.make_async_remote_copy(src_ref=src, dst_ref=dst, send_sem=send_sems.at[sem],
                                            recv_sem=recv_sems.at[sem], device_id=to, device_id_type=MESH)

    def big(r):
        px, py, pk = chips[r]
        to = (px, py, c)
        return [rc(p_i.at[l, pk], r_i.at[r, l], 2 * r, to), rc(p_o.at[l, pk], r_o.at[r, l], 2 * r + 1, to)]

    def peer(rel):
        px = 1 - x if rel & 4 else x
        py = 1 - y if rel & 2 else y
        pc = 1 - c if rel & 1 else c
        return (px, py, pc), 4 * px + 2 * py + pc

    def small_out(rel):
        to, _ = peer(rel)
        return rc(sm, r_sm.at[me], N_EXCH_SEMS - N_DEV + rel, to)

    def small_in(rel):
        to, idx = peer(rel)
        return rc(sm, r_sm.at[idx], N_EXCH_SEMS - N_DEV + rel, to)

    def local():
        return pltpu.make_async_copy(sm, r_sm.at[me], loc_sem.at[0])

    with_big, with_small = p_i is not None, sm is not None

    @pl.when(start)
    def _():
        _handshake(PEERS_ALL)
        if with_small:
            local().start()
        if with_big:
            for r in range(3):
                for cp in big(r):
                    cp.start()
        if with_small:
            for rel in range(1, N_DEV):
                small_out(rel).start()

    @pl.when(finish)
    def _():
        if with_big:
            for r in range(3):
                for cp in big(r):
                    cp.wait()
        if with_small:
            for rel in range(1, N_DEV):
                small_in(rel).wait_recv()
                small_out(rel).wait_send()
            local().wait()


RC = 32
RC_WIDE = 16
ACC_ROWS = 136


def _rsum8(v):
    r = v[0:8]
    for j in range(1, v.shape[0] // 8):
        r = r + v[8 * j:8 * j + 8]
    return r


def _bwd_layer(larr, dy, z, h, aux, wi, caw, cbw, s256, seg, pw, wm, wmt, sb, wo, v1024, e4, *, tile, exch=None):
    S = dy.shape[0]
    T = tile
    nt = S // T
    nblk = T // SGU_BLOCK
    alpha = float((2.0 * 4) ** 0.25)
    n_in = 17 + (5 if exch is not None else 0)
    n_out = 4 + (3 if exch is not None else 0)
    slab = pltpu.VMEM((T, GROUP), F32)
    scratch = dict(
        dbuf=pltpu.VMEM((T + HALO_A, GROUP), F32), ebuf=pltpu.VMEM((T + HALO_B, GROUP), F32),
        fbuf=pltpu.VMEM((T + HALO_C, GROUP), F32), sh=pltpu.VMEM((SUBLANES - 1, T + HALO_A - SUBLANES, GROUP), F32),
        wm_s=pltpu.VMEM((SGU_BLOCK, 4 * SGU_BLOCK), BF16), wmt_s=pltpu.VMEM((4 * SGU_BLOCK, SGU_BLOCK), BF16),
        dsp_acc=pltpu.VMEM((SGU_BLOCK, GROUP), F32), pw_acc=pltpu.VMEM((GROUP, GROUP), F32),
        acc_s=pltpu.VMEM((8 * ACC_ROWS, GROUP), F32), acc_w=pltpu.VMEM((24, D_MODEL), F32),
        dmix_s=pltpu.VMEM((T, D_MODEL), F32), vst_s=pltpu.VMEM((nblk, 4 * SGU_BLOCK, GROUP), BF16),
        dq_s=pltpu.VMEM((T, GROUP), BF16), dxt_s=pltpu.VMEM((D_MODEL, T), F32),
        mean_s=slab, t1_s=slab, t2_s=slab, q_s=slab, xv_s=slab, rv_s=slab, v_s=slab, sp_s=slab, a0_s=slab, sg_s=slab,
        xh_s=slab, ra_s=slab, ub_s=slab, dsp_s=slab, m1_s=slab, m2_s=slab, dpool_s=slab, dvd_s=slab, u_s=slab,
        du_s=slab, cw_s=slab)
    names = list(scratch)

    def body(*refs):
        (dy_ref, z_ref, h_ref, aux_ref, wi_ref, caw_ref, cbw_ref, s256_ref, seg_ref, pw_ref, wm_ref, wmt_ref,
         sb_ref, wo_ref, v1024_ref, e4_ref) = refs[1:17]
        dx_ref, dhb_ref, dzb_ref, osm_ref = refs[n_in:n_in + 4]
        k0 = n_in + n_out
        sc = dict(zip(names, refs[k0:k0 + len(names)]))
        dbuf, ebuf, fbuf, sh = sc["dbuf"], sc["ebuf"], sc["fbuf"], sc["sh"]
        wm_s, wmt_s, dsp_acc, pw_acc, acc_s, acc_w = (sc[n] for n in ("wm_s", "wmt_s", "dsp_acc", "pw_acc", "acc_s",
                                                                        "acc_w"))
        dmix_s, vst_s, dq_s = sc["dmix_s"], sc["vst_s"], sc["dq_s"]
        i = pl.program_id(0)
        tile_idx = nt - 1 - i
        if exch is not None:
            p_i, p_o, sm = refs[17:20]
            r_i, r_o, r_sm = refs[n_in + 4:n_in + 7]
            _exchange_comm(i == 0, i == nt - 1, refs[0][0] + 1, p_i, p_o, sm, r_i, r_o, r_sm, *refs[k0 + len(names):])

        @pl.when(i == 0)
        def _():
            dbuf[T:T + HALO_A, :] = jnp.zeros((HALO_A, GROUP), F32)
            ebuf[T:T + HALO_B, :] = jnp.zeros((HALO_B, GROUP), F32)
            fbuf[T:T + HALO_C, :] = jnp.zeros((HALO_C, GROUP), F32)
            _sgu_masks(wm_ref, wmt_ref, wm_s, wmt_s)
            osm_ref[...] = jnp.zeros_like(osm_ref)
            dsp_acc[...] = jnp.zeros_like(dsp_acc)
            pw_acc[...] = jnp.zeros_like(pw_acc)
            acc_s[...] = jnp.zeros_like(acc_s)
            acc_w[...] = jnp.zeros_like(acc_w)

        def chunks(rc, fn):
            for c in range(T // rc):
                fn(pl.ds(c * rc, rc))

        def hs(j, rows):
            return h_ref[rows, GROUP * j:GROUP * (j + 1)]

        def acc_add(row, val):
            acc_s[8 * row:8 * row + 8, :] += _rsum8(val)

        def put_dh(j, rows, val):
            acc_add(ROW_BIN + j, val)
            dhb_ref[rows, GROUP * j:GROUP * (j + 1)] = val.astype(BF16)

        def dsilu(v, s):
            return s * (1.0 + v * (1.0 - s))

        def vec(r):
            return s256_ref[r:r + 1, :]

        def ln_bwd(rows):
            dyc = dy_ref[rows, :]
            zc = z_ref[rows, :]
            cen = zc - _rowmean(zc)
            rstd = lax.rsqrt(_rowmean(cen * cen) + LN_EPS)
            xhat = cen * rstd
            acc_w[0:8, :] += _rsum8(dyc * xhat)
            acc_w[8:16, :] += _rsum8(dyc)
            gdy = dyc * v1024_ref[1:2, :]
            dz = rstd * (gdy - _rowmean(gdy) - xhat * _rowmean(gdy * xhat))
            acc_w[16:24, :] += _rsum8(dz)
            dzb_ref[rows, :] = dz.astype(BF16)
            dx_ref[rows, :] = alpha * dz
        chunks(RC_WIDE, ln_bwd)

        segm = seg_ref[...]
        dzb = dzb_ref[...]
        for k in range(N_CHIPS):
            dmix_s[:, GROUP * k:GROUP * (k + 1)] = _dot_nt(dzb, wo_ref[k])
        sc["mean_s"][...] = _segdot(aux_ref[:, 0:GROUP], segm)
        pooled_b = aux_ref[:, 2 * GROUP:3 * GROUP].astype(BF16)
        sc["q_s"][...] = _dot(pooled_b, pw_ref[...])

        def centre(rows):
            cen = aux_ref[rows, 0:GROUP] - sc["mean_s"][rows, :]
            sc["t1_s"][rows, :] = cen * cen
            dv_in = hs(10, rows)
            cen_v = dv_in - _rowmean(dv_in)
            rstd_v = lax.rsqrt(_rowmean(cen_v * cen_v) + LN_EPS)
            xv = cen_v * rstd_v
            sc["xv_s"][rows, :] = xv
            sc["rv_s"][rows, :] = jnp.broadcast_to(rstd_v, xv.shape)
            sc["v_s"][rows, :] = xv * vec(4) + vec(5)
        chunks(RC, centre)

        sc["t2_s"][...] = _segdot(sc["t1_s"][...], segm)
        for n in range(nblk):
            blk = slice(n * SGU_BLOCK, (n + 1) * SGU_BLOCK)
            vst_s[n] = _vstack(sc["v_s"][blk, :])
            sc["sp_s"][blk, :] = _dot(wm_s[...], vst_s[n]) + sb_ref[...]

        def mixers(rows):
            a_val, a_glu, a_z = hs(0, rows), hs(1, rows), hs(2, rows)
            sg = _sig(a_glu)
            sc["a0_s"][rows, :] = a_val * sg
            sc["sg_s"][rows, :] = sg
            rstd_a = lax.rsqrt(sc["t2_s"][rows, :] + LN_EPS)
            xh = (aux_ref[rows, 0:GROUP] - sc["mean_s"][rows, :]) * rstd_a
            a2 = xh * vec(1) + vec(2)
            s2 = _sig(a2)
            sz = _sig(a_z)
            dya = dmix_s[rows, 0:GROUP]
            put_dh(2, rows, dya * (a2 * s2) * dsilu(a_z, sz))
            d_a2 = dya * (a_z * sz) * dsilu(a2, s2)
            acc_add(1, d_a2 * xh)
            acc_add(2, d_a2)
            gd = d_a2 * vec(1)
            sc["t1_s"][rows, :] = gd
            sc["t2_s"][rows, :] = gd * xh
            sc["xh_s"][rows, :] = xh
            sc["ra_s"][rows, :] = rstd_a
            b_b, b_c, b_h, b_z = hs(3, rows), hs(4, rows), hs(5, rows), hs(6, rows)
            cb = aux_ref[rows, GROUP:2 * GROUP]
            sz = _sig(b_z)
            dyb = dmix_s[rows, GROUP:2 * GROUP]
            put_dh(3, rows, dyb * cb * (b_z * sz))
            put_dh(6, rows, dyb * b_b * cb * dsilu(b_z, sz))
            ebuf[rows, :] = dyb * b_b * (b_z * sz)
            sc["ub_s"][rows, :] = b_c * b_h
            c_z = hs(8, rows)
            q = sc["q_s"][rows, :]
            sz = _sig(c_z)
            dyc = dmix_s[rows, 2 * GROUP:3 * GROUP]
            acc_add(3, dyc * q * (c_z * sz))
            put_dh(8, rows, dyc * q * vec(3) * dsilu(c_z, sz))
            dq_s[rows, :] = (dyc * vec(3) * (c_z * sz)).astype(BF16)
            d_u, d_z = hs(9, rows), hs(11, rows)
            sp = sc["sp_s"][rows, :]
            sz = _sig(d_z)
            dyd = dmix_s[rows, 3 * GROUP:4 * GROUP]
            put_dh(9, rows, dyd * sp * (d_z * sz))
            put_dh(11, rows, dyd * d_u * sp * dsilu(d_z, sz))
            sc["dsp_s"][rows, :] = dyd * d_u * (d_z * sz)
        chunks(RC, mixers)

        sc["m1_s"][...] = _segdot(sc["t1_s"][...], segm)
        sc["m2_s"][...] = _segdot(sc["t2_s"][...], segm)
        d_q = dq_s[...]
        pw_acc[...] += _dot_tn(pooled_b, d_q)
        sc["dpool_s"][...] = _dot_nt(d_q, pw_ref[...])
        grp = _lane_group(GROUP)
        for n in range(nblk):
            blk = slice(n * SGU_BLOCK, (n + 1) * SGU_BLOCK)
            dspb = sc["dsp_s"][blk, :]
            dsp_acc[...] += dspb
            dspb16 = dspb.astype(BF16)
            dvst = _dot(wmt_s[...], dspb16)
            dvb = None
            for hh in range(4):
                part = jnp.where(grp == hh, dvst[hh * SGU_BLOCK:(hh + 1) * SGU_BLOCK, :], 0.0)
                dvb = part if dvb is None else dvb + part
            sc["dvd_s"][blk, :] = dvb
            dwc = _dot_nt(dspb16, vst_s[n])
            osm_ref[ROW_WC:ROW_WC + SGU_BLOCK, :] += dwc[:, 0:GROUP]
            osm_ref[ROW_WC + SGU_BLOCK:ROW_WC + 2 * SGU_BLOCK, :] += dwc[:, GROUP:2 * GROUP]

        def ln_sums(rows):
            xh = sc["xh_s"][rows, :]
            d_a1 = sc["ra_s"][rows, :] * (sc["t1_s"][rows, :] - sc["m1_s"][rows, :] - xh * sc["m2_s"][rows, :])
            acc_add(0, d_a1)
            dbuf[rows, :] = d_a1
            pos = tile_idx * T + rows.start + lax.broadcasted_iota(jnp.int32, (RC, GROUP), 0) + 1
            lane = lax.broadcasted_iota(jnp.int32, (RC, GROUP), 1) // HEAD
            win = jnp.where(lane == 0, 2, jnp.where(lane == 1, 4, jnp.where(lane == 2, 8, 16)))
            fbuf[rows, :] = sc["dpool_s"][rows, :] / jnp.minimum(pos, win).astype(F32)
            d_v = sc["dvd_s"][rows, :]
            xv = sc["xv_s"][rows, :]
            acc_add(4, d_v * xv)
            acc_add(5, d_v)
            gd = d_v * vec(4)
            put_dh(10, rows, sc["rv_s"][rows, :] * (gd - _rowmean(gd) - xv * _rowmean(gd * xv)))
        chunks(RC, ln_sums)

        span = T + HALO_A - SUBLANES
        for p in range(1, SUBLANES):
            sh[p - 1, :, :] = dbuf[p:p + span, :]

        for r0 in range(0, T, ROWS):
            uc = sc["ub_s"][r0:r0 + ROWS, :]
            acc = None
            for k in range(KB):
                off = (KB - 1) - k + r0
                w = ebuf[off:off + ROWS, :]
                term = cbw_ref[k:k + 1, :] * w
                acc = term if acc is None else acc + term
                acc_add(ROW_CBW + k, uc * w)
            sc["du_s"][r0:r0 + ROWS, :] = acc
        ebuf[T:T + HALO_B, :] = ebuf[0:HALO_B, :]

        hi_lane = (lax.broadcasted_iota(jnp.int32, (1, 128), 1) // HEAD) == 1
        for r0 in range(0, T, ROWS):
            def win(col, j0, j1):
                s = None
                for j in range(j0, j1):
                    term = fbuf[r0 + j:r0 + j + ROWS, 128 * col:128 * (col + 1)]
                    s = term if s is None else s + term
                return s
            sc["cw_s"][r0:r0 + ROWS, 0:128] = win(0, 0, 2) + jnp.where(hi_lane, win(0, 2, 4), 0.0)
            sc["cw_s"][r0:r0 + ROWS, 128:256] = win(1, 0, 8) + jnp.where(hi_lane, win(1, 8, 16), 0.0)
        fbuf[T:T + HALO_C, :] = fbuf[0:HALO_C, :]

        def rest_bc(rows):
            d_u = sc["du_s"][rows, :]
            put_dh(4, rows, d_u * hs(5, rows))
            put_dh(5, rows, d_u * hs(4, rows))
            put_dh(7, rows, sc["cw_s"][rows, :] - sc["dpool_s"][rows, :])
        chunks(RC, rest_bc)

        dxt_s = sc["dxt_s"]

        def dx_term(k):
            term = _dot_nt(wi_ref[k], dhb_ref[:, COLS * k:COLS * (k + 1)])
            if k == 1:
                dxt_s[...] = term
            else:
                dxt_s[...] += term

        def conv_a(rows):
            a0c = sc["a0_s"][rows, :]
            acc = None
            for k in range(KA):
                off = (KA - 1) - k
                p, q8 = off % SUBLANES, off - off % SUBLANES
                w = dbuf[pl.ds(rows.start + q8, RC), :] if p == 0 else sh[p - 1, pl.ds(rows.start + q8, RC), :]
                term = caw_ref[k:k + 1, :] * w
                acc = term if acc is None else acc + term
                acc_add(ROW_CAW + k, a0c * w)
            sc["u_s"][rows, :] = acc
        n_chunks = T // RC
        after = {(n_chunks * j) // 3: j + 1 for j in range(3)}
        for c in range(n_chunks):
            conv_a(pl.ds(c * RC, RC))
            if c in after:
                dx_term(after[c])
        dbuf[T:T + HALO_A, :] = dbuf[0:HALO_A, :]

        def rest_a(rows):
            d_a0 = sc["u_s"][rows, :]
            sg = sc["sg_s"][rows, :]
            put_dh(0, rows, d_a0 * sg)
            put_dh(1, rows, d_a0 * hs(0, rows) * sg * (1.0 - sg))
        chunks(RC, rest_a)
        dx_term(0)
        dx_ref[...] += dxt_s[...].T

        @pl.when(i == nt - 1)
        def _():
            for row in list(range(6)) + list(range(ROW_CBW, ROW_CBW + KB)) + list(range(ROW_CAW, ROW_CAW + KA)) + list(
                    range(ROW_BIN, ROW_BIN + N_SLICES)):
                osm_ref[row:row + 1, :] = _colsum(acc_s[8 * row:8 * row + 8, :])
            for j, row in enumerate((ROW_LNG, ROW_LNB, ROW_BOUT)):
                cs = _colsum(acc_w[8 * j:8 * j + 8, :])
                for q in range(D_MODEL // GROUP):
                    osm_ref[row + q:row + q + 1, :] = cs[:, GROUP * q:GROUP * (q + 1)]
            r = lax.broadcasted_iota(jnp.int32, (SGU_BLOCK, GROUP), 0) // CHUNK
            c = (lax.broadcasted_iota(jnp.int32, (SGU_BLOCK, GROUP), 1) % SGU_BLOCK) // CHUNK
            for half in range(2):
                rows_ = slice(ROW_WC + half * SGU_BLOCK, ROW_WC + (half + 1) * SGU_BLOCK)
                osm_ref[rows_, :] = jnp.where(c <= r, osm_ref[rows_, :], 0.0)
            sb_t = _segdot(dsp_acc[...], e4_ref[...]).T
            osm_ref[ROW_SB:ROW_SB + 8, 0:SGU_BLOCK] = sb_t[0:8, :]
            for g in range(4):
                osm_ref[ROW_PW:ROW_PW + HEAD, HEAD * g:HEAD * (g + 1)] = (
                    pw_acc[HEAD * g:HEAD * (g + 1), HEAD * g:HEAD * (g + 1)])

    def rows(width):
        return pl.BlockSpec((T, width), lambda i, l: (nt - 1 - i, 0))

    consts = (wi, caw, cbw, s256, seg, pw, wm, wmt, sb, wo, v1024, e4)
    unstacked = (wi, seg, wo, e4)
    in_specs = [rows(D_MODEL), rows(D_MODEL), rows(IN_WIDTH), rows(3 * GROUP)] + [
        _whole(a) if any(a is u for u in unstacked) else _of_layer(a) for a in consts]
    out_specs = [rows(D_MODEL), rows(IN_WIDTH), rows(D_MODEL), pl.BlockSpec((SM_ROWS, GROUP), lambda i, l: (0, 0))]
    out_shape = [jax.ShapeDtypeStruct((S, D_MODEL), F32), jax.ShapeDtypeStruct((S, IN_WIDTH), BF16),
                 jax.ShapeDtypeStruct((S, D_MODEL), BF16), jax.ShapeDtypeStruct((SM_ROWS, GROUP), F32)]
    scratch_shapes = list(scratch.values())
    extra, aliases = (), {}
    if exch is not None:
        extra = tuple(exch)
        r_i, r_o = exch[3], exch[4]
        in_specs += [ANY] * 5
        out_specs += [ANY] * 3
        out_shape += [jax.ShapeDtypeStruct(r_i.shape, r_i.dtype), jax.ShapeDtypeStruct(r_o.shape, r_o.dtype),
                      jax.ShapeDtypeStruct((N_DEV, SM_ROWS, GROUP), F32)]
        scratch_shapes += [pltpu.SemaphoreType.DMA((N_EXCH_SEMS,)), pltpu.SemaphoreType.DMA((N_EXCH_SEMS,)),
                           pltpu.SemaphoreType.DMA((1,))]
        aliases = {20: 4, 21: 5}
    grid_spec = pltpu.PrefetchScalarGridSpec(num_scalar_prefetch=1, grid=(nt,), in_specs=in_specs,
                                             out_specs=out_specs, scratch_shapes=scratch_shapes)
    return pl.pallas_call(
        body, name="bwd_layer" if exch is None else "bwd_layer_exchange",
        grid_spec=grid_spec, out_shape=out_shape, input_output_aliases=aliases,
        compiler_params=_vmem_params(dimension_semantics=("arbitrary",), **(
            dict(has_side_effects=True, collective_id=COLLECTIVE_ID["bwd_layer_exchange"]) if exch is not None else {})),
    )(larr, dy, z, h, aux, *consts, *extra)


def _dw(layer, xb, dhb, mixb, dzb, gwi, gwi16, gwo, gwo16, *, k_steps, small=None):
    S = xb.shape[0]
    tk = S // k_steps
    n_steps = N_CHIPS + k_steps

    def body(*refs):
        x_ref, dh_ref, mix_ref, dz_ref = refs[1:5]
        oi_ref, oi16_ref, oo_ref, oo16_ref = refs[n_in:n_in + 4]
        j = pl.program_id(0)
        if small is not None:
            _exchange_comm(j == 0, j == n_steps - 1, None, None, None, refs[9], None, None, refs[n_in + 4],
                           *refs[n_in + 5:])

        @pl.when(j < N_CHIPS)
        def _():
            acc = _dot_tn(x_ref[...], dh_ref[...])
            oi_ref[...] = acc
            oi16_ref[...] = acc.astype(BF16)

        @pl.when(j == N_CHIPS)
        def _():
            oo_ref[...] = jnp.zeros_like(oo_ref)

        @pl.when(j >= N_CHIPS)
        def _():
            oo_ref[...] += _dot_tn(mix_ref[...], dz_ref[...]).reshape(N_CHIPS, GROUP, D_MODEL)

        @pl.when(j == n_steps - 1)
        def _():
            oo16_ref[...] = oo_ref[...].astype(BF16)

    def col_block(j, l):
        return jnp.minimum(j, N_CHIPS - 1)

    def tok_block(j, l):
        return jnp.maximum(j - N_CHIPS, 0)

    oi_spec = pl.BlockSpec((None, None, D_MODEL, COLS), lambda j, l: (l[0], col_block(j, l), 0, 0))
    oo_spec = pl.BlockSpec((None, N_CHIPS, GROUP, D_MODEL), lambda j, l: (l[0], 0, 0, 0))
    in_specs = [pl.BlockSpec((S, D_MODEL), lambda j, l: (0, 0)),
                pl.BlockSpec((S, COLS), lambda j, l: (0, col_block(j, l))),
                pl.BlockSpec((tk, D_MODEL), lambda j, l: (tok_block(j, l), 0)),
                pl.BlockSpec((tk, D_MODEL), lambda j, l: (tok_block(j, l), 0)), ANY, ANY, ANY, ANY]
    out_specs = [oi_spec, oi_spec, oo_spec, oo_spec]
    out_shape = [jax.ShapeDtypeStruct(gwi.shape, F32), jax.ShapeDtypeStruct(gwi.shape, BF16),
                 jax.ShapeDtypeStruct(gwo.shape, F32), jax.ShapeDtypeStruct(gwo.shape, BF16)]
    scratch, extra = [], ()
    if small is not None:
        extra = (small,)
        in_specs += [ANY]
        out_specs += [ANY]
        out_shape += [jax.ShapeDtypeStruct((N_DEV, SM_ROWS, GROUP), F32)]
        scratch = [pltpu.SemaphoreType.DMA((N_EXCH_SEMS,)), pltpu.SemaphoreType.DMA((N_EXCH_SEMS,)), pltpu.SemaphoreType.DMA((1,))]
    n_in = 9 + len(extra)
    grid_spec = pltpu.PrefetchScalarGridSpec(
        num_scalar_prefetch=1, grid=(n_steps,), in_specs=in_specs, out_specs=out_specs, scratch_shapes=scratch)
    return pl.pallas_call(
        body, name="dw" if small is None else "dw_exchange", grid_spec=grid_spec, out_shape=out_shape,
        input_output_aliases={5: 0, 6: 1, 7: 2, 8: 3},
        compiler_params=_vmem_params(dimension_semantics=("arbitrary",), **(
            dict(has_side_effects=True, collective_id=COLLECTIVE_ID["dw_exchange"]) if small is not None else {})),
    )(layer, xb, dhb, mixb, dzb, gwi, gwi16, gwo, gwo16, *extra)


def _adamw_math(w, g, m, v):
    nm = ADAM_B1 * m + (1.0 - ADAM_B1) * g
    nv = ADAM_B2 * v + (1.0 - ADAM_B2) * (g * g)
    c1 = 1.0 - ADAM_B1 ** ADAM_STEP
    c2 = 1.0 - ADAM_B2 ** ADAM_STEP
    return -ADAM_LR * ((nm / c1) / (jnp.sqrt(nv / c2) + ADAM_EPS) + ADAM_WD * w), nm, nv


def _adamw_small(ws, gs, ms, vs):
    n = len(ws)

    def body(*refs):
        for j in range(n):
            d, nm, nv = _adamw_math(*(refs[k * n + j][...] for k in range(4)))
            refs[4 * n + j][...] = d
            refs[5 * n + j][...] = nm
            refs[6 * n + j][...] = nv

    shapes = [jax.ShapeDtypeStruct(w.shape, F32) for w in ws]
    outs = pl.pallas_call(body, name="adamw_small", out_shape=shapes * 3, compiler_params=_vmem_params())(
        *ws, *gs, *ms, *vs)
    return outs[0:n], outs[n:2 * n], outs[2 * n:3 * n]


def _adamw(w, g, m, v, *, rows_per_step, name, copy_g=False):
    R, C = w.shape
    tr = rows_per_step

    def body(w_ref, g_ref, m_ref, v_ref, d_ref, nm_ref, nv_ref, *g_out):
        g_ = g_ref[...]
        d_ref[...], nm_ref[...], nv_ref[...] = _adamw_math(w_ref[...], g_, m_ref[...], v_ref[...])
        if copy_g:
            g_out[0][...] = g_

    spec = pl.BlockSpec((tr, C), lambda i: (i, 0))
    n_out = 4 if copy_g else 3
    return pl.pallas_call(
        body, name=name, grid=(R // tr,),
        in_specs=[spec] * 4, out_specs=[spec] * n_out,
        out_shape=[jax.ShapeDtypeStruct((R, C), F32)] * n_out,
        compiler_params=_vmem_params(dimension_semantics=("arbitrary",)),
    )(w, g, m, v)


def _gather_weights(wi16, wo16, cw):
    L = wi16.shape[0]
    hi_rows, ho_rows = D_MODEL // 2, GROUP // 2
    n_ici = 2 * L + 1
    n_fwd = 2 * L

    def body(wi_ref, wo_ref, cw_ref, *rest):
        wig = rest[0:L]
        wog = rest[L:2 * L]
        cwg = rest[2 * L]
        send_sems, recv_sems, loc_sems, vwi, vwo, vcw = rest[2 * L + 1:]
        x, y, c = _place()
        me_k = 2 * x + y
        sibling = (x, y, 1 - c)
        chips = _other_chips(x, y)

        def half_i(ref, blk):
            return ref.at[blk, pl.ds(c * hi_rows, hi_rows), :]

        def half_o(ref, blk):
            return ref.at[blk, pl.ds(c * ho_rows, ho_rows), :]

        def other_half_i(ref, blk):
            return ref.at[blk, pl.ds((1 - c) * hi_rows, hi_rows), :]

        def other_half_o(ref, blk):
            return ref.at[blk, pl.ds((1 - c) * ho_rows, ho_rows), :]

        stage_in = [pltpu.make_async_copy(wi_ref, vwi, loc_sems.at[0]), pltpu.make_async_copy(wo_ref, vwo, loc_sems.at[1]),
                    pltpu.make_async_copy(cw_ref, vcw, loc_sems.at[2])]
        local = []
        for l in range(L):
            local.append(pltpu.make_async_copy(vwi.at[l], wig[l].at[me_k], loc_sems.at[3 + 2 * l]))
            local.append(pltpu.make_async_copy(vwo.at[l], wog[l].at[me_k], loc_sems.at[3 + 2 * l + 1]))
        local.append(pltpu.make_async_copy(vcw, cwg.at[me_k], loc_sems.at[3 + 2 * L]))
        _handshake(PEERS_COLUMN)
        for cp in stage_in:
            cp.start()

        def remote(src, dst, sem, to):
            return pltpu.make_async_remote_copy(src_ref=src, dst_ref=dst, send_sem=send_sems.at[sem],
                                                recv_sem=recv_sems.at[sem], device_id=to, device_id_type=MESH)

        sends = []
        for r, (px, py, _) in enumerate(chips):
            to = (px, py, c)
            for l in range(L):
                sends.append(remote(half_i(wi_ref, l), half_i(wig[l], me_k), r * n_ici + 2 * l, to))
                sends.append(remote(half_o(wo_ref, l), half_o(wog[l], me_k), r * n_ici + 2 * l + 1, to))
            sends.append(remote(cw_ref, cwg.at[me_k], r * n_ici + 2 * L, to))
        for cp in sends:
            cp.start()
        for cp in stage_in:
            cp.wait()
        for cp in local:
            cp.start()

        base = 3 * n_ici
        fwds = []
        for r, (px, py, pk) in enumerate(chips):
            for l in range(L):
                remote(half_i(wig[l], pk), half_i(wig[l], pk), r * n_ici + 2 * l, sibling).wait_recv()
                f = remote(half_i(wig[l], pk), half_i(wig[l], pk), base + r * n_fwd + 2 * l, sibling)
                f.start()
                fwds.append(f)
                remote(half_o(wog[l], pk), half_o(wog[l], pk), r * n_ici + 2 * l + 1, sibling).wait_recv()
                f = remote(half_o(wog[l], pk), half_o(wog[l], pk), base + r * n_fwd + 2 * l + 1, sibling)
                f.start()
                fwds.append(f)
            remote(cwg.at[pk], cwg.at[pk], r * n_ici + 2 * L, sibling).wait_recv()
        for r, (px, py, pk) in enumerate(chips):
            for l in range(L):
                remote(other_half_i(wig[l], pk), other_half_i(wig[l], pk), base + r * n_fwd + 2 * l, sibling).wait_recv()
                remote(other_half_o(wog[l], pk), other_half_o(wog[l], pk), base + r * n_fwd + 2 * l + 1, sibling).wait_recv()
        for cp in sends + fwds:
            cp.wait_send()
        for cp in local:
            cp.wait()

    n_sem = 3 * n_ici + 3 * n_fwd
    out_shape = ([jax.ShapeDtypeStruct((N_CHIPS, D_MODEL, COLS), BF16)] * L
                 + [jax.ShapeDtypeStruct((N_CHIPS, GROUP, D_MODEL), BF16)] * L
                 + [jax.ShapeDtypeStruct((N_CHIPS,) + cw.shape, F32)])
    outs = pl.pallas_call(
        body, name="gather_weights",
        in_specs=[ANY, ANY, ANY], out_specs=[ANY] * (2 * L + 1), out_shape=out_shape,
        scratch_shapes=[pltpu.SemaphoreType.DMA((n_sem,)), pltpu.SemaphoreType.DMA((n_sem,)),
                        pltpu.SemaphoreType.DMA((2 * L + 4,)), pltpu.VMEM(wi16.shape, BF16), pltpu.VMEM(wo16.shape, BF16),
                        pltpu.VMEM(cw.shape, F32)],
        compiler_params=_vmem_params(has_side_effects=True, collective_id=COLLECTIVE_ID["gather_weights"]),
    )(wi16, wo16, cw)
    return outs[0:L], outs[L:2 * L], outs[2 * L]


def _swap_add(cl_arr, g_i, g16_i, p_i, g_o, g16_o, p_o, *, send_on=None):
    hi, ho = p_i.shape[2], p_o.shape[2]
    n_in = 7 + (2 if send_on is not None else 0)
    n_out = 2 + (2 if send_on is not None else 0)

    def body(*refs):
        cl_ref, gi_ref, gi16_ref, _, go_ref, go16_ref = refs[0:6]
        oi_ref, oo_ref = refs[n_in:n_in + 2]
        ri_v, ro_v, send_sems, recv_sems = refs[n_in + n_out:n_in + n_out + 4]
        k = pl.program_id(0)
        x, y, c = _place()
        l = cl_ref[1]

        def copies(kk):
            pair = ((gi16_ref, hi, ri_v), (go16_ref, ho, ro_v))
            return [pltpu.make_async_remote_copy(
                src_ref=src.at[l, kk, pl.ds((1 - c) * n, n), :], dst_ref=dst.at[kk], send_sem=send_sems.at[2 * kk + j],
                recv_sem=recv_sems.at[2 * kk + j], device_id=(x, y, 1 - c), device_id_type=MESH)
                for j, (src, n, dst) in enumerate(pair)]

        @pl.when(k == 0)
        def _():
            _handshake(PEERS_SIBLING if send_on is None else PEERS_COLUMN)
            for kk in range(N_CHIPS):
                for cp in copies(kk):
                    cp.start()

        for cp in copies(k):
            cp.wait_recv()
        pi_k = (gi_ref[...] + ri_v[k].astype(F32)).astype(oi_ref.dtype)
        po_k = (go_ref[...] + ro_v[k].astype(F32)).astype(oo_ref.dtype)
        oi_ref[...] = pi_k
        oo_ref[...] = po_k

        if send_on is not None:
            qi_ref, qo_ref = refs[n_in + 2:n_in + 4]
            pv_i, pv_o, out_sems, in_sems = refs[n_in + n_out + 4:]
            pv_i[k] = pi_k
            pv_o[k] = po_k
            chips = _other_chips(x, y)

            def onward(r):
                px, py, pk = chips[r]
                return [pltpu.make_async_remote_copy(
                    src_ref=pv.at[pk], dst_ref=q.at[r, l], send_sem=out_sems.at[2 * r + j], recv_sem=in_sems.at[2 * r + j],
                    device_id=(px, py, c), device_id_type=MESH) for j, (pv, q) in enumerate(((pv_i, qi_ref), (pv_o, qo_ref)))]

            for r in range(3):
                @pl.when(k == chips[r][2])
                def _():
                    for cp in onward(r):
                        cp.start()

        @pl.when(k == N_CHIPS - 1)
        def _():
            for kk in range(N_CHIPS):
                for cp in copies(kk):
                    cp.wait_send()
            if send_on is not None:
                for r in range(3):
                    for cp in onward(r):
                        cp.wait()

    def specs(p):
        rows, cols = p.shape[2], p.shape[3]
        mine = pl.BlockSpec((None, None, rows, cols), lambda k, cl: (cl[1], k, cl[0], 0))
        out = pl.BlockSpec((None, None, rows, cols), lambda k, cl: (cl[1], k, 0, 0))
        return mine, out

    (gi_s, pi_s), (go_s, po_s) = specs(p_i), specs(p_o)
    in_specs = [gi_s, ANY, ANY, go_s, ANY, ANY]
    out_specs = [pi_s, po_s]
    out_shape = [jax.ShapeDtypeStruct(p_i.shape, p_i.dtype), jax.ShapeDtypeStruct(p_o.shape, p_o.dtype)]
    scratch = [pltpu.VMEM((N_CHIPS, hi, p_i.shape[3]), BF16), pltpu.VMEM((N_CHIPS, ho, p_o.shape[3]), BF16),
               pltpu.SemaphoreType.DMA((2 * N_CHIPS,)), pltpu.SemaphoreType.DMA((2 * N_CHIPS,))]
    extra, aliases = (), {3: 0, 6: 1}
    if send_on is not None:
        extra = tuple(send_on)
        in_specs += [ANY, ANY]
        out_specs += [ANY, ANY]
        out_shape += [jax.ShapeDtypeStruct(q.shape, q.dtype) for q in send_on]
        scratch += [pltpu.VMEM((N_CHIPS, hi, p_i.shape[3]), BF16), pltpu.VMEM((N_CHIPS, ho, p_o.shape[3]), BF16),
                    pltpu.SemaphoreType.DMA((6,)), pltpu.SemaphoreType.DMA((6,))]
        aliases = {3: 0, 6: 1, 7: 2, 8: 3}
    grid_spec = pltpu.PrefetchScalarGridSpec(num_scalar_prefetch=1, grid=(N_CHIPS,), in_specs=in_specs,
                                             out_specs=out_specs, scratch_shapes=scratch)
    return pl.pallas_call(
        body, name="swap_add" if send_on is None else "swap_add_send", grid_spec=grid_spec, out_shape=out_shape,
        input_output_aliases=aliases,
        compiler_params=_vmem_params(dimension_semantics=("arbitrary",), has_side_effects=True,
                                     collective_id=COLLECTIVE_ID["swap_add" if send_on is None else "swap_add_send"]),
    )(cl_arr, g_i, g16_i, p_i, g_o, g16_o, p_o, *extra)


def _sum_small(r_sms):
    L = len(r_sms)

    def body(*refs):
        o_ref = refs[L]
        for l in range(L):
            acc = refs[l][0]
            for d in range(1, N_DEV):
                acc = acc + refs[l][d]
            o_ref[l] = acc

    return pl.pallas_call(
        body, name="sum_small",
        out_shape=jax.ShapeDtypeStruct((L,) + r_sms[0].shape[1:], F32),
        compiler_params=_vmem_params(),
    )(*r_sms)


def _sum_share(kc_arr, p_i, q_i, p_o, q_o, *, nb):
    L = p_i.shape[0]
    n_steps, slots = L * nb, 2

    def body(kc_ref, pi_ref, a0, a1, a2, po_ref, b0, b1, b2, oi_ref, oo_ref, vi, vo, loc_sems, send_sems, recv_sems):
        del kc_ref
        x, y, c = _place()
        t = pl.program_id(0) * nb + pl.program_id(1)

        def copies(s):
            l, i = s // nb, s % nb
            out = []
            for j, (v, o) in enumerate(((vi, oi_ref), (vo, oo_ref))):
                tr = v.shape[1]
                src, dst = v.at[s % slots], o.at[l, pl.ds((c * nb + i) * tr, tr), :]
                out.append((pltpu.make_async_copy(src, dst, loc_sems.at[2 * s + j]),
                            pltpu.make_async_remote_copy(src_ref=src, dst_ref=dst, send_sem=send_sems.at[2 * s + j],
                                                         recv_sem=recv_sems.at[2 * s + j], device_id=(x, y, 1 - c),
                                                         device_id_type=MESH)))
            return out

        def sent(s):
            for mine, theirs in copies(s):
                mine.wait()
                theirs.wait_send()

        @pl.when(t == 0)
        def _():
            _handshake(PEERS_SIBLING)

        @pl.when(t >= slots)
        def _():
            sent(t - slots)

        f = lambda ref: ref[...].astype(F32)
        vi[t % slots] = ((f(pi_ref) + f(a0)) + f(a1)) + f(a2)
        vo[t % slots] = ((f(po_ref) + f(b0)) + f(b1)) + f(b2)
        for mine, theirs in copies(t):
            mine.start()
            theirs.start()

        @pl.when(t == n_steps - 1)
        def _():
            for s in range(n_steps - slots, n_steps):
                sent(s)
            for s in range(n_steps):
                for _, theirs in copies(s):
                    theirs.wait_recv()

    def specs(p):
        tr, cols = p.shape[2] // nb, p.shape[3]
        chunk = pl.BlockSpec((None, None, tr, cols), lambda l, i, kc: (l, kc[0], i, 0))
        got = [pl.BlockSpec((None, None, tr, cols), lambda l, i, kc, _j=j: (_j, l, i, 0)) for j in range(3)]
        return [chunk] + got, pltpu.VMEM((slots, tr, cols), F32)

    (in_i, v_i), (in_o, v_o) = specs(p_i), specs(p_o)
    grid_spec = pltpu.PrefetchScalarGridSpec(
        num_scalar_prefetch=1, grid=(L, nb), in_specs=in_i + in_o, out_specs=[ANY, ANY],
        scratch_shapes=[v_i, v_o] + [pltpu.SemaphoreType.DMA((2 * n_steps,))] * 3)
    return pl.pallas_call(
        body, name="sum_share", grid_spec=grid_spec,
        out_shape=[jax.ShapeDtypeStruct((L, 2 * p.shape[2], p.shape[3]), F32) for p in (p_i, p_o)],
        compiler_params=_vmem_params(dimension_semantics=("arbitrary",) * 2, has_side_effects=True,
                                     collective_id=COLLECTIVE_ID["sum_share"]),
    )(kc_arr, p_i, q_i, q_i, q_i, p_o, q_o, q_o, q_o)


WEIGHTS = ("ln_g", "ln_b", "w_in", "b_in", "conv_a_w", "conv_a_b", "norm_a_g", "norm_a_b", "conv_b_w", "pool_w",
           "pool_scale", "sgu_ln_g", "sgu_ln_b", "sgu_w", "sgu_bias", "w_out", "b_out")


def _pad_rows(a, rows):
    return jnp.pad(a, ((0, rows - a.shape[0]), (0, 0)))


def _indicator_consts():
    seg = jnp.where((jnp.arange(GROUP)[:, None] // HEAD) == (jnp.arange(GROUP)[None, :] // HEAD),
                    1.0 / HEAD, 0.0).astype(BF16)
    e4 = ((jnp.arange(GROUP)[:, None] // HEAD) == jnp.arange(128)[None, :]).astype(BF16)
    return seg, e4


def _layer_consts(p, conv_full):
    L = conv_full.shape[0]
    same_head = jnp.eye(4, dtype=F32)[:, None, :, None] > 0

    def rows_to(a, rows):
        return jnp.pad(a, ((0, 0), (0, rows - a.shape[1]), (0, 0)))

    s256 = jnp.stack([p[n] for n in ("conv_a_b", "norm_a_g", "norm_a_b", "pool_scale", "sgu_ln_g", "sgu_ln_b")], axis=1)
    pw = jnp.where(same_head, p["pool_w"][:, :, :, None, :], 0.0).reshape(L, GROUP, GROUP)
    return dict(
        caw=rows_to(conv_full[:, :KA], 32), cbw=rows_to(conv_full[:, KA:], 8), s256=rows_to(s256, 8),
        pw=pw.astype(BF16),
        wm=jnp.transpose(p["sgu_w"], (0, 2, 1, 3)).reshape(L, SGU_BLOCK, 4 * SGU_BLOCK),
        wmt=jnp.transpose(p["sgu_w"], (0, 1, 3, 2)).reshape(L, 4 * SGU_BLOCK, SGU_BLOCK),
        sb=jnp.repeat(jnp.transpose(p["sgu_bias"], (0, 2, 1)), HEAD, axis=2),
        v1024=rows_to(jnp.stack([p["b_out"], p["ln_g"], p["ln_b"]], axis=1), 8),
        bin=p["b_in"][:, None, :])


def _unpack_small(sm):
    L = sm.shape[0]
    owc = jnp.concatenate([sm[:, ROW_WC:ROW_WC + SGU_BLOCK], sm[:, ROW_WC + SGU_BLOCK:ROW_WC + 2 * SGU_BLOCK]], axis=2)
    return dict(
        conv_a_b=sm[:, 0], norm_a_g=sm[:, 1], norm_a_b=sm[:, 2], pool_scale=sm[:, 3], sgu_ln_g=sm[:, 4],
        sgu_ln_b=sm[:, 5], conv_b_w=sm[:, ROW_CBW:ROW_CBW + KB], conv_a_w=sm[:, ROW_CAW:ROW_CAW + KA],
        pool_w=jnp.transpose(sm[:, ROW_PW:ROW_PW + HEAD].reshape(L, HEAD, 4, HEAD), (0, 2, 1, 3)),
        ln_g=sm[:, ROW_LNG:ROW_LNG + 4].reshape(L, D_MODEL), ln_b=sm[:, ROW_LNB:ROW_LNB + 4].reshape(L, D_MODEL),
        b_out=sm[:, ROW_BOUT:ROW_BOUT + 4].reshape(L, D_MODEL),
        b_in=sm[:, ROW_BIN:ROW_BIN + N_SLICES].reshape(L, IN_WIDTH),
        sgu_w=jnp.transpose(owc.reshape(L, SGU_BLOCK, 4, SGU_BLOCK), (0, 2, 1, 3)),
        sgu_bias=sm[:, ROW_SB:ROW_SB + 4, 0:SGU_BLOCK])


def _step(p, m, v, x, target, *, tile_f, tile_b, k_steps):
    L = p["ln_g"].shape[0]
    xi, yi, ci = _place()
    me_k = 2 * xi + yi
    hi_rows, ho_rows = D_MODEL // 2, GROUP // 2

    cw = jnp.concatenate([p["conv_a_w"], p["conv_b_w"]], axis=1).reshape(-1, 128)
    cw_rows = cw.shape[0]
    cw = _pad_rows(cw, -(-cw_rows // SUBLANES) * SUBLANES)
    wi16 = p["w_in"].astype(BF16)
    wo16 = p["w_out"].astype(BF16)
    wig0, wog0, cwg = _gather_weights(wi16[0:1], wo16[0:1], cw)
    cwg = cwg[:, :cw_rows].reshape(N_CHIPS, L, KA + KB, HEAD)
    conv_full = jnp.transpose(cwg, (1, 2, 0, 3)).reshape(L, KA + KB, GROUP)
    seg, e4 = _indicator_consts()
    k = _layer_consts(p, conv_full)
    layer = [jnp.full((1,), l, jnp.int32) for l in range(L)]

    hcur = x
    saved, wig, wog = [], [wig0[0]], [wog0[0]]
    for l in range(L):
        nxt = (wi16, wo16) if l + 1 < L else None
        outs = _fwd_layer(layer[l], hcur, wig[l], k["bin"], k["caw"], k["cbw"], k["s256"], seg, k["pw"], k["wm"], k["sb"],
                          wog[l], k["v1024"], tile=tile_f, nxt=nxt, target=None if nxt is not None else target)
        y, xb, h, aux, mixb, z = outs[0:6]
        if nxt is not None:
            wig.append(outs[6])
            wog.append(outs[7])
        saved.append((xb, h, aux, mixb, z))
        hcur = y

    dy = hcur
    loss_local = outs[6][0, 0]

    gwi = lax.empty((L, N_CHIPS, D_MODEL, COLS), F32)
    gwo = lax.empty((L, N_CHIPS, GROUP, D_MODEL), F32)
    gwi16 = lax.empty((L, N_CHIPS, D_MODEL, COLS), BF16)
    gwo16 = lax.empty((L, N_CHIPS, GROUP, D_MODEL), BF16)
    p_i = lax.empty((L, N_CHIPS, hi_rows, COLS), BF16)
    p_o = lax.empty((L, N_CHIPS, ho_rows, D_MODEL), BF16)
    q_i = lax.empty((3, L, hi_rows, COLS), BF16)
    q_o = lax.empty((3, L, ho_rows, D_MODEL), BF16)
    r_sm = [None] * L
    pending = None
    for l in reversed(range(L)):
        xb, h, aux, mixb, z = saved[l]
        exch = None if pending is None else (p_i, p_o, pending, q_i, q_o)
        outs = _bwd_layer(layer[l], dy, z, h, aux, wig[l], k["caw"], k["cbw"], k["s256"], seg, k["pw"], k["wm"],
                          k["wmt"], k["sb"], wog[l], k["v1024"], e4, tile=tile_b, exch=exch)
        dy, dhb, dzb, osm = outs[0:4]
        if l == L - 1:
            osm = osm.at[ROW_LOSS, 0].set(loss_local)
        if exch is not None:
            q_i, q_o, r_sm[l + 1] = outs[4:7]
        larr = layer[l]
        outs = _dw(larr, xb, dhb, mixb, dzb, gwi, gwi16, gwo, gwo16, k_steps=k_steps, small=osm if l == 0 else None)
        gwi, gwi16, gwo, gwo16 = outs[0:4]
        if l == 0:
            r_sm[0] = outs[4]
        cl_arr = jnp.stack([ci, jnp.int32(l)]).astype(jnp.int32)
        if l > 0:
            p_i, p_o = _swap_add(cl_arr, gwi, gwi16, p_i, gwo, gwo16, p_o)
        else:
            p_i, p_o, q_i, q_o = _swap_add(cl_arr, gwi, gwi16, p_i, gwo, gwo16, p_o, send_on=(q_i, q_o))
        pending = osm
    grad_x = dy

    summed = _sum_small(r_sm)
    loss = summed[L - 1, ROW_LOSS, 0]
    grads = _unpack_small(summed)
    for n in ("conv_a_w", "conv_b_w"):
        grads[n] = lax.dynamic_slice_in_dim(grads[n], me_k * HEAD, HEAD, axis=2)

    kc_arr = jnp.stack([me_k, ci]).astype(jnp.int32)
    g_i, g_o = _sum_share(kc_arr, p_i, q_i, p_o, q_o, nb=2)
    grads["w_in"] = g_i
    grads["w_out"] = g_o

    delta, new_m, new_v = {}, {}, {}
    for n, tr in (("w_in", 512), ("w_out", 256)):
        shp = p[n].shape
        args = [a.reshape(shp[0] * shp[1], shp[2]) for a in (p[n], grads[n], m[n], v[n])]
        outs = _adamw(*args, rows_per_step=tr, name="adamw_" + n, copy_g=True)
        delta[n], new_m[n], new_v[n], grads[n] = (a.reshape(shp) for a in outs)
    small = [n for n in WEIGHTS if n not in ("w_in", "w_out")]
    flat = [[a[n].reshape(-1, a[n].shape[-1]) for n in small] for a in (p, grads, m, v)]
    outs = _adamw_small(*flat)
    for j, n in enumerate(small):
        delta[n], new_m[n], new_v[n] = (o[j].reshape(p[n].shape) for o in outs)

    return (loss, grad_x[None], *[grads[n] for n in WEIGHTS], *[delta[n] for n in WEIGHTS],
            *[new_m[n] for n in WEIGHTS], *[new_v[n] for n in WEIGHTS])


def kernel(x, ln_g, ln_b, w_in, b_in, conv_a_w, conv_a_b, norm_a_g, norm_a_b, conv_b_w, pool_w, pool_scale, sgu_ln_g, sgu_ln_b, sgu_w, sgu_bias, w_out, b_out, loss_target, m_ln_g, m_ln_b, m_w_in, m_b_in, m_conv_a_w, m_conv_a_b, m_norm_a_g, m_norm_a_b, m_conv_b_w, m_pool_w, m_pool_scale, m_sgu_ln_g, m_sgu_ln_b, m_sgu_w, m_sgu_bias, m_w_out, m_b_out, v_ln_g, v_ln_b, v_w_in, v_b_in, v_conv_a_w, v_conv_a_b, v_norm_a_g, v_norm_a_b, v_conv_b_w, v_pool_w, v_pool_scale, v_sgu_ln_g, v_sgu_ln_b, v_sgu_w, v_sgu_bias, v_w_out, v_b_out):
    p = dict(ln_g=ln_g, ln_b=ln_b, w_in=w_in, b_in=b_in, conv_a_w=conv_a_w, conv_a_b=conv_a_b, norm_a_g=norm_a_g,
             norm_a_b=norm_a_b, conv_b_w=conv_b_w, pool_w=pool_w, pool_scale=pool_scale, sgu_ln_g=sgu_ln_g,
             sgu_ln_b=sgu_ln_b, sgu_w=sgu_w, sgu_bias=sgu_bias, w_out=w_out, b_out=b_out)
    m = dict(ln_g=m_ln_g, ln_b=m_ln_b, w_in=m_w_in, b_in=m_b_in, conv_a_w=m_conv_a_w, conv_a_b=m_conv_a_b,
             norm_a_g=m_norm_a_g, norm_a_b=m_norm_a_b, conv_b_w=m_conv_b_w, pool_w=m_pool_w, pool_scale=m_pool_scale,
             sgu_ln_g=m_sgu_ln_g, sgu_ln_b=m_sgu_ln_b, sgu_w=m_sgu_w, sgu_bias=m_sgu_bias, w_out=m_w_out, b_out=m_b_out)
    v = dict(ln_g=v_ln_g, ln_b=v_ln_b, w_in=v_w_in, b_in=v_b_in, conv_a_w=v_conv_a_w, conv_a_b=v_conv_a_b,
             norm_a_g=v_norm_a_g, norm_a_b=v_norm_a_b, conv_b_w=v_conv_b_w, pool_w=v_pool_w, pool_scale=v_pool_scale,
             sgu_ln_g=v_sgu_ln_g, sgu_ln_b=v_sgu_ln_b, sgu_w=v_sgu_w, sgu_bias=v_sgu_bias, w_out=v_w_out, b_out=v_b_out)
    return _step(p, m, v, x[0], loss_target[0], tile_f=256, tile_b=256, k_steps=4)
```

```python
import jax
import jax.numpy as jnp
from jax import lax
from jax.experimental import pallas as pl
from jax.experimental.pallas import tpu as pltpu

F32 = jnp.float32
BF16 = jnp.bfloat16
MESH = pl.DeviceIdType.MESH

D_MODEL = 1024
GROUP = 256
HEAD = 64
N_SLICES = 12
IN_WIDTH = N_SLICES * GROUP
N_CHIPS = 4
COLS = IN_WIDTH // N_CHIPS
KA = 31
KB = 3
SUBLANES = 8
HALO_A, HALO_B, HALO_C = 32, 8, 16
N_GATHER_SEMS = 12
N_EXCH_SEMS = 13
SGU_BLOCK = 128
CHUNK = 64
LN_EPS = 1e-5
ROWS = 64
V7X_VMEM_BYTES = 64 * 1024 * 1024
VMEM_LIMIT = V7X_VMEM_BYTES - 8 * 1024 * 1024

ADAM_LR, ADAM_B1, ADAM_B2, ADAM_EPS, ADAM_WD, ADAM_STEP = 0.001, 0.9, 0.999, 1e-08, 0.01, 10


ANY = pl.BlockSpec(memory_space=pl.ANY)


def _vmem_params(**kw):
    return pltpu.CompilerParams(vmem_limit_bytes=VMEM_LIMIT, **kw)


def _whole(a):
    return pl.BlockSpec(a.shape, lambda i, l, _n=a.ndim: (0,) * _n)


def _of_layer(a):
    return pl.BlockSpec((None,) + a.shape[1:], lambda i, l, _n=a.ndim: (l[0],) + (0,) * (_n - 1))


def _place():
    return lax.axis_index("x"), lax.axis_index("y"), lax.axis_index("c")


def _other_chips(x, y):
    return [(1 - x, y, 2 * (1 - x) + y), (x, 1 - y, 2 * x + (1 - y)), (1 - x, 1 - y, 2 * (1 - x) + (1 - y))]


PEERS_SIBLING, PEERS_COLUMN, PEERS_ALL = "sibling", "sibling and the same core of the other chips", "all"
COLLECTIVE_ID = dict(sum_share=0, swap_add=1, gather_weights=2, fwd_layer_gather=3, swap_add_send=4,
                     bwd_layer_exchange=5, dw_exchange=6)


def _peer_ids(peers):
    x, y, c = _place()
    if peers == PEERS_SIBLING:
        return [(x, y, 1 - c)]
    if peers == PEERS_COLUMN:
        return [(x, y, 1 - c)] + [(px, py, c) for px, py, _ in _other_chips(x, y)]
    return [(1 - x if r & 4 else x, 1 - y if r & 2 else y, 1 - c if r & 1 else c) for r in range(1, 8)]


def _handshake_signal(peers):
    for to in _peer_ids(peers):
        pl.semaphore_signal(pltpu.get_barrier_semaphore(), inc=1, device_id=to, device_id_type=MESH)


def _handshake_wait(peers):
    pl.semaphore_wait(pltpu.get_barrier_semaphore(), len(_peer_ids(peers)))


def _handshake(peers):
    _handshake_signal(peers)
    _handshake_wait(peers)


def _sig(v):
    return 0.5 * jnp.tanh(0.5 * v) + 0.5


def _dot(a, b):
    return jnp.dot(a, b, preferred_element_type=F32)


def _dot_nt(a, b):
    return lax.dot_general(a, b, (((1,), (1,)), ((), ())), preferred_element_type=F32)


def _dot_tn(a, b):
    return lax.dot_general(a, b, (((0,), (0,)), ((), ())), preferred_element_type=F32)


def _segdot(v, m):
    hi = v.astype(BF16)
    lo = (v - hi.astype(F32)).astype(BF16)
    return _dot(hi, m) + _dot(lo, m)


def _colsum(v):
    return jnp.sum(v, axis=0, keepdims=True)


def _rowmean(v):
    return jnp.mean(v, axis=-1, keepdims=True)


def _lane_group(n):
    return lax.broadcasted_iota(jnp.int32, (1, n), 1) // HEAD


def _pool_cnt(tile, t_rows):
    pos = tile * t_rows + lax.broadcasted_iota(jnp.int32, (t_rows, GROUP), 0) + 1
    grp = lax.broadcasted_iota(jnp.int32, (t_rows, GROUP), 1) // HEAD
    win = jnp.where(grp == 0, 2, jnp.where(grp == 1, 4, jnp.where(grp == 2, 8, 16)))
    return jnp.minimum(pos, win).astype(F32)


def _sgu_masks(wm_ref, wmt_ref, wm_s, wmt_s):
    r = lax.broadcasted_iota(jnp.int32, (SGU_BLOCK, 4 * SGU_BLOCK), 0) // CHUNK
    c = (lax.broadcasted_iota(jnp.int32, (SGU_BLOCK, 4 * SGU_BLOCK), 1) % SGU_BLOCK) // CHUNK
    wm_s[...] = jnp.where(c <= r, wm_ref[...], 0.0).astype(BF16)
    if wmt_ref is not None:
        rt = (lax.broadcasted_iota(jnp.int32, (4 * SGU_BLOCK, SGU_BLOCK), 0) % SGU_BLOCK) // CHUNK
        ct = lax.broadcasted_iota(jnp.int32, (4 * SGU_BLOCK, SGU_BLOCK), 1) // CHUNK
        wmt_s[...] = jnp.where(rt <= ct, wmt_ref[...], 0.0).astype(BF16)


def _vstack(v_blk):
    grp = _lane_group(GROUP)
    return jnp.concatenate([jnp.where(grp == h, v_blk, 0.0) for h in range(4)], axis=0).astype(BF16)


def _gather_next(step, nt, nwi, nwo, gwi, gwo, send_sems, recv_sems, loc_sems, vwi, vwo):
    x, y, c = _place()
    me_k = 2 * x + y
    sibling = (x, y, 1 - c)
    chips = _other_chips(x, y)
    hi, ho = D_MODEL // 2, GROUP // 2
    fwd_sems = N_GATHER_SEMS // 2

    def rc(src, dst, sem, to):
        return pltpu.make_async_remote_copy(src_ref=src, dst_ref=dst, send_sem=send_sems.at[sem],
                                            recv_sem=recv_sems.at[sem], device_id=to, device_id_type=MESH)

    def blk(ref, k, n, cc):
        return ref.at[k, pl.ds(cc * n, n), :]

    def ici(r):
        px, py, _ = chips[r]
        to = (px, py, c)
        return [rc(nwi.at[pl.ds(c * hi, hi), :], blk(gwi, me_k, hi, c), 2 * r, to),
                rc(nwo.at[pl.ds(c * ho, ho), :], blk(gwo, me_k, ho, c), 2 * r + 1, to)]

    def landed(r, cc, base):
        pk = chips[r][2]
        return [rc(blk(gwi, pk, hi, cc), blk(gwi, pk, hi, cc), base + 2 * r, sibling),
                rc(blk(gwo, pk, ho, cc), blk(gwo, pk, ho, cc), base + 2 * r + 1, sibling)]

    def stage_in():
        return [pltpu.make_async_copy(nwi, vwi, loc_sems.at[0]), pltpu.make_async_copy(nwo, vwo, loc_sems.at[1])]

    def local():
        return [pltpu.make_async_copy(vwi, gwi.at[me_k], loc_sems.at[2]),
                pltpu.make_async_copy(vwo, gwo.at[me_k], loc_sems.at[3])]

    @pl.when(step == 0)
    def _():
        _handshake_signal(PEERS_COLUMN)
        for cp in stage_in():
            cp.start()

    @pl.when(step == 1)
    def _():
        _handshake_wait(PEERS_COLUMN)
        for r in range(3):
            for cp in ici(r):
                cp.start()
        for cp in stage_in():
            cp.wait()
        for cp in local():
            cp.start()

    @pl.when(step == (3 * nt) // 4)
    def _():
        for r in range(3):
            for got, fwd in zip(landed(r, c, 0), landed(r, c, fwd_sems)):
                got.wait_recv()
                fwd.start()

    @pl.when(step == nt - 1)
    def _():
        for r in range(3):
            for got in landed(r, 1 - c, fwd_sems):
                got.wait_recv()
        for r in range(3):
            for cp in ici(r) + landed(r, c, fwd_sems):
                cp.wait_send()
        for cp in local():
            cp.wait()


def _fwd_layer(larr, x, wi, bin_, caw, cbw, s256, seg, pw, wm, sb, wo, v1024, *, tile, nxt=None, target=None):
    assert nxt is None or target is None
    S = x.shape[0]
    T = tile
    nt = S // T
    alpha = float((2.0 * 4) ** 0.25)
    n_in = 13 + (2 if nxt is not None else 0) + (1 if target is not None else 0)
    n_out = 6 + (2 if nxt is not None else 0) + (1 if target is not None else 0)

    def body(*refs):
        l_ref = refs[0]
        (x_ref, wi_ref, bin_ref, caw_ref, cbw_ref, s256_ref, seg_ref, pw_ref, wm_ref, sb_ref, wo_ref,
         v1024_ref) = refs[1:13]
        y_ref, xb_ref, h_ref, aux_ref, mix_ref, z_ref = refs[n_in:n_in + 6]
        abuf, bbuf, cbuf, wm_s, shf = refs[n_in + n_out:n_in + n_out + 5]
        i = pl.program_id(0)
        if nxt is not None:
            _gather_next(i, nt, refs[13].at[l_ref[0] + 1], refs[14].at[l_ref[0] + 1], refs[n_in + 6], refs[n_in + 7],
                         *refs[n_in + n_out + 5:])

        @pl.when(i == 0)
        def _():
            abuf[0:HALO_A, :] = jnp.zeros((HALO_A, GROUP), F32)
            bbuf[0:HALO_B, :] = jnp.zeros((HALO_B, GROUP), F32)
            cbuf[0:HALO_C, :] = jnp.zeros((HALO_C, GROUP), F32)
            _sgu_masks(wm_ref, None, wm_s, None)

        x = x_ref[...]
        xb = x.astype(BF16)
        xb_ref[...] = xb
        for k in range(N_CHIPS):
            h_ref[:, COLS * k:COLS * (k + 1)] = _dot(xb, wi_ref[k]) + bin_ref[:, COLS * k:COLS * (k + 1)]

        def hs(j):
            return h_ref[:, GROUP * j:GROUP * (j + 1)]

        abuf[HALO_A:HALO_A + T, :] = hs(0) * _sig(hs(1))
        span = T + HALO_A - SUBLANES
        for p in range(1, SUBLANES):
            shf[p - 1, :, :] = abuf[p:p + span, :]
        for r0 in range(0, T, ROWS):
            acc = None
            for k in range(KA):
                off = HALO_A - (KA - 1) + k
                p, q8 = off % SUBLANES, off - off % SUBLANES
                win = abuf[r0 + q8:r0 + q8 + ROWS, :] if p == 0 else shf[p - 1, r0 + q8:r0 + q8 + ROWS, :]
                term = caw_ref[k:k + 1, :] * win
                acc = term if acc is None else acc + term
            aux_ref[r0:r0 + ROWS, 0:GROUP] = acc + s256_ref[0:1, :]
        abuf[0:HALO_A, :] = abuf[T:T + HALO_A, :]
        a1 = aux_ref[:, 0:GROUP]
        segm = seg_ref[...]
        cen = a1 - _segdot(a1, segm)
        var = _segdot(cen * cen, segm)
        a2 = cen * lax.rsqrt(var + LN_EPS) * s256_ref[1:2, :] + s256_ref[2:3, :]
        az = hs(2)
        mix_ref[:, 0:GROUP] = (a2 * _sig(a2) * (az * _sig(az))).astype(BF16)

        bbuf[HALO_B:HALO_B + T, :] = hs(4) * hs(5)
        for r0 in range(0, T, ROWS):
            acc = None
            for k in range(KB):
                off = HALO_B - (KB - 1) + k + r0
                term = cbw_ref[k:k + 1, :] * bbuf[off:off + ROWS, :]
                acc = term if acc is None else acc + term
            aux_ref[r0:r0 + ROWS, GROUP:2 * GROUP] = acc
        bbuf[0:HALO_B, :] = bbuf[T:T + HALO_B, :]
        bz = hs(6)
        mix_ref[:, GROUP:2 * GROUP] = (hs(3) * aux_ref[:, GROUP:2 * GROUP] * (bz * _sig(bz))).astype(BF16)

        ch = hs(7)
        cbuf[HALO_C:HALO_C + T, :] = ch
        hi_lane = (lax.broadcasted_iota(jnp.int32, (1, 128), 1) // HEAD) == 1
        for r0 in range(0, T, ROWS):
            def win(col, j0, j1):
                s = None
                for j in range(j0, j1):
                    off = HALO_C - j + r0
                    term = cbuf[off:off + ROWS, 128 * col:128 * (col + 1)]
                    s = term if s is None else s + term
                return s
            w0 = win(0, 0, 2) + jnp.where(hi_lane, win(0, 2, 4), 0.0)
            w1 = win(1, 0, 8) + jnp.where(hi_lane, win(1, 8, 16), 0.0)
            aux_ref[r0:r0 + ROWS, 2 * GROUP:2 * GROUP + 128] = w0
            aux_ref[r0:r0 + ROWS, 2 * GROUP + 128:3 * GROUP] = w1
        cbuf[0:HALO_C, :] = cbuf[T:T + HALO_C, :]
        pooled = aux_ref[:, 2 * GROUP:3 * GROUP] / _pool_cnt(i, T) - ch
        aux_ref[:, 2 * GROUP:3 * GROUP] = pooled
        q = _dot(pooled.astype(BF16), pw_ref[...])
        cz = hs(8)
        mix_ref[:, 2 * GROUP:3 * GROUP] = (q * s256_ref[3:4, :] * (cz * _sig(cz))).astype(BF16)

        dv = hs(10)
        cen = dv - _rowmean(dv)
        var = _rowmean(cen * cen)
        v = cen * lax.rsqrt(var + LN_EPS) * s256_ref[4:5, :] + s256_ref[5:6, :]
        sps = []
        for n in range(T // SGU_BLOCK):
            vb = v[n * SGU_BLOCK:(n + 1) * SGU_BLOCK, :]
            sps.append(_dot(wm_s[...], _vstack(vb)) + sb_ref[...])
        sp = jnp.concatenate(sps, axis=0)
        dz = hs(11)
        mix_ref[:, 3 * GROUP:4 * GROUP] = (hs(9) * sp * (dz * _sig(dz))).astype(BF16)

        out = v1024_ref[0:1, :]
        for k in range(N_CHIPS):
            out = out + _dot(mix_ref[:, GROUP * k:GROUP * (k + 1)], wo_ref[k])
        z = alpha * x + out
        z_ref[...] = z
        cen = z - _rowmean(z)
        var = _rowmean(cen * cen)
        y = cen * lax.rsqrt(var + LN_EPS) * v1024_ref[1:2, :] + v1024_ref[2:3, :]
        if target is None:
            y_ref[...] = y
        else:
            t_ref, loss_ref = refs[13], refs[n_in + 6]

            @pl.when(i == 0)
            def _():
                loss_ref[...] = jnp.zeros_like(loss_ref)
            err = y - t_ref[...]
            y_ref[...] = err * (1.0 / D_MODEL)
            loss_ref[...] += jnp.sum(_colsum(err * err), axis=1, keepdims=True) * (0.5 / D_MODEL)

    def rows(width):
        return pl.BlockSpec((T, width), lambda i, l: (i, 0))

    consts = (wi, bin_, caw, cbw, s256, seg, pw, wm, sb, wo, v1024)
    in_specs = [rows(D_MODEL)] + [_whole(a) if a is wi or a is seg or a is wo else _of_layer(a) for a in consts]
    out_specs = [rows(D_MODEL), rows(D_MODEL), rows(IN_WIDTH), rows(3 * GROUP), rows(D_MODEL), rows(D_MODEL)]
    out_shape = [jax.ShapeDtypeStruct((S, D_MODEL), F32), jax.ShapeDtypeStruct((S, D_MODEL), BF16),
                 jax.ShapeDtypeStruct((S, IN_WIDTH), F32), jax.ShapeDtypeStruct((S, 3 * GROUP), F32),
                 jax.ShapeDtypeStruct((S, D_MODEL), BF16), jax.ShapeDtypeStruct((S, D_MODEL), F32)]
    scratch = [pltpu.VMEM((T + HALO_A, GROUP), F32), pltpu.VMEM((T + HALO_B, GROUP), F32),
               pltpu.VMEM((T + HALO_C, GROUP), F32), pltpu.VMEM((SGU_BLOCK, 4 * SGU_BLOCK), BF16),
               pltpu.VMEM((SUBLANES - 1, T + HALO_A - SUBLANES, GROUP), F32)]
    extra = ()
    if nxt is not None:
        extra = tuple(nxt)
        in_specs += [ANY, ANY]
        out_specs += [ANY, ANY]
        out_shape += [jax.ShapeDtypeStruct((N_CHIPS, D_MODEL, COLS), BF16),
                      jax.ShapeDtypeStruct((N_CHIPS, GROUP, D_MODEL), BF16)]
        scratch += [pltpu.SemaphoreType.DMA((N_GATHER_SEMS,)), pltpu.SemaphoreType.DMA((N_GATHER_SEMS,)),
                    pltpu.SemaphoreType.DMA((4,)), pltpu.VMEM((D_MODEL, COLS), BF16), pltpu.VMEM((GROUP, D_MODEL), BF16)]
    if target is not None:
        extra = (target,)
        in_specs += [rows(D_MODEL)]
        out_specs += [pl.BlockSpec((8, 128), lambda i, l: (0, 0))]
        out_shape += [jax.ShapeDtypeStruct((8, 128), F32)]
    grid_spec = pltpu.PrefetchScalarGridSpec(num_scalar_prefetch=1, grid=(nt,), in_specs=in_specs,
                                             out_specs=out_specs, scratch_shapes=scratch)
    return pl.pallas_call(
        body, name=("fwd_layer_loss" if target is not None else "fwd_layer") if nxt is None else "fwd_layer_gather",
        grid_spec=grid_spec, out_shape=out_shape,
        compiler_params=_vmem_params(dimension_semantics=("arbitrary",), **(
            dict(has_side_effects=True, collective_id=COLLECTIVE_ID["fwd_layer_gather"]) if nxt is not None else {})),
    )(larr, x, *consts, *extra)


ROW_CBW = 8
ROW_CAW = 16
ROW_LOSS = 7
ROW_PW = 48
ROW_LNG = 112
ROW_LNB = 116
ROW_BOUT = 120
ROW_BIN = 124
ROW_WC = 136
ROW_SB = 392
SM_ROWS = 400
N_DEV = 8


def _exchange_comm(step, n_steps, l, p_i, p_o, sm, r_i, r_o, r_sm, send_sems, recv_sems, loc_sem):
    x, y, c = _place()
    me = 4 * x + 2 * y + c
    chips = _other_chips(x, y)

    def rc(src, dst, sem, to):
        return pltpu.make_async_remote_copy(src_ref=src, dst_ref=dst, send_sem=send_sems.at[sem],
                                            recv_sem=recv_sems.at[sem], device_id=to, device_id_type=MESH)

    def big(r):
        px, py, pk = chips[r]
        to = (px, py, c)
        return [rc(p_i.at[l, pk], r_i.at[r, l], 2 * r, to), rc(p_o.at[l, pk], r_o.at[r, l], 2 * r + 1, to)]

    def peer(rel):
        px = 1 - x if rel & 4 else x
        py = 1 - y if rel & 2 else y
        pc = 1 - c if rel & 1 else c
        return (px, py, pc), 4 * px + 2 * py + pc

    def small_out(rel):
        to, _ = peer(rel)
        return rc(sm, r_sm.at[me], N_EXCH_SEMS - N_DEV + rel, to)

    def small_in(rel):
        to, idx = peer(rel)
        return rc(sm, r_sm.at[idx], N_EXCH_SEMS - N_DEV + rel, to)

    def local():
        return pltpu.make_async_copy(sm, r_sm.at[me], loc_sem.at[0])

    with_big, with_small = p_i is not None, sm is not None

    @pl.when(step == 0)
    def _():
        _handshake_signal(PEERS_ALL)

    @pl.when(step == 1)
    def _():
        _handshake_wait(PEERS_ALL)
        if with_small:
            local().start()
        if with_big:
            for r in range(3):
                for cp in big(r):
                    cp.start()
        if with_small:
            for rel in range(1, N_DEV):
                small_out(rel).start()

    @pl.when(step == n_steps - 1)
    def _():
        if with_big:
            for r in range(3):
                for cp in big(r):
                    cp.wait()
        if with_small:
            for rel in range(1, N_DEV):
                small_in(rel).wait_recv()
                small_out(rel).wait_send()
            local().wait()


RC = 32
RC_WIDE = 16
ACC_ROWS = 136


def _rsum8(v):
    r = v[0:8]
    for j in range(1, v.shape[0] // 8):
        r = r + v[8 * j:8 * j + 8]
    return r


def _bwd_layer(larr, dy, z, h, aux, wi, caw, cbw, s256, seg, pw, wm, wmt, sb, wo, v1024, e4, *, tile, exch=None):
    S = dy.shape[0]
    T = tile
    nt = S // T
    nblk = T // SGU_BLOCK
    alpha = float((2.0 * 4) ** 0.25)
    n_in = 17 + (5 if exch is not None else 0)
    n_out = 4 + (3 if exch is not None else 0)
    slab = pltpu.VMEM((T, GROUP), F32)
    scratch = dict(
        dbuf=pltpu.VMEM((T + HALO_A, GROUP), F32), ebuf=pltpu.VMEM((T + HALO_B, GROUP), F32),
        fbuf=pltpu.VMEM((T + HALO_C, GROUP), F32), sh=pltpu.VMEM((SUBLANES - 1, T + HALO_A - SUBLANES, GROUP), F32),
        wm_s=pltpu.VMEM((SGU_BLOCK, 4 * SGU_BLOCK), BF16), wmt_s=pltpu.VMEM((4 * SGU_BLOCK, SGU_BLOCK), BF16),
        dsp_acc=pltpu.VMEM((SGU_BLOCK, GROUP), F32), pw_acc=pltpu.VMEM((GROUP, GROUP), F32),
        acc_s=pltpu.VMEM((8 * ACC_ROWS, GROUP), F32), acc_w=pltpu.VMEM((24, D_MODEL), F32),
        dmix_s=pltpu.VMEM((T, D_MODEL), F32), vst_s=pltpu.VMEM((nblk, 4 * SGU_BLOCK, GROUP), BF16),
        dq_s=pltpu.VMEM((T, GROUP), BF16), dxt_s=pltpu.VMEM((D_MODEL, T), F32),
        mean_s=slab, t1_s=slab, t2_s=slab, q_s=slab, xv_s=slab, rv_s=slab, v_s=slab, sp_s=slab, a0_s=slab, sg_s=slab,
        xh_s=slab, ra_s=slab, ub_s=slab, dsp_s=slab, m1_s=slab, m2_s=slab, dpool_s=slab, dvd_s=slab, u_s=slab,
        du_s=slab, cw_s=slab)
    names = list(scratch)

    def body(*refs):
        (dy_ref, z_ref, h_ref, aux_ref, wi_ref, caw_ref, cbw_ref, s256_ref, seg_ref, pw_ref, wm_ref, wmt_ref,
         sb_ref, wo_ref, v1024_ref, e4_ref) = refs[1:17]
        dx_ref, dhb_ref, dzb_ref, osm_ref = refs[n_in:n_in + 4]
        k0 = n_in + n_out
        sc = dict(zip(names, refs[k0:k0 + len(names)]))
        dbuf, ebuf, fbuf, sh = sc["dbuf"], sc["ebuf"], sc["fbuf"], sc["sh"]
        wm_s, wmt_s, dsp_acc, pw_acc, acc_s, acc_w = (sc[n] for n in ("wm_s", "wmt_s", "dsp_acc", "pw_acc", "acc_s",
                                                                        "acc_w"))
        dmix_s, vst_s, dq_s = sc["dmix_s"], sc["vst_s"], sc["dq_s"]
        i = pl.program_id(0)
        tile_idx = nt - 1 - i
        if exch is not None:
            p_i, p_o, sm = refs[17:20]
            r_i, r_o, r_sm = refs[n_in + 4:n_in + 7]
            _exchange_comm(i, nt, refs[0][0] + 1, p_i, p_o, sm, r_i, r_o, r_sm, *refs[k0 + len(names):])

        @pl.when(i == 0)
        def _():
            dbuf[T:T + HALO_A, :] = jnp.zeros((HALO_A, GROUP), F32)
            ebuf[T:T + HALO_B, :] = jnp.zeros((HALO_B, GROUP), F32)
            fbuf[T:T + HALO_C, :] = jnp.zeros((HALO_C, GROUP), F32)
            _sgu_masks(wm_ref, wmt_ref, wm_s, wmt_s)
            osm_ref[...] = jnp.zeros_like(osm_ref)
            dsp_acc[...] = jnp.zeros_like(dsp_acc)
            pw_acc[...] = jnp.zeros_like(pw_acc)
            acc_s[...] = jnp.zeros_like(acc_s)
            acc_w[...] = jnp.zeros_like(acc_w)

        def chunks(rc, fn):
            for c in range(T // rc):
                fn(pl.ds(c * rc, rc))

        def hs(j, rows):
            return h_ref[rows, GROUP * j:GROUP * (j + 1)]

        def acc_add(row, val):
            acc_s[8 * row:8 * row + 8, :] += _rsum8(val)

        def put_dh(j, rows, val):
            acc_add(ROW_BIN + j, val)
            dhb_ref[rows, GROUP * j:GROUP * (j + 1)] = val.astype(BF16)

        def dsilu(v, s):
            return s * (1.0 + v * (1.0 - s))

        def vec(r):
            return s256_ref[r:r + 1, :]

        def ln_bwd(rows):
            dyc = dy_ref[rows, :]
            zc = z_ref[rows, :]
            cen = zc - _rowmean(zc)
            rstd = lax.rsqrt(_rowmean(cen * cen) + LN_EPS)
            xhat = cen * rstd
            acc_w[0:8, :] += _rsum8(dyc * xhat)
            acc_w[8:16, :] += _rsum8(dyc)
            gdy = dyc * v1024_ref[1:2, :]
            dz = rstd * (gdy - _rowmean(gdy) - xhat * _rowmean(gdy * xhat))
            acc_w[16:24, :] += _rsum8(dz)
            dzb_ref[rows, :] = dz.astype(BF16)
            dx_ref[rows, :] = alpha * dz
        chunks(RC_WIDE, ln_bwd)

        segm = seg_ref[...]
        dzb = dzb_ref[...]
        for k in range(N_CHIPS):
            dmix_s[:, GROUP * k:GROUP * (k + 1)] = _dot_nt(dzb, wo_ref[k])
        sc["mean_s"][...] = _segdot(aux_ref[:, 0:GROUP], segm)
        pooled_b = aux_ref[:, 2 * GROUP:3 * GROUP].astype(BF16)
        sc["q_s"][...] = _dot(pooled_b, pw_ref[...])

        def centre(rows):
            cen = aux_ref[rows, 0:GROUP] - sc["mean_s"][rows, :]
            sc["t1_s"][rows, :] = cen * cen
            dv_in = hs(10, rows)
            cen_v = dv_in - _rowmean(dv_in)
            rstd_v = lax.rsqrt(_rowmean(cen_v * cen_v) + LN_EPS)
            xv = cen_v * rstd_v
            sc["xv_s"][rows, :] = xv
            sc["rv_s"][rows, :] = jnp.broadcast_to(rstd_v, xv.shape)
            sc["v_s"][rows, :] = xv * vec(4) + vec(5)
        chunks(RC, centre)

        sc["t2_s"][...] = _segdot(sc["t1_s"][...], segm)
        for n in range(nblk):
            blk = slice(n * SGU_BLOCK, (n + 1) * SGU_BLOCK)
            vst_s[n] = _vstack(sc["v_s"][blk, :])
            sc["sp_s"][blk, :] = _dot(wm_s[...], vst_s[n]) + sb_ref[...]

        def mixers(rows):
            a_val, a_glu, a_z = hs(0, rows), hs(1, rows), hs(2, rows)
            sg = _sig(a_glu)
            sc["a0_s"][rows, :] = a_val * sg
            sc["sg_s"][rows, :] = sg
            rstd_a = lax.rsqrt(sc["t2_s"][rows, :] + LN_EPS)
            xh = (aux_ref[rows, 0:GROUP] - sc["mean_s"][rows, :]) * rstd_a
            a2 = xh * vec(1) + vec(2)
            s2 = _sig(a2)
            sz = _sig(a_z)
            dya = dmix_s[rows, 0:GROUP]
            put_dh(2, rows, dya * (a2 * s2) * dsilu(a_z, sz))
            d_a2 = dya * (a_z * sz) * dsilu(a2, s2)
            acc_add(1, d_a2 * xh)
            acc_add(2, d_a2)
            gd = d_a2 * vec(1)
            sc["t1_s"][rows, :] = gd
            sc["t2_s"][rows, :] = gd * xh
            sc["xh_s"][rows, :] = xh
            sc["ra_s"][rows, :] = rstd_a
            b_b, b_c, b_h, b_z = hs(3, rows), hs(4, rows), hs(5, rows), hs(6, rows)
            cb = aux_ref[rows, GROUP:2 * GROUP]
            sz = _sig(b_z)
            dyb = dmix_s[rows, GROUP:2 * GROUP]
            put_dh(3, rows, dyb * cb * (b_z * sz))
            put_dh(6, rows, dyb * b_b * cb * dsilu(b_z, sz))
            ebuf[rows, :] = dyb * b_b * (b_z * sz)
            sc["ub_s"][rows, :] = b_c * b_h
            c_z = hs(8, rows)
            q = sc["q_s"][rows, :]
            sz = _sig(c_z)
            dyc = dmix_s[rows, 2 * GROUP:3 * GROUP]
            acc_add(3, dyc * q * (c_z * sz))
            put_dh(8, rows, dyc * q * vec(3) * dsilu(c_z, sz))
            dq_s[rows, :] = (dyc * vec(3) * (c_z * sz)).astype(BF16)
            d_u, d_z = hs(9, rows), hs(11, rows)
            sp = sc["sp_s"][rows, :]
            sz = _sig(d_z)
            dyd = dmix_s[rows, 3 * GROUP:4 * GROUP]
            put_dh(9, rows, dyd * sp * (d_z * sz))
            put_dh(11, rows, dyd * d_u * sp * dsilu(d_z, sz))
            sc["dsp_s"][rows, :] = dyd * d_u * (d_z * sz)
        chunks(RC, mixers)

        sc["m1_s"][...] = _segdot(sc["t1_s"][...], segm)
        sc["m2_s"][...] = _segdot(sc["t2_s"][...], segm)
        d_q = dq_s[...]
        pw_acc[...] += _dot_tn(pooled_b, d_q)
        sc["dpool_s"][...] = _dot_nt(d_q, pw_ref[...])
        grp = _lane_group(GROUP)
        for n in range(nblk):
            blk = slice(n * SGU_BLOCK, (n + 1) * SGU_BLOCK)
            dspb = sc["dsp_s"][blk, :]
            dsp_acc[...] += dspb
            dspb16 = dspb.astype(BF16)
            dvst = _dot(wmt_s[...], dspb16)
            dvb = None
            for hh in range(4):
                part = jnp.where(grp == hh, dvst[hh * SGU_BLOCK:(hh + 1) * SGU_BLOCK, :], 0.0)
                dvb = part if dvb is None else dvb + part
            sc["dvd_s"][blk, :] = dvb
            dwc = _dot_nt(dspb16, vst_s[n])
            osm_ref[ROW_WC:ROW_WC + SGU_BLOCK, :] += dwc[:, 0:GROUP]
            osm_ref[ROW_WC + SGU_BLOCK:ROW_WC + 2 * SGU_BLOCK, :] += dwc[:, GROUP:2 * GROUP]

        def ln_sums(rows):
            xh = sc["xh_s"][rows, :]
            d_a1 = sc["ra_s"][rows, :] * (sc["t1_s"][rows, :] - sc["m1_s"][rows, :] - xh * sc["m2_s"][rows, :])
            acc_add(0, d_a1)
            dbuf[rows, :] = d_a1
            pos = tile_idx * T + rows.start + lax.broadcasted_iota(jnp.int32, (RC, GROUP), 0) + 1
            lane = lax.broadcasted_iota(jnp.int32, (RC, GROUP), 1) // HEAD
            win = jnp.where(lane == 0, 2, jnp.where(lane == 1, 4, jnp.where(lane == 2, 8, 16)))
            fbuf[rows, :] = sc["dpool_s"][rows, :] / jnp.minimum(pos, win).astype(F32)
            d_v = sc["dvd_s"][rows, :]
            xv = sc["xv_s"][rows, :]
            acc_add(4, d_v * xv)
            acc_add(5, d_v)
            gd = d_v * vec(4)
            put_dh(10, rows, sc["rv_s"][rows, :] * (gd - _rowmean(gd) - xv * _rowmean(gd * xv)))
        chunks(RC, ln_sums)

        span = T + HALO_A - SUBLANES
        for p in range(1, SUBLANES):
            sh[p - 1, :, :] = dbuf[p:p + span, :]

        for r0 in range(0, T, ROWS):
            uc = sc["ub_s"][r0:r0 + ROWS, :]
            acc = None
            for k in range(KB):
                off = (KB - 1) - k + r0
                w = ebuf[off:off + ROWS, :]
                term = cbw_ref[k:k + 1, :] * w
                acc = term if acc is None else acc + term
                acc_add(ROW_CBW + k, uc * w)
            sc["du_s"][r0:r0 + ROWS, :] = acc
        ebuf[T:T + HALO_B, :] = ebuf[0:HALO_B, :]

        hi_lane = (lax.broadcasted_iota(jnp.int32, (1, 128), 1) // HEAD) == 1
        for r0 in range(0, T, ROWS):
            def win(col, j0, j1):
                s = None
                for j in range(j0, j1):
                    term = fbuf[r0 + j:r0 + j + ROWS, 128 * col:128 * (col + 1)]
                    s = term if s is None else s + term
                return s
            sc["cw_s"][r0:r0 + ROWS, 0:128] = win(0, 0, 2) + jnp.where(hi_lane, win(0, 2, 4), 0.0)
            sc["cw_s"][r0:r0 + ROWS, 128:256] = win(1, 0, 8) + jnp.where(hi_lane, win(1, 8, 16), 0.0)
        fbuf[T:T + HALO_C, :] = fbuf[0:HALO_C, :]

        def rest_bc(rows):
            d_u = sc["du_s"][rows, :]
            put_dh(4, rows, d_u * hs(5, rows))
            put_dh(5, rows, d_u * hs(4, rows))
            put_dh(7, rows, sc["cw_s"][rows, :] - sc["dpool_s"][rows, :])
        chunks(RC, rest_bc)

        dxt_s = sc["dxt_s"]

        def dx_term(k):
            term = _dot_nt(wi_ref[k], dhb_ref[:, COLS * k:COLS * (k + 1)])
            if k == 1:
                dxt_s[...] = term
            else:
                dxt_s[...] += term

        def conv_a(rows):
            a0c = sc["a0_s"][rows, :]
            acc = None
            for k in range(KA):
                off = (KA - 1) - k
                p, q8 = off % SUBLANES, off - off % SUBLANES
                w = dbuf[pl.ds(rows.start + q8, RC), :] if p == 0 else sh[p - 1, pl.ds(rows.start + q8, RC), :]
                term = caw_ref[k:k + 1, :] * w
                acc = term if acc is None else acc + term
                acc_add(ROW_CAW + k, a0c * w)
            sc["u_s"][rows, :] = acc
        n_chunks = T // RC
        after = {(n_chunks * j) // 3: j + 1 for j in range(3)}
        for c in range(n_chunks):
            conv_a(pl.ds(c * RC, RC))
            if c in after:
                dx_term(after[c])
        dbuf[T:T + HALO_A, :] = dbuf[0:HALO_A, :]

        def rest_a(rows):
            d_a0 = sc["u_s"][rows, :]
            sg = sc["sg_s"][rows, :]
            put_dh(0, rows, d_a0 * sg)
            put_dh(1, rows, d_a0 * hs(0, rows) * sg * (1.0 - sg))
        chunks(RC, rest_a)
        dx_term(0)
        dx_ref[...] += dxt_s[...].T

        @pl.when(i == nt - 1)
        def _():
            for row in list(range(6)) + list(range(ROW_CBW, ROW_CBW + KB)) + list(range(ROW_CAW, ROW_CAW + KA)) + list(
                    range(ROW_BIN, ROW_BIN + N_SLICES)):
                osm_ref[row:row + 1, :] = _colsum(acc_s[8 * row:8 * row + 8, :])
            for j, row in enumerate((ROW_LNG, ROW_LNB, ROW_BOUT)):
                cs = _colsum(acc_w[8 * j:8 * j + 8, :])
                for q in range(D_MODEL // GROUP):
                    osm_ref[row + q:row + q + 1, :] = cs[:, GROUP * q:GROUP * (q + 1)]
            r = lax.broadcasted_iota(jnp.int32, (SGU_BLOCK, GROUP), 0) // CHUNK
            c = (lax.broadcasted_iota(jnp.int32, (SGU_BLOCK, GROUP), 1) % SGU_BLOCK) // CHUNK
            for half in range(2):
                rows_ = slice(ROW_WC + half * SGU_BLOCK, ROW_WC + (half + 1) * SGU_BLOCK)
                osm_ref[rows_, :] = jnp.where(c <= r, osm_ref[rows_, :], 0.0)
            sb_t = _segdot(dsp_acc[...], e4_ref[...]).T
            osm_ref[ROW_SB:ROW_SB + 8, 0:SGU_BLOCK] = sb_t[0:8, :]
            for g in range(4):
                osm_ref[ROW_PW:ROW_PW + HEAD, HEAD * g:HEAD * (g + 1)] = (
                    pw_acc[HEAD * g:HEAD * (g + 1), HEAD * g:HEAD * (g + 1)])

    def rows(width):
        return pl.BlockSpec((T, width), lambda i, l: (nt - 1 - i, 0))

    consts = (wi, caw, cbw, s256, seg, pw, wm, wmt, sb, wo, v1024, e4)
    unstacked = (wi, seg, wo, e4)
    in_specs = [rows(D_MODEL), rows(D_MODEL), rows(IN_WIDTH), rows(3 * GROUP)] + [
        _whole(a) if any(a is u for u in unstacked) else _of_layer(a) for a in consts]
    out_specs = [rows(D_MODEL), rows(IN_WIDTH), rows(D_MODEL), pl.BlockSpec((SM_ROWS, GROUP), lambda i, l: (0, 0))]
    out_shape = [jax.ShapeDtypeStruct((S, D_MODEL), F32), jax.ShapeDtypeStruct((S, IN_WIDTH), BF16),
                 jax.ShapeDtypeStruct((S, D_MODEL), BF16), jax.ShapeDtypeStruct((SM_ROWS, GROUP), F32)]
    scratch_shapes = list(scratch.values())
    extra, aliases = (), {}
    if exch is not None:
        extra = tuple(exch)
        r_i, r_o = exch[3], exch[4]
        in_specs += [ANY] * 5
        out_specs += [ANY] * 3
        out_shape += [jax.ShapeDtypeStruct(r_i.shape, r_i.dtype), jax.ShapeDtypeStruct(r_o.shape, r_o.dtype),
                      jax.ShapeDtypeStruct((N_DEV, SM_ROWS, GROUP), F32)]
        scratch_shapes += [pltpu.SemaphoreType.DMA((N_EXCH_SEMS,)), pltpu.SemaphoreType.DMA((N_EXCH_SEMS,)),
                           pltpu.SemaphoreType.DMA((1,))]
        aliases = {20: 4, 21: 5}
    grid_spec = pltpu.PrefetchScalarGridSpec(num_scalar_prefetch=1, grid=(nt,), in_specs=in_specs,
                                             out_specs=out_specs, scratch_shapes=scratch_shapes)
    return pl.pallas_call(
        body, name="bwd_layer" if exch is None else "bwd_layer_exchange",
        grid_spec=grid_spec, out_shape=out_shape, input_output_aliases=aliases,
        compiler_params=_vmem_params(dimension_semantics=("arbitrary",), **(
            dict(has_side_effects=True, collective_id=COLLECTIVE_ID["bwd_layer_exchange"]) if exch is not None else {})),
    )(larr, dy, z, h, aux, *consts, *extra)


def _dw(layer, xb, dhb, mixb, dzb, gwi, gwi16, gwo, gwo16, *, k_steps, small=None):
    S = xb.shape[0]
    tk = S // k_steps
    n_steps = N_CHIPS + k_steps

    def body(*refs):
        x_ref, dh_ref, mix_ref, dz_ref = refs[1:5]
        oi_ref, oi16_ref, oo_ref, oo16_ref = refs[n_in:n_in + 4]
        j = pl.program_id(0)
        if small is not None:
            _exchange_comm(j, n_steps, None, None, None, refs[9], None, None, refs[n_in + 4],
                           *refs[n_in + 5:])

        @pl.when(j < N_CHIPS)
        def _():
            acc = _dot_tn(x_ref[...], dh_ref[...])
            oi_ref[...] = acc
            oi16_ref[...] = acc.astype(BF16)

        @pl.when(j == N_CHIPS)
        def _():
            oo_ref[...] = jnp.zeros_like(oo_ref)

        @pl.when(j >= N_CHIPS)
        def _():
            oo_ref[...] += _dot_tn(mix_ref[...], dz_ref[...]).reshape(N_CHIPS, GROUP, D_MODEL)

        @pl.when(j == n_steps - 1)
        def _():
            oo16_ref[...] = oo_ref[...].astype(BF16)

    def col_block(j, l):
        return jnp.minimum(j, N_CHIPS - 1)

    def tok_block(j, l):
        return jnp.maximum(j - N_CHIPS, 0)

    oi_spec = pl.BlockSpec((None, None, D_MODEL, COLS), lambda j, l: (l[0], col_block(j, l), 0, 0))
    oo_spec = pl.BlockSpec((None, N_CHIPS, GROUP, D_MODEL), lambda j, l: (l[0], 0, 0, 0))
    in_specs = [pl.BlockSpec((S, D_MODEL), lambda j, l: (0, 0)),
                pl.BlockSpec((S, COLS), lambda j, l: (0, col_block(j, l))),
                pl.BlockSpec((tk, D_MODEL), lambda j, l: (tok_block(j, l), 0)),
                pl.BlockSpec((tk, D_MODEL), lambda j, l: (tok_block(j, l), 0)), ANY, ANY, ANY, ANY]
    out_specs = [oi_spec, oi_spec, oo_spec, oo_spec]
    out_shape = [jax.ShapeDtypeStruct(gwi.shape, F32), jax.ShapeDtypeStruct(gwi.shape, BF16),
                 jax.ShapeDtypeStruct(gwo.shape, F32), jax.ShapeDtypeStruct(gwo.shape, BF16)]
    scratch, extra = [], ()
    if small is not None:
        extra = (small,)
        in_specs += [ANY]
        out_specs += [ANY]
        out_shape += [jax.ShapeDtypeStruct((N_DEV, SM_ROWS, GROUP), F32)]
        scratch = [pltpu.SemaphoreType.DMA((N_EXCH_SEMS,)), pltpu.SemaphoreType.DMA((N_EXCH_SEMS,)), pltpu.SemaphoreType.DMA((1,))]
    n_in = 9 + len(extra)
    grid_spec = pltpu.PrefetchScalarGridSpec(
        num_scalar_prefetch=1, grid=(n_steps,), in_specs=in_specs, out_specs=out_specs, scratch_shapes=scratch)
    return pl.pallas_call(
        body, name="dw" if small is None else "dw_exchange", grid_spec=grid_spec, out_shape=out_shape,
        input_output_aliases={5: 0, 6: 1, 7: 2, 8: 3},
        compiler_params=_vmem_params(dimension_semantics=("arbitrary",), **(
            dict(has_side_effects=True, collective_id=COLLECTIVE_ID["dw_exchange"]) if small is not None else {})),
    )(layer, xb, dhb, mixb, dzb, gwi, gwi16, gwo, gwo16, *extra)


def _adamw_math(w, g, m, v):
    nm = ADAM_B1 * m + (1.0 - ADAM_B1) * g
    nv = ADAM_B2 * v + (1.0 - ADAM_B2) * (g * g)
    c1 = 1.0 - ADAM_B1 ** ADAM_STEP
    c2 = 1.0 - ADAM_B2 ** ADAM_STEP
    return -ADAM_LR * ((nm / c1) / (jnp.sqrt(nv / c2) + ADAM_EPS) + ADAM_WD * w), nm, nv


def _adamw_small(ws, gs, ms, vs):
    n = len(ws)

    def body(*refs):
        for j in range(n):
            d, nm, nv = _adamw_math(*(refs[k * n + j][...] for k in range(4)))
            refs[4 * n + j][...] = d
            refs[5 * n + j][...] = nm
            refs[6 * n + j][...] = nv

    shapes = [jax.ShapeDtypeStruct(w.shape, F32) for w in ws]
    outs = pl.pallas_call(body, name="adamw_small", out_shape=shapes * 3, compiler_params=_vmem_params())(
        *ws, *gs, *ms, *vs)
    return outs[0:n], outs[n:2 * n], outs[2 * n:3 * n]


def _adamw(w, g, m, v, *, rows_per_step, name, copy_g=False):
    R, C = w.shape
    tr = rows_per_step

    def body(w_ref, g_ref, m_ref, v_ref, d_ref, nm_ref, nv_ref, *g_out):
        g_ = g_ref[...]
        d_ref[...], nm_ref[...], nv_ref[...] = _adamw_math(w_ref[...], g_, m_ref[...], v_ref[...])
        if copy_g:
            g_out[0][...] = g_

    spec = pl.BlockSpec((tr, C), lambda i: (i, 0))
    n_out = 4 if copy_g else 3
    return pl.pallas_call(
        body, name=name, grid=(R // tr,),
        in_specs=[spec] * 4, out_specs=[spec] * n_out,
        out_shape=[jax.ShapeDtypeStruct((R, C), F32)] * n_out,
        compiler_params=_vmem_params(dimension_semantics=("arbitrary",)),
    )(w, g, m, v)


def _gather_weights(wi16, wo16, cw):
    L = wi16.shape[0]
    hi_rows, ho_rows = D_MODEL // 2, GROUP // 2
    n_ici = 2 * L + 1
    n_fwd = 2 * L

    def body(wi_ref, wo_ref, cw_ref, *rest):
        wig = rest[0:L]
        wog = rest[L:2 * L]
        cwg = rest[2 * L]
        send_sems, recv_sems, loc_sems, vwi, vwo, vcw = rest[2 * L + 1:]
        x, y, c = _place()
        me_k = 2 * x + y
        sibling = (x, y, 1 - c)
        chips = _other_chips(x, y)

        def half_i(ref, blk):
            return ref.at[blk, pl.ds(c * hi_rows, hi_rows), :]

        def half_o(ref, blk):
            return ref.at[blk, pl.ds(c * ho_rows, ho_rows), :]

        def other_half_i(ref, blk):
            return ref.at[blk, pl.ds((1 - c) * hi_rows, hi_rows), :]

        def other_half_o(ref, blk):
            return ref.at[blk, pl.ds((1 - c) * ho_rows, ho_rows), :]

        stage_in = [pltpu.make_async_copy(wi_ref, vwi, loc_sems.at[0]), pltpu.make_async_copy(wo_ref, vwo, loc_sems.at[1]),
                    pltpu.make_async_copy(cw_ref, vcw, loc_sems.at[2])]
        local = []
        for l in range(L):
            local.append(pltpu.make_async_copy(vwi.at[l], wig[l].at[me_k], loc_sems.at[3 + 2 * l]))
            local.append(pltpu.make_async_copy(vwo.at[l], wog[l].at[me_k], loc_sems.at[3 + 2 * l + 1]))
        local.append(pltpu.make_async_copy(vcw, cwg.at[me_k], loc_sems.at[3 + 2 * L]))
        _handshake(PEERS_COLUMN)
        for cp in stage_in:
            cp.start()

        def remote(src, dst, sem, to):
            return pltpu.make_async_remote_copy(src_ref=src, dst_ref=dst, send_sem=send_sems.at[sem],
                                                recv_sem=recv_sems.at[sem], device_id=to, device_id_type=MESH)

        sends = []
        for r, (px, py, _) in enumerate(chips):
            to = (px, py, c)
            for l in range(L):
                sends.append(remote(half_i(wi_ref, l), half_i(wig[l], me_k), r * n_ici + 2 * l, to))
                sends.append(remote(half_o(wo_ref, l), half_o(wog[l], me_k), r * n_ici + 2 * l + 1, to))
            sends.append(remote(cw_ref, cwg.at[me_k], r * n_ici + 2 * L, to))
        for cp in sends:
            cp.start()
        for cp in stage_in:
            cp.wait()
        for cp in local:
            cp.start()

        base = 3 * n_ici
        fwds = []
        for r, (px, py, pk) in enumerate(chips):
            for l in range(L):
                remote(half_i(wig[l], pk), half_i(wig[l], pk), r * n_ici + 2 * l, sibling).wait_recv()
                f = remote(half_i(wig[l], pk), half_i(wig[l], pk), base + r * n_fwd + 2 * l, sibling)
                f.start()
                fwds.append(f)
                remote(half_o(wog[l], pk), half_o(wog[l], pk), r * n_ici + 2 * l + 1, sibling).wait_recv()
                f = remote(half_o(wog[l], pk), half_o(wog[l], pk), base + r * n_fwd + 2 * l + 1, sibling)
                f.start()
                fwds.append(f)
            remote(cwg.at[pk], cwg.at[pk], r * n_ici + 2 * L, sibling).wait_recv()
        for r, (px, py, pk) in enumerate(chips):
            for l in range(L):
                remote(other_half_i(wig[l], pk), other_half_i(wig[l], pk), base + r * n_fwd + 2 * l, sibling).wait_recv()
                remote(other_half_o(wog[l], pk), other_half_o(wog[l], pk), base + r * n_fwd + 2 * l + 1, sibling).wait_recv()
        for cp in sends + fwds:
            cp.wait_send()
        for cp in local:
            cp.wait()

    n_sem = 3 * n_ici + 3 * n_fwd
    out_shape = ([jax.ShapeDtypeStruct((N_CHIPS, D_MODEL, COLS), BF16)] * L
                 + [jax.ShapeDtypeStruct((N_CHIPS, GROUP, D_MODEL), BF16)] * L
                 + [jax.ShapeDtypeStruct((N_CHIPS,) + cw.shape, F32)])
    outs = pl.pallas_call(
        body, name="gather_weights",
        in_specs=[ANY, ANY, ANY], out_specs=[ANY] * (2 * L + 1), out_shape=out_shape,
        scratch_shapes=[pltpu.SemaphoreType.DMA((n_sem,)), pltpu.SemaphoreType.DMA((n_sem,)),
                        pltpu.SemaphoreType.DMA((2 * L + 4,)), pltpu.VMEM(wi16.shape, BF16), pltpu.VMEM(wo16.shape, BF16),
                        pltpu.VMEM(cw.shape, F32)],
        compiler_params=_vmem_params(has_side_effects=True, collective_id=COLLECTIVE_ID["gather_weights"]),
    )(wi16, wo16, cw)
    return outs[0:L], outs[L:2 * L], outs[2 * L]


def _swap_add(cl_arr, g_i, g16_i, p_i, g_o, g16_o, p_o, *, send_on=None):
    hi, ho = p_i.shape[2], p_o.shape[2]
    n_in = 7 + (2 if send_on is not None else 0)
    n_out = 2 + (2 if send_on is not None else 0)

    def body(*refs):
        cl_ref, gi_ref, gi16_ref, _, go_ref, go16_ref = refs[0:6]
        oi_ref, oo_ref = refs[n_in:n_in + 2]
        ri_v, ro_v, send_sems, recv_sems = refs[n_in + n_out:n_in + n_out + 4]
        k = pl.program_id(0)
        x, y, c = _place()
        l = cl_ref[1]

        def copies(kk):
            pair = ((gi16_ref, hi, ri_v), (go16_ref, ho, ro_v))
            return [pltpu.make_async_remote_copy(
                src_ref=src.at[l, kk, pl.ds((1 - c) * n, n), :], dst_ref=dst.at[kk], send_sem=send_sems.at[2 * kk + j],
                recv_sem=recv_sems.at[2 * kk + j], device_id=(x, y, 1 - c), device_id_type=MESH)
                for j, (src, n, dst) in enumerate(pair)]

        @pl.when(k == 0)
        def _():
            _handshake(PEERS_SIBLING if send_on is None else PEERS_COLUMN)
            for kk in range(N_CHIPS):
                for cp in copies(kk):
                    cp.start()

        for cp in copies(k):
            cp.wait_recv()
        pi_k = (gi_ref[...] + ri_v[k].astype(F32)).astype(oi_ref.dtype)
        po_k = (go_ref[...] + ro_v[k].astype(F32)).astype(oo_ref.dtype)
        oi_ref[...] = pi_k
        oo_ref[...] = po_k

        if send_on is not None:
            qi_ref, qo_ref = refs[n_in + 2:n_in + 4]
            pv_i, pv_o, out_sems, in_sems = refs[n_in + n_out + 4:]
            pv_i[k] = pi_k
            pv_o[k] = po_k
            chips = _other_chips(x, y)

            def onward(r):
                px, py, pk = chips[r]
                return [pltpu.make_async_remote_copy(
                    src_ref=pv.at[pk], dst_ref=q.at[r, l], send_sem=out_sems.at[2 * r + j], recv_sem=in_sems.at[2 * r + j],
                    device_id=(px, py, c), device_id_type=MESH) for j, (pv, q) in enumerate(((pv_i, qi_ref), (pv_o, qo_ref)))]

            for r in range(3):
                @pl.when(k == chips[r][2])
                def _():
                    for cp in onward(r):
                        cp.start()

        @pl.when(k == N_CHIPS - 1)
        def _():
            for kk in range(N_CHIPS):
                for cp in copies(kk):
                    cp.wait_send()
            if send_on is not None:
                for r in range(3):
                    for cp in onward(r):
                        cp.wait()

    def specs(p):
        rows, cols = p.shape[2], p.shape[3]
        mine = pl.BlockSpec((None, None, rows, cols), lambda k, cl: (cl[1], k, cl[0], 0))
        out = pl.BlockSpec((None, None, rows, cols), lambda k, cl: (cl[1], k, 0, 0))
        return mine, out

    (gi_s, pi_s), (go_s, po_s) = specs(p_i), specs(p_o)
    in_specs = [gi_s, ANY, ANY, go_s, ANY, ANY]
    out_specs = [pi_s, po_s]
    out_shape = [jax.ShapeDtypeStruct(p_i.shape, p_i.dtype), jax.ShapeDtypeStruct(p_o.shape, p_o.dtype)]
    scratch = [pltpu.VMEM((N_CHIPS, hi, p_i.shape[3]), BF16), pltpu.VMEM((N_CHIPS, ho, p_o.shape[3]), BF16),
               pltpu.SemaphoreType.DMA((2 * N_CHIPS,)), pltpu.SemaphoreType.DMA((2 * N_CHIPS,))]
    extra, aliases = (), {3: 0, 6: 1}
    if send_on is not None:
        extra = tuple(send_on)
        in_specs += [ANY, ANY]
        out_specs += [ANY, ANY]
        out_shape += [jax.ShapeDtypeStruct(q.shape, q.dtype) for q in send_on]
        scratch += [pltpu.VMEM((N_CHIPS, hi, p_i.shape[3]), BF16), pltpu.VMEM((N_CHIPS, ho, p_o.shape[3]), BF16),
                    pltpu.SemaphoreType.DMA((6,)), pltpu.SemaphoreType.DMA((6,))]
        aliases = {3: 0, 6: 1, 7: 2, 8: 3}
    grid_spec = pltpu.PrefetchScalarGridSpec(num_scalar_prefetch=1, grid=(N_CHIPS,), in_specs=in_specs,
                                             out_specs=out_specs, scratch_shapes=scratch)
    return pl.pallas_call(
        body, name="swap_add" if send_on is None else "swap_add_send", grid_spec=grid_spec, out_shape=out_shape,
        input_output_aliases=aliases,
        compiler_params=_vmem_params(dimension_semantics=("arbitrary",), has_side_effects=True,
                                     collective_id=COLLECTIVE_ID["swap_add" if send_on is None else "swap_add_send"]),
    )(cl_arr, g_i, g16_i, p_i, g_o, g16_o, p_o, *extra)


def _sum_small(r_sms):
    L = len(r_sms)

    def body(*refs):
        o_ref = refs[L]
        for l in range(L):
            acc = refs[l][0]
            for d in range(1, N_DEV):
                acc = acc + refs[l][d]
            o_ref[l] = acc

    return pl.pallas_call(
        body, name="sum_small",
        out_shape=jax.ShapeDtypeStruct((L,) + r_sms[0].shape[1:], F32),
        compiler_params=_vmem_params(),
    )(*r_sms)


def _sum_share(kc_arr, p_i, q_i, p_o, q_o, *, nb):
    L = p_i.shape[0]
    n_steps, slots = L * nb, 2

    def body(kc_ref, pi_ref, a0, a1, a2, po_ref, b0, b1, b2, oi_ref, oo_ref, vi, vo, loc_sems, send_sems, recv_sems):
        del kc_ref
        x, y, c = _place()
        t = pl.program_id(0) * nb + pl.program_id(1)

        def copies(s):
            l, i = s // nb, s % nb
            out = []
            for j, (v, o) in enumerate(((vi, oi_ref), (vo, oo_ref))):
                tr = v.shape[1]
                src, dst = v.at[s % slots], o.at[l, pl.ds((c * nb + i) * tr, tr), :]
                out.append((pltpu.make_async_copy(src, dst, loc_sems.at[2 * s + j]),
                            pltpu.make_async_remote_copy(src_ref=src, dst_ref=dst, send_sem=send_sems.at[2 * s + j],
                                                         recv_sem=recv_sems.at[2 * s + j], device_id=(x, y, 1 - c),
                                                         device_id_type=MESH)))
            return out

        def sent(s):
            for mine, theirs in copies(s):
                mine.wait()
                theirs.wait_send()

        @pl.when(t == 0)
        def _():
            _handshake(PEERS_SIBLING)

        @pl.when(t >= slots)
        def _():
            sent(t - slots)

        f = lambda ref: ref[...].astype(F32)
        vi[t % slots] = ((f(pi_ref) + f(a0)) + f(a1)) + f(a2)
        vo[t % slots] = ((f(po_ref) + f(b0)) + f(b1)) + f(b2)
        for mine, theirs in copies(t):
            mine.start()
            theirs.start()

        @pl.when(t == n_steps - 1)
        def _():
            for s in range(n_steps - slots, n_steps):
                sent(s)
            for s in range(n_steps):
                for _, theirs in copies(s):
                    theirs.wait_recv()

    def specs(p):
        tr, cols = p.shape[2] // nb, p.shape[3]
        chunk = pl.BlockSpec((None, None, tr, cols), lambda l, i, kc: (l, kc[0], i, 0))
        got = [pl.BlockSpec((None, None, tr, cols), lambda l, i, kc, _j=j: (_j, l, i, 0)) for j in range(3)]
        return [chunk] + got, pltpu.VMEM((slots, tr, cols), F32)

    (in_i, v_i), (in_o, v_o) = specs(p_i), specs(p_o)
    grid_spec = pltpu.PrefetchScalarGridSpec(
        num_scalar_prefetch=1, grid=(L, nb), in_specs=in_i + in_o, out_specs=[ANY, ANY],
        scratch_shapes=[v_i, v_o] + [pltpu.SemaphoreType.DMA((2 * n_steps,))] * 3)
    return pl.pallas_call(
        body, name="sum_share", grid_spec=grid_spec,
        out_shape=[jax.ShapeDtypeStruct((L, 2 * p.shape[2], p.shape[3]), F32) for p in (p_i, p_o)],
        compiler_params=_vmem_params(dimension_semantics=("arbitrary",) * 2, has_side_effects=True,
                                     collective_id=COLLECTIVE_ID["sum_share"]),
    )(kc_arr, p_i, q_i, q_i, q_i, p_o, q_o, q_o, q_o)


WEIGHTS = ("ln_g", "ln_b", "w_in", "b_in", "conv_a_w", "conv_a_b", "norm_a_g", "norm_a_b", "conv_b_w", "pool_w",
           "pool_scale", "sgu_ln_g", "sgu_ln_b", "sgu_w", "sgu_bias", "w_out", "b_out")


def _pad_rows(a, rows):
    return jnp.pad(a, ((0, rows - a.shape[0]), (0, 0)))


def _indicator_consts():
    seg = jnp.where((jnp.arange(GROUP)[:, None] // HEAD) == (jnp.arange(GROUP)[None, :] // HEAD),
                    1.0 / HEAD, 0.0).astype(BF16)
    e4 = ((jnp.arange(GROUP)[:, None] // HEAD) == jnp.arange(128)[None, :]).astype(BF16)
    return seg, e4


def _layer_consts(p, conv_full):
    L = conv_full.shape[0]
    same_head = jnp.eye(4, dtype=F32)[:, None, :, None] > 0

    def rows_to(a, rows):
        return jnp.pad(a, ((0, 0), (0, rows - a.shape[1]), (0, 0)))

    s256 = jnp.stack([p[n] for n in ("conv_a_b", "norm_a_g", "norm_a_b", "pool_scale", "sgu_ln_g", "sgu_ln_b")], axis=1)
    pw = jnp.where(same_head, p["pool_w"][:, :, :, None, :], 0.0).reshape(L, GROUP, GROUP)
    return dict(
        caw=rows_to(conv_full[:, :KA], 32), cbw=rows_to(conv_full[:, KA:], 8), s256=rows_to(s256, 8),
        pw=pw.astype(BF16),
        wm=jnp.transpose(p["sgu_w"], (0, 2, 1, 3)).reshape(L, SGU_BLOCK, 4 * SGU_BLOCK),
        wmt=jnp.transpose(p["sgu_w"], (0, 1, 3, 2)).reshape(L, 4 * SGU_BLOCK, SGU_BLOCK),
        sb=jnp.repeat(jnp.transpose(p["sgu_bias"], (0, 2, 1)), HEAD, axis=2),
        v1024=rows_to(jnp.stack([p["b_out"], p["ln_g"], p["ln_b"]], axis=1), 8),
        bin=p["b_in"][:, None, :])


def _unpack_small(sm):
    L = sm.shape[0]
    owc = jnp.concatenate([sm[:, ROW_WC:ROW_WC + SGU_BLOCK], sm[:, ROW_WC + SGU_BLOCK:ROW_WC + 2 * SGU_BLOCK]], axis=2)
    return dict(
        conv_a_b=sm[:, 0], norm_a_g=sm[:, 1], norm_a_b=sm[:, 2], pool_scale=sm[:, 3], sgu_ln_g=sm[:, 4],
        sgu_ln_b=sm[:, 5], conv_b_w=sm[:, ROW_CBW:ROW_CBW + KB], conv_a_w=sm[:, ROW_CAW:ROW_CAW + KA],
        pool_w=jnp.transpose(sm[:, ROW_PW:ROW_PW + HEAD].reshape(L, HEAD, 4, HEAD), (0, 2, 1, 3)),
        ln_g=sm[:, ROW_LNG:ROW_LNG + 4].reshape(L, D_MODEL), ln_b=sm[:, ROW_LNB:ROW_LNB + 4].reshape(L, D_MODEL),
        b_out=sm[:, ROW_BOUT:ROW_BOUT + 4].reshape(L, D_MODEL),
        b_in=sm[:, ROW_BIN:ROW_BIN + N_SLICES].reshape(L, IN_WIDTH),
        sgu_w=jnp.transpose(owc.reshape(L, SGU_BLOCK, 4, SGU_BLOCK), (0, 2, 1, 3)),
        sgu_bias=sm[:, ROW_SB:ROW_SB + 4, 0:SGU_BLOCK])


def _step(p, m, v, x, target, *, tile_f, tile_b, k_steps):
    L = p["ln_g"].shape[0]
    xi, yi, ci = _place()
    me_k = 2 * xi + yi
    hi_rows, ho_rows = D_MODEL // 2, GROUP // 2

    cw = jnp.concatenate([p["conv_a_w"], p["conv_b_w"]], axis=1).reshape(-1, 128)
    cw_rows = cw.shape[0]
    cw = _pad_rows(cw, -(-cw_rows // SUBLANES) * SUBLANES)
    wi16 = p["w_in"].astype(BF16)
    wo16 = p["w_out"].astype(BF16)
    wig0, wog0, cwg = _gather_weights(wi16[0:1], wo16[0:1], cw)
    cwg = cwg[:, :cw_rows].reshape(N_CHIPS, L, KA + KB, HEAD)
    conv_full = jnp.transpose(cwg, (1, 2, 0, 3)).reshape(L, KA + KB, GROUP)
    seg, e4 = _indicator_consts()
    k = _layer_consts(p, conv_full)
    layer = [jnp.full((1,), l, jnp.int32) for l in range(L)]

    hcur = x
    saved, wig, wog = [], [wig0[0]], [wog0[0]]
    for l in range(L):
        nxt = (wi16, wo16) if l + 1 < L else None
        outs = _fwd_layer(layer[l], hcur, wig[l], k["bin"], k["caw"], k["cbw"], k["s256"], seg, k["pw"], k["wm"], k["sb"],
                          wog[l], k["v1024"], tile=tile_f, nxt=nxt, target=None if nxt is not None else target)
        y, xb, h, aux, mixb, z = outs[0:6]
        if nxt is not None:
            wig.append(outs[6])
            wog.append(outs[7])
        saved.append((xb, h, aux, mixb, z))
        hcur = y

    dy = hcur
    loss_local = outs[6][0, 0]

    gwi = lax.empty((L, N_CHIPS, D_MODEL, COLS), F32)
    gwo = lax.empty((L, N_CHIPS, GROUP, D_MODEL), F32)
    gwi16 = lax.empty((L, N_CHIPS, D_MODEL, COLS), BF16)
    gwo16 = lax.empty((L, N_CHIPS, GROUP, D_MODEL), BF16)
    p_i = lax.empty((L, N_CHIPS, hi_rows, COLS), BF16)
    p_o = lax.empty((L, N_CHIPS, ho_rows, D_MODEL), BF16)
    q_i = lax.empty((3, L, hi_rows, COLS), BF16)
    q_o = lax.empty((3, L, ho_rows, D_MODEL), BF16)
    r_sm = [None] * L
    pending = None
    for l in reversed(range(L)):
        xb, h, aux, mixb, z = saved[l]
        exch = None if pending is None else (p_i, p_o, pending, q_i, q_o)
        outs = _bwd_layer(layer[l], dy, z, h, aux, wig[l], k["caw"], k["cbw"], k["s256"], seg, k["pw"], k["wm"],
                          k["wmt"], k["sb"], wog[l], k["v1024"], e4, tile=tile_b, exch=exch)
        dy, dhb, dzb, osm = outs[0:4]
        if l == L - 1:
            osm = osm.at[ROW_LOSS, 0].set(loss_local)
        if exch is not None:
            q_i, q_o, r_sm[l + 1] = outs[4:7]
        larr = layer[l]
        outs = _dw(larr, xb, dhb, mixb, dzb, gwi, gwi16, gwo, gwo16, k_steps=k_steps, small=osm if l == 0 else None)
        gwi, gwi16, gwo, gwo16 = outs[0:4]
        if l == 0:
            r_sm[0] = outs[4]
        cl_arr = jnp.stack([ci, jnp.int32(l)]).astype(jnp.int32)
        if l > 0:
            p_i, p_o = _swap_add(cl_arr, gwi, gwi16, p_i, gwo, gwo16, p_o)
        else:
            p_i, p_o, q_i, q_o = _swap_add(cl_arr, gwi, gwi16, p_i, gwo, gwo16, p_o, send_on=(q_i, q_o))
        pending = osm
    grad_x = dy

    summed = _sum_small(r_sm)
    loss = summed[L - 1, ROW_LOSS, 0]
    grads = _unpack_small(summed)
    for n in ("conv_a_w", "conv_b_w"):
        grads[n] = lax.dynamic_slice_in_dim(grads[n], me_k * HEAD, HEAD, axis=2)

    kc_arr = jnp.stack([me_k, ci]).astype(jnp.int32)
    g_i, g_o = _sum_share(kc_arr, p_i, q_i, p_o, q_o, nb=2)
    grads["w_in"] = g_i
    grads["w_out"] = g_o

    delta, new_m, new_v = {}, {}, {}
    for n, tr in (("w_in", 512), ("w_out", 256)):
        shp = p[n].shape
        args = [a.reshape(shp[0] * shp[1], shp[2]) for a in (p[n], grads[n], m[n], v[n])]
        outs = _adamw(*args, rows_per_step=tr, name="adamw_" + n, copy_g=True)
        delta[n], new_m[n], new_v[n], grads[n] = (a.reshape(shp) for a in outs)
    small = [n for n in WEIGHTS if n not in ("w_in", "w_out")]
    flat = [[a[n].reshape(-1, a[n].shape[-1]) for n in small] for a in (p, grads, m, v)]
    outs = _adamw_small(*flat)
    for j, n in enumerate(small):
        delta[n], new_m[n], new_v[n] = (o[j].reshape(p[n].shape) for o in outs)

    return (loss, grad_x[None], *[grads[n] for n in WEIGHTS], *[delta[n] for n in WEIGHTS],
            *[new_m[n] for n in WEIGHTS], *[new_v[n] for n in WEIGHTS])


def kernel(x, ln_g, ln_b, w_in, b_in, conv_a_w, conv_a_b, norm_a_g, norm_a_b, conv_b_w, pool_w, pool_scale, sgu_ln_g, sgu_ln_b, sgu_w, sgu_bias, w_out, b_out, loss_target, m_ln_g, m_ln_b, m_w_in, m_b_in, m_conv_a_w, m_conv_a_b, m_norm_a_g, m_norm_a_b, m_conv_b_w, m_pool_w, m_pool_scale, m_sgu_ln_g, m_sgu_ln_b, m_sgu_w, m_sgu_bias, m_w_out, m_b_out, v_ln_g, v_ln_b, v_w_in, v_b_in, v_conv_a_w, v_conv_a_b, v_norm_a_g, v_norm_a_b, v_conv_b_w, v_pool_w, v_pool_scale, v_sgu_ln_g, v_sgu_ln_b, v_sgu_w, v_sgu_bias, v_w_out, v_b_out):
    p = dict(ln_g=ln_g, ln_b=ln_b, w_in=w_in, b_in=b_in, conv_a_w=conv_a_w, conv_a_b=conv_a_b, norm_a_g=norm_a_g,
             norm_a_b=norm_a_b, conv_b_w=conv_b_w, pool_w=pool_w, pool_scale=pool_scale, sgu_ln_g=sgu_ln_g,
             sgu_ln_b=sgu_ln_b, sgu_w=sgu_w, sgu_bias=sgu_bias, w_out=w_out, b_out=b_out)
    m = dict(ln_g=m_ln_g, ln_b=m_ln_b, w_in=m_w_in, b_in=m_b_in, conv_a_w=m_conv_a_w, conv_a_b=m_conv_a_b,
             norm_a_g=m_norm_a_g, norm_a_b=m_norm_a_b, conv_b_w=m_conv_b_w, pool_w=m_pool_w, pool_scale=m_pool_scale,
             sgu_ln_g=m_sgu_ln_g, sgu_ln_b=m_sgu_ln_b, sgu_w=m_sgu_w, sgu_bias=m_sgu_bias, w_out=m_w_out, b_out=m_b_out)
    v = dict(ln_g=v_ln_g, ln_b=v_ln_b, w_in=v_w_in, b_in=v_b_in, conv_a_w=v_conv_a_w, conv_a_b=v_conv_a_b,
             norm_a_g=v_norm_a_g, norm_a_b=v_norm_a_b, conv_b_w=v_conv_b_w, pool_w=v_pool_w, pool_scale=v_pool_scale,
             sgu_ln_g=v_sgu_ln_g, sgu_ln_b=v_sgu_ln_b, sgu_w=v_sgu_w, sgu_bias=v_sgu_bias, w_out=v_w_out, b_out=v_b_out)
    return _step(p, m, v, x[0], loss_target[0], tile_f=256, tile_b=256, k_steps=4)
```

```python
import jax
import jax.numpy as jnp
from jax import lax
from jax.experimental import pallas as pl
from jax.experimental.pallas import tpu as pltpu

F32 = jnp.float32
BF16 = jnp.bfloat16
MESH = pl.DeviceIdType.MESH

D_MODEL = 1024
GROUP = 256
HEAD = 64
N_SLICES = 12
IN_WIDTH = N_SLICES * GROUP
N_CHIPS = 4
COLS = IN_WIDTH // N_CHIPS
KA = 31
KB = 3
SUBLANES = 8
HALO_A, HALO_B, HALO_C = 32, 8, 16
N_GATHER_SEMS = 12
N_EXCH_SEMS = 13
SGU_BLOCK = 128
CHUNK = 64
LN_EPS = 1e-5
ROWS = 64
V7X_VMEM_BYTES = 64 * 1024 * 1024
VMEM_LIMIT = V7X_VMEM_BYTES - 8 * 1024 * 1024

ADAM_LR, ADAM_B1, ADAM_B2, ADAM_EPS, ADAM_WD, ADAM_STEP = 0.001, 0.9, 0.999, 1e-08, 0.01, 10


ANY = pl.BlockSpec(memory_space=pl.ANY)


def _vmem_params(**kw):
    return pltpu.CompilerParams(vmem_limit_bytes=VMEM_LIMIT, **kw)


def _whole(a):
    return pl.BlockSpec(a.shape, lambda i, l, _n=a.ndim: (0,) * _n)


def _of_layer(a):
    return pl.BlockSpec((None,) + a.shape[1:], lambda i, l, _n=a.ndim: (l[0],) + (0,) * (_n - 1))


def _place():
    return lax.axis_index("x"), lax.axis_index("y"), lax.axis_index("c")


def _other_chips(x, y):
    return [(1 - x, y, 2 * (1 - x) + y), (x, 1 - y, 2 * x + (1 - y)), (1 - x, 1 - y, 2 * (1 - x) + (1 - y))]


PEERS_SIBLING, PEERS_COLUMN, PEERS_ALL = "sibling", "sibling and the same core of the other chips", "all"
COLLECTIVE_ID = dict(sum_share=0, swap_add=1, gather_weights=2, fwd_layer_gather=3, swap_add_send=4,
                     bwd_layer_exchange=5, dw_exchange=6, dw_swap=7)


def _handshake(peers):
    x, y, c = _place()
    if peers == PEERS_SIBLING:
        ids = [(x, y, 1 - c)]
    elif peers == PEERS_COLUMN:
        ids = [(x, y, 1 - c)] + [(px, py, c) for px, py, _ in _other_chips(x, y)]
    else:
        ids = [(1 - x if r & 4 else x, 1 - y if r & 2 else y, 1 - c if r & 1 else c) for r in range(1, 8)]
    barrier = pltpu.get_barrier_semaphore()
    for to in ids:
        pl.semaphore_signal(barrier, inc=1, device_id=to, device_id_type=MESH)
    pl.semaphore_wait(barrier, len(ids))


def _sig(v):
    return 0.5 * jnp.tanh(0.5 * v) + 0.5


def _dot(a, b):
    return jnp.dot(a, b, preferred_element_type=F32)


def _dot_nt(a, b):
    return lax.dot_general(a, b, (((1,), (1,)), ((), ())), preferred_element_type=F32)


def _dot_tn(a, b):
    return lax.dot_general(a, b, (((0,), (0,)), ((), ())), preferred_element_type=F32)


def _segdot(v, m):
    hi = v.astype(BF16)
    lo = (v - hi.astype(F32)).astype(BF16)
    return _dot(hi, m) + _dot(lo, m)


def _colsum(v):
    return jnp.sum(v, axis=0, keepdims=True)


def _rowmean(v):
    return jnp.mean(v, axis=-1, keepdims=True)


def _lane_group(n):
    return lax.broadcasted_iota(jnp.int32, (1, n), 1) // HEAD


def _pool_cnt(tile, t_rows):
    pos = tile * t_rows + lax.broadcasted_iota(jnp.int32, (t_rows, GROUP), 0) + 1
    grp = lax.broadcasted_iota(jnp.int32, (t_rows, GROUP), 1) // HEAD
    win = jnp.where(grp == 0, 2, jnp.where(grp == 1, 4, jnp.where(grp == 2, 8, 16)))
    return jnp.minimum(pos, win).astype(F32)


def _sgu_masks(wm_ref, wmt_ref, wm_s, wmt_s):
    r = lax.broadcasted_iota(jnp.int32, (SGU_BLOCK, 4 * SGU_BLOCK), 0) // CHUNK
    c = (lax.broadcasted_iota(jnp.int32, (SGU_BLOCK, 4 * SGU_BLOCK), 1) % SGU_BLOCK) // CHUNK
    wm_s[...] = jnp.where(c <= r, wm_ref[...], 0.0).astype(BF16)
    if wmt_ref is not None:
        rt = (lax.broadcasted_iota(jnp.int32, (4 * SGU_BLOCK, SGU_BLOCK), 0) % SGU_BLOCK) // CHUNK
        ct = lax.broadcasted_iota(jnp.int32, (4 * SGU_BLOCK, SGU_BLOCK), 1) // CHUNK
        wmt_s[...] = jnp.where(rt <= ct, wmt_ref[...], 0.0).astype(BF16)


def _vstack(v_blk):
    grp = _lane_group(GROUP)
    return jnp.concatenate([jnp.where(grp == h, v_blk, 0.0) for h in range(4)], axis=0).astype(BF16)


def _gather_next(step, nt, nwi, nwo, gwi, gwo, send_sems, recv_sems, loc_sems, vwi, vwo):
    x, y, c = _place()
    me_k = 2 * x + y
    sibling = (x, y, 1 - c)
    chips = _other_chips(x, y)
    hi, ho = D_MODEL // 2, GROUP // 2
    fwd_sems = N_GATHER_SEMS // 2

    def rc(src, dst, sem, to):
        return pltpu.make_async_remote_copy(src_ref=src, dst_ref=dst, send_sem=send_sems.at[sem],
                                            recv_sem=recv_sems.at[sem], device_id=to, device_id_type=MESH)

    def blk(ref, k, n, cc):
        return ref.at[k, pl.ds(cc * n, n), :]

    def ici(r):
        px, py, _ = chips[r]
        to = (px, py, c)
        return [rc(nwi.at[pl.ds(c * hi, hi), :], blk(gwi, me_k, hi, c), 2 * r, to),
                rc(nwo.at[pl.ds(c * ho, ho), :], blk(gwo, me_k, ho, c), 2 * r + 1, to)]

    def landed(r, cc, base):
        pk = chips[r][2]
        return [rc(blk(gwi, pk, hi, cc), blk(gwi, pk, hi, cc), base + 2 * r, sibling),
                rc(blk(gwo, pk, ho, cc), blk(gwo, pk, ho, cc), base + 2 * r + 1, sibling)]

    def stage_in():
        return [pltpu.make_async_copy(nwi, vwi, loc_sems.at[0]), pltpu.make_async_copy(nwo, vwo, loc_sems.at[1])]

    def local():
        return [pltpu.make_async_copy(vwi, gwi.at[me_k], loc_sems.at[2]),
                pltpu.make_async_copy(vwo, gwo.at[me_k], loc_sems.at[3])]

    @pl.when(step == 0)
    def _():
        _handshake(PEERS_COLUMN)
        for cp in stage_in():
            cp.start()
        for r in range(3):
            for cp in ici(r):
                cp.start()

    @pl.when(step == 1)
    def _():
        for cp in stage_in():
            cp.wait()
        for cp in local():
            cp.start()

    @pl.when(step == (3 * nt) // 4)
    def _():
        for r in range(3):
            for got, fwd in zip(landed(r, c, 0), landed(r, c, fwd_sems)):
                got.wait_recv()
                fwd.start()

    @pl.when(step == nt - 1)
    def _():
        for r in range(3):
            for got in landed(r, 1 - c, fwd_sems):
                got.wait_recv()
        for r in range(3):
            for cp in ici(r) + landed(r, c, fwd_sems):
                cp.wait_send()
        for cp in local():
            cp.wait()


def _fwd_layer(larr, x, wi, bin_, caw, cbw, s256, seg, pw, wm, sb, wo, v1024, *, tile, nxt=None, target=None):
    assert nxt is None or target is None
    S = x.shape[0]
    T = tile
    nt = S // T
    alpha = float((2.0 * 4) ** 0.25)
    n_in = 13 + (2 if nxt is not None else 0) + (1 if target is not None else 0)
    n_out = 6 + (2 if nxt is not None else 0) + (1 if target is not None else 0)

    def body(*refs):
        l_ref = refs[0]
        (x_ref, wi_ref, bin_ref, caw_ref, cbw_ref, s256_ref, seg_ref, pw_ref, wm_ref, sb_ref, wo_ref,
         v1024_ref) = refs[1:13]
        y_ref, xb_ref, h_ref, aux_ref, mix_ref, z_ref = refs[n_in:n_in + 6]
        abuf, bbuf, cbuf, wm_s, shf = refs[n_in + n_out:n_in + n_out + 5]
        i = pl.program_id(0)
        if nxt is not None:
            _gather_next(i, nt, refs[13].at[l_ref[0] + 1], refs[14].at[l_ref[0] + 1], refs[n_in + 6], refs[n_in + 7],
                         *refs[n_in + n_out + 5:])

        @pl.when(i == 0)
        def _():
            abuf[0:HALO_A, :] = jnp.zeros((HALO_A, GROUP), F32)
            bbuf[0:HALO_B, :] = jnp.zeros((HALO_B, GROUP), F32)
            cbuf[0:HALO_C, :] = jnp.zeros((HALO_C, GROUP), F32)
            _sgu_masks(wm_ref, None, wm_s, None)

        x = x_ref[...]
        xb = x.astype(BF16)
        xb_ref[...] = xb
        for k in range(N_CHIPS):
            h_ref[:, COLS * k:COLS * (k + 1)] = _dot(xb, wi_ref[k]) + bin_ref[:, COLS * k:COLS * (k + 1)]

        def hs(j):
            return h_ref[:, GROUP * j:GROUP * (j + 1)]

        abuf[HALO_A:HALO_A + T, :] = hs(0) * _sig(hs(1))
        span = T + HALO_A - SUBLANES
        for p in range(1, SUBLANES):
            shf[p - 1, :, :] = abuf[p:p + span, :]
        for r0 in range(0, T, ROWS):
            acc = None
            for k in range(KA):
                off = HALO_A - (KA - 1) + k
                p, q8 = off % SUBLANES, off - off % SUBLANES
                win = abuf[r0 + q8:r0 + q8 + ROWS, :] if p == 0 else shf[p - 1, r0 + q8:r0 + q8 + ROWS, :]
                term = caw_ref[k:k + 1, :] * win
                acc = term if acc is None else acc + term
            aux_ref[r0:r0 + ROWS, 0:GROUP] = acc + s256_ref[0:1, :]
        abuf[0:HALO_A, :] = abuf[T:T + HALO_A, :]
        a1 = aux_ref[:, 0:GROUP]
        segm = seg_ref[...]
        cen = a1 - _segdot(a1, segm)
        var = _segdot(cen * cen, segm)
        a2 = cen * lax.rsqrt(var + LN_EPS) * s256_ref[1:2, :] + s256_ref[2:3, :]
        az = hs(2)
        mix_ref[:, 0:GROUP] = (a2 * _sig(a2) * (az * _sig(az))).astype(BF16)

        bbuf[HALO_B:HALO_B + T, :] = hs(4) * hs(5)
        for r0 in range(0, T, ROWS):
            acc = None
            for k in range(KB):
                off = HALO_B - (KB - 1) + k + r0
                term = cbw_ref[k:k + 1, :] * bbuf[off:off + ROWS, :]
                acc = term if acc is None else acc + term
            aux_ref[r0:r0 + ROWS, GROUP:2 * GROUP] = acc
        bbuf[0:HALO_B, :] = bbuf[T:T + HALO_B, :]
        bz = hs(6)
        mix_ref[:, GROUP:2 * GROUP] = (hs(3) * aux_ref[:, GROUP:2 * GROUP] * (bz * _sig(bz))).astype(BF16)

        ch = hs(7)
        cbuf[HALO_C:HALO_C + T, :] = ch
        hi_lane = (lax.broadcasted_iota(jnp.int32, (1, 128), 1) // HEAD) == 1
        for r0 in range(0, T, ROWS):
            def win(col, j0, j1):
                s = None
                for j in range(j0, j1):
                    off = HALO_C - j + r0
                    term = cbuf[off:off + ROWS, 128 * col:128 * (col + 1)]
                    s = term if s is None else s + term
                return s
            w0 = win(0, 0, 2) + jnp.where(hi_lane, win(0, 2, 4), 0.0)
            w1 = win(1, 0, 8) + jnp.where(hi_lane, win(1, 8, 16), 0.0)
            aux_ref[r0:r0 + ROWS, 2 * GROUP:2 * GROUP + 128] = w0
            aux_ref[r0:r0 + ROWS, 2 * GROUP + 128:3 * GROUP] = w1
        cbuf[0:HALO_C, :] = cbuf[T:T + HALO_C, :]
        pooled = aux_ref[:, 2 * GROUP:3 * GROUP] / _pool_cnt(i, T) - ch
        aux_ref[:, 2 * GROUP:3 * GROUP] = pooled
        q = _dot(pooled.astype(BF16), pw_ref[...])
        cz = hs(8)
        mix_ref[:, 2 * GROUP:3 * GROUP] = (q * s256_ref[3:4, :] * (cz * _sig(cz))).astype(BF16)

        dv = hs(10)
        cen = dv - _rowmean(dv)
        var = _rowmean(cen * cen)
        v = cen * lax.rsqrt(var + LN_EPS) * s256_ref[4:5, :] + s256_ref[5:6, :]
        sps = []
        for n in range(T // SGU_BLOCK):
            vb = v[n * SGU_BLOCK:(n + 1) * SGU_BLOCK, :]
            sps.append(_dot(wm_s[...], _vstack(vb)) + sb_ref[...])
        sp = jnp.concatenate(sps, axis=0)
        dz = hs(11)
        mix_ref[:, 3 * GROUP:4 * GROUP] = (hs(9) * sp * (dz * _sig(dz))).astype(BF16)

        out = v1024_ref[0:1, :]
        for k in range(N_CHIPS):
            out = out + _dot(mix_ref[:, GROUP * k:GROUP * (k + 1)], wo_ref[k])
        z = alpha * x + out
        z_ref[...] = z
        cen = z - _rowmean(z)
        var = _rowmean(cen * cen)
        y = cen * lax.rsqrt(var + LN_EPS) * v1024_ref[1:2, :] + v1024_ref[2:3, :]
        if target is None:
            y_ref[...] = y
        else:
            t_ref, loss_ref = refs[13], refs[n_in + 6]

            @pl.when(i == 0)
            def _():
                loss_ref[...] = jnp.zeros_like(loss_ref)
            err = y - t_ref[...]
            y_ref[...] = err * (1.0 / D_MODEL)
            loss_ref[...] += jnp.sum(_colsum(err * err), axis=1, keepdims=True) * (0.5 / D_MODEL)

    def rows(width):
        return pl.BlockSpec((T, width), lambda i, l: (i, 0))

    consts = (wi, bin_, caw, cbw, s256, seg, pw, wm, sb, wo, v1024)
    in_specs = [rows(D_MODEL)] + [_whole(a) if a is wi or a is seg or a is wo else _of_layer(a) for a in consts]
    out_specs = [rows(D_MODEL), rows(D_MODEL), rows(IN_WIDTH), rows(3 * GROUP), rows(D_MODEL), rows(D_MODEL)]
    out_shape = [jax.ShapeDtypeStruct((S, D_MODEL), F32), jax.ShapeDtypeStruct((S, D_MODEL), BF16),
                 jax.ShapeDtypeStruct((S, IN_WIDTH), F32), jax.ShapeDtypeStruct((S, 3 * GROUP), F32),
                 jax.ShapeDtypeStruct((S, D_MODEL), BF16), jax.ShapeDtypeStruct((S, D_MODEL), F32)]
    scratch = [pltpu.VMEM((T + HALO_A, GROUP), F32), pltpu.VMEM((T + HALO_B, GROUP), F32),
               pltpu.VMEM((T + HALO_C, GROUP), F32), pltpu.VMEM((SGU_BLOCK, 4 * SGU_BLOCK), BF16),
               pltpu.VMEM((SUBLANES - 1, T + HALO_A - SUBLANES, GROUP), F32)]
    extra = ()
    if nxt is not None:
        extra = tuple(nxt)
        in_specs += [ANY, ANY]
        out_specs += [ANY, ANY]
        out_shape += [jax.ShapeDtypeStruct((N_CHIPS, D_MODEL, COLS), BF16),
                      jax.ShapeDtypeStruct((N_CHIPS, GROUP, D_MODEL), BF16)]
        scratch += [pltpu.SemaphoreType.DMA((N_GATHER_SEMS,)), pltpu.SemaphoreType.DMA((N_GATHER_SEMS,)),
                    pltpu.SemaphoreType.DMA((4,)), pltpu.VMEM((D_MODEL, COLS), BF16), pltpu.VMEM((GROUP, D_MODEL), BF16)]
    if target is not None:
        extra = (target,)
        in_specs += [rows(D_MODEL)]
        out_specs += [pl.BlockSpec((8, 128), lambda i, l: (0, 0))]
        out_shape += [jax.ShapeDtypeStruct((8, 128), F32)]
    grid_spec = pltpu.PrefetchScalarGridSpec(num_scalar_prefetch=1, grid=(nt,), in_specs=in_specs,
                                             out_specs=out_specs, scratch_shapes=scratch)
    return pl.pallas_call(
        body, name=("fwd_layer_loss" if target is not None else "fwd_layer") if nxt is None else "fwd_layer_gather",
        grid_spec=grid_spec, out_shape=out_shape,
        compiler_params=_vmem_params(dimension_semantics=("arbitrary",), **(
            dict(has_side_effects=True, collective_id=COLLECTIVE_ID["fwd_layer_gather"]) if nxt is not None else {})),
    )(larr, x, *consts, *extra)


ROW_CBW = 8
ROW_CAW = 16
ROW_LOSS = 7
ROW_PW = 48
ROW_LNG = 112
ROW_LNB = 116
ROW_BOUT = 120
ROW_BIN = 124
ROW_WC = 136
ROW_SB = 392
SM_ROWS = 400
N_DEV = 8


def _exchange_comm(start, finish, l, p_i, p_o, sm, r_i, r_o, r_sm, send_sems, recv_sems, loc_sem):
    x, y, c = _place()
    me = 4 * x + 2 * y + c
    chips = _other_chips(x, y)

    def rc(src, dst, sem, to):
        return pltpu.make_async_remote_copy(src_ref=src, dst_ref=dst, send_sem=send_sems.at[sem],
                                            recv_sem=recv_sems.at[sem], device_id=to, device_id_type=MESH)

    def big(r):
        px, py, pk = chips[r]
        to = (px, py, c)
        return [rc(p_i.at[l, pk], r_i.at[r, l], 2 * r, to), rc(p_o.at[l, pk], r_o.at[r, l], 2 * r + 1, to)]

    def peer(rel):
        px = 1 - x if rel & 4 else x
        py = 1 - y if rel & 2 else y
        pc = 1 - c if rel & 1 else c
        return (px, py, pc), 4 * px + 2 * py + pc

    def small_out(rel):
        to, _ = peer(rel)
        return rc(sm, r_sm.at[me], N_EXCH_SEMS - N_DEV + rel, to)

    def small_in(rel):
        to, idx = peer(rel)
        return rc(sm, r_sm.at[idx], N_EXCH_SEMS - N_DEV + rel, to)

    def local():
        return pltpu.make_async_copy(sm, r_sm.at[me], loc_sem.at[0])

    with_big, with_small = p_i is not None, sm is not None

    @pl.when(start)
    def _():
        _handshake(PEERS_ALL)
        if with_small:
            local().start()
        if with_big:
            for r in range(3):
                for cp in big(r):
                    cp.start()
        if with_small:
            for rel in range(1, N_DEV):
                small_out(rel).start()

    @pl.when(finish)
    def _():
        if with_big:
            for r in range(3):
                for cp in big(r):
                    cp.wait()
        if with_small:
            for rel in range(1, N_DEV):
                small_in(rel).wait_recv()
                small_out(rel).wait_send()
            local().wait()


RC = 32
RC_WIDE = 16
ACC_ROWS = 136


def _rsum8(v):
    r = v[0:8]
    for j in range(1, v.shape[0] // 8):
        r = r + v[8 * j:8 * j + 8]
    return r


def _bwd_layer(larr, dy, z, h, aux, wi, caw, cbw, s256, seg, pw, wm, wmt, sb, wo, v1024, e4, *, tile, exch=None):
    S = dy.shape[0]
    T = tile
    nt = S // T
    nblk = T // SGU_BLOCK
    alpha = float((2.0 * 4) ** 0.25)
    n_in = 17 + (5 if exch is not None else 0)
    n_out = 4 + (3 if exch is not None else 0)
    slab = pltpu.VMEM((T, GROUP), F32)
    scratch = dict(
        dbuf=pltpu.VMEM((T + HALO_A, GROUP), F32), ebuf=pltpu.VMEM((T + HALO_B, GROUP), F32),
        fbuf=pltpu.VMEM((T + HALO_C, GROUP), F32), sh=pltpu.VMEM((SUBLANES - 1, T + HALO_A - SUBLANES, GROUP), F32),
        wm_s=pltpu.VMEM((SGU_BLOCK, 4 * SGU_BLOCK), BF16), wmt_s=pltpu.VMEM((4 * SGU_BLOCK, SGU_BLOCK), BF16),
        dsp_acc=pltpu.VMEM((SGU_BLOCK, GROUP), F32), pw_acc=pltpu.VMEM((GROUP, GROUP), F32),
        acc_s=pltpu.VMEM((8 * ACC_ROWS, GROUP), F32), acc_w=pltpu.VMEM((24, D_MODEL), F32),
        dmix_s=pltpu.VMEM((T, D_MODEL), F32), vst_s=pltpu.VMEM((nblk, 4 * SGU_BLOCK, GROUP), BF16),
        dq_s=pltpu.VMEM((T, GROUP), BF16), dxt_s=pltpu.VMEM((D_MODEL, T), F32),
        mean_s=slab, t1_s=slab, t2_s=slab, q_s=slab, xv_s=slab, rv_s=slab, v_s=slab, sp_s=slab, a0_s=slab, sg_s=slab,
        xh_s=slab, ra_s=slab, ub_s=slab, dsp_s=slab, m1_s=slab, m2_s=slab, dpool_s=slab, dvd_s=slab, u_s=slab,
        du_s=slab, cw_s=slab)
    names = list(scratch)

    def body(*refs):
        (dy_ref, z_ref, h_ref, aux_ref, wi_ref, caw_ref, cbw_ref, s256_ref, seg_ref, pw_ref, wm_ref, wmt_ref,
         sb_ref, wo_ref, v1024_ref, e4_ref) = refs[1:17]
        dx_ref, dhb_ref, dzb_ref, osm_ref = refs[n_in:n_in + 4]
        k0 = n_in + n_out
        sc = dict(zip(names, refs[k0:k0 + len(names)]))
        dbuf, ebuf, fbuf, sh = sc["dbuf"], sc["ebuf"], sc["fbuf"], sc["sh"]
        wm_s, wmt_s, dsp_acc, pw_acc, acc_s, acc_w = (sc[n] for n in ("wm_s", "wmt_s", "dsp_acc", "pw_acc", "acc_s",
                                                                        "acc_w"))
        dmix_s, vst_s, dq_s = sc["dmix_s"], sc["vst_s"], sc["dq_s"]
        i = pl.program_id(0)
        tile_idx = nt - 1 - i
        if exch is not None:
            p_i, p_o, sm = refs[17:20]
            r_i, r_o, r_sm = refs[n_in + 4:n_in + 7]
            _exchange_comm(i == 0, i == nt - 1, refs[0][0] + 1, p_i, p_o, sm, r_i, r_o, r_sm, *refs[k0 + len(names):])

        @pl.when(i == 0)
        def _():
            dbuf[T:T + HALO_A, :] = jnp.zeros((HALO_A, GROUP), F32)
            ebuf[T:T + HALO_B, :] = jnp.zeros((HALO_B, GROUP), F32)
            fbuf[T:T + HALO_C, :] = jnp.zeros((HALO_C, GROUP), F32)
            _sgu_masks(wm_ref, wmt_ref, wm_s, wmt_s)
            osm_ref[...] = jnp.zeros_like(osm_ref)
            dsp_acc[...] = jnp.zeros_like(dsp_acc)
            pw_acc[...] = jnp.zeros_like(pw_acc)
            acc_s[...] = jnp.zeros_like(acc_s)
            acc_w[...] = jnp.zeros_like(acc_w)

        def chunks(rc, fn):
            for c in range(T // rc):
                fn(pl.ds(c * rc, rc))

        def hs(j, rows):
            return h_ref[rows, GROUP * j:GROUP * (j + 1)]

        def acc_add(row, val):
            acc_s[8 * row:8 * row + 8, :] += _rsum8(val)

        def put_dh(j, rows, val):
            acc_add(ROW_BIN + j, val)
            dhb_ref[rows, GROUP * j:GROUP * (j + 1)] = val.astype(BF16)

        def dsilu(v, s):
            return s * (1.0 + v * (1.0 - s))

        def vec(r):
            return s256_ref[r:r + 1, :]

        def ln_bwd(rows):
            dyc = dy_ref[rows, :]
            zc = z_ref[rows, :]
            cen = zc - _rowmean(zc)
            rstd = lax.rsqrt(_rowmean(cen * cen) + LN_EPS)
            xhat = cen * rstd
            acc_w[0:8, :] += _rsum8(dyc * xhat)
            acc_w[8:16, :] += _rsum8(dyc)
            gdy = dyc * v1024_ref[1:2, :]
            dz = rstd * (gdy - _rowmean(gdy) - xhat * _rowmean(gdy * xhat))
            acc_w[16:24, :] += _rsum8(dz)
            dzb_ref[rows, :] = dz.astype(BF16)
            dx_ref[rows, :] = alpha * dz
        chunks(RC_WIDE, ln_bwd)

        segm = seg_ref[...]
        dzb = dzb_ref[...]
        for k in range(N_CHIPS):
            dmix_s[:, GROUP * k:GROUP * (k + 1)] = _dot_nt(dzb, wo_ref[k])
        sc["mean_s"][...] = _segdot(aux_ref[:, 0:GROUP], segm)
        pooled_b = aux_ref[:, 2 * GROUP:3 * GROUP].astype(BF16)
        sc["q_s"][...] = _dot(pooled_b, pw_ref[...])

        def centre(rows):
            cen = aux_ref[rows, 0:GROUP] - sc["mean_s"][rows, :]
            sc["t1_s"][rows, :] = cen * cen
            dv_in = hs(10, rows)
            cen_v = dv_in - _rowmean(dv_in)
            rstd_v = lax.rsqrt(_rowmean(cen_v * cen_v) + LN_EPS)
            xv = cen_v * rstd_v
            sc["xv_s"][rows, :] = xv
            sc["rv_s"][rows, :] = jnp.broadcast_to(rstd_v, xv.shape)
            sc["v_s"][rows, :] = xv * vec(4) + vec(5)
        chunks(RC, centre)

        sc["t2_s"][...] = _segdot(sc["t1_s"][...], segm)
        for n in range(nblk):
            blk = slice(n * SGU_BLOCK, (n + 1) * SGU_BLOCK)
            vst_s[n] = _vstack(sc["v_s"][blk, :])
            sc["sp_s"][blk, :] = _dot(wm_s[...], vst_s[n]) + sb_ref[...]

        def mixers(rows):
            a_val, a_glu, a_z = hs(0, rows), hs(1, rows), hs(2, rows)
            sg = _sig(a_glu)
            sc["a0_s"][rows, :] = a_val * sg
            sc["sg_s"][rows, :] = sg
            rstd_a = lax.rsqrt(sc["t2_s"][rows, :] + LN_EPS)
            xh = (aux_ref[rows, 0:GROUP] - sc["mean_s"][rows, :]) * rstd_a
            a2 = xh * vec(1) + vec(2)
            s2 = _sig(a2)
            sz = _sig(a_z)
            dya = dmix_s[rows, 0:GROUP]
            put_dh(2, rows, dya * (a2 * s2) * dsilu(a_z, sz))
            d_a2 = dya * (a_z * sz) * dsilu(a2, s2)
            acc_add(1, d_a2 * xh)
            acc_add(2, d_a2)
            gd = d_a2 * vec(1)
            sc["t1_s"][rows, :] = gd
            sc["t2_s"][rows, :] = gd * xh
            sc["xh_s"][rows, :] = xh
            sc["ra_s"][rows, :] = rstd_a
            b_b, b_c, b_h, b_z = hs(3, rows), hs(4, rows), hs(5, rows), hs(6, rows)
            cb = aux_ref[rows, GROUP:2 * GROUP]
            sz = _sig(b_z)
            dyb = dmix_s[rows, GROUP:2 * GROUP]
            put_dh(3, rows, dyb * cb * (b_z * sz))
            put_dh(6, rows, dyb * b_b * cb * dsilu(b_z, sz))
            ebuf[rows, :] = dyb * b_b * (b_z * sz)
            sc["ub_s"][rows, :] = b_c * b_h
            c_z = hs(8, rows)
            q = sc["q_s"][rows, :]
            sz = _sig(c_z)
            dyc = dmix_s[rows, 2 * GROUP:3 * GROUP]
            acc_add(3, dyc * q * (c_z * sz))
            put_dh(8, rows, dyc * q * vec(3) * dsilu(c_z, sz))
            dq_s[rows, :] = (dyc * vec(3) * (c_z * sz)).astype(BF16)
            d_u, d_z = hs(9, rows), hs(11, rows)
            sp = sc["sp_s"][rows, :]
            sz = _sig(d_z)
            dyd = dmix_s[rows, 3 * GROUP:4 * GROUP]
            put_dh(9, rows, dyd * sp * (d_z * sz))
            put_dh(11, rows, dyd * d_u * sp * dsilu(d_z, sz))
            sc["dsp_s"][rows, :] = dyd * d_u * (d_z * sz)
        chunks(RC, mixers)

        sc["m1_s"][...] = _segdot(sc["t1_s"][...], segm)
        sc["m2_s"][...] = _segdot(sc["t2_s"][...], segm)
        d_q = dq_s[...]
        pw_acc[...] += _dot_tn(pooled_b, d_q)
        sc["dpool_s"][...] = _dot_nt(d_q, pw_ref[...])
        grp = _lane_group(GROUP)
        for n in range(nblk):
            blk = slice(n * SGU_BLOCK, (n + 1) * SGU_BLOCK)
            dspb = sc["dsp_s"][blk, :]
            dsp_acc[...] += dspb
            dspb16 = dspb.astype(BF16)
            dvst = _dot(wmt_s[...], dspb16)
            dvb = None
            for hh in range(4):
                part = jnp.where(grp == hh, dvst[hh * SGU_BLOCK:(hh + 1) * SGU_BLOCK, :], 0.0)
                dvb = part if dvb is None else dvb + part
            sc["dvd_s"][blk, :] = dvb
            dwc = _dot_nt(dspb16, vst_s[n])
            osm_ref[ROW_WC:ROW_WC + SGU_BLOCK, :] += dwc[:, 0:GROUP]
            osm_ref[ROW_WC + SGU_BLOCK:ROW_WC + 2 * SGU_BLOCK, :] += dwc[:, GROUP:2 * GROUP]

        def ln_sums(rows):
            xh = sc["xh_s"][rows, :]
            d_a1 = sc["ra_s"][rows, :] * (sc["t1_s"][rows, :] - sc["m1_s"][rows, :] - xh * sc["m2_s"][rows, :])
            acc_add(0, d_a1)
            dbuf[rows, :] = d_a1
            pos = tile_idx * T + rows.start + lax.broadcasted_iota(jnp.int32, (RC, GROUP), 0) + 1
            lane = lax.broadcasted_iota(jnp.int32, (RC, GROUP), 1) // HEAD
            win = jnp.where(lane == 0, 2, jnp.where(lane == 1, 4, jnp.where(lane == 2, 8, 16)))
            fbuf[rows, :] = sc["dpool_s"][rows, :] / jnp.minimum(pos, win).astype(F32)
            d_v = sc["dvd_s"][rows, :]
            xv = sc["xv_s"][rows, :]
            acc_add(4, d_v * xv)
            acc_add(5, d_v)
            gd = d_v * vec(4)
            put_dh(10, rows, sc["rv_s"][rows, :] * (gd - _rowmean(gd) - xv * _rowmean(gd * xv)))
        chunks(RC, ln_sums)

        span = T + HALO_A - SUBLANES
        for p in range(1, SUBLANES):
            sh[p - 1, :, :] = dbuf[p:p + span, :]

        for r0 in range(0, T, ROWS):
            uc = sc["ub_s"][r0:r0 + ROWS, :]
            acc = None
            for k in range(KB):
                off = (KB - 1) - k + r0
                w = ebuf[off:off + ROWS, :]
                term = cbw_ref[k:k + 1, :] * w
                acc = term if acc is None else acc + term
                acc_add(ROW_CBW + k, uc * w)
            sc["du_s"][r0:r0 + ROWS, :] = acc
        ebuf[T:T + HALO_B, :] = ebuf[0:HALO_B, :]

        hi_lane = (lax.broadcasted_iota(jnp.int32, (1, 128), 1) // HEAD) == 1
        for r0 in range(0, T, ROWS):
            def win(col, j0, j1):
                s = None
                for j in range(j0, j1):
                    term = fbuf[r0 + j:r0 + j + ROWS, 128 * col:128 * (col + 1)]
                    s = term if s is None else s + term
                return s
            sc["cw_s"][r0:r0 + ROWS, 0:128] = win(0, 0, 2) + jnp.where(hi_lane, win(0, 2, 4), 0.0)
            sc["cw_s"][r0:r0 + ROWS, 128:256] = win(1, 0, 8) + jnp.where(hi_lane, win(1, 8, 16), 0.0)
        fbuf[T:T + HALO_C, :] = fbuf[0:HALO_C, :]

        def rest_bc(rows):
            d_u = sc["du_s"][rows, :]
            put_dh(4, rows, d_u * hs(5, rows))
            put_dh(5, rows, d_u * hs(4, rows))
            put_dh(7, rows, sc["cw_s"][rows, :] - sc["dpool_s"][rows, :])
        chunks(RC, rest_bc)

        dxt_s = sc["dxt_s"]

        def dx_term(k):
            term = _dot_nt(wi_ref[k], dhb_ref[:, COLS * k:COLS * (k + 1)])
            if k == 1:
                dxt_s[...] = term
            else:
                dxt_s[...] += term

        def conv_a(rows):
            a0c = sc["a0_s"][rows, :]
            acc = None
            for k in range(KA):
                off = (KA - 1) - k
                p, q8 = off % SUBLANES, off - off % SUBLANES
                w = dbuf[pl.ds(rows.start + q8, RC), :] if p == 0 else sh[p - 1, pl.ds(rows.start + q8, RC), :]
                term = caw_ref[k:k + 1, :] * w
                acc = term if acc is None else acc + term
                acc_add(ROW_CAW + k, a0c * w)
            sc["u_s"][rows, :] = acc
        n_chunks = T // RC
        after = {(n_chunks * j) // 3: j + 1 for j in range(3)}
        for c in range(n_chunks):
            conv_a(pl.ds(c * RC, RC))
            if c in after:
                dx_term(after[c])
        dbuf[T:T + HALO_A, :] = dbuf[0:HALO_A, :]

        def rest_a(rows):
            d_a0 = sc["u_s"][rows, :]
            sg = sc["sg_s"][rows, :]
            put_dh(0, rows, d_a0 * sg)
            put_dh(1, rows, d_a0 * hs(0, rows) * sg * (1.0 - sg))
        chunks(RC, rest_a)
        dx_term(0)
        dx_ref[...] += dxt_s[...].T

        @pl.when(i == nt - 1)
        def _():
            for row in list(range(6)) + list(range(ROW_CBW, ROW_CBW + KB)) + list(range(ROW_CAW, ROW_CAW + KA)) + list(
                    range(ROW_BIN, ROW_BIN + N_SLICES)):
                osm_ref[row:row + 1, :] = _colsum(acc_s[8 * row:8 * row + 8, :])
            for j, row in enumerate((ROW_LNG, ROW_LNB, ROW_BOUT)):
                cs = _colsum(acc_w[8 * j:8 * j + 8, :])
                for q in range(D_MODEL // GROUP):
                    osm_ref[row + q:row + q + 1, :] = cs[:, GROUP * q:GROUP * (q + 1)]
            r = lax.broadcasted_iota(jnp.int32, (SGU_BLOCK, GROUP), 0) // CHUNK
            c = (lax.broadcasted_iota(jnp.int32, (SGU_BLOCK, GROUP), 1) % SGU_BLOCK) // CHUNK
            for half in range(2):
                rows_ = slice(ROW_WC + half * SGU_BLOCK, ROW_WC + (half + 1) * SGU_BLOCK)
                osm_ref[rows_, :] = jnp.where(c <= r, osm_ref[rows_, :], 0.0)
            sb_t = _segdot(dsp_acc[...], e4_ref[...]).T
            osm_ref[ROW_SB:ROW_SB + 8, 0:SGU_BLOCK] = sb_t[0:8, :]
            for g in range(4):
                osm_ref[ROW_PW:ROW_PW + HEAD, HEAD * g:HEAD * (g + 1)] = (
                    pw_acc[HEAD * g:HEAD * (g + 1), HEAD * g:HEAD * (g + 1)])

    def rows(width):
        return pl.BlockSpec((T, width), lambda i, l: (nt - 1 - i, 0))

    consts = (wi, caw, cbw, s256, seg, pw, wm, wmt, sb, wo, v1024, e4)
    unstacked = (wi, seg, wo, e4)
    in_specs = [rows(D_MODEL), rows(D_MODEL), rows(IN_WIDTH), rows(3 * GROUP)] + [
        _whole(a) if any(a is u for u in unstacked) else _of_layer(a) for a in consts]
    out_specs = [rows(D_MODEL), rows(IN_WIDTH), rows(D_MODEL), pl.BlockSpec((SM_ROWS, GROUP), lambda i, l: (0, 0))]
    out_shape = [jax.ShapeDtypeStruct((S, D_MODEL), F32), jax.ShapeDtypeStruct((S, IN_WIDTH), BF16),
                 jax.ShapeDtypeStruct((S, D_MODEL), BF16), jax.ShapeDtypeStruct((SM_ROWS, GROUP), F32)]
    scratch_shapes = list(scratch.values())
    extra, aliases = (), {}
    if exch is not None:
        extra = tuple(exch)
        r_i, r_o = exch[3], exch[4]
        in_specs += [ANY] * 5
        out_specs += [ANY] * 3
        out_shape += [jax.ShapeDtypeStruct(r_i.shape, r_i.dtype), jax.ShapeDtypeStruct(r_o.shape, r_o.dtype),
                      jax.ShapeDtypeStruct((N_DEV, SM_ROWS, GROUP), F32)]
        scratch_shapes += [pltpu.SemaphoreType.DMA((N_EXCH_SEMS,)), pltpu.SemaphoreType.DMA((N_EXCH_SEMS,)),
                           pltpu.SemaphoreType.DMA((1,))]
        aliases = {20: 4, 21: 5}
    grid_spec = pltpu.PrefetchScalarGridSpec(num_scalar_prefetch=1, grid=(nt,), in_specs=in_specs,
                                             out_specs=out_specs, scratch_shapes=scratch_shapes)
    return pl.pallas_call(
        body, name="bwd_layer" if exch is None else "bwd_layer_exchange",
        grid_spec=grid_spec, out_shape=out_shape, input_output_aliases=aliases,
        compiler_params=_vmem_params(dimension_semantics=("arbitrary",), **(
            dict(has_side_effects=True, collective_id=COLLECTIVE_ID["bwd_layer_exchange"]) if exch is not None else {})),
    )(larr, dy, z, h, aux, *consts, *extra)


def _dw(layer, xb, dhb, mixb, dzb, gwi, gwi16, gwo, gwo16, *, k_steps, small=None):
    S = xb.shape[0]
    tk = S // k_steps
    n_steps = N_CHIPS + k_steps

    def body(*refs):
        x_ref, dh_ref, mix_ref, dz_ref = refs[1:5]
        oi_ref, oi16_ref, oo_ref, oo16_ref = refs[n_in:n_in + 4]
        j = pl.program_id(0)
        if small is not None:
            _exchange_comm(j == 0, j == n_steps - 1, None, None, None, refs[9], None, None, refs[n_in + 4],
                           *refs[n_in + 5:])

        @pl.when(j < N_CHIPS)
        def _():
            acc = _dot_tn(x_ref[...], dh_ref[...])
            oi_ref[...] = acc
            oi16_ref[...] = acc.astype(BF16)

        @pl.when(j == N_CHIPS)
        def _():
            oo_ref[...] = jnp.zeros_like(oo_ref)

        @pl.when(j >= N_CHIPS)
        def _():
            oo_ref[...] += _dot_tn(mix_ref[...], dz_ref[...]).reshape(N_CHIPS, GROUP, D_MODEL)

        @pl.when(j == n_steps - 1)
        def _():
            oo16_ref[...] = oo_ref[...].astype(BF16)

    def col_block(j, l):
        return jnp.minimum(j, N_CHIPS - 1)

    def tok_block(j, l):
        return jnp.maximum(j - N_CHIPS, 0)

    oi_spec = pl.BlockSpec((None, None, D_MODEL, COLS), lambda j, l: (l[0], col_block(j, l), 0, 0))
    oo_spec = pl.BlockSpec((None, N_CHIPS, GROUP, D_MODEL), lambda j, l: (l[0], 0, 0, 0))
    in_specs = [pl.BlockSpec((S, D_MODEL), lambda j, l: (0, 0)),
                pl.BlockSpec((S, COLS), lambda j, l: (0, col_block(j, l))),
                pl.BlockSpec((tk, D_MODEL), lambda j, l: (tok_block(j, l), 0)),
                pl.BlockSpec((tk, D_MODEL), lambda j, l: (tok_block(j, l), 0)), ANY, ANY, ANY, ANY]
    out_specs = [oi_spec, oi_spec, oo_spec, oo_spec]
    out_shape = [jax.ShapeDtypeStruct(gwi.shape, F32), jax.ShapeDtypeStruct(gwi.shape, BF16),
                 jax.ShapeDtypeStruct(gwo.shape, F32), jax.ShapeDtypeStruct(gwo.shape, BF16)]
    scratch, extra = [], ()
    if small is not None:
        extra = (small,)
        in_specs += [ANY]
        out_specs += [ANY]
        out_shape += [jax.ShapeDtypeStruct((N_DEV, SM_ROWS, GROUP), F32)]
        scratch = [pltpu.SemaphoreType.DMA((N_EXCH_SEMS,)), pltpu.SemaphoreType.DMA((N_EXCH_SEMS,)), pltpu.SemaphoreType.DMA((1,))]
    n_in = 9 + len(extra)
    grid_spec = pltpu.PrefetchScalarGridSpec(
        num_scalar_prefetch=1, grid=(n_steps,), in_specs=in_specs, out_specs=out_specs, scratch_shapes=scratch)
    return pl.pallas_call(
        body, name="dw" if small is None else "dw_exchange", grid_spec=grid_spec, out_shape=out_shape,
        input_output_aliases={5: 0, 6: 1, 7: 2, 8: 3},
        compiler_params=_vmem_params(dimension_semantics=("arbitrary",), **(
            dict(has_side_effects=True, collective_id=COLLECTIVE_ID["dw_exchange"]) if small is not None else {})),
    )(layer, xb, dhb, mixb, dzb, gwi, gwi16, gwo, gwo16, *extra)


def _dw_swap(cl_arr, xb, dhb, mixb, dzb, p_i, p_o, *, k_steps):
    S = xb.shape[0]
    tk = S // k_steps
    n_steps = N_CHIPS + k_steps
    hi, ho = p_i.shape[2], p_o.shape[2]

    def body(cl_ref, x_ref, dh_ref, mix_ref, dz_ref, pi_in, po_in, pi_ref, po_ref,
             acc_i, acc_o, snd_i, snd_o, rcv_i, rcv_o, send_sems, recv_sems):
        del cl_ref, pi_in, po_in
        j = pl.program_id(0)
        x, y, c = _place()
        mine_i, theirs_i = (pl.ds(pl.multiple_of(cc * hi, hi), hi) for cc in (c, 1 - c))
        mine_o, theirs_o = (pl.ds(pl.multiple_of(cc * ho, ho), ho) for cc in (c, 1 - c))

        def to_sibling(src, dst, sem):
            return pltpu.make_async_remote_copy(src_ref=src, dst_ref=dst, send_sem=send_sems.at[sem],
                                                recv_sem=recv_sems.at[sem], device_id=(x, y, 1 - c), device_id_type=MESH)

        def chunk_copy(k):
            return to_sibling(snd_i.at[k % 2], rcv_i.at[k], k)

        def out_copy():
            return to_sibling(snd_o, rcv_o, N_CHIPS)

        @pl.when(j == 0)
        def _():
            _handshake(PEERS_SIBLING)

        @pl.when((j >= 1) & (j <= N_CHIPS))
        def _():
            chunk_copy(j - 1).wait_recv()
            pi_ref[...] = (acc_i[mine_i, :] + rcv_i[j - 1].astype(F32)).astype(pi_ref.dtype)

        @pl.when(j < N_CHIPS)
        def _():
            @pl.when(j >= 2)
            def _():
                chunk_copy(j - 2).wait_send()

            acc_i[...] = _dot_tn(x_ref[...], dh_ref[...])
            snd_i[j % 2] = acc_i[theirs_i, :].astype(BF16)
            chunk_copy(j).start()

        @pl.when(j == N_CHIPS)
        def _():
            acc_o[...] = jnp.zeros_like(acc_o)

        @pl.when(j >= N_CHIPS)
        def _():
            acc_o[...] += _dot_tn(mix_ref[...], dz_ref[...]).reshape(N_CHIPS, GROUP, D_MODEL)

        @pl.when(j == n_steps - 1)
        def _():
            snd_o[...] = acc_o[:, theirs_o, :].astype(BF16)
            out_copy().start()
            for k in (N_CHIPS - 2, N_CHIPS - 1):
                chunk_copy(k).wait_send()
            out_copy().wait_recv()
            po_ref[...] = (acc_o[:, mine_o, :] + rcv_o[...].astype(F32)).astype(po_ref.dtype)
            out_copy().wait_send()

    def col_block(j):
        return jnp.minimum(j, N_CHIPS - 1)

    def tok_block(j):
        return jnp.maximum(j - N_CHIPS, 0)

    in_specs = [pl.BlockSpec((S, D_MODEL), lambda j, cl: (0, 0)),
                pl.BlockSpec((S, COLS), lambda j, cl: (0, col_block(j))),
                pl.BlockSpec((tk, D_MODEL), lambda j, cl: (tok_block(j), 0)),
                pl.BlockSpec((tk, D_MODEL), lambda j, cl: (tok_block(j), 0)), ANY, ANY]
    out_specs = [pl.BlockSpec((None, None, hi, COLS), lambda j, cl: (cl[1], jnp.clip(j - 1, 0, N_CHIPS - 1), 0, 0)),
                 pl.BlockSpec((None, N_CHIPS, ho, D_MODEL), lambda j, cl: (cl[1], 0, 0, 0))]
    scratch = [pltpu.VMEM((D_MODEL, COLS), F32), pltpu.VMEM((N_CHIPS, GROUP, D_MODEL), F32),
               pltpu.VMEM((2, hi, COLS), BF16), pltpu.VMEM((N_CHIPS, ho, D_MODEL), BF16),
               pltpu.VMEM((N_CHIPS, hi, COLS), BF16), pltpu.VMEM((N_CHIPS, ho, D_MODEL), BF16),
               pltpu.SemaphoreType.DMA((N_CHIPS + 1,)), pltpu.SemaphoreType.DMA((N_CHIPS + 1,))]
    grid_spec = pltpu.PrefetchScalarGridSpec(
        num_scalar_prefetch=1, grid=(n_steps,), in_specs=in_specs, out_specs=out_specs, scratch_shapes=scratch)
    return pl.pallas_call(
        body, name="dw_swap", grid_spec=grid_spec,
        out_shape=[jax.ShapeDtypeStruct(p_i.shape, p_i.dtype), jax.ShapeDtypeStruct(p_o.shape, p_o.dtype)],
        input_output_aliases={5: 0, 6: 1},
        compiler_params=_vmem_params(dimension_semantics=("arbitrary",), has_side_effects=True,
                                     collective_id=COLLECTIVE_ID["dw_swap"]),
    )(cl_arr, xb, dhb, mixb, dzb, p_i, p_o)


def _adamw_math(w, g, m, v):
    nm = ADAM_B1 * m + (1.0 - ADAM_B1) * g
    nv = ADAM_B2 * v + (1.0 - ADAM_B2) * (g * g)
    c1 = 1.0 - ADAM_B1 ** ADAM_STEP
    c2 = 1.0 - ADAM_B2 ** ADAM_STEP
    return -ADAM_LR * ((nm / c1) / (jnp.sqrt(nv / c2) + ADAM_EPS) + ADAM_WD * w), nm, nv


def _adamw_small(ws, gs, ms, vs):
    n = len(ws)

    def body(*refs):
        for j in range(n):
            d, nm, nv = _adamw_math(*(refs[k * n + j][...] for k in range(4)))
            refs[4 * n + j][...] = d
            refs[5 * n + j][...] = nm
            refs[6 * n + j][...] = nv

    shapes = [jax.ShapeDtypeStruct(w.shape, F32) for w in ws]
    outs = pl.pallas_call(body, name="adamw_small", out_shape=shapes * 3, compiler_params=_vmem_params())(
        *ws, *gs, *ms, *vs)
    return outs[0:n], outs[n:2 * n], outs[2 * n:3 * n]


def _adamw(w, g, m, v, *, rows_per_step, name, copy_g=False):
    R, C = w.shape
    tr = rows_per_step

    def body(w_ref, g_ref, m_ref, v_ref, d_ref, nm_ref, nv_ref, *g_out):
        g_ = g_ref[...]
        d_ref[...], nm_ref[...], nv_ref[...] = _adamw_math(w_ref[...], g_, m_ref[...], v_ref[...])
        if copy_g:
            g_out[0][...] = g_

    spec = pl.BlockSpec((tr, C), lambda i: (i, 0))
    n_out = 4 if copy_g else 3
    return pl.pallas_call(
        body, name=name, grid=(R // tr,),
        in_specs=[spec] * 4, out_specs=[spec] * n_out,
        out_shape=[jax.ShapeDtypeStruct((R, C), F32)] * n_out,
        compiler_params=_vmem_params(dimension_semantics=("arbitrary",)),
    )(w, g, m, v)


def _gather_weights(wi16, wo16, cw):
    L = wi16.shape[0]
    hi_rows, ho_rows = D_MODEL // 2, GROUP // 2
    n_ici = 2 * L + 1
    n_fwd = 2 * L

    def body(wi_ref, wo_ref, cw_ref, *rest):
        wig = rest[0:L]
        wog = rest[L:2 * L]
        cwg = rest[2 * L]
        send_sems, recv_sems, loc_sems, vwi, vwo, vcw = rest[2 * L + 1:]
        x, y, c = _place()
        me_k = 2 * x + y
        sibling = (x, y, 1 - c)
        chips = _other_chips(x, y)

        def half_i(ref, blk):
            return ref.at[blk, pl.ds(c * hi_rows, hi_rows), :]

        def half_o(ref, blk):
            return ref.at[blk, pl.ds(c * ho_rows, ho_rows), :]

        def other_half_i(ref, blk):
            return ref.at[blk, pl.ds((1 - c) * hi_rows, hi_rows), :]

        def other_half_o(ref, blk):
            return ref.at[blk, pl.ds((1 - c) * ho_rows, ho_rows), :]

        stage_in = [pltpu.make_async_copy(wi_ref, vwi, loc_sems.at[0]), pltpu.make_async_copy(wo_ref, vwo, loc_sems.at[1]),
                    pltpu.make_async_copy(cw_ref, vcw, loc_sems.at[2])]
        local = []
        for l in range(L):
            local.append(pltpu.make_async_copy(vwi.at[l], wig[l].at[me_k], loc_sems.at[3 + 2 * l]))
            local.append(pltpu.make_async_copy(vwo.at[l], wog[l].at[me_k], loc_sems.at[3 + 2 * l + 1]))
        local.append(pltpu.make_async_copy(vcw, cwg.at[me_k], loc_sems.at[3 + 2 * L]))
        _handshake(PEERS_COLUMN)
        for cp in stage_in:
            cp.start()

        def remote(src, dst, sem, to):
            return pltpu.make_async_remote_copy(src_ref=src, dst_ref=dst, send_sem=send_sems.at[sem],
                                                recv_sem=recv_sems.at[sem], device_id=to, device_id_type=MESH)

        sends = []
        for r, (px, py, _) in enumerate(chips):
            to = (px, py, c)
            for l in range(L):
                sends.append(remote(half_i(wi_ref, l), half_i(wig[l], me_k), r * n_ici + 2 * l, to))
                sends.append(remote(half_o(wo_ref, l), half_o(wog[l], me_k), r * n_ici + 2 * l + 1, to))
            sends.append(remote(cw_ref, cwg.at[me_k], r * n_ici + 2 * L, to))
        for cp in sends:
            cp.start()
        for cp in stage_in:
            cp.wait()
        for cp in local:
            cp.start()

        base = 3 * n_ici
        fwds = []
        for r, (px, py, pk) in enumerate(chips):
            for l in range(L):
                remote(half_i(wig[l], pk), half_i(wig[l], pk), r * n_ici + 2 * l, sibling).wait_recv()
                f = remote(half_i(wig[l], pk), half_i(wig[l], pk), base + r * n_fwd + 2 * l, sibling)
                f.start()
                fwds.append(f)
                remote(half_o(wog[l], pk), half_o(wog[l], pk), r * n_ici + 2 * l + 1, sibling).wait_recv()
                f = remote(half_o(wog[l], pk), half_o(wog[l], pk), base + r * n_fwd + 2 * l + 1, sibling)
                f.start()
                fwds.append(f)
            remote(cwg.at[pk], cwg.at[pk], r * n_ici + 2 * L, sibling).wait_recv()
        for r, (px, py, pk) in enumerate(chips):
            for l in range(L):
                remote(other_half_i(wig[l], pk), other_half_i(wig[l], pk), base + r * n_fwd + 2 * l, sibling).wait_recv()
                remote(other_half_o(wog[l], pk), other_half_o(wog[l], pk), base + r * n_fwd + 2 * l + 1, sibling).wait_recv()
        for cp in sends + fwds:
            cp.wait_send()
        for cp in local:
            cp.wait()

    n_sem = 3 * n_ici + 3 * n_fwd
    out_shape = ([jax.ShapeDtypeStruct((N_CHIPS, D_MODEL, COLS), BF16)] * L
                 + [jax.ShapeDtypeStruct((N_CHIPS, GROUP, D_MODEL), BF16)] * L
                 + [jax.ShapeDtypeStruct((N_CHIPS,) + cw.shape, F32)])
    outs = pl.pallas_call(
        body, name="gather_weights",
        in_specs=[ANY, ANY, ANY], out_specs=[ANY] * (2 * L + 1), out_shape=out_shape,
        scratch_shapes=[pltpu.SemaphoreType.DMA((n_sem,)), pltpu.SemaphoreType.DMA((n_sem,)),
                        pltpu.SemaphoreType.DMA((2 * L + 4,)), pltpu.VMEM(wi16.shape, BF16), pltpu.VMEM(wo16.shape, BF16),
                        pltpu.VMEM(cw.shape, F32)],
        compiler_params=_vmem_params(has_side_effects=True, collective_id=COLLECTIVE_ID["gather_weights"]),
    )(wi16, wo16, cw)
    return outs[0:L], outs[L:2 * L], outs[2 * L]


def _swap_add(cl_arr, g_i, g16_i, p_i, g_o, g16_o, p_o, *, send_on=None):
    hi, ho = p_i.shape[2], p_o.shape[2]
    n_in = 7 + (2 if send_on is not None else 0)
    n_out = 2 + (2 if send_on is not None else 0)

    def body(*refs):
        cl_ref, gi_ref, gi16_ref, _, go_ref, go16_ref = refs[0:6]
        oi_ref, oo_ref = refs[n_in:n_in + 2]
        ri_v, ro_v, send_sems, recv_sems = refs[n_in + n_out:n_in + n_out + 4]
        k = pl.program_id(0)
        x, y, c = _place()
        l = cl_ref[1]

        def copies(kk):
            pair = ((gi16_ref, hi, ri_v), (go16_ref, ho, ro_v))
            return [pltpu.make_async_remote_copy(
                src_ref=src.at[l, kk, pl.ds((1 - c) * n, n), :], dst_ref=dst.at[kk], send_sem=send_sems.at[2 * kk + j],
                recv_sem=recv_sems.at[2 * kk + j], device_id=(x, y, 1 - c), device_id_type=MESH)
                for j, (src, n, dst) in enumerate(pair)]

        @pl.when(k == 0)
        def _():
            _handshake(PEERS_SIBLING if send_on is None else PEERS_COLUMN)
            for kk in range(N_CHIPS):
                for cp in copies(kk):
                    cp.start()

        for cp in copies(k):
            cp.wait_recv()
        pi_k = (gi_ref[...] + ri_v[k].astype(F32)).astype(oi_ref.dtype)
        po_k = (go_ref[...] + ro_v[k].astype(F32)).astype(oo_ref.dtype)
        oi_ref[...] = pi_k
        oo_ref[...] = po_k

        if send_on is not None:
            qi_ref, qo_ref = refs[n_in + 2:n_in + 4]
            pv_i, pv_o, out_sems, in_sems = refs[n_in + n_out + 4:]
            pv_i[k] = pi_k
            pv_o[k] = po_k
            chips = _other_chips(x, y)

            def onward(r):
                px, py, pk = chips[r]
                return [pltpu.make_async_remote_copy(
                    src_ref=pv.at[pk], dst_ref=q.at[r, l], send_sem=out_sems.at[2 * r + j], recv_sem=in_sems.at[2 * r + j],
                    device_id=(px, py, c), device_id_type=MESH) for j, (pv, q) in enumerate(((pv_i, qi_ref), (pv_o, qo_ref)))]

            for r in range(3):
                @pl.when(k == chips[r][2])
                def _():
                    for cp in onward(r):
                        cp.start()

        @pl.when(k == N_CHIPS - 1)
        def _():
            for kk in range(N_CHIPS):
                for cp in copies(kk):
                    cp.wait_send()
            if send_on is not None:
                for r in range(3):
                    for cp in onward(r):
                        cp.wait()

    def specs(p):
        rows, cols = p.shape[2], p.shape[3]
        mine = pl.BlockSpec((None, None, rows, cols), lambda k, cl: (cl[1], k, cl[0], 0))
        out = pl.BlockSpec((None, None, rows, cols), lambda k, cl: (cl[1], k, 0, 0))
        return mine, out

    (gi_s, pi_s), (go_s, po_s) = specs(p_i), specs(p_o)
    in_specs = [gi_s, ANY, ANY, go_s, ANY, ANY]
    out_specs = [pi_s, po_s]
    out_shape = [jax.ShapeDtypeStruct(p_i.shape, p_i.dtype), jax.ShapeDtypeStruct(p_o.shape, p_o.dtype)]
    scratch = [pltpu.VMEM((N_CHIPS, hi, p_i.shape[3]), BF16), pltpu.VMEM((N_CHIPS, ho, p_o.shape[3]), BF16),
               pltpu.SemaphoreType.DMA((2 * N_CHIPS,)), pltpu.SemaphoreType.DMA((2 * N_CHIPS,))]
    extra, aliases = (), {3: 0, 6: 1}
    if send_on is not None:
        extra = tuple(send_on)
        in_specs += [ANY, ANY]
        out_specs += [ANY, ANY]
        out_shape += [jax.ShapeDtypeStruct(q.shape, q.dtype) for q in send_on]
        scratch += [pltpu.VMEM((N_CHIPS, hi, p_i.shape[3]), BF16), pltpu.VMEM((N_CHIPS, ho, p_o.shape[3]), BF16),
                    pltpu.SemaphoreType.DMA((6,)), pltpu.SemaphoreType.DMA((6,))]
        aliases = {3: 0, 6: 1, 7: 2, 8: 3}
    grid_spec = pltpu.PrefetchScalarGridSpec(num_scalar_prefetch=1, grid=(N_CHIPS,), in_specs=in_specs,
                                             out_specs=out_specs, scratch_shapes=scratch)
    return pl.pallas_call(
        body, name="swap_add" if send_on is None else "swap_add_send", grid_spec=grid_spec, out_shape=out_shape,
        input_output_aliases=aliases,
        compiler_params=_vmem_params(dimension_semantics=("arbitrary",), has_side_effects=True,
                                     collective_id=COLLECTIVE_ID["swap_add" if send_on is None else "swap_add_send"]),
    )(cl_arr, g_i, g16_i, p_i, g_o, g16_o, p_o, *extra)


def _sum_small(r_sms):
    L = len(r_sms)

    def body(*refs):
        o_ref = refs[L]
        for l in range(L):
            acc = refs[l][0]
            for d in range(1, N_DEV):
                acc = acc + refs[l][d]
            o_ref[l] = acc

    return pl.pallas_call(
        body, name="sum_small",
        out_shape=jax.ShapeDtypeStruct((L,) + r_sms[0].shape[1:], F32),
        compiler_params=_vmem_params(),
    )(*r_sms)


def _sum_share(kc_arr, p_i, q_i, p_o, q_o, *, nb):
    L = p_i.shape[0]
    n_steps, slots = L * nb, 2

    def body(kc_ref, pi_ref, a0, a1, a2, po_ref, b0, b1, b2, oi_ref, oo_ref, vi, vo, loc_sems, send_sems, recv_sems):
        del kc_ref
        x, y, c = _place()
        t = pl.program_id(0) * nb + pl.program_id(1)

        def copies(s):
            l, i = s // nb, s % nb
            out = []
            for j, (v, o) in enumerate(((vi, oi_ref), (vo, oo_ref))):
                tr = v.shape[1]
                src, dst = v.at[s % slots], o.at[l, pl.ds((c * nb + i) * tr, tr), :]
                out.append((pltpu.make_async_copy(src, dst, loc_sems.at[2 * s + j]),
                            pltpu.make_async_remote_copy(src_ref=src, dst_ref=dst, send_sem=send_sems.at[2 * s + j],
                                                         recv_sem=recv_sems.at[2 * s + j], device_id=(x, y, 1 - c),
                                                         device_id_type=MESH)))
            return out

        def sent(s):
            for mine, theirs in copies(s):
                mine.wait()
                theirs.wait_send()

        @pl.when(t == 0)
        def _():
            _handshake(PEERS_SIBLING)

        @pl.when(t >= slots)
        def _():
            sent(t - slots)

        f = lambda ref: ref[...].astype(F32)
        vi[t % slots] = ((f(pi_ref) + f(a0)) + f(a1)) + f(a2)
        vo[t % slots] = ((f(po_ref) + f(b0)) + f(b1)) + f(b2)
        for mine, theirs in copies(t):
            mine.start()
            theirs.start()

        @pl.when(t == n_steps - 1)
        def _():
            for s in range(n_steps - slots, n_steps):
                sent(s)
            for s in range(n_steps):
                for _, theirs in copies(s):
                    theirs.wait_recv()

    def specs(p):
        tr, cols = p.shape[2] // nb, p.shape[3]
        chunk = pl.BlockSpec((None, None, tr, cols), lambda l, i, kc: (l, kc[0], i, 0))
        got = [pl.BlockSpec((None, None, tr, cols), lambda l, i, kc, _j=j: (_j, l, i, 0)) for j in range(3)]
        return [chunk] + got, pltpu.VMEM((slots, tr, cols), F32)

    (in_i, v_i), (in_o, v_o) = specs(p_i), specs(p_o)
    grid_spec = pltpu.PrefetchScalarGridSpec(
        num_scalar_prefetch=1, grid=(L, nb), in_specs=in_i + in_o, out_specs=[ANY, ANY],
        scratch_shapes=[v_i, v_o] + [pltpu.SemaphoreType.DMA((2 * n_steps,))] * 3)
    return pl.pallas_call(
        body, name="sum_share", grid_spec=grid_spec,
        out_shape=[jax.ShapeDtypeStruct((L, 2 * p.shape[2], p.shape[3]), F32) for p in (p_i, p_o)],
        compiler_params=_vmem_params(dimension_semantics=("arbitrary",) * 2, has_side_effects=True,
                                     collective_id=COLLECTIVE_ID["sum_share"]),
    )(kc_arr, p_i, q_i, q_i, q_i, p_o, q_o, q_o, q_o)


WEIGHTS = ("ln_g", "ln_b", "w_in", "b_in", "conv_a_w", "conv_a_b", "norm_a_g", "norm_a_b", "conv_b_w", "pool_w",
           "pool_scale", "sgu_ln_g", "sgu_ln_b", "sgu_w", "sgu_bias", "w_out", "b_out")


def _pad_rows(a, rows):
    return jnp.pad(a, ((0, rows - a.shape[0]), (0, 0)))


def _indicator_consts():
    seg = jnp.where((jnp.arange(GROUP)[:, None] // HEAD) == (jnp.arange(GROUP)[None, :] // HEAD),
                    1.0 / HEAD, 0.0).astype(BF16)
    e4 = ((jnp.arange(GROUP)[:, None] // HEAD) == jnp.arange(128)[None, :]).astype(BF16)
    return seg, e4


def _layer_consts(p, conv_full):
    L = conv_full.shape[0]
    same_head = jnp.eye(4, dtype=F32)[:, None, :, None] > 0

    def rows_to(a, rows):
        return jnp.pad(a, ((0, 0), (0, rows - a.shape[1]), (0, 0)))

    s256 = jnp.stack([p[n] for n in ("conv_a_b", "norm_a_g", "norm_a_b", "pool_scale", "sgu_ln_g", "sgu_ln_b")], axis=1)
    pw = jnp.where(same_head, p["pool_w"][:, :, :, None, :], 0.0).reshape(L, GROUP, GROUP)
    return dict(
        caw=rows_to(conv_full[:, :KA], 32), cbw=rows_to(conv_full[:, KA:], 8), s256=rows_to(s256, 8),
        pw=pw.astype(BF16),
        wm=jnp.transpose(p["sgu_w"], (0, 2, 1, 3)).reshape(L, SGU_BLOCK, 4 * SGU_BLOCK),
        wmt=jnp.transpose(p["sgu_w"], (0, 1, 3, 2)).reshape(L, 4 * SGU_BLOCK, SGU_BLOCK),
        sb=jnp.repeat(jnp.transpose(p["sgu_bias"], (0, 2, 1)), HEAD, axis=2),
        v1024=rows_to(jnp.stack([p["b_out"], p["ln_g"], p["ln_b"]], axis=1), 8),
        bin=p["b_in"][:, None, :])


def _unpack_small(sm):
    L = sm.shape[0]
    owc = jnp.concatenate([sm[:, ROW_WC:ROW_WC + SGU_BLOCK], sm[:, ROW_WC + SGU_BLOCK:ROW_WC + 2 * SGU_BLOCK]], axis=2)
    return dict(
        conv_a_b=sm[:, 0], norm_a_g=sm[:, 1], norm_a_b=sm[:, 2], pool_scale=sm[:, 3], sgu_ln_g=sm[:, 4],
        sgu_ln_b=sm[:, 5], conv_b_w=sm[:, ROW_CBW:ROW_CBW + KB], conv_a_w=sm[:, ROW_CAW:ROW_CAW + KA],
        pool_w=jnp.transpose(sm[:, ROW_PW:ROW_PW + HEAD].reshape(L, HEAD, 4, HEAD), (0, 2, 1, 3)),
        ln_g=sm[:, ROW_LNG:ROW_LNG + 4].reshape(L, D_MODEL), ln_b=sm[:, ROW_LNB:ROW_LNB + 4].reshape(L, D_MODEL),
        b_out=sm[:, ROW_BOUT:ROW_BOUT + 4].reshape(L, D_MODEL),
        b_in=sm[:, ROW_BIN:ROW_BIN + N_SLICES].reshape(L, IN_WIDTH),
        sgu_w=jnp.transpose(owc.reshape(L, SGU_BLOCK, 4, SGU_BLOCK), (0, 2, 1, 3)),
        sgu_bias=sm[:, ROW_SB:ROW_SB + 4, 0:SGU_BLOCK])


def _step(p, m, v, x, target, *, tile_f, tile_b, k_steps):
    L = p["ln_g"].shape[0]
    xi, yi, ci = _place()
    me_k = 2 * xi + yi
    hi_rows, ho_rows = D_MODEL // 2, GROUP // 2

    cw = jnp.concatenate([p["conv_a_w"], p["conv_b_w"]], axis=1).reshape(-1, 128)
    cw_rows = cw.shape[0]
    cw = _pad_rows(cw, -(-cw_rows // SUBLANES) * SUBLANES)
    wi16 = p["w_in"].astype(BF16)
    wo16 = p["w_out"].astype(BF16)
    wig0, wog0, cwg = _gather_weights(wi16[0:1], wo16[0:1], cw)
    cwg = cwg[:, :cw_rows].reshape(N_CHIPS, L, KA + KB, HEAD)
    conv_full = jnp.transpose(cwg, (1, 2, 0, 3)).reshape(L, KA + KB, GROUP)
    seg, e4 = _indicator_consts()
    k = _layer_consts(p, conv_full)
    layer = [jnp.full((1,), l, jnp.int32) for l in range(L)]

    hcur = x
    saved, wig, wog = [], [wig0[0]], [wog0[0]]
    for l in range(L):
        nxt = (wi16, wo16) if l + 1 < L else None
        outs = _fwd_layer(layer[l], hcur, wig[l], k["bin"], k["caw"], k["cbw"], k["s256"], seg, k["pw"], k["wm"], k["sb"],
                          wog[l], k["v1024"], tile=tile_f, nxt=nxt, target=None if nxt is not None else target)
        y, xb, h, aux, mixb, z = outs[0:6]
        if nxt is not None:
            wig.append(outs[6])
            wog.append(outs[7])
        saved.append((xb, h, aux, mixb, z))
        hcur = y

    dy = hcur
    loss_local = outs[6][0, 0]

    gwi = lax.empty((L, N_CHIPS, D_MODEL, COLS), F32)
    gwo = lax.empty((L, N_CHIPS, GROUP, D_MODEL), F32)
    gwi16 = lax.empty((L, N_CHIPS, D_MODEL, COLS), BF16)
    gwo16 = lax.empty((L, N_CHIPS, GROUP, D_MODEL), BF16)
    p_i = lax.empty((L, N_CHIPS, hi_rows, COLS), BF16)
    p_o = lax.empty((L, N_CHIPS, ho_rows, D_MODEL), BF16)
    q_i = lax.empty((3, L, hi_rows, COLS), BF16)
    q_o = lax.empty((3, L, ho_rows, D_MODEL), BF16)
    r_sm = [None] * L
    pending = None
    for l in reversed(range(L)):
        xb, h, aux, mixb, z = saved[l]
        exch = None if pending is None else (p_i, p_o, pending, q_i, q_o)
        outs = _bwd_layer(layer[l], dy, z, h, aux, wig[l], k["caw"], k["cbw"], k["s256"], seg, k["pw"], k["wm"],
                          k["wmt"], k["sb"], wog[l], k["v1024"], e4, tile=tile_b, exch=exch)
        dy, dhb, dzb, osm = outs[0:4]
        if l == L - 1:
            osm = osm.at[ROW_LOSS, 0].set(loss_local)
        if exch is not None:
            q_i, q_o, r_sm[l + 1] = outs[4:7]
        cl_arr = jnp.stack([ci, jnp.int32(l)]).astype(jnp.int32)
        if l > 0:
            p_i, p_o = _dw_swap(cl_arr, xb, dhb, mixb, dzb, p_i, p_o, k_steps=k_steps)
        else:
            outs = _dw(layer[l], xb, dhb, mixb, dzb, gwi, gwi16, gwo, gwo16, k_steps=k_steps, small=osm)
            gwi, gwi16, gwo, gwo16, r_sm[0] = outs
            p_i, p_o, q_i, q_o = _swap_add(cl_arr, gwi, gwi16, p_i, gwo, gwo16, p_o, send_on=(q_i, q_o))
        pending = osm
    grad_x = dy

    summed = _sum_small(r_sm)
    loss = summed[L - 1, ROW_LOSS, 0]
    grads = _unpack_small(summed)
    for n in ("conv_a_w", "conv_b_w"):
        grads[n] = lax.dynamic_slice_in_dim(grads[n], me_k * HEAD, HEAD, axis=2)

    kc_arr = jnp.stack([me_k, ci]).astype(jnp.int32)
    g_i, g_o = _sum_share(kc_arr, p_i, q_i, p_o, q_o, nb=2)
    grads["w_in"] = g_i
    grads["w_out"] = g_o

    delta, new_m, new_v = {}, {}, {}
    for n, tr in (("w_in", 512), ("w_out", 256)):
        shp = p[n].shape
        args = [a.reshape(shp[0] * shp[1], shp[2]) for a in (p[n], grads[n], m[n], v[n])]
        outs = _adamw(*args, rows_per_step=tr, name="adamw_" + n, copy_g=True)
        delta[n], new_m[n], new_v[n], grads[n] = (a.reshape(shp) for a in outs)
    small = [n for n in WEIGHTS if n not in ("w_in", "w_out")]
    flat = [[a[n].reshape(-1, a[n].shape[-1]) for n in small] for a in (p, grads, m, v)]
    outs = _adamw_small(*flat)
    for j, n in enumerate(small):
        delta[n], new_m[n], new_v[n] = (o[j].reshape(p[n].shape) for o in outs)

    return (loss, grad_x[None], *[grads[n] for n in WEIGHTS], *[delta[n] for n in WEIGHTS],
            *[new_m[n] for n in WEIGHTS], *[new_v[n] for n in WEIGHTS])


def kernel(x, ln_g, ln_b, w_in, b_in, conv_a_w, conv_a_b, norm_a_g, norm_a_b, conv_b_w, pool_w, pool_scale, sgu_ln_g, sgu_ln_b, sgu_w, sgu_bias, w_out, b_out, loss_target, m_ln_g, m_ln_b, m_w_in, m_b_in, m_conv_a_w, m_conv_a_b, m_norm_a_g, m_norm_a_b, m_conv_b_w, m_pool_w, m_pool_scale, m_sgu_ln_g, m_sgu_ln_b, m_sgu_w, m_sgu_bias, m_w_out, m_b_out, v_ln_g, v_ln_b, v_w_in, v_b_in, v_conv_a_w, v_conv_a_b, v_norm_a_g, v_norm_a_b, v_conv_b_w, v_pool_w, v_pool_scale, v_sgu_ln_g, v_sgu_ln_b, v_sgu_w, v_sgu_bias, v_w_out, v_b_out):
    p = dict(ln_g=ln_g, ln_b=ln_b, w_in=w_in, b_in=b_in, conv_a_w=conv_a_w, conv_a_b=conv_a_b, norm_a_g=norm_a_g,
             norm_a_b=norm_a_b, conv_b_w=conv_b_w, pool_w=pool_w, pool_scale=pool_scale, sgu_ln_g=sgu_ln_g,
             sgu_ln_b=sgu_ln_b, sgu_w=sgu_w, sgu_bias=sgu_bias, w_out=w_out, b_out=b_out)
    m = dict(ln_g=m_ln_g, ln_b=m_ln_b, w_in=m_w_in, b_in=m_b_in, conv_a_w=m_conv_a_w, conv_a_b=m_conv_a_b,
             norm_a_g=m_norm_a_g, norm_a_b=m_norm_a_b, conv_b_w=m_conv_b_w, pool_w=m_pool_w, pool_scale=m_pool_scale,
             sgu_ln_g=m_sgu_ln_g, sgu_ln_b=m_sgu_ln_b, sgu_w=m_sgu_w, sgu_bias=m_sgu_bias, w_out=m_w_out, b_out=m_b_out)
    v = dict(ln_g=v_ln_g, ln_b=v_ln_b, w_in=v_w_in, b_in=v_b_in, conv_a_w=v_conv_a_w, conv_a_b=v_conv_a_b,
             norm_a_g=v_norm_a_g, norm_a_b=v_norm_a_b, conv_b_w=v_conv_b_w, pool_w=v_pool_w, pool_scale=v_pool_scale,
             sgu_ln_g=v_sgu_ln_g, sgu_ln_b=v_sgu_ln_b, sgu_w=v_sgu_w, sgu_bias=v_sgu_bias, w_out=v_w_out, b_out=v_b_out)
    return _step(p, m, v, x[0], loss_target[0], tile_f=256, tile_b=256, k_steps=4)
```

```python
import jax
import jax.numpy as jnp
from jax import lax
from jax.experimental import pallas as pl
from jax.experimental.pallas import tpu as pltpu

F32 = jnp.float32
BF16 = jnp.bfloat16
MESH = pl.DeviceIdType.MESH

D_MODEL = 1024
GROUP = 256
HEAD = 64
N_SLICES = 12
IN_WIDTH = N_SLICES * GROUP
N_CHIPS = 4
COLS = IN_WIDTH // N_CHIPS
KA = 31
KB = 3
SUBLANES = 8
HALO_A, HALO_B, HALO_C = 32, 8, 16
N_GATHER_SEMS = 12
N_EXCH_SEMS = 13
SGU_BLOCK = 128
CHUNK = 64
LN_EPS = 1e-5
ROWS = 64
V7X_VMEM_BYTES = 64 * 1024 * 1024
VMEM_LIMIT = V7X_VMEM_BYTES - 8 * 1024 * 1024

ADAM_LR, ADAM_B1, ADAM_B2, ADAM_EPS, ADAM_WD, ADAM_STEP = 0.001, 0.9, 0.999, 1e-08, 0.01, 10


ANY = pl.BlockSpec(memory_space=pl.ANY)


def _vmem_params(**kw):
    return pltpu.CompilerParams(vmem_limit_bytes=VMEM_LIMIT, **kw)


def _whole(a):
    return pl.BlockSpec(a.shape, lambda i, l, _n=a.ndim: (0,) * _n)


def _of_layer(a):
    return pl.BlockSpec((None,) + a.shape[1:], lambda i, l, _n=a.ndim: (l[0],) + (0,) * (_n - 1))


def _place():
    return lax.axis_index("x"), lax.axis_index("y"), lax.axis_index("c")


def _other_chips(x, y):
    return [(1 - x, y, 2 * (1 - x) + y), (x, 1 - y, 2 * x + (1 - y)), (1 - x, 1 - y, 2 * (1 - x) + (1 - y))]


PEERS_SIBLING, PEERS_COLUMN, PEERS_ALL = "sibling", "sibling and the same core of the other chips", "all"
COLLECTIVE_ID = dict(sum_share=0, swap_add=1, gather_weights=2, fwd_layer_gather=3, swap_add_send=4,
                     bwd_layer_exchange=5, dw_exchange=6, dw_swap=7)


def _handshake(peers):
    x, y, c = _place()
    if peers == PEERS_SIBLING:
        ids = [(x, y, 1 - c)]
    elif peers == PEERS_COLUMN:
        ids = [(x, y, 1 - c)] + [(px, py, c) for px, py, _ in _other_chips(x, y)]
    else:
        ids = [(1 - x if r & 4 else x, 1 - y if r & 2 else y, 1 - c if r & 1 else c) for r in range(1, 8)]
    barrier = pltpu.get_barrier_semaphore()
    for to in ids:
        pl.semaphore_signal(barrier, inc=1, device_id=to, device_id_type=MESH)
    pl.semaphore_wait(barrier, len(ids))


def _sig(v):
    return 0.5 * jnp.tanh(0.5 * v) + 0.5


def _dot(a, b):
    return jnp.dot(a, b, preferred_element_type=F32)


def _dot_nt(a, b):
    return lax.dot_general(a, b, (((1,), (1,)), ((), ())), preferred_element_type=F32)


def _dot_tn(a, b):
    return lax.dot_general(a, b, (((0,), (0,)), ((), ())), preferred_element_type=F32)


def _segdot(v, m):
    hi = v.astype(BF16)
    lo = (v - hi.astype(F32)).astype(BF16)
    return _dot(hi, m) + _dot(lo, m)


def _colsum(v):
    return jnp.sum(v, axis=0, keepdims=True)


def _rowmean(v):
    return jnp.mean(v, axis=-1, keepdims=True)


def _lane_group(n):
    return lax.broadcasted_iota(jnp.int32, (1, n), 1) // HEAD


def _pool_cnt(tile, t_rows):
    pos = tile * t_rows + lax.broadcasted_iota(jnp.int32, (t_rows, GROUP), 0) + 1
    grp = lax.broadcasted_iota(jnp.int32, (t_rows, GROUP), 1) // HEAD
    win = jnp.where(grp == 0, 2, jnp.where(grp == 1, 4, jnp.where(grp == 2, 8, 16)))
    return jnp.minimum(pos, win).astype(F32)


def _sgu_masks(wm_ref, wmt_ref, wm_s, wmt_s):
    r = lax.broadcasted_iota(jnp.int32, (SGU_BLOCK, 4 * SGU_BLOCK), 0) // CHUNK
    c = (lax.broadcasted_iota(jnp.int32, (SGU_BLOCK, 4 * SGU_BLOCK), 1) % SGU_BLOCK) // CHUNK
    wm_s[...] = jnp.where(c <= r, wm_ref[...], 0.0).astype(BF16)
    if wmt_ref is not None:
        rt = (lax.broadcasted_iota(jnp.int32, (4 * SGU_BLOCK, SGU_BLOCK), 0) % SGU_BLOCK) // CHUNK
        ct = lax.broadcasted_iota(jnp.int32, (4 * SGU_BLOCK, SGU_BLOCK), 1) // CHUNK
        wmt_s[...] = jnp.where(rt <= ct, wmt_ref[...], 0.0).astype(BF16)


def _vstack(v_blk):
    grp = _lane_group(GROUP)
    return jnp.concatenate([jnp.where(grp == h, v_blk, 0.0) for h in range(4)], axis=0).astype(BF16)


def _gather_next(step, nt, nwi, nwo, gwi, gwo, send_sems, recv_sems, loc_sems, vwi, vwo):
    x, y, c = _place()
    me_k = 2 * x + y
    sibling = (x, y, 1 - c)
    chips = _other_chips(x, y)
    hi, ho = D_MODEL // 2, GROUP // 2
    fwd_sems = N_GATHER_SEMS // 2

    def rc(src, dst, sem, to):
        return pltpu.make_async_remote_copy(src_ref=src, dst_ref=dst, send_sem=send_sems.at[sem],
                                            recv_sem=recv_sems.at[sem], device_id=to, device_id_type=MESH)

    def blk(ref, k, n, cc):
        return ref.at[k, pl.ds(cc * n, n), :]

    def ici(r):
        px, py, _ = chips[r]
        to = (px, py, c)
        return [rc(nwi.at[pl.ds(c * hi, hi), :], blk(gwi, me_k, hi, c), 2 * r, to),
                rc(nwo.at[pl.ds(c * ho, ho), :], blk(gwo, me_k, ho, c), 2 * r + 1, to)]

    def landed(r, cc, base):
        pk = chips[r][2]
        return [rc(blk(gwi, pk, hi, cc), blk(gwi, pk, hi, cc), base + 2 * r, sibling),
                rc(blk(gwo, pk, ho, cc), blk(gwo, pk, ho, cc), base + 2 * r + 1, sibling)]

    def stage_in():
        return [pltpu.make_async_copy(nwi, vwi, loc_sems.at[0]), pltpu.make_async_copy(nwo, vwo, loc_sems.at[1])]

    def local():
        return [pltpu.make_async_copy(vwi, gwi.at[me_k], loc_sems.at[2]),
                pltpu.make_async_copy(vwo, gwo.at[me_k], loc_sems.at[3])]

    @pl.when(step == 0)
    def _():
        _handshake(PEERS_COLUMN)
        for cp in stage_in():
            cp.start()
        for r in range(3):
            for cp in ici(r):
                cp.start()

    @pl.when(step == 1)
    def _():
        for cp in stage_in():
            cp.wait()
        for cp in local():
            cp.start()

    @pl.when(step == (3 * nt) // 4)
    def _():
        for r in range(3):
            for got, fwd in zip(landed(r, c, 0), landed(r, c, fwd_sems)):
                got.wait_recv()
                fwd.start()

    @pl.when(step == nt - 1)
    def _():
        for r in range(3):
            for got in landed(r, 1 - c, fwd_sems):
                got.wait_recv()
        for r in range(3):
            for cp in ici(r) + landed(r, c, fwd_sems):
                cp.wait_send()
        for cp in local():
            cp.wait()


def _fwd_layer(larr, x, wi, bin_, caw, cbw, s256, seg, pw, wm, sb, wo, v1024, *, tile, nxt=None, target=None):
    assert nxt is None or target is None
    S = x.shape[0]
    T = tile
    nt = S // T
    alpha = float((2.0 * 4) ** 0.25)
    n_in = 13 + (2 if nxt is not None else 0) + (1 if target is not None else 0)
    n_out = 6 + (2 if nxt is not None else 0) + (1 if target is not None else 0)

    def body(*refs):
        l_ref = refs[0]
        (x_ref, wi_ref, bin_ref, caw_ref, cbw_ref, s256_ref, seg_ref, pw_ref, wm_ref, sb_ref, wo_ref,
         v1024_ref) = refs[1:13]
        y_ref, xb_ref, h_ref, aux_ref, mix_ref, z_ref = refs[n_in:n_in + 6]
        abuf, bbuf, cbuf, wm_s, shf = refs[n_in + n_out:n_in + n_out + 5]
        i = pl.program_id(0)
        if nxt is not None:
            _gather_next(i, nt, refs[13].at[l_ref[0] + 1], refs[14].at[l_ref[0] + 1], refs[n_in + 6], refs[n_in + 7],
                         *refs[n_in + n_out + 5:])

        @pl.when(i == 0)
        def _():
            abuf[0:HALO_A, :] = jnp.zeros((HALO_A, GROUP), F32)
            bbuf[0:HALO_B, :] = jnp.zeros((HALO_B, GROUP), F32)
            cbuf[0:HALO_C, :] = jnp.zeros((HALO_C, GROUP), F32)
            _sgu_masks(wm_ref, None, wm_s, None)

        x = x_ref[...]
        xb = x.astype(BF16)
        xb_ref[...] = xb
        for k in range(N_CHIPS):
            h_ref[:, COLS * k:COLS * (k + 1)] = _dot(xb, wi_ref[k]) + bin_ref[:, COLS * k:COLS * (k + 1)]

        def hs(j):
            return h_ref[:, GROUP * j:GROUP * (j + 1)]

        abuf[HALO_A:HALO_A + T, :] = hs(0) * _sig(hs(1))
        span = T + HALO_A - SUBLANES
        for p in range(1, SUBLANES):
            shf[p - 1, :, :] = abuf[p:p + span, :]
        for r0 in range(0, T, ROWS):
            acc = None
            for k in range(KA):
                off = HALO_A - (KA - 1) + k
                p, q8 = off % SUBLANES, off - off % SUBLANES
                win = abuf[r0 + q8:r0 + q8 + ROWS, :] if p == 0 else shf[p - 1, r0 + q8:r0 + q8 + ROWS, :]
                term = caw_ref[k:k + 1, :] * win
                acc = term if acc is None else acc + term
            aux_ref[r0:r0 + ROWS, 0:GROUP] = acc + s256_ref[0:1, :]
        abuf[0:HALO_A, :] = abuf[T:T + HALO_A, :]
        a1 = aux_ref[:, 0:GROUP]
        segm = seg_ref[...]
        cen = a1 - _segdot(a1, segm)
        var = _segdot(cen * cen, segm)
        a2 = cen * lax.rsqrt(var + LN_EPS) * s256_ref[1:2, :] + s256_ref[2:3, :]
        az = hs(2)
        mix_ref[:, 0:GROUP] = (a2 * _sig(a2) * (az * _sig(az))).astype(BF16)

        bbuf[HALO_B:HALO_B + T, :] = hs(4) * hs(5)
        for r0 in range(0, T, ROWS):
            acc = None
            for k in range(KB):
                off = HALO_B - (KB - 1) + k + r0
                term = cbw_ref[k:k + 1, :] * bbuf[off:off + ROWS, :]
                acc = term if acc is None else acc + term
            aux_ref[r0:r0 + ROWS, GROUP:2 * GROUP] = acc
        bbuf[0:HALO_B, :] = bbuf[T:T + HALO_B, :]
        bz = hs(6)
        mix_ref[:, GROUP:2 * GROUP] = (hs(3) * aux_ref[:, GROUP:2 * GROUP] * (bz * _sig(bz))).astype(BF16)

        ch = hs(7)
        cbuf[HALO_C:HALO_C + T, :] = ch
        hi_lane = (lax.broadcasted_iota(jnp.int32, (1, 128), 1) // HEAD) == 1
        for r0 in range(0, T, ROWS):
            def win(col, j0, j1):
                s = None
                for j in range(j0, j1):
                    off = HALO_C - j + r0
                    term = cbuf[off:off + ROWS, 128 * col:128 * (col + 1)]
                    s = term if s is None else s + term
                return s
            w0 = win(0, 0, 2) + jnp.where(hi_lane, win(0, 2, 4), 0.0)
            w1 = win(1, 0, 8) + jnp.where(hi_lane, win(1, 8, 16), 0.0)
            aux_ref[r0:r0 + ROWS, 2 * GROUP:2 * GROUP + 128] = w0
            aux_ref[r0:r0 + ROWS, 2 * GROUP + 128:3 * GROUP] = w1
        cbuf[0:HALO_C, :] = cbuf[T:T + HALO_C, :]
        pooled = aux_ref[:, 2 * GROUP:3 * GROUP] / _pool_cnt(i, T) - ch
        aux_ref[:, 2 * GROUP:3 * GROUP] = pooled
        q = _dot(pooled.astype(BF16), pw_ref[...])
        cz = hs(8)
        mix_ref[:, 2 * GROUP:3 * GROUP] = (q * s256_ref[3:4, :] * (cz * _sig(cz))).astype(BF16)

        dv = hs(10)
        cen = dv - _rowmean(dv)
        var = _rowmean(cen * cen)
        v = cen * lax.rsqrt(var + LN_EPS) * s256_ref[4:5, :] + s256_ref[5:6, :]
        sps = []
        for n in range(T // SGU_BLOCK):
            vb = v[n * SGU_BLOCK:(n + 1) * SGU_BLOCK, :]
            sps.append(_dot(wm_s[...], _vstack(vb)) + sb_ref[...])
        sp = jnp.concatenate(sps, axis=0)
        dz = hs(11)
        mix_ref[:, 3 * GROUP:4 * GROUP] = (hs(9) * sp * (dz * _sig(dz))).astype(BF16)

        out = v1024_ref[0:1, :]
        for k in range(N_CHIPS):
            out = out + _dot(mix_ref[:, GROUP * k:GROUP * (k + 1)], wo_ref[k])
        z = alpha * x + out
        z_ref[...] = z
        cen = z - _rowmean(z)
        var = _rowmean(cen * cen)
        y = cen * lax.rsqrt(var + LN_EPS) * v1024_ref[1:2, :] + v1024_ref[2:3, :]
        if target is None:
            y_ref[...] = y
        else:
            t_ref, loss_ref = refs[13], refs[n_in + 6]

            @pl.when(i == 0)
            def _():
                loss_ref[...] = jnp.zeros_like(loss_ref)
            err = y - t_ref[...]
            y_ref[...] = err * (1.0 / D_MODEL)
            loss_ref[...] += jnp.sum(_colsum(err * err), axis=1, keepdims=True) * (0.5 / D_MODEL)

    def rows(width):
        return pl.BlockSpec((T, width), lambda i, l: (i, 0))

    consts = (wi, bin_, caw, cbw, s256, seg, pw, wm, sb, wo, v1024)
    in_specs = [rows(D_MODEL)] + [_whole(a) if a is wi or a is seg or a is wo else _of_layer(a) for a in consts]
    out_specs = [rows(D_MODEL), rows(D_MODEL), rows(IN_WIDTH), rows(3 * GROUP), rows(D_MODEL), rows(D_MODEL)]
    out_shape = [jax.ShapeDtypeStruct((S, D_MODEL), F32), jax.ShapeDtypeStruct((S, D_MODEL), BF16),
                 jax.ShapeDtypeStruct((S, IN_WIDTH), F32), jax.ShapeDtypeStruct((S, 3 * GROUP), F32),
                 jax.ShapeDtypeStruct((S, D_MODEL), BF16), jax.ShapeDtypeStruct((S, D_MODEL), F32)]
    scratch = [pltpu.VMEM((T + HALO_A, GROUP), F32), pltpu.VMEM((T + HALO_B, GROUP), F32),
               pltpu.VMEM((T + HALO_C, GROUP), F32), pltpu.VMEM((SGU_BLOCK, 4 * SGU_BLOCK), BF16),
               pltpu.VMEM((SUBLANES - 1, T + HALO_A - SUBLANES, GROUP), F32)]
    extra = ()
    if nxt is not None:
        extra = tuple(nxt)
        in_specs += [ANY, ANY]
        out_specs += [ANY, ANY]
        out_shape += [jax.ShapeDtypeStruct((N_CHIPS, D_MODEL, COLS), BF16),
                      jax.ShapeDtypeStruct((N_CHIPS, GROUP, D_MODEL), BF16)]
        scratch += [pltpu.SemaphoreType.DMA((N_GATHER_SEMS,)), pltpu.SemaphoreType.DMA((N_GATHER_SEMS,)),
                    pltpu.SemaphoreType.DMA((4,)), pltpu.VMEM((D_MODEL, COLS), BF16), pltpu.VMEM((GROUP, D_MODEL), BF16)]
    if target is not None:
        extra = (target,)
        in_specs += [rows(D_MODEL)]
        out_specs += [pl.BlockSpec((8, 128), lambda i, l: (0, 0))]
        out_shape += [jax.ShapeDtypeStruct((8, 128), F32)]
    grid_spec = pltpu.PrefetchScalarGridSpec(num_scalar_prefetch=1, grid=(nt,), in_specs=in_specs,
                                             out_specs=out_specs, scratch_shapes=scratch)
    return pl.pallas_call(
        body, name=("fwd_layer_loss" if target is not None else "fwd_layer") if nxt is None else "fwd_layer_gather",
        grid_spec=grid_spec, out_shape=out_shape,
        compiler_params=_vmem_params(dimension_semantics=("arbitrary",), **(
            dict(has_side_effects=True, collective_id=COLLECTIVE_ID["fwd_layer_gather"]) if nxt is not None else {})),
    )(larr, x, *consts, *extra)


ROW_CBW = 8
ROW_CAW = 16
ROW_LOSS = 7
ROW_PW = 48
ROW_LNG = 112
ROW_LNB = 116
ROW_BOUT = 120
ROW_BIN = 124
ROW_WC = 136
ROW_SB = 392
SM_ROWS = 400
N_DEV = 8


def _exchange_comm(start, finish, l, p_i, p_o, sm, r_i, r_o, r_sm, send_sems, recv_sems, loc_sem):
    x, y, c = _place()
    me = 4 * x + 2 * y + c
    chips = _other_chips(x, y)

    def rc(src, dst, sem, to):
        return pltpu.make_async_remote_copy(src_ref=src, dst_ref=dst, send_sem=send_sems.at[sem],
                                            recv_sem=recv_sems.at[sem], device_id=to, device_id_type=MESH)

    def big(r):
        px, py, pk = chips[r]
        to = (px, py, c)
        return [rc(p_i.at[l, pk], r_i.at[r, l], 2 * r, to), rc(p_o.at[l, pk], r_o.at[r, l], 2 * r + 1, to)]

    def peer(rel):
        px = 1 - x if rel & 4 else x
        py = 1 - y if rel & 2 else y
        pc = 1 - c if rel & 1 else c
        return (px, py, pc), 4 * px + 2 * py + pc

    def small_out(rel):
        to, _ = peer(rel)
        return rc(sm, r_sm.at[me], N_EXCH_SEMS - N_DEV + rel, to)

    def small_in(rel):
        to, idx = peer(rel)
        return rc(sm, r_sm.at[idx], N_EXCH_SEMS - N_DEV + rel, to)

    def local():
        return pltpu.make_async_copy(sm, r_sm.at[me], loc_sem.at[0])

    with_big, with_small = p_i is not None, sm is not None

    @pl.when(start)
    def _():
        _handshake(PEERS_ALL)
        if with_small:
            local().start()
        if with_big:
            for r in range(3):
                for cp in big(r):
                    cp.start()
        if with_small:
            for rel in range(1, N_DEV):
                small_out(rel).start()

    @pl.when(finish)
    def _():
        if with_big:
            for r in range(3):
                for cp in big(r):
                    cp.wait()
        if with_small:
            for rel in range(1, N_DEV):
                small_in(rel).wait_recv()
                small_out(rel).wait_send()
            local().wait()


RC = 32
RC_WIDE = 16
ACC_ROWS = 136


def _rsum8(v):
    r = v[0:8]
    for j in range(1, v.shape[0] // 8):
        r = r + v[8 * j:8 * j + 8]
    return r


def _bwd_layer(larr, dy, z, h, aux, wi, caw, cbw, s256, seg, pw, wm, wmt, sb, wo, v1024, e4, *, tile, exch=None):
    S = dy.shape[0]
    T = tile
    nt = S // T
    nblk = T // SGU_BLOCK
    alpha = float((2.0 * 4) ** 0.25)
    n_in = 17 + (5 if exch is not None else 0)
    n_out = 4 + (3 if exch is not None else 0)
    slab = pltpu.VMEM((T, GROUP), F32)
    scratch = dict(
        dbuf=pltpu.VMEM((T + HALO_A, GROUP), F32), ebuf=pltpu.VMEM((T + HALO_B, GROUP), F32),
        fbuf=pltpu.VMEM((T + HALO_C, GROUP), F32), sh=pltpu.VMEM((SUBLANES - 1, T + HALO_A - SUBLANES, GROUP), F32),
        wm_s=pltpu.VMEM((SGU_BLOCK, 4 * SGU_BLOCK), BF16), wmt_s=pltpu.VMEM((4 * SGU_BLOCK, SGU_BLOCK), BF16),
        dsp_acc=pltpu.VMEM((SGU_BLOCK, GROUP), F32), pw_acc=pltpu.VMEM((GROUP, GROUP), F32),
        acc_s=pltpu.VMEM((8 * ACC_ROWS, GROUP), F32), acc_w=pltpu.VMEM((24, D_MODEL), F32),
        dmix_s=pltpu.VMEM((T, D_MODEL), F32), vst_s=pltpu.VMEM((nblk, 4 * SGU_BLOCK, GROUP), BF16),
        dq_s=pltpu.VMEM((T, GROUP), BF16), dxt_s=pltpu.VMEM((D_MODEL, T), F32),
        mean_s=slab, t1_s=slab, t2_s=slab, q_s=slab, xv_s=slab, rv_s=slab, v_s=slab, sp_s=slab, a0_s=slab, sg_s=slab,
        xh_s=slab, ra_s=slab, ub_s=slab, dsp_s=slab, m1_s=slab, m2_s=slab, dpool_s=slab, dvd_s=slab, u_s=slab,
        du_s=slab, cw_s=slab)
    names = list(scratch)

    def body(*refs):
        (dy_ref, z_ref, h_ref, aux_ref, wi_ref, caw_ref, cbw_ref, s256_ref, seg_ref, pw_ref, wm_ref, wmt_ref,
         sb_ref, wo_ref, v1024_ref, e4_ref) = refs[1:17]
        dx_ref, dhb_ref, dzb_ref, osm_ref = refs[n_in:n_in + 4]
        k0 = n_in + n_out
        sc = dict(zip(names, refs[k0:k0 + len(names)]))
        dbuf, ebuf, fbuf, sh = sc["dbuf"], sc["ebuf"], sc["fbuf"], sc["sh"]
        wm_s, wmt_s, dsp_acc, pw_acc, acc_s, acc_w = (sc[n] for n in ("wm_s", "wmt_s", "dsp_acc", "pw_acc", "acc_s",
                                                                        "acc_w"))
        dmix_s, vst_s, dq_s = sc["dmix_s"], sc["vst_s"], sc["dq_s"]
        i = pl.program_id(0)
        tile_idx = nt - 1 - i
        if exch is not None:
            p_i, p_o, sm = refs[17:20]
            r_i, r_o, r_sm = refs[n_in + 4:n_in + 7]
            _exchange_comm(i == 0, i == nt - 1, refs[0][0] + 1, p_i, p_o, sm, r_i, r_o, r_sm, *refs[k0 + len(names):])

        @pl.when(i == 0)
        def _():
            dbuf[T:T + HALO_A, :] = jnp.zeros((HALO_A, GROUP), F32)
            ebuf[T:T + HALO_B, :] = jnp.zeros((HALO_B, GROUP), F32)
            fbuf[T:T + HALO_C, :] = jnp.zeros((HALO_C, GROUP), F32)
            _sgu_masks(wm_ref, wmt_ref, wm_s, wmt_s)
            osm_ref[...] = jnp.zeros_like(osm_ref)
            dsp_acc[...] = jnp.zeros_like(dsp_acc)
            pw_acc[...] = jnp.zeros_like(pw_acc)
            acc_s[...] = jnp.zeros_like(acc_s)
            acc_w[...] = jnp.zeros_like(acc_w)

        def chunks(rc, fn):
            for c in range(T // rc):
                fn(pl.ds(c * rc, rc))

        def hs(j, rows):
            return h_ref[rows, GROUP * j:GROUP * (j + 1)]

        def acc_add(row, val):
            acc_s[8 * row:8 * row + 8, :] += _rsum8(val)

        def put_dh(j, rows, val):
            acc_add(ROW_BIN + j, val)
            dhb_ref[rows, GROUP * j:GROUP * (j + 1)] = val.astype(BF16)

        def dsilu(v, s):
            return s * (1.0 + v * (1.0 - s))

        def vec(r):
            return s256_ref[r:r + 1, :]

        def ln_bwd(rows):
            dyc = dy_ref[rows, :]
            zc = z_ref[rows, :]
            cen = zc - _rowmean(zc)
            rstd = lax.rsqrt(_rowmean(cen * cen) + LN_EPS)
            xhat = cen * rstd
            acc_w[0:8, :] += _rsum8(dyc * xhat)
            acc_w[8:16, :] += _rsum8(dyc)
            gdy = dyc * v1024_ref[1:2, :]
            dz = rstd * (gdy - _rowmean(gdy) - xhat * _rowmean(gdy * xhat))
            acc_w[16:24, :] += _rsum8(dz)
            dzb_ref[rows, :] = dz.astype(BF16)
            dx_ref[rows, :] = alpha * dz
        chunks(RC_WIDE, ln_bwd)

        segm = seg_ref[...]
        dzb = dzb_ref[...]
        for k in range(N_CHIPS):
            dmix_s[:, GROUP * k:GROUP * (k + 1)] = _dot_nt(dzb, wo_ref[k])
        sc["mean_s"][...] = _segdot(aux_ref[:, 0:GROUP], segm)
        pooled_b = aux_ref[:, 2 * GROUP:3 * GROUP].astype(BF16)
        sc["q_s"][...] = _dot(pooled_b, pw_ref[...])

        def centre(rows):
            cen = aux_ref[rows, 0:GROUP] - sc["mean_s"][rows, :]
            sc["t1_s"][rows, :] = cen * cen
            dv_in = hs(10, rows)
            cen_v = dv_in - _rowmean(dv_in)
            rstd_v = lax.rsqrt(_rowmean(cen_v * cen_v) + LN_EPS)
            xv = cen_v * rstd_v
            sc["xv_s"][rows, :] = xv
            sc["rv_s"][rows, :] = jnp.broadcast_to(rstd_v, xv.shape)
            sc["v_s"][rows, :] = xv * vec(4) + vec(5)
        chunks(RC, centre)

        sc["t2_s"][...] = _segdot(sc["t1_s"][...], segm)
        for n in range(nblk):
            blk = slice(n * SGU_BLOCK, (n + 1) * SGU_BLOCK)
            vst_s[n] = _vstack(sc["v_s"][blk, :])
            sc["sp_s"][blk, :] = _dot(wm_s[...], vst_s[n]) + sb_ref[...]

        def mixers(rows):
            a_val, a_glu, a_z = hs(0, rows), hs(1, rows), hs(2, rows)
            sg = _sig(a_glu)
            sc["a0_s"][rows, :] = a_val * sg
            sc["sg_s"][rows, :] = sg
            rstd_a = lax.rsqrt(sc["t2_s"][rows, :] + LN_EPS)
            xh = (aux_ref[rows, 0:GROUP] - sc["mean_s"][rows, :]) * rstd_a
            a2 = xh * vec(1) + vec(2)
            s2 = _sig(a2)
            sz = _sig(a_z)
            dya = dmix_s[rows, 0:GROUP]
            put_dh(2, rows, dya * (a2 * s2) * dsilu(a_z, sz))
            d_a2 = dya * (a_z * sz) * dsilu(a2, s2)
            acc_add(1, d_a2 * xh)
            acc_add(2, d_a2)
            gd = d_a2 * vec(1)
            sc["t1_s"][rows, :] = gd
            sc["t2_s"][rows, :] = gd * xh
            sc["xh_s"][rows, :] = xh
            sc["ra_s"][rows, :] = rstd_a
            b_b, b_c, b_h, b_z = hs(3, rows), hs(4, rows), hs(5, rows), hs(6, rows)
            cb = aux_ref[rows, GROUP:2 * GROUP]
            sz = _sig(b_z)
            dyb = dmix_s[rows, GROUP:2 * GROUP]
            put_dh(3, rows, dyb * cb * (b_z * sz))
            put_dh(6, rows, dyb * b_b * cb * dsilu(b_z, sz))
            ebuf[rows, :] = dyb * b_b * (b_z * sz)
            sc["ub_s"][rows, :] = b_c * b_h
            c_z = hs(8, rows)
            q = sc["q_s"][rows, :]
            sz = _sig(c_z)
            dyc = dmix_s[rows, 2 * GROUP:3 * GROUP]
            acc_add(3, dyc * q * (c_z * sz))
            put_dh(8, rows, dyc * q * vec(3) * dsilu(c_z, sz))
            dq_s[rows, :] = (dyc * vec(3) * (c_z * sz)).astype(BF16)
            d_u, d_z = hs(9, rows), hs(11, rows)
            sp = sc["sp_s"][rows, :]
            sz = _sig(d_z)
            dyd = dmix_s[rows, 3 * GROUP:4 * GROUP]
            put_dh(9, rows, dyd * sp * (d_z * sz))
            put_dh(11, rows, dyd * d_u * sp * dsilu(d_z, sz))
            sc["dsp_s"][rows, :] = dyd * d_u * (d_z * sz)
        chunks(RC, mixers)

        sc["m1_s"][...] = _segdot(sc["t1_s"][...], segm)
        sc["m2_s"][...] = _segdot(sc["t2_s"][...], segm)
        d_q = dq_s[...]
        pw_acc[...] += _dot_tn(pooled_b, d_q)
        sc["dpool_s"][...] = _dot_nt(d_q, pw_ref[...])
        grp = _lane_group(GROUP)
        for n in range(nblk):
            blk = slice(n * SGU_BLOCK, (n + 1) * SGU_BLOCK)
            dspb = sc["dsp_s"][blk, :]
            dsp_acc[...] += dspb
            dspb16 = dspb.astype(BF16)
            dvst = _dot(wmt_s[...], dspb16)
            dvb = None
            for hh in range(4):
                part = jnp.where(grp == hh, dvst[hh * SGU_BLOCK:(hh + 1) * SGU_BLOCK, :], 0.0)
                dvb = part if dvb is None else dvb + part
            sc["dvd_s"][blk, :] = dvb
            dwc = _dot_nt(dspb16, vst_s[n])
            osm_ref[ROW_WC:ROW_WC + SGU_BLOCK, :] += dwc[:, 0:GROUP]
            osm_ref[ROW_WC + SGU_BLOCK:ROW_WC + 2 * SGU_BLOCK, :] += dwc[:, GROUP:2 * GROUP]

        def ln_sums(rows):
            xh = sc["xh_s"][rows, :]
            d_a1 = sc["ra_s"][rows, :] * (sc["t1_s"][rows, :] - sc["m1_s"][rows, :] - xh * sc["m2_s"][rows, :])
            acc_add(0, d_a1)
            dbuf[rows, :] = d_a1
            pos = tile_idx * T + rows.start + lax.broadcasted_iota(jnp.int32, (RC, GROUP), 0) + 1
            lane = lax.broadcasted_iota(jnp.int32, (RC, GROUP), 1) // HEAD
            win = jnp.where(lane == 0, 2, jnp.where(lane == 1, 4, jnp.where(lane == 2, 8, 16)))
            fbuf[rows, :] = sc["dpool_s"][rows, :] / jnp.minimum(pos, win).astype(F32)
            d_v = sc["dvd_s"][rows, :]
            xv = sc["xv_s"][rows, :]
            acc_add(4, d_v * xv)
            acc_add(5, d_v)
            gd = d_v * vec(4)
            put_dh(10, rows, sc["rv_s"][rows, :] * (gd - _rowmean(gd) - xv * _rowmean(gd * xv)))
        chunks(RC, ln_sums)

        span = T + HALO_A - SUBLANES
        for p in range(1, SUBLANES):
            sh[p - 1, :, :] = dbuf[p:p + span, :]

        for r0 in range(0, T, ROWS):
            uc = sc["ub_s"][r0:r0 + ROWS, :]
            acc = None
            for k in range(KB):
                off = (KB - 1) - k + r0
                w = ebuf[off:off + ROWS, :]
                term = cbw_ref[k:k + 1, :] * w
                acc = term if acc is None else acc + term
                acc_add(ROW_CBW + k, uc * w)
            sc["du_s"][r0:r0 + ROWS, :] = acc
        ebuf[T:T + HALO_B, :] = ebuf[0:HALO_B, :]

        hi_lane = (lax.broadcasted_iota(jnp.int32, (1, 128), 1) // HEAD) == 1
        for r0 in range(0, T, ROWS):
            def win(col, j0, j1):
                s = None
                for j in range(j0, j1):
                    term = fbuf[r0 + j:r0 + j + ROWS, 128 * col:128 * (col + 1)]
                    s = term if s is None else s + term
                return s
            sc["cw_s"][r0:r0 + ROWS, 0:128] = win(0, 0, 2) + jnp.where(hi_lane, win(0, 2, 4), 0.0)
            sc["cw_s"][r0:r0 + ROWS, 128:256] = win(1, 0, 8) + jnp.where(hi_lane, win(1, 8, 16), 0.0)
        fbuf[T:T + HALO_C, :] = fbuf[0:HALO_C, :]

        def rest_bc(rows):
            d_u = sc["du_s"][rows, :]
            put_dh(4, rows, d_u * hs(5, rows))
            put_dh(5, rows, d_u * hs(4, rows))
            put_dh(7, rows, sc["cw_s"][rows, :] - sc["dpool_s"][rows, :])
        chunks(RC, rest_bc)

        dxt_s = sc["dxt_s"]

        def dx_term(k):
            term = _dot_nt(wi_ref[k], dhb_ref[:, COLS * k:COLS * (k + 1)])
            if k == 1:
                dxt_s[...] = term
            else:
                dxt_s[...] += term

        def conv_a(rows):
            a0c = sc["a0_s"][rows, :]
            acc = None
            for k in range(KA):
                off = (KA - 1) - k
                p, q8 = off % SUBLANES, off - off % SUBLANES
                w = dbuf[pl.ds(rows.start + q8, RC), :] if p == 0 else sh[p - 1, pl.ds(rows.start + q8, RC), :]
                term = caw_ref[k:k + 1, :] * w
                acc = term if acc is None else acc + term
                acc_add(ROW_CAW + k, a0c * w)
            sc["u_s"][rows, :] = acc
        n_chunks = T // RC
        after = {(n_chunks * j) // 3: j + 1 for j in range(3)}
        for c in range(n_chunks):
            conv_a(pl.ds(c * RC, RC))
            if c in after:
                dx_term(after[c])
        dbuf[T:T + HALO_A, :] = dbuf[0:HALO_A, :]

        def rest_a(rows):
            d_a0 = sc["u_s"][rows, :]
            sg = sc["sg_s"][rows, :]
            put_dh(0, rows, d_a0 * sg)
            put_dh(1, rows, d_a0 * hs(0, rows) * sg * (1.0 - sg))
        chunks(RC, rest_a)
        dx_term(0)
        dx_ref[...] += dxt_s[...].T

        @pl.when(i == nt - 1)
        def _():
            for row in list(range(6)) + list(range(ROW_CBW, ROW_CBW + KB)) + list(range(ROW_CAW, ROW_CAW + KA)) + list(
                    range(ROW_BIN, ROW_BIN + N_SLICES)):
                osm_ref[row:row + 1, :] = _colsum(acc_s[8 * row:8 * row + 8, :])
            for j, row in enumerate((ROW_LNG, ROW_LNB, ROW_BOUT)):
                cs = _colsum(acc_w[8 * j:8 * j + 8, :])
                for q in range(D_MODEL // GROUP):
                    osm_ref[row + q:row + q + 1, :] = cs[:, GROUP * q:GROUP * (q + 1)]
            r = lax.broadcasted_iota(jnp.int32, (SGU_BLOCK, GROUP), 0) // CHUNK
            c = (lax.broadcasted_iota(jnp.int32, (SGU_BLOCK, GROUP), 1) % SGU_BLOCK) // CHUNK
            for half in range(2):
                rows_ = slice(ROW_WC + half * SGU_BLOCK, ROW_WC + (half + 1) * SGU_BLOCK)
                osm_ref[rows_, :] = jnp.where(c <= r, osm_ref[rows_, :], 0.0)
            sb_t = _segdot(dsp_acc[...], e4_ref[...]).T
            osm_ref[ROW_SB:ROW_SB + 8, 0:SGU_BLOCK] = sb_t[0:8, :]
            for g in range(4):
                osm_ref[ROW_PW:ROW_PW + HEAD, HEAD * g:HEAD * (g + 1)] = (
                    pw_acc[HEAD * g:HEAD * (g + 1), HEAD * g:HEAD * (g + 1)])

    def rows(width):
        return pl.BlockSpec((T, width), lambda i, l: (nt - 1 - i, 0))

    consts = (wi, caw, cbw, s256, seg, pw, wm, wmt, sb, wo, v1024, e4)
    unstacked = (wi, seg, wo, e4)
    in_specs = [rows(D_MODEL), rows(D_MODEL), rows(IN_WIDTH), rows(3 * GROUP)] + [
        _whole(a) if any(a is u for u in unstacked) else _of_layer(a) for a in consts]
    out_specs = [rows(D_MODEL), rows(IN_WIDTH), rows(D_MODEL), pl.BlockSpec((SM_ROWS, GROUP), lambda i, l: (0, 0))]
    out_shape = [jax.ShapeDtypeStruct((S, D_MODEL), F32), jax.ShapeDtypeStruct((S, IN_WIDTH), BF16),
                 jax.ShapeDtypeStruct((S, D_MODEL), BF16), jax.ShapeDtypeStruct((SM_ROWS, GROUP), F32)]
    scratch_shapes = list(scratch.values())
    extra, aliases = (), {}
    if exch is not None:
        extra = tuple(exch)
        r_i, r_o = exch[3], exch[4]
        in_specs += [ANY] * 5
        out_specs += [ANY] * 3
        out_shape += [jax.ShapeDtypeStruct(r_i.shape, r_i.dtype), jax.ShapeDtypeStruct(r_o.shape, r_o.dtype),
                      jax.ShapeDtypeStruct((N_DEV, SM_ROWS, GROUP), F32)]
        scratch_shapes += [pltpu.SemaphoreType.DMA((N_EXCH_SEMS,)), pltpu.SemaphoreType.DMA((N_EXCH_SEMS,)),
                           pltpu.SemaphoreType.DMA((1,))]
        aliases = {20: 4, 21: 5}
    grid_spec = pltpu.PrefetchScalarGridSpec(num_scalar_prefetch=1, grid=(nt,), in_specs=in_specs,
                                             out_specs=out_specs, scratch_shapes=scratch_shapes)
    return pl.pallas_call(
        body, name="bwd_layer" if exch is None else "bwd_layer_exchange",
        grid_spec=grid_spec, out_shape=out_shape, input_output_aliases=aliases,
        compiler_params=_vmem_params(dimension_semantics=("arbitrary",), **(
            dict(has_side_effects=True, collective_id=COLLECTIVE_ID["bwd_layer_exchange"]) if exch is not None else {})),
    )(larr, dy, z, h, aux, *consts, *extra)


def _dw(layer, xb, dhb, mixb, dzb, gwi, gwi16, gwo, gwo16, *, k_steps, small=None):
    S = xb.shape[0]
    tk = S // k_steps
    n_steps = N_CHIPS + k_steps

    def body(*refs):
        x_ref, dh_ref, mix_ref, dz_ref = refs[1:5]
        oi_ref, oi16_ref, oo_ref, oo16_ref = refs[n_in:n_in + 4]
        j = pl.program_id(0)
        if small is not None:
            _exchange_comm(j == 0, j == n_steps - 1, None, None, None, refs[9], None, None, refs[n_in + 4],
                           *refs[n_in + 5:])

        @pl.when(j < N_CHIPS)
        def _():
            acc = _dot_tn(x_ref[...], dh_ref[...])
            oi_ref[...] = acc
            oi16_ref[...] = acc.astype(BF16)

        @pl.when(j == N_CHIPS)
        def _():
            oo_ref[...] = jnp.zeros_like(oo_ref)

        @pl.when(j >= N_CHIPS)
        def _():
            oo_ref[...] += _dot_tn(mix_ref[...], dz_ref[...]).reshape(N_CHIPS, GROUP, D_MODEL)

        @pl.when(j == n_steps - 1)
        def _():
            oo16_ref[...] = oo_ref[...].astype(BF16)

    def col_block(j, l):
        return jnp.minimum(j, N_CHIPS - 1)

    def tok_block(j, l):
        return jnp.maximum(j - N_CHIPS, 0)

    oi_spec = pl.BlockSpec((None, None, D_MODEL, COLS), lambda j, l: (l[0], col_block(j, l), 0, 0))
    oo_spec = pl.BlockSpec((None, N_CHIPS, GROUP, D_MODEL), lambda j, l: (l[0], 0, 0, 0))
    in_specs = [pl.BlockSpec((S, D_MODEL), lambda j, l: (0, 0)),
                pl.BlockSpec((S, COLS), lambda j, l: (0, col_block(j, l))),
                pl.BlockSpec((tk, D_MODEL), lambda j, l: (tok_block(j, l), 0)),
                pl.BlockSpec((tk, D_MODEL), lambda j, l: (tok_block(j, l), 0)), ANY, ANY, ANY, ANY]
    out_specs = [oi_spec, oi_spec, oo_spec, oo_spec]
    out_shape = [jax.ShapeDtypeStruct(gwi.shape, F32), jax.ShapeDtypeStruct(gwi.shape, BF16),
                 jax.ShapeDtypeStruct(gwo.shape, F32), jax.ShapeDtypeStruct(gwo.shape, BF16)]
    scratch, extra = [], ()
    if small is not None:
        extra = (small,)
        in_specs += [ANY]
        out_specs += [ANY]
        out_shape += [jax.ShapeDtypeStruct((N_DEV, SM_ROWS, GROUP), F32)]
        scratch = [pltpu.SemaphoreType.DMA((N_EXCH_SEMS,)), pltpu.SemaphoreType.DMA((N_EXCH_SEMS,)), pltpu.SemaphoreType.DMA((1,))]
    n_in = 9 + len(extra)
    grid_spec = pltpu.PrefetchScalarGridSpec(
        num_scalar_prefetch=1, grid=(n_steps,), in_specs=in_specs, out_specs=out_specs, scratch_shapes=scratch)
    return pl.pallas_call(
        body, name="dw" if small is None else "dw_exchange", grid_spec=grid_spec, out_shape=out_shape,
        input_output_aliases={5: 0, 6: 1, 7: 2, 8: 3},
        compiler_params=_vmem_params(dimension_semantics=("arbitrary",), **(
            dict(has_side_effects=True, collective_id=COLLECTIVE_ID["dw_exchange"]) if small is not None else {})),
    )(layer, xb, dhb, mixb, dzb, gwi, gwi16, gwo, gwo16, *extra)


def _dw_swap(cl_arr, xb, dhb, mixb, dzb, p_i, p_o, *, k_steps):
    S = xb.shape[0]
    tk = S // k_steps
    n_steps = N_CHIPS + k_steps
    hi, ho = p_i.shape[2], p_o.shape[2]

    def body(cl_ref, x_ref, dh_ref, mix_ref, dz_ref, pi_in, po_in, pi_ref, po_ref,
             own_i, acc_o, snd_i, snd_o, rcv_i, rcv_o, send_sems, recv_sems):
        del cl_ref, pi_in, po_in
        j = pl.program_id(0)
        x, y, c = _place()
        mine_o, theirs_o = (pl.ds(pl.multiple_of(cc * ho, ho), ho) for cc in (c, 1 - c))

        def to_sibling(src, dst, sem):
            return pltpu.make_async_remote_copy(src_ref=src, dst_ref=dst, send_sem=send_sems.at[sem],
                                                recv_sem=recv_sems.at[sem], device_id=(x, y, 1 - c), device_id_type=MESH)

        def chunk_copy(k):
            return to_sibling(snd_i.at[k % 2], rcv_i.at[k], k)

        def out_copy():
            return to_sibling(snd_o, rcv_o, N_CHIPS)

        @pl.when(j == 0)
        def _():
            _handshake(PEERS_SIBLING)

        @pl.when(j < N_CHIPS)
        def _():
            @pl.when(j >= 2)
            def _():
                chunk_copy(j - 2).wait_send()

            acc = _dot_tn(x_ref[...], dh_ref[...])
            top, bottom = acc[:hi], acc[hi:]
            own_i[j % 2] = jnp.where(c == 0, top, bottom)
            snd_i[j % 2] = jnp.where(c == 0, bottom, top).astype(BF16)
            chunk_copy(j).start()

        @pl.when(j == N_CHIPS)
        def _():
            acc_o[...] = jnp.zeros_like(acc_o)

        @pl.when(j >= N_CHIPS)
        def _():
            acc_o[...] += _dot_tn(mix_ref[...], dz_ref[...]).reshape(N_CHIPS, GROUP, D_MODEL)

        @pl.when((j >= 1) & (j <= N_CHIPS))
        def _():
            chunk_copy(j - 1).wait_recv()
            pi_ref[...] = (own_i[(j - 1) % 2] + rcv_i[j - 1].astype(F32)).astype(pi_ref.dtype)

        @pl.when(j == n_steps - 1)
        def _():
            snd_o[...] = acc_o[:, theirs_o, :].astype(BF16)
            out_copy().start()
            for k in (N_CHIPS - 2, N_CHIPS - 1):
                chunk_copy(k).wait_send()
            out_copy().wait_recv()
            po_ref[...] = (acc_o[:, mine_o, :] + rcv_o[...].astype(F32)).astype(po_ref.dtype)
            out_copy().wait_send()

    def col_block(j):
        return jnp.minimum(j, N_CHIPS - 1)

    def tok_block(j):
        return jnp.maximum(j - N_CHIPS, 0)

    in_specs = [pl.BlockSpec((S, D_MODEL), lambda j, cl: (0, 0)),
                pl.BlockSpec((S, COLS), lambda j, cl: (0, col_block(j))),
                pl.BlockSpec((tk, D_MODEL), lambda j, cl: (tok_block(j), 0)),
                pl.BlockSpec((tk, D_MODEL), lambda j, cl: (tok_block(j), 0)), ANY, ANY]
    out_specs = [pl.BlockSpec((None, None, hi, COLS), lambda j, cl: (cl[1], jnp.clip(j - 1, 0, N_CHIPS - 1), 0, 0)),
                 pl.BlockSpec((None, N_CHIPS, ho, D_MODEL), lambda j, cl: (cl[1], 0, 0, 0))]
    scratch = [pltpu.VMEM((2, hi, COLS), F32), pltpu.VMEM((N_CHIPS, GROUP, D_MODEL), F32),
               pltpu.VMEM((2, hi, COLS), BF16), pltpu.VMEM((N_CHIPS, ho, D_MODEL), BF16),
               pltpu.VMEM((N_CHIPS, hi, COLS), BF16), pltpu.VMEM((N_CHIPS, ho, D_MODEL), BF16),
               pltpu.SemaphoreType.DMA((N_CHIPS + 1,)), pltpu.SemaphoreType.DMA((N_CHIPS + 1,))]
    grid_spec = pltpu.PrefetchScalarGridSpec(
        num_scalar_prefetch=1, grid=(n_steps,), in_specs=in_specs, out_specs=out_specs, scratch_shapes=scratch)
    return pl.pallas_call(
        body, name="dw_swap", grid_spec=grid_spec,
        out_shape=[jax.ShapeDtypeStruct(p_i.shape, p_i.dtype), jax.ShapeDtypeStruct(p_o.shape, p_o.dtype)],
        input_output_aliases={5: 0, 6: 1},
        compiler_params=_vmem_params(dimension_semantics=("arbitrary",), has_side_effects=True,
                                     collective_id=COLLECTIVE_ID["dw_swap"]),
    )(cl_arr, xb, dhb, mixb, dzb, p_i, p_o)


def _adamw_math(w, g, m, v):
    nm = ADAM_B1 * m + (1.0 - ADAM_B1) * g
    nv = ADAM_B2 * v + (1.0 - ADAM_B2) * (g * g)
    c1 = 1.0 - ADAM_B1 ** ADAM_STEP
    c2 = 1.0 - ADAM_B2 ** ADAM_STEP
    return -ADAM_LR * ((nm / c1) / (jnp.sqrt(nv / c2) + ADAM_EPS) + ADAM_WD * w), nm, nv


def _adamw_small(ws, gs, ms, vs):
    n = len(ws)

    def body(*refs):
        for j in range(n):
            d, nm, nv = _adamw_math(*(refs[k * n + j][...] for k in range(4)))
            refs[4 * n + j][...] = d
            refs[5 * n + j][...] = nm
            refs[6 * n + j][...] = nv

    shapes = [jax.ShapeDtypeStruct(w.shape, F32) for w in ws]
    outs = pl.pallas_call(body, name="adamw_small", out_shape=shapes * 3, compiler_params=_vmem_params())(
        *ws, *gs, *ms, *vs)
    return outs[0:n], outs[n:2 * n], outs[2 * n:3 * n]


def _adamw(w, g, m, v, *, rows_per_step, name, copy_g=False):
    R, C = w.shape
    tr = rows_per_step

    def body(w_ref, g_ref, m_ref, v_ref, d_ref, nm_ref, nv_ref, *g_out):
        g_ = g_ref[...]
        d_ref[...], nm_ref[...], nv_ref[...] = _adamw_math(w_ref[...], g_, m_ref[...], v_ref[...])
        if copy_g:
            g_out[0][...] = g_

    spec = pl.BlockSpec((tr, C), lambda i: (i, 0))
    n_out = 4 if copy_g else 3
    return pl.pallas_call(
        body, name=name, grid=(R // tr,),
        in_specs=[spec] * 4, out_specs=[spec] * n_out,
        out_shape=[jax.ShapeDtypeStruct((R, C), F32)] * n_out,
        compiler_params=_vmem_params(dimension_semantics=("arbitrary",)),
    )(w, g, m, v)


def _gather_weights(wi16, wo16, cw):
    L = wi16.shape[0]
    hi_rows, ho_rows = D_MODEL // 2, GROUP // 2
    n_ici = 2 * L + 1
    n_fwd = 2 * L

    def body(wi_ref, wo_ref, cw_ref, *rest):
        wig = rest[0:L]
        wog = rest[L:2 * L]
        cwg = rest[2 * L]
        send_sems, recv_sems, loc_sems, vwi, vwo, vcw = rest[2 * L + 1:]
        x, y, c = _place()
        me_k = 2 * x + y
        sibling = (x, y, 1 - c)
        chips = _other_chips(x, y)

        def half_i(ref, blk):
            return ref.at[blk, pl.ds(c * hi_rows, hi_rows), :]

        def half_o(ref, blk):
            return ref.at[blk, pl.ds(c * ho_rows, ho_rows), :]

        def other_half_i(ref, blk):
            return ref.at[blk, pl.ds((1 - c) * hi_rows, hi_rows), :]

        def other_half_o(ref, blk):
            return ref.at[blk, pl.ds((1 - c) * ho_rows, ho_rows), :]

        stage_in = [pltpu.make_async_copy(wi_ref, vwi, loc_sems.at[0]), pltpu.make_async_copy(wo_ref, vwo, loc_sems.at[1]),
                    pltpu.make_async_copy(cw_ref, vcw, loc_sems.at[2])]
        local = []
        for l in range(L):
            local.append(pltpu.make_async_copy(vwi.at[l], wig[l].at[me_k], loc_sems.at[3 + 2 * l]))
            local.append(pltpu.make_async_copy(vwo.at[l], wog[l].at[me_k], loc_sems.at[3 + 2 * l + 1]))
        local.append(pltpu.make_async_copy(vcw, cwg.at[me_k], loc_sems.at[3 + 2 * L]))
        _handshake(PEERS_COLUMN)
        for cp in stage_in:
            cp.start()

        def remote(src, dst, sem, to):
            return pltpu.make_async_remote_copy(src_ref=src, dst_ref=dst, send_sem=send_sems.at[sem],
                                                recv_sem=recv_sems.at[sem], device_id=to, device_id_type=MESH)

        sends = []
        for r, (px, py, _) in enumerate(chips):
            to = (px, py, c)
            for l in range(L):
                sends.append(remote(half_i(wi_ref, l), half_i(wig[l], me_k), r * n_ici + 2 * l, to))
                sends.append(remote(half_o(wo_ref, l), half_o(wog[l], me_k), r * n_ici + 2 * l + 1, to))
            sends.append(remote(cw_ref, cwg.at[me_k], r * n_ici + 2 * L, to))
        for cp in sends:
            cp.start()
        for cp in stage_in:
            cp.wait()
        for cp in local:
            cp.start()

        base = 3 * n_ici
        fwds = []
        for r, (px, py, pk) in enumerate(chips):
            for l in range(L):
                remote(half_i(wig[l], pk), half_i(wig[l], pk), r * n_ici + 2 * l, sibling).wait_recv()
                f = remote(half_i(wig[l], pk), half_i(wig[l], pk), base + r * n_fwd + 2 * l, sibling)
                f.start()
                fwds.append(f)
                remote(half_o(wog[l], pk), half_o(wog[l], pk), r * n_ici + 2 * l + 1, sibling).wait_recv()
                f = remote(half_o(wog[l], pk), half_o(wog[l], pk), base + r * n_fwd + 2 * l + 1, sibling)
                f.start()
                fwds.append(f)
            remote(cwg.at[pk], cwg.at[pk], r * n_ici + 2 * L, sibling).wait_recv()
        for r, (px, py, pk) in enumerate(chips):
            for l in range(L):
                remote(other_half_i(wig[l], pk), other_half_i(wig[l], pk), base + r * n_fwd + 2 * l, sibling).wait_recv()
                remote(other_half_o(wog[l], pk), other_half_o(wog[l], pk), base + r * n_fwd + 2 * l + 1, sibling).wait_recv()
        for cp in sends + fwds:
            cp.wait_send()
        for cp in local:
            cp.wait()

    n_sem = 3 * n_ici + 3 * n_fwd
    out_shape = ([jax.ShapeDtypeStruct((N_CHIPS, D_MODEL, COLS), BF16)] * L
                 + [jax.ShapeDtypeStruct((N_CHIPS, GROUP, D_MODEL), BF16)] * L
                 + [jax.ShapeDtypeStruct((N_CHIPS,) + cw.shape, F32)])
    outs = pl.pallas_call(
        body, name="gather_weights",
        in_specs=[ANY, ANY, ANY], out_specs=[ANY] * (2 * L + 1), out_shape=out_shape,
        scratch_shapes=[pltpu.SemaphoreType.DMA((n_sem,)), pltpu.SemaphoreType.DMA((n_sem,)),
                        pltpu.SemaphoreType.DMA((2 * L + 4,)), pltpu.VMEM(wi16.shape, BF16), pltpu.VMEM(wo16.shape, BF16),
                        pltpu.VMEM(cw.shape, F32)],
        compiler_params=_vmem_params(has_side_effects=True, collective_id=COLLECTIVE_ID["gather_weights"]),
    )(wi16, wo16, cw)
    return outs[0:L], outs[L:2 * L], outs[2 * L]


def _swap_add(cl_arr, g_i, g16_i, p_i, g_o, g16_o, p_o, *, send_on=None):
    hi, ho = p_i.shape[2], p_o.shape[2]
    n_in = 7 + (2 if send_on is not None else 0)
    n_out = 2 + (2 if send_on is not None else 0)

    def body(*refs):
        cl_ref, gi_ref, gi16_ref, _, go_ref, go16_ref = refs[0:6]
        oi_ref, oo_ref = refs[n_in:n_in + 2]
        ri_v, ro_v, send_sems, recv_sems = refs[n_in + n_out:n_in + n_out + 4]
        k = pl.program_id(0)
        x, y, c = _place()
        l = cl_ref[1]

        def copies(kk):
            pair = ((gi16_ref, hi, ri_v), (go16_ref, ho, ro_v))
            return [pltpu.make_async_remote_copy(
                src_ref=src.at[l, kk, pl.ds((1 - c) * n, n), :], dst_ref=dst.at[kk], send_sem=send_sems.at[2 * kk + j],
                recv_sem=recv_sems.at[2 * kk + j], device_id=(x, y, 1 - c), device_id_type=MESH)
                for j, (src, n, dst) in enumerate(pair)]

        @pl.when(k == 0)
        def _():
            _handshake(PEERS_SIBLING if send_on is None else PEERS_COLUMN)
            for kk in range(N_CHIPS):
                for cp in copies(kk):
                    cp.start()

        for cp in copies(k):
            cp.wait_recv()
        pi_k = (gi_ref[...] + ri_v[k].astype(F32)).astype(oi_ref.dtype)
        po_k = (go_ref[...] + ro_v[k].astype(F32)).astype(oo_ref.dtype)
        oi_ref[...] = pi_k
        oo_ref[...] = po_k

        if send_on is not None:
            qi_ref, qo_ref = refs[n_in + 2:n_in + 4]
            pv_i, pv_o, out_sems, in_sems = refs[n_in + n_out + 4:]
            pv_i[k] = pi_k
            pv_o[k] = po_k
            chips = _other_chips(x, y)

            def onward(r):
                px, py, pk = chips[r]
                return [pltpu.make_async_remote_copy(
                    src_ref=pv.at[pk], dst_ref=q.at[r, l], send_sem=out_sems.at[2 * r + j], recv_sem=in_sems.at[2 * r + j],
                    device_id=(px, py, c), device_id_type=MESH) for j, (pv, q) in enumerate(((pv_i, qi_ref), (pv_o, qo_ref)))]

            for r in range(3):
                @pl.when(k == chips[r][2])
                def _():
                    for cp in onward(r):
                        cp.start()

        @pl.when(k == N_CHIPS - 1)
        def _():
            for kk in range(N_CHIPS):
                for cp in copies(kk):
                    cp.wait_send()
            if send_on is not None:
                for r in range(3):
                    for cp in onward(r):
                        cp.wait()

    def specs(p):
        rows, cols = p.shape[2], p.shape[3]
        mine = pl.BlockSpec((None, None, rows, cols), lambda k, cl: (cl[1], k, cl[0], 0))
        out = pl.BlockSpec((None, None, rows, cols), lambda k, cl: (cl[1], k, 0, 0))
        return mine, out

    (gi_s, pi_s), (go_s, po_s) = specs(p_i), specs(p_o)
    in_specs = [gi_s, ANY, ANY, go_s, ANY, ANY]
    out_specs = [pi_s, po_s]
    out_shape = [jax.ShapeDtypeStruct(p_i.shape, p_i.dtype), jax.ShapeDtypeStruct(p_o.shape, p_o.dtype)]
    scratch = [pltpu.VMEM((N_CHIPS, hi, p_i.shape[3]), BF16), pltpu.VMEM((N_CHIPS, ho, p_o.shape[3]), BF16),
               pltpu.SemaphoreType.DMA((2 * N_CHIPS,)), pltpu.SemaphoreType.DMA((2 * N_CHIPS,))]
    extra, aliases = (), {3: 0, 6: 1}
    if send_on is not None:
        extra = tuple(send_on)
        in_specs += [ANY, ANY]
        out_specs += [ANY, ANY]
        out_shape += [jax.ShapeDtypeStruct(q.shape, q.dtype) for q in send_on]
        scratch += [pltpu.VMEM((N_CHIPS, hi, p_i.shape[3]), BF16), pltpu.VMEM((N_CHIPS, ho, p_o.shape[3]), BF16),
                    pltpu.SemaphoreType.DMA((6,)), pltpu.SemaphoreType.DMA((6,))]
        aliases = {3: 0, 6: 1, 7: 2, 8: 3}
    grid_spec = pltpu.PrefetchScalarGridSpec(num_scalar_prefetch=1, grid=(N_CHIPS,), in_specs=in_specs,
                                             out_specs=out_specs, scratch_shapes=scratch)
    return pl.pallas_call(
        body, name="swap_add" if send_on is None else "swap_add_send", grid_spec=grid_spec, out_shape=out_shape,
        input_output_aliases=aliases,
        compiler_params=_vmem_params(dimension_semantics=("arbitrary",), has_side_effects=True,
                                     collective_id=COLLECTIVE_ID["swap_add" if send_on is None else "swap_add_send"]),
    )(cl_arr, g_i, g16_i, p_i, g_o, g16_o, p_o, *extra)


def _sum_small(r_sms):
    L = len(r_sms)

    def body(*refs):
        o_ref = refs[L]
        for l in range(L):
            acc = refs[l][0]
            for d in range(1, N_DEV):
                acc = acc + refs[l][d]
            o_ref[l] = acc

    return pl.pallas_call(
        body, name="sum_small",
        out_shape=jax.ShapeDtypeStruct((L,) + r_sms[0].shape[1:], F32),
        compiler_params=_vmem_params(),
    )(*r_sms)


def _sum_share(kc_arr, p_i, q_i, p_o, q_o, *, nb):
    L = p_i.shape[0]
    n_steps, slots = L * nb, 2

    def body(kc_ref, pi_ref, a0, a1, a2, po_ref, b0, b1, b2, oi_ref, oo_ref, vi, vo, loc_sems, send_sems, recv_sems):
        del kc_ref
        x, y, c = _place()
        t = pl.program_id(0) * nb + pl.program_id(1)

        def copies(s):
            l, i = s // nb, s % nb
            out = []
            for j, (v, o) in enumerate(((vi, oi_ref), (vo, oo_ref))):
                tr = v.shape[1]
                src, dst = v.at[s % slots], o.at[l, pl.ds((c * nb + i) * tr, tr), :]
                out.append((pltpu.make_async_copy(src, dst, loc_sems.at[2 * s + j]),
                            pltpu.make_async_remote_copy(src_ref=src, dst_ref=dst, send_sem=send_sems.at[2 * s + j],
                                                         recv_sem=recv_sems.at[2 * s + j], device_id=(x, y, 1 - c),
                                                         device_id_type=MESH)))
            return out

        def sent(s):
            for mine, theirs in copies(s):
                mine.wait()
                theirs.wait_send()

        @pl.when(t == 0)
        def _():
            _handshake(PEERS_SIBLING)

        @pl.when(t >= slots)
        def _():
            sent(t - slots)

        f = lambda ref: ref[...].astype(F32)
        vi[t % slots] = ((f(pi_ref) + f(a0)) + f(a1)) + f(a2)
        vo[t % slots] = ((f(po_ref) + f(b0)) + f(b1)) + f(b2)
        for mine, theirs in copies(t):
            mine.start()
            theirs.start()

        @pl.when(t == n_steps - 1)
        def _():
            for s in range(n_steps - slots, n_steps):
                sent(s)
            for s in range(n_steps):
                for _, theirs in copies(s):
                    theirs.wait_recv()

    def specs(p):
        tr, cols = p.shape[2] // nb, p.shape[3]
        chunk = pl.BlockSpec((None, None, tr, cols), lambda l, i, kc: (l, kc[0], i, 0))
        got = [pl.BlockSpec((None, None, tr, cols), lambda l, i, kc, _j=j: (_j, l, i, 0)) for j in range(3)]
        return [chunk] + got, pltpu.VMEM((slots, tr, cols), F32)

    (in_i, v_i), (in_o, v_o) = specs(p_i), specs(p_o)
    grid_spec = pltpu.PrefetchScalarGridSpec(
        num_scalar_prefetch=1, grid=(L, nb), in_specs=in_i + in_o, out_specs=[ANY, ANY],
        scratch_shapes=[v_i, v_o] + [pltpu.SemaphoreType.DMA((2 * n_steps,))] * 3)
    return pl.pallas_call(
        body, name="sum_share", grid_spec=grid_spec,
        out_shape=[jax.ShapeDtypeStruct((L, 2 * p.shape[2], p.shape[3]), F32) for p in (p_i, p_o)],
        compiler_params=_vmem_params(dimension_semantics=("arbitrary",) * 2, has_side_effects=True,
                                     collective_id=COLLECTIVE_ID["sum_share"]),
    )(kc_arr, p_i, q_i, q_i, q_i, p_o, q_o, q_o, q_o)


WEIGHTS = ("ln_g", "ln_b", "w_in", "b_in", "conv_a_w", "conv_a_b", "norm_a_g", "norm_a_b", "conv_b_w", "pool_w",
           "pool_scale", "sgu_ln_g", "sgu_ln_b", "sgu_w", "sgu_bias", "w_out", "b_out")


def _pad_rows(a, rows):
    return jnp.pad(a, ((0, rows - a.shape[0]), (0, 0)))


def _indicator_consts():
    seg = jnp.where((jnp.arange(GROUP)[:, None] // HEAD) == (jnp.arange(GROUP)[None, :] // HEAD),
                    1.0 / HEAD, 0.0).astype(BF16)
    e4 = ((jnp.arange(GROUP)[:, None] // HEAD) == jnp.arange(128)[None, :]).astype(BF16)
    return seg, e4


def _layer_consts(p, conv_full):
    L = conv_full.shape[0]
    same_head = jnp.eye(4, dtype=F32)[:, None, :, None] > 0

    def rows_to(a, rows):
        return jnp.pad(a, ((0, 0), (0, rows - a.shape[1]), (0, 0)))

    s256 = jnp.stack([p[n] for n in ("conv_a_b", "norm_a_g", "norm_a_b", "pool_scale", "sgu_ln_g", "sgu_ln_b")], axis=1)
    pw = jnp.where(same_head, p["pool_w"][:, :, :, None, :], 0.0).reshape(L, GROUP, GROUP)
    return dict(
        caw=rows_to(conv_full[:, :KA], 32), cbw=rows_to(conv_full[:, KA:], 8), s256=rows_to(s256, 8),
        pw=pw.astype(BF16),
        wm=jnp.transpose(p["sgu_w"], (0, 2, 1, 3)).reshape(L, SGU_BLOCK, 4 * SGU_BLOCK),
        wmt=jnp.transpose(p["sgu_w"], (0, 1, 3, 2)).reshape(L, 4 * SGU_BLOCK, SGU_BLOCK),
        sb=jnp.repeat(jnp.transpose(p["sgu_bias"], (0, 2, 1)), HEAD, axis=2),
        v1024=rows_to(jnp.stack([p["b_out"], p["ln_g"], p["ln_b"]], axis=1), 8),
        bin=p["b_in"][:, None, :])


def _unpack_small(sm):
    L = sm.shape[0]
    owc = jnp.concatenate([sm[:, ROW_WC:ROW_WC + SGU_BLOCK], sm[:, ROW_WC + SGU_BLOCK:ROW_WC + 2 * SGU_BLOCK]], axis=2)
    return dict(
        conv_a_b=sm[:, 0], norm_a_g=sm[:, 1], norm_a_b=sm[:, 2], pool_scale=sm[:, 3], sgu_ln_g=sm[:, 4],
        sgu_ln_b=sm[:, 5], conv_b_w=sm[:, ROW_CBW:ROW_CBW + KB], conv_a_w=sm[:, ROW_CAW:ROW_CAW + KA],
        pool_w=jnp.transpose(sm[:, ROW_PW:ROW_PW + HEAD].reshape(L, HEAD, 4, HEAD), (0, 2, 1, 3)),
        ln_g=sm[:, ROW_LNG:ROW_LNG + 4].reshape(L, D_MODEL), ln_b=sm[:, ROW_LNB:ROW_LNB + 4].reshape(L, D_MODEL),
        b_out=sm[:, ROW_BOUT:ROW_BOUT + 4].reshape(L, D_MODEL),
        b_in=sm[:, ROW_BIN:ROW_BIN + N_SLICES].reshape(L, IN_WIDTH),
        sgu_w=jnp.transpose(owc.reshape(L, SGU_BLOCK, 4, SGU_BLOCK), (0, 2, 1, 3)),
        sgu_bias=sm[:, ROW_SB:ROW_SB + 4, 0:SGU_BLOCK])


def _step(p, m, v, x, target, *, tile_f, tile_b, k_steps):
    L = p["ln_g"].shape[0]
    xi, yi, ci = _place()
    me_k = 2 * xi + yi
    hi_rows, ho_rows = D_MODEL // 2, GROUP // 2

    cw = jnp.concatenate([p["conv_a_w"], p["conv_b_w"]], axis=1).reshape(-1, 128)
    cw_rows = cw.shape[0]
    cw = _pad_rows(cw, -(-cw_rows // SUBLANES) * SUBLANES)
    wi16 = p["w_in"].astype(BF16)
    wo16 = p["w_out"].astype(BF16)
    wig0, wog0, cwg = _gather_weights(wi16[0:1], wo16[0:1], cw)
    cwg = cwg[:, :cw_rows].reshape(N_CHIPS, L, KA + KB, HEAD)
    conv_full = jnp.transpose(cwg, (1, 2, 0, 3)).reshape(L, KA + KB, GROUP)
    seg, e4 = _indicator_consts()
    k = _layer_consts(p, conv_full)
    layer = [jnp.full((1,), l, jnp.int32) for l in range(L)]

    hcur = x
    saved, wig, wog = [], [wig0[0]], [wog0[0]]
    for l in range(L):
        nxt = (wi16, wo16) if l + 1 < L else None
        outs = _fwd_layer(layer[l], hcur, wig[l], k["bin"], k["caw"], k["cbw"], k["s256"], seg, k["pw"], k["wm"], k["sb"],
                          wog[l], k["v1024"], tile=tile_f, nxt=nxt, target=None if nxt is not None else target)
        y, xb, h, aux, mixb, z = outs[0:6]
        if nxt is not None:
            wig.append(outs[6])
            wog.append(outs[7])
        saved.append((xb, h, aux, mixb, z))
        hcur = y

    dy = hcur
    loss_local = outs[6][0, 0]

    gwi = lax.empty((L, N_CHIPS, D_MODEL, COLS), F32)
    gwo = lax.empty((L, N_CHIPS, GROUP, D_MODEL), F32)
    gwi16 = lax.empty((L, N_CHIPS, D_MODEL, COLS), BF16)
    gwo16 = lax.empty((L, N_CHIPS, GROUP, D_MODEL), BF16)
    p_i = lax.empty((L, N_CHIPS, hi_rows, COLS), BF16)
    p_o = lax.empty((L, N_CHIPS, ho_rows, D_MODEL), BF16)
    q_i = lax.empty((3, L, hi_rows, COLS), BF16)
    q_o = lax.empty((3, L, ho_rows, D_MODEL), BF16)
    r_sm = [None] * L
    pending = None
    for l in reversed(range(L)):
        xb, h, aux, mixb, z = saved[l]
        exch = None if pending is None else (p_i, p_o, pending, q_i, q_o)
        outs = _bwd_layer(layer[l], dy, z, h, aux, wig[l], k["caw"], k["cbw"], k["s256"], seg, k["pw"], k["wm"],
                          k["wmt"], k["sb"], wog[l], k["v1024"], e4, tile=tile_b, exch=exch)
        dy, dhb, dzb, osm = outs[0:4]
        if l == L - 1:
            osm = osm.at[ROW_LOSS, 0].set(loss_local)
        if exch is not None:
            q_i, q_o, r_sm[l + 1] = outs[4:7]
        cl_arr = jnp.stack([ci, jnp.int32(l)]).astype(jnp.int32)
        if l > 0:
            p_i, p_o = _dw_swap(cl_arr, xb, dhb, mixb, dzb, p_i, p_o, k_steps=k_steps)
        else:
            outs = _dw(layer[l], xb, dhb, mixb, dzb, gwi, gwi16, gwo, gwo16, k_steps=k_steps, small=osm)
            gwi, gwi16, gwo, gwo16, r_sm[0] = outs
            p_i, p_o, q_i, q_o = _swap_add(cl_arr, gwi, gwi16, p_i, gwo, gwo16, p_o, send_on=(q_i, q_o))
        pending = osm
    grad_x = dy

    summed = _sum_small(r_sm)
    loss = summed[L - 1, ROW_LOSS, 0]
    grads = _unpack_small(summed)
    for n in ("conv_a_w", "conv_b_w"):
        grads[n] = lax.dynamic_slice_in_dim(grads[n], me_k * HEAD, HEAD, axis=2)

    kc_arr = jnp.stack([me_k, ci]).astype(jnp.int32)
    g_i, g_o = _sum_share(kc_arr, p_i, q_i, p_o, q_o, nb=2)
    grads["w_in"] = g_i
    grads["w_out"] = g_o

    delta, new_m, new_v = {}, {}, {}
    for n, tr in (("w_in", 512), ("w_out", 256)):
        shp = p[n].shape
        args = [a.reshape(shp[0] * shp[1], shp[2]) for a in (p[n], grads[n], m[n], v[n])]
        outs = _adamw(*args, rows_per_step=tr, name="adamw_" + n, copy_g=True)
        delta[n], new_m[n], new_v[n], grads[n] = (a.reshape(shp) for a in outs)
    small = [n for n in WEIGHTS if n not in ("w_in", "w_out")]
    flat = [[a[n].reshape(-1, a[n].shape[-1]) for n in small] for a in (p, grads, m, v)]
    outs = _adamw_small(*flat)
    for j, n in enumerate(small):
        delta[n], new_m[n], new_v[n] = (o[j].reshape(p[n].shape) for o in outs)

    return (loss, grad_x[None], *[grads[n] for n in WEIGHTS], *[delta[n] for n in WEIGHTS],
            *[new_m[n] for n in WEIGHTS], *[new_v[n] for n in WEIGHTS])


def kernel(x, ln_g, ln_b, w_in, b_in, conv_a_w, conv_a_b, norm_a_g, norm_a_b, conv_b_w, pool_w, pool_scale, sgu_ln_g, sgu_ln_b, sgu_w, sgu_bias, w_out, b_out, loss_target, m_ln_g, m_ln_b, m_w_in, m_b_in, m_conv_a_w, m_conv_a_b, m_norm_a_g, m_norm_a_b, m_conv_b_w, m_pool_w, m_pool_scale, m_sgu_ln_g, m_sgu_ln_b, m_sgu_w, m_sgu_bias, m_w_out, m_b_out, v_ln_g, v_ln_b, v_w_in, v_b_in, v_conv_a_w, v_conv_a_b, v_norm_a_g, v_norm_a_b, v_conv_b_w, v_pool_w, v_pool_scale, v_sgu_ln_g, v_sgu_ln_b, v_sgu_w, v_sgu_bias, v_w_out, v_b_out):
    p = dict(ln_g=ln_g, ln_b=ln_b, w_in=w_in, b_in=b_in, conv_a_w=conv_a_w, conv_a_b=conv_a_b, norm_a_g=norm_a_g,
             norm_a_b=norm_a_b, conv_b_w=conv_b_w, pool_w=pool_w, pool_scale=pool_scale, sgu_ln_g=sgu_ln_g,
             sgu_ln_b=sgu_ln_b, sgu_w=sgu_w, sgu_bias=sgu_bias, w_out=w_out, b_out=b_out)
    m = dict(ln_g=m_ln_g, ln_b=m_ln_b, w_in=m_w_in, b_in=m_b_in, conv_a_w=m_conv_a_w, conv_a_b=m_conv_a_b,
             norm_a_g=m_norm_a_g, norm_a_b=m_norm_a_b, conv_b_w=m_conv_b_w, pool_w=m_pool_w, pool_scale=m_pool_scale,
             sgu_ln_g=m_sgu_ln_g, sgu_ln_b=m_sgu_ln_b, sgu_w=m_sgu_w, sgu_bias=m_sgu_bias, w_out=m_w_out, b_out=m_b_out)
    v = dict(ln_g=v_ln_g, ln_b=v_ln_b, w_in=v_w_in, b_in=v_b_in, conv_a_w=v_conv_a_w, conv_a_b=v_conv_a_b,
             norm_a_g=v_norm_a_g, norm_a_b=v_norm_a_b, conv_b_w=v_conv_b_w, pool_w=v_pool_w, pool_scale=v_pool_scale,
             sgu_ln_g=v_sgu_ln_g, sgu_ln_b=v_sgu_ln_b, sgu_w=v_sgu_w, sgu_bias=v_sgu_bias, w_out=v_w_out, b_out=v_b_out)
    return _step(p, m, v, x[0], loss_target[0], tile_f=256, tile_b=256, k_steps=4)
```

```python
import jax
import jax.numpy as jnp
from jax import lax
from jax.experimental import pallas as pl
from jax.experimental.pallas import tpu as pltpu

F32 = jnp.float32
BF16 = jnp.bfloat16
MESH = pl.DeviceIdType.MESH

D_MODEL = 1024
GROUP = 256
HEAD = 64
N_SLICES = 12
IN_WIDTH = N_SLICES * GROUP
N_CHIPS = 4
COLS = IN_WIDTH // N_CHIPS
KA = 31
KB = 3
SUBLANES = 8
HALO_A, HALO_B, HALO_C = 32, 8, 16
N_GATHER_SEMS = 12
N_EXCH_SEMS = 13
SGU_BLOCK = 128
CHUNK = 64
LN_EPS = 1e-5
ROWS = 64
V7X_VMEM_BYTES = 64 * 1024 * 1024
VMEM_LIMIT = V7X_VMEM_BYTES - 8 * 1024 * 1024

ADAM_LR, ADAM_B1, ADAM_B2, ADAM_EPS, ADAM_WD, ADAM_STEP = 0.001, 0.9, 0.999, 1e-08, 0.01, 10


ANY = pl.BlockSpec(memory_space=pl.ANY)


def _vmem_params(**kw):
    return pltpu.CompilerParams(vmem_limit_bytes=VMEM_LIMIT, **kw)


def _whole(a):
    return pl.BlockSpec(a.shape, lambda i, l, _n=a.ndim: (0,) * _n)


def _of_layer(a):
    return pl.BlockSpec((None,) + a.shape[1:], lambda i, l, _n=a.ndim: (l[0],) + (0,) * (_n - 1))


def _place():
    return lax.axis_index("x"), lax.axis_index("y"), lax.axis_index("c")


def _other_chips(x, y):
    return [(1 - x, y, 2 * (1 - x) + y), (x, 1 - y, 2 * x + (1 - y)), (1 - x, 1 - y, 2 * (1 - x) + (1 - y))]


PEERS_SIBLING, PEERS_COLUMN, PEERS_ALL = "sibling", "sibling and the same core of the other chips", "all"
COLLECTIVE_ID = dict(sum_share=0, dw_swap=1, gather_weights=2, fwd_layer_gather=3, bwd_layer_exchange=4,
                     dw_swap_exchange=5)


def _handshake(peers):
    x, y, c = _place()
    if peers == PEERS_SIBLING:
        ids = [(x, y, 1 - c)]
    elif peers == PEERS_COLUMN:
        ids = [(x, y, 1 - c)] + [(px, py, c) for px, py, _ in _other_chips(x, y)]
    else:
        ids = [(1 - x if r & 4 else x, 1 - y if r & 2 else y, 1 - c if r & 1 else c) for r in range(1, 8)]
    barrier = pltpu.get_barrier_semaphore()
    for to in ids:
        pl.semaphore_signal(barrier, inc=1, device_id=to, device_id_type=MESH)
    pl.semaphore_wait(barrier, len(ids))


def _sig(v):
    return 0.5 * jnp.tanh(0.5 * v) + 0.5


def _dot(a, b):
    return jnp.dot(a, b, preferred_element_type=F32)


def _dot_nt(a, b):
    return lax.dot_general(a, b, (((1,), (1,)), ((), ())), preferred_element_type=F32)


def _dot_tn(a, b):
    return lax.dot_general(a, b, (((0,), (0,)), ((), ())), preferred_element_type=F32)


def _segdot(v, m):
    hi = v.astype(BF16)
    lo = (v - hi.astype(F32)).astype(BF16)
    return _dot(hi, m) + _dot(lo, m)


def _colsum(v):
    return jnp.sum(v, axis=0, keepdims=True)


def _rowmean(v):
    return jnp.mean(v, axis=-1, keepdims=True)


def _lane_group(n):
    return lax.broadcasted_iota(jnp.int32, (1, n), 1) // HEAD


def _pool_cnt(tile, t_rows):
    pos = tile * t_rows + lax.broadcasted_iota(jnp.int32, (t_rows, GROUP), 0) + 1
    grp = lax.broadcasted_iota(jnp.int32, (t_rows, GROUP), 1) // HEAD
    win = jnp.where(grp == 0, 2, jnp.where(grp == 1, 4, jnp.where(grp == 2, 8, 16)))
    return jnp.minimum(pos, win).astype(F32)


def _sgu_masks(wm_ref, wmt_ref, wm_s, wmt_s):
    r = lax.broadcasted_iota(jnp.int32, (SGU_BLOCK, 4 * SGU_BLOCK), 0) // CHUNK
    c = (lax.broadcasted_iota(jnp.int32, (SGU_BLOCK, 4 * SGU_BLOCK), 1) % SGU_BLOCK) // CHUNK
    wm_s[...] = jnp.where(c <= r, wm_ref[...], 0.0).astype(BF16)
    if wmt_ref is not None:
        rt = (lax.broadcasted_iota(jnp.int32, (4 * SGU_BLOCK, SGU_BLOCK), 0) % SGU_BLOCK) // CHUNK
        ct = lax.broadcasted_iota(jnp.int32, (4 * SGU_BLOCK, SGU_BLOCK), 1) // CHUNK
        wmt_s[...] = jnp.where(rt <= ct, wmt_ref[...], 0.0).astype(BF16)


def _vstack(v_blk):
    grp = _lane_group(GROUP)
    return jnp.concatenate([jnp.where(grp == h, v_blk, 0.0) for h in range(4)], axis=0).astype(BF16)


def _gather_next(step, nt, nwi, nwo, gwi, gwo, send_sems, recv_sems, loc_sems, vwi, vwo):
    x, y, c = _place()
    me_k = 2 * x + y
    sibling = (x, y, 1 - c)
    chips = _other_chips(x, y)
    hi, ho = D_MODEL // 2, GROUP // 2
    fwd_sems = N_GATHER_SEMS // 2

    def rc(src, dst, sem, to):
        return pltpu.make_async_remote_copy(src_ref=src, dst_ref=dst, send_sem=send_sems.at[sem],
                                            recv_sem=recv_sems.at[sem], device_id=to, device_id_type=MESH)

    def blk(ref, k, n, cc):
        return ref.at[k, pl.ds(cc * n, n), :]

    def ici(r):
        px, py, _ = chips[r]
        to = (px, py, c)
        return [rc(nwi.at[pl.ds(c * hi, hi), :], blk(gwi, me_k, hi, c), 2 * r, to),
                rc(nwo.at[pl.ds(c * ho, ho), :], blk(gwo, me_k, ho, c), 2 * r + 1, to)]

    def landed(r, cc, base):
        pk = chips[r][2]
        return [rc(blk(gwi, pk, hi, cc), blk(gwi, pk, hi, cc), base + 2 * r, sibling),
                rc(blk(gwo, pk, ho, cc), blk(gwo, pk, ho, cc), base + 2 * r + 1, sibling)]

    def stage_in():
        return [pltpu.make_async_copy(nwi, vwi, loc_sems.at[0]), pltpu.make_async_copy(nwo, vwo, loc_sems.at[1])]

    def local():
        return [pltpu.make_async_copy(vwi, gwi.at[me_k], loc_sems.at[2]),
                pltpu.make_async_copy(vwo, gwo.at[me_k], loc_sems.at[3])]

    @pl.when(step == 0)
    def _():
        _handshake(PEERS_COLUMN)
        for cp in stage_in():
            cp.start()
        for r in range(3):
            for cp in ici(r):
                cp.start()

    @pl.when(step == 1)
    def _():
        for cp in stage_in():
            cp.wait()
        for cp in local():
            cp.start()

    @pl.when(step == (3 * nt) // 4)
    def _():
        for r in range(3):
            for got, fwd in zip(landed(r, c, 0), landed(r, c, fwd_sems)):
                got.wait_recv()
                fwd.start()

    @pl.when(step == nt - 1)
    def _():
        for r in range(3):
            for got in landed(r, 1 - c, fwd_sems):
                got.wait_recv()
        for r in range(3):
            for cp in ici(r) + landed(r, c, fwd_sems):
                cp.wait_send()
        for cp in local():
            cp.wait()


def _fwd_layer(larr, x, wi, bin_, caw, cbw, s256, seg, pw, wm, sb, wo, v1024, *, tile, nxt=None, target=None):
    assert nxt is None or target is None
    S = x.shape[0]
    T = tile
    nt = S // T
    alpha = float((2.0 * 4) ** 0.25)
    n_in = 13 + (2 if nxt is not None else 0) + (1 if target is not None else 0)
    n_out = 6 + (2 if nxt is not None else 0) + (1 if target is not None else 0)

    def body(*refs):
        l_ref = refs[0]
        (x_ref, wi_ref, bin_ref, caw_ref, cbw_ref, s256_ref, seg_ref, pw_ref, wm_ref, sb_ref, wo_ref,
         v1024_ref) = refs[1:13]
        y_ref, xb_ref, h_ref, aux_ref, mix_ref, z_ref = refs[n_in:n_in + 6]
        abuf, bbuf, cbuf, wm_s, shf = refs[n_in + n_out:n_in + n_out + 5]
        i = pl.program_id(0)
        if nxt is not None:
            _gather_next(i, nt, refs[13].at[l_ref[0] + 1], refs[14].at[l_ref[0] + 1], refs[n_in + 6], refs[n_in + 7],
                         *refs[n_in + n_out + 5:])

        @pl.when(i == 0)
        def _():
            abuf[0:HALO_A, :] = jnp.zeros((HALO_A, GROUP), F32)
            bbuf[0:HALO_B, :] = jnp.zeros((HALO_B, GROUP), F32)
            cbuf[0:HALO_C, :] = jnp.zeros((HALO_C, GROUP), F32)
            _sgu_masks(wm_ref, None, wm_s, None)

        x = x_ref[...]
        xb = x.astype(BF16)
        xb_ref[...] = xb
        for k in range(N_CHIPS):
            h_ref[:, COLS * k:COLS * (k + 1)] = _dot(xb, wi_ref[k]) + bin_ref[:, COLS * k:COLS * (k + 1)]

        def hs(j):
            return h_ref[:, GROUP * j:GROUP * (j + 1)]

        abuf[HALO_A:HALO_A + T, :] = hs(0) * _sig(hs(1))
        span = T + HALO_A - SUBLANES
        for p in range(1, SUBLANES):
            shf[p - 1, :, :] = abuf[p:p + span, :]
        for r0 in range(0, T, ROWS):
            acc = None
            for k in range(KA):
                off = HALO_A - (KA - 1) + k
                p, q8 = off % SUBLANES, off - off % SUBLANES
                win = abuf[r0 + q8:r0 + q8 + ROWS, :] if p == 0 else shf[p - 1, r0 + q8:r0 + q8 + ROWS, :]
                term = caw_ref[k:k + 1, :] * win
                acc = term if acc is None else acc + term
            aux_ref[r0:r0 + ROWS, 0:GROUP] = acc + s256_ref[0:1, :]
        abuf[0:HALO_A, :] = abuf[T:T + HALO_A, :]
        a1 = aux_ref[:, 0:GROUP]
        segm = seg_ref[...]
        cen = a1 - _segdot(a1, segm)
        var = _segdot(cen * cen, segm)
        a2 = cen * lax.rsqrt(var + LN_EPS) * s256_ref[1:2, :] + s256_ref[2:3, :]
        az = hs(2)
        mix_ref[:, 0:GROUP] = (a2 * _sig(a2) * (az * _sig(az))).astype(BF16)

        bbuf[HALO_B:HALO_B + T, :] = hs(4) * hs(5)
        for r0 in range(0, T, ROWS):
            acc = None
            for k in range(KB):
                off = HALO_B - (KB - 1) + k + r0
                term = cbw_ref[k:k + 1, :] * bbuf[off:off + ROWS, :]
                acc = term if acc is None else acc + term
            aux_ref[r0:r0 + ROWS, GROUP:2 * GROUP] = acc
        bbuf[0:HALO_B, :] = bbuf[T:T + HALO_B, :]
        bz = hs(6)
        mix_ref[:, GROUP:2 * GROUP] = (hs(3) * aux_ref[:, GROUP:2 * GROUP] * (bz * _sig(bz))).astype(BF16)

        ch = hs(7)
        cbuf[HALO_C:HALO_C + T, :] = ch
        hi_lane = (lax.broadcasted_iota(jnp.int32, (1, 128), 1) // HEAD) == 1
        for r0 in range(0, T, ROWS):
            def win(col, j0, j1):
                s = None
                for j in range(j0, j1):
                    off = HALO_C - j + r0
                    term = cbuf[off:off + ROWS, 128 * col:128 * (col + 1)]
                    s = term if s is None else s + term
                return s
            w0 = win(0, 0, 2) + jnp.where(hi_lane, win(0, 2, 4), 0.0)
            w1 = win(1, 0, 8) + jnp.where(hi_lane, win(1, 8, 16), 0.0)
            aux_ref[r0:r0 + ROWS, 2 * GROUP:2 * GROUP + 128] = w0
            aux_ref[r0:r0 + ROWS, 2 * GROUP + 128:3 * GROUP] = w1
        cbuf[0:HALO_C, :] = cbuf[T:T + HALO_C, :]
        pooled = aux_ref[:, 2 * GROUP:3 * GROUP] / _pool_cnt(i, T) - ch
        aux_ref[:, 2 * GROUP:3 * GROUP] = pooled
        q = _dot(pooled.astype(BF16), pw_ref[...])
        cz = hs(8)
        mix_ref[:, 2 * GROUP:3 * GROUP] = (q * s256_ref[3:4, :] * (cz * _sig(cz))).astype(BF16)

        dv = hs(10)
        cen = dv - _rowmean(dv)
        var = _rowmean(cen * cen)
        v = cen * lax.rsqrt(var + LN_EPS) * s256_ref[4:5, :] + s256_ref[5:6, :]
        sps = []
        for n in range(T // SGU_BLOCK):
            vb = v[n * SGU_BLOCK:(n + 1) * SGU_BLOCK, :]
            sps.append(_dot(wm_s[...], _vstack(vb)) + sb_ref[...])
        sp = jnp.concatenate(sps, axis=0)
        dz = hs(11)
        mix_ref[:, 3 * GROUP:4 * GROUP] = (hs(9) * sp * (dz * _sig(dz))).astype(BF16)

        out = v1024_ref[0:1, :]
        for k in range(N_CHIPS):
            out = out + _dot(mix_ref[:, GROUP * k:GROUP * (k + 1)], wo_ref[k])
        z = alpha * x + out
        z_ref[...] = z
        cen = z - _rowmean(z)
        var = _rowmean(cen * cen)
        y = cen * lax.rsqrt(var + LN_EPS) * v1024_ref[1:2, :] + v1024_ref[2:3, :]
        if target is None:
            y_ref[...] = y
        else:
            t_ref, loss_ref = refs[13], refs[n_in + 6]

            @pl.when(i == 0)
            def _():
                loss_ref[...] = jnp.zeros_like(loss_ref)
            err = y - t_ref[...]
            y_ref[...] = err * (1.0 / D_MODEL)
            loss_ref[...] += jnp.sum(_colsum(err * err), axis=1, keepdims=True) * (0.5 / D_MODEL)

    def rows(width):
        return pl.BlockSpec((T, width), lambda i, l: (i, 0))

    consts = (wi, bin_, caw, cbw, s256, seg, pw, wm, sb, wo, v1024)
    in_specs = [rows(D_MODEL)] + [_whole(a) if a is wi or a is seg or a is wo else _of_layer(a) for a in consts]
    out_specs = [rows(D_MODEL), rows(D_MODEL), rows(IN_WIDTH), rows(3 * GROUP), rows(D_MODEL), rows(D_MODEL)]
    out_shape = [jax.ShapeDtypeStruct((S, D_MODEL), F32), jax.ShapeDtypeStruct((S, D_MODEL), BF16),
                 jax.ShapeDtypeStruct((S, IN_WIDTH), F32), jax.ShapeDtypeStruct((S, 3 * GROUP), F32),
                 jax.ShapeDtypeStruct((S, D_MODEL), BF16), jax.ShapeDtypeStruct((S, D_MODEL), F32)]
    scratch = [pltpu.VMEM((T + HALO_A, GROUP), F32), pltpu.VMEM((T + HALO_B, GROUP), F32),
               pltpu.VMEM((T + HALO_C, GROUP), F32), pltpu.VMEM((SGU_BLOCK, 4 * SGU_BLOCK), BF16),
               pltpu.VMEM((SUBLANES - 1, T + HALO_A - SUBLANES, GROUP), F32)]
    extra = ()
    if nxt is not None:
        extra = tuple(nxt)
        in_specs += [ANY, ANY]
        out_specs += [ANY, ANY]
        out_shape += [jax.ShapeDtypeStruct((N_CHIPS, D_MODEL, COLS), BF16),
                      jax.ShapeDtypeStruct((N_CHIPS, GROUP, D_MODEL), BF16)]
        scratch += [pltpu.SemaphoreType.DMA((N_GATHER_SEMS,)), pltpu.SemaphoreType.DMA((N_GATHER_SEMS,)),
                    pltpu.SemaphoreType.DMA((4,)), pltpu.VMEM((D_MODEL, COLS), BF16), pltpu.VMEM((GROUP, D_MODEL), BF16)]
    if target is not None:
        extra = (target,)
        in_specs += [rows(D_MODEL)]
        out_specs += [pl.BlockSpec((8, 128), lambda i, l: (0, 0))]
        out_shape += [jax.ShapeDtypeStruct((8, 128), F32)]
    grid_spec = pltpu.PrefetchScalarGridSpec(num_scalar_prefetch=1, grid=(nt,), in_specs=in_specs,
                                             out_specs=out_specs, scratch_shapes=scratch)
    return pl.pallas_call(
        body, name=("fwd_layer_loss" if target is not None else "fwd_layer") if nxt is None else "fwd_layer_gather",
        grid_spec=grid_spec, out_shape=out_shape,
        compiler_params=_vmem_params(dimension_semantics=("arbitrary",), **(
            dict(has_side_effects=True, collective_id=COLLECTIVE_ID["fwd_layer_gather"]) if nxt is not None else {})),
    )(larr, x, *consts, *extra)


ROW_CBW = 8
ROW_CAW = 16
ROW_LOSS = 7
ROW_PW = 48
ROW_LNG = 112
ROW_LNB = 116
ROW_BOUT = 120
ROW_BIN = 124
ROW_WC = 136
ROW_SB = 392
SM_ROWS = 400
N_DEV = 8


def _exchange_comm(start, finish, l, p_i, p_o, sm, r_i, r_o, r_sm, send_sems, recv_sems, loc_sem):
    x, y, c = _place()
    me = 4 * x + 2 * y + c
    chips = _other_chips(x, y)

    def rc(src, dst, sem, to):
        return pltpu.make_async_remote_copy(src_ref=src, dst_ref=dst, send_sem=send_sems.at[sem],
                                            recv_sem=recv_sems.at[sem], device_id=to, device_id_type=MESH)

    def big(r):
        px, py, pk = chips[r]
        to = (px, py, c)
        return [rc(p_i.at[l, pk], r_i.at[r, l], 2 * r, to), rc(p_o.at[l, pk], r_o.at[r, l], 2 * r + 1, to)]

    def peer(rel):
        px = 1 - x if rel & 4 else x
        py = 1 - y if rel & 2 else y
        pc = 1 - c if rel & 1 else c
        return (px, py, pc), 4 * px + 2 * py + pc

    def small_out(rel):
        to, _ = peer(rel)
        return rc(sm, r_sm.at[me], N_EXCH_SEMS - N_DEV + rel, to)

    def small_in(rel):
        to, idx = peer(rel)
        return rc(sm, r_sm.at[idx], N_EXCH_SEMS - N_DEV + rel, to)

    def local():
        return pltpu.make_async_copy(sm, r_sm.at[me], loc_sem.at[0])

    with_big, with_small = p_i is not None, sm is not None

    @pl.when(start)
    def _():
        _handshake(PEERS_ALL)
        if with_small:
            local().start()
        if with_big:
            for r in range(3):
                for cp in big(r):
                    cp.start()
        if with_small:
            for rel in range(1, N_DEV):
                small_out(rel).start()

    @pl.when(finish)
    def _():
        if with_big:
            for r in range(3):
                for cp in big(r):
                    cp.wait()
        if with_small:
            for rel in range(1, N_DEV):
                small_in(rel).wait_recv()
                small_out(rel).wait_send()
            local().wait()


RC = 32
RC_WIDE = 16
ACC_ROWS = 136


def _rsum8(v):
    r = v[0:8]
    for j in range(1, v.shape[0] // 8):
        r = r + v[8 * j:8 * j + 8]
    return r


def _bwd_layer(larr, dy, z, h, aux, wi, caw, cbw, s256, seg, pw, wm, wmt, sb, wo, v1024, e4, *, tile, exch=None):
    S = dy.shape[0]
    T = tile
    nt = S // T
    nblk = T // SGU_BLOCK
    alpha = float((2.0 * 4) ** 0.25)
    n_in = 17 + (5 if exch is not None else 0)
    n_out = 4 + (3 if exch is not None else 0)
    slab = pltpu.VMEM((T, GROUP), F32)
    scratch = dict(
        dbuf=pltpu.VMEM((T + HALO_A, GROUP), F32), ebuf=pltpu.VMEM((T + HALO_B, GROUP), F32),
        fbuf=pltpu.VMEM((T + HALO_C, GROUP), F32), sh=pltpu.VMEM((SUBLANES - 1, T + HALO_A - SUBLANES, GROUP), F32),
        wm_s=pltpu.VMEM((SGU_BLOCK, 4 * SGU_BLOCK), BF16), wmt_s=pltpu.VMEM((4 * SGU_BLOCK, SGU_BLOCK), BF16),
        dsp_acc=pltpu.VMEM((SGU_BLOCK, GROUP), F32), pw_acc=pltpu.VMEM((GROUP, GROUP), F32),
        acc_s=pltpu.VMEM((8 * ACC_ROWS, GROUP), F32), acc_w=pltpu.VMEM((24, D_MODEL), F32),
        dmix_s=pltpu.VMEM((T, D_MODEL), F32), vst_s=pltpu.VMEM((nblk, 4 * SGU_BLOCK, GROUP), BF16),
        dq_s=pltpu.VMEM((T, GROUP), BF16), dxt_s=pltpu.VMEM((D_MODEL, T), F32),
        mean_s=slab, t1_s=slab, t2_s=slab, q_s=slab, xv_s=slab, rv_s=slab, v_s=slab, sp_s=slab, a0_s=slab, sg_s=slab,
        xh_s=slab, ra_s=slab, ub_s=slab, dsp_s=slab, m1_s=slab, m2_s=slab, dpool_s=slab, dvd_s=slab, u_s=slab,
        du_s=slab, cw_s=slab)
    names = list(scratch)

    def body(*refs):
        (dy_ref, z_ref, h_ref, aux_ref, wi_ref, caw_ref, cbw_ref, s256_ref, seg_ref, pw_ref, wm_ref, wmt_ref,
         sb_ref, wo_ref, v1024_ref, e4_ref) = refs[1:17]
        dx_ref, dhb_ref, dzb_ref, osm_ref = refs[n_in:n_in + 4]
        k0 = n_in + n_out
        sc = dict(zip(names, refs[k0:k0 + len(names)]))
        dbuf, ebuf, fbuf, sh = sc["dbuf"], sc["ebuf"], sc["fbuf"], sc["sh"]
        wm_s, wmt_s, dsp_acc, pw_acc, acc_s, acc_w = (sc[n] for n in ("wm_s", "wmt_s", "dsp_acc", "pw_acc", "acc_s",
                                                                        "acc_w"))
        dmix_s, vst_s, dq_s = sc["dmix_s"], sc["vst_s"], sc["dq_s"]
        i = pl.program_id(0)
        tile_idx = nt - 1 - i
        if exch is not None:
            p_i, p_o, sm = refs[17:20]
            r_i, r_o, r_sm = refs[n_in + 4:n_in + 7]
            _exchange_comm(i == 0, i == nt - 1, refs[0][0] + 1, p_i, p_o, sm, r_i, r_o, r_sm, *refs[k0 + len(names):])

        @pl.when(i == 0)
        def _():
            dbuf[T:T + HALO_A, :] = jnp.zeros((HALO_A, GROUP), F32)
            ebuf[T:T + HALO_B, :] = jnp.zeros((HALO_B, GROUP), F32)
            fbuf[T:T + HALO_C, :] = jnp.zeros((HALO_C, GROUP), F32)
            _sgu_masks(wm_ref, wmt_ref, wm_s, wmt_s)
            osm_ref[...] = jnp.zeros_like(osm_ref)
            dsp_acc[...] = jnp.zeros_like(dsp_acc)
            pw_acc[...] = jnp.zeros_like(pw_acc)
            acc_s[...] = jnp.zeros_like(acc_s)
            acc_w[...] = jnp.zeros_like(acc_w)

        def chunks(rc, fn):
            for c in range(T // rc):
                fn(pl.ds(c * rc, rc))

        def hs(j, rows):
            return h_ref[rows, GROUP * j:GROUP * (j + 1)]

        def acc_add(row, val):
            acc_s[8 * row:8 * row + 8, :] += _rsum8(val)

        def put_dh(j, rows, val):
            acc_add(ROW_BIN + j, val)
            dhb_ref[rows, GROUP * j:GROUP * (j + 1)] = val.astype(BF16)

        def dsilu(v, s):
            return s * (1.0 + v * (1.0 - s))

        def vec(r):
            return s256_ref[r:r + 1, :]

        def ln_bwd(rows):
            dyc = dy_ref[rows, :]
            zc = z_ref[rows, :]
            cen = zc - _rowmean(zc)
            rstd = lax.rsqrt(_rowmean(cen * cen) + LN_EPS)
            xhat = cen * rstd
            acc_w[0:8, :] += _rsum8(dyc * xhat)
            acc_w[8:16, :] += _rsum8(dyc)
            gdy = dyc * v1024_ref[1:2, :]
            dz = rstd * (gdy - _rowmean(gdy) - xhat * _rowmean(gdy * xhat))
            acc_w[16:24, :] += _rsum8(dz)
            dzb_ref[rows, :] = dz.astype(BF16)
            dx_ref[rows, :] = alpha * dz
        chunks(RC_WIDE, ln_bwd)

        segm = seg_ref[...]
        dzb = dzb_ref[...]
        for k in range(N_CHIPS):
            dmix_s[:, GROUP * k:GROUP * (k + 1)] = _dot_nt(dzb, wo_ref[k])
        sc["mean_s"][...] = _segdot(aux_ref[:, 0:GROUP], segm)
        pooled_b = aux_ref[:, 2 * GROUP:3 * GROUP].astype(BF16)
        sc["q_s"][...] = _dot(pooled_b, pw_ref[...])

        def centre(rows):
            cen = aux_ref[rows, 0:GROUP] - sc["mean_s"][rows, :]
            sc["t1_s"][rows, :] = cen * cen
            dv_in = hs(10, rows)
            cen_v = dv_in - _rowmean(dv_in)
            rstd_v = lax.rsqrt(_rowmean(cen_v * cen_v) + LN_EPS)
            xv = cen_v * rstd_v
            sc["xv_s"][rows, :] = xv
            sc["rv_s"][rows, :] = jnp.broadcast_to(rstd_v, xv.shape)
            sc["v_s"][rows, :] = xv * vec(4) + vec(5)
        chunks(RC, centre)

        sc["t2_s"][...] = _segdot(sc["t1_s"][...], segm)
        for n in range(nblk):
            blk = slice(n * SGU_BLOCK, (n + 1) * SGU_BLOCK)
            vst_s[n] = _vstack(sc["v_s"][blk, :])
            sc["sp_s"][blk, :] = _dot(wm_s[...], vst_s[n]) + sb_ref[...]

        def mixers(rows):
            a_val, a_glu, a_z = hs(0, rows), hs(1, rows), hs(2, rows)
            sg = _sig(a_glu)
            sc["a0_s"][rows, :] = a_val * sg
            sc["sg_s"][rows, :] = sg
            rstd_a = lax.rsqrt(sc["t2_s"][rows, :] + LN_EPS)
            xh = (aux_ref[rows, 0:GROUP] - sc["mean_s"][rows, :]) * rstd_a
            a2 = xh * vec(1) + vec(2)
            s2 = _sig(a2)
            sz = _sig(a_z)
            dya = dmix_s[rows, 0:GROUP]
            put_dh(2, rows, dya * (a2 * s2) * dsilu(a_z, sz))
            d_a2 = dya * (a_z * sz) * dsilu(a2, s2)
            acc_add(1, d_a2 * xh)
            acc_add(2, d_a2)
            gd = d_a2 * vec(1)
            sc["t1_s"][rows, :] = gd
            sc["t2_s"][rows, :] = gd * xh
            sc["xh_s"][rows, :] = xh
            sc["ra_s"][rows, :] = rstd_a
            b_b, b_c, b_h, b_z = hs(3, rows), hs(4, rows), hs(5, rows), hs(6, rows)
            cb = aux_ref[rows, GROUP:2 * GROUP]
            sz = _sig(b_z)
            dyb = dmix_s[rows, GROUP:2 * GROUP]
            put_dh(3, rows, dyb * cb * (b_z * sz))
            put_dh(6, rows, dyb * b_b * cb * dsilu(b_z, sz))
            ebuf[rows, :] = dyb * b_b * (b_z * sz)
            sc["ub_s"][rows, :] = b_c * b_h
            c_z = hs(8, rows)
            q = sc["q_s"][rows, :]
            sz = _sig(c_z)
            dyc = dmix_s[rows, 2 * GROUP:3 * GROUP]
            acc_add(3, dyc * q * (c_z * sz))
            put_dh(8, rows, dyc * q * vec(3) * dsilu(c_z, sz))
            dq_s[rows, :] = (dyc * vec(3) * (c_z * sz)).astype(BF16)
            d_u, d_z = hs(9, rows), hs(11, rows)
            sp = sc["sp_s"][rows, :]
            sz = _sig(d_z)
            dyd = dmix_s[rows, 3 * GROUP:4 * GROUP]
            put_dh(9, rows, dyd * sp * (d_z * sz))
            put_dh(11, rows, dyd * d_u * sp * dsilu(d_z, sz))
            sc["dsp_s"][rows, :] = dyd * d_u * (d_z * sz)
        chunks(RC, mixers)

        sc["m1_s"][...] = _segdot(sc["t1_s"][...], segm)
        sc["m2_s"][...] = _segdot(sc["t2_s"][...], segm)
        d_q = dq_s[...]
        pw_acc[...] += _dot_tn(pooled_b, d_q)
        sc["dpool_s"][...] = _dot_nt(d_q, pw_ref[...])
        grp = _lane_group(GROUP)
        for n in range(nblk):
            blk = slice(n * SGU_BLOCK, (n + 1) * SGU_BLOCK)
            dspb = sc["dsp_s"][blk, :]
            dsp_acc[...] += dspb
            dspb16 = dspb.astype(BF16)
            dvst = _dot(wmt_s[...], dspb16)
            dvb = None
            for hh in range(4):
                part = jnp.where(grp == hh, dvst[hh * SGU_BLOCK:(hh + 1) * SGU_BLOCK, :], 0.0)
                dvb = part if dvb is None else dvb + part
            sc["dvd_s"][blk, :] = dvb
            dwc = _dot_nt(dspb16, vst_s[n])
            osm_ref[ROW_WC:ROW_WC + SGU_BLOCK, :] += dwc[:, 0:GROUP]
            osm_ref[ROW_WC + SGU_BLOCK:ROW_WC + 2 * SGU_BLOCK, :] += dwc[:, GROUP:2 * GROUP]

        def ln_sums(rows):
            xh = sc["xh_s"][rows, :]
            d_a1 = sc["ra_s"][rows, :] * (sc["t1_s"][rows, :] - sc["m1_s"][rows, :] - xh * sc["m2_s"][rows, :])
            acc_add(0, d_a1)
            dbuf[rows, :] = d_a1
            pos = tile_idx * T + rows.start + lax.broadcasted_iota(jnp.int32, (RC, GROUP), 0) + 1
            lane = lax.broadcasted_iota(jnp.int32, (RC, GROUP), 1) // HEAD
            win = jnp.where(lane == 0, 2, jnp.where(lane == 1, 4, jnp.where(lane == 2, 8, 16)))
            fbuf[rows, :] = sc["dpool_s"][rows, :] / jnp.minimum(pos, win).astype(F32)
            d_v = sc["dvd_s"][rows, :]
            xv = sc["xv_s"][rows, :]
            acc_add(4, d_v * xv)
            acc_add(5, d_v)
            gd = d_v * vec(4)
            put_dh(10, rows, sc["rv_s"][rows, :] * (gd - _rowmean(gd) - xv * _rowmean(gd * xv)))
        chunks(RC, ln_sums)

        span = T + HALO_A - SUBLANES
        for p in range(1, SUBLANES):
            sh[p - 1, :, :] = dbuf[p:p + span, :]

        for r0 in range(0, T, ROWS):
            uc = sc["ub_s"][r0:r0 + ROWS, :]
            acc = None
            for k in range(KB):
                off = (KB - 1) - k + r0
                w = ebuf[off:off + ROWS, :]
                term = cbw_ref[k:k + 1, :] * w
                acc = term if acc is None else acc + term
                acc_add(ROW_CBW + k, uc * w)
            sc["du_s"][r0:r0 + ROWS, :] = acc
        ebuf[T:T + HALO_B, :] = ebuf[0:HALO_B, :]

        hi_lane = (lax.broadcasted_iota(jnp.int32, (1, 128), 1) // HEAD) == 1
        for r0 in range(0, T, ROWS):
            def win(col, j0, j1):
                s = None
                for j in range(j0, j1):
                    term = fbuf[r0 + j:r0 + j + ROWS, 128 * col:128 * (col + 1)]
                    s = term if s is None else s + term
                return s
            sc["cw_s"][r0:r0 + ROWS, 0:128] = win(0, 0, 2) + jnp.where(hi_lane, win(0, 2, 4), 0.0)
            sc["cw_s"][r0:r0 + ROWS, 128:256] = win(1, 0, 8) + jnp.where(hi_lane, win(1, 8, 16), 0.0)
        fbuf[T:T + HALO_C, :] = fbuf[0:HALO_C, :]

        def rest_bc(rows):
            d_u = sc["du_s"][rows, :]
            put_dh(4, rows, d_u * hs(5, rows))
            put_dh(5, rows, d_u * hs(4, rows))
            put_dh(7, rows, sc["cw_s"][rows, :] - sc["dpool_s"][rows, :])
        chunks(RC, rest_bc)

        dxt_s = sc["dxt_s"]

        def dx_term(k):
            term = _dot_nt(wi_ref[k], dhb_ref[:, COLS * k:COLS * (k + 1)])
            if k == 1:
                dxt_s[...] = term
            else:
                dxt_s[...] += term

        def conv_a(rows):
            a0c = sc["a0_s"][rows, :]
            acc = None
            for k in range(KA):
                off = (KA - 1) - k
                p, q8 = off % SUBLANES, off - off % SUBLANES
                w = dbuf[pl.ds(rows.start + q8, RC), :] if p == 0 else sh[p - 1, pl.ds(rows.start + q8, RC), :]
                term = caw_ref[k:k + 1, :] * w
                acc = term if acc is None else acc + term
                acc_add(ROW_CAW + k, a0c * w)
            sc["u_s"][rows, :] = acc
        n_chunks = T // RC
        after = {(n_chunks * j) // 3: j + 1 for j in range(3)}
        for c in range(n_chunks):
            conv_a(pl.ds(c * RC, RC))
            if c in after:
                dx_term(after[c])
        dbuf[T:T + HALO_A, :] = dbuf[0:HALO_A, :]

        def rest_a(rows):
            d_a0 = sc["u_s"][rows, :]
            sg = sc["sg_s"][rows, :]
            put_dh(0, rows, d_a0 * sg)
            put_dh(1, rows, d_a0 * hs(0, rows) * sg * (1.0 - sg))
        chunks(RC, rest_a)
        dx_term(0)
        dx_ref[...] += dxt_s[...].T

        @pl.when(i == nt - 1)
        def _():
            for row in list(range(6)) + list(range(ROW_CBW, ROW_CBW + KB)) + list(range(ROW_CAW, ROW_CAW + KA)) + list(
                    range(ROW_BIN, ROW_BIN + N_SLICES)):
                osm_ref[row:row + 1, :] = _colsum(acc_s[8 * row:8 * row + 8, :])
            for j, row in enumerate((ROW_LNG, ROW_LNB, ROW_BOUT)):
                cs = _colsum(acc_w[8 * j:8 * j + 8, :])
                for q in range(D_MODEL // GROUP):
                    osm_ref[row + q:row + q + 1, :] = cs[:, GROUP * q:GROUP * (q + 1)]
            r = lax.broadcasted_iota(jnp.int32, (SGU_BLOCK, GROUP), 0) // CHUNK
            c = (lax.broadcasted_iota(jnp.int32, (SGU_BLOCK, GROUP), 1) % SGU_BLOCK) // CHUNK
            for half in range(2):
                rows_ = slice(ROW_WC + half * SGU_BLOCK, ROW_WC + (half + 1) * SGU_BLOCK)
                osm_ref[rows_, :] = jnp.where(c <= r, osm_ref[rows_, :], 0.0)
            sb_t = _segdot(dsp_acc[...], e4_ref[...]).T
            osm_ref[ROW_SB:ROW_SB + 8, 0:SGU_BLOCK] = sb_t[0:8, :]
            for g in range(4):
                osm_ref[ROW_PW:ROW_PW + HEAD, HEAD * g:HEAD * (g + 1)] = (
                    pw_acc[HEAD * g:HEAD * (g + 1), HEAD * g:HEAD * (g + 1)])

    def rows(width):
        return pl.BlockSpec((T, width), lambda i, l: (nt - 1 - i, 0))

    consts = (wi, caw, cbw, s256, seg, pw, wm, wmt, sb, wo, v1024, e4)
    unstacked = (wi, seg, wo, e4)
    in_specs = [rows(D_MODEL), rows(D_MODEL), rows(IN_WIDTH), rows(3 * GROUP)] + [
        _whole(a) if any(a is u for u in unstacked) else _of_layer(a) for a in consts]
    out_specs = [rows(D_MODEL), rows(IN_WIDTH), rows(D_MODEL), pl.BlockSpec((SM_ROWS, GROUP), lambda i, l: (0, 0))]
    out_shape = [jax.ShapeDtypeStruct((S, D_MODEL), F32), jax.ShapeDtypeStruct((S, IN_WIDTH), BF16),
                 jax.ShapeDtypeStruct((S, D_MODEL), BF16), jax.ShapeDtypeStruct((SM_ROWS, GROUP), F32)]
    scratch_shapes = list(scratch.values())
    extra, aliases = (), {}
    if exch is not None:
        extra = tuple(exch)
        r_i, r_o = exch[3], exch[4]
        in_specs += [ANY] * 5
        out_specs += [ANY] * 3
        out_shape += [jax.ShapeDtypeStruct(r_i.shape, r_i.dtype), jax.ShapeDtypeStruct(r_o.shape, r_o.dtype),
                      jax.ShapeDtypeStruct((N_DEV, SM_ROWS, GROUP), F32)]
        scratch_shapes += [pltpu.SemaphoreType.DMA((N_EXCH_SEMS,)), pltpu.SemaphoreType.DMA((N_EXCH_SEMS,)),
                           pltpu.SemaphoreType.DMA((1,))]
        aliases = {20: 4, 21: 5}
    grid_spec = pltpu.PrefetchScalarGridSpec(num_scalar_prefetch=1, grid=(nt,), in_specs=in_specs,
                                             out_specs=out_specs, scratch_shapes=scratch_shapes)
    return pl.pallas_call(
        body, name="bwd_layer" if exch is None else "bwd_layer_exchange",
        grid_spec=grid_spec, out_shape=out_shape, input_output_aliases=aliases,
        compiler_params=_vmem_params(dimension_semantics=("arbitrary",), **(
            dict(has_side_effects=True, collective_id=COLLECTIVE_ID["bwd_layer_exchange"]) if exch is not None else {})),
    )(larr, dy, z, h, aux, *consts, *extra)


def _dw_swap(cl_arr, xb, dhb, mixb, dzb, p_i, p_o, *, k_steps, last=None):
    S = xb.shape[0]
    tk = S // k_steps
    n_steps = N_CHIPS + k_steps
    hi, ho = p_i.shape[2], p_o.shape[2]
    n_in = 7 + (3 if last is not None else 0)
    n_out = 2 + (3 if last is not None else 0)

    def body(*refs):
        cl_ref, x_ref, dh_ref, mix_ref, dz_ref = refs[0:5]
        pi_ref, po_ref = refs[n_in:n_in + 2]
        own_i, acc_o, snd_i, snd_o, rcv_i, rcv_o, send_sems, recv_sems = refs[n_in + n_out:n_in + n_out + 8]
        j = pl.program_id(0)
        l = cl_ref[1]
        x, y, c = _place()
        mine_o, theirs_o = (pl.ds(pl.multiple_of(cc * ho, ho), ho) for cc in (c, 1 - c))

        def to_sibling(src, dst, sem):
            return pltpu.make_async_remote_copy(src_ref=src, dst_ref=dst, send_sem=send_sems.at[sem],
                                                recv_sem=recv_sems.at[sem], device_id=(x, y, 1 - c), device_id_type=MESH)

        def chunk_copy(k):
            return to_sibling(snd_i.at[k % 2], rcv_i.at[k], k)

        def out_copy():
            return to_sibling(snd_o, rcv_o, N_CHIPS)

        if last is None:
            @pl.when(j == 0)
            def _():
                _handshake(PEERS_SIBLING)
        else:
            qi_ref, qo_ref, r_sm = refs[n_in + 2:n_in + 5]
            out_sems, in_sems = refs[n_in + n_out + 8:n_in + n_out + 10]
            _exchange_comm(j == 0, j == n_steps - 1, None, None, None, refs[7], None, None, r_sm,
                           *refs[n_in + n_out + 10:])
            chips = _other_chips(x, y)

            def onward(r):
                px, py, pk = chips[r]
                return [pltpu.make_async_remote_copy(
                    src_ref=v.at[pk], dst_ref=q.at[r, l], send_sem=out_sems.at[2 * r + n], recv_sem=in_sems.at[2 * r + n],
                    device_id=(px, py, c), device_id_type=MESH) for n, (v, q) in enumerate(((rcv_i, qi_ref), (rcv_o, qo_ref)))]

        @pl.when(j < N_CHIPS)
        def _():
            @pl.when(j >= 2)
            def _():
                chunk_copy(j - 2).wait_send()

            acc = _dot_tn(x_ref[...], dh_ref[...])
            top, bottom = acc[:hi], acc[hi:]
            own_i[j % 2] = jnp.where(c == 0, top, bottom)
            snd_i[j % 2] = jnp.where(c == 0, bottom, top).astype(BF16)
            chunk_copy(j).start()

        @pl.when(j == N_CHIPS)
        def _():
            acc_o[...] = jnp.zeros_like(acc_o)

        @pl.when(j >= N_CHIPS)
        def _():
            acc_o[...] += _dot_tn(mix_ref[...], dz_ref[...]).reshape(N_CHIPS, GROUP, D_MODEL)

        @pl.when((j >= 1) & (j <= N_CHIPS))
        def _():
            chunk_copy(j - 1).wait_recv()
            summed = (own_i[(j - 1) % 2] + rcv_i[j - 1].astype(F32)).astype(pi_ref.dtype)
            pi_ref[...] = summed
            if last is not None:
                rcv_i[j - 1] = summed
                for r in range(3):
                    @pl.when(j - 1 == chips[r][2])
                    def _():
                        onward(r)[0].start()

        @pl.when(j == n_steps - 1)
        def _():
            snd_o[...] = acc_o[:, theirs_o, :].astype(BF16)
            out_copy().start()
            for k in (N_CHIPS - 2, N_CHIPS - 1):
                chunk_copy(k).wait_send()
            out_copy().wait_recv()
            summed = (acc_o[:, mine_o, :] + rcv_o[...].astype(F32)).astype(po_ref.dtype)
            po_ref[...] = summed
            if last is not None:
                rcv_o[...] = summed
                for r in range(3):
                    onward(r)[1].start()
            out_copy().wait_send()
            if last is not None:
                for r in range(3):
                    for cp in onward(r):
                        cp.wait()

    def col_block(j):
        return jnp.minimum(j, N_CHIPS - 1)

    def tok_block(j):
        return jnp.maximum(j - N_CHIPS, 0)

    in_specs = [pl.BlockSpec((S, D_MODEL), lambda j, cl: (0, 0)),
                pl.BlockSpec((S, COLS), lambda j, cl: (0, col_block(j))),
                pl.BlockSpec((tk, D_MODEL), lambda j, cl: (tok_block(j), 0)),
                pl.BlockSpec((tk, D_MODEL), lambda j, cl: (tok_block(j), 0)), ANY, ANY]
    out_specs = [pl.BlockSpec((None, None, hi, COLS), lambda j, cl: (cl[1], jnp.clip(j - 1, 0, N_CHIPS - 1), 0, 0)),
                 pl.BlockSpec((None, N_CHIPS, ho, D_MODEL), lambda j, cl: (cl[1], 0, 0, 0))]
    out_shape = [jax.ShapeDtypeStruct(p_i.shape, p_i.dtype), jax.ShapeDtypeStruct(p_o.shape, p_o.dtype)]
    scratch = [pltpu.VMEM((2, hi, COLS), F32), pltpu.VMEM((N_CHIPS, GROUP, D_MODEL), F32),
               pltpu.VMEM((2, hi, COLS), BF16), pltpu.VMEM((N_CHIPS, ho, D_MODEL), BF16),
               pltpu.VMEM((N_CHIPS, hi, COLS), BF16), pltpu.VMEM((N_CHIPS, ho, D_MODEL), BF16),
               pltpu.SemaphoreType.DMA((N_CHIPS + 1,)), pltpu.SemaphoreType.DMA((N_CHIPS + 1,))]
    extra, aliases, kind = (), {5: 0, 6: 1}, "dw_swap"
    if last is not None:
        extra, aliases, kind = tuple(last), {5: 0, 6: 1, 8: 2, 9: 3}, "dw_swap_exchange"
        in_specs += [ANY, ANY, ANY]
        out_specs += [ANY, ANY, ANY]
        out_shape += [jax.ShapeDtypeStruct(q.shape, q.dtype) for q in last[1:]]
        out_shape += [jax.ShapeDtypeStruct((N_DEV, SM_ROWS, GROUP), F32)]
        scratch += [pltpu.SemaphoreType.DMA((6,)), pltpu.SemaphoreType.DMA((6,)), pltpu.SemaphoreType.DMA((N_EXCH_SEMS,)),
                    pltpu.SemaphoreType.DMA((N_EXCH_SEMS,)), pltpu.SemaphoreType.DMA((1,))]
    grid_spec = pltpu.PrefetchScalarGridSpec(
        num_scalar_prefetch=1, grid=(n_steps,), in_specs=in_specs, out_specs=out_specs, scratch_shapes=scratch)
    return pl.pallas_call(
        body, name=kind, grid_spec=grid_spec, out_shape=out_shape, input_output_aliases=aliases,
        compiler_params=_vmem_params(dimension_semantics=("arbitrary",), has_side_effects=True,
                                     collective_id=COLLECTIVE_ID[kind]),
    )(cl_arr, xb, dhb, mixb, dzb, p_i, p_o, *extra)


def _adamw_math(w, g, m, v):
    nm = ADAM_B1 * m + (1.0 - ADAM_B1) * g
    nv = ADAM_B2 * v + (1.0 - ADAM_B2) * (g * g)
    c1 = 1.0 - ADAM_B1 ** ADAM_STEP
    c2 = 1.0 - ADAM_B2 ** ADAM_STEP
    return -ADAM_LR * ((nm / c1) / (jnp.sqrt(nv / c2) + ADAM_EPS) + ADAM_WD * w), nm, nv


def _adamw_small(ws, gs, ms, vs):
    n = len(ws)

    def body(*refs):
        for j in range(n):
            d, nm, nv = _adamw_math(*(refs[k * n + j][...] for k in range(4)))
            refs[4 * n + j][...] = d
            refs[5 * n + j][...] = nm
            refs[6 * n + j][...] = nv

    shapes = [jax.ShapeDtypeStruct(w.shape, F32) for w in ws]
    outs = pl.pallas_call(body, name="adamw_small", out_shape=shapes * 3, compiler_params=_vmem_params())(
        *ws, *gs, *ms, *vs)
    return outs[0:n], outs[n:2 * n], outs[2 * n:3 * n]


def _adamw(w, g, m, v, *, rows_per_step, name, copy_g=False):
    R, C = w.shape
    tr = rows_per_step

    def body(w_ref, g_ref, m_ref, v_ref, d_ref, nm_ref, nv_ref, *g_out):
        g_ = g_ref[...]
        d_ref[...], nm_ref[...], nv_ref[...] = _adamw_math(w_ref[...], g_, m_ref[...], v_ref[...])
        if copy_g:
            g_out[0][...] = g_

    spec = pl.BlockSpec((tr, C), lambda i: (i, 0))
    n_out = 4 if copy_g else 3
    return pl.pallas_call(
        body, name=name, grid=(R // tr,),
        in_specs=[spec] * 4, out_specs=[spec] * n_out,
        out_shape=[jax.ShapeDtypeStruct((R, C), F32)] * n_out,
        compiler_params=_vmem_params(dimension_semantics=("arbitrary",)),
    )(w, g, m, v)


def _gather_weights(wi16, wo16, cw):
    L = wi16.shape[0]
    hi_rows, ho_rows = D_MODEL // 2, GROUP // 2
    n_ici = 2 * L + 1
    n_fwd = 2 * L

    def body(wi_ref, wo_ref, cw_ref, *rest):
        wig = rest[0:L]
        wog = rest[L:2 * L]
        cwg = rest[2 * L]
        send_sems, recv_sems, loc_sems, vwi, vwo, vcw = rest[2 * L + 1:]
        x, y, c = _place()
        me_k = 2 * x + y
        sibling = (x, y, 1 - c)
        chips = _other_chips(x, y)

        def half_i(ref, blk):
            return ref.at[blk, pl.ds(c * hi_rows, hi_rows), :]

        def half_o(ref, blk):
            return ref.at[blk, pl.ds(c * ho_rows, ho_rows), :]

        def other_half_i(ref, blk):
            return ref.at[blk, pl.ds((1 - c) * hi_rows, hi_rows), :]

        def other_half_o(ref, blk):
            return ref.at[blk, pl.ds((1 - c) * ho_rows, ho_rows), :]

        stage_in = [pltpu.make_async_copy(wi_ref, vwi, loc_sems.at[0]), pltpu.make_async_copy(wo_ref, vwo, loc_sems.at[1]),
                    pltpu.make_async_copy(cw_ref, vcw, loc_sems.at[2])]
        local = []
        for l in range(L):
            local.append(pltpu.make_async_copy(vwi.at[l], wig[l].at[me_k], loc_sems.at[3 + 2 * l]))
            local.append(pltpu.make_async_copy(vwo.at[l], wog[l].at[me_k], loc_sems.at[3 + 2 * l + 1]))
        local.append(pltpu.make_async_copy(vcw, cwg.at[me_k], loc_sems.at[3 + 2 * L]))
        _handshake(PEERS_COLUMN)
        for cp in stage_in:
            cp.start()

        def remote(src, dst, sem, to):
            return pltpu.make_async_remote_copy(src_ref=src, dst_ref=dst, send_sem=send_sems.at[sem],
                                                recv_sem=recv_sems.at[sem], device_id=to, device_id_type=MESH)

        sends = []
        for r, (px, py, _) in enumerate(chips):
            to = (px, py, c)
            for l in range(L):
                sends.append(remote(half_i(wi_ref, l), half_i(wig[l], me_k), r * n_ici + 2 * l, to))
                sends.append(remote(half_o(wo_ref, l), half_o(wog[l], me_k), r * n_ici + 2 * l + 1, to))
            sends.append(remote(cw_ref, cwg.at[me_k], r * n_ici + 2 * L, to))
        for cp in sends:
            cp.start()
        for cp in stage_in:
            cp.wait()
        for cp in local:
            cp.start()

        base = 3 * n_ici
        fwds = []
        for r, (px, py, pk) in enumerate(chips):
            for l in range(L):
                remote(half_i(wig[l], pk), half_i(wig[l], pk), r * n_ici + 2 * l, sibling).wait_recv()
                f = remote(half_i(wig[l], pk), half_i(wig[l], pk), base + r * n_fwd + 2 * l, sibling)
                f.start()
                fwds.append(f)
                remote(half_o(wog[l], pk), half_o(wog[l], pk), r * n_ici + 2 * l + 1, sibling).wait_recv()
                f = remote(half_o(wog[l], pk), half_o(wog[l], pk), base + r * n_fwd + 2 * l + 1, sibling)
                f.start()
                fwds.append(f)
            remote(cwg.at[pk], cwg.at[pk], r * n_ici + 2 * L, sibling).wait_recv()
        for r, (px, py, pk) in enumerate(chips):
            for l in range(L):
                remote(other_half_i(wig[l], pk), other_half_i(wig[l], pk), base + r * n_fwd + 2 * l, sibling).wait_recv()
                remote(other_half_o(wog[l], pk), other_half_o(wog[l], pk), base + r * n_fwd + 2 * l + 1, sibling).wait_recv()
        for cp in sends + fwds:
            cp.wait_send()
        for cp in local:
            cp.wait()

    n_sem = 3 * n_ici + 3 * n_fwd
    out_shape = ([jax.ShapeDtypeStruct((N_CHIPS, D_MODEL, COLS), BF16)] * L
                 + [jax.ShapeDtypeStruct((N_CHIPS, GROUP, D_MODEL), BF16)] * L
                 + [jax.ShapeDtypeStruct((N_CHIPS,) + cw.shape, F32)])
    outs = pl.pallas_call(
        body, name="gather_weights",
        in_specs=[ANY, ANY, ANY], out_specs=[ANY] * (2 * L + 1), out_shape=out_shape,
        scratch_shapes=[pltpu.SemaphoreType.DMA((n_sem,)), pltpu.SemaphoreType.DMA((n_sem,)),
                        pltpu.SemaphoreType.DMA((2 * L + 4,)), pltpu.VMEM(wi16.shape, BF16), pltpu.VMEM(wo16.shape, BF16),
                        pltpu.VMEM(cw.shape, F32)],
        compiler_params=_vmem_params(has_side_effects=True, collective_id=COLLECTIVE_ID["gather_weights"]),
    )(wi16, wo16, cw)
    return outs[0:L], outs[L:2 * L], outs[2 * L]


def _sum_small(r_sms):
    L = len(r_sms)

    def body(*refs):
        o_ref = refs[L]
        for l in range(L):
            acc = refs[l][0]
            for d in range(1, N_DEV):
                acc = acc + refs[l][d]
            o_ref[l] = acc

    return pl.pallas_call(
        body, name="sum_small",
        out_shape=jax.ShapeDtypeStruct((L,) + r_sms[0].shape[1:], F32),
        compiler_params=_vmem_params(),
    )(*r_sms)


def _sum_share(kc_arr, p_i, q_i, p_o, q_o, *, nb):
    L = p_i.shape[0]
    n_steps, slots = L * nb, 2

    def body(kc_ref, pi_ref, a0, a1, a2, po_ref, b0, b1, b2, oi_ref, oo_ref, vi, vo, loc_sems, send_sems, recv_sems):
        del kc_ref
        x, y, c = _place()
        t = pl.program_id(0) * nb + pl.program_id(1)

        def copies(s):
            l, i = s // nb, s % nb
            out = []
            for j, (v, o) in enumerate(((vi, oi_ref), (vo, oo_ref))):
                tr = v.shape[1]
                src, dst = v.at[s % slots], o.at[l, pl.ds((c * nb + i) * tr, tr), :]
                out.append((pltpu.make_async_copy(src, dst, loc_sems.at[2 * s + j]),
                            pltpu.make_async_remote_copy(src_ref=src, dst_ref=dst, send_sem=send_sems.at[2 * s + j],
                                                         recv_sem=recv_sems.at[2 * s + j], device_id=(x, y, 1 - c),
                                                         device_id_type=MESH)))
            return out

        def sent(s):
            for mine, theirs in copies(s):
                mine.wait()
                theirs.wait_send()

        @pl.when(t == 0)
        def _():
            _handshake(PEERS_SIBLING)

        @pl.when(t >= slots)
        def _():
            sent(t - slots)

        f = lambda ref: ref[...].astype(F32)
        vi[t % slots] = ((f(pi_ref) + f(a0)) + f(a1)) + f(a2)
        vo[t % slots] = ((f(po_ref) + f(b0)) + f(b1)) + f(b2)
        for mine, theirs in copies(t):
            mine.start()
            theirs.start()

        @pl.when(t == n_steps - 1)
        def _():
            for s in range(n_steps - slots, n_steps):
                sent(s)
            for s in range(n_steps):
                for _, theirs in copies(s):
                    theirs.wait_recv()

    def specs(p):
        tr, cols = p.shape[2] // nb, p.shape[3]
        chunk = pl.BlockSpec((None, None, tr, cols), lambda l, i, kc: (l, kc[0], i, 0))
        got = [pl.BlockSpec((None, None, tr, cols), lambda l, i, kc, _j=j: (_j, l, i, 0)) for j in range(3)]
        return [chunk] + got, pltpu.VMEM((slots, tr, cols), F32)

    (in_i, v_i), (in_o, v_o) = specs(p_i), specs(p_o)
    grid_spec = pltpu.PrefetchScalarGridSpec(
        num_scalar_prefetch=1, grid=(L, nb), in_specs=in_i + in_o, out_specs=[ANY, ANY],
        scratch_shapes=[v_i, v_o] + [pltpu.SemaphoreType.DMA((2 * n_steps,))] * 3)
    return pl.pallas_call(
        body, name="sum_share", grid_spec=grid_spec,
        out_shape=[jax.ShapeDtypeStruct((L, 2 * p.shape[2], p.shape[3]), F32) for p in (p_i, p_o)],
        compiler_params=_vmem_params(dimension_semantics=("arbitrary",) * 2, has_side_effects=True,
                                     collective_id=COLLECTIVE_ID["sum_share"]),
    )(kc_arr, p_i, q_i, q_i, q_i, p_o, q_o, q_o, q_o)


WEIGHTS = ("ln_g", "ln_b", "w_in", "b_in", "conv_a_w", "conv_a_b", "norm_a_g", "norm_a_b", "conv_b_w", "pool_w",
           "pool_scale", "sgu_ln_g", "sgu_ln_b", "sgu_w", "sgu_bias", "w_out", "b_out")


def _pad_rows(a, rows):
    return jnp.pad(a, ((0, rows - a.shape[0]), (0, 0)))


def _indicator_consts():
    seg = jnp.where((jnp.arange(GROUP)[:, None] // HEAD) == (jnp.arange(GROUP)[None, :] // HEAD),
                    1.0 / HEAD, 0.0).astype(BF16)
    e4 = ((jnp.arange(GROUP)[:, None] // HEAD) == jnp.arange(128)[None, :]).astype(BF16)
    return seg, e4


def _layer_consts(p, conv_full):
    L = conv_full.shape[0]
    same_head = jnp.eye(4, dtype=F32)[:, None, :, None] > 0

    def rows_to(a, rows):
        return jnp.pad(a, ((0, 0), (0, rows - a.shape[1]), (0, 0)))

    s256 = jnp.stack([p[n] for n in ("conv_a_b", "norm_a_g", "norm_a_b", "pool_scale", "sgu_ln_g", "sgu_ln_b")], axis=1)
    pw = jnp.where(same_head, p["pool_w"][:, :, :, None, :], 0.0).reshape(L, GROUP, GROUP)
    return dict(
        caw=rows_to(conv_full[:, :KA], 32), cbw=rows_to(conv_full[:, KA:], 8), s256=rows_to(s256, 8),
        pw=pw.astype(BF16),
        wm=jnp.transpose(p["sgu_w"], (0, 2, 1, 3)).reshape(L, SGU_BLOCK, 4 * SGU_BLOCK),
        wmt=jnp.transpose(p["sgu_w"], (0, 1, 3, 2)).reshape(L, 4 * SGU_BLOCK, SGU_BLOCK),
        sb=jnp.repeat(jnp.transpose(p["sgu_bias"], (0, 2, 1)), HEAD, axis=2),
        v1024=rows_to(jnp.stack([p["b_out"], p["ln_g"], p["ln_b"]], axis=1), 8),
        bin=p["b_in"][:, None, :])


def _unpack_small(sm):
    L = sm.shape[0]
    owc = jnp.concatenate([sm[:, ROW_WC:ROW_WC + SGU_BLOCK], sm[:, ROW_WC + SGU_BLOCK:ROW_WC + 2 * SGU_BLOCK]], axis=2)
    return dict(
        conv_a_b=sm[:, 0], norm_a_g=sm[:, 1], norm_a_b=sm[:, 2], pool_scale=sm[:, 3], sgu_ln_g=sm[:, 4],
        sgu_ln_b=sm[:, 5], conv_b_w=sm[:, ROW_CBW:ROW_CBW + KB], conv_a_w=sm[:, ROW_CAW:ROW_CAW + KA],
        pool_w=jnp.transpose(sm[:, ROW_PW:ROW_PW + HEAD].reshape(L, HEAD, 4, HEAD), (0, 2, 1, 3)),
        ln_g=sm[:, ROW_LNG:ROW_LNG + 4].reshape(L, D_MODEL), ln_b=sm[:, ROW_LNB:ROW_LNB + 4].reshape(L, D_MODEL),
        b_out=sm[:, ROW_BOUT:ROW_BOUT + 4].reshape(L, D_MODEL),
        b_in=sm[:, ROW_BIN:ROW_BIN + N_SLICES].reshape(L, IN_WIDTH),
        sgu_w=jnp.transpose(owc.reshape(L, SGU_BLOCK, 4, SGU_BLOCK), (0, 2, 1, 3)),
        sgu_bias=sm[:, ROW_SB:ROW_SB + 4, 0:SGU_BLOCK])


def _step(p, m, v, x, target, *, tile_f, tile_b, k_steps):
    L = p["ln_g"].shape[0]
    xi, yi, ci = _place()
    me_k = 2 * xi + yi
    hi_rows, ho_rows = D_MODEL // 2, GROUP // 2

    cw = jnp.concatenate([p["conv_a_w"], p["conv_b_w"]], axis=1).reshape(-1, 128)
    cw_rows = cw.shape[0]
    cw = _pad_rows(cw, -(-cw_rows // SUBLANES) * SUBLANES)
    wi16 = p["w_in"].astype(BF16)
    wo16 = p["w_out"].astype(BF16)
    wig0, wog0, cwg = _gather_weights(wi16[0:1], wo16[0:1], cw)
    cwg = cwg[:, :cw_rows].reshape(N_CHIPS, L, KA + KB, HEAD)
    conv_full = jnp.transpose(cwg, (1, 2, 0, 3)).reshape(L, KA + KB, GROUP)
    seg, e4 = _indicator_consts()
    k = _layer_consts(p, conv_full)
    layer = [jnp.full((1,), l, jnp.int32) for l in range(L)]

    hcur = x
    saved, wig, wog = [], [wig0[0]], [wog0[0]]
    for l in range(L):
        nxt = (wi16, wo16) if l + 1 < L else None
        outs = _fwd_layer(layer[l], hcur, wig[l], k["bin"], k["caw"], k["cbw"], k["s256"], seg, k["pw"], k["wm"], k["sb"],
                          wog[l], k["v1024"], tile=tile_f, nxt=nxt, target=None if nxt is not None else target)
        y, xb, h, aux, mixb, z = outs[0:6]
        if nxt is not None:
            wig.append(outs[6])
            wog.append(outs[7])
        saved.append((xb, h, aux, mixb, z))
        hcur = y

    dy = hcur
    loss_local = outs[6][0, 0]

    p_i = lax.empty((L, N_CHIPS, hi_rows, COLS), BF16)
    p_o = lax.empty((L, N_CHIPS, ho_rows, D_MODEL), BF16)
    q_i = lax.empty((3, L, hi_rows, COLS), BF16)
    q_o = lax.empty((3, L, ho_rows, D_MODEL), BF16)
    r_sm = [None] * L
    pending = None
    for l in reversed(range(L)):
        xb, h, aux, mixb, z = saved[l]
        exch = None if pending is None else (p_i, p_o, pending, q_i, q_o)
        outs = _bwd_layer(layer[l], dy, z, h, aux, wig[l], k["caw"], k["cbw"], k["s256"], seg, k["pw"], k["wm"],
                          k["wmt"], k["sb"], wog[l], k["v1024"], e4, tile=tile_b, exch=exch)
        dy, dhb, dzb, osm = outs[0:4]
        if l == L - 1:
            osm = osm.at[ROW_LOSS, 0].set(loss_local)
        if exch is not None:
            q_i, q_o, r_sm[l + 1] = outs[4:7]
        cl_arr = jnp.stack([ci, jnp.int32(l)]).astype(jnp.int32)
        if l > 0:
            p_i, p_o = _dw_swap(cl_arr, xb, dhb, mixb, dzb, p_i, p_o, k_steps=k_steps)
        else:
            p_i, p_o, q_i, q_o, r_sm[0] = _dw_swap(cl_arr, xb, dhb, mixb, dzb, p_i, p_o, k_steps=k_steps,
                                                   last=(osm, q_i, q_o))
        pending = osm
    grad_x = dy

    summed = _sum_small(r_sm)
    loss = summed[L - 1, ROW_LOSS, 0]
    grads = _unpack_small(summed)
    for n in ("conv_a_w", "conv_b_w"):
        grads[n] = lax.dynamic_slice_in_dim(grads[n], me_k * HEAD, HEAD, axis=2)

    kc_arr = jnp.stack([me_k, ci]).astype(jnp.int32)
    g_i, g_o = _sum_share(kc_arr, p_i, q_i, p_o, q_o, nb=2)
    grads["w_in"] = g_i
    grads["w_out"] = g_o

    delta, new_m, new_v = {}, {}, {}
    for n, tr in (("w_in", 512), ("w_out", 256)):
        shp = p[n].shape
        args = [a.reshape(shp[0] * shp[1], shp[2]) for a in (p[n], grads[n], m[n], v[n])]
        outs = _adamw(*args, rows_per_step=tr, name="adamw_" + n, copy_g=True)
        delta[n], new_m[n], new_v[n], grads[n] = (a.reshape(shp) for a in outs)
    small = [n for n in WEIGHTS if n not in ("w_in", "w_out")]
    flat = [[a[n].reshape(-1, a[n].shape[-1]) for n in small] for a in (p, grads, m, v)]
    outs = _adamw_small(*flat)
    for j, n in enumerate(small):
        delta[n], new_m[n], new_v[n] = (o[j].reshape(p[n].shape) for o in outs)

    return (loss, grad_x[None], *[grads[n] for n in WEIGHTS], *[delta[n] for n in WEIGHTS],
            *[new_m[n] for n in WEIGHTS], *[new_v[n] for n in WEIGHTS])


def kernel(x, ln_g, ln_b, w_in, b_in, conv_a_w, conv_a_b, norm_a_g, norm_a_b, conv_b_w, pool_w, pool_scale, sgu_ln_g, sgu_ln_b, sgu_w, sgu_bias, w_out, b_out, loss_target, m_ln_g, m_ln_b, m_w_in, m_b_in, m_conv_a_w, m_conv_a_b, m_norm_a_g, m_norm_a_b, m_conv_b_w, m_pool_w, m_pool_scale, m_sgu_ln_g, m_sgu_ln_b, m_sgu_w, m_sgu_bias, m_w_out, m_b_out, v_ln_g, v_ln_b, v_w_in, v_b_in, v_conv_a_w, v_conv_a_b, v_norm_a_g, v_norm_a_b, v_conv_b_w, v_pool_w, v_pool_scale, v_sgu_ln_g, v_sgu_ln_b, v_sgu_w, v_sgu_bias, v_w_out, v_b_out):
    p = dict(ln_g=ln_g, ln_b=ln_b, w_in=w_in, b_in=b_in, conv_a_w=conv_a_w, conv_a_b=conv_a_b, norm_a_g=norm_a_g,
             norm_a_b=norm_a_b, conv_b_w=conv_b_w, pool_w=pool_w, pool_scale=pool_scale, sgu_ln_g=sgu_ln_g,
             sgu_ln_b=sgu_ln_b, sgu_w=sgu_w, sgu_bias=sgu_bias, w_out=w_out, b_out=b_out)
    m = dict(ln_g=m_ln_g, ln_b=m_ln_b, w_in=m_w_in, b_in=m_b_in, conv_a_w=m_conv_a_w, conv_a_b=m_conv_a_b,
             norm_a_g=m_norm_a_g, norm_a_b=m_norm_a_b, conv_b_w=m_conv_b_w, pool_w=m_pool_w, pool_scale=m_pool_scale,
             sgu_ln_g=m_sgu_ln_g, sgu_ln_b=m_sgu_ln_b, sgu_w=m_sgu_w, sgu_bias=m_sgu_bias, w_out=m_w_out, b_out=m_b_out)
    v = dict(ln_g=v_ln_g, ln_b=v_ln_b, w_in=v_w_in, b_in=v_b_in, conv_a_w=v_conv_a_w, conv_a_b=v_conv_a_b,
             norm_a_g=v_norm_a_g, norm_a_b=v_norm_a_b, conv_b_w=v_conv_b_w, pool_w=v_pool_w, pool_scale=v_pool_scale,
             sgu_ln_g=v_sgu_ln_g, sgu_ln_b=v_sgu_ln_b, sgu_w=v_sgu_w, sgu_bias=v_sgu_bias, w_out=v_w_out, b_out=v_b_out)
    return _step(p, m, v, x[0], loss_target[0], tile_f=256, tile_b=256, k_steps=4)
```

```python
import jax
import jax.numpy as jnp
from jax import lax
from jax.experimental import pallas as pl
from jax.experimental.pallas import tpu as pltpu

F32 = jnp.float32
BF16 = jnp.bfloat16
MESH = pl.DeviceIdType.MESH

D_MODEL = 1024
GROUP = 256
HEAD = 64
N_SLICES = 12
IN_WIDTH = N_SLICES * GROUP
N_CHIPS = 4
COLS = IN_WIDTH // N_CHIPS
KA = 31
KB = 3
SUBLANES = 8
HALO_A, HALO_B, HALO_C = 32, 8, 16
N_GATHER_SEMS = 12
N_EXCH_SEMS = 10
SGU_BLOCK = 128
CHUNK = 64
LN_EPS = 1e-5
ROWS = 64
V7X_VMEM_BYTES = 64 * 1024 * 1024
VMEM_LIMIT = V7X_VMEM_BYTES - 8 * 1024 * 1024

ADAM_LR, ADAM_B1, ADAM_B2, ADAM_EPS, ADAM_WD, ADAM_STEP = 0.001, 0.9, 0.999, 1e-08, 0.01, 10


ANY = pl.BlockSpec(memory_space=pl.ANY)


def _vmem_params(**kw):
    return pltpu.CompilerParams(vmem_limit_bytes=VMEM_LIMIT, **kw)


def _whole(a):
    return pl.BlockSpec(a.shape, lambda i, l, _n=a.ndim: (0,) * _n)


def _of_layer(a):
    return pl.BlockSpec((None,) + a.shape[1:], lambda i, l, _n=a.ndim: (l[0],) + (0,) * (_n - 1))


def _place():
    return lax.axis_index("x"), lax.axis_index("y"), lax.axis_index("c")


def _other_chips(x, y):
    return [(1 - x, y, 2 * (1 - x) + y), (x, 1 - y, 2 * x + (1 - y)), (1 - x, 1 - y, 2 * (1 - x) + (1 - y))]


PEERS_SIBLING, PEERS_COLUMN, PEERS_ALL = "sibling", "sibling and the same core of the other chips", "all"
COLLECTIVE_ID = dict(sum_share=0, dw_swap=1, gather_weights=2, fwd_layer_gather=3, bwd_layer_exchange=4,
                     dw_swap_exchange=5)


def _handshake(peers):
    x, y, c = _place()
    if peers == PEERS_SIBLING:
        ids = [(x, y, 1 - c)]
    elif peers == PEERS_COLUMN:
        ids = [(x, y, 1 - c)] + [(px, py, c) for px, py, _ in _other_chips(x, y)]
    else:
        ids = [(1 - x if r & 4 else x, 1 - y if r & 2 else y, 1 - c if r & 1 else c) for r in range(1, 8)]
    barrier = pltpu.get_barrier_semaphore()
    for to in ids:
        pl.semaphore_signal(barrier, inc=1, device_id=to, device_id_type=MESH)
    pl.semaphore_wait(barrier, len(ids))


def _sig(v):
    return 0.5 * jnp.tanh(0.5 * v) + 0.5


def _dot(a, b):
    return jnp.dot(a, b, preferred_element_type=F32)


def _dot_nt(a, b):
    return lax.dot_general(a, b, (((1,), (1,)), ((), ())), preferred_element_type=F32)


def _dot_tn(a, b):
    return lax.dot_general(a, b, (((0,), (0,)), ((), ())), preferred_element_type=F32)


def _segdot(v, m):
    hi = v.astype(BF16)
    lo = (v - hi.astype(F32)).astype(BF16)
    return _dot(hi, m) + _dot(lo, m)


def _colsum(v):
    return jnp.sum(v, axis=0, keepdims=True)


def _rowmean(v):
    return jnp.mean(v, axis=-1, keepdims=True)


def _lane_group(n):
    return lax.broadcasted_iota(jnp.int32, (1, n), 1) // HEAD


def _pool_cnt(tile, t_rows):
    pos = tile * t_rows + lax.broadcasted_iota(jnp.int32, (t_rows, GROUP), 0) + 1
    grp = lax.broadcasted_iota(jnp.int32, (t_rows, GROUP), 1) // HEAD
    win = jnp.where(grp == 0, 2, jnp.where(grp == 1, 4, jnp.where(grp == 2, 8, 16)))
    return jnp.minimum(pos, win).astype(F32)


def _sgu_masks(wm_ref, wmt_ref, wm_s, wmt_s):
    r = lax.broadcasted_iota(jnp.int32, (SGU_BLOCK, 4 * SGU_BLOCK), 0) // CHUNK
    c = (lax.broadcasted_iota(jnp.int32, (SGU_BLOCK, 4 * SGU_BLOCK), 1) % SGU_BLOCK) // CHUNK
    wm_s[...] = jnp.where(c <= r, wm_ref[...], 0.0).astype(BF16)
    if wmt_ref is not None:
        rt = (lax.broadcasted_iota(jnp.int32, (4 * SGU_BLOCK, SGU_BLOCK), 0) % SGU_BLOCK) // CHUNK
        ct = lax.broadcasted_iota(jnp.int32, (4 * SGU_BLOCK, SGU_BLOCK), 1) // CHUNK
        wmt_s[...] = jnp.where(rt <= ct, wmt_ref[...], 0.0).astype(BF16)


def _vstack(v_blk):
    grp = _lane_group(GROUP)
    return jnp.concatenate([jnp.where(grp == h, v_blk, 0.0) for h in range(4)], axis=0).astype(BF16)


def _gather_next(step, nt, nwi, nwo, gwi, gwo, send_sems, recv_sems, loc_sems, vwi, vwo):
    x, y, c = _place()
    me_k = 2 * x + y
    sibling = (x, y, 1 - c)
    chips = _other_chips(x, y)
    hi, ho = D_MODEL // 2, GROUP // 2
    fwd_sems = N_GATHER_SEMS // 2

    def rc(src, dst, sem, to):
        return pltpu.make_async_remote_copy(src_ref=src, dst_ref=dst, send_sem=send_sems.at[sem],
                                            recv_sem=recv_sems.at[sem], device_id=to, device_id_type=MESH)

    def blk(ref, k, n, cc):
        return ref.at[k, pl.ds(cc * n, n), :]

    def ici(r):
        px, py, _ = chips[r]
        to = (px, py, c)
        return [rc(nwi.at[pl.ds(c * hi, hi), :], blk(gwi, me_k, hi, c), 2 * r, to),
                rc(nwo.at[pl.ds(c * ho, ho), :], blk(gwo, me_k, ho, c), 2 * r + 1, to)]

    def landed(r, cc, base):
        pk = chips[r][2]
        return [rc(blk(gwi, pk, hi, cc), blk(gwi, pk, hi, cc), base + 2 * r, sibling),
                rc(blk(gwo, pk, ho, cc), blk(gwo, pk, ho, cc), base + 2 * r + 1, sibling)]

    def stage_in():
        return [pltpu.make_async_copy(nwi, vwi, loc_sems.at[0]), pltpu.make_async_copy(nwo, vwo, loc_sems.at[1])]

    def local():
        return [pltpu.make_async_copy(vwi, gwi.at[me_k], loc_sems.at[2]),
                pltpu.make_async_copy(vwo, gwo.at[me_k], loc_sems.at[3])]

    @pl.when(step == 0)
    def _():
        _handshake(PEERS_COLUMN)
        for cp in stage_in():
            cp.start()
        for r in range(3):
            for cp in ici(r):
                cp.start()

    @pl.when(step == 1)
    def _():
        for cp in stage_in():
            cp.wait()
        for cp in local():
            cp.start()

    @pl.when(step == (3 * nt) // 4)
    def _():
        for r in range(3):
            for got, fwd in zip(landed(r, c, 0), landed(r, c, fwd_sems)):
                got.wait_recv()
                fwd.start()

    @pl.when(step == nt - 1)
    def _():
        for r in range(3):
            for got in landed(r, 1 - c, fwd_sems):
                got.wait_recv()
        for r in range(3):
            for cp in ici(r) + landed(r, c, fwd_sems):
                cp.wait_send()
        for cp in local():
            cp.wait()


def _fwd_layer(larr, x, wi, bin_, caw, cbw, s256, seg, pw, wm, sb, wo, v1024, *, tile, nxt=None, target=None):
    assert nxt is None or target is None
    S = x.shape[0]
    T = tile
    nt = S // T
    alpha = float((2.0 * 4) ** 0.25)
    n_in = 13 + (2 if nxt is not None else 0) + (1 if target is not None else 0)
    n_out = 6 + (2 if nxt is not None else 0) + (1 if target is not None else 0)

    def body(*refs):
        l_ref = refs[0]
        (x_ref, wi_ref, bin_ref, caw_ref, cbw_ref, s256_ref, seg_ref, pw_ref, wm_ref, sb_ref, wo_ref,
         v1024_ref) = refs[1:13]
        y_ref, xb_ref, h_ref, aux_ref, mix_ref, z_ref = refs[n_in:n_in + 6]
        abuf, bbuf, cbuf, wm_s, shf = refs[n_in + n_out:n_in + n_out + 5]
        i = pl.program_id(0)
        if nxt is not None:
            _gather_next(i, nt, refs[13].at[l_ref[0] + 1], refs[14].at[l_ref[0] + 1], refs[n_in + 6], refs[n_in + 7],
                         *refs[n_in + n_out + 5:])

        @pl.when(i == 0)
        def _():
            abuf[0:HALO_A, :] = jnp.zeros((HALO_A, GROUP), F32)
            bbuf[0:HALO_B, :] = jnp.zeros((HALO_B, GROUP), F32)
            cbuf[0:HALO_C, :] = jnp.zeros((HALO_C, GROUP), F32)
            _sgu_masks(wm_ref, None, wm_s, None)

        x = x_ref[...]
        xb = x.astype(BF16)
        xb_ref[...] = xb
        for k in range(N_CHIPS):
            h_ref[:, COLS * k:COLS * (k + 1)] = _dot(xb, wi_ref[k]) + bin_ref[:, COLS * k:COLS * (k + 1)]

        def hs(j):
            return h_ref[:, GROUP * j:GROUP * (j + 1)]

        abuf[HALO_A:HALO_A + T, :] = hs(0) * _sig(hs(1))
        span = T + HALO_A - SUBLANES
        for p in range(1, SUBLANES):
            shf[p - 1, :, :] = abuf[p:p + span, :]
        for r0 in range(0, T, ROWS):
            acc = None
            for k in range(KA):
                off = HALO_A - (KA - 1) + k
                p, q8 = off % SUBLANES, off - off % SUBLANES
                win = abuf[r0 + q8:r0 + q8 + ROWS, :] if p == 0 else shf[p - 1, r0 + q8:r0 + q8 + ROWS, :]
                term = caw_ref[k:k + 1, :] * win
                acc = term if acc is None else acc + term
            aux_ref[r0:r0 + ROWS, 0:GROUP] = acc + s256_ref[0:1, :]
        abuf[0:HALO_A, :] = abuf[T:T + HALO_A, :]
        a1 = aux_ref[:, 0:GROUP]
        segm = seg_ref[...]
        cen = a1 - _segdot(a1, segm)
        var = _segdot(cen * cen, segm)
        a2 = cen * lax.rsqrt(var + LN_EPS) * s256_ref[1:2, :] + s256_ref[2:3, :]
        az = hs(2)
        mix_ref[:, 0:GROUP] = (a2 * _sig(a2) * (az * _sig(az))).astype(BF16)

        bbuf[HALO_B:HALO_B + T, :] = hs(4) * hs(5)
        for r0 in range(0, T, ROWS):
            acc = None
            for k in range(KB):
                off = HALO_B - (KB - 1) + k + r0
                term = cbw_ref[k:k + 1, :] * bbuf[off:off + ROWS, :]
                acc = term if acc is None else acc + term
            aux_ref[r0:r0 + ROWS, GROUP:2 * GROUP] = acc
        bbuf[0:HALO_B, :] = bbuf[T:T + HALO_B, :]
        bz = hs(6)
        mix_ref[:, GROUP:2 * GROUP] = (hs(3) * aux_ref[:, GROUP:2 * GROUP] * (bz * _sig(bz))).astype(BF16)

        ch = hs(7)
        cbuf[HALO_C:HALO_C + T, :] = ch
        hi_lane = (lax.broadcasted_iota(jnp.int32, (1, 128), 1) // HEAD) == 1
        for r0 in range(0, T, ROWS):
            def win(col, j0, j1):
                s = None
                for j in range(j0, j1):
                    off = HALO_C - j + r0
                    term = cbuf[off:off + ROWS, 128 * col:128 * (col + 1)]
                    s = term if s is None else s + term
                return s
            w0 = win(0, 0, 2) + jnp.where(hi_lane, win(0, 2, 4), 0.0)
            w1 = win(1, 0, 8) + jnp.where(hi_lane, win(1, 8, 16), 0.0)
            aux_ref[r0:r0 + ROWS, 2 * GROUP:2 * GROUP + 128] = w0
            aux_ref[r0:r0 + ROWS, 2 * GROUP + 128:3 * GROUP] = w1
        cbuf[0:HALO_C, :] = cbuf[T:T + HALO_C, :]
        pooled = aux_ref[:, 2 * GROUP:3 * GROUP] / _pool_cnt(i, T) - ch
        aux_ref[:, 2 * GROUP:3 * GROUP] = pooled
        q = _dot(pooled.astype(BF16), pw_ref[...])
        cz = hs(8)
        mix_ref[:, 2 * GROUP:3 * GROUP] = (q * s256_ref[3:4, :] * (cz * _sig(cz))).astype(BF16)

        dv = hs(10)
        cen = dv - _rowmean(dv)
        var = _rowmean(cen * cen)
        v = cen * lax.rsqrt(var + LN_EPS) * s256_ref[4:5, :] + s256_ref[5:6, :]
        sps = []
        for n in range(T // SGU_BLOCK):
            vb = v[n * SGU_BLOCK:(n + 1) * SGU_BLOCK, :]
            sps.append(_dot(wm_s[...], _vstack(vb)) + sb_ref[...])
        sp = jnp.concatenate(sps, axis=0)
        dz = hs(11)
        mix_ref[:, 3 * GROUP:4 * GROUP] = (hs(9) * sp * (dz * _sig(dz))).astype(BF16)

        out = v1024_ref[0:1, :]
        for k in range(N_CHIPS):
            out = out + _dot(mix_ref[:, GROUP * k:GROUP * (k + 1)], wo_ref[k])
        z = alpha * x + out
        z_ref[...] = z
        cen = z - _rowmean(z)
        var = _rowmean(cen * cen)
        y = cen * lax.rsqrt(var + LN_EPS) * v1024_ref[1:2, :] + v1024_ref[2:3, :]
        if target is None:
            y_ref[...] = y
        else:
            t_ref, loss_ref = refs[13], refs[n_in + 6]

            @pl.when(i == 0)
            def _():
                loss_ref[...] = jnp.zeros_like(loss_ref)
            err = y - t_ref[...]
            y_ref[...] = err * (1.0 / D_MODEL)
            loss_ref[...] += jnp.sum(_colsum(err * err), axis=1, keepdims=True) * (0.5 / D_MODEL)

    def rows(width):
        return pl.BlockSpec((T, width), lambda i, l: (i, 0))

    consts = (wi, bin_, caw, cbw, s256, seg, pw, wm, sb, wo, v1024)
    in_specs = [rows(D_MODEL)] + [_whole(a) if a is wi or a is seg or a is wo else _of_layer(a) for a in consts]
    out_specs = [rows(D_MODEL), rows(D_MODEL), rows(IN_WIDTH), rows(3 * GROUP), rows(D_MODEL), rows(D_MODEL)]
    out_shape = [jax.ShapeDtypeStruct((S, D_MODEL), F32), jax.ShapeDtypeStruct((S, D_MODEL), BF16),
                 jax.ShapeDtypeStruct((S, IN_WIDTH), F32), jax.ShapeDtypeStruct((S, 3 * GROUP), F32),
                 jax.ShapeDtypeStruct((S, D_MODEL), BF16), jax.ShapeDtypeStruct((S, D_MODEL), F32)]
    scratch = [pltpu.VMEM((T + HALO_A, GROUP), F32), pltpu.VMEM((T + HALO_B, GROUP), F32),
               pltpu.VMEM((T + HALO_C, GROUP), F32), pltpu.VMEM((SGU_BLOCK, 4 * SGU_BLOCK), BF16),
               pltpu.VMEM((SUBLANES - 1, T + HALO_A - SUBLANES, GROUP), F32)]
    extra = ()
    if nxt is not None:
        extra = tuple(nxt)
        in_specs += [ANY, ANY]
        out_specs += [ANY, ANY]
        out_shape += [jax.ShapeDtypeStruct((N_CHIPS, D_MODEL, COLS), BF16),
                      jax.ShapeDtypeStruct((N_CHIPS, GROUP, D_MODEL), BF16)]
        scratch += [pltpu.SemaphoreType.DMA((N_GATHER_SEMS,)), pltpu.SemaphoreType.DMA((N_GATHER_SEMS,)),
                    pltpu.SemaphoreType.DMA((4,)), pltpu.VMEM((D_MODEL, COLS), BF16), pltpu.VMEM((GROUP, D_MODEL), BF16)]
    if target is not None:
        extra = (target,)
        in_specs += [rows(D_MODEL)]
        out_specs += [pl.BlockSpec((8, 128), lambda i, l: (0, 0))]
        out_shape += [jax.ShapeDtypeStruct((8, 128), F32)]
    grid_spec = pltpu.PrefetchScalarGridSpec(num_scalar_prefetch=1, grid=(nt,), in_specs=in_specs,
                                             out_specs=out_specs, scratch_shapes=scratch)
    return pl.pallas_call(
        body, name=("fwd_layer_loss" if target is not None else "fwd_layer") if nxt is None else "fwd_layer_gather",
        grid_spec=grid_spec, out_shape=out_shape,
        compiler_params=_vmem_params(dimension_semantics=("arbitrary",), **(
            dict(has_side_effects=True, collective_id=COLLECTIVE_ID["fwd_layer_gather"]) if nxt is not None else {})),
    )(larr, x, *consts, *extra)


ROW_CBW = 8
ROW_CAW = 16
ROW_LOSS = 7
ROW_PW = 48
ROW_LNG = 112
ROW_LNB = 116
ROW_BOUT = 120
ROW_BIN = 124
ROW_WC = 136
ROW_SB = 392
SM_ROWS = 400


def _exchange_comm(start, mid, finish, l, p_i, p_o, sm, r_i, r_o, r_sm, send_sems, recv_sems, loc_sems, vm):
    x, y, c = _place()
    me_k = 2 * x + y
    chips = _other_chips(x, y)

    def rc(src, dst, sem, to):
        return pltpu.make_async_remote_copy(src_ref=src, dst_ref=dst, send_sem=send_sems.at[sem],
                                            recv_sem=recv_sems.at[sem], device_id=to, device_id_type=MESH)

    def big(r):
        px, py, pk = chips[r]
        to = (px, py, c)
        return [rc(p_i.at[l, pk], r_i.at[r, l], 2 * r, to), rc(p_o.at[l, pk], r_o.at[r, l], 2 * r + 1, to)]

    def stage():
        return pltpu.make_async_copy(sm, vm.at[0], loc_sems.at[0])

    def to_sibling():
        return rc(sm, vm.at[1], N_EXCH_SEMS - 4, (x, y, 1 - c))

    def chip_sum(r):
        px, py, pk = chips[r]
        return rc(vm.at[2], r_sm.at[me_k], N_EXCH_SEMS - 3 + r, (px, py, c))

    def keep():
        return pltpu.make_async_copy(vm.at[2], r_sm.at[me_k], loc_sems.at[1])

    with_big, with_small = p_i is not None, sm is not None

    @pl.when(start)
    def _():
        _handshake(PEERS_COLUMN)
        if with_small:
            stage().start()
            to_sibling().start()
        if with_big:
            for r in range(3):
                for cp in big(r):
                    cp.start()

    if with_small:
        @pl.when(mid)
        def _():
            stage().wait()
            to_sibling().wait_recv()
            vm[2] = vm[0] + vm[1]
            keep().start()
            for r in range(3):
                chip_sum(r).start()

    @pl.when(finish)
    def _():
        if with_big:
            for r in range(3):
                for cp in big(r):
                    cp.wait()
        if with_small:
            to_sibling().wait_send()
            for r in range(3):
                chip_sum(r).wait()
            keep().wait()


RC = 32
RC_WIDE = 16
ACC_ROWS = 136


def _rsum8(v):
    r = v[0:8]
    for j in range(1, v.shape[0] // 8):
        r = r + v[8 * j:8 * j + 8]
    return r


def _bwd_layer(larr, dy, z, h, aux, wi, caw, cbw, s256, seg, pw, wm, wmt, sb, wo, v1024, e4, *, tile, exch=None):
    S = dy.shape[0]
    T = tile
    nt = S // T
    nblk = T // SGU_BLOCK
    alpha = float((2.0 * 4) ** 0.25)
    n_in = 17 + (5 if exch is not None else 0)
    n_out = 4 + (3 if exch is not None else 0)
    slab = pltpu.VMEM((T, GROUP), F32)
    scratch = dict(
        dbuf=pltpu.VMEM((T + HALO_A, GROUP), F32), ebuf=pltpu.VMEM((T + HALO_B, GROUP), F32),
        fbuf=pltpu.VMEM((T + HALO_C, GROUP), F32), sh=pltpu.VMEM((SUBLANES - 1, T + HALO_A - SUBLANES, GROUP), F32),
        wm_s=pltpu.VMEM((SGU_BLOCK, 4 * SGU_BLOCK), BF16), wmt_s=pltpu.VMEM((4 * SGU_BLOCK, SGU_BLOCK), BF16),
        dsp_acc=pltpu.VMEM((SGU_BLOCK, GROUP), F32), pw_acc=pltpu.VMEM((GROUP, GROUP), F32),
        acc_s=pltpu.VMEM((8 * ACC_ROWS, GROUP), F32), acc_w=pltpu.VMEM((24, D_MODEL), F32),
        dmix_s=pltpu.VMEM((T, D_MODEL), F32), vst_s=pltpu.VMEM((nblk, 4 * SGU_BLOCK, GROUP), BF16),
        dq_s=pltpu.VMEM((T, GROUP), BF16), dxt_s=pltpu.VMEM((D_MODEL, T), F32),
        mean_s=slab, t1_s=slab, t2_s=slab, q_s=slab, xv_s=slab, rv_s=slab, v_s=slab, sp_s=slab, a0_s=slab, sg_s=slab,
        xh_s=slab, ra_s=slab, ub_s=slab, dsp_s=slab, m1_s=slab, m2_s=slab, dpool_s=slab, dvd_s=slab, u_s=slab,
        du_s=slab, cw_s=slab)
    names = list(scratch)

    def body(*refs):
        (dy_ref, z_ref, h_ref, aux_ref, wi_ref, caw_ref, cbw_ref, s256_ref, seg_ref, pw_ref, wm_ref, wmt_ref,
         sb_ref, wo_ref, v1024_ref, e4_ref) = refs[1:17]
        dx_ref, dhb_ref, dzb_ref, osm_ref = refs[n_in:n_in + 4]
        k0 = n_in + n_out
        sc = dict(zip(names, refs[k0:k0 + len(names)]))
        dbuf, ebuf, fbuf, sh = sc["dbuf"], sc["ebuf"], sc["fbuf"], sc["sh"]
        wm_s, wmt_s, dsp_acc, pw_acc, acc_s, acc_w = (sc[n] for n in ("wm_s", "wmt_s", "dsp_acc", "pw_acc", "acc_s",
                                                                        "acc_w"))
        dmix_s, vst_s, dq_s = sc["dmix_s"], sc["vst_s"], sc["dq_s"]
        i = pl.program_id(0)
        tile_idx = nt - 1 - i
        if exch is not None:
            p_i, p_o, sm = refs[17:20]
            r_i, r_o, r_sm = refs[n_in + 4:n_in + 7]
            _exchange_comm(i == 0, i == 1, i == nt - 1, refs[0][0] + 1, p_i, p_o, sm, r_i, r_o, r_sm, *refs[k0 + len(names):])

        @pl.when(i == 0)
        def _():
            dbuf[T:T + HALO_A, :] = jnp.zeros((HALO_A, GROUP), F32)
            ebuf[T:T + HALO_B, :] = jnp.zeros((HALO_B, GROUP), F32)
            fbuf[T:T + HALO_C, :] = jnp.zeros((HALO_C, GROUP), F32)
            _sgu_masks(wm_ref, wmt_ref, wm_s, wmt_s)
            osm_ref[...] = jnp.zeros_like(osm_ref)
            dsp_acc[...] = jnp.zeros_like(dsp_acc)
            pw_acc[...] = jnp.zeros_like(pw_acc)
            acc_s[...] = jnp.zeros_like(acc_s)
            acc_w[...] = jnp.zeros_like(acc_w)

        def chunks(rc, fn):
            for c in range(T // rc):
                fn(pl.ds(c * rc, rc))

        def hs(j, rows):
            return h_ref[rows, GROUP * j:GROUP * (j + 1)]

        def acc_add(row, val):
            acc_s[8 * row:8 * row + 8, :] += _rsum8(val)

        def put_dh(j, rows, val):
            acc_add(ROW_BIN + j, val)
            dhb_ref[rows, GROUP * j:GROUP * (j + 1)] = val.astype(BF16)

        def dsilu(v, s):
            return s * (1.0 + v * (1.0 - s))

        def vec(r):
            return s256_ref[r:r + 1, :]

        def ln_bwd(rows):
            dyc = dy_ref[rows, :]
            zc = z_ref[rows, :]
            cen = zc - _rowmean(zc)
            rstd = lax.rsqrt(_rowmean(cen * cen) + LN_EPS)
            xhat = cen * rstd
            acc_w[0:8, :] += _rsum8(dyc * xhat)
            acc_w[8:16, :] += _rsum8(dyc)
            gdy = dyc * v1024_ref[1:2, :]
            dz = rstd * (gdy - _rowmean(gdy) - xhat * _rowmean(gdy * xhat))
            acc_w[16:24, :] += _rsum8(dz)
            dzb_ref[rows, :] = dz.astype(BF16)
            dx_ref[rows, :] = alpha * dz
        chunks(RC_WIDE, ln_bwd)

        segm = seg_ref[...]
        dzb = dzb_ref[...]
        for k in range(N_CHIPS):
            dmix_s[:, GROUP * k:GROUP * (k + 1)] = _dot_nt(dzb, wo_ref[k])
        sc["mean_s"][...] = _segdot(aux_ref[:, 0:GROUP], segm)
        pooled_b = aux_ref[:, 2 * GROUP:3 * GROUP].astype(BF16)
        sc["q_s"][...] = _dot(pooled_b, pw_ref[...])

        def centre(rows):
            cen = aux_ref[rows, 0:GROUP] - sc["mean_s"][rows, :]
            sc["t1_s"][rows, :] = cen * cen
            dv_in = hs(10, rows)
            cen_v = dv_in - _rowmean(dv_in)
            rstd_v = lax.rsqrt(_rowmean(cen_v * cen_v) + LN_EPS)
            xv = cen_v * rstd_v
            sc["xv_s"][rows, :] = xv
            sc["rv_s"][rows, :] = jnp.broadcast_to(rstd_v, xv.shape)
            sc["v_s"][rows, :] = xv * vec(4) + vec(5)
        chunks(RC, centre)

        sc["t2_s"][...] = _segdot(sc["t1_s"][...], segm)
        for n in range(nblk):
            blk = slice(n * SGU_BLOCK, (n + 1) * SGU_BLOCK)
            vst_s[n] = _vstack(sc["v_s"][blk, :])
            sc["sp_s"][blk, :] = _dot(wm_s[...], vst_s[n]) + sb_ref[...]

        def mixers(rows):
            a_val, a_glu, a_z = hs(0, rows), hs(1, rows), hs(2, rows)
            sg = _sig(a_glu)
            sc["a0_s"][rows, :] = a_val * sg
            sc["sg_s"][rows, :] = sg
            rstd_a = lax.rsqrt(sc["t2_s"][rows, :] + LN_EPS)
            xh = (aux_ref[rows, 0:GROUP] - sc["mean_s"][rows, :]) * rstd_a
            a2 = xh * vec(1) + vec(2)
            s2 = _sig(a2)
            sz = _sig(a_z)
            dya = dmix_s[rows, 0:GROUP]
            put_dh(2, rows, dya * (a2 * s2) * dsilu(a_z, sz))
            d_a2 = dya * (a_z * sz) * dsilu(a2, s2)
            acc_add(1, d_a2 * xh)
            acc_add(2, d_a2)
            gd = d_a2 * vec(1)
            sc["t1_s"][rows, :] = gd
            sc["t2_s"][rows, :] = gd * xh
            sc["xh_s"][rows, :] = xh
            sc["ra_s"][rows, :] = rstd_a
            b_b, b_c, b_h, b_z = hs(3, rows), hs(4, rows), hs(5, rows), hs(6, rows)
            cb = aux_ref[rows, GROUP:2 * GROUP]
            sz = _sig(b_z)
            dyb = dmix_s[rows, GROUP:2 * GROUP]
            put_dh(3, rows, dyb * cb * (b_z * sz))
            put_dh(6, rows, dyb * b_b * cb * dsilu(b_z, sz))
            ebuf[rows, :] = dyb * b_b * (b_z * sz)
            sc["ub_s"][rows, :] = b_c * b_h
            c_z = hs(8, rows)
            q = sc["q_s"][rows, :]
            sz = _sig(c_z)
            dyc = dmix_s[rows, 2 * GROUP:3 * GROUP]
            acc_add(3, dyc * q * (c_z * sz))
            put_dh(8, rows, dyc * q * vec(3) * dsilu(c_z, sz))
            dq_s[rows, :] = (dyc * vec(3) * (c_z * sz)).astype(BF16)
            d_u, d_z = hs(9, rows), hs(11, rows)
            sp = sc["sp_s"][rows, :]
            sz = _sig(d_z)
            dyd = dmix_s[rows, 3 * GROUP:4 * GROUP]
            put_dh(9, rows, dyd * sp * (d_z * sz))
            put_dh(11, rows, dyd * d_u * sp * dsilu(d_z, sz))
            sc["dsp_s"][rows, :] = dyd * d_u * (d_z * sz)
        chunks(RC, mixers)

        sc["m1_s"][...] = _segdot(sc["t1_s"][...], segm)
        sc["m2_s"][...] = _segdot(sc["t2_s"][...], segm)
        d_q = dq_s[...]
        pw_acc[...] += _dot_tn(pooled_b, d_q)
        sc["dpool_s"][...] = _dot_nt(d_q, pw_ref[...])
        grp = _lane_group(GROUP)
        for n in range(nblk):
            blk = slice(n * SGU_BLOCK, (n + 1) * SGU_BLOCK)
            dspb = sc["dsp_s"][blk, :]
            dsp_acc[...] += dspb
            dspb16 = dspb.astype(BF16)
            dvst = _dot(wmt_s[...], dspb16)
            dvb = None
            for hh in range(4):
                part = jnp.where(grp == hh, dvst[hh * SGU_BLOCK:(hh + 1) * SGU_BLOCK, :], 0.0)
                dvb = part if dvb is None else dvb + part
            sc["dvd_s"][blk, :] = dvb
            dwc = _dot_nt(dspb16, vst_s[n])
            osm_ref[ROW_WC:ROW_WC + SGU_BLOCK, :] += dwc[:, 0:GROUP]
            osm_ref[ROW_WC + SGU_BLOCK:ROW_WC + 2 * SGU_BLOCK, :] += dwc[:, GROUP:2 * GROUP]

        def ln_sums(rows):
            xh = sc["xh_s"][rows, :]
            d_a1 = sc["ra_s"][rows, :] * (sc["t1_s"][rows, :] - sc["m1_s"][rows, :] - xh * sc["m2_s"][rows, :])
            acc_add(0, d_a1)
            dbuf[rows, :] = d_a1
            pos = tile_idx * T + rows.start + lax.broadcasted_iota(jnp.int32, (RC, GROUP), 0) + 1
            lane = lax.broadcasted_iota(jnp.int32, (RC, GROUP), 1) // HEAD
            win = jnp.where(lane == 0, 2, jnp.where(lane == 1, 4, jnp.where(lane == 2, 8, 16)))
            fbuf[rows, :] = sc["dpool_s"][rows, :] / jnp.minimum(pos, win).astype(F32)
            d_v = sc["dvd_s"][rows, :]
            xv = sc["xv_s"][rows, :]
            acc_add(4, d_v * xv)
            acc_add(5, d_v)
            gd = d_v * vec(4)
            put_dh(10, rows, sc["rv_s"][rows, :] * (gd - _rowmean(gd) - xv * _rowmean(gd * xv)))
        chunks(RC, ln_sums)

        span = T + HALO_A - SUBLANES
        for p in range(1, SUBLANES):
            sh[p - 1, :, :] = dbuf[p:p + span, :]

        for r0 in range(0, T, ROWS):
            uc = sc["ub_s"][r0:r0 + ROWS, :]
            acc = None
            for k in range(KB):
                off = (KB - 1) - k + r0
                w = ebuf[off:off + ROWS, :]
                term = cbw_ref[k:k + 1, :] * w
                acc = term if acc is None else acc + term
                acc_add(ROW_CBW + k, uc * w)
            sc["du_s"][r0:r0 + ROWS, :] = acc
        ebuf[T:T + HALO_B, :] = ebuf[0:HALO_B, :]

        hi_lane = (lax.broadcasted_iota(jnp.int32, (1, 128), 1) // HEAD) == 1
        for r0 in range(0, T, ROWS):
            def win(col, j0, j1):
                s = None
                for j in range(j0, j1):
                    term = fbuf[r0 + j:r0 + j + ROWS, 128 * col:128 * (col + 1)]
                    s = term if s is None else s + term
                return s
            sc["cw_s"][r0:r0 + ROWS, 0:128] = win(0, 0, 2) + jnp.where(hi_lane, win(0, 2, 4), 0.0)
            sc["cw_s"][r0:r0 + ROWS, 128:256] = win(1, 0, 8) + jnp.where(hi_lane, win(1, 8, 16), 0.0)
        fbuf[T:T + HALO_C, :] = fbuf[0:HALO_C, :]

        def rest_bc(rows):
            d_u = sc["du_s"][rows, :]
            put_dh(4, rows, d_u * hs(5, rows))
            put_dh(5, rows, d_u * hs(4, rows))
            put_dh(7, rows, sc["cw_s"][rows, :] - sc["dpool_s"][rows, :])
        chunks(RC, rest_bc)

        dxt_s = sc["dxt_s"]

        def dx_term(k):
            term = _dot_nt(wi_ref[k], dhb_ref[:, COLS * k:COLS * (k + 1)])
            if k == 1:
                dxt_s[...] = term
            else:
                dxt_s[...] += term

        def conv_a(rows):
            a0c = sc["a0_s"][rows, :]
            acc = None
            for k in range(KA):
                off = (KA - 1) - k
                p, q8 = off % SUBLANES, off - off % SUBLANES
                w = dbuf[pl.ds(rows.start + q8, RC), :] if p == 0 else sh[p - 1, pl.ds(rows.start + q8, RC), :]
                term = caw_ref[k:k + 1, :] * w
                acc = term if acc is None else acc + term
                acc_add(ROW_CAW + k, a0c * w)
            sc["u_s"][rows, :] = acc
        n_chunks = T // RC
        after = {(n_chunks * j) // 3: j + 1 for j in range(3)}
        for c in range(n_chunks):
            conv_a(pl.ds(c * RC, RC))
            if c in after:
                dx_term(after[c])
        dbuf[T:T + HALO_A, :] = dbuf[0:HALO_A, :]

        def rest_a(rows):
            d_a0 = sc["u_s"][rows, :]
            sg = sc["sg_s"][rows, :]
            put_dh(0, rows, d_a0 * sg)
            put_dh(1, rows, d_a0 * hs(0, rows) * sg * (1.0 - sg))
        chunks(RC, rest_a)
        dx_term(0)
        dx_ref[...] += dxt_s[...].T

        @pl.when(i == nt - 1)
        def _():
            for row in list(range(6)) + list(range(ROW_CBW, ROW_CBW + KB)) + list(range(ROW_CAW, ROW_CAW + KA)) + list(
                    range(ROW_BIN, ROW_BIN + N_SLICES)):
                osm_ref[row:row + 1, :] = _colsum(acc_s[8 * row:8 * row + 8, :])
            for j, row in enumerate((ROW_LNG, ROW_LNB, ROW_BOUT)):
                cs = _colsum(acc_w[8 * j:8 * j + 8, :])
                for q in range(D_MODEL // GROUP):
                    osm_ref[row + q:row + q + 1, :] = cs[:, GROUP * q:GROUP * (q + 1)]
            r = lax.broadcasted_iota(jnp.int32, (SGU_BLOCK, GROUP), 0) // CHUNK
            c = (lax.broadcasted_iota(jnp.int32, (SGU_BLOCK, GROUP), 1) % SGU_BLOCK) // CHUNK
            for half in range(2):
                rows_ = slice(ROW_WC + half * SGU_BLOCK, ROW_WC + (half + 1) * SGU_BLOCK)
                osm_ref[rows_, :] = jnp.where(c <= r, osm_ref[rows_, :], 0.0)
            sb_t = _segdot(dsp_acc[...], e4_ref[...]).T
            osm_ref[ROW_SB:ROW_SB + 8, 0:SGU_BLOCK] = sb_t[0:8, :]
            for g in range(4):
                osm_ref[ROW_PW:ROW_PW + HEAD, HEAD * g:HEAD * (g + 1)] = (
                    pw_acc[HEAD * g:HEAD * (g + 1), HEAD * g:HEAD * (g + 1)])

    def rows(width):
        return pl.BlockSpec((T, width), lambda i, l: (nt - 1 - i, 0))

    consts = (wi, caw, cbw, s256, seg, pw, wm, wmt, sb, wo, v1024, e4)
    unstacked = (wi, seg, wo, e4)
    in_specs = [rows(D_MODEL), rows(D_MODEL), rows(IN_WIDTH), rows(3 * GROUP)] + [
        _whole(a) if any(a is u for u in unstacked) else _of_layer(a) for a in consts]
    out_specs = [rows(D_MODEL), rows(IN_WIDTH), rows(D_MODEL), pl.BlockSpec((SM_ROWS, GROUP), lambda i, l: (0, 0))]
    out_shape = [jax.ShapeDtypeStruct((S, D_MODEL), F32), jax.ShapeDtypeStruct((S, IN_WIDTH), BF16),
                 jax.ShapeDtypeStruct((S, D_MODEL), BF16), jax.ShapeDtypeStruct((SM_ROWS, GROUP), F32)]
    scratch_shapes = list(scratch.values())
    extra, aliases = (), {}
    if exch is not None:
        extra = tuple(exch)
        r_i, r_o = exch[3], exch[4]
        in_specs += [ANY] * 5
        out_specs += [ANY] * 3
        out_shape += [jax.ShapeDtypeStruct(r_i.shape, r_i.dtype), jax.ShapeDtypeStruct(r_o.shape, r_o.dtype),
                      jax.ShapeDtypeStruct((N_CHIPS, SM_ROWS, GROUP), F32)]
        scratch_shapes += [pltpu.SemaphoreType.DMA((N_EXCH_SEMS,)), pltpu.SemaphoreType.DMA((N_EXCH_SEMS,)),
                           pltpu.SemaphoreType.DMA((2,)), pltpu.VMEM((3, SM_ROWS, GROUP), F32)]
        aliases = {20: 4, 21: 5}
    grid_spec = pltpu.PrefetchScalarGridSpec(num_scalar_prefetch=1, grid=(nt,), in_specs=in_specs,
                                             out_specs=out_specs, scratch_shapes=scratch_shapes)
    return pl.pallas_call(
        body, name="bwd_layer" if exch is None else "bwd_layer_exchange",
        grid_spec=grid_spec, out_shape=out_shape, input_output_aliases=aliases,
        compiler_params=_vmem_params(dimension_semantics=("arbitrary",), **(
            dict(has_side_effects=True, collective_id=COLLECTIVE_ID["bwd_layer_exchange"]) if exch is not None else {})),
    )(larr, dy, z, h, aux, *consts, *extra)


def _dw_swap(cl_arr, xb, dhb, mixb, dzb, p_i, p_o, *, k_steps, last=None):
    S = xb.shape[0]
    tk = S // k_steps
    n_steps = N_CHIPS + k_steps
    hi, ho = p_i.shape[2], p_o.shape[2]
    n_in = 7 + (3 if last is not None else 0)
    n_out = 2 + (3 if last is not None else 0)

    def body(*refs):
        cl_ref, x_ref, dh_ref, mix_ref, dz_ref = refs[0:5]
        pi_ref, po_ref = refs[n_in:n_in + 2]
        own_i, acc_o, snd_i, snd_o, rcv_i, rcv_o, send_sems, recv_sems = refs[n_in + n_out:n_in + n_out + 8]
        j = pl.program_id(0)
        l = cl_ref[1]
        x, y, c = _place()
        mine_o, theirs_o = (pl.ds(pl.multiple_of(cc * ho, ho), ho) for cc in (c, 1 - c))

        def to_sibling(src, dst, sem):
            return pltpu.make_async_remote_copy(src_ref=src, dst_ref=dst, send_sem=send_sems.at[sem],
                                                recv_sem=recv_sems.at[sem], device_id=(x, y, 1 - c), device_id_type=MESH)

        def chunk_copy(k):
            return to_sibling(snd_i.at[k % 2], rcv_i.at[k], k)

        def out_copy():
            return to_sibling(snd_o, rcv_o, N_CHIPS)

        if last is None:
            @pl.when(j == 0)
            def _():
                _handshake(PEERS_SIBLING)
        else:
            qi_ref, qo_ref, r_sm = refs[n_in + 2:n_in + 5]
            out_sems, in_sems = refs[n_in + n_out + 8:n_in + n_out + 10]
            _exchange_comm(j == 0, j == 1, j == n_steps - 1, None, None, None, refs[7], None, None, r_sm,
                           *refs[n_in + n_out + 10:])
            chips = _other_chips(x, y)

            def onward(r):
                px, py, pk = chips[r]
                return [pltpu.make_async_remote_copy(
                    src_ref=v.at[pk], dst_ref=q.at[r, l], send_sem=out_sems.at[2 * r + n], recv_sem=in_sems.at[2 * r + n],
                    device_id=(px, py, c), device_id_type=MESH) for n, (v, q) in enumerate(((rcv_i, qi_ref), (rcv_o, qo_ref)))]

        @pl.when(j < N_CHIPS)
        def _():
            @pl.when(j >= 2)
            def _():
                chunk_copy(j - 2).wait_send()

            acc = _dot_tn(x_ref[...], dh_ref[...])
            top, bottom = acc[:hi], acc[hi:]
            own_i[j % 2] = jnp.where(c == 0, top, bottom)
            snd_i[j % 2] = jnp.where(c == 0, bottom, top).astype(BF16)
            chunk_copy(j).start()

        @pl.when(j == N_CHIPS)
        def _():
            acc_o[...] = jnp.zeros_like(acc_o)

        @pl.when(j >= N_CHIPS)
        def _():
            acc_o[...] += _dot_tn(mix_ref[...], dz_ref[...]).reshape(N_CHIPS, GROUP, D_MODEL)

        @pl.when((j >= 1) & (j <= N_CHIPS))
        def _():
            chunk_copy(j - 1).wait_recv()
            summed = (own_i[(j - 1) % 2] + rcv_i[j - 1].astype(F32)).astype(pi_ref.dtype)
            pi_ref[...] = summed
            if last is not None:
                rcv_i[j - 1] = summed
                for r in range(3):
                    @pl.when(j - 1 == chips[r][2])
                    def _():
                        onward(r)[0].start()

        @pl.when(j == n_steps - 1)
        def _():
            snd_o[...] = acc_o[:, theirs_o, :].astype(BF16)
            out_copy().start()
            for k in (N_CHIPS - 2, N_CHIPS - 1):
                chunk_copy(k).wait_send()
            out_copy().wait_recv()
            summed = (acc_o[:, mine_o, :] + rcv_o[...].astype(F32)).astype(po_ref.dtype)
            po_ref[...] = summed
            if last is not None:
                rcv_o[...] = summed
                for r in range(3):
                    onward(r)[1].start()
            out_copy().wait_send()
            if last is not None:
                for r in range(3):
                    for cp in onward(r):
                        cp.wait()

    def col_block(j):
        return jnp.minimum(j, N_CHIPS - 1)

    def tok_block(j):
        return jnp.maximum(j - N_CHIPS, 0)

    in_specs = [pl.BlockSpec((S, D_MODEL), lambda j, cl: (0, 0)),
                pl.BlockSpec((S, COLS), lambda j, cl: (0, col_block(j))),
                pl.BlockSpec((tk, D_MODEL), lambda j, cl: (tok_block(j), 0)),
                pl.BlockSpec((tk, D_MODEL), lambda j, cl: (tok_block(j), 0)), ANY, ANY]
    out_specs = [pl.BlockSpec((None, None, hi, COLS), lambda j, cl: (cl[1], jnp.clip(j - 1, 0, N_CHIPS - 1), 0, 0)),
                 pl.BlockSpec((None, N_CHIPS, ho, D_MODEL), lambda j, cl: (cl[1], 0, 0, 0))]
    out_shape = [jax.ShapeDtypeStruct(p_i.shape, p_i.dtype), jax.ShapeDtypeStruct(p_o.shape, p_o.dtype)]
    scratch = [pltpu.VMEM((2, hi, COLS), F32), pltpu.VMEM((N_CHIPS, GROUP, D_MODEL), F32),
               pltpu.VMEM((2, hi, COLS), BF16), pltpu.VMEM((N_CHIPS, ho, D_MODEL), BF16),
               pltpu.VMEM((N_CHIPS, hi, COLS), BF16), pltpu.VMEM((N_CHIPS, ho, D_MODEL), BF16),
               pltpu.SemaphoreType.DMA((N_CHIPS + 1,)), pltpu.SemaphoreType.DMA((N_CHIPS + 1,))]
    extra, aliases, kind = (), {5: 0, 6: 1}, "dw_swap"
    if last is not None:
        extra, aliases, kind = tuple(last), {5: 0, 6: 1, 8: 2, 9: 3}, "dw_swap_exchange"
        in_specs += [ANY, ANY, ANY]
        out_specs += [ANY, ANY, ANY]
        out_shape += [jax.ShapeDtypeStruct(q.shape, q.dtype) for q in last[1:]]
        out_shape += [jax.ShapeDtypeStruct((N_CHIPS, SM_ROWS, GROUP), F32)]
        scratch += [pltpu.SemaphoreType.DMA((6,)), pltpu.SemaphoreType.DMA((6,)), pltpu.SemaphoreType.DMA((N_EXCH_SEMS,)),
                    pltpu.SemaphoreType.DMA((N_EXCH_SEMS,)), pltpu.SemaphoreType.DMA((2,)),
                    pltpu.VMEM((3, SM_ROWS, GROUP), F32)]
    grid_spec = pltpu.PrefetchScalarGridSpec(
        num_scalar_prefetch=1, grid=(n_steps,), in_specs=in_specs, out_specs=out_specs, scratch_shapes=scratch)
    return pl.pallas_call(
        body, name=kind, grid_spec=grid_spec, out_shape=out_shape, input_output_aliases=aliases,
        compiler_params=_vmem_params(dimension_semantics=("arbitrary",), has_side_effects=True,
                                     collective_id=COLLECTIVE_ID[kind]),
    )(cl_arr, xb, dhb, mixb, dzb, p_i, p_o, *extra)


def _adamw_math(w, g, m, v):
    nm = ADAM_B1 * m + (1.0 - ADAM_B1) * g
    nv = ADAM_B2 * v + (1.0 - ADAM_B2) * (g * g)
    c1 = 1.0 - ADAM_B1 ** ADAM_STEP
    c2 = 1.0 - ADAM_B2 ** ADAM_STEP
    return -ADAM_LR * ((nm / c1) / (jnp.sqrt(nv / c2) + ADAM_EPS) + ADAM_WD * w), nm, nv


def _adamw_small(ws, gs, ms, vs):
    n = len(ws)

    def body(*refs):
        for j in range(n):
            d, nm, nv = _adamw_math(*(refs[k * n + j][...] for k in range(4)))
            refs[4 * n + j][...] = d
            refs[5 * n + j][...] = nm
            refs[6 * n + j][...] = nv

    shapes = [jax.ShapeDtypeStruct(w.shape, F32) for w in ws]
    outs = pl.pallas_call(body, name="adamw_small", out_shape=shapes * 3, compiler_params=_vmem_params())(
        *ws, *gs, *ms, *vs)
    return outs[0:n], outs[n:2 * n], outs[2 * n:3 * n]


def _adamw(w, g, m, v, *, rows_per_step, name, copy_g=False):
    R, C = w.shape
    tr = rows_per_step

    def body(w_ref, g_ref, m_ref, v_ref, d_ref, nm_ref, nv_ref, *g_out):
        g_ = g_ref[...]
        d_ref[...], nm_ref[...], nv_ref[...] = _adamw_math(w_ref[...], g_, m_ref[...], v_ref[...])
        if copy_g:
            g_out[0][...] = g_

    spec = pl.BlockSpec((tr, C), lambda i: (i, 0))
    n_out = 4 if copy_g else 3
    return pl.pallas_call(
        body, name=name, grid=(R // tr,),
        in_specs=[spec] * 4, out_specs=[spec] * n_out,
        out_shape=[jax.ShapeDtypeStruct((R, C), F32)] * n_out,
        compiler_params=_vmem_params(dimension_semantics=("arbitrary",)),
    )(w, g, m, v)


def _gather_weights(wi16, wo16, cw):
    L = wi16.shape[0]
    hi_rows, ho_rows = D_MODEL // 2, GROUP // 2
    n_ici = 2 * L + 1
    n_fwd = 2 * L

    def body(wi_ref, wo_ref, cw_ref, *rest):
        wig = rest[0:L]
        wog = rest[L:2 * L]
        cwg = rest[2 * L]
        send_sems, recv_sems, loc_sems, vwi, vwo, vcw = rest[2 * L + 1:]
        x, y, c = _place()
        me_k = 2 * x + y
        sibling = (x, y, 1 - c)
        chips = _other_chips(x, y)

        def half_i(ref, blk):
            return ref.at[blk, pl.ds(c * hi_rows, hi_rows), :]

        def half_o(ref, blk):
            return ref.at[blk, pl.ds(c * ho_rows, ho_rows), :]

        def other_half_i(ref, blk):
            return ref.at[blk, pl.ds((1 - c) * hi_rows, hi_rows), :]

        def other_half_o(ref, blk):
            return ref.at[blk, pl.ds((1 - c) * ho_rows, ho_rows), :]

        stage_in = [pltpu.make_async_copy(wi_ref, vwi, loc_sems.at[0]), pltpu.make_async_copy(wo_ref, vwo, loc_sems.at[1]),
                    pltpu.make_async_copy(cw_ref, vcw, loc_sems.at[2])]
        local = []
        for l in range(L):
            local.append(pltpu.make_async_copy(vwi.at[l], wig[l].at[me_k], loc_sems.at[3 + 2 * l]))
            local.append(pltpu.make_async_copy(vwo.at[l], wog[l].at[me_k], loc_sems.at[3 + 2 * l + 1]))
        local.append(pltpu.make_async_copy(vcw, cwg.at[me_k], loc_sems.at[3 + 2 * L]))
        _handshake(PEERS_COLUMN)
        for cp in stage_in:
            cp.start()

        def remote(src, dst, sem, to):
            return pltpu.make_async_remote_copy(src_ref=src, dst_ref=dst, send_sem=send_sems.at[sem],
                                                recv_sem=recv_sems.at[sem], device_id=to, device_id_type=MESH)

        sends = []
        for r, (px, py, _) in enumerate(chips):
            to = (px, py, c)
            for l in range(L):
                sends.append(remote(half_i(wi_ref, l), half_i(wig[l], me_k), r * n_ici + 2 * l, to))
                sends.append(remote(half_o(wo_ref, l), half_o(wog[l], me_k), r * n_ici + 2 * l + 1, to))
            sends.append(remote(cw_ref, cwg.at[me_k], r * n_ici + 2 * L, to))
        for cp in sends:
            cp.start()
        for cp in stage_in:
            cp.wait()
        for cp in local:
            cp.start()

        base = 3 * n_ici
        fwds = []
        for r, (px, py, pk) in enumerate(chips):
            for l in range(L):
                remote(half_i(wig[l], pk), half_i(wig[l], pk), r * n_ici + 2 * l, sibling).wait_recv()
                f = remote(half_i(wig[l], pk), half_i(wig[l], pk), base + r * n_fwd + 2 * l, sibling)
                f.start()
                fwds.append(f)
                remote(half_o(wog[l], pk), half_o(wog[l], pk), r * n_ici + 2 * l + 1, sibling).wait_recv()
                f = remote(half_o(wog[l], pk), half_o(wog[l], pk), base + r * n_fwd + 2 * l + 1, sibling)
                f.start()
                fwds.append(f)
            remote(cwg.at[pk], cwg.at[pk], r * n_ici + 2 * L, sibling).wait_recv()
        for r, (px, py, pk) in enumerate(chips):
            for l in range(L):
                remote(other_half_i(wig[l], pk), other_half_i(wig[l], pk), base + r * n_fwd + 2 * l, sibling).wait_recv()
                remote(other_half_o(wog[l], pk), other_half_o(wog[l], pk), base + r * n_fwd + 2 * l + 1, sibling).wait_recv()
        for cp in sends + fwds:
            cp.wait_send()
        for cp in local:
            cp.wait()

    n_sem = 3 * n_ici + 3 * n_fwd
    out_shape = ([jax.ShapeDtypeStruct((N_CHIPS, D_MODEL, COLS), BF16)] * L
                 + [jax.ShapeDtypeStruct((N_CHIPS, GROUP, D_MODEL), BF16)] * L
                 + [jax.ShapeDtypeStruct((N_CHIPS,) + cw.shape, F32)])
    outs = pl.pallas_call(
        body, name="gather_weights",
        in_specs=[ANY, ANY, ANY], out_specs=[ANY] * (2 * L + 1), out_shape=out_shape,
        scratch_shapes=[pltpu.SemaphoreType.DMA((n_sem,)), pltpu.SemaphoreType.DMA((n_sem,)),
                        pltpu.SemaphoreType.DMA((2 * L + 4,)), pltpu.VMEM(wi16.shape, BF16), pltpu.VMEM(wo16.shape, BF16),
                        pltpu.VMEM(cw.shape, F32)],
        compiler_params=_vmem_params(has_side_effects=True, collective_id=COLLECTIVE_ID["gather_weights"]),
    )(wi16, wo16, cw)
    return outs[0:L], outs[L:2 * L], outs[2 * L]


def _sum_small(r_sms):
    L = len(r_sms)

    def body(*refs):
        o_ref = refs[L]
        for l in range(L):
            acc = refs[l][0]
            for k in range(1, N_CHIPS):
                acc = acc + refs[l][k]
            o_ref[l] = acc

    return pl.pallas_call(
        body, name="sum_small",
        out_shape=jax.ShapeDtypeStruct((L,) + r_sms[0].shape[1:], F32),
        compiler_params=_vmem_params(),
    )(*r_sms)


def _sum_share(kc_arr, p_i, q_i, p_o, q_o, *, nb):
    L = p_i.shape[0]
    n_steps, slots = L * nb, 2

    def body(kc_ref, pi_ref, a0, a1, a2, po_ref, b0, b1, b2, oi_ref, oo_ref, vi, vo, loc_sems, send_sems, recv_sems):
        del kc_ref
        x, y, c = _place()
        t = pl.program_id(0) * nb + pl.program_id(1)

        def copies(s):
            l, i = s // nb, s % nb
            out = []
            for j, (v, o) in enumerate(((vi, oi_ref), (vo, oo_ref))):
                tr = v.shape[1]
                src, dst = v.at[s % slots], o.at[l, pl.ds((c * nb + i) * tr, tr), :]
                out.append((pltpu.make_async_copy(src, dst, loc_sems.at[2 * s + j]),
                            pltpu.make_async_remote_copy(src_ref=src, dst_ref=dst, send_sem=send_sems.at[2 * s + j],
                                                         recv_sem=recv_sems.at[2 * s + j], device_id=(x, y, 1 - c),
                                                         device_id_type=MESH)))
            return out

        def sent(s):
            for mine, theirs in copies(s):
                mine.wait()
                theirs.wait_send()

        @pl.when(t == 0)
        def _():
            _handshake(PEERS_SIBLING)

        @pl.when(t >= slots)
        def _():
            sent(t - slots)

        f = lambda ref: ref[...].astype(F32)
        vi[t % slots] = ((f(pi_ref) + f(a0)) + f(a1)) + f(a2)
        vo[t % slots] = ((f(po_ref) + f(b0)) + f(b1)) + f(b2)
        for mine, theirs in copies(t):
            mine.start()
            theirs.start()

        @pl.when(t == n_steps - 1)
        def _():
            for s in range(n_steps - slots, n_steps):
                sent(s)
            for s in range(n_steps):
                for _, theirs in copies(s):
                    theirs.wait_recv()

    def specs(p):
        tr, cols = p.shape[2] // nb, p.shape[3]
        chunk = pl.BlockSpec((None, None, tr, cols), lambda l, i, kc: (l, kc[0], i, 0))
        got = [pl.BlockSpec((None, None, tr, cols), lambda l, i, kc, _j=j: (_j, l, i, 0)) for j in range(3)]
        return [chunk] + got, pltpu.VMEM((slots, tr, cols), F32)

    (in_i, v_i), (in_o, v_o) = specs(p_i), specs(p_o)
    grid_spec = pltpu.PrefetchScalarGridSpec(
        num_scalar_prefetch=1, grid=(L, nb), in_specs=in_i + in_o, out_specs=[ANY, ANY],
        scratch_shapes=[v_i, v_o] + [pltpu.SemaphoreType.DMA((2 * n_steps,))] * 3)
    return pl.pallas_call(
        body, name="sum_share", grid_spec=grid_spec,
        out_shape=[jax.ShapeDtypeStruct((L, 2 * p.shape[2], p.shape[3]), F32) for p in (p_i, p_o)],
        compiler_params=_vmem_params(dimension_semantics=("arbitrary",) * 2, has_side_effects=True,
                                     collective_id=COLLECTIVE_ID["sum_share"]),
    )(kc_arr, p_i, q_i, q_i, q_i, p_o, q_o, q_o, q_o)


WEIGHTS = ("ln_g", "ln_b", "w_in", "b_in", "conv_a_w", "conv_a_b", "norm_a_g", "norm_a_b", "conv_b_w", "pool_w",
           "pool_scale", "sgu_ln_g", "sgu_ln_b", "sgu_w", "sgu_bias", "w_out", "b_out")


def _pad_rows(a, rows):
    return jnp.pad(a, ((0, rows - a.shape[0]), (0, 0)))


def _indicator_consts():
    seg = jnp.where((jnp.arange(GROUP)[:, None] // HEAD) == (jnp.arange(GROUP)[None, :] // HEAD),
                    1.0 / HEAD, 0.0).astype(BF16)
    e4 = ((jnp.arange(GROUP)[:, None] // HEAD) == jnp.arange(128)[None, :]).astype(BF16)
    return seg, e4


def _layer_consts(p, conv_full):
    L = conv_full.shape[0]
    same_head = jnp.eye(4, dtype=F32)[:, None, :, None] > 0

    def rows_to(a, rows):
        return jnp.pad(a, ((0, 0), (0, rows - a.shape[1]), (0, 0)))

    s256 = jnp.stack([p[n] for n in ("conv_a_b", "norm_a_g", "norm_a_b", "pool_scale", "sgu_ln_g", "sgu_ln_b")], axis=1)
    pw = jnp.where(same_head, p["pool_w"][:, :, :, None, :], 0.0).reshape(L, GROUP, GROUP)
    return dict(
        caw=rows_to(conv_full[:, :KA], 32), cbw=rows_to(conv_full[:, KA:], 8), s256=rows_to(s256, 8),
        pw=pw.astype(BF16),
        wm=jnp.transpose(p["sgu_w"], (0, 2, 1, 3)).reshape(L, SGU_BLOCK, 4 * SGU_BLOCK),
        wmt=jnp.transpose(p["sgu_w"], (0, 1, 3, 2)).reshape(L, 4 * SGU_BLOCK, SGU_BLOCK),
        sb=jnp.repeat(jnp.transpose(p["sgu_bias"], (0, 2, 1)), HEAD, axis=2),
        v1024=rows_to(jnp.stack([p["b_out"], p["ln_g"], p["ln_b"]], axis=1), 8),
        bin=p["b_in"][:, None, :])


def _unpack_small(sm):
    L = sm.shape[0]
    owc = jnp.concatenate([sm[:, ROW_WC:ROW_WC + SGU_BLOCK], sm[:, ROW_WC + SGU_BLOCK:ROW_WC + 2 * SGU_BLOCK]], axis=2)
    return dict(
        conv_a_b=sm[:, 0], norm_a_g=sm[:, 1], norm_a_b=sm[:, 2], pool_scale=sm[:, 3], sgu_ln_g=sm[:, 4],
        sgu_ln_b=sm[:, 5], conv_b_w=sm[:, ROW_CBW:ROW_CBW + KB], conv_a_w=sm[:, ROW_CAW:ROW_CAW + KA],
        pool_w=jnp.transpose(sm[:, ROW_PW:ROW_PW + HEAD].reshape(L, HEAD, 4, HEAD), (0, 2, 1, 3)),
        ln_g=sm[:, ROW_LNG:ROW_LNG + 4].reshape(L, D_MODEL), ln_b=sm[:, ROW_LNB:ROW_LNB + 4].reshape(L, D_MODEL),
        b_out=sm[:, ROW_BOUT:ROW_BOUT + 4].reshape(L, D_MODEL),
        b_in=sm[:, ROW_BIN:ROW_BIN + N_SLICES].reshape(L, IN_WIDTH),
        sgu_w=jnp.transpose(owc.reshape(L, SGU_BLOCK, 4, SGU_BLOCK), (0, 2, 1, 3)),
        sgu_bias=sm[:, ROW_SB:ROW_SB + 4, 0:SGU_BLOCK])


def _step(p, m, v, x, target, *, tile_f, tile_b, k_steps):
    L = p["ln_g"].shape[0]
    xi, yi, ci = _place()
    me_k = 2 * xi + yi
    hi_rows, ho_rows = D_MODEL // 2, GROUP // 2

    cw = jnp.concatenate([p["conv_a_w"], p["conv_b_w"]], axis=1).reshape(-1, 128)
    cw_rows = cw.shape[0]
    cw = _pad_rows(cw, -(-cw_rows // SUBLANES) * SUBLANES)
    wi16 = p["w_in"].astype(BF16)
    wo16 = p["w_out"].astype(BF16)
    wig0, wog0, cwg = _gather_weights(wi16[0:1], wo16[0:1], cw)
    cwg = cwg[:, :cw_rows].reshape(N_CHIPS, L, KA + KB, HEAD)
    conv_full = jnp.transpose(cwg, (1, 2, 0, 3)).reshape(L, KA + KB, GROUP)
    seg, e4 = _indicator_consts()
    k = _layer_consts(p, conv_full)
    layer = [jnp.full((1,), l, jnp.int32) for l in range(L)]

    hcur = x
    saved, wig, wog = [], [wig0[0]], [wog0[0]]
    for l in range(L):
        nxt = (wi16, wo16) if l + 1 < L else None
        outs = _fwd_layer(layer[l], hcur, wig[l], k["bin"], k["caw"], k["cbw"], k["s256"], seg, k["pw"], k["wm"], k["sb"],
                          wog[l], k["v1024"], tile=tile_f, nxt=nxt, target=None if nxt is not None else target)
        y, xb, h, aux, mixb, z = outs[0:6]
        if nxt is not None:
            wig.append(outs[6])
            wog.append(outs[7])
        saved.append((xb, h, aux, mixb, z))
        hcur = y

    dy = hcur
    loss_local = outs[6][0, 0]

    p_i = lax.empty((L, N_CHIPS, hi_rows, COLS), BF16)
    p_o = lax.empty((L, N_CHIPS, ho_rows, D_MODEL), BF16)
    q_i = lax.empty((3, L, hi_rows, COLS), BF16)
    q_o = lax.empty((3, L, ho_rows, D_MODEL), BF16)
    r_sm = [None] * L
    pending = None
    for l in reversed(range(L)):
        xb, h, aux, mixb, z = saved[l]
        exch = None if pending is None else (p_i, p_o, pending, q_i, q_o)
        outs = _bwd_layer(layer[l], dy, z, h, aux, wig[l], k["caw"], k["cbw"], k["s256"], seg, k["pw"], k["wm"],
                          k["wmt"], k["sb"], wog[l], k["v1024"], e4, tile=tile_b, exch=exch)
        dy, dhb, dzb, osm = outs[0:4]
        if l == L - 1:
            osm = osm.at[ROW_LOSS, 0].set(loss_local)
        if exch is not None:
            q_i, q_o, r_sm[l + 1] = outs[4:7]
        cl_arr = jnp.stack([ci, jnp.int32(l)]).astype(jnp.int32)
        if l > 0:
            p_i, p_o = _dw_swap(cl_arr, xb, dhb, mixb, dzb, p_i, p_o, k_steps=k_steps)
        else:
            p_i, p_o, q_i, q_o, r_sm[0] = _dw_swap(cl_arr, xb, dhb, mixb, dzb, p_i, p_o, k_steps=k_steps,
                                                   last=(osm, q_i, q_o))
        pending = osm
    grad_x = dy

    summed = _sum_small(r_sm)
    loss = summed[L - 1, ROW_LOSS, 0]
    grads = _unpack_small(summed)
    for n in ("conv_a_w", "conv_b_w"):
        grads[n] = lax.dynamic_slice_in_dim(grads[n], me_k * HEAD, HEAD, axis=2)

    kc_arr = jnp.stack([me_k, ci]).astype(jnp.int32)
    g_i, g_o = _sum_share(kc_arr, p_i, q_i, p_o, q_o, nb=2)
    grads["w_in"] = g_i
    grads["w_out"] = g_o

    delta, new_m, new_v = {}, {}, {}
    for n, tr in (("w_in", 512), ("w_out", 256)):
        shp = p[n].shape
        args = [a.reshape(shp[0] * shp[1], shp[2]) for a in (p[n], grads[n], m[n], v[n])]
        outs = _adamw(*args, rows_per_step=tr, name="adamw_" + n, copy_g=True)
        delta[n], new_m[n], new_v[n], grads[n] = (a.reshape(shp) for a in outs)
    small = [n for n in WEIGHTS if n not in ("w_in", "w_out")]
    flat = [[a[n].reshape(-1, a[n].shape[-1]) for n in small] for a in (p, grads, m, v)]
    outs = _adamw_small(*flat)
    for j, n in enumerate(small):
        delta[n], new_m[n], new_v[n] = (o[j].reshape(p[n].shape) for o in outs)

    return (loss, grad_x[None], *[grads[n] for n in WEIGHTS], *[delta[n] for n in WEIGHTS],
            *[new_m[n] for n in WEIGHTS], *[new_v[n] for n in WEIGHTS])


def kernel(x, ln_g, ln_b, w_in, b_in, conv_a_w, conv_a_b, norm_a_g, norm_a_b, conv_b_w, pool_w, pool_scale, sgu_ln_g, sgu_ln_b, sgu_w, sgu_bias, w_out, b_out, loss_target, m_ln_g, m_ln_b, m_w_in, m_b_in, m_conv_a_w, m_conv_a_b, m_norm_a_g, m_norm_a_b, m_conv_b_w, m_pool_w, m_pool_scale, m_sgu_ln_g, m_sgu_ln_b, m_sgu_w, m_sgu_bias, m_w_out, m_b_out, v_ln_g, v_ln_b, v_w_in, v_b_in, v_conv_a_w, v_conv_a_b, v_norm_a_g, v_norm_a_b, v_conv_b_w, v_pool_w, v_pool_scale, v_sgu_ln_g, v_sgu_ln_b, v_sgu_w, v_sgu_bias, v_w_out, v_b_out):
    p = dict(ln_g=ln_g, ln_b=ln_b, w_in=w_in, b_in=b_in, conv_a_w=conv_a_w, conv_a_b=conv_a_b, norm_a_g=norm_a_g,
             norm_a_b=norm_a_b, conv_b_w=conv_b_w, pool_w=pool_w, pool_scale=pool_scale, sgu_ln_g=sgu_ln_g,
             sgu_ln_b=sgu_ln_b, sgu_w=sgu_w, sgu_bias=sgu_bias, w_out=w_out, b_out=b_out)
    m = dict(ln_g=m_ln_g, ln_b=m_ln_b, w_in=m_w_in, b_in=m_b_in, conv_a_w=m_conv_a_w, conv_a_b=m_conv_a_b,
             norm_a_g=m_norm_a_g, norm_a_b=m_norm_a_b, conv_b_w=m_conv_b_w, pool_w=m_pool_w, pool_scale=m_pool_scale,
             sgu_ln_g=m_sgu_ln_g, sgu_ln_b=m_sgu_ln_b, sgu_w=m_sgu_w, sgu_bias=m_sgu_bias, w_out=m_w_out, b_out=m_b_out)
    v = dict(ln_g=v_ln_g, ln_b=v_ln_b, w_in=v_w_in, b_in=v_b_in, conv_a_w=v_conv_a_w, conv_a_b=v_conv_a_b,
             norm_a_g=v_norm_a_g, norm_a_b=v_norm_a_b, conv_b_w=v_conv_b_w, pool_w=v_pool_w, pool_scale=v_pool_scale,
             sgu_ln_g=v_sgu_ln_g, sgu_ln_b=v_sgu_ln_b, sgu_w=v_sgu_w, sgu_bias=v_sgu_bias, w_out=v_w_out, b_out=v_b_out)
    return _step(p, m, v, x[0], loss_target[0], tile_f=256, tile_b=256, k_steps=4)
```

```python
import jax
import jax.numpy as jnp
from jax import lax
from jax.experimental import pallas as pl
from jax.experimental.pallas import tpu as pltpu

F32 = jnp.float32
BF16 = jnp.bfloat16
MESH = pl.DeviceIdType.MESH

D_MODEL = 1024
GROUP = 256
HEAD = 64
N_SLICES = 12
IN_WIDTH = N_SLICES * GROUP
N_CHIPS = 4
COLS = IN_WIDTH // N_CHIPS
KA = 31
KB = 3
SUBLANES = 8
HALO_A, HALO_B, HALO_C = 32, 8, 16
N_GATHER_SEMS = 12
N_EXCH_SEMS = 10
SGU_BLOCK = 128
CHUNK = 64
LN_EPS = 1e-5
ROWS = 64
V7X_VMEM_BYTES = 64 * 1024 * 1024
VMEM_LIMIT = V7X_VMEM_BYTES - 8 * 1024 * 1024

ADAM_LR, ADAM_B1, ADAM_B2, ADAM_EPS, ADAM_WD, ADAM_STEP = 0.001, 0.9, 0.999, 1e-08, 0.01, 10


ANY = pl.BlockSpec(memory_space=pl.ANY)


def _vmem_params(**kw):
    return pltpu.CompilerParams(vmem_limit_bytes=VMEM_LIMIT, **kw)


def _whole(a):
    return pl.BlockSpec(a.shape, lambda i, l, _n=a.ndim: (0,) * _n)


def _of_layer(a):
    return pl.BlockSpec((None,) + a.shape[1:], lambda i, l, _n=a.ndim: (l[0],) + (0,) * (_n - 1))


def _place():
    return lax.axis_index("x"), lax.axis_index("y"), lax.axis_index("c")


def _other_chips(x, y):
    return [(1 - x, y, 2 * (1 - x) + y), (x, 1 - y, 2 * x + (1 - y)), (1 - x, 1 - y, 2 * (1 - x) + (1 - y))]


PEERS_SIBLING, PEERS_COLUMN, PEERS_ALL = "sibling", "sibling and the same core of the other chips", "all"
COLLECTIVE_ID = dict(sum_share=0, dw_swap=1, gather_weights=2, fwd_layer_gather=3, bwd_layer_exchange=4,
                     dw_swap_exchange=5)


def _handshake(peers):
    x, y, c = _place()
    if peers == PEERS_SIBLING:
        ids = [(x, y, 1 - c)]
    elif peers == PEERS_COLUMN:
        ids = [(x, y, 1 - c)] + [(px, py, c) for px, py, _ in _other_chips(x, y)]
    else:
        ids = [(1 - x if r & 4 else x, 1 - y if r & 2 else y, 1 - c if r & 1 else c) for r in range(1, 8)]
    barrier = pltpu.get_barrier_semaphore()
    for to in ids:
        pl.semaphore_signal(barrier, inc=1, device_id=to, device_id_type=MESH)
    pl.semaphore_wait(barrier, len(ids))


def _sig(v):
    return 0.5 * jnp.tanh(0.5 * v) + 0.5


def _dot(a, b):
    return jnp.dot(a, b, preferred_element_type=F32)


def _dot_nt(a, b):
    return lax.dot_general(a, b, (((1,), (1,)), ((), ())), preferred_element_type=F32)


def _dot_tn(a, b):
    return lax.dot_general(a, b, (((0,), (0,)), ((), ())), preferred_element_type=F32)


def _segdot(v, m):
    hi = v.astype(BF16)
    lo = (v - hi.astype(F32)).astype(BF16)
    return _dot(hi, m) + _dot(lo, m)


def _colsum(v):
    return jnp.sum(v, axis=0, keepdims=True)


def _rowmean(v):
    return jnp.mean(v, axis=-1, keepdims=True)


def _lane_group(n):
    return lax.broadcasted_iota(jnp.int32, (1, n), 1) // HEAD


def _pool_cnt(tile, t_rows):
    pos = tile * t_rows + lax.broadcasted_iota(jnp.int32, (t_rows, GROUP), 0) + 1
    grp = lax.broadcasted_iota(jnp.int32, (t_rows, GROUP), 1) // HEAD
    win = jnp.where(grp == 0, 2, jnp.where(grp == 1, 4, jnp.where(grp == 2, 8, 16)))
    return jnp.minimum(pos, win).astype(F32)


def _sgu_masks(wm_ref, wmt_ref, wm_s, wmt_s):
    r = lax.broadcasted_iota(jnp.int32, (SGU_BLOCK, 4 * SGU_BLOCK), 0) // CHUNK
    c = (lax.broadcasted_iota(jnp.int32, (SGU_BLOCK, 4 * SGU_BLOCK), 1) % SGU_BLOCK) // CHUNK
    wm_s[...] = jnp.where(c <= r, wm_ref[...], 0.0).astype(BF16)
    if wmt_ref is not None:
        rt = (lax.broadcasted_iota(jnp.int32, (4 * SGU_BLOCK, SGU_BLOCK), 0) % SGU_BLOCK) // CHUNK
        ct = lax.broadcasted_iota(jnp.int32, (4 * SGU_BLOCK, SGU_BLOCK), 1) // CHUNK
        wmt_s[...] = jnp.where(rt <= ct, wmt_ref[...], 0.0).astype(BF16)


def _vstack(v_blk):
    grp = _lane_group(GROUP)
    return jnp.concatenate([jnp.where(grp == h, v_blk, 0.0) for h in range(4)], axis=0).astype(BF16)


def _gather_next(step, nt, nwi, nwo, gwi, gwo, send_sems, recv_sems, loc_sems, vwi, vwo):
    x, y, c = _place()
    me_k = 2 * x + y
    sibling = (x, y, 1 - c)
    chips = _other_chips(x, y)
    hi, ho = D_MODEL // 2, GROUP // 2
    fwd_sems = N_GATHER_SEMS // 2

    def rc(src, dst, sem, to):
        return pltpu.make_async_remote_copy(src_ref=src, dst_ref=dst, send_sem=send_sems.at[sem],
                                            recv_sem=recv_sems.at[sem], device_id=to, device_id_type=MESH)

    def blk(ref, k, n, cc):
        return ref.at[k, pl.ds(cc * n, n), :]

    def ici(r):
        px, py, _ = chips[r]
        to = (px, py, c)
        return [rc(nwi.at[pl.ds(c * hi, hi), :], blk(gwi, me_k, hi, c), 2 * r, to),
                rc(nwo.at[pl.ds(c * ho, ho), :], blk(gwo, me_k, ho, c), 2 * r + 1, to)]

    def landed(r, cc, base):
        pk = chips[r][2]
        return [rc(blk(gwi, pk, hi, cc), blk(gwi, pk, hi, cc), base + 2 * r, sibling),
                rc(blk(gwo, pk, ho, cc), blk(gwo, pk, ho, cc), base + 2 * r + 1, sibling)]

    def stage_in():
        return [pltpu.make_async_copy(nwi, vwi, loc_sems.at[0]), pltpu.make_async_copy(nwo, vwo, loc_sems.at[1])]

    def local():
        return [pltpu.make_async_copy(vwi, gwi.at[me_k], loc_sems.at[2]),
                pltpu.make_async_copy(vwo, gwo.at[me_k], loc_sems.at[3])]

    @pl.when(step == 0)
    def _():
        _handshake(PEERS_COLUMN)
        for cp in stage_in():
            cp.start()
        for r in range(3):
            for cp in ici(r):
                cp.start()

    @pl.when(step == 1)
    def _():
        for cp in stage_in():
            cp.wait()
        for cp in local():
            cp.start()

    @pl.when(step == (3 * nt) // 4)
    def _():
        for r in range(3):
            for got, fwd in zip(landed(r, c, 0), landed(r, c, fwd_sems)):
                got.wait_recv()
                fwd.start()

    @pl.when(step == nt - 1)
    def _():
        for r in range(3):
            for got in landed(r, 1 - c, fwd_sems):
                got.wait_recv()
        for r in range(3):
            for cp in ici(r) + landed(r, c, fwd_sems):
                cp.wait_send()
        for cp in local():
            cp.wait()


def _fwd_layer(larr, x, wi, bin_, caw, cbw, s256, seg, pw, wm, sb, wo, v1024, *, tile, nxt=None, target=None):
    assert nxt is None or target is None
    S = x.shape[0]
    T = tile
    nt = S // T
    alpha = float((2.0 * 4) ** 0.25)
    n_in = 13 + (2 if nxt is not None else 0) + (1 if target is not None else 0)
    n_out = 6 + (2 if nxt is not None else 0) + (1 if target is not None else 0)

    def body(*refs):
        l_ref = refs[0]
        (x_ref, wi_ref, bin_ref, caw_ref, cbw_ref, s256_ref, seg_ref, pw_ref, wm_ref, sb_ref, wo_ref,
         v1024_ref) = refs[1:13]
        y_ref, xb_ref, h_ref, aux_ref, mix_ref, z_ref = refs[n_in:n_in + 6]
        abuf, bbuf, cbuf, wm_s, shf = refs[n_in + n_out:n_in + n_out + 5]
        i = pl.program_id(0)
        if nxt is not None:
            _gather_next(i, nt, refs[13].at[l_ref[0] + 1], refs[14].at[l_ref[0] + 1], refs[n_in + 6], refs[n_in + 7],
                         *refs[n_in + n_out + 5:])

        @pl.when(i == 0)
        def _():
            abuf[0:HALO_A, :] = jnp.zeros((HALO_A, GROUP), F32)
            bbuf[0:HALO_B, :] = jnp.zeros((HALO_B, GROUP), F32)
            cbuf[0:HALO_C, :] = jnp.zeros((HALO_C, GROUP), F32)
            _sgu_masks(wm_ref, None, wm_s, None)

        x = x_ref[...]
        xb = x.astype(BF16)
        xb_ref[...] = xb
        for k in range(N_CHIPS):
            h_ref[:, COLS * k:COLS * (k + 1)] = _dot(xb, wi_ref[k]) + bin_ref[:, COLS * k:COLS * (k + 1)]

        def hs(j):
            return h_ref[:, GROUP * j:GROUP * (j + 1)]

        abuf[HALO_A:HALO_A + T, :] = hs(0) * _sig(hs(1))
        span = T + HALO_A - SUBLANES
        for p in range(1, SUBLANES):
            shf[p - 1, :, :] = abuf[p:p + span, :]
        for r0 in range(0, T, ROWS):
            acc = None
            for k in range(KA):
                off = HALO_A - (KA - 1) + k
                p, q8 = off % SUBLANES, off - off % SUBLANES
                win = abuf[r0 + q8:r0 + q8 + ROWS, :] if p == 0 else shf[p - 1, r0 + q8:r0 + q8 + ROWS, :]
                term = caw_ref[k:k + 1, :] * win
                acc = term if acc is None else acc + term
            aux_ref[r0:r0 + ROWS, 0:GROUP] = acc + s256_ref[0:1, :]
        abuf[0:HALO_A, :] = abuf[T:T + HALO_A, :]
        a1 = aux_ref[:, 0:GROUP]
        segm = seg_ref[...]
        cen = a1 - _segdot(a1, segm)
        var = _segdot(cen * cen, segm)
        a2 = cen * lax.rsqrt(var + LN_EPS) * s256_ref[1:2, :] + s256_ref[2:3, :]
        az = hs(2)
        mix_ref[:, 0:GROUP] = (a2 * _sig(a2) * (az * _sig(az))).astype(BF16)

        bbuf[HALO_B:HALO_B + T, :] = hs(4) * hs(5)
        for r0 in range(0, T, ROWS):
            acc = None
            for k in range(KB):
                off = HALO_B - (KB - 1) + k + r0
                term = cbw_ref[k:k + 1, :] * bbuf[off:off + ROWS, :]
                acc = term if acc is None else acc + term
            aux_ref[r0:r0 + ROWS, GROUP:2 * GROUP] = acc
        bbuf[0:HALO_B, :] = bbuf[T:T + HALO_B, :]
        bz = hs(6)
        mix_ref[:, GROUP:2 * GROUP] = (hs(3) * aux_ref[:, GROUP:2 * GROUP] * (bz * _sig(bz))).astype(BF16)

        ch = hs(7)
        cbuf[HALO_C:HALO_C + T, :] = ch
        hi_lane = (lax.broadcasted_iota(jnp.int32, (1, 128), 1) // HEAD) == 1
        for r0 in range(0, T, ROWS):
            def win(col, j0, j1):
                s = None
                for j in range(j0, j1):
                    off = HALO_C - j + r0
                    term = cbuf[off:off + ROWS, 128 * col:128 * (col + 1)]
                    s = term if s is None else s + term
                return s
            w0 = win(0, 0, 2) + jnp.where(hi_lane, win(0, 2, 4), 0.0)
            w1 = win(1, 0, 8) + jnp.where(hi_lane, win(1, 8, 16), 0.0)
            aux_ref[r0:r0 + ROWS, 2 * GROUP:2 * GROUP + 128] = w0
            aux_ref[r0:r0 + ROWS, 2 * GROUP + 128:3 * GROUP] = w1
        cbuf[0:HALO_C, :] = cbuf[T:T + HALO_C, :]
        pooled = aux_ref[:, 2 * GROUP:3 * GROUP] / _pool_cnt(i, T) - ch
        aux_ref[:, 2 * GROUP:3 * GROUP] = pooled
        q = _dot(pooled.astype(BF16), pw_ref[...])
        cz = hs(8)
        mix_ref[:, 2 * GROUP:3 * GROUP] = (q * s256_ref[3:4, :] * (cz * _sig(cz))).astype(BF16)

        dv = hs(10)
        cen = dv - _rowmean(dv)
        var = _rowmean(cen * cen)
        v = cen * lax.rsqrt(var + LN_EPS) * s256_ref[4:5, :] + s256_ref[5:6, :]
        sps = []
        for n in range(T // SGU_BLOCK):
            vb = v[n * SGU_BLOCK:(n + 1) * SGU_BLOCK, :]
            sps.append(_dot(wm_s[...], _vstack(vb)) + sb_ref[...])
        sp = jnp.concatenate(sps, axis=0)
        dz = hs(11)
        mix_ref[:, 3 * GROUP:4 * GROUP] = (hs(9) * sp * (dz * _sig(dz))).astype(BF16)

        out = v1024_ref[0:1, :]
        for k in range(N_CHIPS):
            out = out + _dot(mix_ref[:, GROUP * k:GROUP * (k + 1)], wo_ref[k])
        z = alpha * x + out
        z_ref[...] = z
        cen = z - _rowmean(z)
        var = _rowmean(cen * cen)
        y = cen * lax.rsqrt(var + LN_EPS) * v1024_ref[1:2, :] + v1024_ref[2:3, :]
        if target is None:
            y_ref[...] = y
        else:
            t_ref, loss_ref = refs[13], refs[n_in + 6]

            @pl.when(i == 0)
            def _():
                loss_ref[...] = jnp.zeros_like(loss_ref)
            err = y - t_ref[...]
            y_ref[...] = err * (1.0 / D_MODEL)
            loss_ref[...] += jnp.sum(_colsum(err * err), axis=1, keepdims=True) * (0.5 / D_MODEL)

    def rows(width):
        return pl.BlockSpec((T, width), lambda i, l: (i, 0))

    consts = (wi, bin_, caw, cbw, s256, seg, pw, wm, sb, wo, v1024)
    in_specs = [rows(D_MODEL)] + [_whole(a) if a is wi or a is seg or a is wo else _of_layer(a) for a in consts]
    out_specs = [rows(D_MODEL), rows(D_MODEL), rows(IN_WIDTH), rows(3 * GROUP), rows(D_MODEL), rows(D_MODEL)]
    out_shape = [jax.ShapeDtypeStruct((S, D_MODEL), F32), jax.ShapeDtypeStruct((S, D_MODEL), BF16),
                 jax.ShapeDtypeStruct((S, IN_WIDTH), F32), jax.ShapeDtypeStruct((S, 3 * GROUP), F32),
                 jax.ShapeDtypeStruct((S, D_MODEL), BF16), jax.ShapeDtypeStruct((S, D_MODEL), F32)]
    scratch = [pltpu.VMEM((T + HALO_A, GROUP), F32), pltpu.VMEM((T + HALO_B, GROUP), F32),
               pltpu.VMEM((T + HALO_C, GROUP), F32), pltpu.VMEM((SGU_BLOCK, 4 * SGU_BLOCK), BF16),
               pltpu.VMEM((SUBLANES - 1, T + HALO_A - SUBLANES, GROUP), F32)]
    extra = ()
    if nxt is not None:
        extra = tuple(nxt)
        in_specs += [ANY, ANY]
        out_specs += [ANY, ANY]
        out_shape += [jax.ShapeDtypeStruct((N_CHIPS, D_MODEL, COLS), BF16),
                      jax.ShapeDtypeStruct((N_CHIPS, GROUP, D_MODEL), BF16)]
        scratch += [pltpu.SemaphoreType.DMA((N_GATHER_SEMS,)), pltpu.SemaphoreType.DMA((N_GATHER_SEMS,)),
                    pltpu.SemaphoreType.DMA((4,)), pltpu.VMEM((D_MODEL, COLS), BF16), pltpu.VMEM((GROUP, D_MODEL), BF16)]
    if target is not None:
        extra = (target,)
        in_specs += [rows(D_MODEL)]
        out_specs += [pl.BlockSpec((8, 128), lambda i, l: (0, 0))]
        out_shape += [jax.ShapeDtypeStruct((8, 128), F32)]
    grid_spec = pltpu.PrefetchScalarGridSpec(num_scalar_prefetch=1, grid=(nt,), in_specs=in_specs,
                                             out_specs=out_specs, scratch_shapes=scratch)
    return pl.pallas_call(
        body, name=("fwd_layer_loss" if target is not None else "fwd_layer") if nxt is None else "fwd_layer_gather",
        grid_spec=grid_spec, out_shape=out_shape,
        compiler_params=_vmem_params(dimension_semantics=("arbitrary",), **(
            dict(has_side_effects=True, collective_id=COLLECTIVE_ID["fwd_layer_gather"]) if nxt is not None else {})),
    )(larr, x, *consts, *extra)


ROW_CBW = 8
ROW_CAW = 16
ROW_LOSS = 7
ROW_PW = 48
ROW_LNG = 112
ROW_LNB = 116
ROW_BOUT = 120
ROW_BIN = 124
ROW_WC = 136
ROW_SB = 392
SM_ROWS = 400


def _exchange_comm(start, mid, finish, l, p_i, p_o, sm, r_i, r_o, r_sm, send_sems, recv_sems, loc_sems, vm):
    x, y, c = _place()
    me_k = 2 * x + y
    chips = _other_chips(x, y)

    def rc(src, dst, sem, to):
        return pltpu.make_async_remote_copy(src_ref=src, dst_ref=dst, send_sem=send_sems.at[sem],
                                            recv_sem=recv_sems.at[sem], device_id=to, device_id_type=MESH)

    def big(r):
        px, py, pk = chips[r]
        to = (px, py, c)
        return [rc(p_i.at[l, pk], r_i.at[r, l], 2 * r, to), rc(p_o.at[l, pk], r_o.at[r, l], 2 * r + 1, to)]

    def stage():
        return pltpu.make_async_copy(sm, vm.at[0], loc_sems.at[0])

    def to_sibling():
        return rc(sm, vm.at[1], N_EXCH_SEMS - 4, (x, y, 1 - c))

    def chip_sum(r):
        px, py, pk = chips[r]
        return rc(vm.at[2], r_sm.at[me_k], N_EXCH_SEMS - 3 + r, (px, py, c))

    def keep():
        return pltpu.make_async_copy(vm.at[2], r_sm.at[me_k], loc_sems.at[1])

    with_big, with_small = p_i is not None, sm is not None

    @pl.when(start)
    def _():
        _handshake(PEERS_COLUMN)
        if with_small:
            stage().start()
            to_sibling().start()
        if with_big:
            for r in range(3):
                for cp in big(r):
                    cp.start()

    if with_small:
        @pl.when(mid)
        def _():
            stage().wait()
            to_sibling().wait_recv()
            vm[2] = vm[0] + vm[1]
            keep().start()
            for r in range(3):
                chip_sum(r).start()

    @pl.when(finish)
    def _():
        if with_big:
            for r in range(3):
                for cp in big(r):
                    cp.wait()
        if with_small:
            to_sibling().wait_send()
            for r in range(3):
                chip_sum(r).wait()
            keep().wait()


RC = 32
RC_WIDE = 16
ACC_ROWS = 136


def _rsum8(v):
    r = v[0:8]
    for j in range(1, v.shape[0] // 8):
        r = r + v[8 * j:8 * j + 8]
    return r


def _bwd_layer(larr, dy, z, h, aux, wi, caw, cbw, s256, seg, pw, wm, wmt, sb, wo, v1024, e4, *, tile, exch=None):
    S = dy.shape[0]
    T = tile
    nt = S // T
    nblk = T // SGU_BLOCK
    alpha = float((2.0 * 4) ** 0.25)
    n_in = 17 + (5 if exch is not None else 0)
    n_out = 4 + (3 if exch is not None else 0)
    slab = pltpu.VMEM((T, GROUP), F32)
    scratch = dict(
        dbuf=pltpu.VMEM((T + HALO_A, GROUP), F32), ebuf=pltpu.VMEM((T + HALO_B, GROUP), F32),
        fbuf=pltpu.VMEM((T + HALO_C, GROUP), F32), sh=pltpu.VMEM((SUBLANES - 1, T + HALO_A - SUBLANES, GROUP), F32),
        wm_s=pltpu.VMEM((SGU_BLOCK, 4 * SGU_BLOCK), BF16), wmt_s=pltpu.VMEM((4 * SGU_BLOCK, SGU_BLOCK), BF16),
        dsp_acc=pltpu.VMEM((SGU_BLOCK, GROUP), F32), pw_acc=pltpu.VMEM((GROUP, GROUP), F32),
        acc_s=pltpu.VMEM((8 * ACC_ROWS, GROUP), F32), acc_w=pltpu.VMEM((24, D_MODEL), F32),
        dmix_s=pltpu.VMEM((T, D_MODEL), F32), vst_s=pltpu.VMEM((nblk, 4 * SGU_BLOCK, GROUP), BF16),
        dq_s=pltpu.VMEM((T, GROUP), BF16), dxt_s=pltpu.VMEM((D_MODEL, T), F32),
        mean_s=slab, t1_s=slab, t2_s=slab, q_s=slab, xv_s=slab, rv_s=slab, v_s=slab, sp_s=slab, a0_s=slab, sg_s=slab,
        xh_s=slab, ra_s=slab, ub_s=slab, dsp_s=slab, m1_s=slab, m2_s=slab, dpool_s=slab, dvd_s=slab, u_s=slab,
        du_s=slab, cw_s=slab)
    names = list(scratch)

    def body(*refs):
        (dy_ref, z_ref, h_ref, aux_ref, wi_ref, caw_ref, cbw_ref, s256_ref, seg_ref, pw_ref, wm_ref, wmt_ref,
         sb_ref, wo_ref, v1024_ref, e4_ref) = refs[1:17]
        dx_ref, dhb_ref, dzb_ref, osm_ref = refs[n_in:n_in + 4]
        k0 = n_in + n_out
        sc = dict(zip(names, refs[k0:k0 + len(names)]))
        dbuf, ebuf, fbuf, sh = sc["dbuf"], sc["ebuf"], sc["fbuf"], sc["sh"]
        wm_s, wmt_s, dsp_acc, pw_acc, acc_s, acc_w = (sc[n] for n in ("wm_s", "wmt_s", "dsp_acc", "pw_acc", "acc_s",
                                                                        "acc_w"))
        dmix_s, vst_s, dq_s = sc["dmix_s"], sc["vst_s"], sc["dq_s"]
        i = pl.program_id(0)
        tile_idx = nt - 1 - i
        if exch is not None:
            p_i, p_o, sm = refs[17:20]
            r_i, r_o, r_sm = refs[n_in + 4:n_in + 7]
            _exchange_comm(i == 0, i == 1, i == nt - 1, refs[0][0] + 1, p_i, p_o, sm, r_i, r_o, r_sm, *refs[k0 + len(names):])

        @pl.when(i == 0)
        def _():
            dbuf[T:T + HALO_A, :] = jnp.zeros((HALO_A, GROUP), F32)
            ebuf[T:T + HALO_B, :] = jnp.zeros((HALO_B, GROUP), F32)
            fbuf[T:T + HALO_C, :] = jnp.zeros((HALO_C, GROUP), F32)
            _sgu_masks(wm_ref, wmt_ref, wm_s, wmt_s)
            osm_ref[...] = jnp.zeros_like(osm_ref)
            dsp_acc[...] = jnp.zeros_like(dsp_acc)
            pw_acc[...] = jnp.zeros_like(pw_acc)
            acc_s[...] = jnp.zeros_like(acc_s)
            acc_w[...] = jnp.zeros_like(acc_w)

        def chunks(rc, fn):
            for c in range(T // rc):
                fn(pl.ds(c * rc, rc))

        def hs(j, rows):
            return h_ref[rows, GROUP * j:GROUP * (j + 1)]

        def acc_add(row, val):
            acc_s[8 * row:8 * row + 8, :] += _rsum8(val)

        def put_dh(j, rows, val):
            acc_add(ROW_BIN + j, val)
            dhb_ref[rows, GROUP * j:GROUP * (j + 1)] = val.astype(BF16)

        def dsilu(v, s):
            return s * (1.0 + v * (1.0 - s))

        def vec(r):
            return s256_ref[r:r + 1, :]

        def ln_bwd(rows):
            dyc = dy_ref[rows, :]
            zc = z_ref[rows, :]
            cen = zc - _rowmean(zc)
            rstd = lax.rsqrt(_rowmean(cen * cen) + LN_EPS)
            xhat = cen * rstd
            acc_w[0:8, :] += _rsum8(dyc * xhat)
            acc_w[8:16, :] += _rsum8(dyc)
            gdy = dyc * v1024_ref[1:2, :]
            dz = rstd * (gdy - _rowmean(gdy) - xhat * _rowmean(gdy * xhat))
            acc_w[16:24, :] += _rsum8(dz)
            dzb_ref[rows, :] = dz.astype(BF16)
            dx_ref[rows, :] = alpha * dz
        chunks(RC_WIDE, ln_bwd)

        segm = seg_ref[...]
        dzb = dzb_ref[...]
        for k in range(N_CHIPS):
            dmix_s[:, GROUP * k:GROUP * (k + 1)] = _dot_nt(dzb, wo_ref[k])
        sc["mean_s"][...] = _segdot(aux_ref[:, 0:GROUP], segm)
        pooled_b = aux_ref[:, 2 * GROUP:3 * GROUP].astype(BF16)
        sc["q_s"][...] = _dot(pooled_b, pw_ref[...])

        def centre(rows):
            cen = aux_ref[rows, 0:GROUP] - sc["mean_s"][rows, :]
            sc["t1_s"][rows, :] = cen * cen
            dv_in = hs(10, rows)
            cen_v = dv_in - _rowmean(dv_in)
            rstd_v = lax.rsqrt(_rowmean(cen_v * cen_v) + LN_EPS)
            xv = cen_v * rstd_v
            sc["xv_s"][rows, :] = xv
            sc["rv_s"][rows, :] = jnp.broadcast_to(rstd_v, xv.shape)
            sc["v_s"][rows, :] = xv * vec(4) + vec(5)
        chunks(RC, centre)

        sc["t2_s"][...] = _segdot(sc["t1_s"][...], segm)
        for n in range(nblk):
            blk = slice(n * SGU_BLOCK, (n + 1) * SGU_BLOCK)
            vst_s[n] = _vstack(sc["v_s"][blk, :])
            sc["sp_s"][blk, :] = _dot(wm_s[...], vst_s[n]) + sb_ref[...]

        def mixers(rows):
            a_val, a_glu, a_z = hs(0, rows), hs(1, rows), hs(2, rows)
            sg = _sig(a_glu)
            sc["a0_s"][rows, :] = a_val * sg
            sc["sg_s"][rows, :] = sg
            rstd_a = lax.rsqrt(sc["t2_s"][rows, :] + LN_EPS)
            xh = (aux_ref[rows, 0:GROUP] - sc["mean_s"][rows, :]) * rstd_a
            a2 = xh * vec(1) + vec(2)
            s2 = _sig(a2)
            sz = _sig(a_z)
            dya = dmix_s[rows, 0:GROUP]
            put_dh(2, rows, dya * (a2 * s2) * dsilu(a_z, sz))
            d_a2 = dya * (a_z * sz) * dsilu(a2, s2)
            acc_add(1, d_a2 * xh)
            acc_add(2, d_a2)
            gd = d_a2 * vec(1)
            sc["t1_s"][rows, :] = gd
            sc["t2_s"][rows, :] = gd * xh
            sc["xh_s"][rows, :] = xh
            sc["ra_s"][rows, :] = rstd_a
            b_b, b_c, b_h, b_z = hs(3, rows), hs(4, rows), hs(5, rows), hs(6, rows)
            cb = aux_ref[rows, GROUP:2 * GROUP]
            sz = _sig(b_z)
            dyb = dmix_s[rows, GROUP:2 * GROUP]
            put_dh(3, rows, dyb * cb * (b_z * sz))
            put_dh(6, rows, dyb * b_b * cb * dsilu(b_z, sz))
            ebuf[rows, :] = dyb * b_b * (b_z * sz)
            sc["ub_s"][rows, :] = b_c * b_h
            c_z = hs(8, rows)
            q = sc["q_s"][rows, :]
            sz = _sig(c_z)
            dyc = dmix_s[rows, 2 * GROUP:3 * GROUP]
            acc_add(3, dyc * q * (c_z * sz))
            put_dh(8, rows, dyc * q * vec(3) * dsilu(c_z, sz))
            dq_s[rows, :] = (dyc * vec(3) * (c_z * sz)).astype(BF16)
            d_u, d_z = hs(9, rows), hs(11, rows)
            sp = sc["sp_s"][rows, :]
            sz = _sig(d_z)
            dyd = dmix_s[rows, 3 * GROUP:4 * GROUP]
            put_dh(9, rows, dyd * sp * (d_z * sz))
            put_dh(11, rows, dyd * d_u * sp * dsilu(d_z, sz))
            sc["dsp_s"][rows, :] = dyd * d_u * (d_z * sz)
        chunks(RC, mixers)

        sc["m1_s"][...] = _segdot(sc["t1_s"][...], segm)
        sc["m2_s"][...] = _segdot(sc["t2_s"][...], segm)
        d_q = dq_s[...]
        pw_acc[...] += _dot_tn(pooled_b, d_q)
        sc["dpool_s"][...] = _dot_nt(d_q, pw_ref[...])
        grp = _lane_group(GROUP)
        for n in range(nblk):
            blk = slice(n * SGU_BLOCK, (n + 1) * SGU_BLOCK)
            dspb = sc["dsp_s"][blk, :]
            dsp_acc[...] += dspb
            dspb16 = dspb.astype(BF16)
            dvst = _dot(wmt_s[...], dspb16)
            dvb = None
            for hh in range(4):
                part = jnp.where(grp == hh, dvst[hh * SGU_BLOCK:(hh + 1) * SGU_BLOCK, :], 0.0)
                dvb = part if dvb is None else dvb + part
            sc["dvd_s"][blk, :] = dvb
            dwc = _dot_nt(dspb16, vst_s[n])
            osm_ref[ROW_WC:ROW_WC + SGU_BLOCK, :] += dwc[:, 0:GROUP]
            osm_ref[ROW_WC + SGU_BLOCK:ROW_WC + 2 * SGU_BLOCK, :] += dwc[:, GROUP:2 * GROUP]

        def ln_sums(rows):
            xh = sc["xh_s"][rows, :]
            d_a1 = sc["ra_s"][rows, :] * (sc["t1_s"][rows, :] - sc["m1_s"][rows, :] - xh * sc["m2_s"][rows, :])
            acc_add(0, d_a1)
            dbuf[rows, :] = d_a1
            pos = tile_idx * T + rows.start + lax.broadcasted_iota(jnp.int32, (RC, GROUP), 0) + 1
            lane = lax.broadcasted_iota(jnp.int32, (RC, GROUP), 1) // HEAD
            win = jnp.where(lane == 0, 2, jnp.where(lane == 1, 4, jnp.where(lane == 2, 8, 16)))
            fbuf[rows, :] = sc["dpool_s"][rows, :] / jnp.minimum(pos, win).astype(F32)
            d_v = sc["dvd_s"][rows, :]
            xv = sc["xv_s"][rows, :]
            acc_add(4, d_v * xv)
            acc_add(5, d_v)
            gd = d_v * vec(4)
            put_dh(10, rows, sc["rv_s"][rows, :] * (gd - _rowmean(gd) - xv * _rowmean(gd * xv)))
        chunks(RC, ln_sums)

        span = T + HALO_A - SUBLANES
        for p in range(1, SUBLANES):
            sh[p - 1, :, :] = dbuf[p:p + span, :]

        for r0 in range(0, T, ROWS):
            uc = sc["ub_s"][r0:r0 + ROWS, :]
            acc = None
            for k in range(KB):
                off = (KB - 1) - k + r0
                w = ebuf[off:off + ROWS, :]
                term = cbw_ref[k:k + 1, :] * w
                acc = term if acc is None else acc + term
                acc_add(ROW_CBW + k, uc * w)
            sc["du_s"][r0:r0 + ROWS, :] = acc
        ebuf[T:T + HALO_B, :] = ebuf[0:HALO_B, :]

        hi_lane = (lax.broadcasted_iota(jnp.int32, (1, 128), 1) // HEAD) == 1
        for r0 in range(0, T, ROWS):
            def win(col, j0, j1):
                s = None
                for j in range(j0, j1):
                    term = fbuf[r0 + j:r0 + j + ROWS, 128 * col:128 * (col + 1)]
                    s = term if s is None else s + term
                return s
            sc["cw_s"][r0:r0 + ROWS, 0:128] = win(0, 0, 2) + jnp.where(hi_lane, win(0, 2, 4), 0.0)
            sc["cw_s"][r0:r0 + ROWS, 128:256] = win(1, 0, 8) + jnp.where(hi_lane, win(1, 8, 16), 0.0)
        fbuf[T:T + HALO_C, :] = fbuf[0:HALO_C, :]

        def rest_bc(rows):
            d_u = sc["du_s"][rows, :]
            put_dh(4, rows, d_u * hs(5, rows))
            put_dh(5, rows, d_u * hs(4, rows))
            put_dh(7, rows, sc["cw_s"][rows, :] - sc["dpool_s"][rows, :])
        chunks(RC, rest_bc)

        dxt_s = sc["dxt_s"]

        def dx_term(k):
            term = _dot_nt(wi_ref[k], dhb_ref[:, COLS * k:COLS * (k + 1)])
            if k == 1:
                dxt_s[...] = term
            else:
                dxt_s[...] += term

        def conv_a(rows):
            a0c = sc["a0_s"][rows, :]
            acc = None
            for k in range(KA):
                off = (KA - 1) - k
                p, q8 = off % SUBLANES, off - off % SUBLANES
                w = dbuf[pl.ds(rows.start + q8, RC), :] if p == 0 else sh[p - 1, pl.ds(rows.start + q8, RC), :]
                term = caw_ref[k:k + 1, :] * w
                acc = term if acc is None else acc + term
                acc_add(ROW_CAW + k, a0c * w)
            sc["u_s"][rows, :] = acc
        n_chunks = T // RC
        after = {(n_chunks * j) // 3: j + 1 for j in range(3)}
        for c in range(n_chunks):
            conv_a(pl.ds(c * RC, RC))
            if c in after:
                dx_term(after[c])
        dbuf[T:T + HALO_A, :] = dbuf[0:HALO_A, :]

        def rest_a(rows):
            d_a0 = sc["u_s"][rows, :]
            sg = sc["sg_s"][rows, :]
            put_dh(0, rows, d_a0 * sg)
            put_dh(1, rows, d_a0 * hs(0, rows) * sg * (1.0 - sg))
        chunks(RC, rest_a)
        dx_term(0)
        dx_ref[...] += dxt_s[...].T

        @pl.when(i == nt - 1)
        def _():
            for row in list(range(6)) + list(range(ROW_CBW, ROW_CBW + KB)) + list(range(ROW_CAW, ROW_CAW + KA)) + list(
                    range(ROW_BIN, ROW_BIN + N_SLICES)):
                osm_ref[row:row + 1, :] = _colsum(acc_s[8 * row:8 * row + 8, :])
            for j, row in enumerate((ROW_LNG, ROW_LNB, ROW_BOUT)):
                cs = _colsum(acc_w[8 * j:8 * j + 8, :])
                for q in range(D_MODEL // GROUP):
                    osm_ref[row + q:row + q + 1, :] = cs[:, GROUP * q:GROUP * (q + 1)]
            r = lax.broadcasted_iota(jnp.int32, (SGU_BLOCK, GROUP), 0) // CHUNK
            c = (lax.broadcasted_iota(jnp.int32, (SGU_BLOCK, GROUP), 1) % SGU_BLOCK) // CHUNK
            for half in range(2):
                rows_ = slice(ROW_WC + half * SGU_BLOCK, ROW_WC + (half + 1) * SGU_BLOCK)
                osm_ref[rows_, :] = jnp.where(c <= r, osm_ref[rows_, :], 0.0)
            sb_t = _segdot(dsp_acc[...], e4_ref[...]).T
            osm_ref[ROW_SB:ROW_SB + 8, 0:SGU_BLOCK] = sb_t[0:8, :]
            for g in range(4):
                osm_ref[ROW_PW:ROW_PW + HEAD, HEAD * g:HEAD * (g + 1)] = (
                    pw_acc[HEAD * g:HEAD * (g + 1), HEAD * g:HEAD * (g + 1)])

    def rows(width):
        return pl.BlockSpec((T, width), lambda i, l: (nt - 1 - i, 0))

    consts = (wi, caw, cbw, s256, seg, pw, wm, wmt, sb, wo, v1024, e4)
    unstacked = (wi, seg, wo, e4)
    in_specs = [rows(D_MODEL), rows(D_MODEL), rows(IN_WIDTH), rows(3 * GROUP)] + [
        _whole(a) if any(a is u for u in unstacked) else _of_layer(a) for a in consts]
    out_specs = [rows(D_MODEL), rows(IN_WIDTH), rows(D_MODEL), pl.BlockSpec((SM_ROWS, GROUP), lambda i, l: (0, 0))]
    out_shape = [jax.ShapeDtypeStruct((S, D_MODEL), F32), jax.ShapeDtypeStruct((S, IN_WIDTH), BF16),
                 jax.ShapeDtypeStruct((S, D_MODEL), BF16), jax.ShapeDtypeStruct((SM_ROWS, GROUP), F32)]
    scratch_shapes = list(scratch.values())
    extra, aliases = (), {}
    if exch is not None:
        extra = tuple(exch)
        r_i, r_o = exch[3], exch[4]
        in_specs += [ANY] * 5
        out_specs += [ANY] * 3
        out_shape += [jax.ShapeDtypeStruct(r_i.shape, r_i.dtype), jax.ShapeDtypeStruct(r_o.shape, r_o.dtype),
                      jax.ShapeDtypeStruct((N_CHIPS, SM_ROWS, GROUP), F32)]
        scratch_shapes += [pltpu.SemaphoreType.DMA((N_EXCH_SEMS,)), pltpu.SemaphoreType.DMA((N_EXCH_SEMS,)),
                           pltpu.SemaphoreType.DMA((2,)), pltpu.VMEM((3, SM_ROWS, GROUP), F32)]
        aliases = {20: 4, 21: 5}
    grid_spec = pltpu.PrefetchScalarGridSpec(num_scalar_prefetch=1, grid=(nt,), in_specs=in_specs,
                                             out_specs=out_specs, scratch_shapes=scratch_shapes)
    return pl.pallas_call(
        body, name="bwd_layer" if exch is None else "bwd_layer_exchange",
        grid_spec=grid_spec, out_shape=out_shape, input_output_aliases=aliases,
        compiler_params=_vmem_params(dimension_semantics=("arbitrary",), **(
            dict(has_side_effects=True, collective_id=COLLECTIVE_ID["bwd_layer_exchange"]) if exch is not None else {})),
    )(larr, dy, z, h, aux, *consts, *extra)


def _dw_swap(cl_arr, xb, dhb, mixb, dzb, p_i, p_o, *, k_steps, last=None):
    S = xb.shape[0]
    tk = S // k_steps
    n_steps = N_CHIPS + k_steps
    hi, ho = p_i.shape[2], p_o.shape[2]
    n_in = 7 + (3 if last is not None else 0)
    n_out = 2 + (3 if last is not None else 0)

    def body(*refs):
        cl_ref, x_ref, dh_ref, mix_ref, dz_ref = refs[0:5]
        pi_ref, po_ref = refs[n_in:n_in + 2]
        own_i, acc_o, snd_i, snd_o, rcv_i, rcv_o, send_sems, recv_sems = refs[n_in + n_out:n_in + n_out + 8]
        j = pl.program_id(0)
        l, me_k = cl_ref[1], cl_ref[2]
        x, y, c = _place()
        mine_o, theirs_o = (pl.ds(pl.multiple_of(cc * ho, ho), ho) for cc in (c, 1 - c))

        def to_sibling(src, dst, sem):
            return pltpu.make_async_remote_copy(src_ref=src, dst_ref=dst, send_sem=send_sems.at[sem],
                                                recv_sem=recv_sems.at[sem], device_id=(x, y, 1 - c), device_id_type=MESH)

        def chunk_of(s):
            return (me_k + 1 + s) % N_CHIPS

        def chunk_copy(s):
            return to_sibling(snd_i.at[s % 2], rcv_i.at[chunk_of(s)], s)

        def out_copy():
            return to_sibling(snd_o, rcv_o, N_CHIPS)

        if last is None:
            @pl.when(j == 0)
            def _():
                _handshake(PEERS_SIBLING)
        else:
            qi_ref, qo_ref, r_sm = refs[n_in + 2:n_in + 5]
            out_sems, in_sems = refs[n_in + n_out + 8:n_in + n_out + 10]
            _exchange_comm(j == 0, j == 1, j == n_steps - 1, None, None, None, refs[7], None, None, r_sm,
                           *refs[n_in + n_out + 10:])
            chips = _other_chips(x, y)

            def onward(r):
                px, py, pk = chips[r]
                return [pltpu.make_async_remote_copy(
                    src_ref=v.at[pk], dst_ref=q.at[r, l], send_sem=out_sems.at[2 * r + n], recv_sem=in_sems.at[2 * r + n],
                    device_id=(px, py, c), device_id_type=MESH) for n, (v, q) in enumerate(((rcv_i, qi_ref), (rcv_o, qo_ref)))]

        @pl.when(j < N_CHIPS)
        def _():
            @pl.when(j >= 2)
            def _():
                chunk_copy(j - 2).wait_send()

            acc = _dot_tn(x_ref[...], dh_ref[...])
            top, bottom = acc[:hi], acc[hi:]
            own_i[j % 2] = jnp.where(c == 0, top, bottom)
            snd_i[j % 2] = jnp.where(c == 0, bottom, top).astype(BF16)
            chunk_copy(j).start()

        @pl.when(j == N_CHIPS)
        def _():
            acc_o[...] = jnp.zeros_like(acc_o)

        @pl.when(j >= N_CHIPS)
        def _():
            acc_o[...] += _dot_tn(mix_ref[...], dz_ref[...]).reshape(N_CHIPS, GROUP, D_MODEL)

        @pl.when((j >= 1) & (j <= N_CHIPS))
        def _():
            chunk_copy(j - 1).wait_recv()
            summed = (own_i[(j - 1) % 2] + rcv_i[chunk_of(j - 1)].astype(F32)).astype(pi_ref.dtype)
            pi_ref[...] = summed
            if last is not None:
                rcv_i[chunk_of(j - 1)] = summed
                for r in range(3):
                    @pl.when(chunk_of(j - 1) == chips[r][2])
                    def _():
                        onward(r)[0].start()

        @pl.when(j == n_steps - 1)
        def _():
            snd_o[...] = acc_o[:, theirs_o, :].astype(BF16)
            out_copy().start()
            for k in (N_CHIPS - 2, N_CHIPS - 1):
                chunk_copy(k).wait_send()
            out_copy().wait_recv()
            summed = (acc_o[:, mine_o, :] + rcv_o[...].astype(F32)).astype(po_ref.dtype)
            po_ref[...] = summed
            if last is not None:
                rcv_o[...] = summed
                for r in range(3):
                    onward(r)[1].start()
            out_copy().wait_send()
            if last is not None:
                for r in range(3):
                    for cp in onward(r):
                        cp.wait()

    def col_block(j, cl):
        return (cl[2] + 1 + jnp.clip(j, 0, N_CHIPS - 1)) % N_CHIPS

    def tok_block(j):
        return jnp.maximum(j - N_CHIPS, 0)

    in_specs = [pl.BlockSpec((S, D_MODEL), lambda j, cl: (0, 0)),
                pl.BlockSpec((S, COLS), lambda j, cl: (0, col_block(j, cl))),
                pl.BlockSpec((tk, D_MODEL), lambda j, cl: (tok_block(j), 0)),
                pl.BlockSpec((tk, D_MODEL), lambda j, cl: (tok_block(j), 0)), ANY, ANY]
    out_specs = [pl.BlockSpec((None, None, hi, COLS), lambda j, cl: (cl[1], col_block(j - 1, cl), 0, 0)),
                 pl.BlockSpec((None, N_CHIPS, ho, D_MODEL), lambda j, cl: (cl[1], 0, 0, 0))]
    out_shape = [jax.ShapeDtypeStruct(p_i.shape, p_i.dtype), jax.ShapeDtypeStruct(p_o.shape, p_o.dtype)]
    scratch = [pltpu.VMEM((2, hi, COLS), F32), pltpu.VMEM((N_CHIPS, GROUP, D_MODEL), F32),
               pltpu.VMEM((2, hi, COLS), BF16), pltpu.VMEM((N_CHIPS, ho, D_MODEL), BF16),
               pltpu.VMEM((N_CHIPS, hi, COLS), BF16), pltpu.VMEM((N_CHIPS, ho, D_MODEL), BF16),
               pltpu.SemaphoreType.DMA((N_CHIPS + 1,)), pltpu.SemaphoreType.DMA((N_CHIPS + 1,))]
    extra, aliases, kind = (), {5: 0, 6: 1}, "dw_swap"
    if last is not None:
        extra, aliases, kind = tuple(last), {5: 0, 6: 1, 8: 2, 9: 3}, "dw_swap_exchange"
        in_specs += [ANY, ANY, ANY]
        out_specs += [ANY, ANY, ANY]
        out_shape += [jax.ShapeDtypeStruct(q.shape, q.dtype) for q in last[1:]]
        out_shape += [jax.ShapeDtypeStruct((N_CHIPS, SM_ROWS, GROUP), F32)]
        scratch += [pltpu.SemaphoreType.DMA((6,)), pltpu.SemaphoreType.DMA((6,)), pltpu.SemaphoreType.DMA((N_EXCH_SEMS,)),
                    pltpu.SemaphoreType.DMA((N_EXCH_SEMS,)), pltpu.SemaphoreType.DMA((2,)),
                    pltpu.VMEM((3, SM_ROWS, GROUP), F32)]
    grid_spec = pltpu.PrefetchScalarGridSpec(
        num_scalar_prefetch=1, grid=(n_steps,), in_specs=in_specs, out_specs=out_specs, scratch_shapes=scratch)
    return pl.pallas_call(
        body, name=kind, grid_spec=grid_spec, out_shape=out_shape, input_output_aliases=aliases,
        compiler_params=_vmem_params(dimension_semantics=("arbitrary",), has_side_effects=True,
                                     collective_id=COLLECTIVE_ID[kind]),
    )(cl_arr, xb, dhb, mixb, dzb, p_i, p_o, *extra)


def _adamw_math(w, g, m, v):
    nm = ADAM_B1 * m + (1.0 - ADAM_B1) * g
    nv = ADAM_B2 * v + (1.0 - ADAM_B2) * (g * g)
    c1 = 1.0 - ADAM_B1 ** ADAM_STEP
    c2 = 1.0 - ADAM_B2 ** ADAM_STEP
    return -ADAM_LR * ((nm / c1) / (jnp.sqrt(nv / c2) + ADAM_EPS) + ADAM_WD * w), nm, nv


def _adamw_small(ws, gs, ms, vs):
    n = len(ws)

    def body(*refs):
        for j in range(n):
            d, nm, nv = _adamw_math(*(refs[k * n + j][...] for k in range(4)))
            refs[4 * n + j][...] = d
            refs[5 * n + j][...] = nm
            refs[6 * n + j][...] = nv

    shapes = [jax.ShapeDtypeStruct(w.shape, F32) for w in ws]
    outs = pl.pallas_call(body, name="adamw_small", out_shape=shapes * 3, compiler_params=_vmem_params())(
        *ws, *gs, *ms, *vs)
    return outs[0:n], outs[n:2 * n], outs[2 * n:3 * n]


def _adamw(w, g, m, v, *, rows_per_step, name, copy_g=False):
    R, C = w.shape
    tr = rows_per_step

    def body(w_ref, g_ref, m_ref, v_ref, d_ref, nm_ref, nv_ref, *g_out):
        g_ = g_ref[...]
        d_ref[...], nm_ref[...], nv_ref[...] = _adamw_math(w_ref[...], g_, m_ref[...], v_ref[...])
        if copy_g:
            g_out[0][...] = g_

    spec = pl.BlockSpec((tr, C), lambda i: (i, 0))
    n_out = 4 if copy_g else 3
    return pl.pallas_call(
        body, name=name, grid=(R // tr,),
        in_specs=[spec] * 4, out_specs=[spec] * n_out,
        out_shape=[jax.ShapeDtypeStruct((R, C), F32)] * n_out,
        compiler_params=_vmem_params(dimension_semantics=("arbitrary",)),
    )(w, g, m, v)


def _gather_weights(wi16, wo16, cw):
    L = wi16.shape[0]
    hi_rows, ho_rows = D_MODEL // 2, GROUP // 2
    n_ici = 2 * L + 1
    n_fwd = 2 * L

    def body(wi_ref, wo_ref, cw_ref, *rest):
        wig = rest[0:L]
        wog = rest[L:2 * L]
        cwg = rest[2 * L]
        send_sems, recv_sems, loc_sems, vwi, vwo, vcw = rest[2 * L + 1:]
        x, y, c = _place()
        me_k = 2 * x + y
        sibling = (x, y, 1 - c)
        chips = _other_chips(x, y)

        def half_i(ref, blk):
            return ref.at[blk, pl.ds(c * hi_rows, hi_rows), :]

        def half_o(ref, blk):
            return ref.at[blk, pl.ds(c * ho_rows, ho_rows), :]

        def other_half_i(ref, blk):
            return ref.at[blk, pl.ds((1 - c) * hi_rows, hi_rows), :]

        def other_half_o(ref, blk):
            return ref.at[blk, pl.ds((1 - c) * ho_rows, ho_rows), :]

        stage_in = [pltpu.make_async_copy(wi_ref, vwi, loc_sems.at[0]), pltpu.make_async_copy(wo_ref, vwo, loc_sems.at[1]),
                    pltpu.make_async_copy(cw_ref, vcw, loc_sems.at[2])]
        local = []
        for l in range(L):
            local.append(pltpu.make_async_copy(vwi.at[l], wig[l].at[me_k], loc_sems.at[3 + 2 * l]))
            local.append(pltpu.make_async_copy(vwo.at[l], wog[l].at[me_k], loc_sems.at[3 + 2 * l + 1]))
        local.append(pltpu.make_async_copy(vcw, cwg.at[me_k], loc_sems.at[3 + 2 * L]))
        _handshake(PEERS_COLUMN)
        for cp in stage_in:
            cp.start()

        def remote(src, dst, sem, to):
            return pltpu.make_async_remote_copy(src_ref=src, dst_ref=dst, send_sem=send_sems.at[sem],
                                                recv_sem=recv_sems.at[sem], device_id=to, device_id_type=MESH)

        sends = []
        for r, (px, py, _) in enumerate(chips):
            to = (px, py, c)
            for l in range(L):
                sends.append(remote(half_i(wi_ref, l), half_i(wig[l], me_k), r * n_ici + 2 * l, to))
                sends.append(remote(half_o(wo_ref, l), half_o(wog[l], me_k), r * n_ici + 2 * l + 1, to))
            sends.append(remote(cw_ref, cwg.at[me_k], r * n_ici + 2 * L, to))
        for cp in sends:
            cp.start()
        for cp in stage_in:
            cp.wait()
        for cp in local:
            cp.start()

        base = 3 * n_ici
        fwds = []
        for r, (px, py, pk) in enumerate(chips):
            for l in range(L):
                remote(half_i(wig[l], pk), half_i(wig[l], pk), r * n_ici + 2 * l, sibling).wait_recv()
                f = remote(half_i(wig[l], pk), half_i(wig[l], pk), base + r * n_fwd + 2 * l, sibling)
                f.start()
                fwds.append(f)
                remote(half_o(wog[l], pk), half_o(wog[l], pk), r * n_ici + 2 * l + 1, sibling).wait_recv()
                f = remote(half_o(wog[l], pk), half_o(wog[l], pk), base + r * n_fwd + 2 * l + 1, sibling)
                f.start()
                fwds.append(f)
            remote(cwg.at[pk], cwg.at[pk], r * n_ici + 2 * L, sibling).wait_recv()
        for r, (px, py, pk) in enumerate(chips):
            for l in range(L):
                remote(other_half_i(wig[l], pk), other_half_i(wig[l], pk), base + r * n_fwd + 2 * l, sibling).wait_recv()
                remote(other_half_o(wog[l], pk), other_half_o(wog[l], pk), base + r * n_fwd + 2 * l + 1, sibling).wait_recv()
        for cp in sends + fwds:
            cp.wait_send()
        for cp in local:
            cp.wait()

    n_sem = 3 * n_ici + 3 * n_fwd
    out_shape = ([jax.ShapeDtypeStruct((N_CHIPS, D_MODEL, COLS), BF16)] * L
                 + [jax.ShapeDtypeStruct((N_CHIPS, GROUP, D_MODEL), BF16)] * L
                 + [jax.ShapeDtypeStruct((N_CHIPS,) + cw.shape, F32)])
    outs = pl.pallas_call(
        body, name="gather_weights",
        in_specs=[ANY, ANY, ANY], out_specs=[ANY] * (2 * L + 1), out_shape=out_shape,
        scratch_shapes=[pltpu.SemaphoreType.DMA((n_sem,)), pltpu.SemaphoreType.DMA((n_sem,)),
                        pltpu.SemaphoreType.DMA((2 * L + 4,)), pltpu.VMEM(wi16.shape, BF16), pltpu.VMEM(wo16.shape, BF16),
                        pltpu.VMEM(cw.shape, F32)],
        compiler_params=_vmem_params(has_side_effects=True, collective_id=COLLECTIVE_ID["gather_weights"]),
    )(wi16, wo16, cw)
    return outs[0:L], outs[L:2 * L], outs[2 * L]


def _sum_small(r_sms):
    L = len(r_sms)

    def body(*refs):
        o_ref = refs[L]
        for l in range(L):
            acc = refs[l][0]
            for k in range(1, N_CHIPS):
                acc = acc + refs[l][k]
            o_ref[l] = acc

    return pl.pallas_call(
        body, name="sum_small",
        out_shape=jax.ShapeDtypeStruct((L,) + r_sms[0].shape[1:], F32),
        compiler_params=_vmem_params(),
    )(*r_sms)


def _sum_share(kc_arr, p_i, q_i, p_o, q_o, *, nb):
    L = p_i.shape[0]
    n_steps, slots = L * nb, 2

    def body(kc_ref, pi_ref, a0, a1, a2, po_ref, b0, b1, b2, oi_ref, oo_ref, vi, vo, loc_sems, send_sems, recv_sems):
        del kc_ref
        x, y, c = _place()
        t = pl.program_id(0) * nb + pl.program_id(1)

        def copies(s):
            l, i = s // nb, s % nb
            out = []
            for j, (v, o) in enumerate(((vi, oi_ref), (vo, oo_ref))):
                tr = v.shape[1]
                src, dst = v.at[s % slots], o.at[l, pl.ds((c * nb + i) * tr, tr), :]
                out.append((pltpu.make_async_copy(src, dst, loc_sems.at[2 * s + j]),
                            pltpu.make_async_remote_copy(src_ref=src, dst_ref=dst, send_sem=send_sems.at[2 * s + j],
                                                         recv_sem=recv_sems.at[2 * s + j], device_id=(x, y, 1 - c),
                                                         device_id_type=MESH)))
            return out

        def sent(s):
            for mine, theirs in copies(s):
                mine.wait()
                theirs.wait_send()

        @pl.when(t == 0)
        def _():
            _handshake(PEERS_SIBLING)

        @pl.when(t >= slots)
        def _():
            sent(t - slots)

        f = lambda ref: ref[...].astype(F32)
        vi[t % slots] = ((f(pi_ref) + f(a0)) + f(a1)) + f(a2)
        vo[t % slots] = ((f(po_ref) + f(b0)) + f(b1)) + f(b2)
        for mine, theirs in copies(t):
            mine.start()
            theirs.start()

        @pl.when(t == n_steps - 1)
        def _():
            for s in range(n_steps - slots, n_steps):
                sent(s)
            for s in range(n_steps):
                for _, theirs in copies(s):
                    theirs.wait_recv()

    def specs(p):
        tr, cols = p.shape[2] // nb, p.shape[3]
        chunk = pl.BlockSpec((None, None, tr, cols), lambda l, i, kc: (l, kc[0], i, 0))
        got = [pl.BlockSpec((None, None, tr, cols), lambda l, i, kc, _j=j: (_j, l, i, 0)) for j in range(3)]
        return [chunk] + got, pltpu.VMEM((slots, tr, cols), F32)

    (in_i, v_i), (in_o, v_o) = specs(p_i), specs(p_o)
    grid_spec = pltpu.PrefetchScalarGridSpec(
        num_scalar_prefetch=1, grid=(L, nb), in_specs=in_i + in_o, out_specs=[ANY, ANY],
        scratch_shapes=[v_i, v_o] + [pltpu.SemaphoreType.DMA((2 * n_steps,))] * 3)
    return pl.pallas_call(
        body, name="sum_share", grid_spec=grid_spec,
        out_shape=[jax.ShapeDtypeStruct((L, 2 * p.shape[2], p.shape[3]), F32) for p in (p_i, p_o)],
        compiler_params=_vmem_params(dimension_semantics=("arbitrary",) * 2, has_side_effects=True,
                                     collective_id=COLLECTIVE_ID["sum_share"]),
    )(kc_arr, p_i, q_i, q_i, q_i, p_o, q_o, q_o, q_o)


WEIGHTS = ("ln_g", "ln_b", "w_in", "b_in", "conv_a_w", "conv_a_b", "norm_a_g", "norm_a_b", "conv_b_w", "pool_w",
           "pool_scale", "sgu_ln_g", "sgu_ln_b", "sgu_w", "sgu_bias", "w_out", "b_out")


def _pad_rows(a, rows):
    return jnp.pad(a, ((0, rows - a.shape[0]), (0, 0)))


def _indicator_consts():
    seg = jnp.where((jnp.arange(GROUP)[:, None] // HEAD) == (jnp.arange(GROUP)[None, :] // HEAD),
                    1.0 / HEAD, 0.0).astype(BF16)
    e4 = ((jnp.arange(GROUP)[:, None] // HEAD) == jnp.arange(128)[None, :]).astype(BF16)
    return seg, e4


def _layer_consts(p, conv_full):
    L = conv_full.shape[0]
    same_head = jnp.eye(4, dtype=F32)[:, None, :, None] > 0

    def rows_to(a, rows):
        return jnp.pad(a, ((0, 0), (0, rows - a.shape[1]), (0, 0)))

    s256 = jnp.stack([p[n] for n in ("conv_a_b", "norm_a_g", "norm_a_b", "pool_scale", "sgu_ln_g", "sgu_ln_b")], axis=1)
    pw = jnp.where(same_head, p["pool_w"][:, :, :, None, :], 0.0).reshape(L, GROUP, GROUP)
    return dict(
        caw=rows_to(conv_full[:, :KA], 32), cbw=rows_to(conv_full[:, KA:], 8), s256=rows_to(s256, 8),
        pw=pw.astype(BF16),
        wm=jnp.transpose(p["sgu_w"], (0, 2, 1, 3)).reshape(L, SGU_BLOCK, 4 * SGU_BLOCK),
        wmt=jnp.transpose(p["sgu_w"], (0, 1, 3, 2)).reshape(L, 4 * SGU_BLOCK, SGU_BLOCK),
        sb=jnp.repeat(jnp.transpose(p["sgu_bias"], (0, 2, 1)), HEAD, axis=2),
        v1024=rows_to(jnp.stack([p["b_out"], p["ln_g"], p["ln_b"]], axis=1), 8),
        bin=p["b_in"][:, None, :])


def _unpack_small(sm):
    L = sm.shape[0]
    owc = jnp.concatenate([sm[:, ROW_WC:ROW_WC + SGU_BLOCK], sm[:, ROW_WC + SGU_BLOCK:ROW_WC + 2 * SGU_BLOCK]], axis=2)
    return dict(
        conv_a_b=sm[:, 0], norm_a_g=sm[:, 1], norm_a_b=sm[:, 2], pool_scale=sm[:, 3], sgu_ln_g=sm[:, 4],
        sgu_ln_b=sm[:, 5], conv_b_w=sm[:, ROW_CBW:ROW_CBW + KB], conv_a_w=sm[:, ROW_CAW:ROW_CAW + KA],
        pool_w=jnp.transpose(sm[:, ROW_PW:ROW_PW + HEAD].reshape(L, HEAD, 4, HEAD), (0, 2, 1, 3)),
        ln_g=sm[:, ROW_LNG:ROW_LNG + 4].reshape(L, D_MODEL), ln_b=sm[:, ROW_LNB:ROW_LNB + 4].reshape(L, D_MODEL),
        b_out=sm[:, ROW_BOUT:ROW_BOUT + 4].reshape(L, D_MODEL),
        b_in=sm[:, ROW_BIN:ROW_BIN + N_SLICES].reshape(L, IN_WIDTH),
        sgu_w=jnp.transpose(owc.reshape(L, SGU_BLOCK, 4, SGU_BLOCK), (0, 2, 1, 3)),
        sgu_bias=sm[:, ROW_SB:ROW_SB + 4, 0:SGU_BLOCK])


def _step(p, m, v, x, target, *, tile_f, tile_b, k_steps):
    L = p["ln_g"].shape[0]
    xi, yi, ci = _place()
    me_k = 2 * xi + yi
    hi_rows, ho_rows = D_MODEL // 2, GROUP // 2

    cw = jnp.concatenate([p["conv_a_w"], p["conv_b_w"]], axis=1).reshape(-1, 128)
    cw_rows = cw.shape[0]
    cw = _pad_rows(cw, -(-cw_rows // SUBLANES) * SUBLANES)
    wi16 = p["w_in"].astype(BF16)
    wo16 = p["w_out"].astype(BF16)
    wig0, wog0, cwg = _gather_weights(wi16[0:1], wo16[0:1], cw)
    cwg = cwg[:, :cw_rows].reshape(N_CHIPS, L, KA + KB, HEAD)
    conv_full = jnp.transpose(cwg, (1, 2, 0, 3)).reshape(L, KA + KB, GROUP)
    seg, e4 = _indicator_consts()
    k = _layer_consts(p, conv_full)
    layer = [jnp.full((1,), l, jnp.int32) for l in range(L)]

    hcur = x
    saved, wig, wog = [], [wig0[0]], [wog0[0]]
    for l in range(L):
        nxt = (wi16, wo16) if l + 1 < L else None
        outs = _fwd_layer(layer[l], hcur, wig[l], k["bin"], k["caw"], k["cbw"], k["s256"], seg, k["pw"], k["wm"], k["sb"],
                          wog[l], k["v1024"], tile=tile_f, nxt=nxt, target=None if nxt is not None else target)
        y, xb, h, aux, mixb, z = outs[0:6]
        if nxt is not None:
            wig.append(outs[6])
            wog.append(outs[7])
        saved.append((xb, h, aux, mixb, z))
        hcur = y

    dy = hcur
    loss_local = outs[6][0, 0]

    p_i = lax.empty((L, N_CHIPS, hi_rows, COLS), BF16)
    p_o = lax.empty((L, N_CHIPS, ho_rows, D_MODEL), BF16)
    q_i = lax.empty((3, L, hi_rows, COLS), BF16)
    q_o = lax.empty((3, L, ho_rows, D_MODEL), BF16)
    r_sm = [None] * L
    pending = None
    for l in reversed(range(L)):
        xb, h, aux, mixb, z = saved[l]
        exch = None if pending is None else (p_i, p_o, pending, q_i, q_o)
        outs = _bwd_layer(layer[l], dy, z, h, aux, wig[l], k["caw"], k["cbw"], k["s256"], seg, k["pw"], k["wm"],
                          k["wmt"], k["sb"], wog[l], k["v1024"], e4, tile=tile_b, exch=exch)
        dy, dhb, dzb, osm = outs[0:4]
        if l == L - 1:
            osm = osm.at[ROW_LOSS, 0].set(loss_local)
        if exch is not None:
            q_i, q_o, r_sm[l + 1] = outs[4:7]
        cl_arr = jnp.stack([ci, jnp.int32(l), me_k]).astype(jnp.int32)
        if l > 0:
            p_i, p_o = _dw_swap(cl_arr, xb, dhb, mixb, dzb, p_i, p_o, k_steps=k_steps)
        else:
            p_i, p_o, q_i, q_o, r_sm[0] = _dw_swap(cl_arr, xb, dhb, mixb, dzb, p_i, p_o, k_steps=k_steps,
                                                   last=(osm, q_i, q_o))
        pending = osm
    grad_x = dy

    summed = _sum_small(r_sm)
    loss = summed[L - 1, ROW_LOSS, 0]
    grads = _unpack_small(summed)
    for n in ("conv_a_w", "conv_b_w"):
        grads[n] = lax.dynamic_slice_in_dim(grads[n], me_k * HEAD, HEAD, axis=2)

    kc_arr = jnp.stack([me_k, ci]).astype(jnp.int32)
    g_i, g_o = _sum_share(kc_arr, p_i, q_i, p_o, q_o, nb=2)
    grads["w_in"] = g_i
    grads["w_out"] = g_o

    delta, new_m, new_v = {}, {}, {}
    for n, tr in (("w_in", 512), ("w_out", 256)):
        shp = p[n].shape
        args = [a.reshape(shp[0] * shp[1], shp[2]) for a in (p[n], grads[n], m[n], v[n])]
        outs = _adamw(*args, rows_per_step=tr, name="adamw_" + n, copy_g=True)
        delta[n], new_m[n], new_v[n], grads[n] = (a.reshape(shp) for a in outs)
    small = [n for n in WEIGHTS if n not in ("w_in", "w_out")]
    flat = [[a[n].reshape(-1, a[n].shape[-1]) for n in small] for a in (p, grads, m, v)]
    outs = _adamw_small(*flat)
    for j, n in enumerate(small):
        delta[n], new_m[n], new_v[n] = (o[j].reshape(p[n].shape) for o in outs)

    return (loss, grad_x[None], *[grads[n] for n in WEIGHTS], *[delta[n] for n in WEIGHTS],
            *[new_m[n] for n in WEIGHTS], *[new_v[n] for n in WEIGHTS])


def kernel(x, ln_g, ln_b, w_in, b_in, conv_a_w, conv_a_b, norm_a_g, norm_a_b, conv_b_w, pool_w, pool_scale, sgu_ln_g, sgu_ln_b, sgu_w, sgu_bias, w_out, b_out, loss_target, m_ln_g, m_ln_b, m_w_in, m_b_in, m_conv_a_w, m_conv_a_b, m_norm_a_g, m_norm_a_b, m_conv_b_w, m_pool_w, m_pool_scale, m_sgu_ln_g, m_sgu_ln_b, m_sgu_w, m_sgu_bias, m_w_out, m_b_out, v_ln_g, v_ln_b, v_w_in, v_b_in, v_conv_a_w, v_conv_a_b, v_norm_a_g, v_norm_a_b, v_conv_b_w, v_pool_w, v_pool_scale, v_sgu_ln_g, v_sgu_ln_b, v_sgu_w, v_sgu_bias, v_w_out, v_b_out):
    p = dict(ln_g=ln_g, ln_b=ln_b, w_in=w_in, b_in=b_in, conv_a_w=conv_a_w, conv_a_b=conv_a_b, norm_a_g=norm_a_g,
             norm_a_b=norm_a_b, conv_b_w=conv_b_w, pool_w=pool_w, pool_scale=pool_scale, sgu_ln_g=sgu_ln_g,
             sgu_ln_b=sgu_ln_b, sgu_w=sgu_w, sgu_bias=sgu_bias, w_out=w_out, b_out=b_out)
    m = dict(ln_g=m_ln_g, ln_b=m_ln_b, w_in=m_w_in, b_in=m_b_in, conv_a_w=m_conv_a_w, conv_a_b=m_conv_a_b,
             norm_a_g=m_norm_a_g, norm_a_b=m_norm_a_b, conv_b_w=m_conv_b_w, pool_w=m_pool_w, pool_scale=m_pool_scale,
             sgu_ln_g=m_sgu_ln_g, sgu_ln_b=m_sgu_ln_b, sgu_w=m_sgu_w, sgu_bias=m_sgu_bias, w_out=m_w_out, b_out=m_b_out)
    v = dict(ln_g=v_ln_g, ln_b=v_ln_b, w_in=v_w_in, b_in=v_b_in, conv_a_w=v_conv_a_w, conv_a_b=v_conv_a_b,
             norm_a_g=v_norm_a_g, norm_a_b=v_norm_a_b, conv_b_w=v_conv_b_w, pool_w=v_pool_w, pool_scale=v_pool_scale,
             sgu_ln_g=v_sgu_ln_g, sgu_ln_b=v_sgu_ln_b, sgu_w=v_sgu_w, sgu_bias=v_sgu_bias, w_out=v_w_out, b_out=v_b_out)
    return _step(p, m, v, x[0], loss_target[0], tile_f=256, tile_b=256, k_steps=4)
```

```python
import jax
import jax.numpy as jnp
from jax import lax
from jax.experimental import pallas as pl
from jax.experimental.pallas import tpu as pltpu

F32 = jnp.float32
BF16 = jnp.bfloat16
MESH = pl.DeviceIdType.MESH

D_MODEL = 1024
GROUP = 256
HEAD = 64
N_SLICES = 12
IN_WIDTH = N_SLICES * GROUP
N_CHIPS = 4
COLS = IN_WIDTH // N_CHIPS
KA = 31
KB = 3
SUBLANES = 8
HALO_A, HALO_B, HALO_C = 32, 8, 16
N_GATHER_SEMS = 12
N_EXCH_SEMS = 10
SGU_BLOCK = 128
CHUNK = 64
LN_EPS = 1e-5
ROWS = 64
V7X_VMEM_BYTES = 64 * 1024 * 1024
VMEM_LIMIT = V7X_VMEM_BYTES - 8 * 1024 * 1024

ADAM_LR, ADAM_B1, ADAM_B2, ADAM_EPS, ADAM_WD, ADAM_STEP = 0.001, 0.9, 0.999, 1e-08, 0.01, 10


ANY = pl.BlockSpec(memory_space=pl.ANY)


def _vmem_params(**kw):
    return pltpu.CompilerParams(vmem_limit_bytes=VMEM_LIMIT, **kw)


def _whole(a):
    return pl.BlockSpec(a.shape, lambda i, l, _n=a.ndim: (0,) * _n)


def _of_layer(a):
    return pl.BlockSpec((None,) + a.shape[1:], lambda i, l, _n=a.ndim: (l[0],) + (0,) * (_n - 1))


def _place():
    return lax.axis_index("x"), lax.axis_index("y"), lax.axis_index("c")


def _other_chips(x, y):
    return [(1 - x, y, 2 * (1 - x) + y), (x, 1 - y, 2 * x + (1 - y)), (1 - x, 1 - y, 2 * (1 - x) + (1 - y))]


PEERS_SIBLING, PEERS_COLUMN = "sibling", "sibling and the same core of the other chips"
COLLECTIVE_ID = dict(sum_share=0, dw_swap=1, gather_weights=2, fwd_layer_gather=3, bwd_layer_exchange=4,
                     dw_swap_exchange=5)


def _handshake(peers):
    x, y, c = _place()
    ids = [(x, y, 1 - c)]
    if peers == PEERS_COLUMN:
        ids += [(px, py, c) for px, py, _ in _other_chips(x, y)]
    barrier = pltpu.get_barrier_semaphore()
    for to in ids:
        pl.semaphore_signal(barrier, inc=1, device_id=to, device_id_type=MESH)
    pl.semaphore_wait(barrier, len(ids))


def _sig(v):
    return 0.5 * jnp.tanh(0.5 * v) + 0.5


def _dot(a, b):
    return jnp.dot(a, b, preferred_element_type=F32)


def _dot_nt(a, b):
    return lax.dot_general(a, b, (((1,), (1,)), ((), ())), preferred_element_type=F32)


def _dot_tn(a, b):
    return lax.dot_general(a, b, (((0,), (0,)), ((), ())), preferred_element_type=F32)


def _segdot(v, m):
    hi = v.astype(BF16)
    lo = (v - hi.astype(F32)).astype(BF16)
    return _dot(hi, m) + _dot(lo, m)


def _colsum(v):
    return jnp.sum(v, axis=0, keepdims=True)


def _rowmean(v):
    return jnp.mean(v, axis=-1, keepdims=True)


def _lane_group(n):
    return lax.broadcasted_iota(jnp.int32, (1, n), 1) // HEAD


def _pool_cnt(tile, t_rows):
    pos = tile * t_rows + lax.broadcasted_iota(jnp.int32, (t_rows, GROUP), 0) + 1
    grp = lax.broadcasted_iota(jnp.int32, (t_rows, GROUP), 1) // HEAD
    win = jnp.where(grp == 0, 2, jnp.where(grp == 1, 4, jnp.where(grp == 2, 8, 16)))
    return jnp.minimum(pos, win).astype(F32)


def _sgu_masks(wm_ref, wmt_ref, wm_s, wmt_s):
    r = lax.broadcasted_iota(jnp.int32, (SGU_BLOCK, 4 * SGU_BLOCK), 0) // CHUNK
    c = (lax.broadcasted_iota(jnp.int32, (SGU_BLOCK, 4 * SGU_BLOCK), 1) % SGU_BLOCK) // CHUNK
    wm_s[...] = jnp.where(c <= r, wm_ref[...], 0.0).astype(BF16)
    if wmt_ref is not None:
        rt = (lax.broadcasted_iota(jnp.int32, (4 * SGU_BLOCK, SGU_BLOCK), 0) % SGU_BLOCK) // CHUNK
        ct = lax.broadcasted_iota(jnp.int32, (4 * SGU_BLOCK, SGU_BLOCK), 1) // CHUNK
        wmt_s[...] = jnp.where(rt <= ct, wmt_ref[...], 0.0).astype(BF16)


def _vstack(v_blk):
    grp = _lane_group(GROUP)
    return jnp.concatenate([jnp.where(grp == h, v_blk, 0.0) for h in range(4)], axis=0).astype(BF16)


def _gather_next(step, nt, nwi, nwo, gwi, gwo, send_sems, recv_sems, loc_sems, vwi, vwo):
    x, y, c = _place()
    me_k = 2 * x + y
    sibling = (x, y, 1 - c)
    chips = _other_chips(x, y)
    hi, ho = D_MODEL // 2, GROUP // 2
    fwd_sems = N_GATHER_SEMS // 2

    def rc(src, dst, sem, to):
        return pltpu.make_async_remote_copy(src_ref=src, dst_ref=dst, send_sem=send_sems.at[sem],
                                            recv_sem=recv_sems.at[sem], device_id=to, device_id_type=MESH)

    def blk(ref, k, n, cc):
        return ref.at[k, pl.ds(cc * n, n), :]

    def ici(r):
        px, py, _ = chips[r]
        to = (px, py, c)
        return [rc(nwi.at[pl.ds(c * hi, hi), :], blk(gwi, me_k, hi, c), 2 * r, to),
                rc(nwo.at[pl.ds(c * ho, ho), :], blk(gwo, me_k, ho, c), 2 * r + 1, to)]

    def landed(r, cc, base):
        pk = chips[r][2]
        return [rc(blk(gwi, pk, hi, cc), blk(gwi, pk, hi, cc), base + 2 * r, sibling),
                rc(blk(gwo, pk, ho, cc), blk(gwo, pk, ho, cc), base + 2 * r + 1, sibling)]

    def stage_in():
        return [pltpu.make_async_copy(nwi, vwi, loc_sems.at[0]), pltpu.make_async_copy(nwo, vwo, loc_sems.at[1])]

    def local():
        return [pltpu.make_async_copy(vwi, gwi.at[me_k], loc_sems.at[2]),
                pltpu.make_async_copy(vwo, gwo.at[me_k], loc_sems.at[3])]

    @pl.when(step == 0)
    def _():
        _handshake(PEERS_COLUMN)
        for cp in stage_in():
            cp.start()
        for r in range(3):
            for cp in ici(r):
                cp.start()

    @pl.when(step == 1)
    def _():
        for cp in stage_in():
            cp.wait()
        for cp in local():
            cp.start()

    @pl.when(step == (3 * nt) // 4)
    def _():
        for r in range(3):
            for got, fwd in zip(landed(r, c, 0), landed(r, c, fwd_sems)):
                got.wait_recv()
                fwd.start()

    @pl.when(step == nt - 1)
    def _():
        for r in range(3):
            for got in landed(r, 1 - c, fwd_sems):
                got.wait_recv()
        for r in range(3):
            for cp in ici(r) + landed(r, c, fwd_sems):
                cp.wait_send()
        for cp in local():
            cp.wait()


def _fwd_layer(larr, x, wi, bin_, caw, cbw, s256, seg, pw, wm, sb, wo, v1024, *, tile, nxt=None, target=None):
    assert nxt is None or target is None
    S = x.shape[0]
    T = tile
    nt = S // T
    alpha = float((2.0 * 4) ** 0.25)
    n_in = 13 + (2 if nxt is not None else 0) + (1 if target is not None else 0)
    n_out = 6 + (2 if nxt is not None else 0) + (1 if target is not None else 0)

    def body(*refs):
        l_ref = refs[0]
        (x_ref, wi_ref, bin_ref, caw_ref, cbw_ref, s256_ref, seg_ref, pw_ref, wm_ref, sb_ref, wo_ref,
         v1024_ref) = refs[1:13]
        y_ref, xb_ref, h_ref, aux_ref, mix_ref, z_ref = refs[n_in:n_in + 6]
        abuf, bbuf, cbuf, wm_s, shf = refs[n_in + n_out:n_in + n_out + 5]
        i = pl.program_id(0)
        if nxt is not None:
            _gather_next(i, nt, refs[13].at[l_ref[0] + 1], refs[14].at[l_ref[0] + 1], refs[n_in + 6], refs[n_in + 7],
                         *refs[n_in + n_out + 5:])

        @pl.when(i == 0)
        def _():
            abuf[0:HALO_A, :] = jnp.zeros((HALO_A, GROUP), F32)
            bbuf[0:HALO_B, :] = jnp.zeros((HALO_B, GROUP), F32)
            cbuf[0:HALO_C, :] = jnp.zeros((HALO_C, GROUP), F32)
            _sgu_masks(wm_ref, None, wm_s, None)

        x = x_ref[...]
        xb = x.astype(BF16)
        xb_ref[...] = xb
        for k in range(N_CHIPS):
            h_ref[:, COLS * k:COLS * (k + 1)] = _dot(xb, wi_ref[k]) + bin_ref[:, COLS * k:COLS * (k + 1)]

        def hs(j):
            return h_ref[:, GROUP * j:GROUP * (j + 1)]

        abuf[HALO_A:HALO_A + T, :] = hs(0) * _sig(hs(1))
        span = T + HALO_A - SUBLANES
        for p in range(1, SUBLANES):
            shf[p - 1, :, :] = abuf[p:p + span, :]
        for r0 in range(0, T, ROWS):
            acc = None
            for k in range(KA):
                off = HALO_A - (KA - 1) + k
                p, q8 = off % SUBLANES, off - off % SUBLANES
                win = abuf[r0 + q8:r0 + q8 + ROWS, :] if p == 0 else shf[p - 1, r0 + q8:r0 + q8 + ROWS, :]
                term = caw_ref[k:k + 1, :] * win
                acc = term if acc is None else acc + term
            aux_ref[r0:r0 + ROWS, 0:GROUP] = acc + s256_ref[0:1, :]
        abuf[0:HALO_A, :] = abuf[T:T + HALO_A, :]
        a1 = aux_ref[:, 0:GROUP]
        segm = seg_ref[...]
        cen = a1 - _segdot(a1, segm)
        var = _segdot(cen * cen, segm)
        a2 = cen * lax.rsqrt(var + LN_EPS) * s256_ref[1:2, :] + s256_ref[2:3, :]
        az = hs(2)
        mix_ref[:, 0:GROUP] = (a2 * _sig(a2) * (az * _sig(az))).astype(BF16)

        bbuf[HALO_B:HALO_B + T, :] = hs(4) * hs(5)
        for r0 in range(0, T, ROWS):
            acc = None
            for k in range(KB):
                off = HALO_B - (KB - 1) + k + r0
                term = cbw_ref[k:k + 1, :] * bbuf[off:off + ROWS, :]
                acc = term if acc is None else acc + term
            aux_ref[r0:r0 + ROWS, GROUP:2 * GROUP] = acc
        bbuf[0:HALO_B, :] = bbuf[T:T + HALO_B, :]
        bz = hs(6)
        mix_ref[:, GROUP:2 * GROUP] = (hs(3) * aux_ref[:, GROUP:2 * GROUP] * (bz * _sig(bz))).astype(BF16)

        ch = hs(7)
        cbuf[HALO_C:HALO_C + T, :] = ch
        hi_lane = (lax.broadcasted_iota(jnp.int32, (1, 128), 1) // HEAD) == 1
        for r0 in range(0, T, ROWS):
            def win(col, j0, j1):
                s = None
                for j in range(j0, j1):
                    off = HALO_C - j + r0
                    term = cbuf[off:off + ROWS, 128 * col:128 * (col + 1)]
                    s = term if s is None else s + term
                return s
            w0 = win(0, 0, 2) + jnp.where(hi_lane, win(0, 2, 4), 0.0)
            w1 = win(1, 0, 8) + jnp.where(hi_lane, win(1, 8, 16), 0.0)
            aux_ref[r0:r0 + ROWS, 2 * GROUP:2 * GROUP + 128] = w0
            aux_ref[r0:r0 + ROWS, 2 * GROUP + 128:3 * GROUP] = w1
        cbuf[0:HALO_C, :] = cbuf[T:T + HALO_C, :]
        pooled = aux_ref[:, 2 * GROUP:3 * GROUP] / _pool_cnt(i, T) - ch
        aux_ref[:, 2 * GROUP:3 * GROUP] = pooled
        q = _dot(pooled.astype(BF16), pw_ref[...])
        cz = hs(8)
        mix_ref[:, 2 * GROUP:3 * GROUP] = (q * s256_ref[3:4, :] * (cz * _sig(cz))).astype(BF16)

        dv = hs(10)
        cen = dv - _rowmean(dv)
        var = _rowmean(cen * cen)
        v = cen * lax.rsqrt(var + LN_EPS) * s256_ref[4:5, :] + s256_ref[5:6, :]
        sps = []
        for n in range(T // SGU_BLOCK):
            vb = v[n * SGU_BLOCK:(n + 1) * SGU_BLOCK, :]
            sps.append(_dot(wm_s[...], _vstack(vb)) + sb_ref[...])
        sp = jnp.concatenate(sps, axis=0)
        dz = hs(11)
        mix_ref[:, 3 * GROUP:4 * GROUP] = (hs(9) * sp * (dz * _sig(dz))).astype(BF16)

        out = v1024_ref[0:1, :]
        for k in range(N_CHIPS):
            out = out + _dot(mix_ref[:, GROUP * k:GROUP * (k + 1)], wo_ref[k])
        z = alpha * x + out
        z_ref[...] = z
        cen = z - _rowmean(z)
        var = _rowmean(cen * cen)
        y = cen * lax.rsqrt(var + LN_EPS) * v1024_ref[1:2, :] + v1024_ref[2:3, :]
        if target is None:
            y_ref[...] = y
        else:
            t_ref, loss_ref = refs[13], refs[n_in + 6]

            @pl.when(i == 0)
            def _():
                loss_ref[...] = jnp.zeros_like(loss_ref)
            err = y - t_ref[...]
            y_ref[...] = err * (1.0 / D_MODEL)
            loss_ref[...] += jnp.sum(_colsum(err * err), axis=1, keepdims=True) * (0.5 / D_MODEL)

    def rows(width):
        return pl.BlockSpec((T, width), lambda i, l: (i, 0))

    consts = (wi, bin_, caw, cbw, s256, seg, pw, wm, sb, wo, v1024)
    in_specs = [rows(D_MODEL)] + [_whole(a) if a is wi or a is seg or a is wo else _of_layer(a) for a in consts]
    out_specs = [rows(D_MODEL), rows(D_MODEL), rows(IN_WIDTH), rows(3 * GROUP), rows(D_MODEL), rows(D_MODEL)]
    out_shape = [jax.ShapeDtypeStruct((S, D_MODEL), F32), jax.ShapeDtypeStruct((S, D_MODEL), BF16),
                 jax.ShapeDtypeStruct((S, IN_WIDTH), F32), jax.ShapeDtypeStruct((S, 3 * GROUP), F32),
                 jax.ShapeDtypeStruct((S, D_MODEL), BF16), jax.ShapeDtypeStruct((S, D_MODEL), F32)]
    scratch = [pltpu.VMEM((T + HALO_A, GROUP), F32), pltpu.VMEM((T + HALO_B, GROUP), F32),
               pltpu.VMEM((T + HALO_C, GROUP), F32), pltpu.VMEM((SGU_BLOCK, 4 * SGU_BLOCK), BF16),
               pltpu.VMEM((SUBLANES - 1, T + HALO_A - SUBLANES, GROUP), F32)]
    extra = ()
    if nxt is not None:
        extra = tuple(nxt)
        in_specs += [ANY, ANY]
        out_specs += [ANY, ANY]
        out_shape += [jax.ShapeDtypeStruct((N_CHIPS, D_MODEL, COLS), BF16),
                      jax.ShapeDtypeStruct((N_CHIPS, GROUP, D_MODEL), BF16)]
        scratch += [pltpu.SemaphoreType.DMA((N_GATHER_SEMS,)), pltpu.SemaphoreType.DMA((N_GATHER_SEMS,)),
                    pltpu.SemaphoreType.DMA((4,)), pltpu.VMEM((D_MODEL, COLS), BF16), pltpu.VMEM((GROUP, D_MODEL), BF16)]
    if target is not None:
        extra = (target,)
        in_specs += [rows(D_MODEL)]
        out_specs += [pl.BlockSpec((8, 128), lambda i, l: (0, 0))]
        out_shape += [jax.ShapeDtypeStruct((8, 128), F32)]
    grid_spec = pltpu.PrefetchScalarGridSpec(num_scalar_prefetch=1, grid=(nt,), in_specs=in_specs,
                                             out_specs=out_specs, scratch_shapes=scratch)
    return pl.pallas_call(
        body, name=("fwd_layer_loss" if target is not None else "fwd_layer") if nxt is None else "fwd_layer_gather",
        grid_spec=grid_spec, out_shape=out_shape,
        compiler_params=_vmem_params(dimension_semantics=("arbitrary",), **(
            dict(has_side_effects=True, collective_id=COLLECTIVE_ID["fwd_layer_gather"]) if nxt is not None else {})),
    )(larr, x, *consts, *extra)


ROW_CBW = 8
ROW_CAW = 16
ROW_LOSS = 7
ROW_PW = 48
ROW_LNG = 112
ROW_LNB = 116
ROW_BOUT = 120
ROW_BIN = 124
ROW_WC = 136
ROW_SB = 392
SM_ROWS = 400


def _exchange_comm(start, mid, finish, l, p_i, p_o, sm, r_i, r_o, r_sm, send_sems, recv_sems, loc_sems, vm):
    x, y, c = _place()
    me_k = 2 * x + y
    chips = _other_chips(x, y)

    def rc(src, dst, sem, to):
        return pltpu.make_async_remote_copy(src_ref=src, dst_ref=dst, send_sem=send_sems.at[sem],
                                            recv_sem=recv_sems.at[sem], device_id=to, device_id_type=MESH)

    def big(r):
        px, py, pk = chips[r]
        to = (px, py, c)
        return [rc(p_i.at[l, pk], r_i.at[r, l], 2 * r, to), rc(p_o.at[l, pk], r_o.at[r, l], 2 * r + 1, to)]

    def stage():
        return pltpu.make_async_copy(sm, vm.at[0], loc_sems.at[0])

    def to_sibling():
        return rc(sm, vm.at[1], N_EXCH_SEMS - 4, (x, y, 1 - c))

    half = pl.ds(pl.multiple_of(c * (SM_ROWS // 2), SUBLANES), SM_ROWS // 2)

    def chip_sum(r):
        px, py, pk = chips[r]
        return rc(vm.at[2, half], r_sm.at[me_k, half], N_EXCH_SEMS - 3 + r, (px, py, c))

    def keep():
        return pltpu.make_async_copy(vm.at[2, half], r_sm.at[me_k, half], loc_sems.at[1])

    with_big, with_small = p_i is not None, sm is not None

    @pl.when(start)
    def _():
        _handshake(PEERS_COLUMN)
        if with_small:
            stage().start()
            to_sibling().start()
        if with_big:
            for r in range(3):
                for cp in big(r):
                    cp.start()

    if with_small:
        @pl.when(mid)
        def _():
            stage().wait()
            to_sibling().wait_recv()
            vm[2] = vm[0] + vm[1]
            keep().start()
            for r in range(3):
                chip_sum(r).start()

    @pl.when(finish)
    def _():
        if with_big:
            for r in range(3):
                for cp in big(r):
                    cp.wait()
        if with_small:
            to_sibling().wait_send()
            for r in range(3):
                chip_sum(r).wait()
            keep().wait()


RC = 32
RC_WIDE = 16
ACC_ROWS = 136


def _rsum8(v):
    r = v[0:8]
    for j in range(1, v.shape[0] // 8):
        r = r + v[8 * j:8 * j + 8]
    return r


def _bwd_layer(larr, dy, z, h, aux, wi, caw, cbw, s256, seg, pw, wm, wmt, sb, wo, v1024, e4, *, tile, exch=None):
    S = dy.shape[0]
    T = tile
    nt = S // T
    nblk = T // SGU_BLOCK
    alpha = float((2.0 * 4) ** 0.25)
    n_in = 17 + (5 if exch is not None else 0)
    n_out = 4 + (3 if exch is not None else 0)
    slab = pltpu.VMEM((T, GROUP), F32)
    scratch = dict(
        dbuf=pltpu.VMEM((T + HALO_A, GROUP), F32), ebuf=pltpu.VMEM((T + HALO_B, GROUP), F32),
        fbuf=pltpu.VMEM((T + HALO_C, GROUP), F32), sh=pltpu.VMEM((SUBLANES - 1, T + HALO_A - SUBLANES, GROUP), F32),
        wm_s=pltpu.VMEM((SGU_BLOCK, 4 * SGU_BLOCK), BF16), wmt_s=pltpu.VMEM((4 * SGU_BLOCK, SGU_BLOCK), BF16),
        dsp_acc=pltpu.VMEM((SGU_BLOCK, GROUP), F32), pw_acc=pltpu.VMEM((GROUP, GROUP), F32),
        acc_s=pltpu.VMEM((8 * ACC_ROWS, GROUP), F32), acc_w=pltpu.VMEM((24, D_MODEL), F32),
        dmix_s=pltpu.VMEM((T, D_MODEL), F32), vst_s=pltpu.VMEM((nblk, 4 * SGU_BLOCK, GROUP), BF16),
        dq_s=pltpu.VMEM((T, GROUP), BF16), dxt_s=pltpu.VMEM((D_MODEL, T), F32),
        mean_s=slab, t1_s=slab, t2_s=slab, q_s=slab, xv_s=slab, rv_s=slab, v_s=slab, sp_s=slab, a0_s=slab, sg_s=slab,
        xh_s=slab, ra_s=slab, ub_s=slab, dsp_s=slab, m1_s=slab, m2_s=slab, dpool_s=slab, dvd_s=slab, u_s=slab,
        du_s=slab, cw_s=slab)
    names = list(scratch)

    def body(*refs):
        (dy_ref, z_ref, h_ref, aux_ref, wi_ref, caw_ref, cbw_ref, s256_ref, seg_ref, pw_ref, wm_ref, wmt_ref,
         sb_ref, wo_ref, v1024_ref, e4_ref) = refs[1:17]
        dx_ref, dhb_ref, dzb_ref, osm_ref = refs[n_in:n_in + 4]
        k0 = n_in + n_out
        sc = dict(zip(names, refs[k0:k0 + len(names)]))
        dbuf, ebuf, fbuf, sh = sc["dbuf"], sc["ebuf"], sc["fbuf"], sc["sh"]
        wm_s, wmt_s, dsp_acc, pw_acc, acc_s, acc_w = (sc[n] for n in ("wm_s", "wmt_s", "dsp_acc", "pw_acc", "acc_s",
                                                                        "acc_w"))
        dmix_s, vst_s, dq_s = sc["dmix_s"], sc["vst_s"], sc["dq_s"]
        i = pl.program_id(0)
        tile_idx = nt - 1 - i
        if exch is not None:
            p_i, p_o, sm = refs[17:20]
            r_i, r_o, r_sm = refs[n_in + 4:n_in + 7]
            _exchange_comm(i == 0, i == 1, i == nt - 1, refs[0][0] + 1, p_i, p_o, sm, r_i, r_o, r_sm, *refs[k0 + len(names):])

        @pl.when(i == 0)
        def _():
            dbuf[T:T + HALO_A, :] = jnp.zeros((HALO_A, GROUP), F32)
            ebuf[T:T + HALO_B, :] = jnp.zeros((HALO_B, GROUP), F32)
            fbuf[T:T + HALO_C, :] = jnp.zeros((HALO_C, GROUP), F32)
            _sgu_masks(wm_ref, wmt_ref, wm_s, wmt_s)
            osm_ref[...] = jnp.zeros_like(osm_ref)
            dsp_acc[...] = jnp.zeros_like(dsp_acc)
            pw_acc[...] = jnp.zeros_like(pw_acc)
            acc_s[...] = jnp.zeros_like(acc_s)
            acc_w[...] = jnp.zeros_like(acc_w)

        def chunks(rc, fn):
            for c in range(T // rc):
                fn(pl.ds(c * rc, rc))

        def hs(j, rows):
            return h_ref[rows, GROUP * j:GROUP * (j + 1)]

        def acc_add(row, val):
            acc_s[8 * row:8 * row + 8, :] += _rsum8(val)

        def put_dh(j, rows, val):
            acc_add(ROW_BIN + j, val)
            dhb_ref[rows, GROUP * j:GROUP * (j + 1)] = val.astype(BF16)

        def dsilu(v, s):
            return s * (1.0 + v * (1.0 - s))

        def vec(r):
            return s256_ref[r:r + 1, :]

        def ln_bwd(rows):
            dyc = dy_ref[rows, :]
            zc = z_ref[rows, :]
            cen = zc - _rowmean(zc)
            rstd = lax.rsqrt(_rowmean(cen * cen) + LN_EPS)
            xhat = cen * rstd
            acc_w[0:8, :] += _rsum8(dyc * xhat)
            acc_w[8:16, :] += _rsum8(dyc)
            gdy = dyc * v1024_ref[1:2, :]
            dz = rstd * (gdy - _rowmean(gdy) - xhat * _rowmean(gdy * xhat))
            acc_w[16:24, :] += _rsum8(dz)
            dzb_ref[rows, :] = dz.astype(BF16)
            dx_ref[rows, :] = alpha * dz
        chunks(RC_WIDE, ln_bwd)

        segm = seg_ref[...]
        dzb = dzb_ref[...]
        for k in range(N_CHIPS):
            dmix_s[:, GROUP * k:GROUP * (k + 1)] = _dot_nt(dzb, wo_ref[k])
        sc["mean_s"][...] = _segdot(aux_ref[:, 0:GROUP], segm)
        pooled_b = aux_ref[:, 2 * GROUP:3 * GROUP].astype(BF16)
        sc["q_s"][...] = _dot(pooled_b, pw_ref[...])

        def centre(rows):
            cen = aux_ref[rows, 0:GROUP] - sc["mean_s"][rows, :]
            sc["t1_s"][rows, :] = cen * cen
            dv_in = hs(10, rows)
            cen_v = dv_in - _rowmean(dv_in)
            rstd_v = lax.rsqrt(_rowmean(cen_v * cen_v) + LN_EPS)
            xv = cen_v * rstd_v
            sc["xv_s"][rows, :] = xv
            sc["rv_s"][rows, :] = jnp.broadcast_to(rstd_v, xv.shape)
            sc["v_s"][rows, :] = xv * vec(4) + vec(5)
        chunks(RC, centre)

        sc["t2_s"][...] = _segdot(sc["t1_s"][...], segm)
        for n in range(nblk):
            blk = slice(n * SGU_BLOCK, (n + 1) * SGU_BLOCK)
            vst_s[n] = _vstack(sc["v_s"][blk, :])
            sc["sp_s"][blk, :] = _dot(wm_s[...], vst_s[n]) + sb_ref[...]

        def mixers(rows):
            a_val, a_glu, a_z = hs(0, rows), hs(1, rows), hs(2, rows)
            sg = _sig(a_glu)
            sc["a0_s"][rows, :] = a_val * sg
            sc["sg_s"][rows, :] = sg
            rstd_a = lax.rsqrt(sc["t2_s"][rows, :] + LN_EPS)
            xh = (aux_ref[rows, 0:GROUP] - sc["mean_s"][rows, :]) * rstd_a
            a2 = xh * vec(1) + vec(2)
            s2 = _sig(a2)
            sz = _sig(a_z)
            dya = dmix_s[rows, 0:GROUP]
            put_dh(2, rows, dya * (a2 * s2) * dsilu(a_z, sz))
            d_a2 = dya * (a_z * sz) * dsilu(a2, s2)
            acc_add(1, d_a2 * xh)
            acc_add(2, d_a2)
            gd = d_a2 * vec(1)
            sc["t1_s"][rows, :] = gd
            sc["t2_s"][rows, :] = gd * xh
            sc["xh_s"][rows, :] = xh
            sc["ra_s"][rows, :] = rstd_a
            b_b, b_c, b_h, b_z = hs(3, rows), hs(4, rows), hs(5, rows), hs(6, rows)
            cb = aux_ref[rows, GROUP:2 * GROUP]
            sz = _sig(b_z)
            dyb = dmix_s[rows, GROUP:2 * GROUP]
            put_dh(3, rows, dyb * cb * (b_z * sz))
            put_dh(6, rows, dyb * b_b * cb * dsilu(b_z, sz))
            ebuf[rows, :] = dyb * b_b * (b_z * sz)
            sc["ub_s"][rows, :] = b_c * b_h
            c_z = hs(8, rows)
            q = sc["q_s"][rows, :]
            sz = _sig(c_z)
            dyc = dmix_s[rows, 2 * GROUP:3 * GROUP]
            acc_add(3, dyc * q * (c_z * sz))
            put_dh(8, rows, dyc * q * vec(3) * dsilu(c_z, sz))
            dq_s[rows, :] = (dyc * vec(3) * (c_z * sz)).astype(BF16)
            d_u, d_z = hs(9, rows), hs(11, rows)
            sp = sc["sp_s"][rows, :]
            sz = _sig(d_z)
            dyd = dmix_s[rows, 3 * GROUP:4 * GROUP]
            put_dh(9, rows, dyd * sp * (d_z * sz))
            put_dh(11, rows, dyd * d_u * sp * dsilu(d_z, sz))
            sc["dsp_s"][rows, :] = dyd * d_u * (d_z * sz)
        chunks(RC, mixers)

        sc["m1_s"][...] = _segdot(sc["t1_s"][...], segm)
        sc["m2_s"][...] = _segdot(sc["t2_s"][...], segm)
        d_q = dq_s[...]
        pw_acc[...] += _dot_tn(pooled_b, d_q)
        sc["dpool_s"][...] = _dot_nt(d_q, pw_ref[...])
        grp = _lane_group(GROUP)
        for n in range(nblk):
            blk = slice(n * SGU_BLOCK, (n + 1) * SGU_BLOCK)
            dspb = sc["dsp_s"][blk, :]
            dsp_acc[...] += dspb
            dspb16 = dspb.astype(BF16)
            dvst = _dot(wmt_s[...], dspb16)
            dvb = None
            for hh in range(4):
                part = jnp.where(grp == hh, dvst[hh * SGU_BLOCK:(hh + 1) * SGU_BLOCK, :], 0.0)
                dvb = part if dvb is None else dvb + part
            sc["dvd_s"][blk, :] = dvb
            dwc = _dot_nt(dspb16, vst_s[n])
            osm_ref[ROW_WC:ROW_WC + SGU_BLOCK, :] += dwc[:, 0:GROUP]
            osm_ref[ROW_WC + SGU_BLOCK:ROW_WC + 2 * SGU_BLOCK, :] += dwc[:, GROUP:2 * GROUP]

        def ln_sums(rows):
            xh = sc["xh_s"][rows, :]
            d_a1 = sc["ra_s"][rows, :] * (sc["t1_s"][rows, :] - sc["m1_s"][rows, :] - xh * sc["m2_s"][rows, :])
            acc_add(0, d_a1)
            dbuf[rows, :] = d_a1
            pos = tile_idx * T + rows.start + lax.broadcasted_iota(jnp.int32, (RC, GROUP), 0) + 1
            lane = lax.broadcasted_iota(jnp.int32, (RC, GROUP), 1) // HEAD
            win = jnp.where(lane == 0, 2, jnp.where(lane == 1, 4, jnp.where(lane == 2, 8, 16)))
            fbuf[rows, :] = sc["dpool_s"][rows, :] / jnp.minimum(pos, win).astype(F32)
            d_v = sc["dvd_s"][rows, :]
            xv = sc["xv_s"][rows, :]
            acc_add(4, d_v * xv)
            acc_add(5, d_v)
            gd = d_v * vec(4)
            put_dh(10, rows, sc["rv_s"][rows, :] * (gd - _rowmean(gd) - xv * _rowmean(gd * xv)))
        chunks(RC, ln_sums)

        span = T + HALO_A - SUBLANES
        for p in range(1, SUBLANES):
            sh[p - 1, :, :] = dbuf[p:p + span, :]

        for r0 in range(0, T, ROWS):
            uc = sc["ub_s"][r0:r0 + ROWS, :]
            acc = None
            for k in range(KB):
                off = (KB - 1) - k + r0
                w = ebuf[off:off + ROWS, :]
                term = cbw_ref[k:k + 1, :] * w
                acc = term if acc is None else acc + term
                acc_add(ROW_CBW + k, uc * w)
            sc["du_s"][r0:r0 + ROWS, :] = acc
        ebuf[T:T + HALO_B, :] = ebuf[0:HALO_B, :]

        hi_lane = (lax.broadcasted_iota(jnp.int32, (1, 128), 1) // HEAD) == 1
        for r0 in range(0, T, ROWS):
            def win(col, j0, j1):
                s = None
                for j in range(j0, j1):
                    term = fbuf[r0 + j:r0 + j + ROWS, 128 * col:128 * (col + 1)]
                    s = term if s is None else s + term
                return s
            sc["cw_s"][r0:r0 + ROWS, 0:128] = win(0, 0, 2) + jnp.where(hi_lane, win(0, 2, 4), 0.0)
            sc["cw_s"][r0:r0 + ROWS, 128:256] = win(1, 0, 8) + jnp.where(hi_lane, win(1, 8, 16), 0.0)
        fbuf[T:T + HALO_C, :] = fbuf[0:HALO_C, :]

        def rest_bc(rows):
            d_u = sc["du_s"][rows, :]
            put_dh(4, rows, d_u * hs(5, rows))
            put_dh(5, rows, d_u * hs(4, rows))
            put_dh(7, rows, sc["cw_s"][rows, :] - sc["dpool_s"][rows, :])
        chunks(RC, rest_bc)

        dxt_s = sc["dxt_s"]

        def dx_term(k):
            term = _dot_nt(wi_ref[k], dhb_ref[:, COLS * k:COLS * (k + 1)])
            if k == 1:
                dxt_s[...] = term
            else:
                dxt_s[...] += term

        def conv_a(rows):
            a0c = sc["a0_s"][rows, :]
            acc = None
            for k in range(KA):
                off = (KA - 1) - k
                p, q8 = off % SUBLANES, off - off % SUBLANES
                w = dbuf[pl.ds(rows.start + q8, RC), :] if p == 0 else sh[p - 1, pl.ds(rows.start + q8, RC), :]
                term = caw_ref[k:k + 1, :] * w
                acc = term if acc is None else acc + term
                acc_add(ROW_CAW + k, a0c * w)
            sc["u_s"][rows, :] = acc
        n_chunks = T // RC
        after = {(n_chunks * j) // 3: j + 1 for j in range(3)}
        for c in range(n_chunks):
            conv_a(pl.ds(c * RC, RC))
            if c in after:
                dx_term(after[c])
        dbuf[T:T + HALO_A, :] = dbuf[0:HALO_A, :]

        def rest_a(rows):
            d_a0 = sc["u_s"][rows, :]
            sg = sc["sg_s"][rows, :]
            put_dh(0, rows, d_a0 * sg)
            put_dh(1, rows, d_a0 * hs(0, rows) * sg * (1.0 - sg))
        chunks(RC, rest_a)
        dx_term(0)
        dx_ref[...] += dxt_s[...].T

        @pl.when(i == nt - 1)
        def _():
            for row in list(range(6)) + list(range(ROW_CBW, ROW_CBW + KB)) + list(range(ROW_CAW, ROW_CAW + KA)) + list(
                    range(ROW_BIN, ROW_BIN + N_SLICES)):
                osm_ref[row:row + 1, :] = _colsum(acc_s[8 * row:8 * row + 8, :])
            for j, row in enumerate((ROW_LNG, ROW_LNB, ROW_BOUT)):
                cs = _colsum(acc_w[8 * j:8 * j + 8, :])
                for q in range(D_MODEL // GROUP):
                    osm_ref[row + q:row + q + 1, :] = cs[:, GROUP * q:GROUP * (q + 1)]
            r = lax.broadcasted_iota(jnp.int32, (SGU_BLOCK, GROUP), 0) // CHUNK
            c = (lax.broadcasted_iota(jnp.int32, (SGU_BLOCK, GROUP), 1) % SGU_BLOCK) // CHUNK
            for half in range(2):
                rows_ = slice(ROW_WC + half * SGU_BLOCK, ROW_WC + (half + 1) * SGU_BLOCK)
                osm_ref[rows_, :] = jnp.where(c <= r, osm_ref[rows_, :], 0.0)
            sb_t = _segdot(dsp_acc[...], e4_ref[...]).T
            osm_ref[ROW_SB:ROW_SB + 8, 0:SGU_BLOCK] = sb_t[0:8, :]
            for g in range(4):
                osm_ref[ROW_PW:ROW_PW + HEAD, HEAD * g:HEAD * (g + 1)] = (
                    pw_acc[HEAD * g:HEAD * (g + 1), HEAD * g:HEAD * (g + 1)])

    def rows(width):
        return pl.BlockSpec((T, width), lambda i, l: (nt - 1 - i, 0))

    consts = (wi, caw, cbw, s256, seg, pw, wm, wmt, sb, wo, v1024, e4)
    unstacked = (wi, seg, wo, e4)
    in_specs = [rows(D_MODEL), rows(D_MODEL), rows(IN_WIDTH), rows(3 * GROUP)] + [
        _whole(a) if any(a is u for u in unstacked) else _of_layer(a) for a in consts]
    out_specs = [rows(D_MODEL), rows(IN_WIDTH), rows(D_MODEL), pl.BlockSpec((SM_ROWS, GROUP), lambda i, l: (0, 0))]
    out_shape = [jax.ShapeDtypeStruct((S, D_MODEL), F32), jax.ShapeDtypeStruct((S, IN_WIDTH), BF16),
                 jax.ShapeDtypeStruct((S, D_MODEL), BF16), jax.ShapeDtypeStruct((SM_ROWS, GROUP), F32)]
    scratch_shapes = list(scratch.values())
    extra, aliases = (), {}
    if exch is not None:
        extra = tuple(exch)
        r_i, r_o = exch[3], exch[4]
        in_specs += [ANY] * 5
        out_specs += [ANY] * 3
        out_shape += [jax.ShapeDtypeStruct(r_i.shape, r_i.dtype), jax.ShapeDtypeStruct(r_o.shape, r_o.dtype),
                      jax.ShapeDtypeStruct((N_CHIPS, SM_ROWS, GROUP), F32)]
        scratch_shapes += [pltpu.SemaphoreType.DMA((N_EXCH_SEMS,)), pltpu.SemaphoreType.DMA((N_EXCH_SEMS,)),
                           pltpu.SemaphoreType.DMA((2,)), pltpu.VMEM((3, SM_ROWS, GROUP), F32)]
        aliases = {20: 4, 21: 5}
    grid_spec = pltpu.PrefetchScalarGridSpec(num_scalar_prefetch=1, grid=(nt,), in_specs=in_specs,
                                             out_specs=out_specs, scratch_shapes=scratch_shapes)
    return pl.pallas_call(
        body, name="bwd_layer" if exch is None else "bwd_layer_exchange",
        grid_spec=grid_spec, out_shape=out_shape, input_output_aliases=aliases,
        compiler_params=_vmem_params(dimension_semantics=("arbitrary",), **(
            dict(has_side_effects=True, collective_id=COLLECTIVE_ID["bwd_layer_exchange"]) if exch is not None else {})),
    )(larr, dy, z, h, aux, *consts, *extra)


def _dw_swap(cl_arr, xb, dhb, mixb, dzb, p_i, p_o, *, k_steps, last=None):
    S = xb.shape[0]
    tk = S // k_steps
    n_steps = N_CHIPS + k_steps
    hi, ho = p_i.shape[2], p_o.shape[2]
    n_in = 7 + (3 if last is not None else 0)
    n_out = 2 + (3 if last is not None else 0)

    def body(*refs):
        cl_ref, x_ref, dh_ref, mix_ref, dz_ref = refs[0:5]
        pi_ref, po_ref = refs[n_in:n_in + 2]
        own_i, acc_o, snd_i, snd_o, rcv_i, rcv_o, send_sems, recv_sems = refs[n_in + n_out:n_in + n_out + 8]
        j = pl.program_id(0)
        l, me_k = cl_ref[1], cl_ref[2]
        x, y, c = _place()
        mine_o, theirs_o = (pl.ds(pl.multiple_of(cc * ho, ho), ho) for cc in (c, 1 - c))

        def to_sibling(src, dst, sem):
            return pltpu.make_async_remote_copy(src_ref=src, dst_ref=dst, send_sem=send_sems.at[sem],
                                                recv_sem=recv_sems.at[sem], device_id=(x, y, 1 - c), device_id_type=MESH)

        def chunk_of(s):
            return (me_k + 1 + s) % N_CHIPS

        def chunk_copy(s):
            return to_sibling(snd_i.at[s % 2], rcv_i.at[chunk_of(s)], s)

        def out_copy():
            return to_sibling(snd_o, rcv_o, N_CHIPS)

        if last is None:
            @pl.when(j == 0)
            def _():
                _handshake(PEERS_SIBLING)
        else:
            qi_ref, qo_ref, r_sm = refs[n_in + 2:n_in + 5]
            out_sems, in_sems = refs[n_in + n_out + 8:n_in + n_out + 10]
            _exchange_comm(j == 0, j == 1, j == n_steps - 1, None, None, None, refs[7], None, None, r_sm,
                           *refs[n_in + n_out + 10:])
            chips = _other_chips(x, y)

            def onward(r):
                px, py, pk = chips[r]
                return [pltpu.make_async_remote_copy(
                    src_ref=v.at[pk], dst_ref=q.at[r, l], send_sem=out_sems.at[2 * r + n], recv_sem=in_sems.at[2 * r + n],
                    device_id=(px, py, c), device_id_type=MESH) for n, (v, q) in enumerate(((rcv_i, qi_ref), (rcv_o, qo_ref)))]

        @pl.when(j < N_CHIPS)
        def _():
            @pl.when(j >= 2)
            def _():
                chunk_copy(j - 2).wait_send()

            acc = _dot_tn(x_ref[...], dh_ref[...])
            top, bottom = acc[:hi], acc[hi:]
            own_i[j % 2] = jnp.where(c == 0, top, bottom)
            snd_i[j % 2] = jnp.where(c == 0, bottom, top).astype(BF16)
            chunk_copy(j).start()

        @pl.when(j == N_CHIPS)
        def _():
            acc_o[...] = jnp.zeros_like(acc_o)

        @pl.when(j >= N_CHIPS)
        def _():
            acc_o[...] += _dot_tn(mix_ref[...], dz_ref[...]).reshape(N_CHIPS, GROUP, D_MODEL)

        @pl.when((j >= 1) & (j <= N_CHIPS))
        def _():
            chunk_copy(j - 1).wait_recv()
            summed = (own_i[(j - 1) % 2] + rcv_i[chunk_of(j - 1)].astype(F32)).astype(pi_ref.dtype)
            pi_ref[...] = summed
            if last is not None:
                rcv_i[chunk_of(j - 1)] = summed
                for r in range(3):
                    @pl.when(chunk_of(j - 1) == chips[r][2])
                    def _():
                        onward(r)[0].start()

        @pl.when(j == n_steps - 1)
        def _():
            snd_o[...] = acc_o[:, theirs_o, :].astype(BF16)
            out_copy().start()
            for k in (N_CHIPS - 2, N_CHIPS - 1):
                chunk_copy(k).wait_send()
            out_copy().wait_recv()
            summed = (acc_o[:, mine_o, :] + rcv_o[...].astype(F32)).astype(po_ref.dtype)
            po_ref[...] = summed
            if last is not None:
                rcv_o[...] = summed
                for r in range(3):
                    onward(r)[1].start()
            out_copy().wait_send()
            if last is not None:
                for r in range(3):
                    for cp in onward(r):
                        cp.wait()

    def col_block(j, cl):
        return (cl[2] + 1 + jnp.clip(j, 0, N_CHIPS - 1)) % N_CHIPS

    def tok_block(j):
        return jnp.maximum(j - N_CHIPS, 0)

    in_specs = [pl.BlockSpec((S, D_MODEL), lambda j, cl: (0, 0)),
                pl.BlockSpec((S, COLS), lambda j, cl: (0, col_block(j, cl))),
                pl.BlockSpec((tk, D_MODEL), lambda j, cl: (tok_block(j), 0)),
                pl.BlockSpec((tk, D_MODEL), lambda j, cl: (tok_block(j), 0)), ANY, ANY]
    out_specs = [pl.BlockSpec((None, None, hi, COLS), lambda j, cl: (cl[1], col_block(j - 1, cl), 0, 0)),
                 pl.BlockSpec((None, N_CHIPS, ho, D_MODEL), lambda j, cl: (cl[1], 0, 0, 0))]
    out_shape = [jax.ShapeDtypeStruct(p_i.shape, p_i.dtype), jax.ShapeDtypeStruct(p_o.shape, p_o.dtype)]
    scratch = [pltpu.VMEM((2, hi, COLS), F32), pltpu.VMEM((N_CHIPS, GROUP, D_MODEL), F32),
               pltpu.VMEM((2, hi, COLS), BF16), pltpu.VMEM((N_CHIPS, ho, D_MODEL), BF16),
               pltpu.VMEM((N_CHIPS, hi, COLS), BF16), pltpu.VMEM((N_CHIPS, ho, D_MODEL), BF16),
               pltpu.SemaphoreType.DMA((N_CHIPS + 1,)), pltpu.SemaphoreType.DMA((N_CHIPS + 1,))]
    extra, aliases, kind = (), {5: 0, 6: 1}, "dw_swap"
    if last is not None:
        extra, aliases, kind = tuple(last), {5: 0, 6: 1, 8: 2, 9: 3}, "dw_swap_exchange"
        in_specs += [ANY, ANY, ANY]
        out_specs += [ANY, ANY, ANY]
        out_shape += [jax.ShapeDtypeStruct(q.shape, q.dtype) for q in last[1:]]
        out_shape += [jax.ShapeDtypeStruct((N_CHIPS, SM_ROWS, GROUP), F32)]
        scratch += [pltpu.SemaphoreType.DMA((6,)), pltpu.SemaphoreType.DMA((6,)), pltpu.SemaphoreType.DMA((N_EXCH_SEMS,)),
                    pltpu.SemaphoreType.DMA((N_EXCH_SEMS,)), pltpu.SemaphoreType.DMA((2,)),
                    pltpu.VMEM((3, SM_ROWS, GROUP), F32)]
    grid_spec = pltpu.PrefetchScalarGridSpec(
        num_scalar_prefetch=1, grid=(n_steps,), in_specs=in_specs, out_specs=out_specs, scratch_shapes=scratch)
    return pl.pallas_call(
        body, name=kind, grid_spec=grid_spec, out_shape=out_shape, input_output_aliases=aliases,
        compiler_params=_vmem_params(dimension_semantics=("arbitrary",), has_side_effects=True,
                                     collective_id=COLLECTIVE_ID[kind]),
    )(cl_arr, xb, dhb, mixb, dzb, p_i, p_o, *extra)


def _adamw_math(w, g, m, v):
    nm = ADAM_B1 * m + (1.0 - ADAM_B1) * g
    nv = ADAM_B2 * v + (1.0 - ADAM_B2) * (g * g)
    c1 = 1.0 - ADAM_B1 ** ADAM_STEP
    c2 = 1.0 - ADAM_B2 ** ADAM_STEP
    return -ADAM_LR * ((nm / c1) / (jnp.sqrt(nv / c2) + ADAM_EPS) + ADAM_WD * w), nm, nv


def _adamw_small(ws, gs, ms, vs):
    n = len(ws)

    def body(*refs):
        for j in range(n):
            d, nm, nv = _adamw_math(*(refs[k * n + j][...] for k in range(4)))
            refs[4 * n + j][...] = d
            refs[5 * n + j][...] = nm
            refs[6 * n + j][...] = nv

    shapes = [jax.ShapeDtypeStruct(w.shape, F32) for w in ws]
    outs = pl.pallas_call(body, name="adamw_small", out_shape=shapes * 3, compiler_params=_vmem_params())(
        *ws, *gs, *ms, *vs)
    return outs[0:n], outs[n:2 * n], outs[2 * n:3 * n]


def _adamw(w, g, m, v, *, rows_per_step, name, copy_g=False):
    R, C = w.shape
    tr = rows_per_step

    def body(w_ref, g_ref, m_ref, v_ref, d_ref, nm_ref, nv_ref, *g_out):
        g_ = g_ref[...]
        d_ref[...], nm_ref[...], nv_ref[...] = _adamw_math(w_ref[...], g_, m_ref[...], v_ref[...])
        if copy_g:
            g_out[0][...] = g_

    spec = pl.BlockSpec((tr, C), lambda i: (i, 0))
    n_out = 4 if copy_g else 3
    return pl.pallas_call(
        body, name=name, grid=(R // tr,),
        in_specs=[spec] * 4, out_specs=[spec] * n_out,
        out_shape=[jax.ShapeDtypeStruct((R, C), F32)] * n_out,
        compiler_params=_vmem_params(dimension_semantics=("arbitrary",)),
    )(w, g, m, v)


def _gather_weights(wi16, wo16, cw):
    L = wi16.shape[0]
    hi_rows, ho_rows = D_MODEL // 2, GROUP // 2
    n_ici = 2 * L + 1
    n_fwd = 2 * L

    def body(wi_ref, wo_ref, cw_ref, *rest):
        wig = rest[0:L]
        wog = rest[L:2 * L]
        cwg = rest[2 * L]
        send_sems, recv_sems, loc_sems, vwi, vwo, vcw = rest[2 * L + 1:]
        x, y, c = _place()
        me_k = 2 * x + y
        sibling = (x, y, 1 - c)
        chips = _other_chips(x, y)

        def half_i(ref, blk):
            return ref.at[blk, pl.ds(c * hi_rows, hi_rows), :]

        def half_o(ref, blk):
            return ref.at[blk, pl.ds(c * ho_rows, ho_rows), :]

        def other_half_i(ref, blk):
            return ref.at[blk, pl.ds((1 - c) * hi_rows, hi_rows), :]

        def other_half_o(ref, blk):
            return ref.at[blk, pl.ds((1 - c) * ho_rows, ho_rows), :]

        stage_in = [pltpu.make_async_copy(wi_ref, vwi, loc_sems.at[0]), pltpu.make_async_copy(wo_ref, vwo, loc_sems.at[1]),
                    pltpu.make_async_copy(cw_ref, vcw, loc_sems.at[2])]
        local = []
        for l in range(L):
            local.append(pltpu.make_async_copy(vwi.at[l], wig[l].at[me_k], loc_sems.at[3 + 2 * l]))
            local.append(pltpu.make_async_copy(vwo.at[l], wog[l].at[me_k], loc_sems.at[3 + 2 * l + 1]))
        local.append(pltpu.make_async_copy(vcw, cwg.at[me_k], loc_sems.at[3 + 2 * L]))
        _handshake(PEERS_COLUMN)
        for cp in stage_in:
            cp.start()

        def remote(src, dst, sem, to):
            return pltpu.make_async_remote_copy(src_ref=src, dst_ref=dst, send_sem=send_sems.at[sem],
                                                recv_sem=recv_sems.at[sem], device_id=to, device_id_type=MESH)

        sends = []
        for r, (px, py, _) in enumerate(chips):
            to = (px, py, c)
            for l in range(L):
                sends.append(remote(half_i(wi_ref, l), half_i(wig[l], me_k), r * n_ici + 2 * l, to))
                sends.append(remote(half_o(wo_ref, l), half_o(wog[l], me_k), r * n_ici + 2 * l + 1, to))
            sends.append(remote(cw_ref, cwg.at[me_k], r * n_ici + 2 * L, to))
        for cp in sends:
            cp.start()
        for cp in stage_in:
            cp.wait()
        for cp in local:
            cp.start()

        base = 3 * n_ici
        fwds = []
        for r, (px, py, pk) in enumerate(chips):
            for l in range(L):
                remote(half_i(wig[l], pk), half_i(wig[l], pk), r * n_ici + 2 * l, sibling).wait_recv()
                f = remote(half_i(wig[l], pk), half_i(wig[l], pk), base + r * n_fwd + 2 * l, sibling)
                f.start()
                fwds.append(f)
                remote(half_o(wog[l], pk), half_o(wog[l], pk), r * n_ici + 2 * l + 1, sibling).wait_recv()
                f = remote(half_o(wog[l], pk), half_o(wog[l], pk), base + r * n_fwd + 2 * l + 1, sibling)
                f.start()
                fwds.append(f)
            remote(cwg.at[pk], cwg.at[pk], r * n_ici + 2 * L, sibling).wait_recv()
        for r, (px, py, pk) in enumerate(chips):
            for l in range(L):
                remote(other_half_i(wig[l], pk), other_half_i(wig[l], pk), base + r * n_fwd + 2 * l, sibling).wait_recv()
                remote(other_half_o(wog[l], pk), other_half_o(wog[l], pk), base + r * n_fwd + 2 * l + 1, sibling).wait_recv()
        for cp in sends + fwds:
            cp.wait_send()
        for cp in local:
            cp.wait()

    n_sem = 3 * n_ici + 3 * n_fwd
    out_shape = ([jax.ShapeDtypeStruct((N_CHIPS, D_MODEL, COLS), BF16)] * L
                 + [jax.ShapeDtypeStruct((N_CHIPS, GROUP, D_MODEL), BF16)] * L
                 + [jax.ShapeDtypeStruct((N_CHIPS,) + cw.shape, F32)])
    outs = pl.pallas_call(
        body, name="gather_weights",
        in_specs=[ANY, ANY, ANY], out_specs=[ANY] * (2 * L + 1), out_shape=out_shape,
        scratch_shapes=[pltpu.SemaphoreType.DMA((n_sem,)), pltpu.SemaphoreType.DMA((n_sem,)),
                        pltpu.SemaphoreType.DMA((2 * L + 4,)), pltpu.VMEM(wi16.shape, BF16), pltpu.VMEM(wo16.shape, BF16),
                        pltpu.VMEM(cw.shape, F32)],
        compiler_params=_vmem_params(has_side_effects=True, collective_id=COLLECTIVE_ID["gather_weights"]),
    )(wi16, wo16, cw)
    return outs[0:L], outs[L:2 * L], outs[2 * L]


def _sum_share(kc_arr, p_i, q_i, p_o, q_o, r_sms, *, nb):
    L = p_i.shape[0]
    n_steps, slots = L * nb, 2

    def body(kc_ref, pi_ref, a0, a1, a2, po_ref, b0, b1, b2, *rest):
        del kc_ref
        sm_refs, (oi_ref, oo_ref, os_ref, vi, vo, vs, loc_sems, send_sems, recv_sems) = rest[:L], rest[L:]
        x, y, c = _place()
        t = pl.program_id(0) * nb + pl.program_id(1)

        def small_copies():
            dst = os_ref.at[:, pl.ds(pl.multiple_of(c * (SM_ROWS // 2), SUBLANES), SM_ROWS // 2), :]
            return (pltpu.make_async_copy(vs, dst, loc_sems.at[2 * n_steps]),
                    pltpu.make_async_remote_copy(src_ref=vs, dst_ref=dst, send_sem=send_sems.at[2 * n_steps],
                                                 recv_sem=recv_sems.at[2 * n_steps], device_id=(x, y, 1 - c),
                                                 device_id_type=MESH))

        def copies(s):
            l, i = s // nb, s % nb
            out = []
            for j, (v, o) in enumerate(((vi, oi_ref), (vo, oo_ref))):
                tr = v.shape[1]
                src, dst = v.at[s % slots], o.at[l, pl.ds((c * nb + i) * tr, tr), :]
                out.append((pltpu.make_async_copy(src, dst, loc_sems.at[2 * s + j]),
                            pltpu.make_async_remote_copy(src_ref=src, dst_ref=dst, send_sem=send_sems.at[2 * s + j],
                                                         recv_sem=recv_sems.at[2 * s + j], device_id=(x, y, 1 - c),
                                                         device_id_type=MESH)))
            return out

        def sent(s):
            for mine, theirs in copies(s):
                mine.wait()
                theirs.wait_send()

        @pl.when(t == 0)
        def _():
            _handshake(PEERS_SIBLING)
            for l in range(L):
                vs[l] = ((sm_refs[l][0] + sm_refs[l][1]) + sm_refs[l][2]) + sm_refs[l][3]
            for cp in small_copies():
                cp.start()

        @pl.when(t >= slots)
        def _():
            sent(t - slots)

        f = lambda ref: ref[...].astype(F32)
        vi[t % slots] = ((f(pi_ref) + f(a0)) + f(a1)) + f(a2)
        vo[t % slots] = ((f(po_ref) + f(b0)) + f(b1)) + f(b2)
        for mine, theirs in copies(t):
            mine.start()
            theirs.start()

        @pl.when(t == n_steps - 1)
        def _():
            for s in range(n_steps - slots, n_steps):
                sent(s)
            for s in range(n_steps):
                for _, theirs in copies(s):
                    theirs.wait_recv()
            mine, theirs = small_copies()
            mine.wait()
            theirs.wait()

    def specs(p):
        tr, cols = p.shape[2] // nb, p.shape[3]
        chunk = pl.BlockSpec((None, None, tr, cols), lambda l, i, kc: (l, kc[0], i, 0))
        got = [pl.BlockSpec((None, None, tr, cols), lambda l, i, kc, _j=j: (_j, l, i, 0)) for j in range(3)]
        return [chunk] + got, pltpu.VMEM((slots, tr, cols), F32)

    (in_i, v_i), (in_o, v_o) = specs(p_i), specs(p_o)
    in_sm = [pl.BlockSpec((N_CHIPS, SM_ROWS // 2, GROUP), lambda l, i, kc: (0, kc[1], 0))] * L
    grid_spec = pltpu.PrefetchScalarGridSpec(
        num_scalar_prefetch=1, grid=(L, nb), in_specs=in_i + in_o + in_sm, out_specs=[ANY, ANY, ANY],
        scratch_shapes=[v_i, v_o, pltpu.VMEM((L, SM_ROWS // 2, GROUP), F32)]
        + [pltpu.SemaphoreType.DMA((2 * n_steps + 1,))] * 3)
    return pl.pallas_call(
        body, name="sum_share", grid_spec=grid_spec,
        out_shape=[jax.ShapeDtypeStruct((L, 2 * p.shape[2], p.shape[3]), F32) for p in (p_i, p_o)]
        + [jax.ShapeDtypeStruct((L, SM_ROWS, GROUP), F32)],
        compiler_params=_vmem_params(dimension_semantics=("arbitrary",) * 2, has_side_effects=True,
                                     collective_id=COLLECTIVE_ID["sum_share"]),
    )(kc_arr, p_i, q_i, q_i, q_i, p_o, q_o, q_o, q_o, *r_sms)


WEIGHTS = ("ln_g", "ln_b", "w_in", "b_in", "conv_a_w", "conv_a_b", "norm_a_g", "norm_a_b", "conv_b_w", "pool_w",
           "pool_scale", "sgu_ln_g", "sgu_ln_b", "sgu_w", "sgu_bias", "w_out", "b_out")


def _pad_rows(a, rows):
    return jnp.pad(a, ((0, rows - a.shape[0]), (0, 0)))


def _indicator_consts():
    seg = jnp.where((jnp.arange(GROUP)[:, None] // HEAD) == (jnp.arange(GROUP)[None, :] // HEAD),
                    1.0 / HEAD, 0.0).astype(BF16)
    e4 = ((jnp.arange(GROUP)[:, None] // HEAD) == jnp.arange(128)[None, :]).astype(BF16)
    return seg, e4


def _layer_consts(p, conv_full):
    L = conv_full.shape[0]
    same_head = jnp.eye(4, dtype=F32)[:, None, :, None] > 0

    def rows_to(a, rows):
        return jnp.pad(a, ((0, 0), (0, rows - a.shape[1]), (0, 0)))

    s256 = jnp.stack([p[n] for n in ("conv_a_b", "norm_a_g", "norm_a_b", "pool_scale", "sgu_ln_g", "sgu_ln_b")], axis=1)
    pw = jnp.where(same_head, p["pool_w"][:, :, :, None, :], 0.0).reshape(L, GROUP, GROUP)
    return dict(
        caw=rows_to(conv_full[:, :KA], 32), cbw=rows_to(conv_full[:, KA:], 8), s256=rows_to(s256, 8),
        pw=pw.astype(BF16),
        wm=jnp.transpose(p["sgu_w"], (0, 2, 1, 3)).reshape(L, SGU_BLOCK, 4 * SGU_BLOCK),
        wmt=jnp.transpose(p["sgu_w"], (0, 1, 3, 2)).reshape(L, 4 * SGU_BLOCK, SGU_BLOCK),
        sb=jnp.repeat(jnp.transpose(p["sgu_bias"], (0, 2, 1)), HEAD, axis=2),
        v1024=rows_to(jnp.stack([p["b_out"], p["ln_g"], p["ln_b"]], axis=1), 8),
        bin=p["b_in"][:, None, :])


def _unpack_small(sm):
    L = sm.shape[0]
    owc = jnp.concatenate([sm[:, ROW_WC:ROW_WC + SGU_BLOCK], sm[:, ROW_WC + SGU_BLOCK:ROW_WC + 2 * SGU_BLOCK]], axis=2)
    return dict(
        conv_a_b=sm[:, 0], norm_a_g=sm[:, 1], norm_a_b=sm[:, 2], pool_scale=sm[:, 3], sgu_ln_g=sm[:, 4],
        sgu_ln_b=sm[:, 5], conv_b_w=sm[:, ROW_CBW:ROW_CBW + KB], conv_a_w=sm[:, ROW_CAW:ROW_CAW + KA],
        pool_w=jnp.transpose(sm[:, ROW_PW:ROW_PW + HEAD].reshape(L, HEAD, 4, HEAD), (0, 2, 1, 3)),
        ln_g=sm[:, ROW_LNG:ROW_LNG + 4].reshape(L, D_MODEL), ln_b=sm[:, ROW_LNB:ROW_LNB + 4].reshape(L, D_MODEL),
        b_out=sm[:, ROW_BOUT:ROW_BOUT + 4].reshape(L, D_MODEL),
        b_in=sm[:, ROW_BIN:ROW_BIN + N_SLICES].reshape(L, IN_WIDTH),
        sgu_w=jnp.transpose(owc.reshape(L, SGU_BLOCK, 4, SGU_BLOCK), (0, 2, 1, 3)),
        sgu_bias=sm[:, ROW_SB:ROW_SB + 4, 0:SGU_BLOCK])


def _step(p, m, v, x, target, *, tile_f, tile_b, k_steps):
    L = p["ln_g"].shape[0]
    xi, yi, ci = _place()
    me_k = 2 * xi + yi
    hi_rows, ho_rows = D_MODEL // 2, GROUP // 2

    cw = jnp.concatenate([p["conv_a_w"], p["conv_b_w"]], axis=1).reshape(-1, 128)
    cw_rows = cw.shape[0]
    cw = _pad_rows(cw, -(-cw_rows // SUBLANES) * SUBLANES)
    wi16 = p["w_in"].astype(BF16)
    wo16 = p["w_out"].astype(BF16)
    wig0, wog0, cwg = _gather_weights(wi16[0:1], wo16[0:1], cw)
    cwg = cwg[:, :cw_rows].reshape(N_CHIPS, L, KA + KB, HEAD)
    conv_full = jnp.transpose(cwg, (1, 2, 0, 3)).reshape(L, KA + KB, GROUP)
    seg, e4 = _indicator_consts()
    k = _layer_consts(p, conv_full)
    layer = [jnp.full((1,), l, jnp.int32) for l in range(L)]

    hcur = x
    saved, wig, wog = [], [wig0[0]], [wog0[0]]
    for l in range(L):
        nxt = (wi16, wo16) if l + 1 < L else None
        outs = _fwd_layer(layer[l], hcur, wig[l], k["bin"], k["caw"], k["cbw"], k["s256"], seg, k["pw"], k["wm"], k["sb"],
                          wog[l], k["v1024"], tile=tile_f, nxt=nxt, target=None if nxt is not None else target)
        y, xb, h, aux, mixb, z = outs[0:6]
        if nxt is not None:
            wig.append(outs[6])
            wog.append(outs[7])
        saved.append((xb, h, aux, mixb, z))
        hcur = y

    dy = hcur
    loss_local = outs[6][0, 0]

    p_i = lax.empty((L, N_CHIPS, hi_rows, COLS), BF16)
    p_o = lax.empty((L, N_CHIPS, ho_rows, D_MODEL), BF16)
    q_i = lax.empty((3, L, hi_rows, COLS), BF16)
    q_o = lax.empty((3, L, ho_rows, D_MODEL), BF16)
    r_sm = [None] * L
    pending = None
    for l in reversed(range(L)):
        xb, h, aux, mixb, z = saved[l]
        exch = None if pending is None else (p_i, p_o, pending, q_i, q_o)
        outs = _bwd_layer(layer[l], dy, z, h, aux, wig[l], k["caw"], k["cbw"], k["s256"], seg, k["pw"], k["wm"],
                          k["wmt"], k["sb"], wog[l], k["v1024"], e4, tile=tile_b, exch=exch)
        dy, dhb, dzb, osm = outs[0:4]
        if l == L - 1:
            osm = osm.at[ROW_LOSS, 0].set(loss_local)
        if exch is not None:
            q_i, q_o, r_sm[l + 1] = outs[4:7]
        cl_arr = jnp.stack([ci, jnp.int32(l), me_k]).astype(jnp.int32)
        if l > 0:
            p_i, p_o = _dw_swap(cl_arr, xb, dhb, mixb, dzb, p_i, p_o, k_steps=k_steps)
        else:
            p_i, p_o, q_i, q_o, r_sm[0] = _dw_swap(cl_arr, xb, dhb, mixb, dzb, p_i, p_o, k_steps=k_steps,
                                                   last=(osm, q_i, q_o))
        pending = osm
    grad_x = dy

    kc_arr = jnp.stack([me_k, ci]).astype(jnp.int32)
    g_i, g_o, summed = _sum_share(kc_arr, p_i, q_i, p_o, q_o, r_sm, nb=2)
    loss = summed[L - 1, ROW_LOSS, 0]
    grads = _unpack_small(summed)
    for n in ("conv_a_w", "conv_b_w"):
        grads[n] = lax.dynamic_slice_in_dim(grads[n], me_k * HEAD, HEAD, axis=2)

    grads["w_in"] = g_i
    grads["w_out"] = g_o

    delta, new_m, new_v = {}, {}, {}
    for n, tr in (("w_in", 512), ("w_out", 256)):
        shp = p[n].shape
        args = [a.reshape(shp[0] * shp[1], shp[2]) for a in (p[n], grads[n], m[n], v[n])]
        outs = _adamw(*args, rows_per_step=tr, name="adamw_" + n, copy_g=True)
        delta[n], new_m[n], new_v[n], grads[n] = (a.reshape(shp) for a in outs)
    small = [n for n in WEIGHTS if n not in ("w_in", "w_out")]
    flat = [[a[n].reshape(-1, a[n].shape[-1]) for n in small] for a in (p, grads, m, v)]
    outs = _adamw_small(*flat)
    for j, n in enumerate(small):
        delta[n], new_m[n], new_v[n] = (o[j].reshape(p[n].shape) for o in outs)

    return (loss, grad_x[None], *[grads[n] for n in WEIGHTS], *[delta[n] for n in WEIGHTS],
            *[new_m[n] for n in WEIGHTS], *[new_v[n] for n in WEIGHTS])


def kernel(x, ln_g, ln_b, w_in, b_in, conv_a_w, conv_a_b, norm_a_g, norm_a_b, conv_b_w, pool_w, pool_scale, sgu_ln_g, sgu_ln_b, sgu_w, sgu_bias, w_out, b_out, loss_target, m_ln_g, m_ln_b, m_w_in, m_b_in, m_conv_a_w, m_conv_a_b, m_norm_a_g, m_norm_a_b, m_conv_b_w, m_pool_w, m_pool_scale, m_sgu_ln_g, m_sgu_ln_b, m_sgu_w, m_sgu_bias, m_w_out, m_b_out, v_ln_g, v_ln_b, v_w_in, v_b_in, v_conv_a_w, v_conv_a_b, v_norm_a_g, v_norm_a_b, v_conv_b_w, v_pool_w, v_pool_scale, v_sgu_ln_g, v_sgu_ln_b, v_sgu_w, v_sgu_bias, v_w_out, v_b_out):
    p = dict(ln_g=ln_g, ln_b=ln_b, w_in=w_in, b_in=b_in, conv_a_w=conv_a_w, conv_a_b=conv_a_b, norm_a_g=norm_a_g,
             norm_a_b=norm_a_b, conv_b_w=conv_b_w, pool_w=pool_w, pool_scale=pool_scale, sgu_ln_g=sgu_ln_g,
             sgu_ln_b=sgu_ln_b, sgu_w=sgu_w, sgu_bias=sgu_bias, w_out=w_out, b_out=b_out)
    m = dict(ln_g=m_ln_g, ln_b=m_ln_b, w_in=m_w_in, b_in=m_b_in, conv_a_w=m_conv_a_w, conv_a_b=m_conv_a_b,
             norm_a_g=m_norm_a_g, norm_a_b=m_norm_a_b, conv_b_w=m_conv_b_w, pool_w=m_pool_w, pool_scale=m_pool_scale,
             sgu_ln_g=m_sgu_ln_g, sgu_ln_b=m_sgu_ln_b, sgu_w=m_sgu_w, sgu_bias=m_sgu_bias, w_out=m_w_out, b_out=m_b_out)
    v = dict(ln_g=v_ln_g, ln_b=v_ln_b, w_in=v_w_in, b_in=v_b_in, conv_a_w=v_conv_a_w, conv_a_b=v_conv_a_b,
             norm_a_g=v_norm_a_g, norm_a_b=v_norm_a_b, conv_b_w=v_conv_b_w, pool_w=v_pool_w, pool_scale=v_pool_scale,
             sgu_ln_g=v_sgu_ln_g, sgu_ln_b=v_sgu_ln_b, sgu_w=v_sgu_w, sgu_bias=v_sgu_bias, w_out=v_w_out, b_out=v_b_out)
    return _step(p, m, v, x[0], loss_target[0], tile_f=256, tile_b=256, k_steps=4)
```

```python
import jax
import jax.numpy as jnp
from jax import lax
from jax.experimental import pallas as pl
from jax.experimental.pallas import tpu as pltpu

F32 = jnp.float32
BF16 = jnp.bfloat16
MESH = pl.DeviceIdType.MESH

D_MODEL = 1024
GROUP = 256
HEAD = 64
N_SLICES = 12
IN_WIDTH = N_SLICES * GROUP
N_CHIPS = 4
COLS = IN_WIDTH // N_CHIPS
KA = 31
KB = 3
SUBLANES = 8
HALO_A, HALO_B, HALO_C = 32, 8, 16
N_GATHER_SEMS = 12
N_EXCH_SEMS = 10
SGU_BLOCK = 128
CHUNK = 64
LN_EPS = 1e-5
ROWS = 64
V7X_VMEM_BYTES = 64 * 1024 * 1024
VMEM_LIMIT = V7X_VMEM_BYTES - 8 * 1024 * 1024

ADAM_LR, ADAM_B1, ADAM_B2, ADAM_EPS, ADAM_WD, ADAM_STEP = 0.001, 0.9, 0.999, 1e-08, 0.01, 10


ANY = pl.BlockSpec(memory_space=pl.ANY)


def _vmem_params(**kw):
    return pltpu.CompilerParams(vmem_limit_bytes=VMEM_LIMIT, **kw)


def _whole(a):
    return pl.BlockSpec(a.shape, lambda i, l, _n=a.ndim: (0,) * _n)


def _of_layer(a):
    return pl.BlockSpec((None,) + a.shape[1:], lambda i, l, _n=a.ndim: (l[0],) + (0,) * (_n - 1))


def _place():
    return lax.axis_index("x"), lax.axis_index("y"), lax.axis_index("c")


def _other_chips(x, y):
    return [(1 - x, y, 2 * (1 - x) + y), (x, 1 - y, 2 * x + (1 - y)), (1 - x, 1 - y, 2 * (1 - x) + (1 - y))]


PEERS_SIBLING, PEERS_COLUMN = "sibling", "sibling and the same core of the other chips"
COLLECTIVE_ID = dict(sum_share=0, dw_swap=1, gather_weights=2, fwd_layer_gather=3, bwd_layer_exchange=4,
                     dw_swap_exchange=5)


def _handshake(peers):
    x, y, c = _place()
    ids = [(x, y, 1 - c)]
    if peers == PEERS_COLUMN:
        ids += [(px, py, c) for px, py, _ in _other_chips(x, y)]
    barrier = pltpu.get_barrier_semaphore()
    for to in ids:
        pl.semaphore_signal(barrier, inc=1, device_id=to, device_id_type=MESH)
    pl.semaphore_wait(barrier, len(ids))


def _sig(v):
    return 0.5 * jnp.tanh(0.5 * v) + 0.5


def _dot(a, b):
    return jnp.dot(a, b, preferred_element_type=F32)


def _dot_nt(a, b):
    return lax.dot_general(a, b, (((1,), (1,)), ((), ())), preferred_element_type=F32)


def _dot_tn(a, b):
    return lax.dot_general(a, b, (((0,), (0,)), ((), ())), preferred_element_type=F32)


def _segdot(v, m):
    hi = v.astype(BF16)
    lo = (v - hi.astype(F32)).astype(BF16)
    return _dot(hi, m) + _dot(lo, m)


def _colsum(v):
    return jnp.sum(v, axis=0, keepdims=True)


def _rowmean(v):
    return jnp.mean(v, axis=-1, keepdims=True)


def _lane_group(n):
    return lax.broadcasted_iota(jnp.int32, (1, n), 1) // HEAD


def _pool_cnt(tile, t_rows):
    pos = tile * t_rows + lax.broadcasted_iota(jnp.int32, (t_rows, GROUP), 0) + 1
    grp = lax.broadcasted_iota(jnp.int32, (t_rows, GROUP), 1) // HEAD
    win = jnp.where(grp == 0, 2, jnp.where(grp == 1, 4, jnp.where(grp == 2, 8, 16)))
    return jnp.minimum(pos, win).astype(F32)


def _sgu_masks(wm_ref, wmt_ref, wm_s, wmt_s):
    r = lax.broadcasted_iota(jnp.int32, (SGU_BLOCK, 4 * SGU_BLOCK), 0) // CHUNK
    c = (lax.broadcasted_iota(jnp.int32, (SGU_BLOCK, 4 * SGU_BLOCK), 1) % SGU_BLOCK) // CHUNK
    wm_s[...] = jnp.where(c <= r, wm_ref[...], 0.0).astype(BF16)
    if wmt_ref is not None:
        rt = (lax.broadcasted_iota(jnp.int32, (4 * SGU_BLOCK, SGU_BLOCK), 0) % SGU_BLOCK) // CHUNK
        ct = lax.broadcasted_iota(jnp.int32, (4 * SGU_BLOCK, SGU_BLOCK), 1) // CHUNK
        wmt_s[...] = jnp.where(rt <= ct, wmt_ref[...], 0.0).astype(BF16)


def _vstack(v_blk):
    grp = _lane_group(GROUP)
    return jnp.concatenate([jnp.where(grp == h, v_blk, 0.0) for h in range(4)], axis=0).astype(BF16)


def _gather_next(step, nt, nwi, nwo, gwi, gwo, send_sems, recv_sems, loc_sems, vwi, vwo):
    x, y, c = _place()
    me_k = 2 * x + y
    sibling = (x, y, 1 - c)
    chips = _other_chips(x, y)
    hi, ho = D_MODEL // 2, GROUP // 2
    fwd_sems = N_GATHER_SEMS // 2

    def rc(src, dst, sem, to):
        return pltpu.make_async_remote_copy(src_ref=src, dst_ref=dst, send_sem=send_sems.at[sem],
                                            recv_sem=recv_sems.at[sem], device_id=to, device_id_type=MESH)

    def blk(ref, k, n, cc):
        return ref.at[k, pl.ds(cc * n, n), :]

    def ici(r):
        px, py, _ = chips[r]
        to = (px, py, c)
        return [rc(nwi.at[pl.ds(c * hi, hi), :], blk(gwi, me_k, hi, c), 2 * r, to),
                rc(nwo.at[pl.ds(c * ho, ho), :], blk(gwo, me_k, ho, c), 2 * r + 1, to)]

    def landed(r, cc, base):
        pk = chips[r][2]
        return [rc(blk(gwi, pk, hi, cc), blk(gwi, pk, hi, cc), base + 2 * r, sibling),
                rc(blk(gwo, pk, ho, cc), blk(gwo, pk, ho, cc), base + 2 * r + 1, sibling)]

    def stage_in():
        return [pltpu.make_async_copy(nwi, vwi, loc_sems.at[0]), pltpu.make_async_copy(nwo, vwo, loc_sems.at[1])]

    def local():
        return [pltpu.make_async_copy(vwi, gwi.at[me_k], loc_sems.at[2]),
                pltpu.make_async_copy(vwo, gwo.at[me_k], loc_sems.at[3])]

    @pl.when(step == 0)
    def _():
        _handshake(PEERS_COLUMN)
        for cp in stage_in():
            cp.start()
        for r in range(3):
            for cp in ici(r):
                cp.start()

    @pl.when(step == 1)
    def _():
        for cp in stage_in():
            cp.wait()
        for cp in local():
            cp.start()

    @pl.when(step == (3 * nt) // 4)
    def _():
        for r in range(3):
            for got, fwd in zip(landed(r, c, 0), landed(r, c, fwd_sems)):
                got.wait_recv()
                fwd.start()

    @pl.when(step == nt - 1)
    def _():
        for r in range(3):
            for got in landed(r, 1 - c, fwd_sems):
                got.wait_recv()
        for r in range(3):
            for cp in ici(r) + landed(r, c, fwd_sems):
                cp.wait_send()
        for cp in local():
            cp.wait()


def _fwd_layer(larr, x, wi, bin_, caw, cbw, s256, seg, pw, wm, sb, wo, v1024, *, tile, nxt=None, target=None):
    assert nxt is None or target is None
    S = x.shape[0]
    T = tile
    nt = S // T
    alpha = float((2.0 * 4) ** 0.25)
    n_in = 13 + (2 if nxt is not None else 0) + (1 if target is not None else 0)
    n_out = 6 + (2 if nxt is not None else 0) + (1 if target is not None else 0)

    def body(*refs):
        l_ref = refs[0]
        (x_ref, wi_ref, bin_ref, caw_ref, cbw_ref, s256_ref, seg_ref, pw_ref, wm_ref, sb_ref, wo_ref,
         v1024_ref) = refs[1:13]
        y_ref, xb_ref, h_ref, aux_ref, mix_ref, z_ref = refs[n_in:n_in + 6]
        abuf, bbuf, cbuf, wm_s, shf = refs[n_in + n_out:n_in + n_out + 5]
        i = pl.program_id(0)
        if nxt is not None:
            _gather_next(i, nt, refs[13].at[l_ref[0] + 1], refs[14].at[l_ref[0] + 1], refs[n_in + 6], refs[n_in + 7],
                         *refs[n_in + n_out + 5:])

        @pl.when(i == 0)
        def _():
            abuf[0:HALO_A, :] = jnp.zeros((HALO_A, GROUP), F32)
            bbuf[0:HALO_B, :] = jnp.zeros((HALO_B, GROUP), F32)
            cbuf[0:HALO_C, :] = jnp.zeros((HALO_C, GROUP), F32)
            _sgu_masks(wm_ref, None, wm_s, None)

        x = x_ref[...]
        xb = x.astype(BF16)
        xb_ref[...] = xb
        for k in range(N_CHIPS):
            h_ref[:, COLS * k:COLS * (k + 1)] = _dot(xb, wi_ref[k]) + bin_ref[:, COLS * k:COLS * (k + 1)]

        def hs(j):
            return h_ref[:, GROUP * j:GROUP * (j + 1)]

        abuf[HALO_A:HALO_A + T, :] = hs(0) * _sig(hs(1))
        span = T + HALO_A - SUBLANES
        for p in range(1, SUBLANES):
            shf[p - 1, :, :] = abuf[p:p + span, :]
        for r0 in range(0, T, ROWS):
            acc = None
            for k in range(KA):
                off = HALO_A - (KA - 1) + k
                p, q8 = off % SUBLANES, off - off % SUBLANES
                win = abuf[r0 + q8:r0 + q8 + ROWS, :] if p == 0 else shf[p - 1, r0 + q8:r0 + q8 + ROWS, :]
                term = caw_ref[k:k + 1, :] * win
                acc = term if acc is None else acc + term
            aux_ref[r0:r0 + ROWS, 0:GROUP] = acc + s256_ref[0:1, :]
        abuf[0:HALO_A, :] = abuf[T:T + HALO_A, :]
        a1 = aux_ref[:, 0:GROUP]
        segm = seg_ref[...]
        cen = a1 - _segdot(a1, segm)
        var = _segdot(cen * cen, segm)
        a2 = cen * lax.rsqrt(var + LN_EPS) * s256_ref[1:2, :] + s256_ref[2:3, :]
        az = hs(2)
        mix_ref[:, 0:GROUP] = (a2 * _sig(a2) * (az * _sig(az))).astype(BF16)

        bbuf[HALO_B:HALO_B + T, :] = hs(4) * hs(5)
        for r0 in range(0, T, ROWS):
            acc = None
            for k in range(KB):
                off = HALO_B - (KB - 1) + k + r0
                term = cbw_ref[k:k + 1, :] * bbuf[off:off + ROWS, :]
                acc = term if acc is None else acc + term
            aux_ref[r0:r0 + ROWS, GROUP:2 * GROUP] = acc
        bbuf[0:HALO_B, :] = bbuf[T:T + HALO_B, :]
        bz = hs(6)
        mix_ref[:, GROUP:2 * GROUP] = (hs(3) * aux_ref[:, GROUP:2 * GROUP] * (bz * _sig(bz))).astype(BF16)

        ch = hs(7)
        cbuf[HALO_C:HALO_C + T, :] = ch
        hi_lane = (lax.broadcasted_iota(jnp.int32, (1, 128), 1) // HEAD) == 1
        for r0 in range(0, T, ROWS):
            def win(col, j0, j1):
                s = None
                for j in range(j0, j1):
                    off = HALO_C - j + r0
                    term = cbuf[off:off + ROWS, 128 * col:128 * (col + 1)]
                    s = term if s is None else s + term
                return s
            w0 = win(0, 0, 2) + jnp.where(hi_lane, win(0, 2, 4), 0.0)
            w1 = win(1, 0, 8) + jnp.where(hi_lane, win(1, 8, 16), 0.0)
            aux_ref[r0:r0 + ROWS, 2 * GROUP:2 * GROUP + 128] = w0
            aux_ref[r0:r0 + ROWS, 2 * GROUP + 128:3 * GROUP] = w1
        cbuf[0:HALO_C, :] = cbuf[T:T + HALO_C, :]
        pooled = aux_ref[:, 2 * GROUP:3 * GROUP] / _pool_cnt(i, T) - ch
        aux_ref[:, 2 * GROUP:3 * GROUP] = pooled
        q = _dot(pooled.astype(BF16), pw_ref[...])
        cz = hs(8)
        mix_ref[:, 2 * GROUP:3 * GROUP] = (q * s256_ref[3:4, :] * (cz * _sig(cz))).astype(BF16)

        dv = hs(10)
        cen = dv - _rowmean(dv)
        var = _rowmean(cen * cen)
        v = cen * lax.rsqrt(var + LN_EPS) * s256_ref[4:5, :] + s256_ref[5:6, :]
        sps = []
        for n in range(T // SGU_BLOCK):
            vb = v[n * SGU_BLOCK:(n + 1) * SGU_BLOCK, :]
            sps.append(_dot(wm_s[...], _vstack(vb)) + sb_ref[...])
        sp = jnp.concatenate(sps, axis=0)
        dz = hs(11)
        mix_ref[:, 3 * GROUP:4 * GROUP] = (hs(9) * sp * (dz * _sig(dz))).astype(BF16)

        out = v1024_ref[0:1, :]
        for k in range(N_CHIPS):
            out = out + _dot(mix_ref[:, GROUP * k:GROUP * (k + 1)], wo_ref[k])
        z = alpha * x + out
        z_ref[...] = z
        cen = z - _rowmean(z)
        var = _rowmean(cen * cen)
        y = cen * lax.rsqrt(var + LN_EPS) * v1024_ref[1:2, :] + v1024_ref[2:3, :]
        if target is None:
            y_ref[...] = y
        else:
            t_ref, loss_ref = refs[13], refs[n_in + 6]

            @pl.when(i == 0)
            def _():
                loss_ref[...] = jnp.zeros_like(loss_ref)
            err = y - t_ref[...]
            y_ref[...] = err * (1.0 / D_MODEL)
            loss_ref[...] += jnp.sum(_colsum(err * err), axis=1, keepdims=True) * (0.5 / D_MODEL)

    def rows(width):
        return pl.BlockSpec((T, width), lambda i, l: (i, 0))

    consts = (wi, bin_, caw, cbw, s256, seg, pw, wm, sb, wo, v1024)
    in_specs = [rows(D_MODEL)] + [_whole(a) if a is wi or a is seg or a is wo else _of_layer(a) for a in consts]
    out_specs = [rows(D_MODEL), rows(D_MODEL), rows(IN_WIDTH), rows(3 * GROUP), rows(D_MODEL), rows(D_MODEL)]
    out_shape = [jax.ShapeDtypeStruct((S, D_MODEL), F32), jax.ShapeDtypeStruct((S, D_MODEL), BF16),
                 jax.ShapeDtypeStruct((S, IN_WIDTH), F32), jax.ShapeDtypeStruct((S, 3 * GROUP), F32),
                 jax.ShapeDtypeStruct((S, D_MODEL), BF16), jax.ShapeDtypeStruct((S, D_MODEL), F32)]
    scratch = [pltpu.VMEM((T + HALO_A, GROUP), F32), pltpu.VMEM((T + HALO_B, GROUP), F32),
               pltpu.VMEM((T + HALO_C, GROUP), F32), pltpu.VMEM((SGU_BLOCK, 4 * SGU_BLOCK), BF16),
               pltpu.VMEM((SUBLANES - 1, T + HALO_A - SUBLANES, GROUP), F32)]
    extra = ()
    if nxt is not None:
        extra = tuple(nxt)
        in_specs += [ANY, ANY]
        out_specs += [ANY, ANY]
        out_shape += [jax.ShapeDtypeStruct((N_CHIPS, D_MODEL, COLS), BF16),
                      jax.ShapeDtypeStruct((N_CHIPS, GROUP, D_MODEL), BF16)]
        scratch += [pltpu.SemaphoreType.DMA((N_GATHER_SEMS,)), pltpu.SemaphoreType.DMA((N_GATHER_SEMS,)),
                    pltpu.SemaphoreType.DMA((4,)), pltpu.VMEM((D_MODEL, COLS), BF16), pltpu.VMEM((GROUP, D_MODEL), BF16)]
    if target is not None:
        extra = (target,)
        in_specs += [rows(D_MODEL)]
        out_specs += [pl.BlockSpec((8, 128), lambda i, l: (0, 0))]
        out_shape += [jax.ShapeDtypeStruct((8, 128), F32)]
    grid_spec = pltpu.PrefetchScalarGridSpec(num_scalar_prefetch=1, grid=(nt,), in_specs=in_specs,
                                             out_specs=out_specs, scratch_shapes=scratch)
    return pl.pallas_call(
        body, name=("fwd_layer_loss" if target is not None else "fwd_layer") if nxt is None else "fwd_layer_gather",
        grid_spec=grid_spec, out_shape=out_shape,
        compiler_params=_vmem_params(dimension_semantics=("arbitrary",), **(
            dict(has_side_effects=True, collective_id=COLLECTIVE_ID["fwd_layer_gather"]) if nxt is not None else {})),
    )(larr, x, *consts, *extra)


ROW_CBW = 8
ROW_CAW = 16
ROW_LOSS = 7
ROW_PW = 48
ROW_LNG = 112
ROW_LNB = 116
ROW_BOUT = 120
ROW_BIN = 124
ROW_WC = 136
ROW_SB = 392
SM_ROWS = 400


def _exchange_comm(start, mid, finish, l, p_i, p_o, sm, r_i, r_o, r_sm, send_sems, recv_sems, loc_sems, vm):
    x, y, c = _place()
    me_k = 2 * x + y
    chips = _other_chips(x, y)

    def rc(src, dst, sem, to):
        return pltpu.make_async_remote_copy(src_ref=src, dst_ref=dst, send_sem=send_sems.at[sem],
                                            recv_sem=recv_sems.at[sem], device_id=to, device_id_type=MESH)

    def big(r):
        px, py, pk = chips[r]
        to = (px, py, c)
        return [rc(p_i.at[l, pk], r_i.at[r, l], 2 * r, to), rc(p_o.at[l, pk], r_o.at[r, l], 2 * r + 1, to)]

    def stage():
        return pltpu.make_async_copy(sm, vm.at[0], loc_sems.at[0])

    def to_sibling():
        return rc(sm, vm.at[1], N_EXCH_SEMS - 4, (x, y, 1 - c))

    half = pl.ds(pl.multiple_of(c * (SM_ROWS // 2), SUBLANES), SM_ROWS // 2)

    def chip_sum(r):
        px, py, pk = chips[r]
        return rc(vm.at[2, half], r_sm.at[me_k, half], N_EXCH_SEMS - 3 + r, (px, py, c))

    def keep():
        return pltpu.make_async_copy(vm.at[2, half], r_sm.at[me_k, half], loc_sems.at[1])

    with_big, with_small = p_i is not None, sm is not None

    @pl.when(start)
    def _():
        _handshake(PEERS_COLUMN)
        if with_small:
            stage().start()
            to_sibling().start()
        if with_big:
            for r in range(3):
                for cp in big(r):
                    cp.start()

    if with_small:
        @pl.when(mid)
        def _():
            stage().wait()
            to_sibling().wait_recv()
            vm[2] = vm[0] + vm[1]
            keep().start()
            for r in range(3):
                chip_sum(r).start()

    @pl.when(finish)
    def _():
        if with_big:
            for r in range(3):
                for cp in big(r):
                    cp.wait()
        if with_small:
            to_sibling().wait_send()
            for r in range(3):
                chip_sum(r).wait()
            keep().wait()


RC = 32
RC_WIDE = 16
ACC_ROWS = 136


def _rsum8(v):
    r = v[0:8]
    for j in range(1, v.shape[0] // 8):
        r = r + v[8 * j:8 * j + 8]
    return r


def _bwd_layer(larr, dy, z, h, aux, wi, caw, cbw, s256, seg, pw, wm, wmt, sb, wo, v1024, e4, *, tile, exch=None):
    S = dy.shape[0]
    T = tile
    nt = S // T
    nblk = T // SGU_BLOCK
    alpha = float((2.0 * 4) ** 0.25)
    n_in = 17 + (5 if exch is not None else 0)
    n_out = 4 + (3 if exch is not None else 0)
    slab = pltpu.VMEM((T, GROUP), F32)
    scratch = dict(
        dbuf=pltpu.VMEM((T + HALO_A, GROUP), F32), ebuf=pltpu.VMEM((T + HALO_B, GROUP), F32),
        fbuf=pltpu.VMEM((T + HALO_C, GROUP), F32), sh=pltpu.VMEM((SUBLANES - 1, T + HALO_A - SUBLANES, GROUP), F32),
        wm_s=pltpu.VMEM((SGU_BLOCK, 4 * SGU_BLOCK), BF16), wmt_s=pltpu.VMEM((4 * SGU_BLOCK, SGU_BLOCK), BF16),
        dsp_acc=pltpu.VMEM((SGU_BLOCK, GROUP), F32), pw_acc=pltpu.VMEM((GROUP, GROUP), F32),
        acc_s=pltpu.VMEM((8 * ACC_ROWS, GROUP), F32), acc_w=pltpu.VMEM((24, D_MODEL), F32),
        dmix_s=pltpu.VMEM((T, D_MODEL), F32), vst_s=pltpu.VMEM((nblk, 4 * SGU_BLOCK, GROUP), BF16),
        dq_s=pltpu.VMEM((T, GROUP), BF16), dxt_s=pltpu.VMEM((D_MODEL, T), F32),
        mean_s=slab, t1_s=slab, t2_s=slab, q_s=slab, xv_s=slab, rv_s=slab, v_s=slab, sp_s=slab, a0_s=slab, sg_s=slab,
        xh_s=slab, ra_s=slab, ub_s=slab, dsp_s=slab, m1_s=slab, m2_s=slab, dpool_s=slab, dvd_s=slab, u_s=slab,
        du_s=slab, cw_s=slab)
    names = list(scratch)

    def body(*refs):
        (dy_ref, z_ref, h_ref, aux_ref, wi_ref, caw_ref, cbw_ref, s256_ref, seg_ref, pw_ref, wm_ref, wmt_ref,
         sb_ref, wo_ref, v1024_ref, e4_ref) = refs[1:17]
        dx_ref, dhb_ref, dzb_ref, osm_ref = refs[n_in:n_in + 4]
        k0 = n_in + n_out
        sc = dict(zip(names, refs[k0:k0 + len(names)]))
        dbuf, ebuf, fbuf, sh = sc["dbuf"], sc["ebuf"], sc["fbuf"], sc["sh"]
        wm_s, wmt_s, dsp_acc, pw_acc, acc_s, acc_w = (sc[n] for n in ("wm_s", "wmt_s", "dsp_acc", "pw_acc", "acc_s",
                                                                        "acc_w"))
        dmix_s, vst_s, dq_s = sc["dmix_s"], sc["vst_s"], sc["dq_s"]
        i = pl.program_id(0)
        tile_idx = nt - 1 - i
        if exch is not None:
            p_i, p_o, sm = refs[17:20]
            r_i, r_o, r_sm = refs[n_in + 4:n_in + 7]
            _exchange_comm(i == 0, i == 1, i == nt - 1, refs[0][0] + 1, p_i, p_o, sm, r_i, r_o, r_sm, *refs[k0 + len(names):])

        @pl.when(i == 0)
        def _():
            dbuf[T:T + HALO_A, :] = jnp.zeros((HALO_A, GROUP), F32)
            ebuf[T:T + HALO_B, :] = jnp.zeros((HALO_B, GROUP), F32)
            fbuf[T:T + HALO_C, :] = jnp.zeros((HALO_C, GROUP), F32)
            _sgu_masks(wm_ref, wmt_ref, wm_s, wmt_s)
            osm_ref[...] = jnp.zeros_like(osm_ref)
            dsp_acc[...] = jnp.zeros_like(dsp_acc)
            pw_acc[...] = jnp.zeros_like(pw_acc)
            acc_s[...] = jnp.zeros_like(acc_s)
            acc_w[...] = jnp.zeros_like(acc_w)

        def chunks(rc, fn):
            for c in range(T // rc):
                fn(pl.ds(c * rc, rc))

        def hs(j, rows):
            return h_ref[rows, GROUP * j:GROUP * (j + 1)]

        def acc_add(row, val):
            acc_s[8 * row:8 * row + 8, :] += _rsum8(val)

        def put_dh(j, rows, val):
            acc_add(ROW_BIN + j, val)
            dhb_ref[rows, GROUP * j:GROUP * (j + 1)] = val.astype(BF16)

        def dsilu(v, s):
            return s * (1.0 + v * (1.0 - s))

        def vec(r):
            return s256_ref[r:r + 1, :]

        def ln_bwd(rows):
            dyc = dy_ref[rows, :]
            zc = z_ref[rows, :]
            cen = zc - _rowmean(zc)
            rstd = lax.rsqrt(_rowmean(cen * cen) + LN_EPS)
            xhat = cen * rstd
            acc_w[0:8, :] += _rsum8(dyc * xhat)
            acc_w[8:16, :] += _rsum8(dyc)
            gdy = dyc * v1024_ref[1:2, :]
            dz = rstd * (gdy - _rowmean(gdy) - xhat * _rowmean(gdy * xhat))
            acc_w[16:24, :] += _rsum8(dz)
            dzb_ref[rows, :] = dz.astype(BF16)
            dx_ref[rows, :] = alpha * dz
        chunks(RC_WIDE, ln_bwd)

        segm = seg_ref[...]
        dzb = dzb_ref[...]
        for k in range(N_CHIPS):
            dmix_s[:, GROUP * k:GROUP * (k + 1)] = _dot_nt(dzb, wo_ref[k])
        sc["mean_s"][...] = _segdot(aux_ref[:, 0:GROUP], segm)
        pooled_b = aux_ref[:, 2 * GROUP:3 * GROUP].astype(BF16)
        sc["q_s"][...] = _dot(pooled_b, pw_ref[...])

        def centre(rows):
            cen = aux_ref[rows, 0:GROUP] - sc["mean_s"][rows, :]
            sc["t1_s"][rows, :] = cen * cen
            dv_in = hs(10, rows)
            cen_v = dv_in - _rowmean(dv_in)
            rstd_v = lax.rsqrt(_rowmean(cen_v * cen_v) + LN_EPS)
            xv = cen_v * rstd_v
            sc["xv_s"][rows, :] = xv
            sc["rv_s"][rows, :] = jnp.broadcast_to(rstd_v, xv.shape)
            sc["v_s"][rows, :] = xv * vec(4) + vec(5)
        chunks(RC, centre)

        sc["t2_s"][...] = _segdot(sc["t1_s"][...], segm)
        for n in range(nblk):
            blk = slice(n * SGU_BLOCK, (n + 1) * SGU_BLOCK)
            vst_s[n] = _vstack(sc["v_s"][blk, :])
            sc["sp_s"][blk, :] = _dot(wm_s[...], vst_s[n]) + sb_ref[...]

        def mixers(rows):
            a_val, a_glu, a_z = hs(0, rows), hs(1, rows), hs(2, rows)
            sg = _sig(a_glu)
            sc["a0_s"][rows, :] = a_val * sg
            sc["sg_s"][rows, :] = sg
            rstd_a = lax.rsqrt(sc["t2_s"][rows, :] + LN_EPS)
            xh = (aux_ref[rows, 0:GROUP] - sc["mean_s"][rows, :]) * rstd_a
            a2 = xh * vec(1) + vec(2)
            s2 = _sig(a2)
            sz = _sig(a_z)
            dya = dmix_s[rows, 0:GROUP]
            put_dh(2, rows, dya * (a2 * s2) * dsilu(a_z, sz))
            d_a2 = dya * (a_z * sz) * dsilu(a2, s2)
            acc_add(1, d_a2 * xh)
            acc_add(2, d_a2)
            gd = d_a2 * vec(1)
            sc["t1_s"][rows, :] = gd
            sc["t2_s"][rows, :] = gd * xh
            sc["xh_s"][rows, :] = xh
            sc["ra_s"][rows, :] = rstd_a
            b_b, b_c, b_h, b_z = hs(3, rows), hs(4, rows), hs(5, rows), hs(6, rows)
            cb = aux_ref[rows, GROUP:2 * GROUP]
            sz = _sig(b_z)
            dyb = dmix_s[rows, GROUP:2 * GROUP]
            put_dh(3, rows, dyb * cb * (b_z * sz))
            put_dh(6, rows, dyb * b_b * cb * dsilu(b_z, sz))
            ebuf[rows, :] = dyb * b_b * (b_z * sz)
            sc["ub_s"][rows, :] = b_c * b_h
            c_z = hs(8, rows)
            q = sc["q_s"][rows, :]
            sz = _sig(c_z)
            dyc = dmix_s[rows, 2 * GROUP:3 * GROUP]
            acc_add(3, dyc * q * (c_z * sz))
            put_dh(8, rows, dyc * q * vec(3) * dsilu(c_z, sz))
            dq_s[rows, :] = (dyc * vec(3) * (c_z * sz)).astype(BF16)
            d_u, d_z = hs(9, rows), hs(11, rows)
            sp = sc["sp_s"][rows, :]
            sz = _sig(d_z)
            dyd = dmix_s[rows, 3 * GROUP:4 * GROUP]
            put_dh(9, rows, dyd * sp * (d_z * sz))
            put_dh(11, rows, dyd * d_u * sp * dsilu(d_z, sz))
            sc["dsp_s"][rows, :] = dyd * d_u * (d_z * sz)
        chunks(RC, mixers)

        sc["m1_s"][...] = _segdot(sc["t1_s"][...], segm)
        sc["m2_s"][...] = _segdot(sc["t2_s"][...], segm)
        d_q = dq_s[...]
        pw_acc[...] += _dot_tn(pooled_b, d_q)
        sc["dpool_s"][...] = _dot_nt(d_q, pw_ref[...])
        grp = _lane_group(GROUP)
        for n in range(nblk):
            blk = slice(n * SGU_BLOCK, (n + 1) * SGU_BLOCK)
            dspb = sc["dsp_s"][blk, :]
            dsp_acc[...] += dspb
            dspb16 = dspb.astype(BF16)
            dvst = _dot(wmt_s[...], dspb16)
            dvb = None
            for hh in range(4):
                part = jnp.where(grp == hh, dvst[hh * SGU_BLOCK:(hh + 1) * SGU_BLOCK, :], 0.0)
                dvb = part if dvb is None else dvb + part
            sc["dvd_s"][blk, :] = dvb
            dwc = _dot_nt(dspb16, vst_s[n])
            osm_ref[ROW_WC:ROW_WC + SGU_BLOCK, :] += dwc[:, 0:GROUP]
            osm_ref[ROW_WC + SGU_BLOCK:ROW_WC + 2 * SGU_BLOCK, :] += dwc[:, GROUP:2 * GROUP]

        def ln_sums(rows):
            xh = sc["xh_s"][rows, :]
            d_a1 = sc["ra_s"][rows, :] * (sc["t1_s"][rows, :] - sc["m1_s"][rows, :] - xh * sc["m2_s"][rows, :])
            acc_add(0, d_a1)
            dbuf[rows, :] = d_a1
            pos = tile_idx * T + rows.start + lax.broadcasted_iota(jnp.int32, (RC, GROUP), 0) + 1
            lane = lax.broadcasted_iota(jnp.int32, (RC, GROUP), 1) // HEAD
            win = jnp.where(lane == 0, 2, jnp.where(lane == 1, 4, jnp.where(lane == 2, 8, 16)))
            fbuf[rows, :] = sc["dpool_s"][rows, :] / jnp.minimum(pos, win).astype(F32)
            d_v = sc["dvd_s"][rows, :]
            xv = sc["xv_s"][rows, :]
            acc_add(4, d_v * xv)
            acc_add(5, d_v)
            gd = d_v * vec(4)
            put_dh(10, rows, sc["rv_s"][rows, :] * (gd - _rowmean(gd) - xv * _rowmean(gd * xv)))
        chunks(RC, ln_sums)

        span = T + HALO_A - SUBLANES
        for p in range(1, SUBLANES):
            sh[p - 1, :, :] = dbuf[p:p + span, :]

        for r0 in range(0, T, ROWS):
            uc = sc["ub_s"][r0:r0 + ROWS, :]
            acc = None
            for k in range(KB):
                off = (KB - 1) - k + r0
                w = ebuf[off:off + ROWS, :]
                term = cbw_ref[k:k + 1, :] * w
                acc = term if acc is None else acc + term
                acc_add(ROW_CBW + k, uc * w)
            sc["du_s"][r0:r0 + ROWS, :] = acc
        ebuf[T:T + HALO_B, :] = ebuf[0:HALO_B, :]

        hi_lane = (lax.broadcasted_iota(jnp.int32, (1, 128), 1) // HEAD) == 1
        for r0 in range(0, T, ROWS):
            def win(col, j0, j1):
                s = None
                for j in range(j0, j1):
                    term = fbuf[r0 + j:r0 + j + ROWS, 128 * col:128 * (col + 1)]
                    s = term if s is None else s + term
                return s
            sc["cw_s"][r0:r0 + ROWS, 0:128] = win(0, 0, 2) + jnp.where(hi_lane, win(0, 2, 4), 0.0)
            sc["cw_s"][r0:r0 + ROWS, 128:256] = win(1, 0, 8) + jnp.where(hi_lane, win(1, 8, 16), 0.0)
        fbuf[T:T + HALO_C, :] = fbuf[0:HALO_C, :]

        def rest_bc(rows):
            d_u = sc["du_s"][rows, :]
            put_dh(4, rows, d_u * hs(5, rows))
            put_dh(5, rows, d_u * hs(4, rows))
            put_dh(7, rows, sc["cw_s"][rows, :] - sc["dpool_s"][rows, :])
        chunks(RC, rest_bc)

        dxt_s = sc["dxt_s"]

        def dx_term(k):
            term = _dot_nt(wi_ref[k], dhb_ref[:, COLS * k:COLS * (k + 1)])
            if k == 1:
                dxt_s[...] = term
            else:
                dxt_s[...] += term

        def conv_a(rows):
            a0c = sc["a0_s"][rows, :]
            acc = None
            for k in range(KA):
                off = (KA - 1) - k
                p, q8 = off % SUBLANES, off - off % SUBLANES
                w = dbuf[pl.ds(rows.start + q8, RC), :] if p == 0 else sh[p - 1, pl.ds(rows.start + q8, RC), :]
                term = caw_ref[k:k + 1, :] * w
                acc = term if acc is None else acc + term
                acc_add(ROW_CAW + k, a0c * w)
            sc["u_s"][rows, :] = acc
        n_chunks = T // RC
        after = {(n_chunks * j) // 3: j + 1 for j in range(3)}
        for c in range(n_chunks):
            conv_a(pl.ds(c * RC, RC))
            if c in after:
                dx_term(after[c])
        dbuf[T:T + HALO_A, :] = dbuf[0:HALO_A, :]

        def rest_a(rows):
            d_a0 = sc["u_s"][rows, :]
            sg = sc["sg_s"][rows, :]
            put_dh(0, rows, d_a0 * sg)
            put_dh(1, rows, d_a0 * hs(0, rows) * sg * (1.0 - sg))
        chunks(RC, rest_a)
        dx_term(0)
        dx_ref[...] += dxt_s[...].T

        @pl.when(i == nt - 1)
        def _():
            for row in list(range(6)) + list(range(ROW_CBW, ROW_CBW + KB)) + list(range(ROW_CAW, ROW_CAW + KA)) + list(
                    range(ROW_BIN, ROW_BIN + N_SLICES)):
                osm_ref[row:row + 1, :] = _colsum(acc_s[8 * row:8 * row + 8, :])
            for j, row in enumerate((ROW_LNG, ROW_LNB, ROW_BOUT)):
                cs = _colsum(acc_w[8 * j:8 * j + 8, :])
                for q in range(D_MODEL // GROUP):
                    osm_ref[row + q:row + q + 1, :] = cs[:, GROUP * q:GROUP * (q + 1)]
            r = lax.broadcasted_iota(jnp.int32, (SGU_BLOCK, GROUP), 0) // CHUNK
            c = (lax.broadcasted_iota(jnp.int32, (SGU_BLOCK, GROUP), 1) % SGU_BLOCK) // CHUNK
            for half in range(2):
                rows_ = slice(ROW_WC + half * SGU_BLOCK, ROW_WC + (half + 1) * SGU_BLOCK)
                osm_ref[rows_, :] = jnp.where(c <= r, osm_ref[rows_, :], 0.0)
            sb_t = _segdot(dsp_acc[...], e4_ref[...]).T
            osm_ref[ROW_SB:ROW_SB + 8, 0:SGU_BLOCK] = sb_t[0:8, :]
            for g in range(4):
                osm_ref[ROW_PW:ROW_PW + HEAD, HEAD * g:HEAD * (g + 1)] = (
                    pw_acc[HEAD * g:HEAD * (g + 1), HEAD * g:HEAD * (g + 1)])

    def rows(width):
        return pl.BlockSpec((T, width), lambda i, l: (nt - 1 - i, 0))

    consts = (wi, caw, cbw, s256, seg, pw, wm, wmt, sb, wo, v1024, e4)
    unstacked = (wi, seg, wo, e4)
    in_specs = [rows(D_MODEL), rows(D_MODEL), rows(IN_WIDTH), rows(3 * GROUP)] + [
        _whole(a) if any(a is u for u in unstacked) else _of_layer(a) for a in consts]
    out_specs = [rows(D_MODEL), rows(IN_WIDTH), rows(D_MODEL), pl.BlockSpec((SM_ROWS, GROUP), lambda i, l: (0, 0))]
    out_shape = [jax.ShapeDtypeStruct((S, D_MODEL), F32), jax.ShapeDtypeStruct((S, IN_WIDTH), BF16),
                 jax.ShapeDtypeStruct((S, D_MODEL), BF16), jax.ShapeDtypeStruct((SM_ROWS, GROUP), F32)]
    scratch_shapes = list(scratch.values())
    extra, aliases = (), {}
    if exch is not None:
        extra = tuple(exch)
        r_i, r_o = exch[3], exch[4]
        in_specs += [ANY] * 5
        out_specs += [ANY] * 3
        out_shape += [jax.ShapeDtypeStruct(r_i.shape, r_i.dtype), jax.ShapeDtypeStruct(r_o.shape, r_o.dtype),
                      jax.ShapeDtypeStruct((N_CHIPS, SM_ROWS, GROUP), F32)]
        scratch_shapes += [pltpu.SemaphoreType.DMA((N_EXCH_SEMS,)), pltpu.SemaphoreType.DMA((N_EXCH_SEMS,)),
                           pltpu.SemaphoreType.DMA((2,)), pltpu.VMEM((3, SM_ROWS, GROUP), F32)]
        aliases = {20: 4, 21: 5}
    grid_spec = pltpu.PrefetchScalarGridSpec(num_scalar_prefetch=1, grid=(nt,), in_specs=in_specs,
                                             out_specs=out_specs, scratch_shapes=scratch_shapes)
    return pl.pallas_call(
        body, name="bwd_layer" if exch is None else "bwd_layer_exchange",
        grid_spec=grid_spec, out_shape=out_shape, input_output_aliases=aliases,
        compiler_params=_vmem_params(dimension_semantics=("arbitrary",), **(
            dict(has_side_effects=True, collective_id=COLLECTIVE_ID["bwd_layer_exchange"]) if exch is not None else {})),
    )(larr, dy, z, h, aux, *consts, *extra)


def _dw_swap(cl_arr, xb, dhb, mixb, dzb, p_i, p_o, *, k_steps, last=None):
    S = xb.shape[0]
    tk = S // k_steps
    n_steps = N_CHIPS + k_steps
    hi, ho = p_i.shape[2], p_o.shape[2]
    n_in = 7 + (3 if last is not None else 0)
    n_out = 2 + (3 if last is not None else 0)

    def body(*refs):
        cl_ref, x_ref, dh_ref, mix_ref, dz_ref = refs[0:5]
        pi_ref, po_ref = refs[n_in:n_in + 2]
        own_i, acc_o, snd_i, snd_o, rcv_i, rcv_o, send_sems, recv_sems = refs[n_in + n_out:n_in + n_out + 8]
        j = pl.program_id(0)
        l, me_k = cl_ref[1], cl_ref[2]
        x, y, c = _place()
        mine_o, theirs_o = (pl.ds(pl.multiple_of(cc * ho, ho), ho) for cc in (c, 1 - c))

        def to_sibling(src, dst, sem):
            return pltpu.make_async_remote_copy(src_ref=src, dst_ref=dst, send_sem=send_sems.at[sem],
                                                recv_sem=recv_sems.at[sem], device_id=(x, y, 1 - c), device_id_type=MESH)

        def chunk_of(s):
            return (me_k + 1 + s) % N_CHIPS

        def chunk_copy(s):
            return to_sibling(snd_i.at[s % 2], rcv_i.at[chunk_of(s)], s)

        def out_copy():
            return to_sibling(snd_o, rcv_o, N_CHIPS)

        if last is None:
            @pl.when(j == 0)
            def _():
                _handshake(PEERS_SIBLING)
        else:
            qi_ref, qo_ref, r_sm = refs[n_in + 2:n_in + 5]
            out_sems, in_sems = refs[n_in + n_out + 8:n_in + n_out + 10]
            _exchange_comm(j == 0, j == 1, j == n_steps - 1, None, None, None, refs[7], None, None, r_sm,
                           *refs[n_in + n_out + 10:])
            chips = _other_chips(x, y)

            def onward(r):
                px, py, pk = chips[r]
                return [pltpu.make_async_remote_copy(
                    src_ref=v.at[pk], dst_ref=q.at[r, l], send_sem=out_sems.at[2 * r + n], recv_sem=in_sems.at[2 * r + n],
                    device_id=(px, py, c), device_id_type=MESH) for n, (v, q) in enumerate(((rcv_i, qi_ref), (rcv_o, qo_ref)))]

        @pl.when(j < N_CHIPS)
        def _():
            @pl.when(j >= 2)
            def _():
                chunk_copy(j - 2).wait_send()

            acc = _dot_tn(x_ref[...], dh_ref[...])
            top, bottom = acc[:hi], acc[hi:]
            own_i[j % 2] = jnp.where(c == 0, top, bottom)
            snd_i[j % 2] = jnp.where(c == 0, bottom, top).astype(BF16)
            chunk_copy(j).start()

        @pl.when(j == N_CHIPS)
        def _():
            acc_o[...] = jnp.zeros_like(acc_o)

        @pl.when(j >= N_CHIPS)
        def _():
            acc_o[...] += _dot_tn(mix_ref[...], dz_ref[...]).reshape(N_CHIPS, GROUP, D_MODEL)

        @pl.when((j >= 1) & (j <= N_CHIPS))
        def _():
            chunk_copy(j - 1).wait_recv()
            summed = (own_i[(j - 1) % 2] + rcv_i[chunk_of(j - 1)].astype(F32)).astype(pi_ref.dtype)
            pi_ref[...] = summed
            if last is not None:
                rcv_i[chunk_of(j - 1)] = summed
                for r in range(3):
                    @pl.when(chunk_of(j - 1) == chips[r][2])
                    def _():
                        onward(r)[0].start()

        @pl.when(j == n_steps - 1)
        def _():
            snd_o[...] = acc_o[:, theirs_o, :].astype(BF16)
            out_copy().start()
            for k in (N_CHIPS - 2, N_CHIPS - 1):
                chunk_copy(k).wait_send()
            out_copy().wait_recv()
            summed = (acc_o[:, mine_o, :] + rcv_o[...].astype(F32)).astype(po_ref.dtype)
            po_ref[...] = summed
            if last is not None:
                rcv_o[...] = summed
                for r in range(3):
                    onward(r)[1].start()
            out_copy().wait_send()
            if last is not None:
                for r in range(3):
                    for cp in onward(r):
                        cp.wait()

    def col_block(j, cl):
        return (cl[2] + 1 + jnp.clip(j, 0, N_CHIPS - 1)) % N_CHIPS

    def tok_block(j):
        return jnp.maximum(j - N_CHIPS, 0)

    in_specs = [pl.BlockSpec((S, D_MODEL), lambda j, cl: (0, 0)),
                pl.BlockSpec((S, COLS), lambda j, cl: (0, col_block(j, cl))),
                pl.BlockSpec((tk, D_MODEL), lambda j, cl: (tok_block(j), 0)),
                pl.BlockSpec((tk, D_MODEL), lambda j, cl: (tok_block(j), 0)), ANY, ANY]
    out_specs = [pl.BlockSpec((None, None, hi, COLS), lambda j, cl: (cl[1], col_block(j - 1, cl), 0, 0)),
                 pl.BlockSpec((None, N_CHIPS, ho, D_MODEL), lambda j, cl: (cl[1], 0, 0, 0))]
    out_shape = [jax.ShapeDtypeStruct(p_i.shape, p_i.dtype), jax.ShapeDtypeStruct(p_o.shape, p_o.dtype)]
    scratch = [pltpu.VMEM((2, hi, COLS), F32), pltpu.VMEM((N_CHIPS, GROUP, D_MODEL), F32),
               pltpu.VMEM((2, hi, COLS), BF16), pltpu.VMEM((N_CHIPS, ho, D_MODEL), BF16),
               pltpu.VMEM((N_CHIPS, hi, COLS), BF16), pltpu.VMEM((N_CHIPS, ho, D_MODEL), BF16),
               pltpu.SemaphoreType.DMA((N_CHIPS + 1,)), pltpu.SemaphoreType.DMA((N_CHIPS + 1,))]
    extra, aliases, kind = (), {5: 0, 6: 1}, "dw_swap"
    if last is not None:
        extra, aliases, kind = tuple(last), {5: 0, 6: 1, 8: 2, 9: 3}, "dw_swap_exchange"
        in_specs += [ANY, ANY, ANY]
        out_specs += [ANY, ANY, ANY]
        out_shape += [jax.ShapeDtypeStruct(q.shape, q.dtype) for q in last[1:]]
        out_shape += [jax.ShapeDtypeStruct((N_CHIPS, SM_ROWS, GROUP), F32)]
        scratch += [pltpu.SemaphoreType.DMA((6,)), pltpu.SemaphoreType.DMA((6,)), pltpu.SemaphoreType.DMA((N_EXCH_SEMS,)),
                    pltpu.SemaphoreType.DMA((N_EXCH_SEMS,)), pltpu.SemaphoreType.DMA((2,)),
                    pltpu.VMEM((3, SM_ROWS, GROUP), F32)]
    grid_spec = pltpu.PrefetchScalarGridSpec(
        num_scalar_prefetch=1, grid=(n_steps,), in_specs=in_specs, out_specs=out_specs, scratch_shapes=scratch)
    return pl.pallas_call(
        body, name=kind, grid_spec=grid_spec, out_shape=out_shape, input_output_aliases=aliases,
        compiler_params=_vmem_params(dimension_semantics=("arbitrary",), has_side_effects=True,
                                     collective_id=COLLECTIVE_ID[kind]),
    )(cl_arr, xb, dhb, mixb, dzb, p_i, p_o, *extra)


def _adamw_math(w, g, m, v):
    nm = ADAM_B1 * m + (1.0 - ADAM_B1) * g
    nv = ADAM_B2 * v + (1.0 - ADAM_B2) * (g * g)
    c1 = 1.0 - ADAM_B1 ** ADAM_STEP
    c2 = 1.0 - ADAM_B2 ** ADAM_STEP
    return -ADAM_LR * ((nm / c1) / (jnp.sqrt(nv / c2) + ADAM_EPS) + ADAM_WD * w), nm, nv


def _adamw_small(ws, gs, ms, vs):
    n = len(ws)

    def body(*refs):
        for j in range(n):
            d, nm, nv = _adamw_math(*(refs[k * n + j][...] for k in range(4)))
            refs[4 * n + j][...] = d
            refs[5 * n + j][...] = nm
            refs[6 * n + j][...] = nv

    shapes = [jax.ShapeDtypeStruct(w.shape, F32) for w in ws]
    outs = pl.pallas_call(body, name="adamw_small", out_shape=shapes * 3, compiler_params=_vmem_params())(
        *ws, *gs, *ms, *vs)
    return outs[0:n], outs[n:2 * n], outs[2 * n:3 * n]


def _adamw(w, g, m, v, *, rows_per_step, name, copy_g=False):
    R, C = w.shape
    tr = rows_per_step
    n_steps, slots = R // tr, 3
    n_out = 4 if copy_g else 3

    def body(*refs):
        ins, outs = refs[0:4], refs[4:4 + n_out]
        bufs, sems = refs[4 + n_out:8 + n_out], refs[8 + n_out]
        i = pl.program_id(0)

        def fetch(s):
            return [pltpu.make_async_copy(src.at[pl.ds(pl.multiple_of(s * tr, tr), tr), :], buf.at[s % slots],
                                          sems.at[j, s % slots]) for j, (src, buf) in enumerate(zip(ins, bufs))]

        @pl.when(i == 0)
        def _():
            for s in range(min(2, n_steps)):
                for cp in fetch(s):
                    cp.start()

        @pl.when(i + 2 < n_steps)
        def _():
            for cp in fetch(i + 2):
                cp.start()

        for cp in fetch(i):
            cp.wait()
        w_, g_, m_, v_ = (buf[i % slots] for buf in bufs)
        outs[0][...], outs[1][...], outs[2][...] = _adamw_math(w_, g_, m_, v_)
        if copy_g:
            outs[3][...] = g_

    spec = pl.BlockSpec((tr, C), lambda i: (i, 0))
    return pl.pallas_call(
        body, name=name, grid=(n_steps,),
        in_specs=[ANY] * 4, out_specs=[spec] * n_out,
        out_shape=[jax.ShapeDtypeStruct((R, C), F32)] * n_out,
        scratch_shapes=[pltpu.VMEM((slots, tr, C), F32)] * 4 + [pltpu.SemaphoreType.DMA((4, slots))],
        compiler_params=_vmem_params(dimension_semantics=("arbitrary",)),
    )(w, g, m, v)


def _gather_weights(wi16, wo16, cw):
    L = wi16.shape[0]
    hi_rows, ho_rows = D_MODEL // 2, GROUP // 2
    n_ici = 2 * L + 1
    n_fwd = 2 * L

    def body(wi_ref, wo_ref, cw_ref, *rest):
        wig = rest[0:L]
        wog = rest[L:2 * L]
        cwg = rest[2 * L]
        send_sems, recv_sems, loc_sems, vwi, vwo, vcw = rest[2 * L + 1:]
        x, y, c = _place()
        me_k = 2 * x + y
        sibling = (x, y, 1 - c)
        chips = _other_chips(x, y)

        def half_i(ref, blk):
            return ref.at[blk, pl.ds(c * hi_rows, hi_rows), :]

        def half_o(ref, blk):
            return ref.at[blk, pl.ds(c * ho_rows, ho_rows), :]

        def other_half_i(ref, blk):
            return ref.at[blk, pl.ds((1 - c) * hi_rows, hi_rows), :]

        def other_half_o(ref, blk):
            return ref.at[blk, pl.ds((1 - c) * ho_rows, ho_rows), :]

        stage_in = [pltpu.make_async_copy(wi_ref, vwi, loc_sems.at[0]), pltpu.make_async_copy(wo_ref, vwo, loc_sems.at[1]),
                    pltpu.make_async_copy(cw_ref, vcw, loc_sems.at[2])]
        local = []
        for l in range(L):
            local.append(pltpu.make_async_copy(vwi.at[l], wig[l].at[me_k], loc_sems.at[3 + 2 * l]))
            local.append(pltpu.make_async_copy(vwo.at[l], wog[l].at[me_k], loc_sems.at[3 + 2 * l + 1]))
        local.append(pltpu.make_async_copy(vcw, cwg.at[me_k], loc_sems.at[3 + 2 * L]))
        _handshake(PEERS_COLUMN)
        for cp in stage_in:
            cp.start()

        def remote(src, dst, sem, to):
            return pltpu.make_async_remote_copy(src_ref=src, dst_ref=dst, send_sem=send_sems.at[sem],
                                                recv_sem=recv_sems.at[sem], device_id=to, device_id_type=MESH)

        sends = []
        for r, (px, py, _) in enumerate(chips):
            to = (px, py, c)
            for l in range(L):
                sends.append(remote(half_i(wi_ref, l), half_i(wig[l], me_k), r * n_ici + 2 * l, to))
                sends.append(remote(half_o(wo_ref, l), half_o(wog[l], me_k), r * n_ici + 2 * l + 1, to))
            sends.append(remote(cw_ref, cwg.at[me_k], r * n_ici + 2 * L, to))
        for cp in sends:
            cp.start()
        for cp in stage_in:
            cp.wait()
        for cp in local:
            cp.start()

        base = 3 * n_ici
        fwds = []
        for r, (px, py, pk) in enumerate(chips):
            for l in range(L):
                remote(half_i(wig[l], pk), half_i(wig[l], pk), r * n_ici + 2 * l, sibling).wait_recv()
                f = remote(half_i(wig[l], pk), half_i(wig[l], pk), base + r * n_fwd + 2 * l, sibling)
                f.start()
                fwds.append(f)
                remote(half_o(wog[l], pk), half_o(wog[l], pk), r * n_ici + 2 * l + 1, sibling).wait_recv()
                f = remote(half_o(wog[l], pk), half_o(wog[l], pk), base + r * n_fwd + 2 * l + 1, sibling)
                f.start()
                fwds.append(f)
            remote(cwg.at[pk], cwg.at[pk], r * n_ici + 2 * L, sibling).wait_recv()
        for r, (px, py, pk) in enumerate(chips):
            for l in range(L):
                remote(other_half_i(wig[l], pk), other_half_i(wig[l], pk), base + r * n_fwd + 2 * l, sibling).wait_recv()
                remote(other_half_o(wog[l], pk), other_half_o(wog[l], pk), base + r * n_fwd + 2 * l + 1, sibling).wait_recv()
        for cp in sends + fwds:
            cp.wait_send()
        for cp in local:
            cp.wait()

    n_sem = 3 * n_ici + 3 * n_fwd
    out_shape = ([jax.ShapeDtypeStruct((N_CHIPS, D_MODEL, COLS), BF16)] * L
                 + [jax.ShapeDtypeStruct((N_CHIPS, GROUP, D_MODEL), BF16)] * L
                 + [jax.ShapeDtypeStruct((N_CHIPS,) + cw.shape, F32)])
    outs = pl.pallas_call(
        body, name="gather_weights",
        in_specs=[ANY, ANY, ANY], out_specs=[ANY] * (2 * L + 1), out_shape=out_shape,
        scratch_shapes=[pltpu.SemaphoreType.DMA((n_sem,)), pltpu.SemaphoreType.DMA((n_sem,)),
                        pltpu.SemaphoreType.DMA((2 * L + 4,)), pltpu.VMEM(wi16.shape, BF16), pltpu.VMEM(wo16.shape, BF16),
                        pltpu.VMEM(cw.shape, F32)],
        compiler_params=_vmem_params(has_side_effects=True, collective_id=COLLECTIVE_ID["gather_weights"]),
    )(wi16, wo16, cw)
    return outs[0:L], outs[L:2 * L], outs[2 * L]


def _sum_share(kc_arr, p_i, q_i, p_o, q_o, r_sms, *, nb):
    L = p_i.shape[0]
    n_steps, slots = L * nb, 2

    def body(kc_ref, pi_ref, a0, a1, a2, po_ref, b0, b1, b2, *rest):
        del kc_ref
        sm_refs, (oi_ref, oo_ref, os_ref, vi, vo, vs, loc_sems, send_sems, recv_sems) = rest[:L], rest[L:]
        x, y, c = _place()
        t = pl.program_id(0) * nb + pl.program_id(1)

        def small_copies():
            dst = os_ref.at[:, pl.ds(pl.multiple_of(c * (SM_ROWS // 2), SUBLANES), SM_ROWS // 2), :]
            return (pltpu.make_async_copy(vs, dst, loc_sems.at[2 * n_steps]),
                    pltpu.make_async_remote_copy(src_ref=vs, dst_ref=dst, send_sem=send_sems.at[2 * n_steps],
                                                 recv_sem=recv_sems.at[2 * n_steps], device_id=(x, y, 1 - c),
                                                 device_id_type=MESH))

        def copies(s):
            l, i = s // nb, s % nb
            out = []
            for j, (v, o) in enumerate(((vi, oi_ref), (vo, oo_ref))):
                tr = v.shape[1]
                src, dst = v.at[s % slots], o.at[l, pl.ds((c * nb + i) * tr, tr), :]
                out.append((pltpu.make_async_copy(src, dst, loc_sems.at[2 * s + j]),
                            pltpu.make_async_remote_copy(src_ref=src, dst_ref=dst, send_sem=send_sems.at[2 * s + j],
                                                         recv_sem=recv_sems.at[2 * s + j], device_id=(x, y, 1 - c),
                                                         device_id_type=MESH)))
            return out

        def sent(s):
            for mine, theirs in copies(s):
                mine.wait()
                theirs.wait_send()

        @pl.when(t == 0)
        def _():
            _handshake(PEERS_SIBLING)
            for l in range(L):
                vs[l] = ((sm_refs[l][0] + sm_refs[l][1]) + sm_refs[l][2]) + sm_refs[l][3]
            for cp in small_copies():
                cp.start()

        @pl.when(t >= slots)
        def _():
            sent(t - slots)

        f = lambda ref: ref[...].astype(F32)
        vi[t % slots] = ((f(pi_ref) + f(a0)) + f(a1)) + f(a2)
        vo[t % slots] = ((f(po_ref) + f(b0)) + f(b1)) + f(b2)
        for mine, theirs in copies(t):
            mine.start()
            theirs.start()

        @pl.when(t == n_steps - 1)
        def _():
            for s in range(n_steps - slots, n_steps):
                sent(s)
            for s in range(n_steps):
                for _, theirs in copies(s):
                    theirs.wait_recv()
            mine, theirs = small_copies()
            mine.wait()
            theirs.wait()

    def specs(p):
        tr, cols = p.shape[2] // nb, p.shape[3]
        chunk = pl.BlockSpec((None, None, tr, cols), lambda l, i, kc: (l, kc[0], i, 0))
        got = [pl.BlockSpec((None, None, tr, cols), lambda l, i, kc, _j=j: (_j, l, i, 0)) for j in range(3)]
        return [chunk] + got, pltpu.VMEM((slots, tr, cols), F32)

    (in_i, v_i), (in_o, v_o) = specs(p_i), specs(p_o)
    in_sm = [pl.BlockSpec((N_CHIPS, SM_ROWS // 2, GROUP), lambda l, i, kc: (0, kc[1], 0))] * L
    grid_spec = pltpu.PrefetchScalarGridSpec(
        num_scalar_prefetch=1, grid=(L, nb), in_specs=in_i + in_o + in_sm, out_specs=[ANY, ANY, ANY],
        scratch_shapes=[v_i, v_o, pltpu.VMEM((L, SM_ROWS // 2, GROUP), F32)]
        + [pltpu.SemaphoreType.DMA((2 * n_steps + 1,))] * 3)
    return pl.pallas_call(
        body, name="sum_share", grid_spec=grid_spec,
        out_shape=[jax.ShapeDtypeStruct((L, 2 * p.shape[2], p.shape[3]), F32) for p in (p_i, p_o)]
        + [jax.ShapeDtypeStruct((L, SM_ROWS, GROUP), F32)],
        compiler_params=_vmem_params(dimension_semantics=("arbitrary",) * 2, has_side_effects=True,
                                     collective_id=COLLECTIVE_ID["sum_share"]),
    )(kc_arr, p_i, q_i, q_i, q_i, p_o, q_o, q_o, q_o, *r_sms)


WEIGHTS = ("ln_g", "ln_b", "w_in", "b_in", "conv_a_w", "conv_a_b", "norm_a_g", "norm_a_b", "conv_b_w", "pool_w",
           "pool_scale", "sgu_ln_g", "sgu_ln_b", "sgu_w", "sgu_bias", "w_out", "b_out")


def _pad_rows(a, rows):
    return jnp.pad(a, ((0, rows - a.shape[0]), (0, 0)))


def _indicator_consts():
    seg = jnp.where((jnp.arange(GROUP)[:, None] // HEAD) == (jnp.arange(GROUP)[None, :] // HEAD),
                    1.0 / HEAD, 0.0).astype(BF16)
    e4 = ((jnp.arange(GROUP)[:, None] // HEAD) == jnp.arange(128)[None, :]).astype(BF16)
    return seg, e4


def _layer_consts(p, conv_full):
    L = conv_full.shape[0]
    same_head = jnp.eye(4, dtype=F32)[:, None, :, None] > 0

    def rows_to(a, rows):
        return jnp.pad(a, ((0, 0), (0, rows - a.shape[1]), (0, 0)))

    s256 = jnp.stack([p[n] for n in ("conv_a_b", "norm_a_g", "norm_a_b", "pool_scale", "sgu_ln_g", "sgu_ln_b")], axis=1)
    pw = jnp.where(same_head, p["pool_w"][:, :, :, None, :], 0.0).reshape(L, GROUP, GROUP)
    return dict(
        caw=rows_to(conv_full[:, :KA], 32), cbw=rows_to(conv_full[:, KA:], 8), s256=rows_to(s256, 8),
        pw=pw.astype(BF16),
        wm=jnp.transpose(p["sgu_w"], (0, 2, 1, 3)).reshape(L, SGU_BLOCK, 4 * SGU_BLOCK),
        wmt=jnp.transpose(p["sgu_w"], (0, 1, 3, 2)).reshape(L, 4 * SGU_BLOCK, SGU_BLOCK),
        sb=jnp.repeat(jnp.transpose(p["sgu_bias"], (0, 2, 1)), HEAD, axis=2),
        v1024=rows_to(jnp.stack([p["b_out"], p["ln_g"], p["ln_b"]], axis=1), 8),
        bin=p["b_in"][:, None, :])


def _unpack_small(sm):
    L = sm.shape[0]
    owc = jnp.concatenate([sm[:, ROW_WC:ROW_WC + SGU_BLOCK], sm[:, ROW_WC + SGU_BLOCK:ROW_WC + 2 * SGU_BLOCK]], axis=2)
    return dict(
        conv_a_b=sm[:, 0], norm_a_g=sm[:, 1], norm_a_b=sm[:, 2], pool_scale=sm[:, 3], sgu_ln_g=sm[:, 4],
        sgu_ln_b=sm[:, 5], conv_b_w=sm[:, ROW_CBW:ROW_CBW + KB], conv_a_w=sm[:, ROW_CAW:ROW_CAW + KA],
        pool_w=jnp.transpose(sm[:, ROW_PW:ROW_PW + HEAD].reshape(L, HEAD, 4, HEAD), (0, 2, 1, 3)),
        ln_g=sm[:, ROW_LNG:ROW_LNG + 4].reshape(L, D_MODEL), ln_b=sm[:, ROW_LNB:ROW_LNB + 4].reshape(L, D_MODEL),
        b_out=sm[:, ROW_BOUT:ROW_BOUT + 4].reshape(L, D_MODEL),
        b_in=sm[:, ROW_BIN:ROW_BIN + N_SLICES].reshape(L, IN_WIDTH),
        sgu_w=jnp.transpose(owc.reshape(L, SGU_BLOCK, 4, SGU_BLOCK), (0, 2, 1, 3)),
        sgu_bias=sm[:, ROW_SB:ROW_SB + 4, 0:SGU_BLOCK])


def _step(p, m, v, x, target, *, tile_f, tile_b, k_steps):
    L = p["ln_g"].shape[0]
    xi, yi, ci = _place()
    me_k = 2 * xi + yi
    hi_rows, ho_rows = D_MODEL // 2, GROUP // 2

    cw = jnp.concatenate([p["conv_a_w"], p["conv_b_w"]], axis=1).reshape(-1, 128)
    cw_rows = cw.shape[0]
    cw = _pad_rows(cw, -(-cw_rows // SUBLANES) * SUBLANES)
    wi16 = p["w_in"].astype(BF16)
    wo16 = p["w_out"].astype(BF16)
    wig0, wog0, cwg = _gather_weights(wi16[0:1], wo16[0:1], cw)
    cwg = cwg[:, :cw_rows].reshape(N_CHIPS, L, KA + KB, HEAD)
    conv_full = jnp.transpose(cwg, (1, 2, 0, 3)).reshape(L, KA + KB, GROUP)
    seg, e4 = _indicator_consts()
    k = _layer_consts(p, conv_full)
    layer = [jnp.full((1,), l, jnp.int32) for l in range(L)]

    hcur = x
    saved, wig, wog = [], [wig0[0]], [wog0[0]]
    for l in range(L):
        nxt = (wi16, wo16) if l + 1 < L else None
        outs = _fwd_layer(layer[l], hcur, wig[l], k["bin"], k["caw"], k["cbw"], k["s256"], seg, k["pw"], k["wm"], k["sb"],
                          wog[l], k["v1024"], tile=tile_f, nxt=nxt, target=None if nxt is not None else target)
        y, xb, h, aux, mixb, z = outs[0:6]
        if nxt is not None:
            wig.append(outs[6])
            wog.append(outs[7])
        saved.append((xb, h, aux, mixb, z))
        hcur = y

    dy = hcur
    loss_local = outs[6][0, 0]

    p_i = lax.empty((L, N_CHIPS, hi_rows, COLS), BF16)
    p_o = lax.empty((L, N_CHIPS, ho_rows, D_MODEL), BF16)
    q_i = lax.empty((3, L, hi_rows, COLS), BF16)
    q_o = lax.empty((3, L, ho_rows, D_MODEL), BF16)
    r_sm = [None] * L
    pending = None
    for l in reversed(range(L)):
        xb, h, aux, mixb, z = saved[l]
        exch = None if pending is None else (p_i, p_o, pending, q_i, q_o)
        outs = _bwd_layer(layer[l], dy, z, h, aux, wig[l], k["caw"], k["cbw"], k["s256"], seg, k["pw"], k["wm"],
                          k["wmt"], k["sb"], wog[l], k["v1024"], e4, tile=tile_b, exch=exch)
        dy, dhb, dzb, osm = outs[0:4]
        if l == L - 1:
            osm = osm.at[ROW_LOSS, 0].set(loss_local)
        if exch is not None:
            q_i, q_o, r_sm[l + 1] = outs[4:7]
        cl_arr = jnp.stack([ci, jnp.int32(l), me_k]).astype(jnp.int32)
        if l > 0:
            p_i, p_o = _dw_swap(cl_arr, xb, dhb, mixb, dzb, p_i, p_o, k_steps=k_steps)
        else:
            p_i, p_o, q_i, q_o, r_sm[0] = _dw_swap(cl_arr, xb, dhb, mixb, dzb, p_i, p_o, k_steps=k_steps,
                                                   last=(osm, q_i, q_o))
        pending = osm
    grad_x = dy

    kc_arr = jnp.stack([me_k, ci]).astype(jnp.int32)
    g_i, g_o, summed = _sum_share(kc_arr, p_i, q_i, p_o, q_o, r_sm, nb=2)
    loss = summed[L - 1, ROW_LOSS, 0]
    grads = _unpack_small(summed)
    for n in ("conv_a_w", "conv_b_w"):
        grads[n] = lax.dynamic_slice_in_dim(grads[n], me_k * HEAD, HEAD, axis=2)

    grads["w_in"] = g_i
    grads["w_out"] = g_o

    delta, new_m, new_v = {}, {}, {}
    for n, tr in (("w_in", 512), ("w_out", 256)):
        shp = p[n].shape
        args = [a.reshape(shp[0] * shp[1], shp[2]) for a in (p[n], grads[n], m[n], v[n])]
        outs = _adamw(*args, rows_per_step=tr, name="adamw_" + n, copy_g=True)
        delta[n], new_m[n], new_v[n], grads[n] = (a.reshape(shp) for a in outs)
    small = [n for n in WEIGHTS if n not in ("w_in", "w_out")]
    flat = [[a[n].reshape(-1, a[n].shape[-1]) for n in small] for a in (p, grads, m, v)]
    outs = _adamw_small(*flat)
    for j, n in enumerate(small):
        delta[n], new_m[n], new_v[n] = (o[j].reshape(p[n].shape) for o in outs)

    return (loss, grad_x[None], *[grads[n] for n in WEIGHTS], *[delta[n] for n in WEIGHTS],
            *[new_m[n] for n in WEIGHTS], *[new_v[n] for n in WEIGHTS])


def kernel(x, ln_g, ln_b, w_in, b_in, conv_a_w, conv_a_b, norm_a_g, norm_a_b, conv_b_w, pool_w, pool_scale, sgu_ln_g, sgu_ln_b, sgu_w, sgu_bias, w_out, b_out, loss_target, m_ln_g, m_ln_b, m_w_in, m_b_in, m_conv_a_w, m_conv_a_b, m_norm_a_g, m_norm_a_b, m_conv_b_w, m_pool_w, m_pool_scale, m_sgu_ln_g, m_sgu_ln_b, m_sgu_w, m_sgu_bias, m_w_out, m_b_out, v_ln_g, v_ln_b, v_w_in, v_b_in, v_conv_a_w, v_conv_a_b, v_norm_a_g, v_norm_a_b, v_conv_b_w, v_pool_w, v_pool_scale, v_sgu_ln_g, v_sgu_ln_b, v_sgu_w, v_sgu_bias, v_w_out, v_b_out):
    p = dict(ln_g=ln_g, ln_b=ln_b, w_in=w_in, b_in=b_in, conv_a_w=conv_a_w, conv_a_b=conv_a_b, norm_a_g=norm_a_g,
             norm_a_b=norm_a_b, conv_b_w=conv_b_w, pool_w=pool_w, pool_scale=pool_scale, sgu_ln_g=sgu_ln_g,
             sgu_ln_b=sgu_ln_b, sgu_w=sgu_w, sgu_bias=sgu_bias, w_out=w_out, b_out=b_out)
    m = dict(ln_g=m_ln_g, ln_b=m_ln_b, w_in=m_w_in, b_in=m_b_in, conv_a_w=m_conv_a_w, conv_a_b=m_conv_a_b,
             norm_a_g=m_norm_a_g, norm_a_b=m_norm_a_b, conv_b_w=m_conv_b_w, pool_w=m_pool_w, pool_scale=m_pool_scale,
             sgu_ln_g=m_sgu_ln_g, sgu_ln_b=m_sgu_ln_b, sgu_w=m_sgu_w, sgu_bias=m_sgu_bias, w_out=m_w_out, b_out=m_b_out)
    v = dict(ln_g=v_ln_g, ln_b=v_ln_b, w_in=v_w_in, b_in=v_b_in, conv_a_w=v_conv_a_w, conv_a_b=v_conv_a_b,
             norm_a_g=v_norm_a_g, norm_a_b=v_norm_a_b, conv_b_w=v_conv_b_w, pool_w=v_pool_w, pool_scale=v_pool_scale,
             sgu_ln_g=v_sgu_ln_g, sgu_ln_b=v_sgu_ln_b, sgu_w=v_sgu_w, sgu_bias=v_sgu_bias, w_out=v_w_out, b_out=v_b_out)
    return _step(p, m, v, x[0], loss_target[0], tile_f=256, tile_b=256, k_steps=4)
```

```python
import jax
import jax.numpy as jnp
from jax import lax
from jax.experimental import pallas as pl
from jax.experimental.pallas import tpu as pltpu

F32 = jnp.float32
BF16 = jnp.bfloat16
MESH = pl.DeviceIdType.MESH

D_MODEL = 1024
GROUP = 256
HEAD = 64
N_SLICES = 12
IN_WIDTH = N_SLICES * GROUP
N_CHIPS = 4
COLS = IN_WIDTH // N_CHIPS
KA = 31
KB = 3
SUBLANES = 8
HALO_A, HALO_B, HALO_C = 32, 8, 16
N_GATHER_SEMS = 12
N_EXCH_SEMS = 10
SGU_BLOCK = 128
CHUNK = 64
LN_EPS = 1e-5
ROWS = 64
V7X_VMEM_BYTES = 64 * 1024 * 1024
VMEM_LIMIT = V7X_VMEM_BYTES - 8 * 1024 * 1024

ADAM_LR, ADAM_B1, ADAM_B2, ADAM_EPS, ADAM_WD, ADAM_STEP = 0.001, 0.9, 0.999, 1e-08, 0.01, 10


ANY = pl.BlockSpec(memory_space=pl.ANY)


def _vmem_params(**kw):
    return pltpu.CompilerParams(vmem_limit_bytes=VMEM_LIMIT, **kw)


def _whole(a):
    return pl.BlockSpec(a.shape, lambda i, l, _n=a.ndim: (0,) * _n)


def _of_layer(a):
    return pl.BlockSpec((None,) + a.shape[1:], lambda i, l, _n=a.ndim: (l[0],) + (0,) * (_n - 1))


def _place():
    return lax.axis_index("x"), lax.axis_index("y"), lax.axis_index("c")


def _other_chips(x, y):
    return [(1 - x, y, 2 * (1 - x) + y), (x, 1 - y, 2 * x + (1 - y)), (1 - x, 1 - y, 2 * (1 - x) + (1 - y))]


PEERS_SIBLING, PEERS_COLUMN = "sibling", "sibling and the same core of the other chips"
COLLECTIVE_ID = dict(sum_share=0, dw_swap=1, gather_weights=2, fwd_layer_gather=3, bwd_layer_exchange=4,
                     dw_swap_exchange=5)


def _handshake(peers):
    x, y, c = _place()
    ids = [(x, y, 1 - c)]
    if peers == PEERS_COLUMN:
        ids += [(px, py, c) for px, py, _ in _other_chips(x, y)]
    barrier = pltpu.get_barrier_semaphore()
    for to in ids:
        pl.semaphore_signal(barrier, inc=1, device_id=to, device_id_type=MESH)
    pl.semaphore_wait(barrier, len(ids))


def _sig(v):
    return 0.5 * jnp.tanh(0.5 * v) + 0.5


def _dot(a, b):
    return jnp.dot(a, b, preferred_element_type=F32)


def _dot_nt(a, b):
    return lax.dot_general(a, b, (((1,), (1,)), ((), ())), preferred_element_type=F32)


def _dot_tn(a, b):
    return lax.dot_general(a, b, (((0,), (0,)), ((), ())), preferred_element_type=F32)


def _segdot(v, m):
    hi = v.astype(BF16)
    lo = (v - hi.astype(F32)).astype(BF16)
    return _dot(hi, m) + _dot(lo, m)


def _colsum(v):
    return jnp.sum(v, axis=0, keepdims=True)


def _rowmean(v):
    return jnp.mean(v, axis=-1, keepdims=True)


def _lane_group(n):
    return lax.broadcasted_iota(jnp.int32, (1, n), 1) // HEAD


def _pool_cnt(tile, t_rows):
    pos = tile * t_rows + lax.broadcasted_iota(jnp.int32, (t_rows, GROUP), 0) + 1
    grp = lax.broadcasted_iota(jnp.int32, (t_rows, GROUP), 1) // HEAD
    win = jnp.where(grp == 0, 2, jnp.where(grp == 1, 4, jnp.where(grp == 2, 8, 16)))
    return jnp.minimum(pos, win).astype(F32)


def _sgu_masks(wm_ref, wmt_ref, wm_s, wmt_s):
    r = lax.broadcasted_iota(jnp.int32, (SGU_BLOCK, 4 * SGU_BLOCK), 0) // CHUNK
    c = (lax.broadcasted_iota(jnp.int32, (SGU_BLOCK, 4 * SGU_BLOCK), 1) % SGU_BLOCK) // CHUNK
    wm_s[...] = jnp.where(c <= r, wm_ref[...], 0.0).astype(BF16)
    if wmt_ref is not None:
        rt = (lax.broadcasted_iota(jnp.int32, (4 * SGU_BLOCK, SGU_BLOCK), 0) % SGU_BLOCK) // CHUNK
        ct = lax.broadcasted_iota(jnp.int32, (4 * SGU_BLOCK, SGU_BLOCK), 1) // CHUNK
        wmt_s[...] = jnp.where(rt <= ct, wmt_ref[...], 0.0).astype(BF16)


def _vstack(v_blk):
    grp = _lane_group(GROUP)
    return jnp.concatenate([jnp.where(grp == h, v_blk, 0.0) for h in range(4)], axis=0).astype(BF16)


def _gather_next(step, nt, nwi, nwo, gwi, gwo, send_sems, recv_sems, loc_sems, vwi, vwo):
    x, y, c = _place()
    me_k = 2 * x + y
    sibling = (x, y, 1 - c)
    chips = _other_chips(x, y)
    hi, ho = D_MODEL // 2, GROUP // 2
    fwd_sems = N_GATHER_SEMS // 2

    def rc(src, dst, sem, to):
        return pltpu.make_async_remote_copy(src_ref=src, dst_ref=dst, send_sem=send_sems.at[sem],
                                            recv_sem=recv_sems.at[sem], device_id=to, device_id_type=MESH)

    def blk(ref, k, n, cc):
        return ref.at[k, pl.ds(cc * n, n), :]

    def ici(r):
        px, py, _ = chips[r]
        to = (px, py, c)
        return [rc(nwi.at[pl.ds(c * hi, hi), :], blk(gwi, me_k, hi, c), 2 * r, to),
                rc(nwo.at[pl.ds(c * ho, ho), :], blk(gwo, me_k, ho, c), 2 * r + 1, to)]

    def landed(r, cc, base):
        pk = chips[r][2]
        return [rc(blk(gwi, pk, hi, cc), blk(gwi, pk, hi, cc), base + 2 * r, sibling),
                rc(blk(gwo, pk, ho, cc), blk(gwo, pk, ho, cc), base + 2 * r + 1, sibling)]

    def stage_in():
        return [pltpu.make_async_copy(nwi, vwi, loc_sems.at[0]), pltpu.make_async_copy(nwo, vwo, loc_sems.at[1])]

    def local():
        return [pltpu.make_async_copy(vwi, gwi.at[me_k], loc_sems.at[2]),
                pltpu.make_async_copy(vwo, gwo.at[me_k], loc_sems.at[3])]

    @pl.when(step == 0)
    def _():
        _handshake(PEERS_COLUMN)
        for cp in stage_in():
            cp.start()
        for r in range(3):
            for cp in ici(r):
                cp.start()

    @pl.when(step == 1)
    def _():
        for cp in stage_in():
            cp.wait()
        for cp in local():
            cp.start()

    @pl.when(step == (3 * nt) // 4)
    def _():
        for r in range(3):
            for got, fwd in zip(landed(r, c, 0), landed(r, c, fwd_sems)):
                got.wait_recv()
                fwd.start()

    @pl.when(step == nt - 1)
    def _():
        for r in range(3):
            for got in landed(r, 1 - c, fwd_sems):
                got.wait_recv()
        for r in range(3):
            for cp in ici(r) + landed(r, c, fwd_sems):
                cp.wait_send()
        for cp in local():
            cp.wait()


def _fwd_layer(larr, x, wi, bin_, caw, cbw, s256, seg, pw, wm, sb, wo, v1024, *, tile, nxt=None, target=None):
    assert nxt is None or target is None
    S = x.shape[0]
    T = tile
    nt = S // T
    alpha = float((2.0 * 4) ** 0.25)
    n_in = 13 + (2 if nxt is not None else 0) + (1 if target is not None else 0)
    n_out = 6 + (2 if nxt is not None else 0) + (1 if target is not None else 0)

    def body(*refs):
        l_ref = refs[0]
        (x_ref, wi_ref, bin_ref, caw_ref, cbw_ref, s256_ref, seg_ref, pw_ref, wm_ref, sb_ref, wo_ref,
         v1024_ref) = refs[1:13]
        y_ref, xb_ref, h_ref, aux_ref, mix_ref, z_ref = refs[n_in:n_in + 6]
        abuf, bbuf, cbuf, wm_s, shf = refs[n_in + n_out:n_in + n_out + 5]
        i = pl.program_id(0)
        if nxt is not None:
            _gather_next(i, nt, refs[13].at[l_ref[0] + 1], refs[14].at[l_ref[0] + 1], refs[n_in + 6], refs[n_in + 7],
                         *refs[n_in + n_out + 5:])

        @pl.when(i == 0)
        def _():
            abuf[0:HALO_A, :] = jnp.zeros((HALO_A, GROUP), F32)
            bbuf[0:HALO_B, :] = jnp.zeros((HALO_B, GROUP), F32)
            cbuf[0:HALO_C, :] = jnp.zeros((HALO_C, GROUP), F32)
            _sgu_masks(wm_ref, None, wm_s, None)

        x = x_ref[...]
        xb = x.astype(BF16)
        xb_ref[...] = xb
        for k in range(N_CHIPS):
            h_ref[:, COLS * k:COLS * (k + 1)] = _dot(xb, wi_ref[k]) + bin_ref[:, COLS * k:COLS * (k + 1)]

        def hs(j):
            return h_ref[:, GROUP * j:GROUP * (j + 1)]

        abuf[HALO_A:HALO_A + T, :] = hs(0) * _sig(hs(1))
        span = T + HALO_A - SUBLANES
        for p in range(1, SUBLANES):
            shf[p - 1, :, :] = abuf[p:p + span, :]
        for r0 in range(0, T, ROWS):
            acc = None
            for k in range(KA):
                off = HALO_A - (KA - 1) + k
                p, q8 = off % SUBLANES, off - off % SUBLANES
                win = abuf[r0 + q8:r0 + q8 + ROWS, :] if p == 0 else shf[p - 1, r0 + q8:r0 + q8 + ROWS, :]
                term = caw_ref[k:k + 1, :] * win
                acc = term if acc is None else acc + term
            aux_ref[r0:r0 + ROWS, 0:GROUP] = acc + s256_ref[0:1, :]
        abuf[0:HALO_A, :] = abuf[T:T + HALO_A, :]
        a1 = aux_ref[:, 0:GROUP]
        segm = seg_ref[...]
        cen = a1 - _segdot(a1, segm)
        var = _segdot(cen * cen, segm)
        a2 = cen * lax.rsqrt(var + LN_EPS) * s256_ref[1:2, :] + s256_ref[2:3, :]
        az = hs(2)
        mix_ref[:, 0:GROUP] = (a2 * _sig(a2) * (az * _sig(az))).astype(BF16)

        bbuf[HALO_B:HALO_B + T, :] = hs(4) * hs(5)
        for r0 in range(0, T, ROWS):
            acc = None
            for k in range(KB):
                off = HALO_B - (KB - 1) + k + r0
                term = cbw_ref[k:k + 1, :] * bbuf[off:off + ROWS, :]
                acc = term if acc is None else acc + term
            aux_ref[r0:r0 + ROWS, GROUP:2 * GROUP] = acc
        bbuf[0:HALO_B, :] = bbuf[T:T + HALO_B, :]
        bz = hs(6)
        mix_ref[:, GROUP:2 * GROUP] = (hs(3) * aux_ref[:, GROUP:2 * GROUP] * (bz * _sig(bz))).astype(BF16)

        ch = hs(7)
        cbuf[HALO_C:HALO_C + T, :] = ch
        hi_lane = (lax.broadcasted_iota(jnp.int32, (1, 128), 1) // HEAD) == 1
        for r0 in range(0, T, ROWS):
            def win(col, j0, j1):
                s = None
                for j in range(j0, j1):
                    off = HALO_C - j + r0
                    term = cbuf[off:off + ROWS, 128 * col:128 * (col + 1)]
                    s = term if s is None else s + term
                return s
            w0 = win(0, 0, 2) + jnp.where(hi_lane, win(0, 2, 4), 0.0)
            w1 = win(1, 0, 8) + jnp.where(hi_lane, win(1, 8, 16), 0.0)
            aux_ref[r0:r0 + ROWS, 2 * GROUP:2 * GROUP + 128] = w0
            aux_ref[r0:r0 + ROWS, 2 * GROUP + 128:3 * GROUP] = w1
        cbuf[0:HALO_C, :] = cbuf[T:T + HALO_C, :]
        pooled = aux_ref[:, 2 * GROUP:3 * GROUP] / _pool_cnt(i, T) - ch
        aux_ref[:, 2 * GROUP:3 * GROUP] = pooled
        q = _dot(pooled.astype(BF16), pw_ref[...])
        cz = hs(8)
        mix_ref[:, 2 * GROUP:3 * GROUP] = (q * s256_ref[3:4, :] * (cz * _sig(cz))).astype(BF16)

        dv = hs(10)
        cen = dv - _rowmean(dv)
        var = _rowmean(cen * cen)
        v = cen * lax.rsqrt(var + LN_EPS) * s256_ref[4:5, :] + s256_ref[5:6, :]
        sps = []
        for n in range(T // SGU_BLOCK):
            vb = v[n * SGU_BLOCK:(n + 1) * SGU_BLOCK, :]
            sps.append(_dot(wm_s[...], _vstack(vb)) + sb_ref[...])
        sp = jnp.concatenate(sps, axis=0)
        dz = hs(11)
        mix_ref[:, 3 * GROUP:4 * GROUP] = (hs(9) * sp * (dz * _sig(dz))).astype(BF16)

        out = v1024_ref[0:1, :]
        for k in range(N_CHIPS):
            out = out + _dot(mix_ref[:, GROUP * k:GROUP * (k + 1)], wo_ref[k])
        z = alpha * x + out
        z_ref[...] = z
        cen = z - _rowmean(z)
        var = _rowmean(cen * cen)
        y = cen * lax.rsqrt(var + LN_EPS) * v1024_ref[1:2, :] + v1024_ref[2:3, :]
        if target is None:
            y_ref[...] = y
        else:
            t_ref, loss_ref = refs[13], refs[n_in + 6]

            @pl.when(i == 0)
            def _():
                loss_ref[...] = jnp.zeros_like(loss_ref)
            err = y - t_ref[...]
            y_ref[...] = err * (1.0 / D_MODEL)
            loss_ref[...] += jnp.sum(_colsum(err * err), axis=1, keepdims=True) * (0.5 / D_MODEL)

    def rows(width):
        return pl.BlockSpec((T, width), lambda i, l: (i, 0))

    consts = (wi, bin_, caw, cbw, s256, seg, pw, wm, sb, wo, v1024)
    in_specs = [rows(D_MODEL)] + [_whole(a) if a is wi or a is seg or a is wo else _of_layer(a) for a in consts]
    out_specs = [rows(D_MODEL), rows(D_MODEL), rows(IN_WIDTH), rows(3 * GROUP), rows(D_MODEL), rows(D_MODEL)]
    out_shape = [jax.ShapeDtypeStruct((S, D_MODEL), F32), jax.ShapeDtypeStruct((S, D_MODEL), BF16),
                 jax.ShapeDtypeStruct((S, IN_WIDTH), F32), jax.ShapeDtypeStruct((S, 3 * GROUP), F32),
                 jax.ShapeDtypeStruct((S, D_MODEL), BF16), jax.ShapeDtypeStruct((S, D_MODEL), F32)]
    scratch = [pltpu.VMEM((T + HALO_A, GROUP), F32), pltpu.VMEM((T + HALO_B, GROUP), F32),
               pltpu.VMEM((T + HALO_C, GROUP), F32), pltpu.VMEM((SGU_BLOCK, 4 * SGU_BLOCK), BF16),
               pltpu.VMEM((SUBLANES - 1, T + HALO_A - SUBLANES, GROUP), F32)]
    extra = ()
    if nxt is not None:
        extra = tuple(nxt)
        in_specs += [ANY, ANY]
        out_specs += [ANY, ANY]
        out_shape += [jax.ShapeDtypeStruct((N_CHIPS, D_MODEL, COLS), BF16),
                      jax.ShapeDtypeStruct((N_CHIPS, GROUP, D_MODEL), BF16)]
        scratch += [pltpu.SemaphoreType.DMA((N_GATHER_SEMS,)), pltpu.SemaphoreType.DMA((N_GATHER_SEMS,)),
                    pltpu.SemaphoreType.DMA((4,)), pltpu.VMEM((D_MODEL, COLS), BF16), pltpu.VMEM((GROUP, D_MODEL), BF16)]
    if target is not None:
        extra = (target,)
        in_specs += [rows(D_MODEL)]
        out_specs += [pl.BlockSpec((8, 128), lambda i, l: (0, 0))]
        out_shape += [jax.ShapeDtypeStruct((8, 128), F32)]
    grid_spec = pltpu.PrefetchScalarGridSpec(num_scalar_prefetch=1, grid=(nt,), in_specs=in_specs,
                                             out_specs=out_specs, scratch_shapes=scratch)
    return pl.pallas_call(
        body, name=("fwd_layer_loss" if target is not None else "fwd_layer") if nxt is None else "fwd_layer_gather",
        grid_spec=grid_spec, out_shape=out_shape,
        compiler_params=_vmem_params(dimension_semantics=("arbitrary",), **(
            dict(has_side_effects=True, collective_id=COLLECTIVE_ID["fwd_layer_gather"]) if nxt is not None else {})),
    )(larr, x, *consts, *extra)


ROW_CBW = 8
ROW_CAW = 16
ROW_LOSS = 7
ROW_PW = 48
ROW_LNG = 112
ROW_LNB = 116
ROW_BOUT = 120
ROW_BIN = 124
ROW_WC = 136
ROW_SB = 392
SM_ROWS = 400


def _exchange_comm(start, mid, finish, l, p_i, p_o, sm, r_i, r_o, r_sm, send_sems, recv_sems, loc_sems, vm):
    x, y, c = _place()
    me_k = 2 * x + y
    chips = _other_chips(x, y)

    def rc(src, dst, sem, to):
        return pltpu.make_async_remote_copy(src_ref=src, dst_ref=dst, send_sem=send_sems.at[sem],
                                            recv_sem=recv_sems.at[sem], device_id=to, device_id_type=MESH)

    def big(r):
        px, py, pk = chips[r]
        to = (px, py, c)
        return [rc(p_i.at[l, pk], r_i.at[r, l], 2 * r, to), rc(p_o.at[l, pk], r_o.at[r, l], 2 * r + 1, to)]

    def stage():
        return pltpu.make_async_copy(sm, vm.at[0], loc_sems.at[0])

    def to_sibling():
        return rc(sm, vm.at[1], N_EXCH_SEMS - 4, (x, y, 1 - c))

    half = pl.ds(pl.multiple_of(c * (SM_ROWS // 2), SUBLANES), SM_ROWS // 2)

    def chip_sum(r):
        px, py, pk = chips[r]
        return rc(vm.at[2, half], r_sm.at[me_k, half], N_EXCH_SEMS - 3 + r, (px, py, c))

    def keep():
        return pltpu.make_async_copy(vm.at[2, half], r_sm.at[me_k, half], loc_sems.at[1])

    with_big, with_small = p_i is not None, sm is not None

    @pl.when(start)
    def _():
        _handshake(PEERS_COLUMN)
        if with_small:
            stage().start()
            to_sibling().start()
        if with_big:
            for r in range(3):
                for cp in big(r):
                    cp.start()

    if with_small:
        @pl.when(mid)
        def _():
            stage().wait()
            to_sibling().wait_recv()
            vm[2] = vm[0] + vm[1]
            keep().start()
            for r in range(3):
                chip_sum(r).start()

    @pl.when(finish)
    def _():
        if with_big:
            for r in range(3):
                for cp in big(r):
                    cp.wait()
        if with_small:
            to_sibling().wait_send()
            for r in range(3):
                chip_sum(r).wait()
            keep().wait()


RC = 32
RC_WIDE = 16
ACC_ROWS = 136


def _rsum8(v):
    r = v[0:8]
    for j in range(1, v.shape[0] // 8):
        r = r + v[8 * j:8 * j + 8]
    return r


def _bwd_layer(larr, dy, z, h, aux, wi, caw, cbw, s256, seg, pw, wm, wmt, sb, wo, v1024, e4, *, tile, exch=None):
    S = dy.shape[0]
    T = tile
    nt = S // T
    nblk = T // SGU_BLOCK
    alpha = float((2.0 * 4) ** 0.25)
    n_in = 17 + (5 if exch is not None else 0)
    n_out = 4 + (3 if exch is not None else 0)
    slab = pltpu.VMEM((T, GROUP), F32)
    scratch = dict(
        dbuf=pltpu.VMEM((T + HALO_A, GROUP), F32), ebuf=pltpu.VMEM((T + HALO_B, GROUP), F32),
        fbuf=pltpu.VMEM((T + HALO_C, GROUP), F32), sh=pltpu.VMEM((SUBLANES - 1, T + HALO_A - SUBLANES, GROUP), F32),
        wm_s=pltpu.VMEM((SGU_BLOCK, 4 * SGU_BLOCK), BF16), wmt_s=pltpu.VMEM((4 * SGU_BLOCK, SGU_BLOCK), BF16),
        dsp_acc=pltpu.VMEM((SGU_BLOCK, GROUP), F32), pw_acc=pltpu.VMEM((GROUP, GROUP), F32),
        acc_s=pltpu.VMEM((8 * ACC_ROWS, GROUP), F32), acc_w=pltpu.VMEM((24, D_MODEL), F32),
        dmix_s=pltpu.VMEM((T, D_MODEL), F32), vst_s=pltpu.VMEM((nblk, 4 * SGU_BLOCK, GROUP), BF16),
        dq_s=pltpu.VMEM((T, GROUP), BF16), dxt_s=pltpu.VMEM((D_MODEL, T), F32),
        mean_s=slab, t1_s=slab, t2_s=slab, q_s=slab, xv_s=slab, rv_s=slab, v_s=slab, sp_s=slab, a0_s=slab, sg_s=slab,
        xh_s=slab, ra_s=slab, ub_s=slab, dsp_s=slab, m1_s=slab, m2_s=slab, dpool_s=slab, dvd_s=slab, u_s=slab,
        du_s=slab, cw_s=slab)
    names = list(scratch)

    def body(*refs):
        (dy_ref, z_ref, h_ref, aux_ref, wi_ref, caw_ref, cbw_ref, s256_ref, seg_ref, pw_ref, wm_ref, wmt_ref,
         sb_ref, wo_ref, v1024_ref, e4_ref) = refs[1:17]
        dx_ref, dhb_ref, dzb_ref, osm_ref = refs[n_in:n_in + 4]
        k0 = n_in + n_out
        sc = dict(zip(names, refs[k0:k0 + len(names)]))
        dbuf, ebuf, fbuf, sh = sc["dbuf"], sc["ebuf"], sc["fbuf"], sc["sh"]
        wm_s, wmt_s, dsp_acc, pw_acc, acc_s, acc_w = (sc[n] for n in ("wm_s", "wmt_s", "dsp_acc", "pw_acc", "acc_s",
                                                                        "acc_w"))
        dmix_s, vst_s, dq_s = sc["dmix_s"], sc["vst_s"], sc["dq_s"]
        i = pl.program_id(0)
        tile_idx = nt - 1 - i
        if exch is not None:
            p_i, p_o, sm = refs[17:20]
            r_i, r_o, r_sm = refs[n_in + 4:n_in + 7]
            _exchange_comm(i == 0, i == 1, i == nt - 1, refs[0][0] + 1, p_i, p_o, sm, r_i, r_o, r_sm, *refs[k0 + len(names):])

        @pl.when(i == 0)
        def _():
            dbuf[T:T + HALO_A, :] = jnp.zeros((HALO_A, GROUP), F32)
            ebuf[T:T + HALO_B, :] = jnp.zeros((HALO_B, GROUP), F32)
            fbuf[T:T + HALO_C, :] = jnp.zeros((HALO_C, GROUP), F32)
            _sgu_masks(wm_ref, wmt_ref, wm_s, wmt_s)
            osm_ref[...] = jnp.zeros_like(osm_ref)
            dsp_acc[...] = jnp.zeros_like(dsp_acc)
            pw_acc[...] = jnp.zeros_like(pw_acc)
            acc_s[...] = jnp.zeros_like(acc_s)
            acc_w[...] = jnp.zeros_like(acc_w)

        def chunks(rc, fn):
            for c in range(T // rc):
                fn(pl.ds(c * rc, rc))

        def hs(j, rows):
            return h_ref[rows, GROUP * j:GROUP * (j + 1)]

        def acc_add(row, val):
            acc_s[8 * row:8 * row + 8, :] += _rsum8(val)

        def put_dh(j, rows, val):
            acc_add(ROW_BIN + j, val)
            dhb_ref[rows, GROUP * j:GROUP * (j + 1)] = val.astype(BF16)

        def dsilu(v, s):
            return s * (1.0 + v * (1.0 - s))

        def vec(r):
            return s256_ref[r:r + 1, :]

        def ln_bwd(rows):
            dyc = dy_ref[rows, :]
            zc = z_ref[rows, :]
            cen = zc - _rowmean(zc)
            rstd = lax.rsqrt(_rowmean(cen * cen) + LN_EPS)
            xhat = cen * rstd
            acc_w[0:8, :] += _rsum8(dyc * xhat)
            acc_w[8:16, :] += _rsum8(dyc)
            gdy = dyc * v1024_ref[1:2, :]
            dz = rstd * (gdy - _rowmean(gdy) - xhat * _rowmean(gdy * xhat))
            acc_w[16:24, :] += _rsum8(dz)
            dzb_ref[rows, :] = dz.astype(BF16)
            dx_ref[rows, :] = alpha * dz
        chunks(RC_WIDE, ln_bwd)

        segm = seg_ref[...]
        dzb = dzb_ref[...]
        for k in range(N_CHIPS):
            dmix_s[:, GROUP * k:GROUP * (k + 1)] = _dot_nt(dzb, wo_ref[k])
        sc["mean_s"][...] = _segdot(aux_ref[:, 0:GROUP], segm)
        pooled_b = aux_ref[:, 2 * GROUP:3 * GROUP].astype(BF16)
        sc["q_s"][...] = _dot(pooled_b, pw_ref[...])

        def centre(rows):
            cen = aux_ref[rows, 0:GROUP] - sc["mean_s"][rows, :]
            sc["t1_s"][rows, :] = cen * cen
            dv_in = hs(10, rows)
            cen_v = dv_in - _rowmean(dv_in)
            rstd_v = lax.rsqrt(_rowmean(cen_v * cen_v) + LN_EPS)
            xv = cen_v * rstd_v
            sc["xv_s"][rows, :] = xv
            sc["rv_s"][rows, :] = jnp.broadcast_to(rstd_v, xv.shape)
            sc["v_s"][rows, :] = xv * vec(4) + vec(5)
        chunks(RC, centre)

        sc["t2_s"][...] = _segdot(sc["t1_s"][...], segm)
        for n in range(nblk):
            blk = slice(n * SGU_BLOCK, (n + 1) * SGU_BLOCK)
            vst_s[n] = _vstack(sc["v_s"][blk, :])
            sc["sp_s"][blk, :] = _dot(wm_s[...], vst_s[n]) + sb_ref[...]

        def mixers(rows):
            a_val, a_glu, a_z = hs(0, rows), hs(1, rows), hs(2, rows)
            sg = _sig(a_glu)
            sc["a0_s"][rows, :] = a_val * sg
            sc["sg_s"][rows, :] = sg
            rstd_a = lax.rsqrt(sc["t2_s"][rows, :] + LN_EPS)
            xh = (aux_ref[rows, 0:GROUP] - sc["mean_s"][rows, :]) * rstd_a
            a2 = xh * vec(1) + vec(2)
            s2 = _sig(a2)
            sz = _sig(a_z)
            dya = dmix_s[rows, 0:GROUP]
            put_dh(2, rows, dya * (a2 * s2) * dsilu(a_z, sz))
            d_a2 = dya * (a_z * sz) * dsilu(a2, s2)
            acc_add(1, d_a2 * xh)
            acc_add(2, d_a2)
            gd = d_a2 * vec(1)
            sc["t1_s"][rows, :] = gd
            sc["t2_s"][rows, :] = gd * xh
            sc["xh_s"][rows, :] = xh
            sc["ra_s"][rows, :] = rstd_a
            b_b, b_c, b_h, b_z = hs(3, rows), hs(4, rows), hs(5, rows), hs(6, rows)
            cb = aux_ref[rows, GROUP:2 * GROUP]
            sz = _sig(b_z)
            dyb = dmix_s[rows, GROUP:2 * GROUP]
            put_dh(3, rows, dyb * cb * (b_z * sz))
            put_dh(6, rows, dyb * b_b * cb * dsilu(b_z, sz))
            ebuf[rows, :] = dyb * b_b * (b_z * sz)
            sc["ub_s"][rows, :] = b_c * b_h
            c_z = hs(8, rows)
            q = sc["q_s"][rows, :]
            sz = _sig(c_z)
            dyc = dmix_s[rows, 2 * GROUP:3 * GROUP]
            acc_add(3, dyc * q * (c_z * sz))
            put_dh(8, rows, dyc * q * vec(3) * dsilu(c_z, sz))
            dq_s[rows, :] = (dyc * vec(3) * (c_z * sz)).astype(BF16)
            d_u, d_z = hs(9, rows), hs(11, rows)
            sp = sc["sp_s"][rows, :]
            sz = _sig(d_z)
            dyd = dmix_s[rows, 3 * GROUP:4 * GROUP]
            put_dh(9, rows, dyd * sp * (d_z * sz))
            put_dh(11, rows, dyd * d_u * sp * dsilu(d_z, sz))
            sc["dsp_s"][rows, :] = dyd * d_u * (d_z * sz)
        chunks(RC, mixers)

        sc["m1_s"][...] = _segdot(sc["t1_s"][...], segm)
        sc["m2_s"][...] = _segdot(sc["t2_s"][...], segm)
        d_q = dq_s[...]
        pw_acc[...] += _dot_tn(pooled_b, d_q)
        sc["dpool_s"][...] = _dot_nt(d_q, pw_ref[...])
        grp = _lane_group(GROUP)
        for n in range(nblk):
            blk = slice(n * SGU_BLOCK, (n + 1) * SGU_BLOCK)
            dspb = sc["dsp_s"][blk, :]
            dsp_acc[...] += dspb
            dspb16 = dspb.astype(BF16)
            dvst = _dot(wmt_s[...], dspb16)
            dvb = None
            for hh in range(4):
                part = jnp.where(grp == hh, dvst[hh * SGU_BLOCK:(hh + 1) * SGU_BLOCK, :], 0.0)
                dvb = part if dvb is None else dvb + part
            sc["dvd_s"][blk, :] = dvb
            dwc = _dot_nt(dspb16, vst_s[n])
            osm_ref[ROW_WC:ROW_WC + SGU_BLOCK, :] += dwc[:, 0:GROUP]
            osm_ref[ROW_WC + SGU_BLOCK:ROW_WC + 2 * SGU_BLOCK, :] += dwc[:, GROUP:2 * GROUP]

        def ln_sums(rows):
            xh = sc["xh_s"][rows, :]
            d_a1 = sc["ra_s"][rows, :] * (sc["t1_s"][rows, :] - sc["m1_s"][rows, :] - xh * sc["m2_s"][rows, :])
            acc_add(0, d_a1)
            dbuf[rows, :] = d_a1
            pos = tile_idx * T + rows.start + lax.broadcasted_iota(jnp.int32, (RC, GROUP), 0) + 1
            lane = lax.broadcasted_iota(jnp.int32, (RC, GROUP), 1) // HEAD
            win = jnp.where(lane == 0, 2, jnp.where(lane == 1, 4, jnp.where(lane == 2, 8, 16)))
            fbuf[rows, :] = sc["dpool_s"][rows, :] / jnp.minimum(pos, win).astype(F32)
            d_v = sc["dvd_s"][rows, :]
            xv = sc["xv_s"][rows, :]
            acc_add(4, d_v * xv)
            acc_add(5, d_v)
            gd = d_v * vec(4)
            put_dh(10, rows, sc["rv_s"][rows, :] * (gd - _rowmean(gd) - xv * _rowmean(gd * xv)))
        chunks(RC, ln_sums)

        span = T + HALO_A - SUBLANES
        for p in range(1, SUBLANES):
            sh[p - 1, :, :] = dbuf[p:p + span, :]

        for r0 in range(0, T, ROWS):
            uc = sc["ub_s"][r0:r0 + ROWS, :]
            acc = None
            for k in range(KB):
                off = (KB - 1) - k + r0
                w = ebuf[off:off + ROWS, :]
                term = cbw_ref[k:k + 1, :] * w
                acc = term if acc is None else acc + term
                acc_add(ROW_CBW + k, uc * w)
            sc["du_s"][r0:r0 + ROWS, :] = acc
        ebuf[T:T + HALO_B, :] = ebuf[0:HALO_B, :]

        hi_lane = (lax.broadcasted_iota(jnp.int32, (1, 128), 1) // HEAD) == 1
        for r0 in range(0, T, ROWS):
            def win(col, j0, j1):
                s = None
                for j in range(j0, j1):
                    term = fbuf[r0 + j:r0 + j + ROWS, 128 * col:128 * (col + 1)]
                    s = term if s is None else s + term
                return s
            sc["cw_s"][r0:r0 + ROWS, 0:128] = win(0, 0, 2) + jnp.where(hi_lane, win(0, 2, 4), 0.0)
            sc["cw_s"][r0:r0 + ROWS, 128:256] = win(1, 0, 8) + jnp.where(hi_lane, win(1, 8, 16), 0.0)
        fbuf[T:T + HALO_C, :] = fbuf[0:HALO_C, :]

        def rest_bc(rows):
            d_u = sc["du_s"][rows, :]
            put_dh(4, rows, d_u * hs(5, rows))
            put_dh(5, rows, d_u * hs(4, rows))
            put_dh(7, rows, sc["cw_s"][rows, :] - sc["dpool_s"][rows, :])
        chunks(RC, rest_bc)

        dxt_s = sc["dxt_s"]

        def dx_term(k):
            term = _dot_nt(wi_ref[k], dhb_ref[:, COLS * k:COLS * (k + 1)])
            if k == 1:
                dxt_s[...] = term
            else:
                dxt_s[...] += term

        def conv_a(rows):
            a0c = sc["a0_s"][rows, :]
            acc = None
            for k in range(KA):
                off = (KA - 1) - k
                p, q8 = off % SUBLANES, off - off % SUBLANES
                w = dbuf[pl.ds(rows.start + q8, RC), :] if p == 0 else sh[p - 1, pl.ds(rows.start + q8, RC), :]
                term = caw_ref[k:k + 1, :] * w
                acc = term if acc is None else acc + term
                acc_add(ROW_CAW + k, a0c * w)
            sc["u_s"][rows, :] = acc
        n_chunks = T // RC
        after = {(n_chunks * j) // 3: j + 1 for j in range(3)}
        for c in range(n_chunks):
            conv_a(pl.ds(c * RC, RC))
            if c in after:
                dx_term(after[c])
        dbuf[T:T + HALO_A, :] = dbuf[0:HALO_A, :]

        def rest_a(rows):
            d_a0 = sc["u_s"][rows, :]
            sg = sc["sg_s"][rows, :]
            put_dh(0, rows, d_a0 * sg)
            put_dh(1, rows, d_a0 * hs(0, rows) * sg * (1.0 - sg))
        chunks(RC, rest_a)
        dx_term(0)
        dx_ref[...] += dxt_s[...].T

        @pl.when(i == nt - 1)
        def _():
            for row in list(range(6)) + list(range(ROW_CBW, ROW_CBW + KB)) + list(range(ROW_CAW, ROW_CAW + KA)) + list(
                    range(ROW_BIN, ROW_BIN + N_SLICES)):
                osm_ref[row:row + 1, :] = _colsum(acc_s[8 * row:8 * row + 8, :])
            for j, row in enumerate((ROW_LNG, ROW_LNB, ROW_BOUT)):
                cs = _colsum(acc_w[8 * j:8 * j + 8, :])
                for q in range(D_MODEL // GROUP):
                    osm_ref[row + q:row + q + 1, :] = cs[:, GROUP * q:GROUP * (q + 1)]
            r = lax.broadcasted_iota(jnp.int32, (SGU_BLOCK, GROUP), 0) // CHUNK
            c = (lax.broadcasted_iota(jnp.int32, (SGU_BLOCK, GROUP), 1) % SGU_BLOCK) // CHUNK
            for half in range(2):
                rows_ = slice(ROW_WC + half * SGU_BLOCK, ROW_WC + (half + 1) * SGU_BLOCK)
                osm_ref[rows_, :] = jnp.where(c <= r, osm_ref[rows_, :], 0.0)
            sb_t = _segdot(dsp_acc[...], e4_ref[...]).T
            osm_ref[ROW_SB:ROW_SB + 8, 0:SGU_BLOCK] = sb_t[0:8, :]
            for g in range(4):
                osm_ref[ROW_PW:ROW_PW + HEAD, HEAD * g:HEAD * (g + 1)] = (
                    pw_acc[HEAD * g:HEAD * (g + 1), HEAD * g:HEAD * (g + 1)])

    def rows(width):
        return pl.BlockSpec((T, width), lambda i, l: (nt - 1 - i, 0))

    consts = (wi, caw, cbw, s256, seg, pw, wm, wmt, sb, wo, v1024, e4)
    unstacked = (wi, seg, wo, e4)
    in_specs = [rows(D_MODEL), rows(D_MODEL), rows(IN_WIDTH), rows(3 * GROUP)] + [
        _whole(a) if any(a is u for u in unstacked) else _of_layer(a) for a in consts]
    out_specs = [rows(D_MODEL), rows(IN_WIDTH), rows(D_MODEL), pl.BlockSpec((SM_ROWS, GROUP), lambda i, l: (0, 0))]
    out_shape = [jax.ShapeDtypeStruct((S, D_MODEL), F32), jax.ShapeDtypeStruct((S, IN_WIDTH), BF16),
                 jax.ShapeDtypeStruct((S, D_MODEL), BF16), jax.ShapeDtypeStruct((SM_ROWS, GROUP), F32)]
    scratch_shapes = list(scratch.values())
    extra, aliases = (), {}
    if exch is not None:
        extra = tuple(exch)
        r_i, r_o = exch[3], exch[4]
        in_specs += [ANY] * 5
        out_specs += [ANY] * 3
        out_shape += [jax.ShapeDtypeStruct(r_i.shape, r_i.dtype), jax.ShapeDtypeStruct(r_o.shape, r_o.dtype),
                      jax.ShapeDtypeStruct((N_CHIPS, SM_ROWS, GROUP), F32)]
        scratch_shapes += [pltpu.SemaphoreType.DMA((N_EXCH_SEMS,)), pltpu.SemaphoreType.DMA((N_EXCH_SEMS,)),
                           pltpu.SemaphoreType.DMA((2,)), pltpu.VMEM((3, SM_ROWS, GROUP), F32)]
        aliases = {20: 4, 21: 5}
    grid_spec = pltpu.PrefetchScalarGridSpec(num_scalar_prefetch=1, grid=(nt,), in_specs=in_specs,
                                             out_specs=out_specs, scratch_shapes=scratch_shapes)
    return pl.pallas_call(
        body, name="bwd_layer" if exch is None else "bwd_layer_exchange",
        grid_spec=grid_spec, out_shape=out_shape, input_output_aliases=aliases,
        compiler_params=_vmem_params(dimension_semantics=("arbitrary",), **(
            dict(has_side_effects=True, collective_id=COLLECTIVE_ID["bwd_layer_exchange"]) if exch is not None else {})),
    )(larr, dy, z, h, aux, *consts, *extra)


def _dw_swap(cl_arr, xb, dhb, mixb, dzb, p_i, p_o, *, k_steps, last=None):
    S = xb.shape[0]
    tk = S // k_steps
    n_steps = N_CHIPS + k_steps
    hi, ho = p_i.shape[2], p_o.shape[2]
    n_in = 7 + (3 if last is not None else 0)
    n_out = 2 + (3 if last is not None else 0)

    def body(*refs):
        cl_ref, x_ref, dh_ref, mix_ref, dz_ref = refs[0:5]
        pi_ref, po_ref = refs[n_in:n_in + 2]
        own_i, acc_o, snd_i, snd_o, rcv_i, rcv_o, send_sems, recv_sems = refs[n_in + n_out:n_in + n_out + 8]
        j = pl.program_id(0)
        l, me_k = cl_ref[1], cl_ref[2]
        x, y, c = _place()
        mine_o, theirs_o = (pl.ds(pl.multiple_of(cc * ho, ho), ho) for cc in (c, 1 - c))

        def to_sibling(src, dst, sem):
            return pltpu.make_async_remote_copy(src_ref=src, dst_ref=dst, send_sem=send_sems.at[sem],
                                                recv_sem=recv_sems.at[sem], device_id=(x, y, 1 - c), device_id_type=MESH)

        def chunk_of(s):
            return (me_k + 1 + s) % N_CHIPS

        def chunk_copy(s):
            return to_sibling(snd_i.at[s % 2], rcv_i.at[chunk_of(s)], s)

        def out_copy():
            return to_sibling(snd_o, rcv_o, N_CHIPS)

        if last is None:
            @pl.when(j == 0)
            def _():
                _handshake(PEERS_SIBLING)
        else:
            qi_ref, qo_ref, r_sm = refs[n_in + 2:n_in + 5]
            out_sems, in_sems = refs[n_in + n_out + 8:n_in + n_out + 10]
            _exchange_comm(j == 0, j == 1, j == n_steps - 1, None, None, None, refs[7], None, None, r_sm,
                           *refs[n_in + n_out + 10:])
            chips = _other_chips(x, y)

            def onward(r):
                px, py, pk = chips[r]
                return [pltpu.make_async_remote_copy(
                    src_ref=v.at[pk], dst_ref=q.at[r, l], send_sem=out_sems.at[2 * r + n], recv_sem=in_sems.at[2 * r + n],
                    device_id=(px, py, c), device_id_type=MESH) for n, (v, q) in enumerate(((rcv_i, qi_ref), (rcv_o, qo_ref)))]

        @pl.when(j < N_CHIPS)
        def _():
            @pl.when(j >= 2)
            def _():
                chunk_copy(j - 2).wait_send()

            acc = _dot_tn(x_ref[...], dh_ref[...])
            top, bottom = acc[:hi], acc[hi:]
            own_i[j % 2] = jnp.where(c == 0, top, bottom)
            snd_i[j % 2] = jnp.where(c == 0, bottom, top).astype(BF16)
            chunk_copy(j).start()

        @pl.when(j == N_CHIPS)
        def _():
            acc_o[...] = jnp.zeros_like(acc_o)

        @pl.when(j >= N_CHIPS)
        def _():
            acc_o[...] += _dot_tn(mix_ref[...], dz_ref[...]).reshape(N_CHIPS, GROUP, D_MODEL)

        @pl.when((j >= 1) & (j <= N_CHIPS))
        def _():
            chunk_copy(j - 1).wait_recv()
            summed = (own_i[(j - 1) % 2] + rcv_i[chunk_of(j - 1)].astype(F32)).astype(pi_ref.dtype)
            pi_ref[...] = summed
            if last is not None:
                rcv_i[chunk_of(j - 1)] = summed
                for r in range(3):
                    @pl.when(chunk_of(j - 1) == chips[r][2])
                    def _():
                        onward(r)[0].start()

        @pl.when(j == n_steps - 1)
        def _():
            snd_o[...] = acc_o[:, theirs_o, :].astype(BF16)
            out_copy().start()
            for k in (N_CHIPS - 2, N_CHIPS - 1):
                chunk_copy(k).wait_send()
            out_copy().wait_recv()
            summed = (acc_o[:, mine_o, :] + rcv_o[...].astype(F32)).astype(po_ref.dtype)
            po_ref[...] = summed
            if last is not None:
                rcv_o[...] = summed
                for r in range(3):
                    onward(r)[1].start()
            out_copy().wait_send()
            if last is not None:
                for r in range(3):
                    for cp in onward(r):
                        cp.wait()

    def col_block(j, cl):
        return (cl[2] + 1 + jnp.clip(j, 0, N_CHIPS - 1)) % N_CHIPS

    def tok_block(j):
        return jnp.maximum(j - N_CHIPS, 0)

    in_specs = [pl.BlockSpec((S, D_MODEL), lambda j, cl: (0, 0)),
                pl.BlockSpec((S, COLS), lambda j, cl: (0, col_block(j, cl))),
                pl.BlockSpec((tk, D_MODEL), lambda j, cl: (tok_block(j), 0)),
                pl.BlockSpec((tk, D_MODEL), lambda j, cl: (tok_block(j), 0)), ANY, ANY]
    out_specs = [pl.BlockSpec((None, None, hi, COLS), lambda j, cl: (cl[1], col_block(j - 1, cl), 0, 0)),
                 pl.BlockSpec((None, N_CHIPS, ho, D_MODEL), lambda j, cl: (cl[1], 0, 0, 0))]
    out_shape = [jax.ShapeDtypeStruct(p_i.shape, p_i.dtype), jax.ShapeDtypeStruct(p_o.shape, p_o.dtype)]
    scratch = [pltpu.VMEM((2, hi, COLS), F32), pltpu.VMEM((N_CHIPS, GROUP, D_MODEL), F32),
               pltpu.VMEM((2, hi, COLS), BF16), pltpu.VMEM((N_CHIPS, ho, D_MODEL), BF16),
               pltpu.VMEM((N_CHIPS, hi, COLS), BF16), pltpu.VMEM((N_CHIPS, ho, D_MODEL), BF16),
               pltpu.SemaphoreType.DMA((N_CHIPS + 1,)), pltpu.SemaphoreType.DMA((N_CHIPS + 1,))]
    extra, aliases, kind = (), {5: 0, 6: 1}, "dw_swap"
    if last is not None:
        extra, aliases, kind = tuple(last), {5: 0, 6: 1, 8: 2, 9: 3}, "dw_swap_exchange"
        in_specs += [ANY, ANY, ANY]
        out_specs += [ANY, ANY, ANY]
        out_shape += [jax.ShapeDtypeStruct(q.shape, q.dtype) for q in last[1:]]
        out_shape += [jax.ShapeDtypeStruct((N_CHIPS, SM_ROWS, GROUP), F32)]
        scratch += [pltpu.SemaphoreType.DMA((6,)), pltpu.SemaphoreType.DMA((6,)), pltpu.SemaphoreType.DMA((N_EXCH_SEMS,)),
                    pltpu.SemaphoreType.DMA((N_EXCH_SEMS,)), pltpu.SemaphoreType.DMA((2,)),
                    pltpu.VMEM((3, SM_ROWS, GROUP), F32)]
    grid_spec = pltpu.PrefetchScalarGridSpec(
        num_scalar_prefetch=1, grid=(n_steps,), in_specs=in_specs, out_specs=out_specs, scratch_shapes=scratch)
    return pl.pallas_call(
        body, name=kind, grid_spec=grid_spec, out_shape=out_shape, input_output_aliases=aliases,
        compiler_params=_vmem_params(dimension_semantics=("arbitrary",), has_side_effects=True,
                                     collective_id=COLLECTIVE_ID[kind]),
    )(cl_arr, xb, dhb, mixb, dzb, p_i, p_o, *extra)


def _adamw_math(w, g, m, v):
    nm = ADAM_B1 * m + (1.0 - ADAM_B1) * g
    nv = ADAM_B2 * v + (1.0 - ADAM_B2) * (g * g)
    c1 = 1.0 - ADAM_B1 ** ADAM_STEP
    c2 = 1.0 - ADAM_B2 ** ADAM_STEP
    return -ADAM_LR * ((nm / c1) / (jnp.sqrt(nv / c2) + ADAM_EPS) + ADAM_WD * w), nm, nv


def _adamw_small(ws, gs, ms, vs):
    n = len(ws)

    def body(*refs):
        for j in range(n):
            d, nm, nv = _adamw_math(*(refs[k * n + j][...] for k in range(4)))
            refs[4 * n + j][...] = d
            refs[5 * n + j][...] = nm
            refs[6 * n + j][...] = nv

    shapes = [jax.ShapeDtypeStruct(w.shape, F32) for w in ws]
    outs = pl.pallas_call(body, name="adamw_small", out_shape=shapes * 3, compiler_params=_vmem_params())(
        *ws, *gs, *ms, *vs)
    return outs[0:n], outs[n:2 * n], outs[2 * n:3 * n]


def _adamw(groups, *, name):
    G, slots = len(groups), 3
    arrays = [a for grp in groups for a in grp[:4]]
    trs = [grp[4] for grp in groups for _ in range(4)]
    n_steps = arrays[0].shape[0] // trs[0]
    assert all(a.shape[0] == n_steps * tr for a, tr in zip(arrays, trs))

    def body(*refs):
        ins, outs = refs[0:4 * G], refs[4 * G:8 * G]
        bufs, sems = refs[8 * G:12 * G], refs[12 * G]
        i = pl.program_id(0)

        def fetch(s):
            return [pltpu.make_async_copy(src.at[pl.ds(pl.multiple_of(s * tr, tr), tr), :], buf.at[s % slots],
                                          sems.at[j, s % slots]) for j, (src, buf, tr) in enumerate(zip(ins, bufs, trs))]

        @pl.when(i == 0)
        def _():
            for s in range(min(2, n_steps)):
                for cp in fetch(s):
                    cp.start()

        @pl.when(i + 2 < n_steps)
        def _():
            for cp in fetch(i + 2):
                cp.start()

        for cp in fetch(i):
            cp.wait()
        for k in range(G):
            w_, g_, m_, v_ = (buf[i % slots] for buf in bufs[4 * k:4 * k + 4])
            o = outs[4 * k:4 * k + 4]
            o[0][...], o[1][...], o[2][...] = _adamw_math(w_, g_, m_, v_)
            o[3][...] = g_

    return pl.pallas_call(
        body, name=name, grid=(n_steps,),
        in_specs=[ANY] * (4 * G), out_specs=[pl.BlockSpec((tr, a.shape[1]), lambda i: (i, 0)) for a, tr in zip(arrays, trs)],
        out_shape=[jax.ShapeDtypeStruct(a.shape, F32) for a in arrays],
        scratch_shapes=[pltpu.VMEM((slots, tr, a.shape[1]), F32) for a, tr in zip(arrays, trs)]
        + [pltpu.SemaphoreType.DMA((4 * G, slots))],
        compiler_params=_vmem_params(dimension_semantics=("arbitrary",)),
    )(*arrays)


def _gather_weights(wi16, wo16, cw):
    L = wi16.shape[0]
    hi_rows, ho_rows = D_MODEL // 2, GROUP // 2
    n_ici = 2 * L + 1
    n_fwd = 2 * L

    def body(wi_ref, wo_ref, cw_ref, *rest):
        wig = rest[0:L]
        wog = rest[L:2 * L]
        cwg = rest[2 * L]
        send_sems, recv_sems, loc_sems, vwi, vwo, vcw = rest[2 * L + 1:]
        x, y, c = _place()
        me_k = 2 * x + y
        sibling = (x, y, 1 - c)
        chips = _other_chips(x, y)

        def half_i(ref, blk):
            return ref.at[blk, pl.ds(c * hi_rows, hi_rows), :]

        def half_o(ref, blk):
            return ref.at[blk, pl.ds(c * ho_rows, ho_rows), :]

        def other_half_i(ref, blk):
            return ref.at[blk, pl.ds((1 - c) * hi_rows, hi_rows), :]

        def other_half_o(ref, blk):
            return ref.at[blk, pl.ds((1 - c) * ho_rows, ho_rows), :]

        stage_in = [pltpu.make_async_copy(wi_ref, vwi, loc_sems.at[0]), pltpu.make_async_copy(wo_ref, vwo, loc_sems.at[1]),
                    pltpu.make_async_copy(cw_ref, vcw, loc_sems.at[2])]
        local = []
        for l in range(L):
            local.append(pltpu.make_async_copy(vwi.at[l], wig[l].at[me_k], loc_sems.at[3 + 2 * l]))
            local.append(pltpu.make_async_copy(vwo.at[l], wog[l].at[me_k], loc_sems.at[3 + 2 * l + 1]))
        local.append(pltpu.make_async_copy(vcw, cwg.at[me_k], loc_sems.at[3 + 2 * L]))
        _handshake(PEERS_COLUMN)
        for cp in stage_in:
            cp.start()

        def remote(src, dst, sem, to):
            return pltpu.make_async_remote_copy(src_ref=src, dst_ref=dst, send_sem=send_sems.at[sem],
                                                recv_sem=recv_sems.at[sem], device_id=to, device_id_type=MESH)

        sends = []
        for r, (px, py, _) in enumerate(chips):
            to = (px, py, c)
            for l in range(L):
                sends.append(remote(half_i(wi_ref, l), half_i(wig[l], me_k), r * n_ici + 2 * l, to))
                sends.append(remote(half_o(wo_ref, l), half_o(wog[l], me_k), r * n_ici + 2 * l + 1, to))
            sends.append(remote(cw_ref, cwg.at[me_k], r * n_ici + 2 * L, to))
        for cp in sends:
            cp.start()
        for cp in stage_in:
            cp.wait()
        for cp in local:
            cp.start()

        base = 3 * n_ici
        fwds = []
        for r, (px, py, pk) in enumerate(chips):
            for l in range(L):
                remote(half_i(wig[l], pk), half_i(wig[l], pk), r * n_ici + 2 * l, sibling).wait_recv()
                f = remote(half_i(wig[l], pk), half_i(wig[l], pk), base + r * n_fwd + 2 * l, sibling)
                f.start()
                fwds.append(f)
                remote(half_o(wog[l], pk), half_o(wog[l], pk), r * n_ici + 2 * l + 1, sibling).wait_recv()
                f = remote(half_o(wog[l], pk), half_o(wog[l], pk), base + r * n_fwd + 2 * l + 1, sibling)
                f.start()
                fwds.append(f)
            remote(cwg.at[pk], cwg.at[pk], r * n_ici + 2 * L, sibling).wait_recv()
        for r, (px, py, pk) in enumerate(chips):
            for l in range(L):
                remote(other_half_i(wig[l], pk), other_half_i(wig[l], pk), base + r * n_fwd + 2 * l, sibling).wait_recv()
                remote(other_half_o(wog[l], pk), other_half_o(wog[l], pk), base + r * n_fwd + 2 * l + 1, sibling).wait_recv()
        for cp in sends + fwds:
            cp.wait_send()
        for cp in local:
            cp.wait()

    n_sem = 3 * n_ici + 3 * n_fwd
    out_shape = ([jax.ShapeDtypeStruct((N_CHIPS, D_MODEL, COLS), BF16)] * L
                 + [jax.ShapeDtypeStruct((N_CHIPS, GROUP, D_MODEL), BF16)] * L
                 + [jax.ShapeDtypeStruct((N_CHIPS,) + cw.shape, F32)])
    outs = pl.pallas_call(
        body, name="gather_weights",
        in_specs=[ANY, ANY, ANY], out_specs=[ANY] * (2 * L + 1), out_shape=out_shape,
        scratch_shapes=[pltpu.SemaphoreType.DMA((n_sem,)), pltpu.SemaphoreType.DMA((n_sem,)),
                        pltpu.SemaphoreType.DMA((2 * L + 4,)), pltpu.VMEM(wi16.shape, BF16), pltpu.VMEM(wo16.shape, BF16),
                        pltpu.VMEM(cw.shape, F32)],
        compiler_params=_vmem_params(has_side_effects=True, collective_id=COLLECTIVE_ID["gather_weights"]),
    )(wi16, wo16, cw)
    return outs[0:L], outs[L:2 * L], outs[2 * L]


def _sum_share(kc_arr, p_i, q_i, p_o, q_o, r_sms, *, nb):
    L = p_i.shape[0]
    n_steps, slots = L * nb, 2

    def body(kc_ref, pi_ref, a0, a1, a2, po_ref, b0, b1, b2, *rest):
        del kc_ref
        sm_refs, (oi_ref, oo_ref, os_ref, vi, vo, vs, loc_sems, send_sems, recv_sems) = rest[:L], rest[L:]
        x, y, c = _place()
        t = pl.program_id(0) * nb + pl.program_id(1)

        def small_copies():
            dst = os_ref.at[:, pl.ds(pl.multiple_of(c * (SM_ROWS // 2), SUBLANES), SM_ROWS // 2), :]
            return (pltpu.make_async_copy(vs, dst, loc_sems.at[2 * n_steps]),
                    pltpu.make_async_remote_copy(src_ref=vs, dst_ref=dst, send_sem=send_sems.at[2 * n_steps],
                                                 recv_sem=recv_sems.at[2 * n_steps], device_id=(x, y, 1 - c),
                                                 device_id_type=MESH))

        def copies(s):
            l, i = s // nb, s % nb
            out = []
            for j, (v, o) in enumerate(((vi, oi_ref), (vo, oo_ref))):
                tr = v.shape[1]
                src, dst = v.at[s % slots], o.at[l, pl.ds((c * nb + i) * tr, tr), :]
                out.append((pltpu.make_async_copy(src, dst, loc_sems.at[2 * s + j]),
                            pltpu.make_async_remote_copy(src_ref=src, dst_ref=dst, send_sem=send_sems.at[2 * s + j],
                                                         recv_sem=recv_sems.at[2 * s + j], device_id=(x, y, 1 - c),
                                                         device_id_type=MESH)))
            return out

        def sent(s):
            for mine, theirs in copies(s):
                mine.wait()
                theirs.wait_send()

        @pl.when(t == 0)
        def _():
            _handshake(PEERS_SIBLING)
            for l in range(L):
                vs[l] = ((sm_refs[l][0] + sm_refs[l][1]) + sm_refs[l][2]) + sm_refs[l][3]
            for cp in small_copies():
                cp.start()

        @pl.when(t >= slots)
        def _():
            sent(t - slots)

        f = lambda ref: ref[...].astype(F32)
        vi[t % slots] = ((f(pi_ref) + f(a0)) + f(a1)) + f(a2)
        vo[t % slots] = ((f(po_ref) + f(b0)) + f(b1)) + f(b2)
        for mine, theirs in copies(t):
            mine.start()
            theirs.start()

        @pl.when(t == n_steps - 1)
        def _():
            for s in range(n_steps - slots, n_steps):
                sent(s)
            for s in range(n_steps):
                for _, theirs in copies(s):
                    theirs.wait_recv()
            mine, theirs = small_copies()
            mine.wait()
            theirs.wait()

    def specs(p):
        tr, cols = p.shape[2] // nb, p.shape[3]
        chunk = pl.BlockSpec((None, None, tr, cols), lambda l, i, kc: (l, kc[0], i, 0))
        got = [pl.BlockSpec((None, None, tr, cols), lambda l, i, kc, _j=j: (_j, l, i, 0)) for j in range(3)]
        return [chunk] + got, pltpu.VMEM((slots, tr, cols), F32)

    (in_i, v_i), (in_o, v_o) = specs(p_i), specs(p_o)
    in_sm = [pl.BlockSpec((N_CHIPS, SM_ROWS // 2, GROUP), lambda l, i, kc: (0, kc[1], 0))] * L
    grid_spec = pltpu.PrefetchScalarGridSpec(
        num_scalar_prefetch=1, grid=(L, nb), in_specs=in_i + in_o + in_sm, out_specs=[ANY, ANY, ANY],
        scratch_shapes=[v_i, v_o, pltpu.VMEM((L, SM_ROWS // 2, GROUP), F32)]
        + [pltpu.SemaphoreType.DMA((2 * n_steps + 1,))] * 3)
    return pl.pallas_call(
        body, name="sum_share", grid_spec=grid_spec,
        out_shape=[jax.ShapeDtypeStruct((L, 2 * p.shape[2], p.shape[3]), F32) for p in (p_i, p_o)]
        + [jax.ShapeDtypeStruct((L, SM_ROWS, GROUP), F32)],
        compiler_params=_vmem_params(dimension_semantics=("arbitrary",) * 2, has_side_effects=True,
                                     collective_id=COLLECTIVE_ID["sum_share"]),
    )(kc_arr, p_i, q_i, q_i, q_i, p_o, q_o, q_o, q_o, *r_sms)


WEIGHTS = ("ln_g", "ln_b", "w_in", "b_in", "conv_a_w", "conv_a_b", "norm_a_g", "norm_a_b", "conv_b_w", "pool_w",
           "pool_scale", "sgu_ln_g", "sgu_ln_b", "sgu_w", "sgu_bias", "w_out", "b_out")


def _pad_rows(a, rows):
    return jnp.pad(a, ((0, rows - a.shape[0]), (0, 0)))


def _indicator_consts():
    seg = jnp.where((jnp.arange(GROUP)[:, None] // HEAD) == (jnp.arange(GROUP)[None, :] // HEAD),
                    1.0 / HEAD, 0.0).astype(BF16)
    e4 = ((jnp.arange(GROUP)[:, None] // HEAD) == jnp.arange(128)[None, :]).astype(BF16)
    return seg, e4


def _layer_consts(p, conv_full):
    L = conv_full.shape[0]
    same_head = jnp.eye(4, dtype=F32)[:, None, :, None] > 0

    def rows_to(a, rows):
        return jnp.pad(a, ((0, 0), (0, rows - a.shape[1]), (0, 0)))

    s256 = jnp.stack([p[n] for n in ("conv_a_b", "norm_a_g", "norm_a_b", "pool_scale", "sgu_ln_g", "sgu_ln_b")], axis=1)
    pw = jnp.where(same_head, p["pool_w"][:, :, :, None, :], 0.0).reshape(L, GROUP, GROUP)
    return dict(
        caw=rows_to(conv_full[:, :KA], 32), cbw=rows_to(conv_full[:, KA:], 8), s256=rows_to(s256, 8),
        pw=pw.astype(BF16),
        wm=jnp.transpose(p["sgu_w"], (0, 2, 1, 3)).reshape(L, SGU_BLOCK, 4 * SGU_BLOCK),
        wmt=jnp.transpose(p["sgu_w"], (0, 1, 3, 2)).reshape(L, 4 * SGU_BLOCK, SGU_BLOCK),
        sb=jnp.repeat(jnp.transpose(p["sgu_bias"], (0, 2, 1)), HEAD, axis=2),
        v1024=rows_to(jnp.stack([p["b_out"], p["ln_g"], p["ln_b"]], axis=1), 8),
        bin=p["b_in"][:, None, :])


def _unpack_small(sm):
    L = sm.shape[0]
    owc = jnp.concatenate([sm[:, ROW_WC:ROW_WC + SGU_BLOCK], sm[:, ROW_WC + SGU_BLOCK:ROW_WC + 2 * SGU_BLOCK]], axis=2)
    return dict(
        conv_a_b=sm[:, 0], norm_a_g=sm[:, 1], norm_a_b=sm[:, 2], pool_scale=sm[:, 3], sgu_ln_g=sm[:, 4],
        sgu_ln_b=sm[:, 5], conv_b_w=sm[:, ROW_CBW:ROW_CBW + KB], conv_a_w=sm[:, ROW_CAW:ROW_CAW + KA],
        pool_w=jnp.transpose(sm[:, ROW_PW:ROW_PW + HEAD].reshape(L, HEAD, 4, HEAD), (0, 2, 1, 3)),
        ln_g=sm[:, ROW_LNG:ROW_LNG + 4].reshape(L, D_MODEL), ln_b=sm[:, ROW_LNB:ROW_LNB + 4].reshape(L, D_MODEL),
        b_out=sm[:, ROW_BOUT:ROW_BOUT + 4].reshape(L, D_MODEL),
        b_in=sm[:, ROW_BIN:ROW_BIN + N_SLICES].reshape(L, IN_WIDTH),
        sgu_w=jnp.transpose(owc.reshape(L, SGU_BLOCK, 4, SGU_BLOCK), (0, 2, 1, 3)),
        sgu_bias=sm[:, ROW_SB:ROW_SB + 4, 0:SGU_BLOCK])


def _step(p, m, v, x, target, *, tile_f, tile_b, k_steps):
    L = p["ln_g"].shape[0]
    xi, yi, ci = _place()
    me_k = 2 * xi + yi
    hi_rows, ho_rows = D_MODEL // 2, GROUP // 2

    cw = jnp.concatenate([p["conv_a_w"], p["conv_b_w"]], axis=1).reshape(-1, 128)
    cw_rows = cw.shape[0]
    cw = _pad_rows(cw, -(-cw_rows // SUBLANES) * SUBLANES)
    wi16 = p["w_in"].astype(BF16)
    wo16 = p["w_out"].astype(BF16)
    wig0, wog0, cwg = _gather_weights(wi16[0:1], wo16[0:1], cw)
    cwg = cwg[:, :cw_rows].reshape(N_CHIPS, L, KA + KB, HEAD)
    conv_full = jnp.transpose(cwg, (1, 2, 0, 3)).reshape(L, KA + KB, GROUP)
    seg, e4 = _indicator_consts()
    k = _layer_consts(p, conv_full)
    layer = [jnp.full((1,), l, jnp.int32) for l in range(L)]

    hcur = x
    saved, wig, wog = [], [wig0[0]], [wog0[0]]
    for l in range(L):
        nxt = (wi16, wo16) if l + 1 < L else None
        outs = _fwd_layer(layer[l], hcur, wig[l], k["bin"], k["caw"], k["cbw"], k["s256"], seg, k["pw"], k["wm"], k["sb"],
                          wog[l], k["v1024"], tile=tile_f, nxt=nxt, target=None if nxt is not None else target)
        y, xb, h, aux, mixb, z = outs[0:6]
        if nxt is not None:
            wig.append(outs[6])
            wog.append(outs[7])
        saved.append((xb, h, aux, mixb, z))
        hcur = y

    dy = hcur
    loss_local = outs[6][0, 0]

    p_i = lax.empty((L, N_CHIPS, hi_rows, COLS), BF16)
    p_o = lax.empty((L, N_CHIPS, ho_rows, D_MODEL), BF16)
    q_i = lax.empty((3, L, hi_rows, COLS), BF16)
    q_o = lax.empty((3, L, ho_rows, D_MODEL), BF16)
    r_sm = [None] * L
    pending = None
    for l in reversed(range(L)):
        xb, h, aux, mixb, z = saved[l]
        exch = None if pending is None else (p_i, p_o, pending, q_i, q_o)
        outs = _bwd_layer(layer[l], dy, z, h, aux, wig[l], k["caw"], k["cbw"], k["s256"], seg, k["pw"], k["wm"],
                          k["wmt"], k["sb"], wog[l], k["v1024"], e4, tile=tile_b, exch=exch)
        dy, dhb, dzb, osm = outs[0:4]
        if l == L - 1:
            osm = osm.at[ROW_LOSS, 0].set(loss_local)
        if exch is not None:
            q_i, q_o, r_sm[l + 1] = outs[4:7]
        cl_arr = jnp.stack([ci, jnp.int32(l), me_k]).astype(jnp.int32)
        if l > 0:
            p_i, p_o = _dw_swap(cl_arr, xb, dhb, mixb, dzb, p_i, p_o, k_steps=k_steps)
        else:
            p_i, p_o, q_i, q_o, r_sm[0] = _dw_swap(cl_arr, xb, dhb, mixb, dzb, p_i, p_o, k_steps=k_steps,
                                                   last=(osm, q_i, q_o))
        pending = osm
    grad_x = dy

    kc_arr = jnp.stack([me_k, ci]).astype(jnp.int32)
    g_i, g_o, summed = _sum_share(kc_arr, p_i, q_i, p_o, q_o, r_sm, nb=2)
    loss = summed[L - 1, ROW_LOSS, 0]
    grads = _unpack_small(summed)
    for n in ("conv_a_w", "conv_b_w"):
        grads[n] = lax.dynamic_slice_in_dim(grads[n], me_k * HEAD, HEAD, axis=2)

    grads["w_in"] = g_i
    grads["w_out"] = g_o

    delta, new_m, new_v = {}, {}, {}
    projections = (("w_in", 512), ("w_out", 128))
    groups = [tuple(a.reshape(-1, a.shape[-1]) for a in (p[n], grads[n], m[n], v[n])) + (tr,) for n, tr in projections]
    outs = _adamw(groups, name="adamw_projections")
    for k, (n, _) in enumerate(projections):
        delta[n], new_m[n], new_v[n], grads[n] = (a.reshape(p[n].shape) for a in outs[4 * k:4 * k + 4])
    small = [n for n in WEIGHTS if n not in ("w_in", "w_out")]
    flat = [[a[n].reshape(-1, a[n].shape[-1]) for n in small] for a in (p, grads, m, v)]
    outs = _adamw_small(*flat)
    for j, n in enumerate(small):
        delta[n], new_m[n], new_v[n] = (o[j].reshape(p[n].shape) for o in outs)

    return (loss, grad_x[None], *[grads[n] for n in WEIGHTS], *[delta[n] for n in WEIGHTS],
            *[new_m[n] for n in WEIGHTS], *[new_v[n] for n in WEIGHTS])


def kernel(x, ln_g, ln_b, w_in, b_in, conv_a_w, conv_a_b, norm_a_g, norm_a_b, conv_b_w, pool_w, pool_scale, sgu_ln_g, sgu_ln_b, sgu_w, sgu_bias, w_out, b_out, loss_target, m_ln_g, m_ln_b, m_w_in, m_b_in, m_conv_a_w, m_conv_a_b, m_norm_a_g, m_norm_a_b, m_conv_b_w, m_pool_w, m_pool_scale, m_sgu_ln_g, m_sgu_ln_b, m_sgu_w, m_sgu_bias, m_w_out, m_b_out, v_ln_g, v_ln_b, v_w_in, v_b_in, v_conv_a_w, v_conv_a_b, v_norm_a_g, v_norm_a_b, v_conv_b_w, v_pool_w, v_pool_scale, v_sgu_ln_g, v_sgu_ln_b, v_sgu_w, v_sgu_bias, v_w_out, v_b_out):
    p = dict(ln_g=ln_g, ln_b=ln_b, w_in=w_in, b_in=b_in, conv_a_w=conv_a_w, conv_a_b=conv_a_b, norm_a_g=norm_a_g,
             norm_a_b=norm_a_b, conv_b_w=conv_b_w, pool_w=pool_w, pool_scale=pool_scale, sgu_ln_g=sgu_ln_g,
             sgu_ln_b=sgu_ln_b, sgu_w=sgu_w, sgu_bias=sgu_bias, w_out=w_out, b_out=b_out)
    m = dict(ln_g=m_ln_g, ln_b=m_ln_b, w_in=m_w_in, b_in=m_b_in, conv_a_w=m_conv_a_w, conv_a_b=m_conv_a_b,
             norm_a_g=m_norm_a_g, norm_a_b=m_norm_a_b, conv_b_w=m_conv_b_w, pool_w=m_pool_w, pool_scale=m_pool_scale,
             sgu_ln_g=m_sgu_ln_g, sgu_ln_b=m_sgu_ln_b, sgu_w=m_sgu_w, sgu_bias=m_sgu_bias, w_out=m_w_out, b_out=m_b_out)
    v = dict(ln_g=v_ln_g, ln_b=v_ln_b, w_in=v_w_in, b_in=v_b_in, conv_a_w=v_conv_a_w, conv_a_b=v_conv_a_b,
             norm_a_g=v_norm_a_g, norm_a_b=v_norm_a_b, conv_b_w=v_conv_b_w, pool_w=v_pool_w, pool_scale=v_pool_scale,
             sgu_ln_g=v_sgu_ln_g, sgu_ln_b=v_sgu_ln_b, sgu_w=v_sgu_w, sgu_bias=v_sgu_bias, w_out=v_w_out, b_out=v_b_out)
    return _step(p, m, v, x[0], loss_target[0], tile_f=256, tile_b=256, k_steps=4)
```
